```python
import jax
import jax.numpy as jnp
from jax import lax
import numpy as np

D_MODEL = 1024
BATCH = 16
SEQ = 2048
DEPTH = 2

HEAD_DIM = 64
DIL_PATTERNS = ((128, 1), (512, 4), (2048, 16))
DIL_HEADS_PER_GROUP = 4
DIL_HEADS = len(DIL_PATTERNS) * DIL_HEADS_PER_GROUP
DIL_QBLOCK = 64
NA_HEADS = 8
GRID_W = 64
NA_ROWS_MAX = 8
NA_COLS = 16
NA_QCOLS = 16
D_FF = 2816
ROPE_THETA = 10000.0
RMS_EPS = 1e-6
NEG_INF = -1e30
DIL_QKV = 3 * DIL_HEADS * HEAD_DIM
NA_QKV = 3 * NA_HEADS * HEAD_DIM
IN_WIDTH = DIL_QKV + NA_QKV + 2 * D_MODEL

kernel_name = 'hybrid_dilated_neighbourhood_macaron'


def rms_norm(x, g):
    x32 = x.astype(jnp.float32)
    y = x32 * lax.rsqrt(jnp.mean(x32 * x32, axis=-1, keepdims=True) + RMS_EPS)
    return (y * g.astype(jnp.float32)).astype(x.dtype)


def swiglu(x, w_up, w_down):
    gate, up = jnp.split(x @ w_up, 2, axis=-1)
    return (jax.nn.silu(gate) * up) @ w_down


def rotary(t, pos):
    half = HEAD_DIM // 2
    inv_freq = ROPE_THETA ** (-jnp.arange(half, dtype=jnp.float32) / half)
    ang = pos.astype(jnp.float32)[:, None] * inv_freq[None, :]
    cos = jnp.cos(ang).astype(t.dtype)
    sin = jnp.sin(ang).astype(t.dtype)
    t1, t2 = t[..., :half], t[..., half:]
    return jnp.concatenate([t1 * cos - t2 * sin, t2 * cos + t1 * sin], axis=-1)


def dilated_window_attention(q, k, v, dilation, half):
    b, g, s, hd = q.shape
    sub_len = s // dilation
    bq = min(DIL_QBLOCK, sub_len)
    nb = -(-sub_len // bq)
    padded = nb * bq
    nk = bq + 2 * half

    def to_sub(t):
        return t.reshape(b, g, sub_len, dilation, hd).transpose(0, 1, 3, 2, 4)

    qs = jnp.pad(to_sub(q), ((0, 0), (0, 0), (0, 0), (0, padded - sub_len), (0, 0)))
    qs = qs.reshape(b, g, dilation, nb, bq, hd)
    key_idx = np.arange(nb)[:, None] * bq + np.arange(nk)[None, :]
    kpad = ((0, 0), (0, 0), (0, 0), (half, padded - sub_len + half), (0, 0))
    ks = jnp.pad(to_sub(k), kpad)[:, :, :, key_idx]
    vs = jnp.pad(to_sub(v), kpad)[:, :, :, key_idx]
    q_pos = np.arange(nb)[:, None] * bq + np.arange(bq)[None, :]
    k_pos = key_idx - half
    rel = k_pos[:, None, :] - q_pos[:, :, None]
    valid = (np.abs(rel) <= half) & (k_pos[:, None, :] >= 0) & (k_pos[:, None, :] < sub_len)
    scores = jnp.einsum('bgrnqd,bgrnkd->bgrnqk', qs, ks).astype(jnp.float32) * (hd ** -0.5)
    scores = jnp.where(valid, scores, NEG_INF)
    m = jnp.max(scores, axis=-1, keepdims=True)
    p = jnp.exp(scores - m)
    den = jnp.sum(p, axis=-1)
    o = jnp.einsum('bgrnqk,bgrnkd->bgrnqd', p.astype(v.dtype), vs).astype(jnp.float32) / den[..., None]
    lse = m[..., 0] + jnp.log(den)
    o = o.reshape(b, g, dilation, padded, hd)[:, :, :, :sub_len]
    o = o.transpose(0, 1, 3, 2, 4).reshape(b, g, s, hd)
    lse = lse.reshape(b, g, dilation, padded)[..., :sub_len].transpose(0, 1, 3, 2).reshape(b, g, s)
    return o, lse


def neighbourhood_attention(q, k, v, rel_bias):
    b, nh, s, hd = q.shape
    rows = s // GRID_W
    kr = min(NA_ROWS_MAX, rows)
    n_cb = GRID_W // NA_QCOLS
    kcw = 2 * NA_QCOLS
    q_cols = np.arange(GRID_W).reshape(n_cb, NA_QCOLS)
    key_cols = (np.clip(np.arange(n_cb) * NA_QCOLS - NA_QCOLS // 2, 0, GRID_W - kcw)[:, None]
                + np.arange(kcw)[None, :])
    win_lo = np.clip(q_cols - NA_COLS // 2, 0, GRID_W - NA_COLS)
    col_valid = ((key_cols[:, None, :] >= win_lo[:, :, None])
                 & (key_cols[:, None, :] < win_lo[:, :, None] + NA_COLS))
    col_idx = np.clip(key_cols[:, None, :] - q_cols[:, :, None] + NA_COLS - 1, 0, 2 * NA_COLS - 2)
    row_ids = np.arange(rows)
    row_lo = np.clip(row_ids - kr // 2, 0, rows - kr)
    row_idx = row_lo[:, None] + np.arange(kr)[None, :] - row_ids[:, None] + NA_ROWS_MAX - 1
    bias = rel_bias.astype(jnp.float32)[:, row_idx][..., col_idx]
    bias = jnp.where(col_valid[:, :, None, :], bias.transpose(1, 0, 3, 4, 2, 5), NEG_INF)
    kg = k.reshape(b, nh, rows, GRID_W, hd)
    vg = v.reshape(b, nh, rows, GRID_W, hd)
    q_rows = q.reshape(b, nh, rows, n_cb, NA_QCOLS, hd).transpose(2, 0, 1, 3, 4, 5)
    scale = hd ** -0.5

    def one_row(args):
        q_r, lo, bias_r = args
        k_r = lax.dynamic_slice_in_dim(kg, lo, kr, axis=2)[:, :, :, key_cols]
        v_r = lax.dynamic_slice_in_dim(vg, lo, kr, axis=2)[:, :, :, key_cols]
        sc = jnp.einsum('bhcqd,bhrckd->bhcqrk', q_r, k_r).astype(jnp.float32) * scale + bias_r
        p = jax.nn.softmax(sc.reshape(b, nh, n_cb, NA_QCOLS, kr * kcw), axis=-1).reshape(sc.shape)
        return jnp.einsum('bhcqrk,bhrckd->bhcqd', p.astype(v.dtype), v_r)

    out = lax.map(one_row, (q_rows, jnp.asarray(row_lo, dtype=jnp.int32), bias))
    return out.reshape(rows, b, nh, GRID_W, hd).transpose(1, 2, 0, 3, 4).reshape(b, nh, s, hd)


def hybrid_mixer(h, w_in, rel_bias, w_branch_a, w_branch_b, w_out, pos):
    b, s, _ = h.shape
    proj = h @ w_in
    a_qkv = proj[..., :DIL_QKV]
    b_qkv = proj[..., DIL_QKV:DIL_QKV + NA_QKV]
    gate_a, gate_b = jnp.split(jax.nn.sigmoid(proj[..., DIL_QKV + NA_QKV:]), 2, axis=-1)

    def heads(t, n):
        return t.reshape(b, s, n, HEAD_DIM).transpose(0, 2, 1, 3)

    qa, ka, va = (heads(t, DIL_HEADS) for t in jnp.split(a_qkv, 3, axis=-1))
    qa, ka = rotary(qa, pos), rotary(ka, pos)
    outs, lses = [], []
    for gi, (window, dilation) in enumerate(DIL_PATTERNS):
        grp = slice(gi * DIL_HEADS_PER_GROUP, (gi + 1) * DIL_HEADS_PER_GROUP)
        o, lse = dilated_window_attention(qa[:, grp], ka[:, grp], va[:, grp], dilation, (window // 2) // dilation)
        outs.append(o)
        lses.append(lse)
    mix_w = jax.nn.softmax(jnp.stack(lses), axis=0)
    ya = jnp.sum(mix_w[..., None] * jnp.stack(outs), axis=0).astype(h.dtype)
    ya = ya.transpose(0, 2, 1, 3).reshape(b, s, DIL_HEADS_PER_GROUP * HEAD_DIM)

    qb, kb, vb = (heads(t, NA_HEADS) for t in jnp.split(b_qkv, 3, axis=-1))
    yb = neighbourhood_attention(qb, kb, vb, rel_bias)
    yb = yb.transpose(0, 2, 1, 3).reshape(b, s, NA_HEADS * HEAD_DIM)

    merged = gate_a * (ya @ w_branch_a) + gate_b * (yb @ w_branch_b)
    return merged @ w_out


def _fwd_setup_inputs(seed: int = 0) -> dict:
    key = jax.random.key(seed)
    ks = jax.random.split(key, 14)
    f32 = jnp.float32

    def normal(k, shape, scale):
        return jax.random.normal(k, shape, f32) * scale

    def gain(k, shape):
        return 1.0 + 0.05 * jax.random.normal(k, shape, f32)

    return {
        'x': normal(ks[0], (BATCH, SEQ, D_MODEL), 1.0),
        'ffn1_norm': gain(ks[1], (DEPTH, D_MODEL)),
        'ffn1_w_up': normal(ks[2], (DEPTH, D_MODEL, 2 * D_FF), D_MODEL ** -0.5),
        'ffn1_w_down': normal(ks[3], (DEPTH, D_FF, D_MODEL), D_FF ** -0.5),
        'mix_norm': gain(ks[4], (DEPTH, D_MODEL)),
        'w_in': normal(ks[5], (DEPTH, D_MODEL, IN_WIDTH), D_MODEL ** -0.5),
        'na_rel_bias': normal(ks[6], (DEPTH, NA_HEADS, 2 * NA_ROWS_MAX - 1, 2 * NA_COLS - 1), 0.1),
        'w_branch_a': normal(ks[7], (DEPTH, DIL_HEADS_PER_GROUP * HEAD_DIM, D_MODEL), (DIL_HEADS_PER_GROUP * HEAD_DIM) ** -0.5),
        'w_branch_b': normal(ks[8], (DEPTH, NA_HEADS * HEAD_DIM, D_MODEL), (NA_HEADS * HEAD_DIM) ** -0.5),
        'w_out': normal(ks[9], (DEPTH, D_MODEL, D_MODEL), D_MODEL ** -0.5),
        'ffn2_norm': gain(ks[10], (DEPTH, D_MODEL)),
        'ffn2_w_up': normal(ks[11], (DEPTH, D_MODEL, 2 * D_FF), D_MODEL ** -0.5),
        'ffn2_w_down': normal(ks[12], (DEPTH, D_FF, D_MODEL), D_FF ** -0.5),
        'final_norm': gain(ks[13], (D_MODEL,)),
    }


def _fwd_reference(x, ffn1_norm, ffn1_w_up, ffn1_w_down, mix_norm, w_in, na_rel_bias, w_branch_a,
              w_branch_b, w_out, ffn2_norm, ffn2_w_up, ffn2_w_down, final_norm):
    pos = jnp.arange(x.shape[1])
    for l in range(DEPTH):
        x = x + 0.5 * swiglu(rms_norm(x, ffn1_norm[l]), ffn1_w_up[l], ffn1_w_down[l])
        x = x + hybrid_mixer(rms_norm(x, mix_norm[l]), w_in[l], na_rel_bias[l], w_branch_a[l],
                             w_branch_b[l], w_out[l], pos)
        x = x + 0.5 * swiglu(rms_norm(x, ffn2_norm[l]), ffn2_w_up[l], ffn2_w_down[l])
    return rms_norm(x, final_norm)


import jax as _jax
import jax.numpy as _jnp

TWIN_FORMAT = 'train_step'
FWD_PARAMS = ['x', 'ffn1_norm', 'ffn1_w_up', 'ffn1_w_down', 'mix_norm', 'w_in', 'na_rel_bias', 'w_branch_a', 'w_branch_b', 'w_out', 'ffn2_norm', 'ffn2_w_up', 'ffn2_w_down', 'final_norm']
TWIN_WEIGHTS = ['ffn1_norm', 'ffn1_w_up', 'ffn1_w_down', 'mix_norm', 'w_in', 'na_rel_bias', 'w_branch_a', 'w_branch_b', 'w_out', 'ffn2_norm', 'ffn2_w_up', 'ffn2_w_down', 'final_norm']
TWIN_DIFF_INPUT = 'x'
TWIN_INPUTS = ['x', 'ffn1_norm', 'ffn1_w_up', 'ffn1_w_down', 'mix_norm', 'w_in', 'na_rel_bias', 'w_branch_a', 'w_branch_b', 'w_out', 'ffn2_norm', 'ffn2_w_up', 'ffn2_w_down', 'final_norm', 'loss_target', 'm_ffn1_norm', 'm_ffn1_w_up', 'm_ffn1_w_down', 'm_mix_norm', 'm_w_in', 'm_na_rel_bias', 'm_w_branch_a', 'm_w_branch_b', 'm_w_out', 'm_ffn2_norm', 'm_ffn2_w_up', 'm_ffn2_w_down', 'm_final_norm', 'v_ffn1_norm', 'v_ffn1_w_up', 'v_ffn1_w_down', 'v_mix_norm', 'v_w_in', 'v_na_rel_bias', 'v_w_branch_a', 'v_w_branch_b', 'v_w_out', 'v_ffn2_norm', 'v_ffn2_w_up', 'v_ffn2_w_down', 'v_final_norm']
TWIN_OUTPUTS = ['loss', 'grad_x', 'grad_ffn1_norm', 'grad_ffn1_w_up', 'grad_ffn1_w_down', 'grad_mix_norm', 'grad_w_in', 'grad_na_rel_bias', 'grad_w_branch_a', 'grad_w_branch_b', 'grad_w_out', 'grad_ffn2_norm', 'grad_ffn2_w_up', 'grad_ffn2_w_down', 'grad_final_norm', 'delta_ffn1_norm', 'delta_ffn1_w_up', 'delta_ffn1_w_down', 'delta_mix_norm', 'delta_w_in', 'delta_na_rel_bias', 'delta_w_branch_a', 'delta_w_branch_b', 'delta_w_out', 'delta_ffn2_norm', 'delta_ffn2_w_up', 'delta_ffn2_w_down', 'delta_final_norm', 'new_m_ffn1_norm', 'new_m_ffn1_w_up', 'new_m_ffn1_w_down', 'new_m_mix_norm', 'new_m_w_in', 'new_m_na_rel_bias', 'new_m_w_branch_a', 'new_m_w_branch_b', 'new_m_w_out', 'new_m_ffn2_norm', 'new_m_ffn2_w_up', 'new_m_ffn2_w_down', 'new_m_final_norm', 'new_v_ffn1_norm', 'new_v_ffn1_w_up', 'new_v_ffn1_w_down', 'new_v_mix_norm', 'new_v_w_in', 'new_v_na_rel_bias', 'new_v_w_branch_a', 'new_v_w_branch_b', 'new_v_w_out', 'new_v_ffn2_norm', 'new_v_ffn2_w_up', 'new_v_ffn2_w_down', 'new_v_final_norm']
TWIN_LEAF_KINDS = {'loss': 'loss', 'grad_x': 'grad_x', 'grad_ffn1_norm': 'grad_w', 'grad_ffn1_w_up': 'grad_w', 'grad_ffn1_w_down': 'grad_w', 'grad_mix_norm': 'grad_w', 'grad_w_in': 'grad_w', 'grad_na_rel_bias': 'grad_w', 'grad_w_branch_a': 'grad_w', 'grad_w_branch_b': 'grad_w', 'grad_w_out': 'grad_w', 'grad_ffn2_norm': 'grad_w', 'grad_ffn2_w_up': 'grad_w', 'grad_ffn2_w_down': 'grad_w', 'grad_final_norm': 'grad_w', 'delta_ffn1_norm': 'delta_w', 'delta_ffn1_w_up': 'delta_w', 'delta_ffn1_w_down': 'delta_w', 'delta_mix_norm': 'delta_w', 'delta_w_in': 'delta_w', 'delta_na_rel_bias': 'delta_w', 'delta_w_branch_a': 'delta_w', 'delta_w_branch_b': 'delta_w', 'delta_w_out': 'delta_w', 'delta_ffn2_norm': 'delta_w', 'delta_ffn2_w_up': 'delta_w', 'delta_ffn2_w_down': 'delta_w', 'delta_final_norm': 'delta_w', 'new_m_ffn1_norm': 'new_m', 'new_m_ffn1_w_up': 'new_m', 'new_m_ffn1_w_down': 'new_m', 'new_m_mix_norm': 'new_m', 'new_m_w_in': 'new_m', 'new_m_na_rel_bias': 'new_m', 'new_m_w_branch_a': 'new_m', 'new_m_w_branch_b': 'new_m', 'new_m_w_out': 'new_m', 'new_m_ffn2_norm': 'new_m', 'new_m_ffn2_w_up': 'new_m', 'new_m_ffn2_w_down': 'new_m', 'new_m_final_norm': 'new_m', 'new_v_ffn1_norm': 'new_v', 'new_v_ffn1_w_up': 'new_v', 'new_v_ffn1_w_down': 'new_v', 'new_v_mix_norm': 'new_v', 'new_v_w_in': 'new_v', 'new_v_na_rel_bias': 'new_v', 'new_v_w_branch_a': 'new_v', 'new_v_w_branch_b': 'new_v', 'new_v_w_out': 'new_v', 'new_v_ffn2_norm': 'new_v', 'new_v_ffn2_w_up': 'new_v', 'new_v_ffn2_w_down': 'new_v', 'new_v_final_norm': 'new_v'}


def _forward(args):
    return _fwd_reference(*[args[k] for k in FWD_PARAMS])


def _output_shape():
    out = _jax.eval_shape(lambda: _forward(_fwd_setup_inputs(0)))
    return out.shape, out.dtype

N_MICROBATCH = 1
ADAM_LR = 0.001
ADAM_B1 = 0.9
ADAM_B2 = 0.999
ADAM_EPS = 1e-08
ADAM_WD = 0.01
ADAM_STEP = 10
PER_EXAMPLE_BATCH_AXIS = {'x': 0, 'loss_target': 0}
SHARED_INPUTS = []
_WEIGHT_DTYPES = {'ffn1_norm': _jnp.float32, 'ffn1_w_up': _jnp.float32, 'ffn1_w_down': _jnp.float32, 'mix_norm': _jnp.float32, 'w_in': _jnp.float32, 'na_rel_bias': _jnp.float32, 'w_branch_a': _jnp.float32, 'w_branch_b': _jnp.float32, 'w_out': _jnp.float32, 'ffn2_norm': _jnp.float32, 'ffn2_w_up': _jnp.float32, 'ffn2_w_down': _jnp.float32, 'final_norm': _jnp.float32}
MOMENT_SCALE = {'ffn1_norm': 7.828558e-02, 'ffn1_w_up': 3.315909e-02, 'ffn1_w_down': 5.399903e-02, 'mix_norm': 3.930914e-02, 'w_in': 1.628627e-02, 'na_rel_bias': 9.323757e-03, 'w_branch_a': 1.182706e-02, 'w_branch_b': 1.991369e-02, 'w_out': 2.313015e-02, 'ffn2_norm': 7.253371e-02, 'ffn2_w_up': 3.024621e-02, 'ffn2_w_down': 4.948207e-02, 'final_norm': 3.201077e+01}


def _to_microbatches(a, axis):
    t = _jnp.moveaxis(a, axis, 0)
    t = t.reshape((N_MICROBATCH, t.shape[0] // N_MICROBATCH) + t.shape[1:])
    return _jnp.moveaxis(t, 1, axis + 1)


def setup_inputs(seed: int = 0) -> dict:
    inp = _fwd_setup_inputs(seed)
    key = _jax.random.fold_in(_jax.random.key(seed), 7919)
    shape, _ = _output_shape()
    out = dict(inp)
    out["loss_target"] = _jax.random.normal(_jax.random.fold_in(key, 0), shape, _jnp.float32)
    for i, name in enumerate(TWIN_WEIGHTS):
        w = inp[name].astype(_jnp.float32)
        if MOMENT_SCALE is None:
            s = _jnp.sqrt(_jnp.mean(_jnp.square(w)) + 1e-30)
        else:
            s = MOMENT_SCALE[name]
        km, kv = _jax.random.split(_jax.random.fold_in(key, i + 1))
        out[name] = w
        out["m_" + name] = s * _jax.random.normal(km, w.shape, _jnp.float32)
        out["v_" + name] = (s * s) * _jax.random.uniform(kv, w.shape, _jnp.float32, 0.5, 1.5)
    if N_MICROBATCH > 1:
        for name, axis in PER_EXAMPLE_BATCH_AXIS.items():
            out[name] = _to_microbatches(out[name], axis)
    return {'x': out['x'], 'ffn1_norm': out['ffn1_norm'], 'ffn1_w_up': out['ffn1_w_up'], 'ffn1_w_down': out['ffn1_w_down'], 'mix_norm': out['mix_norm'], 'w_in': out['w_in'], 'na_rel_bias': out['na_rel_bias'], 'w_branch_a': out['w_branch_a'], 'w_branch_b': out['w_branch_b'], 'w_out': out['w_out'], 'ffn2_norm': out['ffn2_norm'], 'ffn2_w_up': out['ffn2_w_up'], 'ffn2_w_down': out['ffn2_w_down'], 'final_norm': out['final_norm'], 'loss_target': out['loss_target'], 'm_ffn1_norm': out['m_ffn1_norm'], 'm_ffn1_w_up': out['m_ffn1_w_up'], 'm_ffn1_w_down': out['m_ffn1_w_down'], 'm_mix_norm': out['m_mix_norm'], 'm_w_in': out['m_w_in'], 'm_na_rel_bias': out['m_na_rel_bias'], 'm_w_branch_a': out['m_w_branch_a'], 'm_w_branch_b': out['m_w_branch_b'], 'm_w_out': out['m_w_out'], 'm_ffn2_norm': out['m_ffn2_norm'], 'm_ffn2_w_up': out['m_ffn2_w_up'], 'm_ffn2_w_down': out['m_ffn2_w_down'], 'm_final_norm': out['m_final_norm'], 'v_ffn1_norm': out['v_ffn1_norm'], 'v_ffn1_w_up': out['v_ffn1_w_up'], 'v_ffn1_w_down': out['v_ffn1_w_down'], 'v_mix_norm': out['v_mix_norm'], 'v_w_in': out['v_w_in'], 'v_na_rel_bias': out['v_na_rel_bias'], 'v_w_branch_a': out['v_w_branch_a'], 'v_w_branch_b': out['v_w_branch_b'], 'v_w_out': out['v_w_out'], 'v_ffn2_norm': out['v_ffn2_norm'], 'v_ffn2_w_up': out['v_ffn2_w_up'], 'v_ffn2_w_down': out['v_ffn2_w_down'], 'v_final_norm': out['v_final_norm']}


def _loss(weights, diff, rest, loss_target):
    with _jax.named_scope("forward"):
        args = {**rest, TWIN_DIFF_INPUT: diff, **{k: w.astype(_WEIGHT_DTYPES[k]) for k, w in weights.items()}}
        y = _forward(args)
    with _jax.named_scope("loss_head"):
        err = _jnp.square(y.astype(_jnp.float32) - loss_target)
        return 0.5 * _jnp.sum(_jnp.mean(err, axis=-1)) if err.ndim else 0.5 * err


def _adamw(w, g, m, v):
    m = ADAM_B1 * m + (1.0 - ADAM_B1) * g
    v = ADAM_B2 * v + (1.0 - ADAM_B2) * _jnp.square(g)
    m_hat = m / (1.0 - ADAM_B1 ** ADAM_STEP)
    v_hat = v / (1.0 - ADAM_B2 ** ADAM_STEP)
    delta = -ADAM_LR * (m_hat / (_jnp.sqrt(v_hat) + ADAM_EPS) + ADAM_WD * w)
    return delta, m, v


def reference(x, ffn1_norm, ffn1_w_up, ffn1_w_down, mix_norm, w_in, na_rel_bias, w_branch_a, w_branch_b, w_out, ffn2_norm, ffn2_w_up, ffn2_w_down, final_norm, loss_target, m_ffn1_norm, m_ffn1_w_up, m_ffn1_w_down, m_mix_norm, m_w_in, m_na_rel_bias, m_w_branch_a, m_w_branch_b, m_w_out, m_ffn2_norm, m_ffn2_w_up, m_ffn2_w_down, m_final_norm, v_ffn1_norm, v_ffn1_w_up, v_ffn1_w_down, v_mix_norm, v_w_in, v_na_rel_bias, v_w_branch_a, v_w_branch_b, v_w_out, v_ffn2_norm, v_ffn2_w_up, v_ffn2_w_down, v_final_norm):
    given = dict(x=x, ffn1_norm=ffn1_norm, ffn1_w_up=ffn1_w_up, ffn1_w_down=ffn1_w_down, mix_norm=mix_norm, w_in=w_in, na_rel_bias=na_rel_bias, w_branch_a=w_branch_a, w_branch_b=w_branch_b, w_out=w_out, ffn2_norm=ffn2_norm, ffn2_w_up=ffn2_w_up, ffn2_w_down=ffn2_w_down, final_norm=final_norm, loss_target=loss_target, m_ffn1_norm=m_ffn1_norm, m_ffn1_w_up=m_ffn1_w_up, m_ffn1_w_down=m_ffn1_w_down, m_mix_norm=m_mix_norm, m_w_in=m_w_in, m_na_rel_bias=m_na_rel_bias, m_w_branch_a=m_w_branch_a, m_w_branch_b=m_w_branch_b, m_w_out=m_w_out, m_ffn2_norm=m_ffn2_norm, m_ffn2_w_up=m_ffn2_w_up, m_ffn2_w_down=m_ffn2_w_down, m_final_norm=m_final_norm, v_ffn1_norm=v_ffn1_norm, v_ffn1_w_up=v_ffn1_w_up, v_ffn1_w_down=v_ffn1_w_down, v_mix_norm=v_mix_norm, v_w_in=v_w_in, v_na_rel_bias=v_na_rel_bias, v_w_branch_a=v_w_branch_a, v_w_branch_b=v_w_branch_b, v_w_out=v_w_out, v_ffn2_norm=v_ffn2_norm, v_ffn2_w_up=v_ffn2_w_up, v_ffn2_w_down=v_ffn2_w_down, v_final_norm=v_final_norm)
    weights = {n: given[n] for n in TWIN_WEIGHTS}
    shared = {n: given[n] for n in SHARED_INPUTS}
    per_example = {n: given[n] for n in ['x']}
    grad_fn = _jax.value_and_grad(_loss, argnums=(0, 1))

    def one_microbatch(ex, loss_target):
        ex = dict(ex)
        diff = ex.pop(TWIN_DIFF_INPUT)
        return grad_fn(weights, diff, {**shared, **ex}, loss_target)

    if N_MICROBATCH == 1:
        loss, (grad_w, grad_x) = one_microbatch(per_example, given["loss_target"])
    else:
        def body(carry, xs):
            loss_sum, grad_sum = carry
            l_k, (gw_k, gx_k) = one_microbatch(xs[0], xs[1])
            with _jax.named_scope("update"):
                return (loss_sum + l_k, _jax.tree.map(_jnp.add, grad_sum, gw_k)), gx_k

        init = (_jnp.zeros((), _jnp.float32), _jax.tree.map(_jnp.zeros_like, weights))
        (loss, grad_w), grad_x = _jax.lax.scan(body, init, (per_example, given["loss_target"]))
    with _jax.named_scope("update"):
        delta_w, new_m, new_v = {}, {}, {}
        for n in TWIN_WEIGHTS:
            delta_w[n], new_m[n], new_v[n] = _adamw(weights[n], grad_w[n], given["m_" + n], given["v_" + n])
    return (loss, grad_x, *[grad_w[n] for n in TWIN_WEIGHTS], *[delta_w[n] for n in TWIN_WEIGHTS],
            *[new_m[n] for n in TWIN_WEIGHTS], *[new_v[n] for n in TWIN_WEIGHTS])
```

```python
import numpy as np

import jax
import jax.numpy as jnp
from jax import lax
from jax.experimental import pallas as pl
from jax.experimental.pallas import tpu as pltpu

F32 = jnp.float32
BF16 = jnp.bfloat16
SDS = jax.ShapeDtypeStruct
BS = pl.BlockSpec
MESH = pl.DeviceIdType.MESH

D = 1024
S = 2048
F = 2816
DEPTH = 2
HEAD_DIM = 64
DILATIONS = (1, 4, 16)
HALF = 64
QKV_A = 2304
QKV_B = 1536
IN_W = 5888
N_DEV = 8
NA_ROWS = 32
GRID_W = 64
NA_KR = 8
ROPE_THETA = 10000.0
RMS_EPS = 1e-6
NEG = -1e30
SCALE = HEAD_DIM ** -0.5
ADAM_LR, ADAM_B1, ADAM_B2, ADAM_EPS, ADAM_WD, ADAM_STEP = 0.001, 0.9, 0.999, 1e-08, 0.01, 10
VMEM_LIMIT_V7X = 52 * 1024 * 1024
SMALL_ROWS = 120
BIAS_PAD = 3840


def _cp(*sem):
    return pltpu.CompilerParams(dimension_semantics=sem, vmem_limit_bytes=VMEM_LIMIT_V7X)


def _dot_nn(a, b):
    return jnp.dot(a, b, preferred_element_type=F32)


def _dot_nt(a, b):
    return lax.dot_general(a, b, (((1,), (1,)), ((), ())), preferred_element_type=F32)


def _dot_tn(a, b):
    return lax.dot_general(a, b, (((0,), (0,)), ((), ())), preferred_element_type=F32)


def _ds(start, size, stride):
    return pl.ds(start, size) if stride == 1 else pl.ds(start, size, stride=stride)


def _norm_fwd(x, g, tag):
    t = x.shape[0]
    tm = 512

    def body(x_ref, g_ref, o_ref):
        xv = x_ref[...]
        r = lax.rsqrt(jnp.mean(xv * xv, axis=-1, keepdims=True) + RMS_EPS)
        o_ref[...] = (xv * r * g_ref[...]).astype(BF16)

    return pl.pallas_call(
        body, name=f"norm_fwd_{tag}", grid=(t // tm,),
        in_specs=[BS((tm, D), lambda i: (i, 0)), BS((1, D), lambda i: (0, 0))],
        out_specs=BS((tm, D), lambda i: (i, 0)),
        out_shape=SDS((t, D), BF16), compiler_params=_cp("parallel"),
    )(x, g.reshape(1, D))


def _norm_bwd(x, g, dh, dres, tag):
    t = x.shape[0]
    tm = 512

    def body(x_ref, g_ref, dh_ref, dr_ref, dx_ref, dg_ref):
        @pl.when(pl.program_id(0) == 0)
        def _():
            dg_ref[...] = jnp.zeros_like(dg_ref)

        xv = x_ref[...]
        r = lax.rsqrt(jnp.mean(xv * xv, axis=-1, keepdims=True) + RMS_EPS)
        xh = xv * r
        dh = dh_ref[...]
        u = dh * g_ref[...]
        dx_ref[...] = dr_ref[...] + r * (u - xh * jnp.mean(xh * u, axis=-1, keepdims=True))
        dg_ref[...] += jnp.sum(dh * xh, axis=0, keepdims=True)

    row = BS((tm, D), lambda i: (i, 0))
    vec = BS((1, D), lambda i: (0, 0))
    return pl.pallas_call(
        body, name=f"norm_bwd_{tag}", grid=(t // tm,),
        in_specs=[row, vec, row, row], out_specs=[row, vec],
        out_shape=[SDS((t, D), F32), SDS((1, D), F32)], compiler_params=_cp("arbitrary"),
    )(x, g.reshape(1, D), dh, dres)


def _loss_head(x, g, tgt):
    t = x.shape[0]
    tm = 512

    def body(x_ref, g_ref, t_ref, loss_ref, dx_ref, dg_ref):
        @pl.when(pl.program_id(0) == 0)
        def _():
            dg_ref[...] = jnp.zeros_like(dg_ref)
            loss_ref[...] = jnp.zeros_like(loss_ref)

        xv = x_ref[...]
        gv = g_ref[...]
        r = lax.rsqrt(jnp.mean(xv * xv, axis=-1, keepdims=True) + RMS_EPS)
        xh = xv * r
        e = xh * gv - t_ref[...]
        loss_ref[...] += 0.5 * jnp.sum(jnp.mean(e * e, axis=-1, keepdims=True), axis=0, keepdims=True)
        dy = e * (1.0 / D)
        u = dy * gv
        dx_ref[...] = r * (u - xh * jnp.mean(xh * u, axis=-1, keepdims=True))
        dg_ref[...] += jnp.sum(dy * xh, axis=0, keepdims=True)

    row = BS((tm, D), lambda i: (i, 0))
    vec = BS((1, D), lambda i: (0, 0))
    return pl.pallas_call(
        body, name="loss_head", grid=(t // tm,),
        in_specs=[row, vec, row], out_specs=[BS((1, 128), lambda i: (0, 0)), row, vec],
        out_shape=[SDS((1, 128), F32), SDS((t, D), F32), SDS((1, D), F32)],
        compiler_params=_cp("arbitrary"),
    )(x, g.reshape(1, D), tgt)


def _mm_nn(a, w, tag, res=None, scale=1.0, tm=512, tn=None):
    c_n, t, k = a.shape
    n = w.shape[2]
    tn = n if tn is None else tn

    def body(*refs):
        a_ref, w_ref = refs[0], refs[1]
        o_ref = refs[-1]
        acc = _dot_nn(a_ref[0].astype(BF16), w_ref[0])
        for c in range(1, c_n):
            acc = acc + _dot_nn(a_ref[c].astype(BF16), w_ref[c])
        if scale != 1.0:
            acc = acc * scale
        if res is not None:
            acc = refs[2][...] + acc
        o_ref[...] = acc

    in_specs = [BS((c_n, tm, k), lambda i, j: (0, i, 0)), BS((c_n, k, tn), lambda i, j: (0, 0, j))]
    args = [a, w]
    if res is not None:
        in_specs.append(BS((tm, tn), lambda i, j: (i, j)))
        args.append(res)
    return pl.pallas_call(
        body, name=f"mm_nn_{tag}", grid=(t // tm, n // tn), in_specs=in_specs,
        out_specs=BS((tm, tn), lambda i, j: (i, j)), out_shape=SDS((t, n), F32),
        compiler_params=_cp("parallel", "parallel"),
    )(*args)


def _mm_nt_rows(a, w, tag, tm, tn, n_total, w_row0):
    t, k = a.shape
    assert w_row0 % tn == 0 and n_total % tn == 0
    j0 = w_row0 // tn

    def body(a_ref, w_ref, o_ref):
        o_ref[...] = _dot_nt(a_ref[...].astype(BF16), w_ref[...])

    return pl.pallas_call(
        body, name=f"mm_nt_{tag}", grid=(t // tm, n_total // tn),
        in_specs=[BS((tm, k), lambda i, j: (i, 0)), BS((tn, k), lambda i, j: (j0 + j, 0))],
        out_specs=BS((tm, tn), lambda i, j: (i, j)), out_shape=SDS((t, n_total), F32),
        compiler_params=_cp("parallel", "parallel"),
    )(a, w)


def _mm_tn(a, b, tag, scale=1.0, tmm=None, tk=512):
    c_n, t, m = a.shape
    n = b.shape[1]
    tmm = m if tmm is None else tmm
    nk = t // tk

    def body(a_ref, b_ref, o_ref, acc_ref):
        kk = pl.program_id(2)

        @pl.when(kk == 0)
        def _():
            acc_ref[...] = jnp.zeros_like(acc_ref)

        acc_ref[...] += _dot_tn(a_ref[...].astype(BF16), b_ref[...].astype(BF16))

        @pl.when(kk == nk - 1)
        def _():
            o_ref[...] = (acc_ref[...] * scale).astype(BF16)

    return pl.pallas_call(
        body, name=f"mm_tn_{tag}", grid=(c_n, m // tmm, nk),
        in_specs=[BS((None, tk, tmm), lambda c, mi, kk: (c, kk, mi)), BS((tk, n), lambda c, mi, kk: (kk, 0))],
        out_specs=BS((None, tmm, n), lambda c, mi, kk: (c, mi, 0)),
        out_shape=SDS((c_n, m, n), BF16), scratch_shapes=[pltpu.VMEM((tmm, n), F32)],
        compiler_params=_cp("parallel", "parallel", "arbitrary"),
    )(a, b)


def _ffn_up(hn, wut, tag):
    t = hn.shape[0]
    tm, tn = 512, 1408

    def body(h_ref, w_ref, gu_ref, act_ref):
        h = h_ref[...]
        g = _dot_nt(h, w_ref[0])
        u = _dot_nt(h, w_ref[1])
        gu_ref[0] = g.astype(BF16)
        gu_ref[1] = u.astype(BF16)
        act_ref[...] = (g * jax.nn.sigmoid(g) * u).astype(BF16)

    return pl.pallas_call(
        body, name=f"ffn_up_{tag}", grid=(t // tm, F // tn),
        in_specs=[BS((tm, D), lambda i, j: (i, 0)), BS((2, tn, D), lambda i, j: (0, j, 0))],
        out_specs=[BS((2, tm, tn), lambda i, j: (0, i, j)), BS((tm, tn), lambda i, j: (i, j))],
        out_shape=[SDS((2, t, F), BF16), SDS((t, F), BF16)],
        compiler_params=_cp("parallel", "parallel"),
    )(hn, wut)


def _ffn_dact(dxo, wd, gu, tag):
    t = dxo.shape[0]
    tm, tn = 512, 1408

    def body(d_ref, w_ref, gu_ref, o_ref):
        dact = _dot_nt(d_ref[...].astype(BF16), w_ref[...]) * 0.5
        g = gu_ref[0].astype(F32)
        u = gu_ref[1].astype(F32)
        sg = jax.nn.sigmoid(g)
        o_ref[0] = (dact * u * (sg * (1.0 + g * (1.0 - sg)))).astype(BF16)
        o_ref[1] = (dact * (g * sg)).astype(BF16)

    return pl.pallas_call(
        body, name=f"ffn_dact_{tag}", grid=(t // tm, F // tn),
        in_specs=[BS((tm, D), lambda i, j: (i, 0)), BS((tn, D), lambda i, j: (j, 0)),
                  BS((2, tm, tn), lambda i, j: (0, i, j))],
        out_specs=BS((2, tm, tn), lambda i, j: (0, i, j)),
        out_shape=SDS((2, t, F), BF16), compiler_params=_cp("parallel", "parallel"),
    )(dxo, wd, gu)


def _rope_tables():
    half = HEAD_DIM // 2
    inv_freq = ROPE_THETA ** (-jnp.arange(half, dtype=F32) / half)
    ang = jnp.arange(S).astype(F32)[:, None] * inv_freq[None, :]
    cos, sin = jnp.cos(ang), jnp.sin(ang)
    return jnp.concatenate([cos, cos, cos, cos], axis=1), jnp.concatenate([-sin, sin, -sin, sin], axis=1)


def _swap_halves(t, first_half):
    return jnp.where(first_half, pltpu.roll(t, 96, 1), pltpu.roll(t, 32, 1))


def _rope_fwd(proj, cos_t, sin_t):
    t = proj.shape[0]
    tm = 512
    width = 2 * QKV_A // 3

    def body(x_ref, c_ref, s_ref, o_ref):
        c = c_ref[...]
        sg = s_ref[...]
        first = (lax.broadcasted_iota(jnp.int32, (tm, 128), 1) % HEAD_DIM) < HEAD_DIM // 2
        for j in range(width // 128):
            v = x_ref[:, 128 * j:128 * (j + 1)]
            o_ref[:, 128 * j:128 * (j + 1)] = v * c + _swap_halves(v, first) * sg

    tab = BS((tm, 128), lambda i: (i % (S // tm), 0))
    return pl.pallas_call(
        body, name="rope_fwd", grid=(t // tm,),
        in_specs=[BS((tm, width), lambda i: (i, 0)), tab, tab],
        out_specs=BS((tm, width), lambda i: (i, 0)), out_shape=SDS((t, width), F32),
        compiler_params=_cp("parallel"),
    )(proj, cos_t, sin_t)


def _rope_bwd(dqs, dks, cos_t, sin_t):
    t = dqs[0].shape[0]
    tm = 512

    def body(*refs):
        c = refs[6][...]
        sg = refs[7][...]
        o_ref = refs[8]
        first = (lax.broadcasted_iota(jnp.int32, (tm, 128), 1) % HEAD_DIM) < HEAD_DIM // 2
        for a in range(6):
            for hp in range(2):
                v = refs[a][:, 128 * hp:128 * (hp + 1)]
                col = 128 * (2 * a + hp)
                o_ref[:, col:col + 128] = (v * c + _swap_halves(v * sg, first)).astype(BF16)

    blk = BS((tm, 256), lambda i: (i, 0))
    tab = BS((tm, 128), lambda i: (i % (S // tm), 0))
    return pl.pallas_call(
        body, name="rope_bwd", grid=(t // tm,), in_specs=[blk] * 6 + [tab, tab],
        out_specs=BS((tm, 1536), lambda i: (i, 0)), out_shape=SDS((t, 1536), BF16),
        compiler_params=_cp("parallel"),
    )(*dqs, *dks, cos_t, sin_t)


def _head_masks():
    lane = lax.broadcasted_iota(jnp.int32, (1, 128), 1)
    m0 = (lane < HEAD_DIM).astype(F32)
    return m0, 1.0 - m0


def _dil_geometry(d):
    sub = S // d
    q_rows = 128
    k_rows = min(256, sub)
    return sub, q_rows, sub // q_rows, k_rows


def _dil_tile(idx, d):
    sub, q_rows, nb, k_rows = _dil_geometry(d)
    r = idx // nb
    n = idx % nb
    k_sub = jnp.clip(q_rows * n - HALF, 0, sub - k_rows)
    if d == 1:
        q_start = pl.multiple_of(q_rows * n, q_rows)
        k_start = pl.multiple_of(k_sub, HALF)
    else:
        q_start = q_rows * n * d + r
        k_start = k_sub * d + r
    ii = lax.broadcasted_iota(jnp.int32, (q_rows, k_rows), 0)
    jj = lax.broadcasted_iota(jnp.int32, (q_rows, k_rows), 1)
    valid = jnp.abs(jj - ii + (k_sub - q_rows * n)) <= HALF
    return q_start, k_start, valid


def _dil_specs(grp):
    qs = BS((S, 128), lambda b, hp: (b, 2 * grp + hp))
    ks = BS((S, 128), lambda b, hp: (b, 6 + 2 * grp + hp))
    vs = BS((S, 128), lambda b, hp: (b, 12 + 2 * grp + hp))
    own = BS((S, 128), lambda b, hp: (b, hp))
    return qs, ks, vs, own


def _dil_fwd(qkr, proj, grp):
    t = qkr.shape[0]
    d = DILATIONS[grp]
    _, q_rows, nb, k_rows = _dil_geometry(d)

    def body(q_ref, k_ref, v_ref, o_ref, l_ref):
        masks = _head_masks()

        def step(idx, carry):
            q_start, k_start, valid = _dil_tile(idx, d)
            q = q_ref[_ds(q_start, q_rows, d), :]
            kb = k_ref[_ds(k_start, k_rows, d), :].astype(BF16)
            v = v_ref[_ds(k_start, k_rows, d), :]
            o2 = jnp.zeros((q_rows, 128), F32)
            l2 = jnp.zeros((q_rows, 128), F32)
            for mh in masks:
                s = jnp.where(valid, _dot_nt((q * mh).astype(BF16), kb) * SCALE, NEG)
                mx = jnp.max(s, axis=1, keepdims=True)
                p = jnp.exp(s - mx)
                den = jnp.sum(p, axis=1, keepdims=True)
                o2 = o2 + _dot_nn(p.astype(BF16), (v * mh).astype(BF16)) / den
                l2 = l2 + (mx + jnp.log(den)) * mh
            o_ref[_ds(q_start, q_rows, d), :] = o2
            l_ref[_ds(q_start, q_rows, d), :] = l2
            return carry

        lax.fori_loop(0, d * nb, step, 0)

    qs, ks, vs, own = _dil_specs(grp)
    return pl.pallas_call(
        body, name=f"dil_fwd_{grp}", grid=(t // S, 2), in_specs=[qs, ks, vs], out_specs=[own, own],
        out_shape=[SDS((t, 256), F32), SDS((t, 256), F32)], compiler_params=_cp("parallel", "parallel"),
    )(qkr, qkr, proj)


def _dil_bwd(qkr, proj, do, dlp, lse, grp):
    t = qkr.shape[0]
    d = DILATIONS[grp]
    _, q_rows, nb, k_rows = _dil_geometry(d)

    def body(q_ref, k_ref, v_ref, do_ref, dl_ref, l_ref, dq_ref, dk_ref, dv_ref):
        masks = _head_masks()
        dk_ref[...] = jnp.zeros_like(dk_ref)
        dv_ref[...] = jnp.zeros_like(dv_ref)

        def step(idx, carry):
            q_start, k_start, valid = _dil_tile(idx, d)
            q_rows_ds = _ds(q_start, q_rows, d)
            k_rows_ds = _ds(k_start, k_rows, d)
            q = q_ref[q_rows_ds, :]
            k = k_ref[k_rows_ds, :]
            kb = k.astype(BF16)
            vb = v_ref[k_rows_ds, :].astype(BF16)
            do2 = do_ref[q_rows_ds, :]
            dl2 = dl_ref[q_rows_ds, :]
            l2 = l_ref[q_rows_ds, :]
            dq2 = jnp.zeros((q_rows, 128), F32)
            dkw = jnp.zeros((k_rows, 128), F32)
            dvw = jnp.zeros((k_rows, 128), F32)
            for h, mh in enumerate(masks):
                col = HEAD_DIM * h
                qh = (q * mh).astype(BF16)
                s = jnp.where(valid, _dot_nt(qh, kb) * SCALE, NEG)
                p = jnp.exp(s - l2[:, col:col + 1])
                doh = (do2 * mh).astype(BF16)
                ds = (p * (_dot_nt(doh, vb) - dl2[:, col:col + 1])).astype(BF16)
                dq2 = dq2 + _dot_nn(ds, (k * mh).astype(BF16))
                dkw = dkw + _dot_tn(ds, qh)
                dvw = dvw + _dot_tn(p.astype(BF16), doh)
            dq_ref[q_rows_ds, :] = dq2 * SCALE
            dk_ref[k_rows_ds, :] += dkw * SCALE
            dv_ref[k_rows_ds, :] += dvw
            return carry

        lax.fori_loop(0, d * nb, step, 0)

    qs, ks, vs, own = _dil_specs(grp)
    return pl.pallas_call(
        body, name=f"dil_bwd_{grp}", grid=(t // S, 2), in_specs=[qs, ks, vs, own, own, own],
        out_specs=[own, own, own], out_shape=[SDS((t, 256), F32)] * 3,
        compiler_params=_cp("parallel", "parallel"),
    )(qkr, qkr, proj, do, dlp, lse)


def _mix_weights(l0, l1, l2):
    mx = jnp.maximum(jnp.maximum(l0, l1), l2)
    e0, e1, e2 = jnp.exp(l0 - mx), jnp.exp(l1 - mx), jnp.exp(l2 - mx)
    den = e0 + e1 + e2
    return e0 / den, e1 / den, e2 / den


def _combine_fwd(outs, lses):
    t = outs[0].shape[0]
    tm = 512

    def body(o0, o1, o2, l0, l1, l2, y_ref):
        w0, w1, w2 = _mix_weights(l0[...], l1[...], l2[...])
        y_ref[...] = w0 * o0[...] + w1 * o1[...] + w2 * o2[...]

    blk = BS((tm, 256), lambda i: (i, 0))
    return pl.pallas_call(
        body, name="combine_fwd", grid=(t // tm,), in_specs=[blk] * 6, out_specs=blk,
        out_shape=SDS((t, 256), F32), compiler_params=_cp("parallel"),
    )(*outs, *lses)


def _head_sum(x):
    a = lax.broadcasted_iota(jnp.int32, (256, 256), 0) // HEAD_DIM
    b = lax.broadcasted_iota(jnp.int32, (256, 256), 1) // HEAD_DIM
    ones = (a == b).astype(BF16)
    hi = x.astype(BF16)
    lo = (x - hi.astype(F32)).astype(BF16)
    return _dot_nn(hi, ones) + _dot_nn(lo, ones)


def _combine_bwd(dya, outs, lses):
    t = dya.shape[0]
    tm = 512

    def body(dy_ref, o0, o1, o2, l0, l1, l2, d0, d1, d2, e0, e1, e2):
        ws = _mix_weights(l0[...], l1[...], l2[...])
        dy = dy_ref[...]
        ya = ws[0] * o0[...] + ws[1] * o1[...] + ws[2] * o2[...]
        hs = _head_sum(dy * ya)
        for w, d_ref, e_ref in zip(ws, (d0, d1, d2), (e0, e1, e2)):
            d_ref[...] = w * dy
            e_ref[...] = w * hs

    blk = BS((tm, 256), lambda i: (i, 0))
    return pl.pallas_call(
        body, name="combine_bwd", grid=(t // tm,), in_specs=[blk] * 7, out_specs=[blk] * 6,
        out_shape=[SDS((t, 256), F32)] * 6, compiler_params=_cp("parallel"),
    )(dya, *outs, *lses)


def _na_bias_table(rel_bias):
    qc = np.arange(GRID_W)[:, None]
    kc = np.arange(GRID_W)[None, :]
    win_lo = np.clip(qc - 8, 0, GRID_W - 16)
    col_valid = (kc >= win_lo) & (kc < win_lo + 16)
    col_idx = np.clip(kc - qc + 15, 0, 30)
    row_idx = np.arange(NA_KR)[:, None] + np.arange(NA_KR)[None, :]
    b = rel_bias.astype(F32)[:, row_idx][..., col_idx]
    b = jnp.where(col_valid[None, None, None], b, NEG)
    return b.transpose(0, 1, 3, 2, 4).reshape(8, NA_KR, GRID_W, NA_KR * GRID_W)


def _na_row(i):
    lo = jnp.clip(i - NA_KR // 2, 0, NA_ROWS - NA_KR)
    return pl.multiple_of(GRID_W * i, GRID_W), pl.multiple_of(GRID_W * lo, GRID_W), lo - i + NA_KR - 1


def _na_fwd(proj, bias):
    t = proj.shape[0]
    kw = NA_KR * GRID_W

    def body(q_ref, k_ref, v_ref, b_ref, o_ref, l_ref):
        masks = _head_masks()

        def step(i, carry):
            q_start, k_start, cls = _na_row(i)
            q = q_ref[pl.ds(q_start, GRID_W), :]
            kb = k_ref[pl.ds(k_start, kw), :].astype(BF16)
            v = v_ref[pl.ds(k_start, kw), :]
            o2 = jnp.zeros((GRID_W, 128), F32)
            l2 = jnp.zeros((GRID_W, 128), F32)
            for h, mh in enumerate(masks):
                s = _dot_nt((q * mh).astype(BF16), kb) * SCALE + b_ref[h, cls]
                mx = jnp.max(s, axis=1, keepdims=True)
                p = jnp.exp(s - mx)
                den = jnp.sum(p, axis=1, keepdims=True)
                o2 = o2 + _dot_nn((p / den).astype(BF16), (v * mh).astype(BF16))
                l2 = l2 + (mx + jnp.log(den)) * mh
            o_ref[pl.ds(q_start, GRID_W), :] = o2
            l_ref[pl.ds(q_start, GRID_W), :] = l2
            return carry

        lax.fori_loop(0, NA_ROWS, step, 0)

    c0 = QKV_A // 128
    own = BS((S, 128), lambda b, hp: (b, hp))
    return pl.pallas_call(
        body, name="na_fwd", grid=(t // S, 4),
        in_specs=[BS((S, 128), lambda b, hp: (b, c0 + hp)), BS((S, 128), lambda b, hp: (b, c0 + 4 + hp)),
                  BS((S, 128), lambda b, hp: (b, c0 + 8 + hp)),
                  BS((2, NA_KR, GRID_W, kw), lambda b, hp: (hp, 0, 0, 0))],
        out_specs=[own, own], out_shape=[SDS((t, 512), F32), SDS((t, 512), F32)],
        compiler_params=_cp("parallel", "parallel"),
    )(proj, proj, proj, bias)


def _na_bwd(proj, bias, dyb, yb, lse):
    t = proj.shape[0]
    kw = NA_KR * GRID_W

    def body(q_ref, k_ref, v_ref, b_ref, do_ref, o_ref, l_ref, dq_ref, dk_ref, dv_ref, db_ref):
        masks = _head_masks()

        @pl.when(pl.program_id(1) == 0)
        def _():
            db_ref[...] = jnp.zeros_like(db_ref)

        dk_ref[...] = jnp.zeros_like(dk_ref)
        dv_ref[...] = jnp.zeros_like(dv_ref)

        def step(i, carry):
            q_start, k_start, cls = _na_row(i)
            q = q_ref[pl.ds(q_start, GRID_W), :]
            k = k_ref[pl.ds(k_start, kw), :]
            kb = k.astype(BF16)
            vb = v_ref[pl.ds(k_start, kw), :].astype(BF16)
            do2 = do_ref[pl.ds(q_start, GRID_W), :]
            o2 = o_ref[pl.ds(q_start, GRID_W), :]
            l2 = l_ref[pl.ds(q_start, GRID_W), :]
            dq2 = jnp.zeros((GRID_W, 128), F32)
            dkw = jnp.zeros((kw, 128), F32)
            dvw = jnp.zeros((kw, 128), F32)
            for h, mh in enumerate(masks):
                col = HEAD_DIM * h
                delta = jnp.sum(do2 * o2 * mh, axis=1, keepdims=True)
                qh = (q * mh).astype(BF16)
                s = _dot_nt(qh, kb) * SCALE + b_ref[h, cls]
                p = jnp.exp(s - l2[:, col:col + 1])
                doh = (do2 * mh).astype(BF16)
                ds = p * (_dot_nt(doh, vb) - delta)
                db_ref[h, cls] += ds
                dsb = ds.astype(BF16)
                dq2 = dq2 + _dot_nn(dsb, (k * mh).astype(BF16))
                dkw = dkw + _dot_tn(dsb, qh)
                dvw = dvw + _dot_tn(p.astype(BF16), doh)
            dq_ref[pl.ds(q_start, GRID_W), :] = dq2 * SCALE
            dk_ref[pl.ds(k_start, kw), :] += dkw * SCALE
            dv_ref[pl.ds(k_start, kw), :] += dvw
            return carry

        lax.fori_loop(0, NA_ROWS, step, 0)

    c0 = QKV_A // 128
    own = BS((S, 128), lambda hp, b: (b, hp))
    tab = BS((2, NA_KR, GRID_W, kw), lambda hp, b: (hp, 0, 0, 0))
    return pl.pallas_call(
        body, name="na_bwd", grid=(4, t // S),
        in_specs=[BS((S, 128), lambda hp, b: (b, c0 + hp)), BS((S, 128), lambda hp, b: (b, c0 + 4 + hp)),
                  BS((S, 128), lambda hp, b: (b, c0 + 8 + hp)), tab, own, own, own],
        out_specs=[own, own, own, tab],
        out_shape=[SDS((t, 512), F32)] * 3 + [SDS((8, NA_KR, GRID_W, kw), F32)],
        compiler_params=_cp("parallel", "arbitrary"),
    )(proj, proj, proj, bias, dyb, yb, lse)


def _na_dbias(db):
    kw = NA_KR * GRID_W

    def body(x_ref, o_ref, z_ref):
        lane = lax.broadcasted_iota(jnp.int32, (GRID_W, kw), 1)
        sub = lax.broadcasted_iota(jnp.int32, (GRID_W, kw), 0)
        rel = (lane % GRID_W) - sub + 15
        key_row = lax.broadcasted_iota(jnp.int32, (kw, 128), 0) // GRID_W
        dr = lax.broadcasted_iota(jnp.int32, (kw, 128), 1)
        acc = jnp.zeros((32, 128), F32)
        z_ref[...] = jnp.zeros_like(z_ref)
        for cls in range(NA_KR):
            xv = x_ref[cls]
            for dc in range(31):
                z_ref[dc:dc + 1, :] = jnp.sum(jnp.where(rel == dc, xv, 0.0), axis=0, keepdims=True)
            z = z_ref[...]
            ind = (key_row + cls == dr).astype(BF16)
            hi = z.astype(BF16)
            lo = (z - hi.astype(F32)).astype(BF16)
            acc = acc + _dot_nn(hi, ind) + _dot_nn(lo, ind)
        o_ref[...] = acc

    return pl.pallas_call(
        body, name="na_dbias", grid=(8,),
        in_specs=[BS((None, NA_KR, GRID_W, kw), lambda h: (h, 0, 0, 0))],
        out_specs=BS((None, 32, 128), lambda h: (h, 0, 0)), out_shape=SDS((8, 32, 128), F32),
        scratch_shapes=[pltpu.VMEM((32, kw), F32)], compiler_params=_cp("parallel"),
    )(db)


def _merge_fwd(ya, yb, proj, wat, wbt):
    t = ya.shape[0]
    tm, tn = 512, 256
    ca = (QKV_A + QKV_B) // tn
    cb = ca + D // tn

    def body(ya_ref, yb_ref, la_ref, lb_ref, wa_ref, wb_ref, m_ref, za_ref, zb_ref):
        za = _dot_nt(ya_ref[...].astype(BF16), wa_ref[...])
        zb = _dot_nt(yb_ref[...].astype(BF16), wb_ref[...])
        m_ref[...] = (jax.nn.sigmoid(la_ref[...]) * za + jax.nn.sigmoid(lb_ref[...]) * zb).astype(BF16)
        za_ref[...] = za.astype(BF16)
        zb_ref[...] = zb.astype(BF16)

    out = BS((tm, tn), lambda i, j: (i, j))
    return pl.pallas_call(
        body, name="merge_fwd", grid=(t // tm, D // tn),
        in_specs=[BS((tm, 256), lambda i, j: (i, 0)), BS((tm, 512), lambda i, j: (i, 0)),
                  BS((tm, tn), lambda i, j: (i, ca + j)), BS((tm, tn), lambda i, j: (i, cb + j)),
                  BS((tn, 256), lambda i, j: (j, 0)), BS((tn, 512), lambda i, j: (j, 0))],
        out_specs=[out, out, out], out_shape=[SDS((t, D), BF16)] * 3,
        compiler_params=_cp("parallel", "parallel"),
    )(ya, yb, proj, proj, wat, wbt)


def _merge_bwd(dm, za, zb, proj):
    t = dm.shape[0]
    tm, tn = 512, 256
    ca = (QKV_A + QKV_B) // tn
    cb = ca + D // tn

    def body(dm_ref, za_ref, zb_ref, la_ref, lb_ref, dza_ref, dzb_ref, dl_ref):
        dmv = dm_ref[...]
        ga = jax.nn.sigmoid(la_ref[...])
        gb = jax.nn.sigmoid(lb_ref[...])
        dza_ref[...] = (dmv * ga).astype(BF16)
        dzb_ref[...] = (dmv * gb).astype(BF16)
        dl_ref[0] = (dmv * za_ref[...].astype(F32) * ga * (1.0 - ga)).astype(BF16)
        dl_ref[1] = (dmv * zb_ref[...].astype(F32) * gb * (1.0 - gb)).astype(BF16)

    blk = BS((tm, tn), lambda i, j: (i, j))
    return pl.pallas_call(
        body, name="merge_bwd", grid=(t // tm, D // tn),
        in_specs=[blk, blk, blk, BS((tm, tn), lambda i, j: (i, ca + j)), BS((tm, tn), lambda i, j: (i, cb + j))],
        out_specs=[blk, blk, BS((2, tm, tn), lambda i, j: (0, i, j))],
        out_shape=[SDS((t, D), BF16), SDS((t, D), BF16), SDS((2, t, D), BF16)],
        compiler_params=_cp("parallel", "parallel"),
    )(dm, za, zb, proj, proj)


def _sum_slots(recv0, recv1, tag):
    _, r, c = recv0.shape
    tr = r if r * c <= 512 * 1024 else r // 2

    def body(a_ref, b_ref, o_ref):
        for layer, ref in enumerate((a_ref, b_ref)):
            acc = ref[0].astype(F32)
            for s in range(1, N_DEV):
                acc = acc + ref[s].astype(F32)
            o_ref[layer] = acc

    blk = BS((N_DEV, tr, c), lambda i: (0, i, 0))
    return pl.pallas_call(
        body, name=f"sum_slots_{tag}", grid=(r // tr,), in_specs=[blk, blk],
        out_specs=BS((2, tr, c), lambda i: (0, i, 0)), out_shape=SDS((2, r, c), F32),
        compiler_params=_cp("parallel"),
    )(recv0, recv1)


def _adamw(w, g, m, v, tag):
    layers, r, c = w.shape
    tr = 256 if r % 256 == 0 and r > 256 else r

    def body(w_ref, g_ref, m_ref, v_ref, d_ref, mo_ref, vo_ref):
        gv = g_ref[...]
        mn = ADAM_B1 * m_ref[...] + (1.0 - ADAM_B1) * gv
        vn = ADAM_B2 * v_ref[...] + (1.0 - ADAM_B2) * (gv * gv)
        m_hat = mn / (1.0 - ADAM_B1 ** ADAM_STEP)
        v_hat = vn / (1.0 - ADAM_B2 ** ADAM_STEP)
        d_ref[...] = -ADAM_LR * (m_hat / (jnp.sqrt(v_hat) + ADAM_EPS) + ADAM_WD * w_ref[...])
        mo_ref[...] = mn
        vo_ref[...] = vn

    blk = BS((None, tr, c), lambda l, i: (l, i, 0))
    return pl.pallas_call(
        body, name=f"adamw_{tag}", grid=(layers, r // tr), in_specs=[blk] * 4, out_specs=[blk] * 3,
        out_shape=[SDS((layers, r, c), F32)] * 3, compiler_params=_cp("parallel", "parallel"),
    )(w, g, m, v)


def _place():
    return lax.axis_index("x"), lax.axis_index("y"), lax.axis_index("c")


def _flip(coord, bit):
    return 1 - coord if bit else coord


def _allgather(shards, tag):
    n_arr = len(shards)
    hbm = BS(memory_space=pl.ANY)

    def body(*refs):
        ins, outs = refs[:n_arr], refs[n_arr:2 * n_arr]
        send_sems, recv_sems, local_sems = refs[2 * n_arr:]
        x, y, c = _place()
        me, sibling = (x, y, c), (x, y, 1 - c)
        chips = [(1 - x, y), (x, 1 - y), (1 - x, 1 - y)]

        def rows(a, p):
            r = shards[a].shape[0]
            return outs[a].at[pl.ds((4 * p[0] + 2 * p[1] + p[2]) * r, r), :]

        def copy(a, k, block, to, src=None):
            return pltpu.make_async_remote_copy(
                src_ref=rows(a, block) if src is None else src, dst_ref=rows(a, block),
                send_sem=send_sems.at[a, k], recv_sem=recv_sems.at[a, k], device_id=to, device_id_type=MESH)

        mine = [pltpu.make_async_copy(ins[a], rows(a, me), local_sems.at[a]) for a in range(n_arr)]
        for cp in mine:
            cp.start()
        first = []
        for a in range(n_arr):
            first.append(copy(a, 0, me, sibling, src=ins[a]))
            first += [copy(a, 1 + j, me, (*chip, c), src=ins[a]) for j, chip in enumerate(chips)]
        for cp in first:
            cp.start()
        passed = []
        for a in range(n_arr):
            for j, chip in enumerate(chips):
                copy(a, 1 + j, (*chip, c), me).wait_recv()
                passed.append(copy(a, 4 + j, (*chip, c), sibling))
                passed[-1].start()
        for a in range(n_arr):
            copy(a, 0, sibling, me).wait_recv()
            for j, chip in enumerate(chips):
                copy(a, 4 + j, (*chip, 1 - c), me).wait_recv()
        for cp in first + passed:
            cp.wait_send()
        for cp in mine:
            cp.wait()

    return pl.pallas_call(
        body, name=f"allgather_{tag}", in_specs=[hbm] * n_arr, out_specs=[hbm] * n_arr,
        out_shape=[SDS((N_DEV * s.shape[0], s.shape[1]), s.dtype) for s in shards],
        scratch_shapes=[pltpu.SemaphoreType.DMA((n_arr, 7)), pltpu.SemaphoreType.DMA((n_arr, 7)),
                        pltpu.SemaphoreType.DMA((n_arr,))],
        compiler_params=pltpu.CompilerParams(has_side_effects=True),
    )(*shards)


def _scatter_blocks(fulls, tag):
    n_arr = len(fulls)
    hbm = BS(memory_space=pl.ANY)

    def body(*refs):
        ins, outs = refs[:n_arr], refs[n_arr:2 * n_arr]
        send_sems, recv_sems, local_sems = refs[2 * n_arr:]
        x, y, c = _place()
        me = 4 * x + 2 * y + c
        peers = []
        for mask in range(1, N_DEV):
            p = (_flip(x, mask & 4), _flip(y, mask & 2), _flip(c, mask & 1))
            peers.append((p, 4 * p[0] + 2 * p[1] + p[2]))

        def block(a, idx):
            r = fulls[a].shape[0] // N_DEV
            return ins[a].at[pl.ds(idx * r, r), :]

        def copy(a, k, slot):
            p, idx = peers[k]
            return pltpu.make_async_remote_copy(
                src_ref=block(a, idx), dst_ref=outs[a].at[slot], send_sem=send_sems.at[a, k],
                recv_sem=recv_sems.at[a, k], device_id=p, device_id_type=MESH)

        mine = [pltpu.make_async_copy(block(a, me), outs[a].at[me], local_sems.at[a]) for a in range(n_arr)]
        for cp in mine:
            cp.start()
        sends = [copy(a, k, me) for a in range(n_arr) for k in range(N_DEV - 1)]
        for cp in sends:
            cp.start()
        for a in range(n_arr):
            for k in range(N_DEV - 1):
                copy(a, k, peers[k][1]).wait_recv()
        for cp in sends:
            cp.wait_send()
        for cp in mine:
            cp.wait()

    return pl.pallas_call(
        body, name=f"scatter_blocks_{tag}", in_specs=[hbm] * n_arr, out_specs=[hbm] * n_arr,
        out_shape=[SDS((N_DEV, f.shape[0] // N_DEV, f.shape[1]), f.dtype) for f in fulls],
        scratch_shapes=[pltpu.SemaphoreType.DMA((n_arr, 7)), pltpu.SemaphoreType.DMA((n_arr, 7)),
                        pltpu.SemaphoreType.DMA((n_arr,))],
        compiler_params=pltpu.CompilerParams(has_side_effects=True),
    )(*fulls)


def _allreduce_small(vec):
    rows = vec.shape[0]

    def body(x_ref, o_ref, buf_ref, send_sems, recv_sems):
        x, y, c = _place()
        me = 4 * x + 2 * y + c
        buf_ref[me] = x_ref[...]
        peers = []
        for mask in range(1, N_DEV):
            p = (_flip(x, mask & 4), _flip(y, mask & 2), _flip(c, mask & 1))
            peers.append((p, 4 * p[0] + 2 * p[1] + p[2]))

        def copy(k, slot):
            return pltpu.make_async_remote_copy(
                src_ref=x_ref, dst_ref=buf_ref.at[slot], send_sem=send_sems.at[k], recv_sem=recv_sems.at[k],
                device_id=peers[k][0], device_id_type=MESH)

        sends = [copy(k, me) for k in range(N_DEV - 1)]
        for cp in sends:
            cp.start()
        for k in range(N_DEV - 1):
            copy(k, peers[k][1]).wait_recv()
        for cp in sends:
            cp.wait_send()
        acc = buf_ref[0]
        for s in range(1, N_DEV):
            acc = acc + buf_ref[s]
        o_ref[...] = acc

    vmem = BS(memory_space=pltpu.VMEM)
    return pl.pallas_call(
        body, name="allreduce_small", in_specs=[vmem], out_specs=vmem, out_shape=SDS((rows, 128), F32),
        scratch_shapes=[pltpu.VMEM((N_DEV, rows, 128), F32), pltpu.SemaphoreType.DMA((7,)),
                        pltpu.SemaphoreType.DMA((7,))],
        compiler_params=pltpu.CompilerParams(has_side_effects=True),
    )(vec)


def _ffn_forward(x, norm_g, wut, wd, tag):
    hn = _norm_fwd(x, norm_g, tag)
    gu, act = _ffn_up(hn, wut, tag)
    out = _mm_nn(act[None], wd[None], f"down_{tag}", res=x, scale=0.5)
    return out, (x, hn, gu, act)


def _ffn_backward(dxo, saved, norm_g, wut, wd, tag):
    x, hn, gu, act = saved
    du = _ffn_dact(dxo, wd, gu, tag)
    d_wd = _mm_tn(act[None], dxo, f"dwd_{tag}", scale=0.5, tmm=1408)[0]
    d_wut = _mm_tn(du, hn, f"dwu_{tag}", tmm=1408)
    dhn = _mm_nn(du, wut, f"dhn_{tag}", tn=512)
    dx, dg = _norm_bwd(x, norm_g, dhn, dxo, tag)
    return dx, dg, d_wut.reshape(2 * F, D), d_wd


def _mixer_forward(x, norm_g, w, bias, tables, tag):
    wint, wat, wbt, wo = w
    hn = _norm_fwd(x, norm_g, tag)
    proj = _mm_nt_rows(hn, wint, f"proj_{tag}", 512, IN_W // 2, IN_W, 0)
    qkr = _rope_fwd(proj, *tables)
    outs, lses = [], []
    for grp in range(3):
        o, l = _dil_fwd(qkr, proj, grp)
        outs.append(o)
        lses.append(l)
    ya = _combine_fwd(outs, lses)
    yb, lse_b = _na_fwd(proj, bias)
    merged, za, zb = _merge_fwd(ya, yb, proj, wat, wbt)
    out = _mm_nn(merged[None], wo[None], f"out_{tag}", res=x)
    return out, (x, hn, proj, qkr, outs, lses, ya, yb, lse_b, merged, za, zb)


def _mixer_backward(dxo, saved, norm_g, w, bias, tables, tag):
    wint, wat, wbt, wo = w
    x, hn, proj, qkr, outs, lses, ya, yb, lse_b, merged, za, zb = saved
    dm = _mm_nt_rows(dxo, wo, f"dmerged_{tag}", 512, D, D, 0)
    d_wo = _mm_tn(merged[None], dxo, f"dwo_{tag}")[0]
    dza, dzb, dlog = _merge_bwd(dm, za, zb, proj)
    dya = _mm_nn(dza[None], wat[None], f"dya_{tag}")
    dyb = _mm_nn(dzb[None], wbt[None], f"dyb_{tag}")
    d_wat = _mm_tn(dza[None], ya, f"dwa_{tag}")[0]
    d_wbt = _mm_tn(dzb[None], yb, f"dwb_{tag}")[0]
    cb = _combine_bwd(dya, outs, lses)
    dqs, dks, dvs = [], [], []
    for grp in range(3):
        dq, dk, dv = _dil_bwd(qkr, proj, cb[grp], cb[3 + grp], lses[grp], grp)
        dqs.append(dq)
        dks.append(dk)
        dvs.append(dv)
    dqk = _rope_bwd(dqs, dks, *tables)
    dqb, dkb, dvb, dbias_tab = _na_bwd(proj, bias, dyb, yb, lse_b)
    dbias = _na_dbias(dbias_tab)
    dproj = jnp.concatenate(
        [dqk] + [t.astype(BF16) for t in (*dvs, dqb, dkb, dvb)] + [dlog[0], dlog[1]], axis=1)
    d_wint = _mm_tn(dproj[None], hn, f"dwin_{tag}", tmm=2944)[0]
    dhn = _mm_nn(dproj[None], wint[None], f"dhnm_{tag}", tm=256, tn=512)
    dx, dg = _norm_bwd(x, norm_g, dhn, dxo, f"mix_{tag}")
    dbias = dbias[:, :31, :15].transpose(0, 2, 1)
    return dx, dg, dbias, d_wint, d_wat, d_wbt, d_wo


def _pack_small(norms, biases, final, loss=None):
    parts = []
    for layer in range(DEPTH):
        parts += [norms[0][layer], norms[1][layer], norms[2][layer],
                  jnp.pad(biases[layer].reshape(-1), (0, BIAS_PAD - 8 * 15 * 31))]
    parts.append(final)
    flat = jnp.concatenate([p.reshape(-1).astype(F32) for p in parts])
    if loss is not None:
        flat = jnp.concatenate([flat, loss.reshape(-1)])
    return jnp.pad(flat, (0, SMALL_ROWS * 128 - flat.shape[0])).reshape(SMALL_ROWS, 128)


def _unpack_small(packed):
    flat = packed.reshape(-1)
    norms, biases = ([], [], []), []
    pos = 0
    for _ in range(DEPTH):
        for k in range(3):
            norms[k].append(flat[pos:pos + D])
            pos += D
        biases.append(flat[pos:pos + 8 * 15 * 31].reshape(8, 15, 31))
        pos += BIAS_PAD
    final = flat[pos:pos + D]
    pos += D
    return [jnp.stack(n) for n in norms], jnp.stack(biases), final, flat[pos]


def kernel(x, ffn1_norm, ffn1_w_up, ffn1_w_down, mix_norm, w_in, na_rel_bias, w_branch_a, w_branch_b, w_out, ffn2_norm, ffn2_w_up, ffn2_w_down, final_norm, loss_target, m_ffn1_norm, m_ffn1_w_up, m_ffn1_w_down, m_mix_norm, m_w_in, m_na_rel_bias, m_w_branch_a, m_w_branch_b, m_w_out, m_ffn2_norm, m_ffn2_w_up, m_ffn2_w_down, m_final_norm, v_ffn1_norm, v_ffn1_w_up, v_ffn1_w_down, v_mix_norm, v_w_in, v_na_rel_bias, v_w_branch_a, v_w_branch_b, v_w_out, v_ffn2_norm, v_ffn2_w_up, v_ffn2_w_down, v_final_norm):
    t = x.shape[0] * x.shape[1]
    xs = x.reshape(t, D)
    tgt = loss_target.reshape(t, D)
    tables = _rope_tables()

    col_sharded = dict(up1=ffn1_w_up, win=w_in, wa=w_branch_a, wb=w_branch_b, up2=ffn2_w_up)
    row_sharded = dict(down1=ffn1_w_down, wo=w_out, down2=ffn2_w_down)
    order = ("up1", "down1", "win", "wa", "wb", "wo", "up2", "down2")
    shards = []
    for layer in range(DEPTH):
        for name in order:
            if name in col_sharded:
                shards.append(col_sharded[name][layer].T.astype(BF16))
            else:
                shards.append(row_sharded[name][layer].astype(BF16))
    full = _allgather(shards, "weights")
    weights = [dict(zip(order, full[layer * len(order):(layer + 1) * len(order)])) for layer in range(DEPTH)]

    saved = []
    h = xs
    for layer in range(DEPTH):
        w = weights[layer]
        bias = _na_bias_table(na_rel_bias[layer])
        h, s1 = _ffn_forward(h, ffn1_norm[layer], w["up1"].reshape(2, F, D), w["down1"], f"f1l{layer}")
        h, s2 = _mixer_forward(h, mix_norm[layer], (w["win"], w["wa"], w["wb"], w["wo"]), bias, tables, f"l{layer}")
        h, s3 = _ffn_forward(h, ffn2_norm[layer], w["up2"].reshape(2, F, D), w["down2"], f"f2l{layer}")
        saved.append((s1, s2, s3, bias))
    loss_part, dh, d_final = _loss_head(h, final_norm, tgt)

    grads = [None] * DEPTH
    d_norms = ([None] * DEPTH, [None] * DEPTH, [None] * DEPTH)
    d_bias = [None] * DEPTH
    for layer in reversed(range(DEPTH)):
        w = weights[layer]
        s1, s2, s3, bias = saved[layer]
        dh, d_norms[2][layer], d_up2, d_down2 = _ffn_backward(
            dh, s3, ffn2_norm[layer], w["up2"].reshape(2, F, D), w["down2"], f"f2l{layer}")
        dh, d_norms[1][layer], d_bias[layer], d_win, d_wa, d_wb, d_wo = _mixer_backward(
            dh, s2, mix_norm[layer], (w["win"], w["wa"], w["wb"], w["wo"]), bias, tables, f"l{layer}")
        dh, d_norms[0][layer], d_up1, d_down1 = _ffn_backward(
            dh, s1, ffn1_norm[layer], w["up1"].reshape(2, F, D), w["down1"], f"f1l{layer}")
        grads[layer] = dict(up1=d_up1, down1=d_down1, win=d_win, wa=d_wa, wb=d_wb, wo=d_wo, up2=d_up2, down2=d_down2)
    grad_x = dh.reshape(x.shape)

    recv = _scatter_blocks([grads[layer][name] for layer in range(DEPTH) for name in order], "grads")
    small = _allreduce_small(_pack_small(d_norms, d_bias, d_final, loss_part[0, :1]))
    g_norms, g_bias, g_final, loss = _unpack_small(small)

    originals = dict(up1=(ffn1_w_up, m_ffn1_w_up, v_ffn1_w_up), down1=(ffn1_w_down, m_ffn1_w_down, v_ffn1_w_down),
                     win=(w_in, m_w_in, v_w_in), wa=(w_branch_a, m_w_branch_a, v_w_branch_a),
                     wb=(w_branch_b, m_w_branch_b, v_w_branch_b), wo=(w_out, m_w_out, v_w_out),
                     up2=(ffn2_w_up, m_ffn2_w_up, v_ffn2_w_up), down2=(ffn2_w_down, m_ffn2_w_down, v_ffn2_w_down))
    big = {}
    for k, name in enumerate(order):
        g = _sum_slots(recv[k], recv[len(order) + k], name)
        if name in col_sharded:
            g = jnp.swapaxes(g, 1, 2)
        wv, mv, vv = originals[name]
        big[name] = (g, *_adamw(wv, g, mv, vv, name))

    w_small = _pack_small((ffn1_norm, mix_norm, ffn2_norm), na_rel_bias, final_norm)
    m_small = _pack_small((m_ffn1_norm, m_mix_norm, m_ffn2_norm), m_na_rel_bias, m_final_norm)
    v_small = _pack_small((v_ffn1_norm, v_mix_norm, v_ffn2_norm), v_na_rel_bias, v_final_norm)
    upd = _adamw(w_small[None], small[None], m_small[None], v_small[None], "small")
    small_out = [(g_norms, g_bias, g_final)] + [_unpack_small(u[0])[:3] for u in upd]

    outputs = [loss, grad_x]
    for kind in range(4):
        norms, bias_k, final_k = small_out[kind]
        outputs += [norms[0], big["up1"][kind], big["down1"][kind], norms[1], big["win"][kind], bias_k,
                    big["wa"][kind], big["wb"][kind], big["wo"][kind], norms[2], big["up2"][kind],
                    big["down2"][kind], final_k]
    return tuple(outputs)
```

```python
import numpy as np

import jax
import jax.numpy as jnp
from jax import lax
from jax.experimental import pallas as pl
from jax.experimental.pallas import tpu as pltpu

F32 = jnp.float32
BF16 = jnp.bfloat16
SDS = jax.ShapeDtypeStruct
BS = pl.BlockSpec
MESH = pl.DeviceIdType.MESH

D = 1024
S = 2048
F = 2816
DEPTH = 2
HEAD_DIM = 64
DILATIONS = (1, 4, 16)
HALF = 64
QKV_A = 2304
QKV_B = 1536
IN_W = 5888
N_DEV = 8
NA_ROWS = 32
GRID_W = 64
NA_KR = 8
ROPE_THETA = 10000.0
RMS_EPS = 1e-6
NEG = -1e30
SCALE = HEAD_DIM ** -0.5
ADAM_LR, ADAM_B1, ADAM_B2, ADAM_EPS, ADAM_WD, ADAM_STEP = 0.001, 0.9, 0.999, 1e-08, 0.01, 10
VMEM_LIMIT_V7X = 52 * 1024 * 1024
SMALL_ROWS = 120
BIAS_PAD = 3840


def _cp(*sem):
    return pltpu.CompilerParams(dimension_semantics=sem, vmem_limit_bytes=VMEM_LIMIT_V7X)


def _dot_nn(a, b):
    return jnp.dot(a, b, preferred_element_type=F32)


def _dot_nt(a, b):
    return lax.dot_general(a, b, (((1,), (1,)), ((), ())), preferred_element_type=F32)


def _dot_tn(a, b):
    return lax.dot_general(a, b, (((0,), (0,)), ((), ())), preferred_element_type=F32)


def _ds(start, size, stride):
    return pl.ds(start, size) if stride == 1 else pl.ds(start, size, stride=stride)


def _norm_fwd(x, g, tag):
    t = x.shape[0]
    tm = 512

    def body(x_ref, g_ref, o_ref):
        xv = x_ref[...]
        r = lax.rsqrt(jnp.mean(xv * xv, axis=-1, keepdims=True) + RMS_EPS)
        o_ref[...] = (xv * r * g_ref[...]).astype(BF16)

    return pl.pallas_call(
        body, name=f"norm_fwd_{tag}", grid=(t // tm,),
        in_specs=[BS((tm, D), lambda i: (i, 0)), BS((1, D), lambda i: (0, 0))],
        out_specs=BS((tm, D), lambda i: (i, 0)),
        out_shape=SDS((t, D), BF16), compiler_params=_cp("parallel"),
    )(x, g.reshape(1, D))


def _norm_bwd(x, g, dh, dres, tag):
    t = x.shape[0]
    tm = 512

    def body(x_ref, g_ref, dh_ref, dr_ref, dx_ref, dg_ref):
        @pl.when(pl.program_id(0) == 0)
        def _():
            dg_ref[...] = jnp.zeros_like(dg_ref)

        xv = x_ref[...]
        r = lax.rsqrt(jnp.mean(xv * xv, axis=-1, keepdims=True) + RMS_EPS)
        xh = xv * r
        dh = dh_ref[...]
        u = dh * g_ref[...]
        dx_ref[...] = dr_ref[...] + r * (u - xh * jnp.mean(xh * u, axis=-1, keepdims=True))
        dg_ref[...] += jnp.sum(dh * xh, axis=0, keepdims=True)

    row = BS((tm, D), lambda i: (i, 0))
    vec = BS((1, D), lambda i: (0, 0))
    return pl.pallas_call(
        body, name=f"norm_bwd_{tag}", grid=(t // tm,),
        in_specs=[row, vec, row, row], out_specs=[row, vec],
        out_shape=[SDS((t, D), F32), SDS((1, D), F32)], compiler_params=_cp("arbitrary"),
    )(x, g.reshape(1, D), dh, dres)


def _loss_head(x, g, tgt):
    t = x.shape[0]
    tm = 512

    def body(x_ref, g_ref, t_ref, loss_ref, dx_ref, dg_ref):
        @pl.when(pl.program_id(0) == 0)
        def _():
            dg_ref[...] = jnp.zeros_like(dg_ref)
            loss_ref[...] = jnp.zeros_like(loss_ref)

        xv = x_ref[...]
        gv = g_ref[...]
        r = lax.rsqrt(jnp.mean(xv * xv, axis=-1, keepdims=True) + RMS_EPS)
        xh = xv * r
        e = xh * gv - t_ref[...]
        loss_ref[...] += 0.5 * jnp.sum(jnp.mean(e * e, axis=-1, keepdims=True), axis=0, keepdims=True)
        dy = e * (1.0 / D)
        u = dy * gv
        dx_ref[...] = r * (u - xh * jnp.mean(xh * u, axis=-1, keepdims=True))
        dg_ref[...] += jnp.sum(dy * xh, axis=0, keepdims=True)

    row = BS((tm, D), lambda i: (i, 0))
    vec = BS((1, D), lambda i: (0, 0))
    return pl.pallas_call(
        body, name="loss_head", grid=(t // tm,),
        in_specs=[row, vec, row], out_specs=[BS((1, 128), lambda i: (0, 0)), row, vec],
        out_shape=[SDS((1, 128), F32), SDS((t, D), F32), SDS((1, D), F32)],
        compiler_params=_cp("arbitrary"),
    )(x, g.reshape(1, D), tgt)


def _mm_nn(a, w, tag, res=None, scale=1.0, tm=512, tn=None):
    c_n, t, k = a.shape
    n = w.shape[2]
    tn = n if tn is None else tn

    def body(*refs):
        a_ref, w_ref = refs[0], refs[1]
        o_ref = refs[-1]
        acc = _dot_nn(a_ref[0].astype(BF16), w_ref[0])
        for c in range(1, c_n):
            acc = acc + _dot_nn(a_ref[c].astype(BF16), w_ref[c])
        if scale != 1.0:
            acc = acc * scale
        if res is not None:
            acc = refs[2][...] + acc
        o_ref[...] = acc

    in_specs = [BS((c_n, tm, k), lambda i, j: (0, i, 0)), BS((c_n, k, tn), lambda i, j: (0, 0, j))]
    args = [a, w]
    if res is not None:
        in_specs.append(BS((tm, tn), lambda i, j: (i, j)))
        args.append(res)
    return pl.pallas_call(
        body, name=f"mm_nn_{tag}", grid=(t // tm, n // tn), in_specs=in_specs,
        out_specs=BS((tm, tn), lambda i, j: (i, j)), out_shape=SDS((t, n), F32),
        compiler_params=_cp("parallel", "parallel"),
    )(*args)


def _mm_nt_rows(a, w, tag, tm, tn, n_total, w_row0):
    t, k = a.shape
    assert w_row0 % tn == 0 and n_total % tn == 0
    j0 = w_row0 // tn

    def body(a_ref, w_ref, o_ref):
        o_ref[...] = _dot_nt(a_ref[...].astype(BF16), w_ref[...])

    return pl.pallas_call(
        body, name=f"mm_nt_{tag}", grid=(t // tm, n_total // tn),
        in_specs=[BS((tm, k), lambda i, j: (i, 0)), BS((tn, k), lambda i, j: (j0 + j, 0))],
        out_specs=BS((tm, tn), lambda i, j: (i, j)), out_shape=SDS((t, n_total), F32),
        compiler_params=_cp("parallel", "parallel"),
    )(a, w)


def _mm_tn(a, b, tag, scale=1.0, tmm=None, tk=512):
    c_n, t, m = a.shape
    n = b.shape[1]
    tmm = m if tmm is None else tmm
    nk = t // tk

    def body(a_ref, b_ref, o_ref, acc_ref):
        kk = pl.program_id(2)

        @pl.when(kk == 0)
        def _():
            acc_ref[...] = jnp.zeros_like(acc_ref)

        acc_ref[...] += _dot_tn(a_ref[...].astype(BF16), b_ref[...].astype(BF16))

        @pl.when(kk == nk - 1)
        def _():
            o_ref[...] = (acc_ref[...] * scale).astype(BF16)

    return pl.pallas_call(
        body, name=f"mm_tn_{tag}", grid=(c_n, m // tmm, nk),
        in_specs=[BS((None, tk, tmm), lambda c, mi, kk: (c, kk, mi)), BS((tk, n), lambda c, mi, kk: (kk, 0))],
        out_specs=BS((None, tmm, n), lambda c, mi, kk: (c, mi, 0)),
        out_shape=SDS((c_n, m, n), BF16), scratch_shapes=[pltpu.VMEM((tmm, n), F32)],
        compiler_params=_cp("parallel", "parallel", "arbitrary"),
    )(a, b)


def _ffn_up(hn, wut, tag):
    t = hn.shape[0]
    tm, tn = 512, 1408

    def body(h_ref, w_ref, gu_ref, act_ref):
        h = h_ref[...]
        g = _dot_nt(h, w_ref[0])
        u = _dot_nt(h, w_ref[1])
        gu_ref[0] = g.astype(BF16)
        gu_ref[1] = u.astype(BF16)
        act_ref[...] = (g * jax.nn.sigmoid(g) * u).astype(BF16)

    return pl.pallas_call(
        body, name=f"ffn_up_{tag}", grid=(t // tm, F // tn),
        in_specs=[BS((tm, D), lambda i, j: (i, 0)), BS((2, tn, D), lambda i, j: (0, j, 0))],
        out_specs=[BS((2, tm, tn), lambda i, j: (0, i, j)), BS((tm, tn), lambda i, j: (i, j))],
        out_shape=[SDS((2, t, F), BF16), SDS((t, F), BF16)],
        compiler_params=_cp("parallel", "parallel"),
    )(hn, wut)


def _ffn_dact(dxo, wd, gu, tag):
    t = dxo.shape[0]
    tm, tn = 512, 1408

    def body(d_ref, w_ref, gu_ref, o_ref):
        dact = _dot_nt(d_ref[...].astype(BF16), w_ref[...]) * 0.5
        g = gu_ref[0].astype(F32)
        u = gu_ref[1].astype(F32)
        sg = jax.nn.sigmoid(g)
        o_ref[0] = (dact * u * (sg * (1.0 + g * (1.0 - sg)))).astype(BF16)
        o_ref[1] = (dact * (g * sg)).astype(BF16)

    return pl.pallas_call(
        body, name=f"ffn_dact_{tag}", grid=(t // tm, F // tn),
        in_specs=[BS((tm, D), lambda i, j: (i, 0)), BS((tn, D), lambda i, j: (j, 0)),
                  BS((2, tm, tn), lambda i, j: (0, i, j))],
        out_specs=BS((2, tm, tn), lambda i, j: (0, i, j)),
        out_shape=SDS((2, t, F), BF16), compiler_params=_cp("parallel", "parallel"),
    )(dxo, wd, gu)


def _rope_tables():
    half = HEAD_DIM // 2
    inv_freq = ROPE_THETA ** (-jnp.arange(half, dtype=F32) / half)
    ang = jnp.arange(S).astype(F32)[:, None] * inv_freq[None, :]
    cos, sin = jnp.cos(ang), jnp.sin(ang)
    return jnp.concatenate([cos, cos, cos, cos], axis=1), jnp.concatenate([-sin, sin, -sin, sin], axis=1)


def _swap_halves(t, first_half):
    return jnp.where(first_half, pltpu.roll(t, 96, 1), pltpu.roll(t, 32, 1))


def _rope_fwd(proj, cos_t, sin_t):
    t = proj.shape[0]
    tm = 512
    width = 2 * QKV_A // 3

    def body(x_ref, c_ref, s_ref, o_ref):
        c = c_ref[...]
        sg = s_ref[...]
        first = (lax.broadcasted_iota(jnp.int32, (tm, 128), 1) % HEAD_DIM) < HEAD_DIM // 2
        for j in range(width // 128):
            v = x_ref[:, 128 * j:128 * (j + 1)]
            o_ref[:, 128 * j:128 * (j + 1)] = v * c + _swap_halves(v, first) * sg

    tab = BS((tm, 128), lambda i: (i % (S // tm), 0))
    return pl.pallas_call(
        body, name="rope_fwd", grid=(t // tm,),
        in_specs=[BS((tm, width), lambda i: (i, 0)), tab, tab],
        out_specs=BS((tm, width), lambda i: (i, 0)), out_shape=SDS((t, width), F32),
        compiler_params=_cp("parallel"),
    )(proj, cos_t, sin_t)


def _rope_bwd(dqs, dks, cos_t, sin_t):
    t = dqs[0].shape[0]
    tm = 512

    def body(*refs):
        c = refs[6][...]
        sg = refs[7][...]
        o_ref = refs[8]
        first = (lax.broadcasted_iota(jnp.int32, (tm, 128), 1) % HEAD_DIM) < HEAD_DIM // 2
        for a in range(6):
            for hp in range(2):
                v = refs[a][:, 128 * hp:128 * (hp + 1)]
                col = 128 * (2 * a + hp)
                o_ref[:, col:col + 128] = (v * c + _swap_halves(v * sg, first)).astype(BF16)

    blk = BS((tm, 256), lambda i: (i, 0))
    tab = BS((tm, 128), lambda i: (i % (S // tm), 0))
    return pl.pallas_call(
        body, name="rope_bwd", grid=(t // tm,), in_specs=[blk] * 6 + [tab, tab],
        out_specs=BS((tm, 1536), lambda i: (i, 0)), out_shape=SDS((t, 1536), BF16),
        compiler_params=_cp("parallel"),
    )(*dqs, *dks, cos_t, sin_t)


def _head_masks():
    lane = lax.broadcasted_iota(jnp.int32, (1, 128), 1)
    m0 = (lane < HEAD_DIM).astype(F32)
    return m0, 1.0 - m0


def _dil_geometry(d):
    sub = S // d
    q_rows = 128
    k_rows = min(256, sub)
    return sub, q_rows, sub // q_rows, k_rows


def _dil_tile(idx, d):
    sub, q_rows, nb, k_rows = _dil_geometry(d)
    r = idx // nb
    n = idx % nb
    k_sub = jnp.clip(q_rows * n - HALF, 0, sub - k_rows)
    if d == 1:
        q_start = pl.multiple_of(q_rows * n, q_rows)
        k_start = pl.multiple_of(k_sub, HALF)
    else:
        q_start = q_rows * n * d + r
        k_start = k_sub * d + r
    ii = lax.broadcasted_iota(jnp.int32, (q_rows, k_rows), 0)
    jj = lax.broadcasted_iota(jnp.int32, (q_rows, k_rows), 1)
    valid = jnp.abs(jj - ii + (k_sub - q_rows * n)) <= HALF
    return q_start, k_start, valid


def _dil_specs(grp):
    qs = BS((S, 128), lambda b, hp: (b, 2 * grp + hp))
    ks = BS((S, 128), lambda b, hp: (b, 6 + 2 * grp + hp))
    vs = BS((S, 128), lambda b, hp: (b, 12 + 2 * grp + hp))
    own = BS((S, 128), lambda b, hp: (b, hp))
    return qs, ks, vs, own


def _dil_fwd(qkr, proj, grp):
    t = qkr.shape[0]
    d = DILATIONS[grp]
    _, q_rows, nb, k_rows = _dil_geometry(d)

    def body(q_ref, k_ref, v_ref, o_ref, l_ref):
        masks = _head_masks()

        def step(idx, carry):
            q_start, k_start, valid = _dil_tile(idx, d)
            q = q_ref[_ds(q_start, q_rows, d), :]
            kb = k_ref[_ds(k_start, k_rows, d), :].astype(BF16)
            v = v_ref[_ds(k_start, k_rows, d), :]
            o2 = jnp.zeros((q_rows, 128), F32)
            l2 = jnp.zeros((q_rows, 128), F32)
            for mh in masks:
                s = jnp.where(valid, _dot_nt((q * mh).astype(BF16), kb) * SCALE, NEG)
                mx = jnp.max(s, axis=1, keepdims=True)
                p = jnp.exp(s - mx)
                den = jnp.sum(p, axis=1, keepdims=True)
                o2 = o2 + _dot_nn(p.astype(BF16), (v * mh).astype(BF16)) / den
                l2 = l2 + (mx + jnp.log(den)) * mh
            o_ref[_ds(q_start, q_rows, d), :] = o2
            l_ref[_ds(q_start, q_rows, d), :] = l2
            return carry

        lax.fori_loop(0, d * nb, step, 0)

    qs, ks, vs, own = _dil_specs(grp)
    return pl.pallas_call(
        body, name=f"dil_fwd_{grp}", grid=(t // S, 2), in_specs=[qs, ks, vs], out_specs=[own, own],
        out_shape=[SDS((t, 256), F32), SDS((t, 256), F32)], compiler_params=_cp("parallel", "parallel"),
    )(qkr, qkr, proj)


def _dil_bwd(qkr, proj, do, dlp, lse, grp):
    t = qkr.shape[0]
    d = DILATIONS[grp]
    _, q_rows, nb, k_rows = _dil_geometry(d)

    def body(q_ref, k_ref, v_ref, do_ref, dl_ref, l_ref, dq_ref, dk_ref, dv_ref):
        masks = _head_masks()
        dk_ref[...] = jnp.zeros_like(dk_ref)
        dv_ref[...] = jnp.zeros_like(dv_ref)

        def step(idx, carry):
            q_start, k_start, valid = _dil_tile(idx, d)
            q_rows_ds = _ds(q_start, q_rows, d)
            k_rows_ds = _ds(k_start, k_rows, d)
            q = q_ref[q_rows_ds, :]
            k = k_ref[k_rows_ds, :]
            kb = k.astype(BF16)
            vb = v_ref[k_rows_ds, :].astype(BF16)
            do2 = do_ref[q_rows_ds, :]
            dl2 = dl_ref[q_rows_ds, :]
            l2 = l_ref[q_rows_ds, :]
            dq2 = jnp.zeros((q_rows, 128), F32)
            dkw = jnp.zeros((k_rows, 128), F32)
            dvw = jnp.zeros((k_rows, 128), F32)
            for h, mh in enumerate(masks):
                col = HEAD_DIM * h
                qh = (q * mh).astype(BF16)
                s = jnp.where(valid, _dot_nt(qh, kb) * SCALE, NEG)
                p = jnp.exp(s - l2[:, col:col + 1])
                doh = (do2 * mh).astype(BF16)
                ds = (p * (_dot_nt(doh, vb) - dl2[:, col:col + 1])).astype(BF16)
                dq2 = dq2 + _dot_nn(ds, (k * mh).astype(BF16))
                dkw = dkw + _dot_tn(ds, qh)
                dvw = dvw + _dot_tn(p.astype(BF16), doh)
            dq_ref[q_rows_ds, :] = dq2 * SCALE
            dk_ref[k_rows_ds, :] += dkw * SCALE
            dv_ref[k_rows_ds, :] += dvw
            return carry

        lax.fori_loop(0, d * nb, step, 0)

    qs, ks, vs, own = _dil_specs(grp)
    return pl.pallas_call(
        body, name=f"dil_bwd_{grp}", grid=(t // S, 2), in_specs=[qs, ks, vs, own, own, own],
        out_specs=[own, own, own], out_shape=[SDS((t, 256), F32)] * 3,
        compiler_params=_cp("parallel", "parallel"),
    )(qkr, qkr, proj, do, dlp, lse)


def _mix_weights(l0, l1, l2):
    mx = jnp.maximum(jnp.maximum(l0, l1), l2)
    e0, e1, e2 = jnp.exp(l0 - mx), jnp.exp(l1 - mx), jnp.exp(l2 - mx)
    den = e0 + e1 + e2
    return e0 / den, e1 / den, e2 / den


def _combine_fwd(outs, lses):
    t = outs[0].shape[0]
    tm = 512

    def body(o0, o1, o2, l0, l1, l2, y_ref):
        w0, w1, w2 = _mix_weights(l0[...], l1[...], l2[...])
        y_ref[...] = w0 * o0[...] + w1 * o1[...] + w2 * o2[...]

    blk = BS((tm, 256), lambda i: (i, 0))
    return pl.pallas_call(
        body, name="combine_fwd", grid=(t // tm,), in_specs=[blk] * 6, out_specs=blk,
        out_shape=SDS((t, 256), F32), compiler_params=_cp("parallel"),
    )(*outs, *lses)


def _head_sum(x):
    a = lax.broadcasted_iota(jnp.int32, (256, 256), 0) // HEAD_DIM
    b = lax.broadcasted_iota(jnp.int32, (256, 256), 1) // HEAD_DIM
    ones = (a == b).astype(BF16)
    hi = x.astype(BF16)
    lo = (x - hi.astype(F32)).astype(BF16)
    return _dot_nn(hi, ones) + _dot_nn(lo, ones)


def _combine_bwd(dya, outs, lses):
    t = dya.shape[0]
    tm = 512

    def body(dy_ref, o0, o1, o2, l0, l1, l2, d0, d1, d2, e0, e1, e2):
        ws = _mix_weights(l0[...], l1[...], l2[...])
        dy = dy_ref[...]
        ya = ws[0] * o0[...] + ws[1] * o1[...] + ws[2] * o2[...]
        hs = _head_sum(dy * ya)
        for w, d_ref, e_ref in zip(ws, (d0, d1, d2), (e0, e1, e2)):
            d_ref[...] = w * dy
            e_ref[...] = w * hs

    blk = BS((tm, 256), lambda i: (i, 0))
    return pl.pallas_call(
        body, name="combine_bwd", grid=(t // tm,), in_specs=[blk] * 7, out_specs=[blk] * 6,
        out_shape=[SDS((t, 256), F32)] * 6, compiler_params=_cp("parallel"),
    )(dya, *outs, *lses)


def _na_bias_table(rel_bias):
    qc = np.arange(GRID_W)[:, None]
    kc = np.arange(GRID_W)[None, :]
    win_lo = np.clip(qc - 8, 0, GRID_W - 16)
    col_valid = (kc >= win_lo) & (kc < win_lo + 16)
    col_idx = np.clip(kc - qc + 15, 0, 30)
    row_idx = np.arange(NA_KR)[:, None] + np.arange(NA_KR)[None, :]
    rows = (row_idx[..., None] == np.arange(2 * NA_KR - 1)).astype(np.float32)
    cols = (col_idx[..., None] == np.arange(31)).astype(np.float32)
    b = jnp.einsum("hrd,ckr,qjd->hcqkj", rel_bias.astype(F32), rows, cols, precision=lax.Precision.HIGHEST)
    b = jnp.where(col_valid[None, None, :, None, :], b, NEG)
    return b.reshape(8, NA_KR, GRID_W, NA_KR * GRID_W)


def _na_row(i):
    lo = jnp.clip(i - NA_KR // 2, 0, NA_ROWS - NA_KR)
    return pl.multiple_of(GRID_W * i, GRID_W), pl.multiple_of(GRID_W * lo, GRID_W), lo - i + NA_KR - 1


def _na_fwd(proj, bias):
    t = proj.shape[0]
    kw = NA_KR * GRID_W

    def body(q_ref, k_ref, v_ref, b_ref, o_ref, l_ref):
        masks = _head_masks()

        def step(i, carry):
            q_start, k_start, cls = _na_row(i)
            q = q_ref[pl.ds(q_start, GRID_W), :]
            kb = k_ref[pl.ds(k_start, kw), :].astype(BF16)
            v = v_ref[pl.ds(k_start, kw), :]
            o2 = jnp.zeros((GRID_W, 128), F32)
            l2 = jnp.zeros((GRID_W, 128), F32)
            for h, mh in enumerate(masks):
                s = _dot_nt((q * mh).astype(BF16), kb) * SCALE + b_ref[h, cls]
                mx = jnp.max(s, axis=1, keepdims=True)
                p = jnp.exp(s - mx)
                den = jnp.sum(p, axis=1, keepdims=True)
                o2 = o2 + _dot_nn((p / den).astype(BF16), (v * mh).astype(BF16))
                l2 = l2 + (mx + jnp.log(den)) * mh
            o_ref[pl.ds(q_start, GRID_W), :] = o2
            l_ref[pl.ds(q_start, GRID_W), :] = l2
            return carry

        lax.fori_loop(0, NA_ROWS, step, 0)

    c0 = QKV_A // 128
    own = BS((S, 128), lambda b, hp: (b, hp))
    return pl.pallas_call(
        body, name="na_fwd", grid=(t // S, 4),
        in_specs=[BS((S, 128), lambda b, hp: (b, c0 + hp)), BS((S, 128), lambda b, hp: (b, c0 + 4 + hp)),
                  BS((S, 128), lambda b, hp: (b, c0 + 8 + hp)),
                  BS((2, NA_KR, GRID_W, kw), lambda b, hp: (hp, 0, 0, 0))],
        out_specs=[own, own], out_shape=[SDS((t, 512), F32), SDS((t, 512), F32)],
        compiler_params=_cp("parallel", "parallel"),
    )(proj, proj, proj, bias)


def _na_bwd(proj, bias, dyb, yb, lse):
    t = proj.shape[0]
    kw = NA_KR * GRID_W

    def body(q_ref, k_ref, v_ref, b_ref, do_ref, o_ref, l_ref, dq_ref, dk_ref, dv_ref, db_ref):
        masks = _head_masks()

        @pl.when(pl.program_id(1) == 0)
        def _():
            db_ref[...] = jnp.zeros_like(db_ref)

        dk_ref[...] = jnp.zeros_like(dk_ref)
        dv_ref[...] = jnp.zeros_like(dv_ref)

        def step(i, carry):
            q_start, k_start, cls = _na_row(i)
            q = q_ref[pl.ds(q_start, GRID_W), :]
            k = k_ref[pl.ds(k_start, kw), :]
            kb = k.astype(BF16)
            vb = v_ref[pl.ds(k_start, kw), :].astype(BF16)
            do2 = do_ref[pl.ds(q_start, GRID_W), :]
            o2 = o_ref[pl.ds(q_start, GRID_W), :]
            l2 = l_ref[pl.ds(q_start, GRID_W), :]
            dq2 = jnp.zeros((GRID_W, 128), F32)
            dkw = jnp.zeros((kw, 128), F32)
            dvw = jnp.zeros((kw, 128), F32)
            for h, mh in enumerate(masks):
                col = HEAD_DIM * h
                delta = jnp.sum(do2 * o2 * mh, axis=1, keepdims=True)
                qh = (q * mh).astype(BF16)
                s = _dot_nt(qh, kb) * SCALE + b_ref[h, cls]
                p = jnp.exp(s - l2[:, col:col + 1])
                doh = (do2 * mh).astype(BF16)
                ds = p * (_dot_nt(doh, vb) - delta)
                db_ref[h, cls] += ds
                dsb = ds.astype(BF16)
                dq2 = dq2 + _dot_nn(dsb, (k * mh).astype(BF16))
                dkw = dkw + _dot_tn(dsb, qh)
                dvw = dvw + _dot_tn(p.astype(BF16), doh)
            dq_ref[pl.ds(q_start, GRID_W), :] = dq2 * SCALE
            dk_ref[pl.ds(k_start, kw), :] += dkw * SCALE
            dv_ref[pl.ds(k_start, kw), :] += dvw
            return carry

        lax.fori_loop(0, NA_ROWS, step, 0)

    c0 = QKV_A // 128
    own = BS((S, 128), lambda hp, b: (b, hp))
    tab = BS((2, NA_KR, GRID_W, kw), lambda hp, b: (hp, 0, 0, 0))
    return pl.pallas_call(
        body, name="na_bwd", grid=(4, t // S),
        in_specs=[BS((S, 128), lambda hp, b: (b, c0 + hp)), BS((S, 128), lambda hp, b: (b, c0 + 4 + hp)),
                  BS((S, 128), lambda hp, b: (b, c0 + 8 + hp)), tab, own, own, own],
        out_specs=[own, own, own, tab],
        out_shape=[SDS((t, 512), F32)] * 3 + [SDS((8, NA_KR, GRID_W, kw), F32)],
        compiler_params=_cp("parallel", "arbitrary"),
    )(proj, proj, proj, bias, dyb, yb, lse)


def _na_dbias(db):
    kw = NA_KR * GRID_W

    def body(x_ref, o_ref, z_ref):
        lane = lax.broadcasted_iota(jnp.int32, (GRID_W, kw), 1)
        sub = lax.broadcasted_iota(jnp.int32, (GRID_W, kw), 0)
        rel = (lane % GRID_W) - sub + 15
        key_row = lax.broadcasted_iota(jnp.int32, (kw, 128), 0) // GRID_W
        dr = lax.broadcasted_iota(jnp.int32, (kw, 128), 1)
        acc = jnp.zeros((32, 128), F32)
        z_ref[...] = jnp.zeros_like(z_ref)
        for cls in range(NA_KR):
            xv = x_ref[cls]
            for dc in range(31):
                z_ref[dc:dc + 1, :] = jnp.sum(jnp.where(rel == dc, xv, 0.0), axis=0, keepdims=True)
            z = z_ref[...]
            ind = (key_row + cls == dr).astype(BF16)
            hi = z.astype(BF16)
            lo = (z - hi.astype(F32)).astype(BF16)
            acc = acc + _dot_nn(hi, ind) + _dot_nn(lo, ind)
        o_ref[...] = acc

    return pl.pallas_call(
        body, name="na_dbias", grid=(8,),
        in_specs=[BS((None, NA_KR, GRID_W, kw), lambda h: (h, 0, 0, 0))],
        out_specs=BS((None, 32, 128), lambda h: (h, 0, 0)), out_shape=SDS((8, 32, 128), F32),
        scratch_shapes=[pltpu.VMEM((32, kw), F32)], compiler_params=_cp("parallel"),
    )(db)


def _merge_fwd(ya, yb, proj, wat, wbt):
    t = ya.shape[0]
    tm, tn = 512, 256
    ca = (QKV_A + QKV_B) // tn
    cb = ca + D // tn

    def body(ya_ref, yb_ref, la_ref, lb_ref, wa_ref, wb_ref, m_ref, za_ref, zb_ref):
        za = _dot_nt(ya_ref[...].astype(BF16), wa_ref[...])
        zb = _dot_nt(yb_ref[...].astype(BF16), wb_ref[...])
        m_ref[...] = (jax.nn.sigmoid(la_ref[...]) * za + jax.nn.sigmoid(lb_ref[...]) * zb).astype(BF16)
        za_ref[...] = za.astype(BF16)
        zb_ref[...] = zb.astype(BF16)

    out = BS((tm, tn), lambda i, j: (i, j))
    return pl.pallas_call(
        body, name="merge_fwd", grid=(t // tm, D // tn),
        in_specs=[BS((tm, 256), lambda i, j: (i, 0)), BS((tm, 512), lambda i, j: (i, 0)),
                  BS((tm, tn), lambda i, j: (i, ca + j)), BS((tm, tn), lambda i, j: (i, cb + j)),
                  BS((tn, 256), lambda i, j: (j, 0)), BS((tn, 512), lambda i, j: (j, 0))],
        out_specs=[out, out, out], out_shape=[SDS((t, D), BF16)] * 3,
        compiler_params=_cp("parallel", "parallel"),
    )(ya, yb, proj, proj, wat, wbt)


def _merge_bwd(dm, za, zb, proj):
    t = dm.shape[0]
    tm, tn = 512, 256
    ca = (QKV_A + QKV_B) // tn
    cb = ca + D // tn

    def body(dm_ref, za_ref, zb_ref, la_ref, lb_ref, dza_ref, dzb_ref, dl_ref):
        dmv = dm_ref[...]
        ga = jax.nn.sigmoid(la_ref[...])
        gb = jax.nn.sigmoid(lb_ref[...])
        dza_ref[...] = (dmv * ga).astype(BF16)
        dzb_ref[...] = (dmv * gb).astype(BF16)
        dl_ref[0] = (dmv * za_ref[...].astype(F32) * ga * (1.0 - ga)).astype(BF16)
        dl_ref[1] = (dmv * zb_ref[...].astype(F32) * gb * (1.0 - gb)).astype(BF16)

    blk = BS((tm, tn), lambda i, j: (i, j))
    return pl.pallas_call(
        body, name="merge_bwd", grid=(t // tm, D // tn),
        in_specs=[blk, blk, blk, BS((tm, tn), lambda i, j: (i, ca + j)), BS((tm, tn), lambda i, j: (i, cb + j))],
        out_specs=[blk, blk, BS((2, tm, tn), lambda i, j: (0, i, j))],
        out_shape=[SDS((t, D), BF16), SDS((t, D), BF16), SDS((2, t, D), BF16)],
        compiler_params=_cp("parallel", "parallel"),
    )(dm, za, zb, proj, proj)


def _sum_slots(recv0, recv1, tag):
    _, r, c = recv0.shape
    tr = r if r * c <= 512 * 1024 else r // 2

    def body(a_ref, b_ref, o_ref):
        for layer, ref in enumerate((a_ref, b_ref)):
            acc = ref[0].astype(F32)
            for s in range(1, N_DEV):
                acc = acc + ref[s].astype(F32)
            o_ref[layer] = acc

    blk = BS((N_DEV, tr, c), lambda i: (0, i, 0))
    return pl.pallas_call(
        body, name=f"sum_slots_{tag}", grid=(r // tr,), in_specs=[blk, blk],
        out_specs=BS((2, tr, c), lambda i: (0, i, 0)), out_shape=SDS((2, r, c), F32),
        compiler_params=_cp("parallel"),
    )(recv0, recv1)


def _adamw(w, g, m, v, tag):
    layers, r, c = w.shape
    tr = 256 if r % 256 == 0 and r > 256 else r

    def body(w_ref, g_ref, m_ref, v_ref, d_ref, mo_ref, vo_ref):
        gv = g_ref[...]
        mn = ADAM_B1 * m_ref[...] + (1.0 - ADAM_B1) * gv
        vn = ADAM_B2 * v_ref[...] + (1.0 - ADAM_B2) * (gv * gv)
        m_hat = mn / (1.0 - ADAM_B1 ** ADAM_STEP)
        v_hat = vn / (1.0 - ADAM_B2 ** ADAM_STEP)
        d_ref[...] = -ADAM_LR * (m_hat / (jnp.sqrt(v_hat) + ADAM_EPS) + ADAM_WD * w_ref[...])
        mo_ref[...] = mn
        vo_ref[...] = vn

    blk = BS((None, tr, c), lambda l, i: (l, i, 0))
    return pl.pallas_call(
        body, name=f"adamw_{tag}", grid=(layers, r // tr), in_specs=[blk] * 4, out_specs=[blk] * 3,
        out_shape=[SDS((layers, r, c), F32)] * 3, compiler_params=_cp("parallel", "parallel"),
    )(w, g, m, v)


def _place():
    return lax.axis_index("x"), lax.axis_index("y"), lax.axis_index("c")


def _flip(coord, bit):
    return 1 - coord if bit else coord


def _allgather(shards, tag):
    n_arr = len(shards)
    hbm = BS(memory_space=pl.ANY)

    def body(*refs):
        ins, outs = refs[:n_arr], refs[n_arr:2 * n_arr]
        send_sems, recv_sems, local_sems = refs[2 * n_arr:]
        x, y, c = _place()
        me, sibling = (x, y, c), (x, y, 1 - c)
        chips = [(1 - x, y), (x, 1 - y), (1 - x, 1 - y)]

        def rows(a, p):
            r = shards[a].shape[0]
            return outs[a].at[pl.ds((4 * p[0] + 2 * p[1] + p[2]) * r, r), :]

        def copy(a, k, block, to, src=None):
            return pltpu.make_async_remote_copy(
                src_ref=rows(a, block) if src is None else src, dst_ref=rows(a, block),
                send_sem=send_sems.at[a, k], recv_sem=recv_sems.at[a, k], device_id=to, device_id_type=MESH)

        mine = [pltpu.make_async_copy(ins[a], rows(a, me), local_sems.at[a]) for a in range(n_arr)]
        for cp in mine:
            cp.start()
        first = []
        for a in range(n_arr):
            first.append(copy(a, 0, me, sibling, src=ins[a]))
            first += [copy(a, 1 + j, me, (*chip, c), src=ins[a]) for j, chip in enumerate(chips)]
        for cp in first:
            cp.start()
        passed = []
        for a in range(n_arr):
            for j, chip in enumerate(chips):
                copy(a, 1 + j, (*chip, c), me).wait_recv()
                passed.append(copy(a, 4 + j, (*chip, c), sibling))
                passed[-1].start()
        for a in range(n_arr):
            copy(a, 0, sibling, me).wait_recv()
            for j, chip in enumerate(chips):
                copy(a, 4 + j, (*chip, 1 - c), me).wait_recv()
        for cp in first + passed:
            cp.wait_send()
        for cp in mine:
            cp.wait()

    return pl.pallas_call(
        body, name=f"allgather_{tag}", in_specs=[hbm] * n_arr, out_specs=[hbm] * n_arr,
        out_shape=[SDS((N_DEV * s.shape[0], s.shape[1]), s.dtype) for s in shards],
        scratch_shapes=[pltpu.SemaphoreType.DMA((n_arr, 7)), pltpu.SemaphoreType.DMA((n_arr, 7)),
                        pltpu.SemaphoreType.DMA((n_arr,))],
        compiler_params=pltpu.CompilerParams(has_side_effects=True),
    )(*shards)


def _peers(x, y, c):
    peers = []
    for mask in range(1, N_DEV):
        p = (_flip(x, mask & 4), _flip(y, mask & 2), _flip(c, mask & 1))
        peers.append((p, 4 * p[0] + 2 * p[1] + p[2]))
    return peers


def _exchange_refs(mode, src, land, me, peer):
    if mode == "gather":
        r = src.shape[0]
        return src, land.at[pl.ds(me * r, r), :], land.at[pl.ds(peer * r, r), :]
    r = land.shape[1]
    return src.at[pl.ds(peer * r, r), :], land.at[me], land.at[peer]


HBM_SPEC = BS(memory_space=pltpu.HBM)
SEM_SPEC = BS(memory_space=pltpu.SEMAPHORE)
DATAFLOW = pltpu.SideEffectType.DATAFLOW_SIDE_EFFECTING


def _own_block_placed(mode, src, me):
    if mode == "gather":
        r, c = src.shape
        return lax.dynamic_update_slice(lax.empty((N_DEV * r, c), src.dtype), src, (me * r, 0))
    r, c = src.shape[0] // N_DEV, src.shape[1]
    own = lax.dynamic_slice(src, (me * r, 0), (r, c))
    return lax.dynamic_update_slice(lax.empty((N_DEV, r, c), src.dtype), own[None], (me, 0, 0))


def _exchange_start(mode, srcs, after, tag):
    n = len(srcs)
    x, y, c = _place()
    lands = [_own_block_placed(mode, s, 4 * x + 2 * y + c) for s in srcs]

    def body(*refs):
        src_refs, land_refs = refs[:n], refs[n:2 * n]
        send_sems, recv_sems = refs[2 * n + 1], refs[2 * n + 2]
        token = refs[-1]
        bx, by, bc = _place()
        me = 4 * bx + 2 * by + bc
        for a in range(n):
            for k, (p, idx) in enumerate(_peers(bx, by, bc)):
                out, there, _ = _exchange_refs(mode, src_refs[a], land_refs[a], me, idx)
                pltpu.make_async_remote_copy(
                    src_ref=out, dst_ref=there, send_sem=send_sems.at[7 * a + k], recv_sem=recv_sems.at[7 * a + k],
                    device_id=p, device_id_type=MESH).start()
        token[...] = jnp.zeros_like(token)

    res = pl.pallas_call(
        body, name=f"{mode}_start_{tag}",
        out_shape=(pltpu.SemaphoreType.DMA((7 * n,)), pltpu.SemaphoreType.DMA((7 * n,)),
                   *[pltpu.HBM(s.shape, s.dtype) for s in srcs], *[pltpu.HBM(l.shape, l.dtype) for l in lands],
                   SDS((8, 128), F32)),
        in_specs=[HBM_SPEC] * (2 * n) + [BS(memory_space=pl.ANY)],
        out_specs=(SEM_SPEC, SEM_SPEC, *[HBM_SPEC] * (2 * n), BS(memory_space=pltpu.VMEM)),
        input_output_aliases={i: 2 + i for i in range(2 * n)},
        compiler_params=pltpu.CompilerParams(has_side_effects=DATAFLOW),
    )(*[pltpu.with_memory_space_constraint(s, pltpu.HBM) for s in srcs],
      *[pltpu.with_memory_space_constraint(l, pltpu.HBM) for l in lands], after)
    return (mode, res[0], res[1], res[2:2 + n], res[2 + n:2 + 2 * n]), res[-1]


def _exchange_wait(handle, after, tag):
    mode, send_sems, recv_sems, srcs, lands = handle
    n = len(srcs)

    def body(*refs):
        src_refs, land_refs = refs[:n], refs[n:2 * n]
        send_ref, recv_ref = refs[2 * n], refs[2 * n + 1]
        bx, by, bc = _place()
        me = 4 * bx + 2 * by + bc
        for a in range(n):
            for k, (p, idx) in enumerate(_peers(bx, by, bc)):
                out, _, here = _exchange_refs(mode, src_refs[a], land_refs[a], me, idx)
                cp = pltpu.make_async_remote_copy(
                    src_ref=out, dst_ref=here, send_sem=send_ref.at[7 * a + k], recv_sem=recv_ref.at[7 * a + k],
                    device_id=p, device_id_type=MESH)
                cp.wait_send()
                cp.wait_recv()

    res = pl.pallas_call(
        body, name=f"{mode}_wait_{tag}",
        out_shape=(*[pltpu.HBM(s.shape, s.dtype) for s in srcs], *[pltpu.HBM(l.shape, l.dtype) for l in lands]),
        in_specs=[HBM_SPEC] * (2 * n) + [SEM_SPEC, SEM_SPEC, BS(memory_space=pl.ANY)],
        out_specs=tuple([HBM_SPEC] * (2 * n)),
        input_output_aliases={i: i for i in range(2 * n)},
        compiler_params=pltpu.CompilerParams(has_side_effects=DATAFLOW),
    )(*srcs, *lands, send_sems, recv_sems, after)
    return list(res[n:])


def _allreduce_small(vec):
    rows = vec.shape[0]

    def body(x_ref, o_ref, buf_ref, send_sems, recv_sems):
        x, y, c = _place()
        me = 4 * x + 2 * y + c
        buf_ref[me] = x_ref[...]
        peers = []
        for mask in range(1, N_DEV):
            p = (_flip(x, mask & 4), _flip(y, mask & 2), _flip(c, mask & 1))
            peers.append((p, 4 * p[0] + 2 * p[1] + p[2]))

        def copy(k, slot):
            return pltpu.make_async_remote_copy(
                src_ref=x_ref, dst_ref=buf_ref.at[slot], send_sem=send_sems.at[k], recv_sem=recv_sems.at[k],
                device_id=peers[k][0], device_id_type=MESH)

        sends = [copy(k, me) for k in range(N_DEV - 1)]
        for cp in sends:
            cp.start()
        for k in range(N_DEV - 1):
            copy(k, peers[k][1]).wait_recv()
        for cp in sends:
            cp.wait_send()
        acc = buf_ref[0]
        for s in range(1, N_DEV):
            acc = acc + buf_ref[s]
        o_ref[...] = acc

    vmem = BS(memory_space=pltpu.VMEM)
    return pl.pallas_call(
        body, name="allreduce_small", in_specs=[vmem], out_specs=vmem, out_shape=SDS((rows, 128), F32),
        scratch_shapes=[pltpu.VMEM((N_DEV, rows, 128), F32), pltpu.SemaphoreType.DMA((7,)),
                        pltpu.SemaphoreType.DMA((7,))],
        compiler_params=pltpu.CompilerParams(has_side_effects=True),
    )(vec)


def _ffn_forward(x, norm_g, wut, wd, tag):
    hn = _norm_fwd(x, norm_g, tag)
    gu, act = _ffn_up(hn, wut, tag)
    out = _mm_nn(act[None], wd[None], f"down_{tag}", res=x, scale=0.5)
    return out, (x, hn, gu, act)


def _ffn_backward(dxo, saved, norm_g, wut, wd, tag):
    x, hn, gu, act = saved
    du = _ffn_dact(dxo, wd, gu, tag)
    d_wd = _mm_tn(act[None], dxo, f"dwd_{tag}", scale=0.5, tmm=1408)[0]
    d_wut = _mm_tn(du, hn, f"dwu_{tag}", tmm=1408)
    dhn = _mm_nn(du, wut, f"dhn_{tag}", tn=512)
    dx, dg = _norm_bwd(x, norm_g, dhn, dxo, tag)
    return dx, dg, d_wut.reshape(2 * F, D), d_wd


def _mixer_forward(x, norm_g, w, bias, tables, tag):
    wint, wat, wbt, wo = w
    hn = _norm_fwd(x, norm_g, tag)
    proj = _mm_nt_rows(hn, wint, f"proj_{tag}", 512, IN_W // 2, IN_W, 0)
    qkr = _rope_fwd(proj, *tables)
    outs, lses = [], []
    for grp in range(3):
        o, l = _dil_fwd(qkr, proj, grp)
        outs.append(o)
        lses.append(l)
    ya = _combine_fwd(outs, lses)
    yb, lse_b = _na_fwd(proj, bias)
    merged, za, zb = _merge_fwd(ya, yb, proj, wat, wbt)
    out = _mm_nn(merged[None], wo[None], f"out_{tag}", res=x)
    return out, (x, hn, proj, qkr, outs, lses, ya, yb, lse_b, merged, za, zb)


def _mixer_backward(dxo, saved, norm_g, w, bias, tables, tag):
    wint, wat, wbt, wo = w
    x, hn, proj, qkr, outs, lses, ya, yb, lse_b, merged, za, zb = saved
    dm = _mm_nt_rows(dxo, wo, f"dmerged_{tag}", 512, D, D, 0)
    d_wo = _mm_tn(merged[None], dxo, f"dwo_{tag}")[0]
    dza, dzb, dlog = _merge_bwd(dm, za, zb, proj)
    dya = _mm_nn(dza[None], wat[None], f"dya_{tag}")
    dyb = _mm_nn(dzb[None], wbt[None], f"dyb_{tag}")
    d_wat = _mm_tn(dza[None], ya, f"dwa_{tag}")[0]
    d_wbt = _mm_tn(dzb[None], yb, f"dwb_{tag}")[0]
    cb = _combine_bwd(dya, outs, lses)
    dqs, dks, dvs = [], [], []
    for grp in range(3):
        dq, dk, dv = _dil_bwd(qkr, proj, cb[grp], cb[3 + grp], lses[grp], grp)
        dqs.append(dq)
        dks.append(dk)
        dvs.append(dv)
    dqk = _rope_bwd(dqs, dks, *tables)
    dqb, dkb, dvb, dbias_tab = _na_bwd(proj, bias, dyb, yb, lse_b)
    dbias = _na_dbias(dbias_tab)
    dproj = jnp.concatenate(
        [dqk] + [t.astype(BF16) for t in (*dvs, dqb, dkb, dvb)] + [dlog[0], dlog[1]], axis=1)
    d_wint = _mm_tn(dproj[None], hn, f"dwin_{tag}", tmm=2944)[0]
    dhn = _mm_nn(dproj[None], wint[None], f"dhnm_{tag}", tm=256, tn=512)
    dx, dg = _norm_bwd(x, norm_g, dhn, dxo, f"mix_{tag}")
    dbias = dbias[:, :31, :15].transpose(0, 2, 1)
    return dx, dg, dbias, d_wint, d_wat, d_wbt, d_wo


def _pack_small(norms, biases, final, loss=None):
    parts = []
    for layer in range(DEPTH):
        parts += [norms[0][layer], norms[1][layer], norms[2][layer],
                  jnp.pad(biases[layer].reshape(-1), (0, BIAS_PAD - 8 * 15 * 31))]
    parts.append(final)
    flat = jnp.concatenate([p.reshape(-1).astype(F32) for p in parts])
    if loss is not None:
        flat = jnp.concatenate([flat, loss.reshape(-1)])
    return jnp.pad(flat, (0, SMALL_ROWS * 128 - flat.shape[0])).reshape(SMALL_ROWS, 128)


def _unpack_small(packed):
    flat = packed.reshape(-1)
    norms, biases = ([], [], []), []
    pos = 0
    for _ in range(DEPTH):
        for k in range(3):
            norms[k].append(flat[pos:pos + D])
            pos += D
        biases.append(flat[pos:pos + 8 * 15 * 31].reshape(8, 15, 31))
        pos += BIAS_PAD
    final = flat[pos:pos + D]
    pos += D
    return [jnp.stack(n) for n in norms], jnp.stack(biases), final, flat[pos]


def kernel(x, ffn1_norm, ffn1_w_up, ffn1_w_down, mix_norm, w_in, na_rel_bias, w_branch_a, w_branch_b, w_out, ffn2_norm, ffn2_w_up, ffn2_w_down, final_norm, loss_target, m_ffn1_norm, m_ffn1_w_up, m_ffn1_w_down, m_mix_norm, m_w_in, m_na_rel_bias, m_w_branch_a, m_w_branch_b, m_w_out, m_ffn2_norm, m_ffn2_w_up, m_ffn2_w_down, m_final_norm, v_ffn1_norm, v_ffn1_w_up, v_ffn1_w_down, v_mix_norm, v_w_in, v_na_rel_bias, v_w_branch_a, v_w_branch_b, v_w_out, v_ffn2_norm, v_ffn2_w_up, v_ffn2_w_down, v_final_norm):
    t = x.shape[0] * x.shape[1]
    xs = x.reshape(t, D)
    tgt = loss_target.reshape(t, D)
    tables = _rope_tables()

    col_sharded = dict(up1=ffn1_w_up, win=w_in, wa=w_branch_a, wb=w_branch_b, up2=ffn2_w_up)
    row_sharded = dict(down1=ffn1_w_down, wo=w_out, down2=ffn2_w_down)
    sublayers = (("up1", "down1"), ("win", "wa", "wb", "wo"), ("up2", "down2"))
    shard = [{} for _ in range(DEPTH)]
    for layer in range(DEPTH):
        for name, arr in col_sharded.items():
            shard[layer][name] = arr[layer].T.astype(BF16)
        for name, arr in row_sharded.items():
            shard[layer][name] = arr[layer].astype(BF16)

    weights = [{} for _ in range(DEPTH)]
    first = _allgather([shard[0][n] for n in sublayers[0]], "first")
    weights[0].update(zip(sublayers[0], first))
    pending = {}
    after = first[0]
    for layer in range(DEPTH):
        for k, names in enumerate(sublayers):
            if (layer, k) != (0, 0):
                pending[layer, k], after = _exchange_start(
                    "gather", [shard[layer][n] for n in names], after, f"w{layer}{k}")
    zero = after[0, 0]

    def arrived(layer, k, behind):
        if (layer, k) in pending:
            got = _exchange_wait(pending.pop((layer, k)), behind, f"w{layer}{k}")
            weights[layer].update(zip(sublayers[k], got))
        return weights[layer]

    saved = []
    h = xs
    for layer in range(DEPTH):
        bias = _na_bias_table(na_rel_bias[layer])
        w = arrived(layer, 0, h)
        h, s1 = _ffn_forward(h, ffn1_norm[layer] + zero, w["up1"].reshape(2, F, D), w["down1"], f"f1l{layer}")
        w = arrived(layer, 1, h)
        h, s2 = _mixer_forward(h, mix_norm[layer], (w["win"], w["wa"], w["wb"], w["wo"]), bias, tables, f"l{layer}")
        w = arrived(layer, 2, h)
        h, s3 = _ffn_forward(h, ffn2_norm[layer], w["up2"].reshape(2, F, D), w["down2"], f"f2l{layer}")
        saved.append((s1, s2, s3, bias))
    loss_part, dh, d_final = _loss_head(h, final_norm, tgt)

    d_norms = ([None] * DEPTH, [None] * DEPTH, [None] * DEPTH)
    d_bias = [None] * DEPTH
    sent = {}
    after = dh
    for layer in reversed(range(DEPTH)):
        w = weights[layer]
        s1, s2, s3, bias = saved[layer]
        dh, d_norms[2][layer], d_up2, d_down2 = _ffn_backward(
            dh, s3, ffn2_norm[layer], w["up2"].reshape(2, F, D), w["down2"], f"f2l{layer}")
        sent[layer, 2], after = _exchange_start("scatter", [d_up2, d_down2], after, f"g{layer}2")
        dh, d_norms[1][layer], d_bias[layer], d_win, d_wa, d_wb, d_wo = _mixer_backward(
            dh, s2, mix_norm[layer], (w["win"], w["wa"], w["wb"], w["wo"]), bias, tables, f"l{layer}")
        sent[layer, 1], after = _exchange_start("scatter", [d_win, d_wa, d_wb, d_wo], after, f"g{layer}1")
        dh, d_norms[0][layer], d_up1, d_down1 = _ffn_backward(
            dh, s1, ffn1_norm[layer], w["up1"].reshape(2, F, D), w["down1"], f"f1l{layer}")
        sent[layer, 0], after = _exchange_start("scatter", [d_up1, d_down1], after, f"g{layer}0")
    grad_x = dh.reshape(x.shape)

    small = _allreduce_small(_pack_small(d_norms, d_bias, d_final, loss_part[0, :1]))
    g_norms, g_bias, g_final, loss = _unpack_small(small)

    originals = dict(up1=(ffn1_w_up, m_ffn1_w_up, v_ffn1_w_up), down1=(ffn1_w_down, m_ffn1_w_down, v_ffn1_w_down),
                     win=(w_in, m_w_in, v_w_in), wa=(w_branch_a, m_w_branch_a, v_w_branch_a),
                     wb=(w_branch_b, m_w_branch_b, v_w_branch_b), wo=(w_out, m_w_out, v_w_out),
                     up2=(ffn2_w_up, m_ffn2_w_up, v_ffn2_w_up), down2=(ffn2_w_down, m_ffn2_w_down, v_ffn2_w_down))
    big = {}
    behind = small
    for k in (2, 1, 0):
        recv = [_exchange_wait(sent[layer, k], behind, f"g{layer}{k}") for layer in (1, 0)]
        for i, name in enumerate(sublayers[k]):
            g = _sum_slots(recv[1][i], recv[0][i], name)
            if name in col_sharded:
                g = jnp.swapaxes(g, 1, 2)
            wv, mv, vv = originals[name]
            big[name] = (g, *_adamw(wv, g, mv, vv, name))
            behind = big[name][1]

    w_small = _pack_small((ffn1_norm, mix_norm, ffn2_norm), na_rel_bias, final_norm)
    m_small = _pack_small((m_ffn1_norm, m_mix_norm, m_ffn2_norm), m_na_rel_bias, m_final_norm)
    v_small = _pack_small((v_ffn1_norm, v_mix_norm, v_ffn2_norm), v_na_rel_bias, v_final_norm)
    upd = _adamw(w_small[None], small[None], m_small[None], v_small[None], "small")
    small_out = [(g_norms, g_bias, g_final)] + [_unpack_small(u[0])[:3] for u in upd]

    outputs = [loss, grad_x]
    for kind in range(4):
        norms, bias_k, final_k = small_out[kind]
        outputs += [norms[0], big["up1"][kind], big["down1"][kind], norms[1], big["win"][kind], bias_k,
                    big["wa"][kind], big["wb"][kind], big["wo"][kind], norms[2], big["up2"][kind],
                    big["down2"][kind], final_k]
    return tuple(outputs)
```

```python
import numpy as np

import jax
import jax.numpy as jnp
from jax import lax
from jax.experimental import pallas as pl
from jax.experimental.pallas import tpu as pltpu

F32 = jnp.float32
BF16 = jnp.bfloat16
SDS = jax.ShapeDtypeStruct
BS = pl.BlockSpec
MESH = pl.DeviceIdType.MESH

D = 1024
S = 2048
F = 2816
DEPTH = 2
HEAD_DIM = 64
DILATIONS = (1, 4, 16)
HALF = 64
QKV_A = 2304
QKV_B = 1536
IN_W = 5888
N_DEV = 8
NA_ROWS = 32
GRID_W = 64
NA_KR = 8
ROPE_THETA = 10000.0
RMS_EPS = 1e-6
NEG = -1e30
SCALE = HEAD_DIM ** -0.5
ADAM_LR, ADAM_B1, ADAM_B2, ADAM_EPS, ADAM_WD, ADAM_STEP = 0.001, 0.9, 0.999, 1e-08, 0.01, 10
VMEM_LIMIT_V7X = 52 * 1024 * 1024
SMALL_ROWS = 120
BIAS_PAD = 3840


def _cp(*sem):
    return pltpu.CompilerParams(dimension_semantics=sem, vmem_limit_bytes=VMEM_LIMIT_V7X)


def _dot_nn(a, b):
    return jnp.dot(a, b, preferred_element_type=F32)


def _dot_nt(a, b):
    return lax.dot_general(a, b, (((1,), (1,)), ((), ())), preferred_element_type=F32)


def _dot_tn(a, b):
    return lax.dot_general(a, b, (((0,), (0,)), ((), ())), preferred_element_type=F32)


def _ds(start, size, stride):
    return pl.ds(start, size) if stride == 1 else pl.ds(start, size, stride=stride)


def _norm_fwd(x, g, tag):
    t = x.shape[0]
    tm = 512

    def body(x_ref, g_ref, o_ref):
        xv = x_ref[...]
        r = lax.rsqrt(jnp.mean(xv * xv, axis=-1, keepdims=True) + RMS_EPS)
        o_ref[...] = (xv * r * g_ref[...]).astype(BF16)

    return pl.pallas_call(
        body, name=f"norm_fwd_{tag}", grid=(t // tm,),
        in_specs=[BS((tm, D), lambda i: (i, 0)), BS((1, D), lambda i: (0, 0))],
        out_specs=BS((tm, D), lambda i: (i, 0)),
        out_shape=SDS((t, D), BF16), compiler_params=_cp("parallel"),
    )(x, g.reshape(1, D))


def _norm_bwd(x, g, dh, dres, tag):
    t = x.shape[0]
    tm = 512

    def body(x_ref, g_ref, dh_ref, dr_ref, dx_ref, dg_ref):
        @pl.when(pl.program_id(0) == 0)
        def _():
            dg_ref[...] = jnp.zeros_like(dg_ref)

        xv = x_ref[...]
        r = lax.rsqrt(jnp.mean(xv * xv, axis=-1, keepdims=True) + RMS_EPS)
        xh = xv * r
        dh = dh_ref[...]
        u = dh * g_ref[...]
        dx_ref[...] = dr_ref[...] + r * (u - xh * jnp.mean(xh * u, axis=-1, keepdims=True))
        dg_ref[...] += jnp.sum(dh * xh, axis=0, keepdims=True)

    row = BS((tm, D), lambda i: (i, 0))
    vec = BS((1, D), lambda i: (0, 0))
    return pl.pallas_call(
        body, name=f"norm_bwd_{tag}", grid=(t // tm,),
        in_specs=[row, vec, row, row], out_specs=[row, vec],
        out_shape=[SDS((t, D), F32), SDS((1, D), F32)], compiler_params=_cp("arbitrary"),
    )(x, g.reshape(1, D), dh, dres)


def _loss_head(x, g, tgt):
    t = x.shape[0]
    tm = 512

    def body(x_ref, g_ref, t_ref, loss_ref, dx_ref, dg_ref):
        @pl.when(pl.program_id(0) == 0)
        def _():
            dg_ref[...] = jnp.zeros_like(dg_ref)
            loss_ref[...] = jnp.zeros_like(loss_ref)

        xv = x_ref[...]
        gv = g_ref[...]
        r = lax.rsqrt(jnp.mean(xv * xv, axis=-1, keepdims=True) + RMS_EPS)
        xh = xv * r
        e = xh * gv - t_ref[...]
        loss_ref[...] += 0.5 * jnp.sum(jnp.mean(e * e, axis=-1, keepdims=True), axis=0, keepdims=True)
        dy = e * (1.0 / D)
        u = dy * gv
        dx_ref[...] = r * (u - xh * jnp.mean(xh * u, axis=-1, keepdims=True))
        dg_ref[...] += jnp.sum(dy * xh, axis=0, keepdims=True)

    row = BS((tm, D), lambda i: (i, 0))
    vec = BS((1, D), lambda i: (0, 0))
    return pl.pallas_call(
        body, name="loss_head", grid=(t // tm,),
        in_specs=[row, vec, row], out_specs=[BS((1, 128), lambda i: (0, 0)), row, vec],
        out_shape=[SDS((1, 128), F32), SDS((t, D), F32), SDS((1, D), F32)],
        compiler_params=_cp("arbitrary"),
    )(x, g.reshape(1, D), tgt)


def _mm_nn(a, w, tag, res=None, scale=1.0, tm=512, tn=None):
    c_n, t, k = a.shape
    n = w.shape[2]
    tn = n if tn is None else tn

    def body(*refs):
        a_ref, w_ref = refs[0], refs[1]
        o_ref = refs[-1]
        acc = _dot_nn(a_ref[0].astype(BF16), w_ref[0])
        for c in range(1, c_n):
            acc = acc + _dot_nn(a_ref[c].astype(BF16), w_ref[c])
        if scale != 1.0:
            acc = acc * scale
        if res is not None:
            acc = refs[2][...] + acc
        o_ref[...] = acc

    in_specs = [BS((c_n, tm, k), lambda i, j: (0, i, 0)), BS((c_n, k, tn), lambda i, j: (0, 0, j))]
    args = [a, w]
    if res is not None:
        in_specs.append(BS((tm, tn), lambda i, j: (i, j)))
        args.append(res)
    return pl.pallas_call(
        body, name=f"mm_nn_{tag}", grid=(t // tm, n // tn), in_specs=in_specs,
        out_specs=BS((tm, tn), lambda i, j: (i, j)), out_shape=SDS((t, n), F32),
        compiler_params=_cp("parallel", "parallel"),
    )(*args)


def _mm_nt_rows(a, w, tag, tm, tn, n_total, w_row0):
    t, k = a.shape
    assert w_row0 % tn == 0 and n_total % tn == 0
    j0 = w_row0 // tn

    def body(a_ref, w_ref, o_ref):
        o_ref[...] = _dot_nt(a_ref[...].astype(BF16), w_ref[...])

    return pl.pallas_call(
        body, name=f"mm_nt_{tag}", grid=(t // tm, n_total // tn),
        in_specs=[BS((tm, k), lambda i, j: (i, 0)), BS((tn, k), lambda i, j: (j0 + j, 0))],
        out_specs=BS((tm, tn), lambda i, j: (i, j)), out_shape=SDS((t, n_total), F32),
        compiler_params=_cp("parallel", "parallel"),
    )(a, w)


def _mm_tn(a, b, tag, scale=1.0, tmm=None, tk=512):
    c_n, t, m = a.shape
    n = b.shape[1]
    tmm = m if tmm is None else tmm
    nk = t // tk

    def body(a_ref, b_ref, o_ref, acc_ref):
        kk = pl.program_id(2)

        @pl.when(kk == 0)
        def _():
            acc_ref[...] = jnp.zeros_like(acc_ref)

        acc_ref[...] += _dot_tn(a_ref[...].astype(BF16), b_ref[...].astype(BF16))

        @pl.when(kk == nk - 1)
        def _():
            o_ref[...] = (acc_ref[...] * scale).astype(BF16)

    return pl.pallas_call(
        body, name=f"mm_tn_{tag}", grid=(c_n, m // tmm, nk),
        in_specs=[BS((None, tk, tmm), lambda c, mi, kk: (c, kk, mi)), BS((tk, n), lambda c, mi, kk: (kk, 0))],
        out_specs=BS((None, tmm, n), lambda c, mi, kk: (c, mi, 0)),
        out_shape=SDS((c_n, m, n), BF16), scratch_shapes=[pltpu.VMEM((tmm, n), F32)],
        compiler_params=_cp("parallel", "parallel", "arbitrary"),
    )(a, b)


def _ffn_up(hn, wut, tag):
    t = hn.shape[0]
    tm, tn = 512, 1408

    def body(h_ref, w_ref, gu_ref, act_ref):
        h = h_ref[...]
        g = _dot_nt(h, w_ref[0])
        u = _dot_nt(h, w_ref[1])
        gu_ref[0] = g.astype(BF16)
        gu_ref[1] = u.astype(BF16)
        act_ref[...] = (g * jax.nn.sigmoid(g) * u).astype(BF16)

    return pl.pallas_call(
        body, name=f"ffn_up_{tag}", grid=(t // tm, F // tn),
        in_specs=[BS((tm, D), lambda i, j: (i, 0)), BS((2, tn, D), lambda i, j: (0, j, 0))],
        out_specs=[BS((2, tm, tn), lambda i, j: (0, i, j)), BS((tm, tn), lambda i, j: (i, j))],
        out_shape=[SDS((2, t, F), BF16), SDS((t, F), BF16)],
        compiler_params=_cp("parallel", "parallel"),
    )(hn, wut)


def _ffn_dact(dxo, wd, gu, tag):
    t = dxo.shape[0]
    tm, tn = 512, 1408

    def body(d_ref, w_ref, gu_ref, o_ref):
        dact = _dot_nt(d_ref[...].astype(BF16), w_ref[...]) * 0.5
        g = gu_ref[0].astype(F32)
        u = gu_ref[1].astype(F32)
        sg = jax.nn.sigmoid(g)
        o_ref[0] = (dact * u * (sg * (1.0 + g * (1.0 - sg)))).astype(BF16)
        o_ref[1] = (dact * (g * sg)).astype(BF16)

    return pl.pallas_call(
        body, name=f"ffn_dact_{tag}", grid=(t // tm, F // tn),
        in_specs=[BS((tm, D), lambda i, j: (i, 0)), BS((tn, D), lambda i, j: (j, 0)),
                  BS((2, tm, tn), lambda i, j: (0, i, j))],
        out_specs=BS((2, tm, tn), lambda i, j: (0, i, j)),
        out_shape=SDS((2, t, F), BF16), compiler_params=_cp("parallel", "parallel"),
    )(dxo, wd, gu)


def _rope_tables():
    half = HEAD_DIM // 2
    inv_freq = ROPE_THETA ** (-jnp.arange(half, dtype=F32) / half)
    ang = jnp.arange(S).astype(F32)[:, None] * inv_freq[None, :]
    cos, sin = jnp.cos(ang), jnp.sin(ang)
    return jnp.concatenate([cos, cos, cos, cos], axis=1), jnp.concatenate([-sin, sin, -sin, sin], axis=1)


def _swap_halves(t, first_half):
    return jnp.where(first_half, pltpu.roll(t, 96, 1), pltpu.roll(t, 32, 1))


def _rope_fwd(proj, cos_t, sin_t):
    t = proj.shape[0]
    tm = 512
    width = 2 * QKV_A // 3

    def body(x_ref, c_ref, s_ref, o_ref):
        c = c_ref[...]
        sg = s_ref[...]
        first = (lax.broadcasted_iota(jnp.int32, (tm, 128), 1) % HEAD_DIM) < HEAD_DIM // 2
        for j in range(width // 128):
            v = x_ref[:, 128 * j:128 * (j + 1)]
            o_ref[:, 128 * j:128 * (j + 1)] = v * c + _swap_halves(v, first) * sg

    tab = BS((tm, 128), lambda i: (i % (S // tm), 0))
    return pl.pallas_call(
        body, name="rope_fwd", grid=(t // tm,),
        in_specs=[BS((tm, width), lambda i: (i, 0)), tab, tab],
        out_specs=BS((tm, width), lambda i: (i, 0)), out_shape=SDS((t, width), F32),
        compiler_params=_cp("parallel"),
    )(proj, cos_t, sin_t)


def _rope_bwd(dqs, dks, cos_t, sin_t):
    t = dqs[0].shape[0]
    tm = 512

    def body(*refs):
        c = refs[6][...]
        sg = refs[7][...]
        o_ref = refs[8]
        first = (lax.broadcasted_iota(jnp.int32, (tm, 128), 1) % HEAD_DIM) < HEAD_DIM // 2
        for a in range(6):
            for hp in range(2):
                v = refs[a][:, 128 * hp:128 * (hp + 1)]
                col = 128 * (2 * a + hp)
                o_ref[:, col:col + 128] = (v * c + _swap_halves(v * sg, first)).astype(BF16)

    blk = BS((tm, 256), lambda i: (i, 0))
    tab = BS((tm, 128), lambda i: (i % (S // tm), 0))
    return pl.pallas_call(
        body, name="rope_bwd", grid=(t // tm,), in_specs=[blk] * 6 + [tab, tab],
        out_specs=BS((tm, 1536), lambda i: (i, 0)), out_shape=SDS((t, 1536), BF16),
        compiler_params=_cp("parallel"),
    )(*dqs, *dks, cos_t, sin_t)


def _head_masks():
    lane = lax.broadcasted_iota(jnp.int32, (1, 128), 1)
    m0 = (lane < HEAD_DIM).astype(F32)
    return m0, 1.0 - m0


def _dil_geometry(d):
    sub = S // d
    q_rows = 128
    k_rows = min(256, sub)
    return sub, q_rows, sub // q_rows, k_rows


def _dil_tile(idx, d):
    sub, q_rows, nb, k_rows = _dil_geometry(d)
    r = idx // nb
    n = idx % nb
    k_sub = jnp.clip(q_rows * n - HALF, 0, sub - k_rows)
    if d == 1:
        q_start = pl.multiple_of(q_rows * n, q_rows)
        k_start = pl.multiple_of(k_sub, HALF)
    else:
        q_start = q_rows * n * d + r
        k_start = k_sub * d + r
    ii = lax.broadcasted_iota(jnp.int32, (q_rows, k_rows), 0)
    jj = lax.broadcasted_iota(jnp.int32, (q_rows, k_rows), 1)
    valid = jnp.abs(jj - ii + (k_sub - q_rows * n)) <= HALF
    return q_start, k_start, valid


def _dil_specs(grp):
    qs = BS((S, 128), lambda b, hp: (b, 2 * grp + hp))
    ks = BS((S, 128), lambda b, hp: (b, 6 + 2 * grp + hp))
    vs = BS((S, 128), lambda b, hp: (b, 12 + 2 * grp + hp))
    own = BS((S, 128), lambda b, hp: (b, hp))
    return qs, ks, vs, own


def _dil_fwd(qkr, proj, grp):
    t = qkr.shape[0]
    d = DILATIONS[grp]
    _, q_rows, nb, k_rows = _dil_geometry(d)

    def body(q_ref, k_ref, v_ref, o_ref, l_ref):
        masks = _head_masks()

        def step(idx, carry):
            q_start, k_start, valid = _dil_tile(idx, d)
            q = q_ref[_ds(q_start, q_rows, d), :]
            kb = k_ref[_ds(k_start, k_rows, d), :].astype(BF16)
            v = v_ref[_ds(k_start, k_rows, d), :]
            o2 = jnp.zeros((q_rows, 128), F32)
            l2 = jnp.zeros((q_rows, 128), F32)
            for mh in masks:
                s = jnp.where(valid, _dot_nt((q * mh).astype(BF16), kb) * SCALE, NEG)
                mx = jnp.max(s, axis=1, keepdims=True)
                p = jnp.exp(s - mx)
                den = jnp.sum(p, axis=1, keepdims=True)
                o2 = o2 + _dot_nn(p.astype(BF16), (v * mh).astype(BF16)) / den
                l2 = l2 + (mx + jnp.log(den)) * mh
            o_ref[_ds(q_start, q_rows, d), :] = o2
            l_ref[_ds(q_start, q_rows, d), :] = l2
            return carry

        lax.fori_loop(0, d * nb, step, 0)

    qs, ks, vs, own = _dil_specs(grp)
    return pl.pallas_call(
        body, name=f"dil_fwd_{grp}", grid=(t // S, 2), in_specs=[qs, ks, vs], out_specs=[own, own],
        out_shape=[SDS((t, 256), F32), SDS((t, 256), F32)], compiler_params=_cp("parallel", "parallel"),
    )(qkr, qkr, proj)


def _dil_bwd(qkr, proj, do, dlp, lse, grp):
    t = qkr.shape[0]
    d = DILATIONS[grp]
    _, q_rows, nb, k_rows = _dil_geometry(d)

    def body(q_ref, k_ref, v_ref, do_ref, dl_ref, l_ref, dq_ref, dk_ref, dv_ref):
        masks = _head_masks()
        dk_ref[...] = jnp.zeros_like(dk_ref)
        dv_ref[...] = jnp.zeros_like(dv_ref)

        def step(idx, carry):
            q_start, k_start, valid = _dil_tile(idx, d)
            q_rows_ds = _ds(q_start, q_rows, d)
            k_rows_ds = _ds(k_start, k_rows, d)
            q = q_ref[q_rows_ds, :]
            k = k_ref[k_rows_ds, :]
            kb = k.astype(BF16)
            vb = v_ref[k_rows_ds, :].astype(BF16)
            do2 = do_ref[q_rows_ds, :]
            dl2 = dl_ref[q_rows_ds, :]
            l2 = l_ref[q_rows_ds, :]
            dq2 = jnp.zeros((q_rows, 128), F32)
            dkw = jnp.zeros((k_rows, 128), F32)
            dvw = jnp.zeros((k_rows, 128), F32)
            for h, mh in enumerate(masks):
                col = HEAD_DIM * h
                qh = (q * mh).astype(BF16)
                s = jnp.where(valid, _dot_nt(qh, kb) * SCALE, NEG)
                p = jnp.exp(s - l2[:, col:col + 1])
                doh = (do2 * mh).astype(BF16)
                ds = (p * (_dot_nt(doh, vb) - dl2[:, col:col + 1])).astype(BF16)
                dq2 = dq2 + _dot_nn(ds, (k * mh).astype(BF16))
                dkw = dkw + _dot_tn(ds, qh)
                dvw = dvw + _dot_tn(p.astype(BF16), doh)
            dq_ref[q_rows_ds, :] = dq2 * SCALE
            dk_ref[k_rows_ds, :] += dkw * SCALE
            dv_ref[k_rows_ds, :] += dvw
            return carry

        lax.fori_loop(0, d * nb, step, 0)

    qs, ks, vs, own = _dil_specs(grp)
    return pl.pallas_call(
        body, name=f"dil_bwd_{grp}", grid=(t // S, 2), in_specs=[qs, ks, vs, own, own, own],
        out_specs=[own, own, own], out_shape=[SDS((t, 256), F32)] * 3,
        compiler_params=_cp("parallel", "parallel"),
    )(qkr, qkr, proj, do, dlp, lse)


def _mix_weights(l0, l1, l2):
    mx = jnp.maximum(jnp.maximum(l0, l1), l2)
    e0, e1, e2 = jnp.exp(l0 - mx), jnp.exp(l1 - mx), jnp.exp(l2 - mx)
    den = e0 + e1 + e2
    return e0 / den, e1 / den, e2 / den


def _combine_fwd(outs, lses):
    t = outs[0].shape[0]
    tm = 512

    def body(o0, o1, o2, l0, l1, l2, y_ref):
        w0, w1, w2 = _mix_weights(l0[...], l1[...], l2[...])
        y_ref[...] = w0 * o0[...] + w1 * o1[...] + w2 * o2[...]

    blk = BS((tm, 256), lambda i: (i, 0))
    return pl.pallas_call(
        body, name="combine_fwd", grid=(t // tm,), in_specs=[blk] * 6, out_specs=blk,
        out_shape=SDS((t, 256), F32), compiler_params=_cp("parallel"),
    )(*outs, *lses)


def _head_sum(x):
    a = lax.broadcasted_iota(jnp.int32, (256, 256), 0) // HEAD_DIM
    b = lax.broadcasted_iota(jnp.int32, (256, 256), 1) // HEAD_DIM
    ones = (a == b).astype(BF16)
    hi = x.astype(BF16)
    lo = (x - hi.astype(F32)).astype(BF16)
    return _dot_nn(hi, ones) + _dot_nn(lo, ones)


def _combine_bwd(dya, outs, lses):
    t = dya.shape[0]
    tm = 512

    def body(dy_ref, o0, o1, o2, l0, l1, l2, d0, d1, d2, e0, e1, e2):
        ws = _mix_weights(l0[...], l1[...], l2[...])
        dy = dy_ref[...]
        ya = ws[0] * o0[...] + ws[1] * o1[...] + ws[2] * o2[...]
        hs = _head_sum(dy * ya)
        for w, d_ref, e_ref in zip(ws, (d0, d1, d2), (e0, e1, e2)):
            d_ref[...] = w * dy
            e_ref[...] = w * hs

    blk = BS((tm, 256), lambda i: (i, 0))
    return pl.pallas_call(
        body, name="combine_bwd", grid=(t // tm,), in_specs=[blk] * 7, out_specs=[blk] * 6,
        out_shape=[SDS((t, 256), F32)] * 6, compiler_params=_cp("parallel"),
    )(dya, *outs, *lses)


def _na_bias_table(rel_bias):
    qc = np.arange(GRID_W)[:, None]
    kc = np.arange(GRID_W)[None, :]
    win_lo = np.clip(qc - 8, 0, GRID_W - 16)
    col_valid = (kc >= win_lo) & (kc < win_lo + 16)
    col_idx = np.clip(kc - qc + 15, 0, 30)
    row_idx = np.arange(NA_KR)[:, None] + np.arange(NA_KR)[None, :]
    rows = (row_idx[..., None] == np.arange(2 * NA_KR - 1)).astype(np.float32)
    cols = (col_idx[..., None] == np.arange(31)).astype(np.float32)
    b = jnp.einsum("hrd,ckr,qjd->hcqkj", rel_bias.astype(F32), rows, cols, precision=lax.Precision.HIGHEST)
    b = jnp.where(col_valid[None, None, :, None, :], b, NEG)
    return b.reshape(8, NA_KR, GRID_W, NA_KR * GRID_W)


def _na_row(i):
    lo = jnp.clip(i - NA_KR // 2, 0, NA_ROWS - NA_KR)
    return pl.multiple_of(GRID_W * i, GRID_W), pl.multiple_of(GRID_W * lo, GRID_W), lo - i + NA_KR - 1


def _na_fwd(proj, bias):
    t = proj.shape[0]
    kw = NA_KR * GRID_W

    def body(q_ref, k_ref, v_ref, b_ref, o_ref, l_ref):
        masks = _head_masks()

        def step(i, carry):
            q_start, k_start, cls = _na_row(i)
            q = q_ref[pl.ds(q_start, GRID_W), :]
            kb = k_ref[pl.ds(k_start, kw), :].astype(BF16)
            v = v_ref[pl.ds(k_start, kw), :]
            o2 = jnp.zeros((GRID_W, 128), F32)
            l2 = jnp.zeros((GRID_W, 128), F32)
            for h, mh in enumerate(masks):
                s = _dot_nt((q * mh).astype(BF16), kb) * SCALE + b_ref[h, cls]
                mx = jnp.max(s, axis=1, keepdims=True)
                p = jnp.exp(s - mx)
                den = jnp.sum(p, axis=1, keepdims=True)
                o2 = o2 + _dot_nn((p / den).astype(BF16), (v * mh).astype(BF16))
                l2 = l2 + (mx + jnp.log(den)) * mh
            o_ref[pl.ds(q_start, GRID_W), :] = o2
            l_ref[pl.ds(q_start, GRID_W), :] = l2
            return carry

        lax.fori_loop(0, NA_ROWS, step, 0)

    c0 = QKV_A // 128
    own = BS((S, 128), lambda b, hp: (b, hp))
    return pl.pallas_call(
        body, name="na_fwd", grid=(t // S, 4),
        in_specs=[BS((S, 128), lambda b, hp: (b, c0 + hp)), BS((S, 128), lambda b, hp: (b, c0 + 4 + hp)),
                  BS((S, 128), lambda b, hp: (b, c0 + 8 + hp)),
                  BS((2, NA_KR, GRID_W, kw), lambda b, hp: (hp, 0, 0, 0))],
        out_specs=[own, own], out_shape=[SDS((t, 512), F32), SDS((t, 512), F32)],
        compiler_params=_cp("parallel", "parallel"),
    )(proj, proj, proj, bias)


def _na_bwd(proj, bias, dyb, yb, lse):
    t = proj.shape[0]
    kw = NA_KR * GRID_W

    def body(q_ref, k_ref, v_ref, b_ref, do_ref, o_ref, l_ref, dq_ref, dk_ref, dv_ref, db_ref):
        masks = _head_masks()

        @pl.when(pl.program_id(1) == 0)
        def _():
            db_ref[...] = jnp.zeros_like(db_ref)

        dk_ref[...] = jnp.zeros_like(dk_ref)
        dv_ref[...] = jnp.zeros_like(dv_ref)

        def step(i, carry):
            q_start, k_start, cls = _na_row(i)
            q = q_ref[pl.ds(q_start, GRID_W), :]
            k = k_ref[pl.ds(k_start, kw), :]
            kb = k.astype(BF16)
            vb = v_ref[pl.ds(k_start, kw), :].astype(BF16)
            do2 = do_ref[pl.ds(q_start, GRID_W), :]
            o2 = o_ref[pl.ds(q_start, GRID_W), :]
            l2 = l_ref[pl.ds(q_start, GRID_W), :]
            dq2 = jnp.zeros((GRID_W, 128), F32)
            dkw = jnp.zeros((kw, 128), F32)
            dvw = jnp.zeros((kw, 128), F32)
            for h, mh in enumerate(masks):
                col = HEAD_DIM * h
                delta = jnp.sum(do2 * o2 * mh, axis=1, keepdims=True)
                qh = (q * mh).astype(BF16)
                s = _dot_nt(qh, kb) * SCALE + b_ref[h, cls]
                p = jnp.exp(s - l2[:, col:col + 1])
                doh = (do2 * mh).astype(BF16)
                ds = p * (_dot_nt(doh, vb) - delta)
                db_ref[h, cls] += ds
                dsb = ds.astype(BF16)
                dq2 = dq2 + _dot_nn(dsb, (k * mh).astype(BF16))
                dkw = dkw + _dot_tn(dsb, qh)
                dvw = dvw + _dot_tn(p.astype(BF16), doh)
            dq_ref[pl.ds(q_start, GRID_W), :] = dq2 * SCALE
            dk_ref[pl.ds(k_start, kw), :] += dkw * SCALE
            dv_ref[pl.ds(k_start, kw), :] += dvw
            return carry

        lax.fori_loop(0, NA_ROWS, step, 0)

    c0 = QKV_A // 128
    own = BS((S, 128), lambda hp, b: (b, hp))
    tab = BS((2, NA_KR, GRID_W, kw), lambda hp, b: (hp, 0, 0, 0))
    return pl.pallas_call(
        body, name="na_bwd", grid=(4, t // S),
        in_specs=[BS((S, 128), lambda hp, b: (b, c0 + hp)), BS((S, 128), lambda hp, b: (b, c0 + 4 + hp)),
                  BS((S, 128), lambda hp, b: (b, c0 + 8 + hp)), tab, own, own, own],
        out_specs=[own, own, own, tab],
        out_shape=[SDS((t, 512), F32)] * 3 + [SDS((8, NA_KR, GRID_W, kw), F32)],
        compiler_params=_cp("parallel", "arbitrary"),
    )(proj, proj, proj, bias, dyb, yb, lse)


def _na_dbias(db):
    kw = NA_KR * GRID_W

    def body(x_ref, o_ref, z_ref):
        lane = lax.broadcasted_iota(jnp.int32, (GRID_W, kw), 1)
        sub = lax.broadcasted_iota(jnp.int32, (GRID_W, kw), 0)
        rel = (lane % GRID_W) - sub + 15
        key_row = lax.broadcasted_iota(jnp.int32, (kw, 128), 0) // GRID_W
        dr = lax.broadcasted_iota(jnp.int32, (kw, 128), 1)
        acc = jnp.zeros((32, 128), F32)
        z_ref[...] = jnp.zeros_like(z_ref)
        for cls in range(NA_KR):
            xv = x_ref[cls]
            for dc in range(31):
                z_ref[dc:dc + 1, :] = jnp.sum(jnp.where(rel == dc, xv, 0.0), axis=0, keepdims=True)
            z = z_ref[...]
            ind = (key_row + cls == dr).astype(BF16)
            hi = z.astype(BF16)
            lo = (z - hi.astype(F32)).astype(BF16)
            acc = acc + _dot_nn(hi, ind) + _dot_nn(lo, ind)
        o_ref[...] = acc

    return pl.pallas_call(
        body, name="na_dbias", grid=(8,),
        in_specs=[BS((None, NA_KR, GRID_W, kw), lambda h: (h, 0, 0, 0))],
        out_specs=BS((None, 32, 128), lambda h: (h, 0, 0)), out_shape=SDS((8, 32, 128), F32),
        scratch_shapes=[pltpu.VMEM((32, kw), F32)], compiler_params=_cp("parallel"),
    )(db)


def _merge_fwd(ya, yb, proj, wat, wbt):
    t = ya.shape[0]
    tm, tn = 512, 256
    ca = (QKV_A + QKV_B) // tn
    cb = ca + D // tn

    def body(ya_ref, yb_ref, la_ref, lb_ref, wa_ref, wb_ref, m_ref, za_ref, zb_ref):
        za = _dot_nt(ya_ref[...].astype(BF16), wa_ref[...])
        zb = _dot_nt(yb_ref[...].astype(BF16), wb_ref[...])
        m_ref[...] = (jax.nn.sigmoid(la_ref[...]) * za + jax.nn.sigmoid(lb_ref[...]) * zb).astype(BF16)
        za_ref[...] = za.astype(BF16)
        zb_ref[...] = zb.astype(BF16)

    out = BS((tm, tn), lambda i, j: (i, j))
    return pl.pallas_call(
        body, name="merge_fwd", grid=(t // tm, D // tn),
        in_specs=[BS((tm, 256), lambda i, j: (i, 0)), BS((tm, 512), lambda i, j: (i, 0)),
                  BS((tm, tn), lambda i, j: (i, ca + j)), BS((tm, tn), lambda i, j: (i, cb + j)),
                  BS((tn, 256), lambda i, j: (j, 0)), BS((tn, 512), lambda i, j: (j, 0))],
        out_specs=[out, out, out], out_shape=[SDS((t, D), BF16)] * 3,
        compiler_params=_cp("parallel", "parallel"),
    )(ya, yb, proj, proj, wat, wbt)


def _merge_bwd(dm, za, zb, proj):
    t = dm.shape[0]
    tm, tn = 512, 256
    ca = (QKV_A + QKV_B) // tn
    cb = ca + D // tn

    def body(dm_ref, za_ref, zb_ref, la_ref, lb_ref, dza_ref, dzb_ref, dl_ref):
        dmv = dm_ref[...]
        ga = jax.nn.sigmoid(la_ref[...])
        gb = jax.nn.sigmoid(lb_ref[...])
        dza_ref[...] = (dmv * ga).astype(BF16)
        dzb_ref[...] = (dmv * gb).astype(BF16)
        dl_ref[0] = (dmv * za_ref[...].astype(F32) * ga * (1.0 - ga)).astype(BF16)
        dl_ref[1] = (dmv * zb_ref[...].astype(F32) * gb * (1.0 - gb)).astype(BF16)

    blk = BS((tm, tn), lambda i, j: (i, j))
    return pl.pallas_call(
        body, name="merge_bwd", grid=(t // tm, D // tn),
        in_specs=[blk, blk, blk, BS((tm, tn), lambda i, j: (i, ca + j)), BS((tm, tn), lambda i, j: (i, cb + j))],
        out_specs=[blk, blk, BS((2, tm, tn), lambda i, j: (0, i, j))],
        out_shape=[SDS((t, D), BF16), SDS((t, D), BF16), SDS((2, t, D), BF16)],
        compiler_params=_cp("parallel", "parallel"),
    )(dm, za, zb, proj, proj)


def _sum_slots(recv0, recv1, tag):
    _, r, c = recv0.shape
    tr = r if r * c <= 512 * 1024 else r // 2

    def body(a_ref, b_ref, o_ref):
        for layer, ref in enumerate((a_ref, b_ref)):
            acc = ref[0].astype(F32)
            for s in range(1, N_DEV):
                acc = acc + ref[s].astype(F32)
            o_ref[layer] = acc

    blk = BS((N_DEV, tr, c), lambda i: (0, i, 0))
    return pl.pallas_call(
        body, name=f"sum_slots_{tag}", grid=(r // tr,), in_specs=[blk, blk],
        out_specs=BS((2, tr, c), lambda i: (0, i, 0)), out_shape=SDS((2, r, c), F32),
        compiler_params=_cp("parallel"),
    )(recv0, recv1)


def _adamw(w, g, m, v, tag):
    layers, r, c = w.shape
    tr = 256 if r % 256 == 0 and r > 256 else r

    def body(w_ref, g_ref, m_ref, v_ref, d_ref, mo_ref, vo_ref):
        gv = g_ref[...]
        mn = ADAM_B1 * m_ref[...] + (1.0 - ADAM_B1) * gv
        vn = ADAM_B2 * v_ref[...] + (1.0 - ADAM_B2) * (gv * gv)
        m_hat = mn / (1.0 - ADAM_B1 ** ADAM_STEP)
        v_hat = vn / (1.0 - ADAM_B2 ** ADAM_STEP)
        d_ref[...] = -ADAM_LR * (m_hat / (jnp.sqrt(v_hat) + ADAM_EPS) + ADAM_WD * w_ref[...])
        mo_ref[...] = mn
        vo_ref[...] = vn

    blk = BS((None, tr, c), lambda l, i: (l, i, 0))
    return pl.pallas_call(
        body, name=f"adamw_{tag}", grid=(layers, r // tr), in_specs=[blk] * 4, out_specs=[blk] * 3,
        out_shape=[SDS((layers, r, c), F32)] * 3, compiler_params=_cp("parallel", "parallel"),
    )(w, g, m, v)


def _place():
    return lax.axis_index("x"), lax.axis_index("y"), lax.axis_index("c")


def _flip(coord, bit):
    return 1 - coord if bit else coord


def _allgather(shards, tag):
    n_arr = len(shards)
    hbm = BS(memory_space=pl.ANY)

    def body(*refs):
        ins, outs = refs[:n_arr], refs[n_arr:2 * n_arr]
        send_sems, recv_sems, local_sems = refs[2 * n_arr:]
        x, y, c = _place()
        me, sibling = (x, y, c), (x, y, 1 - c)
        chips = [(1 - x, y), (x, 1 - y), (1 - x, 1 - y)]

        def rows(a, p):
            r = shards[a].shape[0]
            return outs[a].at[pl.ds((4 * p[0] + 2 * p[1] + p[2]) * r, r), :]

        def copy(a, k, block, to, src=None):
            return pltpu.make_async_remote_copy(
                src_ref=rows(a, block) if src is None else src, dst_ref=rows(a, block),
                send_sem=send_sems.at[a, k], recv_sem=recv_sems.at[a, k], device_id=to, device_id_type=MESH)

        mine = [pltpu.make_async_copy(ins[a], rows(a, me), local_sems.at[a]) for a in range(n_arr)]
        for cp in mine:
            cp.start()
        first = []
        for a in range(n_arr):
            first.append(copy(a, 0, me, sibling, src=ins[a]))
            first += [copy(a, 1 + j, me, (*chip, c), src=ins[a]) for j, chip in enumerate(chips)]
        for cp in first:
            cp.start()
        passed = []
        for a in range(n_arr):
            for j, chip in enumerate(chips):
                copy(a, 1 + j, (*chip, c), me).wait_recv()
                passed.append(copy(a, 4 + j, (*chip, c), sibling))
                passed[-1].start()
        for a in range(n_arr):
            copy(a, 0, sibling, me).wait_recv()
            for j, chip in enumerate(chips):
                copy(a, 4 + j, (*chip, 1 - c), me).wait_recv()
        for cp in first + passed:
            cp.wait_send()
        for cp in mine:
            cp.wait()

    return pl.pallas_call(
        body, name=f"allgather_{tag}", in_specs=[hbm] * n_arr, out_specs=[hbm] * n_arr,
        out_shape=[SDS((N_DEV * s.shape[0], s.shape[1]), s.dtype) for s in shards],
        scratch_shapes=[pltpu.SemaphoreType.DMA((n_arr, 7)), pltpu.SemaphoreType.DMA((n_arr, 7)),
                        pltpu.SemaphoreType.DMA((n_arr,))],
        compiler_params=pltpu.CompilerParams(has_side_effects=True),
    )(*shards)


def _peers(x, y, c):
    peers = []
    for mask in range(1, N_DEV):
        p = (_flip(x, mask & 4), _flip(y, mask & 2), _flip(c, mask & 1))
        peers.append((p, 4 * p[0] + 2 * p[1] + p[2]))
    return peers


def _exchange_refs(mode, src, land, me, peer):
    if mode == "gather":
        r = src.shape[0]
        return src, land.at[pl.ds(me * r, r), :], land.at[pl.ds(peer * r, r), :]
    r = land.shape[1]
    return src.at[pl.ds(peer * r, r), :], land.at[me], land.at[peer]


HBM_SPEC = BS(memory_space=pltpu.HBM)
SEM_SPEC = BS(memory_space=pltpu.SEMAPHORE)
DATAFLOW = pltpu.SideEffectType.DATAFLOW_SIDE_EFFECTING


def _own_block_placed(mode, src, me):
    if mode == "gather":
        r, c = src.shape
        return lax.dynamic_update_slice(lax.empty((N_DEV * r, c), src.dtype), src, (me * r, 0))
    r, c = src.shape[0] // N_DEV, src.shape[1]
    own = lax.dynamic_slice(src, (me * r, 0), (r, c))
    return lax.dynamic_update_slice(lax.empty((N_DEV, r, c), src.dtype), own[None], (me, 0, 0))


def _exchange_start(mode, srcs, after, tag):
    n = len(srcs)
    x, y, c = _place()
    lands = [_own_block_placed(mode, s, 4 * x + 2 * y + c) for s in srcs]

    def body(*refs):
        src_refs, land_refs = refs[:n], refs[n:2 * n]
        send_sems, recv_sems = refs[2 * n + 1], refs[2 * n + 2]
        token = refs[-1]
        bx, by, bc = _place()
        me = 4 * bx + 2 * by + bc
        for a in range(n):
            for k, (p, idx) in enumerate(_peers(bx, by, bc)):
                out, there, _ = _exchange_refs(mode, src_refs[a], land_refs[a], me, idx)
                pltpu.make_async_remote_copy(
                    src_ref=out, dst_ref=there, send_sem=send_sems.at[7 * a + k], recv_sem=recv_sems.at[7 * a + k],
                    device_id=p, device_id_type=MESH).start()
        token[...] = jnp.zeros_like(token)

    res = pl.pallas_call(
        body, name=f"{mode}_start_{tag}",
        out_shape=(pltpu.SemaphoreType.DMA((7 * n,)), pltpu.SemaphoreType.DMA((7 * n,)),
                   *[pltpu.HBM(s.shape, s.dtype) for s in srcs], *[pltpu.HBM(l.shape, l.dtype) for l in lands],
                   SDS((8, 128), F32)),
        in_specs=[HBM_SPEC] * (2 * n) + [BS(memory_space=pl.ANY)],
        out_specs=(SEM_SPEC, SEM_SPEC, *[HBM_SPEC] * (2 * n), BS(memory_space=pltpu.VMEM)),
        input_output_aliases={i: 2 + i for i in range(2 * n)},
        compiler_params=pltpu.CompilerParams(has_side_effects=DATAFLOW),
    )(*[pltpu.with_memory_space_constraint(s, pltpu.HBM) for s in srcs],
      *[pltpu.with_memory_space_constraint(l, pltpu.HBM) for l in lands], after)
    return (mode, res[0], res[1], res[2:2 + n], res[2 + n:2 + 2 * n]), res[-1]


def _exchange_wait(handle, after, tag):
    mode, send_sems, recv_sems, srcs, lands = handle
    n = len(srcs)

    def body(*refs):
        src_refs, land_refs = refs[:n], refs[n:2 * n]
        send_ref, recv_ref = refs[2 * n], refs[2 * n + 1]
        bx, by, bc = _place()
        me = 4 * bx + 2 * by + bc
        for a in range(n):
            for k, (p, idx) in enumerate(_peers(bx, by, bc)):
                out, _, here = _exchange_refs(mode, src_refs[a], land_refs[a], me, idx)
                cp = pltpu.make_async_remote_copy(
                    src_ref=out, dst_ref=here, send_sem=send_ref.at[7 * a + k], recv_sem=recv_ref.at[7 * a + k],
                    device_id=p, device_id_type=MESH)
                cp.wait_send()
                cp.wait_recv()

    res = pl.pallas_call(
        body, name=f"{mode}_wait_{tag}",
        out_shape=(*[pltpu.HBM(s.shape, s.dtype) for s in srcs], *[pltpu.HBM(l.shape, l.dtype) for l in lands]),
        in_specs=[HBM_SPEC] * (2 * n) + [SEM_SPEC, SEM_SPEC, BS(memory_space=pl.ANY)],
        out_specs=tuple([HBM_SPEC] * (2 * n)),
        input_output_aliases={i: i for i in range(2 * n)},
        compiler_params=pltpu.CompilerParams(has_side_effects=DATAFLOW),
    )(*srcs, *lands, send_sems, recv_sems, after)
    return list(res[n:])


def _allreduce_small(vec):
    rows = vec.shape[0]

    def body(x_ref, o_ref, buf_ref, send_sems, recv_sems):
        x, y, c = _place()
        me = 4 * x + 2 * y + c
        buf_ref[me] = x_ref[...]
        peers = []
        for mask in range(1, N_DEV):
            p = (_flip(x, mask & 4), _flip(y, mask & 2), _flip(c, mask & 1))
            peers.append((p, 4 * p[0] + 2 * p[1] + p[2]))

        def copy(k, slot):
            return pltpu.make_async_remote_copy(
                src_ref=x_ref, dst_ref=buf_ref.at[slot], send_sem=send_sems.at[k], recv_sem=recv_sems.at[k],
                device_id=peers[k][0], device_id_type=MESH)

        sends = [copy(k, me) for k in range(N_DEV - 1)]
        for cp in sends:
            cp.start()
        for k in range(N_DEV - 1):
            copy(k, peers[k][1]).wait_recv()
        for cp in sends:
            cp.wait_send()
        acc = buf_ref[0]
        for s in range(1, N_DEV):
            acc = acc + buf_ref[s]
        o_ref[...] = acc

    vmem = BS(memory_space=pltpu.VMEM)
    return pl.pallas_call(
        body, name="allreduce_small", in_specs=[vmem], out_specs=vmem, out_shape=SDS((rows, 128), F32),
        scratch_shapes=[pltpu.VMEM((N_DEV, rows, 128), F32), pltpu.SemaphoreType.DMA((7,)),
                        pltpu.SemaphoreType.DMA((7,))],
        compiler_params=pltpu.CompilerParams(has_side_effects=True),
    )(vec)


def _ffn_forward(x, norm_g, wut, wd, tag):
    hn = _norm_fwd(x, norm_g, tag)
    gu, act = _ffn_up(hn, wut, tag)
    out = _mm_nn(act[None], wd[None], f"down_{tag}", res=x, scale=0.5)
    return out, (x, hn, gu, act)


def _ffn_backward(dxo, saved, norm_g, wut, wd, tag, send):
    x, hn, gu, act = saved
    du = _ffn_dact(dxo, wd, gu, tag)
    d_wd = _mm_tn(act[None], dxo, f"dwd_{tag}", scale=0.5, tmm=1408)[0]
    d_wut = _mm_tn(du, hn, f"dwu_{tag}", tmm=1408)
    zero = send([d_wut.reshape(2 * F, D), d_wd])
    dhn = _mm_nn(du, wut, f"dhn_{tag}", tn=512)
    return _norm_bwd(x, norm_g + zero, dhn, dxo, tag)


def _mixer_forward(x, norm_g, w, bias, tables, tag):
    wint, wat, wbt, wo = w
    hn = _norm_fwd(x, norm_g, tag)
    proj = _mm_nt_rows(hn, wint, f"proj_{tag}", 512, IN_W // 2, IN_W, 0)
    qkr = _rope_fwd(proj, *tables)
    outs, lses = [], []
    for grp in range(3):
        o, l = _dil_fwd(qkr, proj, grp)
        outs.append(o)
        lses.append(l)
    ya = _combine_fwd(outs, lses)
    yb, lse_b = _na_fwd(proj, bias)
    merged, za, zb = _merge_fwd(ya, yb, proj, wat, wbt)
    out = _mm_nn(merged[None], wo[None], f"out_{tag}", res=x)
    return out, (x, hn, proj, qkr, outs, lses, ya, yb, lse_b, merged, za, zb)


def _mixer_backward(dxo, saved, norm_g, w, bias, tables, tag, send):
    wint, wat, wbt, wo = w
    x, hn, proj, qkr, outs, lses, ya, yb, lse_b, merged, za, zb = saved
    dm = _mm_nt_rows(dxo, wo, f"dmerged_{tag}", 512, D, D, 0)
    d_wo = _mm_tn(merged[None], dxo, f"dwo_{tag}")[0]
    dza, dzb, dlog = _merge_bwd(dm, za, zb, proj)
    dya = _mm_nn(dza[None], wat[None], f"dya_{tag}")
    dyb = _mm_nn(dzb[None], wbt[None], f"dyb_{tag}")
    d_wat = _mm_tn(dza[None], ya, f"dwa_{tag}")[0]
    d_wbt = _mm_tn(dzb[None], yb, f"dwb_{tag}")[0]
    cb = _combine_bwd(dya, outs, lses)
    dqs, dks, dvs = [], [], []
    for grp in range(3):
        dq, dk, dv = _dil_bwd(qkr, proj, cb[grp], cb[3 + grp], lses[grp], grp)
        dqs.append(dq)
        dks.append(dk)
        dvs.append(dv)
    dqk = _rope_bwd(dqs, dks, *tables)
    dqb, dkb, dvb, dbias_tab = _na_bwd(proj, bias, dyb, yb, lse_b)
    dbias = _na_dbias(dbias_tab)
    dproj = jnp.concatenate(
        [dqk] + [t.astype(BF16) for t in (*dvs, dqb, dkb, dvb)] + [dlog[0], dlog[1]], axis=1)
    d_wint = _mm_tn(dproj[None], hn, f"dwin_{tag}", tmm=2944)[0]
    zero = send([d_wint, d_wat, d_wbt, d_wo])
    dhn = _mm_nn(dproj[None], wint[None], f"dhnm_{tag}", tm=256, tn=512)
    dx, dg = _norm_bwd(x, norm_g + zero, dhn, dxo, f"mix_{tag}")
    dbias = dbias[:, :31, :15].transpose(0, 2, 1)
    return dx, dg, dbias


def _pack_small(norms, biases, final, loss=None):
    parts = []
    for layer in range(DEPTH):
        parts += [norms[0][layer], norms[1][layer], norms[2][layer],
                  jnp.pad(biases[layer].reshape(-1), (0, BIAS_PAD - 8 * 15 * 31))]
    parts.append(final)
    flat = jnp.concatenate([p.reshape(-1).astype(F32) for p in parts])
    if loss is not None:
        flat = jnp.concatenate([flat, loss.reshape(-1)])
    return jnp.pad(flat, (0, SMALL_ROWS * 128 - flat.shape[0])).reshape(SMALL_ROWS, 128)


def _unpack_small(packed):
    flat = packed.reshape(-1)
    norms, biases = ([], [], []), []
    pos = 0
    for _ in range(DEPTH):
        for k in range(3):
            norms[k].append(flat[pos:pos + D])
            pos += D
        biases.append(flat[pos:pos + 8 * 15 * 31].reshape(8, 15, 31))
        pos += BIAS_PAD
    final = flat[pos:pos + D]
    pos += D
    return [jnp.stack(n) for n in norms], jnp.stack(biases), final, flat[pos]


def kernel(x, ffn1_norm, ffn1_w_up, ffn1_w_down, mix_norm, w_in, na_rel_bias, w_branch_a, w_branch_b, w_out, ffn2_norm, ffn2_w_up, ffn2_w_down, final_norm, loss_target, m_ffn1_norm, m_ffn1_w_up, m_ffn1_w_down, m_mix_norm, m_w_in, m_na_rel_bias, m_w_branch_a, m_w_branch_b, m_w_out, m_ffn2_norm, m_ffn2_w_up, m_ffn2_w_down, m_final_norm, v_ffn1_norm, v_ffn1_w_up, v_ffn1_w_down, v_mix_norm, v_w_in, v_na_rel_bias, v_w_branch_a, v_w_branch_b, v_w_out, v_ffn2_norm, v_ffn2_w_up, v_ffn2_w_down, v_final_norm):
    t = x.shape[0] * x.shape[1]
    xs = x.reshape(t, D)
    tgt = loss_target.reshape(t, D)
    tables = _rope_tables()

    col_sharded = dict(up1=ffn1_w_up, win=w_in, wa=w_branch_a, wb=w_branch_b, up2=ffn2_w_up)
    row_sharded = dict(down1=ffn1_w_down, wo=w_out, down2=ffn2_w_down)
    sublayers = (("up1", "down1"), ("win", "wa", "wb", "wo"), ("up2", "down2"))
    shard = [{} for _ in range(DEPTH)]
    for layer in range(DEPTH):
        for name, arr in col_sharded.items():
            shard[layer][name] = arr[layer].T.astype(BF16)
        for name, arr in row_sharded.items():
            shard[layer][name] = arr[layer].astype(BF16)

    weights = [{} for _ in range(DEPTH)]
    first = _allgather([shard[0][n] for n in sublayers[0]], "first")
    weights[0].update(zip(sublayers[0], first))
    pending = {}
    after = first[0]
    for layer in range(DEPTH):
        for k, names in enumerate(sublayers):
            if (layer, k) != (0, 0):
                pending[layer, k], after = _exchange_start(
                    "gather", [shard[layer][n] for n in names], after, f"w{layer}{k}")
    zero = after[0, 0]

    def arrived(layer, k, behind):
        if (layer, k) in pending:
            got = _exchange_wait(pending.pop((layer, k)), behind, f"w{layer}{k}")
            weights[layer].update(zip(sublayers[k], got))
        return weights[layer]

    saved = []
    h = xs
    for layer in range(DEPTH):
        bias = _na_bias_table(na_rel_bias[layer])
        w = arrived(layer, 0, h)
        h, s1 = _ffn_forward(h, ffn1_norm[layer] + zero, w["up1"].reshape(2, F, D), w["down1"], f"f1l{layer}")
        w = arrived(layer, 1, h)
        h, s2 = _mixer_forward(h, mix_norm[layer], (w["win"], w["wa"], w["wb"], w["wo"]), bias, tables, f"l{layer}")
        w = arrived(layer, 2, h)
        h, s3 = _ffn_forward(h, ffn2_norm[layer], w["up2"].reshape(2, F, D), w["down2"], f"f2l{layer}")
        saved.append((s1, s2, s3, bias))
    loss_part, dh, d_final = _loss_head(h, final_norm, tgt)

    d_norms = ([None] * DEPTH, [None] * DEPTH, [None] * DEPTH)
    d_bias = [None] * DEPTH
    sent = {}

    def sender(layer, k):
        def send(grads):
            sent[layer, k], token = _exchange_start("scatter", grads, grads[0], f"g{layer}{k}")
            return token[0, 0]
        return send

    for layer in reversed(range(DEPTH)):
        w = weights[layer]
        s1, s2, s3, bias = saved[layer]
        dh, d_norms[2][layer] = _ffn_backward(
            dh, s3, ffn2_norm[layer], w["up2"].reshape(2, F, D), w["down2"], f"f2l{layer}", sender(layer, 2))
        dh, d_norms[1][layer], d_bias[layer] = _mixer_backward(
            dh, s2, mix_norm[layer], (w["win"], w["wa"], w["wb"], w["wo"]), bias, tables, f"l{layer}", sender(layer, 1))
        dh, d_norms[0][layer] = _ffn_backward(
            dh, s1, ffn1_norm[layer], w["up1"].reshape(2, F, D), w["down1"], f"f1l{layer}", sender(layer, 0))
    grad_x = dh.reshape(x.shape)

    small = _allreduce_small(_pack_small(d_norms, d_bias, d_final, loss_part[0, :1]))
    g_norms, g_bias, g_final, loss = _unpack_small(small)

    originals = dict(up1=(ffn1_w_up, m_ffn1_w_up, v_ffn1_w_up), down1=(ffn1_w_down, m_ffn1_w_down, v_ffn1_w_down),
                     win=(w_in, m_w_in, v_w_in), wa=(w_branch_a, m_w_branch_a, v_w_branch_a),
                     wb=(w_branch_b, m_w_branch_b, v_w_branch_b), wo=(w_out, m_w_out, v_w_out),
                     up2=(ffn2_w_up, m_ffn2_w_up, v_ffn2_w_up), down2=(ffn2_w_down, m_ffn2_w_down, v_ffn2_w_down))
    big = {}
    behind = small
    for k in (2, 1, 0):
        recv = [_exchange_wait(sent[layer, k], behind, f"g{layer}{k}") for layer in (1, 0)]
        for i, name in enumerate(sublayers[k]):
            g = _sum_slots(recv[1][i], recv[0][i], name)
            if name in col_sharded:
                g = jnp.swapaxes(g, 1, 2)
            wv, mv, vv = originals[name]
            big[name] = (g, *_adamw(wv, g, mv, vv, name))
            behind = big[name][1]

    w_small = _pack_small((ffn1_norm, mix_norm, ffn2_norm), na_rel_bias, final_norm)
    m_small = _pack_small((m_ffn1_norm, m_mix_norm, m_ffn2_norm), m_na_rel_bias, m_final_norm)
    v_small = _pack_small((v_ffn1_norm, v_mix_norm, v_ffn2_norm), v_na_rel_bias, v_final_norm)
    upd = _adamw(w_small[None], small[None], m_small[None], v_small[None], "small")
    small_out = [(g_norms, g_bias, g_final)] + [_unpack_small(u[0])[:3] for u in upd]

    outputs = [loss, grad_x]
    for kind in range(4):
        norms, bias_k, final_k = small_out[kind]
        outputs += [norms[0], big["up1"][kind], big["down1"][kind], norms[1], big["win"][kind], bias_k,
                    big["wa"][kind], big["wb"][kind], big["wo"][kind], norms[2], big["up2"][kind],
                    big["down2"][kind], final_k]
    return tuple(outputs)
```

```python
import numpy as np

import jax
import jax.numpy as jnp
from jax import lax
from jax.experimental import pallas as pl
from jax.experimental.pallas import tpu as pltpu

F32 = jnp.float32
BF16 = jnp.bfloat16
SDS = jax.ShapeDtypeStruct
BS = pl.BlockSpec
MESH = pl.DeviceIdType.MESH

D = 1024
S = 2048
F = 2816
DEPTH = 2
HEAD_DIM = 64
DILATIONS = (1, 4, 16)
HALF = 64
QKV_A = 2304
QKV_B = 1536
IN_W = 5888
N_DEV = 8
NA_ROWS = 32
GRID_W = 64
NA_KR = 8
ROPE_THETA = 10000.0
RMS_EPS = 1e-6
NEG = -1e30
SCALE = HEAD_DIM ** -0.5
ADAM_LR, ADAM_B1, ADAM_B2, ADAM_EPS, ADAM_WD, ADAM_STEP = 0.001, 0.9, 0.999, 1e-08, 0.01, 10
VMEM_LIMIT_V7X = 52 * 1024 * 1024
SMALL_ROWS = 120
BIAS_PAD = 3840


def _cp(*sem):
    return pltpu.CompilerParams(dimension_semantics=sem, vmem_limit_bytes=VMEM_LIMIT_V7X)


def _dot_nn(a, b):
    return jnp.dot(a, b, preferred_element_type=F32)


def _dot_nt(a, b):
    return lax.dot_general(a, b, (((1,), (1,)), ((), ())), preferred_element_type=F32)


def _dot_tn(a, b):
    return lax.dot_general(a, b, (((0,), (0,)), ((), ())), preferred_element_type=F32)


def _ds(start, size, stride):
    return pl.ds(start, size) if stride == 1 else pl.ds(start, size, stride=stride)


def _norm_fwd(x, g, tag):
    t = x.shape[0]
    tm = 512

    def body(x_ref, g_ref, o_ref):
        xv = x_ref[...]
        r = lax.rsqrt(jnp.mean(xv * xv, axis=-1, keepdims=True) + RMS_EPS)
        o_ref[...] = (xv * r * g_ref[...]).astype(BF16)

    return pl.pallas_call(
        body, name=f"norm_fwd_{tag}", grid=(t // tm,),
        in_specs=[BS((tm, D), lambda i: (i, 0)), BS((1, D), lambda i: (0, 0))],
        out_specs=BS((tm, D), lambda i: (i, 0)),
        out_shape=SDS((t, D), BF16), compiler_params=_cp("parallel"),
    )(x, g.reshape(1, D))


def _norm_bwd(x, g, dh, dres, tag):
    t = x.shape[0]
    tm = 512

    def body(x_ref, g_ref, dh_ref, dr_ref, dx_ref, dg_ref):
        @pl.when(pl.program_id(0) == 0)
        def _():
            dg_ref[...] = jnp.zeros_like(dg_ref)

        xv = x_ref[...]
        r = lax.rsqrt(jnp.mean(xv * xv, axis=-1, keepdims=True) + RMS_EPS)
        xh = xv * r
        dh = dh_ref[...]
        u = dh * g_ref[...]
        dx_ref[...] = dr_ref[...] + r * (u - xh * jnp.mean(xh * u, axis=-1, keepdims=True))
        dg_ref[...] += jnp.sum(dh * xh, axis=0, keepdims=True)

    row = BS((tm, D), lambda i: (i, 0))
    vec = BS((1, D), lambda i: (0, 0))
    return pl.pallas_call(
        body, name=f"norm_bwd_{tag}", grid=(t // tm,),
        in_specs=[row, vec, row, row], out_specs=[row, vec],
        out_shape=[SDS((t, D), F32), SDS((1, D), F32)], compiler_params=_cp("arbitrary"),
    )(x, g.reshape(1, D), dh, dres)


def _loss_head(x, g, tgt):
    t = x.shape[0]
    tm = 512

    def body(x_ref, g_ref, t_ref, loss_ref, dx_ref, dg_ref):
        @pl.when(pl.program_id(0) == 0)
        def _():
            dg_ref[...] = jnp.zeros_like(dg_ref)
            loss_ref[...] = jnp.zeros_like(loss_ref)

        xv = x_ref[...]
        gv = g_ref[...]
        r = lax.rsqrt(jnp.mean(xv * xv, axis=-1, keepdims=True) + RMS_EPS)
        xh = xv * r
        e = xh * gv - t_ref[...]
        loss_ref[...] += 0.5 * jnp.sum(jnp.mean(e * e, axis=-1, keepdims=True), axis=0, keepdims=True)
        dy = e * (1.0 / D)
        u = dy * gv
        dx_ref[...] = r * (u - xh * jnp.mean(xh * u, axis=-1, keepdims=True))
        dg_ref[...] += jnp.sum(dy * xh, axis=0, keepdims=True)

    row = BS((tm, D), lambda i: (i, 0))
    vec = BS((1, D), lambda i: (0, 0))
    return pl.pallas_call(
        body, name="loss_head", grid=(t // tm,),
        in_specs=[row, vec, row], out_specs=[BS((1, 128), lambda i: (0, 0)), row, vec],
        out_shape=[SDS((1, 128), F32), SDS((t, D), F32), SDS((1, D), F32)],
        compiler_params=_cp("arbitrary"),
    )(x, g.reshape(1, D), tgt)


def _mm_nn(a, w, tag, res=None, scale=1.0, tm=512, tn=None):
    c_n, t, k = a.shape
    n = w.shape[2]
    tn = n if tn is None else tn

    def body(*refs):
        a_ref, w_ref = refs[0], refs[1]
        o_ref = refs[-1]
        acc = _dot_nn(a_ref[0].astype(BF16), w_ref[0])
        for c in range(1, c_n):
            acc = acc + _dot_nn(a_ref[c].astype(BF16), w_ref[c])
        if scale != 1.0:
            acc = acc * scale
        if res is not None:
            acc = refs[2][...] + acc
        o_ref[...] = acc

    in_specs = [BS((c_n, tm, k), lambda i, j: (0, i, 0)), BS((c_n, k, tn), lambda i, j: (0, 0, j))]
    args = [a, w]
    if res is not None:
        in_specs.append(BS((tm, tn), lambda i, j: (i, j)))
        args.append(res)
    return pl.pallas_call(
        body, name=f"mm_nn_{tag}", grid=(t // tm, n // tn), in_specs=in_specs,
        out_specs=BS((tm, tn), lambda i, j: (i, j)), out_shape=SDS((t, n), F32),
        compiler_params=_cp("parallel", "parallel"),
    )(*args)


def _mm_nt_rows(a, w, tag, tm, tn, n_total, w_row0):
    t, k = a.shape
    assert w_row0 % tn == 0 and n_total % tn == 0
    j0 = w_row0 // tn

    def body(a_ref, w_ref, o_ref):
        o_ref[...] = _dot_nt(a_ref[...].astype(BF16), w_ref[...])

    return pl.pallas_call(
        body, name=f"mm_nt_{tag}", grid=(t // tm, n_total // tn),
        in_specs=[BS((tm, k), lambda i, j: (i, 0)), BS((tn, k), lambda i, j: (j0 + j, 0))],
        out_specs=BS((tm, tn), lambda i, j: (i, j)), out_shape=SDS((t, n_total), F32),
        compiler_params=_cp("parallel", "parallel"),
    )(a, w)


def _mm_tn(a, b, tag, scale=1.0, tmm=None, tk=512):
    c_n, t, m = a.shape
    n = b.shape[1]
    tmm = m if tmm is None else tmm
    nk = t // tk

    def body(a_ref, b_ref, o_ref, acc_ref):
        kk = pl.program_id(2)

        @pl.when(kk == 0)
        def _():
            acc_ref[...] = jnp.zeros_like(acc_ref)

        acc_ref[...] += _dot_tn(a_ref[...].astype(BF16), b_ref[...].astype(BF16))

        @pl.when(kk == nk - 1)
        def _():
            o_ref[...] = (acc_ref[...] * scale).astype(BF16)

    return pl.pallas_call(
        body, name=f"mm_tn_{tag}", grid=(c_n, m // tmm, nk),
        in_specs=[BS((None, tk, tmm), lambda c, mi, kk: (c, kk, mi)), BS((tk, n), lambda c, mi, kk: (kk, 0))],
        out_specs=BS((None, tmm, n), lambda c, mi, kk: (c, mi, 0)),
        out_shape=SDS((c_n, m, n), BF16), scratch_shapes=[pltpu.VMEM((tmm, n), F32)],
        compiler_params=_cp("parallel", "parallel", "arbitrary"),
    )(a, b)


def _ffn_up(hn, wut, tag):
    t = hn.shape[0]
    tm, tn = 512, 1408

    def body(h_ref, w_ref, gu_ref, act_ref):
        h = h_ref[...]
        g = _dot_nt(h, w_ref[0])
        u = _dot_nt(h, w_ref[1])
        gu_ref[0] = g.astype(BF16)
        gu_ref[1] = u.astype(BF16)
        act_ref[...] = (g * jax.nn.sigmoid(g) * u).astype(BF16)

    return pl.pallas_call(
        body, name=f"ffn_up_{tag}", grid=(t // tm, F // tn),
        in_specs=[BS((tm, D), lambda i, j: (i, 0)), BS((2, tn, D), lambda i, j: (0, j, 0))],
        out_specs=[BS((2, tm, tn), lambda i, j: (0, i, j)), BS((tm, tn), lambda i, j: (i, j))],
        out_shape=[SDS((2, t, F), BF16), SDS((t, F), BF16)],
        compiler_params=_cp("parallel", "parallel"),
    )(hn, wut)


def _ffn_dact(dxo, wd, gu, tag):
    t = dxo.shape[0]
    tm, tn = 512, 1408

    def body(d_ref, w_ref, gu_ref, o_ref):
        dact = _dot_nt(d_ref[...].astype(BF16), w_ref[...]) * 0.5
        g = gu_ref[0].astype(F32)
        u = gu_ref[1].astype(F32)
        sg = jax.nn.sigmoid(g)
        o_ref[0] = (dact * u * (sg * (1.0 + g * (1.0 - sg)))).astype(BF16)
        o_ref[1] = (dact * (g * sg)).astype(BF16)

    return pl.pallas_call(
        body, name=f"ffn_dact_{tag}", grid=(t // tm, F // tn),
        in_specs=[BS((tm, D), lambda i, j: (i, 0)), BS((tn, D), lambda i, j: (j, 0)),
                  BS((2, tm, tn), lambda i, j: (0, i, j))],
        out_specs=BS((2, tm, tn), lambda i, j: (0, i, j)),
        out_shape=SDS((2, t, F), BF16), compiler_params=_cp("parallel", "parallel"),
    )(dxo, wd, gu)


def _rope_tables():
    half = HEAD_DIM // 2
    inv_freq = ROPE_THETA ** (-jnp.arange(half, dtype=F32) / half)
    ang = jnp.arange(S).astype(F32)[:, None] * inv_freq[None, :]
    cos, sin = jnp.cos(ang), jnp.sin(ang)
    return jnp.concatenate([cos, cos, cos, cos], axis=1), jnp.concatenate([-sin, sin, -sin, sin], axis=1)


def _swap_halves(t, first_half):
    return jnp.where(first_half, pltpu.roll(t, 96, 1), pltpu.roll(t, 32, 1))


def _rope_fwd(proj, cos_t, sin_t):
    t = proj.shape[0]
    tm = 512
    width = 2 * QKV_A // 3

    def body(x_ref, c_ref, s_ref, o_ref):
        c = c_ref[...]
        sg = s_ref[...]
        first = (lax.broadcasted_iota(jnp.int32, (tm, 128), 1) % HEAD_DIM) < HEAD_DIM // 2
        for j in range(width // 128):
            v = x_ref[:, 128 * j:128 * (j + 1)]
            o_ref[:, 128 * j:128 * (j + 1)] = v * c + _swap_halves(v, first) * sg

    tab = BS((tm, 128), lambda i: (i % (S // tm), 0))
    return pl.pallas_call(
        body, name="rope_fwd", grid=(t // tm,),
        in_specs=[BS((tm, width), lambda i: (i, 0)), tab, tab],
        out_specs=BS((tm, width), lambda i: (i, 0)), out_shape=SDS((t, width), F32),
        compiler_params=_cp("parallel"),
    )(proj, cos_t, sin_t)


def _rope_bwd(dqs, dks, cos_t, sin_t):
    t = dqs[0].shape[0]
    tm = 512

    def body(*refs):
        c = refs[6][...]
        sg = refs[7][...]
        o_ref = refs[8]
        first = (lax.broadcasted_iota(jnp.int32, (tm, 128), 1) % HEAD_DIM) < HEAD_DIM // 2
        for a in range(6):
            for hp in range(2):
                v = refs[a][:, 128 * hp:128 * (hp + 1)]
                col = 128 * (2 * a + hp)
                o_ref[:, col:col + 128] = (v * c + _swap_halves(v * sg, first)).astype(BF16)

    blk = BS((tm, 256), lambda i: (i, 0))
    tab = BS((tm, 128), lambda i: (i % (S // tm), 0))
    return pl.pallas_call(
        body, name="rope_bwd", grid=(t // tm,), in_specs=[blk] * 6 + [tab, tab],
        out_specs=BS((tm, 1536), lambda i: (i, 0)), out_shape=SDS((t, 1536), BF16),
        compiler_params=_cp("parallel"),
    )(*dqs, *dks, cos_t, sin_t)


def _head_masks():
    lane = lax.broadcasted_iota(jnp.int32, (1, 128), 1)
    m0 = (lane < HEAD_DIM).astype(F32)
    return m0, 1.0 - m0


def _dil_geometry(d):
    sub = S // d
    q_rows = 128
    k_rows = min(256, sub)
    return sub, q_rows, sub // q_rows, k_rows


def _dil_tile(idx, d):
    sub, q_rows, nb, k_rows = _dil_geometry(d)
    r = idx // nb
    n = idx % nb
    k_sub = jnp.clip(q_rows * n - HALF, 0, sub - k_rows)
    if d == 1:
        q_start = pl.multiple_of(q_rows * n, q_rows)
        k_start = pl.multiple_of(k_sub, HALF)
    else:
        q_start = q_rows * n * d + r
        k_start = k_sub * d + r
    ii = lax.broadcasted_iota(jnp.int32, (q_rows, k_rows), 0)
    jj = lax.broadcasted_iota(jnp.int32, (q_rows, k_rows), 1)
    valid = jnp.abs(jj - ii + (k_sub - q_rows * n)) <= HALF
    return q_start, k_start, valid


def _dil_specs(grp):
    qs = BS((S, 128), lambda b, hp: (b, 2 * grp + hp))
    ks = BS((S, 128), lambda b, hp: (b, 6 + 2 * grp + hp))
    vs = BS((S, 128), lambda b, hp: (b, 12 + 2 * grp + hp))
    own = BS((S, 128), lambda b, hp: (b, hp))
    return qs, ks, vs, own


def _dil_fwd(qkr, proj, grp):
    t = qkr.shape[0]
    d = DILATIONS[grp]
    _, q_rows, nb, k_rows = _dil_geometry(d)

    def body(q_ref, k_ref, v_ref, o_ref, l_ref):
        masks = _head_masks()

        def step(idx, carry):
            q_start, k_start, valid = _dil_tile(idx, d)
            q = q_ref[_ds(q_start, q_rows, d), :]
            kb = k_ref[_ds(k_start, k_rows, d), :].astype(BF16)
            v = v_ref[_ds(k_start, k_rows, d), :]
            o2 = jnp.zeros((q_rows, 128), F32)
            l2 = jnp.zeros((q_rows, 128), F32)
            for mh in masks:
                s = jnp.where(valid, _dot_nt((q * mh).astype(BF16), kb) * SCALE, NEG)
                mx = jnp.max(s, axis=1, keepdims=True)
                p = jnp.exp(s - mx)
                den = jnp.sum(p, axis=1, keepdims=True)
                o2 = o2 + _dot_nn(p.astype(BF16), (v * mh).astype(BF16)) / den
                l2 = l2 + (mx + jnp.log(den)) * mh
            o_ref[_ds(q_start, q_rows, d), :] = o2
            l_ref[_ds(q_start, q_rows, d), :] = l2
            return carry

        lax.fori_loop(0, d * nb, step, 0)

    qs, ks, vs, own = _dil_specs(grp)
    return pl.pallas_call(
        body, name=f"dil_fwd_{grp}", grid=(t // S, 2), in_specs=[qs, ks, vs], out_specs=[own, own],
        out_shape=[SDS((t, 256), F32), SDS((t, 256), F32)], compiler_params=_cp("parallel", "parallel"),
    )(qkr, qkr, proj)


def _dil_bwd(qkr, proj, do, dlp, lse, grp):
    t = qkr.shape[0]
    d = DILATIONS[grp]
    _, q_rows, nb, k_rows = _dil_geometry(d)

    def body(q_ref, k_ref, v_ref, do_ref, dl_ref, l_ref, dq_ref, dk_ref, dv_ref):
        masks = _head_masks()
        dk_ref[...] = jnp.zeros_like(dk_ref)
        dv_ref[...] = jnp.zeros_like(dv_ref)

        def step(idx, carry):
            q_start, k_start, valid = _dil_tile(idx, d)
            q_rows_ds = _ds(q_start, q_rows, d)
            k_rows_ds = _ds(k_start, k_rows, d)
            q = q_ref[q_rows_ds, :]
            k = k_ref[k_rows_ds, :]
            kb = k.astype(BF16)
            vb = v_ref[k_rows_ds, :].astype(BF16)
            do2 = do_ref[q_rows_ds, :]
            dl2 = dl_ref[q_rows_ds, :]
            l2 = l_ref[q_rows_ds, :]
            dq2 = jnp.zeros((q_rows, 128), F32)
            dkw = jnp.zeros((k_rows, 128), F32)
            dvw = jnp.zeros((k_rows, 128), F32)
            for h, mh in enumerate(masks):
                col = HEAD_DIM * h
                qh = (q * mh).astype(BF16)
                s = jnp.where(valid, _dot_nt(qh, kb) * SCALE, NEG)
                p = jnp.exp(s - l2[:, col:col + 1])
                doh = (do2 * mh).astype(BF16)
                ds = (p * (_dot_nt(doh, vb) - dl2[:, col:col + 1])).astype(BF16)
                dq2 = dq2 + _dot_nn(ds, (k * mh).astype(BF16))
                dkw = dkw + _dot_tn(ds, qh)
                dvw = dvw + _dot_tn(p.astype(BF16), doh)
            dq_ref[q_rows_ds, :] = dq2 * SCALE
            dk_ref[k_rows_ds, :] += dkw * SCALE
            dv_ref[k_rows_ds, :] += dvw
            return carry

        lax.fori_loop(0, d * nb, step, 0)

    qs, ks, vs, own = _dil_specs(grp)
    return pl.pallas_call(
        body, name=f"dil_bwd_{grp}", grid=(t // S, 2), in_specs=[qs, ks, vs, own, own, own],
        out_specs=[own, own, own], out_shape=[SDS((t, 256), F32)] * 3,
        compiler_params=_cp("parallel", "parallel"),
    )(qkr, qkr, proj, do, dlp, lse)


def _mix_weights(l0, l1, l2):
    mx = jnp.maximum(jnp.maximum(l0, l1), l2)
    e0, e1, e2 = jnp.exp(l0 - mx), jnp.exp(l1 - mx), jnp.exp(l2 - mx)
    den = e0 + e1 + e2
    return e0 / den, e1 / den, e2 / den


def _combine_fwd(outs, lses):
    t = outs[0].shape[0]
    tm = 512

    def body(o0, o1, o2, l0, l1, l2, y_ref):
        w0, w1, w2 = _mix_weights(l0[...], l1[...], l2[...])
        y_ref[...] = w0 * o0[...] + w1 * o1[...] + w2 * o2[...]

    blk = BS((tm, 256), lambda i: (i, 0))
    return pl.pallas_call(
        body, name="combine_fwd", grid=(t // tm,), in_specs=[blk] * 6, out_specs=blk,
        out_shape=SDS((t, 256), F32), compiler_params=_cp("parallel"),
    )(*outs, *lses)


def _head_sum(x):
    a = lax.broadcasted_iota(jnp.int32, (256, 256), 0) // HEAD_DIM
    b = lax.broadcasted_iota(jnp.int32, (256, 256), 1) // HEAD_DIM
    ones = (a == b).astype(BF16)
    hi = x.astype(BF16)
    lo = (x - hi.astype(F32)).astype(BF16)
    return _dot_nn(hi, ones) + _dot_nn(lo, ones)


def _combine_bwd(dya, outs, lses):
    t = dya.shape[0]
    tm = 512

    def body(dy_ref, o0, o1, o2, l0, l1, l2, d0, d1, d2, e0, e1, e2):
        ws = _mix_weights(l0[...], l1[...], l2[...])
        dy = dy_ref[...]
        ya = ws[0] * o0[...] + ws[1] * o1[...] + ws[2] * o2[...]
        hs = _head_sum(dy * ya)
        for w, d_ref, e_ref in zip(ws, (d0, d1, d2), (e0, e1, e2)):
            d_ref[...] = w * dy
            e_ref[...] = w * hs

    blk = BS((tm, 256), lambda i: (i, 0))
    return pl.pallas_call(
        body, name="combine_bwd", grid=(t // tm,), in_specs=[blk] * 7, out_specs=[blk] * 6,
        out_shape=[SDS((t, 256), F32)] * 6, compiler_params=_cp("parallel"),
    )(dya, *outs, *lses)


def _na_bias_table(rel_bias):
    qc = np.arange(GRID_W)[:, None]
    kc = np.arange(GRID_W)[None, :]
    win_lo = np.clip(qc - 8, 0, GRID_W - 16)
    col_valid = (kc >= win_lo) & (kc < win_lo + 16)
    col_idx = np.clip(kc - qc + 15, 0, 30)
    row_idx = np.arange(NA_KR)[:, None] + np.arange(NA_KR)[None, :]
    rows = (row_idx[..., None] == np.arange(2 * NA_KR - 1)).astype(np.float32)
    cols = (col_idx[..., None] == np.arange(31)).astype(np.float32)
    b = jnp.einsum("hrd,ckr,qjd->hcqkj", rel_bias.astype(F32), rows, cols, precision=lax.Precision.HIGHEST)
    b = jnp.where(col_valid[None, None, :, None, :], b, NEG)
    return b.reshape(8, NA_KR, GRID_W, NA_KR * GRID_W)


def _na_row(i):
    lo = jnp.clip(i - NA_KR // 2, 0, NA_ROWS - NA_KR)
    return pl.multiple_of(GRID_W * i, GRID_W), pl.multiple_of(GRID_W * lo, GRID_W), lo - i + NA_KR - 1


def _na_fwd(proj, bias):
    t = proj.shape[0]
    kw = NA_KR * GRID_W

    def body(q_ref, k_ref, v_ref, b_ref, o_ref, l_ref):
        masks = _head_masks()

        def step(i, carry):
            q_start, k_start, cls = _na_row(i)
            q = q_ref[pl.ds(q_start, GRID_W), :]
            kb = k_ref[pl.ds(k_start, kw), :].astype(BF16)
            v = v_ref[pl.ds(k_start, kw), :]
            o2 = jnp.zeros((GRID_W, 128), F32)
            l2 = jnp.zeros((GRID_W, 128), F32)
            for h, mh in enumerate(masks):
                s = _dot_nt((q * mh).astype(BF16), kb) * SCALE + b_ref[h, cls]
                mx = jnp.max(s, axis=1, keepdims=True)
                p = jnp.exp(s - mx)
                den = jnp.sum(p, axis=1, keepdims=True)
                o2 = o2 + _dot_nn((p / den).astype(BF16), (v * mh).astype(BF16))
                l2 = l2 + (mx + jnp.log(den)) * mh
            o_ref[pl.ds(q_start, GRID_W), :] = o2
            l_ref[pl.ds(q_start, GRID_W), :] = l2
            return carry

        lax.fori_loop(0, NA_ROWS, step, 0)

    c0 = QKV_A // 128
    own = BS((S, 128), lambda b, hp: (b, hp))
    return pl.pallas_call(
        body, name="na_fwd", grid=(t // S, 4),
        in_specs=[BS((S, 128), lambda b, hp: (b, c0 + hp)), BS((S, 128), lambda b, hp: (b, c0 + 4 + hp)),
                  BS((S, 128), lambda b, hp: (b, c0 + 8 + hp)),
                  BS((2, NA_KR, GRID_W, kw), lambda b, hp: (hp, 0, 0, 0))],
        out_specs=[own, own], out_shape=[SDS((t, 512), F32), SDS((t, 512), F32)],
        compiler_params=_cp("parallel", "parallel"),
    )(proj, proj, proj, bias)


def _na_bwd(proj, bias, dyb, yb, lse):
    t = proj.shape[0]
    kw = NA_KR * GRID_W

    def body(q_ref, k_ref, v_ref, b_ref, do_ref, o_ref, l_ref, dq_ref, dk_ref, dv_ref, db_ref):
        masks = _head_masks()

        @pl.when(pl.program_id(1) == 0)
        def _():
            db_ref[...] = jnp.zeros_like(db_ref)

        dk_ref[...] = jnp.zeros_like(dk_ref)
        dv_ref[...] = jnp.zeros_like(dv_ref)

        def step(i, carry):
            q_start, k_start, cls = _na_row(i)
            q = q_ref[pl.ds(q_start, GRID_W), :]
            k = k_ref[pl.ds(k_start, kw), :]
            kb = k.astype(BF16)
            vb = v_ref[pl.ds(k_start, kw), :].astype(BF16)
            do2 = do_ref[pl.ds(q_start, GRID_W), :]
            o2 = o_ref[pl.ds(q_start, GRID_W), :]
            l2 = l_ref[pl.ds(q_start, GRID_W), :]
            dq2 = jnp.zeros((GRID_W, 128), F32)
            dkw = jnp.zeros((kw, 128), F32)
            dvw = jnp.zeros((kw, 128), F32)
            for h, mh in enumerate(masks):
                col = HEAD_DIM * h
                delta = jnp.sum(do2 * o2 * mh, axis=1, keepdims=True)
                qh = (q * mh).astype(BF16)
                s = _dot_nt(qh, kb) * SCALE + b_ref[h, cls]
                p = jnp.exp(s - l2[:, col:col + 1])
                doh = (do2 * mh).astype(BF16)
                ds = p * (_dot_nt(doh, vb) - delta)
                db_ref[h, cls] += ds
                dsb = ds.astype(BF16)
                dq2 = dq2 + _dot_nn(dsb, (k * mh).astype(BF16))
                dkw = dkw + _dot_tn(dsb, qh)
                dvw = dvw + _dot_tn(p.astype(BF16), doh)
            dq_ref[pl.ds(q_start, GRID_W), :] = dq2 * SCALE
            dk_ref[pl.ds(k_start, kw), :] += dkw * SCALE
            dv_ref[pl.ds(k_start, kw), :] += dvw
            return carry

        lax.fori_loop(0, NA_ROWS, step, 0)

    c0 = QKV_A // 128
    own = BS((S, 128), lambda hp, b: (b, hp))
    tab = BS((2, NA_KR, GRID_W, kw), lambda hp, b: (hp, 0, 0, 0))
    return pl.pallas_call(
        body, name="na_bwd", grid=(4, t // S),
        in_specs=[BS((S, 128), lambda hp, b: (b, c0 + hp)), BS((S, 128), lambda hp, b: (b, c0 + 4 + hp)),
                  BS((S, 128), lambda hp, b: (b, c0 + 8 + hp)), tab, own, own, own],
        out_specs=[own, own, own, tab],
        out_shape=[SDS((t, 512), F32)] * 3 + [SDS((8, NA_KR, GRID_W, kw), F32)],
        compiler_params=_cp("parallel", "arbitrary"),
    )(proj, proj, proj, bias, dyb, yb, lse)


def _na_dbias(db):
    kw = NA_KR * GRID_W

    def body(x_ref, o_ref, z_ref):
        lane = lax.broadcasted_iota(jnp.int32, (GRID_W, kw), 1)
        sub = lax.broadcasted_iota(jnp.int32, (GRID_W, kw), 0)
        rel = (lane % GRID_W) - sub + 15
        key_row = lax.broadcasted_iota(jnp.int32, (kw, 128), 0) // GRID_W
        dr = lax.broadcasted_iota(jnp.int32, (kw, 128), 1)
        acc = jnp.zeros((32, 128), F32)
        z_ref[...] = jnp.zeros_like(z_ref)
        for cls in range(NA_KR):
            xv = x_ref[cls]
            for dc in range(31):
                z_ref[dc:dc + 1, :] = jnp.sum(jnp.where(rel == dc, xv, 0.0), axis=0, keepdims=True)
            z = z_ref[...]
            ind = (key_row + cls == dr).astype(BF16)
            hi = z.astype(BF16)
            lo = (z - hi.astype(F32)).astype(BF16)
            acc = acc + _dot_nn(hi, ind) + _dot_nn(lo, ind)
        o_ref[...] = acc

    return pl.pallas_call(
        body, name="na_dbias", grid=(8,),
        in_specs=[BS((None, NA_KR, GRID_W, kw), lambda h: (h, 0, 0, 0))],
        out_specs=BS((None, 32, 128), lambda h: (h, 0, 0)), out_shape=SDS((8, 32, 128), F32),
        scratch_shapes=[pltpu.VMEM((32, kw), F32)], compiler_params=_cp("parallel"),
    )(db)


def _merge_fwd(ya, yb, proj, wat, wbt):
    t = ya.shape[0]
    tm, tn = 512, 256
    ca = (QKV_A + QKV_B) // tn
    cb = ca + D // tn

    def body(ya_ref, yb_ref, la_ref, lb_ref, wa_ref, wb_ref, m_ref, za_ref, zb_ref):
        za = _dot_nt(ya_ref[...].astype(BF16), wa_ref[...])
        zb = _dot_nt(yb_ref[...].astype(BF16), wb_ref[...])
        m_ref[...] = (jax.nn.sigmoid(la_ref[...]) * za + jax.nn.sigmoid(lb_ref[...]) * zb).astype(BF16)
        za_ref[...] = za.astype(BF16)
        zb_ref[...] = zb.astype(BF16)

    out = BS((tm, tn), lambda i, j: (i, j))
    return pl.pallas_call(
        body, name="merge_fwd", grid=(t // tm, D // tn),
        in_specs=[BS((tm, 256), lambda i, j: (i, 0)), BS((tm, 512), lambda i, j: (i, 0)),
                  BS((tm, tn), lambda i, j: (i, ca + j)), BS((tm, tn), lambda i, j: (i, cb + j)),
                  BS((tn, 256), lambda i, j: (j, 0)), BS((tn, 512), lambda i, j: (j, 0))],
        out_specs=[out, out, out], out_shape=[SDS((t, D), BF16)] * 3,
        compiler_params=_cp("parallel", "parallel"),
    )(ya, yb, proj, proj, wat, wbt)


def _merge_bwd(dm, za, zb, proj):
    t = dm.shape[0]
    tm, tn = 512, 256
    ca = (QKV_A + QKV_B) // tn
    cb = ca + D // tn

    def body(dm_ref, za_ref, zb_ref, la_ref, lb_ref, dza_ref, dzb_ref, dl_ref):
        dmv = dm_ref[...]
        ga = jax.nn.sigmoid(la_ref[...])
        gb = jax.nn.sigmoid(lb_ref[...])
        dza_ref[...] = (dmv * ga).astype(BF16)
        dzb_ref[...] = (dmv * gb).astype(BF16)
        dl_ref[0] = (dmv * za_ref[...].astype(F32) * ga * (1.0 - ga)).astype(BF16)
        dl_ref[1] = (dmv * zb_ref[...].astype(F32) * gb * (1.0 - gb)).astype(BF16)

    blk = BS((tm, tn), lambda i, j: (i, j))
    return pl.pallas_call(
        body, name="merge_bwd", grid=(t // tm, D // tn),
        in_specs=[blk, blk, blk, BS((tm, tn), lambda i, j: (i, ca + j)), BS((tm, tn), lambda i, j: (i, cb + j))],
        out_specs=[blk, blk, BS((2, tm, tn), lambda i, j: (0, i, j))],
        out_shape=[SDS((t, D), BF16), SDS((t, D), BF16), SDS((2, t, D), BF16)],
        compiler_params=_cp("parallel", "parallel"),
    )(dm, za, zb, proj, proj)


def _sum_slots(recv0, recv1, tag):
    _, r, c = recv0.shape
    tr = r if r * c <= 512 * 1024 else r // 2

    def body(a_ref, b_ref, o_ref):
        for layer, ref in enumerate((a_ref, b_ref)):
            acc = ref[0].astype(F32)
            for s in range(1, N_DEV):
                acc = acc + ref[s].astype(F32)
            o_ref[layer] = acc

    blk = BS((N_DEV, tr, c), lambda i: (0, i, 0))
    return pl.pallas_call(
        body, name=f"sum_slots_{tag}", grid=(r // tr,), in_specs=[blk, blk],
        out_specs=BS((2, tr, c), lambda i: (0, i, 0)), out_shape=SDS((2, r, c), F32),
        compiler_params=_cp("parallel"),
    )(recv0, recv1)


def _adamw(w, g, m, v, tag):
    layers, r, c = w.shape
    tr = next(r // k for k in (1, 2, 4, 8) if r // k <= 384 and r % (8 * k) == 0)

    def body(w_ref, g_ref, m_ref, v_ref, d_ref, mo_ref, vo_ref):
        gv = g_ref[...]
        mn = ADAM_B1 * m_ref[...] + (1.0 - ADAM_B1) * gv
        vn = ADAM_B2 * v_ref[...] + (1.0 - ADAM_B2) * (gv * gv)
        m_hat = mn / (1.0 - ADAM_B1 ** ADAM_STEP)
        v_hat = vn / (1.0 - ADAM_B2 ** ADAM_STEP)
        d_ref[...] = -ADAM_LR * (m_hat / (jnp.sqrt(v_hat) + ADAM_EPS) + ADAM_WD * w_ref[...])
        mo_ref[...] = mn
        vo_ref[...] = vn

    blk = BS((None, tr, c), lambda l, i: (l, i, 0))
    return pl.pallas_call(
        body, name=f"adamw_{tag}", grid=(layers, r // tr), in_specs=[blk] * 4, out_specs=[blk] * 3,
        out_shape=[SDS((layers, r, c), F32)] * 3, compiler_params=_cp("parallel", "parallel"),
    )(w, g, m, v)


def _place():
    return lax.axis_index("x"), lax.axis_index("y"), lax.axis_index("c")


def _flip(coord, bit):
    return 1 - coord if bit else coord


def _allgather(shards, tag):
    n_arr = len(shards)
    hbm = BS(memory_space=pl.ANY)

    def body(*refs):
        ins, outs = refs[:n_arr], refs[n_arr:2 * n_arr]
        send_sems, recv_sems, local_sems = refs[2 * n_arr:]
        x, y, c = _place()
        me, sibling = (x, y, c), (x, y, 1 - c)
        chips = [(1 - x, y), (x, 1 - y), (1 - x, 1 - y)]

        def rows(a, p):
            r = shards[a].shape[0]
            return outs[a].at[pl.ds((4 * p[0] + 2 * p[1] + p[2]) * r, r), :]

        def copy(a, k, block, to, src=None):
            return pltpu.make_async_remote_copy(
                src_ref=rows(a, block) if src is None else src, dst_ref=rows(a, block),
                send_sem=send_sems.at[a, k], recv_sem=recv_sems.at[a, k], device_id=to, device_id_type=MESH)

        mine = [pltpu.make_async_copy(ins[a], rows(a, me), local_sems.at[a]) for a in range(n_arr)]
        for cp in mine:
            cp.start()
        first = []
        for a in range(n_arr):
            first.append(copy(a, 0, me, sibling, src=ins[a]))
            first += [copy(a, 1 + j, me, (*chip, c), src=ins[a]) for j, chip in enumerate(chips)]
        for cp in first:
            cp.start()
        passed = []
        for a in range(n_arr):
            for j, chip in enumerate(chips):
                copy(a, 1 + j, (*chip, c), me).wait_recv()
                passed.append(copy(a, 4 + j, (*chip, c), sibling))
                passed[-1].start()
        for a in range(n_arr):
            copy(a, 0, sibling, me).wait_recv()
            for j, chip in enumerate(chips):
                copy(a, 4 + j, (*chip, 1 - c), me).wait_recv()
        for cp in first + passed:
            cp.wait_send()
        for cp in mine:
            cp.wait()

    return pl.pallas_call(
        body, name=f"allgather_{tag}", in_specs=[hbm] * n_arr, out_specs=[hbm] * n_arr,
        out_shape=[SDS((N_DEV * s.shape[0], s.shape[1]), s.dtype) for s in shards],
        scratch_shapes=[pltpu.SemaphoreType.DMA((n_arr, 7)), pltpu.SemaphoreType.DMA((n_arr, 7)),
                        pltpu.SemaphoreType.DMA((n_arr,))],
        compiler_params=pltpu.CompilerParams(has_side_effects=True),
    )(*shards)


def _peers(x, y, c):
    peers = []
    for mask in range(1, N_DEV):
        p = (_flip(x, mask & 4), _flip(y, mask & 2), _flip(c, mask & 1))
        peers.append((p, 4 * p[0] + 2 * p[1] + p[2]))
    return peers


def _exchange_refs(mode, src, land, me, peer):
    if mode == "gather":
        r = src.shape[0]
        return src, land.at[pl.ds(me * r, r), :], land.at[pl.ds(peer * r, r), :]
    r = land.shape[1]
    return src.at[pl.ds(peer * r, r), :], land.at[me], land.at[peer]


HBM_SPEC = BS(memory_space=pltpu.HBM)
SEM_SPEC = BS(memory_space=pltpu.SEMAPHORE)
DATAFLOW = pltpu.SideEffectType.DATAFLOW_SIDE_EFFECTING


def _place_own(mode, srcs, tag):
    n = len(srcs)
    if mode == "gather":
        shapes = [(N_DEV * s.shape[0], s.shape[1]) for s in srcs]
    else:
        shapes = [(N_DEV, s.shape[0] // N_DEV, s.shape[1]) for s in srcs]

    def body(*refs):
        src_refs, land_refs, sems = refs[:n], refs[n:2 * n], refs[2 * n]
        x, y, c = _place()
        me = 4 * x + 2 * y + c
        copies = []
        for a in range(n):
            if mode == "gather":
                r = srcs[a].shape[0]
                cp = pltpu.make_async_copy(src_refs[a], land_refs[a].at[pl.ds(me * r, r), :], sems.at[a])
            else:
                r = shapes[a][1]
                cp = pltpu.make_async_copy(src_refs[a].at[pl.ds(me * r, r), :], land_refs[a].at[me], sems.at[a])
            cp.start()
            copies.append(cp)
        for cp in copies:
            cp.wait()

    hbm = BS(memory_space=pl.ANY)
    return pl.pallas_call(
        body, name=f"{mode}_own_{tag}", in_specs=[hbm] * n, out_specs=[hbm] * n,
        out_shape=[SDS(sh, s.dtype) for sh, s in zip(shapes, srcs)],
        scratch_shapes=[pltpu.SemaphoreType.DMA((n,))],
    )(*srcs)


def _exchange_start(mode, srcs, after, tag):
    n = len(srcs)
    lands = _place_own(mode, srcs, tag)
    behind = [] if after is None else [after]

    def body(*refs):
        src_refs, land_refs = refs[:n], refs[n:2 * n]
        send_sems, recv_sems = refs[2 * n + len(behind)], refs[2 * n + len(behind) + 1]
        token = refs[-1]
        bx, by, bc = _place()
        me = 4 * bx + 2 * by + bc
        for a in range(n):
            for k, (p, idx) in enumerate(_peers(bx, by, bc)):
                out, there, _ = _exchange_refs(mode, src_refs[a], land_refs[a], me, idx)
                pltpu.make_async_remote_copy(
                    src_ref=out, dst_ref=there, send_sem=send_sems.at[7 * a + k], recv_sem=recv_sems.at[7 * a + k],
                    device_id=p, device_id_type=MESH).start()
        token[...] = jnp.zeros_like(token)

    res = pl.pallas_call(
        body, name=f"{mode}_start_{tag}",
        out_shape=(pltpu.SemaphoreType.DMA((7 * n,)), pltpu.SemaphoreType.DMA((7 * n,)),
                   *[pltpu.HBM(s.shape, s.dtype) for s in srcs], *[pltpu.HBM(l.shape, l.dtype) for l in lands],
                   SDS((8, 128), F32)),
        in_specs=[HBM_SPEC] * (2 * n) + [BS(memory_space=pl.ANY)] * len(behind),
        out_specs=(SEM_SPEC, SEM_SPEC, *[HBM_SPEC] * (2 * n), BS(memory_space=pltpu.VMEM)),
        input_output_aliases={i: 2 + i for i in range(2 * n)},
        compiler_params=pltpu.CompilerParams(has_side_effects=DATAFLOW),
    )(*[pltpu.with_memory_space_constraint(s, pltpu.HBM) for s in srcs],
      *[pltpu.with_memory_space_constraint(l, pltpu.HBM) for l in lands], *behind)
    return (mode, res[0], res[1], res[2:2 + n], res[2 + n:2 + 2 * n]), res[-1]


def _exchange_wait(handle, after, tag):
    mode, send_sems, recv_sems, srcs, lands = handle
    n = len(srcs)

    def body(*refs):
        src_refs, land_refs = refs[:n], refs[n:2 * n]
        send_ref, recv_ref = refs[2 * n], refs[2 * n + 1]
        bx, by, bc = _place()
        me = 4 * bx + 2 * by + bc
        for a in range(n):
            for k, (p, idx) in enumerate(_peers(bx, by, bc)):
                out, _, here = _exchange_refs(mode, src_refs[a], land_refs[a], me, idx)
                cp = pltpu.make_async_remote_copy(
                    src_ref=out, dst_ref=here, send_sem=send_ref.at[7 * a + k], recv_sem=recv_ref.at[7 * a + k],
                    device_id=p, device_id_type=MESH)
                cp.wait_send()
                cp.wait_recv()

    res = pl.pallas_call(
        body, name=f"{mode}_wait_{tag}",
        out_shape=(*[pltpu.HBM(s.shape, s.dtype) for s in srcs], *[pltpu.HBM(l.shape, l.dtype) for l in lands]),
        in_specs=[HBM_SPEC] * (2 * n) + [SEM_SPEC, SEM_SPEC, BS(memory_space=pl.ANY)],
        out_specs=tuple([HBM_SPEC] * (2 * n)),
        input_output_aliases={i: i for i in range(2 * n)},
        compiler_params=pltpu.CompilerParams(has_side_effects=DATAFLOW),
    )(*srcs, *lands, send_sems, recv_sems, after)
    return list(res[n:])


def _allreduce_small(vec):
    rows = vec.shape[0]

    def body(x_ref, o_ref, buf_ref, send_sems, recv_sems):
        x, y, c = _place()
        me = 4 * x + 2 * y + c
        buf_ref[me] = x_ref[...]
        peers = []
        for mask in range(1, N_DEV):
            p = (_flip(x, mask & 4), _flip(y, mask & 2), _flip(c, mask & 1))
            peers.append((p, 4 * p[0] + 2 * p[1] + p[2]))

        def copy(k, slot):
            return pltpu.make_async_remote_copy(
                src_ref=x_ref, dst_ref=buf_ref.at[slot], send_sem=send_sems.at[k], recv_sem=recv_sems.at[k],
                device_id=peers[k][0], device_id_type=MESH)

        sends = [copy(k, me) for k in range(N_DEV - 1)]
        for cp in sends:
            cp.start()
        for k in range(N_DEV - 1):
            copy(k, peers[k][1]).wait_recv()
        for cp in sends:
            cp.wait_send()
        acc = buf_ref[0]
        for s in range(1, N_DEV):
            acc = acc + buf_ref[s]
        o_ref[...] = acc

    vmem = BS(memory_space=pltpu.VMEM)
    return pl.pallas_call(
        body, name="allreduce_small", in_specs=[vmem], out_specs=vmem, out_shape=SDS((rows, 128), F32),
        scratch_shapes=[pltpu.VMEM((N_DEV, rows, 128), F32), pltpu.SemaphoreType.DMA((7,)),
                        pltpu.SemaphoreType.DMA((7,))],
        compiler_params=pltpu.CompilerParams(has_side_effects=True),
    )(vec)


def _ffn_forward(x, norm_g, wut, wd, tag):
    hn = _norm_fwd(x, norm_g, tag)
    gu, act = _ffn_up(hn, wut, tag)
    out = _mm_nn(act[None], wd[None], f"down_{tag}", res=x, scale=0.5)
    return out, (x, hn, gu, act)


def _ffn_backward(dxo, saved, norm_g, wut, wd, tag, send):
    x, hn, gu, act = saved
    du = _ffn_dact(dxo, wd, gu, tag)
    d_wd = _mm_tn(act[None], dxo, f"dwd_{tag}", scale=0.5, tmm=1408)[0]
    d_wut = _mm_tn(du, hn, f"dwu_{tag}", tmm=1408)
    zero = send([d_wut.reshape(2 * F, D), d_wd])
    dhn = _mm_nn(du, wut, f"dhn_{tag}", tn=512)
    return _norm_bwd(x, norm_g + zero, dhn, dxo, tag)


def _mixer_forward(x, norm_g, w, bias, tables, tag):
    wint, wat, wbt, wo = w
    hn = _norm_fwd(x, norm_g, tag)
    proj = _mm_nt_rows(hn, wint, f"proj_{tag}", 512, IN_W // 2, IN_W, 0)
    qkr = _rope_fwd(proj, *tables)
    outs, lses = [], []
    for grp in range(3):
        o, l = _dil_fwd(qkr, proj, grp)
        outs.append(o)
        lses.append(l)
    ya = _combine_fwd(outs, lses)
    yb, lse_b = _na_fwd(proj, bias)
    merged, za, zb = _merge_fwd(ya, yb, proj, wat, wbt)
    out = _mm_nn(merged[None], wo[None], f"out_{tag}", res=x)
    return out, (x, hn, proj, qkr, outs, lses, ya, yb, lse_b, merged, za, zb)


def _mixer_backward(dxo, saved, norm_g, w, bias, tables, tag, send):
    wint, wat, wbt, wo = w
    x, hn, proj, qkr, outs, lses, ya, yb, lse_b, merged, za, zb = saved
    dm = _mm_nt_rows(dxo, wo, f"dmerged_{tag}", 512, D, D, 0)
    d_wo = _mm_tn(merged[None], dxo, f"dwo_{tag}")[0]
    dza, dzb, dlog = _merge_bwd(dm, za, zb, proj)
    dya = _mm_nn(dza[None], wat[None], f"dya_{tag}")
    dyb = _mm_nn(dzb[None], wbt[None], f"dyb_{tag}")
    d_wat = _mm_tn(dza[None], ya, f"dwa_{tag}")[0]
    d_wbt = _mm_tn(dzb[None], yb, f"dwb_{tag}")[0]
    cb = _combine_bwd(dya, outs, lses)
    dqs, dks, dvs = [], [], []
    for grp in range(3):
        dq, dk, dv = _dil_bwd(qkr, proj, cb[grp], cb[3 + grp], lses[grp], grp)
        dqs.append(dq)
        dks.append(dk)
        dvs.append(dv)
    dqk = _rope_bwd(dqs, dks, *tables)
    dqb, dkb, dvb, dbias_tab = _na_bwd(proj, bias, dyb, yb, lse_b)
    dbias = _na_dbias(dbias_tab)
    dproj = jnp.concatenate(
        [dqk] + [t.astype(BF16) for t in (*dvs, dqb, dkb, dvb)] + [dlog[0], dlog[1]], axis=1)
    d_wint = _mm_tn(dproj[None], hn, f"dwin_{tag}", tmm=2944)[0]
    zero = send([d_wint, d_wat, d_wbt, d_wo])
    dhn = _mm_nn(dproj[None], wint[None], f"dhnm_{tag}", tm=256, tn=512)
    dx, dg = _norm_bwd(x, norm_g + zero, dhn, dxo, f"mix_{tag}")
    dbias = dbias[:, :31, :15].transpose(0, 2, 1)
    return dx, dg, dbias


def _pack_small(norms, biases, final, loss=None):
    parts = []
    for layer in range(DEPTH):
        parts += [norms[0][layer], norms[1][layer], norms[2][layer],
                  jnp.pad(biases[layer].reshape(-1), (0, BIAS_PAD - 8 * 15 * 31))]
    parts.append(final)
    flat = jnp.concatenate([p.reshape(-1).astype(F32) for p in parts])
    if loss is not None:
        flat = jnp.concatenate([flat, loss.reshape(-1)])
    return jnp.pad(flat, (0, SMALL_ROWS * 128 - flat.shape[0])).reshape(SMALL_ROWS, 128)


def _unpack_small(packed):
    flat = packed.reshape(-1)
    norms, biases = ([], [], []), []
    pos = 0
    for _ in range(DEPTH):
        for k in range(3):
            norms[k].append(flat[pos:pos + D])
            pos += D
        biases.append(flat[pos:pos + 8 * 15 * 31].reshape(8, 15, 31))
        pos += BIAS_PAD
    final = flat[pos:pos + D]
    pos += D
    return [jnp.stack(n) for n in norms], jnp.stack(biases), final, flat[pos]


def kernel(x, ffn1_norm, ffn1_w_up, ffn1_w_down, mix_norm, w_in, na_rel_bias, w_branch_a, w_branch_b, w_out, ffn2_norm, ffn2_w_up, ffn2_w_down, final_norm, loss_target, m_ffn1_norm, m_ffn1_w_up, m_ffn1_w_down, m_mix_norm, m_w_in, m_na_rel_bias, m_w_branch_a, m_w_branch_b, m_w_out, m_ffn2_norm, m_ffn2_w_up, m_ffn2_w_down, m_final_norm, v_ffn1_norm, v_ffn1_w_up, v_ffn1_w_down, v_mix_norm, v_w_in, v_na_rel_bias, v_w_branch_a, v_w_branch_b, v_w_out, v_ffn2_norm, v_ffn2_w_up, v_ffn2_w_down, v_final_norm):
    t = x.shape[0] * x.shape[1]
    xs = x.reshape(t, D)
    tgt = loss_target.reshape(t, D)
    tables = _rope_tables()

    col_sharded = dict(up1=ffn1_w_up, win=w_in, wa=w_branch_a, wb=w_branch_b, up2=ffn2_w_up)
    row_sharded = dict(down1=ffn1_w_down, wo=w_out, down2=ffn2_w_down)
    sublayers = (("up1", "down1"), ("win", "wa", "wb", "wo"), ("up2", "down2"))
    shard = [{} for _ in range(DEPTH)]
    for layer in range(DEPTH):
        for name, arr in col_sharded.items():
            shard[layer][name] = arr[layer].T.astype(BF16)
        for name, arr in row_sharded.items():
            shard[layer][name] = arr[layer].astype(BF16)

    weights = [{} for _ in range(DEPTH)]
    first = _allgather([shard[0][n] for n in sublayers[0]], "first")
    weights[0].update(zip(sublayers[0], first))
    pending = {}
    after = first[0]
    for layer in range(DEPTH):
        for k, names in enumerate(sublayers):
            if (layer, k) != (0, 0):
                pending[layer, k], after = _exchange_start(
                    "gather", [shard[layer][n] for n in names], after, f"w{layer}{k}")
    zero = after[0, 0]

    def arrived(layer, k, behind):
        if (layer, k) in pending:
            got = _exchange_wait(pending.pop((layer, k)), behind, f"w{layer}{k}")
            weights[layer].update(zip(sublayers[k], got))
        return weights[layer]

    saved = []
    h = xs
    for layer in range(DEPTH):
        bias = _na_bias_table(na_rel_bias[layer])
        w = arrived(layer, 0, h)
        h, s1 = _ffn_forward(h, ffn1_norm[layer] + zero, w["up1"].reshape(2, F, D), w["down1"], f"f1l{layer}")
        w = arrived(layer, 1, h)
        h, s2 = _mixer_forward(h, mix_norm[layer], (w["win"], w["wa"], w["wb"], w["wo"]), bias, tables, f"l{layer}")
        w = arrived(layer, 2, h)
        h, s3 = _ffn_forward(h, ffn2_norm[layer], w["up2"].reshape(2, F, D), w["down2"], f"f2l{layer}")
        saved.append((s1, s2, s3, bias))
    loss_part, dh, d_final = _loss_head(h, final_norm, tgt)

    d_norms = ([None] * DEPTH, [None] * DEPTH, [None] * DEPTH)
    d_bias = [None] * DEPTH
    sent = {}

    def sender(layer, k):
        def send(grads):
            sent[layer, k], token = _exchange_start("scatter", grads, None, f"g{layer}{k}")
            return token[0, 0]
        return send

    for layer in reversed(range(DEPTH)):
        w = weights[layer]
        s1, s2, s3, bias = saved[layer]
        dh, d_norms[2][layer] = _ffn_backward(
            dh, s3, ffn2_norm[layer], w["up2"].reshape(2, F, D), w["down2"], f"f2l{layer}", sender(layer, 2))
        dh, d_norms[1][layer], d_bias[layer] = _mixer_backward(
            dh, s2, mix_norm[layer], (w["win"], w["wa"], w["wb"], w["wo"]), bias, tables, f"l{layer}", sender(layer, 1))
        dh, d_norms[0][layer] = _ffn_backward(
            dh, s1, ffn1_norm[layer], w["up1"].reshape(2, F, D), w["down1"], f"f1l{layer}", sender(layer, 0))
    grad_x = dh.reshape(x.shape)

    small = _allreduce_small(_pack_small(d_norms, d_bias, d_final, loss_part[0, :1]))
    g_norms, g_bias, g_final, loss = _unpack_small(small)

    originals = dict(up1=(ffn1_w_up, m_ffn1_w_up, v_ffn1_w_up), down1=(ffn1_w_down, m_ffn1_w_down, v_ffn1_w_down),
                     win=(w_in, m_w_in, v_w_in), wa=(w_branch_a, m_w_branch_a, v_w_branch_a),
                     wb=(w_branch_b, m_w_branch_b, v_w_branch_b), wo=(w_out, m_w_out, v_w_out),
                     up2=(ffn2_w_up, m_ffn2_w_up, v_ffn2_w_up), down2=(ffn2_w_down, m_ffn2_w_down, v_ffn2_w_down))
    big = {}
    behind = small
    for k in (2, 1, 0):
        recv = [_exchange_wait(sent[layer, k], behind, f"g{layer}{k}") for layer in (1, 0)]
        for i, name in enumerate(sublayers[k]):
            g = _sum_slots(recv[1][i], recv[0][i], name)
            wv, mv, vv = originals[name]
            if name in col_sharded:
                wv, mv, vv = (jnp.swapaxes(t, 1, 2) for t in (wv, mv, vv))
            big[name] = (g, *_adamw(wv, g, mv, vv, name))
            behind = big[name][1]
            if name in col_sharded:
                big[name] = tuple(jnp.swapaxes(t, 1, 2) for t in big[name])

    w_small = _pack_small((ffn1_norm, mix_norm, ffn2_norm), na_rel_bias, final_norm)
    m_small = _pack_small((m_ffn1_norm, m_mix_norm, m_ffn2_norm), m_na_rel_bias, m_final_norm)
    v_small = _pack_small((v_ffn1_norm, v_mix_norm, v_ffn2_norm), v_na_rel_bias, v_final_norm)
    upd = _adamw(w_small[None], small[None], m_small[None], v_small[None], "small")
    small_out = [(g_norms, g_bias, g_final)] + [_unpack_small(u[0])[:3] for u in upd]

    outputs = [loss, grad_x]
    for kind in range(4):
        norms, bias_k, final_k = small_out[kind]
        outputs += [norms[0], big["up1"][kind], big["down1"][kind], norms[1], big["win"][kind], bias_k,
                    big["wa"][kind], big["wb"][kind], big["wo"][kind], norms[2], big["up2"][kind],
                    big["down2"][kind], final_k]
    return tuple(outputs)
```

```python
import numpy as np

import jax
import jax.numpy as jnp
from jax import lax
from jax.experimental import pallas as pl
from jax.experimental.pallas import tpu as pltpu

F32 = jnp.float32
BF16 = jnp.bfloat16
SDS = jax.ShapeDtypeStruct
BS = pl.BlockSpec
MESH = pl.DeviceIdType.MESH

D = 1024
S = 2048
F = 2816
DEPTH = 2
HEAD_DIM = 64
DILATIONS = (1, 4, 16)
HALF = 64
QKV_A = 2304
QKV_B = 1536
IN_W = 5888
N_DEV = 8
NA_ROWS = 32
GRID_W = 64
NA_KR = 8
ROPE_THETA = 10000.0
RMS_EPS = 1e-6
NEG = -1e30
SCALE = HEAD_DIM ** -0.5
ADAM_LR, ADAM_B1, ADAM_B2, ADAM_EPS, ADAM_WD, ADAM_STEP = 0.001, 0.9, 0.999, 1e-08, 0.01, 10
VMEM_LIMIT_V7X = 52 * 1024 * 1024
SMALL_ROWS = 120
BIAS_PAD = 3840
NA_FWD_ROWS = 4
NA_BWD_ROWS = 2
DIL_FWD_TILES = 4
DIL_BWD_TILES = 2


def _cp(*sem):
    return pltpu.CompilerParams(dimension_semantics=sem, vmem_limit_bytes=VMEM_LIMIT_V7X)


def _dot_nn(a, b):
    return jnp.dot(a, b, preferred_element_type=F32)


def _dot_nt(a, b):
    return lax.dot_general(a, b, (((1,), (1,)), ((), ())), preferred_element_type=F32)


def _dot_tn(a, b):
    return lax.dot_general(a, b, (((0,), (0,)), ((), ())), preferred_element_type=F32)


def _ds(start, size, stride):
    return pl.ds(start, size) if stride == 1 else pl.ds(start, size, stride=stride)


def _norm_fwd(x, g, tag):
    t = x.shape[0]
    tm = 512

    def body(x_ref, g_ref, o_ref):
        xv = x_ref[...]
        r = lax.rsqrt(jnp.mean(xv * xv, axis=-1, keepdims=True) + RMS_EPS)
        o_ref[...] = (xv * r * g_ref[...]).astype(BF16)

    return pl.pallas_call(
        body, name=f"norm_fwd_{tag}", grid=(t // tm,),
        in_specs=[BS((tm, D), lambda i: (i, 0)), BS((1, D), lambda i: (0, 0))],
        out_specs=BS((tm, D), lambda i: (i, 0)),
        out_shape=SDS((t, D), BF16), compiler_params=_cp("parallel"),
    )(x, g.reshape(1, D))


def _norm_bwd(x, g, dh, dres, tag):
    t = x.shape[0]
    tm = 512

    def body(x_ref, g_ref, dh_ref, dr_ref, dx_ref, dg_ref):
        @pl.when(pl.program_id(0) == 0)
        def _():
            dg_ref[...] = jnp.zeros_like(dg_ref)

        xv = x_ref[...]
        r = lax.rsqrt(jnp.mean(xv * xv, axis=-1, keepdims=True) + RMS_EPS)
        xh = xv * r
        dh = dh_ref[...]
        u = dh * g_ref[...]
        dx_ref[...] = dr_ref[...] + r * (u - xh * jnp.mean(xh * u, axis=-1, keepdims=True))
        dg_ref[...] += jnp.sum(dh * xh, axis=0, keepdims=True)

    row = BS((tm, D), lambda i: (i, 0))
    vec = BS((1, D), lambda i: (0, 0))
    return pl.pallas_call(
        body, name=f"norm_bwd_{tag}", grid=(t // tm,),
        in_specs=[row, vec, row, row], out_specs=[row, vec],
        out_shape=[SDS((t, D), F32), SDS((1, D), F32)], compiler_params=_cp("arbitrary"),
    )(x, g.reshape(1, D), dh, dres)


def _loss_head(x, g, tgt):
    t = x.shape[0]
    tm = 512

    def body(x_ref, g_ref, t_ref, loss_ref, dx_ref, dg_ref):
        @pl.when(pl.program_id(0) == 0)
        def _():
            dg_ref[...] = jnp.zeros_like(dg_ref)
            loss_ref[...] = jnp.zeros_like(loss_ref)

        xv = x_ref[...]
        gv = g_ref[...]
        r = lax.rsqrt(jnp.mean(xv * xv, axis=-1, keepdims=True) + RMS_EPS)
        xh = xv * r
        e = xh * gv - t_ref[...]
        loss_ref[...] += 0.5 * jnp.sum(jnp.mean(e * e, axis=-1, keepdims=True), axis=0, keepdims=True)
        dy = e * (1.0 / D)
        u = dy * gv
        dx_ref[...] = r * (u - xh * jnp.mean(xh * u, axis=-1, keepdims=True))
        dg_ref[...] += jnp.sum(dy * xh, axis=0, keepdims=True)

    row = BS((tm, D), lambda i: (i, 0))
    vec = BS((1, D), lambda i: (0, 0))
    return pl.pallas_call(
        body, name="loss_head", grid=(t // tm,),
        in_specs=[row, vec, row], out_specs=[BS((1, 128), lambda i: (0, 0)), row, vec],
        out_shape=[SDS((1, 128), F32), SDS((t, D), F32), SDS((1, D), F32)],
        compiler_params=_cp("arbitrary"),
    )(x, g.reshape(1, D), tgt)


def _mm_nn(a, w, tag, res=None, scale=1.0, tm=512, tn=None):
    c_n, t, k = a.shape
    n = w.shape[2]
    tn = n if tn is None else tn

    def body(*refs):
        a_ref, w_ref = refs[0], refs[1]
        o_ref = refs[-1]
        acc = _dot_nn(a_ref[0].astype(BF16), w_ref[0])
        for c in range(1, c_n):
            acc = acc + _dot_nn(a_ref[c].astype(BF16), w_ref[c])
        if scale != 1.0:
            acc = acc * scale
        if res is not None:
            acc = refs[2][...] + acc
        o_ref[...] = acc

    in_specs = [BS((c_n, tm, k), lambda i, j: (0, i, 0)), BS((c_n, k, tn), lambda i, j: (0, 0, j))]
    args = [a, w]
    if res is not None:
        in_specs.append(BS((tm, tn), lambda i, j: (i, j)))
        args.append(res)
    return pl.pallas_call(
        body, name=f"mm_nn_{tag}", grid=(t // tm, n // tn), in_specs=in_specs,
        out_specs=BS((tm, tn), lambda i, j: (i, j)), out_shape=SDS((t, n), F32),
        compiler_params=_cp("parallel", "parallel"),
    )(*args)


def _mm_nt_rows(a, w, tag, tm, tn, n_total, w_row0):
    t, k = a.shape
    assert w_row0 % tn == 0 and n_total % tn == 0
    j0 = w_row0 // tn

    def body(a_ref, w_ref, o_ref):
        o_ref[...] = _dot_nt(a_ref[...].astype(BF16), w_ref[...])

    return pl.pallas_call(
        body, name=f"mm_nt_{tag}", grid=(t // tm, n_total // tn),
        in_specs=[BS((tm, k), lambda i, j: (i, 0)), BS((tn, k), lambda i, j: (j0 + j, 0))],
        out_specs=BS((tm, tn), lambda i, j: (i, j)), out_shape=SDS((t, n_total), F32),
        compiler_params=_cp("parallel", "parallel"),
    )(a, w)


def _mm_tn(a, b, tag, scale=1.0, tmm=None, tk=512):
    c_n, t, m = a.shape
    n = b.shape[1]
    tmm = m if tmm is None else tmm
    nk = t // tk

    def body(a_ref, b_ref, o_ref, acc_ref):
        kk = pl.program_id(2)

        @pl.when(kk == 0)
        def _():
            acc_ref[...] = jnp.zeros_like(acc_ref)

        acc_ref[...] += _dot_tn(a_ref[...].astype(BF16), b_ref[...].astype(BF16))

        @pl.when(kk == nk - 1)
        def _():
            o_ref[...] = (acc_ref[...] * scale).astype(BF16)

    return pl.pallas_call(
        body, name=f"mm_tn_{tag}", grid=(c_n, m // tmm, nk),
        in_specs=[BS((None, tk, tmm), lambda c, mi, kk: (c, kk, mi)), BS((tk, n), lambda c, mi, kk: (kk, 0))],
        out_specs=BS((None, tmm, n), lambda c, mi, kk: (c, mi, 0)),
        out_shape=SDS((c_n, m, n), BF16), scratch_shapes=[pltpu.VMEM((tmm, n), F32)],
        compiler_params=_cp("parallel", "parallel", "arbitrary"),
    )(a, b)


def _ffn_up(hn, wut, tag):
    t = hn.shape[0]
    tm, tn = 512, 1408

    def body(h_ref, w_ref, gu_ref, act_ref):
        h = h_ref[...]
        g = _dot_nt(h, w_ref[0])
        u = _dot_nt(h, w_ref[1])
        gu_ref[0] = g.astype(BF16)
        gu_ref[1] = u.astype(BF16)
        act_ref[...] = (g * jax.nn.sigmoid(g) * u).astype(BF16)

    return pl.pallas_call(
        body, name=f"ffn_up_{tag}", grid=(t // tm, F // tn),
        in_specs=[BS((tm, D), lambda i, j: (i, 0)), BS((2, tn, D), lambda i, j: (0, j, 0))],
        out_specs=[BS((2, tm, tn), lambda i, j: (0, i, j)), BS((tm, tn), lambda i, j: (i, j))],
        out_shape=[SDS((2, t, F), BF16), SDS((t, F), BF16)],
        compiler_params=_cp("parallel", "parallel"),
    )(hn, wut)


def _ffn_dact(dxo, wd, gu, tag):
    t = dxo.shape[0]
    tm, tn = 512, 1408

    def body(d_ref, w_ref, gu_ref, o_ref):
        dact = _dot_nt(d_ref[...].astype(BF16), w_ref[...]) * 0.5
        g = gu_ref[0].astype(F32)
        u = gu_ref[1].astype(F32)
        sg = jax.nn.sigmoid(g)
        o_ref[0] = (dact * u * (sg * (1.0 + g * (1.0 - sg)))).astype(BF16)
        o_ref[1] = (dact * (g * sg)).astype(BF16)

    return pl.pallas_call(
        body, name=f"ffn_dact_{tag}", grid=(t // tm, F // tn),
        in_specs=[BS((tm, D), lambda i, j: (i, 0)), BS((tn, D), lambda i, j: (j, 0)),
                  BS((2, tm, tn), lambda i, j: (0, i, j))],
        out_specs=BS((2, tm, tn), lambda i, j: (0, i, j)),
        out_shape=SDS((2, t, F), BF16), compiler_params=_cp("parallel", "parallel"),
    )(dxo, wd, gu)


def _rope_tables():
    half = HEAD_DIM // 2
    inv_freq = ROPE_THETA ** (-jnp.arange(half, dtype=F32) / half)
    ang = jnp.arange(S).astype(F32)[:, None] * inv_freq[None, :]
    cos, sin = jnp.cos(ang), jnp.sin(ang)
    return jnp.concatenate([cos, cos, cos, cos], axis=1), jnp.concatenate([-sin, sin, -sin, sin], axis=1)


def _swap_halves(t, first_half):
    return jnp.where(first_half, pltpu.roll(t, 96, 1), pltpu.roll(t, 32, 1))


def _rope_fwd(proj, cos_t, sin_t):
    t = proj.shape[0]
    tm = 512
    width = 2 * QKV_A // 3

    def body(x_ref, c_ref, s_ref, o_ref):
        c = c_ref[...]
        sg = s_ref[...]
        first = (lax.broadcasted_iota(jnp.int32, (tm, 128), 1) % HEAD_DIM) < HEAD_DIM // 2
        for j in range(width // 128):
            v = x_ref[:, 128 * j:128 * (j + 1)]
            o_ref[:, 128 * j:128 * (j + 1)] = v * c + _swap_halves(v, first) * sg

    tab = BS((tm, 128), lambda i: (i % (S // tm), 0))
    return pl.pallas_call(
        body, name="rope_fwd", grid=(t // tm,),
        in_specs=[BS((tm, width), lambda i: (i, 0)), tab, tab],
        out_specs=BS((tm, width), lambda i: (i, 0)), out_shape=SDS((t, width), F32),
        compiler_params=_cp("parallel"),
    )(proj, cos_t, sin_t)


def _rope_bwd(dqs, dks, cos_t, sin_t):
    t = dqs[0].shape[0]
    tm = 512

    def body(*refs):
        c = refs[6][...]
        sg = refs[7][...]
        o_ref = refs[8]
        first = (lax.broadcasted_iota(jnp.int32, (tm, 128), 1) % HEAD_DIM) < HEAD_DIM // 2
        for a in range(6):
            for hp in range(2):
                v = refs[a][:, 128 * hp:128 * (hp + 1)]
                col = 128 * (2 * a + hp)
                o_ref[:, col:col + 128] = (v * c + _swap_halves(v * sg, first)).astype(BF16)

    blk = BS((tm, 256), lambda i: (i, 0))
    tab = BS((tm, 128), lambda i: (i % (S // tm), 0))
    return pl.pallas_call(
        body, name="rope_bwd", grid=(t // tm,), in_specs=[blk] * 6 + [tab, tab],
        out_specs=BS((tm, 1536), lambda i: (i, 0)), out_shape=SDS((t, 1536), BF16),
        compiler_params=_cp("parallel"),
    )(*dqs, *dks, cos_t, sin_t)


def _head_masks():
    lane = lax.broadcasted_iota(jnp.int32, (1, 128), 1)
    m0 = (lane < HEAD_DIM).astype(F32)
    return m0, 1.0 - m0


def _dil_geometry(d):
    sub = S // d
    q_rows = 128
    k_rows = min(256, sub)
    return sub, q_rows, sub // q_rows, k_rows


def _dil_tile(idx, d):
    sub, q_rows, nb, k_rows = _dil_geometry(d)
    r = idx // nb
    n = idx % nb
    k_sub = jnp.clip(q_rows * n - HALF, 0, sub - k_rows)
    if d == 1:
        q_start = pl.multiple_of(q_rows * n, q_rows)
        k_start = pl.multiple_of(k_sub, HALF)
    else:
        q_start = q_rows * n * d + r
        k_start = k_sub * d + r
    ii = lax.broadcasted_iota(jnp.int32, (q_rows, k_rows), 0)
    jj = lax.broadcasted_iota(jnp.int32, (q_rows, k_rows), 1)
    valid = jnp.abs(jj - ii + (k_sub - q_rows * n)) <= HALF
    return q_start, k_start, valid


def _dil_specs(grp):
    qs = BS((S, 128), lambda b, hp: (b, 2 * grp + hp))
    ks = BS((S, 128), lambda b, hp: (b, 6 + 2 * grp + hp))
    vs = BS((S, 128), lambda b, hp: (b, 12 + 2 * grp + hp))
    own = BS((S, 128), lambda b, hp: (b, hp))
    return qs, ks, vs, own


def _dil_fwd(qkr, proj, grp):
    t = qkr.shape[0]
    d = DILATIONS[grp]
    _, q_rows, nb, k_rows = _dil_geometry(d)

    def body(q_ref, k_ref, v_ref, o_ref, l_ref):
        masks = _head_masks()

        def step(i0, carry):
            geo = [_dil_tile(i0 * DIL_FWD_TILES + j, d) for j in range(DIL_FWD_TILES)]
            tiles = [(j, h) for j in range(DIL_FWD_TILES) for h in range(2)]
            qs = [q_ref[_ds(g[0], q_rows, d), :] for g in geo]
            kbs = [k_ref[_ds(g[1], k_rows, d), :].astype(BF16) for g in geo]
            ss = [jnp.where(geo[j][2], _dot_nt((qs[j] * masks[h]).astype(BF16), kbs[j]) * SCALE, NEG) for j, h in tiles]
            mxs = [jnp.max(s, axis=1, keepdims=True) for s in ss]
            ps = [jnp.exp(s - mx) for s, mx in zip(ss, mxs)]
            dens = [jnp.sum(p, axis=1, keepdims=True) for p in ps]
            vs = [v_ref[_ds(g[1], k_rows, d), :] for g in geo]
            outs = [_dot_nn(p.astype(BF16), (vs[j] * masks[h]).astype(BF16)) / den
                    for p, den, (j, h) in zip(ps, dens, tiles)]
            for j, g in enumerate(geo):
                o_ref[_ds(g[0], q_rows, d), :] = outs[2 * j] + outs[2 * j + 1]
                l_ref[_ds(g[0], q_rows, d), :] = (
                    (mxs[2 * j] + jnp.log(dens[2 * j])) * masks[0] + (mxs[2 * j + 1] + jnp.log(dens[2 * j + 1])) * masks[1])
            return carry

        lax.fori_loop(0, d * nb // DIL_FWD_TILES, step, 0)

    qs, ks, vs, own = _dil_specs(grp)
    return pl.pallas_call(
        body, name=f"dil_fwd_{grp}", grid=(t // S, 2), in_specs=[qs, ks, vs], out_specs=[own, own],
        out_shape=[SDS((t, 256), F32), SDS((t, 256), F32)], compiler_params=_cp("parallel", "parallel"),
    )(qkr, qkr, proj)


def _dil_bwd(qkr, proj, do, dlp, lse, grp):
    t = qkr.shape[0]
    d = DILATIONS[grp]
    _, q_rows, nb, k_rows = _dil_geometry(d)

    def body(q_ref, k_ref, v_ref, do_ref, dl_ref, l_ref, dq_ref, dk_ref, dv_ref):
        masks = _head_masks()
        dk_ref[...] = jnp.zeros_like(dk_ref)
        dv_ref[...] = jnp.zeros_like(dv_ref)

        def step(i0, carry):
            geo = [_dil_tile(i0 * DIL_BWD_TILES + j, d) for j in range(DIL_BWD_TILES)]
            tiles = [(j, h) for j in range(DIL_BWD_TILES) for h in range(2)]
            q_ds = [_ds(g[0], q_rows, d) for g in geo]
            k_ds = [_ds(g[1], k_rows, d) for g in geo]
            qs = [q_ref[r, :] for r in q_ds]
            ks = [k_ref[r, :] for r in k_ds]
            kbs = [k.astype(BF16) for k in ks]
            vbs = [v_ref[r, :].astype(BF16) for r in k_ds]
            dos = [do_ref[r, :] for r in q_ds]
            dls = [dl_ref[r, :] for r in q_ds]
            lss = [l_ref[r, :] for r in q_ds]
            qhs = [(qs[j] * masks[h]).astype(BF16) for j, h in tiles]
            dohs = [(dos[j] * masks[h]).astype(BF16) for j, h in tiles]
            ss = [jnp.where(geo[j][2], _dot_nt(qh, kbs[j]) * SCALE, NEG) for qh, (j, h) in zip(qhs, tiles)]
            ps = [jnp.exp(s - lss[j][:, HEAD_DIM * h:HEAD_DIM * h + 1]) for s, (j, h) in zip(ss, tiles)]
            dps = [_dot_nt(doh, vbs[j]) for doh, (j, h) in zip(dohs, tiles)]
            dss = [(p * (dp - dls[j][:, HEAD_DIM * h:HEAD_DIM * h + 1])).astype(BF16)
                   for p, dp, (j, h) in zip(ps, dps, tiles)]
            dqs = [_dot_nn(ds, (ks[j] * masks[h]).astype(BF16)) for ds, (j, h) in zip(dss, tiles)]
            dkws = [_dot_tn(ds, qh) for ds, qh in zip(dss, qhs)]
            dvws = [_dot_tn(p.astype(BF16), doh) for p, doh in zip(ps, dohs)]
            for j in range(DIL_BWD_TILES):
                dq_ref[q_ds[j], :] = (dqs[2 * j] + dqs[2 * j + 1]) * SCALE
                dk_ref[k_ds[j], :] += (dkws[2 * j] + dkws[2 * j + 1]) * SCALE
                dv_ref[k_ds[j], :] += dvws[2 * j] + dvws[2 * j + 1]
            return carry

        lax.fori_loop(0, d * nb // DIL_BWD_TILES, step, 0)

    qs, ks, vs, own = _dil_specs(grp)
    return pl.pallas_call(
        body, name=f"dil_bwd_{grp}", grid=(t // S, 2), in_specs=[qs, ks, vs, own, own, own],
        out_specs=[own, own, own], out_shape=[SDS((t, 256), F32)] * 3,
        compiler_params=_cp("parallel", "parallel"),
    )(qkr, qkr, proj, do, dlp, lse)


def _mix_weights(l0, l1, l2):
    mx = jnp.maximum(jnp.maximum(l0, l1), l2)
    e0, e1, e2 = jnp.exp(l0 - mx), jnp.exp(l1 - mx), jnp.exp(l2 - mx)
    den = e0 + e1 + e2
    return e0 / den, e1 / den, e2 / den


def _combine_fwd(outs, lses):
    t = outs[0].shape[0]
    tm = 512

    def body(o0, o1, o2, l0, l1, l2, y_ref):
        w0, w1, w2 = _mix_weights(l0[...], l1[...], l2[...])
        y_ref[...] = w0 * o0[...] + w1 * o1[...] + w2 * o2[...]

    blk = BS((tm, 256), lambda i: (i, 0))
    return pl.pallas_call(
        body, name="combine_fwd", grid=(t // tm,), in_specs=[blk] * 6, out_specs=blk,
        out_shape=SDS((t, 256), F32), compiler_params=_cp("parallel"),
    )(*outs, *lses)


def _head_sum(x):
    a = lax.broadcasted_iota(jnp.int32, (256, 256), 0) // HEAD_DIM
    b = lax.broadcasted_iota(jnp.int32, (256, 256), 1) // HEAD_DIM
    ones = (a == b).astype(BF16)
    hi = x.astype(BF16)
    lo = (x - hi.astype(F32)).astype(BF16)
    return _dot_nn(hi, ones) + _dot_nn(lo, ones)


def _combine_bwd(dya, outs, lses):
    t = dya.shape[0]
    tm = 512

    def body(dy_ref, o0, o1, o2, l0, l1, l2, d0, d1, d2, e0, e1, e2):
        ws = _mix_weights(l0[...], l1[...], l2[...])
        dy = dy_ref[...]
        ya = ws[0] * o0[...] + ws[1] * o1[...] + ws[2] * o2[...]
        hs = _head_sum(dy * ya)
        for w, d_ref, e_ref in zip(ws, (d0, d1, d2), (e0, e1, e2)):
            d_ref[...] = w * dy
            e_ref[...] = w * hs

    blk = BS((tm, 256), lambda i: (i, 0))
    return pl.pallas_call(
        body, name="combine_bwd", grid=(t // tm,), in_specs=[blk] * 7, out_specs=[blk] * 6,
        out_shape=[SDS((t, 256), F32)] * 6, compiler_params=_cp("parallel"),
    )(dya, *outs, *lses)


def _na_bias_table(rel_bias):
    qc = np.arange(GRID_W)[:, None]
    kc = np.arange(GRID_W)[None, :]
    win_lo = np.clip(qc - 8, 0, GRID_W - 16)
    col_valid = (kc >= win_lo) & (kc < win_lo + 16)
    col_idx = np.clip(kc - qc + 15, 0, 30)
    row_idx = np.arange(NA_KR)[:, None] + np.arange(NA_KR)[None, :]
    rows = (row_idx[..., None] == np.arange(2 * NA_KR - 1)).astype(np.float32)
    cols = (col_idx[..., None] == np.arange(31)).astype(np.float32)
    b = jnp.einsum("hrd,ckr,qjd->hcqkj", rel_bias.astype(F32), rows, cols, precision=lax.Precision.HIGHEST)
    b = jnp.where(col_valid[None, None, :, None, :], b, NEG)
    return b.reshape(8, NA_KR, GRID_W, NA_KR * GRID_W)


def _na_row(i):
    lo = jnp.clip(i - NA_KR // 2, 0, NA_ROWS - NA_KR)
    return pl.multiple_of(GRID_W * i, GRID_W), pl.multiple_of(GRID_W * lo, GRID_W), lo - i + NA_KR - 1


def _na_fwd(proj, bias):
    t = proj.shape[0]
    kw = NA_KR * GRID_W

    def body(q_ref, k_ref, v_ref, b_ref, o_ref, l_ref):
        masks = _head_masks()

        def step(i0, carry):
            rows = [_na_row(i0 * NA_FWD_ROWS + j) for j in range(NA_FWD_ROWS)]
            qs = [q_ref[pl.ds(q_start, GRID_W), :] for q_start, _, _ in rows]
            kbs = [k_ref[pl.ds(k_start, kw), :].astype(BF16) for _, k_start, _ in rows]
            tiles = [(j, h) for j in range(NA_FWD_ROWS) for h in range(2)]
            ss = [_dot_nt((qs[j] * masks[h]).astype(BF16), kbs[j]) * SCALE + b_ref[h, rows[j][2]] for j, h in tiles]
            mxs = [jnp.max(s, axis=1, keepdims=True) for s in ss]
            ps = [jnp.exp(s - mx) for s, mx in zip(ss, mxs)]
            dens = [jnp.sum(p, axis=1, keepdims=True) for p in ps]
            pbs = [(p / den).astype(BF16) for p, den in zip(ps, dens)]
            vs = [v_ref[pl.ds(k_start, kw), :] for _, k_start, _ in rows]
            outs = [_dot_nn(pb, (vs[j] * masks[h]).astype(BF16)) for pb, (j, h) in zip(pbs, tiles)]
            for j, (q_start, _, _) in enumerate(rows):
                o_ref[pl.ds(q_start, GRID_W), :] = outs[2 * j] + outs[2 * j + 1]
                l_ref[pl.ds(q_start, GRID_W), :] = (
                    (mxs[2 * j] + jnp.log(dens[2 * j])) * masks[0] + (mxs[2 * j + 1] + jnp.log(dens[2 * j + 1])) * masks[1])
            return carry

        lax.fori_loop(0, NA_ROWS // NA_FWD_ROWS, step, 0)

    c0 = QKV_A // 128
    own = BS((S, 128), lambda b, hp: (b, hp))
    return pl.pallas_call(
        body, name="na_fwd", grid=(t // S, 4),
        in_specs=[BS((S, 128), lambda b, hp: (b, c0 + hp)), BS((S, 128), lambda b, hp: (b, c0 + 4 + hp)),
                  BS((S, 128), lambda b, hp: (b, c0 + 8 + hp)),
                  BS((2, NA_KR, GRID_W, kw), lambda b, hp: (hp, 0, 0, 0))],
        out_specs=[own, own], out_shape=[SDS((t, 512), F32), SDS((t, 512), F32)],
        compiler_params=_cp("parallel", "parallel"),
    )(proj, proj, proj, bias)


def _na_bwd(proj, bias, dyb, yb, lse):
    t = proj.shape[0]
    kw = NA_KR * GRID_W

    def body(q_ref, k_ref, v_ref, b_ref, do_ref, o_ref, l_ref, dq_ref, dk_ref, dv_ref, db_ref):
        masks = _head_masks()

        @pl.when(pl.program_id(1) == 0)
        def _():
            db_ref[...] = jnp.zeros_like(db_ref)

        dk_ref[...] = jnp.zeros_like(dk_ref)
        dv_ref[...] = jnp.zeros_like(dv_ref)

        def step(i0, carry):
            rows = [_na_row(i0 * NA_BWD_ROWS + j) for j in range(NA_BWD_ROWS)]
            tiles = [(j, h) for j in range(NA_BWD_ROWS) for h in range(2)]
            q_ds = [pl.ds(r[0], GRID_W) for r in rows]
            k_ds = [pl.ds(r[1], kw) for r in rows]
            qs = [q_ref[r, :] for r in q_ds]
            ks = [k_ref[r, :] for r in k_ds]
            kbs = [k.astype(BF16) for k in ks]
            vbs = [v_ref[r, :].astype(BF16) for r in k_ds]
            dos = [do_ref[r, :] for r in q_ds]
            os_ = [o_ref[r, :] for r in q_ds]
            lss = [l_ref[r, :] for r in q_ds]
            qhs = [(qs[j] * masks[h]).astype(BF16) for j, h in tiles]
            dohs = [(dos[j] * masks[h]).astype(BF16) for j, h in tiles]
            deltas = [jnp.sum(dos[j] * os_[j] * masks[h], axis=1, keepdims=True) for j, h in tiles]
            ss = [_dot_nt(qh, kbs[j]) * SCALE + b_ref[h, rows[j][2]] for qh, (j, h) in zip(qhs, tiles)]
            ps = [jnp.exp(s - lss[j][:, HEAD_DIM * h:HEAD_DIM * h + 1]) for s, (j, h) in zip(ss, tiles)]
            dps = [_dot_nt(doh, vbs[j]) for doh, (j, h) in zip(dohs, tiles)]
            dss = [p * (dp - delta) for p, dp, delta in zip(ps, dps, deltas)]
            for ds, (j, h) in zip(dss, tiles):
                db_ref[h, rows[j][2]] += ds
            dsbs = [ds.astype(BF16) for ds in dss]
            dqs = [_dot_nn(dsb, (ks[j] * masks[h]).astype(BF16)) for dsb, (j, h) in zip(dsbs, tiles)]
            dkws = [_dot_tn(dsb, qh) for dsb, qh in zip(dsbs, qhs)]
            dvws = [_dot_tn(p.astype(BF16), doh) for p, doh in zip(ps, dohs)]
            for j in range(NA_BWD_ROWS):
                dq_ref[q_ds[j], :] = (dqs[2 * j] + dqs[2 * j + 1]) * SCALE
                dk_ref[k_ds[j], :] += (dkws[2 * j] + dkws[2 * j + 1]) * SCALE
                dv_ref[k_ds[j], :] += dvws[2 * j] + dvws[2 * j + 1]
            return carry

        lax.fori_loop(0, NA_ROWS // NA_BWD_ROWS, step, 0)

    c0 = QKV_A // 128
    own = BS((S, 128), lambda hp, b: (b, hp))
    tab = BS((2, NA_KR, GRID_W, kw), lambda hp, b: (hp, 0, 0, 0))
    return pl.pallas_call(
        body, name="na_bwd", grid=(4, t // S),
        in_specs=[BS((S, 128), lambda hp, b: (b, c0 + hp)), BS((S, 128), lambda hp, b: (b, c0 + 4 + hp)),
                  BS((S, 128), lambda hp, b: (b, c0 + 8 + hp)), tab, own, own, own],
        out_specs=[own, own, own, tab],
        out_shape=[SDS((t, 512), F32)] * 3 + [SDS((8, NA_KR, GRID_W, kw), F32)],
        compiler_params=_cp("parallel", "arbitrary"),
    )(proj, proj, proj, bias, dyb, yb, lse)


def _na_dbias(db):
    kw = NA_KR * GRID_W

    def body(x_ref, o_ref, z_ref):
        lane = lax.broadcasted_iota(jnp.int32, (GRID_W, kw), 1)
        sub = lax.broadcasted_iota(jnp.int32, (GRID_W, kw), 0)
        rel = (lane % GRID_W) - sub + 15
        key_row = lax.broadcasted_iota(jnp.int32, (kw, 128), 0) // GRID_W
        dr = lax.broadcasted_iota(jnp.int32, (kw, 128), 1)
        acc = jnp.zeros((32, 128), F32)
        z_ref[...] = jnp.zeros_like(z_ref)
        for cls in range(NA_KR):
            xv = x_ref[cls]
            for dc in range(31):
                z_ref[dc:dc + 1, :] = jnp.sum(jnp.where(rel == dc, xv, 0.0), axis=0, keepdims=True)
            z = z_ref[...]
            ind = (key_row + cls == dr).astype(BF16)
            hi = z.astype(BF16)
            lo = (z - hi.astype(F32)).astype(BF16)
            acc = acc + _dot_nn(hi, ind) + _dot_nn(lo, ind)
        o_ref[...] = acc

    return pl.pallas_call(
        body, name="na_dbias", grid=(8,),
        in_specs=[BS((None, NA_KR, GRID_W, kw), lambda h: (h, 0, 0, 0))],
        out_specs=BS((None, 32, 128), lambda h: (h, 0, 0)), out_shape=SDS((8, 32, 128), F32),
        scratch_shapes=[pltpu.VMEM((32, kw), F32)], compiler_params=_cp("parallel"),
    )(db)


def _merge_fwd(ya, yb, proj, wat, wbt):
    t = ya.shape[0]
    tm, tn = 512, 256
    ca = (QKV_A + QKV_B) // tn
    cb = ca + D // tn

    def body(ya_ref, yb_ref, la_ref, lb_ref, wa_ref, wb_ref, m_ref, za_ref, zb_ref):
        za = _dot_nt(ya_ref[...].astype(BF16), wa_ref[...])
        zb = _dot_nt(yb_ref[...].astype(BF16), wb_ref[...])
        m_ref[...] = (jax.nn.sigmoid(la_ref[...]) * za + jax.nn.sigmoid(lb_ref[...]) * zb).astype(BF16)
        za_ref[...] = za.astype(BF16)
        zb_ref[...] = zb.astype(BF16)

    out = BS((tm, tn), lambda i, j: (i, j))
    return pl.pallas_call(
        body, name="merge_fwd", grid=(t // tm, D // tn),
        in_specs=[BS((tm, 256), lambda i, j: (i, 0)), BS((tm, 512), lambda i, j: (i, 0)),
                  BS((tm, tn), lambda i, j: (i, ca + j)), BS((tm, tn), lambda i, j: (i, cb + j)),
                  BS((tn, 256), lambda i, j: (j, 0)), BS((tn, 512), lambda i, j: (j, 0))],
        out_specs=[out, out, out], out_shape=[SDS((t, D), BF16)] * 3,
        compiler_params=_cp("parallel", "parallel"),
    )(ya, yb, proj, proj, wat, wbt)


def _merge_bwd(dm, za, zb, proj):
    t = dm.shape[0]
    tm, tn = 512, 256
    ca = (QKV_A + QKV_B) // tn
    cb = ca + D // tn

    def body(dm_ref, za_ref, zb_ref, la_ref, lb_ref, dza_ref, dzb_ref, dl_ref):
        dmv = dm_ref[...]
        ga = jax.nn.sigmoid(la_ref[...])
        gb = jax.nn.sigmoid(lb_ref[...])
        dza_ref[...] = (dmv * ga).astype(BF16)
        dzb_ref[...] = (dmv * gb).astype(BF16)
        dl_ref[0] = (dmv * za_ref[...].astype(F32) * ga * (1.0 - ga)).astype(BF16)
        dl_ref[1] = (dmv * zb_ref[...].astype(F32) * gb * (1.0 - gb)).astype(BF16)

    blk = BS((tm, tn), lambda i, j: (i, j))
    return pl.pallas_call(
        body, name="merge_bwd", grid=(t // tm, D // tn),
        in_specs=[blk, blk, blk, BS((tm, tn), lambda i, j: (i, ca + j)), BS((tm, tn), lambda i, j: (i, cb + j))],
        out_specs=[blk, blk, BS((2, tm, tn), lambda i, j: (0, i, j))],
        out_shape=[SDS((t, D), BF16), SDS((t, D), BF16), SDS((2, t, D), BF16)],
        compiler_params=_cp("parallel", "parallel"),
    )(dm, za, zb, proj, proj)


def _sum_slots(recv0, recv1, tag):
    _, r, c = recv0.shape
    tr = r if r * c <= 512 * 1024 else r // 2

    def body(a_ref, b_ref, o_ref):
        for layer, ref in enumerate((a_ref, b_ref)):
            acc = ref[0].astype(F32)
            for s in range(1, N_DEV):
                acc = acc + ref[s].astype(F32)
            o_ref[layer] = acc

    blk = BS((N_DEV, tr, c), lambda i: (0, i, 0))
    return pl.pallas_call(
        body, name=f"sum_slots_{tag}", grid=(r // tr,), in_specs=[blk, blk],
        out_specs=BS((2, tr, c), lambda i: (0, i, 0)), out_shape=SDS((2, r, c), F32),
        compiler_params=_cp("parallel"),
    )(recv0, recv1)


def _adamw(w, g, m, v, tag):
    layers, r, c = w.shape
    tr = next(r // k for k in (1, 2, 4, 8) if r // k <= 384 and r % (8 * k) == 0)

    def body(w_ref, g_ref, m_ref, v_ref, d_ref, mo_ref, vo_ref):
        gv = g_ref[...]
        mn = ADAM_B1 * m_ref[...] + (1.0 - ADAM_B1) * gv
        vn = ADAM_B2 * v_ref[...] + (1.0 - ADAM_B2) * (gv * gv)
        m_hat = mn / (1.0 - ADAM_B1 ** ADAM_STEP)
        v_hat = vn / (1.0 - ADAM_B2 ** ADAM_STEP)
        d_ref[...] = -ADAM_LR * (m_hat / (jnp.sqrt(v_hat) + ADAM_EPS) + ADAM_WD * w_ref[...])
        mo_ref[...] = mn
        vo_ref[...] = vn

    blk = BS((None, tr, c), lambda l, i: (l, i, 0))
    return pl.pallas_call(
        body, name=f"adamw_{tag}", grid=(layers, r // tr), in_specs=[blk] * 4, out_specs=[blk] * 3,
        out_shape=[SDS((layers, r, c), F32)] * 3, compiler_params=_cp("parallel", "parallel"),
    )(w, g, m, v)


def _place():
    return lax.axis_index("x"), lax.axis_index("y"), lax.axis_index("c")


def _flip(coord, bit):
    return 1 - coord if bit else coord


def _allgather(shards, tag):
    n_arr = len(shards)
    hbm = BS(memory_space=pl.ANY)

    def body(*refs):
        ins, outs = refs[:n_arr], refs[n_arr:2 * n_arr]
        send_sems, recv_sems, local_sems = refs[2 * n_arr:]
        x, y, c = _place()
        me, sibling = (x, y, c), (x, y, 1 - c)
        chips = [(1 - x, y), (x, 1 - y), (1 - x, 1 - y)]

        def rows(a, p):
            r = shards[a].shape[0]
            return outs[a].at[pl.ds((4 * p[0] + 2 * p[1] + p[2]) * r, r), :]

        def copy(a, k, block, to, src=None):
            return pltpu.make_async_remote_copy(
                src_ref=rows(a, block) if src is None else src, dst_ref=rows(a, block),
                send_sem=send_sems.at[a, k], recv_sem=recv_sems.at[a, k], device_id=to, device_id_type=MESH)

        mine = [pltpu.make_async_copy(ins[a], rows(a, me), local_sems.at[a]) for a in range(n_arr)]
        for cp in mine:
            cp.start()
        first = []
        for a in range(n_arr):
            first.append(copy(a, 0, me, sibling, src=ins[a]))
            first += [copy(a, 1 + j, me, (*chip, c), src=ins[a]) for j, chip in enumerate(chips)]
        for cp in first:
            cp.start()
        passed = []
        for a in range(n_arr):
            for j, chip in enumerate(chips):
                copy(a, 1 + j, (*chip, c), me).wait_recv()
                passed.append(copy(a, 4 + j, (*chip, c), sibling))
                passed[-1].start()
        for a in range(n_arr):
            copy(a, 0, sibling, me).wait_recv()
            for j, chip in enumerate(chips):
                copy(a, 4 + j, (*chip, 1 - c), me).wait_recv()
        for cp in first + passed:
            cp.wait_send()
        for cp in mine:
            cp.wait()

    return pl.pallas_call(
        body, name=f"allgather_{tag}", in_specs=[hbm] * n_arr, out_specs=[hbm] * n_arr,
        out_shape=[SDS((N_DEV * s.shape[0], s.shape[1]), s.dtype) for s in shards],
        scratch_shapes=[pltpu.SemaphoreType.DMA((n_arr, 7)), pltpu.SemaphoreType.DMA((n_arr, 7)),
                        pltpu.SemaphoreType.DMA((n_arr,))],
        compiler_params=pltpu.CompilerParams(has_side_effects=True),
    )(*shards)


def _peers(x, y, c):
    peers = []
    for mask in range(1, N_DEV):
        p = (_flip(x, mask & 4), _flip(y, mask & 2), _flip(c, mask & 1))
        peers.append((p, 4 * p[0] + 2 * p[1] + p[2]))
    return peers


def _exchange_refs(mode, src, land, me, peer):
    if mode == "gather":
        r = src.shape[0]
        return src, land.at[pl.ds(me * r, r), :], land.at[pl.ds(peer * r, r), :]
    r = land.shape[1]
    return src.at[pl.ds(peer * r, r), :], land.at[me], land.at[peer]


HBM_SPEC = BS(memory_space=pltpu.HBM)
SEM_SPEC = BS(memory_space=pltpu.SEMAPHORE)
DATAFLOW = pltpu.SideEffectType.DATAFLOW_SIDE_EFFECTING


def _own_block_placed(mode, src, me):
    if mode == "gather":
        r, c = src.shape
        return lax.dynamic_update_slice(lax.empty((N_DEV * r, c), src.dtype), src, (me * r, 0))
    r, c = src.shape[0] // N_DEV, src.shape[1]
    own = lax.dynamic_slice(src, (me * r, 0), (r, c))
    return lax.dynamic_update_slice(lax.empty((N_DEV, r, c), src.dtype), own[None], (me, 0, 0))


def _exchange_start(mode, srcs, after, tag):
    n = len(srcs)
    x, y, c = _place()
    lands = [_own_block_placed(mode, s, 4 * x + 2 * y + c) for s in srcs]
    behind = [] if after is None else [after]

    def body(*refs):
        src_refs, land_refs = refs[:n], refs[n:2 * n]
        send_sems, recv_sems = refs[2 * n + len(behind)], refs[2 * n + len(behind) + 1]
        token = refs[-1]
        bx, by, bc = _place()
        me = 4 * bx + 2 * by + bc
        for a in range(n):
            for k, (p, idx) in enumerate(_peers(bx, by, bc)):
                out, there, _ = _exchange_refs(mode, src_refs[a], land_refs[a], me, idx)
                pltpu.make_async_remote_copy(
                    src_ref=out, dst_ref=there, send_sem=send_sems.at[7 * a + k], recv_sem=recv_sems.at[7 * a + k],
                    device_id=p, device_id_type=MESH).start()
        token[...] = jnp.zeros_like(token)

    res = pl.pallas_call(
        body, name=f"{mode}_start_{tag}",
        out_shape=(pltpu.SemaphoreType.DMA((7 * n,)), pltpu.SemaphoreType.DMA((7 * n,)),
                   *[pltpu.HBM(s.shape, s.dtype) for s in srcs], *[pltpu.HBM(l.shape, l.dtype) for l in lands],
                   SDS((8, 128), F32)),
        in_specs=[HBM_SPEC] * (2 * n) + [BS(memory_space=pl.ANY)] * len(behind),
        out_specs=(SEM_SPEC, SEM_SPEC, *[HBM_SPEC] * (2 * n), BS(memory_space=pltpu.VMEM)),
        input_output_aliases={i: 2 + i for i in range(2 * n)},
        compiler_params=pltpu.CompilerParams(has_side_effects=DATAFLOW),
    )(*[pltpu.with_memory_space_constraint(s, pltpu.HBM) for s in srcs],
      *[pltpu.with_memory_space_constraint(l, pltpu.HBM) for l in lands], *behind)
    return (mode, res[0], res[1], res[2:2 + n], res[2 + n:2 + 2 * n]), res[-1]


def _exchange_wait(handle, after, tag):
    mode, send_sems, recv_sems, srcs, lands = handle
    n = len(srcs)

    def body(*refs):
        src_refs, land_refs = refs[:n], refs[n:2 * n]
        send_ref, recv_ref = refs[2 * n], refs[2 * n + 1]
        bx, by, bc = _place()
        me = 4 * bx + 2 * by + bc
        for a in range(n):
            for k, (p, idx) in enumerate(_peers(bx, by, bc)):
                out, _, here = _exchange_refs(mode, src_refs[a], land_refs[a], me, idx)
                cp = pltpu.make_async_remote_copy(
                    src_ref=out, dst_ref=here, send_sem=send_ref.at[7 * a + k], recv_sem=recv_ref.at[7 * a + k],
                    device_id=p, device_id_type=MESH)
                cp.wait_send()
                cp.wait_recv()

    res = pl.pallas_call(
        body, name=f"{mode}_wait_{tag}",
        out_shape=(*[pltpu.HBM(s.shape, s.dtype) for s in srcs], *[pltpu.HBM(l.shape, l.dtype) for l in lands]),
        in_specs=[HBM_SPEC] * (2 * n) + [SEM_SPEC, SEM_SPEC, BS(memory_space=pl.ANY)],
        out_specs=tuple([HBM_SPEC] * (2 * n)),
        input_output_aliases={i: i for i in range(2 * n)},
        compiler_params=pltpu.CompilerParams(has_side_effects=DATAFLOW),
    )(*srcs, *lands, send_sems, recv_sems, after)
    return list(res[n:])


def _allreduce_small(vec):
    rows = vec.shape[0]

    def body(x_ref, o_ref, buf_ref, send_sems, recv_sems):
        x, y, c = _place()
        me = 4 * x + 2 * y + c
        buf_ref[me] = x_ref[...]
        peers = []
        for mask in range(1, N_DEV):
            p = (_flip(x, mask & 4), _flip(y, mask & 2), _flip(c, mask & 1))
            peers.append((p, 4 * p[0] + 2 * p[1] + p[2]))

        def copy(k, slot):
            return pltpu.make_async_remote_copy(
                src_ref=x_ref, dst_ref=buf_ref.at[slot], send_sem=send_sems.at[k], recv_sem=recv_sems.at[k],
                device_id=peers[k][0], device_id_type=MESH)

        sends = [copy(k, me) for k in range(N_DEV - 1)]
        for cp in sends:
            cp.start()
        for k in range(N_DEV - 1):
            copy(k, peers[k][1]).wait_recv()
        for cp in sends:
            cp.wait_send()
        acc = buf_ref[0]
        for s in range(1, N_DEV):
            acc = acc + buf_ref[s]
        o_ref[...] = acc

    vmem = BS(memory_space=pltpu.VMEM)
    return pl.pallas_call(
        body, name="allreduce_small", in_specs=[vmem], out_specs=vmem, out_shape=SDS((rows, 128), F32),
        scratch_shapes=[pltpu.VMEM((N_DEV, rows, 128), F32), pltpu.SemaphoreType.DMA((7,)),
                        pltpu.SemaphoreType.DMA((7,))],
        compiler_params=pltpu.CompilerParams(has_side_effects=True),
    )(vec)


def _ffn_forward(x, norm_g, wut, wd, tag):
    hn = _norm_fwd(x, norm_g, tag)
    gu, act = _ffn_up(hn, wut, tag)
    out = _mm_nn(act[None], wd[None], f"down_{tag}", res=x, scale=0.5)
    return out, (x, hn, gu, act)


def _ffn_backward(dxo, saved, norm_g, wut, wd, tag, send):
    x, hn, gu, act = saved
    du = _ffn_dact(dxo, wd, gu, tag)
    d_wd = _mm_tn(act[None], dxo, f"dwd_{tag}", scale=0.5, tmm=1408)[0]
    d_wut = _mm_tn(du, hn, f"dwu_{tag}", tmm=1408)
    zero = send([d_wut.reshape(2 * F, D), d_wd])
    dhn = _mm_nn(du, wut, f"dhn_{tag}", tn=512)
    return _norm_bwd(x, norm_g + zero, dhn, dxo, tag)


def _mixer_forward(x, norm_g, w, bias, tables, tag):
    wint, wat, wbt, wo = w
    hn = _norm_fwd(x, norm_g, tag)
    proj = _mm_nt_rows(hn, wint, f"proj_{tag}", 512, IN_W // 2, IN_W, 0)
    qkr = _rope_fwd(proj, *tables)
    outs, lses = [], []
    for grp in range(3):
        o, l = _dil_fwd(qkr, proj, grp)
        outs.append(o)
        lses.append(l)
    ya = _combine_fwd(outs, lses)
    yb, lse_b = _na_fwd(proj, bias)
    merged, za, zb = _merge_fwd(ya, yb, proj, wat, wbt)
    out = _mm_nn(merged[None], wo[None], f"out_{tag}", res=x)
    return out, (x, hn, proj, qkr, outs, lses, ya, yb, lse_b, merged, za, zb)


def _mixer_backward(dxo, saved, norm_g, w, bias, tables, tag, send):
    wint, wat, wbt, wo = w
    x, hn, proj, qkr, outs, lses, ya, yb, lse_b, merged, za, zb = saved
    dm = _mm_nt_rows(dxo, wo, f"dmerged_{tag}", 512, D, D, 0)
    d_wo = _mm_tn(merged[None], dxo, f"dwo_{tag}")[0]
    dza, dzb, dlog = _merge_bwd(dm, za, zb, proj)
    dya = _mm_nn(dza[None], wat[None], f"dya_{tag}")
    dyb = _mm_nn(dzb[None], wbt[None], f"dyb_{tag}")
    d_wat = _mm_tn(dza[None], ya, f"dwa_{tag}")[0]
    d_wbt = _mm_tn(dzb[None], yb, f"dwb_{tag}")[0]
    cb = _combine_bwd(dya, outs, lses)
    dqs, dks, dvs = [], [], []
    for grp in range(3):
        dq, dk, dv = _dil_bwd(qkr, proj, cb[grp], cb[3 + grp], lses[grp], grp)
        dqs.append(dq)
        dks.append(dk)
        dvs.append(dv)
    dqk = _rope_bwd(dqs, dks, *tables)
    dqb, dkb, dvb, dbias_tab = _na_bwd(proj, bias, dyb, yb, lse_b)
    dbias = _na_dbias(dbias_tab)
    dproj = jnp.concatenate(
        [dqk] + [t.astype(BF16) for t in (*dvs, dqb, dkb, dvb)] + [dlog[0], dlog[1]], axis=1)
    d_wint = _mm_tn(dproj[None], hn, f"dwin_{tag}", tmm=2944)[0]
    zero = send([d_wint, d_wat, d_wbt, d_wo])
    dhn = _mm_nn(dproj[None], wint[None], f"dhnm_{tag}", tm=256, tn=512)
    dx, dg = _norm_bwd(x, norm_g + zero, dhn, dxo, f"mix_{tag}")
    dbias = dbias[:, :31, :15].transpose(0, 2, 1)
    return dx, dg, dbias


def _pack_small(norms, biases, final, loss=None):
    parts = []
    for layer in range(DEPTH):
        parts += [norms[0][layer], norms[1][layer], norms[2][layer],
                  jnp.pad(biases[layer].reshape(-1), (0, BIAS_PAD - 8 * 15 * 31))]
    parts.append(final)
    flat = jnp.concatenate([p.reshape(-1).astype(F32) for p in parts])
    if loss is not None:
        flat = jnp.concatenate([flat, loss.reshape(-1)])
    return jnp.pad(flat, (0, SMALL_ROWS * 128 - flat.shape[0])).reshape(SMALL_ROWS, 128)


def _unpack_small(packed):
    flat = packed.reshape(-1)
    norms, biases = ([], [], []), []
    pos = 0
    for _ in range(DEPTH):
        for k in range(3):
            norms[k].append(flat[pos:pos + D])
            pos += D
        biases.append(flat[pos:pos + 8 * 15 * 31].reshape(8, 15, 31))
        pos += BIAS_PAD
    final = flat[pos:pos + D]
    pos += D
    return [jnp.stack(n) for n in norms], jnp.stack(biases), final, flat[pos]


def kernel(x, ffn1_norm, ffn1_w_up, ffn1_w_down, mix_norm, w_in, na_rel_bias, w_branch_a, w_branch_b, w_out, ffn2_norm, ffn2_w_up, ffn2_w_down, final_norm, loss_target, m_ffn1_norm, m_ffn1_w_up, m_ffn1_w_down, m_mix_norm, m_w_in, m_na_rel_bias, m_w_branch_a, m_w_branch_b, m_w_out, m_ffn2_norm, m_ffn2_w_up, m_ffn2_w_down, m_final_norm, v_ffn1_norm, v_ffn1_w_up, v_ffn1_w_down, v_mix_norm, v_w_in, v_na_rel_bias, v_w_branch_a, v_w_branch_b, v_w_out, v_ffn2_norm, v_ffn2_w_up, v_ffn2_w_down, v_final_norm):
    t = x.shape[0] * x.shape[1]
    xs = x.reshape(t, D)
    tgt = loss_target.reshape(t, D)
    tables = _rope_tables()

    col_sharded = dict(up1=ffn1_w_up, win=w_in, wa=w_branch_a, wb=w_branch_b, up2=ffn2_w_up)
    row_sharded = dict(down1=ffn1_w_down, wo=w_out, down2=ffn2_w_down)
    sublayers = (("up1", "down1"), ("win", "wa", "wb", "wo"), ("up2", "down2"))
    shard = [{} for _ in range(DEPTH)]
    for layer in range(DEPTH):
        for name, arr in col_sharded.items():
            shard[layer][name] = arr[layer].T.astype(BF16)
        for name, arr in row_sharded.items():
            shard[layer][name] = arr[layer].astype(BF16)

    weights = [{} for _ in range(DEPTH)]
    first = _allgather([shard[0][n] for n in sublayers[0]], "first")
    weights[0].update(zip(sublayers[0], first))
    pending = {}
    after = first[0]
    for layer in range(DEPTH):
        for k, names in enumerate(sublayers):
            if (layer, k) != (0, 0):
                pending[layer, k], after = _exchange_start(
                    "gather", [shard[layer][n] for n in names], after, f"w{layer}{k}")
    zero = after[0, 0]

    def arrived(layer, k, behind):
        if (layer, k) in pending:
            got = _exchange_wait(pending.pop((layer, k)), behind, f"w{layer}{k}")
            weights[layer].update(zip(sublayers[k], got))
        return weights[layer]

    saved = []
    h = xs
    for layer in range(DEPTH):
        bias = _na_bias_table(na_rel_bias[layer])
        w = arrived(layer, 0, h)
        h, s1 = _ffn_forward(h, ffn1_norm[layer] + zero, w["up1"].reshape(2, F, D), w["down1"], f"f1l{layer}")
        w = arrived(layer, 1, h)
        h, s2 = _mixer_forward(h, mix_norm[layer], (w["win"], w["wa"], w["wb"], w["wo"]), bias, tables, f"l{layer}")
        w = arrived(layer, 2, h)
        h, s3 = _ffn_forward(h, ffn2_norm[layer], w["up2"].reshape(2, F, D), w["down2"], f"f2l{layer}")
        saved.append((s1, s2, s3, bias))
    loss_part, dh, d_final = _loss_head(h, final_norm, tgt)

    d_norms = ([None] * DEPTH, [None] * DEPTH, [None] * DEPTH)
    d_bias = [None] * DEPTH
    sent = {}

    def sender(layer, k):
        def send(grads):
            sent[layer, k], token = _exchange_start("scatter", grads, None, f"g{layer}{k}")
            return token[0, 0]
        return send

    for layer in reversed(range(DEPTH)):
        w = weights[layer]
        s1, s2, s3, bias = saved[layer]
        dh, d_norms[2][layer] = _ffn_backward(
            dh, s3, ffn2_norm[layer], w["up2"].reshape(2, F, D), w["down2"], f"f2l{layer}", sender(layer, 2))
        dh, d_norms[1][layer], d_bias[layer] = _mixer_backward(
            dh, s2, mix_norm[layer], (w["win"], w["wa"], w["wb"], w["wo"]), bias, tables, f"l{layer}", sender(layer, 1))
        dh, d_norms[0][layer] = _ffn_backward(
            dh, s1, ffn1_norm[layer], w["up1"].reshape(2, F, D), w["down1"], f"f1l{layer}", sender(layer, 0))
    grad_x = dh.reshape(x.shape)

    small = _allreduce_small(_pack_small(d_norms, d_bias, d_final, loss_part[0, :1]))
    g_norms, g_bias, g_final, loss = _unpack_small(small)

    originals = dict(up1=(ffn1_w_up, m_ffn1_w_up, v_ffn1_w_up), down1=(ffn1_w_down, m_ffn1_w_down, v_ffn1_w_down),
                     win=(w_in, m_w_in, v_w_in), wa=(w_branch_a, m_w_branch_a, v_w_branch_a),
                     wb=(w_branch_b, m_w_branch_b, v_w_branch_b), wo=(w_out, m_w_out, v_w_out),
                     up2=(ffn2_w_up, m_ffn2_w_up, v_ffn2_w_up), down2=(ffn2_w_down, m_ffn2_w_down, v_ffn2_w_down))
    big = {}
    behind = small
    for k in (2, 1, 0):
        recv = [_exchange_wait(sent[layer, k], behind, f"g{layer}{k}") for layer in (1, 0)]
        for i, name in enumerate(sublayers[k]):
            g = _sum_slots(recv[1][i], recv[0][i], name)
            wv, mv, vv = originals[name]
            if name in col_sharded:
                wv, mv, vv = (jnp.swapaxes(t, 1, 2) for t in (wv, mv, vv))
            big[name] = (g, *_adamw(wv, g, mv, vv, name))
            behind = big[name][1]
            if name in col_sharded:
                big[name] = tuple(jnp.swapaxes(t, 1, 2) for t in big[name])

    w_small = _pack_small((ffn1_norm, mix_norm, ffn2_norm), na_rel_bias, final_norm)
    m_small = _pack_small((m_ffn1_norm, m_mix_norm, m_ffn2_norm), m_na_rel_bias, m_final_norm)
    v_small = _pack_small((v_ffn1_norm, v_mix_norm, v_ffn2_norm), v_na_rel_bias, v_final_norm)
    upd = _adamw(w_small[None], small[None], m_small[None], v_small[None], "small")
    small_out = [(g_norms, g_bias, g_final)] + [_unpack_small(u[0])[:3] for u in upd]

    outputs = [loss, grad_x]
    for kind in range(4):
        norms, bias_k, final_k = small_out[kind]
        outputs += [norms[0], big["up1"][kind], big["down1"][kind], norms[1], big["win"][kind], bias_k,
                    big["wa"][kind], big["wb"][kind], big["wo"][kind], norms[2], big["up2"][kind],
                    big["down2"][kind], final_k]
    return tuple(outputs)
```

```python
import numpy as np

import jax
import jax.numpy as jnp
from jax import lax
from jax.experimental import pallas as pl
from jax.experimental.pallas import tpu as pltpu

F32 = jnp.float32
BF16 = jnp.bfloat16
SDS = jax.ShapeDtypeStruct
BS = pl.BlockSpec
MESH = pl.DeviceIdType.MESH

D = 1024
S = 2048
F = 2816
DEPTH = 2
HEAD_DIM = 64
DILATIONS = (1, 4, 16)
HALF = 64
QKV_A = 2304
QKV_B = 1536
IN_W = 5888
N_DEV = 8
NA_ROWS = 32
GRID_W = 64
NA_KR = 8
ROPE_THETA = 10000.0
RMS_EPS = 1e-6
NEG = -1e30
SCALE = HEAD_DIM ** -0.5
ADAM_LR, ADAM_B1, ADAM_B2, ADAM_EPS, ADAM_WD, ADAM_STEP = 0.001, 0.9, 0.999, 1e-08, 0.01, 10
VMEM_LIMIT_V7X = 52 * 1024 * 1024
SMALL_ROWS = 120
BIAS_PAD = 3840
NA_FWD_ROWS = 4
NA_BWD_ROWS = 2
DIL_FWD_TILES = 4
DIL_BWD_TILES = 2


def _cp(*sem):
    return pltpu.CompilerParams(dimension_semantics=sem, vmem_limit_bytes=VMEM_LIMIT_V7X)


def _dot_nn(a, b):
    return jnp.dot(a, b, preferred_element_type=F32)


def _dot_nt(a, b):
    return lax.dot_general(a, b, (((1,), (1,)), ((), ())), preferred_element_type=F32)


def _dot_tn(a, b):
    return lax.dot_general(a, b, (((0,), (0,)), ((), ())), preferred_element_type=F32)


def _ds(start, size, stride):
    return pl.ds(start, size) if stride == 1 else pl.ds(start, size, stride=stride)


def _norm_fwd(x, g, tag):
    t = x.shape[0]
    tm = 512

    def body(x_ref, g_ref, o_ref):
        xv = x_ref[...]
        r = lax.rsqrt(jnp.mean(xv * xv, axis=-1, keepdims=True) + RMS_EPS)
        o_ref[...] = (xv * r * g_ref[...]).astype(BF16)

    return pl.pallas_call(
        body, name=f"norm_fwd_{tag}", grid=(t // tm,),
        in_specs=[BS((tm, D), lambda i: (i, 0)), BS((1, D), lambda i: (0, 0))],
        out_specs=BS((tm, D), lambda i: (i, 0)),
        out_shape=SDS((t, D), BF16), compiler_params=_cp("parallel"),
    )(x, g.reshape(1, D))


def _norm_bwd(x, g, dh, dres, tag):
    t = x.shape[0]
    tm = 512

    def body(x_ref, g_ref, dh_ref, dr_ref, dx_ref, dg_ref):
        @pl.when(pl.program_id(0) == 0)
        def _():
            dg_ref[...] = jnp.zeros_like(dg_ref)

        xv = x_ref[...]
        r = lax.rsqrt(jnp.mean(xv * xv, axis=-1, keepdims=True) + RMS_EPS)
        xh = xv * r
        dh = dh_ref[...]
        u = dh * g_ref[...]
        dx_ref[...] = dr_ref[...] + r * (u - xh * jnp.mean(xh * u, axis=-1, keepdims=True))
        dg_ref[...] += jnp.sum(dh * xh, axis=0, keepdims=True)

    row = BS((tm, D), lambda i: (i, 0))
    vec = BS((1, D), lambda i: (0, 0))
    return pl.pallas_call(
        body, name=f"norm_bwd_{tag}", grid=(t // tm,),
        in_specs=[row, vec, row, row], out_specs=[row, vec],
        out_shape=[SDS((t, D), F32), SDS((1, D), F32)], compiler_params=_cp("arbitrary"),
    )(x, g.reshape(1, D), dh, dres)


def _loss_head(x, g, tgt):
    t = x.shape[0]
    tm = 512

    def body(x_ref, g_ref, t_ref, loss_ref, dx_ref, dg_ref):
        @pl.when(pl.program_id(0) == 0)
        def _():
            dg_ref[...] = jnp.zeros_like(dg_ref)
            loss_ref[...] = jnp.zeros_like(loss_ref)

        xv = x_ref[...]
        gv = g_ref[...]
        r = lax.rsqrt(jnp.mean(xv * xv, axis=-1, keepdims=True) + RMS_EPS)
        xh = xv * r
        e = xh * gv - t_ref[...]
        loss_ref[...] += 0.5 * jnp.sum(jnp.mean(e * e, axis=-1, keepdims=True), axis=0, keepdims=True)
        dy = e * (1.0 / D)
        u = dy * gv
        dx_ref[...] = r * (u - xh * jnp.mean(xh * u, axis=-1, keepdims=True))
        dg_ref[...] += jnp.sum(dy * xh, axis=0, keepdims=True)

    row = BS((tm, D), lambda i: (i, 0))
    vec = BS((1, D), lambda i: (0, 0))
    return pl.pallas_call(
        body, name="loss_head", grid=(t // tm,),
        in_specs=[row, vec, row], out_specs=[BS((1, 128), lambda i: (0, 0)), row, vec],
        out_shape=[SDS((1, 128), F32), SDS((t, D), F32), SDS((1, D), F32)],
        compiler_params=_cp("arbitrary"),
    )(x, g.reshape(1, D), tgt)


def _mm_nn(a, w, tag, res=None, scale=1.0, tm=512, tn=None):
    c_n, t, k = a.shape
    n = w.shape[2]
    tn = n if tn is None else tn

    def body(*refs):
        a_ref, w_ref = refs[0], refs[1]
        o_ref = refs[-1]
        acc = _dot_nn(a_ref[0].astype(BF16), w_ref[0])
        for c in range(1, c_n):
            acc = acc + _dot_nn(a_ref[c].astype(BF16), w_ref[c])
        if scale != 1.0:
            acc = acc * scale
        if res is not None:
            acc = refs[2][...] + acc
        o_ref[...] = acc

    in_specs = [BS((c_n, tm, k), lambda i, j: (0, i, 0)), BS((c_n, k, tn), lambda i, j: (0, 0, j))]
    args = [a, w]
    if res is not None:
        in_specs.append(BS((tm, tn), lambda i, j: (i, j)))
        args.append(res)
    return pl.pallas_call(
        body, name=f"mm_nn_{tag}", grid=(t // tm, n // tn), in_specs=in_specs,
        out_specs=BS((tm, tn), lambda i, j: (i, j)), out_shape=SDS((t, n), F32),
        compiler_params=_cp("parallel", "parallel"),
    )(*args)


def _mm_nt_rows(a, w, tag, tm, tn, n_total, w_row0):
    t, k = a.shape
    assert w_row0 % tn == 0 and n_total % tn == 0
    j0 = w_row0 // tn

    def body(a_ref, w_ref, o_ref):
        o_ref[...] = _dot_nt(a_ref[...].astype(BF16), w_ref[...])

    return pl.pallas_call(
        body, name=f"mm_nt_{tag}", grid=(t // tm, n_total // tn),
        in_specs=[BS((tm, k), lambda i, j: (i, 0)), BS((tn, k), lambda i, j: (j0 + j, 0))],
        out_specs=BS((tm, tn), lambda i, j: (i, j)), out_shape=SDS((t, n_total), F32),
        compiler_params=_cp("parallel", "parallel"),
    )(a, w)


def _mm_tn(a, b, tag, scale=1.0, tmm=256, tk=None):
    c_n, t, m = a.shape
    n = b.shape[1]
    tk = t if tk is None else tk
    nk = t // tk

    def body_one(a_ref, b_ref, o_ref):
        o_ref[...] = (_dot_tn(a_ref[...].astype(BF16), b_ref[...].astype(BF16)) * scale).astype(BF16)

    def body_acc(a_ref, b_ref, o_ref, acc_ref):
        kk = pl.program_id(2)

        @pl.when(kk == 0)
        def _():
            acc_ref[...] = jnp.zeros_like(acc_ref)

        acc_ref[...] += _dot_tn(a_ref[...].astype(BF16), b_ref[...].astype(BF16))

        @pl.when(kk == nk - 1)
        def _():
            o_ref[...] = (acc_ref[...] * scale).astype(BF16)

    return pl.pallas_call(
        body_one if nk == 1 else body_acc, name=f"mm_tn_{tag}", grid=(c_n, m // tmm, nk),
        in_specs=[BS((None, tk, tmm), lambda c, mi, kk: (c, kk, mi)), BS((tk, n), lambda c, mi, kk: (kk, 0))],
        out_specs=BS((None, tmm, n), lambda c, mi, kk: (c, mi, 0)),
        out_shape=SDS((c_n, m, n), BF16), scratch_shapes=[] if nk == 1 else [pltpu.VMEM((tmm, n), F32)],
        compiler_params=_cp("parallel", "parallel", "arbitrary"),
    )(a, b)


def _ffn_up(hn, wut, tag):
    t = hn.shape[0]
    tm, tn = 512, 1408

    def body(h_ref, w_ref, gu_ref, act_ref):
        h = h_ref[...]
        g = _dot_nt(h, w_ref[0])
        u = _dot_nt(h, w_ref[1])
        gu_ref[0] = g.astype(BF16)
        gu_ref[1] = u.astype(BF16)
        act_ref[...] = (g * jax.nn.sigmoid(g) * u).astype(BF16)

    return pl.pallas_call(
        body, name=f"ffn_up_{tag}", grid=(t // tm, F // tn),
        in_specs=[BS((tm, D), lambda i, j: (i, 0)), BS((2, tn, D), lambda i, j: (0, j, 0))],
        out_specs=[BS((2, tm, tn), lambda i, j: (0, i, j)), BS((tm, tn), lambda i, j: (i, j))],
        out_shape=[SDS((2, t, F), BF16), SDS((t, F), BF16)],
        compiler_params=_cp("parallel", "parallel"),
    )(hn, wut)


def _ffn_dact(dxo, wd, gu, tag):
    t = dxo.shape[0]
    tm, tn = 512, 1408

    def body(d_ref, w_ref, gu_ref, o_ref):
        dact = _dot_nt((d_ref[...] * 0.5).astype(BF16), w_ref[...])
        g = gu_ref[0].astype(F32)
        u = gu_ref[1].astype(F32)
        sg = 0.5 * jnp.tanh(0.5 * g) + 0.5
        o_ref[0] = (dact * u * (sg * (1.0 + g * (1.0 - sg)))).astype(BF16)
        o_ref[1] = (dact * (g * sg)).astype(BF16)

    return pl.pallas_call(
        body, name=f"ffn_dact_{tag}", grid=(t // tm, F // tn),
        in_specs=[BS((tm, D), lambda i, j: (i, 0)), BS((tn, D), lambda i, j: (j, 0)),
                  BS((2, tm, tn), lambda i, j: (0, i, j))],
        out_specs=BS((2, tm, tn), lambda i, j: (0, i, j)),
        out_shape=SDS((2, t, F), BF16), compiler_params=_cp("parallel", "parallel"),
    )(dxo, wd, gu)


def _rope_tables():
    half = HEAD_DIM // 2
    inv_freq = ROPE_THETA ** (-jnp.arange(half, dtype=F32) / half)
    ang = jnp.arange(S).astype(F32)[:, None] * inv_freq[None, :]
    cos, sin = jnp.cos(ang), jnp.sin(ang)
    return jnp.concatenate([cos, cos, cos, cos], axis=1), jnp.concatenate([-sin, sin, -sin, sin], axis=1)


def _swap_halves(t, first_half):
    return jnp.where(first_half, pltpu.roll(t, 96, 1), pltpu.roll(t, 32, 1))


def _rope_fwd(proj, cos_t, sin_t):
    t = proj.shape[0]
    tm = 512
    width = 2 * QKV_A // 3

    def body(x_ref, c_ref, s_ref, o_ref):
        c = c_ref[...]
        sg = s_ref[...]
        first = (lax.broadcasted_iota(jnp.int32, (tm, 128), 1) % HEAD_DIM) < HEAD_DIM // 2
        for j in range(width // 128):
            v = x_ref[:, 128 * j:128 * (j + 1)]
            o_ref[:, 128 * j:128 * (j + 1)] = v * c + _swap_halves(v, first) * sg

    tab = BS((tm, 128), lambda i: (i % (S // tm), 0))
    return pl.pallas_call(
        body, name="rope_fwd", grid=(t // tm,),
        in_specs=[BS((tm, width), lambda i: (i, 0)), tab, tab],
        out_specs=BS((tm, width), lambda i: (i, 0)), out_shape=SDS((t, width), F32),
        compiler_params=_cp("parallel"),
    )(proj, cos_t, sin_t)


def _rope_bwd(dqs, dks, cos_t, sin_t):
    t = dqs[0].shape[0]
    tm = 512

    def body(*refs):
        c = refs[6][...]
        sg = refs[7][...]
        o_ref = refs[8]
        first = (lax.broadcasted_iota(jnp.int32, (tm, 128), 1) % HEAD_DIM) < HEAD_DIM // 2
        for a in range(6):
            for hp in range(2):
                v = refs[a][:, 128 * hp:128 * (hp + 1)]
                col = 128 * (2 * a + hp)
                o_ref[:, col:col + 128] = (v * c + _swap_halves(v * sg, first)).astype(BF16)

    blk = BS((tm, 256), lambda i: (i, 0))
    tab = BS((tm, 128), lambda i: (i % (S // tm), 0))
    return pl.pallas_call(
        body, name="rope_bwd", grid=(t // tm,), in_specs=[blk] * 6 + [tab, tab],
        out_specs=BS((tm, 1536), lambda i: (i, 0)), out_shape=SDS((t, 1536), BF16),
        compiler_params=_cp("parallel"),
    )(*dqs, *dks, cos_t, sin_t)


def _head_masks():
    lane = lax.broadcasted_iota(jnp.int32, (1, 128), 1)
    m0 = (lane < HEAD_DIM).astype(F32)
    return m0, 1.0 - m0


def _dil_geometry(d):
    sub = S // d
    q_rows = 128
    k_rows = min(256, sub)
    return sub, q_rows, sub // q_rows, k_rows


def _dil_tile(idx, d):
    sub, q_rows, nb, k_rows = _dil_geometry(d)
    r = idx // nb
    n = idx % nb
    k_sub = jnp.clip(q_rows * n - HALF, 0, sub - k_rows)
    if d == 1:
        q_start = pl.multiple_of(q_rows * n, q_rows)
        k_start = pl.multiple_of(k_sub, HALF)
    else:
        q_start = q_rows * n * d + r
        k_start = k_sub * d + r
    ii = lax.broadcasted_iota(jnp.int32, (q_rows, k_rows), 0)
    jj = lax.broadcasted_iota(jnp.int32, (q_rows, k_rows), 1)
    valid = jnp.abs(jj - ii + (k_sub - q_rows * n)) <= HALF
    return q_start, k_start, valid


def _dil_specs(grp):
    qs = BS((S, 128), lambda b, hp: (b, 2 * grp + hp))
    ks = BS((S, 128), lambda b, hp: (b, 6 + 2 * grp + hp))
    vs = BS((S, 128), lambda b, hp: (b, 12 + 2 * grp + hp))
    own = BS((S, 128), lambda b, hp: (b, hp))
    return qs, ks, vs, own


def _dil_fwd(qkr, proj, grp):
    t = qkr.shape[0]
    d = DILATIONS[grp]
    _, q_rows, nb, k_rows = _dil_geometry(d)

    def body(q_ref, k_ref, v_ref, o_ref, l_ref):
        masks = _head_masks()

        def step(i0, carry):
            geo = [_dil_tile(i0 * DIL_FWD_TILES + j, d) for j in range(DIL_FWD_TILES)]
            tiles = [(j, h) for j in range(DIL_FWD_TILES) for h in range(2)]
            qs = [q_ref[_ds(g[0], q_rows, d), :] for g in geo]
            kbs = [k_ref[_ds(g[1], k_rows, d), :].astype(BF16) for g in geo]
            ss = [jnp.where(geo[j][2], _dot_nt((qs[j] * masks[h]).astype(BF16), kbs[j]) * SCALE, NEG) for j, h in tiles]
            mxs = [jnp.max(s, axis=1, keepdims=True) for s in ss]
            ps = [jnp.exp(s - mx) for s, mx in zip(ss, mxs)]
            dens = [jnp.sum(p, axis=1, keepdims=True) for p in ps]
            vs = [v_ref[_ds(g[1], k_rows, d), :] for g in geo]
            outs = [_dot_nn(p.astype(BF16), (vs[j] * masks[h]).astype(BF16)) / den
                    for p, den, (j, h) in zip(ps, dens, tiles)]
            for j, g in enumerate(geo):
                o_ref[_ds(g[0], q_rows, d), :] = outs[2 * j] + outs[2 * j + 1]
                l_ref[_ds(g[0], q_rows, d), :] = (
                    (mxs[2 * j] + jnp.log(dens[2 * j])) * masks[0] + (mxs[2 * j + 1] + jnp.log(dens[2 * j + 1])) * masks[1])
            return carry

        lax.fori_loop(0, d * nb // DIL_FWD_TILES, step, 0)

    qs, ks, vs, own = _dil_specs(grp)
    return pl.pallas_call(
        body, name=f"dil_fwd_{grp}", grid=(t // S, 2), in_specs=[qs, ks, vs], out_specs=[own, own],
        out_shape=[SDS((t, 256), F32), SDS((t, 256), F32)], compiler_params=_cp("parallel", "parallel"),
    )(qkr, qkr, proj)


def _dil_bwd(qkr, proj, do, dlp, lse, grp):
    t = qkr.shape[0]
    d = DILATIONS[grp]
    _, q_rows, nb, k_rows = _dil_geometry(d)

    def body(q_ref, k_ref, v_ref, do_ref, dl_ref, l_ref, dq_ref, dk_ref, dv_ref):
        masks = _head_masks()
        dk_ref[...] = jnp.zeros_like(dk_ref)
        dv_ref[...] = jnp.zeros_like(dv_ref)

        def step(i0, carry):
            geo = [_dil_tile(i0 * DIL_BWD_TILES + j, d) for j in range(DIL_BWD_TILES)]
            tiles = [(j, h) for j in range(DIL_BWD_TILES) for h in range(2)]
            q_ds = [_ds(g[0], q_rows, d) for g in geo]
            k_ds = [_ds(g[1], k_rows, d) for g in geo]
            qs = [q_ref[r, :] for r in q_ds]
            ks = [k_ref[r, :] for r in k_ds]
            kbs = [k.astype(BF16) for k in ks]
            vbs = [v_ref[r, :].astype(BF16) for r in k_ds]
            dos = [do_ref[r, :] for r in q_ds]
            dls = [dl_ref[r, :] for r in q_ds]
            lss = [l_ref[r, :] for r in q_ds]
            qhs = [(qs[j] * masks[h]).astype(BF16) for j, h in tiles]
            dohs = [(dos[j] * masks[h]).astype(BF16) for j, h in tiles]
            ss = [jnp.where(geo[j][2], _dot_nt(qh, kbs[j]) * SCALE, NEG) for qh, (j, h) in zip(qhs, tiles)]
            ps = [jnp.exp(s - lss[j][:, HEAD_DIM * h:HEAD_DIM * h + 1]) for s, (j, h) in zip(ss, tiles)]
            dps = [_dot_nt(doh, vbs[j]) for doh, (j, h) in zip(dohs, tiles)]
            dss = [(p * (dp - dls[j][:, HEAD_DIM * h:HEAD_DIM * h + 1])).astype(BF16)
                   for p, dp, (j, h) in zip(ps, dps, tiles)]
            dqs = [_dot_nn(ds, (ks[j] * masks[h]).astype(BF16)) for ds, (j, h) in zip(dss, tiles)]
            dkws = [_dot_tn(ds, qh) for ds, qh in zip(dss, qhs)]
            dvws = [_dot_tn(p.astype(BF16), doh) for p, doh in zip(ps, dohs)]
            for j in range(DIL_BWD_TILES):
                dq_ref[q_ds[j], :] = (dqs[2 * j] + dqs[2 * j + 1]) * SCALE
                dk_ref[k_ds[j], :] += (dkws[2 * j] + dkws[2 * j + 1]) * SCALE
                dv_ref[k_ds[j], :] += dvws[2 * j] + dvws[2 * j + 1]
            return carry

        lax.fori_loop(0, d * nb // DIL_BWD_TILES, step, 0)

    qs, ks, vs, own = _dil_specs(grp)
    return pl.pallas_call(
        body, name=f"dil_bwd_{grp}", grid=(t // S, 2), in_specs=[qs, ks, vs, own, own, own],
        out_specs=[own, own, own], out_shape=[SDS((t, 256), F32)] * 3,
        compiler_params=_cp("parallel", "parallel"),
    )(qkr, qkr, proj, do, dlp, lse)


def _mix_weights(l0, l1, l2):
    mx = jnp.maximum(jnp.maximum(l0, l1), l2)
    e0, e1, e2 = jnp.exp(l0 - mx), jnp.exp(l1 - mx), jnp.exp(l2 - mx)
    den = e0 + e1 + e2
    return e0 / den, e1 / den, e2 / den


def _combine_fwd(outs, lses):
    t = outs[0].shape[0]
    tm = 512

    def body(o0, o1, o2, l0, l1, l2, y_ref):
        w0, w1, w2 = _mix_weights(l0[...], l1[...], l2[...])
        y_ref[...] = w0 * o0[...] + w1 * o1[...] + w2 * o2[...]

    blk = BS((tm, 256), lambda i: (i, 0))
    return pl.pallas_call(
        body, name="combine_fwd", grid=(t // tm,), in_specs=[blk] * 6, out_specs=blk,
        out_shape=SDS((t, 256), F32), compiler_params=_cp("parallel"),
    )(*outs, *lses)


def _head_sum(x):
    a = lax.broadcasted_iota(jnp.int32, (256, 256), 0) // HEAD_DIM
    b = lax.broadcasted_iota(jnp.int32, (256, 256), 1) // HEAD_DIM
    ones = (a == b).astype(BF16)
    hi = x.astype(BF16)
    lo = (x - hi.astype(F32)).astype(BF16)
    return _dot_nn(hi, ones) + _dot_nn(lo, ones)


def _combine_bwd(dya, outs, lses):
    t = dya.shape[0]
    tm = 512

    def body(dy_ref, o0, o1, o2, l0, l1, l2, d0, d1, d2, e0, e1, e2):
        ws = _mix_weights(l0[...], l1[...], l2[...])
        dy = dy_ref[...]
        ya = ws[0] * o0[...] + ws[1] * o1[...] + ws[2] * o2[...]
        hs = _head_sum(dy * ya)
        for w, d_ref, e_ref in zip(ws, (d0, d1, d2), (e0, e1, e2)):
            d_ref[...] = w * dy
            e_ref[...] = w * hs

    blk = BS((tm, 256), lambda i: (i, 0))
    return pl.pallas_call(
        body, name="combine_bwd", grid=(t // tm,), in_specs=[blk] * 7, out_specs=[blk] * 6,
        out_shape=[SDS((t, 256), F32)] * 6, compiler_params=_cp("parallel"),
    )(dya, *outs, *lses)


def _na_bias_table(rel_bias):
    qc = np.arange(GRID_W)[:, None]
    kc = np.arange(GRID_W)[None, :]
    win_lo = np.clip(qc - 8, 0, GRID_W - 16)
    col_valid = (kc >= win_lo) & (kc < win_lo + 16)
    col_idx = np.clip(kc - qc + 15, 0, 30)
    row_idx = np.arange(NA_KR)[:, None] + np.arange(NA_KR)[None, :]
    rows = (row_idx[..., None] == np.arange(2 * NA_KR - 1)).astype(np.float32)
    cols = (col_idx[..., None] == np.arange(31)).astype(np.float32)
    b = jnp.einsum("hrd,ckr,qjd->hcqkj", rel_bias.astype(F32), rows, cols, precision=lax.Precision.HIGHEST)
    b = jnp.where(col_valid[None, None, :, None, :], b, NEG)
    return b.reshape(8, NA_KR, GRID_W, NA_KR * GRID_W)


def _na_row(i):
    lo = jnp.clip(i - NA_KR // 2, 0, NA_ROWS - NA_KR)
    return pl.multiple_of(GRID_W * i, GRID_W), pl.multiple_of(GRID_W * lo, GRID_W), lo - i + NA_KR - 1


def _na_fwd(proj, bias):
    t = proj.shape[0]
    kw = NA_KR * GRID_W

    def body(q_ref, k_ref, v_ref, b_ref, o_ref, l_ref):
        masks = _head_masks()

        def step(i0, carry):
            rows = [_na_row(i0 * NA_FWD_ROWS + j) for j in range(NA_FWD_ROWS)]
            qs = [q_ref[pl.ds(q_start, GRID_W), :] for q_start, _, _ in rows]
            kbs = [k_ref[pl.ds(k_start, kw), :].astype(BF16) for _, k_start, _ in rows]
            tiles = [(j, h) for j in range(NA_FWD_ROWS) for h in range(2)]
            ss = [_dot_nt((qs[j] * masks[h]).astype(BF16), kbs[j]) * SCALE + b_ref[h, rows[j][2]] for j, h in tiles]
            mxs = [jnp.max(s, axis=1, keepdims=True) for s in ss]
            ps = [jnp.exp(s - mx) for s, mx in zip(ss, mxs)]
            dens = [jnp.sum(p, axis=1, keepdims=True) for p in ps]
            pbs = [(p / den).astype(BF16) for p, den in zip(ps, dens)]
            vs = [v_ref[pl.ds(k_start, kw), :] for _, k_start, _ in rows]
            outs = [_dot_nn(pb, (vs[j] * masks[h]).astype(BF16)) for pb, (j, h) in zip(pbs, tiles)]
            for j, (q_start, _, _) in enumerate(rows):
                o_ref[pl.ds(q_start, GRID_W), :] = outs[2 * j] + outs[2 * j + 1]
                l_ref[pl.ds(q_start, GRID_W), :] = (
                    (mxs[2 * j] + jnp.log(dens[2 * j])) * masks[0] + (mxs[2 * j + 1] + jnp.log(dens[2 * j + 1])) * masks[1])
            return carry

        lax.fori_loop(0, NA_ROWS // NA_FWD_ROWS, step, 0)

    c0 = QKV_A // 128
    own = BS((S, 128), lambda b, hp: (b, hp))
    return pl.pallas_call(
        body, name="na_fwd", grid=(t // S, 4),
        in_specs=[BS((S, 128), lambda b, hp: (b, c0 + hp)), BS((S, 128), lambda b, hp: (b, c0 + 4 + hp)),
                  BS((S, 128), lambda b, hp: (b, c0 + 8 + hp)),
                  BS((2, NA_KR, GRID_W, kw), lambda b, hp: (hp, 0, 0, 0))],
        out_specs=[own, own], out_shape=[SDS((t, 512), F32), SDS((t, 512), F32)],
        compiler_params=_cp("parallel", "parallel"),
    )(proj, proj, proj, bias)


def _na_bwd(proj, bias, dyb, yb, lse):
    t = proj.shape[0]
    kw = NA_KR * GRID_W

    def body(q_ref, k_ref, v_ref, b_ref, do_ref, o_ref, l_ref, dq_ref, dk_ref, dv_ref, db_ref):
        masks = _head_masks()

        @pl.when(pl.program_id(1) == 0)
        def _():
            db_ref[...] = jnp.zeros_like(db_ref)

        dk_ref[...] = jnp.zeros_like(dk_ref)
        dv_ref[...] = jnp.zeros_like(dv_ref)

        def step(i0, carry):
            rows = [_na_row(i0 * NA_BWD_ROWS + j) for j in range(NA_BWD_ROWS)]
            tiles = [(j, h) for j in range(NA_BWD_ROWS) for h in range(2)]
            q_ds = [pl.ds(r[0], GRID_W) for r in rows]
            k_ds = [pl.ds(r[1], kw) for r in rows]
            qs = [q_ref[r, :] for r in q_ds]
            ks = [k_ref[r, :] for r in k_ds]
            kbs = [k.astype(BF16) for k in ks]
            vbs = [v_ref[r, :].astype(BF16) for r in k_ds]
            dos = [do_ref[r, :] for r in q_ds]
            os_ = [o_ref[r, :] for r in q_ds]
            lss = [l_ref[r, :] for r in q_ds]
            qhs = [(qs[j] * masks[h]).astype(BF16) for j, h in tiles]
            dohs = [(dos[j] * masks[h]).astype(BF16) for j, h in tiles]
            deltas = [jnp.sum(dos[j] * os_[j] * masks[h], axis=1, keepdims=True) for j, h in tiles]
            ss = [_dot_nt(qh, kbs[j]) * SCALE + b_ref[h, rows[j][2]] for qh, (j, h) in zip(qhs, tiles)]
            ps = [jnp.exp(s - lss[j][:, HEAD_DIM * h:HEAD_DIM * h + 1]) for s, (j, h) in zip(ss, tiles)]
            dps = [_dot_nt(doh, vbs[j]) for doh, (j, h) in zip(dohs, tiles)]
            dss = [p * (dp - delta) for p, dp, delta in zip(ps, dps, deltas)]
            for ds, (j, h) in zip(dss, tiles):
                db_ref[h, rows[j][2]] += ds
            dsbs = [ds.astype(BF16) for ds in dss]
            dqs = [_dot_nn(dsb, (ks[j] * masks[h]).astype(BF16)) for dsb, (j, h) in zip(dsbs, tiles)]
            dkws = [_dot_tn(dsb, qh) for dsb, qh in zip(dsbs, qhs)]
            dvws = [_dot_tn(p.astype(BF16), doh) for p, doh in zip(ps, dohs)]
            for j in range(NA_BWD_ROWS):
                dq_ref[q_ds[j], :] = (dqs[2 * j] + dqs[2 * j + 1]) * SCALE
                dk_ref[k_ds[j], :] += (dkws[2 * j] + dkws[2 * j + 1]) * SCALE
                dv_ref[k_ds[j], :] += dvws[2 * j] + dvws[2 * j + 1]
            return carry

        lax.fori_loop(0, NA_ROWS // NA_BWD_ROWS, step, 0)

    c0 = QKV_A // 128
    own = BS((S, 128), lambda hp, b: (b, hp))
    tab = BS((2, NA_KR, GRID_W, kw), lambda hp, b: (hp, 0, 0, 0))
    return pl.pallas_call(
        body, name="na_bwd", grid=(4, t // S),
        in_specs=[BS((S, 128), lambda hp, b: (b, c0 + hp)), BS((S, 128), lambda hp, b: (b, c0 + 4 + hp)),
                  BS((S, 128), lambda hp, b: (b, c0 + 8 + hp)), tab, own, own, own],
        out_specs=[own, own, own, tab],
        out_shape=[SDS((t, 512), F32)] * 3 + [SDS((8, NA_KR, GRID_W, kw), F32)],
        compiler_params=_cp("parallel", "arbitrary"),
    )(proj, proj, proj, bias, dyb, yb, lse)


def _na_dbias(db):
    kw = NA_KR * GRID_W

    def body(x_ref, o_ref, z_ref):
        lane = lax.broadcasted_iota(jnp.int32, (GRID_W, kw), 1)
        sub = lax.broadcasted_iota(jnp.int32, (GRID_W, kw), 0)
        rel = (lane % GRID_W) - sub + 15
        key_row = lax.broadcasted_iota(jnp.int32, (kw, 128), 0) // GRID_W
        dr = lax.broadcasted_iota(jnp.int32, (kw, 128), 1)
        acc = jnp.zeros((32, 128), F32)
        z_ref[...] = jnp.zeros_like(z_ref)
        for cls in range(NA_KR):
            xv = x_ref[cls]
            for dc in range(31):
                z_ref[dc:dc + 1, :] = jnp.sum(jnp.where(rel == dc, xv, 0.0), axis=0, keepdims=True)
            z = z_ref[...]
            ind = (key_row + cls == dr).astype(BF16)
            hi = z.astype(BF16)
            lo = (z - hi.astype(F32)).astype(BF16)
            acc = acc + _dot_nn(hi, ind) + _dot_nn(lo, ind)
        o_ref[...] = acc

    return pl.pallas_call(
        body, name="na_dbias", grid=(8,),
        in_specs=[BS((None, NA_KR, GRID_W, kw), lambda h: (h, 0, 0, 0))],
        out_specs=BS((None, 32, 128), lambda h: (h, 0, 0)), out_shape=SDS((8, 32, 128), F32),
        scratch_shapes=[pltpu.VMEM((32, kw), F32)], compiler_params=_cp("parallel"),
    )(db)


def _merge_fwd(ya, yb, proj, wat, wbt):
    t = ya.shape[0]
    tm, tn = 512, 256
    ca = (QKV_A + QKV_B) // tn
    cb = ca + D // tn

    def body(ya_ref, yb_ref, la_ref, lb_ref, wa_ref, wb_ref, m_ref, za_ref, zb_ref):
        za = _dot_nt(ya_ref[...].astype(BF16), wa_ref[...])
        zb = _dot_nt(yb_ref[...].astype(BF16), wb_ref[...])
        m_ref[...] = (jax.nn.sigmoid(la_ref[...]) * za + jax.nn.sigmoid(lb_ref[...]) * zb).astype(BF16)
        za_ref[...] = za.astype(BF16)
        zb_ref[...] = zb.astype(BF16)

    out = BS((tm, tn), lambda i, j: (i, j))
    return pl.pallas_call(
        body, name="merge_fwd", grid=(t // tm, D // tn),
        in_specs=[BS((tm, 256), lambda i, j: (i, 0)), BS((tm, 512), lambda i, j: (i, 0)),
                  BS((tm, tn), lambda i, j: (i, ca + j)), BS((tm, tn), lambda i, j: (i, cb + j)),
                  BS((tn, 256), lambda i, j: (j, 0)), BS((tn, 512), lambda i, j: (j, 0))],
        out_specs=[out, out, out], out_shape=[SDS((t, D), BF16)] * 3,
        compiler_params=_cp("parallel", "parallel"),
    )(ya, yb, proj, proj, wat, wbt)


def _merge_bwd(dm, za, zb, proj):
    t = dm.shape[0]
    tm, tn = 512, 256
    ca = (QKV_A + QKV_B) // tn
    cb = ca + D // tn

    def body(dm_ref, za_ref, zb_ref, la_ref, lb_ref, dza_ref, dzb_ref, dl_ref):
        dmv = dm_ref[...]
        ga = jax.nn.sigmoid(la_ref[...])
        gb = jax.nn.sigmoid(lb_ref[...])
        dza_ref[...] = (dmv * ga).astype(BF16)
        dzb_ref[...] = (dmv * gb).astype(BF16)
        dl_ref[0] = (dmv * za_ref[...].astype(F32) * ga * (1.0 - ga)).astype(BF16)
        dl_ref[1] = (dmv * zb_ref[...].astype(F32) * gb * (1.0 - gb)).astype(BF16)

    blk = BS((tm, tn), lambda i, j: (i, j))
    return pl.pallas_call(
        body, name="merge_bwd", grid=(t // tm, D // tn),
        in_specs=[blk, blk, blk, BS((tm, tn), lambda i, j: (i, ca + j)), BS((tm, tn), lambda i, j: (i, cb + j))],
        out_specs=[blk, blk, BS((2, tm, tn), lambda i, j: (0, i, j))],
        out_shape=[SDS((t, D), BF16), SDS((t, D), BF16), SDS((2, t, D), BF16)],
        compiler_params=_cp("parallel", "parallel"),
    )(dm, za, zb, proj, proj)


def _sum_slots(recv0, recv1, tag):
    _, r, c = recv0.shape
    tr = r if r * c <= 512 * 1024 else r // 2

    def body(a_ref, b_ref, o_ref):
        for layer, ref in enumerate((a_ref, b_ref)):
            acc = ref[0].astype(F32)
            for s in range(1, N_DEV):
                acc = acc + ref[s].astype(F32)
            o_ref[layer] = acc

    blk = BS((N_DEV, tr, c), lambda i: (0, i, 0))
    return pl.pallas_call(
        body, name=f"sum_slots_{tag}", grid=(r // tr,), in_specs=[blk, blk],
        out_specs=BS((2, tr, c), lambda i: (0, i, 0)), out_shape=SDS((2, r, c), F32),
        compiler_params=_cp("parallel"),
    )(recv0, recv1)


def _adamw(w, g, m, v, tag):
    layers, r, c = w.shape
    tr = next(r // k for k in (1, 2, 4, 8) if r // k <= 384 and r % (8 * k) == 0)

    def body(w_ref, g_ref, m_ref, v_ref, d_ref, mo_ref, vo_ref):
        gv = g_ref[...]
        mn = ADAM_B1 * m_ref[...] + (1.0 - ADAM_B1) * gv
        vn = ADAM_B2 * v_ref[...] + (1.0 - ADAM_B2) * (gv * gv)
        m_hat = mn / (1.0 - ADAM_B1 ** ADAM_STEP)
        v_hat = vn / (1.0 - ADAM_B2 ** ADAM_STEP)
        d_ref[...] = -ADAM_LR * (m_hat / (jnp.sqrt(v_hat) + ADAM_EPS) + ADAM_WD * w_ref[...])
        mo_ref[...] = mn
        vo_ref[...] = vn

    blk = BS((None, tr, c), lambda l, i: (l, i, 0))
    return pl.pallas_call(
        body, name=f"adamw_{tag}", grid=(layers, r // tr), in_specs=[blk] * 4, out_specs=[blk] * 3,
        out_shape=[SDS((layers, r, c), F32)] * 3, compiler_params=_cp("parallel", "parallel"),
    )(w, g, m, v)


def _place():
    return lax.axis_index("x"), lax.axis_index("y"), lax.axis_index("c")


def _flip(coord, bit):
    return 1 - coord if bit else coord


def _allgather(shards, tag):
    n_arr = len(shards)
    hbm = BS(memory_space=pl.ANY)

    def body(*refs):
        ins, outs = refs[:n_arr], refs[n_arr:2 * n_arr]
        send_sems, recv_sems, local_sems = refs[2 * n_arr:]
        x, y, c = _place()
        me, sibling = (x, y, c), (x, y, 1 - c)
        chips = [(1 - x, y), (x, 1 - y), (1 - x, 1 - y)]

        def rows(a, p):
            r = shards[a].shape[0]
            return outs[a].at[pl.ds((4 * p[0] + 2 * p[1] + p[2]) * r, r), :]

        def copy(a, k, block, to, src=None):
            return pltpu.make_async_remote_copy(
                src_ref=rows(a, block) if src is None else src, dst_ref=rows(a, block),
                send_sem=send_sems.at[a, k], recv_sem=recv_sems.at[a, k], device_id=to, device_id_type=MESH)

        mine = [pltpu.make_async_copy(ins[a], rows(a, me), local_sems.at[a]) for a in range(n_arr)]
        for cp in mine:
            cp.start()
        first = []
        for a in range(n_arr):
            first.append(copy(a, 0, me, sibling, src=ins[a]))
            first += [copy(a, 1 + j, me, (*chip, c), src=ins[a]) for j, chip in enumerate(chips)]
        for cp in first:
            cp.start()
        passed = []
        for a in range(n_arr):
            for j, chip in enumerate(chips):
                copy(a, 1 + j, (*chip, c), me).wait_recv()
                passed.append(copy(a, 4 + j, (*chip, c), sibling))
                passed[-1].start()
        for a in range(n_arr):
            copy(a, 0, sibling, me).wait_recv()
            for j, chip in enumerate(chips):
                copy(a, 4 + j, (*chip, 1 - c), me).wait_recv()
        for cp in first + passed:
            cp.wait_send()
        for cp in mine:
            cp.wait()

    return pl.pallas_call(
        body, name=f"allgather_{tag}", in_specs=[hbm] * n_arr, out_specs=[hbm] * n_arr,
        out_shape=[SDS((N_DEV * s.shape[0], s.shape[1]), s.dtype) for s in shards],
        scratch_shapes=[pltpu.SemaphoreType.DMA((n_arr, 7)), pltpu.SemaphoreType.DMA((n_arr, 7)),
                        pltpu.SemaphoreType.DMA((n_arr,))],
        compiler_params=pltpu.CompilerParams(has_side_effects=True),
    )(*shards)


def _peers(x, y, c):
    peers = []
    for mask in range(1, N_DEV):
        p = (_flip(x, mask & 4), _flip(y, mask & 2), _flip(c, mask & 1))
        peers.append((p, 4 * p[0] + 2 * p[1] + p[2]))
    return peers


def _exchange_refs(mode, src, land, me, peer):
    if mode == "gather":
        r = src.shape[0]
        return src, land.at[pl.ds(me * r, r), :], land.at[pl.ds(peer * r, r), :]
    r = land.shape[1]
    return src.at[pl.ds(peer * r, r), :], land.at[me], land.at[peer]


HBM_SPEC = BS(memory_space=pltpu.HBM)
SEM_SPEC = BS(memory_space=pltpu.SEMAPHORE)
DATAFLOW = pltpu.SideEffectType.DATAFLOW_SIDE_EFFECTING


def _own_block_placed(mode, src, me):
    if mode == "gather":
        r, c = src.shape
        return lax.dynamic_update_slice(lax.empty((N_DEV * r, c), src.dtype), src, (me * r, 0))
    r, c = src.shape[0] // N_DEV, src.shape[1]
    own = lax.dynamic_slice(src, (me * r, 0), (r, c))
    return lax.dynamic_update_slice(lax.empty((N_DEV, r, c), src.dtype), own[None], (me, 0, 0))


def _exchange_start(mode, srcs, after, tag):
    n = len(srcs)
    x, y, c = _place()
    lands = [_own_block_placed(mode, s, 4 * x + 2 * y + c) for s in srcs]
    behind = [] if after is None else [after]

    def body(*refs):
        src_refs, land_refs = refs[:n], refs[n:2 * n]
        send_sems, recv_sems = refs[2 * n + len(behind)], refs[2 * n + len(behind) + 1]
        token = refs[-1]
        bx, by, bc = _place()
        me = 4 * bx + 2 * by + bc
        for a in range(n):
            for k, (p, idx) in enumerate(_peers(bx, by, bc)):
                out, there, _ = _exchange_refs(mode, src_refs[a], land_refs[a], me, idx)
                pltpu.make_async_remote_copy(
                    src_ref=out, dst_ref=there, send_sem=send_sems.at[7 * a + k], recv_sem=recv_sems.at[7 * a + k],
                    device_id=p, device_id_type=MESH).start()
        token[...] = jnp.zeros_like(token)

    res = pl.pallas_call(
        body, name=f"{mode}_start_{tag}",
        out_shape=(pltpu.SemaphoreType.DMA((7 * n,)), pltpu.SemaphoreType.DMA((7 * n,)),
                   *[pltpu.HBM(s.shape, s.dtype) for s in srcs], *[pltpu.HBM(l.shape, l.dtype) for l in lands],
                   SDS((8, 128), F32)),
        in_specs=[HBM_SPEC] * (2 * n) + [BS(memory_space=pl.ANY)] * len(behind),
        out_specs=(SEM_SPEC, SEM_SPEC, *[HBM_SPEC] * (2 * n), BS(memory_space=pltpu.VMEM)),
        input_output_aliases={i: 2 + i for i in range(2 * n)},
        compiler_params=pltpu.CompilerParams(has_side_effects=DATAFLOW),
    )(*[pltpu.with_memory_space_constraint(s, pltpu.HBM) for s in srcs],
      *[pltpu.with_memory_space_constraint(l, pltpu.HBM) for l in lands], *behind)
    return (mode, res[0], res[1], res[2:2 + n], res[2 + n:2 + 2 * n]), res[-1]


def _exchange_wait(handle, after, tag):
    mode, send_sems, recv_sems, srcs, lands = handle
    n = len(srcs)

    def body(*refs):
        src_refs, land_refs = refs[:n], refs[n:2 * n]
        send_ref, recv_ref = refs[2 * n], refs[2 * n + 1]
        bx, by, bc = _place()
        me = 4 * bx + 2 * by + bc
        for a in range(n):
            for k, (p, idx) in enumerate(_peers(bx, by, bc)):
                out, _, here = _exchange_refs(mode, src_refs[a], land_refs[a], me, idx)
                cp = pltpu.make_async_remote_copy(
                    src_ref=out, dst_ref=here, send_sem=send_ref.at[7 * a + k], recv_sem=recv_ref.at[7 * a + k],
                    device_id=p, device_id_type=MESH)
                cp.wait_send()
                cp.wait_recv()

    res = pl.pallas_call(
        body, name=f"{mode}_wait_{tag}",
        out_shape=(*[pltpu.HBM(s.shape, s.dtype) for s in srcs], *[pltpu.HBM(l.shape, l.dtype) for l in lands]),
        in_specs=[HBM_SPEC] * (2 * n) + [SEM_SPEC, SEM_SPEC, BS(memory_space=pl.ANY)],
        out_specs=tuple([HBM_SPEC] * (2 * n)),
        input_output_aliases={i: i for i in range(2 * n)},
        compiler_params=pltpu.CompilerParams(has_side_effects=DATAFLOW),
    )(*srcs, *lands, send_sems, recv_sems, after)
    return list(res[n:])


def _allreduce_small(vec, behind):
    rows = vec.shape[0]

    def body(x_ref, behind_ref, o_ref, buf_ref, send_sems, recv_sems):
        x, y, c = _place()
        me = 4 * x + 2 * y + c
        buf_ref[me] = x_ref[...]
        peers = _peers(x, y, c)

        def copy(k, slot):
            return pltpu.make_async_remote_copy(
                src_ref=x_ref, dst_ref=buf_ref.at[slot], send_sem=send_sems.at[k], recv_sem=recv_sems.at[k],
                device_id=peers[k][0], device_id_type=MESH)

        sends = [copy(k, me) for k in range(N_DEV - 1)]
        for cp in sends:
            cp.start()
        for k in range(N_DEV - 1):
            copy(k, peers[k][1]).wait_recv()
        for cp in sends:
            cp.wait_send()
        acc = buf_ref[0]
        for s in range(1, N_DEV):
            acc = acc + buf_ref[s]
        o_ref[...] = acc

    vmem = BS(memory_space=pltpu.VMEM)
    return pl.pallas_call(
        body, name="allreduce_small", in_specs=[vmem, BS(memory_space=pl.ANY)], out_specs=vmem,
        out_shape=SDS((rows, 128), F32),
        scratch_shapes=[pltpu.VMEM((N_DEV, rows, 128), F32), pltpu.SemaphoreType.DMA((7,)),
                        pltpu.SemaphoreType.DMA((7,))],
        compiler_params=pltpu.CompilerParams(has_side_effects=True),
    )(vec, behind)


def _ffn_forward(x, norm_g, fetch, names, tag):
    hn = _norm_fwd(x, norm_g, tag)
    gu, act = _ffn_up(hn, fetch(names[0], hn).reshape(2, F, D), tag)
    out = _mm_nn(act[None], fetch(names[1], act)[None], f"down_{tag}", res=x, scale=0.5)
    return out, (x, hn, gu, act)


def _ffn_backward(dxo, saved, norm_g, wut, wd, tag, send):
    x, hn, gu, act = saved
    du = _ffn_dact(dxo, wd, gu, tag)
    d_wd = _mm_tn(act[None], dxo, f"dwd_{tag}", scale=0.5, tk=2048)[0]
    d_wut = _mm_tn(du, hn, f"dwu_{tag}")
    zero = send([d_wut.reshape(2 * F, D), d_wd])
    dhn = _mm_nn(du, wut, f"dhn_{tag}", tn=512)
    return _norm_bwd(x, norm_g + zero, dhn, dxo, tag)


def _mixer_forward(x, norm_g, fetch, bias, tables, tag):
    hn = _norm_fwd(x, norm_g, tag)
    proj = _mm_nt_rows(hn, fetch("win", hn), f"proj_{tag}", 512, IN_W // 2, IN_W, 0)
    qkr = _rope_fwd(proj, *tables)
    outs, lses = [], []
    for grp in range(3):
        o, l = _dil_fwd(qkr, proj, grp)
        outs.append(o)
        lses.append(l)
    ya = _combine_fwd(outs, lses)
    yb, lse_b = _na_fwd(proj, bias)
    merged, za, zb = _merge_fwd(ya, yb, proj, fetch("wa", yb), fetch("wb", yb))
    out = _mm_nn(merged[None], fetch("wo", merged)[None], f"out_{tag}", res=x)
    return out, (x, hn, proj, qkr, outs, lses, ya, yb, lse_b, merged, za, zb)


def _mixer_backward(dxo, saved, norm_g, w, bias, tables, tag, send):
    wint, wat, wbt, wo = w
    x, hn, proj, qkr, outs, lses, ya, yb, lse_b, merged, za, zb = saved
    dm = _mm_nt_rows(dxo, wo, f"dmerged_{tag}", 512, D, D, 0)
    d_wo = _mm_tn(merged[None], dxo, f"dwo_{tag}", tk=2048)[0]
    dza, dzb, dlog = _merge_bwd(dm, za, zb, proj)
    dya = _mm_nn(dza[None], wat[None], f"dya_{tag}")
    dyb = _mm_nn(dzb[None], wbt[None], f"dyb_{tag}")
    d_wat = _mm_tn(dza[None], ya, f"dwa_{tag}")[0]
    d_wbt = _mm_tn(dzb[None], yb, f"dwb_{tag}")[0]
    cb = _combine_bwd(dya, outs, lses)
    dqs, dks, dvs = [], [], []
    for grp in range(3):
        dq, dk, dv = _dil_bwd(qkr, proj, cb[grp], cb[3 + grp], lses[grp], grp)
        dqs.append(dq)
        dks.append(dk)
        dvs.append(dv)
    dqk = _rope_bwd(dqs, dks, *tables)
    dqb, dkb, dvb, dbias_tab = _na_bwd(proj, bias, dyb, yb, lse_b)
    dbias = _na_dbias(dbias_tab)
    dproj = jnp.concatenate(
        [dqk] + [t.astype(BF16) for t in (*dvs, dqb, dkb, dvb)] + [dlog[0], dlog[1]], axis=1)
    d_wint = _mm_tn(dproj[None], hn, f"dwin_{tag}")[0]
    zero = send([d_wint, d_wat, d_wbt, d_wo])
    dhn = _mm_nn(dproj[None], wint[None], f"dhnm_{tag}", tm=256, tn=512)
    dx, dg = _norm_bwd(x, norm_g + zero, dhn, dxo, f"mix_{tag}")
    dbias = dbias[:, :31, :15].transpose(0, 2, 1)
    return dx, dg, dbias


def _pack_small(norms, biases, final, loss=None):
    parts = []
    for layer in range(DEPTH):
        parts += [norms[0][layer], norms[1][layer], norms[2][layer],
                  jnp.pad(biases[layer].reshape(-1), (0, BIAS_PAD - 8 * 15 * 31))]
    parts.append(final)
    flat = jnp.concatenate([p.reshape(-1).astype(F32) for p in parts])
    if loss is not None:
        flat = jnp.concatenate([flat, loss.reshape(-1)])
    return jnp.pad(flat, (0, SMALL_ROWS * 128 - flat.shape[0])).reshape(SMALL_ROWS, 128)


def _unpack_small(packed):
    flat = packed.reshape(-1)
    norms, biases = ([], [], []), []
    pos = 0
    for _ in range(DEPTH):
        for k in range(3):
            norms[k].append(flat[pos:pos + D])
            pos += D
        biases.append(flat[pos:pos + 8 * 15 * 31].reshape(8, 15, 31))
        pos += BIAS_PAD
    final = flat[pos:pos + D]
    pos += D
    return [jnp.stack(n) for n in norms], jnp.stack(biases), final, flat[pos]


def kernel(x, ffn1_norm, ffn1_w_up, ffn1_w_down, mix_norm, w_in, na_rel_bias, w_branch_a, w_branch_b, w_out, ffn2_norm, ffn2_w_up, ffn2_w_down, final_norm, loss_target, m_ffn1_norm, m_ffn1_w_up, m_ffn1_w_down, m_mix_norm, m_w_in, m_na_rel_bias, m_w_branch_a, m_w_branch_b, m_w_out, m_ffn2_norm, m_ffn2_w_up, m_ffn2_w_down, m_final_norm, v_ffn1_norm, v_ffn1_w_up, v_ffn1_w_down, v_mix_norm, v_w_in, v_na_rel_bias, v_w_branch_a, v_w_branch_b, v_w_out, v_ffn2_norm, v_ffn2_w_up, v_ffn2_w_down, v_final_norm):
    t = x.shape[0] * x.shape[1]
    xs = x.reshape(t, D)
    tgt = loss_target.reshape(t, D)
    tables = _rope_tables()

    col_sharded = dict(up1=ffn1_w_up, win=w_in, wa=w_branch_a, wb=w_branch_b, up2=ffn2_w_up)
    row_sharded = dict(down1=ffn1_w_down, wo=w_out, down2=ffn2_w_down)
    sublayers = (("up1", "down1"), ("win", "wa", "wb", "wo"), ("up2", "down2"))
    shard = [{} for _ in range(DEPTH)]
    for layer in range(DEPTH):
        for name, arr in col_sharded.items():
            shard[layer][name] = arr[layer].T.astype(BF16)
        for name, arr in row_sharded.items():
            shard[layer][name] = arr[layer].astype(BF16)

    weights = [{} for _ in range(DEPTH)]
    weights[0]["up1"] = _allgather([shard[0]["up1"]], "first")[0]
    travel = [(0, ("down1",)), (0, ("win",)), (0, ("wa", "wb", "wo")), (0, ("up2", "down2")),
              (1, ("up1", "down1")), (1, ("win",)), (1, ("wa", "wb", "wo")), (1, ("up2", "down2"))]
    pending = {}
    after = weights[0]["up1"]
    for i, (layer, names) in enumerate(travel):
        handle, after = _exchange_start("gather", [shard[layer][n] for n in names], after, f"w{i}")
        for n in names:
            pending[layer, n] = (i, handle, names)
    zero = after[0, 0]

    def fetcher(layer):
        def fetch(name, behind):
            if (layer, name) in pending:
                i, handle, names = pending[layer, name]
                for n, got in zip(names, _exchange_wait(handle, behind, f"w{i}")):
                    weights[layer][n] = got
                    del pending[layer, n]
            return weights[layer][name]
        return fetch

    saved = []
    h = xs
    for layer in range(DEPTH):
        bias = _na_bias_table(na_rel_bias[layer])
        fetch = fetcher(layer)
        h, s1 = _ffn_forward(h, ffn1_norm[layer] + zero, fetch, ("up1", "down1"), f"f1l{layer}")
        h, s2 = _mixer_forward(h, mix_norm[layer], fetch, bias, tables, f"l{layer}")
        h, s3 = _ffn_forward(h, ffn2_norm[layer], fetch, ("up2", "down2"), f"f2l{layer}")
        saved.append((s1, s2, s3, bias))
    loss_part, dh, d_final = _loss_head(h, final_norm, tgt)

    d_norms = ([None] * DEPTH, [None] * DEPTH, [None] * DEPTH)
    d_bias = [None] * DEPTH
    sent = {}

    def sender(layer, k):
        def send(grads):
            sent[layer, k], token = _exchange_start("scatter", grads, None, f"g{layer}{k}")
            return token[0, 0]
        return send

    for layer in reversed(range(DEPTH)):
        w = weights[layer]
        s1, s2, s3, bias = saved[layer]
        dh, d_norms[2][layer] = _ffn_backward(
            dh, s3, ffn2_norm[layer], w["up2"].reshape(2, F, D), w["down2"], f"f2l{layer}", sender(layer, 2))
        dh, d_norms[1][layer], d_bias[layer] = _mixer_backward(
            dh, s2, mix_norm[layer], (w["win"], w["wa"], w["wb"], w["wo"]), bias, tables, f"l{layer}", sender(layer, 1))
        dh, d_norms[0][layer] = _ffn_backward(
            dh, s1, ffn1_norm[layer], w["up1"].reshape(2, F, D), w["down1"], f"f1l{layer}", sender(layer, 0))
    grad_x = dh.reshape(x.shape)

    originals = dict(up1=(ffn1_w_up, m_ffn1_w_up, v_ffn1_w_up), down1=(ffn1_w_down, m_ffn1_w_down, v_ffn1_w_down),
                     win=(w_in, m_w_in, v_w_in), wa=(w_branch_a, m_w_branch_a, v_w_branch_a),
                     wb=(w_branch_b, m_w_branch_b, v_w_branch_b), wo=(w_out, m_w_out, v_w_out),
                     up2=(ffn2_w_up, m_ffn2_w_up, v_ffn2_w_up), down2=(ffn2_w_down, m_ffn2_w_down, v_ffn2_w_down))
    big = {}
    behind = dh
    for k in (2, 1, 0):
        recv = [_exchange_wait(sent[layer, k], behind, f"g{layer}{k}") for layer in (1, 0)]
        for i, name in enumerate(sublayers[k]):
            g = _sum_slots(recv[1][i], recv[0][i], name)
            wv, mv, vv = originals[name]
            if name in col_sharded:
                wv, mv, vv = (jnp.swapaxes(t, 1, 2) for t in (wv, mv, vv))
            big[name] = (g, *_adamw(wv, g, mv, vv, name))
            behind = big[name][1]
            if name in col_sharded:
                big[name] = tuple(jnp.swapaxes(t, 1, 2) for t in big[name])

    small = _allreduce_small(_pack_small(d_norms, d_bias, d_final, loss_part[0, :1]), behind)
    g_norms, g_bias, g_final, loss = _unpack_small(small)
    w_small = _pack_small((ffn1_norm, mix_norm, ffn2_norm), na_rel_bias, final_norm)
    m_small = _pack_small((m_ffn1_norm, m_mix_norm, m_ffn2_norm), m_na_rel_bias, m_final_norm)
    v_small = _pack_small((v_ffn1_norm, v_mix_norm, v_ffn2_norm), v_na_rel_bias, v_final_norm)
    upd = _adamw(w_small[None], small[None], m_small[None], v_small[None], "small")
    small_out = [(g_norms, g_bias, g_final)] + [_unpack_small(u[0])[:3] for u in upd]

    outputs = [loss, grad_x]
    for kind in range(4):
        norms, bias_k, final_k = small_out[kind]
        outputs += [norms[0], big["up1"][kind], big["down1"][kind], norms[1], big["win"][kind], bias_k,
                    big["wa"][kind], big["wb"][kind], big["wo"][kind], norms[2], big["up2"][kind],
                    big["down2"][kind], final_k]
    return tuple(outputs)
```

```python
import numpy as np

import jax
import jax.numpy as jnp
from jax import lax
from jax.experimental import pallas as pl
from jax.experimental.pallas import tpu as pltpu

F32 = jnp.float32
BF16 = jnp.bfloat16
SDS = jax.ShapeDtypeStruct
BS = pl.BlockSpec
MESH = pl.DeviceIdType.MESH

D = 1024
S = 2048
F = 2816
DEPTH = 2
HEAD_DIM = 64
DILATIONS = (1, 4, 16)
HALF = 64
QKV_A = 2304
QKV_B = 1536
IN_W = 5888
N_DEV = 8
NA_ROWS = 32
GRID_W = 64
NA_KR = 8
ROPE_THETA = 10000.0
RMS_EPS = 1e-6
NEG = -1e30
SCALE = HEAD_DIM ** -0.5
ADAM_LR, ADAM_B1, ADAM_B2, ADAM_EPS, ADAM_WD, ADAM_STEP = 0.001, 0.9, 0.999, 1e-08, 0.01, 10
VMEM_LIMIT_V7X = 52 * 1024 * 1024
SMALL_ROWS = 120
BIAS_PAD = 3840
NA_FWD_ROWS = 4
NA_BWD_ROWS = 2
DIL_FWD_TILES = 4
DIL_BWD_TILES = 2


def _cp(*sem):
    return pltpu.CompilerParams(dimension_semantics=sem, vmem_limit_bytes=VMEM_LIMIT_V7X)


def _dot_nn(a, b):
    return jnp.dot(a, b, preferred_element_type=F32)


def _dot_nt(a, b):
    return lax.dot_general(a, b, (((1,), (1,)), ((), ())), preferred_element_type=F32)


def _dot_tn(a, b):
    return lax.dot_general(a, b, (((0,), (0,)), ((), ())), preferred_element_type=F32)


def _ds(start, size, stride):
    return pl.ds(start, size) if stride == 1 else pl.ds(start, size, stride=stride)


def _norm_fwd(x, g, tag):
    t = x.shape[0]
    tm = 512

    def body(x_ref, g_ref, o_ref):
        xv = x_ref[...]
        r = lax.rsqrt(jnp.mean(xv * xv, axis=-1, keepdims=True) + RMS_EPS)
        o_ref[...] = (xv * r * g_ref[...]).astype(BF16)

    return pl.pallas_call(
        body, name=f"norm_fwd_{tag}", grid=(t // tm,),
        in_specs=[BS((tm, D), lambda i: (i, 0)), BS((1, D), lambda i: (0, 0))],
        out_specs=BS((tm, D), lambda i: (i, 0)),
        out_shape=SDS((t, D), BF16), compiler_params=_cp("parallel"),
    )(x, g.reshape(1, D))


def _norm_bwd(x, g, dh, dres, tag):
    t = x.shape[0]
    tm = 512

    def body(x_ref, g_ref, dh_ref, dr_ref, dx_ref, dxb_ref, dg_ref):
        @pl.when(pl.program_id(0) == 0)
        def _():
            dg_ref[...] = jnp.zeros_like(dg_ref)

        xv = x_ref[...]
        r = lax.rsqrt(jnp.mean(xv * xv, axis=-1, keepdims=True) + RMS_EPS)
        xh = xv * r
        dh = dh_ref[...]
        u = dh * g_ref[...]
        dx = dr_ref[...] + r * (u - xh * jnp.mean(xh * u, axis=-1, keepdims=True))
        dx_ref[...] = dx
        dxb_ref[...] = dx.astype(BF16)
        dg_ref[...] += jnp.sum(dh * xh, axis=0, keepdims=True)

    row = BS((tm, D), lambda i: (i, 0))
    vec = BS((1, D), lambda i: (0, 0))
    return pl.pallas_call(
        body, name=f"norm_bwd_{tag}", grid=(t // tm,),
        in_specs=[row, vec, row, row], out_specs=[row, row, vec],
        out_shape=[SDS((t, D), F32), SDS((t, D), BF16), SDS((1, D), F32)], compiler_params=_cp("arbitrary"),
    )(x, g.reshape(1, D), dh, dres)


def _loss_head(x, g, tgt):
    t = x.shape[0]
    tm = 512

    def body(x_ref, g_ref, t_ref, loss_ref, dx_ref, dxb_ref, dg_ref):
        @pl.when(pl.program_id(0) == 0)
        def _():
            dg_ref[...] = jnp.zeros_like(dg_ref)
            loss_ref[...] = jnp.zeros_like(loss_ref)

        xv = x_ref[...]
        gv = g_ref[...]
        r = lax.rsqrt(jnp.mean(xv * xv, axis=-1, keepdims=True) + RMS_EPS)
        xh = xv * r
        e = xh * gv - t_ref[...]
        loss_ref[...] += 0.5 * jnp.sum(jnp.mean(e * e, axis=-1, keepdims=True), axis=0, keepdims=True)
        dy = e * (1.0 / D)
        u = dy * gv
        dx = r * (u - xh * jnp.mean(xh * u, axis=-1, keepdims=True))
        dx_ref[...] = dx
        dxb_ref[...] = dx.astype(BF16)
        dg_ref[...] += jnp.sum(dy * xh, axis=0, keepdims=True)

    row = BS((tm, D), lambda i: (i, 0))
    vec = BS((1, D), lambda i: (0, 0))
    return pl.pallas_call(
        body, name="loss_head", grid=(t // tm,),
        in_specs=[row, vec, row], out_specs=[BS((1, 128), lambda i: (0, 0)), row, row, vec],
        out_shape=[SDS((1, 128), F32), SDS((t, D), F32), SDS((t, D), BF16), SDS((1, D), F32)],
        compiler_params=_cp("arbitrary"),
    )(x, g.reshape(1, D), tgt)


def _mm_nn(a, w, tag, res=None, scale=1.0, tm=512, tn=None):
    c_n, t, k = a.shape
    n = w.shape[2]
    tn = n if tn is None else tn

    def body(*refs):
        a_ref, w_ref = refs[0], refs[1]
        o_ref = refs[-1]
        acc = _dot_nn(a_ref[0].astype(BF16), w_ref[0])
        for c in range(1, c_n):
            acc = acc + _dot_nn(a_ref[c].astype(BF16), w_ref[c])
        if scale != 1.0:
            acc = acc * scale
        if res is not None:
            acc = refs[2][...] + acc
        o_ref[...] = acc

    in_specs = [BS((c_n, tm, k), lambda i, j: (0, i, 0)), BS((c_n, k, tn), lambda i, j: (0, 0, j))]
    args = [a, w]
    if res is not None:
        in_specs.append(BS((tm, tn), lambda i, j: (i, j)))
        args.append(res)
    return pl.pallas_call(
        body, name=f"mm_nn_{tag}", grid=(t // tm, n // tn), in_specs=in_specs,
        out_specs=BS((tm, tn), lambda i, j: (i, j)), out_shape=SDS((t, n), F32),
        compiler_params=_cp("parallel", "parallel"),
    )(*args)


def _mm_nt_rows(a, w, tag, tm, tn, n_total, w_row0):
    t, k = a.shape
    assert w_row0 % tn == 0 and n_total % tn == 0
    j0 = w_row0 // tn

    def body(a_ref, w_ref, o_ref):
        o_ref[...] = _dot_nt(a_ref[...].astype(BF16), w_ref[...])

    return pl.pallas_call(
        body, name=f"mm_nt_{tag}", grid=(t // tm, n_total // tn),
        in_specs=[BS((tm, k), lambda i, j: (i, 0)), BS((tn, k), lambda i, j: (j0 + j, 0))],
        out_specs=BS((tm, tn), lambda i, j: (i, j)), out_shape=SDS((t, n_total), F32),
        compiler_params=_cp("parallel", "parallel"),
    )(a, w)


def _mm_tn(a, b, tag, scale=1.0, tmm=256, tk=None):
    c_n, t, m = a.shape
    n = b.shape[1]
    tk = t if tk is None else tk
    nk = t // tk

    def body_one(a_ref, b_ref, o_ref):
        o_ref[...] = (_dot_tn(a_ref[...].astype(BF16), b_ref[...].astype(BF16)) * scale).astype(BF16)

    def body_acc(a_ref, b_ref, o_ref, acc_ref):
        kk = pl.program_id(2)

        @pl.when(kk == 0)
        def _():
            acc_ref[...] = jnp.zeros_like(acc_ref)

        acc_ref[...] += _dot_tn(a_ref[...].astype(BF16), b_ref[...].astype(BF16))

        @pl.when(kk == nk - 1)
        def _():
            o_ref[...] = (acc_ref[...] * scale).astype(BF16)

    return pl.pallas_call(
        body_one if nk == 1 else body_acc, name=f"mm_tn_{tag}", grid=(c_n, m // tmm, nk),
        in_specs=[BS((None, tk, tmm), lambda c, mi, kk: (c, kk, mi)), BS((tk, n), lambda c, mi, kk: (kk, 0))],
        out_specs=BS((None, tmm, n), lambda c, mi, kk: (c, mi, 0)),
        out_shape=SDS((c_n, m, n), BF16), scratch_shapes=[] if nk == 1 else [pltpu.VMEM((tmm, n), F32)],
        compiler_params=_cp("parallel", "parallel", "arbitrary"),
    )(a, b)


def _ffn_up(hn, wut, tag):
    t = hn.shape[0]
    tm, tn = 512, 1408

    def body(h_ref, w_ref, gu_ref, act_ref):
        h = h_ref[...]
        g = _dot_nt(h, w_ref[0])
        u = _dot_nt(h, w_ref[1])
        gu_ref[0] = g.astype(BF16)
        gu_ref[1] = u.astype(BF16)
        act_ref[...] = (g * jax.nn.sigmoid(g) * u).astype(BF16)

    return pl.pallas_call(
        body, name=f"ffn_up_{tag}", grid=(t // tm, F // tn),
        in_specs=[BS((tm, D), lambda i, j: (i, 0)), BS((2, tn, D), lambda i, j: (0, j, 0))],
        out_specs=[BS((2, tm, tn), lambda i, j: (0, i, j)), BS((tm, tn), lambda i, j: (i, j))],
        out_shape=[SDS((2, t, F), BF16), SDS((t, F), BF16)],
        compiler_params=_cp("parallel", "parallel"),
    )(hn, wut)


def _ffn_dact(dxo, wd, gu, tie, tag):
    t = dxo.shape[0]
    tm, tn = 512, 1408

    def body(d_ref, w_ref, gu_ref, tie_ref, o_ref):
        dact = _dot_nt(d_ref[...] * 0.5, w_ref[...])
        g = gu_ref[0].astype(F32)
        u = gu_ref[1].astype(F32)
        sg = 0.5 * jnp.tanh(0.5 * g) + 0.5
        o_ref[0] = (dact * u * (sg * (1.0 + g * (1.0 - sg)))).astype(BF16)
        o_ref[1] = (dact * (g * sg)).astype(BF16)

    return pl.pallas_call(
        body, name=f"ffn_dact_{tag}", grid=(t // tm, F // tn),
        in_specs=[BS((tm, D), lambda i, j: (i, 0)), BS((tn, D), lambda i, j: (j, 0)),
                  BS((2, tm, tn), lambda i, j: (0, i, j)), BS((8, 128), lambda i, j: (0, 0))],
        out_specs=BS((2, tm, tn), lambda i, j: (0, i, j)),
        out_shape=SDS((2, t, F), BF16), compiler_params=_cp("parallel", "parallel"),
    )(dxo, wd, gu, tie)


def _rope_tables():
    half = HEAD_DIM // 2
    inv_freq = ROPE_THETA ** (-jnp.arange(half, dtype=F32) / half)
    ang = jnp.arange(S).astype(F32)[:, None] * inv_freq[None, :]
    cos, sin = jnp.cos(ang), jnp.sin(ang)
    return jnp.concatenate([cos, cos, cos, cos], axis=1), jnp.concatenate([-sin, sin, -sin, sin], axis=1)


def _swap_halves(t, first_half):
    return jnp.where(first_half, pltpu.roll(t, 96, 1), pltpu.roll(t, 32, 1))


def _rope_fwd(proj, cos_t, sin_t):
    t = proj.shape[0]
    tm = 512
    width = 2 * QKV_A // 3

    def body(x_ref, c_ref, s_ref, o_ref):
        c = c_ref[...]
        sg = s_ref[...]
        first = (lax.broadcasted_iota(jnp.int32, (tm, 128), 1) % HEAD_DIM) < HEAD_DIM // 2
        for j in range(width // 128):
            v = x_ref[:, 128 * j:128 * (j + 1)]
            o_ref[:, 128 * j:128 * (j + 1)] = v * c + _swap_halves(v, first) * sg

    tab = BS((tm, 128), lambda i: (i % (S // tm), 0))
    return pl.pallas_call(
        body, name="rope_fwd", grid=(t // tm,),
        in_specs=[BS((tm, width), lambda i: (i, 0)), tab, tab],
        out_specs=BS((tm, width), lambda i: (i, 0)), out_shape=SDS((t, width), F32),
        compiler_params=_cp("parallel"),
    )(proj, cos_t, sin_t)


def _rope_bwd(dqs, dks, cos_t, sin_t):
    t = dqs[0].shape[0]
    tm = 512

    def body(*refs):
        c = refs[6][...]
        sg = refs[7][...]
        o_ref = refs[8]
        first = (lax.broadcasted_iota(jnp.int32, (tm, 128), 1) % HEAD_DIM) < HEAD_DIM // 2
        for a in range(6):
            for hp in range(2):
                v = refs[a][:, 128 * hp:128 * (hp + 1)]
                col = 128 * (2 * a + hp)
                o_ref[:, col:col + 128] = (v * c + _swap_halves(v * sg, first)).astype(BF16)

    blk = BS((tm, 256), lambda i: (i, 0))
    tab = BS((tm, 128), lambda i: (i % (S // tm), 0))
    return pl.pallas_call(
        body, name="rope_bwd", grid=(t // tm,), in_specs=[blk] * 6 + [tab, tab],
        out_specs=BS((tm, 1536), lambda i: (i, 0)), out_shape=SDS((t, 1536), BF16),
        compiler_params=_cp("parallel"),
    )(*dqs, *dks, cos_t, sin_t)


def _head_masks():
    lane = lax.broadcasted_iota(jnp.int32, (1, 128), 1)
    m0 = (lane < HEAD_DIM).astype(F32)
    return m0, 1.0 - m0


def _dil_geometry(d):
    sub = S // d
    q_rows = 128
    k_rows = min(256, sub)
    return sub, q_rows, sub // q_rows, k_rows


def _dil_tile(idx, d):
    sub, q_rows, nb, k_rows = _dil_geometry(d)
    r = idx // nb
    n = idx % nb
    k_sub = jnp.clip(q_rows * n - HALF, 0, sub - k_rows)
    if d == 1:
        q_start = pl.multiple_of(q_rows * n, q_rows)
        k_start = pl.multiple_of(k_sub, HALF)
    else:
        q_start = q_rows * n * d + r
        k_start = k_sub * d + r
    ii = lax.broadcasted_iota(jnp.int32, (q_rows, k_rows), 0)
    jj = lax.broadcasted_iota(jnp.int32, (q_rows, k_rows), 1)
    valid = jnp.abs(jj - ii + (k_sub - q_rows * n)) <= HALF
    return q_start, k_start, valid


def _dil_specs(grp):
    qs = BS((S, 128), lambda b, hp: (b, 2 * grp + hp))
    ks = BS((S, 128), lambda b, hp: (b, 6 + 2 * grp + hp))
    vs = BS((S, 128), lambda b, hp: (b, 12 + 2 * grp + hp))
    own = BS((S, 128), lambda b, hp: (b, hp))
    return qs, ks, vs, own


def _dil_fwd(qkr, proj, grp):
    t = qkr.shape[0]
    d = DILATIONS[grp]
    _, q_rows, nb, k_rows = _dil_geometry(d)

    def body(q_ref, k_ref, v_ref, o_ref, l_ref):
        masks = _head_masks()

        def step(i0, carry):
            geo = [_dil_tile(i0 * DIL_FWD_TILES + j, d) for j in range(DIL_FWD_TILES)]
            tiles = [(j, h) for j in range(DIL_FWD_TILES) for h in range(2)]
            qs = [q_ref[_ds(g[0], q_rows, d), :] for g in geo]
            kbs = [k_ref[_ds(g[1], k_rows, d), :].astype(BF16) for g in geo]
            ss = [jnp.where(geo[j][2], _dot_nt((qs[j] * masks[h]).astype(BF16), kbs[j]) * SCALE, NEG) for j, h in tiles]
            mxs = [jnp.max(s, axis=1, keepdims=True) for s in ss]
            ps = [jnp.exp(s - mx) for s, mx in zip(ss, mxs)]
            dens = [jnp.sum(p, axis=1, keepdims=True) for p in ps]
            vs = [v_ref[_ds(g[1], k_rows, d), :] for g in geo]
            outs = [_dot_nn(p.astype(BF16), (vs[j] * masks[h]).astype(BF16)) / den
                    for p, den, (j, h) in zip(ps, dens, tiles)]
            for j, g in enumerate(geo):
                o_ref[_ds(g[0], q_rows, d), :] = outs[2 * j] + outs[2 * j + 1]
                l_ref[_ds(g[0], q_rows, d), :] = (
                    (mxs[2 * j] + jnp.log(dens[2 * j])) * masks[0] + (mxs[2 * j + 1] + jnp.log(dens[2 * j + 1])) * masks[1])
            return carry

        lax.fori_loop(0, d * nb // DIL_FWD_TILES, step, 0)

    qs, ks, vs, own = _dil_specs(grp)
    return pl.pallas_call(
        body, name=f"dil_fwd_{grp}", grid=(t // S, 2), in_specs=[qs, ks, vs], out_specs=[own, own],
        out_shape=[SDS((t, 256), F32), SDS((t, 256), F32)], compiler_params=_cp("parallel", "parallel"),
    )(qkr, qkr, proj)


def _dil_bwd(qkr, proj, do, dlp, lse, grp):
    t = qkr.shape[0]
    d = DILATIONS[grp]
    _, q_rows, nb, k_rows = _dil_geometry(d)

    def body(q_ref, k_ref, v_ref, do_ref, dl_ref, l_ref, dq_ref, dk_ref, dv_ref):
        masks = _head_masks()
        dk_ref[...] = jnp.zeros_like(dk_ref)
        dv_ref[...] = jnp.zeros_like(dv_ref)

        def step(i0, carry):
            geo = [_dil_tile(i0 * DIL_BWD_TILES + j, d) for j in range(DIL_BWD_TILES)]
            tiles = [(j, h) for j in range(DIL_BWD_TILES) for h in range(2)]
            q_ds = [_ds(g[0], q_rows, d) for g in geo]
            k_ds = [_ds(g[1], k_rows, d) for g in geo]
            qs = [q_ref[r, :] for r in q_ds]
            ks = [k_ref[r, :] for r in k_ds]
            kbs = [k.astype(BF16) for k in ks]
            vbs = [v_ref[r, :].astype(BF16) for r in k_ds]
            dos = [do_ref[r, :] for r in q_ds]
            dls = [dl_ref[r, :] for r in q_ds]
            lss = [l_ref[r, :] for r in q_ds]
            qhs = [(qs[j] * masks[h]).astype(BF16) for j, h in tiles]
            dohs = [(dos[j] * masks[h]).astype(BF16) for j, h in tiles]
            ss = [jnp.where(geo[j][2], _dot_nt(qh, kbs[j]) * SCALE, NEG) for qh, (j, h) in zip(qhs, tiles)]
            ps = [jnp.exp(s - lss[j][:, HEAD_DIM * h:HEAD_DIM * h + 1]) for s, (j, h) in zip(ss, tiles)]
            dps = [_dot_nt(doh, vbs[j]) for doh, (j, h) in zip(dohs, tiles)]
            dss = [(p * (dp - dls[j][:, HEAD_DIM * h:HEAD_DIM * h + 1])).astype(BF16)
                   for p, dp, (j, h) in zip(ps, dps, tiles)]
            dqs = [_dot_nn(ds, (ks[j] * masks[h]).astype(BF16)) for ds, (j, h) in zip(dss, tiles)]
            dkws = [_dot_tn(ds, qh) for ds, qh in zip(dss, qhs)]
            dvws = [_dot_tn(p.astype(BF16), doh) for p, doh in zip(ps, dohs)]
            for j in range(DIL_BWD_TILES):
                dq_ref[q_ds[j], :] = (dqs[2 * j] + dqs[2 * j + 1]) * SCALE
                dk_ref[k_ds[j], :] += (dkws[2 * j] + dkws[2 * j + 1]) * SCALE
                dv_ref[k_ds[j], :] += dvws[2 * j] + dvws[2 * j + 1]
            return carry

        lax.fori_loop(0, d * nb // DIL_BWD_TILES, step, 0)

    qs, ks, vs, own = _dil_specs(grp)
    return pl.pallas_call(
        body, name=f"dil_bwd_{grp}", grid=(t // S, 2), in_specs=[qs, ks, vs, own, own, own],
        out_specs=[own, own, own], out_shape=[SDS((t, 256), F32)] * 3,
        compiler_params=_cp("parallel", "parallel"),
    )(qkr, qkr, proj, do, dlp, lse)


def _mix_weights(l0, l1, l2):
    mx = jnp.maximum(jnp.maximum(l0, l1), l2)
    e0, e1, e2 = jnp.exp(l0 - mx), jnp.exp(l1 - mx), jnp.exp(l2 - mx)
    den = e0 + e1 + e2
    return e0 / den, e1 / den, e2 / den


def _combine_fwd(outs, lses):
    t = outs[0].shape[0]
    tm = 512

    def body(o0, o1, o2, l0, l1, l2, y_ref):
        w0, w1, w2 = _mix_weights(l0[...], l1[...], l2[...])
        y_ref[...] = w0 * o0[...] + w1 * o1[...] + w2 * o2[...]

    blk = BS((tm, 256), lambda i: (i, 0))
    return pl.pallas_call(
        body, name="combine_fwd", grid=(t // tm,), in_specs=[blk] * 6, out_specs=blk,
        out_shape=SDS((t, 256), F32), compiler_params=_cp("parallel"),
    )(*outs, *lses)


def _head_sum(x):
    a = lax.broadcasted_iota(jnp.int32, (256, 256), 0) // HEAD_DIM
    b = lax.broadcasted_iota(jnp.int32, (256, 256), 1) // HEAD_DIM
    ones = (a == b).astype(BF16)
    hi = x.astype(BF16)
    lo = (x - hi.astype(F32)).astype(BF16)
    return _dot_nn(hi, ones) + _dot_nn(lo, ones)


def _combine_bwd(dya, outs, lses):
    t = dya.shape[0]
    tm = 512

    def body(dy_ref, o0, o1, o2, l0, l1, l2, d0, d1, d2, e0, e1, e2):
        ws = _mix_weights(l0[...], l1[...], l2[...])
        dy = dy_ref[...]
        ya = ws[0] * o0[...] + ws[1] * o1[...] + ws[2] * o2[...]
        hs = _head_sum(dy * ya)
        for w, d_ref, e_ref in zip(ws, (d0, d1, d2), (e0, e1, e2)):
            d_ref[...] = w * dy
            e_ref[...] = w * hs

    blk = BS((tm, 256), lambda i: (i, 0))
    return pl.pallas_call(
        body, name="combine_bwd", grid=(t // tm,), in_specs=[blk] * 7, out_specs=[blk] * 6,
        out_shape=[SDS((t, 256), F32)] * 6, compiler_params=_cp("parallel"),
    )(dya, *outs, *lses)


def _na_bias_table(rel_bias):
    qc = np.arange(GRID_W)[:, None]
    kc = np.arange(GRID_W)[None, :]
    win_lo = np.clip(qc - 8, 0, GRID_W - 16)
    col_valid = (kc >= win_lo) & (kc < win_lo + 16)
    col_idx = np.clip(kc - qc + 15, 0, 30)
    row_idx = np.arange(NA_KR)[:, None] + np.arange(NA_KR)[None, :]
    rows = (row_idx[..., None] == np.arange(2 * NA_KR - 1)).astype(np.float32)
    cols = (col_idx[..., None] == np.arange(31)).astype(np.float32)
    b = jnp.einsum("hrd,ckr,qjd->hcqkj", rel_bias.astype(F32), rows, cols, precision=lax.Precision.HIGHEST)
    b = jnp.where(col_valid[None, None, :, None, :], b, NEG)
    return b.reshape(8, NA_KR, GRID_W, NA_KR * GRID_W)


def _na_row(i):
    lo = jnp.clip(i - NA_KR // 2, 0, NA_ROWS - NA_KR)
    return pl.multiple_of(GRID_W * i, GRID_W), pl.multiple_of(GRID_W * lo, GRID_W), lo - i + NA_KR - 1


def _na_fwd(proj, bias):
    t = proj.shape[0]
    kw = NA_KR * GRID_W

    def body(q_ref, k_ref, v_ref, b_ref, o_ref, l_ref):
        masks = _head_masks()

        def step(i0, carry):
            rows = [_na_row(i0 * NA_FWD_ROWS + j) for j in range(NA_FWD_ROWS)]
            qs = [q_ref[pl.ds(q_start, GRID_W), :] for q_start, _, _ in rows]
            kbs = [k_ref[pl.ds(k_start, kw), :].astype(BF16) for _, k_start, _ in rows]
            tiles = [(j, h) for j in range(NA_FWD_ROWS) for h in range(2)]
            ss = [_dot_nt((qs[j] * masks[h]).astype(BF16), kbs[j]) * SCALE + b_ref[h, rows[j][2]] for j, h in tiles]
            mxs = [jnp.max(s, axis=1, keepdims=True) for s in ss]
            ps = [jnp.exp(s - mx) for s, mx in zip(ss, mxs)]
            dens = [jnp.sum(p, axis=1, keepdims=True) for p in ps]
            pbs = [(p / den).astype(BF16) for p, den in zip(ps, dens)]
            vs = [v_ref[pl.ds(k_start, kw), :] for _, k_start, _ in rows]
            outs = [_dot_nn(pb, (vs[j] * masks[h]).astype(BF16)) for pb, (j, h) in zip(pbs, tiles)]
            for j, (q_start, _, _) in enumerate(rows):
                o_ref[pl.ds(q_start, GRID_W), :] = outs[2 * j] + outs[2 * j + 1]
                l_ref[pl.ds(q_start, GRID_W), :] = (
                    (mxs[2 * j] + jnp.log(dens[2 * j])) * masks[0] + (mxs[2 * j + 1] + jnp.log(dens[2 * j + 1])) * masks[1])
            return carry

        lax.fori_loop(0, NA_ROWS // NA_FWD_ROWS, step, 0)

    c0 = QKV_A // 128
    own = BS((S, 128), lambda b, hp: (b, hp))
    return pl.pallas_call(
        body, name="na_fwd", grid=(t // S, 4),
        in_specs=[BS((S, 128), lambda b, hp: (b, c0 + hp)), BS((S, 128), lambda b, hp: (b, c0 + 4 + hp)),
                  BS((S, 128), lambda b, hp: (b, c0 + 8 + hp)),
                  BS((2, NA_KR, GRID_W, kw), lambda b, hp: (hp, 0, 0, 0))],
        out_specs=[own, own], out_shape=[SDS((t, 512), F32), SDS((t, 512), F32)],
        compiler_params=_cp("parallel", "parallel"),
    )(proj, proj, proj, bias)


def _na_bwd(proj, bias, dyb, yb, lse):
    t = proj.shape[0]
    kw = NA_KR * GRID_W

    def body(q_ref, k_ref, v_ref, b_ref, do_ref, o_ref, l_ref, dq_ref, dk_ref, dv_ref, db_ref):
        masks = _head_masks()

        @pl.when(pl.program_id(1) == 0)
        def _():
            db_ref[...] = jnp.zeros_like(db_ref)

        dk_ref[...] = jnp.zeros_like(dk_ref)
        dv_ref[...] = jnp.zeros_like(dv_ref)

        def step(i0, carry):
            rows = [_na_row(i0 * NA_BWD_ROWS + j) for j in range(NA_BWD_ROWS)]
            tiles = [(j, h) for j in range(NA_BWD_ROWS) for h in range(2)]
            q_ds = [pl.ds(r[0], GRID_W) for r in rows]
            k_ds = [pl.ds(r[1], kw) for r in rows]
            qs = [q_ref[r, :] for r in q_ds]
            ks = [k_ref[r, :] for r in k_ds]
            kbs = [k.astype(BF16) for k in ks]
            vbs = [v_ref[r, :].astype(BF16) for r in k_ds]
            dos = [do_ref[r, :] for r in q_ds]
            os_ = [o_ref[r, :] for r in q_ds]
            lss = [l_ref[r, :] for r in q_ds]
            qhs = [(qs[j] * masks[h]).astype(BF16) for j, h in tiles]
            dohs = [(dos[j] * masks[h]).astype(BF16) for j, h in tiles]
            deltas = [jnp.sum(dos[j] * os_[j] * masks[h], axis=1, keepdims=True) for j, h in tiles]
            ss = [_dot_nt(qh, kbs[j]) * SCALE + b_ref[h, rows[j][2]] for qh, (j, h) in zip(qhs, tiles)]
            ps = [jnp.exp(s - lss[j][:, HEAD_DIM * h:HEAD_DIM * h + 1]) for s, (j, h) in zip(ss, tiles)]
            dps = [_dot_nt(doh, vbs[j]) for doh, (j, h) in zip(dohs, tiles)]
            dss = [p * (dp - delta) for p, dp, delta in zip(ps, dps, deltas)]
            for ds, (j, h) in zip(dss, tiles):
                db_ref[h, rows[j][2]] += ds
            dsbs = [ds.astype(BF16) for ds in dss]
            dqs = [_dot_nn(dsb, (ks[j] * masks[h]).astype(BF16)) for dsb, (j, h) in zip(dsbs, tiles)]
            dkws = [_dot_tn(dsb, qh) for dsb, qh in zip(dsbs, qhs)]
            dvws = [_dot_tn(p.astype(BF16), doh) for p, doh in zip(ps, dohs)]
            for j in range(NA_BWD_ROWS):
                dq_ref[q_ds[j], :] = (dqs[2 * j] + dqs[2 * j + 1]) * SCALE
                dk_ref[k_ds[j], :] += (dkws[2 * j] + dkws[2 * j + 1]) * SCALE
                dv_ref[k_ds[j], :] += dvws[2 * j] + dvws[2 * j + 1]
            return carry

        lax.fori_loop(0, NA_ROWS // NA_BWD_ROWS, step, 0)

    c0 = QKV_A // 128
    own = BS((S, 128), lambda hp, b: (b, hp))
    tab = BS((2, NA_KR, GRID_W, kw), lambda hp, b: (hp, 0, 0, 0))
    return pl.pallas_call(
        body, name="na_bwd", grid=(4, t // S),
        in_specs=[BS((S, 128), lambda hp, b: (b, c0 + hp)), BS((S, 128), lambda hp, b: (b, c0 + 4 + hp)),
                  BS((S, 128), lambda hp, b: (b, c0 + 8 + hp)), tab, own, own, own],
        out_specs=[own, own, own, tab],
        out_shape=[SDS((t, 512), F32)] * 3 + [SDS((8, NA_KR, GRID_W, kw), F32)],
        compiler_params=_cp("parallel", "arbitrary"),
    )(proj, proj, proj, bias, dyb, yb, lse)


def _na_dbias(db):
    kw = NA_KR * GRID_W

    def body(x_ref, o_ref, z_ref):
        lane = lax.broadcasted_iota(jnp.int32, (GRID_W, kw), 1)
        sub = lax.broadcasted_iota(jnp.int32, (GRID_W, kw), 0)
        rel = (lane % GRID_W) - sub + 15
        key_row = lax.broadcasted_iota(jnp.int32, (kw, 128), 0) // GRID_W
        dr = lax.broadcasted_iota(jnp.int32, (kw, 128), 1)
        acc = jnp.zeros((32, 128), F32)
        z_ref[...] = jnp.zeros_like(z_ref)
        for cls in range(NA_KR):
            xv = x_ref[cls]
            for dc in range(31):
                z_ref[dc:dc + 1, :] = jnp.sum(jnp.where(rel == dc, xv, 0.0), axis=0, keepdims=True)
            z = z_ref[...]
            ind = (key_row + cls == dr).astype(BF16)
            hi = z.astype(BF16)
            lo = (z - hi.astype(F32)).astype(BF16)
            acc = acc + _dot_nn(hi, ind) + _dot_nn(lo, ind)
        o_ref[...] = acc

    return pl.pallas_call(
        body, name="na_dbias", grid=(8,),
        in_specs=[BS((None, NA_KR, GRID_W, kw), lambda h: (h, 0, 0, 0))],
        out_specs=BS((None, 32, 128), lambda h: (h, 0, 0)), out_shape=SDS((8, 32, 128), F32),
        scratch_shapes=[pltpu.VMEM((32, kw), F32)], compiler_params=_cp("parallel"),
    )(db)


def _merge_fwd(ya, yb, proj, wat, wbt):
    t = ya.shape[0]
    tm, tn = 512, 256
    ca = (QKV_A + QKV_B) // tn
    cb = ca + D // tn

    def body(ya_ref, yb_ref, la_ref, lb_ref, wa_ref, wb_ref, m_ref, za_ref, zb_ref):
        za = _dot_nt(ya_ref[...].astype(BF16), wa_ref[...])
        zb = _dot_nt(yb_ref[...].astype(BF16), wb_ref[...])
        m_ref[...] = (jax.nn.sigmoid(la_ref[...]) * za + jax.nn.sigmoid(lb_ref[...]) * zb).astype(BF16)
        za_ref[...] = za.astype(BF16)
        zb_ref[...] = zb.astype(BF16)

    out = BS((tm, tn), lambda i, j: (i, j))
    return pl.pallas_call(
        body, name="merge_fwd", grid=(t // tm, D // tn),
        in_specs=[BS((tm, 256), lambda i, j: (i, 0)), BS((tm, 512), lambda i, j: (i, 0)),
                  BS((tm, tn), lambda i, j: (i, ca + j)), BS((tm, tn), lambda i, j: (i, cb + j)),
                  BS((tn, 256), lambda i, j: (j, 0)), BS((tn, 512), lambda i, j: (j, 0))],
        out_specs=[out, out, out], out_shape=[SDS((t, D), BF16)] * 3,
        compiler_params=_cp("parallel", "parallel"),
    )(ya, yb, proj, proj, wat, wbt)


def _merge_bwd(dm, za, zb, proj):
    t = dm.shape[0]
    tm, tn = 512, 256
    ca = (QKV_A + QKV_B) // tn
    cb = ca + D // tn

    def body(dm_ref, za_ref, zb_ref, la_ref, lb_ref, dza_ref, dzb_ref, dl_ref):
        dmv = dm_ref[...]
        ga = jax.nn.sigmoid(la_ref[...])
        gb = jax.nn.sigmoid(lb_ref[...])
        dza_ref[...] = (dmv * ga).astype(BF16)
        dzb_ref[...] = (dmv * gb).astype(BF16)
        dl_ref[0] = (dmv * za_ref[...].astype(F32) * ga * (1.0 - ga)).astype(BF16)
        dl_ref[1] = (dmv * zb_ref[...].astype(F32) * gb * (1.0 - gb)).astype(BF16)

    blk = BS((tm, tn), lambda i, j: (i, j))
    return pl.pallas_call(
        body, name="merge_bwd", grid=(t // tm, D // tn),
        in_specs=[blk, blk, blk, BS((tm, tn), lambda i, j: (i, ca + j)), BS((tm, tn), lambda i, j: (i, cb + j))],
        out_specs=[blk, blk, BS((2, tm, tn), lambda i, j: (0, i, j))],
        out_shape=[SDS((t, D), BF16), SDS((t, D), BF16), SDS((2, t, D), BF16)],
        compiler_params=_cp("parallel", "parallel"),
    )(dm, za, zb, proj, proj)


def _sum_slots(recv0, recv1, tag):
    _, r, c = recv0.shape
    tr = r if r * c <= 512 * 1024 else r // 2

    def body(a_ref, b_ref, o_ref):
        for layer, ref in enumerate((a_ref, b_ref)):
            acc = ref[0].astype(F32)
            for s in range(1, N_DEV):
                acc = acc + ref[s].astype(F32)
            o_ref[layer] = acc

    blk = BS((N_DEV, tr, c), lambda i: (0, i, 0))
    return pl.pallas_call(
        body, name=f"sum_slots_{tag}", grid=(r // tr,), in_specs=[blk, blk],
        out_specs=BS((2, tr, c), lambda i: (0, i, 0)), out_shape=SDS((2, r, c), F32),
        compiler_params=_cp("parallel"),
    )(recv0, recv1)


def _adamw(w, g, m, v, tag):
    layers, r, c = w.shape
    tr = next(r // k for k in (1, 2, 4, 8) if r // k <= 384 and r % (8 * k) == 0)

    def body(w_ref, g_ref, m_ref, v_ref, d_ref, mo_ref, vo_ref):
        gv = g_ref[...]
        mn = ADAM_B1 * m_ref[...] + (1.0 - ADAM_B1) * gv
        vn = ADAM_B2 * v_ref[...] + (1.0 - ADAM_B2) * (gv * gv)
        m_hat = mn / (1.0 - ADAM_B1 ** ADAM_STEP)
        v_hat = vn / (1.0 - ADAM_B2 ** ADAM_STEP)
        d_ref[...] = -ADAM_LR * (m_hat / (jnp.sqrt(v_hat) + ADAM_EPS) + ADAM_WD * w_ref[...])
        mo_ref[...] = mn
        vo_ref[...] = vn

    blk = BS((None, tr, c), lambda l, i: (l, i, 0))
    return pl.pallas_call(
        body, name=f"adamw_{tag}", grid=(layers, r // tr), in_specs=[blk] * 4, out_specs=[blk] * 3,
        out_shape=[SDS((layers, r, c), F32)] * 3, compiler_params=_cp("parallel", "parallel"),
    )(w, g, m, v)


def _place():
    return lax.axis_index("x"), lax.axis_index("y"), lax.axis_index("c")


def _flip(coord, bit):
    return 1 - coord if bit else coord


def _allgather(shards, tag):
    n_arr = len(shards)
    hbm = BS(memory_space=pl.ANY)

    def body(*refs):
        ins, outs = refs[:n_arr], refs[n_arr:2 * n_arr]
        send_sems, recv_sems, local_sems = refs[2 * n_arr:]
        x, y, c = _place()
        me, sibling = (x, y, c), (x, y, 1 - c)
        chips = [(1 - x, y), (x, 1 - y), (1 - x, 1 - y)]

        def rows(a, p):
            r = shards[a].shape[0]
            return outs[a].at[pl.ds((4 * p[0] + 2 * p[1] + p[2]) * r, r), :]

        def copy(a, k, block, to, src=None):
            return pltpu.make_async_remote_copy(
                src_ref=rows(a, block) if src is None else src, dst_ref=rows(a, block),
                send_sem=send_sems.at[a, k], recv_sem=recv_sems.at[a, k], device_id=to, device_id_type=MESH)

        mine = [pltpu.make_async_copy(ins[a], rows(a, me), local_sems.at[a]) for a in range(n_arr)]
        for cp in mine:
            cp.start()
        first = []
        for a in range(n_arr):
            first.append(copy(a, 0, me, sibling, src=ins[a]))
            first += [copy(a, 1 + j, me, (*chip, c), src=ins[a]) for j, chip in enumerate(chips)]
        for cp in first:
            cp.start()
        passed = []
        for a in range(n_arr):
            for j, chip in enumerate(chips):
                copy(a, 1 + j, (*chip, c), me).wait_recv()
                passed.append(copy(a, 4 + j, (*chip, c), sibling))
                passed[-1].start()
        for a in range(n_arr):
            copy(a, 0, sibling, me).wait_recv()
            for j, chip in enumerate(chips):
                copy(a, 4 + j, (*chip, 1 - c), me).wait_recv()
        for cp in first + passed:
            cp.wait_send()
        for cp in mine:
            cp.wait()

    return pl.pallas_call(
        body, name=f"allgather_{tag}", in_specs=[hbm] * n_arr, out_specs=[hbm] * n_arr,
        out_shape=[SDS((N_DEV * s.shape[0], s.shape[1]), s.dtype) for s in shards],
        scratch_shapes=[pltpu.SemaphoreType.DMA((n_arr, 7)), pltpu.SemaphoreType.DMA((n_arr, 7)),
                        pltpu.SemaphoreType.DMA((n_arr,))],
        compiler_params=pltpu.CompilerParams(has_side_effects=True),
    )(*shards)


def _peers(x, y, c):
    peers = []
    for mask in range(1, N_DEV):
        p = (_flip(x, mask & 4), _flip(y, mask & 2), _flip(c, mask & 1))
        peers.append((p, 4 * p[0] + 2 * p[1] + p[2]))
    return peers


def _exchange_refs(mode, src, land, me, peer):
    if mode == "gather":
        r = src.shape[0]
        return src, land.at[pl.ds(me * r, r), :], land.at[pl.ds(peer * r, r), :]
    r = land.shape[1]
    return src.at[pl.ds(peer * r, r), :], land.at[me], land.at[peer]


HBM_SPEC = BS(memory_space=pltpu.HBM)
SEM_SPEC = BS(memory_space=pltpu.SEMAPHORE)
DATAFLOW = pltpu.SideEffectType.DATAFLOW_SIDE_EFFECTING


def _own_block_placed(mode, src, me):
    if mode == "gather":
        r, c = src.shape
        return lax.dynamic_update_slice(lax.empty((N_DEV * r, c), src.dtype), src, (me * r, 0))
    r, c = src.shape[0] // N_DEV, src.shape[1]
    own = lax.dynamic_slice(src, (me * r, 0), (r, c))
    return lax.dynamic_update_slice(lax.empty((N_DEV, r, c), src.dtype), own[None], (me, 0, 0))


def _exchange_start(mode, srcs, after, tag):
    n = len(srcs)
    x, y, c = _place()
    lands = [_own_block_placed(mode, s, 4 * x + 2 * y + c) for s in srcs]
    behind = [] if after is None else [after]

    def body(*refs):
        src_refs, land_refs = refs[:n], refs[n:2 * n]
        send_sems, recv_sems = refs[2 * n + len(behind)], refs[2 * n + len(behind) + 1]
        token = refs[-1]
        bx, by, bc = _place()
        me = 4 * bx + 2 * by + bc
        for a in range(n):
            for k, (p, idx) in enumerate(_peers(bx, by, bc)):
                out, there, _ = _exchange_refs(mode, src_refs[a], land_refs[a], me, idx)
                pltpu.make_async_remote_copy(
                    src_ref=out, dst_ref=there, send_sem=send_sems.at[7 * a + k], recv_sem=recv_sems.at[7 * a + k],
                    device_id=p, device_id_type=MESH).start()
        token[...] = jnp.zeros_like(token)

    res = pl.pallas_call(
        body, name=f"{mode}_start_{tag}",
        out_shape=(pltpu.SemaphoreType.DMA((7 * n,)), pltpu.SemaphoreType.DMA((7 * n,)),
                   *[pltpu.HBM(s.shape, s.dtype) for s in srcs], *[pltpu.HBM(l.shape, l.dtype) for l in lands],
                   SDS((8, 128), F32)),
        in_specs=[HBM_SPEC] * (2 * n) + [BS(memory_space=pl.ANY)] * len(behind),
        out_specs=(SEM_SPEC, SEM_SPEC, *[HBM_SPEC] * (2 * n), BS(memory_space=pltpu.VMEM)),
        input_output_aliases={i: 2 + i for i in range(2 * n)},
        compiler_params=pltpu.CompilerParams(has_side_effects=DATAFLOW),
    )(*[pltpu.with_memory_space_constraint(s, pltpu.HBM) for s in srcs],
      *[pltpu.with_memory_space_constraint(l, pltpu.HBM) for l in lands], *behind)
    return (mode, res[0], res[1], res[2:2 + n], res[2 + n:2 + 2 * n]), res[-1]


def _exchange_wait(handle, after, tag):
    mode, send_sems, recv_sems, srcs, lands = handle
    n = len(srcs)

    def body(*refs):
        src_refs, land_refs = refs[:n], refs[n:2 * n]
        send_ref, recv_ref = refs[2 * n], refs[2 * n + 1]
        bx, by, bc = _place()
        me = 4 * bx + 2 * by + bc
        for a in range(n):
            for k, (p, idx) in enumerate(_peers(bx, by, bc)):
                out, _, here = _exchange_refs(mode, src_refs[a], land_refs[a], me, idx)
                cp = pltpu.make_async_remote_copy(
                    src_ref=out, dst_ref=here, send_sem=send_ref.at[7 * a + k], recv_sem=recv_ref.at[7 * a + k],
                    device_id=p, device_id_type=MESH)
                cp.wait_send()
                cp.wait_recv()

    res = pl.pallas_call(
        body, name=f"{mode}_wait_{tag}",
        out_shape=(*[pltpu.HBM(s.shape, s.dtype) for s in srcs], *[pltpu.HBM(l.shape, l.dtype) for l in lands]),
        in_specs=[HBM_SPEC] * (2 * n) + [SEM_SPEC, SEM_SPEC, BS(memory_space=pl.ANY)],
        out_specs=tuple([HBM_SPEC] * (2 * n)),
        input_output_aliases={i: i for i in range(2 * n)},
        compiler_params=pltpu.CompilerParams(has_side_effects=DATAFLOW),
    )(*srcs, *lands, send_sems, recv_sems, after)
    return list(res[n:])


def _allreduce_small(vec, behind):
    rows = vec.shape[0]

    def body(x_ref, behind_ref, o_ref, buf_ref, send_sems, recv_sems):
        x, y, c = _place()
        me = 4 * x + 2 * y + c
        buf_ref[me] = x_ref[...]
        peers = _peers(x, y, c)

        def copy(k, slot):
            return pltpu.make_async_remote_copy(
                src_ref=x_ref, dst_ref=buf_ref.at[slot], send_sem=send_sems.at[k], recv_sem=recv_sems.at[k],
                device_id=peers[k][0], device_id_type=MESH)

        sends = [copy(k, me) for k in range(N_DEV - 1)]
        for cp in sends:
            cp.start()
        for k in range(N_DEV - 1):
            copy(k, peers[k][1]).wait_recv()
        for cp in sends:
            cp.wait_send()
        acc = buf_ref[0]
        for s in range(1, N_DEV):
            acc = acc + buf_ref[s]
        o_ref[...] = acc

    vmem = BS(memory_space=pltpu.VMEM)
    return pl.pallas_call(
        body, name="allreduce_small", in_specs=[vmem, BS(memory_space=pl.ANY)], out_specs=vmem,
        out_shape=SDS((rows, 128), F32),
        scratch_shapes=[pltpu.VMEM((N_DEV, rows, 128), F32), pltpu.SemaphoreType.DMA((7,)),
                        pltpu.SemaphoreType.DMA((7,))],
        compiler_params=pltpu.CompilerParams(has_side_effects=True),
    )(vec, behind)


def _ffn_forward(x, norm_g, fetch, names, tag):
    hn = _norm_fwd(x, norm_g, tag)
    gu, act = _ffn_up(hn, fetch(names[0], hn).reshape(2, F, D), tag)
    out = _mm_nn(act[None], fetch(names[1], act)[None], f"down_{tag}", res=x, scale=0.5)
    return out, (x, hn, gu, act)


def _ffn_backward(dxo, dxo_b, saved, norm_g, wut, wd, tag, send):
    x, hn, gu, act = saved
    d_wd = _mm_tn(act[None], dxo_b, f"dwd_{tag}", scale=0.5)[0]
    du = _ffn_dact(dxo_b, wd, gu, send(("down",), [d_wd]), tag)
    d_wut = _mm_tn(du, hn, f"dwu_{tag}")
    token = send(("up",), [d_wut.reshape(2 * F, D)])
    dhn = _mm_nn(du, wut, f"dhn_{tag}", tn=512)
    return _norm_bwd(x, norm_g + token[0, 0], dhn, dxo, tag)


def _mixer_forward(x, norm_g, fetch, bias, tables, tag):
    hn = _norm_fwd(x, norm_g, tag)
    proj = _mm_nt_rows(hn, fetch("win", hn), f"proj_{tag}", 512, IN_W // 2, IN_W, 0)
    qkr = _rope_fwd(proj, *tables)
    outs, lses = [], []
    for grp in range(3):
        o, l = _dil_fwd(qkr, proj, grp)
        outs.append(o)
        lses.append(l)
    ya = _combine_fwd(outs, lses)
    yb, lse_b = _na_fwd(proj, bias)
    merged, za, zb = _merge_fwd(ya, yb, proj, fetch("wa", yb), fetch("wb", yb))
    out = _mm_nn(merged[None], fetch("wo", merged)[None], f"out_{tag}", res=x)
    return out, (x, hn, proj, qkr, outs, lses, ya, yb, lse_b, merged, za, zb)


def _mixer_backward(dxo, dxo_b, saved, norm_g, w, bias, tables, tag, send):
    wint, wat, wbt, wo = w
    x, hn, proj, qkr, outs, lses, ya, yb, lse_b, merged, za, zb = saved
    dm = _mm_nt_rows(dxo_b, wo, f"dmerged_{tag}", 512, D, D, 0)
    d_wo = _mm_tn(merged[None], dxo_b, f"dwo_{tag}")[0]
    dza, dzb, dlog = _merge_bwd(dm, za, zb, proj)
    dya = _mm_nn(dza[None], wat[None], f"dya_{tag}")
    dyb = _mm_nn(dzb[None], wbt[None], f"dyb_{tag}")
    d_wat = _mm_tn(dza[None], ya, f"dwa_{tag}")[0]
    d_wbt = _mm_tn(dzb[None], yb, f"dwb_{tag}")[0]
    cb = _combine_bwd(dya, outs, lses)
    dqs, dks, dvs = [], [], []
    for grp in range(3):
        dq, dk, dv = _dil_bwd(qkr, proj, cb[grp], cb[3 + grp], lses[grp], grp)
        dqs.append(dq)
        dks.append(dk)
        dvs.append(dv)
    dqk = _rope_bwd(dqs, dks, *tables)
    dqb, dkb, dvb, dbias_tab = _na_bwd(proj, bias, dyb, yb, lse_b)
    dbias = _na_dbias(dbias_tab)
    dproj = jnp.concatenate(
        [dqk] + [t.astype(BF16) for t in (*dvs, dqb, dkb, dvb)] + [dlog[0], dlog[1]], axis=1)
    d_wint = _mm_tn(dproj[None], hn, f"dwin_{tag}")[0]
    token = send(("win", "wa", "wb", "wo"), [d_wint, d_wat, d_wbt, d_wo])
    dhn = _mm_nn(dproj[None], wint[None], f"dhnm_{tag}", tm=256, tn=512)
    dx, dx_b, dg = _norm_bwd(x, norm_g + token[0, 0], dhn, dxo, f"mix_{tag}")
    dbias = dbias[:, :31, :15].transpose(0, 2, 1)
    return dx, dx_b, dg, dbias


def _pack_small(norms, biases, final, loss=None):
    parts = []
    for layer in range(DEPTH):
        parts += [norms[0][layer], norms[1][layer], norms[2][layer],
                  jnp.pad(biases[layer].reshape(-1), (0, BIAS_PAD - 8 * 15 * 31))]
    parts.append(final)
    flat = jnp.concatenate([p.reshape(-1).astype(F32) for p in parts])
    if loss is not None:
        flat = jnp.concatenate([flat, loss.reshape(-1)])
    return jnp.pad(flat, (0, SMALL_ROWS * 128 - flat.shape[0])).reshape(SMALL_ROWS, 128)


def _unpack_small(packed):
    flat = packed.reshape(-1)
    norms, biases = ([], [], []), []
    pos = 0
    for _ in range(DEPTH):
        for k in range(3):
            norms[k].append(flat[pos:pos + D])
            pos += D
        biases.append(flat[pos:pos + 8 * 15 * 31].reshape(8, 15, 31))
        pos += BIAS_PAD
    final = flat[pos:pos + D]
    pos += D
    return [jnp.stack(n) for n in norms], jnp.stack(biases), final, flat[pos]


def kernel(x, ffn1_norm, ffn1_w_up, ffn1_w_down, mix_norm, w_in, na_rel_bias, w_branch_a, w_branch_b, w_out, ffn2_norm, ffn2_w_up, ffn2_w_down, final_norm, loss_target, m_ffn1_norm, m_ffn1_w_up, m_ffn1_w_down, m_mix_norm, m_w_in, m_na_rel_bias, m_w_branch_a, m_w_branch_b, m_w_out, m_ffn2_norm, m_ffn2_w_up, m_ffn2_w_down, m_final_norm, v_ffn1_norm, v_ffn1_w_up, v_ffn1_w_down, v_mix_norm, v_w_in, v_na_rel_bias, v_w_branch_a, v_w_branch_b, v_w_out, v_ffn2_norm, v_ffn2_w_up, v_ffn2_w_down, v_final_norm):
    t = x.shape[0] * x.shape[1]
    xs = x.reshape(t, D)
    tgt = loss_target.reshape(t, D)
    tables = _rope_tables()

    col_sharded = dict(up1=ffn1_w_up, win=w_in, wa=w_branch_a, wb=w_branch_b, up2=ffn2_w_up)
    row_sharded = dict(down1=ffn1_w_down, wo=w_out, down2=ffn2_w_down)
    shard = [{} for _ in range(DEPTH)]
    for layer in range(DEPTH):
        for name, arr in col_sharded.items():
            shard[layer][name] = arr[layer].T.astype(BF16)
        for name, arr in row_sharded.items():
            shard[layer][name] = arr[layer].astype(BF16)

    weights = [{} for _ in range(DEPTH)]
    weights[0]["up1"] = _allgather([shard[0]["up1"]], "first")[0]
    travel = [(0, ("down1",)), (0, ("win",)), (0, ("wa", "wb", "wo")), (0, ("up2", "down2")),
              (1, ("up1", "down1")), (1, ("win",)), (1, ("wa", "wb", "wo")), (1, ("up2", "down2"))]
    pending = {}
    after = weights[0]["up1"]
    for i, (layer, names) in enumerate(travel):
        handle, after = _exchange_start("gather", [shard[layer][n] for n in names], after, f"w{i}")
        for n in names:
            pending[layer, n] = (i, handle, names)
    zero = after[0, 0]

    def fetcher(layer):
        def fetch(name, behind):
            if (layer, name) in pending:
                i, handle, names = pending[layer, name]
                for n, got in zip(names, _exchange_wait(handle, behind, f"w{i}")):
                    weights[layer][n] = got
                    del pending[layer, n]
            return weights[layer][name]
        return fetch

    saved = []
    h = xs
    for layer in range(DEPTH):
        bias = _na_bias_table(na_rel_bias[layer])
        fetch = fetcher(layer)
        h, s1 = _ffn_forward(h, ffn1_norm[layer] + zero, fetch, ("up1", "down1"), f"f1l{layer}")
        h, s2 = _mixer_forward(h, mix_norm[layer], fetch, bias, tables, f"l{layer}")
        h, s3 = _ffn_forward(h, ffn2_norm[layer], fetch, ("up2", "down2"), f"f2l{layer}")
        saved.append((s1, s2, s3, bias))
    loss_part, dh, dh_b, d_final = _loss_head(h, final_norm, tgt)

    d_norms = ([None] * DEPTH, [None] * DEPTH, [None] * DEPTH)
    d_bias = [None] * DEPTH
    sent = {}

    def sender(layer, suffix):
        def send(names, grads):
            tag = f"g{layer}{names[0]}{suffix}"
            handle, token = _exchange_start("scatter", grads, None, tag)
            for i, n in enumerate(names):
                sent[layer, n + suffix] = (handle, i, tag)
            return token
        return send

    for layer in reversed(range(DEPTH)):
        w = weights[layer]
        s1, s2, s3, bias = saved[layer]
        dh, dh_b, d_norms[2][layer] = _ffn_backward(
            dh, dh_b, s3, ffn2_norm[layer], w["up2"].reshape(2, F, D), w["down2"], f"f2l{layer}", sender(layer, "2"))
        dh, dh_b, d_norms[1][layer], d_bias[layer] = _mixer_backward(
            dh, dh_b, s2, mix_norm[layer], (w["win"], w["wa"], w["wb"], w["wo"]), bias, tables, f"l{layer}",
            sender(layer, ""))
        dh, dh_b, d_norms[0][layer] = _ffn_backward(
            dh, dh_b, s1, ffn1_norm[layer], w["up1"].reshape(2, F, D), w["down1"], f"f1l{layer}", sender(layer, "1"))
    grad_x = dh.reshape(x.shape)

    originals = dict(up1=(ffn1_w_up, m_ffn1_w_up, v_ffn1_w_up), down1=(ffn1_w_down, m_ffn1_w_down, v_ffn1_w_down),
                     win=(w_in, m_w_in, v_w_in), wa=(w_branch_a, m_w_branch_a, v_w_branch_a),
                     wb=(w_branch_b, m_w_branch_b, v_w_branch_b), wo=(w_out, m_w_out, v_w_out),
                     up2=(ffn2_w_up, m_ffn2_w_up, v_ffn2_w_up), down2=(ffn2_w_down, m_ffn2_w_down, v_ffn2_w_down))
    big = {}
    behind = dh
    landed = {}

    def received(layer, name):
        handle, i, tag = sent[layer, name]
        if tag not in landed:
            landed[tag] = _exchange_wait(handle, behind, tag)
        return landed[tag][i]

    for name in ("down2", "up2", "win", "wa", "wb", "wo", "down1", "up1"):
        g = _sum_slots(received(0, name), received(1, name), name)
        wv, mv, vv = originals[name]
        if name in col_sharded:
            wv, mv, vv = (jnp.swapaxes(t, 1, 2) for t in (wv, mv, vv))
        big[name] = (g, *_adamw(wv, g, mv, vv, name))
        behind = big[name][1]
        if name in col_sharded:
            big[name] = tuple(jnp.swapaxes(t, 1, 2) for t in big[name])

    small = _allreduce_small(_pack_small(d_norms, d_bias, d_final, loss_part[0, :1]), behind)
    g_norms, g_bias, g_final, loss = _unpack_small(small)
    w_small = _pack_small((ffn1_norm, mix_norm, ffn2_norm), na_rel_bias, final_norm)
    m_small = _pack_small((m_ffn1_norm, m_mix_norm, m_ffn2_norm), m_na_rel_bias, m_final_norm)
    v_small = _pack_small((v_ffn1_norm, v_mix_norm, v_ffn2_norm), v_na_rel_bias, v_final_norm)
    upd = _adamw(w_small[None], small[None], m_small[None], v_small[None], "small")
    small_out = [(g_norms, g_bias, g_final)] + [_unpack_small(u[0])[:3] for u in upd]

    outputs = [loss, grad_x]
    for kind in range(4):
        norms, bias_k, final_k = small_out[kind]
        outputs += [norms[0], big["up1"][kind], big["down1"][kind], norms[1], big["win"][kind], bias_k,
                    big["wa"][kind], big["wb"][kind], big["wo"][kind], norms[2], big["up2"][kind],
                    big["down2"][kind], final_k]
    return tuple(outputs)
```

```python
import numpy as np

import jax
import jax.numpy as jnp
from jax import lax
from jax.experimental import pallas as pl
from jax.experimental.pallas import tpu as pltpu

F32 = jnp.float32
BF16 = jnp.bfloat16
SDS = jax.ShapeDtypeStruct
BS = pl.BlockSpec
MESH = pl.DeviceIdType.MESH

D = 1024
S = 2048
F = 2816
DEPTH = 2
HEAD_DIM = 64
DILATIONS = (1, 4, 16)
HALF = 64
QKV_A = 2304
QKV_B = 1536
IN_W = 5888
N_DEV = 8
NA_ROWS = 32
GRID_W = 64
NA_KR = 8
ROPE_THETA = 10000.0
RMS_EPS = 1e-6
NEG = -1e30
SCALE = HEAD_DIM ** -0.5
ADAM_LR, ADAM_B1, ADAM_B2, ADAM_EPS, ADAM_WD, ADAM_STEP = 0.001, 0.9, 0.999, 1e-08, 0.01, 10
VMEM_LIMIT_V7X = 52 * 1024 * 1024
SMALL_ROWS = 120
BIAS_PAD = 3840
NA_FWD_ROWS = 4
NA_BWD_ROWS = 4
DIL_FWD_TILES = 4
DIL_BWD_TILES = 2


def _cp(*sem):
    return pltpu.CompilerParams(dimension_semantics=sem, vmem_limit_bytes=VMEM_LIMIT_V7X)


def _dot_nn(a, b):
    return jnp.dot(a, b, preferred_element_type=F32)


def _dot_nt(a, b):
    return lax.dot_general(a, b, (((1,), (1,)), ((), ())), preferred_element_type=F32)


def _dot_tn(a, b):
    return lax.dot_general(a, b, (((0,), (0,)), ((), ())), preferred_element_type=F32)


def _ds(start, size, stride):
    return pl.ds(start, size) if stride == 1 else pl.ds(start, size, stride=stride)


def _norm_fwd(x, g, tag):
    t = x.shape[0]
    tm = 512

    def body(x_ref, g_ref, o_ref):
        xv = x_ref[...]
        r = lax.rsqrt(jnp.mean(xv * xv, axis=-1, keepdims=True) + RMS_EPS)
        o_ref[...] = (xv * r * g_ref[...]).astype(BF16)

    return pl.pallas_call(
        body, name=f"norm_fwd_{tag}", grid=(t // tm,),
        in_specs=[BS((tm, D), lambda i: (i, 0)), BS((1, D), lambda i: (0, 0))],
        out_specs=BS((tm, D), lambda i: (i, 0)),
        out_shape=SDS((t, D), BF16), compiler_params=_cp("parallel"),
    )(x, g.reshape(1, D))


def _norm_bwd(x, g, dh, dres, tag):
    t = x.shape[0]
    tm = 512

    def body(x_ref, g_ref, dh_ref, dr_ref, dx_ref, dxb_ref, dg_ref):
        @pl.when(pl.program_id(0) == 0)
        def _():
            dg_ref[...] = jnp.zeros_like(dg_ref)

        xv = x_ref[...]
        r = lax.rsqrt(jnp.mean(xv * xv, axis=-1, keepdims=True) + RMS_EPS)
        xh = xv * r
        dh = dh_ref[...]
        u = dh * g_ref[...]
        dx = dr_ref[...] + r * (u - xh * jnp.mean(xh * u, axis=-1, keepdims=True))
        dx_ref[...] = dx
        dxb_ref[...] = dx.astype(BF16)
        dg_ref[...] += jnp.sum(dh * xh, axis=0, keepdims=True)

    row = BS((tm, D), lambda i: (i, 0))
    vec = BS((1, D), lambda i: (0, 0))
    return pl.pallas_call(
        body, name=f"norm_bwd_{tag}", grid=(t // tm,),
        in_specs=[row, vec, row, row], out_specs=[row, row, vec],
        out_shape=[SDS((t, D), F32), SDS((t, D), BF16), SDS((1, D), F32)], compiler_params=_cp("arbitrary"),
    )(x, g.reshape(1, D), dh, dres)


def _loss_head(x, g, tgt):
    t = x.shape[0]
    tm = 512

    def body(x_ref, g_ref, t_ref, loss_ref, dx_ref, dxb_ref, dg_ref):
        @pl.when(pl.program_id(0) == 0)
        def _():
            dg_ref[...] = jnp.zeros_like(dg_ref)
            loss_ref[...] = jnp.zeros_like(loss_ref)

        xv = x_ref[...]
        gv = g_ref[...]
        r = lax.rsqrt(jnp.mean(xv * xv, axis=-1, keepdims=True) + RMS_EPS)
        xh = xv * r
        e = xh * gv - t_ref[...]
        loss_ref[...] += 0.5 * jnp.sum(jnp.mean(e * e, axis=-1, keepdims=True), axis=0, keepdims=True)
        dy = e * (1.0 / D)
        u = dy * gv
        dx = r * (u - xh * jnp.mean(xh * u, axis=-1, keepdims=True))
        dx_ref[...] = dx
        dxb_ref[...] = dx.astype(BF16)
        dg_ref[...] += jnp.sum(dy * xh, axis=0, keepdims=True)

    row = BS((tm, D), lambda i: (i, 0))
    vec = BS((1, D), lambda i: (0, 0))
    return pl.pallas_call(
        body, name="loss_head", grid=(t // tm,),
        in_specs=[row, vec, row], out_specs=[BS((1, 128), lambda i: (0, 0)), row, row, vec],
        out_shape=[SDS((1, 128), F32), SDS((t, D), F32), SDS((t, D), BF16), SDS((1, D), F32)],
        compiler_params=_cp("arbitrary"),
    )(x, g.reshape(1, D), tgt)


def _mm_nn(a, w, tag, res=None, scale=1.0, tm=512, tn=None):
    c_n, t, k = a.shape
    n = w.shape[2]
    tn = n if tn is None else tn

    def body(*refs):
        a_ref, w_ref = refs[0], refs[1]
        o_ref = refs[-1]
        acc = _dot_nn(a_ref[0].astype(BF16), w_ref[0])
        for c in range(1, c_n):
            acc = acc + _dot_nn(a_ref[c].astype(BF16), w_ref[c])
        if scale != 1.0:
            acc = acc * scale
        if res is not None:
            acc = refs[2][...] + acc
        o_ref[...] = acc

    in_specs = [BS((c_n, tm, k), lambda i, j: (0, i, 0)), BS((c_n, k, tn), lambda i, j: (0, 0, j))]
    args = [a, w]
    if res is not None:
        in_specs.append(BS((tm, tn), lambda i, j: (i, j)))
        args.append(res)
    return pl.pallas_call(
        body, name=f"mm_nn_{tag}", grid=(t // tm, n // tn), in_specs=in_specs,
        out_specs=BS((tm, tn), lambda i, j: (i, j)), out_shape=SDS((t, n), F32),
        compiler_params=_cp("parallel", "parallel"),
    )(*args)


def _mm_nt_rows(a, w, tag, tm, tn, n_total, w_row0):
    t, k = a.shape
    assert w_row0 % tn == 0 and n_total % tn == 0
    j0 = w_row0 // tn

    def body(a_ref, w_ref, o_ref):
        o_ref[...] = _dot_nt(a_ref[...].astype(BF16), w_ref[...])

    return pl.pallas_call(
        body, name=f"mm_nt_{tag}", grid=(n_total // tn, t // tm),
        in_specs=[BS((tm, k), lambda j, i: (i, 0)), BS((tn, k), lambda j, i: (j0 + j, 0))],
        out_specs=BS((tm, tn), lambda j, i: (i, j)), out_shape=SDS((t, n_total), F32),
        compiler_params=_cp("parallel", "parallel"),
    )(a, w)


def _mm_tn(a, b, tag, scale=1.0, tmm=256, tk=None):
    c_n, t, m = a.shape
    n = b.shape[1]
    tk = t if tk is None else tk
    nk = t // tk

    def body_one(a_ref, b_ref, o_ref):
        o_ref[...] = (_dot_tn(a_ref[...].astype(BF16), b_ref[...].astype(BF16)) * scale).astype(BF16)

    def body_acc(a_ref, b_ref, o_ref, acc_ref):
        kk = pl.program_id(2)

        @pl.when(kk == 0)
        def _():
            acc_ref[...] = jnp.zeros_like(acc_ref)

        acc_ref[...] += _dot_tn(a_ref[...].astype(BF16), b_ref[...].astype(BF16))

        @pl.when(kk == nk - 1)
        def _():
            o_ref[...] = (acc_ref[...] * scale).astype(BF16)

    return pl.pallas_call(
        body_one if nk == 1 else body_acc, name=f"mm_tn_{tag}", grid=(c_n, m // tmm, nk),
        in_specs=[BS((None, tk, tmm), lambda c, mi, kk: (c, kk, mi)), BS((tk, n), lambda c, mi, kk: (kk, 0))],
        out_specs=BS((None, tmm, n), lambda c, mi, kk: (c, mi, 0)),
        out_shape=SDS((c_n, m, n), BF16), scratch_shapes=[] if nk == 1 else [pltpu.VMEM((tmm, n), F32)],
        compiler_params=_cp("parallel", "parallel", "arbitrary"),
    )(a, b)


def _ffn_up(hn, wut, tag):
    t = hn.shape[0]
    tm, tn = 512, 1408

    def body(h_ref, w_ref, gu_ref, act_ref):
        h = h_ref[...]
        g = _dot_nt(h, w_ref[0])
        u = _dot_nt(h, w_ref[1])
        gu_ref[0] = g.astype(BF16)
        gu_ref[1] = u.astype(BF16)
        act_ref[...] = (g * jax.nn.sigmoid(g) * u).astype(BF16)

    return pl.pallas_call(
        body, name=f"ffn_up_{tag}", grid=(F // tn, t // tm),
        in_specs=[BS((tm, D), lambda j, i: (i, 0)), BS((2, tn, D), lambda j, i: (0, j, 0))],
        out_specs=[BS((2, tm, tn), lambda j, i: (0, i, j)), BS((tm, tn), lambda j, i: (i, j))],
        out_shape=[SDS((2, t, F), BF16), SDS((t, F), BF16)],
        compiler_params=_cp("parallel", "parallel"),
    )(hn, wut)


def _ffn_dact(dxo, wd, gu, tie, tag):
    t = dxo.shape[0]
    tm, tn = 512, 1408

    def body(d_ref, w_ref, gu_ref, tie_ref, o_ref):
        dact = _dot_nt(d_ref[...] * 0.5, w_ref[...])
        g = gu_ref[0].astype(F32)
        u = gu_ref[1].astype(F32)
        sg = 0.5 * jnp.tanh(0.5 * g) + 0.5
        o_ref[0] = (dact * u * (sg * (1.0 + g * (1.0 - sg)))).astype(BF16)
        o_ref[1] = (dact * (g * sg)).astype(BF16)

    return pl.pallas_call(
        body, name=f"ffn_dact_{tag}", grid=(F // tn, t // tm),
        in_specs=[BS((tm, D), lambda j, i: (i, 0)), BS((tn, D), lambda j, i: (j, 0)),
                  BS((2, tm, tn), lambda j, i: (0, i, j)), BS((8, 128), lambda j, i: (0, 0))],
        out_specs=BS((2, tm, tn), lambda j, i: (0, i, j)),
        out_shape=SDS((2, t, F), BF16), compiler_params=_cp("parallel", "parallel"),
    )(dxo, wd, gu, tie)


def _rope_tables():
    half = HEAD_DIM // 2
    inv_freq = ROPE_THETA ** (-jnp.arange(half, dtype=F32) / half)
    ang = jnp.arange(S).astype(F32)[:, None] * inv_freq[None, :]
    cos, sin = jnp.cos(ang), jnp.sin(ang)
    return jnp.concatenate([cos, cos, cos, cos], axis=1), jnp.concatenate([-sin, sin, -sin, sin], axis=1)


def _swap_halves(t, first_half):
    return jnp.where(first_half, pltpu.roll(t, 96, 1), pltpu.roll(t, 32, 1))


def _rope_fwd(proj, cos_t, sin_t):
    t = proj.shape[0]
    tm = 512
    width = 2 * QKV_A // 3

    def body(x_ref, c_ref, s_ref, o_ref):
        c = c_ref[...]
        sg = s_ref[...]
        first = (lax.broadcasted_iota(jnp.int32, (tm, 128), 1) % HEAD_DIM) < HEAD_DIM // 2
        for j in range(width // 128):
            v = x_ref[:, 128 * j:128 * (j + 1)]
            o_ref[:, 128 * j:128 * (j + 1)] = v * c + _swap_halves(v, first) * sg

    tab = BS((tm, 128), lambda i: (i % (S // tm), 0))
    return pl.pallas_call(
        body, name="rope_fwd", grid=(t // tm,),
        in_specs=[BS((tm, width), lambda i: (i, 0)), tab, tab],
        out_specs=BS((tm, width), lambda i: (i, 0)), out_shape=SDS((t, width), F32),
        compiler_params=_cp("parallel"),
    )(proj, cos_t, sin_t)


def _rope_bwd(dqs, dks, cos_t, sin_t):
    t = dqs[0].shape[0]
    tm = 512

    def body(*refs):
        c = refs[6][...]
        sg = refs[7][...]
        o_ref = refs[8]
        first = (lax.broadcasted_iota(jnp.int32, (tm, 128), 1) % HEAD_DIM) < HEAD_DIM // 2
        for a in range(6):
            for hp in range(2):
                v = refs[a][:, 128 * hp:128 * (hp + 1)]
                col = 128 * (2 * a + hp)
                o_ref[:, col:col + 128] = (v * c + _swap_halves(v * sg, first)).astype(BF16)

    blk = BS((tm, 256), lambda i: (i, 0))
    tab = BS((tm, 128), lambda i: (i % (S // tm), 0))
    return pl.pallas_call(
        body, name="rope_bwd", grid=(t // tm,), in_specs=[blk] * 6 + [tab, tab],
        out_specs=BS((tm, 1536), lambda i: (i, 0)), out_shape=SDS((t, 1536), BF16),
        compiler_params=_cp("parallel"),
    )(*dqs, *dks, cos_t, sin_t)


def _head_masks():
    lane = lax.broadcasted_iota(jnp.int32, (1, 128), 1)
    m0 = (lane < HEAD_DIM).astype(F32)
    return m0, 1.0 - m0


def _dil_geometry(d):
    sub = S // d
    q_rows = 128
    k_rows = min(256, sub)
    return sub, q_rows, sub // q_rows, k_rows


def _dil_tile(idx, d):
    sub, q_rows, nb, k_rows = _dil_geometry(d)
    r = idx // nb
    n = idx % nb
    k_sub = jnp.clip(q_rows * n - HALF, 0, sub - k_rows)
    if d == 1:
        q_start = pl.multiple_of(q_rows * n, q_rows)
        k_start = pl.multiple_of(k_sub, HALF)
    else:
        q_start = q_rows * n * d + r
        k_start = k_sub * d + r
    ii = lax.broadcasted_iota(jnp.int32, (q_rows, k_rows), 0)
    jj = lax.broadcasted_iota(jnp.int32, (q_rows, k_rows), 1)
    valid = jnp.abs(jj - ii + (k_sub - q_rows * n)) <= HALF
    return q_start, k_start, valid


def _dil_specs(grp):
    qs = BS((S, 128), lambda b, hp: (b, 2 * grp + hp))
    ks = BS((S, 128), lambda b, hp: (b, 6 + 2 * grp + hp))
    vs = BS((S, 128), lambda b, hp: (b, 12 + 2 * grp + hp))
    own = BS((S, 128), lambda b, hp: (b, hp))
    return qs, ks, vs, own


def _dil_fwd(qkr, proj, grp):
    t = qkr.shape[0]
    d = DILATIONS[grp]
    _, q_rows, nb, k_rows = _dil_geometry(d)

    def body(q_ref, k_ref, v_ref, o_ref, l_ref):
        masks = _head_masks()

        def step(i0, carry):
            geo = [_dil_tile(i0 * DIL_FWD_TILES + j, d) for j in range(DIL_FWD_TILES)]
            tiles = [(j, h) for j in range(DIL_FWD_TILES) for h in range(2)]
            qs = [q_ref[_ds(g[0], q_rows, d), :] for g in geo]
            kbs = [k_ref[_ds(g[1], k_rows, d), :].astype(BF16) for g in geo]
            ss = [jnp.where(geo[j][2], _dot_nt((qs[j] * masks[h]).astype(BF16), kbs[j]) * SCALE, NEG) for j, h in tiles]
            mxs = [jnp.max(s, axis=1, keepdims=True) for s in ss]
            ps = [jnp.exp(s - mx) for s, mx in zip(ss, mxs)]
            dens = [jnp.sum(p, axis=1, keepdims=True) for p in ps]
            vs = [v_ref[_ds(g[1], k_rows, d), :] for g in geo]
            outs = [_dot_nn(p.astype(BF16), (vs[j] * masks[h]).astype(BF16)) / den
                    for p, den, (j, h) in zip(ps, dens, tiles)]
            for j, g in enumerate(geo):
                o_ref[_ds(g[0], q_rows, d), :] = outs[2 * j] + outs[2 * j + 1]
                l_ref[_ds(g[0], q_rows, d), :] = (
                    (mxs[2 * j] + jnp.log(dens[2 * j])) * masks[0] + (mxs[2 * j + 1] + jnp.log(dens[2 * j + 1])) * masks[1])
            return carry

        lax.fori_loop(0, d * nb // DIL_FWD_TILES, step, 0)

    qs, ks, vs, own = _dil_specs(grp)
    return pl.pallas_call(
        body, name=f"dil_fwd_{grp}", grid=(t // S, 2), in_specs=[qs, ks, vs], out_specs=[own, own],
        out_shape=[SDS((t, 256), F32), SDS((t, 256), F32)], compiler_params=_cp("parallel", "parallel"),
    )(qkr, qkr, proj)


def _dil_bwd(qkr, proj, do, dlp, lse, grp):
    t = qkr.shape[0]
    d = DILATIONS[grp]
    _, q_rows, nb, k_rows = _dil_geometry(d)

    def body(q_ref, k_ref, v_ref, do_ref, dl_ref, l_ref, dq_ref, dk_ref, dv_ref):
        masks = _head_masks()
        dk_ref[...] = jnp.zeros_like(dk_ref)
        dv_ref[...] = jnp.zeros_like(dv_ref)

        def step(i0, carry):
            geo = [_dil_tile(i0 * DIL_BWD_TILES + j, d) for j in range(DIL_BWD_TILES)]
            tiles = [(j, h) for j in range(DIL_BWD_TILES) for h in range(2)]
            q_ds = [_ds(g[0], q_rows, d) for g in geo]
            k_ds = [_ds(g[1], k_rows, d) for g in geo]
            qs = [q_ref[r, :] for r in q_ds]
            ks = [k_ref[r, :] for r in k_ds]
            kbs = [k.astype(BF16) for k in ks]
            vbs = [v_ref[r, :].astype(BF16) for r in k_ds]
            dos = [do_ref[r, :] for r in q_ds]
            dls = [dl_ref[r, :] for r in q_ds]
            lss = [l_ref[r, :] for r in q_ds]
            qhs = [(qs[j] * masks[h]).astype(BF16) for j, h in tiles]
            dohs = [(dos[j] * masks[h]).astype(BF16) for j, h in tiles]
            ss = [jnp.where(geo[j][2], _dot_nt(qh, kbs[j]) * SCALE, NEG) for qh, (j, h) in zip(qhs, tiles)]
            ps = [jnp.exp(s - lss[j][:, HEAD_DIM * h:HEAD_DIM * h + 1]) for s, (j, h) in zip(ss, tiles)]
            dps = [_dot_nt(doh, vbs[j]) for doh, (j, h) in zip(dohs, tiles)]
            dss = [(p * (dp - dls[j][:, HEAD_DIM * h:HEAD_DIM * h + 1])).astype(BF16)
                   for p, dp, (j, h) in zip(ps, dps, tiles)]
            dqs = [_dot_nn(ds, (ks[j] * masks[h]).astype(BF16)) for ds, (j, h) in zip(dss, tiles)]
            dkws = [_dot_tn(ds, qh) for ds, qh in zip(dss, qhs)]
            dvws = [_dot_tn(p.astype(BF16), doh) for p, doh in zip(ps, dohs)]
            for j in range(DIL_BWD_TILES):
                dq_ref[q_ds[j], :] = (dqs[2 * j] + dqs[2 * j + 1]) * SCALE
                dk_ref[k_ds[j], :] += (dkws[2 * j] + dkws[2 * j + 1]) * SCALE
                dv_ref[k_ds[j], :] += dvws[2 * j] + dvws[2 * j + 1]
            return carry

        lax.fori_loop(0, d * nb // DIL_BWD_TILES, step, 0)

    qs, ks, vs, own = _dil_specs(grp)
    return pl.pallas_call(
        body, name=f"dil_bwd_{grp}", grid=(t // S, 2), in_specs=[qs, ks, vs, own, own, own],
        out_specs=[own, own, own], out_shape=[SDS((t, 256), F32)] * 3,
        compiler_params=_cp("parallel", "parallel"),
    )(qkr, qkr, proj, do, dlp, lse)


def _mix_weights(l0, l1, l2):
    mx = jnp.maximum(jnp.maximum(l0, l1), l2)
    e0, e1, e2 = jnp.exp(l0 - mx), jnp.exp(l1 - mx), jnp.exp(l2 - mx)
    den = e0 + e1 + e2
    return e0 / den, e1 / den, e2 / den


def _combine_fwd(outs, lses):
    t = outs[0].shape[0]
    tm = 512

    def body(o0, o1, o2, l0, l1, l2, y_ref):
        w0, w1, w2 = _mix_weights(l0[...], l1[...], l2[...])
        y_ref[...] = w0 * o0[...] + w1 * o1[...] + w2 * o2[...]

    blk = BS((tm, 256), lambda i: (i, 0))
    return pl.pallas_call(
        body, name="combine_fwd", grid=(t // tm,), in_specs=[blk] * 6, out_specs=blk,
        out_shape=SDS((t, 256), F32), compiler_params=_cp("parallel"),
    )(*outs, *lses)


def _head_sum(x):
    a = lax.broadcasted_iota(jnp.int32, (256, 256), 0) // HEAD_DIM
    b = lax.broadcasted_iota(jnp.int32, (256, 256), 1) // HEAD_DIM
    ones = (a == b).astype(BF16)
    hi = x.astype(BF16)
    lo = (x - hi.astype(F32)).astype(BF16)
    return _dot_nn(hi, ones) + _dot_nn(lo, ones)


def _combine_bwd(dya, outs, lses):
    t = dya.shape[0]
    tm = 512

    def body(dy_ref, o0, o1, o2, l0, l1, l2, d0, d1, d2, e0, e1, e2):
        ws = _mix_weights(l0[...], l1[...], l2[...])
        dy = dy_ref[...]
        ya = ws[0] * o0[...] + ws[1] * o1[...] + ws[2] * o2[...]
        hs = _head_sum(dy * ya)
        for w, d_ref, e_ref in zip(ws, (d0, d1, d2), (e0, e1, e2)):
            d_ref[...] = w * dy
            e_ref[...] = w * hs

    blk = BS((tm, 256), lambda i: (i, 0))
    return pl.pallas_call(
        body, name="combine_bwd", grid=(t // tm,), in_specs=[blk] * 7, out_specs=[blk] * 6,
        out_shape=[SDS((t, 256), F32)] * 6, compiler_params=_cp("parallel"),
    )(dya, *outs, *lses)


def _na_bias_table(rel_bias):
    qc = np.arange(GRID_W)[:, None]
    kc = np.arange(GRID_W)[None, :]
    win_lo = np.clip(qc - 8, 0, GRID_W - 16)
    col_valid = (kc >= win_lo) & (kc < win_lo + 16)
    col_idx = np.clip(kc - qc + 15, 0, 30)
    row_idx = np.arange(NA_KR)[:, None] + np.arange(NA_KR)[None, :]
    rows = (row_idx[..., None] == np.arange(2 * NA_KR - 1)).astype(np.float32)
    cols = (col_idx[..., None] == np.arange(31)).astype(np.float32)
    b = jnp.einsum("hrd,ckr,qjd->hcqkj", rel_bias.astype(F32), rows, cols, precision=lax.Precision.HIGHEST)
    b = jnp.where(col_valid[None, None, :, None, :], b, NEG)
    return b.reshape(8, NA_KR, GRID_W, NA_KR * GRID_W)


def _na_row(i):
    lo = jnp.clip(i - NA_KR // 2, 0, NA_ROWS - NA_KR)
    return pl.multiple_of(GRID_W * i, GRID_W), pl.multiple_of(GRID_W * lo, GRID_W), lo - i + NA_KR - 1


def _na_fwd(proj, bias):
    t = proj.shape[0]
    kw = NA_KR * GRID_W

    def body(q_ref, k_ref, v_ref, b_ref, o_ref, l_ref):
        masks = _head_masks()

        def step(i0, carry):
            rows = [_na_row(i0 * NA_FWD_ROWS + j) for j in range(NA_FWD_ROWS)]
            qs = [q_ref[pl.ds(q_start, GRID_W), :] for q_start, _, _ in rows]
            kbs = [k_ref[pl.ds(k_start, kw), :].astype(BF16) for _, k_start, _ in rows]
            tiles = [(j, h) for j in range(NA_FWD_ROWS) for h in range(2)]
            ss = [_dot_nt((qs[j] * masks[h]).astype(BF16), kbs[j]) * SCALE + b_ref[h, rows[j][2]] for j, h in tiles]
            mxs = [jnp.max(s, axis=1, keepdims=True) for s in ss]
            ps = [jnp.exp(s - mx) for s, mx in zip(ss, mxs)]
            dens = [jnp.sum(p, axis=1, keepdims=True) for p in ps]
            pbs = [(p / den).astype(BF16) for p, den in zip(ps, dens)]
            vs = [v_ref[pl.ds(k_start, kw), :] for _, k_start, _ in rows]
            outs = [_dot_nn(pb, (vs[j] * masks[h]).astype(BF16)) for pb, (j, h) in zip(pbs, tiles)]
            for j, (q_start, _, _) in enumerate(rows):
                o_ref[pl.ds(q_start, GRID_W), :] = outs[2 * j] + outs[2 * j + 1]
                l_ref[pl.ds(q_start, GRID_W), :] = (
                    (mxs[2 * j] + jnp.log(dens[2 * j])) * masks[0] + (mxs[2 * j + 1] + jnp.log(dens[2 * j + 1])) * masks[1])
            return carry

        lax.fori_loop(0, NA_ROWS // NA_FWD_ROWS, step, 0)

    c0 = QKV_A // 128
    own = BS((S, 128), lambda b, hp: (b, hp))
    return pl.pallas_call(
        body, name="na_fwd", grid=(t // S, 4),
        in_specs=[BS((S, 128), lambda b, hp: (b, c0 + hp)), BS((S, 128), lambda b, hp: (b, c0 + 4 + hp)),
                  BS((S, 128), lambda b, hp: (b, c0 + 8 + hp)),
                  BS((2, NA_KR, GRID_W, kw), lambda b, hp: (hp, 0, 0, 0))],
        out_specs=[own, own], out_shape=[SDS((t, 512), F32), SDS((t, 512), F32)],
        compiler_params=_cp("parallel", "parallel"),
    )(proj, proj, proj, bias)


def _na_bwd(proj, bias, dyb, yb, lse):
    t = proj.shape[0]
    kw = NA_KR * GRID_W

    def body(q_ref, k_ref, v_ref, b_ref, do_ref, o_ref, l_ref, dq_ref, dk_ref, dv_ref, db_ref):
        masks = _head_masks()

        @pl.when(pl.program_id(1) == 0)
        def _():
            db_ref[...] = jnp.zeros_like(db_ref)

        dk_ref[...] = jnp.zeros_like(dk_ref)
        dv_ref[...] = jnp.zeros_like(dv_ref)

        def step(i0, carry):
            rows = [_na_row(i0 * NA_BWD_ROWS + j) for j in range(NA_BWD_ROWS)]
            tiles = [(j, h) for j in range(NA_BWD_ROWS) for h in range(2)]
            q_ds = [pl.ds(r[0], GRID_W) for r in rows]
            k_ds = [pl.ds(r[1], kw) for r in rows]
            qs = [q_ref[r, :] for r in q_ds]
            ks = [k_ref[r, :] for r in k_ds]
            kbs = [k.astype(BF16) for k in ks]
            vbs = [v_ref[r, :].astype(BF16) for r in k_ds]
            dos = [do_ref[r, :] for r in q_ds]
            os_ = [o_ref[r, :] for r in q_ds]
            lss = [l_ref[r, :] for r in q_ds]
            qhs = [(qs[j] * masks[h]).astype(BF16) for j, h in tiles]
            dohs = [(dos[j] * masks[h]).astype(BF16) for j, h in tiles]
            deltas = [jnp.sum(dos[j] * os_[j] * masks[h], axis=1, keepdims=True) for j, h in tiles]
            ss = [_dot_nt(qh, kbs[j]) * SCALE + b_ref[h, rows[j][2]] for qh, (j, h) in zip(qhs, tiles)]
            ps = [jnp.exp(s - lss[j][:, HEAD_DIM * h:HEAD_DIM * h + 1]) for s, (j, h) in zip(ss, tiles)]
            dps = [_dot_nt(doh, vbs[j]) for doh, (j, h) in zip(dohs, tiles)]
            dss = [p * (dp - delta) for p, dp, delta in zip(ps, dps, deltas)]
            for ds, (j, h) in zip(dss, tiles):
                db_ref[h, rows[j][2]] += ds
            dsbs = [ds.astype(BF16) for ds in dss]
            dqs = [_dot_nn(dsb, (ks[j] * masks[h]).astype(BF16)) for dsb, (j, h) in zip(dsbs, tiles)]
            dkws = [_dot_tn(dsb, qh) for dsb, qh in zip(dsbs, qhs)]
            dvws = [_dot_tn(p.astype(BF16), doh) for p, doh in zip(ps, dohs)]
            for j in range(NA_BWD_ROWS):
                dq_ref[q_ds[j], :] = (dqs[2 * j] + dqs[2 * j + 1]) * SCALE
                dk_ref[k_ds[j], :] += (dkws[2 * j] + dkws[2 * j + 1]) * SCALE
                dv_ref[k_ds[j], :] += dvws[2 * j] + dvws[2 * j + 1]
            return carry

        lax.fori_loop(0, NA_ROWS // NA_BWD_ROWS, step, 0)

    c0 = QKV_A // 128
    own = BS((S, 128), lambda hp, b: (b, hp))
    tab = BS((2, NA_KR, GRID_W, kw), lambda hp, b: (hp, 0, 0, 0))
    return pl.pallas_call(
        body, name="na_bwd", grid=(4, t // S),
        in_specs=[BS((S, 128), lambda hp, b: (b, c0 + hp)), BS((S, 128), lambda hp, b: (b, c0 + 4 + hp)),
                  BS((S, 128), lambda hp, b: (b, c0 + 8 + hp)), tab, own, own, own],
        out_specs=[own, own, own, tab],
        out_shape=[SDS((t, 512), F32)] * 3 + [SDS((8, NA_KR, GRID_W, kw), F32)],
        compiler_params=_cp("parallel", "arbitrary"),
    )(proj, proj, proj, bias, dyb, yb, lse)


def _na_dbias_lane_map():
    kw = NA_KR * GRID_W
    lane = np.arange(kw)
    blk, m = lane // GRID_W, lane % GRID_W
    target = np.full(kw, -1)
    target[m < 16] = (blk * 32 + 15 + m)[m < 16]
    target[m >= 49] = (((blk + 1) % NA_KR) * 32 + m - 49)[m >= 49]
    return jnp.asarray(target[:, None] == np.arange(kw)[None, :], BF16)


def _na_dbias(db):
    kw = NA_KR * GRID_W

    def body(x_ref, map_ref, o_ref, z_ref):
        for cls in range(NA_KR):
            xv = x_ref[cls]
            y = xv[0:8]
            for g in range(1, GRID_W // 8):
                y = y + pltpu.roll(xv[8 * g:8 * g + 8], kw - 8 * g, 1)
            d = y[0:1]
            for s in range(1, 8):
                d = d + pltpu.roll(y[s:s + 1], kw - s, 1)
            z_ref[cls:cls + 1, :] = d
        z = z_ref[...]
        hi = z.astype(BF16)
        lo = (z - hi.astype(F32)).astype(BF16)
        e = _dot_nn(hi, map_ref[...]) + _dot_nn(lo, map_ref[...])
        out = e[0:1]
        for cls in range(1, NA_KR):
            out = out + pltpu.roll(e[cls:cls + 1], 32 * cls, 1)
        o_ref[...] = jnp.broadcast_to(out, (8, kw))

    return pl.pallas_call(
        body, name="na_dbias", grid=(8,),
        in_specs=[BS((None, NA_KR, GRID_W, kw), lambda h: (h, 0, 0, 0)), BS((kw, kw), lambda h: (0, 0))],
        out_specs=BS((None, 8, kw), lambda h: (h, 0, 0)), out_shape=SDS((8, 8, kw), F32),
        scratch_shapes=[pltpu.VMEM((8, kw), F32)], compiler_params=_cp("parallel"),
    )(db, _na_dbias_lane_map())


def _merge_fwd(ya, yb, proj, wat, wbt):
    t = ya.shape[0]
    tm, tn = 512, 256
    ca = (QKV_A + QKV_B) // tn
    cb = ca + D // tn

    def body(ya_ref, yb_ref, la_ref, lb_ref, wa_ref, wb_ref, m_ref, za_ref, zb_ref):
        za = _dot_nt(ya_ref[...].astype(BF16), wa_ref[...])
        zb = _dot_nt(yb_ref[...].astype(BF16), wb_ref[...])
        m_ref[...] = (jax.nn.sigmoid(la_ref[...]) * za + jax.nn.sigmoid(lb_ref[...]) * zb).astype(BF16)
        za_ref[...] = za.astype(BF16)
        zb_ref[...] = zb.astype(BF16)

    out = BS((tm, tn), lambda i, j: (i, j))
    return pl.pallas_call(
        body, name="merge_fwd", grid=(t // tm, D // tn),
        in_specs=[BS((tm, 256), lambda i, j: (i, 0)), BS((tm, 512), lambda i, j: (i, 0)),
                  BS((tm, tn), lambda i, j: (i, ca + j)), BS((tm, tn), lambda i, j: (i, cb + j)),
                  BS((tn, 256), lambda i, j: (j, 0)), BS((tn, 512), lambda i, j: (j, 0))],
        out_specs=[out, out, out], out_shape=[SDS((t, D), BF16)] * 3,
        compiler_params=_cp("parallel", "parallel"),
    )(ya, yb, proj, proj, wat, wbt)


def _merge_bwd(dm, za, zb, proj):
    t = dm.shape[0]
    tm, tn = 512, 256
    ca = (QKV_A + QKV_B) // tn
    cb = ca + D // tn

    def body(dm_ref, za_ref, zb_ref, la_ref, lb_ref, dza_ref, dzb_ref, dl_ref):
        dmv = dm_ref[...]
        ga = jax.nn.sigmoid(la_ref[...])
        gb = jax.nn.sigmoid(lb_ref[...])
        dza_ref[...] = (dmv * ga).astype(BF16)
        dzb_ref[...] = (dmv * gb).astype(BF16)
        dl_ref[0] = (dmv * za_ref[...].astype(F32) * ga * (1.0 - ga)).astype(BF16)
        dl_ref[1] = (dmv * zb_ref[...].astype(F32) * gb * (1.0 - gb)).astype(BF16)

    blk = BS((tm, tn), lambda i, j: (i, j))
    return pl.pallas_call(
        body, name="merge_bwd", grid=(t // tm, D // tn),
        in_specs=[blk, blk, blk, BS((tm, tn), lambda i, j: (i, ca + j)), BS((tm, tn), lambda i, j: (i, cb + j))],
        out_specs=[blk, blk, BS((2, tm, tn), lambda i, j: (0, i, j))],
        out_shape=[SDS((t, D), BF16), SDS((t, D), BF16), SDS((2, t, D), BF16)],
        compiler_params=_cp("parallel", "parallel"),
    )(dm, za, zb, proj, proj)


def _sum_slots(recv0, recv1, tag):
    _, r, c = recv0.shape
    tr = r if r * c <= 512 * 1024 else r // 2

    def body(a_ref, b_ref, o_ref):
        for layer, ref in enumerate((a_ref, b_ref)):
            acc = ref[0].astype(F32)
            for s in range(1, N_DEV):
                acc = acc + ref[s].astype(F32)
            o_ref[layer] = acc

    blk = BS((N_DEV, tr, c), lambda i: (0, i, 0))
    return pl.pallas_call(
        body, name=f"sum_slots_{tag}", grid=(r // tr,), in_specs=[blk, blk],
        out_specs=BS((2, tr, c), lambda i: (0, i, 0)), out_shape=SDS((2, r, c), F32),
        compiler_params=_cp("parallel"),
    )(recv0, recv1)


def _adamw(w, g, m, v, tag):
    layers, r, c = w.shape
    tr = next(r // k for k in (1, 2, 4, 8) if r // k <= 384 and r % (8 * k) == 0)

    def body(w_ref, g_ref, m_ref, v_ref, d_ref, mo_ref, vo_ref):
        gv = g_ref[...]
        mn = ADAM_B1 * m_ref[...] + (1.0 - ADAM_B1) * gv
        vn = ADAM_B2 * v_ref[...] + (1.0 - ADAM_B2) * (gv * gv)
        m_hat = mn / (1.0 - ADAM_B1 ** ADAM_STEP)
        v_hat = vn / (1.0 - ADAM_B2 ** ADAM_STEP)
        d_ref[...] = -ADAM_LR * (m_hat / (jnp.sqrt(v_hat) + ADAM_EPS) + ADAM_WD * w_ref[...])
        mo_ref[...] = mn
        vo_ref[...] = vn

    blk = BS((None, tr, c), lambda l, i: (l, i, 0))
    return pl.pallas_call(
        body, name=f"adamw_{tag}", grid=(layers, r // tr), in_specs=[blk] * 4, out_specs=[blk] * 3,
        out_shape=[SDS((layers, r, c), F32)] * 3, compiler_params=_cp("parallel", "parallel"),
    )(w, g, m, v)


def _place():
    return lax.axis_index("x"), lax.axis_index("y"), lax.axis_index("c")


def _flip(coord, bit):
    return 1 - coord if bit else coord


def _allgather(shards, tag):
    n_arr = len(shards)
    hbm = BS(memory_space=pl.ANY)

    def body(*refs):
        ins, outs = refs[:n_arr], refs[n_arr:2 * n_arr]
        send_sems, recv_sems, local_sems = refs[2 * n_arr:]
        x, y, c = _place()
        me, sibling = (x, y, c), (x, y, 1 - c)
        chips = [(1 - x, y), (x, 1 - y), (1 - x, 1 - y)]

        def rows(a, p):
            r = shards[a].shape[0]
            return outs[a].at[pl.ds((4 * p[0] + 2 * p[1] + p[2]) * r, r), :]

        def copy(a, k, block, to, src=None):
            return pltpu.make_async_remote_copy(
                src_ref=rows(a, block) if src is None else src, dst_ref=rows(a, block),
                send_sem=send_sems.at[a, k], recv_sem=recv_sems.at[a, k], device_id=to, device_id_type=MESH)

        mine = [pltpu.make_async_copy(ins[a], rows(a, me), local_sems.at[a]) for a in range(n_arr)]
        for cp in mine:
            cp.start()
        first = []
        for a in range(n_arr):
            first.append(copy(a, 0, me, sibling, src=ins[a]))
            first += [copy(a, 1 + j, me, (*chip, c), src=ins[a]) for j, chip in enumerate(chips)]
        for cp in first:
            cp.start()
        passed = []
        for a in range(n_arr):
            for j, chip in enumerate(chips):
                copy(a, 1 + j, (*chip, c), me).wait_recv()
                passed.append(copy(a, 4 + j, (*chip, c), sibling))
                passed[-1].start()
        for a in range(n_arr):
            copy(a, 0, sibling, me).wait_recv()
            for j, chip in enumerate(chips):
                copy(a, 4 + j, (*chip, 1 - c), me).wait_recv()
        for cp in first + passed:
            cp.wait_send()
        for cp in mine:
            cp.wait()

    return pl.pallas_call(
        body, name=f"allgather_{tag}", in_specs=[hbm] * n_arr, out_specs=[hbm] * n_arr,
        out_shape=[SDS((N_DEV * s.shape[0], s.shape[1]), s.dtype) for s in shards],
        scratch_shapes=[pltpu.SemaphoreType.DMA((n_arr, 7)), pltpu.SemaphoreType.DMA((n_arr, 7)),
                        pltpu.SemaphoreType.DMA((n_arr,))],
        compiler_params=pltpu.CompilerParams(has_side_effects=True),
    )(*shards)


def _peers(x, y, c):
    peers = []
    for mask in range(1, N_DEV):
        p = (_flip(x, mask & 4), _flip(y, mask & 2), _flip(c, mask & 1))
        peers.append((p, 4 * p[0] + 2 * p[1] + p[2]))
    return peers


def _exchange_refs(mode, src, land, me, peer):
    if mode == "gather":
        r = src.shape[0]
        return src, land.at[pl.ds(me * r, r), :], land.at[pl.ds(peer * r, r), :]
    r = land.shape[1]
    return src.at[pl.ds(peer * r, r), :], land.at[me], land.at[peer]


HBM_SPEC = BS(memory_space=pltpu.HBM)
SEM_SPEC = BS(memory_space=pltpu.SEMAPHORE)
DATAFLOW = pltpu.SideEffectType.DATAFLOW_SIDE_EFFECTING


def _own_block_placed(mode, src, me):
    if mode == "gather":
        r, c = src.shape
        return lax.dynamic_update_slice(lax.empty((N_DEV * r, c), src.dtype), src, (me * r, 0))
    r, c = src.shape[0] // N_DEV, src.shape[1]
    own = lax.dynamic_slice(src, (me * r, 0), (r, c))
    return lax.dynamic_update_slice(lax.empty((N_DEV, r, c), src.dtype), own[None], (me, 0, 0))


def _exchange_start(mode, srcs, after, tag):
    n = len(srcs)
    x, y, c = _place()
    lands = [_own_block_placed(mode, s, 4 * x + 2 * y + c) for s in srcs]
    behind = [] if after is None else [after]

    def body(*refs):
        src_refs, land_refs = refs[:n], refs[n:2 * n]
        send_sems, recv_sems = refs[2 * n + len(behind)], refs[2 * n + len(behind) + 1]
        token = refs[-1]
        bx, by, bc = _place()
        me = 4 * bx + 2 * by + bc
        for a in range(n):
            for k, (p, idx) in enumerate(_peers(bx, by, bc)):
                out, there, _ = _exchange_refs(mode, src_refs[a], land_refs[a], me, idx)
                pltpu.make_async_remote_copy(
                    src_ref=out, dst_ref=there, send_sem=send_sems.at[7 * a + k], recv_sem=recv_sems.at[7 * a + k],
                    device_id=p, device_id_type=MESH).start()
        token[...] = jnp.zeros_like(token)

    res = pl.pallas_call(
        body, name=f"{mode}_start_{tag}",
        out_shape=(pltpu.SemaphoreType.DMA((7 * n,)), pltpu.SemaphoreType.DMA((7 * n,)),
                   *[pltpu.HBM(s.shape, s.dtype) for s in srcs], *[pltpu.HBM(l.shape, l.dtype) for l in lands],
                   SDS((8, 128), F32)),
        in_specs=[HBM_SPEC] * (2 * n) + [BS(memory_space=pl.ANY)] * len(behind),
        out_specs=(SEM_SPEC, SEM_SPEC, *[HBM_SPEC] * (2 * n), BS(memory_space=pltpu.VMEM)),
        input_output_aliases={i: 2 + i for i in range(2 * n)},
        compiler_params=pltpu.CompilerParams(has_side_effects=DATAFLOW),
    )(*[pltpu.with_memory_space_constraint(s, pltpu.HBM) for s in srcs],
      *[pltpu.with_memory_space_constraint(l, pltpu.HBM) for l in lands], *behind)
    return (mode, res[0], res[1], res[2:2 + n], res[2 + n:2 + 2 * n]), res[-1]


def _exchange_wait(handle, after, tag):
    mode, send_sems, recv_sems, srcs, lands = handle
    n = len(srcs)

    def body(*refs):
        src_refs, land_refs = refs[:n], refs[n:2 * n]
        send_ref, recv_ref = refs[2 * n], refs[2 * n + 1]
        bx, by, bc = _place()
        me = 4 * bx + 2 * by + bc
        for a in range(n):
            for k, (p, idx) in enumerate(_peers(bx, by, bc)):
                out, _, here = _exchange_refs(mode, src_refs[a], land_refs[a], me, idx)
                cp = pltpu.make_async_remote_copy(
                    src_ref=out, dst_ref=here, send_sem=send_ref.at[7 * a + k], recv_sem=recv_ref.at[7 * a + k],
                    device_id=p, device_id_type=MESH)
                cp.wait_send()
                cp.wait_recv()

    res = pl.pallas_call(
        body, name=f"{mode}_wait_{tag}",
        out_shape=(*[pltpu.HBM(s.shape, s.dtype) for s in srcs], *[pltpu.HBM(l.shape, l.dtype) for l in lands]),
        in_specs=[HBM_SPEC] * (2 * n) + [SEM_SPEC, SEM_SPEC, BS(memory_space=pl.ANY)],
        out_specs=tuple([HBM_SPEC] * (2 * n)),
        input_output_aliases={i: i for i in range(2 * n)},
        compiler_params=pltpu.CompilerParams(has_side_effects=DATAFLOW),
    )(*srcs, *lands, send_sems, recv_sems, after)
    return list(res[n:])


def _allreduce_small(vec, behind):
    rows = vec.shape[0]

    def body(x_ref, behind_ref, o_ref, buf_ref, send_sems, recv_sems):
        x, y, c = _place()
        me = 4 * x + 2 * y + c
        buf_ref[me] = x_ref[...]
        peers = _peers(x, y, c)

        def copy(k, slot):
            return pltpu.make_async_remote_copy(
                src_ref=x_ref, dst_ref=buf_ref.at[slot], send_sem=send_sems.at[k], recv_sem=recv_sems.at[k],
                device_id=peers[k][0], device_id_type=MESH)

        sends = [copy(k, me) for k in range(N_DEV - 1)]
        for cp in sends:
            cp.start()
        for k in range(N_DEV - 1):
            copy(k, peers[k][1]).wait_recv()
        for cp in sends:
            cp.wait_send()
        acc = buf_ref[0]
        for s in range(1, N_DEV):
            acc = acc + buf_ref[s]
        o_ref[...] = acc

    vmem = BS(memory_space=pltpu.VMEM)
    return pl.pallas_call(
        body, name="allreduce_small", in_specs=[vmem, BS(memory_space=pl.ANY)], out_specs=vmem,
        out_shape=SDS((rows, 128), F32),
        scratch_shapes=[pltpu.VMEM((N_DEV, rows, 128), F32), pltpu.SemaphoreType.DMA((7,)),
                        pltpu.SemaphoreType.DMA((7,))],
        compiler_params=pltpu.CompilerParams(has_side_effects=True),
    )(vec, behind)


def _ffn_forward(x, norm_g, fetch, names, tag):
    hn = _norm_fwd(x, norm_g, tag)
    gu, act = _ffn_up(hn, fetch(names[0], hn).reshape(2, F, D), tag)
    out = _mm_nn(act[None], fetch(names[1], act)[None], f"down_{tag}", res=x, scale=0.5)
    return out, (x, hn, gu, act)


def _ffn_backward(dxo, dxo_b, saved, norm_g, wut, wd, tag, send):
    x, hn, gu, act = saved
    d_wd = _mm_tn(act[None], dxo_b, f"dwd_{tag}", scale=0.5)[0]
    du = _ffn_dact(dxo_b, wd, gu, send(("down",), [d_wd]), tag)
    d_wut = _mm_tn(du, hn, f"dwu_{tag}")
    token = send(("up",), [d_wut.reshape(2 * F, D)])
    dhn = _mm_nn(du, wut, f"dhn_{tag}")
    return _norm_bwd(x, norm_g + token[0, 0], dhn, dxo, tag)


def _mixer_forward(x, norm_g, fetch, bias, tables, tag):
    hn = _norm_fwd(x, norm_g, tag)
    proj = _mm_nt_rows(hn, fetch("win", hn), f"proj_{tag}", 512, IN_W // 2, IN_W, 0)
    qkr = _rope_fwd(proj, *tables)
    outs, lses = [], []
    for grp in range(3):
        o, l = _dil_fwd(qkr, proj, grp)
        outs.append(o)
        lses.append(l)
    ya = _combine_fwd(outs, lses)
    yb, lse_b = _na_fwd(proj, bias)
    merged, za, zb = _merge_fwd(ya, yb, proj, fetch("wa", yb), fetch("wb", yb))
    out = _mm_nn(merged[None], fetch("wo", merged)[None], f"out_{tag}", res=x)
    return out, (x, hn, proj, qkr, outs, lses, ya, yb, lse_b, merged, za, zb)


def _mixer_backward(dxo, dxo_b, saved, norm_g, w, bias, tables, tag, send):
    wint, wat, wbt, wo = w
    x, hn, proj, qkr, outs, lses, ya, yb, lse_b, merged, za, zb = saved
    dm = _mm_nt_rows(dxo_b, wo, f"dmerged_{tag}", 512, D, D, 0)
    d_wo = _mm_tn(merged[None], dxo_b, f"dwo_{tag}")[0]
    dza, dzb, dlog = _merge_bwd(dm, za, zb, proj)
    dya = _mm_nn(dza[None], wat[None], f"dya_{tag}")
    dyb = _mm_nn(dzb[None], wbt[None], f"dyb_{tag}")
    d_wat = _mm_tn(dza[None], ya, f"dwa_{tag}")[0]
    d_wbt = _mm_tn(dzb[None], yb, f"dwb_{tag}")[0]
    cb = _combine_bwd(dya, outs, lses)
    dqs, dks, dvs = [], [], []
    for grp in range(3):
        dq, dk, dv = _dil_bwd(qkr, proj, cb[grp], cb[3 + grp], lses[grp], grp)
        dqs.append(dq)
        dks.append(dk)
        dvs.append(dv)
    dqk = _rope_bwd(dqs, dks, *tables)
    dqb, dkb, dvb, dbias_tab = _na_bwd(proj, bias, dyb, yb, lse_b)
    dbias = _na_dbias(dbias_tab)
    dproj = jnp.concatenate(
        [dqk] + [t.astype(BF16) for t in (*dvs, dqb, dkb, dvb)] + [dlog[0], dlog[1]], axis=1)
    d_wint = _mm_tn(dproj[None], hn, f"dwin_{tag}")[0]
    token = send(("win", "wa", "wb", "wo"), [d_wint, d_wat, d_wbt, d_wo])
    dhn = _mm_nn(dproj[None], wint[None], f"dhnm_{tag}", tm=256)
    dx, dx_b, dg = _norm_bwd(x, norm_g + token[0, 0], dhn, dxo, f"mix_{tag}")
    dbias = dbias[:, 0, :480].reshape(8, 15, 32)[:, :, :31]
    return dx, dx_b, dg, dbias


def _pack_small(norms, biases, final, loss=None):
    parts = []
    for layer in range(DEPTH):
        parts += [norms[0][layer], norms[1][layer], norms[2][layer],
                  jnp.pad(biases[layer].reshape(-1), (0, BIAS_PAD - 8 * 15 * 31))]
    parts.append(final)
    flat = jnp.concatenate([p.reshape(-1).astype(F32) for p in parts])
    if loss is not None:
        flat = jnp.concatenate([flat, loss.reshape(-1)])
    return jnp.pad(flat, (0, SMALL_ROWS * 128 - flat.shape[0])).reshape(SMALL_ROWS, 128)


def _unpack_small(packed):
    flat = packed.reshape(-1)
    norms, biases = ([], [], []), []
    pos = 0
    for _ in range(DEPTH):
        for k in range(3):
            norms[k].append(flat[pos:pos + D])
            pos += D
        biases.append(flat[pos:pos + 8 * 15 * 31].reshape(8, 15, 31))
        pos += BIAS_PAD
    final = flat[pos:pos + D]
    pos += D
    return [jnp.stack(n) for n in norms], jnp.stack(biases), final, flat[pos]


def kernel(x, ffn1_norm, ffn1_w_up, ffn1_w_down, mix_norm, w_in, na_rel_bias, w_branch_a, w_branch_b, w_out, ffn2_norm, ffn2_w_up, ffn2_w_down, final_norm, loss_target, m_ffn1_norm, m_ffn1_w_up, m_ffn1_w_down, m_mix_norm, m_w_in, m_na_rel_bias, m_w_branch_a, m_w_branch_b, m_w_out, m_ffn2_norm, m_ffn2_w_up, m_ffn2_w_down, m_final_norm, v_ffn1_norm, v_ffn1_w_up, v_ffn1_w_down, v_mix_norm, v_w_in, v_na_rel_bias, v_w_branch_a, v_w_branch_b, v_w_out, v_ffn2_norm, v_ffn2_w_up, v_ffn2_w_down, v_final_norm):
    t = x.shape[0] * x.shape[1]
    xs = x.reshape(t, D)
    tgt = loss_target.reshape(t, D)
    tables = _rope_tables()

    col_sharded = dict(up1=ffn1_w_up, win=w_in, wa=w_branch_a, wb=w_branch_b, up2=ffn2_w_up)
    row_sharded = dict(down1=ffn1_w_down, wo=w_out, down2=ffn2_w_down)
    shard = [{} for _ in range(DEPTH)]
    for layer in range(DEPTH):
        for name, arr in col_sharded.items():
            shard[layer][name] = arr[layer].T.astype(BF16)
        for name, arr in row_sharded.items():
            shard[layer][name] = arr[layer].astype(BF16)

    weights = [{} for _ in range(DEPTH)]
    weights[0]["up1"] = _allgather([shard[0]["up1"]], "first")[0]
    travel = [(0, ("down1",)), (0, ("win",)), (0, ("wa", "wb", "wo")), (0, ("up2", "down2")),
              (1, ("up1", "down1")), (1, ("win",)), (1, ("wa", "wb", "wo")), (1, ("up2", "down2"))]
    pending = {}
    after = weights[0]["up1"]
    for i, (layer, names) in enumerate(travel):
        handle, after = _exchange_start("gather", [shard[layer][n] for n in names], after, f"w{i}")
        for n in names:
            pending[layer, n] = (i, handle, names)
    zero = after[0, 0]

    def fetcher(layer):
        def fetch(name, behind):
            if (layer, name) in pending:
                i, handle, names = pending[layer, name]
                for n, got in zip(names, _exchange_wait(handle, behind, f"w{i}")):
                    weights[layer][n] = got
                    del pending[layer, n]
            return weights[layer][name]
        return fetch

    saved = []
    h = xs
    for layer in range(DEPTH):
        bias = _na_bias_table(na_rel_bias[layer])
        fetch = fetcher(layer)
        h, s1 = _ffn_forward(h, ffn1_norm[layer] + zero, fetch, ("up1", "down1"), f"f1l{layer}")
        h, s2 = _mixer_forward(h, mix_norm[layer], fetch, bias, tables, f"l{layer}")
        h, s3 = _ffn_forward(h, ffn2_norm[layer], fetch, ("up2", "down2"), f"f2l{layer}")
        saved.append((s1, s2, s3, bias))
    loss_part, dh, dh_b, d_final = _loss_head(h, final_norm, tgt)

    d_norms = ([None] * DEPTH, [None] * DEPTH, [None] * DEPTH)
    d_bias = [None] * DEPTH
    sent = {}

    def sender(layer, suffix):
        def send(names, grads):
            tag = f"g{layer}{names[0]}{suffix}"
            handle, token = _exchange_start("scatter", grads, None, tag)
            for i, n in enumerate(names):
                sent[layer, n + suffix] = (handle, i, tag)
            return token
        return send

    for layer in reversed(range(DEPTH)):
        w = weights[layer]
        s1, s2, s3, bias = saved[layer]
        dh, dh_b, d_norms[2][layer] = _ffn_backward(
            dh, dh_b, s3, ffn2_norm[layer], w["up2"].reshape(2, F, D), w["down2"], f"f2l{layer}", sender(layer, "2"))
        dh, dh_b, d_norms[1][layer], d_bias[layer] = _mixer_backward(
            dh, dh_b, s2, mix_norm[layer], (w["win"], w["wa"], w["wb"], w["wo"]), bias, tables, f"l{layer}",
            sender(layer, ""))
        dh, dh_b, d_norms[0][layer] = _ffn_backward(
            dh, dh_b, s1, ffn1_norm[layer], w["up1"].reshape(2, F, D), w["down1"], f"f1l{layer}", sender(layer, "1"))
    grad_x = dh.reshape(x.shape)

    originals = dict(up1=(ffn1_w_up, m_ffn1_w_up, v_ffn1_w_up), down1=(ffn1_w_down, m_ffn1_w_down, v_ffn1_w_down),
                     win=(w_in, m_w_in, v_w_in), wa=(w_branch_a, m_w_branch_a, v_w_branch_a),
                     wb=(w_branch_b, m_w_branch_b, v_w_branch_b), wo=(w_out, m_w_out, v_w_out),
                     up2=(ffn2_w_up, m_ffn2_w_up, v_ffn2_w_up), down2=(ffn2_w_down, m_ffn2_w_down, v_ffn2_w_down))
    big = {}
    behind = dh
    landed = {}

    def received(layer, name):
        handle, i, tag = sent[layer, name]
        if tag not in landed:
            landed[tag] = _exchange_wait(handle, behind, tag)
        return landed[tag][i]

    for name in ("down2", "up2", "win", "wa", "wb", "wo", "down1", "up1"):
        g = _sum_slots(received(0, name), received(1, name), name)
        wv, mv, vv = originals[name]
        if name in col_sharded:
            wv, mv, vv = (jnp.swapaxes(t, 1, 2) for t in (wv, mv, vv))
        big[name] = (g, *_adamw(wv, g, mv, vv, name))
        behind = big[name][1]
        if name in col_sharded:
            big[name] = tuple(jnp.swapaxes(t, 1, 2) for t in big[name])

    small = _allreduce_small(_pack_small(d_norms, d_bias, d_final, loss_part[0, :1]), behind)
    g_norms, g_bias, g_final, loss = _unpack_small(small)
    w_small = _pack_small((ffn1_norm, mix_norm, ffn2_norm), na_rel_bias, final_norm)
    m_small = _pack_small((m_ffn1_norm, m_mix_norm, m_ffn2_norm), m_na_rel_bias, m_final_norm)
    v_small = _pack_small((v_ffn1_norm, v_mix_norm, v_ffn2_norm), v_na_rel_bias, v_final_norm)
    upd = _adamw(w_small[None], small[None], m_small[None], v_small[None], "small")
    small_out = [(g_norms, g_bias, g_final)] + [_unpack_small(u[0])[:3] for u in upd]

    outputs = [loss, grad_x]
    for kind in range(4):
        norms, bias_k, final_k = small_out[kind]
        outputs += [norms[0], big["up1"][kind], big["down1"][kind], norms[1], big["win"][kind], bias_k,
                    big["wa"][kind], big["wb"][kind], big["wo"][kind], norms[2], big["up2"][kind],
                    big["down2"][kind], final_k]
    return tuple(outputs)
```

```python
import numpy as np

import jax
import jax.numpy as jnp
from jax import lax
from jax.experimental import pallas as pl
from jax.experimental.pallas import tpu as pltpu

F32 = jnp.float32
BF16 = jnp.bfloat16
SDS = jax.ShapeDtypeStruct
BS = pl.BlockSpec
MESH = pl.DeviceIdType.MESH

D = 1024
S = 2048
F = 2816
DEPTH = 2
HEAD_DIM = 64
DILATIONS = (1, 4, 16)
HALF = 64
QKV_A = 2304
QKV_B = 1536
IN_W = 5888
N_DEV = 8
NA_ROWS = 32
GRID_W = 64
NA_KR = 8
ROPE_THETA = 10000.0
RMS_EPS = 1e-6
NEG = -1e30
SCALE = HEAD_DIM ** -0.5
ADAM_LR, ADAM_B1, ADAM_B2, ADAM_EPS, ADAM_WD, ADAM_STEP = 0.001, 0.9, 0.999, 1e-08, 0.01, 10
VMEM_LIMIT_V7X = 52 * 1024 * 1024
SMALL_ROWS = 120
BIAS_PAD = 3840
NA_FWD_ROWS = 4
NA_BWD_ROWS = 4
DIL_FWD_TILES = 4
DIL_BWD_TILES = 2


def _cp(*sem):
    return pltpu.CompilerParams(dimension_semantics=sem, vmem_limit_bytes=VMEM_LIMIT_V7X)


def _dot_nn(a, b):
    return jnp.dot(a, b, preferred_element_type=F32)


def _dot_nt(a, b):
    return lax.dot_general(a, b, (((1,), (1,)), ((), ())), preferred_element_type=F32)


def _dot_tn(a, b):
    return lax.dot_general(a, b, (((0,), (0,)), ((), ())), preferred_element_type=F32)


def _ds(start, size, stride):
    return pl.ds(start, size) if stride == 1 else pl.ds(start, size, stride=stride)


def _norm_fwd(x, g, tag):
    t = x.shape[0]
    tm = 512

    def body(x_ref, g_ref, o_ref):
        xv = x_ref[...]
        r = lax.rsqrt(jnp.mean(xv * xv, axis=-1, keepdims=True) + RMS_EPS)
        o_ref[...] = (xv * r * g_ref[...]).astype(BF16)

    return pl.pallas_call(
        body, name=f"norm_fwd_{tag}", grid=(t // tm,),
        in_specs=[BS((tm, D), lambda i: (i, 0)), BS((1, D), lambda i: (0, 0))],
        out_specs=BS((tm, D), lambda i: (i, 0)),
        out_shape=SDS((t, D), BF16), compiler_params=_cp("parallel"),
    )(x, g.reshape(1, D))


def _norm_bwd(x, g, dh, dres, tag):
    t = x.shape[0]
    tm = 512

    def body(x_ref, g_ref, dh_ref, dr_ref, dx_ref, dxb_ref, dg_ref):
        @pl.when(pl.program_id(0) == 0)
        def _():
            dg_ref[...] = jnp.zeros_like(dg_ref)

        xv = x_ref[...]
        r = lax.rsqrt(jnp.mean(xv * xv, axis=-1, keepdims=True) + RMS_EPS)
        xh = xv * r
        dh = dh_ref[...]
        u = dh * g_ref[...]
        dx = dr_ref[...] + r * (u - xh * jnp.mean(xh * u, axis=-1, keepdims=True))
        dx_ref[...] = dx
        dxb_ref[...] = dx.astype(BF16)
        dg_ref[...] += jnp.sum(dh * xh, axis=0, keepdims=True)

    row = BS((tm, D), lambda i: (i, 0))
    vec = BS((1, D), lambda i: (0, 0))
    return pl.pallas_call(
        body, name=f"norm_bwd_{tag}", grid=(t // tm,),
        in_specs=[row, vec, row, row], out_specs=[row, row, vec],
        out_shape=[SDS((t, D), F32), SDS((t, D), BF16), SDS((1, D), F32)], compiler_params=_cp("arbitrary"),
    )(x, g.reshape(1, D), dh, dres)


def _loss_head(x, g, tgt):
    t = x.shape[0]
    tm = 512

    def body(x_ref, g_ref, t_ref, loss_ref, dx_ref, dxb_ref, dg_ref):
        @pl.when(pl.program_id(0) == 0)
        def _():
            dg_ref[...] = jnp.zeros_like(dg_ref)
            loss_ref[...] = jnp.zeros_like(loss_ref)

        xv = x_ref[...]
        gv = g_ref[...]
        r = lax.rsqrt(jnp.mean(xv * xv, axis=-1, keepdims=True) + RMS_EPS)
        xh = xv * r
        e = xh * gv - t_ref[...]
        loss_ref[...] += 0.5 * jnp.sum(jnp.mean(e * e, axis=-1, keepdims=True), axis=0, keepdims=True)
        dy = e * (1.0 / D)
        u = dy * gv
        dx = r * (u - xh * jnp.mean(xh * u, axis=-1, keepdims=True))
        dx_ref[...] = dx
        dxb_ref[...] = dx.astype(BF16)
        dg_ref[...] += jnp.sum(dy * xh, axis=0, keepdims=True)

    row = BS((tm, D), lambda i: (i, 0))
    vec = BS((1, D), lambda i: (0, 0))
    return pl.pallas_call(
        body, name="loss_head", grid=(t // tm,),
        in_specs=[row, vec, row], out_specs=[BS((1, 128), lambda i: (0, 0)), row, row, vec],
        out_shape=[SDS((1, 128), F32), SDS((t, D), F32), SDS((t, D), BF16), SDS((1, D), F32)],
        compiler_params=_cp("arbitrary"),
    )(x, g.reshape(1, D), tgt)


def _mm_nn(a, w, tag, res=None, scale=1.0, tm=512, tn=None):
    c_n, t, k = a.shape
    n = w.shape[2]
    tn = n if tn is None else tn

    def body(*refs):
        a_ref, w_ref = refs[0], refs[1]
        o_ref = refs[-1]
        acc = _dot_nn(a_ref[0].astype(BF16), w_ref[0])
        for c in range(1, c_n):
            acc = acc + _dot_nn(a_ref[c].astype(BF16), w_ref[c])
        if scale != 1.0:
            acc = acc * scale
        if res is not None:
            acc = refs[2][...] + acc
        o_ref[...] = acc

    in_specs = [BS((c_n, tm, k), lambda i, j: (0, i, 0)), BS((c_n, k, tn), lambda i, j: (0, 0, j))]
    args = [a, w]
    if res is not None:
        in_specs.append(BS((tm, tn), lambda i, j: (i, j)))
        args.append(res)
    return pl.pallas_call(
        body, name=f"mm_nn_{tag}", grid=(t // tm, n // tn), in_specs=in_specs,
        out_specs=BS((tm, tn), lambda i, j: (i, j)), out_shape=SDS((t, n), F32),
        compiler_params=_cp("parallel", "parallel"),
    )(*args)


def _mm_nn_norm_bwd(a, w, x, g, dres, tag, tm=256):
    c_n, t, k = a.shape

    def body(a_ref, w_ref, x_ref, g_ref, dr_ref, dx_ref, dxb_ref, dg_ref):
        @pl.when(pl.program_id(0) == 0)
        def _():
            dg_ref[...] = jnp.zeros_like(dg_ref)

        dh = _dot_nn(a_ref[0], w_ref[0])
        for c in range(1, c_n):
            dh = dh + _dot_nn(a_ref[c], w_ref[c])
        xv = x_ref[...]
        r = lax.rsqrt(jnp.mean(xv * xv, axis=-1, keepdims=True) + RMS_EPS)
        xh = xv * r
        u = dh * g_ref[...]
        dx = dr_ref[...] + r * (u - xh * jnp.mean(xh * u, axis=-1, keepdims=True))
        dx_ref[...] = dx
        dxb_ref[...] = dx.astype(BF16)
        dg_ref[...] += jnp.sum(dh * xh, axis=0, keepdims=True)

    row = BS((tm, D), lambda i: (i, 0))
    vec = BS((1, D), lambda i: (0, 0))
    return pl.pallas_call(
        body, name=f"mm_nn_norm_bwd_{tag}", grid=(t // tm,),
        in_specs=[BS((c_n, tm, k), lambda i: (0, i, 0)), BS((c_n, k, D), lambda i: (0, 0, 0)), row, vec, row],
        out_specs=[row, row, vec], out_shape=[SDS((t, D), F32), SDS((t, D), BF16), SDS((1, D), F32)],
        compiler_params=_cp("arbitrary"),
    )(a, w, x, g.reshape(1, D), dres)


def _mm_nt_rows(a, w, tag, tm, tn, n_total, w_row0):
    t, k = a.shape
    assert w_row0 % tn == 0 and n_total % tn == 0
    j0 = w_row0 // tn

    def body(a_ref, w_ref, o_ref):
        o_ref[...] = _dot_nt(a_ref[...].astype(BF16), w_ref[...])

    return pl.pallas_call(
        body, name=f"mm_nt_{tag}", grid=(n_total // tn, t // tm),
        in_specs=[BS((tm, k), lambda j, i: (i, 0)), BS((tn, k), lambda j, i: (j0 + j, 0))],
        out_specs=BS((tm, tn), lambda j, i: (i, j)), out_shape=SDS((t, n_total), F32),
        compiler_params=_cp("parallel", "parallel"),
    )(a, w)


def _mm_tn(a, b, tag, scale=1.0, tmm=256, tk=None):
    c_n, t, m = a.shape
    n = b.shape[1]
    tk = t if tk is None else tk
    nk = t // tk

    def body_one(a_ref, b_ref, o_ref):
        o_ref[...] = (_dot_tn(a_ref[...].astype(BF16), b_ref[...].astype(BF16)) * scale).astype(BF16)

    def body_acc(a_ref, b_ref, o_ref, acc_ref):
        kk = pl.program_id(2)

        @pl.when(kk == 0)
        def _():
            acc_ref[...] = jnp.zeros_like(acc_ref)

        acc_ref[...] += _dot_tn(a_ref[...].astype(BF16), b_ref[...].astype(BF16))

        @pl.when(kk == nk - 1)
        def _():
            o_ref[...] = (acc_ref[...] * scale).astype(BF16)

    return pl.pallas_call(
        body_one if nk == 1 else body_acc, name=f"mm_tn_{tag}", grid=(c_n, m // tmm, nk),
        in_specs=[BS((None, tk, tmm), lambda c, mi, kk: (c, kk, mi)), BS((tk, n), lambda c, mi, kk: (kk, 0))],
        out_specs=BS((None, tmm, n), lambda c, mi, kk: (c, mi, 0)),
        out_shape=SDS((c_n, m, n), BF16), scratch_shapes=[] if nk == 1 else [pltpu.VMEM((tmm, n), F32)],
        compiler_params=_cp("parallel", "parallel", "arbitrary"),
    )(a, b)


def _ffn_up(hn, wut, tag):
    t = hn.shape[0]
    tm, tn = 512, 1408

    def body(h_ref, w_ref, gu_ref, act_ref):
        h = h_ref[...]
        g = _dot_nt(h, w_ref[0])
        u = _dot_nt(h, w_ref[1])
        sg = jax.nn.sigmoid(g)
        silu = g * sg
        gu_ref[0] = (u * (sg + silu * (1.0 - sg))).astype(BF16)
        gu_ref[1] = silu.astype(BF16)
        act_ref[...] = (silu * u).astype(BF16)

    return pl.pallas_call(
        body, name=f"ffn_up_{tag}", grid=(F // tn, t // tm),
        in_specs=[BS((tm, D), lambda j, i: (i, 0)), BS((2, tn, D), lambda j, i: (0, j, 0))],
        out_specs=[BS((2, tm, tn), lambda j, i: (0, i, j)), BS((tm, tn), lambda j, i: (i, j))],
        out_shape=[SDS((2, t, F), BF16), SDS((t, F), BF16)],
        compiler_params=_cp("parallel", "parallel"),
    )(hn, wut)


def _ffn_dact(dxo, wd, gu, tie, tag):
    t = dxo.shape[0]
    tm, tn = 512, 1408

    def body(d_ref, w_ref, gu_ref, tie_ref, o_ref):
        dact = _dot_nt(d_ref[...] * 0.5, w_ref[...])
        o_ref[0] = (dact * gu_ref[0].astype(F32)).astype(BF16)
        o_ref[1] = (dact * gu_ref[1].astype(F32)).astype(BF16)

    return pl.pallas_call(
        body, name=f"ffn_dact_{tag}", grid=(F // tn, t // tm),
        in_specs=[BS((tm, D), lambda j, i: (i, 0)), BS((tn, D), lambda j, i: (j, 0)),
                  BS((2, tm, tn), lambda j, i: (0, i, j)), BS((8, 128), lambda j, i: (0, 0))],
        out_specs=BS((2, tm, tn), lambda j, i: (0, i, j)),
        out_shape=SDS((2, t, F), BF16), compiler_params=_cp("parallel", "parallel"),
    )(dxo, wd, gu, tie)


def _rope_tables():
    half = HEAD_DIM // 2
    inv_freq = ROPE_THETA ** (-jnp.arange(half, dtype=F32) / half)
    ang = jnp.arange(S).astype(F32)[:, None] * inv_freq[None, :]
    cos, sin = jnp.cos(ang), jnp.sin(ang)
    return jnp.concatenate([cos, cos, cos, cos], axis=1), jnp.concatenate([-sin, sin, -sin, sin], axis=1)


def _swap_halves(t, first_half):
    return jnp.where(first_half, pltpu.roll(t, 96, 1), pltpu.roll(t, 32, 1))


def _rope_fwd(proj, cos_t, sin_t):
    t = proj.shape[0]
    tm = 512
    width = 2 * QKV_A // 3

    def body(x_ref, c_ref, s_ref, o_ref):
        c = c_ref[...]
        sg = s_ref[...]
        first = (lax.broadcasted_iota(jnp.int32, (tm, 128), 1) % HEAD_DIM) < HEAD_DIM // 2
        for j in range(width // 128):
            v = x_ref[:, 128 * j:128 * (j + 1)]
            o_ref[:, 128 * j:128 * (j + 1)] = v * c + _swap_halves(v, first) * sg

    tab = BS((tm, 128), lambda i: (i % (S // tm), 0))
    return pl.pallas_call(
        body, name="rope_fwd", grid=(t // tm,),
        in_specs=[BS((tm, width), lambda i: (i, 0)), tab, tab],
        out_specs=BS((tm, width), lambda i: (i, 0)), out_shape=SDS((t, width), F32),
        compiler_params=_cp("parallel"),
    )(proj, cos_t, sin_t)


def _rope_bwd(dqs, dks, cos_t, sin_t):
    t = dqs[0].shape[0]
    tm = 512

    def body(*refs):
        c = refs[6][...]
        sg = refs[7][...]
        o_ref = refs[8]
        first = (lax.broadcasted_iota(jnp.int32, (tm, 128), 1) % HEAD_DIM) < HEAD_DIM // 2
        for a in range(6):
            for hp in range(2):
                v = refs[a][:, 128 * hp:128 * (hp + 1)]
                col = 128 * (2 * a + hp)
                o_ref[:, col:col + 128] = (v * c + _swap_halves(v * sg, first)).astype(BF16)

    blk = BS((tm, 256), lambda i: (i, 0))
    tab = BS((tm, 128), lambda i: (i % (S // tm), 0))
    return pl.pallas_call(
        body, name="rope_bwd", grid=(t // tm,), in_specs=[blk] * 6 + [tab, tab],
        out_specs=BS((tm, 1536), lambda i: (i, 0)), out_shape=SDS((t, 1536), BF16),
        compiler_params=_cp("parallel"),
    )(*dqs, *dks, cos_t, sin_t)


def _head_masks():
    lane = lax.broadcasted_iota(jnp.int32, (1, 128), 1)
    m0 = (lane < HEAD_DIM).astype(F32)
    return m0, 1.0 - m0


def _dil_geometry(d):
    sub = S // d
    q_rows = 128
    k_rows = min(256, sub)
    return sub, q_rows, sub // q_rows, k_rows


def _dil_tile(idx, d):
    sub, q_rows, nb, k_rows = _dil_geometry(d)
    r = idx // nb
    n = idx % nb
    k_sub = jnp.clip(q_rows * n - HALF, 0, sub - k_rows)
    if d == 1:
        q_start = pl.multiple_of(q_rows * n, q_rows)
        k_start = pl.multiple_of(k_sub, HALF)
    else:
        q_start = q_rows * n * d + r
        k_start = k_sub * d + r
    ii = lax.broadcasted_iota(jnp.int32, (q_rows, k_rows), 0)
    jj = lax.broadcasted_iota(jnp.int32, (q_rows, k_rows), 1)
    valid = jnp.abs(jj - ii + (k_sub - q_rows * n)) <= HALF
    return q_start, k_start, valid


def _dil_specs(grp):
    qs = BS((S, 128), lambda b, hp: (b, 2 * grp + hp))
    ks = BS((S, 128), lambda b, hp: (b, 6 + 2 * grp + hp))
    vs = BS((S, 128), lambda b, hp: (b, 12 + 2 * grp + hp))
    own = BS((S, 128), lambda b, hp: (b, hp))
    return qs, ks, vs, own


def _dil_fwd(qkr, proj, grp):
    t = qkr.shape[0]
    d = DILATIONS[grp]
    _, q_rows, nb, k_rows = _dil_geometry(d)

    def body(q_ref, k_ref, v_ref, o_ref, l_ref):
        masks = _head_masks()

        def step(i0, carry):
            geo = [_dil_tile(i0 * DIL_FWD_TILES + j, d) for j in range(DIL_FWD_TILES)]
            tiles = [(j, h) for j in range(DIL_FWD_TILES) for h in range(2)]
            qs = [q_ref[_ds(g[0], q_rows, d), :] for g in geo]
            kbs = [k_ref[_ds(g[1], k_rows, d), :].astype(BF16) for g in geo]
            ss = [jnp.where(geo[j][2], _dot_nt((qs[j] * masks[h]).astype(BF16), kbs[j]) * SCALE, NEG) for j, h in tiles]
            mxs = [jnp.max(s, axis=1, keepdims=True) for s in ss]
            ps = [jnp.exp(s - mx) for s, mx in zip(ss, mxs)]
            dens = [jnp.sum(p, axis=1, keepdims=True) for p in ps]
            vs = [v_ref[_ds(g[1], k_rows, d), :] for g in geo]
            outs = [_dot_nn(p.astype(BF16), (vs[j] * masks[h]).astype(BF16)) / den
                    for p, den, (j, h) in zip(ps, dens, tiles)]
            for j, g in enumerate(geo):
                o_ref[_ds(g[0], q_rows, d), :] = outs[2 * j] + outs[2 * j + 1]
                l_ref[_ds(g[0], q_rows, d), :] = (
                    (mxs[2 * j] + jnp.log(dens[2 * j])) * masks[0] + (mxs[2 * j + 1] + jnp.log(dens[2 * j + 1])) * masks[1])
            return carry

        lax.fori_loop(0, d * nb // DIL_FWD_TILES, step, 0)

    qs, ks, vs, own = _dil_specs(grp)
    return pl.pallas_call(
        body, name=f"dil_fwd_{grp}", grid=(t // S, 2), in_specs=[qs, ks, vs], out_specs=[own, own],
        out_shape=[SDS((t, 256), F32), SDS((t, 256), F32)], compiler_params=_cp("parallel", "parallel"),
    )(qkr, qkr, proj)


def _dil_bwd(qkr, proj, do, dlp, lse, grp):
    t = qkr.shape[0]
    d = DILATIONS[grp]
    _, q_rows, nb, k_rows = _dil_geometry(d)

    def body(q_ref, k_ref, v_ref, do_ref, dl_ref, l_ref, dq_ref, dk_ref, dv_ref):
        masks = _head_masks()
        dk_ref[...] = jnp.zeros_like(dk_ref)
        dv_ref[...] = jnp.zeros_like(dv_ref)

        def step(i0, carry):
            geo = [_dil_tile(i0 * DIL_BWD_TILES + j, d) for j in range(DIL_BWD_TILES)]
            tiles = [(j, h) for j in range(DIL_BWD_TILES) for h in range(2)]
            q_ds = [_ds(g[0], q_rows, d) for g in geo]
            k_ds = [_ds(g[1], k_rows, d) for g in geo]
            qs = [q_ref[r, :] for r in q_ds]
            ks = [k_ref[r, :] for r in k_ds]
            kbs = [k.astype(BF16) for k in ks]
            vbs = [v_ref[r, :].astype(BF16) for r in k_ds]
            dos = [do_ref[r, :] for r in q_ds]
            dls = [dl_ref[r, :] for r in q_ds]
            lss = [l_ref[r, :] for r in q_ds]
            qhs = [(qs[j] * masks[h]).astype(BF16) for j, h in tiles]
            dohs = [(dos[j] * masks[h]).astype(BF16) for j, h in tiles]
            ss = [jnp.where(geo[j][2], _dot_nt(qh, kbs[j]) * SCALE, NEG) for qh, (j, h) in zip(qhs, tiles)]
            ps = [jnp.exp(s - lss[j][:, HEAD_DIM * h:HEAD_DIM * h + 1]) for s, (j, h) in zip(ss, tiles)]
            dps = [_dot_nt(doh, vbs[j]) for doh, (j, h) in zip(dohs, tiles)]
            dss = [(p * (dp - dls[j][:, HEAD_DIM * h:HEAD_DIM * h + 1])).astype(BF16)
                   for p, dp, (j, h) in zip(ps, dps, tiles)]
            dqs = [_dot_nn(ds, (ks[j] * masks[h]).astype(BF16)) for ds, (j, h) in zip(dss, tiles)]
            dkws = [_dot_tn(ds, qh) for ds, qh in zip(dss, qhs)]
            dvws = [_dot_tn(p.astype(BF16), doh) for p, doh in zip(ps, dohs)]
            for j in range(DIL_BWD_TILES):
                dq_ref[q_ds[j], :] = (dqs[2 * j] + dqs[2 * j + 1]) * SCALE
                dk_ref[k_ds[j], :] += (dkws[2 * j] + dkws[2 * j + 1]) * SCALE
                dv_ref[k_ds[j], :] += dvws[2 * j] + dvws[2 * j + 1]
            return carry

        lax.fori_loop(0, d * nb // DIL_BWD_TILES, step, 0)

    qs, ks, vs, own = _dil_specs(grp)
    return pl.pallas_call(
        body, name=f"dil_bwd_{grp}", grid=(t // S, 2), in_specs=[qs, ks, vs, own, own, own],
        out_specs=[own, own, own], out_shape=[SDS((t, 256), F32)] * 3,
        compiler_params=_cp("parallel", "parallel"),
    )(qkr, qkr, proj, do, dlp, lse)


def _mix_weights(l0, l1, l2):
    mx = jnp.maximum(jnp.maximum(l0, l1), l2)
    e0, e1, e2 = jnp.exp(l0 - mx), jnp.exp(l1 - mx), jnp.exp(l2 - mx)
    den = e0 + e1 + e2
    return e0 / den, e1 / den, e2 / den


def _combine_fwd(outs, lses):
    t = outs[0].shape[0]
    tm = 512

    def body(o0, o1, o2, l0, l1, l2, y_ref):
        w0, w1, w2 = _mix_weights(l0[...], l1[...], l2[...])
        y_ref[...] = w0 * o0[...] + w1 * o1[...] + w2 * o2[...]

    blk = BS((tm, 256), lambda i: (i, 0))
    return pl.pallas_call(
        body, name="combine_fwd", grid=(t // tm,), in_specs=[blk] * 6, out_specs=blk,
        out_shape=SDS((t, 256), F32), compiler_params=_cp("parallel"),
    )(*outs, *lses)


def _head_sum(x):
    a = lax.broadcasted_iota(jnp.int32, (256, 256), 0) // HEAD_DIM
    b = lax.broadcasted_iota(jnp.int32, (256, 256), 1) // HEAD_DIM
    ones = (a == b).astype(BF16)
    hi = x.astype(BF16)
    lo = (x - hi.astype(F32)).astype(BF16)
    return _dot_nn(hi, ones) + _dot_nn(lo, ones)


def _combine_bwd(dya, outs, lses):
    t = dya.shape[0]
    tm = 512

    def body(dy_ref, o0, o1, o2, l0, l1, l2, d0, d1, d2, e0, e1, e2):
        ws = _mix_weights(l0[...], l1[...], l2[...])
        dy = dy_ref[...]
        ya = ws[0] * o0[...] + ws[1] * o1[...] + ws[2] * o2[...]
        hs = _head_sum(dy * ya)
        for w, d_ref, e_ref in zip(ws, (d0, d1, d2), (e0, e1, e2)):
            d_ref[...] = w * dy
            e_ref[...] = w * hs

    blk = BS((tm, 256), lambda i: (i, 0))
    return pl.pallas_call(
        body, name="combine_bwd", grid=(t // tm,), in_specs=[blk] * 7, out_specs=[blk] * 6,
        out_shape=[SDS((t, 256), F32)] * 6, compiler_params=_cp("parallel"),
    )(dya, *outs, *lses)


def _na_bias_table(rel_bias):
    qc = np.arange(GRID_W)[:, None]
    kc = np.arange(GRID_W)[None, :]
    win_lo = np.clip(qc - 8, 0, GRID_W - 16)
    col_valid = (kc >= win_lo) & (kc < win_lo + 16)
    col_idx = np.clip(kc - qc + 15, 0, 30)
    row_idx = np.arange(NA_KR)[:, None] + np.arange(NA_KR)[None, :]
    rows = (row_idx[..., None] == np.arange(2 * NA_KR - 1)).astype(np.float32)
    cols = (col_idx[..., None] == np.arange(31)).astype(np.float32)
    b = jnp.einsum("hrd,ckr,qjd->hcqkj", rel_bias.astype(F32), rows, cols, precision=lax.Precision.HIGHEST)
    b = jnp.where(col_valid[None, None, :, None, :], b, NEG)
    return b.reshape(8, NA_KR, GRID_W, NA_KR * GRID_W)


def _na_row(i):
    lo = jnp.clip(i - NA_KR // 2, 0, NA_ROWS - NA_KR)
    return pl.multiple_of(GRID_W * i, GRID_W), pl.multiple_of(GRID_W * lo, GRID_W), lo - i + NA_KR - 1


def _na_fwd(proj, bias):
    t = proj.shape[0]
    kw = NA_KR * GRID_W

    def body(q_ref, k_ref, v_ref, b_ref, o_ref, l_ref):
        masks = _head_masks()

        def step(i0, carry):
            rows = [_na_row(i0 * NA_FWD_ROWS + j) for j in range(NA_FWD_ROWS)]
            qs = [q_ref[pl.ds(q_start, GRID_W), :] for q_start, _, _ in rows]
            kbs = [k_ref[pl.ds(k_start, kw), :].astype(BF16) for _, k_start, _ in rows]
            tiles = [(j, h) for j in range(NA_FWD_ROWS) for h in range(2)]
            ss = [_dot_nt((qs[j] * masks[h]).astype(BF16), kbs[j]) * SCALE + b_ref[h, rows[j][2]] for j, h in tiles]
            mxs = [jnp.max(s, axis=1, keepdims=True) for s in ss]
            ps = [jnp.exp(s - mx) for s, mx in zip(ss, mxs)]
            dens = [jnp.sum(p, axis=1, keepdims=True) for p in ps]
            pbs = [(p / den).astype(BF16) for p, den in zip(ps, dens)]
            vs = [v_ref[pl.ds(k_start, kw), :] for _, k_start, _ in rows]
            outs = [_dot_nn(pb, (vs[j] * masks[h]).astype(BF16)) for pb, (j, h) in zip(pbs, tiles)]
            for j, (q_start, _, _) in enumerate(rows):
                o_ref[pl.ds(q_start, GRID_W), :] = outs[2 * j] + outs[2 * j + 1]
                l_ref[pl.ds(q_start, GRID_W), :] = (
                    (mxs[2 * j] + jnp.log(dens[2 * j])) * masks[0] + (mxs[2 * j + 1] + jnp.log(dens[2 * j + 1])) * masks[1])
            return carry

        lax.fori_loop(0, NA_ROWS // NA_FWD_ROWS, step, 0)

    c0 = QKV_A // 128
    own = BS((S, 128), lambda b, hp: (b, hp))
    return pl.pallas_call(
        body, name="na_fwd", grid=(t // S, 4),
        in_specs=[BS((S, 128), lambda b, hp: (b, c0 + hp)), BS((S, 128), lambda b, hp: (b, c0 + 4 + hp)),
                  BS((S, 128), lambda b, hp: (b, c0 + 8 + hp)),
                  BS((2, NA_KR, GRID_W, kw), lambda b, hp: (hp, 0, 0, 0))],
        out_specs=[own, own], out_shape=[SDS((t, 512), F32), SDS((t, 512), F32)],
        compiler_params=_cp("parallel", "parallel"),
    )(proj, proj, proj, bias)


def _na_bwd(proj, bias, dyb, yb, lse):
    t = proj.shape[0]
    kw = NA_KR * GRID_W

    def body(q_ref, k_ref, v_ref, b_ref, do_ref, o_ref, l_ref, dq_ref, dk_ref, dv_ref, db_ref):
        masks = _head_masks()

        @pl.when(pl.program_id(1) == 0)
        def _():
            db_ref[...] = jnp.zeros_like(db_ref)

        dk_ref[...] = jnp.zeros_like(dk_ref)
        dv_ref[...] = jnp.zeros_like(dv_ref)

        def step(i0, carry):
            rows = [_na_row(i0 * NA_BWD_ROWS + j) for j in range(NA_BWD_ROWS)]
            tiles = [(j, h) for j in range(NA_BWD_ROWS) for h in range(2)]
            q_ds = [pl.ds(r[0], GRID_W) for r in rows]
            k_ds = [pl.ds(r[1], kw) for r in rows]
            qs = [q_ref[r, :] for r in q_ds]
            ks = [k_ref[r, :] for r in k_ds]
            kbs = [k.astype(BF16) for k in ks]
            vbs = [v_ref[r, :].astype(BF16) for r in k_ds]
            dos = [do_ref[r, :] for r in q_ds]
            os_ = [o_ref[r, :] for r in q_ds]
            lss = [l_ref[r, :] for r in q_ds]
            qhs = [(qs[j] * masks[h]).astype(BF16) for j, h in tiles]
            dohs = [(dos[j] * masks[h]).astype(BF16) for j, h in tiles]
            deltas = [jnp.sum(dos[j] * os_[j] * masks[h], axis=1, keepdims=True) for j, h in tiles]
            ss = [_dot_nt(qh, kbs[j]) * SCALE + b_ref[h, rows[j][2]] for qh, (j, h) in zip(qhs, tiles)]
            ps = [jnp.exp(s - lss[j][:, HEAD_DIM * h:HEAD_DIM * h + 1]) for s, (j, h) in zip(ss, tiles)]
            dps = [_dot_nt(doh, vbs[j]) for doh, (j, h) in zip(dohs, tiles)]
            dss = [p * (dp - delta) for p, dp, delta in zip(ps, dps, deltas)]
            for ds, (j, h) in zip(dss, tiles):
                db_ref[h, rows[j][2]] += ds
            dsbs = [ds.astype(BF16) for ds in dss]
            dqs = [_dot_nn(dsb, (ks[j] * masks[h]).astype(BF16)) for dsb, (j, h) in zip(dsbs, tiles)]
            dkws = [_dot_tn(dsb, qh) for dsb, qh in zip(dsbs, qhs)]
            dvws = [_dot_tn(p.astype(BF16), doh) for p, doh in zip(ps, dohs)]
            for j in range(NA_BWD_ROWS):
                dq_ref[q_ds[j], :] = (dqs[2 * j] + dqs[2 * j + 1]) * SCALE
                dk_ref[k_ds[j], :] += (dkws[2 * j] + dkws[2 * j + 1]) * SCALE
                dv_ref[k_ds[j], :] += dvws[2 * j] + dvws[2 * j + 1]
            return carry

        lax.fori_loop(0, NA_ROWS // NA_BWD_ROWS, step, 0)

    c0 = QKV_A // 128
    own = BS((S, 128), lambda hp, b: (b, hp))
    tab = BS((2, NA_KR, GRID_W, kw), lambda hp, b: (hp, 0, 0, 0))
    return pl.pallas_call(
        body, name="na_bwd", grid=(4, t // S),
        in_specs=[BS((S, 128), lambda hp, b: (b, c0 + hp)), BS((S, 128), lambda hp, b: (b, c0 + 4 + hp)),
                  BS((S, 128), lambda hp, b: (b, c0 + 8 + hp)), tab, own, own, own],
        out_specs=[own, own, own, tab],
        out_shape=[SDS((t, 512), F32)] * 3 + [SDS((8, NA_KR, GRID_W, kw), F32)],
        compiler_params=_cp("parallel", "arbitrary"),
    )(proj, proj, proj, bias, dyb, yb, lse)


def _na_dbias_lane_map():
    kw = NA_KR * GRID_W
    lane = np.arange(kw)
    blk, m = lane // GRID_W, lane % GRID_W
    target = np.full(kw, -1)
    target[m < 16] = (blk * 32 + 15 + m)[m < 16]
    target[m >= 49] = (((blk + 1) % NA_KR) * 32 + m - 49)[m >= 49]
    return jnp.asarray(target[:, None] == np.arange(kw)[None, :], BF16)


def _na_dbias(db):
    kw = NA_KR * GRID_W

    def body(x_ref, map_ref, o_ref, z_ref):
        for cls in range(NA_KR):
            xv = x_ref[cls]
            y = xv[0:8]
            for g in range(1, GRID_W // 8):
                y = y + pltpu.roll(xv[8 * g:8 * g + 8], kw - 8 * g, 1)
            d = y[0:1]
            for s in range(1, 8):
                d = d + pltpu.roll(y[s:s + 1], kw - s, 1)
            z_ref[cls:cls + 1, :] = d
        z = z_ref[...]
        hi = z.astype(BF16)
        lo = (z - hi.astype(F32)).astype(BF16)
        e = _dot_nn(hi, map_ref[...]) + _dot_nn(lo, map_ref[...])
        out = e[0:1]
        for cls in range(1, NA_KR):
            out = out + pltpu.roll(e[cls:cls + 1], 32 * cls, 1)
        o_ref[...] = jnp.broadcast_to(out, (8, kw))

    return pl.pallas_call(
        body, name="na_dbias", grid=(8,),
        in_specs=[BS((None, NA_KR, GRID_W, kw), lambda h: (h, 0, 0, 0)), BS((kw, kw), lambda h: (0, 0))],
        out_specs=BS((None, 8, kw), lambda h: (h, 0, 0)), out_shape=SDS((8, 8, kw), F32),
        scratch_shapes=[pltpu.VMEM((8, kw), F32)], compiler_params=_cp("parallel"),
    )(db, _na_dbias_lane_map())


def _merge_fwd(ya, yb, proj, wat, wbt):
    t = ya.shape[0]
    tm, tn = 512, 256
    ca = (QKV_A + QKV_B) // tn
    cb = ca + D // tn

    def body(ya_ref, yb_ref, la_ref, lb_ref, wa_ref, wb_ref, m_ref, za_ref, zb_ref):
        za = _dot_nt(ya_ref[...].astype(BF16), wa_ref[...])
        zb = _dot_nt(yb_ref[...].astype(BF16), wb_ref[...])
        m_ref[...] = (jax.nn.sigmoid(la_ref[...]) * za + jax.nn.sigmoid(lb_ref[...]) * zb).astype(BF16)
        za_ref[...] = za.astype(BF16)
        zb_ref[...] = zb.astype(BF16)

    out = BS((tm, tn), lambda i, j: (i, j))
    return pl.pallas_call(
        body, name="merge_fwd", grid=(t // tm, D // tn),
        in_specs=[BS((tm, 256), lambda i, j: (i, 0)), BS((tm, 512), lambda i, j: (i, 0)),
                  BS((tm, tn), lambda i, j: (i, ca + j)), BS((tm, tn), lambda i, j: (i, cb + j)),
                  BS((tn, 256), lambda i, j: (j, 0)), BS((tn, 512), lambda i, j: (j, 0))],
        out_specs=[out, out, out], out_shape=[SDS((t, D), BF16)] * 3,
        compiler_params=_cp("parallel", "parallel"),
    )(ya, yb, proj, proj, wat, wbt)


def _merge_bwd(dm, za, zb, proj):
    t = dm.shape[0]
    tm, tn = 512, 256
    ca = (QKV_A + QKV_B) // tn
    cb = ca + D // tn

    def body(dm_ref, za_ref, zb_ref, la_ref, lb_ref, dza_ref, dzb_ref, dl_ref):
        dmv = dm_ref[...]
        ga = jax.nn.sigmoid(la_ref[...])
        gb = jax.nn.sigmoid(lb_ref[...])
        dza_ref[...] = (dmv * ga).astype(BF16)
        dzb_ref[...] = (dmv * gb).astype(BF16)
        dl_ref[0] = (dmv * za_ref[...].astype(F32) * ga * (1.0 - ga)).astype(BF16)
        dl_ref[1] = (dmv * zb_ref[...].astype(F32) * gb * (1.0 - gb)).astype(BF16)

    blk = BS((tm, tn), lambda i, j: (i, j))
    return pl.pallas_call(
        body, name="merge_bwd", grid=(t // tm, D // tn),
        in_specs=[blk, blk, blk, BS((tm, tn), lambda i, j: (i, ca + j)), BS((tm, tn), lambda i, j: (i, cb + j))],
        out_specs=[blk, blk, BS((2, tm, tn), lambda i, j: (0, i, j))],
        out_shape=[SDS((t, D), BF16), SDS((t, D), BF16), SDS((2, t, D), BF16)],
        compiler_params=_cp("parallel", "parallel"),
    )(dm, za, zb, proj, proj)


def _sum_slots(recv0, recv1, tag):
    _, r, c = recv0.shape
    tr = r if r * c <= 512 * 1024 else r // 2

    def body(a_ref, b_ref, o_ref):
        for layer, ref in enumerate((a_ref, b_ref)):
            acc = ref[0].astype(F32)
            for s in range(1, N_DEV):
                acc = acc + ref[s].astype(F32)
            o_ref[layer] = acc

    blk = BS((N_DEV, tr, c), lambda i: (0, i, 0))
    return pl.pallas_call(
        body, name=f"sum_slots_{tag}", grid=(r // tr,), in_specs=[blk, blk],
        out_specs=BS((2, tr, c), lambda i: (0, i, 0)), out_shape=SDS((2, r, c), F32),
        compiler_params=_cp("parallel"),
    )(recv0, recv1)


def _adamw(w, g, m, v, tag):
    layers, r, c = w.shape
    tr = next(r // k for k in (1, 2, 4, 8) if r // k <= 384 and r % (8 * k) == 0)

    def body(w_ref, g_ref, m_ref, v_ref, d_ref, mo_ref, vo_ref):
        gv = g_ref[...]
        mn = ADAM_B1 * m_ref[...] + (1.0 - ADAM_B1) * gv
        vn = ADAM_B2 * v_ref[...] + (1.0 - ADAM_B2) * (gv * gv)
        m_hat = mn / (1.0 - ADAM_B1 ** ADAM_STEP)
        v_hat = vn / (1.0 - ADAM_B2 ** ADAM_STEP)
        d_ref[...] = -ADAM_LR * (m_hat / (jnp.sqrt(v_hat) + ADAM_EPS) + ADAM_WD * w_ref[...])
        mo_ref[...] = mn
        vo_ref[...] = vn

    blk = BS((None, tr, c), lambda l, i: (l, i, 0))
    return pl.pallas_call(
        body, name=f"adamw_{tag}", grid=(layers, r // tr), in_specs=[blk] * 4, out_specs=[blk] * 3,
        out_shape=[SDS((layers, r, c), F32)] * 3, compiler_params=_cp("parallel", "parallel"),
    )(w, g, m, v)


def _place():
    return lax.axis_index("x"), lax.axis_index("y"), lax.axis_index("c")


def _flip(coord, bit):
    return 1 - coord if bit else coord


def _allgather(shards, tag):
    n_arr = len(shards)
    hbm = BS(memory_space=pl.ANY)

    def body(*refs):
        ins, outs = refs[:n_arr], refs[n_arr:2 * n_arr]
        send_sems, recv_sems, local_sems = refs[2 * n_arr:]
        x, y, c = _place()
        me, sibling = (x, y, c), (x, y, 1 - c)
        chips = [(1 - x, y), (x, 1 - y), (1 - x, 1 - y)]

        def rows(a, p):
            r = shards[a].shape[0]
            return outs[a].at[pl.ds((4 * p[0] + 2 * p[1] + p[2]) * r, r), :]

        def copy(a, k, block, to, src=None):
            return pltpu.make_async_remote_copy(
                src_ref=rows(a, block) if src is None else src, dst_ref=rows(a, block),
                send_sem=send_sems.at[a, k], recv_sem=recv_sems.at[a, k], device_id=to, device_id_type=MESH)

        mine = [pltpu.make_async_copy(ins[a], rows(a, me), local_sems.at[a]) for a in range(n_arr)]
        for cp in mine:
            cp.start()
        first = []
        for a in range(n_arr):
            first.append(copy(a, 0, me, sibling, src=ins[a]))
            first += [copy(a, 1 + j, me, (*chip, c), src=ins[a]) for j, chip in enumerate(chips)]
        for cp in first:
            cp.start()
        passed = []
        for a in range(n_arr):
            for j, chip in enumerate(chips):
                copy(a, 1 + j, (*chip, c), me).wait_recv()
                passed.append(copy(a, 4 + j, (*chip, c), sibling))
                passed[-1].start()
        for a in range(n_arr):
            copy(a, 0, sibling, me).wait_recv()
            for j, chip in enumerate(chips):
                copy(a, 4 + j, (*chip, 1 - c), me).wait_recv()
        for cp in first + passed:
            cp.wait_send()
        for cp in mine:
            cp.wait()

    return pl.pallas_call(
        body, name=f"allgather_{tag}", in_specs=[hbm] * n_arr, out_specs=[hbm] * n_arr,
        out_shape=[SDS((N_DEV * s.shape[0], s.shape[1]), s.dtype) for s in shards],
        scratch_shapes=[pltpu.SemaphoreType.DMA((n_arr, 7)), pltpu.SemaphoreType.DMA((n_arr, 7)),
                        pltpu.SemaphoreType.DMA((n_arr,))],
        compiler_params=pltpu.CompilerParams(has_side_effects=True),
    )(*shards)


def _peers(x, y, c):
    peers = []
    for mask in range(1, N_DEV):
        p = (_flip(x, mask & 4), _flip(y, mask & 2), _flip(c, mask & 1))
        peers.append((p, 4 * p[0] + 2 * p[1] + p[2]))
    return peers


def _exchange_refs(mode, src, land, me, peer):
    if mode == "gather":
        r = src.shape[0]
        return src, land.at[pl.ds(me * r, r), :], land.at[pl.ds(peer * r, r), :]
    r = land.shape[1]
    return src.at[pl.ds(peer * r, r), :], land.at[me], land.at[peer]


HBM_SPEC = BS(memory_space=pltpu.HBM)
SEM_SPEC = BS(memory_space=pltpu.SEMAPHORE)
DATAFLOW = pltpu.SideEffectType.DATAFLOW_SIDE_EFFECTING


def _fresh(shape, dtype, tag):
    def body(o_ref):
        del o_ref

    return pl.pallas_call(body, name=f"fresh_{tag}", out_specs=BS(memory_space=pl.ANY), out_shape=SDS(shape, dtype))()


def _own_block_placed(mode, src, me, tag):
    if mode == "gather":
        r, c = src.shape
        return lax.dynamic_update_slice(_fresh((N_DEV * r, c), src.dtype, tag), src, (me * r, 0))
    r, c = src.shape[0] // N_DEV, src.shape[1]
    own = lax.dynamic_slice(src, (me * r, 0), (r, c))
    return lax.dynamic_update_slice(_fresh((N_DEV, r, c), src.dtype, tag), own[None], (me, 0, 0))


def _exchange_start(mode, srcs, after, tag):
    n = len(srcs)
    x, y, c = _place()
    lands = [_own_block_placed(mode, s, 4 * x + 2 * y + c, f"{tag}_{a}") for a, s in enumerate(srcs)]
    behind = [] if after is None else [after]

    def body(*refs):
        src_refs, land_refs = refs[:n], refs[n:2 * n]
        send_sems, recv_sems = refs[2 * n + len(behind)], refs[2 * n + len(behind) + 1]
        token = refs[-1]
        bx, by, bc = _place()
        me = 4 * bx + 2 * by + bc
        for a in range(n):
            for k, (p, idx) in enumerate(_peers(bx, by, bc)):
                out, there, _ = _exchange_refs(mode, src_refs[a], land_refs[a], me, idx)
                pltpu.make_async_remote_copy(
                    src_ref=out, dst_ref=there, send_sem=send_sems.at[7 * a + k], recv_sem=recv_sems.at[7 * a + k],
                    device_id=p, device_id_type=MESH).start()
        token[...] = jnp.zeros_like(token)

    res = pl.pallas_call(
        body, name=f"{mode}_start_{tag}",
        out_shape=(pltpu.SemaphoreType.DMA((7 * n,)), pltpu.SemaphoreType.DMA((7 * n,)),
                   *[pltpu.HBM(s.shape, s.dtype) for s in srcs], *[pltpu.HBM(l.shape, l.dtype) for l in lands],
                   SDS((8, 128), F32)),
        in_specs=[HBM_SPEC] * (2 * n) + [BS(memory_space=pl.ANY)] * len(behind),
        out_specs=(SEM_SPEC, SEM_SPEC, *[HBM_SPEC] * (2 * n), BS(memory_space=pltpu.VMEM)),
        input_output_aliases={i: 2 + i for i in range(2 * n)},
        compiler_params=pltpu.CompilerParams(has_side_effects=DATAFLOW),
    )(*[pltpu.with_memory_space_constraint(s, pltpu.HBM) for s in srcs],
      *[pltpu.with_memory_space_constraint(l, pltpu.HBM) for l in lands], *behind)
    return (mode, res[0], res[1], res[2:2 + n], res[2 + n:2 + 2 * n]), res[-1]


def _exchange_wait(handle, after, tag):
    mode, send_sems, recv_sems, srcs, lands = handle
    n = len(srcs)

    def body(*refs):
        src_refs, land_refs = refs[:n], refs[n:2 * n]
        send_ref, recv_ref = refs[2 * n], refs[2 * n + 1]
        bx, by, bc = _place()
        me = 4 * bx + 2 * by + bc
        for a in range(n):
            for k, (p, idx) in enumerate(_peers(bx, by, bc)):
                out, _, here = _exchange_refs(mode, src_refs[a], land_refs[a], me, idx)
                cp = pltpu.make_async_remote_copy(
                    src_ref=out, dst_ref=here, send_sem=send_ref.at[7 * a + k], recv_sem=recv_ref.at[7 * a + k],
                    device_id=p, device_id_type=MESH)
                cp.wait_send()
                cp.wait_recv()

    res = pl.pallas_call(
        body, name=f"{mode}_wait_{tag}",
        out_shape=(*[pltpu.HBM(s.shape, s.dtype) for s in srcs], *[pltpu.HBM(l.shape, l.dtype) for l in lands]),
        in_specs=[HBM_SPEC] * (2 * n) + [SEM_SPEC, SEM_SPEC, BS(memory_space=pl.ANY)],
        out_specs=tuple([HBM_SPEC] * (2 * n)),
        input_output_aliases={i: i for i in range(2 * n)},
        compiler_params=pltpu.CompilerParams(has_side_effects=DATAFLOW),
    )(*srcs, *lands, send_sems, recv_sems, after)
    return list(res[n:])


def _allreduce_small(vec, behind):
    rows = vec.shape[0]

    def body(x_ref, behind_ref, o_ref, buf_ref, send_sems, recv_sems):
        x, y, c = _place()
        me = 4 * x + 2 * y + c
        buf_ref[me] = x_ref[...]
        peers = _peers(x, y, c)

        def copy(k, slot):
            return pltpu.make_async_remote_copy(
                src_ref=x_ref, dst_ref=buf_ref.at[slot], send_sem=send_sems.at[k], recv_sem=recv_sems.at[k],
                device_id=peers[k][0], device_id_type=MESH)

        sends = [copy(k, me) for k in range(N_DEV - 1)]
        for cp in sends:
            cp.start()
        for k in range(N_DEV - 1):
            copy(k, peers[k][1]).wait_recv()
        for cp in sends:
            cp.wait_send()
        acc = buf_ref[0]
        for s in range(1, N_DEV):
            acc = acc + buf_ref[s]
        o_ref[...] = acc

    vmem = BS(memory_space=pltpu.VMEM)
    return pl.pallas_call(
        body, name="allreduce_small", in_specs=[vmem, BS(memory_space=pl.ANY)], out_specs=vmem,
        out_shape=SDS((rows, 128), F32),
        scratch_shapes=[pltpu.VMEM((N_DEV, rows, 128), F32), pltpu.SemaphoreType.DMA((7,)),
                        pltpu.SemaphoreType.DMA((7,))],
        compiler_params=pltpu.CompilerParams(has_side_effects=True),
    )(vec, behind)


def _ffn_forward(x, norm_g, fetch, names, tag):
    hn = _norm_fwd(x, norm_g, tag)
    gu, act = _ffn_up(hn, fetch(names[0], hn).reshape(2, F, D), tag)
    out = _mm_nn(act[None], fetch(names[1], act)[None], f"down_{tag}", res=x, scale=0.5)
    return out, (x, hn, gu, act)


def _ffn_backward(dxo, dxo_b, saved, norm_g, wut, wd, tag, send):
    x, hn, gu, act = saved
    d_wd = _mm_tn(act[None], dxo_b, f"dwd_{tag}", scale=0.5)[0]
    du = _ffn_dact(dxo_b, wd, gu, send(("down",), [d_wd]), tag)
    d_wut = _mm_tn(du, hn, f"dwu_{tag}")
    token = send(("up",), [d_wut.reshape(2 * F, D)])
    return _mm_nn_norm_bwd(du, wut, x, norm_g + token[0, 0], dxo, tag)


def _mixer_forward(x, norm_g, fetch, bias, tables, tag):
    hn = _norm_fwd(x, norm_g, tag)
    proj = _mm_nt_rows(hn, fetch("win", hn), f"proj_{tag}", 512, IN_W // 2, IN_W, 0)
    qkr = _rope_fwd(proj, *tables)
    outs, lses = [], []
    for grp in range(3):
        o, l = _dil_fwd(qkr, proj, grp)
        outs.append(o)
        lses.append(l)
    ya = _combine_fwd(outs, lses)
    yb, lse_b = _na_fwd(proj, bias)
    merged, za, zb = _merge_fwd(ya, yb, proj, fetch("wa", yb), fetch("wb", yb))
    out = _mm_nn(merged[None], fetch("wo", merged)[None], f"out_{tag}", res=x)
    return out, (x, hn, proj, qkr, outs, lses, ya, yb, lse_b, merged, za, zb)


def _mixer_backward(dxo, dxo_b, saved, norm_g, w, bias, tables, tag, send):
    wint, wat, wbt, wo = w
    x, hn, proj, qkr, outs, lses, ya, yb, lse_b, merged, za, zb = saved
    dm = _mm_nt_rows(dxo_b, wo, f"dmerged_{tag}", 512, D, D, 0)
    d_wo = _mm_tn(merged[None], dxo_b, f"dwo_{tag}")[0]
    dza, dzb, dlog = _merge_bwd(dm, za, zb, proj)
    dya = _mm_nn(dza[None], wat[None], f"dya_{tag}")
    dyb = _mm_nn(dzb[None], wbt[None], f"dyb_{tag}")
    d_wat = _mm_tn(dza[None], ya, f"dwa_{tag}")[0]
    d_wbt = _mm_tn(dzb[None], yb, f"dwb_{tag}")[0]
    cb = _combine_bwd(dya, outs, lses)
    dqs, dks, dvs = [], [], []
    for grp in range(3):
        dq, dk, dv = _dil_bwd(qkr, proj, cb[grp], cb[3 + grp], lses[grp], grp)
        dqs.append(dq)
        dks.append(dk)
        dvs.append(dv)
    dqk = _rope_bwd(dqs, dks, *tables)
    dqb, dkb, dvb, dbias_tab = _na_bwd(proj, bias, dyb, yb, lse_b)
    dbias = _na_dbias(dbias_tab)
    dproj = jnp.concatenate(
        [dqk] + [t.astype(BF16) for t in (*dvs, dqb, dkb, dvb)] + [dlog[0], dlog[1]], axis=1)
    d_wint = _mm_tn(dproj[None], hn, f"dwin_{tag}")[0]
    token = send(("win", "wa", "wb", "wo"), [d_wint, d_wat, d_wbt, d_wo])
    dx, dx_b, dg = _mm_nn_norm_bwd(dproj[None], wint[None], x, norm_g + token[0, 0], dxo, f"mix_{tag}")
    dbias = dbias[:, 0, :480].reshape(8, 15, 32)[:, :, :31]
    return dx, dx_b, dg, dbias


def _pack_small(norms, biases, final, loss=None):
    parts = []
    for layer in range(DEPTH):
        parts += [norms[0][layer], norms[1][layer], norms[2][layer],
                  jnp.pad(biases[layer].reshape(-1), (0, BIAS_PAD - 8 * 15 * 31))]
    parts.append(final)
    flat = jnp.concatenate([p.reshape(-1).astype(F32) for p in parts])
    if loss is not None:
        flat = jnp.concatenate([flat, loss.reshape(-1)])
    return jnp.pad(flat, (0, SMALL_ROWS * 128 - flat.shape[0])).reshape(SMALL_ROWS, 128)


def _unpack_small(packed):
    flat = packed.reshape(-1)
    norms, biases = ([], [], []), []
    pos = 0
    for _ in range(DEPTH):
        for k in range(3):
            norms[k].append(flat[pos:pos + D])
            pos += D
        biases.append(flat[pos:pos + 8 * 15 * 31].reshape(8, 15, 31))
        pos += BIAS_PAD
    final = flat[pos:pos + D]
    pos += D
    return [jnp.stack(n) for n in norms], jnp.stack(biases), final, flat[pos]


def kernel(x, ffn1_norm, ffn1_w_up, ffn1_w_down, mix_norm, w_in, na_rel_bias, w_branch_a, w_branch_b, w_out, ffn2_norm, ffn2_w_up, ffn2_w_down, final_norm, loss_target, m_ffn1_norm, m_ffn1_w_up, m_ffn1_w_down, m_mix_norm, m_w_in, m_na_rel_bias, m_w_branch_a, m_w_branch_b, m_w_out, m_ffn2_norm, m_ffn2_w_up, m_ffn2_w_down, m_final_norm, v_ffn1_norm, v_ffn1_w_up, v_ffn1_w_down, v_mix_norm, v_w_in, v_na_rel_bias, v_w_branch_a, v_w_branch_b, v_w_out, v_ffn2_norm, v_ffn2_w_up, v_ffn2_w_down, v_final_norm):
    t = x.shape[0] * x.shape[1]
    xs = x.reshape(t, D)
    tgt = loss_target.reshape(t, D)
    tables = _rope_tables()

    col_sharded = dict(up1=ffn1_w_up, win=w_in, wa=w_branch_a, wb=w_branch_b, up2=ffn2_w_up)
    row_sharded = dict(down1=ffn1_w_down, wo=w_out, down2=ffn2_w_down)
    shard = [{} for _ in range(DEPTH)]
    for layer in range(DEPTH):
        for name, arr in col_sharded.items():
            shard[layer][name] = arr[layer].T.astype(BF16)
        for name, arr in row_sharded.items():
            shard[layer][name] = arr[layer].astype(BF16)

    weights = [{} for _ in range(DEPTH)]
    weights[0]["up1"] = _allgather([shard[0]["up1"]], "first")[0]
    travel = [(0, ("down1",)), (0, ("win",)), (0, ("wa", "wb", "wo")), (0, ("up2", "down2")),
              (1, ("up1", "down1")), (1, ("win",)), (1, ("wa", "wb", "wo")), (1, ("up2", "down2"))]
    pending = {}
    after = weights[0]["up1"]
    for i, (layer, names) in enumerate(travel):
        handle, after = _exchange_start("gather", [shard[layer][n] for n in names], after, f"w{i}")
        for n in names:
            pending[layer, n] = (i, handle, names)
    zero = after[0, 0]

    def fetcher(layer):
        def fetch(name, behind):
            if (layer, name) in pending:
                i, handle, names = pending[layer, name]
                for n, got in zip(names, _exchange_wait(handle, behind, f"w{i}")):
                    weights[layer][n] = got
                    del pending[layer, n]
            return weights[layer][name]
        return fetch

    saved = []
    h = xs
    for layer in range(DEPTH):
        bias = _na_bias_table(na_rel_bias[layer])
        fetch = fetcher(layer)
        h, s1 = _ffn_forward(h, ffn1_norm[layer] + zero, fetch, ("up1", "down1"), f"f1l{layer}")
        h, s2 = _mixer_forward(h, mix_norm[layer], fetch, bias, tables, f"l{layer}")
        h, s3 = _ffn_forward(h, ffn2_norm[layer], fetch, ("up2", "down2"), f"f2l{layer}")
        saved.append((s1, s2, s3, bias))
    loss_part, dh, dh_b, d_final = _loss_head(h, final_norm, tgt)

    d_norms = ([None] * DEPTH, [None] * DEPTH, [None] * DEPTH)
    d_bias = [None] * DEPTH
    sent = {}

    def sender(layer, suffix):
        def send(names, grads):
            tag = f"g{layer}{names[0]}{suffix}"
            handle, token = _exchange_start("scatter", grads, None, tag)
            for i, n in enumerate(names):
                sent[layer, n + suffix] = (handle, i, tag)
            return token
        return send

    for layer in reversed(range(DEPTH)):
        w = weights[layer]
        s1, s2, s3, bias = saved[layer]
        dh, dh_b, d_norms[2][layer] = _ffn_backward(
            dh, dh_b, s3, ffn2_norm[layer], w["up2"].reshape(2, F, D), w["down2"], f"f2l{layer}", sender(layer, "2"))
        dh, dh_b, d_norms[1][layer], d_bias[layer] = _mixer_backward(
            dh, dh_b, s2, mix_norm[layer], (w["win"], w["wa"], w["wb"], w["wo"]), bias, tables, f"l{layer}",
            sender(layer, ""))
        dh, dh_b, d_norms[0][layer] = _ffn_backward(
            dh, dh_b, s1, ffn1_norm[layer], w["up1"].reshape(2, F, D), w["down1"], f"f1l{layer}", sender(layer, "1"))
    grad_x = dh.reshape(x.shape)

    originals = dict(up1=(ffn1_w_up, m_ffn1_w_up, v_ffn1_w_up), down1=(ffn1_w_down, m_ffn1_w_down, v_ffn1_w_down),
                     win=(w_in, m_w_in, v_w_in), wa=(w_branch_a, m_w_branch_a, v_w_branch_a),
                     wb=(w_branch_b, m_w_branch_b, v_w_branch_b), wo=(w_out, m_w_out, v_w_out),
                     up2=(ffn2_w_up, m_ffn2_w_up, v_ffn2_w_up), down2=(ffn2_w_down, m_ffn2_w_down, v_ffn2_w_down))
    big = {}
    behind = dh
    landed = {}

    def received(layer, name):
        handle, i, tag = sent[layer, name]
        if tag not in landed:
            landed[tag] = _exchange_wait(handle, behind, tag)
        return landed[tag][i]

    for name in ("down2", "up2", "win", "wa", "wb", "wo", "down1", "up1"):
        g = _sum_slots(received(0, name), received(1, name), name)
        wv, mv, vv = originals[name]
        if name in col_sharded:
            wv, mv, vv = (jnp.swapaxes(t, 1, 2) for t in (wv, mv, vv))
        big[name] = (g, *_adamw(wv, g, mv, vv, name))
        behind = big[name][1]
        if name in col_sharded:
            big[name] = tuple(jnp.swapaxes(t, 1, 2) for t in big[name])

    small = _allreduce_small(_pack_small(d_norms, d_bias, d_final, loss_part[0, :1]), behind)
    g_norms, g_bias, g_final, loss = _unpack_small(small)
    w_small = _pack_small((ffn1_norm, mix_norm, ffn2_norm), na_rel_bias, final_norm)
    m_small = _pack_small((m_ffn1_norm, m_mix_norm, m_ffn2_norm), m_na_rel_bias, m_final_norm)
    v_small = _pack_small((v_ffn1_norm, v_mix_norm, v_ffn2_norm), v_na_rel_bias, v_final_norm)
    upd = _adamw(w_small[None], small[None], m_small[None], v_small[None], "small")
    small_out = [(g_norms, g_bias, g_final)] + [_unpack_small(u[0])[:3] for u in upd]

    outputs = [loss, grad_x]
    for kind in range(4):
        norms, bias_k, final_k = small_out[kind]
        outputs += [norms[0], big["up1"][kind], big["down1"][kind], norms[1], big["win"][kind], bias_k,
                    big["wa"][kind], big["wb"][kind], big["wo"][kind], norms[2], big["up2"][kind],
                    big["down2"][kind], final_k]
    return tuple(outputs)
```

```python
import numpy as np

import jax
import jax.numpy as jnp
from jax import lax
from jax.experimental import pallas as pl
from jax.experimental.pallas import tpu as pltpu

F32 = jnp.float32
BF16 = jnp.bfloat16
SDS = jax.ShapeDtypeStruct
BS = pl.BlockSpec
MESH = pl.DeviceIdType.MESH

D = 1024
S = 2048
F = 2816
DEPTH = 2
HEAD_DIM = 64
DILATIONS = (1, 4, 16)
HALF = 64
QKV_A = 2304
QKV_B = 1536
IN_W = 5888
N_DEV = 8
NA_ROWS = 32
GRID_W = 64
NA_KR = 8
ROPE_THETA = 10000.0
RMS_EPS = 1e-6
NEG = -1e30
SCALE = HEAD_DIM ** -0.5
ADAM_LR, ADAM_B1, ADAM_B2, ADAM_EPS, ADAM_WD, ADAM_STEP = 0.001, 0.9, 0.999, 1e-08, 0.01, 10
VMEM_LIMIT_V7X = 52 * 1024 * 1024
SMALL_ROWS = 120
BIAS_PAD = 3840
NA_FWD_ROWS = 4
NA_BWD_ROWS = 4
DIL_FWD_TILES = 4
DIL_BWD_TILES = 4


def _cp(*sem):
    return pltpu.CompilerParams(dimension_semantics=sem, vmem_limit_bytes=VMEM_LIMIT_V7X)


def _dot_nn(a, b):
    return jnp.dot(a, b, preferred_element_type=F32)


def _dot_nt(a, b):
    return lax.dot_general(a, b, (((1,), (1,)), ((), ())), preferred_element_type=F32)


def _dot_tn(a, b):
    return lax.dot_general(a, b, (((0,), (0,)), ((), ())), preferred_element_type=F32)


def _ds(start, size, stride):
    return pl.ds(start, size) if stride == 1 else pl.ds(start, size, stride=stride)


def _norm_fwd(x, g, tag):
    t = x.shape[0]
    tm = 512

    def body(x_ref, g_ref, o_ref):
        xv = x_ref[...]
        r = lax.rsqrt(jnp.mean(xv * xv, axis=-1, keepdims=True) + RMS_EPS)
        o_ref[...] = (xv * r * g_ref[...]).astype(BF16)

    return pl.pallas_call(
        body, name=f"norm_fwd_{tag}", grid=(t // tm,),
        in_specs=[BS((tm, D), lambda i: (i, 0)), BS((1, D), lambda i: (0, 0))],
        out_specs=BS((tm, D), lambda i: (i, 0)),
        out_shape=SDS((t, D), BF16), compiler_params=_cp("parallel"),
    )(x, g.reshape(1, D))


def _loss_head(x, g, tgt):
    t = x.shape[0]
    tm = 512

    def body(x_ref, g_ref, t_ref, loss_ref, dx_ref, dxb_ref, dg_ref):
        @pl.when(pl.program_id(0) == 0)
        def _():
            dg_ref[...] = jnp.zeros_like(dg_ref)
            loss_ref[...] = jnp.zeros_like(loss_ref)

        xv = x_ref[...]
        gv = g_ref[...]
        r = lax.rsqrt(jnp.mean(xv * xv, axis=-1, keepdims=True) + RMS_EPS)
        xh = xv * r
        e = xh * gv - t_ref[...]
        loss_ref[...] += 0.5 * jnp.sum(jnp.mean(e * e, axis=-1, keepdims=True), axis=0, keepdims=True)
        dy = e * (1.0 / D)
        u = dy * gv
        dx = r * (u - xh * jnp.mean(xh * u, axis=-1, keepdims=True))
        dx_ref[...] = dx
        dxb_ref[...] = dx.astype(BF16)
        dg_ref[...] += jnp.sum(dy * xh, axis=0, keepdims=True)

    row = BS((tm, D), lambda i: (i, 0))
    vec = BS((1, D), lambda i: (0, 0))
    return pl.pallas_call(
        body, name="loss_head", grid=(t // tm,),
        in_specs=[row, vec, row], out_specs=[BS((1, 128), lambda i: (0, 0)), row, row, vec],
        out_shape=[SDS((1, 128), F32), SDS((t, D), F32), SDS((t, D), BF16), SDS((1, D), F32)],
        compiler_params=_cp("arbitrary"),
    )(x, g.reshape(1, D), tgt)


def _mm_nn(a, w, tag, res=None, scale=1.0, tm=512, tn=None, next_g=None):
    c_n, t, k = a.shape
    n = w.shape[2]
    tn = n if tn is None else tn
    assert next_g is None or tn == n
    n_in = 2 + (res is not None) + (next_g is not None)

    def body(*refs):
        a_ref, w_ref = refs[0], refs[1]
        acc = _dot_nn(a_ref[0].astype(BF16), w_ref[0])
        for c in range(1, c_n):
            acc = acc + _dot_nn(a_ref[c].astype(BF16), w_ref[c])
        if scale != 1.0:
            acc = acc * scale
        if res is not None:
            acc = refs[2][...] + acc
        refs[n_in][...] = acc
        if next_g is not None:
            r = lax.rsqrt(jnp.mean(acc * acc, axis=-1, keepdims=True) + RMS_EPS)
            refs[n_in + 1][...] = (acc * r * refs[n_in - 1][...]).astype(BF16)

    in_specs = [BS((c_n, tm, k), lambda i, j: (0, i, 0)), BS((c_n, k, tn), lambda i, j: (0, 0, j))]
    args = [a, w]
    out_specs = [BS((tm, tn), lambda i, j: (i, j))]
    out_shape = [SDS((t, n), F32)]
    if res is not None:
        in_specs.append(BS((tm, tn), lambda i, j: (i, j)))
        args.append(res)
    if next_g is not None:
        in_specs.append(BS((1, n), lambda i, j: (0, 0)))
        args.append(next_g.reshape(1, n))
        out_specs.append(BS((tm, tn), lambda i, j: (i, j)))
        out_shape.append(SDS((t, n), BF16))
    got = pl.pallas_call(
        body, name=f"mm_nn_{tag}", grid=(t // tm, n // tn), in_specs=in_specs, out_specs=out_specs,
        out_shape=out_shape, compiler_params=_cp("parallel", "parallel"),
    )(*args)
    return got if next_g is not None else got[0]


def _mm_nn_norm_bwd(a, w, x, g, dres, tag, tm=256):
    c_n, t, k = a.shape

    def body(a_ref, w_ref, x_ref, g_ref, dr_ref, dx_ref, dxb_ref, dg_ref):
        @pl.when(pl.program_id(0) == 0)
        def _():
            dg_ref[...] = jnp.zeros_like(dg_ref)

        dh = _dot_nn(a_ref[0], w_ref[0])
        for c in range(1, c_n):
            dh = dh + _dot_nn(a_ref[c], w_ref[c])
        xv = x_ref[...]
        r = lax.rsqrt(jnp.mean(xv * xv, axis=-1, keepdims=True) + RMS_EPS)
        xh = xv * r
        u = dh * g_ref[...]
        dx = dr_ref[...] + r * (u - xh * jnp.mean(xh * u, axis=-1, keepdims=True))
        dx_ref[...] = dx
        dxb_ref[...] = dx.astype(BF16)
        dg_ref[...] += jnp.sum(dh * xh, axis=0, keepdims=True)

    row = BS((tm, D), lambda i: (i, 0))
    vec = BS((1, D), lambda i: (0, 0))
    return pl.pallas_call(
        body, name=f"mm_nn_norm_bwd_{tag}", grid=(t // tm,),
        in_specs=[BS((c_n, tm, k), lambda i: (0, i, 0)), BS((c_n, k, D), lambda i: (0, 0, 0)), row, vec, row],
        out_specs=[row, row, vec], out_shape=[SDS((t, D), F32), SDS((t, D), BF16), SDS((1, D), F32)],
        compiler_params=_cp("arbitrary"),
    )(a, w, x, g.reshape(1, D), dres)


def _mm_nt_rows(a, w, tag, tm, tn, n_total, w_row0, rope=None):
    t, k = a.shape
    assert w_row0 % tn == 0 and n_total % tn == 0
    j0 = w_row0 // tn

    def body(a_ref, w_ref, *rest):
        o_ref = rest[-1]
        o_ref[...] = _dot_nt(a_ref[...].astype(BF16), w_ref[...])
        if rope is not None:
            @pl.when(pl.program_id(0) == 0)
            def _():
                c = rest[0][...]
                sg = rest[1][...]
                first = (lax.broadcasted_iota(jnp.int32, (tm, 128), 1) % HEAD_DIM) < HEAD_DIM // 2
                for col in range(0, rope[2], 128):
                    v = o_ref[:, col:col + 128]
                    o_ref[:, col:col + 128] = v * c + _swap_halves(v, first) * sg

    in_specs = [BS((tm, k), lambda j, i: (i, 0)), BS((tn, k), lambda j, i: (j0 + j, 0))]
    args = [a, w]
    if rope is not None:
        assert rope[2] <= tn
        in_specs += [BS((tm, 128), lambda j, i: (i % (S // tm), 0))] * 2
        args += [rope[0], rope[1]]
    return pl.pallas_call(
        body, name=f"mm_nt_{tag}", grid=(n_total // tn, t // tm), in_specs=in_specs,
        out_specs=BS((tm, tn), lambda j, i: (i, j)), out_shape=SDS((t, n_total), F32),
        compiler_params=_cp("parallel", "parallel"),
    )(*args)


def _mm_tn(a, b, tag, scale=1.0, tmm=256, tk=None):
    c_n, t, m = a.shape
    n = b.shape[1]
    tk = t if tk is None else tk
    nk = t // tk

    def body_one(a_ref, b_ref, o_ref):
        o_ref[...] = (_dot_tn(a_ref[...].astype(BF16), b_ref[...].astype(BF16)) * scale).astype(BF16)

    def body_acc(a_ref, b_ref, o_ref, acc_ref):
        kk = pl.program_id(2)

        @pl.when(kk == 0)
        def _():
            acc_ref[...] = jnp.zeros_like(acc_ref)

        acc_ref[...] += _dot_tn(a_ref[...].astype(BF16), b_ref[...].astype(BF16))

        @pl.when(kk == nk - 1)
        def _():
            o_ref[...] = (acc_ref[...] * scale).astype(BF16)

    return pl.pallas_call(
        body_one if nk == 1 else body_acc, name=f"mm_tn_{tag}", grid=(c_n, m // tmm, nk),
        in_specs=[BS((None, tk, tmm), lambda c, mi, kk: (c, kk, mi)), BS((tk, n), lambda c, mi, kk: (kk, 0))],
        out_specs=BS((None, tmm, n), lambda c, mi, kk: (c, mi, 0)),
        out_shape=SDS((c_n, m, n), BF16), scratch_shapes=[] if nk == 1 else [pltpu.VMEM((tmm, n), F32)],
        compiler_params=_cp("parallel", "parallel", "arbitrary"),
    )(a, b)


def _ffn_up(hn, wut, tag):
    t = hn.shape[0]
    tm, tn = 512, 1408

    def body(h_ref, w_ref, gu_ref, act_ref):
        h = h_ref[...]
        g = _dot_nt(h, w_ref[0])
        u = _dot_nt(h, w_ref[1])
        sg = jax.nn.sigmoid(g)
        silu = g * sg
        gu_ref[0] = (u * (sg + silu * (1.0 - sg))).astype(BF16)
        gu_ref[1] = silu.astype(BF16)
        act_ref[...] = (silu * u).astype(BF16)

    return pl.pallas_call(
        body, name=f"ffn_up_{tag}", grid=(F // tn, t // tm),
        in_specs=[BS((tm, D), lambda j, i: (i, 0)), BS((2, tn, D), lambda j, i: (0, j, 0))],
        out_specs=[BS((2, tm, tn), lambda j, i: (0, i, j)), BS((tm, tn), lambda j, i: (i, j))],
        out_shape=[SDS((2, t, F), BF16), SDS((t, F), BF16)],
        compiler_params=_cp("parallel", "parallel"),
    )(hn, wut)


def _ffn_dact(dxo, wd, gu, tie, tag):
    t = dxo.shape[0]
    tm, tn = 512, 1408

    def body(d_ref, w_ref, gu_ref, tie_ref, o_ref):
        dact = _dot_nt(d_ref[...] * 0.5, w_ref[...])
        o_ref[0] = (dact * gu_ref[0].astype(F32)).astype(BF16)
        o_ref[1] = (dact * gu_ref[1].astype(F32)).astype(BF16)

    return pl.pallas_call(
        body, name=f"ffn_dact_{tag}", grid=(F // tn, t // tm),
        in_specs=[BS((tm, D), lambda j, i: (i, 0)), BS((tn, D), lambda j, i: (j, 0)),
                  BS((2, tm, tn), lambda j, i: (0, i, j)), BS((8, 128), lambda j, i: (0, 0))],
        out_specs=BS((2, tm, tn), lambda j, i: (0, i, j)),
        out_shape=SDS((2, t, F), BF16), compiler_params=_cp("parallel", "parallel"),
    )(dxo, wd, gu, tie)


def _rope_tables():
    half = HEAD_DIM // 2
    inv_freq = ROPE_THETA ** (-jnp.arange(half, dtype=F32) / half)
    ang = jnp.arange(S).astype(F32)[:, None] * inv_freq[None, :]
    cos, sin = jnp.cos(ang), jnp.sin(ang)
    return jnp.concatenate([cos, cos, cos, cos], axis=1), jnp.concatenate([-sin, sin, -sin, sin], axis=1)


def _swap_halves(t, first_half):
    return jnp.where(first_half, pltpu.roll(t, 96, 1), pltpu.roll(t, 32, 1))


def _rope_bwd(dqs, dks, cos_t, sin_t):
    t = dqs[0].shape[0]
    tm = 512

    def body(*refs):
        c = refs[6][...]
        sg = refs[7][...]
        o_ref = refs[8]
        first = (lax.broadcasted_iota(jnp.int32, (tm, 128), 1) % HEAD_DIM) < HEAD_DIM // 2
        for a in range(6):
            for hp in range(2):
                v = refs[a][:, 128 * hp:128 * (hp + 1)]
                col = 128 * (2 * a + hp)
                o_ref[:, col:col + 128] = (v * c + _swap_halves(v * sg, first)).astype(BF16)

    blk = BS((tm, 256), lambda i: (i, 0))
    tab = BS((tm, 128), lambda i: (i % (S // tm), 0))
    return pl.pallas_call(
        body, name="rope_bwd", grid=(t // tm,), in_specs=[blk] * 6 + [tab, tab],
        out_specs=BS((tm, 1536), lambda i: (i, 0)), out_shape=SDS((t, 1536), BF16),
        compiler_params=_cp("parallel"),
    )(*dqs, *dks, cos_t, sin_t)


def _head_masks():
    lane = lax.broadcasted_iota(jnp.int32, (1, 128), 1)
    m0 = (lane < HEAD_DIM).astype(F32)
    return m0, 1.0 - m0


def _dil_geometry(d):
    sub = S // d
    q_rows = 128
    k_rows = min(256, sub)
    return sub, q_rows, sub // q_rows, k_rows


def _dil_tile(idx, d):
    sub, q_rows, nb, k_rows = _dil_geometry(d)
    r = idx // nb
    n = idx % nb
    k_sub = jnp.clip(q_rows * n - HALF, 0, sub - k_rows)
    if d == 1:
        q_start = pl.multiple_of(q_rows * n, q_rows)
        k_start = pl.multiple_of(k_sub, HALF)
    else:
        q_start = q_rows * n * d + r
        k_start = k_sub * d + r
    ii = lax.broadcasted_iota(jnp.int32, (q_rows, k_rows), 0)
    jj = lax.broadcasted_iota(jnp.int32, (q_rows, k_rows), 1)
    valid = jnp.abs(jj - ii + (k_sub - q_rows * n)) <= HALF
    return q_start, k_start, valid


def _dil_specs(grp):
    qs = BS((S, 128), lambda b, hp: (b, 2 * grp + hp))
    ks = BS((S, 128), lambda b, hp: (b, 6 + 2 * grp + hp))
    vs = BS((S, 128), lambda b, hp: (b, 12 + 2 * grp + hp))
    own = BS((S, 128), lambda b, hp: (b, hp))
    return qs, ks, vs, own


def _dil_fwd(qkr, proj, grp):
    t = qkr.shape[0]
    d = DILATIONS[grp]
    _, q_rows, nb, k_rows = _dil_geometry(d)

    def body(q_ref, k_ref, v_ref, o_ref, l_ref):
        masks = _head_masks()

        def step(i0, carry):
            geo = [_dil_tile(i0 * DIL_FWD_TILES + j, d) for j in range(DIL_FWD_TILES)]
            tiles = [(j, h) for j in range(DIL_FWD_TILES) for h in range(2)]
            qs = [q_ref[_ds(g[0], q_rows, d), :] for g in geo]
            kbs = [k_ref[_ds(g[1], k_rows, d), :].astype(BF16) for g in geo]
            ss = [jnp.where(geo[j][2], _dot_nt((qs[j] * masks[h]).astype(BF16), kbs[j]) * SCALE, NEG) for j, h in tiles]
            mxs = [jnp.max(s, axis=1, keepdims=True) for s in ss]
            ps = [jnp.exp(s - mx) for s, mx in zip(ss, mxs)]
            dens = [jnp.sum(p, axis=1, keepdims=True) for p in ps]
            vs = [v_ref[_ds(g[1], k_rows, d), :] for g in geo]
            outs = [_dot_nn(p.astype(BF16), (vs[j] * masks[h]).astype(BF16)) / den
                    for p, den, (j, h) in zip(ps, dens, tiles)]
            for j, g in enumerate(geo):
                o_ref[_ds(g[0], q_rows, d), :] = outs[2 * j] + outs[2 * j + 1]
                l_ref[_ds(g[0], q_rows, d), :] = (
                    (mxs[2 * j] + jnp.log(dens[2 * j])) * masks[0] + (mxs[2 * j + 1] + jnp.log(dens[2 * j + 1])) * masks[1])
            return carry

        lax.fori_loop(0, d * nb // DIL_FWD_TILES, step, 0)

    qs, ks, vs, own = _dil_specs(grp)
    return pl.pallas_call(
        body, name=f"dil_fwd_{grp}", grid=(t // S, 2), in_specs=[qs, ks, vs], out_specs=[own, own],
        out_shape=[SDS((t, 256), F32), SDS((t, 256), F32)], compiler_params=_cp("parallel", "parallel"),
    )(qkr, qkr, proj)


def _dil_bwd(qkr, proj, do, dlp, lse, grp):
    t = qkr.shape[0]
    d = DILATIONS[grp]
    _, q_rows, nb, k_rows = _dil_geometry(d)

    def body(q_ref, k_ref, v_ref, do_ref, dl_ref, l_ref, dq_ref, dk_ref, dv_ref):
        masks = _head_masks()
        dk_ref[...] = jnp.zeros_like(dk_ref)
        dv_ref[...] = jnp.zeros_like(dv_ref)

        def step(i0, carry):
            geo = [_dil_tile(i0 * DIL_BWD_TILES + j, d) for j in range(DIL_BWD_TILES)]
            tiles = [(j, h) for j in range(DIL_BWD_TILES) for h in range(2)]
            q_ds = [_ds(g[0], q_rows, d) for g in geo]
            k_ds = [_ds(g[1], k_rows, d) for g in geo]
            qs = [q_ref[r, :] for r in q_ds]
            ks = [k_ref[r, :] for r in k_ds]
            kbs = [k.astype(BF16) for k in ks]
            vbs = [v_ref[r, :].astype(BF16) for r in k_ds]
            dos = [do_ref[r, :] for r in q_ds]
            dls = [dl_ref[r, :] for r in q_ds]
            lss = [l_ref[r, :] for r in q_ds]
            qhs = [(qs[j] * masks[h]).astype(BF16) for j, h in tiles]
            dohs = [(dos[j] * masks[h]).astype(BF16) for j, h in tiles]
            ss = [jnp.where(geo[j][2], _dot_nt(qh, kbs[j]) * SCALE, NEG) for qh, (j, h) in zip(qhs, tiles)]
            ps = [jnp.exp(s - lss[j][:, HEAD_DIM * h:HEAD_DIM * h + 1]) for s, (j, h) in zip(ss, tiles)]
            dps = [_dot_nt(doh, vbs[j]) for doh, (j, h) in zip(dohs, tiles)]
            dss = [(p * (dp - dls[j][:, HEAD_DIM * h:HEAD_DIM * h + 1])).astype(BF16)
                   for p, dp, (j, h) in zip(ps, dps, tiles)]
            dqs = [_dot_nn(ds, (ks[j] * masks[h]).astype(BF16)) for ds, (j, h) in zip(dss, tiles)]
            dkws = [_dot_tn(ds, qh) for ds, qh in zip(dss, qhs)]
            dvws = [_dot_tn(p.astype(BF16), doh) for p, doh in zip(ps, dohs)]
            for j in range(DIL_BWD_TILES):
                dq_ref[q_ds[j], :] = (dqs[2 * j] + dqs[2 * j + 1]) * SCALE
                dk_ref[k_ds[j], :] += (dkws[2 * j] + dkws[2 * j + 1]) * SCALE
                dv_ref[k_ds[j], :] += dvws[2 * j] + dvws[2 * j + 1]
            return carry

        lax.fori_loop(0, d * nb // DIL_BWD_TILES, step, 0)

    qs, ks, vs, own = _dil_specs(grp)
    return pl.pallas_call(
        body, name=f"dil_bwd_{grp}", grid=(t // S, 2), in_specs=[qs, ks, vs, own, own, own],
        out_specs=[own, own, own], out_shape=[SDS((t, 256), F32)] * 3,
        compiler_params=_cp("parallel", "parallel"),
    )(qkr, qkr, proj, do, dlp, lse)


def _mix_weights(l0, l1, l2):
    mx = jnp.maximum(jnp.maximum(l0, l1), l2)
    e0, e1, e2 = jnp.exp(l0 - mx), jnp.exp(l1 - mx), jnp.exp(l2 - mx)
    den = e0 + e1 + e2
    return e0 / den, e1 / den, e2 / den


def _combine_fwd(outs, lses):
    t = outs[0].shape[0]
    tm = 512

    def body(o0, o1, o2, l0, l1, l2, y_ref):
        w0, w1, w2 = _mix_weights(l0[...], l1[...], l2[...])
        y_ref[...] = w0 * o0[...] + w1 * o1[...] + w2 * o2[...]

    blk = BS((tm, 256), lambda i: (i, 0))
    return pl.pallas_call(
        body, name="combine_fwd", grid=(t // tm,), in_specs=[blk] * 6, out_specs=blk,
        out_shape=SDS((t, 256), F32), compiler_params=_cp("parallel"),
    )(*outs, *lses)


def _head_sum(x):
    a = lax.broadcasted_iota(jnp.int32, (256, 256), 0) // HEAD_DIM
    b = lax.broadcasted_iota(jnp.int32, (256, 256), 1) // HEAD_DIM
    ones = (a == b).astype(BF16)
    hi = x.astype(BF16)
    lo = (x - hi.astype(F32)).astype(BF16)
    return _dot_nn(hi, ones) + _dot_nn(lo, ones)


def _combine_bwd(dya, outs, lses):
    t = dya.shape[0]
    tm = 512

    def body(dy_ref, o0, o1, o2, l0, l1, l2, d0, d1, d2, e0, e1, e2):
        ws = _mix_weights(l0[...], l1[...], l2[...])
        dy = dy_ref[...]
        ya = ws[0] * o0[...] + ws[1] * o1[...] + ws[2] * o2[...]
        hs = _head_sum(dy * ya)
        for w, d_ref, e_ref in zip(ws, (d0, d1, d2), (e0, e1, e2)):
            d_ref[...] = w * dy
            e_ref[...] = w * hs

    blk = BS((tm, 256), lambda i: (i, 0))
    return pl.pallas_call(
        body, name="combine_bwd", grid=(t // tm,), in_specs=[blk] * 7, out_specs=[blk] * 6,
        out_shape=[SDS((t, 256), F32)] * 6, compiler_params=_cp("parallel"),
    )(dya, *outs, *lses)


def _na_bias_table(rel_bias):
    qc = np.arange(GRID_W)[:, None]
    kc = np.arange(GRID_W)[None, :]
    win_lo = np.clip(qc - 8, 0, GRID_W - 16)
    col_valid = (kc >= win_lo) & (kc < win_lo + 16)
    col_idx = np.clip(kc - qc + 15, 0, 30)
    row_idx = np.arange(NA_KR)[:, None] + np.arange(NA_KR)[None, :]
    rows = (row_idx[..., None] == np.arange(2 * NA_KR - 1)).astype(np.float32)
    cols = (col_idx[..., None] == np.arange(31)).astype(np.float32)
    b = jnp.einsum("hrd,ckr,qjd->hcqkj", rel_bias.astype(F32), rows, cols, precision=lax.Precision.HIGHEST)
    b = jnp.where(col_valid[None, None, :, None, :], b, NEG)
    return b.reshape(8, NA_KR, GRID_W, NA_KR * GRID_W)


def _na_row(i):
    lo = jnp.clip(i - NA_KR // 2, 0, NA_ROWS - NA_KR)
    return pl.multiple_of(GRID_W * i, GRID_W), pl.multiple_of(GRID_W * lo, GRID_W), lo - i + NA_KR - 1


def _na_fwd(proj, bias):
    t = proj.shape[0]
    kw = NA_KR * GRID_W

    def body(q_ref, k_ref, v_ref, b_ref, o_ref, l_ref):
        masks = _head_masks()

        def step(i0, carry):
            rows = [_na_row(i0 * NA_FWD_ROWS + j) for j in range(NA_FWD_ROWS)]
            qs = [q_ref[pl.ds(q_start, GRID_W), :] for q_start, _, _ in rows]
            kbs = [k_ref[pl.ds(k_start, kw), :].astype(BF16) for _, k_start, _ in rows]
            tiles = [(j, h) for j in range(NA_FWD_ROWS) for h in range(2)]
            ss = [_dot_nt((qs[j] * masks[h]).astype(BF16), kbs[j]) * SCALE + b_ref[h, rows[j][2]] for j, h in tiles]
            mxs = [jnp.max(s, axis=1, keepdims=True) for s in ss]
            ps = [jnp.exp(s - mx) for s, mx in zip(ss, mxs)]
            dens = [jnp.sum(p, axis=1, keepdims=True) for p in ps]
            pbs = [(p / den).astype(BF16) for p, den in zip(ps, dens)]
            vs = [v_ref[pl.ds(k_start, kw), :] for _, k_start, _ in rows]
            outs = [_dot_nn(pb, (vs[j] * masks[h]).astype(BF16)) for pb, (j, h) in zip(pbs, tiles)]
            for j, (q_start, _, _) in enumerate(rows):
                o_ref[pl.ds(q_start, GRID_W), :] = outs[2 * j] + outs[2 * j + 1]
                l_ref[pl.ds(q_start, GRID_W), :] = (
                    (mxs[2 * j] + jnp.log(dens[2 * j])) * masks[0] + (mxs[2 * j + 1] + jnp.log(dens[2 * j + 1])) * masks[1])
            return carry

        lax.fori_loop(0, NA_ROWS // NA_FWD_ROWS, step, 0)

    c0 = QKV_A // 128
    own = BS((S, 128), lambda b, hp: (b, hp))
    return pl.pallas_call(
        body, name="na_fwd", grid=(t // S, 4),
        in_specs=[BS((S, 128), lambda b, hp: (b, c0 + hp)), BS((S, 128), lambda b, hp: (b, c0 + 4 + hp)),
                  BS((S, 128), lambda b, hp: (b, c0 + 8 + hp)),
                  BS((2, NA_KR, GRID_W, kw), lambda b, hp: (hp, 0, 0, 0))],
        out_specs=[own, own], out_shape=[SDS((t, 512), F32), SDS((t, 512), F32)],
        compiler_params=_cp("parallel", "parallel"),
    )(proj, proj, proj, bias)


def _na_bwd(proj, bias, dyb, yb, lse):
    t = proj.shape[0]
    kw = NA_KR * GRID_W

    def body(q_ref, k_ref, v_ref, b_ref, do_ref, o_ref, l_ref, dq_ref, dk_ref, dv_ref, db_ref):
        masks = _head_masks()

        @pl.when(pl.program_id(1) == 0)
        def _():
            db_ref[...] = jnp.zeros_like(db_ref)

        dk_ref[...] = jnp.zeros_like(dk_ref)
        dv_ref[...] = jnp.zeros_like(dv_ref)

        def step(i0, carry):
            rows = [_na_row(i0 * NA_BWD_ROWS + j) for j in range(NA_BWD_ROWS)]
            tiles = [(j, h) for j in range(NA_BWD_ROWS) for h in range(2)]
            q_ds = [pl.ds(r[0], GRID_W) for r in rows]
            k_ds = [pl.ds(r[1], kw) for r in rows]
            qs = [q_ref[r, :] for r in q_ds]
            ks = [k_ref[r, :] for r in k_ds]
            kbs = [k.astype(BF16) for k in ks]
            vbs = [v_ref[r, :].astype(BF16) for r in k_ds]
            dos = [do_ref[r, :] for r in q_ds]
            os_ = [o_ref[r, :] for r in q_ds]
            lss = [l_ref[r, :] for r in q_ds]
            qhs = [(qs[j] * masks[h]).astype(BF16) for j, h in tiles]
            dohs = [(dos[j] * masks[h]).astype(BF16) for j, h in tiles]
            deltas = [jnp.sum(dos[j] * os_[j] * masks[h], axis=1, keepdims=True) for j, h in tiles]
            ss = [_dot_nt(qh, kbs[j]) * SCALE + b_ref[h, rows[j][2]] for qh, (j, h) in zip(qhs, tiles)]
            ps = [jnp.exp(s - lss[j][:, HEAD_DIM * h:HEAD_DIM * h + 1]) for s, (j, h) in zip(ss, tiles)]
            dps = [_dot_nt(doh, vbs[j]) for doh, (j, h) in zip(dohs, tiles)]
            dss = [p * (dp - delta) for p, dp, delta in zip(ps, dps, deltas)]
            for ds, (j, h) in zip(dss, tiles):
                db_ref[h, rows[j][2]] += ds
            dsbs = [ds.astype(BF16) for ds in dss]
            dqs = [_dot_nn(dsb, (ks[j] * masks[h]).astype(BF16)) for dsb, (j, h) in zip(dsbs, tiles)]
            dkws = [_dot_tn(dsb, qh) for dsb, qh in zip(dsbs, qhs)]
            dvws = [_dot_tn(p.astype(BF16), doh) for p, doh in zip(ps, dohs)]
            for j in range(NA_BWD_ROWS):
                dq_ref[q_ds[j], :] = (dqs[2 * j] + dqs[2 * j + 1]) * SCALE
                dk_ref[k_ds[j], :] += (dkws[2 * j] + dkws[2 * j + 1]) * SCALE
                dv_ref[k_ds[j], :] += dvws[2 * j] + dvws[2 * j + 1]
            return carry

        lax.fori_loop(0, NA_ROWS // NA_BWD_ROWS, step, 0)

    c0 = QKV_A // 128
    own = BS((S, 128), lambda hp, b: (b, hp))
    tab = BS((2, NA_KR, GRID_W, kw), lambda hp, b: (hp, 0, 0, 0))
    return pl.pallas_call(
        body, name="na_bwd", grid=(4, t // S),
        in_specs=[BS((S, 128), lambda hp, b: (b, c0 + hp)), BS((S, 128), lambda hp, b: (b, c0 + 4 + hp)),
                  BS((S, 128), lambda hp, b: (b, c0 + 8 + hp)), tab, own, own, own],
        out_specs=[own, own, own, tab],
        out_shape=[SDS((t, 512), F32)] * 3 + [SDS((8, NA_KR, GRID_W, kw), F32)],
        compiler_params=_cp("parallel", "arbitrary"),
    )(proj, proj, proj, bias, dyb, yb, lse)


def _na_dbias_lane_map():
    kw = NA_KR * GRID_W
    lane = np.arange(kw)
    blk, m = lane // GRID_W, lane % GRID_W
    target = np.full(kw, -1)
    target[m < 16] = (blk * 32 + 15 + m)[m < 16]
    target[m >= 49] = (((blk + 1) % NA_KR) * 32 + m - 49)[m >= 49]
    return jnp.asarray(target[:, None] == np.arange(kw)[None, :], BF16)


def _na_dbias(db):
    kw = NA_KR * GRID_W

    def body(x_ref, map_ref, o_ref, z_ref):
        for cls in range(NA_KR):
            xv = x_ref[cls]
            y = xv[0:8]
            for g in range(1, GRID_W // 8):
                y = y + pltpu.roll(xv[8 * g:8 * g + 8], kw - 8 * g, 1)
            d = y[0:1]
            for s in range(1, 8):
                d = d + pltpu.roll(y[s:s + 1], kw - s, 1)
            z_ref[cls:cls + 1, :] = d
        z = z_ref[...]
        hi = z.astype(BF16)
        lo = (z - hi.astype(F32)).astype(BF16)
        e = _dot_nn(hi, map_ref[...]) + _dot_nn(lo, map_ref[...])
        out = e[0:1]
        for cls in range(1, NA_KR):
            out = out + pltpu.roll(e[cls:cls + 1], 32 * cls, 1)
        o_ref[...] = jnp.broadcast_to(out, (8, kw))

    return pl.pallas_call(
        body, name="na_dbias", grid=(8,),
        in_specs=[BS((None, NA_KR, GRID_W, kw), lambda h: (h, 0, 0, 0)), BS((kw, kw), lambda h: (0, 0))],
        out_specs=BS((None, 8, kw), lambda h: (h, 0, 0)), out_shape=SDS((8, 8, kw), F32),
        scratch_shapes=[pltpu.VMEM((8, kw), F32)], compiler_params=_cp("parallel"),
    )(db, _na_dbias_lane_map())


def _merge_fwd(ya, yb, proj, wat, wbt):
    t = ya.shape[0]
    tm, tn = 512, 256
    ca = (QKV_A + QKV_B) // tn
    cb = ca + D // tn

    def body(ya_ref, yb_ref, la_ref, lb_ref, wa_ref, wb_ref, m_ref, za_ref, zb_ref):
        za = _dot_nt(ya_ref[...].astype(BF16), wa_ref[...])
        zb = _dot_nt(yb_ref[...].astype(BF16), wb_ref[...])
        m_ref[...] = (jax.nn.sigmoid(la_ref[...]) * za + jax.nn.sigmoid(lb_ref[...]) * zb).astype(BF16)
        za_ref[...] = za.astype(BF16)
        zb_ref[...] = zb.astype(BF16)

    out = BS((tm, tn), lambda i, j: (i, j))
    return pl.pallas_call(
        body, name="merge_fwd", grid=(t // tm, D // tn),
        in_specs=[BS((tm, 256), lambda i, j: (i, 0)), BS((tm, 512), lambda i, j: (i, 0)),
                  BS((tm, tn), lambda i, j: (i, ca + j)), BS((tm, tn), lambda i, j: (i, cb + j)),
                  BS((tn, 256), lambda i, j: (j, 0)), BS((tn, 512), lambda i, j: (j, 0))],
        out_specs=[out, out, out], out_shape=[SDS((t, D), BF16)] * 3,
        compiler_params=_cp("parallel", "parallel"),
    )(ya, yb, proj, proj, wat, wbt)


def _merge_bwd(dm, za, zb, proj):
    t = dm.shape[0]
    tm, tn = 512, 256
    ca = (QKV_A + QKV_B) // tn
    cb = ca + D // tn

    def body(dm_ref, za_ref, zb_ref, la_ref, lb_ref, dza_ref, dzb_ref, dl_ref):
        dmv = dm_ref[...]
        ga = jax.nn.sigmoid(la_ref[...])
        gb = jax.nn.sigmoid(lb_ref[...])
        dza_ref[...] = (dmv * ga).astype(BF16)
        dzb_ref[...] = (dmv * gb).astype(BF16)
        dl_ref[0] = (dmv * za_ref[...].astype(F32) * ga * (1.0 - ga)).astype(BF16)
        dl_ref[1] = (dmv * zb_ref[...].astype(F32) * gb * (1.0 - gb)).astype(BF16)

    blk = BS((tm, tn), lambda i, j: (i, j))
    return pl.pallas_call(
        body, name="merge_bwd", grid=(t // tm, D // tn),
        in_specs=[blk, blk, blk, BS((tm, tn), lambda i, j: (i, ca + j)), BS((tm, tn), lambda i, j: (i, cb + j))],
        out_specs=[blk, blk, BS((2, tm, tn), lambda i, j: (0, i, j))],
        out_shape=[SDS((t, D), BF16), SDS((t, D), BF16), SDS((2, t, D), BF16)],
        compiler_params=_cp("parallel", "parallel"),
    )(dm, za, zb, proj, proj)


def _sum_slots(recv0, recv1, tag):
    _, r, c = recv0.shape
    tr = r if r * c <= 512 * 1024 else r // 2

    def body(a_ref, b_ref, o_ref):
        for layer, ref in enumerate((a_ref, b_ref)):
            acc = ref[0].astype(F32)
            for s in range(1, N_DEV):
                acc = acc + ref[s].astype(F32)
            o_ref[layer] = acc

    blk = BS((N_DEV, tr, c), lambda i: (0, i, 0))
    return pl.pallas_call(
        body, name=f"sum_slots_{tag}", grid=(r // tr,), in_specs=[blk, blk],
        out_specs=BS((2, tr, c), lambda i: (0, i, 0)), out_shape=SDS((2, r, c), F32),
        compiler_params=_cp("parallel"),
    )(recv0, recv1)


def _adamw(w, g, m, v, tag):
    layers, r, c = w.shape
    tr = next(r // k for k in (1, 2, 4, 8) if r // k <= 384 and r % (8 * k) == 0)

    def body(w_ref, g_ref, m_ref, v_ref, d_ref, mo_ref, vo_ref):
        gv = g_ref[...]
        mn = ADAM_B1 * m_ref[...] + (1.0 - ADAM_B1) * gv
        vn = ADAM_B2 * v_ref[...] + (1.0 - ADAM_B2) * (gv * gv)
        m_hat = mn / (1.0 - ADAM_B1 ** ADAM_STEP)
        v_hat = vn / (1.0 - ADAM_B2 ** ADAM_STEP)
        d_ref[...] = -ADAM_LR * (m_hat / (jnp.sqrt(v_hat) + ADAM_EPS) + ADAM_WD * w_ref[...])
        mo_ref[...] = mn
        vo_ref[...] = vn

    blk = BS((None, tr, c), lambda l, i: (l, i, 0))
    return pl.pallas_call(
        body, name=f"adamw_{tag}", grid=(layers, r // tr), in_specs=[blk] * 4, out_specs=[blk] * 3,
        out_shape=[SDS((layers, r, c), F32)] * 3, compiler_params=_cp("parallel", "parallel"),
    )(w, g, m, v)


def _place():
    return lax.axis_index("x"), lax.axis_index("y"), lax.axis_index("c")


def _flip(coord, bit):
    return 1 - coord if bit else coord


def _allgather(shards, tag):
    n_arr = len(shards)
    hbm = BS(memory_space=pl.ANY)

    def body(*refs):
        ins, outs = refs[:n_arr], refs[n_arr:2 * n_arr]
        send_sems, recv_sems, local_sems = refs[2 * n_arr:]
        x, y, c = _place()
        me, sibling = (x, y, c), (x, y, 1 - c)
        chips = [(1 - x, y), (x, 1 - y), (1 - x, 1 - y)]

        def rows(a, p):
            r = shards[a].shape[0]
            return outs[a].at[pl.ds((4 * p[0] + 2 * p[1] + p[2]) * r, r), :]

        def copy(a, k, block, to, src=None):
            return pltpu.make_async_remote_copy(
                src_ref=rows(a, block) if src is None else src, dst_ref=rows(a, block),
                send_sem=send_sems.at[a, k], recv_sem=recv_sems.at[a, k], device_id=to, device_id_type=MESH)

        mine = [pltpu.make_async_copy(ins[a], rows(a, me), local_sems.at[a]) for a in range(n_arr)]
        for cp in mine:
            cp.start()
        first = []
        for a in range(n_arr):
            first.append(copy(a, 0, me, sibling, src=ins[a]))
            first += [copy(a, 1 + j, me, (*chip, c), src=ins[a]) for j, chip in enumerate(chips)]
        for cp in first:
            cp.start()
        passed = []
        for a in range(n_arr):
            for j, chip in enumerate(chips):
                copy(a, 1 + j, (*chip, c), me).wait_recv()
                passed.append(copy(a, 4 + j, (*chip, c), sibling))
                passed[-1].start()
        for a in range(n_arr):
            copy(a, 0, sibling, me).wait_recv()
            for j, chip in enumerate(chips):
                copy(a, 4 + j, (*chip, 1 - c), me).wait_recv()
        for cp in first + passed:
            cp.wait_send()
        for cp in mine:
            cp.wait()

    return pl.pallas_call(
        body, name=f"allgather_{tag}", in_specs=[hbm] * n_arr, out_specs=[hbm] * n_arr,
        out_shape=[SDS((N_DEV * s.shape[0], s.shape[1]), s.dtype) for s in shards],
        scratch_shapes=[pltpu.SemaphoreType.DMA((n_arr, 7)), pltpu.SemaphoreType.DMA((n_arr, 7)),
                        pltpu.SemaphoreType.DMA((n_arr,))],
        compiler_params=pltpu.CompilerParams(has_side_effects=True),
    )(*shards)


def _peers(x, y, c):
    peers = []
    for mask in range(1, N_DEV):
        p = (_flip(x, mask & 4), _flip(y, mask & 2), _flip(c, mask & 1))
        peers.append((p, 4 * p[0] + 2 * p[1] + p[2]))
    return peers


def _exchange_refs(mode, src, land, me, peer):
    if mode == "gather":
        r = src.shape[0]
        return src, land.at[pl.ds(me * r, r), :], land.at[pl.ds(peer * r, r), :]
    r = land.shape[1]
    return src.at[pl.ds(peer * r, r), :], land.at[me], land.at[peer]


HBM_SPEC = BS(memory_space=pltpu.HBM)
SEM_SPEC = BS(memory_space=pltpu.SEMAPHORE)
DATAFLOW = pltpu.SideEffectType.DATAFLOW_SIDE_EFFECTING


def _fresh(shape, dtype, tag):
    def body(o_ref):
        del o_ref

    return pl.pallas_call(body, name=f"fresh_{tag}", out_specs=BS(memory_space=pl.ANY), out_shape=SDS(shape, dtype))()


def _own_block_placed(mode, src, me, tag):
    if mode == "gather":
        r, c = src.shape
        return lax.dynamic_update_slice(_fresh((N_DEV * r, c), src.dtype, tag), src, (me * r, 0))
    r, c = src.shape[0] // N_DEV, src.shape[1]
    own = lax.dynamic_slice(src, (me * r, 0), (r, c))
    return lax.dynamic_update_slice(_fresh((N_DEV, r, c), src.dtype, tag), own[None], (me, 0, 0))


def _exchange_start(mode, srcs, after, tag):
    n = len(srcs)
    x, y, c = _place()
    lands = [_own_block_placed(mode, s, 4 * x + 2 * y + c, f"{tag}_{a}") for a, s in enumerate(srcs)]
    behind = [] if after is None else [after]

    def body(*refs):
        src_refs, land_refs = refs[:n], refs[n:2 * n]
        send_sems, recv_sems = refs[2 * n + len(behind)], refs[2 * n + len(behind) + 1]
        token = refs[-1]
        bx, by, bc = _place()
        me = 4 * bx + 2 * by + bc
        for a in range(n):
            for k, (p, idx) in enumerate(_peers(bx, by, bc)):
                out, there, _ = _exchange_refs(mode, src_refs[a], land_refs[a], me, idx)
                pltpu.make_async_remote_copy(
                    src_ref=out, dst_ref=there, send_sem=send_sems.at[7 * a + k], recv_sem=recv_sems.at[7 * a + k],
                    device_id=p, device_id_type=MESH).start()
        token[...] = jnp.zeros_like(token)

    res = pl.pallas_call(
        body, name=f"{mode}_start_{tag}",
        out_shape=(pltpu.SemaphoreType.DMA((7 * n,)), pltpu.SemaphoreType.DMA((7 * n,)),
                   *[pltpu.HBM(s.shape, s.dtype) for s in srcs], *[pltpu.HBM(l.shape, l.dtype) for l in lands],
                   SDS((8, 128), F32)),
        in_specs=[HBM_SPEC] * (2 * n) + [BS(memory_space=pl.ANY)] * len(behind),
        out_specs=(SEM_SPEC, SEM_SPEC, *[HBM_SPEC] * (2 * n), BS(memory_space=pltpu.VMEM)),
        input_output_aliases={i: 2 + i for i in range(2 * n)},
        compiler_params=pltpu.CompilerParams(has_side_effects=DATAFLOW),
    )(*[pltpu.with_memory_space_constraint(s, pltpu.HBM) for s in srcs],
      *[pltpu.with_memory_space_constraint(l, pltpu.HBM) for l in lands], *behind)
    return (mode, res[0], res[1], res[2:2 + n], res[2 + n:2 + 2 * n]), res[-1]


def _exchange_wait(handle, after, tag):
    mode, send_sems, recv_sems, srcs, lands = handle
    n = len(srcs)

    def body(*refs):
        src_refs, land_refs = refs[:n], refs[n:2 * n]
        send_ref, recv_ref = refs[2 * n], refs[2 * n + 1]
        bx, by, bc = _place()
        me = 4 * bx + 2 * by + bc
        for a in range(n):
            for k, (p, idx) in enumerate(_peers(bx, by, bc)):
                out, _, here = _exchange_refs(mode, src_refs[a], land_refs[a], me, idx)
                cp = pltpu.make_async_remote_copy(
                    src_ref=out, dst_ref=here, send_sem=send_ref.at[7 * a + k], recv_sem=recv_ref.at[7 * a + k],
                    device_id=p, device_id_type=MESH)
                cp.wait_send()
                cp.wait_recv()

    res = pl.pallas_call(
        body, name=f"{mode}_wait_{tag}",
        out_shape=(*[pltpu.HBM(s.shape, s.dtype) for s in srcs], *[pltpu.HBM(l.shape, l.dtype) for l in lands]),
        in_specs=[HBM_SPEC] * (2 * n) + [SEM_SPEC, SEM_SPEC, BS(memory_space=pl.ANY)],
        out_specs=tuple([HBM_SPEC] * (2 * n)),
        input_output_aliases={i: i for i in range(2 * n)},
        compiler_params=pltpu.CompilerParams(has_side_effects=DATAFLOW),
    )(*srcs, *lands, send_sems, recv_sems, after)
    return list(res[n:])


def _allreduce_small(vec, behind):
    rows = vec.shape[0]

    def body(x_ref, behind_ref, o_ref, buf_ref, send_sems, recv_sems):
        x, y, c = _place()
        me = 4 * x + 2 * y + c
        buf_ref[me] = x_ref[...]
        peers = _peers(x, y, c)

        def copy(k, slot):
            return pltpu.make_async_remote_copy(
                src_ref=x_ref, dst_ref=buf_ref.at[slot], send_sem=send_sems.at[k], recv_sem=recv_sems.at[k],
                device_id=peers[k][0], device_id_type=MESH)

        sends = [copy(k, me) for k in range(N_DEV - 1)]
        for cp in sends:
            cp.start()
        for k in range(N_DEV - 1):
            copy(k, peers[k][1]).wait_recv()
        for cp in sends:
            cp.wait_send()
        acc = buf_ref[0]
        for s in range(1, N_DEV):
            acc = acc + buf_ref[s]
        o_ref[...] = acc

    vmem = BS(memory_space=pltpu.VMEM)
    return pl.pallas_call(
        body, name="allreduce_small", in_specs=[vmem, BS(memory_space=pl.ANY)], out_specs=vmem,
        out_shape=SDS((rows, 128), F32),
        scratch_shapes=[pltpu.VMEM((N_DEV, rows, 128), F32), pltpu.SemaphoreType.DMA((7,)),
                        pltpu.SemaphoreType.DMA((7,))],
        compiler_params=pltpu.CompilerParams(has_side_effects=True),
    )(vec, behind)


def _ffn_forward(x, hn, fetch, names, tag, next_g):
    gu, act = _ffn_up(hn, fetch(names[0], hn).reshape(2, F, D), tag)
    got = _mm_nn(act[None], fetch(names[1], act)[None], f"down_{tag}", res=x, scale=0.5, next_g=next_g)
    out, hn_next = got if next_g is not None else (got, None)
    return out, hn_next, (x, hn, gu, act)


def _ffn_backward(dxo, dxo_b, saved, norm_g, wut, wd, tag, send):
    x, hn, gu, act = saved
    d_wd = _mm_tn(act[None], dxo_b, f"dwd_{tag}", scale=0.5)[0]
    du = _ffn_dact(dxo_b, wd, gu, send(("down",), [d_wd]), tag)
    d_wut = _mm_tn(du, hn, f"dwu_{tag}")
    token = send(("up",), [d_wut.reshape(2 * F, D)])
    return _mm_nn_norm_bwd(du, wut, x, norm_g + token[0, 0], dxo, tag)


def _mixer_forward(x, hn, fetch, bias, tables, tag, next_g):
    proj = _mm_nt_rows(hn, fetch("win", hn), f"proj_{tag}", 512, IN_W // 2, IN_W, 0, rope=(*tables, 2 * QKV_A // 3))
    qkr = proj
    outs, lses = [], []
    for grp in range(3):
        o, l = _dil_fwd(qkr, proj, grp)
        outs.append(o)
        lses.append(l)
    ya = _combine_fwd(outs, lses)
    yb, lse_b = _na_fwd(proj, bias)
    merged, za, zb = _merge_fwd(ya, yb, proj, fetch("wa", yb), fetch("wb", yb))
    out, hn_next = _mm_nn(merged[None], fetch("wo", merged)[None], f"out_{tag}", res=x, next_g=next_g)
    return out, hn_next, (x, hn, proj, qkr, outs, lses, ya, yb, lse_b, merged, za, zb)


def _mixer_backward(dxo, dxo_b, saved, norm_g, w, bias, tables, tag, send):
    wint, wat, wbt, wo = w
    x, hn, proj, qkr, outs, lses, ya, yb, lse_b, merged, za, zb = saved
    dm = _mm_nt_rows(dxo_b, wo, f"dmerged_{tag}", 512, D, D, 0)
    d_wo = _mm_tn(merged[None], dxo_b, f"dwo_{tag}")[0]
    dza, dzb, dlog = _merge_bwd(dm, za, zb, proj)
    dya = _mm_nn(dza[None], wat[None], f"dya_{tag}")
    dyb = _mm_nn(dzb[None], wbt[None], f"dyb_{tag}")
    d_wat = _mm_tn(dza[None], ya, f"dwa_{tag}")[0]
    d_wbt = _mm_tn(dzb[None], yb, f"dwb_{tag}")[0]
    cb = _combine_bwd(dya, outs, lses)
    dqs, dks, dvs = [], [], []
    for grp in range(3):
        dq, dk, dv = _dil_bwd(qkr, proj, cb[grp], cb[3 + grp], lses[grp], grp)
        dqs.append(dq)
        dks.append(dk)
        dvs.append(dv)
    dqk = _rope_bwd(dqs, dks, *tables)
    dqb, dkb, dvb, dbias_tab = _na_bwd(proj, bias, dyb, yb, lse_b)
    dbias = _na_dbias(dbias_tab)
    dproj = jnp.concatenate(
        [dqk] + [t.astype(BF16) for t in (*dvs, dqb, dkb, dvb)] + [dlog[0], dlog[1]], axis=1)
    d_wint = _mm_tn(dproj[None], hn, f"dwin_{tag}")[0]
    token = send(("win", "wa", "wb", "wo"), [d_wint, d_wat, d_wbt, d_wo])
    dx, dx_b, dg = _mm_nn_norm_bwd(dproj[None], wint[None], x, norm_g + token[0, 0], dxo, f"mix_{tag}")
    dbias = dbias[:, 0, :480].reshape(8, 15, 32)[:, :, :31]
    return dx, dx_b, dg, dbias


def _pack_small(norms, biases, final, loss=None):
    parts = []
    for layer in range(DEPTH):
        parts += [norms[0][layer], norms[1][layer], norms[2][layer],
                  jnp.pad(biases[layer].reshape(-1), (0, BIAS_PAD - 8 * 15 * 31))]
    parts.append(final)
    flat = jnp.concatenate([p.reshape(-1).astype(F32) for p in parts])
    if loss is not None:
        flat = jnp.concatenate([flat, loss.reshape(-1)])
    return jnp.pad(flat, (0, SMALL_ROWS * 128 - flat.shape[0])).reshape(SMALL_ROWS, 128)


def _unpack_small(packed):
    flat = packed.reshape(-1)
    norms, biases = ([], [], []), []
    pos = 0
    for _ in range(DEPTH):
        for k in range(3):
            norms[k].append(flat[pos:pos + D])
            pos += D
        biases.append(flat[pos:pos + 8 * 15 * 31].reshape(8, 15, 31))
        pos += BIAS_PAD
    final = flat[pos:pos + D]
    pos += D
    return [jnp.stack(n) for n in norms], jnp.stack(biases), final, flat[pos]


def kernel(x, ffn1_norm, ffn1_w_up, ffn1_w_down, mix_norm, w_in, na_rel_bias, w_branch_a, w_branch_b, w_out, ffn2_norm, ffn2_w_up, ffn2_w_down, final_norm, loss_target, m_ffn1_norm, m_ffn1_w_up, m_ffn1_w_down, m_mix_norm, m_w_in, m_na_rel_bias, m_w_branch_a, m_w_branch_b, m_w_out, m_ffn2_norm, m_ffn2_w_up, m_ffn2_w_down, m_final_norm, v_ffn1_norm, v_ffn1_w_up, v_ffn1_w_down, v_mix_norm, v_w_in, v_na_rel_bias, v_w_branch_a, v_w_branch_b, v_w_out, v_ffn2_norm, v_ffn2_w_up, v_ffn2_w_down, v_final_norm):
    t = x.shape[0] * x.shape[1]
    xs = x.reshape(t, D)
    tgt = loss_target.reshape(t, D)
    tables = _rope_tables()

    col_sharded = dict(up1=ffn1_w_up, win=w_in, wa=w_branch_a, wb=w_branch_b, up2=ffn2_w_up)
    row_sharded = dict(down1=ffn1_w_down, wo=w_out, down2=ffn2_w_down)
    shard = [{} for _ in range(DEPTH)]
    for layer in range(DEPTH):
        for name, arr in col_sharded.items():
            shard[layer][name] = arr[layer].T.astype(BF16)
        for name, arr in row_sharded.items():
            shard[layer][name] = arr[layer].astype(BF16)

    weights = [{} for _ in range(DEPTH)]
    weights[0]["up1"] = _allgather([shard[0]["up1"]], "first")[0]
    travel = [(0, ("down1",)), (0, ("win",)), (0, ("wa", "wb", "wo")), (0, ("up2", "down2")),
              (1, ("up1", "down1")), (1, ("win",)), (1, ("wa", "wb", "wo")), (1, ("up2", "down2"))]
    pending = {}
    after = weights[0]["up1"]
    for i, (layer, names) in enumerate(travel):
        handle, after = _exchange_start("gather", [shard[layer][n] for n in names], after, f"w{i}")
        for n in names:
            pending[layer, n] = (i, handle, names)
    zero = after[0, 0]

    def fetcher(layer):
        def fetch(name, behind):
            if (layer, name) in pending:
                i, handle, names = pending[layer, name]
                for n, got in zip(names, _exchange_wait(handle, behind, f"w{i}")):
                    weights[layer][n] = got
                    del pending[layer, n]
            return weights[layer][name]
        return fetch

    saved = []
    h = xs
    hn = _norm_fwd(xs, ffn1_norm[0] + zero, "first")
    for layer in range(DEPTH):
        bias = _na_bias_table(na_rel_bias[layer])
        fetch = fetcher(layer)
        after_ffn2 = ffn1_norm[layer + 1] if layer + 1 < DEPTH else None
        h, hn, s1 = _ffn_forward(h, hn, fetch, ("up1", "down1"), f"f1l{layer}", mix_norm[layer])
        h, hn, s2 = _mixer_forward(h, hn, fetch, bias, tables, f"l{layer}", ffn2_norm[layer])
        h, hn, s3 = _ffn_forward(h, hn, fetch, ("up2", "down2"), f"f2l{layer}", after_ffn2)
        saved.append((s1, s2, s3, bias))
    loss_part, dh, dh_b, d_final = _loss_head(h, final_norm, tgt)

    d_norms = ([None] * DEPTH, [None] * DEPTH, [None] * DEPTH)
    d_bias = [None] * DEPTH
    sent = {}

    def sender(layer, suffix):
        def send(names, grads):
            tag = f"g{layer}{names[0]}{suffix}"
            handle, token = _exchange_start("scatter", grads, None, tag)
            for i, n in enumerate(names):
                sent[layer, n + suffix] = (handle, i, tag)
            return token
        return send

    for layer in reversed(range(DEPTH)):
        w = weights[layer]
        s1, s2, s3, bias = saved[layer]
        dh, dh_b, d_norms[2][layer] = _ffn_backward(
            dh, dh_b, s3, ffn2_norm[layer], w["up2"].reshape(2, F, D), w["down2"], f"f2l{layer}", sender(layer, "2"))
        dh, dh_b, d_norms[1][layer], d_bias[layer] = _mixer_backward(
            dh, dh_b, s2, mix_norm[layer], (w["win"], w["wa"], w["wb"], w["wo"]), bias, tables, f"l{layer}",
            sender(layer, ""))
        dh, dh_b, d_norms[0][layer] = _ffn_backward(
            dh, dh_b, s1, ffn1_norm[layer], w["up1"].reshape(2, F, D), w["down1"], f"f1l{layer}", sender(layer, "1"))
    grad_x = dh.reshape(x.shape)

    originals = dict(up1=(ffn1_w_up, m_ffn1_w_up, v_ffn1_w_up), down1=(ffn1_w_down, m_ffn1_w_down, v_ffn1_w_down),
                     win=(w_in, m_w_in, v_w_in), wa=(w_branch_a, m_w_branch_a, v_w_branch_a),
                     wb=(w_branch_b, m_w_branch_b, v_w_branch_b), wo=(w_out, m_w_out, v_w_out),
                     up2=(ffn2_w_up, m_ffn2_w_up, v_ffn2_w_up), down2=(ffn2_w_down, m_ffn2_w_down, v_ffn2_w_down))
    big = {}
    behind = dh
    landed = {}

    def received(layer, name):
        handle, i, tag = sent[layer, name]
        if tag not in landed:
            landed[tag] = _exchange_wait(handle, behind, tag)
        return landed[tag][i]

    for name in ("down2", "up2", "win", "wa", "wb", "wo", "down1", "up1"):
        g = _sum_slots(received(0, name), received(1, name), name)
        wv, mv, vv = originals[name]
        if name in col_sharded:
            wv, mv, vv = (jnp.swapaxes(t, 1, 2) for t in (wv, mv, vv))
        big[name] = (g, *_adamw(wv, g, mv, vv, name))
        behind = big[name][1]
        if name in col_sharded:
            big[name] = tuple(jnp.swapaxes(t, 1, 2) for t in big[name])

    small = _allreduce_small(_pack_small(d_norms, d_bias, d_final, loss_part[0, :1]), behind)
    g_norms, g_bias, g_final, loss = _unpack_small(small)
    w_small = _pack_small((ffn1_norm, mix_norm, ffn2_norm), na_rel_bias, final_norm)
    m_small = _pack_small((m_ffn1_norm, m_mix_norm, m_ffn2_norm), m_na_rel_bias, m_final_norm)
    v_small = _pack_small((v_ffn1_norm, v_mix_norm, v_ffn2_norm), v_na_rel_bias, v_final_norm)
    upd = _adamw(w_small[None], small[None], m_small[None], v_small[None], "small")
    small_out = [(g_norms, g_bias, g_final)] + [_unpack_small(u[0])[:3] for u in upd]

    outputs = [loss, grad_x]
    for kind in range(4):
        norms, bias_k, final_k = small_out[kind]
        outputs += [norms[0], big["up1"][kind], big["down1"][kind], norms[1], big["win"][kind], bias_k,
                    big["wa"][kind], big["wb"][kind], big["wo"][kind], norms[2], big["up2"][kind],
                    big["down2"][kind], final_k]
    return tuple(outputs)
```

```python
import numpy as np

import jax
import jax.numpy as jnp
from jax import lax
from jax.experimental import pallas as pl
from jax.experimental.pallas import tpu as pltpu

F32 = jnp.float32
BF16 = jnp.bfloat16
SDS = jax.ShapeDtypeStruct
BS = pl.BlockSpec
MESH = pl.DeviceIdType.MESH

D = 1024
S = 2048
F = 2816
DEPTH = 2
HEAD_DIM = 64
DILATIONS = (1, 4, 16)
HALF = 64
QKV_A = 2304
QKV_B = 1536
IN_W = 5888
N_DEV = 8
NA_ROWS = 32
GRID_W = 64
NA_KR = 8
ROPE_THETA = 10000.0
RMS_EPS = 1e-6
NEG = -1e30
SCALE = HEAD_DIM ** -0.5
ADAM_LR, ADAM_B1, ADAM_B2, ADAM_EPS, ADAM_WD, ADAM_STEP = 0.001, 0.9, 0.999, 1e-08, 0.01, 10
VMEM_LIMIT_V7X = 52 * 1024 * 1024
SMALL_ROWS = 120
BIAS_PAD = 3840
NA_FWD_ROWS = 4
NA_BWD_ROWS = 4
DIL_FWD_TILES = 4
DIL_BWD_TILES = 4


def _cp(*sem):
    return pltpu.CompilerParams(dimension_semantics=sem, vmem_limit_bytes=VMEM_LIMIT_V7X)


def _dot_nn(a, b):
    return jnp.dot(a, b, preferred_element_type=F32)


def _dot_nt(a, b):
    return lax.dot_general(a, b, (((1,), (1,)), ((), ())), preferred_element_type=F32)


def _dot_tn(a, b):
    return lax.dot_general(a, b, (((0,), (0,)), ((), ())), preferred_element_type=F32)


def _ds(start, size, stride):
    return pl.ds(start, size) if stride == 1 else pl.ds(start, size, stride=stride)


def _norm_fwd(x, g, tag):
    t = x.shape[0]
    tm = 512

    def body(x_ref, g_ref, o_ref):
        xv = x_ref[...]
        r = lax.rsqrt(jnp.mean(xv * xv, axis=-1, keepdims=True) + RMS_EPS)
        o_ref[...] = (xv * r * g_ref[...]).astype(BF16)

    return pl.pallas_call(
        body, name=f"norm_fwd_{tag}", grid=(t // tm,),
        in_specs=[BS((tm, D), lambda i: (i, 0)), BS((1, D), lambda i: (0, 0))],
        out_specs=BS((tm, D), lambda i: (i, 0)),
        out_shape=SDS((t, D), BF16), compiler_params=_cp("parallel"),
    )(x, g.reshape(1, D))


def _loss_head(x, g, tgt):
    t = x.shape[0]
    tm = 512

    def body(x_ref, g_ref, t_ref, loss_ref, dx_ref, dxb_ref, dg_ref):
        @pl.when(pl.program_id(0) == 0)
        def _():
            dg_ref[...] = jnp.zeros_like(dg_ref)
            loss_ref[...] = jnp.zeros_like(loss_ref)

        xv = x_ref[...]
        gv = g_ref[...]
        r = lax.rsqrt(jnp.mean(xv * xv, axis=-1, keepdims=True) + RMS_EPS)
        xh = xv * r
        e = xh * gv - t_ref[...]
        loss_ref[...] += 0.5 * jnp.sum(jnp.mean(e * e, axis=-1, keepdims=True), axis=0, keepdims=True)
        dy = e * (1.0 / D)
        u = dy * gv
        dx = r * (u - xh * jnp.mean(xh * u, axis=-1, keepdims=True))
        dx_ref[...] = dx
        dxb_ref[...] = dx.astype(BF16)
        dg_ref[...] += jnp.sum(dy * xh, axis=0, keepdims=True)

    row = BS((tm, D), lambda i: (i, 0))
    vec = BS((1, D), lambda i: (0, 0))
    return pl.pallas_call(
        body, name="loss_head", grid=(t // tm,),
        in_specs=[row, vec, row], out_specs=[BS((1, 128), lambda i: (0, 0)), row, row, vec],
        out_shape=[SDS((1, 128), F32), SDS((t, D), F32), SDS((t, D), BF16), SDS((1, D), F32)],
        compiler_params=_cp("arbitrary"),
    )(x, g.reshape(1, D), tgt)


def _mm_nn(a, w, tag, res=None, scale=1.0, tm=512, tn=None, next_g=None):
    c_n, t, k = a.shape
    n = w.shape[2]
    tn = n if tn is None else tn
    assert next_g is None or tn == n
    n_in = 2 + (res is not None) + (next_g is not None)

    def body(*refs):
        a_ref, w_ref = refs[0], refs[1]
        acc = _dot_nn(a_ref[0].astype(BF16), w_ref[0])
        for c in range(1, c_n):
            acc = acc + _dot_nn(a_ref[c].astype(BF16), w_ref[c])
        if scale != 1.0:
            acc = acc * scale
        if res is not None:
            acc = refs[2][...] + acc
        refs[n_in][...] = acc
        if next_g is not None:
            r = lax.rsqrt(jnp.mean(acc * acc, axis=-1, keepdims=True) + RMS_EPS)
            refs[n_in + 1][...] = (acc * r * refs[n_in - 1][...]).astype(BF16)

    in_specs = [BS((c_n, tm, k), lambda i, j: (0, i, 0)), BS((c_n, k, tn), lambda i, j: (0, 0, j))]
    args = [a, w]
    out_specs = [BS((tm, tn), lambda i, j: (i, j))]
    out_shape = [SDS((t, n), F32)]
    if res is not None:
        in_specs.append(BS((tm, tn), lambda i, j: (i, j)))
        args.append(res)
    if next_g is not None:
        in_specs.append(BS((1, n), lambda i, j: (0, 0)))
        args.append(next_g.reshape(1, n))
        out_specs.append(BS((tm, tn), lambda i, j: (i, j)))
        out_shape.append(SDS((t, n), BF16))
    got = pl.pallas_call(
        body, name=f"mm_nn_{tag}", grid=(t // tm, n // tn), in_specs=in_specs, out_specs=out_specs,
        out_shape=out_shape, compiler_params=_cp("parallel", "parallel"),
    )(*args)
    return got if next_g is not None else got[0]


def _mm_nn_norm_bwd(a, w, x, g, dres, tag, tm=256):
    c_n, t, k = a.shape

    def body(a_ref, w_ref, x_ref, g_ref, dr_ref, dx_ref, dxb_ref, dg_ref):
        @pl.when(pl.program_id(0) == 0)
        def _():
            dg_ref[...] = jnp.zeros_like(dg_ref)

        dh = _dot_nn(a_ref[0], w_ref[0])
        for c in range(1, c_n):
            dh = dh + _dot_nn(a_ref[c], w_ref[c])
        xv = x_ref[...]
        r = lax.rsqrt(jnp.mean(xv * xv, axis=-1, keepdims=True) + RMS_EPS)
        xh = xv * r
        u = dh * g_ref[...]
        dx = dr_ref[...] + r * (u - xh * jnp.mean(xh * u, axis=-1, keepdims=True))
        dx_ref[...] = dx
        dxb_ref[...] = dx.astype(BF16)
        dg_ref[...] += jnp.sum(dh * xh, axis=0, keepdims=True)

    row = BS((tm, D), lambda i: (i, 0))
    vec = BS((1, D), lambda i: (0, 0))
    return pl.pallas_call(
        body, name=f"mm_nn_norm_bwd_{tag}", grid=(t // tm,),
        in_specs=[BS((c_n, tm, k), lambda i: (0, i, 0)), BS((c_n, k, D), lambda i: (0, 0, 0)), row, vec, row],
        out_specs=[row, row, vec], out_shape=[SDS((t, D), F32), SDS((t, D), BF16), SDS((1, D), F32)],
        compiler_params=_cp("arbitrary"),
    )(a, w, x, g.reshape(1, D), dres)


def _mm_nt_rows(a, w, tag, tm, tn, n_total, w_row0, rope=None):
    t, k = a.shape
    assert w_row0 % tn == 0 and n_total % tn == 0
    j0 = w_row0 // tn

    def body(a_ref, w_ref, *rest):
        o_ref = rest[-1]
        o_ref[...] = _dot_nt(a_ref[...].astype(BF16), w_ref[...])
        if rope is not None:
            @pl.when(pl.program_id(0) == 0)
            def _():
                c = rest[0][...]
                sg = rest[1][...]
                first = (lax.broadcasted_iota(jnp.int32, (tm, 128), 1) % HEAD_DIM) < HEAD_DIM // 2
                for col in range(0, rope[2], 128):
                    v = o_ref[:, col:col + 128]
                    o_ref[:, col:col + 128] = v * c + _swap_halves(v, first) * sg

    in_specs = [BS((tm, k), lambda j, i: (i, 0)), BS((tn, k), lambda j, i: (j0 + j, 0))]
    args = [a, w]
    if rope is not None:
        assert rope[2] <= tn
        in_specs += [BS((tm, 128), lambda j, i: (i % (S // tm), 0))] * 2
        args += [rope[0], rope[1]]
    return pl.pallas_call(
        body, name=f"mm_nt_{tag}", grid=(n_total // tn, t // tm), in_specs=in_specs,
        out_specs=BS((tm, tn), lambda j, i: (i, j)), out_shape=SDS((t, n_total), F32),
        compiler_params=_cp("parallel", "parallel"),
    )(*args)


def _mm_tn(a, b, tag, scale=1.0, tmm=256, tk=None):
    c_n, t, m = a.shape
    n = b.shape[1]
    tk = t if tk is None else tk
    nk = t // tk

    def body_one(a_ref, b_ref, o_ref):
        o_ref[...] = (_dot_tn(a_ref[...].astype(BF16), b_ref[...].astype(BF16)) * scale).astype(BF16)

    def body_acc(a_ref, b_ref, o_ref, acc_ref):
        kk = pl.program_id(2)

        @pl.when(kk == 0)
        def _():
            acc_ref[...] = jnp.zeros_like(acc_ref)

        acc_ref[...] += _dot_tn(a_ref[...].astype(BF16), b_ref[...].astype(BF16))

        @pl.when(kk == nk - 1)
        def _():
            o_ref[...] = (acc_ref[...] * scale).astype(BF16)

    return pl.pallas_call(
        body_one if nk == 1 else body_acc, name=f"mm_tn_{tag}", grid=(c_n, m // tmm, nk),
        in_specs=[BS((None, tk, tmm), lambda c, mi, kk: (c, kk, mi)), BS((tk, n), lambda c, mi, kk: (kk, 0))],
        out_specs=BS((None, tmm, n), lambda c, mi, kk: (c, mi, 0)),
        out_shape=SDS((c_n, m, n), BF16), scratch_shapes=[] if nk == 1 else [pltpu.VMEM((tmm, n), F32)],
        compiler_params=_cp("parallel", "parallel", "arbitrary"),
    )(a, b)


def _ffn_up(hn, wut, tag):
    t = hn.shape[0]
    tm, tn = 512, 1408

    def body(h_ref, w_ref, gu_ref, act_ref):
        h = h_ref[...]
        g = _dot_nt(h, w_ref[0])
        u = _dot_nt(h, w_ref[1])
        sg = jax.nn.sigmoid(g)
        silu = g * sg
        gu_ref[0] = (u * (sg + silu * (1.0 - sg))).astype(BF16)
        gu_ref[1] = silu.astype(BF16)
        act_ref[...] = (silu * u).astype(BF16)

    return pl.pallas_call(
        body, name=f"ffn_up_{tag}", grid=(F // tn, t // tm),
        in_specs=[BS((tm, D), lambda j, i: (i, 0)), BS((2, tn, D), lambda j, i: (0, j, 0))],
        out_specs=[BS((2, tm, tn), lambda j, i: (0, i, j)), BS((tm, tn), lambda j, i: (i, j))],
        out_shape=[SDS((2, t, F), BF16), SDS((t, F), BF16)],
        compiler_params=_cp("parallel", "parallel"),
    )(hn, wut)


def _ffn_dact(dxo, wd, gu, tie, tag):
    t = dxo.shape[0]
    tm, tn = 512, 1408

    def body(d_ref, w_ref, gu_ref, tie_ref, o_ref):
        dact = _dot_nt(d_ref[...] * 0.5, w_ref[...])
        o_ref[0] = (dact * gu_ref[0].astype(F32)).astype(BF16)
        o_ref[1] = (dact * gu_ref[1].astype(F32)).astype(BF16)

    return pl.pallas_call(
        body, name=f"ffn_dact_{tag}", grid=(F // tn, t // tm),
        in_specs=[BS((tm, D), lambda j, i: (i, 0)), BS((tn, D), lambda j, i: (j, 0)),
                  BS((2, tm, tn), lambda j, i: (0, i, j)), BS((8, 128), lambda j, i: (0, 0))],
        out_specs=BS((2, tm, tn), lambda j, i: (0, i, j)),
        out_shape=SDS((2, t, F), BF16), compiler_params=_cp("parallel", "parallel"),
    )(dxo, wd, gu, tie)


def _rope_tables():
    half = HEAD_DIM // 2
    inv_freq = ROPE_THETA ** (-jnp.arange(half, dtype=F32) / half)
    ang = jnp.arange(S).astype(F32)[:, None] * inv_freq[None, :]
    cos, sin = jnp.cos(ang), jnp.sin(ang)
    return jnp.concatenate([cos, cos, cos, cos], axis=1), jnp.concatenate([-sin, sin, -sin, sin], axis=1)


def _swap_halves(t, first_half):
    return jnp.where(first_half, pltpu.roll(t, 96, 1), pltpu.roll(t, 32, 1))


def _rope_bwd(dqs, dks, cos_t, sin_t):
    t = dqs[0].shape[0]
    tm = 512

    def body(*refs):
        c = refs[6][...]
        sg = refs[7][...]
        o_ref = refs[8]
        first = (lax.broadcasted_iota(jnp.int32, (tm, 128), 1) % HEAD_DIM) < HEAD_DIM // 2
        for a in range(6):
            for hp in range(2):
                v = refs[a][:, 128 * hp:128 * (hp + 1)]
                col = 128 * (2 * a + hp)
                o_ref[:, col:col + 128] = (v * c + _swap_halves(v * sg, first)).astype(BF16)

    blk = BS((tm, 256), lambda i: (i, 0))
    tab = BS((tm, 128), lambda i: (i % (S // tm), 0))
    return pl.pallas_call(
        body, name="rope_bwd", grid=(t // tm,), in_specs=[blk] * 6 + [tab, tab],
        out_specs=BS((tm, 1536), lambda i: (i, 0)), out_shape=SDS((t, 1536), BF16),
        compiler_params=_cp("parallel"),
    )(*dqs, *dks, cos_t, sin_t)


def _head_masks():
    lane = lax.broadcasted_iota(jnp.int32, (1, 128), 1)
    m0 = (lane < HEAD_DIM).astype(F32)
    return m0, 1.0 - m0


def _dil_geometry(d):
    sub = S // d
    q_rows = 128
    k_rows = min(256, sub)
    return sub, q_rows, sub // q_rows, k_rows


def _dil_tile(idx, d):
    sub, q_rows, nb, k_rows = _dil_geometry(d)
    r = idx // nb
    n = idx % nb
    k_sub = jnp.clip(q_rows * n - HALF, 0, sub - k_rows)
    if d == 1:
        q_start = pl.multiple_of(q_rows * n, q_rows)
        k_start = pl.multiple_of(k_sub, HALF)
    else:
        q_start = q_rows * n * d + r
        k_start = k_sub * d + r
    ii = lax.broadcasted_iota(jnp.int32, (q_rows, k_rows), 0)
    jj = lax.broadcasted_iota(jnp.int32, (q_rows, k_rows), 1)
    valid = jnp.abs(jj - ii + (k_sub - q_rows * n)) <= HALF
    return q_start, k_start, valid


def _dil_specs(grp):
    qs = BS((S, 128), lambda b, hp: (b, 2 * grp + hp))
    ks = BS((S, 128), lambda b, hp: (b, 6 + 2 * grp + hp))
    vs = BS((S, 128), lambda b, hp: (b, 12 + 2 * grp + hp))
    own = BS((S, 128), lambda b, hp: (b, hp))
    return qs, ks, vs, own


def _dil_fwd(qkr, proj, grp):
    t = qkr.shape[0]
    d = DILATIONS[grp]
    _, q_rows, nb, k_rows = _dil_geometry(d)

    def body(q_ref, k_ref, v_ref, o_ref, l_ref):
        masks = _head_masks()

        def step(i0, carry):
            geo = [_dil_tile(i0 * DIL_FWD_TILES + j, d) for j in range(DIL_FWD_TILES)]
            tiles = [(j, h) for j in range(DIL_FWD_TILES) for h in range(2)]
            qs = [q_ref[_ds(g[0], q_rows, d), :] for g in geo]
            kbs = [k_ref[_ds(g[1], k_rows, d), :].astype(BF16) for g in geo]
            ss = [jnp.where(geo[j][2], _dot_nt((qs[j] * masks[h]).astype(BF16), kbs[j]) * SCALE, NEG) for j, h in tiles]
            mxs = [jnp.max(s, axis=1, keepdims=True) for s in ss]
            ps = [jnp.exp(s - mx) for s, mx in zip(ss, mxs)]
            dens = [jnp.sum(p, axis=1, keepdims=True) for p in ps]
            vs = [v_ref[_ds(g[1], k_rows, d), :] for g in geo]
            outs = [_dot_nn(p.astype(BF16), (vs[j] * masks[h]).astype(BF16)) / den
                    for p, den, (j, h) in zip(ps, dens, tiles)]
            for j, g in enumerate(geo):
                o_ref[_ds(g[0], q_rows, d), :] = outs[2 * j] + outs[2 * j + 1]
                l_ref[_ds(g[0], q_rows, d), :] = (
                    (mxs[2 * j] + jnp.log(dens[2 * j])) * masks[0] + (mxs[2 * j + 1] + jnp.log(dens[2 * j + 1])) * masks[1])
            return carry

        lax.fori_loop(0, d * nb // DIL_FWD_TILES, step, 0)

    qs, ks, vs, own = _dil_specs(grp)
    return pl.pallas_call(
        body, name=f"dil_fwd_{grp}", grid=(t // S, 2), in_specs=[qs, ks, vs], out_specs=[own, own],
        out_shape=[SDS((t, 256), F32), SDS((t, 256), F32)], compiler_params=_cp("parallel", "parallel"),
    )(qkr, qkr, proj)


def _dil_bwd(qkr, proj, do, dlp, lse, grp):
    t = qkr.shape[0]
    d = DILATIONS[grp]
    _, q_rows, nb, k_rows = _dil_geometry(d)

    def body(q_ref, k_ref, v_ref, do_ref, dl_ref, l_ref, dq_ref, dk_ref, dv_ref):
        masks = _head_masks()
        dk_ref[...] = jnp.zeros_like(dk_ref)
        dv_ref[...] = jnp.zeros_like(dv_ref)

        def step(i0, carry):
            geo = [_dil_tile(i0 * DIL_BWD_TILES + j, d) for j in range(DIL_BWD_TILES)]
            tiles = [(j, h) for j in range(DIL_BWD_TILES) for h in range(2)]
            q_ds = [_ds(g[0], q_rows, d) for g in geo]
            k_ds = [_ds(g[1], k_rows, d) for g in geo]
            qs = [q_ref[r, :] for r in q_ds]
            ks = [k_ref[r, :] for r in k_ds]
            kbs = [k.astype(BF16) for k in ks]
            vbs = [v_ref[r, :].astype(BF16) for r in k_ds]
            dos = [do_ref[r, :] for r in q_ds]
            dls = [dl_ref[r, :] for r in q_ds]
            lss = [l_ref[r, :] for r in q_ds]
            qhs = [(qs[j] * masks[h]).astype(BF16) for j, h in tiles]
            dohs = [(dos[j] * masks[h]).astype(BF16) for j, h in tiles]
            ss = [jnp.where(geo[j][2], _dot_nt(qh, kbs[j]) * SCALE, NEG) for qh, (j, h) in zip(qhs, tiles)]
            ps = [jnp.exp(s - lss[j][:, HEAD_DIM * h:HEAD_DIM * h + 1]) for s, (j, h) in zip(ss, tiles)]
            dps = [_dot_nt(doh, vbs[j]) for doh, (j, h) in zip(dohs, tiles)]
            dss = [(p * (dp - dls[j][:, HEAD_DIM * h:HEAD_DIM * h + 1])).astype(BF16)
                   for p, dp, (j, h) in zip(ps, dps, tiles)]
            dqs = [_dot_nn(ds, (ks[j] * masks[h]).astype(BF16)) for ds, (j, h) in zip(dss, tiles)]
            dkws = [_dot_tn(ds, qh) for ds, qh in zip(dss, qhs)]
            dvws = [_dot_tn(p.astype(BF16), doh) for p, doh in zip(ps, dohs)]
            for j in range(DIL_BWD_TILES):
                dq_ref[q_ds[j], :] = (dqs[2 * j] + dqs[2 * j + 1]) * SCALE
                dk_ref[k_ds[j], :] += (dkws[2 * j] + dkws[2 * j + 1]) * SCALE
                dv_ref[k_ds[j], :] += dvws[2 * j] + dvws[2 * j + 1]
            return carry

        lax.fori_loop(0, d * nb // DIL_BWD_TILES, step, 0)

    qs, ks, vs, own = _dil_specs(grp)
    return pl.pallas_call(
        body, name=f"dil_bwd_{grp}", grid=(t // S, 2), in_specs=[qs, ks, vs, own, own, own],
        out_specs=[own, own, own], out_shape=[SDS((t, 256), F32)] * 3,
        compiler_params=_cp("parallel", "parallel"),
    )(qkr, qkr, proj, do, dlp, lse)


def _mix_weights(l0, l1, l2):
    mx = jnp.maximum(jnp.maximum(l0, l1), l2)
    e0, e1, e2 = jnp.exp(l0 - mx), jnp.exp(l1 - mx), jnp.exp(l2 - mx)
    den = e0 + e1 + e2
    return e0 / den, e1 / den, e2 / den


def _combine_fwd(outs, lses):
    t = outs[0].shape[0]
    tm = 512

    def body(o0, o1, o2, l0, l1, l2, y_ref):
        w0, w1, w2 = _mix_weights(l0[...], l1[...], l2[...])
        y_ref[...] = w0 * o0[...] + w1 * o1[...] + w2 * o2[...]

    blk = BS((tm, 256), lambda i: (i, 0))
    return pl.pallas_call(
        body, name="combine_fwd", grid=(t // tm,), in_specs=[blk] * 6, out_specs=blk,
        out_shape=SDS((t, 256), F32), compiler_params=_cp("parallel"),
    )(*outs, *lses)


def _head_sum(x):
    a = lax.broadcasted_iota(jnp.int32, (256, 256), 0) // HEAD_DIM
    b = lax.broadcasted_iota(jnp.int32, (256, 256), 1) // HEAD_DIM
    ones = (a == b).astype(BF16)
    hi = x.astype(BF16)
    lo = (x - hi.astype(F32)).astype(BF16)
    return _dot_nn(hi, ones) + _dot_nn(lo, ones)


def _combine_bwd(dya, outs, lses):
    t = dya.shape[0]
    tm = 512

    def body(dy_ref, o0, o1, o2, l0, l1, l2, d0, d1, d2, e0, e1, e2):
        ws = _mix_weights(l0[...], l1[...], l2[...])
        dy = dy_ref[...]
        ya = ws[0] * o0[...] + ws[1] * o1[...] + ws[2] * o2[...]
        hs = _head_sum(dy * ya)
        for w, d_ref, e_ref in zip(ws, (d0, d1, d2), (e0, e1, e2)):
            d_ref[...] = w * dy
            e_ref[...] = w * hs

    blk = BS((tm, 256), lambda i: (i, 0))
    return pl.pallas_call(
        body, name="combine_bwd", grid=(t // tm,), in_specs=[blk] * 7, out_specs=[blk] * 6,
        out_shape=[SDS((t, 256), F32)] * 6, compiler_params=_cp("parallel"),
    )(dya, *outs, *lses)


def _na_bias_table(rel_bias):
    qc = np.arange(GRID_W)[:, None]
    kc = np.arange(GRID_W)[None, :]
    win_lo = np.clip(qc - 8, 0, GRID_W - 16)
    col_valid = (kc >= win_lo) & (kc < win_lo + 16)
    col_idx = np.clip(kc - qc + 15, 0, 30)
    row_idx = np.arange(NA_KR)[:, None] + np.arange(NA_KR)[None, :]
    rows = (row_idx[..., None] == np.arange(2 * NA_KR - 1)).astype(np.float32)
    cols = (col_idx[..., None] == np.arange(31)).astype(np.float32)
    b = jnp.einsum("hrd,ckr,qjd->hcqkj", rel_bias.astype(F32), rows, cols, precision=lax.Precision.HIGHEST)
    b = jnp.where(col_valid[None, None, :, None, :], b, NEG)
    return b.reshape(8, NA_KR, GRID_W, NA_KR * GRID_W)


def _na_row(i):
    lo = jnp.clip(i - NA_KR // 2, 0, NA_ROWS - NA_KR)
    return pl.multiple_of(GRID_W * i, GRID_W), pl.multiple_of(GRID_W * lo, GRID_W), lo - i + NA_KR - 1


def _na_fwd(proj, bias):
    t = proj.shape[0]
    kw = NA_KR * GRID_W

    def body(q_ref, k_ref, v_ref, b_ref, o_ref, l_ref):
        masks = _head_masks()

        def step(i0, carry):
            rows = [_na_row(i0 * NA_FWD_ROWS + j) for j in range(NA_FWD_ROWS)]
            qs = [q_ref[pl.ds(q_start, GRID_W), :] for q_start, _, _ in rows]
            kbs = [k_ref[pl.ds(k_start, kw), :].astype(BF16) for _, k_start, _ in rows]
            tiles = [(j, h) for j in range(NA_FWD_ROWS) for h in range(2)]
            ss = [_dot_nt((qs[j] * masks[h]).astype(BF16), kbs[j]) * SCALE + b_ref[h, rows[j][2]] for j, h in tiles]
            mxs = [jnp.max(s, axis=1, keepdims=True) for s in ss]
            ps = [jnp.exp(s - mx) for s, mx in zip(ss, mxs)]
            dens = [jnp.sum(p, axis=1, keepdims=True) for p in ps]
            pbs = [(p / den).astype(BF16) for p, den in zip(ps, dens)]
            vs = [v_ref[pl.ds(k_start, kw), :] for _, k_start, _ in rows]
            outs = [_dot_nn(pb, (vs[j] * masks[h]).astype(BF16)) for pb, (j, h) in zip(pbs, tiles)]
            for j, (q_start, _, _) in enumerate(rows):
                o_ref[pl.ds(q_start, GRID_W), :] = outs[2 * j] + outs[2 * j + 1]
                l_ref[pl.ds(q_start, GRID_W), :] = (
                    (mxs[2 * j] + jnp.log(dens[2 * j])) * masks[0] + (mxs[2 * j + 1] + jnp.log(dens[2 * j + 1])) * masks[1])
            return carry

        lax.fori_loop(0, NA_ROWS // NA_FWD_ROWS, step, 0)

    c0 = QKV_A // 128
    own = BS((S, 128), lambda b, hp: (b, hp))
    return pl.pallas_call(
        body, name="na_fwd", grid=(t // S, 4),
        in_specs=[BS((S, 128), lambda b, hp: (b, c0 + hp)), BS((S, 128), lambda b, hp: (b, c0 + 4 + hp)),
                  BS((S, 128), lambda b, hp: (b, c0 + 8 + hp)),
                  BS((2, NA_KR, GRID_W, kw), lambda b, hp: (hp, 0, 0, 0))],
        out_specs=[own, own], out_shape=[SDS((t, 512), F32), SDS((t, 512), F32)],
        compiler_params=_cp("parallel", "parallel"),
    )(proj, proj, proj, bias)


def _na_bwd(proj, bias, dyb, yb, lse):
    t = proj.shape[0]
    kw = NA_KR * GRID_W

    def body(q_ref, k_ref, v_ref, b_ref, do_ref, o_ref, l_ref, dq_ref, dk_ref, dv_ref, db_ref):
        masks = _head_masks()

        @pl.when(pl.program_id(1) == 0)
        def _():
            db_ref[...] = jnp.zeros_like(db_ref)

        dk_ref[...] = jnp.zeros_like(dk_ref)
        dv_ref[...] = jnp.zeros_like(dv_ref)

        def step(i0, carry):
            rows = [_na_row(i0 * NA_BWD_ROWS + j) for j in range(NA_BWD_ROWS)]
            tiles = [(j, h) for j in range(NA_BWD_ROWS) for h in range(2)]
            q_ds = [pl.ds(r[0], GRID_W) for r in rows]
            k_ds = [pl.ds(r[1], kw) for r in rows]
            qs = [q_ref[r, :] for r in q_ds]
            ks = [k_ref[r, :] for r in k_ds]
            kbs = [k.astype(BF16) for k in ks]
            vbs = [v_ref[r, :].astype(BF16) for r in k_ds]
            dos = [do_ref[r, :] for r in q_ds]
            os_ = [o_ref[r, :] for r in q_ds]
            lss = [l_ref[r, :] for r in q_ds]
            qhs = [(qs[j] * masks[h]).astype(BF16) for j, h in tiles]
            dohs = [(dos[j] * masks[h]).astype(BF16) for j, h in tiles]
            deltas = [jnp.sum(dos[j] * os_[j] * masks[h], axis=1, keepdims=True) for j, h in tiles]
            ss = [_dot_nt(qh, kbs[j]) * SCALE + b_ref[h, rows[j][2]] for qh, (j, h) in zip(qhs, tiles)]
            ps = [jnp.exp(s - lss[j][:, HEAD_DIM * h:HEAD_DIM * h + 1]) for s, (j, h) in zip(ss, tiles)]
            dps = [_dot_nt(doh, vbs[j]) for doh, (j, h) in zip(dohs, tiles)]
            dss = [p * (dp - delta) for p, dp, delta in zip(ps, dps, deltas)]
            for ds, (j, h) in zip(dss, tiles):
                db_ref[h, rows[j][2]] += ds
            dsbs = [ds.astype(BF16) for ds in dss]
            dqs = [_dot_nn(dsb, (ks[j] * masks[h]).astype(BF16)) for dsb, (j, h) in zip(dsbs, tiles)]
            dkws = [_dot_tn(dsb, qh) for dsb, qh in zip(dsbs, qhs)]
            dvws = [_dot_tn(p.astype(BF16), doh) for p, doh in zip(ps, dohs)]
            for j in range(NA_BWD_ROWS):
                dq_ref[q_ds[j], :] = (dqs[2 * j] + dqs[2 * j + 1]) * SCALE
                dk_ref[k_ds[j], :] += (dkws[2 * j] + dkws[2 * j + 1]) * SCALE
                dv_ref[k_ds[j], :] += dvws[2 * j] + dvws[2 * j + 1]
            return carry

        lax.fori_loop(0, NA_ROWS // NA_BWD_ROWS, step, 0)

    c0 = QKV_A // 128
    own = BS((S, 128), lambda hp, b: (b, hp))
    tab = BS((2, NA_KR, GRID_W, kw), lambda hp, b: (hp, 0, 0, 0))
    return pl.pallas_call(
        body, name="na_bwd", grid=(4, t // S),
        in_specs=[BS((S, 128), lambda hp, b: (b, c0 + hp)), BS((S, 128), lambda hp, b: (b, c0 + 4 + hp)),
                  BS((S, 128), lambda hp, b: (b, c0 + 8 + hp)), tab, own, own, own],
        out_specs=[own, own, own, tab],
        out_shape=[SDS((t, 512), F32)] * 3 + [SDS((8, NA_KR, GRID_W, kw), F32)],
        compiler_params=_cp("parallel", "arbitrary"),
    )(proj, proj, proj, bias, dyb, yb, lse)


def _na_dbias_lane_map():
    kw = NA_KR * GRID_W
    lane = np.arange(kw)
    blk, m = lane // GRID_W, lane % GRID_W
    target = np.full(kw, -1)
    target[m < 16] = (blk * 32 + 15 + m)[m < 16]
    target[m >= 49] = (((blk + 1) % NA_KR) * 32 + m - 49)[m >= 49]
    return jnp.asarray(target[:, None] == np.arange(kw)[None, :], BF16)


def _na_dbias(db):
    kw = NA_KR * GRID_W

    def body(x_ref, map_ref, o_ref, z_ref):
        for cls in range(NA_KR):
            xv = x_ref[cls]
            y = xv[0:8]
            for g in range(1, GRID_W // 8):
                y = y + pltpu.roll(xv[8 * g:8 * g + 8], kw - 8 * g, 1)
            d = y[0:1]
            for s in range(1, 8):
                d = d + pltpu.roll(y[s:s + 1], kw - s, 1)
            z_ref[cls:cls + 1, :] = d
        z = z_ref[...]
        hi = z.astype(BF16)
        lo = (z - hi.astype(F32)).astype(BF16)
        e = _dot_nn(hi, map_ref[...]) + _dot_nn(lo, map_ref[...])
        out = e[0:1]
        for cls in range(1, NA_KR):
            out = out + pltpu.roll(e[cls:cls + 1], 32 * cls, 1)
        o_ref[...] = jnp.broadcast_to(out, (8, kw))

    return pl.pallas_call(
        body, name="na_dbias", grid=(8,),
        in_specs=[BS((None, NA_KR, GRID_W, kw), lambda h: (h, 0, 0, 0)), BS((kw, kw), lambda h: (0, 0))],
        out_specs=BS((None, 8, kw), lambda h: (h, 0, 0)), out_shape=SDS((8, 8, kw), F32),
        scratch_shapes=[pltpu.VMEM((8, kw), F32)], compiler_params=_cp("parallel"),
    )(db, _na_dbias_lane_map())


def _merge_fwd(ya, yb, proj, wat, wbt):
    t = ya.shape[0]
    tm, tn = 512, 256
    ca = (QKV_A + QKV_B) // tn
    cb = ca + D // tn

    def body(ya_ref, yb_ref, la_ref, lb_ref, wa_ref, wb_ref, m_ref, za_ref, zb_ref):
        za = _dot_nt(ya_ref[...].astype(BF16), wa_ref[...])
        zb = _dot_nt(yb_ref[...].astype(BF16), wb_ref[...])
        m_ref[...] = (jax.nn.sigmoid(la_ref[...]) * za + jax.nn.sigmoid(lb_ref[...]) * zb).astype(BF16)
        za_ref[...] = za.astype(BF16)
        zb_ref[...] = zb.astype(BF16)

    out = BS((tm, tn), lambda i, j: (i, j))
    return pl.pallas_call(
        body, name="merge_fwd", grid=(t // tm, D // tn),
        in_specs=[BS((tm, 256), lambda i, j: (i, 0)), BS((tm, 512), lambda i, j: (i, 0)),
                  BS((tm, tn), lambda i, j: (i, ca + j)), BS((tm, tn), lambda i, j: (i, cb + j)),
                  BS((tn, 256), lambda i, j: (j, 0)), BS((tn, 512), lambda i, j: (j, 0))],
        out_specs=[out, out, out], out_shape=[SDS((t, D), BF16)] * 3,
        compiler_params=_cp("parallel", "parallel"),
    )(ya, yb, proj, proj, wat, wbt)


def _merge_bwd(dm, za, zb, proj):
    t = dm.shape[0]
    tm, tn = 512, 256
    ca = (QKV_A + QKV_B) // tn
    cb = ca + D // tn

    def body(dm_ref, za_ref, zb_ref, la_ref, lb_ref, dza_ref, dzb_ref, dl_ref):
        dmv = dm_ref[...]
        ga = jax.nn.sigmoid(la_ref[...])
        gb = jax.nn.sigmoid(lb_ref[...])
        dza_ref[...] = (dmv * ga).astype(BF16)
        dzb_ref[...] = (dmv * gb).astype(BF16)
        dl_ref[0] = (dmv * za_ref[...].astype(F32) * ga * (1.0 - ga)).astype(BF16)
        dl_ref[1] = (dmv * zb_ref[...].astype(F32) * gb * (1.0 - gb)).astype(BF16)

    blk = BS((tm, tn), lambda i, j: (i, j))
    return pl.pallas_call(
        body, name="merge_bwd", grid=(t // tm, D // tn),
        in_specs=[blk, blk, blk, BS((tm, tn), lambda i, j: (i, ca + j)), BS((tm, tn), lambda i, j: (i, cb + j))],
        out_specs=[blk, blk, BS((2, tm, tn), lambda i, j: (0, i, j))],
        out_shape=[SDS((t, D), BF16), SDS((t, D), BF16), SDS((2, t, D), BF16)],
        compiler_params=_cp("parallel", "parallel"),
    )(dm, za, zb, proj, proj)


def _sum_slots(recv0, recv1, tag):
    _, r, c = recv0.shape
    tr = r if r * c <= 512 * 1024 else r // 2

    def body(a_ref, b_ref, o_ref):
        for layer, ref in enumerate((a_ref, b_ref)):
            acc = ref[0].astype(F32)
            for s in range(1, N_DEV):
                acc = acc + ref[s].astype(F32)
            o_ref[layer] = acc

    blk = BS((N_DEV, tr, c), lambda i: (0, i, 0))
    return pl.pallas_call(
        body, name=f"sum_slots_{tag}", grid=(r // tr,), in_specs=[blk, blk],
        out_specs=BS((2, tr, c), lambda i: (0, i, 0)), out_shape=SDS((2, r, c), F32),
        compiler_params=_cp("parallel"),
    )(recv0, recv1)


def _adamw(w, g, m, v, tag):
    layers, r, c = w.shape
    tr = next(r // k for k in (1, 2, 4, 8) if r // k <= 384 and r % (8 * k) == 0)

    def body(w_ref, g_ref, m_ref, v_ref, d_ref, mo_ref, vo_ref):
        gv = g_ref[...]
        mn = ADAM_B1 * m_ref[...] + (1.0 - ADAM_B1) * gv
        vn = ADAM_B2 * v_ref[...] + (1.0 - ADAM_B2) * (gv * gv)
        m_hat = mn / (1.0 - ADAM_B1 ** ADAM_STEP)
        v_hat = vn / (1.0 - ADAM_B2 ** ADAM_STEP)
        d_ref[...] = -ADAM_LR * (m_hat / (jnp.sqrt(v_hat) + ADAM_EPS) + ADAM_WD * w_ref[...])
        mo_ref[...] = mn
        vo_ref[...] = vn

    blk = BS((None, tr, c), lambda l, i: (l, i, 0))
    return pl.pallas_call(
        body, name=f"adamw_{tag}", grid=(layers, r // tr), in_specs=[blk] * 4, out_specs=[blk] * 3,
        out_shape=[SDS((layers, r, c), F32)] * 3, compiler_params=_cp("parallel", "parallel"),
    )(w, g, m, v)


def _place():
    return lax.axis_index("x"), lax.axis_index("y"), lax.axis_index("c")


def _flip(coord, bit):
    return 1 - coord if bit else coord


def _allgather(shards, tag):
    n_arr = len(shards)
    hbm = BS(memory_space=pl.ANY)

    def body(*refs):
        ins, outs = refs[:n_arr], refs[n_arr:2 * n_arr]
        send_sems, recv_sems, local_sems = refs[2 * n_arr:]
        x, y, c = _place()
        me, sibling = (x, y, c), (x, y, 1 - c)
        chips = [(1 - x, y), (x, 1 - y), (1 - x, 1 - y)]

        def rows(a, p):
            r = shards[a].shape[0]
            return outs[a].at[pl.ds((4 * p[0] + 2 * p[1] + p[2]) * r, r), :]

        def copy(a, k, block, to, src=None):
            return pltpu.make_async_remote_copy(
                src_ref=rows(a, block) if src is None else src, dst_ref=rows(a, block),
                send_sem=send_sems.at[a, k], recv_sem=recv_sems.at[a, k], device_id=to, device_id_type=MESH)

        mine = [pltpu.make_async_copy(ins[a], rows(a, me), local_sems.at[a]) for a in range(n_arr)]
        for cp in mine:
            cp.start()
        first = []
        for a in range(n_arr):
            first.append(copy(a, 0, me, sibling, src=ins[a]))
            first += [copy(a, 1 + j, me, (*chip, c), src=ins[a]) for j, chip in enumerate(chips)]
        for cp in first:
            cp.start()
        passed = []
        for a in range(n_arr):
            for j, chip in enumerate(chips):
                copy(a, 1 + j, (*chip, c), me).wait_recv()
                passed.append(copy(a, 4 + j, (*chip, c), sibling))
                passed[-1].start()
        for a in range(n_arr):
            copy(a, 0, sibling, me).wait_recv()
            for j, chip in enumerate(chips):
                copy(a, 4 + j, (*chip, 1 - c), me).wait_recv()
        for cp in first + passed:
            cp.wait_send()
        for cp in mine:
            cp.wait()

    return pl.pallas_call(
        body, name=f"allgather_{tag}", in_specs=[hbm] * n_arr, out_specs=[hbm] * n_arr,
        out_shape=[SDS((N_DEV * s.shape[0], s.shape[1]), s.dtype) for s in shards],
        scratch_shapes=[pltpu.SemaphoreType.DMA((n_arr, 7)), pltpu.SemaphoreType.DMA((n_arr, 7)),
                        pltpu.SemaphoreType.DMA((n_arr,))],
        compiler_params=pltpu.CompilerParams(has_side_effects=True),
    )(*shards)


def _peers(x, y, c, with_self=False):
    peers = []
    for mask in list(range(1, N_DEV)) + ([0] if with_self else []):
        p = (_flip(x, mask & 4), _flip(y, mask & 2), _flip(c, mask & 1))
        peers.append((p, 4 * p[0] + 2 * p[1] + p[2]))
    return peers


def _exchange_refs(mode, src, land, me, peer):
    if mode == "gather":
        r = src.shape[0]
        return src, land.at[pl.ds(me * r, r), :], land.at[pl.ds(peer * r, r), :]
    r = land.shape[1]
    return src.at[pl.ds(peer * r, r), :], land.at[me], land.at[peer]


HBM_SPEC = BS(memory_space=pltpu.HBM)
SEM_SPEC = BS(memory_space=pltpu.SEMAPHORE)
DATAFLOW = pltpu.SideEffectType.DATAFLOW_SIDE_EFFECTING


def _fresh(shape, dtype, tag):
    def body(o_ref):
        del o_ref

    return pl.pallas_call(body, name=f"fresh_{tag}", out_specs=BS(memory_space=pl.ANY), out_shape=SDS(shape, dtype))()


def _exchange_start(mode, srcs, after, tag):
    n = len(srcs)
    if mode == "gather":
        lands = [_fresh((N_DEV * s.shape[0], s.shape[1]), s.dtype, f"{tag}_{a}") for a, s in enumerate(srcs)]
    else:
        lands = [_fresh((N_DEV, s.shape[0] // N_DEV, s.shape[1]), s.dtype, f"{tag}_{a}") for a, s in enumerate(srcs)]
    behind = [] if after is None else [after]

    def body(*refs):
        src_refs, land_refs = refs[:n], refs[n:2 * n]
        send_sems, recv_sems = refs[2 * n + len(behind)], refs[2 * n + len(behind) + 1]
        token = refs[-1]
        bx, by, bc = _place()
        me = 4 * bx + 2 * by + bc
        for a in range(n):
            for k, (p, idx) in enumerate(_peers(bx, by, bc, with_self=True)):
                out, there, _ = _exchange_refs(mode, src_refs[a], land_refs[a], me, idx)
                pltpu.make_async_remote_copy(
                    src_ref=out, dst_ref=there, send_sem=send_sems.at[N_DEV * a + k],
                    recv_sem=recv_sems.at[N_DEV * a + k], device_id=p, device_id_type=MESH).start()
        token[...] = jnp.zeros_like(token)

    res = pl.pallas_call(
        body, name=f"{mode}_start_{tag}",
        out_shape=(pltpu.SemaphoreType.DMA((N_DEV * n,)), pltpu.SemaphoreType.DMA((N_DEV * n,)),
                   *[pltpu.HBM(s.shape, s.dtype) for s in srcs], *[pltpu.HBM(l.shape, l.dtype) for l in lands],
                   SDS((8, 128), F32)),
        in_specs=[HBM_SPEC] * (2 * n) + [BS(memory_space=pl.ANY)] * len(behind),
        out_specs=(SEM_SPEC, SEM_SPEC, *[HBM_SPEC] * (2 * n), BS(memory_space=pltpu.VMEM)),
        input_output_aliases={i: 2 + i for i in range(2 * n)},
        compiler_params=pltpu.CompilerParams(has_side_effects=DATAFLOW),
    )(*[pltpu.with_memory_space_constraint(s, pltpu.HBM) for s in srcs],
      *[pltpu.with_memory_space_constraint(l, pltpu.HBM) for l in lands], *behind)
    return (mode, res[0], res[1], res[2:2 + n], res[2 + n:2 + 2 * n]), res[-1]


def _exchange_wait(handle, after, tag):
    mode, send_sems, recv_sems, srcs, lands = handle
    n = len(srcs)

    def body(*refs):
        src_refs, land_refs = refs[:n], refs[n:2 * n]
        send_ref, recv_ref = refs[2 * n], refs[2 * n + 1]
        bx, by, bc = _place()
        me = 4 * bx + 2 * by + bc
        for a in range(n):
            for k, (p, idx) in enumerate(_peers(bx, by, bc, with_self=True)):
                out, _, here = _exchange_refs(mode, src_refs[a], land_refs[a], me, idx)
                cp = pltpu.make_async_remote_copy(
                    src_ref=out, dst_ref=here, send_sem=send_ref.at[N_DEV * a + k], recv_sem=recv_ref.at[N_DEV * a + k],
                    device_id=p, device_id_type=MESH)
                cp.wait_send()
                cp.wait_recv()

    res = pl.pallas_call(
        body, name=f"{mode}_wait_{tag}",
        out_shape=(*[pltpu.HBM(s.shape, s.dtype) for s in srcs], *[pltpu.HBM(l.shape, l.dtype) for l in lands]),
        in_specs=[HBM_SPEC] * (2 * n) + [SEM_SPEC, SEM_SPEC, BS(memory_space=pl.ANY)],
        out_specs=tuple([HBM_SPEC] * (2 * n)),
        input_output_aliases={i: i for i in range(2 * n)},
        compiler_params=pltpu.CompilerParams(has_side_effects=DATAFLOW),
    )(*srcs, *lands, send_sems, recv_sems, after)
    return list(res[n:])


def _allreduce_small(vec, behind):
    rows = vec.shape[0]

    def body(x_ref, behind_ref, o_ref, buf_ref, send_sems, recv_sems):
        x, y, c = _place()
        me = 4 * x + 2 * y + c
        buf_ref[me] = x_ref[...]
        peers = _peers(x, y, c)

        def copy(k, slot):
            return pltpu.make_async_remote_copy(
                src_ref=x_ref, dst_ref=buf_ref.at[slot], send_sem=send_sems.at[k], recv_sem=recv_sems.at[k],
                device_id=peers[k][0], device_id_type=MESH)

        sends = [copy(k, me) for k in range(N_DEV - 1)]
        for cp in sends:
            cp.start()
        for k in range(N_DEV - 1):
            copy(k, peers[k][1]).wait_recv()
        for cp in sends:
            cp.wait_send()
        acc = buf_ref[0]
        for s in range(1, N_DEV):
            acc = acc + buf_ref[s]
        o_ref[...] = acc

    vmem = BS(memory_space=pltpu.VMEM)
    return pl.pallas_call(
        body, name="allreduce_small", in_specs=[vmem, BS(memory_space=pl.ANY)], out_specs=vmem,
        out_shape=SDS((rows, 128), F32),
        scratch_shapes=[pltpu.VMEM((N_DEV, rows, 128), F32), pltpu.SemaphoreType.DMA((7,)),
                        pltpu.SemaphoreType.DMA((7,))],
        compiler_params=pltpu.CompilerParams(has_side_effects=True),
    )(vec, behind)


def _ffn_forward(x, hn, fetch, names, tag, next_g):
    gu, act = _ffn_up(hn, fetch(names[0], hn).reshape(2, F, D), tag)
    got = _mm_nn(act[None], fetch(names[1], act)[None], f"down_{tag}", res=x, scale=0.5, next_g=next_g)
    out, hn_next = got if next_g is not None else (got, None)
    return out, hn_next, (x, hn, gu, act)


def _ffn_backward(dxo, dxo_b, saved, norm_g, wut, wd, tag, send):
    x, hn, gu, act = saved
    d_wd = _mm_tn(act[None], dxo_b, f"dwd_{tag}", scale=0.5)[0]
    du = _ffn_dact(dxo_b, wd, gu, send(("down",), [d_wd]), tag)
    d_wut = _mm_tn(du, hn, f"dwu_{tag}")
    token = send(("up",), [d_wut.reshape(2 * F, D)])
    return _mm_nn_norm_bwd(du, wut, x, norm_g + token[0, 0], dxo, tag)


def _mixer_forward(x, hn, fetch, bias, tables, tag, next_g):
    proj = _mm_nt_rows(hn, fetch("win", hn), f"proj_{tag}", 512, IN_W // 2, IN_W, 0, rope=(*tables, 2 * QKV_A // 3))
    qkr = proj
    outs, lses = [], []
    for grp in range(3):
        o, l = _dil_fwd(qkr, proj, grp)
        outs.append(o)
        lses.append(l)
    ya = _combine_fwd(outs, lses)
    yb, lse_b = _na_fwd(proj, bias)
    merged, za, zb = _merge_fwd(ya, yb, proj, fetch("wa", yb), fetch("wb", yb))
    out, hn_next = _mm_nn(merged[None], fetch("wo", merged)[None], f"out_{tag}", res=x, next_g=next_g)
    return out, hn_next, (x, hn, proj, qkr, outs, lses, ya, yb, lse_b, merged, za, zb)


def _mixer_backward(dxo, dxo_b, saved, norm_g, w, bias, tables, tag, send):
    wint, wat, wbt, wo = w
    x, hn, proj, qkr, outs, lses, ya, yb, lse_b, merged, za, zb = saved
    dm = _mm_nt_rows(dxo_b, wo, f"dmerged_{tag}", 512, D, D, 0)
    d_wo = _mm_tn(merged[None], dxo_b, f"dwo_{tag}")[0]
    dza, dzb, dlog = _merge_bwd(dm, za, zb, proj)
    dya = _mm_nn(dza[None], wat[None], f"dya_{tag}")
    dyb = _mm_nn(dzb[None], wbt[None], f"dyb_{tag}")
    d_wat = _mm_tn(dza[None], ya, f"dwa_{tag}")[0]
    d_wbt = _mm_tn(dzb[None], yb, f"dwb_{tag}")[0]
    cb = _combine_bwd(dya, outs, lses)
    dqs, dks, dvs = [], [], []
    for grp in range(3):
        dq, dk, dv = _dil_bwd(qkr, proj, cb[grp], cb[3 + grp], lses[grp], grp)
        dqs.append(dq)
        dks.append(dk)
        dvs.append(dv)
    dqk = _rope_bwd(dqs, dks, *tables)
    dqb, dkb, dvb, dbias_tab = _na_bwd(proj, bias, dyb, yb, lse_b)
    dbias = _na_dbias(dbias_tab)
    dproj = jnp.concatenate(
        [dqk] + [t.astype(BF16) for t in (*dvs, dqb, dkb, dvb)] + [dlog[0], dlog[1]], axis=1)
    d_wint = _mm_tn(dproj[None], hn, f"dwin_{tag}")[0]
    token = send(("win", "wa", "wb", "wo"), [d_wint, d_wat, d_wbt, d_wo])
    dx, dx_b, dg = _mm_nn_norm_bwd(dproj[None], wint[None], x, norm_g + token[0, 0], dxo, f"mix_{tag}")
    dbias = dbias[:, 0, :480].reshape(8, 15, 32)[:, :, :31]
    return dx, dx_b, dg, dbias


def _pack_small(norms, biases, final, loss=None):
    parts = []
    for layer in range(DEPTH):
        parts += [norms[0][layer], norms[1][layer], norms[2][layer],
                  jnp.pad(biases[layer].reshape(-1), (0, BIAS_PAD - 8 * 15 * 31))]
    parts.append(final)
    flat = jnp.concatenate([p.reshape(-1).astype(F32) for p in parts])
    if loss is not None:
        flat = jnp.concatenate([flat, loss.reshape(-1)])
    return jnp.pad(flat, (0, SMALL_ROWS * 128 - flat.shape[0])).reshape(SMALL_ROWS, 128)


def _unpack_small(packed):
    flat = packed.reshape(-1)
    norms, biases = ([], [], []), []
    pos = 0
    for _ in range(DEPTH):
        for k in range(3):
            norms[k].append(flat[pos:pos + D])
            pos += D
        biases.append(flat[pos:pos + 8 * 15 * 31].reshape(8, 15, 31))
        pos += BIAS_PAD
    final = flat[pos:pos + D]
    pos += D
    return [jnp.stack(n) for n in norms], jnp.stack(biases), final, flat[pos]


def kernel(x, ffn1_norm, ffn1_w_up, ffn1_w_down, mix_norm, w_in, na_rel_bias, w_branch_a, w_branch_b, w_out, ffn2_norm, ffn2_w_up, ffn2_w_down, final_norm, loss_target, m_ffn1_norm, m_ffn1_w_up, m_ffn1_w_down, m_mix_norm, m_w_in, m_na_rel_bias, m_w_branch_a, m_w_branch_b, m_w_out, m_ffn2_norm, m_ffn2_w_up, m_ffn2_w_down, m_final_norm, v_ffn1_norm, v_ffn1_w_up, v_ffn1_w_down, v_mix_norm, v_w_in, v_na_rel_bias, v_w_branch_a, v_w_branch_b, v_w_out, v_ffn2_norm, v_ffn2_w_up, v_ffn2_w_down, v_final_norm):
    t = x.shape[0] * x.shape[1]
    xs = x.reshape(t, D)
    tgt = loss_target.reshape(t, D)
    tables = _rope_tables()

    col_sharded = dict(up1=ffn1_w_up, win=w_in, wa=w_branch_a, wb=w_branch_b, up2=ffn2_w_up)
    row_sharded = dict(down1=ffn1_w_down, wo=w_out, down2=ffn2_w_down)
    shard = [{} for _ in range(DEPTH)]
    for layer in range(DEPTH):
        for name, arr in col_sharded.items():
            shard[layer][name] = arr[layer].T.astype(BF16)
        for name, arr in row_sharded.items():
            shard[layer][name] = arr[layer].astype(BF16)

    weights = [{} for _ in range(DEPTH)]
    travel = [(0, ("up1",)), (0, ("down1",)), (0, ("win",)), (0, ("wa", "wb", "wo")), (0, ("up2", "down2")),
              (1, ("up1", "down1")), (1, ("win",)), (1, ("wa", "wb", "wo")), (1, ("up2", "down2"))]
    pending = {}
    after = None
    for i, (layer, names) in enumerate(travel):
        handle, after = _exchange_start("gather", [shard[layer][n] for n in names], after, f"w{i}")
        for n in names:
            pending[layer, n] = (i, handle, names)
    zero = after[0, 0]

    def fetcher(layer):
        def fetch(name, behind):
            if (layer, name) in pending:
                i, handle, names = pending[layer, name]
                for n, got in zip(names, _exchange_wait(handle, behind, f"w{i}")):
                    weights[layer][n] = got
                    del pending[layer, n]
            return weights[layer][name]
        return fetch

    saved = []
    h = xs
    hn = _norm_fwd(xs, ffn1_norm[0] + zero, "first")
    for layer in range(DEPTH):
        bias = _na_bias_table(na_rel_bias[layer])
        fetch = fetcher(layer)
        after_ffn2 = ffn1_norm[layer + 1] if layer + 1 < DEPTH else None
        h, hn, s1 = _ffn_forward(h, hn, fetch, ("up1", "down1"), f"f1l{layer}", mix_norm[layer])
        h, hn, s2 = _mixer_forward(h, hn, fetch, bias, tables, f"l{layer}", ffn2_norm[layer])
        h, hn, s3 = _ffn_forward(h, hn, fetch, ("up2", "down2"), f"f2l{layer}", after_ffn2)
        saved.append((s1, s2, s3, bias))
    loss_part, dh, dh_b, d_final = _loss_head(h, final_norm, tgt)

    d_norms = ([None] * DEPTH, [None] * DEPTH, [None] * DEPTH)
    d_bias = [None] * DEPTH
    sent = {}

    def sender(layer, suffix):
        def send(names, grads):
            tag = f"g{layer}{names[0]}{suffix}"
            handle, token = _exchange_start("scatter", grads, None, tag)
            for i, n in enumerate(names):
                sent[layer, n + suffix] = (handle, i, tag)
            return token
        return send

    for layer in reversed(range(DEPTH)):
        w = weights[layer]
        s1, s2, s3, bias = saved[layer]
        dh, dh_b, d_norms[2][layer] = _ffn_backward(
            dh, dh_b, s3, ffn2_norm[layer], w["up2"].reshape(2, F, D), w["down2"], f"f2l{layer}", sender(layer, "2"))
        dh, dh_b, d_norms[1][layer], d_bias[layer] = _mixer_backward(
            dh, dh_b, s2, mix_norm[layer], (w["win"], w["wa"], w["wb"], w["wo"]), bias, tables, f"l{layer}",
            sender(layer, ""))
        dh, dh_b, d_norms[0][layer] = _ffn_backward(
            dh, dh_b, s1, ffn1_norm[layer], w["up1"].reshape(2, F, D), w["down1"], f"f1l{layer}", sender(layer, "1"))
    grad_x = dh.reshape(x.shape)

    originals = dict(up1=(ffn1_w_up, m_ffn1_w_up, v_ffn1_w_up), down1=(ffn1_w_down, m_ffn1_w_down, v_ffn1_w_down),
                     win=(w_in, m_w_in, v_w_in), wa=(w_branch_a, m_w_branch_a, v_w_branch_a),
                     wb=(w_branch_b, m_w_branch_b, v_w_branch_b), wo=(w_out, m_w_out, v_w_out),
                     up2=(ffn2_w_up, m_ffn2_w_up, v_ffn2_w_up), down2=(ffn2_w_down, m_ffn2_w_down, v_ffn2_w_down))
    big = {}
    behind = dh
    landed = {}

    def received(layer, name):
        handle, i, tag = sent[layer, name]
        if tag not in landed:
            landed[tag] = _exchange_wait(handle, behind, tag)
        return landed[tag][i]

    for name in ("down2", "up2", "win", "wa", "wb", "wo", "down1", "up1"):
        g = _sum_slots(received(0, name), received(1, name), name)
        wv, mv, vv = originals[name]
        if name in col_sharded:
            wv, mv, vv = (jnp.swapaxes(t, 1, 2) for t in (wv, mv, vv))
        big[name] = (g, *_adamw(wv, g, mv, vv, name))
        behind = big[name][1]
        if name in col_sharded:
            big[name] = tuple(jnp.swapaxes(t, 1, 2) for t in big[name])

    small = _allreduce_small(_pack_small(d_norms, d_bias, d_final, loss_part[0, :1]), behind)
    g_norms, g_bias, g_final, loss = _unpack_small(small)
    w_small = _pack_small((ffn1_norm, mix_norm, ffn2_norm), na_rel_bias, final_norm)
    m_small = _pack_small((m_ffn1_norm, m_mix_norm, m_ffn2_norm), m_na_rel_bias, m_final_norm)
    v_small = _pack_small((v_ffn1_norm, v_mix_norm, v_ffn2_norm), v_na_rel_bias, v_final_norm)
    upd = _adamw(w_small[None], small[None], m_small[None], v_small[None], "small")
    small_out = [(g_norms, g_bias, g_final)] + [_unpack_small(u[0])[:3] for u in upd]

    outputs = [loss, grad_x]
    for kind in range(4):
        norms, bias_k, final_k = small_out[kind]
        outputs += [norms[0], big["up1"][kind], big["down1"][kind], norms[1], big["win"][kind], bias_k,
                    big["wa"][kind], big["wb"][kind], big["wo"][kind], norms[2], big["up2"][kind],
                    big["down2"][kind], final_k]
    return tuple(outputs)
```

```python
import numpy as np

import jax
import jax.numpy as jnp
from jax import lax
from jax.experimental import pallas as pl
from jax.experimental.pallas import tpu as pltpu

F32 = jnp.float32
BF16 = jnp.bfloat16
SDS = jax.ShapeDtypeStruct
BS = pl.BlockSpec
MESH = pl.DeviceIdType.MESH

D = 1024
S = 2048
F = 2816
DEPTH = 2
HEAD_DIM = 64
DILATIONS = (1, 4, 16)
HALF = 64
QKV_A = 2304
QKV_B = 1536
IN_W = 5888
N_DEV = 8
NA_ROWS = 32
GRID_W = 64
NA_KR = 8
ROPE_THETA = 10000.0
RMS_EPS = 1e-6
NEG = -1e30
SCALE = HEAD_DIM ** -0.5
ADAM_LR, ADAM_B1, ADAM_B2, ADAM_EPS, ADAM_WD, ADAM_STEP = 0.001, 0.9, 0.999, 1e-08, 0.01, 10
VMEM_LIMIT_V7X = 52 * 1024 * 1024
SMALL_ROWS = 120
BIAS_PAD = 3840
NA_FWD_ROWS = 4
NA_BWD_ROWS = 4
DIL_FWD_TILES = 4
DIL_BWD_TILES = 4


def _cp(*sem):
    return pltpu.CompilerParams(dimension_semantics=sem, vmem_limit_bytes=VMEM_LIMIT_V7X)


def _dot_nn(a, b):
    return jnp.dot(a, b, preferred_element_type=F32)


def _dot_nt(a, b):
    return lax.dot_general(a, b, (((1,), (1,)), ((), ())), preferred_element_type=F32)


def _dot_tn(a, b):
    return lax.dot_general(a, b, (((0,), (0,)), ((), ())), preferred_element_type=F32)


def _ds(start, size, stride):
    return pl.ds(start, size) if stride == 1 else pl.ds(start, size, stride=stride)


def _norm_fwd(x, g, tag):
    t = x.shape[0]
    tm = 512

    def body(x_ref, g_ref, o_ref):
        xv = x_ref[...]
        r = lax.rsqrt(jnp.mean(xv * xv, axis=-1, keepdims=True) + RMS_EPS)
        o_ref[...] = (xv * r * g_ref[...]).astype(BF16)

    return pl.pallas_call(
        body, name=f"norm_fwd_{tag}", grid=(t // tm,),
        in_specs=[BS((tm, D), lambda i: (i, 0)), BS((1, D), lambda i: (0, 0))],
        out_specs=BS((tm, D), lambda i: (i, 0)),
        out_shape=SDS((t, D), BF16), compiler_params=_cp("parallel"),
    )(x, g.reshape(1, D))


def _loss_head(x, g, tgt):
    t = x.shape[0]
    tm = 512

    def body(x_ref, g_ref, t_ref, loss_ref, dx_ref, dxb_ref, dg_ref):
        @pl.when(pl.program_id(0) == 0)
        def _():
            dg_ref[...] = jnp.zeros_like(dg_ref)
            loss_ref[...] = jnp.zeros_like(loss_ref)

        xv = x_ref[...]
        gv = g_ref[...]
        r = lax.rsqrt(jnp.mean(xv * xv, axis=-1, keepdims=True) + RMS_EPS)
        xh = xv * r
        e = xh * gv - t_ref[...]
        loss_ref[...] += 0.5 * jnp.sum(jnp.mean(e * e, axis=-1, keepdims=True), axis=0, keepdims=True)
        dy = e * (1.0 / D)
        u = dy * gv
        dx = r * (u - xh * jnp.mean(xh * u, axis=-1, keepdims=True))
        dx_ref[...] = dx
        dxb_ref[...] = dx.astype(BF16)
        dg_ref[...] += jnp.sum(dy * xh, axis=0, keepdims=True)

    row = BS((tm, D), lambda i: (i, 0))
    vec = BS((1, D), lambda i: (0, 0))
    return pl.pallas_call(
        body, name="loss_head", grid=(t // tm,),
        in_specs=[row, vec, row], out_specs=[BS((1, 128), lambda i: (0, 0)), row, row, vec],
        out_shape=[SDS((1, 128), F32), SDS((t, D), F32), SDS((t, D), BF16), SDS((1, D), F32)],
        compiler_params=_cp("arbitrary"),
    )(x, g.reshape(1, D), tgt)


def _mm_nn(a, w, tag, res=None, scale=1.0, tm=512, tn=None, next_g=None):
    c_n, t, k = a.shape
    n = w.shape[2]
    tn = n if tn is None else tn
    assert next_g is None or tn == n
    n_in = 2 + (res is not None) + (next_g is not None)

    def body(*refs):
        a_ref, w_ref = refs[0], refs[1]
        acc = _dot_nn(a_ref[0].astype(BF16), w_ref[0])
        for c in range(1, c_n):
            acc = acc + _dot_nn(a_ref[c].astype(BF16), w_ref[c])
        if scale != 1.0:
            acc = acc * scale
        if res is not None:
            acc = refs[2][...] + acc
        refs[n_in][...] = acc
        if next_g is not None:
            r = lax.rsqrt(jnp.mean(acc * acc, axis=-1, keepdims=True) + RMS_EPS)
            refs[n_in + 1][...] = (acc * r * refs[n_in - 1][...]).astype(BF16)

    in_specs = [BS((c_n, tm, k), lambda i, j: (0, i, 0)), BS((c_n, k, tn), lambda i, j: (0, 0, j))]
    args = [a, w]
    out_specs = [BS((tm, tn), lambda i, j: (i, j))]
    out_shape = [SDS((t, n), F32)]
    if res is not None:
        in_specs.append(BS((tm, tn), lambda i, j: (i, j)))
        args.append(res)
    if next_g is not None:
        in_specs.append(BS((1, n), lambda i, j: (0, 0)))
        args.append(next_g.reshape(1, n))
        out_specs.append(BS((tm, tn), lambda i, j: (i, j)))
        out_shape.append(SDS((t, n), BF16))
    got = pl.pallas_call(
        body, name=f"mm_nn_{tag}", grid=(t // tm, n // tn), in_specs=in_specs, out_specs=out_specs,
        out_shape=out_shape, compiler_params=_cp("parallel", "parallel"),
    )(*args)
    return got if next_g is not None else got[0]


def _mm_nn_norm_bwd(a, w, x, g, dres, tag, tm=256):
    c_n, t, k = a.shape

    def body(a_ref, w_ref, x_ref, g_ref, dr_ref, dx_ref, dxb_ref, dg_ref):
        @pl.when(pl.program_id(0) == 0)
        def _():
            dg_ref[...] = jnp.zeros_like(dg_ref)

        dh = _dot_nn(a_ref[0], w_ref[0])
        for c in range(1, c_n):
            dh = dh + _dot_nn(a_ref[c], w_ref[c])
        xv = x_ref[...]
        r = lax.rsqrt(jnp.mean(xv * xv, axis=-1, keepdims=True) + RMS_EPS)
        xh = xv * r
        u = dh * g_ref[...]
        dx = dr_ref[...] + r * (u - xh * jnp.mean(xh * u, axis=-1, keepdims=True))
        dx_ref[...] = dx
        dxb_ref[...] = dx.astype(BF16)
        dg_ref[...] += jnp.sum(dh * xh, axis=0, keepdims=True)

    row = BS((tm, D), lambda i: (i, 0))
    vec = BS((1, D), lambda i: (0, 0))
    return pl.pallas_call(
        body, name=f"mm_nn_norm_bwd_{tag}", grid=(t // tm,),
        in_specs=[BS((c_n, tm, k), lambda i: (0, i, 0)), BS((c_n, k, D), lambda i: (0, 0, 0)), row, vec, row],
        out_specs=[row, row, vec], out_shape=[SDS((t, D), F32), SDS((t, D), BF16), SDS((1, D), F32)],
        compiler_params=_cp("arbitrary"),
    )(a, w, x, g.reshape(1, D), dres)


def _mm_nt_rows(a, w, tag, tm, tn, n_total, w_row0, rope=None):
    t, k = a.shape
    assert w_row0 % tn == 0 and n_total % tn == 0
    j0 = w_row0 // tn

    def body(a_ref, w_ref, *rest):
        o_ref = rest[-1]
        o_ref[...] = _dot_nt(a_ref[...].astype(BF16), w_ref[...])
        if rope is not None:
            @pl.when(pl.program_id(0) == 0)
            def _():
                c = rest[0][...]
                sg = rest[1][...]
                first = (lax.broadcasted_iota(jnp.int32, (tm, 128), 1) % HEAD_DIM) < HEAD_DIM // 2
                for col in range(0, rope[2], 128):
                    v = o_ref[:, col:col + 128]
                    o_ref[:, col:col + 128] = v * c + _swap_halves(v, first) * sg

    in_specs = [BS((tm, k), lambda j, i: (i, 0)), BS((tn, k), lambda j, i: (j0 + j, 0))]
    args = [a, w]
    if rope is not None:
        assert rope[2] <= tn
        in_specs += [BS((tm, 128), lambda j, i: (i % (S // tm), 0))] * 2
        args += [rope[0], rope[1]]
    return pl.pallas_call(
        body, name=f"mm_nt_{tag}", grid=(n_total // tn, t // tm), in_specs=in_specs,
        out_specs=BS((tm, tn), lambda j, i: (i, j)), out_shape=SDS((t, n_total), F32),
        compiler_params=_cp("parallel", "parallel"),
    )(*args)


def _mm_tn(a, b, tag, scale=1.0, tmm=256, tk=None):
    c_n, t, m = a.shape
    n = b.shape[1]
    tk = t if tk is None else tk
    nk = t // tk

    def body_one(a_ref, b_ref, o_ref):
        o_ref[...] = (_dot_tn(a_ref[...].astype(BF16), b_ref[...].astype(BF16)) * scale).astype(BF16)

    def body_acc(a_ref, b_ref, o_ref, acc_ref):
        kk = pl.program_id(2)

        @pl.when(kk == 0)
        def _():
            acc_ref[...] = jnp.zeros_like(acc_ref)

        acc_ref[...] += _dot_tn(a_ref[...].astype(BF16), b_ref[...].astype(BF16))

        @pl.when(kk == nk - 1)
        def _():
            o_ref[...] = (acc_ref[...] * scale).astype(BF16)

    return pl.pallas_call(
        body_one if nk == 1 else body_acc, name=f"mm_tn_{tag}", grid=(c_n, m // tmm, nk),
        in_specs=[BS((None, tk, tmm), lambda c, mi, kk: (c, kk, mi)), BS((tk, n), lambda c, mi, kk: (kk, 0))],
        out_specs=BS((None, tmm, n), lambda c, mi, kk: (c, mi, 0)),
        out_shape=SDS((c_n, m, n), BF16), scratch_shapes=[] if nk == 1 else [pltpu.VMEM((tmm, n), F32)],
        compiler_params=_cp("parallel", "parallel", "arbitrary"),
    )(a, b)


def _ffn_up(hn, wut, tag):
    t = hn.shape[0]
    tm, tn = 512, 1408

    def body(h_ref, w_ref, gu_ref, act_ref):
        h = h_ref[...]
        g = _dot_nt(h, w_ref[0])
        u = _dot_nt(h, w_ref[1])
        sg = jax.nn.sigmoid(g)
        silu = g * sg
        gu_ref[0] = (u * (sg + silu * (1.0 - sg))).astype(BF16)
        gu_ref[1] = silu.astype(BF16)
        act_ref[...] = (silu * u).astype(BF16)

    return pl.pallas_call(
        body, name=f"ffn_up_{tag}", grid=(F // tn, t // tm),
        in_specs=[BS((tm, D), lambda j, i: (i, 0)), BS((2, tn, D), lambda j, i: (0, j, 0))],
        out_specs=[BS((2, tm, tn), lambda j, i: (0, i, j)), BS((tm, tn), lambda j, i: (i, j))],
        out_shape=[SDS((2, t, F), BF16), SDS((t, F), BF16)],
        compiler_params=_cp("parallel", "parallel"),
    )(hn, wut)


def _ffn_dact(dxo, wd, gu, tie, tag):
    t = dxo.shape[0]
    tm, tn = 512, 1408

    def body(d_ref, w_ref, gu_ref, tie_ref, o_ref):
        dact = _dot_nt(d_ref[...] * 0.5, w_ref[...])
        o_ref[0] = (dact * gu_ref[0].astype(F32)).astype(BF16)
        o_ref[1] = (dact * gu_ref[1].astype(F32)).astype(BF16)

    return pl.pallas_call(
        body, name=f"ffn_dact_{tag}", grid=(F // tn, t // tm),
        in_specs=[BS((tm, D), lambda j, i: (i, 0)), BS((tn, D), lambda j, i: (j, 0)),
                  BS((2, tm, tn), lambda j, i: (0, i, j)), BS((8, 128), lambda j, i: (0, 0))],
        out_specs=BS((2, tm, tn), lambda j, i: (0, i, j)),
        out_shape=SDS((2, t, F), BF16), compiler_params=_cp("parallel", "parallel"),
    )(dxo, wd, gu, tie)


def _rope_tables():
    half = HEAD_DIM // 2
    inv_freq = ROPE_THETA ** (-jnp.arange(half, dtype=F32) / half)
    ang = jnp.arange(S).astype(F32)[:, None] * inv_freq[None, :]
    cos, sin = jnp.cos(ang), jnp.sin(ang)
    return jnp.concatenate([cos, cos, cos, cos], axis=1), jnp.concatenate([-sin, sin, -sin, sin], axis=1)


def _swap_halves(t, first_half):
    return jnp.where(first_half, pltpu.roll(t, 96, 1), pltpu.roll(t, 32, 1))


def _rope_bwd(dqs, dks, cos_t, sin_t):
    t = dqs[0].shape[0]
    tm = 512

    def body(*refs):
        c = refs[6][...]
        sg = refs[7][...]
        o_ref = refs[8]
        first = (lax.broadcasted_iota(jnp.int32, (tm, 128), 1) % HEAD_DIM) < HEAD_DIM // 2
        for a in range(6):
            for hp in range(2):
                v = refs[a][:, 128 * hp:128 * (hp + 1)]
                col = 128 * (2 * a + hp)
                o_ref[:, col:col + 128] = (v * c + _swap_halves(v * sg, first)).astype(BF16)

    blk = BS((tm, 256), lambda i: (i, 0))
    tab = BS((tm, 128), lambda i: (i % (S // tm), 0))
    return pl.pallas_call(
        body, name="rope_bwd", grid=(t // tm,), in_specs=[blk] * 6 + [tab, tab],
        out_specs=BS((tm, 1536), lambda i: (i, 0)), out_shape=SDS((t, 1536), BF16),
        compiler_params=_cp("parallel"),
    )(*dqs, *dks, cos_t, sin_t)


def _head_masks():
    lane = lax.broadcasted_iota(jnp.int32, (1, 128), 1)
    m0 = (lane < HEAD_DIM).astype(F32)
    return m0, 1.0 - m0


def _dil_geometry(d):
    sub = S // d
    q_rows = 128
    k_rows = min(256, sub)
    return sub, q_rows, sub // q_rows, k_rows


def _dil_tile(idx, d):
    sub, q_rows, nb, k_rows = _dil_geometry(d)
    r = idx // nb
    n = idx % nb
    k_sub = jnp.clip(q_rows * n - HALF, 0, sub - k_rows)
    if d == 1:
        q_start = pl.multiple_of(q_rows * n, q_rows)
        k_start = pl.multiple_of(k_sub, HALF)
    else:
        q_start = q_rows * n * d + r
        k_start = k_sub * d + r
    ii = lax.broadcasted_iota(jnp.int32, (q_rows, k_rows), 0)
    jj = lax.broadcasted_iota(jnp.int32, (q_rows, k_rows), 1)
    valid = jnp.abs(jj - ii + (k_sub - q_rows * n)) <= HALF
    return q_start, k_start, valid


def _dil_specs(grp):
    qs = BS((S, 128), lambda b, hp: (b, 2 * grp + hp))
    ks = BS((S, 128), lambda b, hp: (b, 6 + 2 * grp + hp))
    vs = BS((S, 128), lambda b, hp: (b, 12 + 2 * grp + hp))
    own = BS((S, 128), lambda b, hp: (b, hp))
    return qs, ks, vs, own


def _dil_fwd(qkr, proj, grp):
    t = qkr.shape[0]
    d = DILATIONS[grp]
    _, q_rows, nb, k_rows = _dil_geometry(d)

    def body(q_ref, k_ref, v_ref, o_ref, l_ref):
        masks = _head_masks()

        def step(i0, carry):
            geo = [_dil_tile(i0 * DIL_FWD_TILES + j, d) for j in range(DIL_FWD_TILES)]
            tiles = [(j, h) for j in range(DIL_FWD_TILES) for h in range(2)]
            qs = [q_ref[_ds(g[0], q_rows, d), :] for g in geo]
            kbs = [k_ref[_ds(g[1], k_rows, d), :].astype(BF16) for g in geo]
            ss = [jnp.where(geo[j][2], _dot_nt((qs[j] * masks[h]).astype(BF16), kbs[j]) * SCALE, NEG) for j, h in tiles]
            mxs = [jnp.max(s, axis=1, keepdims=True) for s in ss]
            ps = [jnp.exp(s - mx) for s, mx in zip(ss, mxs)]
            dens = [jnp.sum(p, axis=1, keepdims=True) for p in ps]
            vs = [v_ref[_ds(g[1], k_rows, d), :] for g in geo]
            outs = [_dot_nn(p.astype(BF16), (vs[j] * masks[h]).astype(BF16)) / den
                    for p, den, (j, h) in zip(ps, dens, tiles)]
            for j, g in enumerate(geo):
                o_ref[_ds(g[0], q_rows, d), :] = outs[2 * j] + outs[2 * j + 1]
                l_ref[_ds(g[0], q_rows, d), :] = (
                    (mxs[2 * j] + jnp.log(dens[2 * j])) * masks[0] + (mxs[2 * j + 1] + jnp.log(dens[2 * j + 1])) * masks[1])
            return carry

        lax.fori_loop(0, d * nb // DIL_FWD_TILES, step, 0)

    qs, ks, vs, own = _dil_specs(grp)
    return pl.pallas_call(
        body, name=f"dil_fwd_{grp}", grid=(t // S, 2), in_specs=[qs, ks, vs], out_specs=[own, own],
        out_shape=[SDS((t, 256), F32), SDS((t, 256), F32)], compiler_params=_cp("parallel", "parallel"),
    )(qkr, qkr, proj)


def _dil_bwd(qkr, proj, do, dlp, lse, grp):
    t = qkr.shape[0]
    d = DILATIONS[grp]
    _, q_rows, nb, k_rows = _dil_geometry(d)

    def body(q_ref, k_ref, v_ref, do_ref, dl_ref, l_ref, dq_ref, dk_ref, dv_ref):
        masks = _head_masks()
        dk_ref[...] = jnp.zeros_like(dk_ref)
        dv_ref[...] = jnp.zeros_like(dv_ref)

        def step(i0, carry):
            geo = [_dil_tile(i0 * DIL_BWD_TILES + j, d) for j in range(DIL_BWD_TILES)]
            tiles = [(j, h) for j in range(DIL_BWD_TILES) for h in range(2)]
            q_ds = [_ds(g[0], q_rows, d) for g in geo]
            k_ds = [_ds(g[1], k_rows, d) for g in geo]
            qs = [q_ref[r, :] for r in q_ds]
            ks = [k_ref[r, :] for r in k_ds]
            kbs = [k.astype(BF16) for k in ks]
            vbs = [v_ref[r, :].astype(BF16) for r in k_ds]
            dos = [do_ref[r, :] for r in q_ds]
            dls = [dl_ref[r, :] for r in q_ds]
            lss = [l_ref[r, :] for r in q_ds]
            qhs = [(qs[j] * masks[h]).astype(BF16) for j, h in tiles]
            dohs = [(dos[j] * masks[h]).astype(BF16) for j, h in tiles]
            ss = [jnp.where(geo[j][2], _dot_nt(qh, kbs[j]) * SCALE, NEG) for qh, (j, h) in zip(qhs, tiles)]
            ps = [jnp.exp(s - lss[j][:, HEAD_DIM * h:HEAD_DIM * h + 1]) for s, (j, h) in zip(ss, tiles)]
            dps = [_dot_nt(doh, vbs[j]) for doh, (j, h) in zip(dohs, tiles)]
            dss = [(p * (dp - dls[j][:, HEAD_DIM * h:HEAD_DIM * h + 1])).astype(BF16)
                   for p, dp, (j, h) in zip(ps, dps, tiles)]
            dqs = [_dot_nn(ds, (ks[j] * masks[h]).astype(BF16)) for ds, (j, h) in zip(dss, tiles)]
            dkws = [_dot_tn(ds, qh) for ds, qh in zip(dss, qhs)]
            dvws = [_dot_tn(p.astype(BF16), doh) for p, doh in zip(ps, dohs)]
            for j in range(DIL_BWD_TILES):
                dq_ref[q_ds[j], :] = (dqs[2 * j] + dqs[2 * j + 1]) * SCALE
                dk_ref[k_ds[j], :] += (dkws[2 * j] + dkws[2 * j + 1]) * SCALE
                dv_ref[k_ds[j], :] += dvws[2 * j] + dvws[2 * j + 1]
            return carry

        lax.fori_loop(0, d * nb // DIL_BWD_TILES, step, 0)

    qs, ks, vs, own = _dil_specs(grp)
    return pl.pallas_call(
        body, name=f"dil_bwd_{grp}", grid=(t // S, 2), in_specs=[qs, ks, vs, own, own, own],
        out_specs=[own, own, own], out_shape=[SDS((t, 256), F32)] * 3,
        compiler_params=_cp("parallel", "parallel"),
    )(qkr, qkr, proj, do, dlp, lse)


def _mix_weights(l0, l1, l2):
    mx = jnp.maximum(jnp.maximum(l0, l1), l2)
    e0, e1, e2 = jnp.exp(l0 - mx), jnp.exp(l1 - mx), jnp.exp(l2 - mx)
    den = e0 + e1 + e2
    return e0 / den, e1 / den, e2 / den


def _combine_fwd(outs, lses):
    t = outs[0].shape[0]
    tm = 512

    def body(o0, o1, o2, l0, l1, l2, y_ref):
        w0, w1, w2 = _mix_weights(l0[...], l1[...], l2[...])
        y_ref[...] = w0 * o0[...] + w1 * o1[...] + w2 * o2[...]

    blk = BS((tm, 256), lambda i: (i, 0))
    return pl.pallas_call(
        body, name="combine_fwd", grid=(t // tm,), in_specs=[blk] * 6, out_specs=blk,
        out_shape=SDS((t, 256), F32), compiler_params=_cp("parallel"),
    )(*outs, *lses)


def _head_sum(x):
    a = lax.broadcasted_iota(jnp.int32, (256, 256), 0) // HEAD_DIM
    b = lax.broadcasted_iota(jnp.int32, (256, 256), 1) // HEAD_DIM
    ones = (a == b).astype(BF16)
    hi = x.astype(BF16)
    lo = (x - hi.astype(F32)).astype(BF16)
    return _dot_nn(hi, ones) + _dot_nn(lo, ones)


def _combine_bwd(dya, outs, lses):
    t = dya.shape[0]
    tm = 512

    def body(dy_ref, o0, o1, o2, l0, l1, l2, d0, d1, d2, e0, e1, e2):
        ws = _mix_weights(l0[...], l1[...], l2[...])
        dy = dy_ref[...]
        ya = ws[0] * o0[...] + ws[1] * o1[...] + ws[2] * o2[...]
        hs = _head_sum(dy * ya)
        for w, d_ref, e_ref in zip(ws, (d0, d1, d2), (e0, e1, e2)):
            d_ref[...] = w * dy
            e_ref[...] = w * hs

    blk = BS((tm, 256), lambda i: (i, 0))
    return pl.pallas_call(
        body, name="combine_bwd", grid=(t // tm,), in_specs=[blk] * 7, out_specs=[blk] * 6,
        out_shape=[SDS((t, 256), F32)] * 6, compiler_params=_cp("parallel"),
    )(dya, *outs, *lses)


def _na_bias_table(rel_bias):
    qc = np.arange(GRID_W)[:, None]
    kc = np.arange(GRID_W)[None, :]
    win_lo = np.clip(qc - 8, 0, GRID_W - 16)
    col_valid = (kc >= win_lo) & (kc < win_lo + 16)
    col_idx = np.clip(kc - qc + 15, 0, 30)
    row_idx = np.arange(NA_KR)[:, None] + np.arange(NA_KR)[None, :]
    rows = (row_idx[..., None] == np.arange(2 * NA_KR - 1)).astype(np.float32)
    cols = (col_idx[..., None] == np.arange(31)).astype(np.float32)
    b = jnp.einsum("hrd,ckr,qjd->ckjhq", rel_bias.astype(F32), rows, cols, precision=lax.Precision.HIGHEST)
    b = jnp.where(col_valid.T[None, None, :, None, :], b, NEG)
    return b.reshape(NA_KR, NA_KR * GRID_W, 4, 128).transpose(2, 0, 1, 3)


def _na_row(i):
    lo = jnp.clip(i - NA_KR // 2, 0, NA_ROWS - NA_KR)
    return pl.multiple_of(GRID_W * i, GRID_W), pl.multiple_of(GRID_W * lo, GRID_W), lo - i + NA_KR - 1


def _both_heads(x, masks):
    return jnp.concatenate([x * masks[0], x * masks[1]], axis=0)


def _own_heads(r, masks):
    half = r.shape[0] // 2
    return r[:half] * masks[0] + r[half:] * masks[1]


def _na_fwd(proj, bias):
    t = proj.shape[0]
    kw = NA_KR * GRID_W

    def body(q_ref, k_ref, v_ref, b_ref, o_ref, l_ref):
        masks = _head_masks()

        def step(i0, carry):
            idx = [i0 * NA_FWD_ROWS + j for j in range(NA_FWD_ROWS)]
            rows = [_na_row(i) for i in idx]
            qbs = [_both_heads(q_ref[pl.ds(r[0], GRID_W), :], masks).astype(BF16) for r in rows]
            kbs = [k_ref[pl.ds(r[1], kw), :].astype(BF16) for r in rows]
            ss = [_dot_nt(kb, qb) * SCALE + b_ref[r[2]] for kb, qb, r in zip(kbs, qbs, rows)]
            mxs = [jnp.max(s, axis=0, keepdims=True) for s in ss]
            ps = [jnp.exp(s - mx) for s, mx in zip(ss, mxs)]
            dens = [jnp.sum(p, axis=0, keepdims=True) for p in ps]
            pbs = [(p / den).astype(BF16) for p, den in zip(ps, dens)]
            vbs = [v_ref[pl.ds(r[1], kw), :].astype(BF16) for r in rows]
            outs = [_own_heads(_dot_tn(pb, vb), masks) for pb, vb in zip(pbs, vbs)]
            for j, r in enumerate(rows):
                o_ref[pl.ds(r[0], GRID_W), :] = outs[j]
                l_ref[pl.ds(idx[j], 1), :] = mxs[j] + jnp.log(dens[j])
            return carry

        lax.fori_loop(0, NA_ROWS // NA_FWD_ROWS, step, 0)

    c0 = QKV_A // 128
    return pl.pallas_call(
        body, name="na_fwd", grid=(t // S, 4),
        in_specs=[BS((S, 128), lambda b, hp: (b, c0 + hp)), BS((S, 128), lambda b, hp: (b, c0 + 4 + hp)),
                  BS((S, 128), lambda b, hp: (b, c0 + 8 + hp)),
                  BS((None, NA_KR, kw, 128), lambda b, hp: (hp, 0, 0, 0))],
        out_specs=[BS((S, 128), lambda b, hp: (b, hp)), BS((None, None, NA_ROWS, 128), lambda b, hp: (b, hp, 0, 0))],
        out_shape=[SDS((t, 512), F32), SDS((t // S, 4, NA_ROWS, 128), F32)],
        compiler_params=_cp("parallel", "parallel"),
    )(proj, proj, proj, bias)


def _na_bwd(proj, bias, dyb, yb, lse):
    t = proj.shape[0]
    kw = NA_KR * GRID_W

    def body(q_ref, k_ref, v_ref, b_ref, do_ref, o_ref, l_ref, dq_ref, dk_ref, dv_ref, db_ref):
        masks = _head_masks()
        ones = jnp.ones((8, 128), BF16)

        @pl.when(pl.program_id(1) == 0)
        def _():
            db_ref[...] = jnp.zeros_like(db_ref)

        dk_ref[...] = jnp.zeros_like(dk_ref)
        dv_ref[...] = jnp.zeros_like(dv_ref)

        def row_sums(x):
            hi = x.astype(BF16)
            lo = (x - hi.astype(F32)).astype(BF16)
            return (_dot_nt(ones, hi) + _dot_nt(ones, lo))[0:1]

        def step(i0, carry):
            idx = [i0 * NA_BWD_ROWS + j for j in range(NA_BWD_ROWS)]
            rows = [_na_row(i) for i in idx]
            q_ds = [pl.ds(r[0], GRID_W) for r in rows]
            k_ds = [pl.ds(r[1], kw) for r in rows]
            qbs = [_both_heads(q_ref[r, :], masks).astype(BF16) for r in q_ds]
            kbs = [k_ref[r, :].astype(BF16) for r in k_ds]
            vbs = [v_ref[r, :].astype(BF16) for r in k_ds]
            dos = [do_ref[r, :] for r in q_ds]
            dobs = [_both_heads(do, masks).astype(BF16) for do in dos]
            deltas = [row_sums(_both_heads(do * o_ref[r, :], masks)) for do, r in zip(dos, q_ds)]
            ss = [_dot_nt(kb, qb) * SCALE + b_ref[r[2]] for kb, qb, r in zip(kbs, qbs, rows)]
            ps = [jnp.exp(s - l_ref[pl.ds(i, 1), :]) for s, i in zip(ss, idx)]
            dps = [_dot_nt(vb, dob) for vb, dob in zip(vbs, dobs)]
            dss = [p * (dp - delta) for p, dp, delta in zip(ps, dps, deltas)]
            for ds, r in zip(dss, rows):
                db_ref[r[2]] += ds
            dsbs = [ds.astype(BF16) for ds in dss]
            dks = [_dot_nn(dsb, qb) for dsb, qb in zip(dsbs, qbs)]
            dvs = [_dot_nn(p.astype(BF16), dob) for p, dob in zip(ps, dobs)]
            dqs = [_own_heads(_dot_tn(dsb, kb), masks) for dsb, kb in zip(dsbs, kbs)]
            for j in range(NA_BWD_ROWS):
                dq_ref[q_ds[j], :] = dqs[j] * SCALE
                dk_ref[k_ds[j], :] += dks[j] * SCALE
                dv_ref[k_ds[j], :] += dvs[j]
            return carry

        lax.fori_loop(0, NA_ROWS // NA_BWD_ROWS, step, 0)

    c0 = QKV_A // 128
    own = BS((S, 128), lambda hp, b: (b, hp))
    tab = BS((None, NA_KR, kw, 128), lambda hp, b: (hp, 0, 0, 0))
    return pl.pallas_call(
        body, name="na_bwd", grid=(4, t // S),
        in_specs=[BS((S, 128), lambda hp, b: (b, c0 + hp)), BS((S, 128), lambda hp, b: (b, c0 + 4 + hp)),
                  BS((S, 128), lambda hp, b: (b, c0 + 8 + hp)), tab, own, own,
                  BS((None, None, NA_ROWS, 128), lambda hp, b: (b, hp, 0, 0))],
        out_specs=[own, own, own, tab],
        out_shape=[SDS((t, 512), F32)] * 3 + [SDS((4, NA_KR, kw, 128), F32)],
        compiler_params=_cp("parallel", "arbitrary"),
    )(proj, proj, proj, bias, dyb, yb, lse)


def _na_dbias_lane_map():
    kw = NA_KR * GRID_W
    lane = np.arange(kw)
    blk, m = lane // GRID_W, lane % GRID_W
    target = np.full(kw, -1)
    target[m < 16] = (blk * 32 + 15 + m)[m < 16]
    target[m >= 49] = (((blk + 1) % NA_KR) * 32 + m - 49)[m >= 49]
    return jnp.asarray(target[:, None] == np.arange(kw)[None, :], BF16)


def _na_dbias(db):
    kw = NA_KR * GRID_W

    def body(x_ref, map_ref, o_ref, z_ref):
        for cls in range(NA_KR):
            xt = x_ref[cls].T
            for h in range(2):
                xv = xt[GRID_W * h:GRID_W * (h + 1)]
                y = xv[0:8]
                for g in range(1, GRID_W // 8):
                    y = y + pltpu.roll(xv[8 * g:8 * g + 8], kw - 8 * g, 1)
                d = y[0:1]
                for s in range(1, 8):
                    d = d + pltpu.roll(y[s:s + 1], kw - s, 1)
                z_ref[h, cls:cls + 1, :] = d
        for h in range(2):
            z = z_ref[h]
            hi = z.astype(BF16)
            lo = (z - hi.astype(F32)).astype(BF16)
            e = _dot_nn(hi, map_ref[...]) + _dot_nn(lo, map_ref[...])
            out = e[0:1]
            for cls in range(1, NA_KR):
                out = out + pltpu.roll(e[cls:cls + 1], 32 * cls, 1)
            o_ref[h] = jnp.broadcast_to(out, (8, kw))

    return pl.pallas_call(
        body, name="na_dbias", grid=(4,),
        in_specs=[BS((None, NA_KR, kw, 128), lambda hp: (hp, 0, 0, 0)), BS((kw, kw), lambda hp: (0, 0))],
        out_specs=BS((2, 8, kw), lambda hp: (hp, 0, 0)), out_shape=SDS((8, 8, kw), F32),
        scratch_shapes=[pltpu.VMEM((2, 8, kw), F32)], compiler_params=_cp("parallel"),
    )(db, _na_dbias_lane_map())


def _merge_fwd(ya, yb, proj, wat, wbt):
    t = ya.shape[0]
    tm, tn = 512, 256
    ca = (QKV_A + QKV_B) // tn
    cb = ca + D // tn

    def body(ya_ref, yb_ref, la_ref, lb_ref, wa_ref, wb_ref, m_ref, za_ref, zb_ref):
        za = _dot_nt(ya_ref[...].astype(BF16), wa_ref[...])
        zb = _dot_nt(yb_ref[...].astype(BF16), wb_ref[...])
        m_ref[...] = (jax.nn.sigmoid(la_ref[...]) * za + jax.nn.sigmoid(lb_ref[...]) * zb).astype(BF16)
        za_ref[...] = za.astype(BF16)
        zb_ref[...] = zb.astype(BF16)

    out = BS((tm, tn), lambda i, j: (i, j))
    return pl.pallas_call(
        body, name="merge_fwd", grid=(t // tm, D // tn),
        in_specs=[BS((tm, 256), lambda i, j: (i, 0)), BS((tm, 512), lambda i, j: (i, 0)),
                  BS((tm, tn), lambda i, j: (i, ca + j)), BS((tm, tn), lambda i, j: (i, cb + j)),
                  BS((tn, 256), lambda i, j: (j, 0)), BS((tn, 512), lambda i, j: (j, 0))],
        out_specs=[out, out, out], out_shape=[SDS((t, D), BF16)] * 3,
        compiler_params=_cp("parallel", "parallel"),
    )(ya, yb, proj, proj, wat, wbt)


def _merge_bwd(dm, za, zb, proj):
    t = dm.shape[0]
    tm, tn = 512, 256
    ca = (QKV_A + QKV_B) // tn
    cb = ca + D // tn

    def body(dm_ref, za_ref, zb_ref, la_ref, lb_ref, dza_ref, dzb_ref, dl_ref):
        dmv = dm_ref[...]
        ga = jax.nn.sigmoid(la_ref[...])
        gb = jax.nn.sigmoid(lb_ref[...])
        dza_ref[...] = (dmv * ga).astype(BF16)
        dzb_ref[...] = (dmv * gb).astype(BF16)
        dl_ref[0] = (dmv * za_ref[...].astype(F32) * ga * (1.0 - ga)).astype(BF16)
        dl_ref[1] = (dmv * zb_ref[...].astype(F32) * gb * (1.0 - gb)).astype(BF16)

    blk = BS((tm, tn), lambda i, j: (i, j))
    return pl.pallas_call(
        body, name="merge_bwd", grid=(t // tm, D // tn),
        in_specs=[blk, blk, blk, BS((tm, tn), lambda i, j: (i, ca + j)), BS((tm, tn), lambda i, j: (i, cb + j))],
        out_specs=[blk, blk, BS((2, tm, tn), lambda i, j: (0, i, j))],
        out_shape=[SDS((t, D), BF16), SDS((t, D), BF16), SDS((2, t, D), BF16)],
        compiler_params=_cp("parallel", "parallel"),
    )(dm, za, zb, proj, proj)


def _sum_slots(recv0, recv1, tag):
    _, r, c = recv0.shape
    tr = r if r * c <= 512 * 1024 else r // 2

    def body(a_ref, b_ref, o_ref):
        for layer, ref in enumerate((a_ref, b_ref)):
            acc = ref[0].astype(F32)
            for s in range(1, N_DEV):
                acc = acc + ref[s].astype(F32)
            o_ref[layer] = acc

    blk = BS((N_DEV, tr, c), lambda i: (0, i, 0))
    return pl.pallas_call(
        body, name=f"sum_slots_{tag}", grid=(r // tr,), in_specs=[blk, blk],
        out_specs=BS((2, tr, c), lambda i: (0, i, 0)), out_shape=SDS((2, r, c), F32),
        compiler_params=_cp("parallel"),
    )(recv0, recv1)


def _adamw(w, g, m, v, tag):
    layers, r, c = w.shape
    tr = next(r // k for k in (1, 2, 4, 8) if r // k <= 384 and r % (8 * k) == 0)

    def body(w_ref, g_ref, m_ref, v_ref, d_ref, mo_ref, vo_ref):
        gv = g_ref[...]
        mn = ADAM_B1 * m_ref[...] + (1.0 - ADAM_B1) * gv
        vn = ADAM_B2 * v_ref[...] + (1.0 - ADAM_B2) * (gv * gv)
        m_hat = mn / (1.0 - ADAM_B1 ** ADAM_STEP)
        v_hat = vn / (1.0 - ADAM_B2 ** ADAM_STEP)
        d_ref[...] = -ADAM_LR * (m_hat / (jnp.sqrt(v_hat) + ADAM_EPS) + ADAM_WD * w_ref[...])
        mo_ref[...] = mn
        vo_ref[...] = vn

    blk = BS((None, tr, c), lambda l, i: (l, i, 0))
    return pl.pallas_call(
        body, name=f"adamw_{tag}", grid=(layers, r // tr), in_specs=[blk] * 4, out_specs=[blk] * 3,
        out_shape=[SDS((layers, r, c), F32)] * 3, compiler_params=_cp("parallel", "parallel"),
    )(w, g, m, v)


def _place():
    return lax.axis_index("x"), lax.axis_index("y"), lax.axis_index("c")


def _flip(coord, bit):
    return 1 - coord if bit else coord


def _allgather(shards, tag):
    n_arr = len(shards)
    hbm = BS(memory_space=pl.ANY)

    def body(*refs):
        ins, outs = refs[:n_arr], refs[n_arr:2 * n_arr]
        send_sems, recv_sems, local_sems = refs[2 * n_arr:]
        x, y, c = _place()
        me, sibling = (x, y, c), (x, y, 1 - c)
        chips = [(1 - x, y), (x, 1 - y), (1 - x, 1 - y)]

        def rows(a, p):
            r = shards[a].shape[0]
            return outs[a].at[pl.ds((4 * p[0] + 2 * p[1] + p[2]) * r, r), :]

        def copy(a, k, block, to, src=None):
            return pltpu.make_async_remote_copy(
                src_ref=rows(a, block) if src is None else src, dst_ref=rows(a, block),
                send_sem=send_sems.at[a, k], recv_sem=recv_sems.at[a, k], device_id=to, device_id_type=MESH)

        mine = [pltpu.make_async_copy(ins[a], rows(a, me), local_sems.at[a]) for a in range(n_arr)]
        for cp in mine:
            cp.start()
        first = []
        for a in range(n_arr):
            first.append(copy(a, 0, me, sibling, src=ins[a]))
            first += [copy(a, 1 + j, me, (*chip, c), src=ins[a]) for j, chip in enumerate(chips)]
        for cp in first:
            cp.start()
        passed = []
        for a in range(n_arr):
            for j, chip in enumerate(chips):
                copy(a, 1 + j, (*chip, c), me).wait_recv()
                passed.append(copy(a, 4 + j, (*chip, c), sibling))
                passed[-1].start()
        for a in range(n_arr):
            copy(a, 0, sibling, me).wait_recv()
            for j, chip in enumerate(chips):
                copy(a, 4 + j, (*chip, 1 - c), me).wait_recv()
        for cp in first + passed:
            cp.wait_send()
        for cp in mine:
            cp.wait()

    return pl.pallas_call(
        body, name=f"allgather_{tag}", in_specs=[hbm] * n_arr, out_specs=[hbm] * n_arr,
        out_shape=[SDS((N_DEV * s.shape[0], s.shape[1]), s.dtype) for s in shards],
        scratch_shapes=[pltpu.SemaphoreType.DMA((n_arr, 7)), pltpu.SemaphoreType.DMA((n_arr, 7)),
                        pltpu.SemaphoreType.DMA((n_arr,))],
        compiler_params=pltpu.CompilerParams(has_side_effects=True),
    )(*shards)


def _peers(x, y, c, with_self=False):
    peers = []
    for mask in list(range(1, N_DEV)) + ([0] if with_self else []):
        p = (_flip(x, mask & 4), _flip(y, mask & 2), _flip(c, mask & 1))
        peers.append((p, 4 * p[0] + 2 * p[1] + p[2]))
    return peers


def _exchange_refs(mode, src, land, me, peer):
    if mode == "gather":
        r = src.shape[0]
        return src, land.at[pl.ds(me * r, r), :], land.at[pl.ds(peer * r, r), :]
    r = land.shape[1]
    return src.at[pl.ds(peer * r, r), :], land.at[me], land.at[peer]


HBM_SPEC = BS(memory_space=pltpu.HBM)
SEM_SPEC = BS(memory_space=pltpu.SEMAPHORE)
DATAFLOW = pltpu.SideEffectType.DATAFLOW_SIDE_EFFECTING


def _fresh(shape, dtype, tag):
    def body(o_ref):
        del o_ref

    return pl.pallas_call(body, name=f"fresh_{tag}", out_specs=BS(memory_space=pl.ANY), out_shape=SDS(shape, dtype))()


def _exchange_start(mode, srcs, after, tag):
    n = len(srcs)
    if mode == "gather":
        lands = [_fresh((N_DEV * s.shape[0], s.shape[1]), s.dtype, f"{tag}_{a}") for a, s in enumerate(srcs)]
    else:
        lands = [_fresh((N_DEV, s.shape[0] // N_DEV, s.shape[1]), s.dtype, f"{tag}_{a}") for a, s in enumerate(srcs)]
    behind = [] if after is None else [after]

    def body(*refs):
        src_refs, land_refs = refs[:n], refs[n:2 * n]
        send_sems, recv_sems = refs[2 * n + len(behind)], refs[2 * n + len(behind) + 1]
        token = refs[-1]
        bx, by, bc = _place()
        me = 4 * bx + 2 * by + bc
        for a in range(n):
            for k, (p, idx) in enumerate(_peers(bx, by, bc, with_self=True)):
                out, there, _ = _exchange_refs(mode, src_refs[a], land_refs[a], me, idx)
                pltpu.make_async_remote_copy(
                    src_ref=out, dst_ref=there, send_sem=send_sems.at[N_DEV * a + k],
                    recv_sem=recv_sems.at[N_DEV * a + k], device_id=p, device_id_type=MESH).start()
        token[...] = jnp.zeros_like(token)

    res = pl.pallas_call(
        body, name=f"{mode}_start_{tag}",
        out_shape=(pltpu.SemaphoreType.DMA((N_DEV * n,)), pltpu.SemaphoreType.DMA((N_DEV * n,)),
                   *[pltpu.HBM(s.shape, s.dtype) for s in srcs], *[pltpu.HBM(l.shape, l.dtype) for l in lands],
                   SDS((8, 128), F32)),
        in_specs=[HBM_SPEC] * (2 * n) + [BS(memory_space=pl.ANY)] * len(behind),
        out_specs=(SEM_SPEC, SEM_SPEC, *[HBM_SPEC] * (2 * n), BS(memory_space=pltpu.VMEM)),
        input_output_aliases={i: 2 + i for i in range(2 * n)},
        compiler_params=pltpu.CompilerParams(has_side_effects=DATAFLOW),
    )(*[pltpu.with_memory_space_constraint(s, pltpu.HBM) for s in srcs],
      *[pltpu.with_memory_space_constraint(l, pltpu.HBM) for l in lands], *behind)
    return (mode, res[0], res[1], res[2:2 + n], res[2 + n:2 + 2 * n]), res[-1]


def _exchange_wait(handle, after, tag):
    mode, send_sems, recv_sems, srcs, lands = handle
    n = len(srcs)
    afters = list(after) if isinstance(after, (tuple, list)) else [after]

    def body(*refs):
        src_refs, land_refs = refs[:n], refs[n:2 * n]
        send_ref, recv_ref = refs[2 * n], refs[2 * n + 1]
        bx, by, bc = _place()
        me = 4 * bx + 2 * by + bc
        for a in range(n):
            for k, (p, idx) in enumerate(_peers(bx, by, bc, with_self=True)):
                out, _, here = _exchange_refs(mode, src_refs[a], land_refs[a], me, idx)
                cp = pltpu.make_async_remote_copy(
                    src_ref=out, dst_ref=here, send_sem=send_ref.at[N_DEV * a + k], recv_sem=recv_ref.at[N_DEV * a + k],
                    device_id=p, device_id_type=MESH)
                cp.wait_send()
                cp.wait_recv()

    res = pl.pallas_call(
        body, name=f"{mode}_wait_{tag}",
        out_shape=(*[pltpu.HBM(s.shape, s.dtype) for s in srcs], *[pltpu.HBM(l.shape, l.dtype) for l in lands]),
        in_specs=[HBM_SPEC] * (2 * n) + [SEM_SPEC, SEM_SPEC] + [BS(memory_space=pl.ANY)] * len(afters),
        out_specs=tuple([HBM_SPEC] * (2 * n)),
        input_output_aliases={i: i for i in range(2 * n)},
        compiler_params=pltpu.CompilerParams(has_side_effects=DATAFLOW),
    )(*srcs, *lands, send_sems, recv_sems, *afters)
    return list(res[n:])


def _allreduce_small(vec, behind):
    rows = vec.shape[0]

    def body(x_ref, behind_ref, o_ref, buf_ref, send_sems, recv_sems):
        x, y, c = _place()
        me = 4 * x + 2 * y + c
        buf_ref[me] = x_ref[...]
        peers = _peers(x, y, c)

        def copy(k, slot):
            return pltpu.make_async_remote_copy(
                src_ref=x_ref, dst_ref=buf_ref.at[slot], send_sem=send_sems.at[k], recv_sem=recv_sems.at[k],
                device_id=peers[k][0], device_id_type=MESH)

        sends = [copy(k, me) for k in range(N_DEV - 1)]
        for cp in sends:
            cp.start()
        for k in range(N_DEV - 1):
            copy(k, peers[k][1]).wait_recv()
        for cp in sends:
            cp.wait_send()
        acc = buf_ref[0]
        for s in range(1, N_DEV):
            acc = acc + buf_ref[s]
        o_ref[...] = acc

    vmem = BS(memory_space=pltpu.VMEM)
    return pl.pallas_call(
        body, name="allreduce_small", in_specs=[vmem, BS(memory_space=pl.ANY)], out_specs=vmem,
        out_shape=SDS((rows, 128), F32),
        scratch_shapes=[pltpu.VMEM((N_DEV, rows, 128), F32), pltpu.SemaphoreType.DMA((7,)),
                        pltpu.SemaphoreType.DMA((7,))],
        compiler_params=pltpu.CompilerParams(has_side_effects=True),
    )(vec, behind)


def _ffn_forward(x, hn, fetch, names, tag, next_g):
    gu, act = _ffn_up(hn, fetch(names[0], hn).reshape(2, F, D), tag)
    got = _mm_nn(act[None], fetch(names[1], act)[None], f"down_{tag}", res=x, scale=0.5, next_g=next_g)
    out, hn_next = got if next_g is not None else (got, None)
    return out, hn_next, (x, hn, gu, act)


def _ffn_backward(dxo, dxo_b, saved, norm_g, wut, wd, tag, send):
    x, hn, gu, act = saved
    d_wd = _mm_tn(act[None], dxo_b, f"dwd_{tag}", scale=0.5)[0]
    du = _ffn_dact(dxo_b, wd, gu, send(("down",), [d_wd]), tag)
    d_wut = _mm_tn(du, hn, f"dwu_{tag}")
    token = send(("up",), [d_wut.reshape(2 * F, D)])
    return _mm_nn_norm_bwd(du, wut, x, norm_g + token[0, 0], dxo, tag)


def _mixer_forward(x, hn, fetch, bias, tables, tag, next_g):
    proj = _mm_nt_rows(hn, fetch("win", hn), f"proj_{tag}", 512, IN_W // 2, IN_W, 0, rope=(*tables, 2 * QKV_A // 3))
    qkr = proj
    outs, lses = [], []
    for grp in range(3):
        o, l = _dil_fwd(qkr, proj, grp)
        outs.append(o)
        lses.append(l)
    ya = _combine_fwd(outs, lses)
    yb, lse_b = _na_fwd(proj, bias)
    merged, za, zb = _merge_fwd(ya, yb, proj, fetch("wa", yb), fetch("wb", yb))
    out, hn_next = _mm_nn(merged[None], fetch("wo", merged)[None], f"out_{tag}", res=x, next_g=next_g)
    return out, hn_next, (x, hn, proj, qkr, outs, lses, ya, yb, lse_b, merged, za, zb)


def _mixer_backward(dxo, dxo_b, saved, norm_g, w, bias, tables, tag, send):
    wint, wat, wbt, wo = w
    x, hn, proj, qkr, outs, lses, ya, yb, lse_b, merged, za, zb = saved
    dm = _mm_nt_rows(dxo_b, wo, f"dmerged_{tag}", 512, D, D, 0)
    d_wo = _mm_tn(merged[None], dxo_b, f"dwo_{tag}")[0]
    dza, dzb, dlog = _merge_bwd(dm, za, zb, proj)
    dya = _mm_nn(dza[None], wat[None], f"dya_{tag}")
    dyb = _mm_nn(dzb[None], wbt[None], f"dyb_{tag}")
    d_wat = _mm_tn(dza[None], ya, f"dwa_{tag}")[0]
    d_wbt = _mm_tn(dzb[None], yb, f"dwb_{tag}")[0]
    cb = _combine_bwd(dya, outs, lses)
    dqs, dks, dvs = [], [], []
    for grp in range(3):
        dq, dk, dv = _dil_bwd(qkr, proj, cb[grp], cb[3 + grp], lses[grp], grp)
        dqs.append(dq)
        dks.append(dk)
        dvs.append(dv)
    dqk = _rope_bwd(dqs, dks, *tables)
    dqb, dkb, dvb, dbias_tab = _na_bwd(proj, bias, dyb, yb, lse_b)
    dbias = _na_dbias(dbias_tab)
    dproj = jnp.concatenate(
        [dqk] + [t.astype(BF16) for t in (*dvs, dqb, dkb, dvb)] + [dlog[0], dlog[1]], axis=1)
    d_wint = _mm_tn(dproj[None], hn, f"dwin_{tag}")[0]
    token = send(("win", "wa", "wb", "wo"), [d_wint, d_wat, d_wbt, d_wo])
    dx, dx_b, dg = _mm_nn_norm_bwd(dproj[None], wint[None], x, norm_g + token[0, 0], dxo, f"mix_{tag}")
    dbias = dbias[:, 0, :480].reshape(8, 15, 32)[:, :, :31]
    return dx, dx_b, dg, dbias


def _pack_small(norms, biases, final, loss=None):
    parts = []
    for layer in range(DEPTH):
        parts += [norms[0][layer], norms[1][layer], norms[2][layer],
                  jnp.pad(biases[layer].reshape(-1), (0, BIAS_PAD - 8 * 15 * 31))]
    parts.append(final)
    flat = jnp.concatenate([p.reshape(-1).astype(F32) for p in parts])
    if loss is not None:
        flat = jnp.concatenate([flat, loss.reshape(-1)])
    return jnp.pad(flat, (0, SMALL_ROWS * 128 - flat.shape[0])).reshape(SMALL_ROWS, 128)


def _unpack_small(packed):
    flat = packed.reshape(-1)
    norms, biases = ([], [], []), []
    pos = 0
    for _ in range(DEPTH):
        for k in range(3):
            norms[k].append(flat[pos:pos + D])
            pos += D
        biases.append(flat[pos:pos + 8 * 15 * 31].reshape(8, 15, 31))
        pos += BIAS_PAD
    final = flat[pos:pos + D]
    pos += D
    return [jnp.stack(n) for n in norms], jnp.stack(biases), final, flat[pos]


def kernel(x, ffn1_norm, ffn1_w_up, ffn1_w_down, mix_norm, w_in, na_rel_bias, w_branch_a, w_branch_b, w_out, ffn2_norm, ffn2_w_up, ffn2_w_down, final_norm, loss_target, m_ffn1_norm, m_ffn1_w_up, m_ffn1_w_down, m_mix_norm, m_w_in, m_na_rel_bias, m_w_branch_a, m_w_branch_b, m_w_out, m_ffn2_norm, m_ffn2_w_up, m_ffn2_w_down, m_final_norm, v_ffn1_norm, v_ffn1_w_up, v_ffn1_w_down, v_mix_norm, v_w_in, v_na_rel_bias, v_w_branch_a, v_w_branch_b, v_w_out, v_ffn2_norm, v_ffn2_w_up, v_ffn2_w_down, v_final_norm):
    t = x.shape[0] * x.shape[1]
    xs = x.reshape(t, D)
    tgt = loss_target.reshape(t, D)
    tables = _rope_tables()

    col_sharded = dict(up1=ffn1_w_up, win=w_in, wa=w_branch_a, wb=w_branch_b, up2=ffn2_w_up)
    row_sharded = dict(down1=ffn1_w_down, wo=w_out, down2=ffn2_w_down)
    shard = [{} for _ in range(DEPTH)]
    for layer in range(DEPTH):
        for name, arr in col_sharded.items():
            shard[layer][name] = arr[layer].T.astype(BF16)
        for name, arr in row_sharded.items():
            shard[layer][name] = arr[layer].astype(BF16)

    weights = [{} for _ in range(DEPTH)]
    travel = [(0, ("up1",)), (0, ("down1",)), (0, ("win",)), (0, ("wa", "wb", "wo")), (0, ("up2", "down2")),
              (1, ("up1", "down1")), (1, ("win",)), (1, ("wa", "wb", "wo")), (1, ("up2", "down2"))]
    pending = {}
    after = None
    for i, (layer, names) in enumerate(travel):
        handle, after = _exchange_start("gather", [shard[layer][n] for n in names], after, f"w{i}")
        for n in names:
            pending[layer, n] = (i, handle, names)
    zero = after[0, 0]

    biases = [_na_bias_table(na_rel_bias[layer]) for layer in range(DEPTH)]

    def fetcher(layer):
        def fetch(name, behind):
            if (layer, name) in pending:
                i, handle, names = pending[layer, name]
                if i == 0:
                    behind = (behind, *biases)
                for n, got in zip(names, _exchange_wait(handle, behind, f"w{i}")):
                    weights[layer][n] = got
                    del pending[layer, n]
            return weights[layer][name]
        return fetch

    saved = []
    h = xs
    hn = _norm_fwd(xs, ffn1_norm[0] + zero, "first")
    for layer in range(DEPTH):
        bias = biases[layer]
        fetch = fetcher(layer)
        after_ffn2 = ffn1_norm[layer + 1] if layer + 1 < DEPTH else None
        h, hn, s1 = _ffn_forward(h, hn, fetch, ("up1", "down1"), f"f1l{layer}", mix_norm[layer])
        h, hn, s2 = _mixer_forward(h, hn, fetch, bias, tables, f"l{layer}", ffn2_norm[layer])
        h, hn, s3 = _ffn_forward(h, hn, fetch, ("up2", "down2"), f"f2l{layer}", after_ffn2)
        saved.append((s1, s2, s3, bias))
    loss_part, dh, dh_b, d_final = _loss_head(h, final_norm, tgt)

    d_norms = ([None] * DEPTH, [None] * DEPTH, [None] * DEPTH)
    d_bias = [None] * DEPTH
    sent = {}

    def sender(layer, suffix):
        def send(names, grads):
            tag = f"g{layer}{names[0]}{suffix}"
            handle, token = _exchange_start("scatter", grads, None, tag)
            for i, n in enumerate(names):
                sent[layer, n + suffix] = (handle, i, tag)
            return token
        return send

    for layer in reversed(range(DEPTH)):
        w = weights[layer]
        s1, s2, s3, bias = saved[layer]
        dh, dh_b, d_norms[2][layer] = _ffn_backward(
            dh, dh_b, s3, ffn2_norm[layer], w["up2"].reshape(2, F, D), w["down2"], f"f2l{layer}", sender(layer, "2"))
        dh, dh_b, d_norms[1][layer], d_bias[layer] = _mixer_backward(
            dh, dh_b, s2, mix_norm[layer], (w["win"], w["wa"], w["wb"], w["wo"]), bias, tables, f"l{layer}",
            sender(layer, ""))
        dh, dh_b, d_norms[0][layer] = _ffn_backward(
            dh, dh_b, s1, ffn1_norm[layer], w["up1"].reshape(2, F, D), w["down1"], f"f1l{layer}", sender(layer, "1"))
    grad_x = dh.reshape(x.shape)

    originals = dict(up1=(ffn1_w_up, m_ffn1_w_up, v_ffn1_w_up), down1=(ffn1_w_down, m_ffn1_w_down, v_ffn1_w_down),
                     win=(w_in, m_w_in, v_w_in), wa=(w_branch_a, m_w_branch_a, v_w_branch_a),
                     wb=(w_branch_b, m_w_branch_b, v_w_branch_b), wo=(w_out, m_w_out, v_w_out),
                     up2=(ffn2_w_up, m_ffn2_w_up, v_ffn2_w_up), down2=(ffn2_w_down, m_ffn2_w_down, v_ffn2_w_down))
    big = {}
    behind = dh
    landed = {}

    def received(layer, name):
        handle, i, tag = sent[layer, name]
        if tag not in landed:
            landed[tag] = _exchange_wait(handle, behind, tag)
        return landed[tag][i]

    for name in ("down2", "up2", "win", "wa", "wb", "wo", "down1", "up1"):
        g = _sum_slots(received(0, name), received(1, name), name)
        wv, mv, vv = originals[name]
        if name in col_sharded:
            wv, mv, vv = (jnp.swapaxes(t, 1, 2) for t in (wv, mv, vv))
        big[name] = (g, *_adamw(wv, g, mv, vv, name))
        behind = big[name][1]
        if name in col_sharded:
            big[name] = tuple(jnp.swapaxes(t, 1, 2) for t in big[name])

    small = _allreduce_small(_pack_small(d_norms, d_bias, d_final, loss_part[0, :1]), behind)
    g_norms, g_bias, g_final, loss = _unpack_small(small)
    w_small = _pack_small((ffn1_norm, mix_norm, ffn2_norm), na_rel_bias, final_norm)
    m_small = _pack_small((m_ffn1_norm, m_mix_norm, m_ffn2_norm), m_na_rel_bias, m_final_norm)
    v_small = _pack_small((v_ffn1_norm, v_mix_norm, v_ffn2_norm), v_na_rel_bias, v_final_norm)
    upd = _adamw(w_small[None], small[None], m_small[None], v_small[None], "small")
    small_out = [(g_norms, g_bias, g_final)] + [_unpack_small(u[0])[:3] for u in upd]

    outputs = [loss, grad_x]
    for kind in range(4):
        norms, bias_k, final_k = small_out[kind]
        outputs += [norms[0], big["up1"][kind], big["down1"][kind], norms[1], big["win"][kind], bias_k,
                    big["wa"][kind], big["wb"][kind], big["wo"][kind], norms[2], big["up2"][kind],
                    big["down2"][kind], final_k]
    return tuple(outputs)
```

```python
import numpy as np

import jax
import jax.numpy as jnp
from jax import lax
from jax.experimental import pallas as pl
from jax.experimental.pallas import tpu as pltpu

F32 = jnp.float32
BF16 = jnp.bfloat16
SDS = jax.ShapeDtypeStruct
BS = pl.BlockSpec
MESH = pl.DeviceIdType.MESH

D = 1024
S = 2048
F = 2816
DEPTH = 2
HEAD_DIM = 64
DILATIONS = (1, 4, 16)
HALF = 64
QKV_A = 2304
QKV_B = 1536
IN_W = 5888
N_DEV = 8
NA_ROWS = 32
GRID_W = 64
NA_KR = 8
ROPE_THETA = 10000.0
RMS_EPS = 1e-6
NEG = -1e30
SCALE = HEAD_DIM ** -0.5
ADAM_LR, ADAM_B1, ADAM_B2, ADAM_EPS, ADAM_WD, ADAM_STEP = 0.001, 0.9, 0.999, 1e-08, 0.01, 10
VMEM_LIMIT_V7X = 52 * 1024 * 1024
SMALL_ROWS = 120
BIAS_PAD = 3840
NA_FWD_ROWS = 4
NA_BWD_ROWS = 4
DIL_FWD_TILES = 4
DIL_BWD_TILES = 4


def _cp(*sem):
    return pltpu.CompilerParams(dimension_semantics=sem, vmem_limit_bytes=VMEM_LIMIT_V7X)


def _dot_nn(a, b):
    return jnp.dot(a, b, preferred_element_type=F32)


def _dot_nt(a, b):
    return lax.dot_general(a, b, (((1,), (1,)), ((), ())), preferred_element_type=F32)


def _dot_tn(a, b):
    return lax.dot_general(a, b, (((0,), (0,)), ((), ())), preferred_element_type=F32)


def _ds(start, size, stride):
    return pl.ds(start, size) if stride == 1 else pl.ds(start, size, stride=stride)


def _norm_fwd(x, g, tag):
    t = x.shape[0]
    tm = 512

    def body(x_ref, g_ref, o_ref):
        xv = x_ref[...]
        r = lax.rsqrt(jnp.mean(xv * xv, axis=-1, keepdims=True) + RMS_EPS)
        o_ref[...] = (xv * r * g_ref[...]).astype(BF16)

    return pl.pallas_call(
        body, name=f"norm_fwd_{tag}", grid=(t // tm,),
        in_specs=[BS((tm, D), lambda i: (i, 0)), BS((1, D), lambda i: (0, 0))],
        out_specs=BS((tm, D), lambda i: (i, 0)),
        out_shape=SDS((t, D), BF16), compiler_params=_cp("parallel"),
    )(x, g.reshape(1, D))


def _loss_head(x, g, tgt):
    t = x.shape[0]
    tm = 512

    def body(x_ref, g_ref, t_ref, loss_ref, dx_ref, dxb_ref, dg_ref):
        @pl.when(pl.program_id(0) == 0)
        def _():
            dg_ref[...] = jnp.zeros_like(dg_ref)
            loss_ref[...] = jnp.zeros_like(loss_ref)

        xv = x_ref[...]
        gv = g_ref[...]
        r = lax.rsqrt(jnp.mean(xv * xv, axis=-1, keepdims=True) + RMS_EPS)
        xh = xv * r
        e = xh * gv - t_ref[...]
        loss_ref[...] += 0.5 * jnp.sum(jnp.mean(e * e, axis=-1, keepdims=True), axis=0, keepdims=True)
        dy = e * (1.0 / D)
        u = dy * gv
        dx = r * (u - xh * jnp.mean(xh * u, axis=-1, keepdims=True))
        dx_ref[...] = dx
        dxb_ref[...] = dx.astype(BF16)
        dg_ref[...] += jnp.sum(dy * xh, axis=0, keepdims=True)

    row = BS((tm, D), lambda i: (i, 0))
    vec = BS((1, D), lambda i: (0, 0))
    return pl.pallas_call(
        body, name="loss_head", grid=(t // tm,),
        in_specs=[row, vec, row], out_specs=[BS((1, 128), lambda i: (0, 0)), row, row, vec],
        out_shape=[SDS((1, 128), F32), SDS((t, D), F32), SDS((t, D), BF16), SDS((1, D), F32)],
        compiler_params=_cp("arbitrary"),
    )(x, g.reshape(1, D), tgt)


def _mm_nn(a, w, tag, res=None, scale=1.0, tm=512, tn=None, next_g=None):
    c_n, t, k = a.shape
    n = w.shape[2]
    tn = n if tn is None else tn
    assert next_g is None or tn == n
    n_in = 2 + (res is not None) + (next_g is not None)

    def body(*refs):
        a_ref, w_ref = refs[0], refs[1]
        acc = _dot_nn(a_ref[0].astype(BF16), w_ref[0])
        for c in range(1, c_n):
            acc = acc + _dot_nn(a_ref[c].astype(BF16), w_ref[c])
        if scale != 1.0:
            acc = acc * scale
        if res is not None:
            acc = refs[2][...] + acc
        refs[n_in][...] = acc
        if next_g is not None:
            r = lax.rsqrt(jnp.mean(acc * acc, axis=-1, keepdims=True) + RMS_EPS)
            refs[n_in + 1][...] = (acc * r * refs[n_in - 1][...]).astype(BF16)

    in_specs = [BS((c_n, tm, k), lambda i, j: (0, i, 0)), BS((c_n, k, tn), lambda i, j: (0, 0, j))]
    args = [a, w]
    out_specs = [BS((tm, tn), lambda i, j: (i, j))]
    out_shape = [SDS((t, n), F32)]
    if res is not None:
        in_specs.append(BS((tm, tn), lambda i, j: (i, j)))
        args.append(res)
    if next_g is not None:
        in_specs.append(BS((1, n), lambda i, j: (0, 0)))
        args.append(next_g.reshape(1, n))
        out_specs.append(BS((tm, tn), lambda i, j: (i, j)))
        out_shape.append(SDS((t, n), BF16))
    got = pl.pallas_call(
        body, name=f"mm_nn_{tag}", grid=(t // tm, n // tn), in_specs=in_specs, out_specs=out_specs,
        out_shape=out_shape, compiler_params=_cp("parallel", "parallel"),
    )(*args)
    return got if next_g is not None else got[0]


def _mm_nn_norm_bwd(a, w, x, g, dres, tag, tm=256):
    c_n, t, k = a.shape

    def body(a_ref, w_ref, x_ref, g_ref, dr_ref, dx_ref, dxb_ref, dg_ref):
        @pl.when(pl.program_id(0) == 0)
        def _():
            dg_ref[...] = jnp.zeros_like(dg_ref)

        dh = _dot_nn(a_ref[0], w_ref[0])
        for c in range(1, c_n):
            dh = dh + _dot_nn(a_ref[c], w_ref[c])
        xv = x_ref[...]
        r = lax.rsqrt(jnp.mean(xv * xv, axis=-1, keepdims=True) + RMS_EPS)
        xh = xv * r
        u = dh * g_ref[...]
        dx = dr_ref[...] + r * (u - xh * jnp.mean(xh * u, axis=-1, keepdims=True))
        dx_ref[...] = dx
        dxb_ref[...] = dx.astype(BF16)
        dg_ref[...] += jnp.sum(dh * xh, axis=0, keepdims=True)

    row = BS((tm, D), lambda i: (i, 0))
    vec = BS((1, D), lambda i: (0, 0))
    return pl.pallas_call(
        body, name=f"mm_nn_norm_bwd_{tag}", grid=(t // tm,),
        in_specs=[BS((c_n, tm, k), lambda i: (0, i, 0)), BS((c_n, k, D), lambda i: (0, 0, 0)), row, vec, row],
        out_specs=[row, row, vec], out_shape=[SDS((t, D), F32), SDS((t, D), BF16), SDS((1, D), F32)],
        compiler_params=_cp("arbitrary"),
    )(a, w, x, g.reshape(1, D), dres)


def _mm_nt_rows(a, w, tag, tm, tn, n_total, w_row0, rope=None):
    t, k = a.shape
    assert w_row0 % tn == 0 and n_total % tn == 0
    j0 = w_row0 // tn

    def body(a_ref, w_ref, *rest):
        o_ref = rest[-1]
        o_ref[...] = _dot_nt(a_ref[...].astype(BF16), w_ref[...])
        if rope is not None:
            @pl.when(pl.program_id(0) == 0)
            def _():
                c = rest[0][...]
                sg = rest[1][...]
                first = (lax.broadcasted_iota(jnp.int32, (tm, 128), 1) % HEAD_DIM) < HEAD_DIM // 2
                for col in range(0, rope[2], 128):
                    v = o_ref[:, col:col + 128]
                    o_ref[:, col:col + 128] = v * c + _swap_halves(v, first) * sg

    in_specs = [BS((tm, k), lambda j, i: (i, 0)), BS((tn, k), lambda j, i: (j0 + j, 0))]
    args = [a, w]
    if rope is not None:
        assert rope[2] <= tn
        in_specs += [BS((tm, 128), lambda j, i: (i % (S // tm), 0))] * 2
        args += [rope[0], rope[1]]
    return pl.pallas_call(
        body, name=f"mm_nt_{tag}", grid=(n_total // tn, t // tm), in_specs=in_specs,
        out_specs=BS((tm, tn), lambda j, i: (i, j)), out_shape=SDS((t, n_total), F32),
        compiler_params=_cp("parallel", "parallel"),
    )(*args)


def _mm_tn(a, b, tag, scale=1.0, tmm=256, tk=None):
    c_n, t, m = a.shape
    n = b.shape[1]
    tk = t if tk is None else tk
    nk = t // tk

    def body_one(a_ref, b_ref, o_ref):
        o_ref[...] = (_dot_tn(a_ref[...].astype(BF16), b_ref[...].astype(BF16)) * scale).astype(BF16)

    def body_acc(a_ref, b_ref, o_ref, acc_ref):
        kk = pl.program_id(2)

        @pl.when(kk == 0)
        def _():
            acc_ref[...] = jnp.zeros_like(acc_ref)

        acc_ref[...] += _dot_tn(a_ref[...].astype(BF16), b_ref[...].astype(BF16))

        @pl.when(kk == nk - 1)
        def _():
            o_ref[...] = (acc_ref[...] * scale).astype(BF16)

    return pl.pallas_call(
        body_one if nk == 1 else body_acc, name=f"mm_tn_{tag}", grid=(c_n, m // tmm, nk),
        in_specs=[BS((None, tk, tmm), lambda c, mi, kk: (c, kk, mi)), BS((tk, n), lambda c, mi, kk: (kk, 0))],
        out_specs=BS((None, tmm, n), lambda c, mi, kk: (c, mi, 0)),
        out_shape=SDS((c_n, m, n), BF16), scratch_shapes=[] if nk == 1 else [pltpu.VMEM((tmm, n), F32)],
        compiler_params=_cp("parallel", "parallel", "arbitrary"),
    )(a, b)


def _ffn_up(hn, wut, tag):
    t = hn.shape[0]
    tm, tn = 512, 1408

    def body(h_ref, w_ref, gu_ref, act_ref):
        h = h_ref[...]
        g = _dot_nt(h, w_ref[0])
        u = _dot_nt(h, w_ref[1])
        sg = jax.nn.sigmoid(g)
        silu = g * sg
        gu_ref[0] = (u * (sg + silu * (1.0 - sg))).astype(BF16)
        gu_ref[1] = silu.astype(BF16)
        act_ref[...] = (silu * u).astype(BF16)

    return pl.pallas_call(
        body, name=f"ffn_up_{tag}", grid=(F // tn, t // tm),
        in_specs=[BS((tm, D), lambda j, i: (i, 0)), BS((2, tn, D), lambda j, i: (0, j, 0))],
        out_specs=[BS((2, tm, tn), lambda j, i: (0, i, j)), BS((tm, tn), lambda j, i: (i, j))],
        out_shape=[SDS((2, t, F), BF16), SDS((t, F), BF16)],
        compiler_params=_cp("parallel", "parallel"),
    )(hn, wut)


def _ffn_dact(dxo, wd, gu, tie, tag):
    t = dxo.shape[0]
    tm, tn = 512, 1408

    def body(d_ref, w_ref, gu_ref, tie_ref, o_ref):
        dact = _dot_nt(d_ref[...] * 0.5, w_ref[...])
        o_ref[0] = (dact * gu_ref[0].astype(F32)).astype(BF16)
        o_ref[1] = (dact * gu_ref[1].astype(F32)).astype(BF16)

    return pl.pallas_call(
        body, name=f"ffn_dact_{tag}", grid=(F // tn, t // tm),
        in_specs=[BS((tm, D), lambda j, i: (i, 0)), BS((tn, D), lambda j, i: (j, 0)),
                  BS((2, tm, tn), lambda j, i: (0, i, j)), BS((8, 128), lambda j, i: (0, 0))],
        out_specs=BS((2, tm, tn), lambda j, i: (0, i, j)),
        out_shape=SDS((2, t, F), BF16), compiler_params=_cp("parallel", "parallel"),
    )(dxo, wd, gu, tie)


def _rope_tables():
    half = HEAD_DIM // 2
    inv_freq = ROPE_THETA ** (-jnp.arange(half, dtype=F32) / half)
    ang = jnp.arange(S).astype(F32)[:, None] * inv_freq[None, :]
    cos, sin = jnp.cos(ang), jnp.sin(ang)
    return jnp.concatenate([cos, cos, cos, cos], axis=1), jnp.concatenate([-sin, sin, -sin, sin], axis=1)


def _swap_halves(t, first_half):
    return jnp.where(first_half, pltpu.roll(t, 96, 1), pltpu.roll(t, 32, 1))


def _rope_bwd(dqs, dks, cos_t, sin_t):
    t = dqs[0].shape[0]
    tm = 512

    def body(*refs):
        c = refs[6][...]
        sg = refs[7][...]
        o_ref = refs[8]
        first = (lax.broadcasted_iota(jnp.int32, (tm, 128), 1) % HEAD_DIM) < HEAD_DIM // 2
        for a in range(6):
            for hp in range(2):
                v = refs[a][:, 128 * hp:128 * (hp + 1)]
                col = 128 * (2 * a + hp)
                o_ref[:, col:col + 128] = (v * c + _swap_halves(v * sg, first)).astype(BF16)

    blk = BS((tm, 256), lambda i: (i, 0))
    tab = BS((tm, 128), lambda i: (i % (S // tm), 0))
    return pl.pallas_call(
        body, name="rope_bwd", grid=(t // tm,), in_specs=[blk] * 6 + [tab, tab],
        out_specs=BS((tm, 1536), lambda i: (i, 0)), out_shape=SDS((t, 1536), BF16),
        compiler_params=_cp("parallel"),
    )(*dqs, *dks, cos_t, sin_t)


def _head_masks():
    lane = lax.broadcasted_iota(jnp.int32, (1, 128), 1)
    m0 = (lane < HEAD_DIM).astype(F32)
    return m0, 1.0 - m0


def _dil_geometry(d):
    sub = S // d
    q_rows = 128
    k_rows = min(256, sub)
    return sub, q_rows, sub // q_rows, k_rows


def _dil_tile(idx, d):
    sub, q_rows, nb, k_rows = _dil_geometry(d)
    r = idx // nb
    n = idx % nb
    k_sub = jnp.clip(q_rows * n - HALF, 0, sub - k_rows)
    if d == 1:
        q_start = pl.multiple_of(q_rows * n, q_rows)
        k_start = pl.multiple_of(k_sub, HALF)
    else:
        q_start = q_rows * n * d + r
        k_start = k_sub * d + r
    ii = lax.broadcasted_iota(jnp.int32, (q_rows, k_rows), 0)
    jj = lax.broadcasted_iota(jnp.int32, (q_rows, k_rows), 1)
    valid = jnp.abs(jj - ii + (k_sub - q_rows * n)) <= HALF
    return q_start, k_start, valid


def _dil_specs(grp):
    qs = BS((S, 128), lambda b, hp: (b, 2 * grp + hp))
    ks = BS((S, 128), lambda b, hp: (b, 6 + 2 * grp + hp))
    vs = BS((S, 128), lambda b, hp: (b, 12 + 2 * grp + hp))
    own = BS((S, 128), lambda b, hp: (b, hp))
    return qs, ks, vs, own


def _dil_fwd(qkr, proj, grp):
    t = qkr.shape[0]
    d = DILATIONS[grp]
    _, q_rows, nb, k_rows = _dil_geometry(d)

    def body(q_ref, k_ref, v_ref, o_ref, l_ref):
        masks = _head_masks()

        def step(i0, carry):
            geo = [_dil_tile(i0 * DIL_FWD_TILES + j, d) for j in range(DIL_FWD_TILES)]
            tiles = [(j, h) for j in range(DIL_FWD_TILES) for h in range(2)]
            qs = [q_ref[_ds(g[0], q_rows, d), :] for g in geo]
            kbs = [k_ref[_ds(g[1], k_rows, d), :].astype(BF16) for g in geo]
            ss = [jnp.where(geo[j][2], _dot_nt((qs[j] * masks[h]).astype(BF16), kbs[j]) * SCALE, NEG) for j, h in tiles]
            mxs = [jnp.max(s, axis=1, keepdims=True) for s in ss]
            ps = [jnp.exp(s - mx) for s, mx in zip(ss, mxs)]
            dens = [jnp.sum(p, axis=1, keepdims=True) for p in ps]
            vs = [v_ref[_ds(g[1], k_rows, d), :] for g in geo]
            outs = [_dot_nn(p.astype(BF16), (vs[j] * masks[h]).astype(BF16)) / den
                    for p, den, (j, h) in zip(ps, dens, tiles)]
            for j, g in enumerate(geo):
                o_ref[_ds(g[0], q_rows, d), :] = outs[2 * j] + outs[2 * j + 1]
                l_ref[_ds(g[0], q_rows, d), :] = (
                    (mxs[2 * j] + jnp.log(dens[2 * j])) * masks[0] + (mxs[2 * j + 1] + jnp.log(dens[2 * j + 1])) * masks[1])
            return carry

        lax.fori_loop(0, d * nb // DIL_FWD_TILES, step, 0)

    qs, ks, vs, own = _dil_specs(grp)
    return pl.pallas_call(
        body, name=f"dil_fwd_{grp}", grid=(t // S, 2), in_specs=[qs, ks, vs], out_specs=[own, own],
        out_shape=[SDS((t, 256), F32), SDS((t, 256), F32)], compiler_params=_cp("parallel", "parallel"),
    )(qkr, qkr, proj)


def _dil_bwd(qkr, proj, do, dlp, lse, grp):
    t = qkr.shape[0]
    d = DILATIONS[grp]
    _, q_rows, nb, k_rows = _dil_geometry(d)

    def body(q_ref, k_ref, v_ref, do_ref, dl_ref, l_ref, dq_ref, dk_ref, dv_ref):
        masks = _head_masks()
        dk_ref[...] = jnp.zeros_like(dk_ref)
        dv_ref[...] = jnp.zeros_like(dv_ref)

        def step(i0, carry):
            geo = [_dil_tile(i0 * DIL_BWD_TILES + j, d) for j in range(DIL_BWD_TILES)]
            tiles = [(j, h) for j in range(DIL_BWD_TILES) for h in range(2)]
            q_ds = [_ds(g[0], q_rows, d) for g in geo]
            k_ds = [_ds(g[1], k_rows, d) for g in geo]
            qs = [q_ref[r, :] for r in q_ds]
            ks = [k_ref[r, :] for r in k_ds]
            kbs = [k.astype(BF16) for k in ks]
            vbs = [v_ref[r, :].astype(BF16) for r in k_ds]
            dos = [do_ref[r, :] for r in q_ds]
            dls = [dl_ref[r, :] for r in q_ds]
            lss = [l_ref[r, :] for r in q_ds]
            qhs = [(qs[j] * masks[h]).astype(BF16) for j, h in tiles]
            dohs = [(dos[j] * masks[h]).astype(BF16) for j, h in tiles]
            ss = [jnp.where(geo[j][2], _dot_nt(qh, kbs[j]) * SCALE, NEG) for qh, (j, h) in zip(qhs, tiles)]
            ps = [jnp.exp(s - lss[j][:, HEAD_DIM * h:HEAD_DIM * h + 1]) for s, (j, h) in zip(ss, tiles)]
            dps = [_dot_nt(doh, vbs[j]) for doh, (j, h) in zip(dohs, tiles)]
            dss = [(p * (dp - dls[j][:, HEAD_DIM * h:HEAD_DIM * h + 1])).astype(BF16)
                   for p, dp, (j, h) in zip(ps, dps, tiles)]
            dqs = [_dot_nn(ds, (ks[j] * masks[h]).astype(BF16)) for ds, (j, h) in zip(dss, tiles)]
            dkws = [_dot_tn(ds, qh) for ds, qh in zip(dss, qhs)]
            dvws = [_dot_tn(p.astype(BF16), doh) for p, doh in zip(ps, dohs)]
            for j in range(DIL_BWD_TILES):
                dq_ref[q_ds[j], :] = (dqs[2 * j] + dqs[2 * j + 1]) * SCALE
                dk_ref[k_ds[j], :] += (dkws[2 * j] + dkws[2 * j + 1]) * SCALE
                dv_ref[k_ds[j], :] += dvws[2 * j] + dvws[2 * j + 1]
            return carry

        lax.fori_loop(0, d * nb // DIL_BWD_TILES, step, 0)

    qs, ks, vs, own = _dil_specs(grp)
    return pl.pallas_call(
        body, name=f"dil_bwd_{grp}", grid=(t // S, 2), in_specs=[qs, ks, vs, own, own, own],
        out_specs=[own, own, own], out_shape=[SDS((t, 256), F32)] * 3,
        compiler_params=_cp("parallel", "parallel"),
    )(qkr, qkr, proj, do, dlp, lse)


def _mix_weights(l0, l1, l2):
    mx = jnp.maximum(jnp.maximum(l0, l1), l2)
    e0, e1, e2 = jnp.exp(l0 - mx), jnp.exp(l1 - mx), jnp.exp(l2 - mx)
    den = e0 + e1 + e2
    return e0 / den, e1 / den, e2 / den


def _combine_fwd(outs, lses):
    t = outs[0].shape[0]
    tm = 512

    def body(o0, o1, o2, l0, l1, l2, y_ref):
        w0, w1, w2 = _mix_weights(l0[...], l1[...], l2[...])
        y_ref[...] = w0 * o0[...] + w1 * o1[...] + w2 * o2[...]

    blk = BS((tm, 256), lambda i: (i, 0))
    return pl.pallas_call(
        body, name="combine_fwd", grid=(t // tm,), in_specs=[blk] * 6, out_specs=blk,
        out_shape=SDS((t, 256), F32), compiler_params=_cp("parallel"),
    )(*outs, *lses)


def _head_sum(x):
    a = lax.broadcasted_iota(jnp.int32, (256, 256), 0) // HEAD_DIM
    b = lax.broadcasted_iota(jnp.int32, (256, 256), 1) // HEAD_DIM
    ones = (a == b).astype(BF16)
    hi = x.astype(BF16)
    lo = (x - hi.astype(F32)).astype(BF16)
    return _dot_nn(hi, ones) + _dot_nn(lo, ones)


def _combine_bwd(dya, outs, lses):
    t = dya.shape[0]
    tm = 512

    def body(dy_ref, o0, o1, o2, l0, l1, l2, d0, d1, d2, e0, e1, e2):
        ws = _mix_weights(l0[...], l1[...], l2[...])
        dy = dy_ref[...]
        ya = ws[0] * o0[...] + ws[1] * o1[...] + ws[2] * o2[...]
        hs = _head_sum(dy * ya)
        for w, d_ref, e_ref in zip(ws, (d0, d1, d2), (e0, e1, e2)):
            d_ref[...] = w * dy
            e_ref[...] = w * hs

    blk = BS((tm, 256), lambda i: (i, 0))
    return pl.pallas_call(
        body, name="combine_bwd", grid=(t // tm,), in_specs=[blk] * 7, out_specs=[blk] * 6,
        out_shape=[SDS((t, 256), F32)] * 6, compiler_params=_cp("parallel"),
    )(dya, *outs, *lses)


def _na_bias_table(rel_bias):
    qc = np.arange(GRID_W)[:, None]
    kc = np.arange(GRID_W)[None, :]
    win_lo = np.clip(qc - 8, 0, GRID_W - 16)
    col_valid = (kc >= win_lo) & (kc < win_lo + 16)
    col_idx = np.clip(kc - qc + 15, 0, 30)
    row_idx = np.arange(NA_KR)[:, None] + np.arange(NA_KR)[None, :]
    rows = (row_idx[..., None] == np.arange(2 * NA_KR - 1)).astype(np.float32)
    cols = (col_idx[..., None] == np.arange(31)).astype(np.float32)
    b = jnp.einsum("hrd,ckr,qjd->ckjhq", rel_bias.astype(F32), rows, cols, precision=lax.Precision.HIGHEST)
    b = jnp.where(col_valid.T[None, None, :, None, :], b, NEG)
    return b.reshape(NA_KR, NA_KR * GRID_W, 4, 128).transpose(2, 0, 1, 3)


def _na_row(i):
    lo = jnp.clip(i - NA_KR // 2, 0, NA_ROWS - NA_KR)
    return pl.multiple_of(GRID_W * i, GRID_W), pl.multiple_of(GRID_W * lo, GRID_W), lo - i + NA_KR - 1


def _both_heads(x, masks):
    return jnp.concatenate([x * masks[0], x * masks[1]], axis=0)


def _own_heads(r, masks):
    half = r.shape[0] // 2
    return r[:half] * masks[0] + r[half:] * masks[1]


def _na_fwd(proj, bias):
    t = proj.shape[0]
    kw = NA_KR * GRID_W

    def body(q_ref, k_ref, v_ref, b_ref, o_ref, l_ref):
        masks = _head_masks()

        def step(i0, carry):
            idx = [i0 * NA_FWD_ROWS + j for j in range(NA_FWD_ROWS)]
            rows = [_na_row(i) for i in idx]
            qbs = [_both_heads(q_ref[pl.ds(r[0], GRID_W), :], masks).astype(BF16) for r in rows]
            kbs = [k_ref[pl.ds(r[1], kw), :].astype(BF16) for r in rows]
            ss = [_dot_nt(kb, qb) * SCALE + b_ref[r[2]] for kb, qb, r in zip(kbs, qbs, rows)]
            mxs = [jnp.max(s, axis=0, keepdims=True) for s in ss]
            ps = [jnp.exp(s - mx) for s, mx in zip(ss, mxs)]
            dens = [jnp.sum(p, axis=0, keepdims=True) for p in ps]
            pbs = [(p / den).astype(BF16) for p, den in zip(ps, dens)]
            vbs = [v_ref[pl.ds(r[1], kw), :].astype(BF16) for r in rows]
            outs = [_own_heads(_dot_tn(pb, vb), masks) for pb, vb in zip(pbs, vbs)]
            for j, r in enumerate(rows):
                o_ref[pl.ds(r[0], GRID_W), :] = outs[j]
                l_ref[pl.ds(idx[j], 1), :] = mxs[j] + jnp.log(dens[j])
            return carry

        lax.fori_loop(0, NA_ROWS // NA_FWD_ROWS, step, 0)

    c0 = QKV_A // 128
    return pl.pallas_call(
        body, name="na_fwd", grid=(t // S, 4),
        in_specs=[BS((S, 128), lambda b, hp: (b, c0 + hp)), BS((S, 128), lambda b, hp: (b, c0 + 4 + hp)),
                  BS((S, 128), lambda b, hp: (b, c0 + 8 + hp)),
                  BS((None, NA_KR, kw, 128), lambda b, hp: (hp, 0, 0, 0))],
        out_specs=[BS((S, 128), lambda b, hp: (b, hp)), BS((None, None, NA_ROWS, 128), lambda b, hp: (b, hp, 0, 0))],
        out_shape=[SDS((t, 512), F32), SDS((t // S, 4, NA_ROWS, 128), F32)],
        compiler_params=_cp("parallel", "parallel"),
    )(proj, proj, proj, bias)


def _na_bwd(proj, bias, dyb, yb, lse):
    t = proj.shape[0]
    kw = NA_KR * GRID_W

    def body(q_ref, k_ref, v_ref, b_ref, do_ref, o_ref, l_ref, dq_ref, dk_ref, dv_ref, db_ref):
        masks = _head_masks()
        ones = jnp.ones((8, 128), BF16)

        @pl.when(pl.program_id(1) == 0)
        def _():
            db_ref[...] = jnp.zeros_like(db_ref)

        dk_ref[...] = jnp.zeros_like(dk_ref)
        dv_ref[...] = jnp.zeros_like(dv_ref)

        def row_sums(x):
            hi = x.astype(BF16)
            lo = (x - hi.astype(F32)).astype(BF16)
            return (_dot_nt(ones, hi) + _dot_nt(ones, lo))[0:1]

        def step(i0, carry):
            idx = [i0 * NA_BWD_ROWS + j for j in range(NA_BWD_ROWS)]
            rows = [_na_row(i) for i in idx]
            q_ds = [pl.ds(r[0], GRID_W) for r in rows]
            k_ds = [pl.ds(r[1], kw) for r in rows]
            qbs = [_both_heads(q_ref[r, :], masks).astype(BF16) for r in q_ds]
            kbs = [k_ref[r, :].astype(BF16) for r in k_ds]
            vbs = [v_ref[r, :].astype(BF16) for r in k_ds]
            dos = [do_ref[r, :] for r in q_ds]
            dobs = [_both_heads(do, masks).astype(BF16) for do in dos]
            deltas = [row_sums(_both_heads(do * o_ref[r, :], masks)) for do, r in zip(dos, q_ds)]
            ss = [_dot_nt(kb, qb) * SCALE + b_ref[r[2]] for kb, qb, r in zip(kbs, qbs, rows)]
            ps = [jnp.exp(s - l_ref[pl.ds(i, 1), :]) for s, i in zip(ss, idx)]
            dps = [_dot_nt(vb, dob) for vb, dob in zip(vbs, dobs)]
            dss = [p * (dp - delta) for p, dp, delta in zip(ps, dps, deltas)]
            for ds, r in zip(dss, rows):
                db_ref[r[2]] += ds
            dsbs = [ds.astype(BF16) for ds in dss]
            dks = [_dot_nn(dsb, qb) for dsb, qb in zip(dsbs, qbs)]
            dvs = [_dot_nn(p.astype(BF16), dob) for p, dob in zip(ps, dobs)]
            dqs = [_own_heads(_dot_tn(dsb, kb), masks) for dsb, kb in zip(dsbs, kbs)]
            for j in range(NA_BWD_ROWS):
                dq_ref[q_ds[j], :] = dqs[j] * SCALE
                dk_ref[k_ds[j], :] += dks[j] * SCALE
                dv_ref[k_ds[j], :] += dvs[j]
            return carry

        lax.fori_loop(0, NA_ROWS // NA_BWD_ROWS, step, 0)

    c0 = QKV_A // 128
    own = BS((S, 128), lambda hp, b: (b, hp))
    tab = BS((None, NA_KR, kw, 128), lambda hp, b: (hp, 0, 0, 0))
    return pl.pallas_call(
        body, name="na_bwd", grid=(4, t // S),
        in_specs=[BS((S, 128), lambda hp, b: (b, c0 + hp)), BS((S, 128), lambda hp, b: (b, c0 + 4 + hp)),
                  BS((S, 128), lambda hp, b: (b, c0 + 8 + hp)), tab, own, own,
                  BS((None, None, NA_ROWS, 128), lambda hp, b: (b, hp, 0, 0))],
        out_specs=[own, own, own, tab],
        out_shape=[SDS((t, 512), F32)] * 3 + [SDS((4, NA_KR, kw, 128), F32)],
        compiler_params=_cp("parallel", "arbitrary"),
    )(proj, proj, proj, bias, dyb, yb, lse)


def _na_dbias_lane_map():
    kw = NA_KR * GRID_W
    lane = np.arange(kw)
    blk, m = lane // GRID_W, lane % GRID_W
    target = np.full(kw, -1)
    target[m < 16] = (blk * 32 + 15 + m)[m < 16]
    target[m >= 49] = (((blk + 1) % NA_KR) * 32 + m - 49)[m >= 49]
    return jnp.asarray(target[:, None] == np.arange(kw)[None, :], BF16)


def _na_dbias(db):
    kw = NA_KR * GRID_W

    def body(x_ref, map_ref, o_ref, z_ref):
        for cls in range(NA_KR):
            xt = x_ref[cls].T
            for h in range(2):
                xv = xt[GRID_W * h:GRID_W * (h + 1)]
                y = xv[0:8]
                for g in range(1, GRID_W // 8):
                    y = y + pltpu.roll(xv[8 * g:8 * g + 8], kw - 8 * g, 1)
                d = y[0:1]
                for s in range(1, 8):
                    d = d + pltpu.roll(y[s:s + 1], kw - s, 1)
                z_ref[h, cls:cls + 1, :] = d
        for h in range(2):
            z = z_ref[h]
            hi = z.astype(BF16)
            lo = (z - hi.astype(F32)).astype(BF16)
            e = _dot_nn(hi, map_ref[...]) + _dot_nn(lo, map_ref[...])
            out = e[0:1]
            for cls in range(1, NA_KR):
                out = out + pltpu.roll(e[cls:cls + 1], 32 * cls, 1)
            o_ref[h] = jnp.broadcast_to(out, (8, kw))

    return pl.pallas_call(
        body, name="na_dbias", grid=(4,),
        in_specs=[BS((None, NA_KR, kw, 128), lambda hp: (hp, 0, 0, 0)), BS((kw, kw), lambda hp: (0, 0))],
        out_specs=BS((2, 8, kw), lambda hp: (hp, 0, 0)), out_shape=SDS((8, 8, kw), F32),
        scratch_shapes=[pltpu.VMEM((2, 8, kw), F32)], compiler_params=_cp("parallel"),
    )(db, _na_dbias_lane_map())


def _merge_fwd(ya, yb, proj, wat, wbt):
    t = ya.shape[0]
    tm, tn = 512, 256
    ca = (QKV_A + QKV_B) // tn
    cb = ca + D // tn

    def body(ya_ref, yb_ref, la_ref, lb_ref, wa_ref, wb_ref, m_ref, za_ref, zb_ref):
        za = _dot_nt(ya_ref[...].astype(BF16), wa_ref[...])
        zb = _dot_nt(yb_ref[...].astype(BF16), wb_ref[...])
        m_ref[...] = (jax.nn.sigmoid(la_ref[...]) * za + jax.nn.sigmoid(lb_ref[...]) * zb).astype(BF16)
        za_ref[...] = za.astype(BF16)
        zb_ref[...] = zb.astype(BF16)

    out = BS((tm, tn), lambda i, j: (i, j))
    return pl.pallas_call(
        body, name="merge_fwd", grid=(t // tm, D // tn),
        in_specs=[BS((tm, 256), lambda i, j: (i, 0)), BS((tm, 512), lambda i, j: (i, 0)),
                  BS((tm, tn), lambda i, j: (i, ca + j)), BS((tm, tn), lambda i, j: (i, cb + j)),
                  BS((tn, 256), lambda i, j: (j, 0)), BS((tn, 512), lambda i, j: (j, 0))],
        out_specs=[out, out, out], out_shape=[SDS((t, D), BF16)] * 3,
        compiler_params=_cp("parallel", "parallel"),
    )(ya, yb, proj, proj, wat, wbt)


def _merge_bwd(dm, za, zb, proj):
    t = dm.shape[0]
    tm, tn = 512, 256
    ca = (QKV_A + QKV_B) // tn
    cb = ca + D // tn

    def body(dm_ref, za_ref, zb_ref, la_ref, lb_ref, dza_ref, dzb_ref, dl_ref):
        dmv = dm_ref[...]
        ga = jax.nn.sigmoid(la_ref[...])
        gb = jax.nn.sigmoid(lb_ref[...])
        dza_ref[...] = (dmv * ga).astype(BF16)
        dzb_ref[...] = (dmv * gb).astype(BF16)
        dl_ref[0] = (dmv * za_ref[...].astype(F32) * ga * (1.0 - ga)).astype(BF16)
        dl_ref[1] = (dmv * zb_ref[...].astype(F32) * gb * (1.0 - gb)).astype(BF16)

    blk = BS((tm, tn), lambda i, j: (i, j))
    return pl.pallas_call(
        body, name="merge_bwd", grid=(t // tm, D // tn),
        in_specs=[blk, blk, blk, BS((tm, tn), lambda i, j: (i, ca + j)), BS((tm, tn), lambda i, j: (i, cb + j))],
        out_specs=[blk, blk, BS((2, tm, tn), lambda i, j: (0, i, j))],
        out_shape=[SDS((t, D), BF16), SDS((t, D), BF16), SDS((2, t, D), BF16)],
        compiler_params=_cp("parallel", "parallel"),
    )(dm, za, zb, proj, proj)


def _sum_slots(recv0, recv1, tag):
    _, r, c = recv0.shape
    tr = r if r * c <= 512 * 1024 else r // 2

    def body(a_ref, b_ref, o_ref):
        for layer, ref in enumerate((a_ref, b_ref)):
            acc = ref[0].astype(F32)
            for s in range(1, N_DEV):
                acc = acc + ref[s].astype(F32)
            o_ref[layer] = acc

    blk = BS((N_DEV, tr, c), lambda i: (0, i, 0))
    return pl.pallas_call(
        body, name=f"sum_slots_{tag}", grid=(r // tr,), in_specs=[blk, blk],
        out_specs=BS((2, tr, c), lambda i: (0, i, 0)), out_shape=SDS((2, r, c), F32),
        compiler_params=_cp("parallel"),
    )(recv0, recv1)


def _adamw(w, g, m, v, tag):
    layers, r, c = w.shape
    tr = next(r // k for k in (1, 2, 4, 8) if r // k <= 384 and r % (8 * k) == 0)

    def body(w_ref, g_ref, m_ref, v_ref, d_ref, mo_ref, vo_ref):
        gv = g_ref[...]
        mn = ADAM_B1 * m_ref[...] + (1.0 - ADAM_B1) * gv
        vn = ADAM_B2 * v_ref[...] + (1.0 - ADAM_B2) * (gv * gv)
        m_hat = mn / (1.0 - ADAM_B1 ** ADAM_STEP)
        v_hat = vn / (1.0 - ADAM_B2 ** ADAM_STEP)
        d_ref[...] = -ADAM_LR * (m_hat / (jnp.sqrt(v_hat) + ADAM_EPS) + ADAM_WD * w_ref[...])
        mo_ref[...] = mn
        vo_ref[...] = vn

    blk = BS((None, tr, c), lambda l, i: (l, i, 0))
    return pl.pallas_call(
        body, name=f"adamw_{tag}", grid=(layers, r // tr), in_specs=[blk] * 4, out_specs=[blk] * 3,
        out_shape=[SDS((layers, r, c), F32)] * 3, compiler_params=_cp("parallel", "parallel"),
    )(w, g, m, v)


def _place():
    return lax.axis_index("x"), lax.axis_index("y"), lax.axis_index("c")


def _flip(coord, bit):
    return 1 - coord if bit else coord


def _peers(x, y, c):
    peers = []
    for mask in range(1, N_DEV):
        p = (_flip(x, mask & 4), _flip(y, mask & 2), _flip(c, mask & 1))
        peers.append((p, 4 * p[0] + 2 * p[1] + p[2]))
    return peers


def _copy_plan(mode, src, land, x, y, c):
    me = 4 * x + 2 * y + c

    def device(mask):
        p = (_flip(x, mask & 4), _flip(y, mask & 2), _flip(c, mask & 1))
        return p, 4 * p[0] + 2 * p[1] + p[2]

    if mode == "scatter":
        r = land.shape[1]
        return [(p, src.at[pl.ds(i * r, r), :], land.at[me], land.at[i])
                for p, i in map(device, (1, 2, 3, 4, 5, 6, 7, 0))]
    r = land.shape[0] // N_DEV

    def rows(i):
        return land.at[pl.ds(i * r, r), :]

    if mode == "gather":
        return [(p, src, rows(me), rows(i)) for p, i in map(device, (1, 4, 2, 6, 0))]
    sibling = device(1)[0]
    return [(sibling, rows(device(m)[1]), rows(device(m)[1]), rows(device(m | 1)[1])) for m in (4, 2, 6)]


COPIES = dict(scatter=8, gather=5, forward=3)
HBM_SPEC = BS(memory_space=pltpu.HBM)
SEM_SPEC = BS(memory_space=pltpu.SEMAPHORE)
DATAFLOW = pltpu.SideEffectType.DATAFLOW_SIDE_EFFECTING


def _fresh(shape, dtype, tag):
    def body(o_ref):
        del o_ref

    return pl.pallas_call(body, name=f"fresh_{tag}", out_specs=BS(memory_space=pl.ANY), out_shape=SDS(shape, dtype))()


def _exchange_start(mode, srcs, lands, after, tag):
    if lands is None and mode == "gather":
        lands = [_fresh((N_DEV * s.shape[0], s.shape[1]), s.dtype, f"{tag}_{a}") for a, s in enumerate(srcs)]
    elif lands is None:
        lands = [_fresh((N_DEV, s.shape[0] // N_DEV, s.shape[1]), s.dtype, f"{tag}_{a}") for a, s in enumerate(srcs)]
    n, n_src, n_cp = len(lands), len(srcs), COPIES[mode]
    behind = [] if after is None else [after]

    def body(*refs):
        src_refs, land_refs = refs[:n_src], refs[n_src:n_src + n]
        send_sems, recv_sems = refs[n_src + n + len(behind)], refs[n_src + n + len(behind) + 1]
        token = refs[-1]
        for a in range(n):
            plan = _copy_plan(mode, src_refs[a] if n_src else None, land_refs[a], *_place())
            for k, (p, out, there, _) in enumerate(plan):
                pltpu.make_async_remote_copy(
                    src_ref=out, dst_ref=there, send_sem=send_sems.at[n_cp * a + k],
                    recv_sem=recv_sems.at[n_cp * a + k], device_id=p, device_id_type=MESH).start()
        token[...] = jnp.zeros_like(token)

    both = [*srcs, *lands]
    res = pl.pallas_call(
        body, name=f"{mode}_start_{tag}",
        out_shape=(pltpu.SemaphoreType.DMA((n_cp * n,)), pltpu.SemaphoreType.DMA((n_cp * n,)),
                   *[pltpu.HBM(v.shape, v.dtype) for v in both], SDS((8, 128), F32)),
        in_specs=[HBM_SPEC] * len(both) + [BS(memory_space=pl.ANY)] * len(behind),
        out_specs=(SEM_SPEC, SEM_SPEC, *[HBM_SPEC] * len(both), BS(memory_space=pltpu.VMEM)),
        input_output_aliases={i: 2 + i for i in range(len(both))},
        compiler_params=pltpu.CompilerParams(has_side_effects=DATAFLOW),
    )(*[pltpu.with_memory_space_constraint(v, pltpu.HBM) for v in both], *behind)
    return (mode, res[0], res[1], res[2:2 + n_src], res[2 + n_src:2 + n_src + n]), res[-1]


def _exchange_wait(handle, after, tag):
    mode, send_sems, recv_sems, srcs, lands = handle
    n, n_src, n_cp = len(lands), len(srcs), COPIES[mode]
    afters = list(after) if isinstance(after, (tuple, list)) else [after]

    def body(*refs):
        src_refs, land_refs = refs[:n_src], refs[n_src:n_src + n]
        send_ref, recv_ref = refs[n_src + n], refs[n_src + n + 1]
        for a in range(n):
            plan = _copy_plan(mode, src_refs[a] if n_src else None, land_refs[a], *_place())
            for k, (p, out, _, here) in enumerate(plan):
                cp = pltpu.make_async_remote_copy(
                    src_ref=out, dst_ref=here, send_sem=send_ref.at[n_cp * a + k], recv_sem=recv_ref.at[n_cp * a + k],
                    device_id=p, device_id_type=MESH)
                cp.wait_send()
                cp.wait_recv()

    both = [*srcs, *lands]
    res = pl.pallas_call(
        body, name=f"{mode}_wait_{tag}", out_shape=tuple(pltpu.HBM(v.shape, v.dtype) for v in both),
        in_specs=[HBM_SPEC] * len(both) + [SEM_SPEC, SEM_SPEC] + [BS(memory_space=pl.ANY)] * len(afters),
        out_specs=tuple([HBM_SPEC] * len(both)),
        input_output_aliases={i: i for i in range(len(both))},
        compiler_params=pltpu.CompilerParams(has_side_effects=DATAFLOW),
    )(*both, send_sems, recv_sems, *afters)
    return list(res[n_src:])


def _allreduce_small(vec, behind):
    rows = vec.shape[0]

    def body(x_ref, behind_ref, o_ref, buf_ref, send_sems, recv_sems):
        x, y, c = _place()
        me = 4 * x + 2 * y + c
        buf_ref[me] = x_ref[...]
        peers = _peers(x, y, c)

        def copy(k, slot):
            return pltpu.make_async_remote_copy(
                src_ref=x_ref, dst_ref=buf_ref.at[slot], send_sem=send_sems.at[k], recv_sem=recv_sems.at[k],
                device_id=peers[k][0], device_id_type=MESH)

        sends = [copy(k, me) for k in range(N_DEV - 1)]
        for cp in sends:
            cp.start()
        for k in range(N_DEV - 1):
            copy(k, peers[k][1]).wait_recv()
        for cp in sends:
            cp.wait_send()
        acc = buf_ref[0]
        for s in range(1, N_DEV):
            acc = acc + buf_ref[s]
        o_ref[...] = acc

    vmem = BS(memory_space=pltpu.VMEM)
    return pl.pallas_call(
        body, name="allreduce_small", in_specs=[vmem, BS(memory_space=pl.ANY)], out_specs=vmem,
        out_shape=SDS((rows, 128), F32),
        scratch_shapes=[pltpu.VMEM((N_DEV, rows, 128), F32), pltpu.SemaphoreType.DMA((7,)),
                        pltpu.SemaphoreType.DMA((7,))],
        compiler_params=pltpu.CompilerParams(has_side_effects=True),
    )(vec, behind)


def _ffn_forward(x, hn, fetch, names, tag, next_g):
    gu, act = _ffn_up(hn, fetch(names[0], hn).reshape(2, F, D), tag)
    got = _mm_nn(act[None], fetch(names[1], act)[None], f"down_{tag}", res=x, scale=0.5, next_g=next_g)
    out, hn_next = got if next_g is not None else (got, None)
    return out, hn_next, (x, hn, gu, act)


def _ffn_backward(dxo, dxo_b, saved, norm_g, wut, wd, tag, send):
    x, hn, gu, act = saved
    d_wd = _mm_tn(act[None], dxo_b, f"dwd_{tag}", scale=0.5)[0]
    du = _ffn_dact(dxo_b, wd, gu, send(("down",), [d_wd]), tag)
    d_wut = _mm_tn(du, hn, f"dwu_{tag}")
    token = send(("up",), [d_wut.reshape(2 * F, D)])
    return _mm_nn_norm_bwd(du, wut, x, norm_g + token[0, 0], dxo, tag)


def _mixer_forward(x, hn, fetch, bias, tables, tag, next_g):
    proj = _mm_nt_rows(hn, fetch("win", hn), f"proj_{tag}", 512, IN_W // 2, IN_W, 0, rope=(*tables, 2 * QKV_A // 3))
    qkr = proj
    outs, lses = [], []
    for grp in range(3):
        o, l = _dil_fwd(qkr, proj, grp)
        outs.append(o)
        lses.append(l)
    ya = _combine_fwd(outs, lses)
    yb, lse_b = _na_fwd(proj, bias)
    merged, za, zb = _merge_fwd(ya, yb, proj, fetch("wa", yb), fetch("wb", yb))
    out, hn_next = _mm_nn(merged[None], fetch("wo", merged)[None], f"out_{tag}", res=x, next_g=next_g)
    return out, hn_next, (x, hn, proj, qkr, outs, lses, ya, yb, lse_b, merged, za, zb)


def _mixer_backward(dxo, dxo_b, saved, norm_g, w, bias, tables, tag, send):
    wint, wat, wbt, wo = w
    x, hn, proj, qkr, outs, lses, ya, yb, lse_b, merged, za, zb = saved
    dm = _mm_nt_rows(dxo_b, wo, f"dmerged_{tag}", 512, D, D, 0)
    d_wo = _mm_tn(merged[None], dxo_b, f"dwo_{tag}")[0]
    dza, dzb, dlog = _merge_bwd(dm, za, zb, proj)
    dya = _mm_nn(dza[None], wat[None], f"dya_{tag}")
    dyb = _mm_nn(dzb[None], wbt[None], f"dyb_{tag}")
    d_wat = _mm_tn(dza[None], ya, f"dwa_{tag}")[0]
    d_wbt = _mm_tn(dzb[None], yb, f"dwb_{tag}")[0]
    cb = _combine_bwd(dya, outs, lses)
    dqs, dks, dvs = [], [], []
    for grp in range(3):
        dq, dk, dv = _dil_bwd(qkr, proj, cb[grp], cb[3 + grp], lses[grp], grp)
        dqs.append(dq)
        dks.append(dk)
        dvs.append(dv)
    dqk = _rope_bwd(dqs, dks, *tables)
    dqb, dkb, dvb, dbias_tab = _na_bwd(proj, bias, dyb, yb, lse_b)
    dbias = _na_dbias(dbias_tab)
    dproj = jnp.concatenate(
        [dqk] + [t.astype(BF16) for t in (*dvs, dqb, dkb, dvb)] + [dlog[0], dlog[1]], axis=1)
    d_wint = _mm_tn(dproj[None], hn, f"dwin_{tag}")[0]
    token = send(("win", "wa", "wb", "wo"), [d_wint, d_wat, d_wbt, d_wo])
    dx, dx_b, dg = _mm_nn_norm_bwd(dproj[None], wint[None], x, norm_g + token[0, 0], dxo, f"mix_{tag}")
    dbias = dbias[:, 0, :480].reshape(8, 15, 32)[:, :, :31]
    return dx, dx_b, dg, dbias


def _pack_small(norms, biases, final, loss=None):
    parts = []
    for layer in range(DEPTH):
        parts += [norms[0][layer], norms[1][layer], norms[2][layer],
                  jnp.pad(biases[layer].reshape(-1), (0, BIAS_PAD - 8 * 15 * 31))]
    parts.append(final)
    flat = jnp.concatenate([p.reshape(-1).astype(F32) for p in parts])
    if loss is not None:
        flat = jnp.concatenate([flat, loss.reshape(-1)])
    return jnp.pad(flat, (0, SMALL_ROWS * 128 - flat.shape[0])).reshape(SMALL_ROWS, 128)


def _unpack_small(packed):
    flat = packed.reshape(-1)
    norms, biases = ([], [], []), []
    pos = 0
    for _ in range(DEPTH):
        for k in range(3):
            norms[k].append(flat[pos:pos + D])
            pos += D
        biases.append(flat[pos:pos + 8 * 15 * 31].reshape(8, 15, 31))
        pos += BIAS_PAD
    final = flat[pos:pos + D]
    pos += D
    return [jnp.stack(n) for n in norms], jnp.stack(biases), final, flat[pos]


def kernel(x, ffn1_norm, ffn1_w_up, ffn1_w_down, mix_norm, w_in, na_rel_bias, w_branch_a, w_branch_b, w_out, ffn2_norm, ffn2_w_up, ffn2_w_down, final_norm, loss_target, m_ffn1_norm, m_ffn1_w_up, m_ffn1_w_down, m_mix_norm, m_w_in, m_na_rel_bias, m_w_branch_a, m_w_branch_b, m_w_out, m_ffn2_norm, m_ffn2_w_up, m_ffn2_w_down, m_final_norm, v_ffn1_norm, v_ffn1_w_up, v_ffn1_w_down, v_mix_norm, v_w_in, v_na_rel_bias, v_w_branch_a, v_w_branch_b, v_w_out, v_ffn2_norm, v_ffn2_w_up, v_ffn2_w_down, v_final_norm):
    t = x.shape[0] * x.shape[1]
    xs = x.reshape(t, D)
    tgt = loss_target.reshape(t, D)
    tables = _rope_tables()

    col_sharded = dict(up1=ffn1_w_up, win=w_in, wa=w_branch_a, wb=w_branch_b, up2=ffn2_w_up)
    row_sharded = dict(down1=ffn1_w_down, wo=w_out, down2=ffn2_w_down)
    shard = [{} for _ in range(DEPTH)]
    for layer in range(DEPTH):
        for name, arr in col_sharded.items():
            shard[layer][name] = arr[layer].T.astype(BF16)
        for name, arr in row_sharded.items():
            shard[layer][name] = arr[layer].astype(BF16)

    weights = [{} for _ in range(DEPTH)]
    travel = [(0, ("up1",)), (0, ("down1",)), (0, ("win",)), (0, ("wa", "wb", "wo")), (0, ("up2", "down2")),
              (1, ("up1", "down1")), (1, ("win",)), (1, ("wa", "wb", "wo")), (1, ("up2", "down2"))]
    group_of, chips_done, sibling_done = {}, {}, {}
    after = None
    for i, (layer, names) in enumerate(travel):
        chips_done[i], after = _exchange_start("gather", [shard[layer][n] for n in names], None, after, f"w{i}")
        for n in names:
            group_of[layer, n] = (i, names)
    zero = after[0, 0]

    biases = [_na_bias_table(na_rel_bias[layer]) for layer in range(DEPTH)]

    def pass_on(i, behind):
        if i in chips_done:
            lands = _exchange_wait(chips_done.pop(i), behind, f"w{i}")
            sibling_done[i], _ = _exchange_start("forward", [], lands, None, f"p{i}")

    def fetcher(layer):
        def fetch(name, behind):
            if (layer, name) in group_of:
                i, names = group_of[layer, name]
                if i == 0:
                    behind = (behind, *biases)
                pass_on(i, behind)
                pass_on(i + 1, behind)
                for n, got in zip(names, _exchange_wait(sibling_done.pop(i), behind, f"p{i}")):
                    weights[layer][n] = got
                    del group_of[layer, n]
            return weights[layer][name]
        return fetch

    saved = []
    h = xs
    hn = _norm_fwd(xs, ffn1_norm[0] + zero, "first")
    for layer in range(DEPTH):
        bias = biases[layer]
        fetch = fetcher(layer)
        after_ffn2 = ffn1_norm[layer + 1] if layer + 1 < DEPTH else None
        h, hn, s1 = _ffn_forward(h, hn, fetch, ("up1", "down1"), f"f1l{layer}", mix_norm[layer])
        h, hn, s2 = _mixer_forward(h, hn, fetch, bias, tables, f"l{layer}", ffn2_norm[layer])
        h, hn, s3 = _ffn_forward(h, hn, fetch, ("up2", "down2"), f"f2l{layer}", after_ffn2)
        saved.append((s1, s2, s3, bias))
    loss_part, dh, dh_b, d_final = _loss_head(h, final_norm, tgt)

    d_norms = ([None] * DEPTH, [None] * DEPTH, [None] * DEPTH)
    d_bias = [None] * DEPTH
    sent = {}

    def sender(layer, suffix):
        def send(names, grads):
            tag = f"g{layer}{names[0]}{suffix}"
            handle, token = _exchange_start("scatter", grads, None, None, tag)
            for i, n in enumerate(names):
                sent[layer, n + suffix] = (handle, i, tag)
            return token
        return send

    for layer in reversed(range(DEPTH)):
        w = weights[layer]
        s1, s2, s3, bias = saved[layer]
        dh, dh_b, d_norms[2][layer] = _ffn_backward(
            dh, dh_b, s3, ffn2_norm[layer], w["up2"].reshape(2, F, D), w["down2"], f"f2l{layer}", sender(layer, "2"))
        dh, dh_b, d_norms[1][layer], d_bias[layer] = _mixer_backward(
            dh, dh_b, s2, mix_norm[layer], (w["win"], w["wa"], w["wb"], w["wo"]), bias, tables, f"l{layer}",
            sender(layer, ""))
        dh, dh_b, d_norms[0][layer] = _ffn_backward(
            dh, dh_b, s1, ffn1_norm[layer], w["up1"].reshape(2, F, D), w["down1"], f"f1l{layer}", sender(layer, "1"))
    grad_x = dh.reshape(x.shape)

    originals = dict(up1=(ffn1_w_up, m_ffn1_w_up, v_ffn1_w_up), down1=(ffn1_w_down, m_ffn1_w_down, v_ffn1_w_down),
                     win=(w_in, m_w_in, v_w_in), wa=(w_branch_a, m_w_branch_a, v_w_branch_a),
                     wb=(w_branch_b, m_w_branch_b, v_w_branch_b), wo=(w_out, m_w_out, v_w_out),
                     up2=(ffn2_w_up, m_ffn2_w_up, v_ffn2_w_up), down2=(ffn2_w_down, m_ffn2_w_down, v_ffn2_w_down))
    big = {}
    behind = dh
    landed = {}

    def received(layer, name):
        handle, i, tag = sent[layer, name]
        if tag not in landed:
            landed[tag] = _exchange_wait(handle, behind, tag)
        return landed[tag][i]

    for name in ("down2", "up2", "win", "wa", "wb", "wo", "down1", "up1"):
        g = _sum_slots(received(0, name), received(1, name), name)
        wv, mv, vv = originals[name]
        if name in col_sharded:
            wv, mv, vv = (jnp.swapaxes(t, 1, 2) for t in (wv, mv, vv))
        big[name] = (g, *_adamw(wv, g, mv, vv, name))
        behind = big[name][1]
        if name in col_sharded:
            big[name] = tuple(jnp.swapaxes(t, 1, 2) for t in big[name])

    small = _allreduce_small(_pack_small(d_norms, d_bias, d_final, loss_part[0, :1]), behind)
    g_norms, g_bias, g_final, loss = _unpack_small(small)
    w_small = _pack_small((ffn1_norm, mix_norm, ffn2_norm), na_rel_bias, final_norm)
    m_small = _pack_small((m_ffn1_norm, m_mix_norm, m_ffn2_norm), m_na_rel_bias, m_final_norm)
    v_small = _pack_small((v_ffn1_norm, v_mix_norm, v_ffn2_norm), v_na_rel_bias, v_final_norm)
    upd = _adamw(w_small[None], small[None], m_small[None], v_small[None], "small")
    small_out = [(g_norms, g_bias, g_final)] + [_unpack_small(u[0])[:3] for u in upd]

    outputs = [loss, grad_x]
    for kind in range(4):
        norms, bias_k, final_k = small_out[kind]
        outputs += [norms[0], big["up1"][kind], big["down1"][kind], norms[1], big["win"][kind], bias_k,
                    big["wa"][kind], big["wb"][kind], big["wo"][kind], norms[2], big["up2"][kind],
                    big["down2"][kind], final_k]
    return tuple(outputs)
```

```python
import numpy as np

import jax
import jax.numpy as jnp
from jax import lax
from jax.experimental import pallas as pl
from jax.experimental.pallas import tpu as pltpu

F32 = jnp.float32
BF16 = jnp.bfloat16
SDS = jax.ShapeDtypeStruct
BS = pl.BlockSpec
MESH = pl.DeviceIdType.MESH

D = 1024
S = 2048
F = 2816
DEPTH = 2
HEAD_DIM = 64
DILATIONS = (1, 4, 16)
HALF = 64
QKV_A = 2304
QKV_B = 1536
IN_W = 5888
N_DEV = 8
NA_ROWS = 32
GRID_W = 64
NA_KR = 8
ROPE_THETA = 10000.0
RMS_EPS = 1e-6
NEG = -1e30
SCALE = HEAD_DIM ** -0.5
ADAM_LR, ADAM_B1, ADAM_B2, ADAM_EPS, ADAM_WD, ADAM_STEP = 0.001, 0.9, 0.999, 1e-08, 0.01, 10
VMEM_LIMIT_V7X = 52 * 1024 * 1024
SMALL_ROWS = 120
BIAS_PAD = 3840
NA_FWD_ROWS = 4
NA_BWD_ROWS = 4
DIL_FWD_TILES = 4
DIL_BWD_TILES = 4


def _cp(*sem):
    return pltpu.CompilerParams(dimension_semantics=sem, vmem_limit_bytes=VMEM_LIMIT_V7X)


def _dot_nn(a, b):
    return jnp.dot(a, b, preferred_element_type=F32)


def _dot_nt(a, b):
    return lax.dot_general(a, b, (((1,), (1,)), ((), ())), preferred_element_type=F32)


def _dot_tn(a, b):
    return lax.dot_general(a, b, (((0,), (0,)), ((), ())), preferred_element_type=F32)


def _ds(start, size, stride):
    return pl.ds(start, size) if stride == 1 else pl.ds(start, size, stride=stride)


def _norm_fwd(x, g, tag):
    t = x.shape[0]
    tm = 512

    def body(x_ref, g_ref, o_ref):
        xv = x_ref[...]
        r = lax.rsqrt(jnp.mean(xv * xv, axis=-1, keepdims=True) + RMS_EPS)
        o_ref[...] = (xv * r * g_ref[...]).astype(BF16)

    return pl.pallas_call(
        body, name=f"norm_fwd_{tag}", grid=(t // tm,),
        in_specs=[BS((tm, D), lambda i: (i, 0)), BS((1, D), lambda i: (0, 0))],
        out_specs=BS((tm, D), lambda i: (i, 0)),
        out_shape=SDS((t, D), BF16), compiler_params=_cp("parallel"),
    )(x, g.reshape(1, D))


def _loss_head(x, g, tgt):
    t = x.shape[0]
    tm = 512

    def body(x_ref, g_ref, t_ref, loss_ref, dx_ref, dxb_ref, dg_ref):
        @pl.when(pl.program_id(0) == 0)
        def _():
            dg_ref[...] = jnp.zeros_like(dg_ref)
            loss_ref[...] = jnp.zeros_like(loss_ref)

        xv = x_ref[...]
        gv = g_ref[...]
        r = lax.rsqrt(jnp.mean(xv * xv, axis=-1, keepdims=True) + RMS_EPS)
        xh = xv * r
        e = xh * gv - t_ref[...]
        loss_ref[...] += 0.5 * jnp.sum(jnp.mean(e * e, axis=-1, keepdims=True), axis=0, keepdims=True)
        dy = e * (1.0 / D)
        u = dy * gv
        dx = r * (u - xh * jnp.mean(xh * u, axis=-1, keepdims=True))
        dx_ref[...] = dx
        dxb_ref[...] = dx.astype(BF16)
        dg_ref[...] += jnp.sum(dy * xh, axis=0, keepdims=True)

    row = BS((tm, D), lambda i: (i, 0))
    vec = BS((1, D), lambda i: (0, 0))
    return pl.pallas_call(
        body, name="loss_head", grid=(t // tm,),
        in_specs=[row, vec, row], out_specs=[BS((1, 128), lambda i: (0, 0)), row, row, vec],
        out_shape=[SDS((1, 128), F32), SDS((t, D), F32), SDS((t, D), BF16), SDS((1, D), F32)],
        compiler_params=_cp("arbitrary"),
    )(x, g.reshape(1, D), tgt)


def _mm_nn(a, w, tag, res=None, scale=1.0, tm=512, tn=None, next_g=None):
    c_n, t, k = a.shape
    n = w.shape[2]
    tn = n if tn is None else tn
    assert next_g is None or tn == n
    n_in = 2 + (res is not None) + (next_g is not None)

    def body(*refs):
        a_ref, w_ref = refs[0], refs[1]
        acc = _dot_nn(a_ref[0].astype(BF16), w_ref[0])
        for c in range(1, c_n):
            acc = acc + _dot_nn(a_ref[c].astype(BF16), w_ref[c])
        if scale != 1.0:
            acc = acc * scale
        if res is not None:
            acc = refs[2][...] + acc
        refs[n_in][...] = acc
        if next_g is not None:
            r = lax.rsqrt(jnp.mean(acc * acc, axis=-1, keepdims=True) + RMS_EPS)
            refs[n_in + 1][...] = (acc * r * refs[n_in - 1][...]).astype(BF16)

    in_specs = [BS((c_n, tm, k), lambda i, j: (0, i, 0)), BS((c_n, k, tn), lambda i, j: (0, 0, j))]
    args = [a, w]
    out_specs = [BS((tm, tn), lambda i, j: (i, j))]
    out_shape = [SDS((t, n), F32)]
    if res is not None:
        in_specs.append(BS((tm, tn), lambda i, j: (i, j)))
        args.append(res)
    if next_g is not None:
        in_specs.append(BS((1, n), lambda i, j: (0, 0)))
        args.append(next_g.reshape(1, n))
        out_specs.append(BS((tm, tn), lambda i, j: (i, j)))
        out_shape.append(SDS((t, n), BF16))
    got = pl.pallas_call(
        body, name=f"mm_nn_{tag}", grid=(t // tm, n // tn), in_specs=in_specs, out_specs=out_specs,
        out_shape=out_shape, compiler_params=_cp("parallel", "parallel"),
    )(*args)
    return got if next_g is not None else got[0]


def _mm_nn_norm_bwd(a, w, x, g, dres, tag, tm=256):
    c_n, t, k = a.shape

    def body(a_ref, w_ref, x_ref, g_ref, dr_ref, dx_ref, dxb_ref, dg_ref):
        @pl.when(pl.program_id(0) == 0)
        def _():
            dg_ref[...] = jnp.zeros_like(dg_ref)

        dh = _dot_nn(a_ref[0], w_ref[0])
        for c in range(1, c_n):
            dh = dh + _dot_nn(a_ref[c], w_ref[c])
        xv = x_ref[...]
        r = lax.rsqrt(jnp.mean(xv * xv, axis=-1, keepdims=True) + RMS_EPS)
        xh = xv * r
        u = dh * g_ref[...]
        dx = dr_ref[...] + r * (u - xh * jnp.mean(xh * u, axis=-1, keepdims=True))
        dx_ref[...] = dx
        dxb_ref[...] = dx.astype(BF16)
        dg_ref[...] += jnp.sum(dh * xh, axis=0, keepdims=True)

    row = BS((tm, D), lambda i: (i, 0))
    vec = BS((1, D), lambda i: (0, 0))
    return pl.pallas_call(
        body, name=f"mm_nn_norm_bwd_{tag}", grid=(t // tm,),
        in_specs=[BS((c_n, tm, k), lambda i: (0, i, 0)), BS((c_n, k, D), lambda i: (0, 0, 0)), row, vec, row],
        out_specs=[row, row, vec], out_shape=[SDS((t, D), F32), SDS((t, D), BF16), SDS((1, D), F32)],
        compiler_params=_cp("arbitrary"),
    )(a, w, x, g.reshape(1, D), dres)


def _mm_nt_rows(a, w, tag, tm, tn, n_total, w_row0, rope=None):
    t, k = a.shape
    assert w_row0 % tn == 0 and n_total % tn == 0
    j0 = w_row0 // tn

    def body(a_ref, w_ref, *rest):
        o_ref = rest[-1]
        o_ref[...] = _dot_nt(a_ref[...].astype(BF16), w_ref[...])
        if rope is not None:
            @pl.when(pl.program_id(0) == 0)
            def _():
                c = rest[0][...]
                sg = rest[1][...]
                first = (lax.broadcasted_iota(jnp.int32, (tm, 128), 1) % HEAD_DIM) < HEAD_DIM // 2
                for col in range(0, rope[2], 128):
                    v = o_ref[:, col:col + 128]
                    o_ref[:, col:col + 128] = v * c + _swap_halves(v, first) * sg

    in_specs = [BS((tm, k), lambda j, i: (i, 0)), BS((tn, k), lambda j, i: (j0 + j, 0))]
    args = [a, w]
    if rope is not None:
        assert rope[2] <= tn
        in_specs += [BS((tm, 128), lambda j, i: (i % (S // tm), 0))] * 2
        args += [rope[0], rope[1]]
    return pl.pallas_call(
        body, name=f"mm_nt_{tag}", grid=(n_total // tn, t // tm), in_specs=in_specs,
        out_specs=BS((tm, tn), lambda j, i: (i, j)), out_shape=SDS((t, n_total), F32),
        compiler_params=_cp("parallel", "parallel"),
    )(*args)


def _mm_tn(a, b, tag, scale=1.0, tmm=256, tk=None):
    c_n, t, m = a.shape
    n = b.shape[1]
    tk = t if tk is None else tk
    nk = t // tk

    def body_one(a_ref, b_ref, o_ref):
        o_ref[...] = (_dot_tn(a_ref[...].astype(BF16), b_ref[...].astype(BF16)) * scale).astype(BF16)

    def body_acc(a_ref, b_ref, o_ref, acc_ref):
        kk = pl.program_id(2)

        @pl.when(kk == 0)
        def _():
            acc_ref[...] = jnp.zeros_like(acc_ref)

        acc_ref[...] += _dot_tn(a_ref[...].astype(BF16), b_ref[...].astype(BF16))

        @pl.when(kk == nk - 1)
        def _():
            o_ref[...] = (acc_ref[...] * scale).astype(BF16)

    return pl.pallas_call(
        body_one if nk == 1 else body_acc, name=f"mm_tn_{tag}", grid=(c_n, m // tmm, nk),
        in_specs=[BS((None, tk, tmm), lambda c, mi, kk: (c, kk, mi)), BS((tk, n), lambda c, mi, kk: (kk, 0))],
        out_specs=BS((None, tmm, n), lambda c, mi, kk: (c, mi, 0)),
        out_shape=SDS((c_n, m, n), BF16), scratch_shapes=[] if nk == 1 else [pltpu.VMEM((tmm, n), F32)],
        compiler_params=_cp("parallel", "parallel", "arbitrary"),
    )(a, b)


def _ffn_up(hn, wut, tag):
    t = hn.shape[0]
    tm, tn = 512, 1408

    def body(h_ref, w_ref, gu_ref, act_ref):
        h = h_ref[...]
        g = _dot_nt(h, w_ref[0])
        u = _dot_nt(h, w_ref[1])
        sg = jax.nn.sigmoid(g)
        silu = g * sg
        gu_ref[0] = (u * (sg + silu * (1.0 - sg))).astype(BF16)
        gu_ref[1] = silu.astype(BF16)
        act_ref[...] = (silu * u).astype(BF16)

    return pl.pallas_call(
        body, name=f"ffn_up_{tag}", grid=(F // tn, t // tm),
        in_specs=[BS((tm, D), lambda j, i: (i, 0)), BS((2, tn, D), lambda j, i: (0, j, 0))],
        out_specs=[BS((2, tm, tn), lambda j, i: (0, i, j)), BS((tm, tn), lambda j, i: (i, j))],
        out_shape=[SDS((2, t, F), BF16), SDS((t, F), BF16)],
        compiler_params=_cp("parallel", "parallel"),
    )(hn, wut)


def _ffn_dact(dxo, wd, gu, tie, tag):
    t = dxo.shape[0]
    tm, tn = 512, 1408

    def body(d_ref, w_ref, gu_ref, tie_ref, o_ref):
        dact = _dot_nt(d_ref[...] * 0.5, w_ref[...])
        o_ref[0] = (dact * gu_ref[0].astype(F32)).astype(BF16)
        o_ref[1] = (dact * gu_ref[1].astype(F32)).astype(BF16)

    return pl.pallas_call(
        body, name=f"ffn_dact_{tag}", grid=(F // tn, t // tm),
        in_specs=[BS((tm, D), lambda j, i: (i, 0)), BS((tn, D), lambda j, i: (j, 0)),
                  BS((2, tm, tn), lambda j, i: (0, i, j)), BS((8, 128), lambda j, i: (0, 0))],
        out_specs=BS((2, tm, tn), lambda j, i: (0, i, j)),
        out_shape=SDS((2, t, F), BF16), compiler_params=_cp("parallel", "parallel"),
    )(dxo, wd, gu, tie)


def _rope_tables():
    half = HEAD_DIM // 2
    inv_freq = ROPE_THETA ** (-jnp.arange(half, dtype=F32) / half)
    ang = jnp.arange(S).astype(F32)[:, None] * inv_freq[None, :]
    cos, sin = jnp.cos(ang), jnp.sin(ang)
    return jnp.concatenate([cos, cos, cos, cos], axis=1), jnp.concatenate([-sin, sin, -sin, sin], axis=1)


def _swap_halves(t, first_half):
    return jnp.where(first_half, pltpu.roll(t, 96, 1), pltpu.roll(t, 32, 1))


def _rope_bwd(dqs, dks, cos_t, sin_t):
    t = dqs[0].shape[0]
    tm = 512

    def body(*refs):
        c = refs[6][...]
        sg = refs[7][...]
        o_ref = refs[8]
        first = (lax.broadcasted_iota(jnp.int32, (tm, 128), 1) % HEAD_DIM) < HEAD_DIM // 2
        for a in range(6):
            for hp in range(2):
                v = refs[a][:, 128 * hp:128 * (hp + 1)]
                col = 128 * (2 * a + hp)
                o_ref[:, col:col + 128] = (v * c + _swap_halves(v * sg, first)).astype(BF16)

    blk = BS((tm, 256), lambda i: (i, 0))
    tab = BS((tm, 128), lambda i: (i % (S // tm), 0))
    return pl.pallas_call(
        body, name="rope_bwd", grid=(t // tm,), in_specs=[blk] * 6 + [tab, tab],
        out_specs=BS((tm, 1536), lambda i: (i, 0)), out_shape=SDS((t, 1536), BF16),
        compiler_params=_cp("parallel"),
    )(*dqs, *dks, cos_t, sin_t)


def _head_masks():
    lane = lax.broadcasted_iota(jnp.int32, (1, 128), 1)
    m0 = (lane < HEAD_DIM).astype(F32)
    return m0, 1.0 - m0


def _dil_geometry(d):
    sub = S // d
    q_rows = 128
    k_rows = min(256, sub)
    return sub, q_rows, sub // q_rows, k_rows


def _dil_tile(idx, d):
    sub, q_rows, nb, k_rows = _dil_geometry(d)
    r = idx // nb
    n = idx % nb
    k_sub = jnp.clip(q_rows * n - HALF, 0, sub - k_rows)
    if d == 1:
        q_start = pl.multiple_of(q_rows * n, q_rows)
        k_start = pl.multiple_of(k_sub, HALF)
    else:
        q_start = q_rows * n * d + r
        k_start = k_sub * d + r
    ii = lax.broadcasted_iota(jnp.int32, (q_rows, k_rows), 0)
    jj = lax.broadcasted_iota(jnp.int32, (q_rows, k_rows), 1)
    valid = jnp.abs(jj - ii + (k_sub - q_rows * n)) <= HALF
    return q_start, k_start, valid


def _dil_specs(grp):
    qs = BS((S, 128), lambda b, hp: (b, 2 * grp + hp))
    ks = BS((S, 128), lambda b, hp: (b, 6 + 2 * grp + hp))
    vs = BS((S, 128), lambda b, hp: (b, 12 + 2 * grp + hp))
    own = BS((S, 128), lambda b, hp: (b, hp))
    return qs, ks, vs, own


def _dil_fwd(qkr, proj, grp):
    t = qkr.shape[0]
    d = DILATIONS[grp]
    _, q_rows, nb, k_rows = _dil_geometry(d)

    def body(q_ref, k_ref, v_ref, o_ref, l_ref):
        masks = _head_masks()

        def step(i0, carry):
            geo = [_dil_tile(i0 * DIL_FWD_TILES + j, d) for j in range(DIL_FWD_TILES)]
            tiles = [(j, h) for j in range(DIL_FWD_TILES) for h in range(2)]
            qs = [q_ref[_ds(g[0], q_rows, d), :] for g in geo]
            kbs = [k_ref[_ds(g[1], k_rows, d), :].astype(BF16) for g in geo]
            ss = [jnp.where(geo[j][2], _dot_nt((qs[j] * masks[h]).astype(BF16), kbs[j]) * SCALE, NEG) for j, h in tiles]
            mxs = [jnp.max(s, axis=1, keepdims=True) for s in ss]
            ps = [jnp.exp(s - mx) for s, mx in zip(ss, mxs)]
            dens = [jnp.sum(p, axis=1, keepdims=True) for p in ps]
            vs = [v_ref[_ds(g[1], k_rows, d), :] for g in geo]
            outs = [_dot_nn(p.astype(BF16), (vs[j] * masks[h]).astype(BF16)) / den
                    for p, den, (j, h) in zip(ps, dens, tiles)]
            for j, g in enumerate(geo):
                o_ref[_ds(g[0], q_rows, d), :] = outs[2 * j] + outs[2 * j + 1]
                l_ref[_ds(g[0], q_rows, d), :] = (
                    (mxs[2 * j] + jnp.log(dens[2 * j])) * masks[0] + (mxs[2 * j + 1] + jnp.log(dens[2 * j + 1])) * masks[1])
            return carry

        lax.fori_loop(0, d * nb // DIL_FWD_TILES, step, 0)

    qs, ks, vs, own = _dil_specs(grp)
    return pl.pallas_call(
        body, name=f"dil_fwd_{grp}", grid=(t // S, 2), in_specs=[qs, ks, vs], out_specs=[own, own],
        out_shape=[SDS((t, 256), F32), SDS((t, 256), F32)], compiler_params=_cp("parallel", "parallel"),
    )(qkr, qkr, proj)


def _dil_bwd(qkr, proj, do, dlp, lse, grp):
    t = qkr.shape[0]
    d = DILATIONS[grp]
    _, q_rows, nb, k_rows = _dil_geometry(d)

    def body(q_ref, k_ref, v_ref, do_ref, dl_ref, l_ref, dq_ref, dk_ref, dv_ref):
        masks = _head_masks()
        dk_ref[...] = jnp.zeros_like(dk_ref)
        dv_ref[...] = jnp.zeros_like(dv_ref)

        def step(i0, carry):
            geo = [_dil_tile(i0 * DIL_BWD_TILES + j, d) for j in range(DIL_BWD_TILES)]
            tiles = [(j, h) for j in range(DIL_BWD_TILES) for h in range(2)]
            q_ds = [_ds(g[0], q_rows, d) for g in geo]
            k_ds = [_ds(g[1], k_rows, d) for g in geo]
            qs = [q_ref[r, :] for r in q_ds]
            ks = [k_ref[r, :] for r in k_ds]
            kbs = [k.astype(BF16) for k in ks]
            vbs = [v_ref[r, :].astype(BF16) for r in k_ds]
            dos = [do_ref[r, :] for r in q_ds]
            dls = [dl_ref[r, :] for r in q_ds]
            lss = [l_ref[r, :] for r in q_ds]
            qhs = [(qs[j] * masks[h]).astype(BF16) for j, h in tiles]
            dohs = [(dos[j] * masks[h]).astype(BF16) for j, h in tiles]
            ss = [jnp.where(geo[j][2], _dot_nt(qh, kbs[j]) * SCALE, NEG) for qh, (j, h) in zip(qhs, tiles)]
            ps = [jnp.exp(s - lss[j][:, HEAD_DIM * h:HEAD_DIM * h + 1]) for s, (j, h) in zip(ss, tiles)]
            dps = [_dot_nt(doh, vbs[j]) for doh, (j, h) in zip(dohs, tiles)]
            dss = [(p * (dp - dls[j][:, HEAD_DIM * h:HEAD_DIM * h + 1])).astype(BF16)
                   for p, dp, (j, h) in zip(ps, dps, tiles)]
            dqs = [_dot_nn(ds, (ks[j] * masks[h]).astype(BF16)) for ds, (j, h) in zip(dss, tiles)]
            dkws = [_dot_tn(ds, qh) for ds, qh in zip(dss, qhs)]
            dvws = [_dot_tn(p.astype(BF16), doh) for p, doh in zip(ps, dohs)]
            for j in range(DIL_BWD_TILES):
                dq_ref[q_ds[j], :] = (dqs[2 * j] + dqs[2 * j + 1]) * SCALE
                dk_ref[k_ds[j], :] += (dkws[2 * j] + dkws[2 * j + 1]) * SCALE
                dv_ref[k_ds[j], :] += dvws[2 * j] + dvws[2 * j + 1]
            return carry

        lax.fori_loop(0, d * nb // DIL_BWD_TILES, step, 0)

    qs, ks, vs, own = _dil_specs(grp)
    return pl.pallas_call(
        body, name=f"dil_bwd_{grp}", grid=(t // S, 2), in_specs=[qs, ks, vs, own, own, own],
        out_specs=[own, own, own], out_shape=[SDS((t, 256), F32)] * 3,
        compiler_params=_cp("parallel", "parallel"),
    )(qkr, qkr, proj, do, dlp, lse)


def _mix_weights(l0, l1, l2):
    mx = jnp.maximum(jnp.maximum(l0, l1), l2)
    e0, e1, e2 = jnp.exp(l0 - mx), jnp.exp(l1 - mx), jnp.exp(l2 - mx)
    den = e0 + e1 + e2
    return e0 / den, e1 / den, e2 / den


def _combine_fwd(outs, lses):
    t = outs[0].shape[0]
    tm = 512

    def body(o0, o1, o2, l0, l1, l2, y_ref):
        w0, w1, w2 = _mix_weights(l0[...], l1[...], l2[...])
        y_ref[...] = w0 * o0[...] + w1 * o1[...] + w2 * o2[...]

    blk = BS((tm, 256), lambda i: (i, 0))
    return pl.pallas_call(
        body, name="combine_fwd", grid=(t // tm,), in_specs=[blk] * 6, out_specs=blk,
        out_shape=SDS((t, 256), F32), compiler_params=_cp("parallel"),
    )(*outs, *lses)


def _head_sum(x):
    a = lax.broadcasted_iota(jnp.int32, (256, 256), 0) // HEAD_DIM
    b = lax.broadcasted_iota(jnp.int32, (256, 256), 1) // HEAD_DIM
    ones = (a == b).astype(BF16)
    hi = x.astype(BF16)
    lo = (x - hi.astype(F32)).astype(BF16)
    return _dot_nn(hi, ones) + _dot_nn(lo, ones)


def _combine_bwd(dya, outs, lses):
    t = dya.shape[0]
    tm = 512

    def body(dy_ref, o0, o1, o2, l0, l1, l2, d0, d1, d2, e0, e1, e2):
        ws = _mix_weights(l0[...], l1[...], l2[...])
        dy = dy_ref[...]
        ya = ws[0] * o0[...] + ws[1] * o1[...] + ws[2] * o2[...]
        hs = _head_sum(dy * ya)
        for w, d_ref, e_ref in zip(ws, (d0, d1, d2), (e0, e1, e2)):
            d_ref[...] = w * dy
            e_ref[...] = w * hs

    blk = BS((tm, 256), lambda i: (i, 0))
    return pl.pallas_call(
        body, name="combine_bwd", grid=(t // tm,), in_specs=[blk] * 7, out_specs=[blk] * 6,
        out_shape=[SDS((t, 256), F32)] * 6, compiler_params=_cp("parallel"),
    )(dya, *outs, *lses)


def _na_bias_table(rel_bias):
    qc = np.arange(GRID_W)[:, None]
    kc = np.arange(GRID_W)[None, :]
    win_lo = np.clip(qc - 8, 0, GRID_W - 16)
    col_valid = (kc >= win_lo) & (kc < win_lo + 16)
    col_idx = np.clip(kc - qc + 15, 0, 30)
    row_idx = np.arange(NA_KR)[:, None] + np.arange(NA_KR)[None, :]
    rows = (row_idx[..., None] == np.arange(2 * NA_KR - 1)).astype(np.float32)
    cols = (col_idx[..., None] == np.arange(31)).astype(np.float32)
    b = jnp.einsum("hrd,ckr,qjd->ckjhq", rel_bias.astype(F32), rows, cols, precision=lax.Precision.HIGHEST)
    b = jnp.where(col_valid.T[None, None, :, None, :], b, NEG)
    return b.reshape(NA_KR, NA_KR * GRID_W, 4, 128).transpose(2, 0, 1, 3)


def _na_row(i):
    lo = jnp.clip(i - NA_KR // 2, 0, NA_ROWS - NA_KR)
    return pl.multiple_of(GRID_W * i, GRID_W), pl.multiple_of(GRID_W * lo, GRID_W), lo - i + NA_KR - 1


def _both_heads(x, masks):
    return jnp.concatenate([x * masks[0], x * masks[1]], axis=0)


def _own_heads(r, masks):
    half = r.shape[0] // 2
    return r[:half] * masks[0] + r[half:] * masks[1]


def _na_fwd(proj, bias):
    t = proj.shape[0]
    kw = NA_KR * GRID_W

    def body(q_ref, k_ref, v_ref, b_ref, o_ref, l_ref):
        masks = _head_masks()

        def step(i0, carry):
            idx = [i0 * NA_FWD_ROWS + j for j in range(NA_FWD_ROWS)]
            rows = [_na_row(i) for i in idx]
            qbs = [_both_heads(q_ref[pl.ds(r[0], GRID_W), :], masks).astype(BF16) for r in rows]
            kbs = [k_ref[pl.ds(r[1], kw), :].astype(BF16) for r in rows]
            ss = [_dot_nt(kb, qb) * SCALE + b_ref[r[2]] for kb, qb, r in zip(kbs, qbs, rows)]
            mxs = [jnp.max(s, axis=0, keepdims=True) for s in ss]
            ps = [jnp.exp(s - mx) for s, mx in zip(ss, mxs)]
            dens = [jnp.sum(p, axis=0, keepdims=True) for p in ps]
            pbs = [(p / den).astype(BF16) for p, den in zip(ps, dens)]
            vbs = [v_ref[pl.ds(r[1], kw), :].astype(BF16) for r in rows]
            outs = [_own_heads(_dot_tn(pb, vb), masks) for pb, vb in zip(pbs, vbs)]
            for j, r in enumerate(rows):
                o_ref[pl.ds(r[0], GRID_W), :] = outs[j]
                l_ref[pl.ds(idx[j], 1), :] = mxs[j] + jnp.log(dens[j])
            return carry

        lax.fori_loop(0, NA_ROWS // NA_FWD_ROWS, step, 0)

    c0 = QKV_A // 128
    return pl.pallas_call(
        body, name="na_fwd", grid=(t // S, 4),
        in_specs=[BS((S, 128), lambda b, hp: (b, c0 + hp)), BS((S, 128), lambda b, hp: (b, c0 + 4 + hp)),
                  BS((S, 128), lambda b, hp: (b, c0 + 8 + hp)),
                  BS((None, NA_KR, kw, 128), lambda b, hp: (hp, 0, 0, 0))],
        out_specs=[BS((S, 128), lambda b, hp: (b, hp)), BS((None, None, NA_ROWS, 128), lambda b, hp: (b, hp, 0, 0))],
        out_shape=[SDS((t, 512), F32), SDS((t // S, 4, NA_ROWS, 128), F32)],
        compiler_params=_cp("parallel", "parallel"),
    )(proj, proj, proj, bias)


def _na_bwd(proj, bias, dyb, yb, lse):
    t = proj.shape[0]
    kw = NA_KR * GRID_W

    def body(q_ref, k_ref, v_ref, b_ref, do_ref, o_ref, l_ref, dq_ref, dk_ref, dv_ref, db_ref):
        masks = _head_masks()
        ones = jnp.ones((8, 128), BF16)

        @pl.when(pl.program_id(1) == 0)
        def _():
            db_ref[...] = jnp.zeros_like(db_ref)

        dk_ref[...] = jnp.zeros_like(dk_ref)
        dv_ref[...] = jnp.zeros_like(dv_ref)

        def row_sums(x):
            hi = x.astype(BF16)
            lo = (x - hi.astype(F32)).astype(BF16)
            return (_dot_nt(ones, hi) + _dot_nt(ones, lo))[0:1]

        def step(i0, carry):
            idx = [i0 * NA_BWD_ROWS + j for j in range(NA_BWD_ROWS)]
            rows = [_na_row(i) for i in idx]
            q_ds = [pl.ds(r[0], GRID_W) for r in rows]
            k_ds = [pl.ds(r[1], kw) for r in rows]
            qbs = [_both_heads(q_ref[r, :], masks).astype(BF16) for r in q_ds]
            kbs = [k_ref[r, :].astype(BF16) for r in k_ds]
            vbs = [v_ref[r, :].astype(BF16) for r in k_ds]
            dos = [do_ref[r, :] for r in q_ds]
            dobs = [_both_heads(do, masks).astype(BF16) for do in dos]
            deltas = [row_sums(_both_heads(do * o_ref[r, :], masks)) for do, r in zip(dos, q_ds)]
            ss = [_dot_nt(kb, qb) * SCALE + b_ref[r[2]] for kb, qb, r in zip(kbs, qbs, rows)]
            ps = [jnp.exp(s - l_ref[pl.ds(i, 1), :]) for s, i in zip(ss, idx)]
            dps = [_dot_nt(vb, dob) for vb, dob in zip(vbs, dobs)]
            dss = [p * (dp - delta) for p, dp, delta in zip(ps, dps, deltas)]
            for ds, r in zip(dss, rows):
                db_ref[r[2]] += ds
            dsbs = [ds.astype(BF16) for ds in dss]
            dks = [_dot_nn(dsb, qb) for dsb, qb in zip(dsbs, qbs)]
            dvs = [_dot_nn(p.astype(BF16), dob) for p, dob in zip(ps, dobs)]
            dqs = [_own_heads(_dot_tn(dsb, kb), masks) for dsb, kb in zip(dsbs, kbs)]
            for j in range(NA_BWD_ROWS):
                dq_ref[q_ds[j], :] = dqs[j] * SCALE
                dk_ref[k_ds[j], :] += dks[j] * SCALE
                dv_ref[k_ds[j], :] += dvs[j]
            return carry

        lax.fori_loop(0, NA_ROWS // NA_BWD_ROWS, step, 0)

    c0 = QKV_A // 128
    own = BS((S, 128), lambda hp, b: (b, hp))
    tab = BS((None, NA_KR, kw, 128), lambda hp, b: (hp, 0, 0, 0))
    return pl.pallas_call(
        body, name="na_bwd", grid=(4, t // S),
        in_specs=[BS((S, 128), lambda hp, b: (b, c0 + hp)), BS((S, 128), lambda hp, b: (b, c0 + 4 + hp)),
                  BS((S, 128), lambda hp, b: (b, c0 + 8 + hp)), tab, own, own,
                  BS((None, None, NA_ROWS, 128), lambda hp, b: (b, hp, 0, 0))],
        out_specs=[own, own, own, tab],
        out_shape=[SDS((t, 512), F32)] * 3 + [SDS((4, NA_KR, kw, 128), F32)],
        compiler_params=_cp("parallel", "arbitrary"),
    )(proj, proj, proj, bias, dyb, yb, lse)


def _na_dbias_lane_map():
    kw = NA_KR * GRID_W
    lane = np.arange(kw)
    blk, m = lane // GRID_W, lane % GRID_W
    target = np.full(kw, -1)
    target[m < 16] = (blk * 32 + 15 + m)[m < 16]
    target[m >= 49] = (((blk + 1) % NA_KR) * 32 + m - 49)[m >= 49]
    return jnp.asarray(target[:, None] == np.arange(kw)[None, :], BF16)


def _na_dbias(db):
    kw = NA_KR * GRID_W

    def body(x_ref, map_ref, o_ref, z_ref):
        for cls in range(NA_KR):
            xt = x_ref[cls].T
            for h in range(2):
                xv = xt[GRID_W * h:GRID_W * (h + 1)]
                y = xv[0:8]
                for g in range(1, GRID_W // 8):
                    y = y + pltpu.roll(xv[8 * g:8 * g + 8], kw - 8 * g, 1)
                d = y[0:1]
                for s in range(1, 8):
                    d = d + pltpu.roll(y[s:s + 1], kw - s, 1)
                z_ref[h, cls:cls + 1, :] = d
        for h in range(2):
            z = z_ref[h]
            hi = z.astype(BF16)
            lo = (z - hi.astype(F32)).astype(BF16)
            e = _dot_nn(hi, map_ref[...]) + _dot_nn(lo, map_ref[...])
            out = e[0:1]
            for cls in range(1, NA_KR):
                out = out + pltpu.roll(e[cls:cls + 1], 32 * cls, 1)
            o_ref[h] = jnp.broadcast_to(out, (8, kw))

    return pl.pallas_call(
        body, name="na_dbias", grid=(4,),
        in_specs=[BS((None, NA_KR, kw, 128), lambda hp: (hp, 0, 0, 0)), BS((kw, kw), lambda hp: (0, 0))],
        out_specs=BS((2, 8, kw), lambda hp: (hp, 0, 0)), out_shape=SDS((8, 8, kw), F32),
        scratch_shapes=[pltpu.VMEM((2, 8, kw), F32)], compiler_params=_cp("parallel"),
    )(db, _na_dbias_lane_map())


def _merge_fwd(ya, yb, proj, wat, wbt):
    t = ya.shape[0]
    tm, tn = 512, 256
    ca = (QKV_A + QKV_B) // tn
    cb = ca + D // tn

    def body(ya_ref, yb_ref, la_ref, lb_ref, wa_ref, wb_ref, m_ref, za_ref, zb_ref):
        za = _dot_nt(ya_ref[...].astype(BF16), wa_ref[...])
        zb = _dot_nt(yb_ref[...].astype(BF16), wb_ref[...])
        m_ref[...] = (jax.nn.sigmoid(la_ref[...]) * za + jax.nn.sigmoid(lb_ref[...]) * zb).astype(BF16)
        za_ref[...] = za.astype(BF16)
        zb_ref[...] = zb.astype(BF16)

    out = BS((tm, tn), lambda i, j: (i, j))
    return pl.pallas_call(
        body, name="merge_fwd", grid=(t // tm, D // tn),
        in_specs=[BS((tm, 256), lambda i, j: (i, 0)), BS((tm, 512), lambda i, j: (i, 0)),
                  BS((tm, tn), lambda i, j: (i, ca + j)), BS((tm, tn), lambda i, j: (i, cb + j)),
                  BS((tn, 256), lambda i, j: (j, 0)), BS((tn, 512), lambda i, j: (j, 0))],
        out_specs=[out, out, out], out_shape=[SDS((t, D), BF16)] * 3,
        compiler_params=_cp("parallel", "parallel"),
    )(ya, yb, proj, proj, wat, wbt)


def _merge_bwd(dm, za, zb, proj):
    t = dm.shape[0]
    tm, tn = 512, 256
    ca = (QKV_A + QKV_B) // tn
    cb = ca + D // tn

    def body(dm_ref, za_ref, zb_ref, la_ref, lb_ref, dza_ref, dzb_ref, dl_ref):
        dmv = dm_ref[...]
        ga = jax.nn.sigmoid(la_ref[...])
        gb = jax.nn.sigmoid(lb_ref[...])
        dza_ref[...] = (dmv * ga).astype(BF16)
        dzb_ref[...] = (dmv * gb).astype(BF16)
        dl_ref[0] = (dmv * za_ref[...].astype(F32) * ga * (1.0 - ga)).astype(BF16)
        dl_ref[1] = (dmv * zb_ref[...].astype(F32) * gb * (1.0 - gb)).astype(BF16)

    blk = BS((tm, tn), lambda i, j: (i, j))
    return pl.pallas_call(
        body, name="merge_bwd", grid=(t // tm, D // tn),
        in_specs=[blk, blk, blk, BS((tm, tn), lambda i, j: (i, ca + j)), BS((tm, tn), lambda i, j: (i, cb + j))],
        out_specs=[blk, blk, BS((2, tm, tn), lambda i, j: (0, i, j))],
        out_shape=[SDS((t, D), BF16), SDS((t, D), BF16), SDS((2, t, D), BF16)],
        compiler_params=_cp("parallel", "parallel"),
    )(dm, za, zb, proj, proj)


def _sum_slots(recv0, recv1, tag):
    _, r, c = recv0.shape
    tr = r if r * c <= 512 * 1024 else r // 2

    def body(a_ref, b_ref, o_ref):
        for layer, ref in enumerate((a_ref, b_ref)):
            acc = ref[0].astype(F32)
            for s in range(1, N_DEV):
                acc = acc + ref[s].astype(F32)
            o_ref[layer] = acc

    blk = BS((N_DEV, tr, c), lambda i: (0, i, 0))
    return pl.pallas_call(
        body, name=f"sum_slots_{tag}", grid=(r // tr,), in_specs=[blk, blk],
        out_specs=BS((2, tr, c), lambda i: (0, i, 0)), out_shape=SDS((2, r, c), F32),
        compiler_params=_cp("parallel"),
    )(recv0, recv1)


def _adamw(w, g, m, v, tag):
    layers, r, c = w.shape
    tr = next(r // k for k in (1, 2, 4, 8) if r // k <= 384 and r % (8 * k) == 0)

    def body(w_ref, g_ref, m_ref, v_ref, d_ref, mo_ref, vo_ref):
        gv = g_ref[...]
        mn = ADAM_B1 * m_ref[...] + (1.0 - ADAM_B1) * gv
        vn = ADAM_B2 * v_ref[...] + (1.0 - ADAM_B2) * (gv * gv)
        m_hat = mn / (1.0 - ADAM_B1 ** ADAM_STEP)
        v_hat = vn / (1.0 - ADAM_B2 ** ADAM_STEP)
        d_ref[...] = -ADAM_LR * (m_hat / (jnp.sqrt(v_hat) + ADAM_EPS) + ADAM_WD * w_ref[...])
        mo_ref[...] = mn
        vo_ref[...] = vn

    blk = BS((None, tr, c), lambda l, i: (l, i, 0))
    return pl.pallas_call(
        body, name=f"adamw_{tag}", grid=(layers, r // tr), in_specs=[blk] * 4, out_specs=[blk] * 3,
        out_shape=[SDS((layers, r, c), F32)] * 3, compiler_params=_cp("parallel", "parallel"),
    )(w, g, m, v)


def _place():
    return lax.axis_index("x"), lax.axis_index("y"), lax.axis_index("c")


def _flip(coord, bit):
    return 1 - coord if bit else coord


def _peers(x, y, c):
    peers = []
    for mask in range(1, N_DEV):
        p = (_flip(x, mask & 4), _flip(y, mask & 2), _flip(c, mask & 1))
        peers.append((p, 4 * p[0] + 2 * p[1] + p[2]))
    return peers


def _copy_plan(mode, src, land, x, y, c):
    me = 4 * x + 2 * y + c

    def device(mask):
        p = (_flip(x, mask & 4), _flip(y, mask & 2), _flip(c, mask & 1))
        return p, 4 * p[0] + 2 * p[1] + p[2]

    if mode == "scatter":
        r = land.shape[1]
        return [(p, src.at[pl.ds(i * r, r), :], land.at[me], land.at[i])
                for p, i in map(device, (1, 2, 3, 4, 5, 6, 7, 0))]
    r = land.shape[0] // N_DEV

    def rows(i):
        return land.at[pl.ds(i * r, r), :]

    if mode == "gather":
        return [(p, src, rows(me), rows(i)) for p, i in map(device, (1, 4, 2, 6, 0))]
    sibling = device(1)[0]
    return [(sibling, rows(device(m)[1]), rows(device(m)[1]), rows(device(m | 1)[1])) for m in (4, 2, 6)]


COPIES = dict(scatter=8, gather=5, forward=3)
HBM_SPEC = BS(memory_space=pltpu.HBM)
SEM_SPEC = BS(memory_space=pltpu.SEMAPHORE)
DATAFLOW = pltpu.SideEffectType.DATAFLOW_SIDE_EFFECTING


def _fresh(shape, dtype, tag):
    def body(o_ref):
        del o_ref

    return pl.pallas_call(body, name=f"fresh_{tag}", out_specs=BS(memory_space=pl.ANY), out_shape=SDS(shape, dtype))()


def _exchange_start(mode, srcs, lands, after, tag):
    if lands is None and mode == "gather":
        lands = [_fresh((N_DEV * s.shape[0], s.shape[1]), s.dtype, f"{tag}_{a}") for a, s in enumerate(srcs)]
    elif lands is None:
        lands = [_fresh((N_DEV, s.shape[0] // N_DEV, s.shape[1]), s.dtype, f"{tag}_{a}") for a, s in enumerate(srcs)]
    n, n_src, n_cp = len(lands), len(srcs), COPIES[mode]
    behind = [] if after is None else [after]

    def body(*refs):
        src_refs, land_refs = refs[:n_src], refs[n_src:n_src + n]
        send_sems, recv_sems = refs[n_src + n + len(behind)], refs[n_src + n + len(behind) + 1]
        token = refs[-1]
        for a in range(n):
            plan = _copy_plan(mode, src_refs[a] if n_src else None, land_refs[a], *_place())
            for k, (p, out, there, _) in enumerate(plan):
                pltpu.make_async_remote_copy(
                    src_ref=out, dst_ref=there, send_sem=send_sems.at[n_cp * a + k],
                    recv_sem=recv_sems.at[n_cp * a + k], device_id=p, device_id_type=MESH).start()
        token[...] = jnp.zeros_like(token)

    both = [*srcs, *lands]
    res = pl.pallas_call(
        body, name=f"{mode}_start_{tag}",
        out_shape=(pltpu.SemaphoreType.DMA((n_cp * n,)), pltpu.SemaphoreType.DMA((n_cp * n,)),
                   *[pltpu.HBM(v.shape, v.dtype) for v in both], SDS((8, 128), F32)),
        in_specs=[HBM_SPEC] * len(both) + [BS(memory_space=pl.ANY)] * len(behind),
        out_specs=(SEM_SPEC, SEM_SPEC, *[HBM_SPEC] * len(both), BS(memory_space=pltpu.VMEM)),
        input_output_aliases={i: 2 + i for i in range(len(both))},
        compiler_params=pltpu.CompilerParams(has_side_effects=DATAFLOW),
    )(*[pltpu.with_memory_space_constraint(v, pltpu.HBM) for v in both], *behind)
    return (mode, res[0], res[1], res[2:2 + n_src], res[2 + n_src:2 + n_src + n]), res[-1]


def _exchange_wait(handle, after, tag, which=None):
    mode, send_sems, recv_sems, srcs, lands = handle
    which = list(range(len(lands))) if which is None else list(which)
    n_cp = COPIES[mode]
    lands = [lands[a] for a in which]
    srcs = [srcs[a] for a in which] if srcs else []
    n, n_src = len(lands), len(srcs)
    afters = list(after) if isinstance(after, (tuple, list)) else [after]

    def body(*refs):
        src_refs, land_refs = refs[:n_src], refs[n_src:n_src + n]
        send_ref, recv_ref = refs[n_src + n], refs[n_src + n + 1]
        for i, a in enumerate(which):
            plan = _copy_plan(mode, src_refs[i] if n_src else None, land_refs[i], *_place())
            for k, (p, out, _, here) in enumerate(plan):
                cp = pltpu.make_async_remote_copy(
                    src_ref=out, dst_ref=here, send_sem=send_ref.at[n_cp * a + k], recv_sem=recv_ref.at[n_cp * a + k],
                    device_id=p, device_id_type=MESH)
                cp.wait_send()
                cp.wait_recv()

    both = [*srcs, *lands]
    res = pl.pallas_call(
        body, name=f"{mode}_wait_{tag}", out_shape=tuple(pltpu.HBM(v.shape, v.dtype) for v in both),
        in_specs=[HBM_SPEC] * len(both) + [SEM_SPEC, SEM_SPEC] + [BS(memory_space=pl.ANY)] * len(afters),
        out_specs=tuple([HBM_SPEC] * len(both)),
        input_output_aliases={i: i for i in range(len(both))},
        compiler_params=pltpu.CompilerParams(has_side_effects=DATAFLOW),
    )(*both, send_sems, recv_sems, *afters)
    return list(res[n_src:])


def _allreduce_small(vec, behind):
    rows = vec.shape[0]

    def body(x_ref, behind_ref, o_ref, buf_ref, send_sems, recv_sems):
        x, y, c = _place()
        me = 4 * x + 2 * y + c
        buf_ref[me] = x_ref[...]
        peers = _peers(x, y, c)

        def copy(k, slot):
            return pltpu.make_async_remote_copy(
                src_ref=x_ref, dst_ref=buf_ref.at[slot], send_sem=send_sems.at[k], recv_sem=recv_sems.at[k],
                device_id=peers[k][0], device_id_type=MESH)

        sends = [copy(k, me) for k in range(N_DEV - 1)]
        for cp in sends:
            cp.start()
        for k in range(N_DEV - 1):
            copy(k, peers[k][1]).wait_recv()
        for cp in sends:
            cp.wait_send()
        acc = buf_ref[0]
        for s in range(1, N_DEV):
            acc = acc + buf_ref[s]
        o_ref[...] = acc

    vmem = BS(memory_space=pltpu.VMEM)
    return pl.pallas_call(
        body, name="allreduce_small", in_specs=[vmem, BS(memory_space=pl.ANY)], out_specs=vmem,
        out_shape=SDS((rows, 128), F32),
        scratch_shapes=[pltpu.VMEM((N_DEV, rows, 128), F32), pltpu.SemaphoreType.DMA((7,)),
                        pltpu.SemaphoreType.DMA((7,))],
        compiler_params=pltpu.CompilerParams(has_side_effects=True),
    )(vec, behind)


def _ffn_forward(x, hn, fetch, names, tag, next_g):
    gu, act = _ffn_up(hn, fetch(names[0], hn).reshape(2, F, D), tag)
    got = _mm_nn(act[None], fetch(names[1], act)[None], f"down_{tag}", res=x, scale=0.5, next_g=next_g)
    out, hn_next = got if next_g is not None else (got, None)
    return out, hn_next, (x, hn, gu, act)


def _ffn_backward(dxo, dxo_b, saved, norm_g, wut, wd, tag, send):
    x, hn, gu, act = saved
    d_wd = _mm_tn(act[None], dxo_b, f"dwd_{tag}", scale=0.5)[0]
    du = _ffn_dact(dxo_b, wd, gu, send(("down",), [d_wd]), tag)
    d_wut = _mm_tn(du, hn, f"dwu_{tag}")
    token = send(("up",), [d_wut.reshape(2 * F, D)])
    return _mm_nn_norm_bwd(du, wut, x, norm_g + token[0, 0], dxo, tag)


def _mixer_forward(x, hn, fetch, bias, tables, tag, next_g):
    proj = _mm_nt_rows(hn, fetch("win", hn), f"proj_{tag}", 512, IN_W // 2, IN_W, 0, rope=(*tables, 2 * QKV_A // 3))
    qkr = proj
    outs, lses = [], []
    for grp in range(3):
        o, l = _dil_fwd(qkr, proj, grp)
        outs.append(o)
        lses.append(l)
    ya = _combine_fwd(outs, lses)
    yb, lse_b = _na_fwd(proj, bias)
    merged, za, zb = _merge_fwd(ya, yb, proj, fetch("wa", yb), fetch("wb", yb))
    out, hn_next = _mm_nn(merged[None], fetch("wo", merged)[None], f"out_{tag}", res=x, next_g=next_g)
    return out, hn_next, (x, hn, proj, qkr, outs, lses, ya, yb, lse_b, merged, za, zb)


def _mixer_backward(dxo, dxo_b, saved, norm_g, w, bias, tables, tag, send):
    wint, wat, wbt, wo = w
    x, hn, proj, qkr, outs, lses, ya, yb, lse_b, merged, za, zb = saved
    dm = _mm_nt_rows(dxo_b, wo, f"dmerged_{tag}", 512, D, D, 0)
    d_wo = _mm_tn(merged[None], dxo_b, f"dwo_{tag}")[0]
    dza, dzb, dlog = _merge_bwd(dm, za, zb, proj)
    dya = _mm_nn(dza[None], wat[None], f"dya_{tag}")
    dyb = _mm_nn(dzb[None], wbt[None], f"dyb_{tag}")
    d_wat = _mm_tn(dza[None], ya, f"dwa_{tag}")[0]
    d_wbt = _mm_tn(dzb[None], yb, f"dwb_{tag}")[0]
    cb = _combine_bwd(dya, outs, lses)
    dqs, dks, dvs = [], [], []
    for grp in range(3):
        dq, dk, dv = _dil_bwd(qkr, proj, cb[grp], cb[3 + grp], lses[grp], grp)
        dqs.append(dq)
        dks.append(dk)
        dvs.append(dv)
    dqk = _rope_bwd(dqs, dks, *tables)
    dqb, dkb, dvb, dbias_tab = _na_bwd(proj, bias, dyb, yb, lse_b)
    dbias = _na_dbias(dbias_tab)
    dproj = jnp.concatenate(
        [dqk] + [t.astype(BF16) for t in (*dvs, dqb, dkb, dvb)] + [dlog[0], dlog[1]], axis=1)
    d_wint = _mm_tn(dproj[None], hn, f"dwin_{tag}")[0]
    token = send(("win", "wa", "wb", "wo"), [d_wint, d_wat, d_wbt, d_wo])
    dx, dx_b, dg = _mm_nn_norm_bwd(dproj[None], wint[None], x, norm_g + token[0, 0], dxo, f"mix_{tag}")
    dbias = dbias[:, 0, :480].reshape(8, 15, 32)[:, :, :31]
    return dx, dx_b, dg, dbias


def _pack_small(norms, biases, final, loss=None):
    parts = []
    for layer in range(DEPTH):
        parts += [norms[0][layer], norms[1][layer], norms[2][layer],
                  jnp.pad(biases[layer].reshape(-1), (0, BIAS_PAD - 8 * 15 * 31))]
    parts.append(final)
    flat = jnp.concatenate([p.reshape(-1).astype(F32) for p in parts])
    if loss is not None:
        flat = jnp.concatenate([flat, loss.reshape(-1)])
    return jnp.pad(flat, (0, SMALL_ROWS * 128 - flat.shape[0])).reshape(SMALL_ROWS, 128)


def _unpack_small(packed):
    flat = packed.reshape(-1)
    norms, biases = ([], [], []), []
    pos = 0
    for _ in range(DEPTH):
        for k in range(3):
            norms[k].append(flat[pos:pos + D])
            pos += D
        biases.append(flat[pos:pos + 8 * 15 * 31].reshape(8, 15, 31))
        pos += BIAS_PAD
    final = flat[pos:pos + D]
    pos += D
    return [jnp.stack(n) for n in norms], jnp.stack(biases), final, flat[pos]


def kernel(x, ffn1_norm, ffn1_w_up, ffn1_w_down, mix_norm, w_in, na_rel_bias, w_branch_a, w_branch_b, w_out, ffn2_norm, ffn2_w_up, ffn2_w_down, final_norm, loss_target, m_ffn1_norm, m_ffn1_w_up, m_ffn1_w_down, m_mix_norm, m_w_in, m_na_rel_bias, m_w_branch_a, m_w_branch_b, m_w_out, m_ffn2_norm, m_ffn2_w_up, m_ffn2_w_down, m_final_norm, v_ffn1_norm, v_ffn1_w_up, v_ffn1_w_down, v_mix_norm, v_w_in, v_na_rel_bias, v_w_branch_a, v_w_branch_b, v_w_out, v_ffn2_norm, v_ffn2_w_up, v_ffn2_w_down, v_final_norm):
    t = x.shape[0] * x.shape[1]
    xs = x.reshape(t, D)
    tgt = loss_target.reshape(t, D)
    tables = _rope_tables()

    col_sharded = dict(up1=ffn1_w_up, win=w_in, wa=w_branch_a, wb=w_branch_b, up2=ffn2_w_up)
    row_sharded = dict(down1=ffn1_w_down, wo=w_out, down2=ffn2_w_down)
    shard = [{} for _ in range(DEPTH)]
    for layer in range(DEPTH):
        for name, arr in col_sharded.items():
            shard[layer][name] = arr[layer].T.astype(BF16)
        for name, arr in row_sharded.items():
            shard[layer][name] = arr[layer].astype(BF16)

    weights = [{} for _ in range(DEPTH)]
    travel = [(0, ("up1",)), (0, ("down1",)), (0, ("win",)), (0, ("wa", "wb", "wo")), (0, ("up2", "down2")),
              (1, ("up1", "down1")), (1, ("win",)), (1, ("wa", "wb", "wo")), (1, ("up2", "down2"))]
    group_of, chips_done, sibling_done = {}, {}, {}
    count = 0
    for i, (layer, names) in enumerate(travel):
        chips_done[i] = list(range(count, count + len(names)))
        count += len(names)
        for n in names:
            group_of[layer, n] = (i, names)
    gathered, token = _exchange_start(
        "gather", [shard[layer][n] for layer, names in travel for n in names], None, None, "w")
    zero = token[0, 0]

    biases = [_na_bias_table(na_rel_bias[layer] + zero) for layer in range(DEPTH)]

    def pass_on(i, behind):
        if i in chips_done:
            lands = _exchange_wait(gathered, behind, f"w{i}", which=chips_done.pop(i))
            sibling_done[i], _ = _exchange_start("forward", [], lands, None, f"p{i}")

    def fetcher(layer):
        def fetch(name, behind):
            if (layer, name) in group_of:
                i, names = group_of[layer, name]
                if i == 0:
                    behind = (behind, *biases)
                pass_on(i, behind)
                pass_on(i + 1, behind)
                for n, got in zip(names, _exchange_wait(sibling_done.pop(i), behind, f"p{i}")):
                    weights[layer][n] = got
                    del group_of[layer, n]
            return weights[layer][name]
        return fetch

    saved = []
    h = xs
    hn = _norm_fwd(xs, ffn1_norm[0] + zero, "first")
    for layer in range(DEPTH):
        bias = biases[layer]
        fetch = fetcher(layer)
        after_ffn2 = ffn1_norm[layer + 1] if layer + 1 < DEPTH else None
        h, hn, s1 = _ffn_forward(h, hn, fetch, ("up1", "down1"), f"f1l{layer}", mix_norm[layer])
        h, hn, s2 = _mixer_forward(h, hn, fetch, bias, tables, f"l{layer}", ffn2_norm[layer])
        h, hn, s3 = _ffn_forward(h, hn, fetch, ("up2", "down2"), f"f2l{layer}", after_ffn2)
        saved.append((s1, s2, s3, bias))
    loss_part, dh, dh_b, d_final = _loss_head(h, final_norm, tgt)

    d_norms = ([None] * DEPTH, [None] * DEPTH, [None] * DEPTH)
    d_bias = [None] * DEPTH
    sent = {}

    def sender(layer, suffix):
        def send(names, grads):
            tag = f"g{layer}{names[0]}{suffix}"
            handle, token = _exchange_start("scatter", grads, None, None, tag)
            for i, n in enumerate(names):
                sent[layer, n + suffix] = (handle, i, tag)
            return token
        return send

    for layer in reversed(range(DEPTH)):
        w = weights[layer]
        s1, s2, s3, bias = saved[layer]
        dh, dh_b, d_norms[2][layer] = _ffn_backward(
            dh, dh_b, s3, ffn2_norm[layer], w["up2"].reshape(2, F, D), w["down2"], f"f2l{layer}", sender(layer, "2"))
        dh, dh_b, d_norms[1][layer], d_bias[layer] = _mixer_backward(
            dh, dh_b, s2, mix_norm[layer], (w["win"], w["wa"], w["wb"], w["wo"]), bias, tables, f"l{layer}",
            sender(layer, ""))
        dh, dh_b, d_norms[0][layer] = _ffn_backward(
            dh, dh_b, s1, ffn1_norm[layer], w["up1"].reshape(2, F, D), w["down1"], f"f1l{layer}", sender(layer, "1"))
    grad_x = dh.reshape(x.shape)

    originals = dict(up1=(ffn1_w_up, m_ffn1_w_up, v_ffn1_w_up), down1=(ffn1_w_down, m_ffn1_w_down, v_ffn1_w_down),
                     win=(w_in, m_w_in, v_w_in), wa=(w_branch_a, m_w_branch_a, v_w_branch_a),
                     wb=(w_branch_b, m_w_branch_b, v_w_branch_b), wo=(w_out, m_w_out, v_w_out),
                     up2=(ffn2_w_up, m_ffn2_w_up, v_ffn2_w_up), down2=(ffn2_w_down, m_ffn2_w_down, v_ffn2_w_down))
    big = {}
    behind = dh
    landed = {}

    def received(layer, name):
        handle, i, tag = sent[layer, name]
        if tag not in landed:
            landed[tag] = _exchange_wait(handle, behind, tag)
        return landed[tag][i]

    for name in ("down2", "up2", "win", "wa", "wb", "wo", "down1", "up1"):
        g = _sum_slots(received(0, name), received(1, name), name)
        wv, mv, vv = originals[name]
        if name in col_sharded:
            wv, mv, vv = (jnp.swapaxes(t, 1, 2) for t in (wv, mv, vv))
        big[name] = (g, *_adamw(wv, g, mv, vv, name))
        behind = big[name][1]
        if name in col_sharded:
            big[name] = tuple(jnp.swapaxes(t, 1, 2) for t in big[name])

    small = _allreduce_small(_pack_small(d_norms, d_bias, d_final, loss_part[0, :1]), behind)
    g_norms, g_bias, g_final, loss = _unpack_small(small)
    w_small = _pack_small((ffn1_norm, mix_norm, ffn2_norm), na_rel_bias, final_norm)
    m_small = _pack_small((m_ffn1_norm, m_mix_norm, m_ffn2_norm), m_na_rel_bias, m_final_norm)
    v_small = _pack_small((v_ffn1_norm, v_mix_norm, v_ffn2_norm), v_na_rel_bias, v_final_norm)
    upd = _adamw(w_small[None], small[None], m_small[None], v_small[None], "small")
    small_out = [(g_norms, g_bias, g_final)] + [_unpack_small(u[0])[:3] for u in upd]

    outputs = [loss, grad_x]
    for kind in range(4):
        norms, bias_k, final_k = small_out[kind]
        outputs += [norms[0], big["up1"][kind], big["down1"][kind], norms[1], big["win"][kind], bias_k,
                    big["wa"][kind], big["wb"][kind], big["wo"][kind], norms[2], big["up2"][kind],
                    big["down2"][kind], final_k]
    return tuple(outputs)
```

```python
import numpy as np

import jax
import jax.numpy as jnp
from jax import lax
from jax.experimental import pallas as pl
from jax.experimental.pallas import tpu as pltpu

F32 = jnp.float32
BF16 = jnp.bfloat16
SDS = jax.ShapeDtypeStruct
BS = pl.BlockSpec
MESH = pl.DeviceIdType.MESH

D = 1024
S = 2048
F = 2816
DEPTH = 2
HEAD_DIM = 64
DILATIONS = (1, 4, 16)
HALF = 64
QKV_A = 2304
QKV_B = 1536
IN_W = 5888
N_DEV = 8
NA_ROWS = 32
GRID_W = 64
NA_KR = 8
ROPE_THETA = 10000.0
RMS_EPS = 1e-6
NEG = -1e30
SCALE = HEAD_DIM ** -0.5
ADAM_LR, ADAM_B1, ADAM_B2, ADAM_EPS, ADAM_WD, ADAM_STEP = 0.001, 0.9, 0.999, 1e-08, 0.01, 10
VMEM_LIMIT_V7X = 52 * 1024 * 1024
SMALL_ROWS = 120
BIAS_PAD = 3840
NA_FWD_ROWS = 4
NA_BWD_ROWS = 4
DIL_FWD_TILES = 4
DIL_BWD_TILES = 4


def _cp(*sem):
    return pltpu.CompilerParams(dimension_semantics=sem, vmem_limit_bytes=VMEM_LIMIT_V7X)


def _dot_nn(a, b):
    return jnp.dot(a, b, preferred_element_type=F32)


def _dot_nt(a, b):
    return lax.dot_general(a, b, (((1,), (1,)), ((), ())), preferred_element_type=F32)


def _dot_tn(a, b):
    return lax.dot_general(a, b, (((0,), (0,)), ((), ())), preferred_element_type=F32)


def _ds(start, size, stride):
    return pl.ds(start, size) if stride == 1 else pl.ds(start, size, stride=stride)


def _norm_fwd(x, g, tag):
    t = x.shape[0]
    tm = 512

    def body(x_ref, g_ref, o_ref):
        xv = x_ref[...]
        r = lax.rsqrt(jnp.mean(xv * xv, axis=-1, keepdims=True) + RMS_EPS)
        o_ref[...] = (xv * r * g_ref[...]).astype(BF16)

    return pl.pallas_call(
        body, name=f"norm_fwd_{tag}", grid=(t // tm,),
        in_specs=[BS((tm, D), lambda i: (i, 0)), BS((1, D), lambda i: (0, 0))],
        out_specs=BS((tm, D), lambda i: (i, 0)),
        out_shape=SDS((t, D), BF16), compiler_params=_cp("parallel"),
    )(x, g.reshape(1, D))


def _loss_head(x, g, tgt):
    t = x.shape[0]
    tm = 512

    def body(x_ref, g_ref, t_ref, loss_ref, dx_ref, dxb_ref, dg_ref):
        @pl.when(pl.program_id(0) == 0)
        def _():
            dg_ref[...] = jnp.zeros_like(dg_ref)
            loss_ref[...] = jnp.zeros_like(loss_ref)

        xv = x_ref[...]
        gv = g_ref[...]
        r = lax.rsqrt(jnp.mean(xv * xv, axis=-1, keepdims=True) + RMS_EPS)
        xh = xv * r
        e = xh * gv - t_ref[...]
        loss_ref[...] += 0.5 * jnp.sum(jnp.mean(e * e, axis=-1, keepdims=True), axis=0, keepdims=True)
        dy = e * (1.0 / D)
        u = dy * gv
        dx = r * (u - xh * jnp.mean(xh * u, axis=-1, keepdims=True))
        dx_ref[...] = dx
        dxb_ref[...] = dx.astype(BF16)
        dg_ref[...] += jnp.sum(dy * xh, axis=0, keepdims=True)

    row = BS((tm, D), lambda i: (i, 0))
    vec = BS((1, D), lambda i: (0, 0))
    return pl.pallas_call(
        body, name="loss_head", grid=(t // tm,),
        in_specs=[row, vec, row], out_specs=[BS((1, 128), lambda i: (0, 0)), row, row, vec],
        out_shape=[SDS((1, 128), F32), SDS((t, D), F32), SDS((t, D), BF16), SDS((1, D), F32)],
        compiler_params=_cp("arbitrary"),
    )(x, g.reshape(1, D), tgt)


def _mm_nn(a, w, tag, res=None, scale=1.0, tm=512, tn=None, next_g=None):
    c_n, t, k = a.shape
    n = w.shape[2]
    tn = n if tn is None else tn
    assert next_g is None or tn == n
    n_in = 2 + (res is not None) + (next_g is not None)

    def body(*refs):
        a_ref, w_ref = refs[0], refs[1]
        acc = _dot_nn(a_ref[0].astype(BF16), w_ref[0])
        for c in range(1, c_n):
            acc = acc + _dot_nn(a_ref[c].astype(BF16), w_ref[c])
        if scale != 1.0:
            acc = acc * scale
        if res is not None:
            acc = refs[2][...] + acc
        refs[n_in][...] = acc
        if next_g is not None:
            r = lax.rsqrt(jnp.mean(acc * acc, axis=-1, keepdims=True) + RMS_EPS)
            refs[n_in + 1][...] = (acc * r * refs[n_in - 1][...]).astype(BF16)

    in_specs = [BS((c_n, tm, k), lambda i, j: (0, i, 0)), BS((c_n, k, tn), lambda i, j: (0, 0, j))]
    args = [a, w]
    out_specs = [BS((tm, tn), lambda i, j: (i, j))]
    out_shape = [SDS((t, n), F32)]
    if res is not None:
        in_specs.append(BS((tm, tn), lambda i, j: (i, j)))
        args.append(res)
    if next_g is not None:
        in_specs.append(BS((1, n), lambda i, j: (0, 0)))
        args.append(next_g.reshape(1, n))
        out_specs.append(BS((tm, tn), lambda i, j: (i, j)))
        out_shape.append(SDS((t, n), BF16))
    got = pl.pallas_call(
        body, name=f"mm_nn_{tag}", grid=(t // tm, n // tn), in_specs=in_specs, out_specs=out_specs,
        out_shape=out_shape, compiler_params=_cp("parallel", "parallel"),
    )(*args)
    return got if next_g is not None else got[0]


def _mm_nn_norm_bwd(a, w, x, g, dres, tag, tm=256):
    c_n, t, k = a.shape

    def body(a_ref, w_ref, x_ref, g_ref, dr_ref, dx_ref, dxb_ref, dg_ref):
        @pl.when(pl.program_id(0) == 0)
        def _():
            dg_ref[...] = jnp.zeros_like(dg_ref)

        dh = _dot_nn(a_ref[0], w_ref[0])
        for c in range(1, c_n):
            dh = dh + _dot_nn(a_ref[c], w_ref[c])
        xv = x_ref[...]
        r = lax.rsqrt(jnp.mean(xv * xv, axis=-1, keepdims=True) + RMS_EPS)
        xh = xv * r
        u = dh * g_ref[...]
        dx = dr_ref[...] + r * (u - xh * jnp.mean(xh * u, axis=-1, keepdims=True))
        dx_ref[...] = dx
        dxb_ref[...] = dx.astype(BF16)
        dg_ref[...] += jnp.sum(dh * xh, axis=0, keepdims=True)

    row = BS((tm, D), lambda i: (i, 0))
    vec = BS((1, D), lambda i: (0, 0))
    return pl.pallas_call(
        body, name=f"mm_nn_norm_bwd_{tag}", grid=(t // tm,),
        in_specs=[BS((c_n, tm, k), lambda i: (0, i, 0)), BS((c_n, k, D), lambda i: (0, 0, 0)), row, vec, row],
        out_specs=[row, row, vec], out_shape=[SDS((t, D), F32), SDS((t, D), BF16), SDS((1, D), F32)],
        compiler_params=_cp("arbitrary"),
    )(a, w, x, g.reshape(1, D), dres)


def _mm_nt_rows(a, w, tag, tm, tn, n_total, w_row0, rope=None):
    t, k = a.shape
    assert w_row0 % tn == 0 and n_total % tn == 0
    j0 = w_row0 // tn

    def body(a_ref, w_ref, *rest):
        o_ref = rest[-1]
        o_ref[...] = _dot_nt(a_ref[...].astype(BF16), w_ref[...])
        if rope is not None:
            @pl.when(pl.program_id(0) == 0)
            def _():
                c = rest[0][...]
                sg = rest[1][...]
                first = (lax.broadcasted_iota(jnp.int32, (tm, 128), 1) % HEAD_DIM) < HEAD_DIM // 2
                for col in range(0, rope[2], 128):
                    v = o_ref[:, col:col + 128]
                    o_ref[:, col:col + 128] = v * c + _swap_halves(v, first) * sg

    in_specs = [BS((tm, k), lambda j, i: (i, 0)), BS((tn, k), lambda j, i: (j0 + j, 0))]
    args = [a, w]
    if rope is not None:
        assert rope[2] <= tn
        in_specs += [BS((tm, 128), lambda j, i: (i % (S // tm), 0))] * 2
        args += [rope[0], rope[1]]
    return pl.pallas_call(
        body, name=f"mm_nt_{tag}", grid=(n_total // tn, t // tm), in_specs=in_specs,
        out_specs=BS((tm, tn), lambda j, i: (i, j)), out_shape=SDS((t, n_total), F32),
        compiler_params=_cp("parallel", "parallel"),
    )(*args)


def _mm_tn(a, b, tag, scale=1.0, tmm=256, tk=None):
    c_n, t, m = a.shape
    n = b.shape[1]
    tk = t if tk is None else tk
    nk = t // tk

    def body_one(a_ref, b_ref, o_ref):
        o_ref[...] = (_dot_tn(a_ref[...].astype(BF16), b_ref[...].astype(BF16)) * scale).astype(BF16)

    def body_acc(a_ref, b_ref, o_ref, acc_ref):
        kk = pl.program_id(2)

        @pl.when(kk == 0)
        def _():
            acc_ref[...] = jnp.zeros_like(acc_ref)

        acc_ref[...] += _dot_tn(a_ref[...].astype(BF16), b_ref[...].astype(BF16))

        @pl.when(kk == nk - 1)
        def _():
            o_ref[...] = (acc_ref[...] * scale).astype(BF16)

    return pl.pallas_call(
        body_one if nk == 1 else body_acc, name=f"mm_tn_{tag}", grid=(c_n, m // tmm, nk),
        in_specs=[BS((None, tk, tmm), lambda c, mi, kk: (c, kk, mi)), BS((tk, n), lambda c, mi, kk: (kk, 0))],
        out_specs=BS((None, tmm, n), lambda c, mi, kk: (c, mi, 0)),
        out_shape=SDS((c_n, m, n), BF16), scratch_shapes=[] if nk == 1 else [pltpu.VMEM((tmm, n), F32)],
        compiler_params=_cp("parallel", "parallel", "arbitrary"),
    )(a, b)


def _ffn_up(hn, wut, tag):
    t = hn.shape[0]
    tm, tn = 512, 1408

    def body(h_ref, w_ref, gu_ref, act_ref):
        h = h_ref[...]
        g = _dot_nt(h, w_ref[0])
        u = _dot_nt(h, w_ref[1])
        sg = jax.nn.sigmoid(g)
        silu = g * sg
        gu_ref[0] = (u * (sg + silu * (1.0 - sg))).astype(BF16)
        gu_ref[1] = silu.astype(BF16)
        act_ref[...] = (silu * u).astype(BF16)

    return pl.pallas_call(
        body, name=f"ffn_up_{tag}", grid=(F // tn, t // tm),
        in_specs=[BS((tm, D), lambda j, i: (i, 0)), BS((2, tn, D), lambda j, i: (0, j, 0))],
        out_specs=[BS((2, tm, tn), lambda j, i: (0, i, j)), BS((tm, tn), lambda j, i: (i, j))],
        out_shape=[SDS((2, t, F), BF16), SDS((t, F), BF16)],
        compiler_params=_cp("parallel", "parallel"),
    )(hn, wut)


def _ffn_dact(dxo, wd, gu, tie, tag):
    t = dxo.shape[0]
    tm, tn = 512, 1408

    def body(d_ref, w_ref, gu_ref, tie_ref, o_ref):
        dact = _dot_nt(d_ref[...] * 0.5, w_ref[...])
        o_ref[0] = (dact * gu_ref[0].astype(F32)).astype(BF16)
        o_ref[1] = (dact * gu_ref[1].astype(F32)).astype(BF16)

    return pl.pallas_call(
        body, name=f"ffn_dact_{tag}", grid=(F // tn, t // tm),
        in_specs=[BS((tm, D), lambda j, i: (i, 0)), BS((tn, D), lambda j, i: (j, 0)),
                  BS((2, tm, tn), lambda j, i: (0, i, j)), BS((8, 128), lambda j, i: (0, 0))],
        out_specs=BS((2, tm, tn), lambda j, i: (0, i, j)),
        out_shape=SDS((2, t, F), BF16), compiler_params=_cp("parallel", "parallel"),
    )(dxo, wd, gu, tie)


def _rope_tables():
    half = HEAD_DIM // 2
    inv_freq = ROPE_THETA ** (-jnp.arange(half, dtype=F32) / half)
    ang = jnp.arange(S).astype(F32)[:, None] * inv_freq[None, :]
    cos, sin = jnp.cos(ang), jnp.sin(ang)
    return jnp.concatenate([cos, cos, cos, cos], axis=1), jnp.concatenate([-sin, sin, -sin, sin], axis=1)


def _swap_halves(t, first_half):
    return jnp.where(first_half, pltpu.roll(t, 96, 1), pltpu.roll(t, 32, 1))


def _rope_bwd(dqs, dks, cos_t, sin_t):
    t = dqs[0].shape[0]
    tm = 512

    def body(*refs):
        c = refs[6][...]
        sg = refs[7][...]
        o_ref = refs[8]
        first = (lax.broadcasted_iota(jnp.int32, (tm, 128), 1) % HEAD_DIM) < HEAD_DIM // 2
        for a in range(6):
            for hp in range(2):
                v = refs[a][:, 128 * hp:128 * (hp + 1)]
                col = 128 * (2 * a + hp)
                o_ref[:, col:col + 128] = (v * c + _swap_halves(v * sg, first)).astype(BF16)

    blk = BS((tm, 256), lambda i: (i, 0))
    tab = BS((tm, 128), lambda i: (i % (S // tm), 0))
    return pl.pallas_call(
        body, name="rope_bwd", grid=(t // tm,), in_specs=[blk] * 6 + [tab, tab],
        out_specs=BS((tm, 1536), lambda i: (i, 0)), out_shape=SDS((t, 1536), BF16),
        compiler_params=_cp("parallel"),
    )(*dqs, *dks, cos_t, sin_t)


def _head_masks():
    lane = lax.broadcasted_iota(jnp.int32, (1, 128), 1)
    m0 = (lane < HEAD_DIM).astype(F32)
    return m0, 1.0 - m0


def _dil_geometry(d):
    sub = S // d
    q_rows = 128
    k_rows = min(256, sub)
    return sub, q_rows, sub // q_rows, k_rows


def _dil_tile(idx, d, keys_on_rows=False):
    sub, q_rows, nb, k_rows = _dil_geometry(d)
    r = idx // nb
    n = idx % nb
    k_sub = jnp.clip(q_rows * n - HALF, 0, sub - k_rows)
    if d == 1:
        q_start = pl.multiple_of(q_rows * n, q_rows)
        k_start = pl.multiple_of(k_sub, HALF)
    else:
        q_start = q_rows * n * d + r
        k_start = k_sub * d + r
    if keys_on_rows:
        ii = lax.broadcasted_iota(jnp.int32, (k_rows, 2 * q_rows), 1) % q_rows
        jj = lax.broadcasted_iota(jnp.int32, (k_rows, 2 * q_rows), 0)
    else:
        ii = lax.broadcasted_iota(jnp.int32, (q_rows, k_rows), 0)
        jj = lax.broadcasted_iota(jnp.int32, (q_rows, k_rows), 1)
    valid = jnp.abs(jj - ii + (k_sub - q_rows * n)) <= HALF
    return q_start, k_start, valid


def _dil_specs(grp):
    qs = BS((S, 128), lambda b, hp: (b, 2 * grp + hp))
    ks = BS((S, 128), lambda b, hp: (b, 6 + 2 * grp + hp))
    vs = BS((S, 128), lambda b, hp: (b, 12 + 2 * grp + hp))
    own = BS((S, 128), lambda b, hp: (b, hp))
    return qs, ks, vs, own


def _dil_fwd(qkr, proj, grp):
    t = qkr.shape[0]
    d = DILATIONS[grp]
    _, q_rows, nb, k_rows = _dil_geometry(d)

    def body(q_ref, k_ref, v_ref, o_ref, l_ref):
        masks = _head_masks()

        def step(i0, carry):
            geo = [_dil_tile(i0 * DIL_FWD_TILES + j, d) for j in range(DIL_FWD_TILES)]
            tiles = [(j, h) for j in range(DIL_FWD_TILES) for h in range(2)]
            qs = [q_ref[_ds(g[0], q_rows, d), :] for g in geo]
            kbs = [k_ref[_ds(g[1], k_rows, d), :].astype(BF16) for g in geo]
            ss = [jnp.where(geo[j][2], _dot_nt((qs[j] * masks[h]).astype(BF16), kbs[j]) * SCALE, NEG) for j, h in tiles]
            mxs = [jnp.max(s, axis=1, keepdims=True) for s in ss]
            ps = [jnp.exp(s - mx) for s, mx in zip(ss, mxs)]
            dens = [jnp.sum(p, axis=1, keepdims=True) for p in ps]
            vs = [v_ref[_ds(g[1], k_rows, d), :] for g in geo]
            outs = [_dot_nn(p.astype(BF16), (vs[j] * masks[h]).astype(BF16)) / den
                    for p, den, (j, h) in zip(ps, dens, tiles)]
            for j, g in enumerate(geo):
                o_ref[_ds(g[0], q_rows, d), :] = outs[2 * j] + outs[2 * j + 1]
                l_ref[_ds(g[0], q_rows, d), :] = (
                    (mxs[2 * j] + jnp.log(dens[2 * j])) * masks[0] + (mxs[2 * j + 1] + jnp.log(dens[2 * j + 1])) * masks[1])
            return carry

        lax.fori_loop(0, d * nb // DIL_FWD_TILES, step, 0)

    qs, ks, vs, own = _dil_specs(grp)
    return pl.pallas_call(
        body, name=f"dil_fwd_{grp}", grid=(t // S, 2), in_specs=[qs, ks, vs], out_specs=[own, own],
        out_shape=[SDS((t, 256), F32), SDS((t, 256), F32)], compiler_params=_cp("parallel", "parallel"),
    )(qkr, qkr, proj)


def _dil_bwd(qkr, proj, do, dlp, lse, grp):
    t = qkr.shape[0]
    d = DILATIONS[grp]
    _, q_rows, nb, k_rows = _dil_geometry(d)

    def body(q_ref, k_ref, v_ref, do_ref, dl_ref, l_ref, dq_ref, dk_ref, dv_ref):
        masks = _head_masks()
        dk_ref[...] = jnp.zeros_like(dk_ref)
        dv_ref[...] = jnp.zeros_like(dv_ref)

        def as_row(x2):
            xt = x2.T
            return jnp.concatenate([xt[0:1], xt[HEAD_DIM:HEAD_DIM + 1]], axis=1)

        def step(i0, carry):
            geo = [_dil_tile(i0 * DIL_BWD_TILES + j, d, keys_on_rows=True) for j in range(DIL_BWD_TILES)]
            q_ds = [_ds(g[0], q_rows, d) for g in geo]
            k_ds = [_ds(g[1], k_rows, d) for g in geo]
            qbs = [_both_heads(q_ref[r, :], masks).astype(BF16) for r in q_ds]
            kbs = [k_ref[r, :].astype(BF16) for r in k_ds]
            vbs = [v_ref[r, :].astype(BF16) for r in k_ds]
            dobs = [_both_heads(do_ref[r, :], masks).astype(BF16) for r in q_ds]
            l_rows = [as_row(l_ref[r, :]) for r in q_ds]
            dl_rows = [as_row(dl_ref[r, :]) for r in q_ds]
            ss = [jnp.where(g[2], _dot_nt(kb, qb) * SCALE, NEG) for g, kb, qb in zip(geo, kbs, qbs)]
            ps = [jnp.exp(s - lr) for s, lr in zip(ss, l_rows)]
            dps = [_dot_nt(vb, dob) for vb, dob in zip(vbs, dobs)]
            dss = [(p * (dp - dr)).astype(BF16) for p, dp, dr in zip(ps, dps, dl_rows)]
            dks = [_dot_nn(ds, qb) for ds, qb in zip(dss, qbs)]
            dvs = [_dot_nn(p.astype(BF16), dob) for p, dob in zip(ps, dobs)]
            dqs = [_own_heads(_dot_tn(ds, kb), masks) for ds, kb in zip(dss, kbs)]
            for j in range(DIL_BWD_TILES):
                dq_ref[q_ds[j], :] = dqs[j] * SCALE
                dk_ref[k_ds[j], :] += dks[j] * SCALE
                dv_ref[k_ds[j], :] += dvs[j]
            return carry

        lax.fori_loop(0, d * nb // DIL_BWD_TILES, step, 0)

    qs, ks, vs, own = _dil_specs(grp)
    return pl.pallas_call(
        body, name=f"dil_bwd_{grp}", grid=(t // S, 2), in_specs=[qs, ks, vs, own, own, own],
        out_specs=[own, own, own], out_shape=[SDS((t, 256), F32)] * 3,
        compiler_params=_cp("parallel", "parallel"),
    )(qkr, qkr, proj, do, dlp, lse)


def _mix_weights(l0, l1, l2):
    mx = jnp.maximum(jnp.maximum(l0, l1), l2)
    e0, e1, e2 = jnp.exp(l0 - mx), jnp.exp(l1 - mx), jnp.exp(l2 - mx)
    den = e0 + e1 + e2
    return e0 / den, e1 / den, e2 / den


def _combine_fwd(outs, lses):
    t = outs[0].shape[0]
    tm = 512

    def body(o0, o1, o2, l0, l1, l2, y_ref):
        w0, w1, w2 = _mix_weights(l0[...], l1[...], l2[...])
        y_ref[...] = w0 * o0[...] + w1 * o1[...] + w2 * o2[...]

    blk = BS((tm, 256), lambda i: (i, 0))
    return pl.pallas_call(
        body, name="combine_fwd", grid=(t // tm,), in_specs=[blk] * 6, out_specs=blk,
        out_shape=SDS((t, 256), F32), compiler_params=_cp("parallel"),
    )(*outs, *lses)


def _head_sum(x):
    a = lax.broadcasted_iota(jnp.int32, (256, 256), 0) // HEAD_DIM
    b = lax.broadcasted_iota(jnp.int32, (256, 256), 1) // HEAD_DIM
    ones = (a == b).astype(BF16)
    hi = x.astype(BF16)
    lo = (x - hi.astype(F32)).astype(BF16)
    return _dot_nn(hi, ones) + _dot_nn(lo, ones)


def _combine_bwd(dya, outs, lses):
    t = dya.shape[0]
    tm = 512

    def body(dy_ref, o0, o1, o2, l0, l1, l2, d0, d1, d2, e0, e1, e2):
        ws = _mix_weights(l0[...], l1[...], l2[...])
        dy = dy_ref[...]
        ya = ws[0] * o0[...] + ws[1] * o1[...] + ws[2] * o2[...]
        hs = _head_sum(dy * ya)
        for w, d_ref, e_ref in zip(ws, (d0, d1, d2), (e0, e1, e2)):
            d_ref[...] = w * dy
            e_ref[...] = w * hs

    blk = BS((tm, 256), lambda i: (i, 0))
    return pl.pallas_call(
        body, name="combine_bwd", grid=(t // tm,), in_specs=[blk] * 7, out_specs=[blk] * 6,
        out_shape=[SDS((t, 256), F32)] * 6, compiler_params=_cp("parallel"),
    )(dya, *outs, *lses)


def _na_bias_table(rel_bias):
    kw = NA_KR * GRID_W
    rev = jnp.pad(rel_bias.astype(F32)[:, :, ::-1], ((0, 0), (0, 0), (0, 128 - 31)))

    def body(r_ref, o_ref):
        lane = lax.broadcasted_iota(jnp.int32, (GRID_W, 128), 1)
        j = lax.broadcasted_iota(jnp.int32, (GRID_W, 128), 0)
        q = lane % GRID_W
        win_lo = jnp.clip(q - 8, 0, GRID_W - 16)
        valid = (j >= win_lo) & (j < win_lo + 16)
        for cls in range(NA_KR):
            for k in range(NA_KR):
                tiles = []
                for h in range(2):
                    row = jnp.broadcast_to(r_ref[h, cls + k:cls + k + 1, :], (GRID_W, 128))
                    tiles.append(pltpu.roll(row, (128 - 15 + GRID_W * h) % 128, 1, stride=1, stride_axis=0))
                o_ref[cls, GRID_W * k:GRID_W * (k + 1), :] = jnp.where(
                    valid, jnp.where(lane < GRID_W, tiles[0], tiles[1]), NEG)

    return pl.pallas_call(
        body, name="na_bias_table", grid=(4,),
        in_specs=[BS((2, 2 * NA_KR - 1, 128), lambda hp: (hp, 0, 0))],
        out_specs=BS((None, NA_KR, kw, 128), lambda hp: (hp, 0, 0, 0)),
        out_shape=SDS((4, NA_KR, kw, 128), F32), compiler_params=_cp("parallel"),
    )(rev)


def _na_row(i):
    lo = jnp.clip(i - NA_KR // 2, 0, NA_ROWS - NA_KR)
    return pl.multiple_of(GRID_W * i, GRID_W), pl.multiple_of(GRID_W * lo, GRID_W), lo - i + NA_KR - 1


def _both_heads(x, masks):
    return jnp.concatenate([x * masks[0], x * masks[1]], axis=0)


def _own_heads(r, masks):
    half = r.shape[0] // 2
    return r[:half] * masks[0] + r[half:] * masks[1]


def _na_fwd(proj, bias):
    t = proj.shape[0]
    kw = NA_KR * GRID_W

    def body(q_ref, k_ref, v_ref, b_ref, o_ref, l_ref):
        masks = _head_masks()

        def step(i0, carry):
            idx = [i0 * NA_FWD_ROWS + j for j in range(NA_FWD_ROWS)]
            rows = [_na_row(i) for i in idx]
            qbs = [_both_heads(q_ref[pl.ds(r[0], GRID_W), :], masks).astype(BF16) for r in rows]
            kbs = [k_ref[pl.ds(r[1], kw), :].astype(BF16) for r in rows]
            ss = [_dot_nt(kb, qb) * SCALE + b_ref[r[2]] for kb, qb, r in zip(kbs, qbs, rows)]
            mxs = [jnp.max(s, axis=0, keepdims=True) for s in ss]
            ps = [jnp.exp(s - mx) for s, mx in zip(ss, mxs)]
            dens = [jnp.sum(p, axis=0, keepdims=True) for p in ps]
            pbs = [(p / den).astype(BF16) for p, den in zip(ps, dens)]
            vbs = [v_ref[pl.ds(r[1], kw), :].astype(BF16) for r in rows]
            outs = [_own_heads(_dot_tn(pb, vb), masks) for pb, vb in zip(pbs, vbs)]
            for j, r in enumerate(rows):
                o_ref[pl.ds(r[0], GRID_W), :] = outs[j]
                l_ref[pl.ds(idx[j], 1), :] = mxs[j] + jnp.log(dens[j])
            return carry

        lax.fori_loop(0, NA_ROWS // NA_FWD_ROWS, step, 0)

    c0 = QKV_A // 128
    return pl.pallas_call(
        body, name="na_fwd", grid=(t // S, 4),
        in_specs=[BS((S, 128), lambda b, hp: (b, c0 + hp)), BS((S, 128), lambda b, hp: (b, c0 + 4 + hp)),
                  BS((S, 128), lambda b, hp: (b, c0 + 8 + hp)),
                  BS((None, NA_KR, kw, 128), lambda b, hp: (hp, 0, 0, 0))],
        out_specs=[BS((S, 128), lambda b, hp: (b, hp)), BS((None, None, NA_ROWS, 128), lambda b, hp: (b, hp, 0, 0))],
        out_shape=[SDS((t, 512), F32), SDS((t // S, 4, NA_ROWS, 128), F32)],
        compiler_params=_cp("parallel", "parallel"),
    )(proj, proj, proj, bias)


def _na_bwd(proj, bias, dyb, yb, lse):
    t = proj.shape[0]
    kw = NA_KR * GRID_W

    def body(q_ref, k_ref, v_ref, b_ref, do_ref, o_ref, l_ref, dq_ref, dk_ref, dv_ref, db_ref):
        masks = _head_masks()
        ones = jnp.ones((8, 128), BF16)

        @pl.when(pl.program_id(1) == 0)
        def _():
            db_ref[...] = jnp.zeros_like(db_ref)

        dk_ref[...] = jnp.zeros_like(dk_ref)
        dv_ref[...] = jnp.zeros_like(dv_ref)

        def row_sums(x):
            hi = x.astype(BF16)
            lo = (x - hi.astype(F32)).astype(BF16)
            return (_dot_nt(ones, hi) + _dot_nt(ones, lo))[0:1]

        def step(i0, carry):
            idx = [i0 * NA_BWD_ROWS + j for j in range(NA_BWD_ROWS)]
            rows = [_na_row(i) for i in idx]
            q_ds = [pl.ds(r[0], GRID_W) for r in rows]
            k_ds = [pl.ds(r[1], kw) for r in rows]
            qbs = [_both_heads(q_ref[r, :], masks).astype(BF16) for r in q_ds]
            kbs = [k_ref[r, :].astype(BF16) for r in k_ds]
            vbs = [v_ref[r, :].astype(BF16) for r in k_ds]
            dos = [do_ref[r, :] for r in q_ds]
            dobs = [_both_heads(do, masks).astype(BF16) for do in dos]
            deltas = [row_sums(_both_heads(do * o_ref[r, :], masks)) for do, r in zip(dos, q_ds)]
            ss = [_dot_nt(kb, qb) * SCALE + b_ref[r[2]] for kb, qb, r in zip(kbs, qbs, rows)]
            ps = [jnp.exp(s - l_ref[pl.ds(i, 1), :]) for s, i in zip(ss, idx)]
            dps = [_dot_nt(vb, dob) for vb, dob in zip(vbs, dobs)]
            dss = [p * (dp - delta) for p, dp, delta in zip(ps, dps, deltas)]
            for ds, r in zip(dss, rows):
                db_ref[r[2]] += ds
            dsbs = [ds.astype(BF16) for ds in dss]
            dks = [_dot_nn(dsb, qb) for dsb, qb in zip(dsbs, qbs)]
            dvs = [_dot_nn(p.astype(BF16), dob) for p, dob in zip(ps, dobs)]
            dqs = [_own_heads(_dot_tn(dsb, kb), masks) for dsb, kb in zip(dsbs, kbs)]
            for j in range(NA_BWD_ROWS):
                dq_ref[q_ds[j], :] = dqs[j] * SCALE
                dk_ref[k_ds[j], :] += dks[j] * SCALE
                dv_ref[k_ds[j], :] += dvs[j]
            return carry

        lax.fori_loop(0, NA_ROWS // NA_BWD_ROWS, step, 0)

    c0 = QKV_A // 128
    own = BS((S, 128), lambda hp, b: (b, hp))
    tab = BS((None, NA_KR, kw, 128), lambda hp, b: (hp, 0, 0, 0))
    return pl.pallas_call(
        body, name="na_bwd", grid=(4, t // S),
        in_specs=[BS((S, 128), lambda hp, b: (b, c0 + hp)), BS((S, 128), lambda hp, b: (b, c0 + 4 + hp)),
                  BS((S, 128), lambda hp, b: (b, c0 + 8 + hp)), tab, own, own,
                  BS((None, None, NA_ROWS, 128), lambda hp, b: (b, hp, 0, 0))],
        out_specs=[own, own, own, tab],
        out_shape=[SDS((t, 512), F32)] * 3 + [SDS((4, NA_KR, kw, 128), F32)],
        compiler_params=_cp("parallel", "arbitrary"),
    )(proj, proj, proj, bias, dyb, yb, lse)


def _na_dbias_lane_map():
    kw = NA_KR * GRID_W
    lane = np.arange(kw)
    blk, m = lane // GRID_W, lane % GRID_W
    target = np.full(kw, -1)
    target[m < 16] = (blk * 32 + 15 + m)[m < 16]
    target[m >= 49] = (((blk + 1) % NA_KR) * 32 + m - 49)[m >= 49]
    return jnp.asarray(target[:, None] == np.arange(kw)[None, :], BF16)


def _na_dbias(db):
    kw = NA_KR * GRID_W

    def body(x_ref, map_ref, o_ref, z_ref):
        for cls in range(NA_KR):
            xt = x_ref[cls].T
            for h in range(2):
                xv = xt[GRID_W * h:GRID_W * (h + 1)]
                y = xv[0:8]
                for g in range(1, GRID_W // 8):
                    y = y + pltpu.roll(xv[8 * g:8 * g + 8], kw - 8 * g, 1)
                d = y[0:1]
                for s in range(1, 8):
                    d = d + pltpu.roll(y[s:s + 1], kw - s, 1)
                z_ref[h, cls:cls + 1, :] = d
        for h in range(2):
            z = z_ref[h]
            hi = z.astype(BF16)
            lo = (z - hi.astype(F32)).astype(BF16)
            e = _dot_nn(hi, map_ref[...]) + _dot_nn(lo, map_ref[...])
            out = e[0:1]
            for cls in range(1, NA_KR):
                out = out + pltpu.roll(e[cls:cls + 1], 32 * cls, 1)
            o_ref[h] = jnp.broadcast_to(out, (8, kw))

    return pl.pallas_call(
        body, name="na_dbias", grid=(4,),
        in_specs=[BS((None, NA_KR, kw, 128), lambda hp: (hp, 0, 0, 0)), BS((kw, kw), lambda hp: (0, 0))],
        out_specs=BS((2, 8, kw), lambda hp: (hp, 0, 0)), out_shape=SDS((8, 8, kw), F32),
        scratch_shapes=[pltpu.VMEM((2, 8, kw), F32)], compiler_params=_cp("parallel"),
    )(db, _na_dbias_lane_map())


def _merge_fwd(ya, yb, proj, wat, wbt):
    t = ya.shape[0]
    tm, tn = 512, 256
    ca = (QKV_A + QKV_B) // tn
    cb = ca + D // tn

    def body(ya_ref, yb_ref, la_ref, lb_ref, wa_ref, wb_ref, m_ref, za_ref, zb_ref):
        za = _dot_nt(ya_ref[...].astype(BF16), wa_ref[...])
        zb = _dot_nt(yb_ref[...].astype(BF16), wb_ref[...])
        m_ref[...] = (jax.nn.sigmoid(la_ref[...]) * za + jax.nn.sigmoid(lb_ref[...]) * zb).astype(BF16)
        za_ref[...] = za.astype(BF16)
        zb_ref[...] = zb.astype(BF16)

    out = BS((tm, tn), lambda i, j: (i, j))
    return pl.pallas_call(
        body, name="merge_fwd", grid=(t // tm, D // tn),
        in_specs=[BS((tm, 256), lambda i, j: (i, 0)), BS((tm, 512), lambda i, j: (i, 0)),
                  BS((tm, tn), lambda i, j: (i, ca + j)), BS((tm, tn), lambda i, j: (i, cb + j)),
                  BS((tn, 256), lambda i, j: (j, 0)), BS((tn, 512), lambda i, j: (j, 0))],
        out_specs=[out, out, out], out_shape=[SDS((t, D), BF16)] * 3,
        compiler_params=_cp("parallel", "parallel"),
    )(ya, yb, proj, proj, wat, wbt)


def _merge_bwd(dxo, wo, za, zb, proj):
    t = dxo.shape[0]
    tm, tn = 512, 256
    ca = (QKV_A + QKV_B) // tn
    cb = ca + D // tn

    def body(d_ref, w_ref, za_ref, zb_ref, la_ref, lb_ref, dza_ref, dzb_ref, dl_ref):
        dmv = _dot_nt(d_ref[...], w_ref[...])
        ga = jax.nn.sigmoid(la_ref[...])
        gb = jax.nn.sigmoid(lb_ref[...])
        dza_ref[...] = (dmv * ga).astype(BF16)
        dzb_ref[...] = (dmv * gb).astype(BF16)
        dl_ref[0] = (dmv * za_ref[...].astype(F32) * ga * (1.0 - ga)).astype(BF16)
        dl_ref[1] = (dmv * zb_ref[...].astype(F32) * gb * (1.0 - gb)).astype(BF16)

    blk = BS((tm, tn), lambda i, j: (i, j))
    return pl.pallas_call(
        body, name="merge_bwd", grid=(t // tm, D // tn),
        in_specs=[BS((tm, D), lambda i, j: (i, 0)), BS((tn, D), lambda i, j: (j, 0)), blk, blk,
                  BS((tm, tn), lambda i, j: (i, ca + j)), BS((tm, tn), lambda i, j: (i, cb + j))],
        out_specs=[blk, blk, BS((2, tm, tn), lambda i, j: (0, i, j))],
        out_shape=[SDS((t, D), BF16), SDS((t, D), BF16), SDS((2, t, D), BF16)],
        compiler_params=_cp("parallel", "parallel"),
    )(dxo, wo, za, zb, proj, proj)


def _sum_slots(recv0, recv1, tag):
    _, r, c = recv0.shape
    tr = r if r * c <= 512 * 1024 else r // 2

    def body(a_ref, b_ref, o_ref):
        for layer, ref in enumerate((a_ref, b_ref)):
            acc = ref[0].astype(F32)
            for s in range(1, N_DEV):
                acc = acc + ref[s].astype(F32)
            o_ref[layer] = acc

    blk = BS((N_DEV, tr, c), lambda i: (0, i, 0))
    return pl.pallas_call(
        body, name=f"sum_slots_{tag}", grid=(r // tr,), in_specs=[blk, blk],
        out_specs=BS((2, tr, c), lambda i: (0, i, 0)), out_shape=SDS((2, r, c), F32),
        compiler_params=_cp("parallel"),
    )(recv0, recv1)


def _adamw(w, g, m, v, tag):
    layers, r, c = w.shape
    tr = next(r // k for k in (1, 2, 4, 8) if r // k <= 384 and r % (8 * k) == 0)

    def body(w_ref, g_ref, m_ref, v_ref, d_ref, mo_ref, vo_ref):
        gv = g_ref[...]
        mn = ADAM_B1 * m_ref[...] + (1.0 - ADAM_B1) * gv
        vn = ADAM_B2 * v_ref[...] + (1.0 - ADAM_B2) * (gv * gv)
        m_hat = mn / (1.0 - ADAM_B1 ** ADAM_STEP)
        v_hat = vn / (1.0 - ADAM_B2 ** ADAM_STEP)
        d_ref[...] = -ADAM_LR * (m_hat / (jnp.sqrt(v_hat) + ADAM_EPS) + ADAM_WD * w_ref[...])
        mo_ref[...] = mn
        vo_ref[...] = vn

    blk = BS((None, tr, c), lambda l, i: (l, i, 0))
    return pl.pallas_call(
        body, name=f"adamw_{tag}", grid=(layers, r // tr), in_specs=[blk] * 4, out_specs=[blk] * 3,
        out_shape=[SDS((layers, r, c), F32)] * 3, compiler_params=_cp("parallel", "parallel"),
    )(w, g, m, v)


def _place():
    return lax.axis_index("x"), lax.axis_index("y"), lax.axis_index("c")


def _flip(coord, bit):
    return 1 - coord if bit else coord


def _peers(x, y, c):
    peers = []
    for mask in range(1, N_DEV):
        p = (_flip(x, mask & 4), _flip(y, mask & 2), _flip(c, mask & 1))
        peers.append((p, 4 * p[0] + 2 * p[1] + p[2]))
    return peers


def _copy_plan(mode, src, land, x, y, c):
    me = 4 * x + 2 * y + c

    def device(mask):
        p = (_flip(x, mask & 4), _flip(y, mask & 2), _flip(c, mask & 1))
        return p, 4 * p[0] + 2 * p[1] + p[2]

    if mode == "scatter":
        r = land.shape[1]
        return [(p, src.at[pl.ds(i * r, r), :], land.at[me], land.at[i])
                for p, i in map(device, (1, 2, 3, 4, 5, 6, 7, 0))]
    r = land.shape[0] // N_DEV

    def rows(i):
        return land.at[pl.ds(i * r, r), :]

    if mode == "gather":
        return [(p, src, rows(me), rows(i)) for p, i in map(device, (1, 4, 2, 6, 0))]
    sibling = device(1)[0]
    return [(sibling, rows(device(m)[1]), rows(device(m)[1]), rows(device(m | 1)[1])) for m in (4, 2, 6)]


COPIES = dict(scatter=8, gather=5, forward=3)
HBM_SPEC = BS(memory_space=pltpu.HBM)
SEM_SPEC = BS(memory_space=pltpu.SEMAPHORE)
DATAFLOW = pltpu.SideEffectType.DATAFLOW_SIDE_EFFECTING


def _fresh(shape, dtype, tag):
    def body(o_ref):
        del o_ref

    return pl.pallas_call(body, name=f"fresh_{tag}", out_specs=BS(memory_space=pl.ANY), out_shape=SDS(shape, dtype))()


def _exchange_start(mode, srcs, lands, after, tag):
    if lands is None and mode == "gather":
        lands = [_fresh((N_DEV * s.shape[0], s.shape[1]), s.dtype, f"{tag}_{a}") for a, s in enumerate(srcs)]
    elif lands is None:
        lands = [_fresh((N_DEV, s.shape[0] // N_DEV, s.shape[1]), s.dtype, f"{tag}_{a}") for a, s in enumerate(srcs)]
    n, n_src, n_cp = len(lands), len(srcs), COPIES[mode]
    behind = [] if after is None else [after]

    def body(*refs):
        src_refs, land_refs = refs[:n_src], refs[n_src:n_src + n]
        send_sems, recv_sems = refs[n_src + n + len(behind)], refs[n_src + n + len(behind) + 1]
        token = refs[-1]
        for a in range(n):
            plan = _copy_plan(mode, src_refs[a] if n_src else None, land_refs[a], *_place())
            for k, (p, out, there, _) in enumerate(plan):
                pltpu.make_async_remote_copy(
                    src_ref=out, dst_ref=there, send_sem=send_sems.at[n_cp * a + k],
                    recv_sem=recv_sems.at[n_cp * a + k], device_id=p, device_id_type=MESH).start()
        token[...] = jnp.zeros_like(token)

    both = [*srcs, *lands]
    res = pl.pallas_call(
        body, name=f"{mode}_start_{tag}",
        out_shape=(pltpu.SemaphoreType.DMA((n_cp * n,)), pltpu.SemaphoreType.DMA((n_cp * n,)),
                   *[pltpu.HBM(v.shape, v.dtype) for v in both], SDS((8, 128), F32)),
        in_specs=[HBM_SPEC] * len(both) + [BS(memory_space=pl.ANY)] * len(behind),
        out_specs=(SEM_SPEC, SEM_SPEC, *[HBM_SPEC] * len(both), BS(memory_space=pltpu.VMEM)),
        input_output_aliases={i: 2 + i for i in range(len(both))},
        compiler_params=pltpu.CompilerParams(has_side_effects=DATAFLOW),
    )(*[pltpu.with_memory_space_constraint(v, pltpu.HBM) for v in both], *behind)
    return (mode, res[0], res[1], res[2:2 + n_src], res[2 + n_src:2 + n_src + n]), res[-1]


def _exchange_wait(handle, after, tag, which=None):
    mode, send_sems, recv_sems, srcs, lands = handle
    which = list(range(len(lands))) if which is None else list(which)
    n_cp = COPIES[mode]
    lands = [lands[a] for a in which]
    srcs = [srcs[a] for a in which] if srcs else []
    n, n_src = len(lands), len(srcs)
    afters = list(after) if isinstance(after, (tuple, list)) else [after]

    def body(*refs):
        src_refs, land_refs = refs[:n_src], refs[n_src:n_src + n]
        send_ref, recv_ref = refs[n_src + n], refs[n_src + n + 1]
        for i, a in enumerate(which):
            plan = _copy_plan(mode, src_refs[i] if n_src else None, land_refs[i], *_place())
            for k, (p, out, _, here) in enumerate(plan):
                cp = pltpu.make_async_remote_copy(
                    src_ref=out, dst_ref=here, send_sem=send_ref.at[n_cp * a + k], recv_sem=recv_ref.at[n_cp * a + k],
                    device_id=p, device_id_type=MESH)
                cp.wait_send()
                cp.wait_recv()

    both = [*srcs, *lands]
    res = pl.pallas_call(
        body, name=f"{mode}_wait_{tag}", out_shape=tuple(pltpu.HBM(v.shape, v.dtype) for v in both),
        in_specs=[HBM_SPEC] * len(both) + [SEM_SPEC, SEM_SPEC] + [BS(memory_space=pl.ANY)] * len(afters),
        out_specs=tuple([HBM_SPEC] * len(both)),
        input_output_aliases={i: i for i in range(len(both))},
        compiler_params=pltpu.CompilerParams(has_side_effects=DATAFLOW),
    )(*both, send_sems, recv_sems, *afters)
    return list(res[n_src:])


def _allreduce_small(vec, behind):
    rows = vec.shape[0]

    def body(x_ref, behind_ref, o_ref, buf_ref, send_sems, recv_sems):
        x, y, c = _place()
        me = 4 * x + 2 * y + c
        buf_ref[me] = x_ref[...]
        peers = _peers(x, y, c)

        def copy(k, slot):
            return pltpu.make_async_remote_copy(
                src_ref=x_ref, dst_ref=buf_ref.at[slot], send_sem=send_sems.at[k], recv_sem=recv_sems.at[k],
                device_id=peers[k][0], device_id_type=MESH)

        sends = [copy(k, me) for k in range(N_DEV - 1)]
        for cp in sends:
            cp.start()
        for k in range(N_DEV - 1):
            copy(k, peers[k][1]).wait_recv()
        for cp in sends:
            cp.wait_send()
        acc = buf_ref[0]
        for s in range(1, N_DEV):
            acc = acc + buf_ref[s]
        o_ref[...] = acc

    vmem = BS(memory_space=pltpu.VMEM)
    return pl.pallas_call(
        body, name="allreduce_small", in_specs=[vmem, BS(memory_space=pl.ANY)], out_specs=vmem,
        out_shape=SDS((rows, 128), F32),
        scratch_shapes=[pltpu.VMEM((N_DEV, rows, 128), F32), pltpu.SemaphoreType.DMA((7,)),
                        pltpu.SemaphoreType.DMA((7,))],
        compiler_params=pltpu.CompilerParams(has_side_effects=True),
    )(vec, behind)


def _ffn_forward(x, hn, fetch, names, tag, next_g):
    gu, act = _ffn_up(hn, fetch(names[0], hn).reshape(2, F, D), tag)
    got = _mm_nn(act[None], fetch(names[1], act)[None], f"down_{tag}", res=x, scale=0.5, next_g=next_g)
    out, hn_next = got if next_g is not None else (got, None)
    return out, hn_next, (x, hn, gu, act)


def _ffn_backward(dxo, dxo_b, saved, norm_g, wut, wd, tag, send):
    x, hn, gu, act = saved
    d_wd = _mm_tn(act[None], dxo_b, f"dwd_{tag}", scale=0.5)[0]
    du = _ffn_dact(dxo_b, wd, gu, send(("down",), [d_wd]), tag)
    d_wut = _mm_tn(du, hn, f"dwu_{tag}")
    token = send(("up",), [d_wut.reshape(2 * F, D)])
    return _mm_nn_norm_bwd(du, wut, x, norm_g + token[0, 0], dxo, tag)


def _mixer_forward(x, hn, fetch, bias, tables, tag, next_g):
    proj = _mm_nt_rows(hn, fetch("win", hn), f"proj_{tag}", 512, IN_W // 2, IN_W, 0, rope=(*tables, 2 * QKV_A // 3))
    qkr = proj
    outs, lses = [], []
    for grp in range(3):
        o, l = _dil_fwd(qkr, proj, grp)
        outs.append(o)
        lses.append(l)
    ya = _combine_fwd(outs, lses)
    yb, lse_b = _na_fwd(proj, bias)
    merged, za, zb = _merge_fwd(ya, yb, proj, fetch("wa", yb), fetch("wb", yb))
    out, hn_next = _mm_nn(merged[None], fetch("wo", merged)[None], f"out_{tag}", res=x, next_g=next_g)
    return out, hn_next, (x, hn, proj, qkr, outs, lses, ya, yb, lse_b, merged, za, zb)


def _mixer_backward(dxo, dxo_b, saved, norm_g, w, bias, tables, tag, send):
    wint, wat, wbt, wo = w
    x, hn, proj, qkr, outs, lses, ya, yb, lse_b, merged, za, zb = saved
    d_wo = _mm_tn(merged[None], dxo_b, f"dwo_{tag}")[0]
    dza, dzb, dlog = _merge_bwd(dxo_b, wo, za, zb, proj)
    dya = _mm_nn(dza[None], wat[None], f"dya_{tag}")
    dyb = _mm_nn(dzb[None], wbt[None], f"dyb_{tag}")
    d_wat = _mm_tn(dza[None], ya, f"dwa_{tag}")[0]
    d_wbt = _mm_tn(dzb[None], yb, f"dwb_{tag}")[0]
    cb = _combine_bwd(dya, outs, lses)
    dqs, dks, dvs = [], [], []
    for grp in range(3):
        dq, dk, dv = _dil_bwd(qkr, proj, cb[grp], cb[3 + grp], lses[grp], grp)
        dqs.append(dq)
        dks.append(dk)
        dvs.append(dv)
    dqk = _rope_bwd(dqs, dks, *tables)
    dqb, dkb, dvb, dbias_tab = _na_bwd(proj, bias, dyb, yb, lse_b)
    dbias = _na_dbias(dbias_tab)
    dproj = jnp.concatenate(
        [dqk] + [t.astype(BF16) for t in (*dvs, dqb, dkb, dvb)] + [dlog[0], dlog[1]], axis=1)
    d_wint = _mm_tn(dproj[None], hn, f"dwin_{tag}")[0]
    token = send(("win", "wa", "wb", "wo"), [d_wint, d_wat, d_wbt, d_wo])
    dx, dx_b, dg = _mm_nn_norm_bwd(dproj[None], wint[None], x, norm_g + token[0, 0], dxo, f"mix_{tag}")
    dbias = dbias[:, 0, :480].reshape(8, 15, 32)[:, :, :31]
    return dx, dx_b, dg, dbias


def _pack_small(norms, biases, final, loss=None):
    parts = []
    for layer in range(DEPTH):
        parts += [norms[0][layer], norms[1][layer], norms[2][layer],
                  jnp.pad(biases[layer].reshape(-1), (0, BIAS_PAD - 8 * 15 * 31))]
    parts.append(final)
    flat = jnp.concatenate([p.reshape(-1).astype(F32) for p in parts])
    if loss is not None:
        flat = jnp.concatenate([flat, loss.reshape(-1)])
    return jnp.pad(flat, (0, SMALL_ROWS * 128 - flat.shape[0])).reshape(SMALL_ROWS, 128)


def _unpack_small(packed):
    flat = packed.reshape(-1)
    norms, biases = ([], [], []), []
    pos = 0
    for _ in range(DEPTH):
        for k in range(3):
            norms[k].append(flat[pos:pos + D])
            pos += D
        biases.append(flat[pos:pos + 8 * 15 * 31].reshape(8, 15, 31))
        pos += BIAS_PAD
    final = flat[pos:pos + D]
    pos += D
    return [jnp.stack(n) for n in norms], jnp.stack(biases), final, flat[pos]


def kernel(x, ffn1_norm, ffn1_w_up, ffn1_w_down, mix_norm, w_in, na_rel_bias, w_branch_a, w_branch_b, w_out, ffn2_norm, ffn2_w_up, ffn2_w_down, final_norm, loss_target, m_ffn1_norm, m_ffn1_w_up, m_ffn1_w_down, m_mix_norm, m_w_in, m_na_rel_bias, m_w_branch_a, m_w_branch_b, m_w_out, m_ffn2_norm, m_ffn2_w_up, m_ffn2_w_down, m_final_norm, v_ffn1_norm, v_ffn1_w_up, v_ffn1_w_down, v_mix_norm, v_w_in, v_na_rel_bias, v_w_branch_a, v_w_branch_b, v_w_out, v_ffn2_norm, v_ffn2_w_up, v_ffn2_w_down, v_final_norm):
    t = x.shape[0] * x.shape[1]
    xs = x.reshape(t, D)
    tgt = loss_target.reshape(t, D)
    tables = _rope_tables()

    col_sharded = dict(up1=ffn1_w_up, win=w_in, wa=w_branch_a, wb=w_branch_b, up2=ffn2_w_up)
    row_sharded = dict(down1=ffn1_w_down, wo=w_out, down2=ffn2_w_down)
    shard = [{} for _ in range(DEPTH)]
    for layer in range(DEPTH):
        for name, arr in col_sharded.items():
            shard[layer][name] = arr[layer].T.astype(BF16)
        for name, arr in row_sharded.items():
            shard[layer][name] = arr[layer].astype(BF16)

    weights = [{} for _ in range(DEPTH)]
    travel = [(0, ("up1",)), (0, ("down1",)), (0, ("win",)), (0, ("wa", "wb", "wo")), (0, ("up2", "down2")),
              (1, ("up1", "down1")), (1, ("win",)), (1, ("wa", "wb", "wo")), (1, ("up2", "down2"))]
    group_of, chips_done, sibling_done = {}, {}, {}
    count = 0
    for i, (layer, names) in enumerate(travel):
        chips_done[i] = list(range(count, count + len(names)))
        count += len(names)
        for n in names:
            group_of[layer, n] = (i, names)
    gathered, token = _exchange_start(
        "gather", [shard[layer][n] for layer, names in travel for n in names], None, None, "w")
    zero = token[0, 0]

    biases = [_na_bias_table(na_rel_bias[layer] + zero) for layer in range(DEPTH)]

    def pass_on(i, behind):
        if i in chips_done:
            lands = _exchange_wait(gathered, behind, f"w{i}", which=chips_done.pop(i))
            sibling_done[i], _ = _exchange_start("forward", [], lands, None, f"p{i}")

    def fetcher(layer):
        def fetch(name, behind):
            if (layer, name) in group_of:
                i, names = group_of[layer, name]
                if i == 0:
                    behind = (behind, *biases)
                pass_on(i, behind)
                pass_on(i + 1, behind)
                for n, got in zip(names, _exchange_wait(sibling_done.pop(i), behind, f"p{i}")):
                    weights[layer][n] = got
                    del group_of[layer, n]
            return weights[layer][name]
        return fetch

    saved = []
    h = xs
    hn = _norm_fwd(xs, ffn1_norm[0] + zero, "first")
    for layer in range(DEPTH):
        bias = biases[layer]
        fetch = fetcher(layer)
        after_ffn2 = ffn1_norm[layer + 1] if layer + 1 < DEPTH else None
        h, hn, s1 = _ffn_forward(h, hn, fetch, ("up1", "down1"), f"f1l{layer}", mix_norm[layer])
        h, hn, s2 = _mixer_forward(h, hn, fetch, bias, tables, f"l{layer}", ffn2_norm[layer])
        h, hn, s3 = _ffn_forward(h, hn, fetch, ("up2", "down2"), f"f2l{layer}", after_ffn2)
        saved.append((s1, s2, s3, bias))
    loss_part, dh, dh_b, d_final = _loss_head(h, final_norm, tgt)

    d_norms = ([None] * DEPTH, [None] * DEPTH, [None] * DEPTH)
    d_bias = [None] * DEPTH
    sent = {}

    def sender(layer, suffix):
        def send(names, grads):
            tag = f"g{layer}{names[0]}{suffix}"
            handle, token = _exchange_start("scatter", grads, None, None, tag)
            for i, n in enumerate(names):
                sent[layer, n + suffix] = (handle, i, tag)
            return token
        return send

    for layer in reversed(range(DEPTH)):
        w = weights[layer]
        s1, s2, s3, bias = saved[layer]
        dh, dh_b, d_norms[2][layer] = _ffn_backward(
            dh, dh_b, s3, ffn2_norm[layer], w["up2"].reshape(2, F, D), w["down2"], f"f2l{layer}", sender(layer, "2"))
        dh, dh_b, d_norms[1][layer], d_bias[layer] = _mixer_backward(
            dh, dh_b, s2, mix_norm[layer], (w["win"], w["wa"], w["wb"], w["wo"]), bias, tables, f"l{layer}",
            sender(layer, ""))
        dh, dh_b, d_norms[0][layer] = _ffn_backward(
            dh, dh_b, s1, ffn1_norm[layer], w["up1"].reshape(2, F, D), w["down1"], f"f1l{layer}", sender(layer, "1"))
    grad_x = dh.reshape(x.shape)

    originals = dict(up1=(ffn1_w_up, m_ffn1_w_up, v_ffn1_w_up), down1=(ffn1_w_down, m_ffn1_w_down, v_ffn1_w_down),
                     win=(w_in, m_w_in, v_w_in), wa=(w_branch_a, m_w_branch_a, v_w_branch_a),
                     wb=(w_branch_b, m_w_branch_b, v_w_branch_b), wo=(w_out, m_w_out, v_w_out),
                     up2=(ffn2_w_up, m_ffn2_w_up, v_ffn2_w_up), down2=(ffn2_w_down, m_ffn2_w_down, v_ffn2_w_down))
    big = {}
    behind = dh
    landed = {}

    def received(layer, name):
        handle, i, tag = sent[layer, name]
        if tag not in landed:
            landed[tag] = _exchange_wait(handle, behind, tag)
        return landed[tag][i]

    for name in ("down2", "up2", "win", "wa", "wb", "wo", "down1", "up1"):
        g = _sum_slots(received(0, name), received(1, name), name)
        wv, mv, vv = originals[name]
        if name in col_sharded:
            wv, mv, vv = (jnp.swapaxes(t, 1, 2) for t in (wv, mv, vv))
        big[name] = (g, *_adamw(wv, g, mv, vv, name))
        behind = big[name][1]
        if name in col_sharded:
            big[name] = tuple(jnp.swapaxes(t, 1, 2) for t in big[name])

    small = _allreduce_small(_pack_small(d_norms, d_bias, d_final, loss_part[0, :1]), behind)
    g_norms, g_bias, g_final, loss = _unpack_small(small)
    w_small = _pack_small((ffn1_norm, mix_norm, ffn2_norm), na_rel_bias, final_norm)
    m_small = _pack_small((m_ffn1_norm, m_mix_norm, m_ffn2_norm), m_na_rel_bias, m_final_norm)
    v_small = _pack_small((v_ffn1_norm, v_mix_norm, v_ffn2_norm), v_na_rel_bias, v_final_norm)
    upd = _adamw(w_small[None], small[None], m_small[None], v_small[None], "small")
    small_out = [(g_norms, g_bias, g_final)] + [_unpack_small(u[0])[:3] for u in upd]

    outputs = [loss, grad_x]
    for kind in range(4):
        norms, bias_k, final_k = small_out[kind]
        outputs += [norms[0], big["up1"][kind], big["down1"][kind], norms[1], big["win"][kind], bias_k,
                    big["wa"][kind], big["wb"][kind], big["wo"][kind], norms[2], big["up2"][kind],
                    big["down2"][kind], final_k]
    return tuple(outputs)
```

```python
import numpy as np

import jax
import jax.numpy as jnp
from jax import lax
from jax.experimental import pallas as pl
from jax.experimental.pallas import tpu as pltpu

F32 = jnp.float32
BF16 = jnp.bfloat16
SDS = jax.ShapeDtypeStruct
BS = pl.BlockSpec
MESH = pl.DeviceIdType.MESH

D = 1024
S = 2048
F = 2816
DEPTH = 2
HEAD_DIM = 64
DILATIONS = (1, 4, 16)
HALF = 64
QKV_A = 2304
QKV_B = 1536
IN_W = 5888
N_DEV = 8
NA_ROWS = 32
GRID_W = 64
NA_KR = 8
ROPE_THETA = 10000.0
RMS_EPS = 1e-6
NEG = -1e30
SCALE = HEAD_DIM ** -0.5
ADAM_LR, ADAM_B1, ADAM_B2, ADAM_EPS, ADAM_WD, ADAM_STEP = 0.001, 0.9, 0.999, 1e-08, 0.01, 10
VMEM_LIMIT_V7X = 52 * 1024 * 1024
SMALL_ROWS = 120
BIAS_PAD = 3840
NA_FWD_ROWS = 8
NA_BWD_ROWS = 4
DIL_FWD_TILES = 4
DIL_BWD_TILES = 4


def _cp(*sem):
    return pltpu.CompilerParams(dimension_semantics=sem, vmem_limit_bytes=VMEM_LIMIT_V7X)


def _dot_nn(a, b):
    return jnp.dot(a, b, preferred_element_type=F32)


def _dot_nt(a, b):
    return lax.dot_general(a, b, (((1,), (1,)), ((), ())), preferred_element_type=F32)


def _dot_tn(a, b):
    return lax.dot_general(a, b, (((0,), (0,)), ((), ())), preferred_element_type=F32)


def _ds(start, size, stride):
    return pl.ds(start, size) if stride == 1 else pl.ds(start, size, stride=stride)


def _norm_fwd(x, g, tag):
    t = x.shape[0]
    tm = 512

    def body(x_ref, g_ref, o_ref):
        xv = x_ref[...]
        r = lax.rsqrt(jnp.mean(xv * xv, axis=-1, keepdims=True) + RMS_EPS)
        o_ref[...] = (xv * r * g_ref[...]).astype(BF16)

    return pl.pallas_call(
        body, name=f"norm_fwd_{tag}", grid=(t // tm,),
        in_specs=[BS((tm, D), lambda i: (i, 0)), BS((1, D), lambda i: (0, 0))],
        out_specs=BS((tm, D), lambda i: (i, 0)),
        out_shape=SDS((t, D), BF16), compiler_params=_cp("parallel"),
    )(x, g.reshape(1, D))


def _loss_head(x, g, tgt):
    t = x.shape[0]
    tm = 512

    def body(x_ref, g_ref, t_ref, loss_ref, dx_ref, dxb_ref, dg_ref):
        @pl.when(pl.program_id(0) == 0)
        def _():
            dg_ref[...] = jnp.zeros_like(dg_ref)
            loss_ref[...] = jnp.zeros_like(loss_ref)

        xv = x_ref[...]
        gv = g_ref[...]
        r = lax.rsqrt(jnp.mean(xv * xv, axis=-1, keepdims=True) + RMS_EPS)
        xh = xv * r
        e = xh * gv - t_ref[...]
        loss_ref[...] += 0.5 * jnp.sum(jnp.mean(e * e, axis=-1, keepdims=True), axis=0, keepdims=True)
        dy = e * (1.0 / D)
        u = dy * gv
        dx = r * (u - xh * jnp.mean(xh * u, axis=-1, keepdims=True))
        dx_ref[...] = dx
        dxb_ref[...] = dx.astype(BF16)
        dg_ref[...] += jnp.sum(dy * xh, axis=0, keepdims=True)

    row = BS((tm, D), lambda i: (i, 0))
    vec = BS((1, D), lambda i: (0, 0))
    return pl.pallas_call(
        body, name="loss_head", grid=(t // tm,),
        in_specs=[row, vec, row], out_specs=[BS((1, 128), lambda i: (0, 0)), row, row, vec],
        out_shape=[SDS((1, 128), F32), SDS((t, D), F32), SDS((t, D), BF16), SDS((1, D), F32)],
        compiler_params=_cp("arbitrary"),
    )(x, g.reshape(1, D), tgt)


def _mm_nn(a, w, tag, res=None, scale=1.0, tm=512, tn=None, next_g=None):
    c_n, t, k = a.shape
    n = w.shape[2]
    tn = n if tn is None else tn
    assert next_g is None or tn == n
    n_in = 2 + (res is not None) + (next_g is not None)

    def body(*refs):
        a_ref, w_ref = refs[0], refs[1]
        acc = _dot_nn(a_ref[0].astype(BF16), w_ref[0])
        for c in range(1, c_n):
            acc = acc + _dot_nn(a_ref[c].astype(BF16), w_ref[c])
        if scale != 1.0:
            acc = acc * scale
        if res is not None:
            acc = refs[2][...] + acc
        refs[n_in][...] = acc
        if next_g is not None:
            r = lax.rsqrt(jnp.mean(acc * acc, axis=-1, keepdims=True) + RMS_EPS)
            refs[n_in + 1][...] = (acc * r * refs[n_in - 1][...]).astype(BF16)

    in_specs = [BS((c_n, tm, k), lambda i, j: (0, i, 0)), BS((c_n, k, tn), lambda i, j: (0, 0, j))]
    args = [a, w]
    out_specs = [BS((tm, tn), lambda i, j: (i, j))]
    out_shape = [SDS((t, n), F32)]
    if res is not None:
        in_specs.append(BS((tm, tn), lambda i, j: (i, j)))
        args.append(res)
    if next_g is not None:
        in_specs.append(BS((1, n), lambda i, j: (0, 0)))
        args.append(next_g.reshape(1, n))
        out_specs.append(BS((tm, tn), lambda i, j: (i, j)))
        out_shape.append(SDS((t, n), BF16))
    got = pl.pallas_call(
        body, name=f"mm_nn_{tag}", grid=(t // tm, n // tn), in_specs=in_specs, out_specs=out_specs,
        out_shape=out_shape, compiler_params=_cp("parallel", "parallel"),
    )(*args)
    return got if next_g is not None else got[0]


def _mm_nn_norm_bwd(a, w, x, g, dres, tag, tm=256):
    c_n, t, k = a.shape

    def body(a_ref, w_ref, x_ref, g_ref, dr_ref, dx_ref, dxb_ref, dg_ref):
        @pl.when(pl.program_id(0) == 0)
        def _():
            dg_ref[...] = jnp.zeros_like(dg_ref)

        dh = _dot_nn(a_ref[0], w_ref[0])
        for c in range(1, c_n):
            dh = dh + _dot_nn(a_ref[c], w_ref[c])
        xv = x_ref[...]
        r = lax.rsqrt(jnp.mean(xv * xv, axis=-1, keepdims=True) + RMS_EPS)
        xh = xv * r
        u = dh * g_ref[...]
        dx = dr_ref[...] + r * (u - xh * jnp.mean(xh * u, axis=-1, keepdims=True))
        dx_ref[...] = dx
        dxb_ref[...] = dx.astype(BF16)
        dg_ref[...] += jnp.sum(dh * xh, axis=0, keepdims=True)

    row = BS((tm, D), lambda i: (i, 0))
    vec = BS((1, D), lambda i: (0, 0))
    return pl.pallas_call(
        body, name=f"mm_nn_norm_bwd_{tag}", grid=(t // tm,),
        in_specs=[BS((c_n, tm, k), lambda i: (0, i, 0)), BS((c_n, k, D), lambda i: (0, 0, 0)), row, vec, row],
        out_specs=[row, row, vec], out_shape=[SDS((t, D), F32), SDS((t, D), BF16), SDS((1, D), F32)],
        compiler_params=_cp("arbitrary"),
    )(a, w, x, g.reshape(1, D), dres)


def _mm_nt_rows(a, w, tag, tm, tn, n_total, w_row0, rope=None):
    t, k = a.shape
    assert w_row0 % tn == 0 and n_total % tn == 0
    j0 = w_row0 // tn

    def body(a_ref, w_ref, *rest):
        o_ref = rest[-1]
        o_ref[...] = _dot_nt(a_ref[...].astype(BF16), w_ref[...])
        if rope is not None:
            @pl.when(pl.program_id(0) == 0)
            def _():
                c = rest[0][...]
                sg = rest[1][...]
                first = (lax.broadcasted_iota(jnp.int32, (tm, 128), 1) % HEAD_DIM) < HEAD_DIM // 2
                for col in range(0, rope[2], 128):
                    v = o_ref[:, col:col + 128]
                    o_ref[:, col:col + 128] = v * c + _swap_halves(v, first) * sg

    in_specs = [BS((tm, k), lambda j, i: (i, 0)), BS((tn, k), lambda j, i: (j0 + j, 0))]
    args = [a, w]
    if rope is not None:
        assert rope[2] <= tn
        in_specs += [BS((tm, 128), lambda j, i: (i % (S // tm), 0))] * 2
        args += [rope[0], rope[1]]
    return pl.pallas_call(
        body, name=f"mm_nt_{tag}", grid=(n_total // tn, t // tm), in_specs=in_specs,
        out_specs=BS((tm, tn), lambda j, i: (i, j)), out_shape=SDS((t, n_total), F32),
        compiler_params=_cp("parallel", "parallel"),
    )(*args)


def _mm_tn(a, b, tag, scale=1.0, tmm=256, tk=None):
    c_n, t, m = a.shape
    n = b.shape[1]
    tk = t if tk is None else tk
    nk = t // tk

    def body_one(a_ref, b_ref, o_ref):
        o_ref[...] = (_dot_tn(a_ref[...].astype(BF16), b_ref[...].astype(BF16)) * scale).astype(BF16)

    def body_acc(a_ref, b_ref, o_ref, acc_ref):
        kk = pl.program_id(2)

        @pl.when(kk == 0)
        def _():
            acc_ref[...] = jnp.zeros_like(acc_ref)

        acc_ref[...] += _dot_tn(a_ref[...].astype(BF16), b_ref[...].astype(BF16))

        @pl.when(kk == nk - 1)
        def _():
            o_ref[...] = (acc_ref[...] * scale).astype(BF16)

    return pl.pallas_call(
        body_one if nk == 1 else body_acc, name=f"mm_tn_{tag}", grid=(c_n, m // tmm, nk),
        in_specs=[BS((None, tk, tmm), lambda c, mi, kk: (c, kk, mi)), BS((tk, n), lambda c, mi, kk: (kk, 0))],
        out_specs=BS((None, tmm, n), lambda c, mi, kk: (c, mi, 0)),
        out_shape=SDS((c_n, m, n), BF16), scratch_shapes=[] if nk == 1 else [pltpu.VMEM((tmm, n), F32)],
        compiler_params=_cp("parallel", "parallel", "arbitrary"),
    )(a, b)


def _ffn_up(hn, wut, tag):
    t = hn.shape[0]
    tm, tn = 512, 1408

    def body(h_ref, w_ref, gu_ref, act_ref):
        h = h_ref[...]
        g = _dot_nt(h, w_ref[0])
        u = _dot_nt(h, w_ref[1])
        sg = jax.nn.sigmoid(g)
        silu = g * sg
        gu_ref[0] = (u * (sg + silu * (1.0 - sg))).astype(BF16)
        gu_ref[1] = silu.astype(BF16)
        act_ref[...] = (silu * u).astype(BF16)

    return pl.pallas_call(
        body, name=f"ffn_up_{tag}", grid=(F // tn, t // tm),
        in_specs=[BS((tm, D), lambda j, i: (i, 0)), BS((2, tn, D), lambda j, i: (0, j, 0))],
        out_specs=[BS((2, tm, tn), lambda j, i: (0, i, j)), BS((tm, tn), lambda j, i: (i, j))],
        out_shape=[SDS((2, t, F), BF16), SDS((t, F), BF16)],
        compiler_params=_cp("parallel", "parallel"),
    )(hn, wut)


def _ffn_dact(dxo, wd, gu, tie, tag):
    t = dxo.shape[0]
    tm, tn = 512, 1408

    def body(d_ref, w_ref, gu_ref, tie_ref, o_ref):
        dact = _dot_nt(d_ref[...] * 0.5, w_ref[...])
        o_ref[0] = (dact * gu_ref[0].astype(F32)).astype(BF16)
        o_ref[1] = (dact * gu_ref[1].astype(F32)).astype(BF16)

    return pl.pallas_call(
        body, name=f"ffn_dact_{tag}", grid=(F // tn, t // tm),
        in_specs=[BS((tm, D), lambda j, i: (i, 0)), BS((tn, D), lambda j, i: (j, 0)),
                  BS((2, tm, tn), lambda j, i: (0, i, j)), BS((8, 128), lambda j, i: (0, 0))],
        out_specs=BS((2, tm, tn), lambda j, i: (0, i, j)),
        out_shape=SDS((2, t, F), BF16), compiler_params=_cp("parallel", "parallel"),
    )(dxo, wd, gu, tie)


def _rope_tables():
    half = HEAD_DIM // 2
    inv_freq = ROPE_THETA ** (-jnp.arange(half, dtype=F32) / half)
    ang = jnp.arange(S).astype(F32)[:, None] * inv_freq[None, :]
    cos, sin = jnp.cos(ang), jnp.sin(ang)
    return jnp.concatenate([cos, cos, cos, cos], axis=1), jnp.concatenate([-sin, sin, -sin, sin], axis=1)


def _swap_halves(t, first_half):
    return jnp.where(first_half, pltpu.roll(t, 96, 1), pltpu.roll(t, 32, 1))


def _rope_bwd(dqs, dks, cos_t, sin_t):
    t = dqs[0].shape[0]
    tm = 512

    def body(*refs):
        c = refs[6][...]
        sg = refs[7][...]
        o_ref = refs[8]
        first = (lax.broadcasted_iota(jnp.int32, (tm, 128), 1) % HEAD_DIM) < HEAD_DIM // 2
        for a in range(6):
            for hp in range(2):
                v = refs[a][:, 128 * hp:128 * (hp + 1)]
                col = 128 * (2 * a + hp)
                o_ref[:, col:col + 128] = (v * c + _swap_halves(v * sg, first)).astype(BF16)

    blk = BS((tm, 256), lambda i: (i, 0))
    tab = BS((tm, 128), lambda i: (i % (S // tm), 0))
    return pl.pallas_call(
        body, name="rope_bwd", grid=(t // tm,), in_specs=[blk] * 6 + [tab, tab],
        out_specs=BS((tm, 1536), lambda i: (i, 0)), out_shape=SDS((t, 1536), BF16),
        compiler_params=_cp("parallel"),
    )(*dqs, *dks, cos_t, sin_t)


def _head_masks():
    lane = lax.broadcasted_iota(jnp.int32, (1, 128), 1)
    m0 = (lane < HEAD_DIM).astype(F32)
    return m0, 1.0 - m0


def _dil_geometry(d):
    sub = S // d
    q_rows = 128
    k_rows = min(256, sub)
    return sub, q_rows, sub // q_rows, k_rows


def _dil_tile(idx, d, keys_on_rows=False):
    sub, q_rows, nb, k_rows = _dil_geometry(d)
    r = idx // nb
    n = idx % nb
    k_sub = jnp.clip(q_rows * n - HALF, 0, sub - k_rows)
    if d == 1:
        q_start = pl.multiple_of(q_rows * n, q_rows)
        k_start = pl.multiple_of(k_sub, HALF)
    else:
        q_start = q_rows * n * d + r
        k_start = k_sub * d + r
    if keys_on_rows:
        ii = lax.broadcasted_iota(jnp.int32, (k_rows, 2 * q_rows), 1) % q_rows
        jj = lax.broadcasted_iota(jnp.int32, (k_rows, 2 * q_rows), 0)
    else:
        ii = lax.broadcasted_iota(jnp.int32, (q_rows, k_rows), 0)
        jj = lax.broadcasted_iota(jnp.int32, (q_rows, k_rows), 1)
    valid = jnp.abs(jj - ii + (k_sub - q_rows * n)) <= HALF
    return q_start, k_start, valid


def _dil_specs(grp):
    qs = BS((S, 128), lambda b, hp: (b, 2 * grp + hp))
    ks = BS((S, 128), lambda b, hp: (b, 6 + 2 * grp + hp))
    vs = BS((S, 128), lambda b, hp: (b, 12 + 2 * grp + hp))
    own = BS((S, 128), lambda b, hp: (b, hp))
    return qs, ks, vs, own


def _dil_fwd(qkr, proj, grp):
    t = qkr.shape[0]
    d = DILATIONS[grp]
    _, q_rows, nb, k_rows = _dil_geometry(d)

    def body(q_ref, k_ref, v_ref, o_ref, l_ref):
        masks = _head_masks()

        def step(i0, carry):
            geo = [_dil_tile(i0 * DIL_FWD_TILES + j, d) for j in range(DIL_FWD_TILES)]
            tiles = [(j, h) for j in range(DIL_FWD_TILES) for h in range(2)]
            qs = [q_ref[_ds(g[0], q_rows, d), :] for g in geo]
            kbs = [k_ref[_ds(g[1], k_rows, d), :].astype(BF16) for g in geo]
            ss = [jnp.where(geo[j][2], _dot_nt((qs[j] * masks[h]).astype(BF16), kbs[j]) * SCALE, NEG) for j, h in tiles]
            mxs = [jnp.max(s, axis=1, keepdims=True) for s in ss]
            ps = [jnp.exp(s - mx) for s, mx in zip(ss, mxs)]
            dens = [jnp.sum(p, axis=1, keepdims=True) for p in ps]
            vs = [v_ref[_ds(g[1], k_rows, d), :] for g in geo]
            outs = [_dot_nn(p.astype(BF16), (vs[j] * masks[h]).astype(BF16)) / den
                    for p, den, (j, h) in zip(ps, dens, tiles)]
            for j, g in enumerate(geo):
                o_ref[_ds(g[0], q_rows, d), :] = outs[2 * j] + outs[2 * j + 1]
                l_ref[_ds(g[0], q_rows, d), :] = (
                    (mxs[2 * j] + jnp.log(dens[2 * j])) * masks[0] + (mxs[2 * j + 1] + jnp.log(dens[2 * j + 1])) * masks[1])
            return carry

        lax.fori_loop(0, d * nb // DIL_FWD_TILES, step, 0)

    qs, ks, vs, own = _dil_specs(grp)
    return pl.pallas_call(
        body, name=f"dil_fwd_{grp}", grid=(t // S, 2), in_specs=[qs, ks, vs], out_specs=[own, own],
        out_shape=[SDS((t, 256), F32), SDS((t, 256), F32)], compiler_params=_cp("parallel", "parallel"),
    )(qkr, qkr, proj)


def _dil_bwd(qkr, proj, do, dlp, lse, grp):
    t = qkr.shape[0]
    d = DILATIONS[grp]
    _, q_rows, nb, k_rows = _dil_geometry(d)

    def body(q_ref, k_ref, v_ref, do_ref, dl_ref, l_ref, dq_ref, dk_ref, dv_ref):
        masks = _head_masks()
        dk_ref[...] = jnp.zeros_like(dk_ref)
        dv_ref[...] = jnp.zeros_like(dv_ref)

        def as_row(x2):
            xt = x2.T
            return jnp.concatenate([xt[0:1], xt[HEAD_DIM:HEAD_DIM + 1]], axis=1)

        def step(i0, carry):
            geo = [_dil_tile(i0 * DIL_BWD_TILES + j, d, keys_on_rows=True) for j in range(DIL_BWD_TILES)]
            q_ds = [_ds(g[0], q_rows, d) for g in geo]
            k_ds = [_ds(g[1], k_rows, d) for g in geo]
            qbs = [_both_heads(q_ref[r, :], masks).astype(BF16) for r in q_ds]
            kbs = [k_ref[r, :].astype(BF16) for r in k_ds]
            vbs = [v_ref[r, :].astype(BF16) for r in k_ds]
            dobs = [_both_heads(do_ref[r, :], masks).astype(BF16) for r in q_ds]
            l_rows = [as_row(l_ref[r, :]) for r in q_ds]
            dl_rows = [as_row(dl_ref[r, :]) for r in q_ds]
            ss = [jnp.where(g[2], _dot_nt(kb, qb) * SCALE, NEG) for g, kb, qb in zip(geo, kbs, qbs)]
            ps = [jnp.exp(s - lr) for s, lr in zip(ss, l_rows)]
            dps = [_dot_nt(vb, dob) for vb, dob in zip(vbs, dobs)]
            dss = [(p * (dp - dr)).astype(BF16) for p, dp, dr in zip(ps, dps, dl_rows)]
            dks = [_dot_nn(ds, qb) for ds, qb in zip(dss, qbs)]
            dvs = [_dot_nn(p.astype(BF16), dob) for p, dob in zip(ps, dobs)]
            dqs = [_own_heads(_dot_tn(ds, kb), masks) for ds, kb in zip(dss, kbs)]
            for j in range(DIL_BWD_TILES):
                dq_ref[q_ds[j], :] = dqs[j] * SCALE
                dk_ref[k_ds[j], :] += dks[j] * SCALE
                dv_ref[k_ds[j], :] += dvs[j]
            return carry

        lax.fori_loop(0, d * nb // DIL_BWD_TILES, step, 0)

    qs, ks, vs, own = _dil_specs(grp)
    return pl.pallas_call(
        body, name=f"dil_bwd_{grp}", grid=(t // S, 2), in_specs=[qs, ks, vs, own, own, own],
        out_specs=[own, own, own], out_shape=[SDS((t, 256), F32)] * 3,
        compiler_params=_cp("parallel", "parallel"),
    )(qkr, qkr, proj, do, dlp, lse)


def _mix_weights(l0, l1, l2):
    mx = jnp.maximum(jnp.maximum(l0, l1), l2)
    e0, e1, e2 = jnp.exp(l0 - mx), jnp.exp(l1 - mx), jnp.exp(l2 - mx)
    den = e0 + e1 + e2
    return e0 / den, e1 / den, e2 / den


def _combine_fwd(outs, lses):
    t = outs[0].shape[0]
    tm = 512

    def body(o0, o1, o2, l0, l1, l2, y_ref):
        w0, w1, w2 = _mix_weights(l0[...], l1[...], l2[...])
        y_ref[...] = w0 * o0[...] + w1 * o1[...] + w2 * o2[...]

    blk = BS((tm, 256), lambda i: (i, 0))
    return pl.pallas_call(
        body, name="combine_fwd", grid=(t // tm,), in_specs=[blk] * 6, out_specs=blk,
        out_shape=SDS((t, 256), F32), compiler_params=_cp("parallel"),
    )(*outs, *lses)


def _head_sum(x):
    a = lax.broadcasted_iota(jnp.int32, (256, 256), 0) // HEAD_DIM
    b = lax.broadcasted_iota(jnp.int32, (256, 256), 1) // HEAD_DIM
    ones = (a == b).astype(BF16)
    hi = x.astype(BF16)
    lo = (x - hi.astype(F32)).astype(BF16)
    return _dot_nn(hi, ones) + _dot_nn(lo, ones)


def _combine_bwd(dya, outs, lses):
    t = dya.shape[0]
    tm = 512

    def body(dy_ref, o0, o1, o2, l0, l1, l2, d0, d1, d2, e0, e1, e2):
        ws = _mix_weights(l0[...], l1[...], l2[...])
        dy = dy_ref[...]
        ya = ws[0] * o0[...] + ws[1] * o1[...] + ws[2] * o2[...]
        hs = _head_sum(dy * ya)
        for w, d_ref, e_ref in zip(ws, (d0, d1, d2), (e0, e1, e2)):
            d_ref[...] = w * dy
            e_ref[...] = w * hs

    blk = BS((tm, 256), lambda i: (i, 0))
    return pl.pallas_call(
        body, name="combine_bwd", grid=(t // tm,), in_specs=[blk] * 7, out_specs=[blk] * 6,
        out_shape=[SDS((t, 256), F32)] * 6, compiler_params=_cp("parallel"),
    )(dya, *outs, *lses)


def _na_bias_table(rel_bias):
    kw = NA_KR * GRID_W
    rev = jnp.pad(rel_bias.astype(F32)[:, :, ::-1], ((0, 0), (0, 0), (0, 128 - 31)))

    def body(r_ref, o_ref):
        lane = lax.broadcasted_iota(jnp.int32, (GRID_W, 128), 1)
        j = lax.broadcasted_iota(jnp.int32, (GRID_W, 128), 0)
        q = lane % GRID_W
        win_lo = jnp.clip(q - 8, 0, GRID_W - 16)
        valid = (j >= win_lo) & (j < win_lo + 16)
        for cls in range(NA_KR):
            for k in range(NA_KR):
                tiles = []
                for h in range(2):
                    row = jnp.broadcast_to(r_ref[h, cls + k:cls + k + 1, :], (GRID_W, 128))
                    tiles.append(pltpu.roll(row, (128 - 15 + GRID_W * h) % 128, 1, stride=1, stride_axis=0))
                o_ref[cls, GRID_W * k:GRID_W * (k + 1), :] = jnp.where(
                    valid, jnp.where(lane < GRID_W, tiles[0], tiles[1]), NEG)

    return pl.pallas_call(
        body, name="na_bias_table", grid=(4,),
        in_specs=[BS((2, 2 * NA_KR - 1, 128), lambda hp: (hp, 0, 0))],
        out_specs=BS((None, NA_KR, kw, 128), lambda hp: (hp, 0, 0, 0)),
        out_shape=SDS((4, NA_KR, kw, 128), F32), compiler_params=_cp("parallel"),
    )(rev)


def _na_row(i):
    lo = jnp.clip(i - NA_KR // 2, 0, NA_ROWS - NA_KR)
    return pl.multiple_of(GRID_W * i, GRID_W), pl.multiple_of(GRID_W * lo, GRID_W), lo - i + NA_KR - 1


def _both_heads(x, masks):
    return jnp.concatenate([x * masks[0], x * masks[1]], axis=0)


def _own_heads(r, masks):
    half = r.shape[0] // 2
    return r[:half] * masks[0] + r[half:] * masks[1]


def _na_fwd(proj, bias):
    t = proj.shape[0]
    kw = NA_KR * GRID_W

    def body(q_ref, k_ref, v_ref, b_ref, o_ref, l_ref):
        masks = _head_masks()

        def step(i0, carry):
            idx = [i0 * NA_FWD_ROWS + j for j in range(NA_FWD_ROWS)]
            rows = [_na_row(i) for i in idx]
            qbs = [_both_heads(q_ref[pl.ds(r[0], GRID_W), :], masks).astype(BF16) for r in rows]
            kbs = [k_ref[pl.ds(r[1], kw), :].astype(BF16) for r in rows]
            ss = [_dot_nt(kb, qb) * SCALE + b_ref[r[2]] for kb, qb, r in zip(kbs, qbs, rows)]
            mxs = [jnp.max(s, axis=0, keepdims=True) for s in ss]
            ps = [jnp.exp(s - mx) for s, mx in zip(ss, mxs)]
            dens = [jnp.sum(p, axis=0, keepdims=True) for p in ps]
            pbs = [(p / den).astype(BF16) for p, den in zip(ps, dens)]
            vbs = [v_ref[pl.ds(r[1], kw), :].astype(BF16) for r in rows]
            outs = [_own_heads(_dot_tn(pb, vb), masks) for pb, vb in zip(pbs, vbs)]
            for j, r in enumerate(rows):
                o_ref[pl.ds(r[0], GRID_W), :] = outs[j]
                l_ref[pl.ds(idx[j], 1), :] = mxs[j] + jnp.log(dens[j])
            return carry

        lax.fori_loop(0, NA_ROWS // NA_FWD_ROWS, step, 0)

    c0 = QKV_A // 128
    return pl.pallas_call(
        body, name="na_fwd", grid=(t // S, 4),
        in_specs=[BS((S, 128), lambda b, hp: (b, c0 + hp)), BS((S, 128), lambda b, hp: (b, c0 + 4 + hp)),
                  BS((S, 128), lambda b, hp: (b, c0 + 8 + hp)),
                  BS((None, NA_KR, kw, 128), lambda b, hp: (hp, 0, 0, 0))],
        out_specs=[BS((S, 128), lambda b, hp: (b, hp)), BS((None, None, NA_ROWS, 128), lambda b, hp: (b, hp, 0, 0))],
        out_shape=[SDS((t, 512), F32), SDS((t // S, 4, NA_ROWS, 128), F32)],
        compiler_params=_cp("parallel", "parallel"),
    )(proj, proj, proj, bias)


def _na_bwd(proj, bias, dyb, yb, lse):
    t = proj.shape[0]
    kw = NA_KR * GRID_W

    def body(q_ref, k_ref, v_ref, b_ref, do_ref, o_ref, l_ref, dq_ref, dk_ref, dv_ref, db_ref):
        masks = _head_masks()
        ones = jnp.ones((8, 128), BF16)

        @pl.when(pl.program_id(1) == 0)
        def _():
            db_ref[...] = jnp.zeros_like(db_ref)

        dk_ref[...] = jnp.zeros_like(dk_ref)
        dv_ref[...] = jnp.zeros_like(dv_ref)

        def row_sums(x):
            hi = x.astype(BF16)
            lo = (x - hi.astype(F32)).astype(BF16)
            return (_dot_nt(ones, hi) + _dot_nt(ones, lo))[0:1]

        def step(i0, carry):
            idx = [i0 * NA_BWD_ROWS + j for j in range(NA_BWD_ROWS)]
            rows = [_na_row(i) for i in idx]
            q_ds = [pl.ds(r[0], GRID_W) for r in rows]
            k_ds = [pl.ds(r[1], kw) for r in rows]
            qbs = [_both_heads(q_ref[r, :], masks).astype(BF16) for r in q_ds]
            kbs = [k_ref[r, :].astype(BF16) for r in k_ds]
            vbs = [v_ref[r, :].astype(BF16) for r in k_ds]
            dos = [do_ref[r, :] for r in q_ds]
            dobs = [_both_heads(do, masks).astype(BF16) for do in dos]
            deltas = [row_sums(_both_heads(do * o_ref[r, :], masks)) for do, r in zip(dos, q_ds)]
            ss = [_dot_nt(kb, qb) * SCALE + b_ref[r[2]] for kb, qb, r in zip(kbs, qbs, rows)]
            ps = [jnp.exp(s - l_ref[pl.ds(i, 1), :]) for s, i in zip(ss, idx)]
            dps = [_dot_nt(vb, dob) for vb, dob in zip(vbs, dobs)]
            dss = [p * (dp - delta) for p, dp, delta in zip(ps, dps, deltas)]
            for ds, r in zip(dss, rows):
                db_ref[r[2]] += ds
            dsbs = [ds.astype(BF16) for ds in dss]
            dks = [_dot_nn(dsb, qb) for dsb, qb in zip(dsbs, qbs)]
            dvs = [_dot_nn(p.astype(BF16), dob) for p, dob in zip(ps, dobs)]
            dqs = [_own_heads(_dot_tn(dsb, kb), masks) for dsb, kb in zip(dsbs, kbs)]
            for j in range(NA_BWD_ROWS):
                dq_ref[q_ds[j], :] = dqs[j] * SCALE
                dk_ref[k_ds[j], :] += dks[j] * SCALE
                dv_ref[k_ds[j], :] += dvs[j]
            return carry

        lax.fori_loop(0, NA_ROWS // NA_BWD_ROWS, step, 0)

    c0 = QKV_A // 128
    own = BS((S, 128), lambda hp, b: (b, hp))
    tab = BS((None, NA_KR, kw, 128), lambda hp, b: (hp, 0, 0, 0))
    return pl.pallas_call(
        body, name="na_bwd", grid=(4, t // S),
        in_specs=[BS((S, 128), lambda hp, b: (b, c0 + hp)), BS((S, 128), lambda hp, b: (b, c0 + 4 + hp)),
                  BS((S, 128), lambda hp, b: (b, c0 + 8 + hp)), tab, own, own,
                  BS((None, None, NA_ROWS, 128), lambda hp, b: (b, hp, 0, 0))],
        out_specs=[own, own, own, tab],
        out_shape=[SDS((t, 512), F32)] * 3 + [SDS((4, NA_KR, kw, 128), F32)],
        compiler_params=_cp("parallel", "arbitrary"),
    )(proj, proj, proj, bias, dyb, yb, lse)


def _na_dbias_lane_map():
    kw = NA_KR * GRID_W
    lane = np.arange(kw)
    blk, m = lane // GRID_W, lane % GRID_W
    target = np.full(kw, -1)
    target[m < 16] = (blk * 32 + 15 + m)[m < 16]
    target[m >= 49] = (((blk + 1) % NA_KR) * 32 + m - 49)[m >= 49]
    return jnp.asarray(target[:, None] == np.arange(kw)[None, :], BF16)


def _na_dbias(db):
    kw = NA_KR * GRID_W

    def body(x_ref, map_ref, o_ref, z_ref):
        for cls in range(NA_KR):
            xt = x_ref[cls].T
            for h in range(2):
                xv = xt[GRID_W * h:GRID_W * (h + 1)]
                y = xv[0:8]
                for g in range(1, GRID_W // 8):
                    y = y + pltpu.roll(xv[8 * g:8 * g + 8], kw - 8 * g, 1)
                d = y[0:1]
                for s in range(1, 8):
                    d = d + pltpu.roll(y[s:s + 1], kw - s, 1)
                z_ref[h, cls:cls + 1, :] = d
        for h in range(2):
            z = z_ref[h]
            hi = z.astype(BF16)
            lo = (z - hi.astype(F32)).astype(BF16)
            e = _dot_nn(hi, map_ref[...]) + _dot_nn(lo, map_ref[...])
            out = e[0:1]
            for cls in range(1, NA_KR):
                out = out + pltpu.roll(e[cls:cls + 1], 32 * cls, 1)
            o_ref[h] = jnp.broadcast_to(out, (8, kw))

    return pl.pallas_call(
        body, name="na_dbias", grid=(4,),
        in_specs=[BS((None, NA_KR, kw, 128), lambda hp: (hp, 0, 0, 0)), BS((kw, kw), lambda hp: (0, 0))],
        out_specs=BS((2, 8, kw), lambda hp: (hp, 0, 0)), out_shape=SDS((8, 8, kw), F32),
        scratch_shapes=[pltpu.VMEM((2, 8, kw), F32)], compiler_params=_cp("parallel"),
    )(db, _na_dbias_lane_map())


def _merge_fwd(ya, yb, proj, wat, wbt):
    t = ya.shape[0]
    tm, tn = 512, 256
    ca = (QKV_A + QKV_B) // tn
    cb = ca + D // tn

    def body(ya_ref, yb_ref, la_ref, lb_ref, wa_ref, wb_ref, m_ref, za_ref, zb_ref):
        za = _dot_nt(ya_ref[...].astype(BF16), wa_ref[...])
        zb = _dot_nt(yb_ref[...].astype(BF16), wb_ref[...])
        m_ref[...] = (jax.nn.sigmoid(la_ref[...]) * za + jax.nn.sigmoid(lb_ref[...]) * zb).astype(BF16)
        za_ref[...] = za.astype(BF16)
        zb_ref[...] = zb.astype(BF16)

    out = BS((tm, tn), lambda i, j: (i, j))
    return pl.pallas_call(
        body, name="merge_fwd", grid=(t // tm, D // tn),
        in_specs=[BS((tm, 256), lambda i, j: (i, 0)), BS((tm, 512), lambda i, j: (i, 0)),
                  BS((tm, tn), lambda i, j: (i, ca + j)), BS((tm, tn), lambda i, j: (i, cb + j)),
                  BS((tn, 256), lambda i, j: (j, 0)), BS((tn, 512), lambda i, j: (j, 0))],
        out_specs=[out, out, out], out_shape=[SDS((t, D), BF16)] * 3,
        compiler_params=_cp("parallel", "parallel"),
    )(ya, yb, proj, proj, wat, wbt)


def _merge_bwd(dxo, wo, za, zb, proj):
    t = dxo.shape[0]
    tm, tn = 512, 256
    ca = (QKV_A + QKV_B) // tn
    cb = ca + D // tn

    def body(d_ref, w_ref, za_ref, zb_ref, la_ref, lb_ref, dza_ref, dzb_ref, dl_ref):
        dmv = _dot_nt(d_ref[...], w_ref[...])
        ga = jax.nn.sigmoid(la_ref[...])
        gb = jax.nn.sigmoid(lb_ref[...])
        dza_ref[...] = (dmv * ga).astype(BF16)
        dzb_ref[...] = (dmv * gb).astype(BF16)
        dl_ref[0] = (dmv * za_ref[...].astype(F32) * ga * (1.0 - ga)).astype(BF16)
        dl_ref[1] = (dmv * zb_ref[...].astype(F32) * gb * (1.0 - gb)).astype(BF16)

    blk = BS((tm, tn), lambda i, j: (i, j))
    return pl.pallas_call(
        body, name="merge_bwd", grid=(t // tm, D // tn),
        in_specs=[BS((tm, D), lambda i, j: (i, 0)), BS((tn, D), lambda i, j: (j, 0)), blk, blk,
                  BS((tm, tn), lambda i, j: (i, ca + j)), BS((tm, tn), lambda i, j: (i, cb + j))],
        out_specs=[blk, blk, BS((2, tm, tn), lambda i, j: (0, i, j))],
        out_shape=[SDS((t, D), BF16), SDS((t, D), BF16), SDS((2, t, D), BF16)],
        compiler_params=_cp("parallel", "parallel"),
    )(dxo, wo, za, zb, proj, proj)


def _adamw_update(w, g, m, v):
    mn = ADAM_B1 * m + (1.0 - ADAM_B1) * g
    vn = ADAM_B2 * v + (1.0 - ADAM_B2) * (g * g)
    m_hat = mn / (1.0 - ADAM_B1 ** ADAM_STEP)
    v_hat = vn / (1.0 - ADAM_B2 ** ADAM_STEP)
    return -ADAM_LR * (m_hat / (jnp.sqrt(v_hat) + ADAM_EPS) + ADAM_WD * w), mn, vn


def _sum_adamw(recv0, recv1, w, m, v, tag):
    _, r, c = recv0.shape
    tr = max(rows for rows in range(16, r + 1, 16) if r % rows == 0 and rows * c <= 192 * 1024)

    def body(a_ref, b_ref, w_ref, m_ref, v_ref, g_ref, d_ref, mo_ref, vo_ref):
        for layer, ref in enumerate((a_ref, b_ref)):
            g = ref[0].astype(F32)
            for s in range(1, N_DEV):
                g = g + ref[s].astype(F32)
            g_ref[layer] = g
            d_ref[layer], mo_ref[layer], vo_ref[layer] = _adamw_update(w_ref[layer], g, m_ref[layer], v_ref[layer])

    slots = BS((N_DEV, tr, c), lambda i: (0, i, 0))
    blk = BS((2, tr, c), lambda i: (0, i, 0))
    return pl.pallas_call(
        body, name=f"sum_adamw_{tag}", grid=(r // tr,), in_specs=[slots, slots, blk, blk, blk],
        out_specs=[blk] * 4, out_shape=[SDS((2, r, c), F32)] * 4, compiler_params=_cp("parallel"),
    )(recv0, recv1, w, m, v)


def _adamw(w, g, m, v, tag):
    layers, r, c = w.shape
    tr = next(r // k for k in (1, 2, 4, 8) if r // k <= 384 and r % (8 * k) == 0)

    def body(w_ref, g_ref, m_ref, v_ref, d_ref, mo_ref, vo_ref):
        d_ref[...], mo_ref[...], vo_ref[...] = _adamw_update(w_ref[...], g_ref[...], m_ref[...], v_ref[...])

    blk = BS((None, tr, c), lambda l, i: (l, i, 0))
    return pl.pallas_call(
        body, name=f"adamw_{tag}", grid=(layers, r // tr), in_specs=[blk] * 4, out_specs=[blk] * 3,
        out_shape=[SDS((layers, r, c), F32)] * 3, compiler_params=_cp("parallel", "parallel"),
    )(w, g, m, v)


def _place():
    return lax.axis_index("x"), lax.axis_index("y"), lax.axis_index("c")


def _flip(coord, bit):
    return 1 - coord if bit else coord


def _peers(x, y, c):
    peers = []
    for mask in range(1, N_DEV):
        p = (_flip(x, mask & 4), _flip(y, mask & 2), _flip(c, mask & 1))
        peers.append((p, 4 * p[0] + 2 * p[1] + p[2]))
    return peers


def _copy_plan(mode, src, land, x, y, c):
    me = 4 * x + 2 * y + c

    def device(mask):
        p = (_flip(x, mask & 4), _flip(y, mask & 2), _flip(c, mask & 1))
        return p, 4 * p[0] + 2 * p[1] + p[2]

    if mode == "scatter":
        r = land.shape[1]
        return [(p, src.at[pl.ds(i * r, r), :], land.at[me], land.at[i])
                for p, i in map(device, (1, 2, 3, 4, 5, 6, 7, 0))]
    r = land.shape[0] // N_DEV

    def rows(i):
        return land.at[pl.ds(i * r, r), :]

    if mode == "gather":
        return [(p, src, rows(me), rows(i)) for p, i in map(device, (1, 4, 2, 6, 0))]
    sibling = device(1)[0]
    return [(sibling, rows(device(m)[1]), rows(device(m)[1]), rows(device(m | 1)[1])) for m in (4, 2, 6)]


COPIES = dict(scatter=8, gather=5, forward=3)
HBM_SPEC = BS(memory_space=pltpu.HBM)
SEM_SPEC = BS(memory_space=pltpu.SEMAPHORE)
DATAFLOW = pltpu.SideEffectType.DATAFLOW_SIDE_EFFECTING


def _fresh(shape, dtype, tag):
    def body(o_ref):
        del o_ref

    return pl.pallas_call(body, name=f"fresh_{tag}", out_specs=BS(memory_space=pl.ANY), out_shape=SDS(shape, dtype))()


def _exchange_start(mode, srcs, lands, after, tag):
    if lands is None and mode == "gather":
        lands = [_fresh((N_DEV * s.shape[0], s.shape[1]), s.dtype, f"{tag}_{a}") for a, s in enumerate(srcs)]
    elif lands is None:
        lands = [_fresh((N_DEV, s.shape[0] // N_DEV, s.shape[1]), s.dtype, f"{tag}_{a}") for a, s in enumerate(srcs)]
    n, n_src, n_cp = len(lands), len(srcs), COPIES[mode]
    behind = [] if after is None else [after]

    def body(*refs):
        src_refs, land_refs = refs[:n_src], refs[n_src:n_src + n]
        send_sems, recv_sems = refs[n_src + n + len(behind)], refs[n_src + n + len(behind) + 1]
        token = refs[-1]
        for a in range(n):
            plan = _copy_plan(mode, src_refs[a] if n_src else None, land_refs[a], *_place())
            for k, (p, out, there, _) in enumerate(plan):
                pltpu.make_async_remote_copy(
                    src_ref=out, dst_ref=there, send_sem=send_sems.at[n_cp * a + k],
                    recv_sem=recv_sems.at[n_cp * a + k], device_id=p, device_id_type=MESH).start()
        token[...] = jnp.zeros_like(token)

    both = [*srcs, *lands]
    res = pl.pallas_call(
        body, name=f"{mode}_start_{tag}",
        out_shape=(pltpu.SemaphoreType.DMA((n_cp * n,)), pltpu.SemaphoreType.DMA((n_cp * n,)),
                   *[pltpu.HBM(v.shape, v.dtype) for v in both], SDS((8, 128), F32)),
        in_specs=[HBM_SPEC] * len(both) + [BS(memory_space=pl.ANY)] * len(behind),
        out_specs=(SEM_SPEC, SEM_SPEC, *[HBM_SPEC] * len(both), BS(memory_space=pltpu.VMEM)),
        input_output_aliases={i: 2 + i for i in range(len(both))},
        compiler_params=pltpu.CompilerParams(has_side_effects=DATAFLOW),
    )(*[pltpu.with_memory_space_constraint(v, pltpu.HBM) for v in both], *behind)
    return (mode, res[0], res[1], res[2:2 + n_src], res[2 + n_src:2 + n_src + n]), res[-1]


def _exchange_wait(handle, after, tag, which=None):
    mode, send_sems, recv_sems, srcs, lands = handle
    which = list(range(len(lands))) if which is None else list(which)
    n_cp = COPIES[mode]
    lands = [lands[a] for a in which]
    srcs = [srcs[a] for a in which] if srcs else []
    n, n_src = len(lands), len(srcs)
    afters = list(after) if isinstance(after, (tuple, list)) else [after]

    def body(*refs):
        src_refs, land_refs = refs[:n_src], refs[n_src:n_src + n]
        send_ref, recv_ref = refs[n_src + n], refs[n_src + n + 1]
        for i, a in enumerate(which):
            plan = _copy_plan(mode, src_refs[i] if n_src else None, land_refs[i], *_place())
            for k, (p, out, _, here) in enumerate(plan):
                cp = pltpu.make_async_remote_copy(
                    src_ref=out, dst_ref=here, send_sem=send_ref.at[n_cp * a + k], recv_sem=recv_ref.at[n_cp * a + k],
                    device_id=p, device_id_type=MESH)
                cp.wait_send()
                cp.wait_recv()

    both = [*srcs, *lands]
    res = pl.pallas_call(
        body, name=f"{mode}_wait_{tag}", out_shape=tuple(pltpu.HBM(v.shape, v.dtype) for v in both),
        in_specs=[HBM_SPEC] * len(both) + [SEM_SPEC, SEM_SPEC] + [BS(memory_space=pl.ANY)] * len(afters),
        out_specs=tuple([HBM_SPEC] * len(both)),
        input_output_aliases={i: i for i in range(len(both))},
        compiler_params=pltpu.CompilerParams(has_side_effects=DATAFLOW),
    )(*both, send_sems, recv_sems, *afters)
    return list(res[n_src:])


def _allreduce_small(vec, behind):
    rows = vec.shape[0]

    def body(x_ref, behind_ref, o_ref, buf_ref, send_sems, recv_sems):
        x, y, c = _place()
        me = 4 * x + 2 * y + c
        buf_ref[me] = x_ref[...]
        peers = _peers(x, y, c)

        def copy(k, slot):
            return pltpu.make_async_remote_copy(
                src_ref=x_ref, dst_ref=buf_ref.at[slot], send_sem=send_sems.at[k], recv_sem=recv_sems.at[k],
                device_id=peers[k][0], device_id_type=MESH)

        sends = [copy(k, me) for k in range(N_DEV - 1)]
        for cp in sends:
            cp.start()
        for k in range(N_DEV - 1):
            copy(k, peers[k][1]).wait_recv()
        for cp in sends:
            cp.wait_send()
        acc = buf_ref[0]
        for s in range(1, N_DEV):
            acc = acc + buf_ref[s]
        o_ref[...] = acc

    vmem = BS(memory_space=pltpu.VMEM)
    return pl.pallas_call(
        body, name="allreduce_small", in_specs=[vmem, BS(memory_space=pl.ANY)], out_specs=vmem,
        out_shape=SDS((rows, 128), F32),
        scratch_shapes=[pltpu.VMEM((N_DEV, rows, 128), F32), pltpu.SemaphoreType.DMA((7,)),
                        pltpu.SemaphoreType.DMA((7,))],
        compiler_params=pltpu.CompilerParams(has_side_effects=True),
    )(vec, behind)


def _ffn_forward(x, hn, fetch, names, tag, next_g):
    gu, act = _ffn_up(hn, fetch(names[0], hn).reshape(2, F, D), tag)
    got = _mm_nn(act[None], fetch(names[1], act)[None], f"down_{tag}", res=x, scale=0.5, next_g=next_g)
    out, hn_next = got if next_g is not None else (got, None)
    return out, hn_next, (x, hn, gu, act)


def _ffn_backward(dxo, dxo_b, saved, norm_g, wut, wd, tag, send):
    x, hn, gu, act = saved
    d_wd = _mm_tn(act[None], dxo_b, f"dwd_{tag}", scale=0.5)[0]
    du = _ffn_dact(dxo_b, wd, gu, send(("down",), [d_wd]), tag)
    d_wut = _mm_tn(du, hn, f"dwu_{tag}")
    token = send(("up",), [d_wut.reshape(2 * F, D)])
    return _mm_nn_norm_bwd(du, wut, x, norm_g + token[0, 0], dxo, tag)


def _mixer_forward(x, hn, fetch, bias, tables, tag, next_g):
    proj = _mm_nt_rows(hn, fetch("win", hn), f"proj_{tag}", 512, IN_W // 2, IN_W, 0, rope=(*tables, 2 * QKV_A // 3))
    qkr = proj
    outs, lses = [], []
    for grp in range(3):
        o, l = _dil_fwd(qkr, proj, grp)
        outs.append(o)
        lses.append(l)
    ya = _combine_fwd(outs, lses)
    yb, lse_b = _na_fwd(proj, bias)
    merged, za, zb = _merge_fwd(ya, yb, proj, fetch("wa", yb), fetch("wb", yb))
    out, hn_next = _mm_nn(merged[None], fetch("wo", merged)[None], f"out_{tag}", res=x, next_g=next_g)
    return out, hn_next, (x, hn, proj, qkr, outs, lses, ya, yb, lse_b, merged, za, zb)


def _mixer_backward(dxo, dxo_b, saved, norm_g, w, bias, tables, tag, send):
    wint, wat, wbt, wo = w
    x, hn, proj, qkr, outs, lses, ya, yb, lse_b, merged, za, zb = saved
    d_wo = _mm_tn(merged[None], dxo_b, f"dwo_{tag}")[0]
    dza, dzb, dlog = _merge_bwd(dxo_b, wo, za, zb, proj)
    dya = _mm_nn(dza[None], wat[None], f"dya_{tag}")
    dyb = _mm_nn(dzb[None], wbt[None], f"dyb_{tag}")
    d_wat = _mm_tn(dza[None], ya, f"dwa_{tag}")[0]
    d_wbt = _mm_tn(dzb[None], yb, f"dwb_{tag}")[0]
    cb = _combine_bwd(dya, outs, lses)
    dqs, dks, dvs = [], [], []
    for grp in range(3):
        dq, dk, dv = _dil_bwd(qkr, proj, cb[grp], cb[3 + grp], lses[grp], grp)
        dqs.append(dq)
        dks.append(dk)
        dvs.append(dv)
    dqk = _rope_bwd(dqs, dks, *tables)
    dqb, dkb, dvb, dbias_tab = _na_bwd(proj, bias, dyb, yb, lse_b)
    dbias = _na_dbias(dbias_tab)
    dproj = jnp.concatenate(
        [dqk] + [t.astype(BF16) for t in (*dvs, dqb, dkb, dvb)] + [dlog[0], dlog[1]], axis=1)
    d_wint = _mm_tn(dproj[None], hn, f"dwin_{tag}")[0]
    token = send(("win", "wa", "wb", "wo"), [d_wint, d_wat, d_wbt, d_wo])
    dx, dx_b, dg = _mm_nn_norm_bwd(dproj[None], wint[None], x, norm_g + token[0, 0], dxo, f"mix_{tag}")
    dbias = dbias[:, 0, :480].reshape(8, 15, 32)[:, :, :31]
    return dx, dx_b, dg, dbias


def _pack_small(norms, biases, final, loss=None):
    parts = []
    for layer in range(DEPTH):
        parts += [norms[0][layer], norms[1][layer], norms[2][layer],
                  jnp.pad(biases[layer].reshape(-1), (0, BIAS_PAD - 8 * 15 * 31))]
    parts.append(final)
    flat = jnp.concatenate([p.reshape(-1).astype(F32) for p in parts])
    if loss is not None:
        flat = jnp.concatenate([flat, loss.reshape(-1)])
    return jnp.pad(flat, (0, SMALL_ROWS * 128 - flat.shape[0])).reshape(SMALL_ROWS, 128)


def _unpack_small(packed):
    flat = packed.reshape(-1)
    norms, biases = ([], [], []), []
    pos = 0
    for _ in range(DEPTH):
        for k in range(3):
            norms[k].append(flat[pos:pos + D])
            pos += D
        biases.append(flat[pos:pos + 8 * 15 * 31].reshape(8, 15, 31))
        pos += BIAS_PAD
    final = flat[pos:pos + D]
    pos += D
    return [jnp.stack(n) for n in norms], jnp.stack(biases), final, flat[pos]


def kernel(x, ffn1_norm, ffn1_w_up, ffn1_w_down, mix_norm, w_in, na_rel_bias, w_branch_a, w_branch_b, w_out, ffn2_norm, ffn2_w_up, ffn2_w_down, final_norm, loss_target, m_ffn1_norm, m_ffn1_w_up, m_ffn1_w_down, m_mix_norm, m_w_in, m_na_rel_bias, m_w_branch_a, m_w_branch_b, m_w_out, m_ffn2_norm, m_ffn2_w_up, m_ffn2_w_down, m_final_norm, v_ffn1_norm, v_ffn1_w_up, v_ffn1_w_down, v_mix_norm, v_w_in, v_na_rel_bias, v_w_branch_a, v_w_branch_b, v_w_out, v_ffn2_norm, v_ffn2_w_up, v_ffn2_w_down, v_final_norm):
    t = x.shape[0] * x.shape[1]
    xs = x.reshape(t, D)
    tgt = loss_target.reshape(t, D)
    tables = _rope_tables()

    col_sharded = dict(up1=ffn1_w_up, win=w_in, wa=w_branch_a, wb=w_branch_b, up2=ffn2_w_up)
    row_sharded = dict(down1=ffn1_w_down, wo=w_out, down2=ffn2_w_down)
    shard = [{} for _ in range(DEPTH)]
    for layer in range(DEPTH):
        for name, arr in col_sharded.items():
            shard[layer][name] = arr[layer].T.astype(BF16)
        for name, arr in row_sharded.items():
            shard[layer][name] = arr[layer].astype(BF16)

    weights = [{} for _ in range(DEPTH)]
    travel = [(0, ("up1",)), (0, ("down1",)), (0, ("win",)), (0, ("wa", "wb", "wo")), (0, ("up2", "down2")),
              (1, ("up1", "down1")), (1, ("win",)), (1, ("wa", "wb", "wo")), (1, ("up2", "down2"))]
    group_of, chips_done, sibling_done = {}, {}, {}
    count = 0
    for i, (layer, names) in enumerate(travel):
        chips_done[i] = list(range(count, count + len(names)))
        count += len(names)
        for n in names:
            group_of[layer, n] = (i, names)
    gathered, token = _exchange_start(
        "gather", [shard[layer][n] for layer, names in travel for n in names], None, None, "w")
    zero = token[0, 0]

    biases = [_na_bias_table(na_rel_bias[layer] + zero) for layer in range(DEPTH)]

    def pass_on(i, behind):
        if i in chips_done:
            lands = _exchange_wait(gathered, behind, f"w{i}", which=chips_done.pop(i))
            sibling_done[i], _ = _exchange_start("forward", [], lands, None, f"p{i}")

    def fetcher(layer):
        def fetch(name, behind):
            if (layer, name) in group_of:
                i, names = group_of[layer, name]
                if i == 0:
                    behind = (behind, *biases)
                pass_on(i, behind)
                pass_on(i + 1, behind)
                for n, got in zip(names, _exchange_wait(sibling_done.pop(i), behind, f"p{i}")):
                    weights[layer][n] = got
                    del group_of[layer, n]
            return weights[layer][name]
        return fetch

    saved = []
    h = xs
    hn = _norm_fwd(xs, ffn1_norm[0] + zero, "first")
    for layer in range(DEPTH):
        bias = biases[layer]
        fetch = fetcher(layer)
        after_ffn2 = ffn1_norm[layer + 1] if layer + 1 < DEPTH else None
        h, hn, s1 = _ffn_forward(h, hn, fetch, ("up1", "down1"), f"f1l{layer}", mix_norm[layer])
        h, hn, s2 = _mixer_forward(h, hn, fetch, bias, tables, f"l{layer}", ffn2_norm[layer])
        h, hn, s3 = _ffn_forward(h, hn, fetch, ("up2", "down2"), f"f2l{layer}", after_ffn2)
        saved.append((s1, s2, s3, bias))
    loss_part, dh, dh_b, d_final = _loss_head(h, final_norm, tgt)

    d_norms = ([None] * DEPTH, [None] * DEPTH, [None] * DEPTH)
    d_bias = [None] * DEPTH
    sent = {}

    def sender(layer, suffix):
        def send(names, grads):
            tag = f"g{layer}{names[0]}{suffix}"
            handle, token = _exchange_start("scatter", grads, None, None, tag)
            for i, n in enumerate(names):
                sent[layer, n + suffix] = (handle, i, tag)
            return token
        return send

    for layer in reversed(range(DEPTH)):
        w = weights[layer]
        s1, s2, s3, bias = saved[layer]
        dh, dh_b, d_norms[2][layer] = _ffn_backward(
            dh, dh_b, s3, ffn2_norm[layer], w["up2"].reshape(2, F, D), w["down2"], f"f2l{layer}", sender(layer, "2"))
        dh, dh_b, d_norms[1][layer], d_bias[layer] = _mixer_backward(
            dh, dh_b, s2, mix_norm[layer], (w["win"], w["wa"], w["wb"], w["wo"]), bias, tables, f"l{layer}",
            sender(layer, ""))
        dh, dh_b, d_norms[0][layer] = _ffn_backward(
            dh, dh_b, s1, ffn1_norm[layer], w["up1"].reshape(2, F, D), w["down1"], f"f1l{layer}", sender(layer, "1"))
    grad_x = dh.reshape(x.shape)

    originals = dict(up1=(ffn1_w_up, m_ffn1_w_up, v_ffn1_w_up), down1=(ffn1_w_down, m_ffn1_w_down, v_ffn1_w_down),
                     win=(w_in, m_w_in, v_w_in), wa=(w_branch_a, m_w_branch_a, v_w_branch_a),
                     wb=(w_branch_b, m_w_branch_b, v_w_branch_b), wo=(w_out, m_w_out, v_w_out),
                     up2=(ffn2_w_up, m_ffn2_w_up, v_ffn2_w_up), down2=(ffn2_w_down, m_ffn2_w_down, v_ffn2_w_down))
    big = {}
    behind = dh
    landed = {}

    def received(layer, name):
        handle, i, tag = sent[layer, name]
        if tag not in landed:
            landed[tag] = _exchange_wait(handle, behind, tag)
        return landed[tag][i]

    for name in ("down2", "up2", "win", "wa", "wb", "wo", "down1", "up1"):
        wv, mv, vv = originals[name]
        if name in col_sharded:
            wv, mv, vv = (jnp.swapaxes(t, 1, 2) for t in (wv, mv, vv))
        big[name] = tuple(_sum_adamw(received(0, name), received(1, name), wv, mv, vv, name))
        behind = big[name][1]
        if name in col_sharded:
            big[name] = tuple(jnp.swapaxes(t, 1, 2) for t in big[name])

    small = _allreduce_small(_pack_small(d_norms, d_bias, d_final, loss_part[0, :1]), behind)
    g_norms, g_bias, g_final, loss = _unpack_small(small)
    w_small = _pack_small((ffn1_norm, mix_norm, ffn2_norm), na_rel_bias, final_norm)
    m_small = _pack_small((m_ffn1_norm, m_mix_norm, m_ffn2_norm), m_na_rel_bias, m_final_norm)
    v_small = _pack_small((v_ffn1_norm, v_mix_norm, v_ffn2_norm), v_na_rel_bias, v_final_norm)
    upd = _adamw(w_small[None], small[None], m_small[None], v_small[None], "small")
    small_out = [(g_norms, g_bias, g_final)] + [_unpack_small(u[0])[:3] for u in upd]

    outputs = [loss, grad_x]
    for kind in range(4):
        norms, bias_k, final_k = small_out[kind]
        outputs += [norms[0], big["up1"][kind], big["down1"][kind], norms[1], big["win"][kind], bias_k,
                    big["wa"][kind], big["wb"][kind], big["wo"][kind], norms[2], big["up2"][kind],
                    big["down2"][kind], final_k]
    return tuple(outputs)
```

```python
import numpy as np

import jax
import jax.numpy as jnp
from jax import lax
from jax.experimental import pallas as pl
from jax.experimental.pallas import tpu as pltpu

F32 = jnp.float32
BF16 = jnp.bfloat16
SDS = jax.ShapeDtypeStruct
BS = pl.BlockSpec
MESH = pl.DeviceIdType.MESH

D = 1024
S = 2048
F = 2816
DEPTH = 2
HEAD_DIM = 64
DILATIONS = (1, 4, 16)
HALF = 64
QKV_A = 2304
QKV_B = 1536
IN_W = 5888
N_DEV = 8
NA_ROWS = 32
GRID_W = 64
NA_KR = 8
ROPE_THETA = 10000.0
RMS_EPS = 1e-6
NEG = -1e30
SCALE = HEAD_DIM ** -0.5
ADAM_LR, ADAM_B1, ADAM_B2, ADAM_EPS, ADAM_WD, ADAM_STEP = 0.001, 0.9, 0.999, 1e-08, 0.01, 10
VMEM_LIMIT_V7X = 52 * 1024 * 1024
SMALL_ROWS = 120
BIAS_PAD = 3840
NA_FWD_ROWS = 8
NA_BWD_ROWS = 4
DIL_FWD_TILES = 8
DIL_BWD_TILES = 4


def _cp(*sem):
    return pltpu.CompilerParams(dimension_semantics=sem, vmem_limit_bytes=VMEM_LIMIT_V7X)


def _dot_nn(a, b):
    return jnp.dot(a, b, preferred_element_type=F32)


def _dot_nt(a, b):
    return lax.dot_general(a, b, (((1,), (1,)), ((), ())), preferred_element_type=F32)


def _dot_tn(a, b):
    return lax.dot_general(a, b, (((0,), (0,)), ((), ())), preferred_element_type=F32)


def _ds(start, size, stride):
    return pl.ds(start, size) if stride == 1 else pl.ds(start, size, stride=stride)


def _norm_fwd(x, g, tag):
    t = x.shape[0]
    tm = 512

    def body(x_ref, g_ref, o_ref):
        xv = x_ref[...]
        r = lax.rsqrt(jnp.mean(xv * xv, axis=-1, keepdims=True) + RMS_EPS)
        o_ref[...] = (xv * r * g_ref[...]).astype(BF16)

    return pl.pallas_call(
        body, name=f"norm_fwd_{tag}", grid=(t // tm,),
        in_specs=[BS((tm, D), lambda i: (i, 0)), BS((1, D), lambda i: (0, 0))],
        out_specs=BS((tm, D), lambda i: (i, 0)),
        out_shape=SDS((t, D), BF16), compiler_params=_cp("parallel"),
    )(x, g.reshape(1, D))


def _loss_head(x, g, tgt):
    t = x.shape[0]
    tm = 512

    def body(x_ref, g_ref, t_ref, loss_ref, dx_ref, dxb_ref, dg_ref):
        @pl.when(pl.program_id(0) == 0)
        def _():
            dg_ref[...] = jnp.zeros_like(dg_ref)
            loss_ref[...] = jnp.zeros_like(loss_ref)

        xv = x_ref[...]
        gv = g_ref[...]
        r = lax.rsqrt(jnp.mean(xv * xv, axis=-1, keepdims=True) + RMS_EPS)
        xh = xv * r
        e = xh * gv - t_ref[...]
        loss_ref[...] += 0.5 * jnp.sum(jnp.mean(e * e, axis=-1, keepdims=True), axis=0, keepdims=True)
        dy = e * (1.0 / D)
        u = dy * gv
        dx = r * (u - xh * jnp.mean(xh * u, axis=-1, keepdims=True))
        dx_ref[...] = dx
        dxb_ref[...] = dx.astype(BF16)
        dg_ref[...] += jnp.sum(dy * xh, axis=0, keepdims=True)

    row = BS((tm, D), lambda i: (i, 0))
    vec = BS((1, D), lambda i: (0, 0))
    return pl.pallas_call(
        body, name="loss_head", grid=(t // tm,),
        in_specs=[row, vec, row], out_specs=[BS((1, 128), lambda i: (0, 0)), row, row, vec],
        out_shape=[SDS((1, 128), F32), SDS((t, D), F32), SDS((t, D), BF16), SDS((1, D), F32)],
        compiler_params=_cp("arbitrary"),
    )(x, g.reshape(1, D), tgt)


def _mm_nn(a, w, tag, res=None, scale=1.0, tm=512, tn=None, next_g=None):
    c_n, t, k = a.shape
    n = w.shape[2]
    tn = n if tn is None else tn
    assert next_g is None or tn == n
    n_in = 2 + (res is not None) + (next_g is not None)

    def body(*refs):
        a_ref, w_ref = refs[0], refs[1]
        acc = _dot_nn(a_ref[0].astype(BF16), w_ref[0])
        for c in range(1, c_n):
            acc = acc + _dot_nn(a_ref[c].astype(BF16), w_ref[c])
        if scale != 1.0:
            acc = acc * scale
        if res is not None:
            acc = refs[2][...] + acc
        refs[n_in][...] = acc
        if next_g is not None:
            r = lax.rsqrt(jnp.mean(acc * acc, axis=-1, keepdims=True) + RMS_EPS)
            refs[n_in + 1][...] = (acc * r * refs[n_in - 1][...]).astype(BF16)

    in_specs = [BS((c_n, tm, k), lambda i, j: (0, i, 0)), BS((c_n, k, tn), lambda i, j: (0, 0, j))]
    args = [a, w]
    out_specs = [BS((tm, tn), lambda i, j: (i, j))]
    out_shape = [SDS((t, n), F32)]
    if res is not None:
        in_specs.append(BS((tm, tn), lambda i, j: (i, j)))
        args.append(res)
    if next_g is not None:
        in_specs.append(BS((1, n), lambda i, j: (0, 0)))
        args.append(next_g.reshape(1, n))
        out_specs.append(BS((tm, tn), lambda i, j: (i, j)))
        out_shape.append(SDS((t, n), BF16))
    got = pl.pallas_call(
        body, name=f"mm_nn_{tag}", grid=(t // tm, n // tn), in_specs=in_specs, out_specs=out_specs,
        out_shape=out_shape, compiler_params=_cp("parallel", "parallel"),
    )(*args)
    return got if next_g is not None else got[0]


def _mm_nn_norm_bwd(parts, w, x, g, dres, tag, tm=256):
    t = parts[0].shape[1]
    n_parts = len(parts)

    def body(*refs):
        w_ref, x_ref, g_ref, dr_ref, dx_ref, dxb_ref, dg_ref = refs[n_parts:]

        @pl.when(pl.program_id(0) == 0)
        def _():
            dg_ref[...] = jnp.zeros_like(dg_ref)

        dh = None
        row = 0
        for a_ref, part in zip(refs, parts):
            for c in range(part.shape[0]):
                term = _dot_nn(a_ref[c].astype(BF16), w_ref[row:row + part.shape[2], :])
                dh = term if dh is None else dh + term
                row += part.shape[2]
        xv = x_ref[...]
        r = lax.rsqrt(jnp.mean(xv * xv, axis=-1, keepdims=True) + RMS_EPS)
        xh = xv * r
        u = dh * g_ref[...]
        dx = dr_ref[...] + r * (u - xh * jnp.mean(xh * u, axis=-1, keepdims=True))
        dx_ref[...] = dx
        dxb_ref[...] = dx.astype(BF16)
        dg_ref[...] += jnp.sum(dh * xh, axis=0, keepdims=True)

    row = BS((tm, D), lambda i: (i, 0))
    vec = BS((1, D), lambda i: (0, 0))
    return pl.pallas_call(
        body, name=f"mm_nn_norm_bwd_{tag}", grid=(t // tm,),
        in_specs=[BS((p.shape[0], tm, p.shape[2]), lambda i: (0, i, 0)) for p in parts]
        + [BS(w.shape, lambda i: (0, 0)), row, vec, row],
        out_specs=[row, row, vec], out_shape=[SDS((t, D), F32), SDS((t, D), BF16), SDS((1, D), F32)],
        compiler_params=_cp("arbitrary"),
    )(*parts, w, x, g.reshape(1, D), dres)


def _mm_nt_rows(a, w, tag, tm, tn, n_total, w_row0, rope=None):
    t, k = a.shape
    assert w_row0 % tn == 0 and n_total % tn == 0
    j0 = w_row0 // tn

    def body(a_ref, w_ref, *rest):
        o_ref = rest[-1]
        o_ref[...] = _dot_nt(a_ref[...].astype(BF16), w_ref[...])
        if rope is not None:
            @pl.when(pl.program_id(0) == 0)
            def _():
                c = rest[0][...]
                sg = rest[1][...]
                first = (lax.broadcasted_iota(jnp.int32, (tm, 128), 1) % HEAD_DIM) < HEAD_DIM // 2
                for col in range(0, rope[2], 128):
                    v = o_ref[:, col:col + 128]
                    o_ref[:, col:col + 128] = v * c + _swap_halves(v, first) * sg

    in_specs = [BS((tm, k), lambda j, i: (i, 0)), BS((tn, k), lambda j, i: (j0 + j, 0))]
    args = [a, w]
    if rope is not None:
        assert rope[2] <= tn
        in_specs += [BS((tm, 128), lambda j, i: (i % (S // tm), 0))] * 2
        args += [rope[0], rope[1]]
    return pl.pallas_call(
        body, name=f"mm_nt_{tag}", grid=(n_total // tn, t // tm), in_specs=in_specs,
        out_specs=BS((tm, tn), lambda j, i: (i, j)), out_shape=SDS((t, n_total), F32),
        compiler_params=_cp("parallel", "parallel"),
    )(*args)


def _mm_tn(a, b, tag, scale=1.0, tmm=256, tk=None):
    c_n, t, m = a.shape
    n = b.shape[1]
    tk = t if tk is None else tk
    nk = t // tk

    def body_one(a_ref, b_ref, o_ref):
        o_ref[...] = (_dot_tn(a_ref[...].astype(BF16), b_ref[...].astype(BF16)) * scale).astype(BF16)

    def body_acc(a_ref, b_ref, o_ref, acc_ref):
        kk = pl.program_id(2)

        @pl.when(kk == 0)
        def _():
            acc_ref[...] = jnp.zeros_like(acc_ref)

        acc_ref[...] += _dot_tn(a_ref[...].astype(BF16), b_ref[...].astype(BF16))

        @pl.when(kk == nk - 1)
        def _():
            o_ref[...] = (acc_ref[...] * scale).astype(BF16)

    return pl.pallas_call(
        body_one if nk == 1 else body_acc, name=f"mm_tn_{tag}", grid=(c_n, m // tmm, nk),
        in_specs=[BS((None, tk, tmm), lambda c, mi, kk: (c, kk, mi)), BS((tk, n), lambda c, mi, kk: (kk, 0))],
        out_specs=BS((None, tmm, n), lambda c, mi, kk: (c, mi, 0)),
        out_shape=SDS((c_n, m, n), BF16), scratch_shapes=[] if nk == 1 else [pltpu.VMEM((tmm, n), F32)],
        compiler_params=_cp("parallel", "parallel", "arbitrary"),
    )(a, b)


def _ffn_up(hn, wut, tag):
    t = hn.shape[0]
    tm, tn = 512, 1408

    def body(h_ref, w_ref, gu_ref, act_ref):
        h = h_ref[...]
        g = _dot_nt(h, w_ref[0])
        u = _dot_nt(h, w_ref[1])
        sg = jax.nn.sigmoid(g)
        silu = g * sg
        gu_ref[0] = (u * (sg + silu * (1.0 - sg))).astype(BF16)
        gu_ref[1] = silu.astype(BF16)
        act_ref[...] = (silu * u).astype(BF16)

    return pl.pallas_call(
        body, name=f"ffn_up_{tag}", grid=(F // tn, t // tm),
        in_specs=[BS((tm, D), lambda j, i: (i, 0)), BS((2, tn, D), lambda j, i: (0, j, 0))],
        out_specs=[BS((2, tm, tn), lambda j, i: (0, i, j)), BS((tm, tn), lambda j, i: (i, j))],
        out_shape=[SDS((2, t, F), BF16), SDS((t, F), BF16)],
        compiler_params=_cp("parallel", "parallel"),
    )(hn, wut)


def _ffn_dact(dxo, wd, gu, tie, tag):
    t = dxo.shape[0]
    tm, tn = 512, 1408

    def body(d_ref, w_ref, gu_ref, tie_ref, o_ref):
        dact = _dot_nt(d_ref[...] * 0.5, w_ref[...])
        o_ref[0] = (dact * gu_ref[0].astype(F32)).astype(BF16)
        o_ref[1] = (dact * gu_ref[1].astype(F32)).astype(BF16)

    return pl.pallas_call(
        body, name=f"ffn_dact_{tag}", grid=(F // tn, t // tm),
        in_specs=[BS((tm, D), lambda j, i: (i, 0)), BS((tn, D), lambda j, i: (j, 0)),
                  BS((2, tm, tn), lambda j, i: (0, i, j)), BS((8, 128), lambda j, i: (0, 0))],
        out_specs=BS((2, tm, tn), lambda j, i: (0, i, j)),
        out_shape=SDS((2, t, F), BF16), compiler_params=_cp("parallel", "parallel"),
    )(dxo, wd, gu, tie)


def _rope_tables():
    half = HEAD_DIM // 2
    inv_freq = ROPE_THETA ** (-jnp.arange(half, dtype=F32) / half)
    ang = jnp.arange(S).astype(F32)[:, None] * inv_freq[None, :]
    cos, sin = jnp.cos(ang), jnp.sin(ang)
    return jnp.concatenate([cos, cos, cos, cos], axis=1), jnp.concatenate([-sin, sin, -sin, sin], axis=1)


def _swap_halves(t, first_half):
    return jnp.where(first_half, pltpu.roll(t, 96, 1), pltpu.roll(t, 32, 1))


def _rope_bwd(dqs, dks, dvs, cos_t, sin_t):
    t = dqs[0].shape[0]
    tm = 512

    def body(*refs):
        c = refs[9][...]
        sg = refs[10][...]
        o_ref = refs[11]
        first = (lax.broadcasted_iota(jnp.int32, (tm, 128), 1) % HEAD_DIM) < HEAD_DIM // 2
        for a in range(6):
            for hp in range(2):
                v = refs[a][:, 128 * hp:128 * (hp + 1)]
                col = 128 * (2 * a + hp)
                o_ref[:, col:col + 128] = (v * c + _swap_halves(v * sg, first)).astype(BF16)
        for a in range(6, 9):
            o_ref[:, 256 * a:256 * (a + 1)] = refs[a][...].astype(BF16)

    blk = BS((tm, 256), lambda i: (i, 0))
    tab = BS((tm, 128), lambda i: (i % (S // tm), 0))
    return pl.pallas_call(
        body, name="rope_bwd", grid=(t // tm,), in_specs=[blk] * 9 + [tab, tab],
        out_specs=BS((None, tm, QKV_A), lambda i: (0, i, 0)), out_shape=SDS((1, t, QKV_A), BF16),
        compiler_params=_cp("parallel"),
    )(*dqs, *dks, *dvs, cos_t, sin_t)


def _head_masks():
    lane = lax.broadcasted_iota(jnp.int32, (1, 128), 1)
    m0 = (lane < HEAD_DIM).astype(F32)
    return m0, 1.0 - m0


def _dil_geometry(d):
    sub = S // d
    q_rows = 128
    k_rows = min(256, sub)
    return sub, q_rows, sub // q_rows, k_rows


def _dil_tile(idx, d, keys_on_rows=False):
    sub, q_rows, nb, k_rows = _dil_geometry(d)
    r = idx // nb
    n = idx % nb
    k_sub = jnp.clip(q_rows * n - HALF, 0, sub - k_rows)
    if d == 1:
        q_start = pl.multiple_of(q_rows * n, q_rows)
        k_start = pl.multiple_of(k_sub, HALF)
    else:
        q_start = q_rows * n * d + r
        k_start = k_sub * d + r
    if keys_on_rows:
        ii = lax.broadcasted_iota(jnp.int32, (k_rows, 2 * q_rows), 1) % q_rows
        jj = lax.broadcasted_iota(jnp.int32, (k_rows, 2 * q_rows), 0)
    else:
        ii = lax.broadcasted_iota(jnp.int32, (q_rows, k_rows), 0)
        jj = lax.broadcasted_iota(jnp.int32, (q_rows, k_rows), 1)
    valid = jnp.abs(jj - ii + (k_sub - q_rows * n)) <= HALF
    return q_start, k_start, valid


def _dil_specs(grp):
    qs = BS((S, 128), lambda b, hp: (b, 2 * grp + hp))
    ks = BS((S, 128), lambda b, hp: (b, 6 + 2 * grp + hp))
    vs = BS((S, 128), lambda b, hp: (b, 12 + 2 * grp + hp))
    own = BS((S, 128), lambda b, hp: (b, hp))
    return qs, ks, vs, own


def _dil_fwd(qkr, proj, grp):
    t = qkr.shape[0]
    d = DILATIONS[grp]
    _, q_rows, nb, k_rows = _dil_geometry(d)

    def body(q_ref, k_ref, v_ref, o_ref, l_ref):
        masks = _head_masks()

        def step(i0, carry):
            geo = [_dil_tile(i0 * DIL_FWD_TILES + j, d) for j in range(DIL_FWD_TILES)]
            tiles = [(j, h) for j in range(DIL_FWD_TILES) for h in range(2)]
            qs = [q_ref[_ds(g[0], q_rows, d), :] for g in geo]
            kbs = [k_ref[_ds(g[1], k_rows, d), :].astype(BF16) for g in geo]
            ss = [jnp.where(geo[j][2], _dot_nt((qs[j] * masks[h]).astype(BF16), kbs[j]) * SCALE, NEG) for j, h in tiles]
            mxs = [jnp.max(s, axis=1, keepdims=True) for s in ss]
            ps = [jnp.exp(s - mx) for s, mx in zip(ss, mxs)]
            dens = [jnp.sum(p, axis=1, keepdims=True) for p in ps]
            vs = [v_ref[_ds(g[1], k_rows, d), :] for g in geo]
            outs = [_dot_nn(p.astype(BF16), (vs[j] * masks[h]).astype(BF16)) / den
                    for p, den, (j, h) in zip(ps, dens, tiles)]
            for j, g in enumerate(geo):
                o_ref[_ds(g[0], q_rows, d), :] = outs[2 * j] + outs[2 * j + 1]
                l_ref[_ds(g[0], q_rows, d), :] = (
                    (mxs[2 * j] + jnp.log(dens[2 * j])) * masks[0] + (mxs[2 * j + 1] + jnp.log(dens[2 * j + 1])) * masks[1])
            return carry

        lax.fori_loop(0, d * nb // DIL_FWD_TILES, step, 0)

    qs, ks, vs, own = _dil_specs(grp)
    return pl.pallas_call(
        body, name=f"dil_fwd_{grp}", grid=(t // S, 2), in_specs=[qs, ks, vs], out_specs=[own, own],
        out_shape=[SDS((t, 256), F32), SDS((t, 256), F32)], compiler_params=_cp("parallel", "parallel"),
    )(qkr, qkr, proj)


def _dil_bwd(qkr, proj, do, dlp, lse, grp):
    t = qkr.shape[0]
    d = DILATIONS[grp]
    _, q_rows, nb, k_rows = _dil_geometry(d)

    def body(q_ref, k_ref, v_ref, do_ref, dl_ref, l_ref, dq_ref, dk_ref, dv_ref):
        masks = _head_masks()
        dk_ref[...] = jnp.zeros_like(dk_ref)
        dv_ref[...] = jnp.zeros_like(dv_ref)

        def as_row(x2):
            xt = x2.T
            return jnp.concatenate([xt[0:1], xt[HEAD_DIM:HEAD_DIM + 1]], axis=1)

        def step(i0, carry):
            geo = [_dil_tile(i0 * DIL_BWD_TILES + j, d, keys_on_rows=True) for j in range(DIL_BWD_TILES)]
            q_ds = [_ds(g[0], q_rows, d) for g in geo]
            k_ds = [_ds(g[1], k_rows, d) for g in geo]
            qbs = [_both_heads(q_ref[r, :], masks).astype(BF16) for r in q_ds]
            kbs = [k_ref[r, :].astype(BF16) for r in k_ds]
            vbs = [v_ref[r, :].astype(BF16) for r in k_ds]
            dobs = [_both_heads(do_ref[r, :], masks).astype(BF16) for r in q_ds]
            l_rows = [as_row(l_ref[r, :]) for r in q_ds]
            dl_rows = [as_row(dl_ref[r, :]) for r in q_ds]
            ss = [jnp.where(g[2], _dot_nt(kb, qb) * SCALE, NEG) for g, kb, qb in zip(geo, kbs, qbs)]
            ps = [jnp.exp(s - lr) for s, lr in zip(ss, l_rows)]
            dps = [_dot_nt(vb, dob) for vb, dob in zip(vbs, dobs)]
            dss = [(p * (dp - dr)).astype(BF16) for p, dp, dr in zip(ps, dps, dl_rows)]
            dks = [_dot_nn(ds, qb) for ds, qb in zip(dss, qbs)]
            dvs = [_dot_nn(p.astype(BF16), dob) for p, dob in zip(ps, dobs)]
            dqs = [_own_heads(_dot_tn(ds, kb), masks) for ds, kb in zip(dss, kbs)]
            for j in range(DIL_BWD_TILES):
                dq_ref[q_ds[j], :] = dqs[j] * SCALE
                dk_ref[k_ds[j], :] += dks[j] * SCALE
                dv_ref[k_ds[j], :] += dvs[j]
            return carry

        lax.fori_loop(0, d * nb // DIL_BWD_TILES, step, 0)

    qs, ks, vs, own = _dil_specs(grp)
    return pl.pallas_call(
        body, name=f"dil_bwd_{grp}", grid=(t // S, 2), in_specs=[qs, ks, vs, own, own, own],
        out_specs=[own, own, own], out_shape=[SDS((t, 256), F32)] * 3,
        compiler_params=_cp("parallel", "parallel"),
    )(qkr, qkr, proj, do, dlp, lse)


def _mix_weights(l0, l1, l2):
    mx = jnp.maximum(jnp.maximum(l0, l1), l2)
    e0, e1, e2 = jnp.exp(l0 - mx), jnp.exp(l1 - mx), jnp.exp(l2 - mx)
    den = e0 + e1 + e2
    return e0 / den, e1 / den, e2 / den


def _combine_fwd(outs, lses):
    t = outs[0].shape[0]
    tm = 512

    def body(o0, o1, o2, l0, l1, l2, y_ref):
        w0, w1, w2 = _mix_weights(l0[...], l1[...], l2[...])
        y_ref[...] = w0 * o0[...] + w1 * o1[...] + w2 * o2[...]

    blk = BS((tm, 256), lambda i: (i, 0))
    return pl.pallas_call(
        body, name="combine_fwd", grid=(t // tm,), in_specs=[blk] * 6, out_specs=blk,
        out_shape=SDS((t, 256), F32), compiler_params=_cp("parallel"),
    )(*outs, *lses)


def _head_sum(x):
    a = lax.broadcasted_iota(jnp.int32, (256, 256), 0) // HEAD_DIM
    b = lax.broadcasted_iota(jnp.int32, (256, 256), 1) // HEAD_DIM
    ones = (a == b).astype(BF16)
    hi = x.astype(BF16)
    lo = (x - hi.astype(F32)).astype(BF16)
    return _dot_nn(hi, ones) + _dot_nn(lo, ones)


def _combine_bwd(dya, outs, lses):
    t = dya.shape[0]
    tm = 512

    def body(dy_ref, o0, o1, o2, l0, l1, l2, d0, d1, d2, e0, e1, e2):
        ws = _mix_weights(l0[...], l1[...], l2[...])
        dy = dy_ref[...]
        ya = ws[0] * o0[...] + ws[1] * o1[...] + ws[2] * o2[...]
        hs = _head_sum(dy * ya)
        for w, d_ref, e_ref in zip(ws, (d0, d1, d2), (e0, e1, e2)):
            d_ref[...] = w * dy
            e_ref[...] = w * hs

    blk = BS((tm, 256), lambda i: (i, 0))
    return pl.pallas_call(
        body, name="combine_bwd", grid=(t // tm,), in_specs=[blk] * 7, out_specs=[blk] * 6,
        out_shape=[SDS((t, 256), F32)] * 6, compiler_params=_cp("parallel"),
    )(dya, *outs, *lses)


def _na_bias_table(rel_bias):
    kw = NA_KR * GRID_W
    rev = jnp.pad(rel_bias.astype(F32)[:, :, ::-1], ((0, 0), (0, 0), (0, 128 - 31)))

    def body(r_ref, o_ref):
        lane = lax.broadcasted_iota(jnp.int32, (GRID_W, 128), 1)
        j = lax.broadcasted_iota(jnp.int32, (GRID_W, 128), 0)
        q = lane % GRID_W
        win_lo = jnp.clip(q - 8, 0, GRID_W - 16)
        valid = (j >= win_lo) & (j < win_lo + 16)
        for cls in range(NA_KR):
            for k in range(NA_KR):
                tiles = []
                for h in range(2):
                    row = jnp.broadcast_to(r_ref[h, cls + k:cls + k + 1, :], (GRID_W, 128))
                    tiles.append(pltpu.roll(row, (128 - 15 + GRID_W * h) % 128, 1, stride=1, stride_axis=0))
                o_ref[cls, GRID_W * k:GRID_W * (k + 1), :] = jnp.where(
                    valid, jnp.where(lane < GRID_W, tiles[0], tiles[1]), NEG)

    return pl.pallas_call(
        body, name="na_bias_table", grid=(4,),
        in_specs=[BS((2, 2 * NA_KR - 1, 128), lambda hp: (hp, 0, 0))],
        out_specs=BS((None, NA_KR, kw, 128), lambda hp: (hp, 0, 0, 0)),
        out_shape=SDS((4, NA_KR, kw, 128), F32), compiler_params=_cp("parallel"),
    )(rev)


def _na_row(i):
    lo = jnp.clip(i - NA_KR // 2, 0, NA_ROWS - NA_KR)
    return pl.multiple_of(GRID_W * i, GRID_W), pl.multiple_of(GRID_W * lo, GRID_W), lo - i + NA_KR - 1


def _both_heads(x, masks):
    return jnp.concatenate([x * masks[0], x * masks[1]], axis=0)


def _own_heads(r, masks):
    half = r.shape[0] // 2
    return r[:half] * masks[0] + r[half:] * masks[1]


def _na_fwd(proj, bias):
    t = proj.shape[0]
    kw = NA_KR * GRID_W

    def body(q_ref, k_ref, v_ref, b_ref, o_ref, l_ref):
        masks = _head_masks()

        def step(i0, carry):
            idx = [i0 * NA_FWD_ROWS + j for j in range(NA_FWD_ROWS)]
            rows = [_na_row(i) for i in idx]
            qbs = [_both_heads(q_ref[pl.ds(r[0], GRID_W), :], masks).astype(BF16) for r in rows]
            kbs = [k_ref[pl.ds(r[1], kw), :].astype(BF16) for r in rows]
            ss = [_dot_nt(kb, qb) * SCALE + b_ref[r[2]] for kb, qb, r in zip(kbs, qbs, rows)]
            mxs = [jnp.max(s, axis=0, keepdims=True) for s in ss]
            ps = [jnp.exp(s - mx) for s, mx in zip(ss, mxs)]
            dens = [jnp.sum(p, axis=0, keepdims=True) for p in ps]
            pbs = [(p / den).astype(BF16) for p, den in zip(ps, dens)]
            vbs = [v_ref[pl.ds(r[1], kw), :].astype(BF16) for r in rows]
            outs = [_own_heads(_dot_tn(pb, vb), masks) for pb, vb in zip(pbs, vbs)]
            for j, r in enumerate(rows):
                o_ref[pl.ds(r[0], GRID_W), :] = outs[j]
                l_ref[pl.ds(idx[j], 1), :] = mxs[j] + jnp.log(dens[j])
            return carry

        lax.fori_loop(0, NA_ROWS // NA_FWD_ROWS, step, 0)

    c0 = QKV_A // 128
    return pl.pallas_call(
        body, name="na_fwd", grid=(t // S, 4),
        in_specs=[BS((S, 128), lambda b, hp: (b, c0 + hp)), BS((S, 128), lambda b, hp: (b, c0 + 4 + hp)),
                  BS((S, 128), lambda b, hp: (b, c0 + 8 + hp)),
                  BS((None, NA_KR, kw, 128), lambda b, hp: (hp, 0, 0, 0))],
        out_specs=[BS((S, 128), lambda b, hp: (b, hp)), BS((None, None, NA_ROWS, 128), lambda b, hp: (b, hp, 0, 0))],
        out_shape=[SDS((t, 512), F32), SDS((t // S, 4, NA_ROWS, 128), F32)],
        compiler_params=_cp("parallel", "parallel"),
    )(proj, proj, proj, bias)


def _na_bwd(proj, bias, dyb, yb, lse):
    t = proj.shape[0]
    kw = NA_KR * GRID_W

    def body(q_ref, k_ref, v_ref, b_ref, do_ref, o_ref, l_ref, d_ref, db_ref):
        masks = _head_masks()
        ones = jnp.ones((8, 128), BF16)

        @pl.when(pl.program_id(1) == 0)
        def _():
            db_ref[...] = jnp.zeros_like(db_ref)

        d_ref[1:3] = jnp.zeros((2, S, 128), F32)

        def row_sums(x):
            hi = x.astype(BF16)
            lo = (x - hi.astype(F32)).astype(BF16)
            return (_dot_nt(ones, hi) + _dot_nt(ones, lo))[0:1]

        def step(i0, carry):
            idx = [i0 * NA_BWD_ROWS + j for j in range(NA_BWD_ROWS)]
            rows = [_na_row(i) for i in idx]
            q_ds = [pl.ds(r[0], GRID_W) for r in rows]
            k_ds = [pl.ds(r[1], kw) for r in rows]
            qbs = [_both_heads(q_ref[r, :], masks).astype(BF16) for r in q_ds]
            kbs = [k_ref[r, :].astype(BF16) for r in k_ds]
            vbs = [v_ref[r, :].astype(BF16) for r in k_ds]
            dos = [do_ref[r, :] for r in q_ds]
            dobs = [_both_heads(do, masks).astype(BF16) for do in dos]
            deltas = [row_sums(_both_heads(do * o_ref[r, :], masks)) for do, r in zip(dos, q_ds)]
            ss = [_dot_nt(kb, qb) * SCALE + b_ref[r[2]] for kb, qb, r in zip(kbs, qbs, rows)]
            ps = [jnp.exp(s - l_ref[pl.ds(i, 1), :]) for s, i in zip(ss, idx)]
            dps = [_dot_nt(vb, dob) for vb, dob in zip(vbs, dobs)]
            dss = [p * (dp - delta) for p, dp, delta in zip(ps, dps, deltas)]
            for ds, r in zip(dss, rows):
                db_ref[r[2]] += ds
            dsbs = [ds.astype(BF16) for ds in dss]
            dks = [_dot_nn(dsb, qb) for dsb, qb in zip(dsbs, qbs)]
            dvs = [_dot_nn(p.astype(BF16), dob) for p, dob in zip(ps, dobs)]
            dqs = [_own_heads(_dot_tn(dsb, kb), masks) for dsb, kb in zip(dsbs, kbs)]
            for j in range(NA_BWD_ROWS):
                d_ref[0, q_ds[j], :] = dqs[j] * SCALE
                d_ref[1, k_ds[j], :] += dks[j] * SCALE
                d_ref[2, k_ds[j], :] += dvs[j]
            return carry

        lax.fori_loop(0, NA_ROWS // NA_BWD_ROWS, step, 0)

    c0 = QKV_A // 128
    own = BS((S, 128), lambda hp, b: (b, hp))
    tab = BS((None, NA_KR, kw, 128), lambda hp, b: (hp, 0, 0, 0))
    return pl.pallas_call(
        body, name="na_bwd", grid=(4, t // S),
        in_specs=[BS((S, 128), lambda hp, b: (b, c0 + hp)), BS((S, 128), lambda hp, b: (b, c0 + 4 + hp)),
                  BS((S, 128), lambda hp, b: (b, c0 + 8 + hp)), tab, own, own,
                  BS((None, None, NA_ROWS, 128), lambda hp, b: (b, hp, 0, 0))],
        out_specs=[BS((3, S, 128), lambda hp, b: (0, b, hp)), tab],
        out_shape=[SDS((3, t, 512), F32), SDS((4, NA_KR, kw, 128), F32)],
        compiler_params=_cp("parallel", "arbitrary"),
    )(proj, proj, proj, bias, dyb, yb, lse)


def _na_dbias_lane_map():
    kw = NA_KR * GRID_W
    lane = np.arange(kw)
    blk, m = lane // GRID_W, lane % GRID_W
    target = np.full(kw, -1)
    target[m < 16] = (blk * 32 + 15 + m)[m < 16]
    target[m >= 49] = (((blk + 1) % NA_KR) * 32 + m - 49)[m >= 49]
    return jnp.asarray(target[:, None] == np.arange(kw)[None, :], BF16)


def _na_dbias(db):
    kw = NA_KR * GRID_W

    def body(x_ref, map_ref, o_ref, z_ref):
        for cls in range(NA_KR):
            xt = x_ref[cls].T
            for h in range(2):
                xv = xt[GRID_W * h:GRID_W * (h + 1)]
                y = xv[0:8]
                for g in range(1, GRID_W // 8):
                    y = y + pltpu.roll(xv[8 * g:8 * g + 8], kw - 8 * g, 1)
                d = y[0:1]
                for s in range(1, 8):
                    d = d + pltpu.roll(y[s:s + 1], kw - s, 1)
                z_ref[h, cls:cls + 1, :] = d
        for h in range(2):
            z = z_ref[h]
            hi = z.astype(BF16)
            lo = (z - hi.astype(F32)).astype(BF16)
            e = _dot_nn(hi, map_ref[...]) + _dot_nn(lo, map_ref[...])
            out = e[0:1]
            for cls in range(1, NA_KR):
                out = out + pltpu.roll(e[cls:cls + 1], 32 * cls, 1)
            o_ref[h] = jnp.broadcast_to(out, (8, kw))

    return pl.pallas_call(
        body, name="na_dbias", grid=(4,),
        in_specs=[BS((None, NA_KR, kw, 128), lambda hp: (hp, 0, 0, 0)), BS((kw, kw), lambda hp: (0, 0))],
        out_specs=BS((2, 8, kw), lambda hp: (hp, 0, 0)), out_shape=SDS((8, 8, kw), F32),
        scratch_shapes=[pltpu.VMEM((2, 8, kw), F32)], compiler_params=_cp("parallel"),
    )(db, _na_dbias_lane_map())


def _merge_fwd(ya, yb, proj, wat, wbt):
    t = ya.shape[0]
    tm, tn = 512, 256
    ca = (QKV_A + QKV_B) // tn
    cb = ca + D // tn

    def body(ya_ref, yb_ref, la_ref, lb_ref, wa_ref, wb_ref, m_ref, za_ref, zb_ref):
        za = _dot_nt(ya_ref[...].astype(BF16), wa_ref[...])
        zb = _dot_nt(yb_ref[...].astype(BF16), wb_ref[...])
        m_ref[...] = (jax.nn.sigmoid(la_ref[...]) * za + jax.nn.sigmoid(lb_ref[...]) * zb).astype(BF16)
        za_ref[...] = za.astype(BF16)
        zb_ref[...] = zb.astype(BF16)

    out = BS((tm, tn), lambda i, j: (i, j))
    return pl.pallas_call(
        body, name="merge_fwd", grid=(t // tm, D // tn),
        in_specs=[BS((tm, 256), lambda i, j: (i, 0)), BS((tm, 512), lambda i, j: (i, 0)),
                  BS((tm, tn), lambda i, j: (i, ca + j)), BS((tm, tn), lambda i, j: (i, cb + j)),
                  BS((tn, 256), lambda i, j: (j, 0)), BS((tn, 512), lambda i, j: (j, 0))],
        out_specs=[out, out, out], out_shape=[SDS((t, D), BF16)] * 3,
        compiler_params=_cp("parallel", "parallel"),
    )(ya, yb, proj, proj, wat, wbt)


def _merge_bwd(dxo, wo, za, zb, proj):
    t = dxo.shape[0]
    tm, tn = 512, 256
    ca = (QKV_A + QKV_B) // tn
    cb = ca + D // tn

    def body(d_ref, w_ref, za_ref, zb_ref, la_ref, lb_ref, dza_ref, dzb_ref, dl_ref):
        dmv = _dot_nt(d_ref[...], w_ref[...])
        ga = jax.nn.sigmoid(la_ref[...])
        gb = jax.nn.sigmoid(lb_ref[...])
        dza_ref[...] = (dmv * ga).astype(BF16)
        dzb_ref[...] = (dmv * gb).astype(BF16)
        dl_ref[0] = (dmv * za_ref[...].astype(F32) * ga * (1.0 - ga)).astype(BF16)
        dl_ref[1] = (dmv * zb_ref[...].astype(F32) * gb * (1.0 - gb)).astype(BF16)

    blk = BS((tm, tn), lambda i, j: (i, j))
    return pl.pallas_call(
        body, name="merge_bwd", grid=(t // tm, D // tn),
        in_specs=[BS((tm, D), lambda i, j: (i, 0)), BS((tn, D), lambda i, j: (j, 0)), blk, blk,
                  BS((tm, tn), lambda i, j: (i, ca + j)), BS((tm, tn), lambda i, j: (i, cb + j))],
        out_specs=[blk, blk, BS((2, tm, tn), lambda i, j: (0, i, j))],
        out_shape=[SDS((t, D), BF16), SDS((t, D), BF16), SDS((2, t, D), BF16)],
        compiler_params=_cp("parallel", "parallel"),
    )(dxo, wo, za, zb, proj, proj)


def _adamw_update(w, g, m, v):
    mn = ADAM_B1 * m + (1.0 - ADAM_B1) * g
    vn = ADAM_B2 * v + (1.0 - ADAM_B2) * (g * g)
    m_hat = mn / (1.0 - ADAM_B1 ** ADAM_STEP)
    v_hat = vn / (1.0 - ADAM_B2 ** ADAM_STEP)
    return -ADAM_LR * (m_hat / (jnp.sqrt(v_hat) + ADAM_EPS) + ADAM_WD * w), mn, vn


def _sum_adamw(recv0, recv1, w, m, v, tag):
    _, r, c = recv0.shape
    tr = max(rows for rows in range(16, r + 1, 16) if r % rows == 0 and rows * c <= 192 * 1024)

    def body(a_ref, b_ref, w_ref, m_ref, v_ref, g_ref, d_ref, mo_ref, vo_ref):
        for layer, ref in enumerate((a_ref, b_ref)):
            g = ref[0].astype(F32)
            for s in range(1, N_DEV):
                g = g + ref[s].astype(F32)
            g_ref[layer] = g
            d_ref[layer], mo_ref[layer], vo_ref[layer] = _adamw_update(w_ref[layer], g, m_ref[layer], v_ref[layer])

    slots = BS((N_DEV, tr, c), lambda i: (0, i, 0))
    blk = BS((2, tr, c), lambda i: (0, i, 0))
    return pl.pallas_call(
        body, name=f"sum_adamw_{tag}", grid=(r // tr,), in_specs=[slots, slots, blk, blk, blk],
        out_specs=[blk] * 4, out_shape=[SDS((2, r, c), F32)] * 4, compiler_params=_cp("parallel"),
    )(recv0, recv1, w, m, v)


def _adamw(w, g, m, v, tag):
    layers, r, c = w.shape
    tr = next(r // k for k in (1, 2, 4, 8) if r // k <= 384 and r % (8 * k) == 0)

    def body(w_ref, g_ref, m_ref, v_ref, d_ref, mo_ref, vo_ref):
        d_ref[...], mo_ref[...], vo_ref[...] = _adamw_update(w_ref[...], g_ref[...], m_ref[...], v_ref[...])

    blk = BS((None, tr, c), lambda l, i: (l, i, 0))
    return pl.pallas_call(
        body, name=f"adamw_{tag}", grid=(layers, r // tr), in_specs=[blk] * 4, out_specs=[blk] * 3,
        out_shape=[SDS((layers, r, c), F32)] * 3, compiler_params=_cp("parallel", "parallel"),
    )(w, g, m, v)


def _place():
    return lax.axis_index("x"), lax.axis_index("y"), lax.axis_index("c")


def _flip(coord, bit):
    return 1 - coord if bit else coord


def _peers(x, y, c):
    peers = []
    for mask in range(1, N_DEV):
        p = (_flip(x, mask & 4), _flip(y, mask & 2), _flip(c, mask & 1))
        peers.append((p, 4 * p[0] + 2 * p[1] + p[2]))
    return peers


def _copy_plan(mode, src, land, x, y, c):
    me = 4 * x + 2 * y + c

    def device(mask):
        p = (_flip(x, mask & 4), _flip(y, mask & 2), _flip(c, mask & 1))
        return p, 4 * p[0] + 2 * p[1] + p[2]

    if mode == "scatter":
        r = land.shape[1]
        return [(p, src.at[pl.ds(i * r, r), :], land.at[me], land.at[i])
                for p, i in map(device, (1, 2, 3, 4, 5, 6, 7, 0))]
    r = land.shape[0] // N_DEV

    def rows(i):
        return land.at[pl.ds(i * r, r), :]

    if mode == "gather":
        return [(p, src, rows(me), rows(i)) for p, i in map(device, (1, 4, 2, 6, 0))]
    sibling = device(1)[0]
    return [(sibling, rows(device(m)[1]), rows(device(m)[1]), rows(device(m | 1)[1])) for m in (4, 2, 6)]


COPIES = dict(scatter=8, gather=5, forward=3)
HBM_SPEC = BS(memory_space=pltpu.HBM)
SEM_SPEC = BS(memory_space=pltpu.SEMAPHORE)
DATAFLOW = pltpu.SideEffectType.DATAFLOW_SIDE_EFFECTING


def _fresh(shape, dtype, tag):
    def body(o_ref):
        del o_ref

    return pl.pallas_call(body, name=f"fresh_{tag}", out_specs=BS(memory_space=pl.ANY), out_shape=SDS(shape, dtype))()


def _exchange_start(mode, srcs, lands, after, tag):
    if lands is None and mode == "gather":
        lands = [_fresh((N_DEV * s.shape[0], s.shape[1]), s.dtype, f"{tag}_{a}") for a, s in enumerate(srcs)]
    elif lands is None:
        lands = [_fresh((N_DEV, s.shape[0] // N_DEV, s.shape[1]), s.dtype, f"{tag}_{a}") for a, s in enumerate(srcs)]
    n, n_src, n_cp = len(lands), len(srcs), COPIES[mode]
    behind = [] if after is None else [after]

    def body(*refs):
        src_refs, land_refs = refs[:n_src], refs[n_src:n_src + n]
        send_sems, recv_sems = refs[n_src + n + len(behind)], refs[n_src + n + len(behind) + 1]
        token = refs[-1]
        for a in range(n):
            plan = _copy_plan(mode, src_refs[a] if n_src else None, land_refs[a], *_place())
            for k, (p, out, there, _) in enumerate(plan):
                pltpu.make_async_remote_copy(
                    src_ref=out, dst_ref=there, send_sem=send_sems.at[n_cp * a + k],
                    recv_sem=recv_sems.at[n_cp * a + k], device_id=p, device_id_type=MESH).start()
        token[...] = jnp.zeros_like(token)

    both = [*srcs, *lands]
    res = pl.pallas_call(
        body, name=f"{mode}_start_{tag}",
        out_shape=(pltpu.SemaphoreType.DMA((n_cp * n,)), pltpu.SemaphoreType.DMA((n_cp * n,)),
                   *[pltpu.HBM(v.shape, v.dtype) for v in both], SDS((8, 128), F32)),
        in_specs=[HBM_SPEC] * len(both) + [BS(memory_space=pl.ANY)] * len(behind),
        out_specs=(SEM_SPEC, SEM_SPEC, *[HBM_SPEC] * len(both), BS(memory_space=pltpu.VMEM)),
        input_output_aliases={i: 2 + i for i in range(len(both))},
        compiler_params=pltpu.CompilerParams(has_side_effects=DATAFLOW),
    )(*[pltpu.with_memory_space_constraint(v, pltpu.HBM) for v in both], *behind)
    return (mode, res[0], res[1], res[2:2 + n_src], res[2 + n_src:2 + n_src + n]), res[-1]


def _exchange_wait(handle, after, tag, which=None):
    mode, send_sems, recv_sems, srcs, lands = handle
    which = list(range(len(lands))) if which is None else list(which)
    n_cp = COPIES[mode]
    lands = [lands[a] for a in which]
    srcs = [srcs[a] for a in which] if srcs else []
    n, n_src = len(lands), len(srcs)
    afters = list(after) if isinstance(after, (tuple, list)) else [after]

    def body(*refs):
        src_refs, land_refs = refs[:n_src], refs[n_src:n_src + n]
        send_ref, recv_ref = refs[n_src + n], refs[n_src + n + 1]
        for i, a in enumerate(which):
            plan = _copy_plan(mode, src_refs[i] if n_src else None, land_refs[i], *_place())
            for k, (p, out, _, here) in enumerate(plan):
                cp = pltpu.make_async_remote_copy(
                    src_ref=out, dst_ref=here, send_sem=send_ref.at[n_cp * a + k], recv_sem=recv_ref.at[n_cp * a + k],
                    device_id=p, device_id_type=MESH)
                cp.wait_send()
                cp.wait_recv()

    both = [*srcs, *lands]
    res = pl.pallas_call(
        body, name=f"{mode}_wait_{tag}", out_shape=tuple(pltpu.HBM(v.shape, v.dtype) for v in both),
        in_specs=[HBM_SPEC] * len(both) + [SEM_SPEC, SEM_SPEC] + [BS(memory_space=pl.ANY)] * len(afters),
        out_specs=tuple([HBM_SPEC] * len(both)),
        input_output_aliases={i: i for i in range(len(both))},
        compiler_params=pltpu.CompilerParams(has_side_effects=DATAFLOW),
    )(*both, send_sems, recv_sems, *afters)
    return list(res[n_src:])


def _allreduce_small(vec, behind):
    rows = vec.shape[0]

    def body(x_ref, behind_ref, o_ref, buf_ref, send_sems, recv_sems):
        x, y, c = _place()
        me = 4 * x + 2 * y + c
        buf_ref[me] = x_ref[...]
        peers = _peers(x, y, c)

        def copy(k, slot):
            return pltpu.make_async_remote_copy(
                src_ref=x_ref, dst_ref=buf_ref.at[slot], send_sem=send_sems.at[k], recv_sem=recv_sems.at[k],
                device_id=peers[k][0], device_id_type=MESH)

        sends = [copy(k, me) for k in range(N_DEV - 1)]
        for cp in sends:
            cp.start()
        for k in range(N_DEV - 1):
            copy(k, peers[k][1]).wait_recv()
        for cp in sends:
            cp.wait_send()
        acc = buf_ref[0]
        for s in range(1, N_DEV):
            acc = acc + buf_ref[s]
        o_ref[...] = acc

    vmem = BS(memory_space=pltpu.VMEM)
    return pl.pallas_call(
        body, name="allreduce_small", in_specs=[vmem, BS(memory_space=pl.ANY)], out_specs=vmem,
        out_shape=SDS((rows, 128), F32),
        scratch_shapes=[pltpu.VMEM((N_DEV, rows, 128), F32), pltpu.SemaphoreType.DMA((7,)),
                        pltpu.SemaphoreType.DMA((7,))],
        compiler_params=pltpu.CompilerParams(has_side_effects=True),
    )(vec, behind)


def _ffn_forward(x, hn, fetch, names, tag, next_g):
    gu, act = _ffn_up(hn, fetch(names[0], hn).reshape(2, F, D), tag)
    got = _mm_nn(act[None], fetch(names[1], act)[None], f"down_{tag}", res=x, scale=0.5, next_g=next_g)
    out, hn_next = got if next_g is not None else (got, None)
    return out, hn_next, (x, hn, gu, act)


def _ffn_backward(dxo, dxo_b, saved, norm_g, wut, wd, tag, send):
    x, hn, gu, act = saved
    d_wd = _mm_tn(act[None], dxo_b, f"dwd_{tag}", scale=0.5)[0]
    du = _ffn_dact(dxo_b, wd, gu, send(("down",), [d_wd]), tag)
    d_wut = _mm_tn(du, hn, f"dwu_{tag}")
    token = send(("up",), [d_wut.reshape(2 * F, D)])
    return _mm_nn_norm_bwd([du], wut.reshape(2 * F, D), x, norm_g + token[0, 0], dxo, tag)


def _mixer_forward(x, hn, fetch, bias, tables, tag, next_g):
    proj = _mm_nt_rows(hn, fetch("win", hn), f"proj_{tag}", 512, IN_W // 2, IN_W, 0, rope=(*tables, 2 * QKV_A // 3))
    qkr = proj
    outs, lses = [], []
    for grp in range(3):
        o, l = _dil_fwd(qkr, proj, grp)
        outs.append(o)
        lses.append(l)
    ya = _combine_fwd(outs, lses)
    yb, lse_b = _na_fwd(proj, bias)
    merged, za, zb = _merge_fwd(ya, yb, proj, fetch("wa", yb), fetch("wb", yb))
    out, hn_next = _mm_nn(merged[None], fetch("wo", merged)[None], f"out_{tag}", res=x, next_g=next_g)
    return out, hn_next, (x, hn, proj, qkr, outs, lses, ya, yb, lse_b, merged, za, zb)


def _mixer_backward(dxo, dxo_b, saved, norm_g, w, bias, tables, tag, send):
    wint, wat, wbt, wo = w
    x, hn, proj, qkr, outs, lses, ya, yb, lse_b, merged, za, zb = saved
    d_wo = _mm_tn(merged[None], dxo_b, f"dwo_{tag}")[0]
    dza, dzb, dlog = _merge_bwd(dxo_b, wo, za, zb, proj)
    dya = _mm_nn(dza[None], wat[None], f"dya_{tag}")
    dyb = _mm_nn(dzb[None], wbt[None], f"dyb_{tag}")
    d_wat = _mm_tn(dza[None], ya, f"dwa_{tag}")[0]
    d_wbt = _mm_tn(dzb[None], yb, f"dwb_{tag}")[0]
    cb = _combine_bwd(dya, outs, lses)
    dqs, dks, dvs = [], [], []
    for grp in range(3):
        dq, dk, dv = _dil_bwd(qkr, proj, cb[grp], cb[3 + grp], lses[grp], grp)
        dqs.append(dq)
        dks.append(dk)
        dvs.append(dv)
    d_qkv_b, dbias_tab = _na_bwd(proj, bias, dyb, yb, lse_b)
    dbias = _na_dbias(dbias_tab)
    dproj = [_rope_bwd(dqs, dks, dvs, *tables), d_qkv_b, dlog]
    d_wint = jnp.concatenate([_mm_tn(p, hn, f"dwin{i}_{tag}").reshape(-1, D) for i, p in enumerate(dproj)])
    token = send(("win", "wa", "wb", "wo"), [d_wint, d_wat, d_wbt, d_wo])
    dx, dx_b, dg = _mm_nn_norm_bwd(dproj, wint, x, norm_g + token[0, 0], dxo, f"mix_{tag}")
    dbias = dbias[:, 0, :480].reshape(8, 15, 32)[:, :, :31]
    return dx, dx_b, dg, dbias


def _pack_small(norms, biases, final, loss=None):
    parts = []
    for layer in range(DEPTH):
        parts += [norms[0][layer], norms[1][layer], norms[2][layer],
                  jnp.pad(biases[layer].reshape(-1), (0, BIAS_PAD - 8 * 15 * 31))]
    parts.append(final)
    flat = jnp.concatenate([p.reshape(-1).astype(F32) for p in parts])
    if loss is not None:
        flat = jnp.concatenate([flat, loss.reshape(-1)])
    return jnp.pad(flat, (0, SMALL_ROWS * 128 - flat.shape[0])).reshape(SMALL_ROWS, 128)


def _unpack_small(packed):
    flat = packed.reshape(-1)
    norms, biases = ([], [], []), []
    pos = 0
    for _ in range(DEPTH):
        for k in range(3):
            norms[k].append(flat[pos:pos + D])
            pos += D
        biases.append(flat[pos:pos + 8 * 15 * 31].reshape(8, 15, 31))
        pos += BIAS_PAD
    final = flat[pos:pos + D]
    pos += D
    return [jnp.stack(n) for n in norms], jnp.stack(biases), final, flat[pos]


def kernel(x, ffn1_norm, ffn1_w_up, ffn1_w_down, mix_norm, w_in, na_rel_bias, w_branch_a, w_branch_b, w_out, ffn2_norm, ffn2_w_up, ffn2_w_down, final_norm, loss_target, m_ffn1_norm, m_ffn1_w_up, m_ffn1_w_down, m_mix_norm, m_w_in, m_na_rel_bias, m_w_branch_a, m_w_branch_b, m_w_out, m_ffn2_norm, m_ffn2_w_up, m_ffn2_w_down, m_final_norm, v_ffn1_norm, v_ffn1_w_up, v_ffn1_w_down, v_mix_norm, v_w_in, v_na_rel_bias, v_w_branch_a, v_w_branch_b, v_w_out, v_ffn2_norm, v_ffn2_w_up, v_ffn2_w_down, v_final_norm):
    t = x.shape[0] * x.shape[1]
    xs = x.reshape(t, D)
    tgt = loss_target.reshape(t, D)
    tables = _rope_tables()

    col_sharded = dict(up1=ffn1_w_up, win=w_in, wa=w_branch_a, wb=w_branch_b, up2=ffn2_w_up)
    row_sharded = dict(down1=ffn1_w_down, wo=w_out, down2=ffn2_w_down)
    shard = [{} for _ in range(DEPTH)]
    for layer in range(DEPTH):
        for name, arr in col_sharded.items():
            shard[layer][name] = arr[layer].T.astype(BF16)
        for name, arr in row_sharded.items():
            shard[layer][name] = arr[layer].astype(BF16)

    weights = [{} for _ in range(DEPTH)]
    travel = [(0, ("up1",)), (0, ("down1",)), (0, ("win",)), (0, ("wa", "wb", "wo")), (0, ("up2", "down2")),
              (1, ("up1", "down1")), (1, ("win",)), (1, ("wa", "wb", "wo")), (1, ("up2", "down2"))]
    group_of, chips_done, sibling_done = {}, {}, {}
    count = 0
    for i, (layer, names) in enumerate(travel):
        chips_done[i] = list(range(count, count + len(names)))
        count += len(names)
        for n in names:
            group_of[layer, n] = (i, names)
    gathered, token = _exchange_start(
        "gather", [shard[layer][n] for layer, names in travel for n in names], None, None, "w")
    zero = token[0, 0]

    biases = [_na_bias_table(na_rel_bias[layer] + zero) for layer in range(DEPTH)]

    def pass_on(i, behind):
        if i in chips_done:
            lands = _exchange_wait(gathered, behind, f"w{i}", which=chips_done.pop(i))
            sibling_done[i], _ = _exchange_start("forward", [], lands, None, f"p{i}")

    def fetcher(layer):
        def fetch(name, behind):
            if (layer, name) in group_of:
                i, names = group_of[layer, name]
                if i == 0:
                    behind = (behind, *biases)
                pass_on(i, behind)
                pass_on(i + 1, behind)
                for n, got in zip(names, _exchange_wait(sibling_done.pop(i), behind, f"p{i}")):
                    weights[layer][n] = got
                    del group_of[layer, n]
            return weights[layer][name]
        return fetch

    saved = []
    h = xs
    hn = _norm_fwd(xs, ffn1_norm[0] + zero, "first")
    for layer in range(DEPTH):
        bias = biases[layer]
        fetch = fetcher(layer)
        after_ffn2 = ffn1_norm[layer + 1] if layer + 1 < DEPTH else None
        h, hn, s1 = _ffn_forward(h, hn, fetch, ("up1", "down1"), f"f1l{layer}", mix_norm[layer])
        h, hn, s2 = _mixer_forward(h, hn, fetch, bias, tables, f"l{layer}", ffn2_norm[layer])
        h, hn, s3 = _ffn_forward(h, hn, fetch, ("up2", "down2"), f"f2l{layer}", after_ffn2)
        saved.append((s1, s2, s3, bias))
    loss_part, dh, dh_b, d_final = _loss_head(h, final_norm, tgt)

    d_norms = ([None] * DEPTH, [None] * DEPTH, [None] * DEPTH)
    d_bias = [None] * DEPTH
    sent = {}

    def sender(layer, suffix):
        def send(names, grads):
            tag = f"g{layer}{names[0]}{suffix}"
            handle, token = _exchange_start("scatter", grads, None, None, tag)
            for i, n in enumerate(names):
                sent[layer, n + suffix] = (handle, i, tag)
            return token
        return send

    for layer in reversed(range(DEPTH)):
        w = weights[layer]
        s1, s2, s3, bias = saved[layer]
        dh, dh_b, d_norms[2][layer] = _ffn_backward(
            dh, dh_b, s3, ffn2_norm[layer], w["up2"].reshape(2, F, D), w["down2"], f"f2l{layer}", sender(layer, "2"))
        dh, dh_b, d_norms[1][layer], d_bias[layer] = _mixer_backward(
            dh, dh_b, s2, mix_norm[layer], (w["win"], w["wa"], w["wb"], w["wo"]), bias, tables, f"l{layer}",
            sender(layer, ""))
        dh, dh_b, d_norms[0][layer] = _ffn_backward(
            dh, dh_b, s1, ffn1_norm[layer], w["up1"].reshape(2, F, D), w["down1"], f"f1l{layer}", sender(layer, "1"))
    grad_x = dh.reshape(x.shape)

    originals = dict(up1=(ffn1_w_up, m_ffn1_w_up, v_ffn1_w_up), down1=(ffn1_w_down, m_ffn1_w_down, v_ffn1_w_down),
                     win=(w_in, m_w_in, v_w_in), wa=(w_branch_a, m_w_branch_a, v_w_branch_a),
                     wb=(w_branch_b, m_w_branch_b, v_w_branch_b), wo=(w_out, m_w_out, v_w_out),
                     up2=(ffn2_w_up, m_ffn2_w_up, v_ffn2_w_up), down2=(ffn2_w_down, m_ffn2_w_down, v_ffn2_w_down))
    big = {}
    behind = dh
    landed = {}

    def received(layer, name):
        handle, i, tag = sent[layer, name]
        if tag not in landed:
            landed[tag] = _exchange_wait(handle, behind, tag)
        return landed[tag][i]

    for name in ("down2", "up2", "win", "wa", "wb", "wo", "down1", "up1"):
        wv, mv, vv = originals[name]
        if name in col_sharded:
            wv, mv, vv = (jnp.swapaxes(t, 1, 2) for t in (wv, mv, vv))
        big[name] = tuple(_sum_adamw(received(0, name), received(1, name), wv, mv, vv, name))
        behind = big[name][1]
        if name in col_sharded:
            big[name] = tuple(jnp.swapaxes(t, 1, 2) for t in big[name])

    small = _allreduce_small(_pack_small(d_norms, d_bias, d_final, loss_part[0, :1]), behind)
    g_norms, g_bias, g_final, loss = _unpack_small(small)
    w_small = _pack_small((ffn1_norm, mix_norm, ffn2_norm), na_rel_bias, final_norm)
    m_small = _pack_small((m_ffn1_norm, m_mix_norm, m_ffn2_norm), m_na_rel_bias, m_final_norm)
    v_small = _pack_small((v_ffn1_norm, v_mix_norm, v_ffn2_norm), v_na_rel_bias, v_final_norm)
    upd = _adamw(w_small[None], small[None], m_small[None], v_small[None], "small")
    small_out = [(g_norms, g_bias, g_final)] + [_unpack_small(u[0])[:3] for u in upd]

    outputs = [loss, grad_x]
    for kind in range(4):
        norms, bias_k, final_k = small_out[kind]
        outputs += [norms[0], big["up1"][kind], big["down1"][kind], norms[1], big["win"][kind], bias_k,
                    big["wa"][kind], big["wb"][kind], big["wo"][kind], norms[2], big["up2"][kind],
                    big["down2"][kind], final_k]
    return tuple(outputs)
```

```python
import numpy as np

import jax
import jax.numpy as jnp
from jax import lax
from jax.experimental import pallas as pl
from jax.experimental.pallas import tpu as pltpu

F32 = jnp.float32
BF16 = jnp.bfloat16
SDS = jax.ShapeDtypeStruct
BS = pl.BlockSpec
MESH = pl.DeviceIdType.MESH

D = 1024
S = 2048
F = 2816
DEPTH = 2
HEAD_DIM = 64
DILATIONS = (1, 4, 16)
HALF = 64
QKV_A = 2304
QKV_B = 1536
IN_W = 5888
N_DEV = 8
NA_ROWS = 32
GRID_W = 64
NA_KR = 8
ROPE_THETA = 10000.0
RMS_EPS = 1e-6
NEG = -1e30
SCALE = HEAD_DIM ** -0.5
ADAM_LR, ADAM_B1, ADAM_B2, ADAM_EPS, ADAM_WD, ADAM_STEP = 0.001, 0.9, 0.999, 1e-08, 0.01, 10
VMEM_LIMIT_V7X = 52 * 1024 * 1024
SMALL_ROWS = 120
BIAS_PAD = 3840
NA_FWD_ROWS = 8
NA_BWD_ROWS = 4
DIL_FWD_TILES = 8
DIL_BWD_TILES = 4


def _cp(*sem):
    return pltpu.CompilerParams(dimension_semantics=sem, vmem_limit_bytes=VMEM_LIMIT_V7X)


def _dot_nn(a, b):
    return jnp.dot(a, b, preferred_element_type=F32)


def _dot_nt(a, b):
    return lax.dot_general(a, b, (((1,), (1,)), ((), ())), preferred_element_type=F32)


def _dot_tn(a, b):
    return lax.dot_general(a, b, (((0,), (0,)), ((), ())), preferred_element_type=F32)


def _ds(start, size, stride):
    return pl.ds(start, size) if stride == 1 else pl.ds(start, size, stride=stride)


def _norm_fwd(x, g, tag):
    t = x.shape[0]
    tm = 512

    def body(x_ref, g_ref, o_ref):
        xv = x_ref[...]
        r = lax.rsqrt(jnp.mean(xv * xv, axis=-1, keepdims=True) + RMS_EPS)
        o_ref[...] = (xv * r * g_ref[...]).astype(BF16)

    return pl.pallas_call(
        body, name=f"norm_fwd_{tag}", grid=(t // tm,),
        in_specs=[BS((tm, D), lambda i: (i, 0)), BS((1, D), lambda i: (0, 0))],
        out_specs=BS((tm, D), lambda i: (i, 0)),
        out_shape=SDS((t, D), BF16), compiler_params=_cp("parallel"),
    )(x, g.reshape(1, D))


def _loss_head(x, g, tgt):
    t = x.shape[0]
    tm = 512

    def body(x_ref, g_ref, t_ref, loss_ref, dx_ref, dxb_ref, dg_ref):
        @pl.when(pl.program_id(0) == 0)
        def _():
            dg_ref[...] = jnp.zeros_like(dg_ref)
            loss_ref[...] = jnp.zeros_like(loss_ref)

        xv = x_ref[...]
        gv = g_ref[...]
        r = lax.rsqrt(jnp.mean(xv * xv, axis=-1, keepdims=True) + RMS_EPS)
        xh = xv * r
        e = xh * gv - t_ref[...]
        loss_ref[...] += 0.5 * jnp.sum(jnp.mean(e * e, axis=-1, keepdims=True), axis=0, keepdims=True)
        dy = e * (1.0 / D)
        u = dy * gv
        dx = r * (u - xh * jnp.mean(xh * u, axis=-1, keepdims=True))
        dx_ref[...] = dx
        dxb_ref[...] = dx.astype(BF16)
        dg_ref[...] += jnp.sum(dy * xh, axis=0, keepdims=True)

    row = BS((tm, D), lambda i: (i, 0))
    vec = BS((1, D), lambda i: (0, 0))
    return pl.pallas_call(
        body, name="loss_head", grid=(t // tm,),
        in_specs=[row, vec, row], out_specs=[BS((1, 128), lambda i: (0, 0)), row, row, vec],
        out_shape=[SDS((1, 128), F32), SDS((t, D), F32), SDS((t, D), BF16), SDS((1, D), F32)],
        compiler_params=_cp("arbitrary"),
    )(x, g.reshape(1, D), tgt)


def _mm_nn(a, w, tag, res=None, scale=1.0, tm=512, tn=None, next_g=None):
    c_n, t, k = a.shape
    n = w.shape[2]
    tn = n if tn is None else tn
    assert next_g is None or tn == n
    n_in = 2 + (res is not None) + (next_g is not None)

    def body(*refs):
        a_ref, w_ref = refs[0], refs[1]
        acc = _dot_nn(a_ref[0].astype(BF16), w_ref[0])
        for c in range(1, c_n):
            acc = acc + _dot_nn(a_ref[c].astype(BF16), w_ref[c])
        if scale != 1.0:
            acc = acc * scale
        if res is not None:
            acc = refs[2][...] + acc
        refs[n_in][...] = acc
        if next_g is not None:
            r = lax.rsqrt(jnp.mean(acc * acc, axis=-1, keepdims=True) + RMS_EPS)
            refs[n_in + 1][...] = (acc * r * refs[n_in - 1][...]).astype(BF16)

    in_specs = [BS((c_n, tm, k), lambda i, j: (0, i, 0)), BS((c_n, k, tn), lambda i, j: (0, 0, j))]
    args = [a, w]
    out_specs = [BS((tm, tn), lambda i, j: (i, j))]
    out_shape = [SDS((t, n), F32)]
    if res is not None:
        in_specs.append(BS((tm, tn), lambda i, j: (i, j)))
        args.append(res)
    if next_g is not None:
        in_specs.append(BS((1, n), lambda i, j: (0, 0)))
        args.append(next_g.reshape(1, n))
        out_specs.append(BS((tm, tn), lambda i, j: (i, j)))
        out_shape.append(SDS((t, n), BF16))
    got = pl.pallas_call(
        body, name=f"mm_nn_{tag}", grid=(t // tm, n // tn), in_specs=in_specs, out_specs=out_specs,
        out_shape=out_shape, compiler_params=_cp("parallel", "parallel"),
    )(*args)
    return got if next_g is not None else got[0]


def _mm_nn_norm_bwd(parts, w, x, g, dres, tag, tm=256):
    t = parts[0].shape[1]
    n_parts = len(parts)

    def body(*refs):
        w_ref, x_ref, g_ref, dr_ref, dx_ref, dxb_ref, dg_ref = refs[n_parts:]

        @pl.when(pl.program_id(0) == 0)
        def _():
            dg_ref[...] = jnp.zeros_like(dg_ref)

        dh = None
        row = 0
        for a_ref, part in zip(refs, parts):
            for c in range(part.shape[0]):
                term = _dot_nn(a_ref[c].astype(BF16), w_ref[row:row + part.shape[2], :])
                dh = term if dh is None else dh + term
                row += part.shape[2]
        xv = x_ref[...]
        r = lax.rsqrt(jnp.mean(xv * xv, axis=-1, keepdims=True) + RMS_EPS)
        xh = xv * r
        u = dh * g_ref[...]
        dx = dr_ref[...] + r * (u - xh * jnp.mean(xh * u, axis=-1, keepdims=True))
        dx_ref[...] = dx
        dxb_ref[...] = dx.astype(BF16)
        dg_ref[...] += jnp.sum(dh * xh, axis=0, keepdims=True)

    row = BS((tm, D), lambda i: (i, 0))
    vec = BS((1, D), lambda i: (0, 0))
    return pl.pallas_call(
        body, name=f"mm_nn_norm_bwd_{tag}", grid=(t // tm,),
        in_specs=[BS((p.shape[0], tm, p.shape[2]), lambda i: (0, i, 0)) for p in parts]
        + [BS(w.shape, lambda i: (0, 0)), row, vec, row],
        out_specs=[row, row, vec], out_shape=[SDS((t, D), F32), SDS((t, D), BF16), SDS((1, D), F32)],
        compiler_params=_cp("arbitrary"),
    )(*parts, w, x, g.reshape(1, D), dres)


def _mm_nt_rows(a, w, tag, tm, tn, n_total, w_row0, rope=None):
    t, k = a.shape
    assert w_row0 % tn == 0 and n_total % tn == 0
    j0 = w_row0 // tn

    def body(a_ref, w_ref, *rest):
        o_ref = rest[-1]
        o_ref[...] = _dot_nt(a_ref[...].astype(BF16), w_ref[...])
        if rope is not None:
            @pl.when(pl.program_id(0) == 0)
            def _():
                c = rest[0][...]
                sg = rest[1][...]
                first = (lax.broadcasted_iota(jnp.int32, (tm, 128), 1) % HEAD_DIM) < HEAD_DIM // 2
                for col in range(0, rope[2], 128):
                    v = o_ref[:, col:col + 128]
                    o_ref[:, col:col + 128] = v * c + _swap_halves(v, first) * sg

    in_specs = [BS((tm, k), lambda j, i: (i, 0)), BS((tn, k), lambda j, i: (j0 + j, 0))]
    args = [a, w]
    if rope is not None:
        assert rope[2] <= tn
        in_specs += [BS((tm, 128), lambda j, i: (i % (S // tm), 0))] * 2
        args += [rope[0], rope[1]]
    return pl.pallas_call(
        body, name=f"mm_nt_{tag}", grid=(n_total // tn, t // tm), in_specs=in_specs,
        out_specs=BS((tm, tn), lambda j, i: (i, j)), out_shape=SDS((t, n_total), F32),
        compiler_params=_cp("parallel", "parallel"),
    )(*args)


def _mm_tn(a, b, tag, scale=1.0, tmm=256, into=None, row0=0, rows=None):
    c_n, t, m = a.shape
    n = b.shape[1]
    tiles = m // tmm
    block0 = row0 // tmm
    assert row0 % tmm == 0 and m % tmm == 0

    def body(a_ref, b_ref, *rest):
        rest[-1][...] = (_dot_tn(a_ref[...].astype(BF16), b_ref[...].astype(BF16)) * scale).astype(BF16)

    in_specs = [BS((None, t, tmm), lambda c, mi: (c, 0, mi)), BS((t, n), lambda c, mi: (0, 0))]
    args = [a, b]
    if into is not None:
        in_specs.append(BS(memory_space=pl.ANY))
        args.append(into)
    return pl.pallas_call(
        body, name=f"mm_tn_{tag}", grid=(c_n, tiles), in_specs=in_specs,
        out_specs=BS((tmm, n), lambda c, mi: (block0 + c * tiles + mi, 0)),
        out_shape=SDS((rows or c_n * m, n) if into is None else into.shape, BF16),
        input_output_aliases={} if into is None else {2: 0},
        compiler_params=_cp("parallel", "parallel"),
    )(*args)


def _ffn_up(hn, wut, tag):
    t = hn.shape[0]
    tm, tn = 512, 1408

    def body(h_ref, w_ref, gu_ref, act_ref):
        h = h_ref[...]
        g = _dot_nt(h, w_ref[0])
        u = _dot_nt(h, w_ref[1])
        sg = jax.nn.sigmoid(g)
        silu = g * sg
        gu_ref[0] = (u * (sg + silu * (1.0 - sg))).astype(BF16)
        gu_ref[1] = silu.astype(BF16)
        act_ref[...] = (silu * u).astype(BF16)

    return pl.pallas_call(
        body, name=f"ffn_up_{tag}", grid=(F // tn, t // tm),
        in_specs=[BS((tm, D), lambda j, i: (i, 0)), BS((2, tn, D), lambda j, i: (0, j, 0))],
        out_specs=[BS((2, tm, tn), lambda j, i: (0, i, j)), BS((tm, tn), lambda j, i: (i, j))],
        out_shape=[SDS((2, t, F), BF16), SDS((t, F), BF16)],
        compiler_params=_cp("parallel", "parallel"),
    )(hn, wut)


def _ffn_dact(dxo, wd, gu, tie, tag):
    t = dxo.shape[0]
    tm, tn = 512, 1408

    def body(d_ref, w_ref, gu_ref, tie_ref, o_ref):
        dact = _dot_nt(d_ref[...] * 0.5, w_ref[...])
        o_ref[0] = (dact * gu_ref[0].astype(F32)).astype(BF16)
        o_ref[1] = (dact * gu_ref[1].astype(F32)).astype(BF16)

    return pl.pallas_call(
        body, name=f"ffn_dact_{tag}", grid=(F // tn, t // tm),
        in_specs=[BS((tm, D), lambda j, i: (i, 0)), BS((tn, D), lambda j, i: (j, 0)),
                  BS((2, tm, tn), lambda j, i: (0, i, j)), BS((8, 128), lambda j, i: (0, 0))],
        out_specs=BS((2, tm, tn), lambda j, i: (0, i, j)),
        out_shape=SDS((2, t, F), BF16), compiler_params=_cp("parallel", "parallel"),
    )(dxo, wd, gu, tie)


def _rope_tables():
    half = HEAD_DIM // 2
    inv_freq = ROPE_THETA ** (-jnp.arange(half, dtype=F32) / half)
    ang = jnp.arange(S).astype(F32)[:, None] * inv_freq[None, :]
    cos, sin = jnp.cos(ang), jnp.sin(ang)
    return jnp.concatenate([cos, cos, cos, cos], axis=1), jnp.concatenate([-sin, sin, -sin, sin], axis=1)


def _swap_halves(t, first_half):
    return jnp.where(first_half, pltpu.roll(t, 96, 1), pltpu.roll(t, 32, 1))


def _rope_bwd(dqs, dks, dvs, cos_t, sin_t):
    t = dqs[0].shape[0]
    tm = 512

    def body(*refs):
        c = refs[9][...]
        sg = refs[10][...]
        o_ref = refs[11]
        first = (lax.broadcasted_iota(jnp.int32, (tm, 128), 1) % HEAD_DIM) < HEAD_DIM // 2
        for a in range(6):
            for hp in range(2):
                v = refs[a][:, 128 * hp:128 * (hp + 1)]
                col = 128 * (2 * a + hp)
                o_ref[:, col:col + 128] = (v * c + _swap_halves(v * sg, first)).astype(BF16)
        for a in range(6, 9):
            o_ref[:, 256 * a:256 * (a + 1)] = refs[a][...].astype(BF16)

    blk = BS((tm, 256), lambda i: (i, 0))
    tab = BS((tm, 128), lambda i: (i % (S // tm), 0))
    return pl.pallas_call(
        body, name="rope_bwd", grid=(t // tm,), in_specs=[blk] * 9 + [tab, tab],
        out_specs=BS((None, tm, QKV_A), lambda i: (0, i, 0)), out_shape=SDS((1, t, QKV_A), BF16),
        compiler_params=_cp("parallel"),
    )(*dqs, *dks, *dvs, cos_t, sin_t)


def _head_masks():
    lane = lax.broadcasted_iota(jnp.int32, (1, 128), 1)
    m0 = (lane < HEAD_DIM).astype(F32)
    return m0, 1.0 - m0


def _dil_geometry(d):
    sub = S // d
    q_rows = 128
    k_rows = min(256, sub)
    return sub, q_rows, sub // q_rows, k_rows


def _dil_tile(idx, d, keys_on_rows=False):
    sub, q_rows, nb, k_rows = _dil_geometry(d)
    r = idx // nb
    n = idx % nb
    k_sub = jnp.clip(q_rows * n - HALF, 0, sub - k_rows)
    if d == 1:
        q_start = pl.multiple_of(q_rows * n, q_rows)
        k_start = pl.multiple_of(k_sub, HALF)
    else:
        q_start = q_rows * n * d + r
        k_start = k_sub * d + r
    if keys_on_rows:
        ii = lax.broadcasted_iota(jnp.int32, (k_rows, 2 * q_rows), 1) % q_rows
        jj = lax.broadcasted_iota(jnp.int32, (k_rows, 2 * q_rows), 0)
    else:
        ii = lax.broadcasted_iota(jnp.int32, (q_rows, k_rows), 0)
        jj = lax.broadcasted_iota(jnp.int32, (q_rows, k_rows), 1)
    valid = jnp.abs(jj - ii + (k_sub - q_rows * n)) <= HALF
    return q_start, k_start, valid


def _dil_specs(grp):
    qs = BS((S, 128), lambda b, hp: (b, 2 * grp + hp))
    ks = BS((S, 128), lambda b, hp: (b, 6 + 2 * grp + hp))
    vs = BS((S, 128), lambda b, hp: (b, 12 + 2 * grp + hp))
    own = BS((S, 128), lambda b, hp: (b, hp))
    return qs, ks, vs, own


def _dil_fwd(qkr, proj, grp):
    t = qkr.shape[0]
    d = DILATIONS[grp]
    _, q_rows, nb, k_rows = _dil_geometry(d)

    def body(q_ref, k_ref, v_ref, o_ref, l_ref):
        masks = _head_masks()

        def step(i0, carry):
            geo = [_dil_tile(i0 * DIL_FWD_TILES + j, d) for j in range(DIL_FWD_TILES)]
            tiles = [(j, h) for j in range(DIL_FWD_TILES) for h in range(2)]
            qs = [q_ref[_ds(g[0], q_rows, d), :] for g in geo]
            kbs = [k_ref[_ds(g[1], k_rows, d), :].astype(BF16) for g in geo]
            ss = [jnp.where(geo[j][2], _dot_nt((qs[j] * masks[h]).astype(BF16), kbs[j]) * SCALE, NEG) for j, h in tiles]
            mxs = [jnp.max(s, axis=1, keepdims=True) for s in ss]
            ps = [jnp.exp(s - mx) for s, mx in zip(ss, mxs)]
            dens = [jnp.sum(p, axis=1, keepdims=True) for p in ps]
            vs = [v_ref[_ds(g[1], k_rows, d), :] for g in geo]
            outs = [_dot_nn(p.astype(BF16), (vs[j] * masks[h]).astype(BF16)) / den
                    for p, den, (j, h) in zip(ps, dens, tiles)]
            for j, g in enumerate(geo):
                o_ref[_ds(g[0], q_rows, d), :] = outs[2 * j] + outs[2 * j + 1]
                l_ref[_ds(g[0], q_rows, d), :] = (
                    (mxs[2 * j] + jnp.log(dens[2 * j])) * masks[0] + (mxs[2 * j + 1] + jnp.log(dens[2 * j + 1])) * masks[1])
            return carry

        lax.fori_loop(0, d * nb // DIL_FWD_TILES, step, 0)

    qs, ks, vs, own = _dil_specs(grp)
    return pl.pallas_call(
        body, name=f"dil_fwd_{grp}", grid=(t // S, 2), in_specs=[qs, ks, vs], out_specs=[own, own],
        out_shape=[SDS((t, 256), F32), SDS((t, 256), F32)], compiler_params=_cp("parallel", "parallel"),
    )(qkr, qkr, proj)


def _dil_bwd(qkr, proj, do, dlp, lse, grp):
    t = qkr.shape[0]
    d = DILATIONS[grp]
    _, q_rows, nb, k_rows = _dil_geometry(d)

    def body(q_ref, k_ref, v_ref, do_ref, dl_ref, l_ref, dq_ref, dk_ref, dv_ref):
        masks = _head_masks()
        dk_ref[...] = jnp.zeros_like(dk_ref)
        dv_ref[...] = jnp.zeros_like(dv_ref)

        def as_row(x2):
            xt = x2.T
            return jnp.concatenate([xt[0:1], xt[HEAD_DIM:HEAD_DIM + 1]], axis=1)

        def step(i0, carry):
            geo = [_dil_tile(i0 * DIL_BWD_TILES + j, d, keys_on_rows=True) for j in range(DIL_BWD_TILES)]
            q_ds = [_ds(g[0], q_rows, d) for g in geo]
            k_ds = [_ds(g[1], k_rows, d) for g in geo]
            qbs = [_both_heads(q_ref[r, :], masks).astype(BF16) for r in q_ds]
            kbs = [k_ref[r, :].astype(BF16) for r in k_ds]
            vbs = [v_ref[r, :].astype(BF16) for r in k_ds]
            dobs = [_both_heads(do_ref[r, :], masks).astype(BF16) for r in q_ds]
            l_rows = [as_row(l_ref[r, :]) for r in q_ds]
            dl_rows = [as_row(dl_ref[r, :]) for r in q_ds]
            ss = [jnp.where(g[2], _dot_nt(kb, qb) * SCALE, NEG) for g, kb, qb in zip(geo, kbs, qbs)]
            ps = [jnp.exp(s - lr) for s, lr in zip(ss, l_rows)]
            dps = [_dot_nt(vb, dob) for vb, dob in zip(vbs, dobs)]
            dss = [(p * (dp - dr)).astype(BF16) for p, dp, dr in zip(ps, dps, dl_rows)]
            dks = [_dot_nn(ds, qb) for ds, qb in zip(dss, qbs)]
            dvs = [_dot_nn(p.astype(BF16), dob) for p, dob in zip(ps, dobs)]
            dqs = [_own_heads(_dot_tn(ds, kb), masks) for ds, kb in zip(dss, kbs)]
            for j in range(DIL_BWD_TILES):
                dq_ref[q_ds[j], :] = dqs[j] * SCALE
                dk_ref[k_ds[j], :] += dks[j] * SCALE
                dv_ref[k_ds[j], :] += dvs[j]
            return carry

        lax.fori_loop(0, d * nb // DIL_BWD_TILES, step, 0)

    qs, ks, vs, own = _dil_specs(grp)
    return pl.pallas_call(
        body, name=f"dil_bwd_{grp}", grid=(t // S, 2), in_specs=[qs, ks, vs, own, own, own],
        out_specs=[own, own, own], out_shape=[SDS((t, 256), F32)] * 3,
        compiler_params=_cp("parallel", "parallel"),
    )(qkr, qkr, proj, do, dlp, lse)


def _mix_weights(l0, l1, l2):
    mx = jnp.maximum(jnp.maximum(l0, l1), l2)
    e0, e1, e2 = jnp.exp(l0 - mx), jnp.exp(l1 - mx), jnp.exp(l2 - mx)
    den = e0 + e1 + e2
    return e0 / den, e1 / den, e2 / den


def _combine_fwd(outs, lses):
    t = outs[0].shape[0]
    tm = 512

    def body(o0, o1, o2, l0, l1, l2, y_ref):
        w0, w1, w2 = _mix_weights(l0[...], l1[...], l2[...])
        y_ref[...] = w0 * o0[...] + w1 * o1[...] + w2 * o2[...]

    blk = BS((tm, 256), lambda i: (i, 0))
    return pl.pallas_call(
        body, name="combine_fwd", grid=(t // tm,), in_specs=[blk] * 6, out_specs=blk,
        out_shape=SDS((t, 256), F32), compiler_params=_cp("parallel"),
    )(*outs, *lses)


def _head_sum(x):
    a = lax.broadcasted_iota(jnp.int32, (256, 256), 0) // HEAD_DIM
    b = lax.broadcasted_iota(jnp.int32, (256, 256), 1) // HEAD_DIM
    ones = (a == b).astype(BF16)
    hi = x.astype(BF16)
    lo = (x - hi.astype(F32)).astype(BF16)
    return _dot_nn(hi, ones) + _dot_nn(lo, ones)


def _combine_bwd(dya, outs, lses):
    t = dya.shape[0]
    tm = 512

    def body(dy_ref, o0, o1, o2, l0, l1, l2, d0, d1, d2, e0, e1, e2):
        ws = _mix_weights(l0[...], l1[...], l2[...])
        dy = dy_ref[...]
        ya = ws[0] * o0[...] + ws[1] * o1[...] + ws[2] * o2[...]
        hs = _head_sum(dy * ya)
        for w, d_ref, e_ref in zip(ws, (d0, d1, d2), (e0, e1, e2)):
            d_ref[...] = w * dy
            e_ref[...] = w * hs

    blk = BS((tm, 256), lambda i: (i, 0))
    return pl.pallas_call(
        body, name="combine_bwd", grid=(t // tm,), in_specs=[blk] * 7, out_specs=[blk] * 6,
        out_shape=[SDS((t, 256), F32)] * 6, compiler_params=_cp("parallel"),
    )(dya, *outs, *lses)


def _na_bias_table(rel_bias):
    kw = NA_KR * GRID_W
    rev = jnp.pad(rel_bias.astype(F32)[:, :, ::-1], ((0, 0), (0, 0), (0, 128 - 31)))

    def body(r_ref, o_ref):
        lane = lax.broadcasted_iota(jnp.int32, (GRID_W, 128), 1)
        j = lax.broadcasted_iota(jnp.int32, (GRID_W, 128), 0)
        q = lane % GRID_W
        win_lo = jnp.clip(q - 8, 0, GRID_W - 16)
        valid = (j >= win_lo) & (j < win_lo + 16)
        for cls in range(NA_KR):
            for k in range(NA_KR):
                tiles = []
                for h in range(2):
                    row = jnp.broadcast_to(r_ref[h, cls + k:cls + k + 1, :], (GRID_W, 128))
                    tiles.append(pltpu.roll(row, (128 - 15 + GRID_W * h) % 128, 1, stride=1, stride_axis=0))
                o_ref[cls, GRID_W * k:GRID_W * (k + 1), :] = jnp.where(
                    valid, jnp.where(lane < GRID_W, tiles[0], tiles[1]), NEG)

    return pl.pallas_call(
        body, name="na_bias_table", grid=(4,),
        in_specs=[BS((2, 2 * NA_KR - 1, 128), lambda hp: (hp, 0, 0))],
        out_specs=BS((None, NA_KR, kw, 128), lambda hp: (hp, 0, 0, 0)),
        out_shape=SDS((4, NA_KR, kw, 128), F32), compiler_params=_cp("parallel"),
    )(rev)


def _na_row(i):
    lo = jnp.clip(i - NA_KR // 2, 0, NA_ROWS - NA_KR)
    return pl.multiple_of(GRID_W * i, GRID_W), pl.multiple_of(GRID_W * lo, GRID_W), lo - i + NA_KR - 1


def _both_heads(x, masks):
    return jnp.concatenate([x * masks[0], x * masks[1]], axis=0)


def _own_heads(r, masks):
    half = r.shape[0] // 2
    return r[:half] * masks[0] + r[half:] * masks[1]


def _na_fwd(proj, bias):
    t = proj.shape[0]
    kw = NA_KR * GRID_W

    def body(q_ref, k_ref, v_ref, b_ref, o_ref, l_ref):
        masks = _head_masks()

        def step(i0, carry):
            idx = [i0 * NA_FWD_ROWS + j for j in range(NA_FWD_ROWS)]
            rows = [_na_row(i) for i in idx]
            qbs = [_both_heads(q_ref[pl.ds(r[0], GRID_W), :], masks).astype(BF16) for r in rows]
            kbs = [k_ref[pl.ds(r[1], kw), :].astype(BF16) for r in rows]
            ss = [_dot_nt(kb, qb) * SCALE + b_ref[r[2]] for kb, qb, r in zip(kbs, qbs, rows)]
            mxs = [jnp.max(s, axis=0, keepdims=True) for s in ss]
            ps = [jnp.exp(s - mx) for s, mx in zip(ss, mxs)]
            dens = [jnp.sum(p, axis=0, keepdims=True) for p in ps]
            pbs = [(p / den).astype(BF16) for p, den in zip(ps, dens)]
            vbs = [v_ref[pl.ds(r[1], kw), :].astype(BF16) for r in rows]
            outs = [_own_heads(_dot_tn(pb, vb), masks) for pb, vb in zip(pbs, vbs)]
            for j, r in enumerate(rows):
                o_ref[pl.ds(r[0], GRID_W), :] = outs[j]
                l_ref[pl.ds(idx[j], 1), :] = mxs[j] + jnp.log(dens[j])
            return carry

        lax.fori_loop(0, NA_ROWS // NA_FWD_ROWS, step, 0)

    c0 = QKV_A // 128
    return pl.pallas_call(
        body, name="na_fwd", grid=(t // S, 4),
        in_specs=[BS((S, 128), lambda b, hp: (b, c0 + hp)), BS((S, 128), lambda b, hp: (b, c0 + 4 + hp)),
                  BS((S, 128), lambda b, hp: (b, c0 + 8 + hp)),
                  BS((None, NA_KR, kw, 128), lambda b, hp: (hp, 0, 0, 0))],
        out_specs=[BS((S, 128), lambda b, hp: (b, hp)), BS((None, None, NA_ROWS, 128), lambda b, hp: (b, hp, 0, 0))],
        out_shape=[SDS((t, 512), F32), SDS((t // S, 4, NA_ROWS, 128), F32)],
        compiler_params=_cp("parallel", "parallel"),
    )(proj, proj, proj, bias)


def _na_bwd(proj, bias, dyb, yb, lse):
    t = proj.shape[0]
    kw = NA_KR * GRID_W

    def body(q_ref, k_ref, v_ref, b_ref, do_ref, o_ref, l_ref, d_ref, db_ref):
        masks = _head_masks()
        ones = jnp.ones((8, 128), BF16)

        @pl.when(pl.program_id(1) == 0)
        def _():
            db_ref[...] = jnp.zeros_like(db_ref)

        d_ref[1:3] = jnp.zeros((2, S, 128), F32)

        def row_sums(x):
            hi = x.astype(BF16)
            lo = (x - hi.astype(F32)).astype(BF16)
            return (_dot_nt(ones, hi) + _dot_nt(ones, lo))[0:1]

        def step(i0, carry):
            idx = [i0 * NA_BWD_ROWS + j for j in range(NA_BWD_ROWS)]
            rows = [_na_row(i) for i in idx]
            q_ds = [pl.ds(r[0], GRID_W) for r in rows]
            k_ds = [pl.ds(r[1], kw) for r in rows]
            qbs = [_both_heads(q_ref[r, :], masks).astype(BF16) for r in q_ds]
            kbs = [k_ref[r, :].astype(BF16) for r in k_ds]
            vbs = [v_ref[r, :].astype(BF16) for r in k_ds]
            dos = [do_ref[r, :] for r in q_ds]
            dobs = [_both_heads(do, masks).astype(BF16) for do in dos]
            deltas = [row_sums(_both_heads(do * o_ref[r, :], masks)) for do, r in zip(dos, q_ds)]
            ss = [_dot_nt(kb, qb) * SCALE + b_ref[r[2]] for kb, qb, r in zip(kbs, qbs, rows)]
            ps = [jnp.exp(s - l_ref[pl.ds(i, 1), :]) for s, i in zip(ss, idx)]
            dps = [_dot_nt(vb, dob) for vb, dob in zip(vbs, dobs)]
            dss = [p * (dp - delta) for p, dp, delta in zip(ps, dps, deltas)]
            for ds, r in zip(dss, rows):
                db_ref[r[2]] += ds
            dsbs = [ds.astype(BF16) for ds in dss]
            dks = [_dot_nn(dsb, qb) for dsb, qb in zip(dsbs, qbs)]
            dvs = [_dot_nn(p.astype(BF16), dob) for p, dob in zip(ps, dobs)]
            dqs = [_own_heads(_dot_tn(dsb, kb), masks) for dsb, kb in zip(dsbs, kbs)]
            for j in range(NA_BWD_ROWS):
                d_ref[0, q_ds[j], :] = dqs[j] * SCALE
                d_ref[1, k_ds[j], :] += dks[j] * SCALE
                d_ref[2, k_ds[j], :] += dvs[j]
            return carry

        lax.fori_loop(0, NA_ROWS // NA_BWD_ROWS, step, 0)

    c0 = QKV_A // 128
    own = BS((S, 128), lambda hp, b: (b, hp))
    tab = BS((None, NA_KR, kw, 128), lambda hp, b: (hp, 0, 0, 0))
    return pl.pallas_call(
        body, name="na_bwd", grid=(4, t // S),
        in_specs=[BS((S, 128), lambda hp, b: (b, c0 + hp)), BS((S, 128), lambda hp, b: (b, c0 + 4 + hp)),
                  BS((S, 128), lambda hp, b: (b, c0 + 8 + hp)), tab, own, own,
                  BS((None, None, NA_ROWS, 128), lambda hp, b: (b, hp, 0, 0))],
        out_specs=[BS((3, S, 128), lambda hp, b: (0, b, hp)), tab],
        out_shape=[SDS((3, t, 512), F32), SDS((4, NA_KR, kw, 128), F32)],
        compiler_params=_cp("parallel", "arbitrary"),
    )(proj, proj, proj, bias, dyb, yb, lse)


def _na_dbias_lane_map():
    kw = NA_KR * GRID_W
    lane = np.arange(kw)
    blk, m = lane // GRID_W, lane % GRID_W
    target = np.full(kw, -1)
    target[m < 16] = (blk * 32 + 15 + m)[m < 16]
    target[m >= 49] = (((blk + 1) % NA_KR) * 32 + m - 49)[m >= 49]
    return jnp.asarray(target[:, None] == np.arange(kw)[None, :], BF16)


def _na_dbias(db):
    kw = NA_KR * GRID_W

    def body(x_ref, map_ref, o_ref, z_ref):
        for cls in range(NA_KR):
            xt = x_ref[cls].T
            for h in range(2):
                xv = xt[GRID_W * h:GRID_W * (h + 1)]
                y = xv[0:8]
                for g in range(1, GRID_W // 8):
                    y = y + pltpu.roll(xv[8 * g:8 * g + 8], kw - 8 * g, 1)
                d = y[0:1]
                for s in range(1, 8):
                    d = d + pltpu.roll(y[s:s + 1], kw - s, 1)
                z_ref[h, cls:cls + 1, :] = d
        for h in range(2):
            z = z_ref[h]
            hi = z.astype(BF16)
            lo = (z - hi.astype(F32)).astype(BF16)
            e = _dot_nn(hi, map_ref[...]) + _dot_nn(lo, map_ref[...])
            out = e[0:1]
            for cls in range(1, NA_KR):
                out = out + pltpu.roll(e[cls:cls + 1], 32 * cls, 1)
            o_ref[h] = jnp.broadcast_to(out, (8, kw))

    return pl.pallas_call(
        body, name="na_dbias", grid=(4,),
        in_specs=[BS((None, NA_KR, kw, 128), lambda hp: (hp, 0, 0, 0)), BS((kw, kw), lambda hp: (0, 0))],
        out_specs=BS((2, 8, kw), lambda hp: (hp, 0, 0)), out_shape=SDS((8, 8, kw), F32),
        scratch_shapes=[pltpu.VMEM((2, 8, kw), F32)], compiler_params=_cp("parallel"),
    )(db, _na_dbias_lane_map())


def _merge_fwd(ya, yb, proj, wat, wbt):
    t = ya.shape[0]
    tm, tn = 512, 256
    ca = (QKV_A + QKV_B) // tn
    cb = ca + D // tn

    def body(ya_ref, yb_ref, la_ref, lb_ref, wa_ref, wb_ref, m_ref, za_ref, zb_ref):
        za = _dot_nt(ya_ref[...].astype(BF16), wa_ref[...])
        zb = _dot_nt(yb_ref[...].astype(BF16), wb_ref[...])
        m_ref[...] = (jax.nn.sigmoid(la_ref[...]) * za + jax.nn.sigmoid(lb_ref[...]) * zb).astype(BF16)
        za_ref[...] = za.astype(BF16)
        zb_ref[...] = zb.astype(BF16)

    out = BS((tm, tn), lambda i, j: (i, j))
    return pl.pallas_call(
        body, name="merge_fwd", grid=(t // tm, D // tn),
        in_specs=[BS((tm, 256), lambda i, j: (i, 0)), BS((tm, 512), lambda i, j: (i, 0)),
                  BS((tm, tn), lambda i, j: (i, ca + j)), BS((tm, tn), lambda i, j: (i, cb + j)),
                  BS((tn, 256), lambda i, j: (j, 0)), BS((tn, 512), lambda i, j: (j, 0))],
        out_specs=[out, out, out], out_shape=[SDS((t, D), BF16)] * 3,
        compiler_params=_cp("parallel", "parallel"),
    )(ya, yb, proj, proj, wat, wbt)


def _merge_bwd(dxo, wo, za, zb, proj):
    t = dxo.shape[0]
    tm, tn = 512, 256
    ca = (QKV_A + QKV_B) // tn
    cb = ca + D // tn

    def body(d_ref, w_ref, za_ref, zb_ref, la_ref, lb_ref, dza_ref, dzb_ref, dl_ref):
        dmv = _dot_nt(d_ref[...], w_ref[...])
        ga = jax.nn.sigmoid(la_ref[...])
        gb = jax.nn.sigmoid(lb_ref[...])
        dza_ref[...] = (dmv * ga).astype(BF16)
        dzb_ref[...] = (dmv * gb).astype(BF16)
        dl_ref[0] = (dmv * za_ref[...].astype(F32) * ga * (1.0 - ga)).astype(BF16)
        dl_ref[1] = (dmv * zb_ref[...].astype(F32) * gb * (1.0 - gb)).astype(BF16)

    blk = BS((tm, tn), lambda i, j: (i, j))
    return pl.pallas_call(
        body, name="merge_bwd", grid=(t // tm, D // tn),
        in_specs=[BS((tm, D), lambda i, j: (i, 0)), BS((tn, D), lambda i, j: (j, 0)), blk, blk,
                  BS((tm, tn), lambda i, j: (i, ca + j)), BS((tm, tn), lambda i, j: (i, cb + j))],
        out_specs=[blk, blk, BS((2, tm, tn), lambda i, j: (0, i, j))],
        out_shape=[SDS((t, D), BF16), SDS((t, D), BF16), SDS((2, t, D), BF16)],
        compiler_params=_cp("parallel", "parallel"),
    )(dxo, wo, za, zb, proj, proj)


def _adamw_update(w, g, m, v):
    mn = ADAM_B1 * m + (1.0 - ADAM_B1) * g
    vn = ADAM_B2 * v + (1.0 - ADAM_B2) * (g * g)
    m_hat = mn / (1.0 - ADAM_B1 ** ADAM_STEP)
    v_hat = vn / (1.0 - ADAM_B2 ** ADAM_STEP)
    return -ADAM_LR * (m_hat / (jnp.sqrt(v_hat) + ADAM_EPS) + ADAM_WD * w), mn, vn


def _sum_adamw(recv0, recv1, w, m, v, tag):
    _, r, c = recv0.shape
    tr = max(rows for rows in range(16, r + 1, 16) if r % rows == 0 and rows * c <= 192 * 1024)

    def body(a_ref, b_ref, w_ref, m_ref, v_ref, g_ref, d_ref, mo_ref, vo_ref):
        for layer, ref in enumerate((a_ref, b_ref)):
            g = ref[0].astype(F32)
            for s in range(1, N_DEV):
                g = g + ref[s].astype(F32)
            g_ref[layer] = g
            d_ref[layer], mo_ref[layer], vo_ref[layer] = _adamw_update(w_ref[layer], g, m_ref[layer], v_ref[layer])

    slots = BS((N_DEV, tr, c), lambda i: (0, i, 0))
    blk = BS((2, tr, c), lambda i: (0, i, 0))
    return pl.pallas_call(
        body, name=f"sum_adamw_{tag}", grid=(r // tr,), in_specs=[slots, slots, blk, blk, blk],
        out_specs=[blk] * 4, out_shape=[SDS((2, r, c), F32)] * 4, compiler_params=_cp("parallel"),
    )(recv0, recv1, w, m, v)


def _adamw(w, g, m, v, tag):
    layers, r, c = w.shape
    tr = next(r // k for k in (1, 2, 4, 8) if r // k <= 384 and r % (8 * k) == 0)

    def body(w_ref, g_ref, m_ref, v_ref, d_ref, mo_ref, vo_ref):
        d_ref[...], mo_ref[...], vo_ref[...] = _adamw_update(w_ref[...], g_ref[...], m_ref[...], v_ref[...])

    blk = BS((None, tr, c), lambda l, i: (l, i, 0))
    return pl.pallas_call(
        body, name=f"adamw_{tag}", grid=(layers, r // tr), in_specs=[blk] * 4, out_specs=[blk] * 3,
        out_shape=[SDS((layers, r, c), F32)] * 3, compiler_params=_cp("parallel", "parallel"),
    )(w, g, m, v)


def _place():
    return lax.axis_index("x"), lax.axis_index("y"), lax.axis_index("c")


def _flip(coord, bit):
    return 1 - coord if bit else coord


def _peers(x, y, c):
    peers = []
    for mask in range(1, N_DEV):
        p = (_flip(x, mask & 4), _flip(y, mask & 2), _flip(c, mask & 1))
        peers.append((p, 4 * p[0] + 2 * p[1] + p[2]))
    return peers


def _copy_plan(mode, src, land, x, y, c):
    me = 4 * x + 2 * y + c

    def device(mask):
        p = (_flip(x, mask & 4), _flip(y, mask & 2), _flip(c, mask & 1))
        return p, 4 * p[0] + 2 * p[1] + p[2]

    if mode == "scatter":
        r = land.shape[1]
        return [(p, src.at[pl.ds(i * r, r), :], land.at[me], land.at[i])
                for p, i in map(device, (1, 2, 3, 4, 5, 6, 7, 0))]
    r = land.shape[0] // N_DEV

    def rows(i):
        return land.at[pl.ds(i * r, r), :]

    if mode == "gather":
        return [(p, src, rows(me), rows(i)) for p, i in map(device, (1, 4, 2, 6, 0))]
    sibling = device(1)[0]
    return [(sibling, rows(device(m)[1]), rows(device(m)[1]), rows(device(m | 1)[1])) for m in (4, 2, 6)]


COPIES = dict(scatter=8, gather=5, forward=3)
HBM_SPEC = BS(memory_space=pltpu.HBM)
SEM_SPEC = BS(memory_space=pltpu.SEMAPHORE)
DATAFLOW = pltpu.SideEffectType.DATAFLOW_SIDE_EFFECTING


def _fresh(shape, dtype, tag):
    def body(o_ref):
        del o_ref

    return pl.pallas_call(body, name=f"fresh_{tag}", out_specs=BS(memory_space=pl.ANY), out_shape=SDS(shape, dtype))()


def _exchange_start(mode, srcs, lands, after, tag):
    if lands is None and mode == "gather":
        lands = [_fresh((N_DEV * s.shape[0], s.shape[1]), s.dtype, f"{tag}_{a}") for a, s in enumerate(srcs)]
    elif lands is None:
        lands = [_fresh((N_DEV, s.shape[0] // N_DEV, s.shape[1]), s.dtype, f"{tag}_{a}") for a, s in enumerate(srcs)]
    n, n_src, n_cp = len(lands), len(srcs), COPIES[mode]
    behind = [] if after is None else [after]

    def body(*refs):
        src_refs, land_refs = refs[:n_src], refs[n_src:n_src + n]
        send_sems, recv_sems = refs[n_src + n + len(behind)], refs[n_src + n + len(behind) + 1]
        token = refs[-1]
        for a in range(n):
            plan = _copy_plan(mode, src_refs[a] if n_src else None, land_refs[a], *_place())
            for k, (p, out, there, _) in enumerate(plan):
                pltpu.make_async_remote_copy(
                    src_ref=out, dst_ref=there, send_sem=send_sems.at[n_cp * a + k],
                    recv_sem=recv_sems.at[n_cp * a + k], device_id=p, device_id_type=MESH).start()
        token[...] = jnp.zeros_like(token)

    both = [*srcs, *lands]
    res = pl.pallas_call(
        body, name=f"{mode}_start_{tag}",
        out_shape=(pltpu.SemaphoreType.DMA((n_cp * n,)), pltpu.SemaphoreType.DMA((n_cp * n,)),
                   *[pltpu.HBM(v.shape, v.dtype) for v in both], SDS((8, 128), F32)),
        in_specs=[HBM_SPEC] * len(both) + [BS(memory_space=pl.ANY)] * len(behind),
        out_specs=(SEM_SPEC, SEM_SPEC, *[HBM_SPEC] * len(both), BS(memory_space=pltpu.VMEM)),
        input_output_aliases={i: 2 + i for i in range(len(both))},
        compiler_params=pltpu.CompilerParams(has_side_effects=DATAFLOW),
    )(*[pltpu.with_memory_space_constraint(v, pltpu.HBM) for v in both], *behind)
    return (mode, res[0], res[1], res[2:2 + n_src], res[2 + n_src:2 + n_src + n]), res[-1]


def _exchange_wait(handle, after, tag, which=None):
    mode, send_sems, recv_sems, srcs, lands = handle
    which = list(range(len(lands))) if which is None else list(which)
    n_cp = COPIES[mode]
    lands = [lands[a] for a in which]
    srcs = [srcs[a] for a in which] if srcs else []
    n, n_src = len(lands), len(srcs)
    afters = list(after) if isinstance(after, (tuple, list)) else [after]

    def body(*refs):
        src_refs, land_refs = refs[:n_src], refs[n_src:n_src + n]
        send_ref, recv_ref = refs[n_src + n], refs[n_src + n + 1]
        for i, a in enumerate(which):
            plan = _copy_plan(mode, src_refs[i] if n_src else None, land_refs[i], *_place())
            for k, (p, out, _, here) in enumerate(plan):
                cp = pltpu.make_async_remote_copy(
                    src_ref=out, dst_ref=here, send_sem=send_ref.at[n_cp * a + k], recv_sem=recv_ref.at[n_cp * a + k],
                    device_id=p, device_id_type=MESH)
                cp.wait_send()
                cp.wait_recv()

    both = [*srcs, *lands]
    res = pl.pallas_call(
        body, name=f"{mode}_wait_{tag}", out_shape=tuple(pltpu.HBM(v.shape, v.dtype) for v in both),
        in_specs=[HBM_SPEC] * len(both) + [SEM_SPEC, SEM_SPEC] + [BS(memory_space=pl.ANY)] * len(afters),
        out_specs=tuple([HBM_SPEC] * len(both)),
        input_output_aliases={i: i for i in range(len(both))},
        compiler_params=pltpu.CompilerParams(has_side_effects=DATAFLOW),
    )(*both, send_sems, recv_sems, *afters)
    return list(res[n_src:])


def _allreduce_small(vec, behind):
    rows = vec.shape[0]

    def body(x_ref, behind_ref, o_ref, buf_ref, send_sems, recv_sems):
        x, y, c = _place()
        me = 4 * x + 2 * y + c
        buf_ref[me] = x_ref[...]
        peers = _peers(x, y, c)

        def copy(k, slot):
            return pltpu.make_async_remote_copy(
                src_ref=x_ref, dst_ref=buf_ref.at[slot], send_sem=send_sems.at[k], recv_sem=recv_sems.at[k],
                device_id=peers[k][0], device_id_type=MESH)

        sends = [copy(k, me) for k in range(N_DEV - 1)]
        for cp in sends:
            cp.start()
        for k in range(N_DEV - 1):
            copy(k, peers[k][1]).wait_recv()
        for cp in sends:
            cp.wait_send()
        acc = buf_ref[0]
        for s in range(1, N_DEV):
            acc = acc + buf_ref[s]
        o_ref[...] = acc

    vmem = BS(memory_space=pltpu.VMEM)
    return pl.pallas_call(
        body, name="allreduce_small", in_specs=[vmem, BS(memory_space=pl.ANY)], out_specs=vmem,
        out_shape=SDS((rows, 128), F32),
        scratch_shapes=[pltpu.VMEM((N_DEV, rows, 128), F32), pltpu.SemaphoreType.DMA((7,)),
                        pltpu.SemaphoreType.DMA((7,))],
        compiler_params=pltpu.CompilerParams(has_side_effects=True),
    )(vec, behind)


def _ffn_forward(x, hn, fetch, names, tag, next_g):
    gu, act = _ffn_up(hn, fetch(names[0], hn).reshape(2, F, D), tag)
    got = _mm_nn(act[None], fetch(names[1], act)[None], f"down_{tag}", res=x, scale=0.5, next_g=next_g)
    out, hn_next = got if next_g is not None else (got, None)
    return out, hn_next, (x, hn, gu, act)


def _ffn_backward(dxo, dxo_b, saved, norm_g, wut, wd, tag, send):
    x, hn, gu, act = saved
    d_wd = _mm_tn(act[None], dxo_b, f"dwd_{tag}", scale=0.5)
    du = _ffn_dact(dxo_b, wd, gu, send(("down",), [d_wd]), tag)
    d_wut = _mm_tn(du, hn, f"dwu_{tag}")
    token = send(("up",), [d_wut])
    return _mm_nn_norm_bwd([du], wut.reshape(2 * F, D), x, norm_g + token[0, 0], dxo, tag)


def _mixer_forward(x, hn, fetch, bias, tables, tag, next_g):
    proj = _mm_nt_rows(hn, fetch("win", hn), f"proj_{tag}", 512, IN_W // 2, IN_W, 0, rope=(*tables, 2 * QKV_A // 3))
    qkr = proj
    outs, lses = [], []
    for grp in range(3):
        o, l = _dil_fwd(qkr, proj, grp)
        outs.append(o)
        lses.append(l)
    ya = _combine_fwd(outs, lses)
    yb, lse_b = _na_fwd(proj, bias)
    merged, za, zb = _merge_fwd(ya, yb, proj, fetch("wa", yb), fetch("wb", yb))
    out, hn_next = _mm_nn(merged[None], fetch("wo", merged)[None], f"out_{tag}", res=x, next_g=next_g)
    return out, hn_next, (x, hn, proj, qkr, outs, lses, ya, yb, lse_b, merged, za, zb)


def _mixer_backward(dxo, dxo_b, saved, norm_g, w, bias, tables, tag, send):
    wint, wat, wbt, wo = w
    x, hn, proj, qkr, outs, lses, ya, yb, lse_b, merged, za, zb = saved
    d_wo = _mm_tn(merged[None], dxo_b, f"dwo_{tag}")
    dza, dzb, dlog = _merge_bwd(dxo_b, wo, za, zb, proj)
    dya = _mm_nn(dza[None], wat[None], f"dya_{tag}")
    dyb = _mm_nn(dzb[None], wbt[None], f"dyb_{tag}")
    d_wat = _mm_tn(dza[None], ya, f"dwa_{tag}")
    d_wbt = _mm_tn(dzb[None], yb, f"dwb_{tag}")
    cb = _combine_bwd(dya, outs, lses)
    dqs, dks, dvs = [], [], []
    for grp in range(3):
        dq, dk, dv = _dil_bwd(qkr, proj, cb[grp], cb[3 + grp], lses[grp], grp)
        dqs.append(dq)
        dks.append(dk)
        dvs.append(dv)
    d_qkv_b, dbias_tab = _na_bwd(proj, bias, dyb, yb, lse_b)
    dbias = _na_dbias(dbias_tab)
    dproj = [_rope_bwd(dqs, dks, dvs, *tables), d_qkv_b, dlog]
    d_wint, row = None, 0
    for i, p in enumerate(dproj):
        d_wint = _mm_tn(p, hn, f"dwin{i}_{tag}", into=d_wint, row0=row, rows=IN_W)
        row += p.shape[0] * p.shape[2]
    token = send(("win", "wa", "wb", "wo"), [d_wint, d_wat, d_wbt, d_wo])
    dx, dx_b, dg = _mm_nn_norm_bwd(dproj, wint, x, norm_g + token[0, 0], dxo, f"mix_{tag}")
    dbias = dbias[:, 0, :480].reshape(8, 15, 32)[:, :, :31]
    return dx, dx_b, dg, dbias


def _pack_small(norms, biases, final, loss=None):
    parts = []
    for layer in range(DEPTH):
        parts += [norms[0][layer], norms[1][layer], norms[2][layer],
                  jnp.pad(biases[layer].reshape(-1), (0, BIAS_PAD - 8 * 15 * 31))]
    parts.append(final)
    flat = jnp.concatenate([p.reshape(-1).astype(F32) for p in parts])
    if loss is not None:
        flat = jnp.concatenate([flat, loss.reshape(-1)])
    return jnp.pad(flat, (0, SMALL_ROWS * 128 - flat.shape[0])).reshape(SMALL_ROWS, 128)


def _unpack_small(packed):
    flat = packed.reshape(-1)
    norms, biases = ([], [], []), []
    pos = 0
    for _ in range(DEPTH):
        for k in range(3):
            norms[k].append(flat[pos:pos + D])
            pos += D
        biases.append(flat[pos:pos + 8 * 15 * 31].reshape(8, 15, 31))
        pos += BIAS_PAD
    final = flat[pos:pos + D]
    pos += D
    return [jnp.stack(n) for n in norms], jnp.stack(biases), final, flat[pos]


def kernel(x, ffn1_norm, ffn1_w_up, ffn1_w_down, mix_norm, w_in, na_rel_bias, w_branch_a, w_branch_b, w_out, ffn2_norm, ffn2_w_up, ffn2_w_down, final_norm, loss_target, m_ffn1_norm, m_ffn1_w_up, m_ffn1_w_down, m_mix_norm, m_w_in, m_na_rel_bias, m_w_branch_a, m_w_branch_b, m_w_out, m_ffn2_norm, m_ffn2_w_up, m_ffn2_w_down, m_final_norm, v_ffn1_norm, v_ffn1_w_up, v_ffn1_w_down, v_mix_norm, v_w_in, v_na_rel_bias, v_w_branch_a, v_w_branch_b, v_w_out, v_ffn2_norm, v_ffn2_w_up, v_ffn2_w_down, v_final_norm):
    t = x.shape[0] * x.shape[1]
    xs = x.reshape(t, D)
    tgt = loss_target.reshape(t, D)
    tables = _rope_tables()

    col_sharded = dict(up1=ffn1_w_up, win=w_in, wa=w_branch_a, wb=w_branch_b, up2=ffn2_w_up)
    row_sharded = dict(down1=ffn1_w_down, wo=w_out, down2=ffn2_w_down)
    shard = [{} for _ in range(DEPTH)]
    for layer in range(DEPTH):
        for name, arr in col_sharded.items():
            shard[layer][name] = arr[layer].T.astype(BF16)
        for name, arr in row_sharded.items():
            shard[layer][name] = arr[layer].astype(BF16)

    weights = [{} for _ in range(DEPTH)]
    travel = [(0, ("up1",)), (0, ("down1",)), (0, ("win",)), (0, ("wa", "wb", "wo")), (0, ("up2", "down2")),
              (1, ("up1", "down1")), (1, ("win",)), (1, ("wa", "wb", "wo")), (1, ("up2", "down2"))]
    group_of, chips_done, sibling_done = {}, {}, {}
    count = 0
    for i, (layer, names) in enumerate(travel):
        chips_done[i] = list(range(count, count + len(names)))
        count += len(names)
        for n in names:
            group_of[layer, n] = (i, names)
    gathered, token = _exchange_start(
        "gather", [shard[layer][n] for layer, names in travel for n in names], None, None, "w")
    zero = token[0, 0]

    biases = [_na_bias_table(na_rel_bias[layer] + zero) for layer in range(DEPTH)]

    def pass_on(i, behind):
        if i in chips_done:
            lands = _exchange_wait(gathered, behind, f"w{i}", which=chips_done.pop(i))
            sibling_done[i], _ = _exchange_start("forward", [], lands, None, f"p{i}")

    def fetcher(layer):
        def fetch(name, behind):
            if (layer, name) in group_of:
                i, names = group_of[layer, name]
                if i == 0:
                    behind = (behind, *biases)
                pass_on(i, behind)
                pass_on(i + 1, behind)
                for n, got in zip(names, _exchange_wait(sibling_done.pop(i), behind, f"p{i}")):
                    weights[layer][n] = got
                    del group_of[layer, n]
            return weights[layer][name]
        return fetch

    saved = []
    h = xs
    hn = _norm_fwd(xs, ffn1_norm[0] + zero, "first")
    for layer in range(DEPTH):
        bias = biases[layer]
        fetch = fetcher(layer)
        after_ffn2 = ffn1_norm[layer + 1] if layer + 1 < DEPTH else None
        h, hn, s1 = _ffn_forward(h, hn, fetch, ("up1", "down1"), f"f1l{layer}", mix_norm[layer])
        h, hn, s2 = _mixer_forward(h, hn, fetch, bias, tables, f"l{layer}", ffn2_norm[layer])
        h, hn, s3 = _ffn_forward(h, hn, fetch, ("up2", "down2"), f"f2l{layer}", after_ffn2)
        saved.append((s1, s2, s3, bias))
    loss_part, dh, dh_b, d_final = _loss_head(h, final_norm, tgt)

    d_norms = ([None] * DEPTH, [None] * DEPTH, [None] * DEPTH)
    d_bias = [None] * DEPTH
    sent = {}

    def sender(layer, suffix):
        def send(names, grads):
            tag = f"g{layer}{names[0]}{suffix}"
            handle, token = _exchange_start("scatter", grads, None, None, tag)
            for i, n in enumerate(names):
                sent[layer, n + suffix] = (handle, i, tag)
            return token
        return send

    for layer in reversed(range(DEPTH)):
        w = weights[layer]
        s1, s2, s3, bias = saved[layer]
        dh, dh_b, d_norms[2][layer] = _ffn_backward(
            dh, dh_b, s3, ffn2_norm[layer], w["up2"].reshape(2, F, D), w["down2"], f"f2l{layer}", sender(layer, "2"))
        dh, dh_b, d_norms[1][layer], d_bias[layer] = _mixer_backward(
            dh, dh_b, s2, mix_norm[layer], (w["win"], w["wa"], w["wb"], w["wo"]), bias, tables, f"l{layer}",
            sender(layer, ""))
        dh, dh_b, d_norms[0][layer] = _ffn_backward(
            dh, dh_b, s1, ffn1_norm[layer], w["up1"].reshape(2, F, D), w["down1"], f"f1l{layer}", sender(layer, "1"))
    grad_x = dh.reshape(x.shape)

    originals = dict(up1=(ffn1_w_up, m_ffn1_w_up, v_ffn1_w_up), down1=(ffn1_w_down, m_ffn1_w_down, v_ffn1_w_down),
                     win=(w_in, m_w_in, v_w_in), wa=(w_branch_a, m_w_branch_a, v_w_branch_a),
                     wb=(w_branch_b, m_w_branch_b, v_w_branch_b), wo=(w_out, m_w_out, v_w_out),
                     up2=(ffn2_w_up, m_ffn2_w_up, v_ffn2_w_up), down2=(ffn2_w_down, m_ffn2_w_down, v_ffn2_w_down))
    big = {}
    behind = dh
    landed = {}

    def received(layer, name):
        handle, i, tag = sent[layer, name]
        if tag not in landed:
            landed[tag] = _exchange_wait(handle, behind, tag)
        return landed[tag][i]

    for name in ("down2", "up2", "win", "wa", "wb", "wo", "down1", "up1"):
        wv, mv, vv = originals[name]
        if name in col_sharded:
            wv, mv, vv = (jnp.swapaxes(t, 1, 2) for t in (wv, mv, vv))
        big[name] = tuple(_sum_adamw(received(0, name), received(1, name), wv, mv, vv, name))
        behind = big[name][1]
        if name in col_sharded:
            big[name] = tuple(jnp.swapaxes(t, 1, 2) for t in big[name])

    small = _allreduce_small(_pack_small(d_norms, d_bias, d_final, loss_part[0, :1]), behind)
    g_norms, g_bias, g_final, loss = _unpack_small(small)
    w_small = _pack_small((ffn1_norm, mix_norm, ffn2_norm), na_rel_bias, final_norm)
    m_small = _pack_small((m_ffn1_norm, m_mix_norm, m_ffn2_norm), m_na_rel_bias, m_final_norm)
    v_small = _pack_small((v_ffn1_norm, v_mix_norm, v_ffn2_norm), v_na_rel_bias, v_final_norm)
    upd = _adamw(w_small[None], small[None], m_small[None], v_small[None], "small")
    small_out = [(g_norms, g_bias, g_final)] + [_unpack_small(u[0])[:3] for u in upd]

    outputs = [loss, grad_x]
    for kind in range(4):
        norms, bias_k, final_k = small_out[kind]
        outputs += [norms[0], big["up1"][kind], big["down1"][kind], norms[1], big["win"][kind], bias_k,
                    big["wa"][kind], big["wb"][kind], big["wo"][kind], norms[2], big["up2"][kind],
                    big["down2"][kind], final_k]
    return tuple(outputs)
```

```python
import numpy as np

import jax
import jax.numpy as jnp
from jax import lax
from jax.experimental import pallas as pl
from jax.experimental.pallas import tpu as pltpu

F32 = jnp.float32
BF16 = jnp.bfloat16
SDS = jax.ShapeDtypeStruct
BS = pl.BlockSpec
MESH = pl.DeviceIdType.MESH

D = 1024
S = 2048
F = 2816
DEPTH = 2
HEAD_DIM = 64
DILATIONS = (1, 4, 16)
HALF = 64
QKV_A = 2304
QKV_B = 1536
IN_W = 5888
N_DEV = 8
NA_ROWS = 32
GRID_W = 64
NA_KR = 8
ROPE_THETA = 10000.0
RMS_EPS = 1e-6
NEG = -1e30
SCALE = HEAD_DIM ** -0.5
ADAM_LR, ADAM_B1, ADAM_B2, ADAM_EPS, ADAM_WD, ADAM_STEP = 0.001, 0.9, 0.999, 1e-08, 0.01, 10
VMEM_LIMIT_V7X = 52 * 1024 * 1024
SMALL_ROWS = 120
BIAS_PAD = 3840
NA_FWD_ROWS = 8
NA_BWD_ROWS = 4
DIL_FWD_TILES = 8
DIL_BWD_TILES = 4


def _cp(*sem):
    return pltpu.CompilerParams(dimension_semantics=sem, vmem_limit_bytes=VMEM_LIMIT_V7X)


def _dot_nn(a, b):
    return jnp.dot(a, b, preferred_element_type=F32)


def _dot_nt(a, b):
    return lax.dot_general(a, b, (((1,), (1,)), ((), ())), preferred_element_type=F32)


def _dot_tn(a, b):
    return lax.dot_general(a, b, (((0,), (0,)), ((), ())), preferred_element_type=F32)


def _ds(start, size, stride):
    return pl.ds(start, size) if stride == 1 else pl.ds(start, size, stride=stride)


def _norm_fwd(x, g, tag):
    t = x.shape[0]
    tm = 512

    def body(x_ref, g_ref, o_ref):
        xv = x_ref[...]
        r = lax.rsqrt(jnp.mean(xv * xv, axis=-1, keepdims=True) + RMS_EPS)
        o_ref[...] = (xv * r * g_ref[...]).astype(BF16)

    return pl.pallas_call(
        body, name=f"norm_fwd_{tag}", grid=(t // tm,),
        in_specs=[BS((tm, D), lambda i: (i, 0)), BS((1, D), lambda i: (0, 0))],
        out_specs=BS((tm, D), lambda i: (i, 0)),
        out_shape=SDS((t, D), BF16), compiler_params=_cp("parallel"),
    )(x, g.reshape(1, D))


def _loss_head(x, g, tgt):
    t = x.shape[0]
    tm = 512

    def body(x_ref, g_ref, t_ref, loss_ref, dx_ref, dxb_ref, dg_ref):
        @pl.when(pl.program_id(0) == 0)
        def _():
            dg_ref[...] = jnp.zeros_like(dg_ref)
            loss_ref[...] = jnp.zeros_like(loss_ref)

        xv = x_ref[...]
        gv = g_ref[...]
        r = lax.rsqrt(jnp.mean(xv * xv, axis=-1, keepdims=True) + RMS_EPS)
        xh = xv * r
        e = xh * gv - t_ref[...]
        loss_ref[...] += 0.5 * jnp.sum(jnp.mean(e * e, axis=-1, keepdims=True), axis=0, keepdims=True)
        dy = e * (1.0 / D)
        u = dy * gv
        dx = r * (u - xh * jnp.mean(xh * u, axis=-1, keepdims=True))
        dx_ref[...] = dx
        dxb_ref[...] = dx.astype(BF16)
        dg_ref[...] += jnp.sum(dy * xh, axis=0, keepdims=True)

    row = BS((tm, D), lambda i: (i, 0))
    vec = BS((1, D), lambda i: (0, 0))
    return pl.pallas_call(
        body, name="loss_head", grid=(t // tm,),
        in_specs=[row, vec, row], out_specs=[BS((1, 128), lambda i: (0, 0)), row, row, vec],
        out_shape=[SDS((1, 128), F32), SDS((t, D), F32), SDS((t, D), BF16), SDS((1, D), F32)],
        compiler_params=_cp("arbitrary"),
    )(x, g.reshape(1, D), tgt)


def _mm_nn(a, w, tag, res=None, scale=1.0, tm=512, tn=None, next_g=None):
    c_n, t, k = a.shape
    n = w.shape[2]
    tn = n if tn is None else tn
    assert next_g is None or tn == n
    n_in = 2 + (res is not None) + (next_g is not None)

    def body(*refs):
        a_ref, w_ref = refs[0], refs[1]
        acc = _dot_nn(a_ref[0].astype(BF16), w_ref[0])
        for c in range(1, c_n):
            acc = acc + _dot_nn(a_ref[c].astype(BF16), w_ref[c])
        if scale != 1.0:
            acc = acc * scale
        if res is not None:
            acc = refs[2][...] + acc
        refs[n_in][...] = acc
        if next_g is not None:
            r = lax.rsqrt(jnp.mean(acc * acc, axis=-1, keepdims=True) + RMS_EPS)
            refs[n_in + 1][...] = (acc * r * refs[n_in - 1][...]).astype(BF16)

    in_specs = [BS((c_n, tm, k), lambda i, j: (0, i, 0)), BS((c_n, k, tn), lambda i, j: (0, 0, j))]
    args = [a, w]
    out_specs = [BS((tm, tn), lambda i, j: (i, j))]
    out_shape = [SDS((t, n), F32)]
    if res is not None:
        in_specs.append(BS((tm, tn), lambda i, j: (i, j)))
        args.append(res)
    if next_g is not None:
        in_specs.append(BS((1, n), lambda i, j: (0, 0)))
        args.append(next_g.reshape(1, n))
        out_specs.append(BS((tm, tn), lambda i, j: (i, j)))
        out_shape.append(SDS((t, n), BF16))
    got = pl.pallas_call(
        body, name=f"mm_nn_{tag}", grid=(t // tm, n // tn), in_specs=in_specs, out_specs=out_specs,
        out_shape=out_shape, compiler_params=_cp("parallel", "parallel"),
    )(*args)
    return got if next_g is not None else got[0]


def _mm_nn_norm_bwd(parts, w, x, g, dres, tag, tm=256):
    t = parts[0].shape[1]
    n_parts = len(parts)

    def body(*refs):
        w_ref, x_ref, g_ref, dr_ref, dx_ref, dxb_ref, dg_ref = refs[n_parts:]

        @pl.when(pl.program_id(0) == 0)
        def _():
            dg_ref[...] = jnp.zeros_like(dg_ref)

        dh = None
        row = 0
        for a_ref, part in zip(refs, parts):
            for c in range(part.shape[0]):
                term = _dot_nn(a_ref[c].astype(BF16), w_ref[row:row + part.shape[2], :])
                dh = term if dh is None else dh + term
                row += part.shape[2]
        xv = x_ref[...]
        r = lax.rsqrt(jnp.mean(xv * xv, axis=-1, keepdims=True) + RMS_EPS)
        xh = xv * r
        u = dh * g_ref[...]
        dx = dr_ref[...] + r * (u - xh * jnp.mean(xh * u, axis=-1, keepdims=True))
        dx_ref[...] = dx
        dxb_ref[...] = dx.astype(BF16)
        dg_ref[...] += jnp.sum(dh * xh, axis=0, keepdims=True)

    row = BS((tm, D), lambda i: (i, 0))
    vec = BS((1, D), lambda i: (0, 0))
    return pl.pallas_call(
        body, name=f"mm_nn_norm_bwd_{tag}", grid=(t // tm,),
        in_specs=[BS((p.shape[0], tm, p.shape[2]), lambda i: (0, i, 0)) for p in parts]
        + [BS(w.shape, lambda i: (0, 0)), row, vec, row],
        out_specs=[row, row, vec], out_shape=[SDS((t, D), F32), SDS((t, D), BF16), SDS((1, D), F32)],
        compiler_params=_cp("arbitrary"),
    )(*parts, w, x, g.reshape(1, D), dres)


def _mm_nt_rows(a, w, tag, tm, tn, n_total, w_row0, rope=None):
    t, k = a.shape
    assert w_row0 % tn == 0 and n_total % tn == 0
    j0 = w_row0 // tn

    def body(a_ref, w_ref, *rest):
        o_ref = rest[-1]
        o_ref[...] = _dot_nt(a_ref[...].astype(BF16), w_ref[...])
        if rope is not None:
            @pl.when(pl.program_id(0) == 0)
            def _():
                c = rest[0][...]
                sg = rest[1][...]
                first = (lax.broadcasted_iota(jnp.int32, (tm, 128), 1) % HEAD_DIM) < HEAD_DIM // 2
                for col in range(0, rope[2], 128):
                    v = o_ref[:, col:col + 128]
                    o_ref[:, col:col + 128] = v * c + _swap_halves(v, first) * sg

    in_specs = [BS((tm, k), lambda j, i: (i, 0)), BS((tn, k), lambda j, i: (j0 + j, 0))]
    args = [a, w]
    if rope is not None:
        assert rope[2] <= tn
        in_specs += [BS((tm, 128), lambda j, i: (i % (S // tm), 0))] * 2
        args += [rope[0], rope[1]]
    return pl.pallas_call(
        body, name=f"mm_nt_{tag}", grid=(n_total // tn, t // tm), in_specs=in_specs,
        out_specs=BS((tm, tn), lambda j, i: (i, j)), out_shape=SDS((t, n_total), F32),
        compiler_params=_cp("parallel", "parallel"),
    )(*args)


def _mm_tn(a, b, tag, scale=1.0, tmm=256, into=None, row0=0, rows=None):
    c_n, t, m = a.shape
    n = b.shape[1]
    tiles = m // tmm
    block0 = row0 // tmm
    assert row0 % tmm == 0 and m % tmm == 0

    def body(a_ref, b_ref, *rest):
        rest[-1][...] = (_dot_tn(a_ref[...].astype(BF16), b_ref[...].astype(BF16)) * scale).astype(BF16)

    in_specs = [BS((None, t, tmm), lambda c, mi: (c, 0, mi)), BS((t, n), lambda c, mi: (0, 0))]
    args = [a, b]
    if into is not None:
        in_specs.append(BS(memory_space=pl.ANY))
        args.append(into)
    return pl.pallas_call(
        body, name=f"mm_tn_{tag}", grid=(c_n, tiles), in_specs=in_specs,
        out_specs=BS((tmm, n), lambda c, mi: (block0 + c * tiles + mi, 0)),
        out_shape=SDS((rows or c_n * m, n) if into is None else into.shape, BF16),
        input_output_aliases={} if into is None else {2: 0},
        compiler_params=_cp("parallel", "parallel"),
    )(*args)


def _ffn_up(hn, wut, tag):
    t = hn.shape[0]
    tm, tn = 512, 1408

    def body(h_ref, w_ref, gu_ref, act_ref):
        h = h_ref[...]
        g = _dot_nt(h, w_ref[0])
        u = _dot_nt(h, w_ref[1])
        sg = jax.nn.sigmoid(g)
        silu = g * sg
        gu_ref[0] = (u * (sg + silu * (1.0 - sg))).astype(BF16)
        gu_ref[1] = silu.astype(BF16)
        act_ref[...] = (silu * u).astype(BF16)

    return pl.pallas_call(
        body, name=f"ffn_up_{tag}", grid=(F // tn, t // tm),
        in_specs=[BS((tm, D), lambda j, i: (i, 0)), BS((2, tn, D), lambda j, i: (0, j, 0))],
        out_specs=[BS((2, tm, tn), lambda j, i: (0, i, j)), BS((tm, tn), lambda j, i: (i, j))],
        out_shape=[SDS((2, t, F), BF16), SDS((t, F), BF16)],
        compiler_params=_cp("parallel", "parallel"),
    )(hn, wut)


def _ffn_dact(dxo, wd, gu, tie, tag):
    t = dxo.shape[0]
    tm, tn = 512, 1408

    def body(d_ref, w_ref, gu_ref, tie_ref, o_ref):
        dact = _dot_nt(d_ref[...] * 0.5, w_ref[...])
        o_ref[0] = (dact * gu_ref[0].astype(F32)).astype(BF16)
        o_ref[1] = (dact * gu_ref[1].astype(F32)).astype(BF16)

    return pl.pallas_call(
        body, name=f"ffn_dact_{tag}", grid=(F // tn, t // tm),
        in_specs=[BS((tm, D), lambda j, i: (i, 0)), BS((tn, D), lambda j, i: (j, 0)),
                  BS((2, tm, tn), lambda j, i: (0, i, j)), BS((8, 128), lambda j, i: (0, 0))],
        out_specs=BS((2, tm, tn), lambda j, i: (0, i, j)),
        out_shape=SDS((2, t, F), BF16), compiler_params=_cp("parallel", "parallel"),
    )(dxo, wd, gu, tie)


def _rope_tables():
    half = HEAD_DIM // 2
    inv_freq = ROPE_THETA ** (-jnp.arange(half, dtype=F32) / half)
    ang = jnp.arange(S).astype(F32)[:, None] * inv_freq[None, :]
    cos, sin = jnp.cos(ang), jnp.sin(ang)
    return jnp.concatenate([cos, cos, cos, cos], axis=1), jnp.concatenate([-sin, sin, -sin, sin], axis=1)


def _swap_halves(t, first_half):
    return jnp.where(first_half, pltpu.roll(t, 96, 1), pltpu.roll(t, 32, 1))


def _rope_bwd(dqs, dks, dvs, cos_t, sin_t):
    t = dqs[0].shape[0]
    tm = 512

    def body(*refs):
        c = refs[9][...]
        sg = refs[10][...]
        o_ref = refs[11]
        first = (lax.broadcasted_iota(jnp.int32, (tm, 128), 1) % HEAD_DIM) < HEAD_DIM // 2
        for a in range(6):
            for hp in range(2):
                v = refs[a][:, 128 * hp:128 * (hp + 1)]
                col = 128 * (2 * a + hp)
                o_ref[:, col:col + 128] = (v * c + _swap_halves(v * sg, first)).astype(BF16)
        for a in range(6, 9):
            o_ref[:, 256 * a:256 * (a + 1)] = refs[a][...].astype(BF16)

    blk = BS((tm, 256), lambda i: (i, 0))
    tab = BS((tm, 128), lambda i: (i % (S // tm), 0))
    return pl.pallas_call(
        body, name="rope_bwd", grid=(t // tm,), in_specs=[blk] * 9 + [tab, tab],
        out_specs=BS((None, tm, QKV_A), lambda i: (0, i, 0)), out_shape=SDS((1, t, QKV_A), BF16),
        compiler_params=_cp("parallel"),
    )(*dqs, *dks, *dvs, cos_t, sin_t)


def _head_masks():
    lane = lax.broadcasted_iota(jnp.int32, (1, 128), 1)
    m0 = (lane < HEAD_DIM).astype(F32)
    return m0, 1.0 - m0


def _dil_geometry(d):
    sub = S // d
    q_rows = 128
    k_rows = min(256, sub)
    return sub, q_rows, sub // q_rows, k_rows


def _dil_tile(idx, d, keys_on_rows=False):
    sub, q_rows, nb, k_rows = _dil_geometry(d)
    r = idx // nb
    n = idx % nb
    k_sub = jnp.clip(q_rows * n - HALF, 0, sub - k_rows)
    if d == 1:
        q_start = pl.multiple_of(q_rows * n, q_rows)
        k_start = pl.multiple_of(k_sub, HALF)
    else:
        q_start = q_rows * n * d + r
        k_start = k_sub * d + r
    if keys_on_rows:
        ii = lax.broadcasted_iota(jnp.int32, (k_rows, 2 * q_rows), 1) % q_rows
        jj = lax.broadcasted_iota(jnp.int32, (k_rows, 2 * q_rows), 0)
    else:
        ii = lax.broadcasted_iota(jnp.int32, (q_rows, k_rows), 0)
        jj = lax.broadcasted_iota(jnp.int32, (q_rows, k_rows), 1)
    valid = jnp.abs(jj - ii + (k_sub - q_rows * n)) <= HALF
    return q_start, k_start, valid


def _dil_specs(grp):
    qs = BS((S, 128), lambda b, hp: (b, 2 * grp + hp))
    ks = BS((S, 128), lambda b, hp: (b, 6 + 2 * grp + hp))
    vs = BS((S, 128), lambda b, hp: (b, 12 + 2 * grp + hp))
    own = BS((S, 128), lambda b, hp: (b, hp))
    return qs, ks, vs, own


def _dil_fwd(qkr, proj, grp):
    t = qkr.shape[0]
    d = DILATIONS[grp]
    _, q_rows, nb, k_rows = _dil_geometry(d)

    def body(q_ref, k_ref, v_ref, o_ref, l_ref):
        masks = _head_masks()

        def step(i0, carry):
            geo = [_dil_tile(i0 * DIL_FWD_TILES + j, d) for j in range(DIL_FWD_TILES)]
            tiles = [(j, h) for j in range(DIL_FWD_TILES) for h in range(2)]
            qs = [q_ref[_ds(g[0], q_rows, d), :] for g in geo]
            kbs = [k_ref[_ds(g[1], k_rows, d), :].astype(BF16) for g in geo]
            ss = [jnp.where(geo[j][2], _dot_nt((qs[j] * masks[h]).astype(BF16), kbs[j]) * SCALE, NEG) for j, h in tiles]
            mxs = [jnp.max(s, axis=1, keepdims=True) for s in ss]
            ps = [jnp.exp(s - mx) for s, mx in zip(ss, mxs)]
            dens = [jnp.sum(p, axis=1, keepdims=True) for p in ps]
            vs = [v_ref[_ds(g[1], k_rows, d), :] for g in geo]
            outs = [_dot_nn(p.astype(BF16), (vs[j] * masks[h]).astype(BF16)) / den
                    for p, den, (j, h) in zip(ps, dens, tiles)]
            for j, g in enumerate(geo):
                o_ref[_ds(g[0], q_rows, d), :] = outs[2 * j] + outs[2 * j + 1]
                l_ref[_ds(g[0], q_rows, d), :] = (
                    (mxs[2 * j] + jnp.log(dens[2 * j])) * masks[0] + (mxs[2 * j + 1] + jnp.log(dens[2 * j + 1])) * masks[1])
            return carry

        lax.fori_loop(0, d * nb // DIL_FWD_TILES, step, 0)

    qs, ks, vs, own = _dil_specs(grp)
    return pl.pallas_call(
        body, name=f"dil_fwd_{grp}", grid=(t // S, 2), in_specs=[qs, ks, vs], out_specs=[own, own],
        out_shape=[SDS((t, 256), F32), SDS((t, 256), F32)], compiler_params=_cp("parallel", "parallel"),
    )(qkr, qkr, proj)


def _dil_bwd(qkr, proj, do, dlp, lse, grp):
    t = qkr.shape[0]
    d = DILATIONS[grp]
    _, q_rows, nb, k_rows = _dil_geometry(d)

    def body(q_ref, k_ref, v_ref, do_ref, dl_ref, l_ref, dq_ref, dk_ref, dv_ref):
        masks = _head_masks()
        dk_ref[...] = jnp.zeros_like(dk_ref)
        dv_ref[...] = jnp.zeros_like(dv_ref)

        def as_row(x2):
            xt = x2.T
            return jnp.concatenate([xt[0:1], xt[HEAD_DIM:HEAD_DIM + 1]], axis=1)

        def step(i0, carry):
            geo = [_dil_tile(i0 * DIL_BWD_TILES + j, d, keys_on_rows=True) for j in range(DIL_BWD_TILES)]
            q_ds = [_ds(g[0], q_rows, d) for g in geo]
            k_ds = [_ds(g[1], k_rows, d) for g in geo]
            qbs = [_both_heads(q_ref[r, :], masks).astype(BF16) for r in q_ds]
            kbs = [k_ref[r, :].astype(BF16) for r in k_ds]
            vbs = [v_ref[r, :].astype(BF16) for r in k_ds]
            dobs = [_both_heads(do_ref[r, :], masks).astype(BF16) for r in q_ds]
            l_rows = [as_row(l_ref[r, :]) for r in q_ds]
            dl_rows = [as_row(dl_ref[r, :]) for r in q_ds]
            ss = [jnp.where(g[2], _dot_nt(kb, qb) * SCALE, NEG) for g, kb, qb in zip(geo, kbs, qbs)]
            ps = [jnp.exp(s - lr) for s, lr in zip(ss, l_rows)]
            dps = [_dot_nt(vb, dob) for vb, dob in zip(vbs, dobs)]
            dss = [(p * (dp - dr)).astype(BF16) for p, dp, dr in zip(ps, dps, dl_rows)]
            dks = [_dot_nn(ds, qb) for ds, qb in zip(dss, qbs)]
            dvs = [_dot_nn(p.astype(BF16), dob) for p, dob in zip(ps, dobs)]
            dqs = [_own_heads(_dot_tn(ds, kb), masks) for ds, kb in zip(dss, kbs)]
            for j in range(DIL_BWD_TILES):
                dq_ref[q_ds[j], :] = dqs[j] * SCALE
                dk_ref[k_ds[j], :] += dks[j] * SCALE
                dv_ref[k_ds[j], :] += dvs[j]
            return carry

        lax.fori_loop(0, d * nb // DIL_BWD_TILES, step, 0)

    qs, ks, vs, own = _dil_specs(grp)
    return pl.pallas_call(
        body, name=f"dil_bwd_{grp}", grid=(t // S, 2), in_specs=[qs, ks, vs, own, own, own],
        out_specs=[own, own, own], out_shape=[SDS((t, 256), F32)] * 3,
        compiler_params=_cp("parallel", "parallel"),
    )(qkr, qkr, proj, do, dlp, lse)


def _mix_weights(l0, l1, l2):
    mx = jnp.maximum(jnp.maximum(l0, l1), l2)
    e0, e1, e2 = jnp.exp(l0 - mx), jnp.exp(l1 - mx), jnp.exp(l2 - mx)
    den = e0 + e1 + e2
    return e0 / den, e1 / den, e2 / den


def _combine_fwd(outs, lses):
    t = outs[0].shape[0]
    tm = 512

    def body(o0, o1, o2, l0, l1, l2, y_ref):
        w0, w1, w2 = _mix_weights(l0[...], l1[...], l2[...])
        y_ref[...] = w0 * o0[...] + w1 * o1[...] + w2 * o2[...]

    blk = BS((tm, 256), lambda i: (i, 0))
    return pl.pallas_call(
        body, name="combine_fwd", grid=(t // tm,), in_specs=[blk] * 6, out_specs=blk,
        out_shape=SDS((t, 256), F32), compiler_params=_cp("parallel"),
    )(*outs, *lses)


def _head_sum(x):
    a = lax.broadcasted_iota(jnp.int32, (256, 256), 0) // HEAD_DIM
    b = lax.broadcasted_iota(jnp.int32, (256, 256), 1) // HEAD_DIM
    ones = (a == b).astype(BF16)
    hi = x.astype(BF16)
    lo = (x - hi.astype(F32)).astype(BF16)
    return _dot_nn(hi, ones) + _dot_nn(lo, ones)


def _combine_bwd(dya, outs, lses):
    t = dya.shape[0]
    tm = 512

    def body(dy_ref, o0, o1, o2, l0, l1, l2, d0, d1, d2, e0, e1, e2):
        ws = _mix_weights(l0[...], l1[...], l2[...])
        dy = dy_ref[...]
        ya = ws[0] * o0[...] + ws[1] * o1[...] + ws[2] * o2[...]
        hs = _head_sum(dy * ya)
        for w, d_ref, e_ref in zip(ws, (d0, d1, d2), (e0, e1, e2)):
            d_ref[...] = w * dy
            e_ref[...] = w * hs

    blk = BS((tm, 256), lambda i: (i, 0))
    return pl.pallas_call(
        body, name="combine_bwd", grid=(t // tm,), in_specs=[blk] * 7, out_specs=[blk] * 6,
        out_shape=[SDS((t, 256), F32)] * 6, compiler_params=_cp("parallel"),
    )(dya, *outs, *lses)


def _na_bias_table(rel_bias):
    kw = NA_KR * GRID_W
    rev = jnp.pad(rel_bias.astype(F32)[:, :, ::-1], ((0, 0), (0, 0), (0, 128 - 31)))

    def body(r_ref, o_ref):
        lane = lax.broadcasted_iota(jnp.int32, (GRID_W, 128), 1)
        j = lax.broadcasted_iota(jnp.int32, (GRID_W, 128), 0)
        q = lane % GRID_W
        win_lo = jnp.clip(q - 8, 0, GRID_W - 16)
        valid = (j >= win_lo) & (j < win_lo + 16)
        for cls in range(NA_KR):
            for k in range(NA_KR):
                tiles = []
                for h in range(2):
                    row = jnp.broadcast_to(r_ref[h, cls + k:cls + k + 1, :], (GRID_W, 128))
                    tiles.append(pltpu.roll(row, (128 - 15 + GRID_W * h) % 128, 1, stride=1, stride_axis=0))
                o_ref[cls, GRID_W * k:GRID_W * (k + 1), :] = jnp.where(
                    valid, jnp.where(lane < GRID_W, tiles[0], tiles[1]), NEG)

    return pl.pallas_call(
        body, name="na_bias_table", grid=(4,),
        in_specs=[BS((2, 2 * NA_KR - 1, 128), lambda hp: (hp, 0, 0))],
        out_specs=BS((None, NA_KR, kw, 128), lambda hp: (hp, 0, 0, 0)),
        out_shape=SDS((4, NA_KR, kw, 128), F32), compiler_params=_cp("parallel"),
    )(rev)


def _na_row(i):
    lo = jnp.clip(i - NA_KR // 2, 0, NA_ROWS - NA_KR)
    return pl.multiple_of(GRID_W * i, GRID_W), pl.multiple_of(GRID_W * lo, GRID_W), lo - i + NA_KR - 1


def _both_heads(x, masks):
    return jnp.concatenate([x * masks[0], x * masks[1]], axis=0)


def _own_heads(r, masks):
    half = r.shape[0] // 2
    return r[:half] * masks[0] + r[half:] * masks[1]


def _na_fwd(proj, bias):
    t = proj.shape[0]
    kw = NA_KR * GRID_W

    def body(q_ref, k_ref, v_ref, b_ref, o_ref, l_ref):
        masks = _head_masks()

        def step(i0, carry):
            idx = [i0 * NA_FWD_ROWS + j for j in range(NA_FWD_ROWS)]
            rows = [_na_row(i) for i in idx]
            qbs = [_both_heads(q_ref[pl.ds(r[0], GRID_W), :], masks).astype(BF16) for r in rows]
            kbs = [k_ref[pl.ds(r[1], kw), :].astype(BF16) for r in rows]
            ss = [_dot_nt(kb, qb) * SCALE + b_ref[r[2]] for kb, qb, r in zip(kbs, qbs, rows)]
            mxs = [jnp.max(s, axis=0, keepdims=True) for s in ss]
            ps = [jnp.exp(s - mx) for s, mx in zip(ss, mxs)]
            dens = [jnp.sum(p, axis=0, keepdims=True) for p in ps]
            pbs = [(p / den).astype(BF16) for p, den in zip(ps, dens)]
            vbs = [v_ref[pl.ds(r[1], kw), :].astype(BF16) for r in rows]
            outs = [_own_heads(_dot_tn(pb, vb), masks) for pb, vb in zip(pbs, vbs)]
            for j, r in enumerate(rows):
                o_ref[pl.ds(r[0], GRID_W), :] = outs[j]
                l_ref[pl.ds(idx[j], 1), :] = mxs[j] + jnp.log(dens[j])
            return carry

        lax.fori_loop(0, NA_ROWS // NA_FWD_ROWS, step, 0)

    c0 = QKV_A // 128
    return pl.pallas_call(
        body, name="na_fwd", grid=(t // S, 4),
        in_specs=[BS((S, 128), lambda b, hp: (b, c0 + hp)), BS((S, 128), lambda b, hp: (b, c0 + 4 + hp)),
                  BS((S, 128), lambda b, hp: (b, c0 + 8 + hp)),
                  BS((None, NA_KR, kw, 128), lambda b, hp: (hp, 0, 0, 0))],
        out_specs=[BS((S, 128), lambda b, hp: (b, hp)), BS((None, None, NA_ROWS, 128), lambda b, hp: (b, hp, 0, 0))],
        out_shape=[SDS((t, 512), F32), SDS((t // S, 4, NA_ROWS, 128), F32)],
        compiler_params=_cp("parallel", "parallel"),
    )(proj, proj, proj, bias)


def _na_bwd(proj, bias, dyb, yb, lse):
    t = proj.shape[0]
    kw = NA_KR * GRID_W

    def body(q_ref, k_ref, v_ref, b_ref, do_ref, o_ref, l_ref, d_ref, db_ref):
        masks = _head_masks()
        ones = jnp.ones((8, 128), BF16)

        @pl.when(pl.program_id(1) == 0)
        def _():
            db_ref[...] = jnp.zeros_like(db_ref)

        d_ref[1:3] = jnp.zeros((2, S, 128), F32)

        def row_sums(x):
            hi = x.astype(BF16)
            lo = (x - hi.astype(F32)).astype(BF16)
            return (_dot_nt(ones, hi) + _dot_nt(ones, lo))[0:1]

        def step(i0, carry):
            idx = [i0 * NA_BWD_ROWS + j for j in range(NA_BWD_ROWS)]
            rows = [_na_row(i) for i in idx]
            q_ds = [pl.ds(r[0], GRID_W) for r in rows]
            k_ds = [pl.ds(r[1], kw) for r in rows]
            qbs = [_both_heads(q_ref[r, :], masks).astype(BF16) for r in q_ds]
            kbs = [k_ref[r, :].astype(BF16) for r in k_ds]
            vbs = [v_ref[r, :].astype(BF16) for r in k_ds]
            dos = [do_ref[r, :] for r in q_ds]
            dobs = [_both_heads(do, masks).astype(BF16) for do in dos]
            deltas = [row_sums(_both_heads(do * o_ref[r, :], masks)) for do, r in zip(dos, q_ds)]
            ss = [_dot_nt(kb, qb) * SCALE + b_ref[r[2]] for kb, qb, r in zip(kbs, qbs, rows)]
            ps = [jnp.exp(s - l_ref[pl.ds(i, 1), :]) for s, i in zip(ss, idx)]
            dps = [_dot_nt(vb, dob) for vb, dob in zip(vbs, dobs)]
            dss = [p * (dp - delta) for p, dp, delta in zip(ps, dps, deltas)]
            for ds, r in zip(dss, rows):
                db_ref[r[2]] += ds
            dsbs = [ds.astype(BF16) for ds in dss]
            dks = [_dot_nn(dsb, qb) for dsb, qb in zip(dsbs, qbs)]
            dvs = [_dot_nn(p.astype(BF16), dob) for p, dob in zip(ps, dobs)]
            dqs = [_own_heads(_dot_tn(dsb, kb), masks) for dsb, kb in zip(dsbs, kbs)]
            for j in range(NA_BWD_ROWS):
                d_ref[0, q_ds[j], :] = dqs[j] * SCALE
                d_ref[1, k_ds[j], :] += dks[j] * SCALE
                d_ref[2, k_ds[j], :] += dvs[j]
            return carry

        lax.fori_loop(0, NA_ROWS // NA_BWD_ROWS, step, 0)

    c0 = QKV_A // 128
    own = BS((S, 128), lambda hp, b: (b, hp))
    tab = BS((None, NA_KR, kw, 128), lambda hp, b: (hp, 0, 0, 0))
    return pl.pallas_call(
        body, name="na_bwd", grid=(4, t // S),
        in_specs=[BS((S, 128), lambda hp, b: (b, c0 + hp)), BS((S, 128), lambda hp, b: (b, c0 + 4 + hp)),
                  BS((S, 128), lambda hp, b: (b, c0 + 8 + hp)), tab, own, own,
                  BS((None, None, NA_ROWS, 128), lambda hp, b: (b, hp, 0, 0))],
        out_specs=[BS((3, S, 128), lambda hp, b: (0, b, hp)), tab],
        out_shape=[SDS((3, t, 512), F32), SDS((4, NA_KR, kw, 128), F32)],
        compiler_params=_cp("parallel", "arbitrary"),
    )(proj, proj, proj, bias, dyb, yb, lse)


def _na_dbias_lane_map():
    kw = NA_KR * GRID_W
    lane = np.arange(kw)
    blk, m = lane // GRID_W, lane % GRID_W
    target = np.full(kw, -1)
    target[m < 16] = (blk * 32 + 15 + m)[m < 16]
    target[m >= 49] = (((blk + 1) % NA_KR) * 32 + m - 49)[m >= 49]
    return jnp.asarray(target[:, None] == np.arange(kw)[None, :], BF16)


def _na_dbias(db):
    kw = NA_KR * GRID_W

    def body(x_ref, map_ref, o_ref, z_ref):
        for cls in range(NA_KR):
            xt = x_ref[cls].T
            for h in range(2):
                xv = xt[GRID_W * h:GRID_W * (h + 1)]
                y = xv[0:8]
                for g in range(1, GRID_W // 8):
                    y = y + pltpu.roll(xv[8 * g:8 * g + 8], kw - 8 * g, 1)
                d = y[0:1]
                for s in range(1, 8):
                    d = d + pltpu.roll(y[s:s + 1], kw - s, 1)
                z_ref[h, cls:cls + 1, :] = d
        for h in range(2):
            z = z_ref[h]
            hi = z.astype(BF16)
            lo = (z - hi.astype(F32)).astype(BF16)
            e = _dot_nn(hi, map_ref[...]) + _dot_nn(lo, map_ref[...])
            out = e[0:1]
            for cls in range(1, NA_KR):
                out = out + pltpu.roll(e[cls:cls + 1], 32 * cls, 1)
            o_ref[h] = jnp.broadcast_to(out, (8, kw))

    return pl.pallas_call(
        body, name="na_dbias", grid=(4,),
        in_specs=[BS((None, NA_KR, kw, 128), lambda hp: (hp, 0, 0, 0)), BS((kw, kw), lambda hp: (0, 0))],
        out_specs=BS((2, 8, kw), lambda hp: (hp, 0, 0)), out_shape=SDS((8, 8, kw), F32),
        scratch_shapes=[pltpu.VMEM((2, 8, kw), F32)], compiler_params=_cp("parallel"),
    )(db, _na_dbias_lane_map())


def _merge_fwd(ya, yb, proj, wat, wbt):
    t = ya.shape[0]
    tm, tn = 1024, 256
    ca = (QKV_A + QKV_B) // tn
    cb = ca + D // tn

    def body(ya_ref, yb_ref, la_ref, lb_ref, wa_ref, wb_ref, m_ref, za_ref, zb_ref):
        za = _dot_nt(ya_ref[...].astype(BF16), wa_ref[...])
        zb = _dot_nt(yb_ref[...].astype(BF16), wb_ref[...])
        m_ref[...] = (jax.nn.sigmoid(la_ref[...]) * za + jax.nn.sigmoid(lb_ref[...]) * zb).astype(BF16)
        za_ref[...] = za.astype(BF16)
        zb_ref[...] = zb.astype(BF16)

    out = BS((tm, tn), lambda i, j: (i, j))
    return pl.pallas_call(
        body, name="merge_fwd", grid=(t // tm, D // tn),
        in_specs=[BS((tm, 256), lambda i, j: (i, 0)), BS((tm, 512), lambda i, j: (i, 0)),
                  BS((tm, tn), lambda i, j: (i, ca + j)), BS((tm, tn), lambda i, j: (i, cb + j)),
                  BS((tn, 256), lambda i, j: (j, 0)), BS((tn, 512), lambda i, j: (j, 0))],
        out_specs=[out, out, out], out_shape=[SDS((t, D), BF16)] * 3,
        compiler_params=_cp("parallel", "parallel"),
    )(ya, yb, proj, proj, wat, wbt)


def _merge_bwd(dxo, wo, za, zb, proj):
    t = dxo.shape[0]
    tm, tn = 1024, 256
    ca = (QKV_A + QKV_B) // tn
    cb = ca + D // tn

    def body(d_ref, w_ref, za_ref, zb_ref, la_ref, lb_ref, dza_ref, dzb_ref, dl_ref):
        dmv = _dot_nt(d_ref[...], w_ref[...])
        ga = jax.nn.sigmoid(la_ref[...])
        gb = jax.nn.sigmoid(lb_ref[...])
        dza_ref[...] = (dmv * ga).astype(BF16)
        dzb_ref[...] = (dmv * gb).astype(BF16)
        dl_ref[0] = (dmv * za_ref[...].astype(F32) * ga * (1.0 - ga)).astype(BF16)
        dl_ref[1] = (dmv * zb_ref[...].astype(F32) * gb * (1.0 - gb)).astype(BF16)

    blk = BS((tm, tn), lambda i, j: (i, j))
    return pl.pallas_call(
        body, name="merge_bwd", grid=(t // tm, D // tn),
        in_specs=[BS((tm, D), lambda i, j: (i, 0)), BS((tn, D), lambda i, j: (j, 0)), blk, blk,
                  BS((tm, tn), lambda i, j: (i, ca + j)), BS((tm, tn), lambda i, j: (i, cb + j))],
        out_specs=[blk, blk, BS((2, tm, tn), lambda i, j: (0, i, j))],
        out_shape=[SDS((t, D), BF16), SDS((t, D), BF16), SDS((2, t, D), BF16)],
        compiler_params=_cp("parallel", "parallel"),
    )(dxo, wo, za, zb, proj, proj)


def _adamw_update(w, g, m, v):
    mn = ADAM_B1 * m + (1.0 - ADAM_B1) * g
    vn = ADAM_B2 * v + (1.0 - ADAM_B2) * (g * g)
    m_hat = mn / (1.0 - ADAM_B1 ** ADAM_STEP)
    v_hat = vn / (1.0 - ADAM_B2 ** ADAM_STEP)
    return -ADAM_LR * (m_hat / (jnp.sqrt(v_hat) + ADAM_EPS) + ADAM_WD * w), mn, vn


def _sum_adamw(recv0, recv1, w, m, v, tag):
    _, r, c = recv0.shape
    tr = max(rows for rows in range(16, r + 1, 16) if r % rows == 0 and rows * c <= 192 * 1024)

    def body(a_ref, b_ref, w_ref, m_ref, v_ref, g_ref, d_ref, mo_ref, vo_ref):
        for layer, ref in enumerate((a_ref, b_ref)):
            g = ref[0].astype(F32)
            for s in range(1, N_DEV):
                g = g + ref[s].astype(F32)
            g_ref[layer] = g
            d_ref[layer], mo_ref[layer], vo_ref[layer] = _adamw_update(w_ref[layer], g, m_ref[layer], v_ref[layer])

    slots = BS((N_DEV, tr, c), lambda i: (0, i, 0))
    blk = BS((2, tr, c), lambda i: (0, i, 0))
    return pl.pallas_call(
        body, name=f"sum_adamw_{tag}", grid=(r // tr,), in_specs=[slots, slots, blk, blk, blk],
        out_specs=[blk] * 4, out_shape=[SDS((2, r, c), F32)] * 4, compiler_params=_cp("parallel"),
    )(recv0, recv1, w, m, v)


def _adamw(w, g, m, v, tag):
    layers, r, c = w.shape
    tr = next(r // k for k in (1, 2, 4, 8) if r // k <= 384 and r % (8 * k) == 0)

    def body(w_ref, g_ref, m_ref, v_ref, d_ref, mo_ref, vo_ref):
        d_ref[...], mo_ref[...], vo_ref[...] = _adamw_update(w_ref[...], g_ref[...], m_ref[...], v_ref[...])

    blk = BS((None, tr, c), lambda l, i: (l, i, 0))
    return pl.pallas_call(
        body, name=f"adamw_{tag}", grid=(layers, r // tr), in_specs=[blk] * 4, out_specs=[blk] * 3,
        out_shape=[SDS((layers, r, c), F32)] * 3, compiler_params=_cp("parallel", "parallel"),
    )(w, g, m, v)


def _place():
    return lax.axis_index("x"), lax.axis_index("y"), lax.axis_index("c")


def _flip(coord, bit):
    return 1 - coord if bit else coord


def _peers(x, y, c):
    peers = []
    for mask in range(1, N_DEV):
        p = (_flip(x, mask & 4), _flip(y, mask & 2), _flip(c, mask & 1))
        peers.append((p, 4 * p[0] + 2 * p[1] + p[2]))
    return peers


def _copy_plan(mode, src, land, x, y, c):
    me = 4 * x + 2 * y + c

    def device(mask):
        p = (_flip(x, mask & 4), _flip(y, mask & 2), _flip(c, mask & 1))
        return p, 4 * p[0] + 2 * p[1] + p[2]

    if mode == "scatter":
        r = land.shape[1]
        return [(p, src.at[pl.ds(i * r, r), :], land.at[me], land.at[i])
                for p, i in map(device, (1, 2, 3, 4, 5, 6, 7, 0))]
    r = land.shape[0] // N_DEV

    def rows(i):
        return land.at[pl.ds(i * r, r), :]

    if mode == "gather":
        return [(p, src, rows(me), rows(i)) for p, i in map(device, (1, 4, 2, 6, 0))]
    sibling = device(1)[0]
    return [(sibling, rows(device(m)[1]), rows(device(m)[1]), rows(device(m | 1)[1])) for m in (4, 2, 6)]


COPIES = dict(scatter=8, gather=5, forward=3)
HBM_SPEC = BS(memory_space=pltpu.HBM)
SEM_SPEC = BS(memory_space=pltpu.SEMAPHORE)
DATAFLOW = pltpu.SideEffectType.DATAFLOW_SIDE_EFFECTING


def _fresh(shape, dtype, tag):
    def body(o_ref):
        del o_ref

    return pl.pallas_call(body, name=f"fresh_{tag}", out_specs=BS(memory_space=pl.ANY), out_shape=SDS(shape, dtype))()


def _exchange_start(mode, srcs, lands, after, tag):
    if lands is None and mode == "gather":
        lands = [_fresh((N_DEV * s.shape[0], s.shape[1]), s.dtype, f"{tag}_{a}") for a, s in enumerate(srcs)]
    elif lands is None:
        lands = [_fresh((N_DEV, s.shape[0] // N_DEV, s.shape[1]), s.dtype, f"{tag}_{a}") for a, s in enumerate(srcs)]
    n, n_src, n_cp = len(lands), len(srcs), COPIES[mode]
    behind = [] if after is None else [after]

    def body(*refs):
        src_refs, land_refs = refs[:n_src], refs[n_src:n_src + n]
        send_sems, recv_sems = refs[n_src + n + len(behind)], refs[n_src + n + len(behind) + 1]
        token = refs[-1]
        for a in range(n):
            plan = _copy_plan(mode, src_refs[a] if n_src else None, land_refs[a], *_place())
            for k, (p, out, there, _) in enumerate(plan):
                pltpu.make_async_remote_copy(
                    src_ref=out, dst_ref=there, send_sem=send_sems.at[n_cp * a + k],
                    recv_sem=recv_sems.at[n_cp * a + k], device_id=p, device_id_type=MESH).start()
        token[...] = jnp.zeros_like(token)

    both = [*srcs, *lands]
    res = pl.pallas_call(
        body, name=f"{mode}_start_{tag}",
        out_shape=(pltpu.SemaphoreType.DMA((n_cp * n,)), pltpu.SemaphoreType.DMA((n_cp * n,)),
                   *[pltpu.HBM(v.shape, v.dtype) for v in both], SDS((8, 128), F32)),
        in_specs=[HBM_SPEC] * len(both) + [BS(memory_space=pl.ANY)] * len(behind),
        out_specs=(SEM_SPEC, SEM_SPEC, *[HBM_SPEC] * len(both), BS(memory_space=pltpu.VMEM)),
        input_output_aliases={i: 2 + i for i in range(len(both))},
        compiler_params=pltpu.CompilerParams(has_side_effects=DATAFLOW),
    )(*[pltpu.with_memory_space_constraint(v, pltpu.HBM) for v in both], *behind)
    return (mode, res[0], res[1], res[2:2 + n_src], res[2 + n_src:2 + n_src + n]), res[-1]


def _exchange_wait(handle, after, tag, which=None):
    mode, send_sems, recv_sems, srcs, lands = handle
    which = list(range(len(lands))) if which is None else list(which)
    n_cp = COPIES[mode]
    lands = [lands[a] for a in which]
    srcs = [srcs[a] for a in which] if srcs else []
    n, n_src = len(lands), len(srcs)
    afters = list(after) if isinstance(after, (tuple, list)) else [after]

    def body(*refs):
        src_refs, land_refs = refs[:n_src], refs[n_src:n_src + n]
        send_ref, recv_ref = refs[n_src + n], refs[n_src + n + 1]
        for i, a in enumerate(which):
            plan = _copy_plan(mode, src_refs[i] if n_src else None, land_refs[i], *_place())
            for k, (p, out, _, here) in enumerate(plan):
                cp = pltpu.make_async_remote_copy(
                    src_ref=out, dst_ref=here, send_sem=send_ref.at[n_cp * a + k], recv_sem=recv_ref.at[n_cp * a + k],
                    device_id=p, device_id_type=MESH)
                cp.wait_send()
                cp.wait_recv()

    both = [*srcs, *lands]
    res = pl.pallas_call(
        body, name=f"{mode}_wait_{tag}", out_shape=tuple(pltpu.HBM(v.shape, v.dtype) for v in both),
        in_specs=[HBM_SPEC] * len(both) + [SEM_SPEC, SEM_SPEC] + [BS(memory_space=pl.ANY)] * len(afters),
        out_specs=tuple([HBM_SPEC] * len(both)),
        input_output_aliases={i: i for i in range(len(both))},
        compiler_params=pltpu.CompilerParams(has_side_effects=DATAFLOW),
    )(*both, send_sems, recv_sems, *afters)
    return list(res[n_src:])


def _allreduce_small(vec, behind):
    rows = vec.shape[0]

    def body(x_ref, behind_ref, o_ref, buf_ref, send_sems, recv_sems):
        x, y, c = _place()
        me = 4 * x + 2 * y + c
        buf_ref[me] = x_ref[...]
        peers = _peers(x, y, c)

        def copy(k, slot):
            return pltpu.make_async_remote_copy(
                src_ref=x_ref, dst_ref=buf_ref.at[slot], send_sem=send_sems.at[k], recv_sem=recv_sems.at[k],
                device_id=peers[k][0], device_id_type=MESH)

        sends = [copy(k, me) for k in range(N_DEV - 1)]
        for cp in sends:
            cp.start()
        for k in range(N_DEV - 1):
            copy(k, peers[k][1]).wait_recv()
        for cp in sends:
            cp.wait_send()
        acc = buf_ref[0]
        for s in range(1, N_DEV):
            acc = acc + buf_ref[s]
        o_ref[...] = acc

    vmem = BS(memory_space=pltpu.VMEM)
    return pl.pallas_call(
        body, name="allreduce_small", in_specs=[vmem, BS(memory_space=pl.ANY)], out_specs=vmem,
        out_shape=SDS((rows, 128), F32),
        scratch_shapes=[pltpu.VMEM((N_DEV, rows, 128), F32), pltpu.SemaphoreType.DMA((7,)),
                        pltpu.SemaphoreType.DMA((7,))],
        compiler_params=pltpu.CompilerParams(has_side_effects=True),
    )(vec, behind)


def _ffn_forward(x, hn, fetch, names, tag, next_g):
    gu, act = _ffn_up(hn, fetch(names[0], hn).reshape(2, F, D), tag)
    got = _mm_nn(act[None], fetch(names[1], act)[None], f"down_{tag}", res=x, scale=0.5, next_g=next_g)
    out, hn_next = got if next_g is not None else (got, None)
    return out, hn_next, (x, hn, gu, act)


def _ffn_backward(dxo, dxo_b, saved, norm_g, wut, wd, tag, send):
    x, hn, gu, act = saved
    d_wd = _mm_tn(act[None], dxo_b, f"dwd_{tag}", scale=0.5)
    du = _ffn_dact(dxo_b, wd, gu, send(("down",), [d_wd]), tag)
    d_wut = _mm_tn(du, hn, f"dwu_{tag}")
    token = send(("up",), [d_wut])
    return _mm_nn_norm_bwd([du], wut.reshape(2 * F, D), x, norm_g + token[0, 0], dxo, tag)


def _mixer_forward(x, hn, fetch, bias, tables, tag, next_g):
    proj = _mm_nt_rows(hn, fetch("win", hn), f"proj_{tag}", 512, IN_W // 2, IN_W, 0, rope=(*tables, 2 * QKV_A // 3))
    qkr = proj
    outs, lses = [], []
    for grp in range(3):
        o, l = _dil_fwd(qkr, proj, grp)
        outs.append(o)
        lses.append(l)
    ya = _combine_fwd(outs, lses)
    yb, lse_b = _na_fwd(proj, bias)
    merged, za, zb = _merge_fwd(ya, yb, proj, fetch("wa", yb), fetch("wb", yb))
    out, hn_next = _mm_nn(merged[None], fetch("wo", merged)[None], f"out_{tag}", res=x, next_g=next_g)
    return out, hn_next, (x, hn, proj, qkr, outs, lses, ya, yb, lse_b, merged, za, zb)


def _mixer_backward(dxo, dxo_b, saved, norm_g, w, bias, tables, tag, send):
    wint, wat, wbt, wo = w
    x, hn, proj, qkr, outs, lses, ya, yb, lse_b, merged, za, zb = saved
    d_wo = _mm_tn(merged[None], dxo_b, f"dwo_{tag}")
    dza, dzb, dlog = _merge_bwd(dxo_b, wo, za, zb, proj)
    dya = _mm_nn(dza[None], wat[None], f"dya_{tag}")
    dyb = _mm_nn(dzb[None], wbt[None], f"dyb_{tag}")
    d_wat = _mm_tn(dza[None], ya, f"dwa_{tag}")
    d_wbt = _mm_tn(dzb[None], yb, f"dwb_{tag}")
    cb = _combine_bwd(dya, outs, lses)
    dqs, dks, dvs = [], [], []
    for grp in range(3):
        dq, dk, dv = _dil_bwd(qkr, proj, cb[grp], cb[3 + grp], lses[grp], grp)
        dqs.append(dq)
        dks.append(dk)
        dvs.append(dv)
    d_qkv_b, dbias_tab = _na_bwd(proj, bias, dyb, yb, lse_b)
    dbias = _na_dbias(dbias_tab)
    dproj = [_rope_bwd(dqs, dks, dvs, *tables), d_qkv_b, dlog]
    d_wint, row = None, 0
    for i, p in enumerate(dproj):
        d_wint = _mm_tn(p, hn, f"dwin{i}_{tag}", into=d_wint, row0=row, rows=IN_W)
        row += p.shape[0] * p.shape[2]
    token = send(("win", "wa", "wb", "wo"), [d_wint, d_wat, d_wbt, d_wo])
    dx, dx_b, dg = _mm_nn_norm_bwd(dproj, wint, x, norm_g + token[0, 0], dxo, f"mix_{tag}")
    dbias = dbias[:, 0, :480].reshape(8, 15, 32)[:, :, :31]
    return dx, dx_b, dg, dbias


def _pack_small(norms, biases, final, loss=None):
    parts = []
    for layer in range(DEPTH):
        parts += [norms[0][layer], norms[1][layer], norms[2][layer],
                  jnp.pad(biases[layer].reshape(-1), (0, BIAS_PAD - 8 * 15 * 31))]
    parts.append(final)
    flat = jnp.concatenate([p.reshape(-1).astype(F32) for p in parts])
    if loss is not None:
        flat = jnp.concatenate([flat, loss.reshape(-1)])
    return jnp.pad(flat, (0, SMALL_ROWS * 128 - flat.shape[0])).reshape(SMALL_ROWS, 128)


def _unpack_small(packed):
    flat = packed.reshape(-1)
    norms, biases = ([], [], []), []
    pos = 0
    for _ in range(DEPTH):
        for k in range(3):
            norms[k].append(flat[pos:pos + D])
            pos += D
        biases.append(flat[pos:pos + 8 * 15 * 31].reshape(8, 15, 31))
        pos += BIAS_PAD
    final = flat[pos:pos + D]
    pos += D
    return [jnp.stack(n) for n in norms], jnp.stack(biases), final, flat[pos]


def kernel(x, ffn1_norm, ffn1_w_up, ffn1_w_down, mix_norm, w_in, na_rel_bias, w_branch_a, w_branch_b, w_out, ffn2_norm, ffn2_w_up, ffn2_w_down, final_norm, loss_target, m_ffn1_norm, m_ffn1_w_up, m_ffn1_w_down, m_mix_norm, m_w_in, m_na_rel_bias, m_w_branch_a, m_w_branch_b, m_w_out, m_ffn2_norm, m_ffn2_w_up, m_ffn2_w_down, m_final_norm, v_ffn1_norm, v_ffn1_w_up, v_ffn1_w_down, v_mix_norm, v_w_in, v_na_rel_bias, v_w_branch_a, v_w_branch_b, v_w_out, v_ffn2_norm, v_ffn2_w_up, v_ffn2_w_down, v_final_norm):
    t = x.shape[0] * x.shape[1]
    xs = x.reshape(t, D)
    tgt = loss_target.reshape(t, D)
    tables = _rope_tables()

    col_sharded = dict(up1=ffn1_w_up, win=w_in, wa=w_branch_a, wb=w_branch_b, up2=ffn2_w_up)
    row_sharded = dict(down1=ffn1_w_down, wo=w_out, down2=ffn2_w_down)
    shard = [{} for _ in range(DEPTH)]
    for layer in range(DEPTH):
        for name, arr in col_sharded.items():
            shard[layer][name] = arr[layer].T.astype(BF16)
        for name, arr in row_sharded.items():
            shard[layer][name] = arr[layer].astype(BF16)

    weights = [{} for _ in range(DEPTH)]
    travel = [(0, ("up1",)), (0, ("down1",)), (0, ("win",)), (0, ("wa", "wb", "wo")), (0, ("up2", "down2")),
              (1, ("up1", "down1")), (1, ("win",)), (1, ("wa", "wb", "wo")), (1, ("up2", "down2"))]
    group_of, chips_done, sibling_done = {}, {}, {}
    count = 0
    for i, (layer, names) in enumerate(travel):
        chips_done[i] = list(range(count, count + len(names)))
        count += len(names)
        for n in names:
            group_of[layer, n] = (i, names)
    gathered, token = _exchange_start(
        "gather", [shard[layer][n] for layer, names in travel for n in names], None, None, "w")
    zero = token[0, 0]

    biases = [_na_bias_table(na_rel_bias[layer] + zero) for layer in range(DEPTH)]

    def pass_on(i, behind):
        if i in chips_done:
            lands = _exchange_wait(gathered, behind, f"w{i}", which=chips_done.pop(i))
            sibling_done[i], _ = _exchange_start("forward", [], lands, None, f"p{i}")

    def fetcher(layer):
        def fetch(name, behind):
            if (layer, name) in group_of:
                i, names = group_of[layer, name]
                if i == 0:
                    behind = (behind, *biases)
                pass_on(i, behind)
                pass_on(i + 1, behind)
                for n, got in zip(names, _exchange_wait(sibling_done.pop(i), behind, f"p{i}")):
                    weights[layer][n] = got
                    del group_of[layer, n]
            return weights[layer][name]
        return fetch

    saved = []
    h = xs
    hn = _norm_fwd(xs, ffn1_norm[0] + zero, "first")
    for layer in range(DEPTH):
        bias = biases[layer]
        fetch = fetcher(layer)
        after_ffn2 = ffn1_norm[layer + 1] if layer + 1 < DEPTH else None
        h, hn, s1 = _ffn_forward(h, hn, fetch, ("up1", "down1"), f"f1l{layer}", mix_norm[layer])
        h, hn, s2 = _mixer_forward(h, hn, fetch, bias, tables, f"l{layer}", ffn2_norm[layer])
        h, hn, s3 = _ffn_forward(h, hn, fetch, ("up2", "down2"), f"f2l{layer}", after_ffn2)
        saved.append((s1, s2, s3, bias))
    loss_part, dh, dh_b, d_final = _loss_head(h, final_norm, tgt)

    d_norms = ([None] * DEPTH, [None] * DEPTH, [None] * DEPTH)
    d_bias = [None] * DEPTH
    sent = {}

    def sender(layer, suffix):
        def send(names, grads):
            tag = f"g{layer}{names[0]}{suffix}"
            handle, token = _exchange_start("scatter", grads, None, None, tag)
            for i, n in enumerate(names):
                sent[layer, n + suffix] = (handle, i, tag)
            return token
        return send

    for layer in reversed(range(DEPTH)):
        w = weights[layer]
        s1, s2, s3, bias = saved[layer]
        dh, dh_b, d_norms[2][layer] = _ffn_backward(
            dh, dh_b, s3, ffn2_norm[layer], w["up2"].reshape(2, F, D), w["down2"], f"f2l{layer}", sender(layer, "2"))
        dh, dh_b, d_norms[1][layer], d_bias[layer] = _mixer_backward(
            dh, dh_b, s2, mix_norm[layer], (w["win"], w["wa"], w["wb"], w["wo"]), bias, tables, f"l{layer}",
            sender(layer, ""))
        dh, dh_b, d_norms[0][layer] = _ffn_backward(
            dh, dh_b, s1, ffn1_norm[layer], w["up1"].reshape(2, F, D), w["down1"], f"f1l{layer}", sender(layer, "1"))
    grad_x = dh.reshape(x.shape)

    originals = dict(up1=(ffn1_w_up, m_ffn1_w_up, v_ffn1_w_up), down1=(ffn1_w_down, m_ffn1_w_down, v_ffn1_w_down),
                     win=(w_in, m_w_in, v_w_in), wa=(w_branch_a, m_w_branch_a, v_w_branch_a),
                     wb=(w_branch_b, m_w_branch_b, v_w_branch_b), wo=(w_out, m_w_out, v_w_out),
                     up2=(ffn2_w_up, m_ffn2_w_up, v_ffn2_w_up), down2=(ffn2_w_down, m_ffn2_w_down, v_ffn2_w_down))
    big = {}
    behind = dh
    landed = {}

    def received(layer, name):
        handle, i, tag = sent[layer, name]
        if tag not in landed:
            landed[tag] = _exchange_wait(handle, behind, tag)
        return landed[tag][i]

    for name in ("down2", "up2", "win", "wa", "wb", "wo", "down1", "up1"):
        wv, mv, vv = originals[name]
        if name in col_sharded:
            wv, mv, vv = (jnp.swapaxes(t, 1, 2) for t in (wv, mv, vv))
        big[name] = tuple(_sum_adamw(received(0, name), received(1, name), wv, mv, vv, name))
        behind = big[name][1]
        if name in col_sharded:
            big[name] = tuple(jnp.swapaxes(t, 1, 2) for t in big[name])

    small = _allreduce_small(_pack_small(d_norms, d_bias, d_final, loss_part[0, :1]), behind)
    g_norms, g_bias, g_final, loss = _unpack_small(small)
    w_small = _pack_small((ffn1_norm, mix_norm, ffn2_norm), na_rel_bias, final_norm)
    m_small = _pack_small((m_ffn1_norm, m_mix_norm, m_ffn2_norm), m_na_rel_bias, m_final_norm)
    v_small = _pack_small((v_ffn1_norm, v_mix_norm, v_ffn2_norm), v_na_rel_bias, v_final_norm)
    upd = _adamw(w_small[None], small[None], m_small[None], v_small[None], "small")
    small_out = [(g_norms, g_bias, g_final)] + [_unpack_small(u[0])[:3] for u in upd]

    outputs = [loss, grad_x]
    for kind in range(4):
        norms, bias_k, final_k = small_out[kind]
        outputs += [norms[0], big["up1"][kind], big["down1"][kind], norms[1], big["win"][kind], bias_k,
                    big["wa"][kind], big["wb"][kind], big["wo"][kind], norms[2], big["up2"][kind],
                    big["down2"][kind], final_k]
    return tuple(outputs)
```

```python
import numpy as np

import jax
import jax.numpy as jnp
from jax import lax
from jax.experimental import pallas as pl
from jax.experimental.pallas import tpu as pltpu

F32 = jnp.float32
BF16 = jnp.bfloat16
SDS = jax.ShapeDtypeStruct
BS = pl.BlockSpec
MESH = pl.DeviceIdType.MESH

D = 1024
S = 2048
F = 2816
DEPTH = 2
HEAD_DIM = 64
DILATIONS = (1, 4, 16)
HALF = 64
QKV_A = 2304
QKV_B = 1536
IN_W = 5888
N_DEV = 8
NA_ROWS = 32
GRID_W = 64
NA_KR = 8
ROPE_THETA = 10000.0
RMS_EPS = 1e-6
NEG = -1e30
SCALE = HEAD_DIM ** -0.5
ADAM_LR, ADAM_B1, ADAM_B2, ADAM_EPS, ADAM_WD, ADAM_STEP = 0.001, 0.9, 0.999, 1e-08, 0.01, 10
VMEM_LIMIT_V7X = 52 * 1024 * 1024
SMALL_ROWS = 120
BIAS_PAD = 3840
NA_FWD_ROWS = 8
NA_BWD_ROWS = 4
DIL_FWD_TILES = 8
DIL_BWD_TILES = 4


def _cp(*sem):
    return pltpu.CompilerParams(dimension_semantics=sem, vmem_limit_bytes=VMEM_LIMIT_V7X)


def _dot_nn(a, b):
    return jnp.dot(a, b, preferred_element_type=F32)


def _dot_nt(a, b):
    return lax.dot_general(a, b, (((1,), (1,)), ((), ())), preferred_element_type=F32)


def _dot_tn(a, b):
    return lax.dot_general(a, b, (((0,), (0,)), ((), ())), preferred_element_type=F32)


def _ds(start, size, stride):
    return pl.ds(start, size) if stride == 1 else pl.ds(start, size, stride=stride)


def _norm_fwd(x, g, tag):
    t = x.shape[0]
    tm = 512

    def body(x_ref, g_ref, o_ref):
        xv = x_ref[...]
        r = lax.rsqrt(jnp.mean(xv * xv, axis=-1, keepdims=True) + RMS_EPS)
        o_ref[...] = (xv * r * g_ref[...]).astype(BF16)

    return pl.pallas_call(
        body, name=f"norm_fwd_{tag}", grid=(t // tm,),
        in_specs=[BS((tm, D), lambda i: (i, 0)), BS((1, D), lambda i: (0, 0))],
        out_specs=BS((tm, D), lambda i: (i, 0)),
        out_shape=SDS((t, D), BF16), compiler_params=_cp("parallel"),
    )(x, g.reshape(1, D))


def _loss_head(x, g, tgt):
    t = x.shape[0]
    tm = 1024

    def body(x_ref, g_ref, t_ref, loss_ref, dx_ref, dxb_ref, dg_ref):
        @pl.when(pl.program_id(0) == 0)
        def _():
            dg_ref[...] = jnp.zeros_like(dg_ref)
            loss_ref[...] = jnp.zeros_like(loss_ref)

        xv = x_ref[...]
        gv = g_ref[...]
        r = lax.rsqrt(jnp.mean(xv * xv, axis=-1, keepdims=True) + RMS_EPS)
        xh = xv * r
        e = xh * gv - t_ref[...]
        loss_ref[...] += 0.5 * jnp.sum(jnp.mean(e * e, axis=-1, keepdims=True), axis=0, keepdims=True)
        dy = e * (1.0 / D)
        u = dy * gv
        dx = r * (u - xh * jnp.mean(xh * u, axis=-1, keepdims=True))
        dx_ref[...] = dx
        dxb_ref[...] = dx.astype(BF16)
        dg_ref[...] += jnp.sum(dy * xh, axis=0, keepdims=True)

    row = BS((tm, D), lambda i: (i, 0))
    vec = BS((1, D), lambda i: (0, 0))
    return pl.pallas_call(
        body, name="loss_head", grid=(t // tm,),
        in_specs=[row, vec, row], out_specs=[BS((1, 128), lambda i: (0, 0)), row, row, vec],
        out_shape=[SDS((1, 128), F32), SDS((t, D), F32), SDS((t, D), BF16), SDS((1, D), F32)],
        compiler_params=_cp("arbitrary"),
    )(x, g.reshape(1, D), tgt)


def _mm_nn(a, w, tag, res=None, scale=1.0, tm=1024, tn=None, next_g=None):
    c_n, t, k = a.shape
    n = w.shape[2]
    tn = n if tn is None else tn
    assert next_g is None or tn == n
    n_in = 2 + (res is not None) + (next_g is not None)

    def body(*refs):
        a_ref, w_ref = refs[0], refs[1]
        acc = _dot_nn(a_ref[0].astype(BF16), w_ref[0])
        for c in range(1, c_n):
            acc = acc + _dot_nn(a_ref[c].astype(BF16), w_ref[c])
        if scale != 1.0:
            acc = acc * scale
        if res is not None:
            acc = refs[2][...] + acc
        refs[n_in][...] = acc
        if next_g is not None:
            r = lax.rsqrt(jnp.mean(acc * acc, axis=-1, keepdims=True) + RMS_EPS)
            refs[n_in + 1][...] = (acc * r * refs[n_in - 1][...]).astype(BF16)

    w_mode = dict(pipeline_mode=pl.Buffered(1)) if tn == n else {}
    in_specs = [BS((c_n, tm, k), lambda i, j: (0, i, 0)), BS((c_n, k, tn), lambda i, j: (0, 0, j), **w_mode)]
    args = [a, w]
    out_specs = [BS((tm, tn), lambda i, j: (i, j))]
    out_shape = [SDS((t, n), F32)]
    if res is not None:
        in_specs.append(BS((tm, tn), lambda i, j: (i, j)))
        args.append(res)
    if next_g is not None:
        in_specs.append(BS((1, n), lambda i, j: (0, 0)))
        args.append(next_g.reshape(1, n))
        out_specs.append(BS((tm, tn), lambda i, j: (i, j)))
        out_shape.append(SDS((t, n), BF16))
    got = pl.pallas_call(
        body, name=f"mm_nn_{tag}", grid=(t // tm, n // tn), in_specs=in_specs, out_specs=out_specs,
        out_shape=out_shape, compiler_params=_cp("parallel", "parallel"),
    )(*args)
    return got if next_g is not None else got[0]


def _mm_nn_norm_bwd(parts, w, x, g, dres, tag, tm=512):
    t = parts[0].shape[1]
    n_parts = len(parts)

    def body(*refs):
        w_ref, x_ref, g_ref, dr_ref, dx_ref, dxb_ref, dg_ref = refs[n_parts:]

        @pl.when(pl.program_id(0) == 0)
        def _():
            dg_ref[...] = jnp.zeros_like(dg_ref)

        dh = None
        row = 0
        for a_ref, part in zip(refs, parts):
            for c in range(part.shape[0]):
                term = _dot_nn(a_ref[c].astype(BF16), w_ref[row:row + part.shape[2], :])
                dh = term if dh is None else dh + term
                row += part.shape[2]
        xv = x_ref[...]
        r = lax.rsqrt(jnp.mean(xv * xv, axis=-1, keepdims=True) + RMS_EPS)
        xh = xv * r
        u = dh * g_ref[...]
        dx = dr_ref[...] + r * (u - xh * jnp.mean(xh * u, axis=-1, keepdims=True))
        dx_ref[...] = dx
        dxb_ref[...] = dx.astype(BF16)
        dg_ref[...] += jnp.sum(dh * xh, axis=0, keepdims=True)

    row = BS((tm, D), lambda i: (i, 0))
    vec = BS((1, D), lambda i: (0, 0))
    return pl.pallas_call(
        body, name=f"mm_nn_norm_bwd_{tag}", grid=(t // tm,),
        in_specs=[BS((p.shape[0], tm, p.shape[2]), lambda i: (0, i, 0)) for p in parts]
        + [BS(w.shape, lambda i: (0, 0), pipeline_mode=pl.Buffered(1)), row, vec, row],
        out_specs=[row, row, vec], out_shape=[SDS((t, D), F32), SDS((t, D), BF16), SDS((1, D), F32)],
        compiler_params=_cp("arbitrary"),
    )(*parts, w, x, g.reshape(1, D), dres)


def _mm_nt_rows(a, w, tag, tm, tn, n_total, w_row0, rope=None):
    t, k = a.shape
    assert w_row0 % tn == 0 and n_total % tn == 0
    j0 = w_row0 // tn

    def body(a_ref, w_ref, *rest):
        o_ref = rest[-1]
        o_ref[...] = _dot_nt(a_ref[...].astype(BF16), w_ref[...])
        if rope is not None:
            @pl.when(pl.program_id(0) == 0)
            def _():
                c = rest[0][...]
                sg = rest[1][...]
                first = (lax.broadcasted_iota(jnp.int32, (tm, 128), 1) % HEAD_DIM) < HEAD_DIM // 2
                for col in range(0, rope[2], 128):
                    v = o_ref[:, col:col + 128]
                    o_ref[:, col:col + 128] = v * c + _swap_halves(v, first) * sg

    in_specs = [BS((tm, k), lambda j, i: (i, 0)), BS((tn, k), lambda j, i: (j0 + j, 0))]
    args = [a, w]
    if rope is not None:
        assert rope[2] <= tn
        in_specs += [BS((tm, 128), lambda j, i: (i % (S // tm), 0))] * 2
        args += [rope[0], rope[1]]
    return pl.pallas_call(
        body, name=f"mm_nt_{tag}", grid=(n_total // tn, t // tm), in_specs=in_specs,
        out_specs=BS((tm, tn), lambda j, i: (i, j)), out_shape=SDS((t, n_total), F32),
        compiler_params=_cp("parallel", "parallel"),
    )(*args)


def _mm_tn(a, b, tag, scale=1.0, tmm=None, into=None, row0=0, rows=None):
    c_n, t, m = a.shape
    n = b.shape[1]
    if tmm is None:
        tmm = max(w for w in (768, 512, 256) if m % w == 0 and row0 % w == 0)
    tiles = m // tmm
    block0 = row0 // tmm
    assert row0 % tmm == 0 and m % tmm == 0

    def body(a_ref, b_ref, *rest):
        rest[-1][...] = (_dot_tn(a_ref[...].astype(BF16), b_ref[...].astype(BF16)) * scale).astype(BF16)

    in_specs = [BS((None, t, tmm), lambda c, mi: (c, 0, mi)), BS((t, n), lambda c, mi: (0, 0))]
    args = [a, b]
    if into is not None:
        in_specs.append(BS(memory_space=pl.ANY))
        args.append(into)
    return pl.pallas_call(
        body, name=f"mm_tn_{tag}", grid=(c_n, tiles), in_specs=in_specs,
        out_specs=BS((tmm, n), lambda c, mi: (block0 + c * tiles + mi, 0)),
        out_shape=SDS((rows or c_n * m, n) if into is None else into.shape, BF16),
        input_output_aliases={} if into is None else {2: 0},
        compiler_params=_cp("parallel", "parallel"),
    )(*args)


def _ffn_up(hn, wut, tag):
    t = hn.shape[0]
    tm, tn = 512, 1408

    def body(h_ref, w_ref, gu_ref, act_ref):
        h = h_ref[...]
        g = _dot_nt(h, w_ref[0])
        u = _dot_nt(h, w_ref[1])
        sg = jax.nn.sigmoid(g)
        silu = g * sg
        gu_ref[0] = (u * (sg + silu * (1.0 - sg))).astype(BF16)
        gu_ref[1] = silu.astype(BF16)
        act_ref[...] = (silu * u).astype(BF16)

    return pl.pallas_call(
        body, name=f"ffn_up_{tag}", grid=(F // tn, t // tm),
        in_specs=[BS((tm, D), lambda j, i: (i, 0)), BS((2, tn, D), lambda j, i: (0, j, 0))],
        out_specs=[BS((2, tm, tn), lambda j, i: (0, i, j)), BS((tm, tn), lambda j, i: (i, j))],
        out_shape=[SDS((2, t, F), BF16), SDS((t, F), BF16)],
        compiler_params=_cp("parallel", "parallel"),
    )(hn, wut)


def _ffn_dact(dxo, wd, gu, tie, tag):
    t = dxo.shape[0]
    tm, tn = 512, 1408

    def body(d_ref, w_ref, gu_ref, tie_ref, o_ref):
        dact = _dot_nt(d_ref[...] * 0.5, w_ref[...])
        o_ref[0] = (dact * gu_ref[0].astype(F32)).astype(BF16)
        o_ref[1] = (dact * gu_ref[1].astype(F32)).astype(BF16)

    return pl.pallas_call(
        body, name=f"ffn_dact_{tag}", grid=(F // tn, t // tm),
        in_specs=[BS((tm, D), lambda j, i: (i, 0)), BS((tn, D), lambda j, i: (j, 0)),
                  BS((2, tm, tn), lambda j, i: (0, i, j)), BS((8, 128), lambda j, i: (0, 0))],
        out_specs=BS((2, tm, tn), lambda j, i: (0, i, j)),
        out_shape=SDS((2, t, F), BF16), compiler_params=_cp("parallel", "parallel"),
    )(dxo, wd, gu, tie)


def _rope_tables():
    half = HEAD_DIM // 2
    inv_freq = ROPE_THETA ** (-jnp.arange(half, dtype=F32) / half)
    ang = jnp.arange(S).astype(F32)[:, None] * inv_freq[None, :]
    cos, sin = jnp.cos(ang), jnp.sin(ang)
    return jnp.concatenate([cos, cos, cos, cos], axis=1), jnp.concatenate([-sin, sin, -sin, sin], axis=1)


def _swap_halves(t, first_half):
    return jnp.where(first_half, pltpu.roll(t, 96, 1), pltpu.roll(t, 32, 1))


def _rope_bwd(dqs, dks, dvs, cos_t, sin_t):
    t = dqs[0].shape[0]
    tm = 1024

    def body(*refs):
        c = refs[9][...]
        sg = refs[10][...]
        o_ref = refs[11]
        first = (lax.broadcasted_iota(jnp.int32, (tm, 128), 1) % HEAD_DIM) < HEAD_DIM // 2
        for a in range(6):
            for hp in range(2):
                v = refs[a][:, 128 * hp:128 * (hp + 1)]
                col = 128 * (2 * a + hp)
                o_ref[:, col:col + 128] = (v * c + _swap_halves(v * sg, first)).astype(BF16)
        for a in range(6, 9):
            o_ref[:, 256 * a:256 * (a + 1)] = refs[a][...].astype(BF16)

    blk = BS((tm, 256), lambda i: (i, 0))
    tab = BS((tm, 128), lambda i: (i % (S // tm), 0))
    return pl.pallas_call(
        body, name="rope_bwd", grid=(t // tm,), in_specs=[blk] * 9 + [tab, tab],
        out_specs=BS((None, tm, QKV_A), lambda i: (0, i, 0)), out_shape=SDS((1, t, QKV_A), BF16),
        compiler_params=_cp("parallel"),
    )(*dqs, *dks, *dvs, cos_t, sin_t)


def _head_masks():
    lane = lax.broadcasted_iota(jnp.int32, (1, 128), 1)
    m0 = (lane < HEAD_DIM).astype(F32)
    return m0, 1.0 - m0


def _dil_geometry(d):
    sub = S // d
    q_rows = 128
    k_rows = min(256, sub)
    return sub, q_rows, sub // q_rows, k_rows


def _dil_tile(idx, d, keys_on_rows=False):
    sub, q_rows, nb, k_rows = _dil_geometry(d)
    r = idx // nb
    n = idx % nb
    k_sub = jnp.clip(q_rows * n - HALF, 0, sub - k_rows)
    if d == 1:
        q_start = pl.multiple_of(q_rows * n, q_rows)
        k_start = pl.multiple_of(k_sub, HALF)
    else:
        q_start = q_rows * n * d + r
        k_start = k_sub * d + r
    if keys_on_rows:
        ii = lax.broadcasted_iota(jnp.int32, (k_rows, 2 * q_rows), 1) % q_rows
        jj = lax.broadcasted_iota(jnp.int32, (k_rows, 2 * q_rows), 0)
    else:
        ii = lax.broadcasted_iota(jnp.int32, (q_rows, k_rows), 0)
        jj = lax.broadcasted_iota(jnp.int32, (q_rows, k_rows), 1)
    valid = jnp.abs(jj - ii + (k_sub - q_rows * n)) <= HALF
    return q_start, k_start, valid


def _dil_specs(grp):
    qs = BS((S, 128), lambda b, hp: (b, 2 * grp + hp))
    ks = BS((S, 128), lambda b, hp: (b, 6 + 2 * grp + hp))
    vs = BS((S, 128), lambda b, hp: (b, 12 + 2 * grp + hp))
    own = BS((S, 128), lambda b, hp: (b, hp))
    return qs, ks, vs, own


def _dil_fwd(qkr, proj, grp):
    t = qkr.shape[0]
    d = DILATIONS[grp]
    _, q_rows, nb, k_rows = _dil_geometry(d)

    def body(q_ref, k_ref, v_ref, o_ref, l_ref):
        masks = _head_masks()

        def step(i0, carry):
            geo = [_dil_tile(i0 * DIL_FWD_TILES + j, d) for j in range(DIL_FWD_TILES)]
            tiles = [(j, h) for j in range(DIL_FWD_TILES) for h in range(2)]
            qs = [q_ref[_ds(g[0], q_rows, d), :] for g in geo]
            kbs = [k_ref[_ds(g[1], k_rows, d), :].astype(BF16) for g in geo]
            ss = [jnp.where(geo[j][2], _dot_nt((qs[j] * masks[h]).astype(BF16), kbs[j]) * SCALE, NEG) for j, h in tiles]
            mxs = [jnp.max(s, axis=1, keepdims=True) for s in ss]
            ps = [jnp.exp(s - mx) for s, mx in zip(ss, mxs)]
            dens = [jnp.sum(p, axis=1, keepdims=True) for p in ps]
            vs = [v_ref[_ds(g[1], k_rows, d), :] for g in geo]
            outs = [_dot_nn(p.astype(BF16), (vs[j] * masks[h]).astype(BF16)) / den
                    for p, den, (j, h) in zip(ps, dens, tiles)]
            for j, g in enumerate(geo):
                o_ref[_ds(g[0], q_rows, d), :] = outs[2 * j] + outs[2 * j + 1]
                l_ref[_ds(g[0], q_rows, d), :] = (
                    (mxs[2 * j] + jnp.log(dens[2 * j])) * masks[0] + (mxs[2 * j + 1] + jnp.log(dens[2 * j + 1])) * masks[1])
            return carry

        lax.fori_loop(0, d * nb // DIL_FWD_TILES, step, 0)

    qs, ks, vs, own = _dil_specs(grp)
    return pl.pallas_call(
        body, name=f"dil_fwd_{grp}", grid=(t // S, 2), in_specs=[qs, ks, vs], out_specs=[own, own],
        out_shape=[SDS((t, 256), F32), SDS((t, 256), F32)], compiler_params=_cp("parallel", "parallel"),
    )(qkr, qkr, proj)


def _dil_bwd(qkr, proj, do, dlp, lse, grp):
    t = qkr.shape[0]
    d = DILATIONS[grp]
    _, q_rows, nb, k_rows = _dil_geometry(d)

    def body(q_ref, k_ref, v_ref, do_ref, dl_ref, l_ref, dq_ref, dk_ref, dv_ref):
        masks = _head_masks()
        dk_ref[...] = jnp.zeros_like(dk_ref)
        dv_ref[...] = jnp.zeros_like(dv_ref)

        def as_row(x2):
            xt = x2.T
            return jnp.concatenate([xt[0:1], xt[HEAD_DIM:HEAD_DIM + 1]], axis=1)

        def step(i0, carry):
            geo = [_dil_tile(i0 * DIL_BWD_TILES + j, d, keys_on_rows=True) for j in range(DIL_BWD_TILES)]
            q_ds = [_ds(g[0], q_rows, d) for g in geo]
            k_ds = [_ds(g[1], k_rows, d) for g in geo]
            qbs = [_both_heads(q_ref[r, :], masks).astype(BF16) for r in q_ds]
            kbs = [k_ref[r, :].astype(BF16) for r in k_ds]
            vbs = [v_ref[r, :].astype(BF16) for r in k_ds]
            dobs = [_both_heads(do_ref[r, :], masks).astype(BF16) for r in q_ds]
            l_rows = [as_row(l_ref[r, :]) for r in q_ds]
            dl_rows = [as_row(dl_ref[r, :]) for r in q_ds]
            ss = [jnp.where(g[2], _dot_nt(kb, qb) * SCALE, NEG) for g, kb, qb in zip(geo, kbs, qbs)]
            ps = [jnp.exp(s - lr) for s, lr in zip(ss, l_rows)]
            dps = [_dot_nt(vb, dob) for vb, dob in zip(vbs, dobs)]
            dss = [(p * (dp - dr)).astype(BF16) for p, dp, dr in zip(ps, dps, dl_rows)]
            dks = [_dot_nn(ds, qb) for ds, qb in zip(dss, qbs)]
            dvs = [_dot_nn(p.astype(BF16), dob) for p, dob in zip(ps, dobs)]
            dqs = [_own_heads(_dot_tn(ds, kb), masks) for ds, kb in zip(dss, kbs)]
            for j in range(DIL_BWD_TILES):
                dq_ref[q_ds[j], :] = dqs[j] * SCALE
                dk_ref[k_ds[j], :] += dks[j] * SCALE
                dv_ref[k_ds[j], :] += dvs[j]
            return carry

        lax.fori_loop(0, d * nb // DIL_BWD_TILES, step, 0)

    qs, ks, vs, own = _dil_specs(grp)
    return pl.pallas_call(
        body, name=f"dil_bwd_{grp}", grid=(t // S, 2), in_specs=[qs, ks, vs, own, own, own],
        out_specs=[own, own, own], out_shape=[SDS((t, 256), F32)] * 3,
        compiler_params=_cp("parallel", "parallel"),
    )(qkr, qkr, proj, do, dlp, lse)


def _mix_weights(l0, l1, l2):
    mx = jnp.maximum(jnp.maximum(l0, l1), l2)
    e0, e1, e2 = jnp.exp(l0 - mx), jnp.exp(l1 - mx), jnp.exp(l2 - mx)
    den = e0 + e1 + e2
    return e0 / den, e1 / den, e2 / den


def _combine_fwd(outs, lses):
    t = outs[0].shape[0]
    tm = 1024

    def body(o0, o1, o2, l0, l1, l2, y_ref):
        w0, w1, w2 = _mix_weights(l0[...], l1[...], l2[...])
        y_ref[...] = w0 * o0[...] + w1 * o1[...] + w2 * o2[...]

    blk = BS((tm, 256), lambda i: (i, 0))
    return pl.pallas_call(
        body, name="combine_fwd", grid=(t // tm,), in_specs=[blk] * 6, out_specs=blk,
        out_shape=SDS((t, 256), F32), compiler_params=_cp("parallel"),
    )(*outs, *lses)


def _head_sum(x):
    a = lax.broadcasted_iota(jnp.int32, (256, 256), 0) // HEAD_DIM
    b = lax.broadcasted_iota(jnp.int32, (256, 256), 1) // HEAD_DIM
    ones = (a == b).astype(BF16)
    hi = x.astype(BF16)
    lo = (x - hi.astype(F32)).astype(BF16)
    return _dot_nn(hi, ones) + _dot_nn(lo, ones)


def _combine_bwd(dya, outs, lses):
    t = dya.shape[0]
    tm = 1024

    def body(dy_ref, o0, o1, o2, l0, l1, l2, d0, d1, d2, e0, e1, e2):
        ws = _mix_weights(l0[...], l1[...], l2[...])
        dy = dy_ref[...]
        ya = ws[0] * o0[...] + ws[1] * o1[...] + ws[2] * o2[...]
        hs = _head_sum(dy * ya)
        for w, d_ref, e_ref in zip(ws, (d0, d1, d2), (e0, e1, e2)):
            d_ref[...] = w * dy
            e_ref[...] = w * hs

    blk = BS((tm, 256), lambda i: (i, 0))
    return pl.pallas_call(
        body, name="combine_bwd", grid=(t // tm,), in_specs=[blk] * 7, out_specs=[blk] * 6,
        out_shape=[SDS((t, 256), F32)] * 6, compiler_params=_cp("parallel"),
    )(dya, *outs, *lses)


def _na_bias_table(rel_bias):
    kw = NA_KR * GRID_W
    rev = jnp.pad(rel_bias.astype(F32)[:, :, ::-1], ((0, 0), (0, 0), (0, 128 - 31)))

    def body(r_ref, o_ref):
        lane = lax.broadcasted_iota(jnp.int32, (GRID_W, 128), 1)
        j = lax.broadcasted_iota(jnp.int32, (GRID_W, 128), 0)
        q = lane % GRID_W
        win_lo = jnp.clip(q - 8, 0, GRID_W - 16)
        valid = (j >= win_lo) & (j < win_lo + 16)
        for cls in range(NA_KR):
            for k in range(NA_KR):
                tiles = []
                for h in range(2):
                    row = jnp.broadcast_to(r_ref[h, cls + k:cls + k + 1, :], (GRID_W, 128))
                    tiles.append(pltpu.roll(row, (128 - 15 + GRID_W * h) % 128, 1, stride=1, stride_axis=0))
                o_ref[cls, GRID_W * k:GRID_W * (k + 1), :] = jnp.where(
                    valid, jnp.where(lane < GRID_W, tiles[0], tiles[1]), NEG)

    return pl.pallas_call(
        body, name="na_bias_table", grid=(4,),
        in_specs=[BS((2, 2 * NA_KR - 1, 128), lambda hp: (hp, 0, 0))],
        out_specs=BS((None, NA_KR, kw, 128), lambda hp: (hp, 0, 0, 0)),
        out_shape=SDS((4, NA_KR, kw, 128), F32), compiler_params=_cp("parallel"),
    )(rev)


def _na_row(i):
    lo = jnp.clip(i - NA_KR // 2, 0, NA_ROWS - NA_KR)
    return pl.multiple_of(GRID_W * i, GRID_W), pl.multiple_of(GRID_W * lo, GRID_W), lo - i + NA_KR - 1


def _both_heads(x, masks):
    return jnp.concatenate([x * masks[0], x * masks[1]], axis=0)


def _own_heads(r, masks):
    half = r.shape[0] // 2
    return r[:half] * masks[0] + r[half:] * masks[1]


def _na_fwd(proj, bias):
    t = proj.shape[0]
    kw = NA_KR * GRID_W

    def body(q_ref, k_ref, v_ref, b_ref, o_ref, l_ref):
        masks = _head_masks()

        def step(i0, carry):
            idx = [i0 * NA_FWD_ROWS + j for j in range(NA_FWD_ROWS)]
            rows = [_na_row(i) for i in idx]
            qbs = [_both_heads(q_ref[pl.ds(r[0], GRID_W), :], masks).astype(BF16) for r in rows]
            kbs = [k_ref[pl.ds(r[1], kw), :].astype(BF16) for r in rows]
            ss = [_dot_nt(kb, qb) * SCALE + b_ref[r[2]] for kb, qb, r in zip(kbs, qbs, rows)]
            mxs = [jnp.max(s, axis=0, keepdims=True) for s in ss]
            ps = [jnp.exp(s - mx) for s, mx in zip(ss, mxs)]
            dens = [jnp.sum(p, axis=0, keepdims=True) for p in ps]
            pbs = [(p / den).astype(BF16) for p, den in zip(ps, dens)]
            vbs = [v_ref[pl.ds(r[1], kw), :].astype(BF16) for r in rows]
            outs = [_own_heads(_dot_tn(pb, vb), masks) for pb, vb in zip(pbs, vbs)]
            for j, r in enumerate(rows):
                o_ref[pl.ds(r[0], GRID_W), :] = outs[j]
                l_ref[pl.ds(idx[j], 1), :] = mxs[j] + jnp.log(dens[j])
            return carry

        lax.fori_loop(0, NA_ROWS // NA_FWD_ROWS, step, 0)

    c0 = QKV_A // 128
    return pl.pallas_call(
        body, name="na_fwd", grid=(t // S, 4),
        in_specs=[BS((S, 128), lambda b, hp: (b, c0 + hp)), BS((S, 128), lambda b, hp: (b, c0 + 4 + hp)),
                  BS((S, 128), lambda b, hp: (b, c0 + 8 + hp)),
                  BS((None, NA_KR, kw, 128), lambda b, hp: (hp, 0, 0, 0))],
        out_specs=[BS((S, 128), lambda b, hp: (b, hp)), BS((None, None, NA_ROWS, 128), lambda b, hp: (b, hp, 0, 0))],
        out_shape=[SDS((t, 512), F32), SDS((t // S, 4, NA_ROWS, 128), F32)],
        compiler_params=_cp("parallel", "parallel"),
    )(proj, proj, proj, bias)


def _na_bwd(proj, bias, dyb, yb, lse):
    t = proj.shape[0]
    kw = NA_KR * GRID_W

    def body(q_ref, k_ref, v_ref, b_ref, do_ref, o_ref, l_ref, d_ref, db_ref):
        masks = _head_masks()
        ones = jnp.ones((8, 128), BF16)

        @pl.when(pl.program_id(1) == 0)
        def _():
            db_ref[...] = jnp.zeros_like(db_ref)

        d_ref[1:3] = jnp.zeros((2, S, 128), F32)

        def row_sums(x):
            hi = x.astype(BF16)
            lo = (x - hi.astype(F32)).astype(BF16)
            return (_dot_nt(ones, hi) + _dot_nt(ones, lo))[0:1]

        def step(i0, carry):
            idx = [i0 * NA_BWD_ROWS + j for j in range(NA_BWD_ROWS)]
            rows = [_na_row(i) for i in idx]
            q_ds = [pl.ds(r[0], GRID_W) for r in rows]
            k_ds = [pl.ds(r[1], kw) for r in rows]
            qbs = [_both_heads(q_ref[r, :], masks).astype(BF16) for r in q_ds]
            kbs = [k_ref[r, :].astype(BF16) for r in k_ds]
            vbs = [v_ref[r, :].astype(BF16) for r in k_ds]
            dos = [do_ref[r, :] for r in q_ds]
            dobs = [_both_heads(do, masks).astype(BF16) for do in dos]
            deltas = [row_sums(_both_heads(do * o_ref[r, :], masks)) for do, r in zip(dos, q_ds)]
            ss = [_dot_nt(kb, qb) * SCALE + b_ref[r[2]] for kb, qb, r in zip(kbs, qbs, rows)]
            ps = [jnp.exp(s - l_ref[pl.ds(i, 1), :]) for s, i in zip(ss, idx)]
            dps = [_dot_nt(vb, dob) for vb, dob in zip(vbs, dobs)]
            dss = [p * (dp - delta) for p, dp, delta in zip(ps, dps, deltas)]
            for ds, r in zip(dss, rows):
                db_ref[r[2]] += ds
            dsbs = [ds.astype(BF16) for ds in dss]
            dks = [_dot_nn(dsb, qb) for dsb, qb in zip(dsbs, qbs)]
            dvs = [_dot_nn(p.astype(BF16), dob) for p, dob in zip(ps, dobs)]
            dqs = [_own_heads(_dot_tn(dsb, kb), masks) for dsb, kb in zip(dsbs, kbs)]
            for j in range(NA_BWD_ROWS):
                d_ref[0, q_ds[j], :] = dqs[j] * SCALE
                d_ref[1, k_ds[j], :] += dks[j] * SCALE
                d_ref[2, k_ds[j], :] += dvs[j]
            return carry

        lax.fori_loop(0, NA_ROWS // NA_BWD_ROWS, step, 0)

    c0 = QKV_A // 128
    own = BS((S, 128), lambda hp, b: (b, hp))
    tab = BS((None, NA_KR, kw, 128), lambda hp, b: (hp, 0, 0, 0))
    return pl.pallas_call(
        body, name="na_bwd", grid=(4, t // S),
        in_specs=[BS((S, 128), lambda hp, b: (b, c0 + hp)), BS((S, 128), lambda hp, b: (b, c0 + 4 + hp)),
                  BS((S, 128), lambda hp, b: (b, c0 + 8 + hp)), tab, own, own,
                  BS((None, None, NA_ROWS, 128), lambda hp, b: (b, hp, 0, 0))],
        out_specs=[BS((3, S, 128), lambda hp, b: (0, b, hp)), tab],
        out_shape=[SDS((3, t, 512), F32), SDS((4, NA_KR, kw, 128), F32)],
        compiler_params=_cp("parallel", "arbitrary"),
    )(proj, proj, proj, bias, dyb, yb, lse)


def _na_dbias_lane_map():
    kw = NA_KR * GRID_W
    lane = np.arange(kw)
    blk, m = lane // GRID_W, lane % GRID_W
    target = np.full(kw, -1)
    target[m < 16] = (blk * 32 + 15 + m)[m < 16]
    target[m >= 49] = (((blk + 1) % NA_KR) * 32 + m - 49)[m >= 49]
    return jnp.asarray(target[:, None] == np.arange(kw)[None, :], BF16)


def _na_dbias(db):
    kw = NA_KR * GRID_W

    def body(x_ref, map_ref, o_ref, z_ref):
        for cls in range(NA_KR):
            xt = x_ref[cls].T
            for h in range(2):
                xv = xt[GRID_W * h:GRID_W * (h + 1)]
                y = xv[0:8]
                for g in range(1, GRID_W // 8):
                    y = y + pltpu.roll(xv[8 * g:8 * g + 8], kw - 8 * g, 1)
                d = y[0:1]
                for s in range(1, 8):
                    d = d + pltpu.roll(y[s:s + 1], kw - s, 1)
                z_ref[h, cls:cls + 1, :] = d
        for h in range(2):
            z = z_ref[h]
            hi = z.astype(BF16)
            lo = (z - hi.astype(F32)).astype(BF16)
            e = _dot_nn(hi, map_ref[...]) + _dot_nn(lo, map_ref[...])
            out = e[0:1]
            for cls in range(1, NA_KR):
                out = out + pltpu.roll(e[cls:cls + 1], 32 * cls, 1)
            o_ref[h] = jnp.broadcast_to(out, (8, kw))

    return pl.pallas_call(
        body, name="na_dbias", grid=(4,),
        in_specs=[BS((None, NA_KR, kw, 128), lambda hp: (hp, 0, 0, 0)), BS((kw, kw), lambda hp: (0, 0))],
        out_specs=BS((2, 8, kw), lambda hp: (hp, 0, 0)), out_shape=SDS((8, 8, kw), F32),
        scratch_shapes=[pltpu.VMEM((2, 8, kw), F32)], compiler_params=_cp("parallel"),
    )(db, _na_dbias_lane_map())


def _merge_fwd(ya, yb, proj, wat, wbt):
    t = ya.shape[0]
    tm, tn = 1024, 256
    ca = (QKV_A + QKV_B) // tn
    cb = ca + D // tn

    def body(ya_ref, yb_ref, la_ref, lb_ref, wa_ref, wb_ref, m_ref, za_ref, zb_ref):
        za = _dot_nt(ya_ref[...].astype(BF16), wa_ref[...])
        zb = _dot_nt(yb_ref[...].astype(BF16), wb_ref[...])
        m_ref[...] = (jax.nn.sigmoid(la_ref[...]) * za + jax.nn.sigmoid(lb_ref[...]) * zb).astype(BF16)
        za_ref[...] = za.astype(BF16)
        zb_ref[...] = zb.astype(BF16)

    out = BS((tm, tn), lambda i, j: (i, j))
    return pl.pallas_call(
        body, name="merge_fwd", grid=(t // tm, D // tn),
        in_specs=[BS((tm, 256), lambda i, j: (i, 0)), BS((tm, 512), lambda i, j: (i, 0)),
                  BS((tm, tn), lambda i, j: (i, ca + j)), BS((tm, tn), lambda i, j: (i, cb + j)),
                  BS((tn, 256), lambda i, j: (j, 0)), BS((tn, 512), lambda i, j: (j, 0))],
        out_specs=[out, out, out], out_shape=[SDS((t, D), BF16)] * 3,
        compiler_params=_cp("parallel", "parallel"),
    )(ya, yb, proj, proj, wat, wbt)


def _merge_bwd(dxo, wo, za, zb, proj):
    t = dxo.shape[0]
    tm, tn = 1024, 256
    ca = (QKV_A + QKV_B) // tn
    cb = ca + D // tn

    def body(d_ref, w_ref, za_ref, zb_ref, la_ref, lb_ref, dza_ref, dzb_ref, dl_ref):
        dmv = _dot_nt(d_ref[...], w_ref[...])
        ga = jax.nn.sigmoid(la_ref[...])
        gb = jax.nn.sigmoid(lb_ref[...])
        dza_ref[...] = (dmv * ga).astype(BF16)
        dzb_ref[...] = (dmv * gb).astype(BF16)
        dl_ref[0] = (dmv * za_ref[...].astype(F32) * ga * (1.0 - ga)).astype(BF16)
        dl_ref[1] = (dmv * zb_ref[...].astype(F32) * gb * (1.0 - gb)).astype(BF16)

    blk = BS((tm, tn), lambda i, j: (i, j))
    return pl.pallas_call(
        body, name="merge_bwd", grid=(t // tm, D // tn),
        in_specs=[BS((tm, D), lambda i, j: (i, 0)), BS((tn, D), lambda i, j: (j, 0)), blk, blk,
                  BS((tm, tn), lambda i, j: (i, ca + j)), BS((tm, tn), lambda i, j: (i, cb + j))],
        out_specs=[blk, blk, BS((2, tm, tn), lambda i, j: (0, i, j))],
        out_shape=[SDS((t, D), BF16), SDS((t, D), BF16), SDS((2, t, D), BF16)],
        compiler_params=_cp("parallel", "parallel"),
    )(dxo, wo, za, zb, proj, proj)


def _adamw_update(w, g, m, v):
    mn = ADAM_B1 * m + (1.0 - ADAM_B1) * g
    vn = ADAM_B2 * v + (1.0 - ADAM_B2) * (g * g)
    m_hat = mn / (1.0 - ADAM_B1 ** ADAM_STEP)
    v_hat = vn / (1.0 - ADAM_B2 ** ADAM_STEP)
    return -ADAM_LR * (m_hat / (jnp.sqrt(v_hat) + ADAM_EPS) + ADAM_WD * w), mn, vn


def _sum_adamw(recv0, recv1, w, m, v, tag):
    _, r, c = recv0.shape
    tr = max(rows for rows in range(16, r + 1, 16) if r % rows == 0 and rows * c <= 192 * 1024)

    def body(a_ref, b_ref, w_ref, m_ref, v_ref, g_ref, d_ref, mo_ref, vo_ref):
        for layer, ref in enumerate((a_ref, b_ref)):
            g = ref[0].astype(F32)
            for s in range(1, N_DEV):
                g = g + ref[s].astype(F32)
            g_ref[layer] = g
            d_ref[layer], mo_ref[layer], vo_ref[layer] = _adamw_update(w_ref[layer], g, m_ref[layer], v_ref[layer])

    slots = BS((N_DEV, tr, c), lambda i: (0, i, 0))
    blk = BS((2, tr, c), lambda i: (0, i, 0))
    return pl.pallas_call(
        body, name=f"sum_adamw_{tag}", grid=(r // tr,), in_specs=[slots, slots, blk, blk, blk],
        out_specs=[blk] * 4, out_shape=[SDS((2, r, c), F32)] * 4, compiler_params=_cp("parallel"),
    )(recv0, recv1, w, m, v)


def _adamw(w, g, m, v, tag):
    layers, r, c = w.shape
    tr = next(r // k for k in (1, 2, 4, 8) if r // k <= 384 and r % (8 * k) == 0)

    def body(w_ref, g_ref, m_ref, v_ref, d_ref, mo_ref, vo_ref):
        d_ref[...], mo_ref[...], vo_ref[...] = _adamw_update(w_ref[...], g_ref[...], m_ref[...], v_ref[...])

    blk = BS((None, tr, c), lambda l, i: (l, i, 0))
    return pl.pallas_call(
        body, name=f"adamw_{tag}", grid=(layers, r // tr), in_specs=[blk] * 4, out_specs=[blk] * 3,
        out_shape=[SDS((layers, r, c), F32)] * 3, compiler_params=_cp("parallel", "parallel"),
    )(w, g, m, v)


def _place():
    return lax.axis_index("x"), lax.axis_index("y"), lax.axis_index("c")


def _flip(coord, bit):
    return 1 - coord if bit else coord


def _peers(x, y, c):
    peers = []
    for mask in range(1, N_DEV):
        p = (_flip(x, mask & 4), _flip(y, mask & 2), _flip(c, mask & 1))
        peers.append((p, 4 * p[0] + 2 * p[1] + p[2]))
    return peers


def _copy_plan(mode, src, land, x, y, c):
    me = 4 * x + 2 * y + c

    def device(mask):
        p = (_flip(x, mask & 4), _flip(y, mask & 2), _flip(c, mask & 1))
        return p, 4 * p[0] + 2 * p[1] + p[2]

    if mode == "scatter":
        r = land.shape[1]
        return [(p, src.at[pl.ds(i * r, r), :], land.at[me], land.at[i])
                for p, i in map(device, (1, 2, 3, 4, 5, 6, 7, 0))]
    r = land.shape[0] // N_DEV

    def rows(i):
        return land.at[pl.ds(i * r, r), :]

    if mode == "gather":
        return [(p, src, rows(me), rows(i)) for p, i in map(device, (1, 4, 2, 6, 0))]
    sibling = device(1)[0]
    return [(sibling, rows(device(m)[1]), rows(device(m)[1]), rows(device(m | 1)[1])) for m in (4, 2, 6)]


COPIES = dict(scatter=8, gather=5, forward=3)
HBM_SPEC = BS(memory_space=pltpu.HBM)
SEM_SPEC = BS(memory_space=pltpu.SEMAPHORE)
DATAFLOW = pltpu.SideEffectType.DATAFLOW_SIDE_EFFECTING


def _fresh(shape, dtype, tag):
    def body(o_ref):
        del o_ref

    return pl.pallas_call(body, name=f"fresh_{tag}", out_specs=BS(memory_space=pl.ANY), out_shape=SDS(shape, dtype))()


def _exchange_start(mode, srcs, lands, after, tag):
    if lands is None and mode == "gather":
        lands = [_fresh((N_DEV * s.shape[0], s.shape[1]), s.dtype, f"{tag}_{a}") for a, s in enumerate(srcs)]
    elif lands is None:
        lands = [_fresh((N_DEV, s.shape[0] // N_DEV, s.shape[1]), s.dtype, f"{tag}_{a}") for a, s in enumerate(srcs)]
    n, n_src, n_cp = len(lands), len(srcs), COPIES[mode]
    behind = [] if after is None else [after]

    def body(*refs):
        src_refs, land_refs = refs[:n_src], refs[n_src:n_src + n]
        send_sems, recv_sems = refs[n_src + n + len(behind)], refs[n_src + n + len(behind) + 1]
        token = refs[-1]
        for a in range(n):
            plan = _copy_plan(mode, src_refs[a] if n_src else None, land_refs[a], *_place())
            for k, (p, out, there, _) in enumerate(plan):
                pltpu.make_async_remote_copy(
                    src_ref=out, dst_ref=there, send_sem=send_sems.at[n_cp * a + k],
                    recv_sem=recv_sems.at[n_cp * a + k], device_id=p, device_id_type=MESH).start()
        token[...] = jnp.zeros_like(token)

    both = [*srcs, *lands]
    res = pl.pallas_call(
        body, name=f"{mode}_start_{tag}",
        out_shape=(pltpu.SemaphoreType.DMA((n_cp * n,)), pltpu.SemaphoreType.DMA((n_cp * n,)),
                   *[pltpu.HBM(v.shape, v.dtype) for v in both], SDS((8, 128), F32)),
        in_specs=[HBM_SPEC] * len(both) + [BS(memory_space=pl.ANY)] * len(behind),
        out_specs=(SEM_SPEC, SEM_SPEC, *[HBM_SPEC] * len(both), BS(memory_space=pltpu.VMEM)),
        input_output_aliases={i: 2 + i for i in range(len(both))},
        compiler_params=pltpu.CompilerParams(has_side_effects=DATAFLOW),
    )(*[pltpu.with_memory_space_constraint(v, pltpu.HBM) for v in both], *behind)
    return (mode, res[0], res[1], res[2:2 + n_src], res[2 + n_src:2 + n_src + n]), res[-1]


def _exchange_wait(handle, after, tag, which=None):
    mode, send_sems, recv_sems, srcs, lands = handle
    which = list(range(len(lands))) if which is None else list(which)
    n_cp = COPIES[mode]
    lands = [lands[a] for a in which]
    srcs = [srcs[a] for a in which] if srcs else []
    n, n_src = len(lands), len(srcs)
    afters = list(after) if isinstance(after, (tuple, list)) else [after]

    def body(*refs):
        src_refs, land_refs = refs[:n_src], refs[n_src:n_src + n]
        send_ref, recv_ref = refs[n_src + n], refs[n_src + n + 1]
        for i, a in enumerate(which):
            plan = _copy_plan(mode, src_refs[i] if n_src else None, land_refs[i], *_place())
            for k, (p, out, _, here) in enumerate(plan):
                cp = pltpu.make_async_remote_copy(
                    src_ref=out, dst_ref=here, send_sem=send_ref.at[n_cp * a + k], recv_sem=recv_ref.at[n_cp * a + k],
                    device_id=p, device_id_type=MESH)
                cp.wait_send()
                cp.wait_recv()

    both = [*srcs, *lands]
    res = pl.pallas_call(
        body, name=f"{mode}_wait_{tag}", out_shape=tuple(pltpu.HBM(v.shape, v.dtype) for v in both),
        in_specs=[HBM_SPEC] * len(both) + [SEM_SPEC, SEM_SPEC] + [BS(memory_space=pl.ANY)] * len(afters),
        out_specs=tuple([HBM_SPEC] * len(both)),
        input_output_aliases={i: i for i in range(len(both))},
        compiler_params=pltpu.CompilerParams(has_side_effects=DATAFLOW),
    )(*both, send_sems, recv_sems, *afters)
    return list(res[n_src:])


def _allreduce_small(vec, behind):
    rows = vec.shape[0]

    def body(x_ref, behind_ref, o_ref, buf_ref, send_sems, recv_sems):
        x, y, c = _place()
        me = 4 * x + 2 * y + c
        buf_ref[me] = x_ref[...]
        peers = _peers(x, y, c)

        def copy(k, slot):
            return pltpu.make_async_remote_copy(
                src_ref=x_ref, dst_ref=buf_ref.at[slot], send_sem=send_sems.at[k], recv_sem=recv_sems.at[k],
                device_id=peers[k][0], device_id_type=MESH)

        sends = [copy(k, me) for k in range(N_DEV - 1)]
        for cp in sends:
            cp.start()
        for k in range(N_DEV - 1):
            copy(k, peers[k][1]).wait_recv()
        for cp in sends:
            cp.wait_send()
        acc = buf_ref[0]
        for s in range(1, N_DEV):
            acc = acc + buf_ref[s]
        o_ref[...] = acc

    vmem = BS(memory_space=pltpu.VMEM)
    return pl.pallas_call(
        body, name="allreduce_small", in_specs=[vmem, BS(memory_space=pl.ANY)], out_specs=vmem,
        out_shape=SDS((rows, 128), F32),
        scratch_shapes=[pltpu.VMEM((N_DEV, rows, 128), F32), pltpu.SemaphoreType.DMA((7,)),
                        pltpu.SemaphoreType.DMA((7,))],
        compiler_params=pltpu.CompilerParams(has_side_effects=True),
    )(vec, behind)


def _ffn_forward(x, hn, fetch, names, tag, next_g):
    gu, act = _ffn_up(hn, fetch(names[0], hn).reshape(2, F, D), tag)
    got = _mm_nn(act[None], fetch(names[1], act)[None], f"down_{tag}", res=x, scale=0.5, next_g=next_g)
    out, hn_next = got if next_g is not None else (got, None)
    return out, hn_next, (x, hn, gu, act)


def _ffn_backward(dxo, dxo_b, saved, norm_g, wut, wd, tag, send):
    x, hn, gu, act = saved
    d_wd = _mm_tn(act[None], dxo_b, f"dwd_{tag}", scale=0.5)
    du = _ffn_dact(dxo_b, wd, gu, send(("down",), [d_wd]), tag)
    d_wut = _mm_tn(du, hn, f"dwu_{tag}")
    token = send(("up",), [d_wut])
    return _mm_nn_norm_bwd([du], wut.reshape(2 * F, D), x, norm_g + token[0, 0], dxo, tag)


def _mixer_forward(x, hn, fetch, bias, tables, tag, next_g):
    proj = _mm_nt_rows(hn, fetch("win", hn), f"proj_{tag}", 512, IN_W // 2, IN_W, 0, rope=(*tables, 2 * QKV_A // 3))
    qkr = proj
    outs, lses = [], []
    for grp in range(3):
        o, l = _dil_fwd(qkr, proj, grp)
        outs.append(o)
        lses.append(l)
    ya = _combine_fwd(outs, lses)
    yb, lse_b = _na_fwd(proj, bias)
    merged, za, zb = _merge_fwd(ya, yb, proj, fetch("wa", yb), fetch("wb", yb))
    out, hn_next = _mm_nn(merged[None], fetch("wo", merged)[None], f"out_{tag}", res=x, next_g=next_g)
    return out, hn_next, (x, hn, proj, qkr, outs, lses, ya, yb, lse_b, merged, za, zb)


def _mixer_backward(dxo, dxo_b, saved, norm_g, w, bias, tables, tag, send):
    wint, wat, wbt, wo = w
    x, hn, proj, qkr, outs, lses, ya, yb, lse_b, merged, za, zb = saved
    d_wo = _mm_tn(merged[None], dxo_b, f"dwo_{tag}")
    dza, dzb, dlog = _merge_bwd(dxo_b, wo, za, zb, proj)
    dya = _mm_nn(dza[None], wat[None], f"dya_{tag}")
    dyb = _mm_nn(dzb[None], wbt[None], f"dyb_{tag}")
    d_wat = _mm_tn(dza[None], ya, f"dwa_{tag}")
    d_wbt = _mm_tn(dzb[None], yb, f"dwb_{tag}")
    cb = _combine_bwd(dya, outs, lses)
    dqs, dks, dvs = [], [], []
    for grp in range(3):
        dq, dk, dv = _dil_bwd(qkr, proj, cb[grp], cb[3 + grp], lses[grp], grp)
        dqs.append(dq)
        dks.append(dk)
        dvs.append(dv)
    d_qkv_b, dbias_tab = _na_bwd(proj, bias, dyb, yb, lse_b)
    dbias = _na_dbias(dbias_tab)
    dproj = [_rope_bwd(dqs, dks, dvs, *tables), d_qkv_b, dlog]
    d_wint, row = None, 0
    for i, p in enumerate(dproj):
        d_wint = _mm_tn(p, hn, f"dwin{i}_{tag}", into=d_wint, row0=row, rows=IN_W)
        row += p.shape[0] * p.shape[2]
    token = send(("win", "wa", "wb", "wo"), [d_wint, d_wat, d_wbt, d_wo])
    dx, dx_b, dg = _mm_nn_norm_bwd(dproj, wint, x, norm_g + token[0, 0], dxo, f"mix_{tag}")
    dbias = dbias[:, 0, :480].reshape(8, 15, 32)[:, :, :31]
    return dx, dx_b, dg, dbias


def _pack_small(norms, biases, final, loss=None):
    parts = []
    for layer in range(DEPTH):
        parts += [norms[0][layer], norms[1][layer], norms[2][layer],
                  jnp.pad(biases[layer].reshape(-1), (0, BIAS_PAD - 8 * 15 * 31))]
    parts.append(final)
    flat = jnp.concatenate([p.reshape(-1).astype(F32) for p in parts])
    if loss is not None:
        flat = jnp.concatenate([flat, loss.reshape(-1)])
    return jnp.pad(flat, (0, SMALL_ROWS * 128 - flat.shape[0])).reshape(SMALL_ROWS, 128)


def _unpack_small(packed):
    flat = packed.reshape(-1)
    norms, biases = ([], [], []), []
    pos = 0
    for _ in range(DEPTH):
        for k in range(3):
            norms[k].append(flat[pos:pos + D])
            pos += D
        biases.append(flat[pos:pos + 8 * 15 * 31].reshape(8, 15, 31))
        pos += BIAS_PAD
    final = flat[pos:pos + D]
    pos += D
    return [jnp.stack(n) for n in norms], jnp.stack(biases), final, flat[pos]


def kernel(x, ffn1_norm, ffn1_w_up, ffn1_w_down, mix_norm, w_in, na_rel_bias, w_branch_a, w_branch_b, w_out, ffn2_norm, ffn2_w_up, ffn2_w_down, final_norm, loss_target, m_ffn1_norm, m_ffn1_w_up, m_ffn1_w_down, m_mix_norm, m_w_in, m_na_rel_bias, m_w_branch_a, m_w_branch_b, m_w_out, m_ffn2_norm, m_ffn2_w_up, m_ffn2_w_down, m_final_norm, v_ffn1_norm, v_ffn1_w_up, v_ffn1_w_down, v_mix_norm, v_w_in, v_na_rel_bias, v_w_branch_a, v_w_branch_b, v_w_out, v_ffn2_norm, v_ffn2_w_up, v_ffn2_w_down, v_final_norm):
    t = x.shape[0] * x.shape[1]
    xs = x.reshape(t, D)
    tgt = loss_target.reshape(t, D)
    tables = _rope_tables()

    col_sharded = dict(up1=ffn1_w_up, win=w_in, wa=w_branch_a, wb=w_branch_b, up2=ffn2_w_up)
    row_sharded = dict(down1=ffn1_w_down, wo=w_out, down2=ffn2_w_down)
    shard = [{} for _ in range(DEPTH)]
    for layer in range(DEPTH):
        for name, arr in col_sharded.items():
            shard[layer][name] = arr[layer].T.astype(BF16)
        for name, arr in row_sharded.items():
            shard[layer][name] = arr[layer].astype(BF16)

    weights = [{} for _ in range(DEPTH)]
    travel = [(0, ("up1",)), (0, ("down1",)), (0, ("win",)), (0, ("wa", "wb", "wo")), (0, ("up2", "down2")),
              (1, ("up1", "down1")), (1, ("win",)), (1, ("wa", "wb", "wo")), (1, ("up2", "down2"))]
    group_of, chips_done, sibling_done = {}, {}, {}
    count = 0
    for i, (layer, names) in enumerate(travel):
        chips_done[i] = list(range(count, count + len(names)))
        count += len(names)
        for n in names:
            group_of[layer, n] = (i, names)
    gathered, token = _exchange_start(
        "gather", [shard[layer][n] for layer, names in travel for n in names], None, None, "w")
    zero = token[0, 0]

    biases = [_na_bias_table(na_rel_bias[layer] + zero) for layer in range(DEPTH)]

    def pass_on(i, behind):
        if i in chips_done:
            lands = _exchange_wait(gathered, behind, f"w{i}", which=chips_done.pop(i))
            sibling_done[i], _ = _exchange_start("forward", [], lands, None, f"p{i}")

    def fetcher(layer):
        def fetch(name, behind):
            if (layer, name) in group_of:
                i, names = group_of[layer, name]
                if i == 0:
                    behind = (behind, *biases)
                pass_on(i, behind)
                pass_on(i + 1, behind)
                for n, got in zip(names, _exchange_wait(sibling_done.pop(i), behind, f"p{i}")):
                    weights[layer][n] = got
                    del group_of[layer, n]
            return weights[layer][name]
        return fetch

    saved = []
    h = xs
    hn = _norm_fwd(xs, ffn1_norm[0] + zero, "first")
    for layer in range(DEPTH):
        bias = biases[layer]
        fetch = fetcher(layer)
        after_ffn2 = ffn1_norm[layer + 1] if layer + 1 < DEPTH else None
        h, hn, s1 = _ffn_forward(h, hn, fetch, ("up1", "down1"), f"f1l{layer}", mix_norm[layer])
        h, hn, s2 = _mixer_forward(h, hn, fetch, bias, tables, f"l{layer}", ffn2_norm[layer])
        h, hn, s3 = _ffn_forward(h, hn, fetch, ("up2", "down2"), f"f2l{layer}", after_ffn2)
        saved.append((s1, s2, s3, bias))
    loss_part, dh, dh_b, d_final = _loss_head(h, final_norm, tgt)

    d_norms = ([None] * DEPTH, [None] * DEPTH, [None] * DEPTH)
    d_bias = [None] * DEPTH
    sent = {}

    def sender(layer, suffix):
        def send(names, grads):
            tag = f"g{layer}{names[0]}{suffix}"
            handle, token = _exchange_start("scatter", grads, None, None, tag)
            for i, n in enumerate(names):
                sent[layer, n + suffix] = (handle, i, tag)
            return token
        return send

    for layer in reversed(range(DEPTH)):
        w = weights[layer]
        s1, s2, s3, bias = saved[layer]
        dh, dh_b, d_norms[2][layer] = _ffn_backward(
            dh, dh_b, s3, ffn2_norm[layer], w["up2"].reshape(2, F, D), w["down2"], f"f2l{layer}", sender(layer, "2"))
        dh, dh_b, d_norms[1][layer], d_bias[layer] = _mixer_backward(
            dh, dh_b, s2, mix_norm[layer], (w["win"], w["wa"], w["wb"], w["wo"]), bias, tables, f"l{layer}",
            sender(layer, ""))
        dh, dh_b, d_norms[0][layer] = _ffn_backward(
            dh, dh_b, s1, ffn1_norm[layer], w["up1"].reshape(2, F, D), w["down1"], f"f1l{layer}", sender(layer, "1"))
    grad_x = dh.reshape(x.shape)

    originals = dict(up1=(ffn1_w_up, m_ffn1_w_up, v_ffn1_w_up), down1=(ffn1_w_down, m_ffn1_w_down, v_ffn1_w_down),
                     win=(w_in, m_w_in, v_w_in), wa=(w_branch_a, m_w_branch_a, v_w_branch_a),
                     wb=(w_branch_b, m_w_branch_b, v_w_branch_b), wo=(w_out, m_w_out, v_w_out),
                     up2=(ffn2_w_up, m_ffn2_w_up, v_ffn2_w_up), down2=(ffn2_w_down, m_ffn2_w_down, v_ffn2_w_down))
    big = {}
    behind = dh
    landed = {}

    def received(layer, name):
        handle, i, tag = sent[layer, name]
        if tag not in landed:
            landed[tag] = _exchange_wait(handle, behind, tag)
        return landed[tag][i]

    for name in ("down2", "up2", "win", "wa", "wb", "wo", "down1", "up1"):
        wv, mv, vv = originals[name]
        if name in col_sharded:
            wv, mv, vv = (jnp.swapaxes(t, 1, 2) for t in (wv, mv, vv))
        big[name] = tuple(_sum_adamw(received(0, name), received(1, name), wv, mv, vv, name))
        behind = big[name][1]
        if name in col_sharded:
            big[name] = tuple(jnp.swapaxes(t, 1, 2) for t in big[name])

    small = _allreduce_small(_pack_small(d_norms, d_bias, d_final, loss_part[0, :1]), behind)
    g_norms, g_bias, g_final, loss = _unpack_small(small)
    w_small = _pack_small((ffn1_norm, mix_norm, ffn2_norm), na_rel_bias, final_norm)
    m_small = _pack_small((m_ffn1_norm, m_mix_norm, m_ffn2_norm), m_na_rel_bias, m_final_norm)
    v_small = _pack_small((v_ffn1_norm, v_mix_norm, v_ffn2_norm), v_na_rel_bias, v_final_norm)
    upd = _adamw(w_small[None], small[None], m_small[None], v_small[None], "small")
    small_out = [(g_norms, g_bias, g_final)] + [_unpack_small(u[0])[:3] for u in upd]

    outputs = [loss, grad_x]
    for kind in range(4):
        norms, bias_k, final_k = small_out[kind]
        outputs += [norms[0], big["up1"][kind], big["down1"][kind], norms[1], big["win"][kind], bias_k,
                    big["wa"][kind], big["wb"][kind], big["wo"][kind], norms[2], big["up2"][kind],
                    big["down2"][kind], final_k]
    return tuple(outputs)
```

```python
import numpy as np

import jax
import jax.numpy as jnp
from jax import lax
from jax.experimental import pallas as pl
from jax.experimental.pallas import tpu as pltpu

F32 = jnp.float32
BF16 = jnp.bfloat16
SDS = jax.ShapeDtypeStruct
BS = pl.BlockSpec
MESH = pl.DeviceIdType.MESH

D = 1024
S = 2048
F = 2816
DEPTH = 2
HEAD_DIM = 64
DILATIONS = (1, 4, 16)
HALF = 64
QKV_A = 2304
QKV_B = 1536
IN_W = 5888
N_DEV = 8
NA_ROWS = 32
GRID_W = 64
NA_KR = 8
ROPE_THETA = 10000.0
RMS_EPS = 1e-6
NEG = -1e30
SCALE = HEAD_DIM ** -0.5
ADAM_LR, ADAM_B1, ADAM_B2, ADAM_EPS, ADAM_WD, ADAM_STEP = 0.001, 0.9, 0.999, 1e-08, 0.01, 10
VMEM_LIMIT_V7X = 52 * 1024 * 1024
SMALL_ROWS = 120
BIAS_PAD = 3840
NA_FWD_ROWS = 8
NA_BWD_ROWS = 4
DIL_FWD_TILES = 8
DIL_BWD_TILES = 4


def _cp(*sem):
    return pltpu.CompilerParams(dimension_semantics=sem, vmem_limit_bytes=VMEM_LIMIT_V7X)


def _dot_nn(a, b):
    return jnp.dot(a, b, preferred_element_type=F32)


def _dot_nt(a, b):
    return lax.dot_general(a, b, (((1,), (1,)), ((), ())), preferred_element_type=F32)


def _dot_tn(a, b):
    return lax.dot_general(a, b, (((0,), (0,)), ((), ())), preferred_element_type=F32)


def _ds(start, size, stride):
    return pl.ds(start, size) if stride == 1 else pl.ds(start, size, stride=stride)


def _norm_fwd(x, g, tag):
    t = x.shape[0]
    tm = 512

    def body(x_ref, g_ref, o_ref):
        xv = x_ref[...]
        r = lax.rsqrt(jnp.mean(xv * xv, axis=-1, keepdims=True) + RMS_EPS)
        o_ref[...] = (xv * r * g_ref[...]).astype(BF16)

    return pl.pallas_call(
        body, name=f"norm_fwd_{tag}", grid=(t // tm,),
        in_specs=[BS((tm, D), lambda i: (i, 0)), BS((1, D), lambda i: (0, 0))],
        out_specs=BS((tm, D), lambda i: (i, 0)),
        out_shape=SDS((t, D), BF16), compiler_params=_cp("parallel"),
    )(x, g.reshape(1, D))


def _loss_head(x, g, tgt):
    t = x.shape[0]
    tm = 1024

    def body(x_ref, g_ref, t_ref, loss_ref, dx_ref, dxb_ref, dg_ref):
        @pl.when(pl.program_id(0) == 0)
        def _():
            dg_ref[...] = jnp.zeros_like(dg_ref)
            loss_ref[...] = jnp.zeros_like(loss_ref)

        xv = x_ref[...]
        gv = g_ref[...]
        r = lax.rsqrt(jnp.mean(xv * xv, axis=-1, keepdims=True) + RMS_EPS)
        xh = xv * r
        e = xh * gv - t_ref[...]
        loss_ref[...] += 0.5 * jnp.sum(jnp.mean(e * e, axis=-1, keepdims=True), axis=0, keepdims=True)
        dy = e * (1.0 / D)
        u = dy * gv
        dx = r * (u - xh * jnp.mean(xh * u, axis=-1, keepdims=True))
        dx_ref[...] = dx
        dxb_ref[...] = dx.astype(BF16)
        dg_ref[...] += jnp.sum(dy * xh, axis=0, keepdims=True)

    row = BS((tm, D), lambda i: (i, 0))
    vec = BS((1, D), lambda i: (0, 0))
    return pl.pallas_call(
        body, name="loss_head", grid=(t // tm,),
        in_specs=[row, vec, row], out_specs=[BS((1, 128), lambda i: (0, 0)), row, row, vec],
        out_shape=[SDS((1, 128), F32), SDS((t, D), F32), SDS((t, D), BF16), SDS((1, D), F32)],
        compiler_params=_cp("arbitrary"),
    )(x, g.reshape(1, D), tgt)


def _mm_nn(a, w, tag, res=None, scale=1.0, tm=1024, tn=None, next_g=None):
    c_n, t, k = a.shape
    n = w.shape[2]
    tn = n if tn is None else tn
    assert next_g is None or tn == n
    n_in = 2 + (res is not None) + (next_g is not None)

    def body(*refs):
        a_ref, w_ref = refs[0], refs[1]
        acc = _dot_nn(a_ref[0].astype(BF16), w_ref[0])
        for c in range(1, c_n):
            acc = acc + _dot_nn(a_ref[c].astype(BF16), w_ref[c])
        if scale != 1.0:
            acc = acc * scale
        if res is not None:
            acc = refs[2][...] + acc
        refs[n_in][...] = acc
        if next_g is not None:
            r = lax.rsqrt(jnp.mean(acc * acc, axis=-1, keepdims=True) + RMS_EPS)
            refs[n_in + 1][...] = (acc * r * refs[n_in - 1][...]).astype(BF16)

    w_mode = dict(pipeline_mode=pl.Buffered(1)) if tn == n else {}
    in_specs = [BS((c_n, tm, k), lambda i, j: (0, i, 0)), BS((c_n, k, tn), lambda i, j: (0, 0, j), **w_mode)]
    args = [a, w]
    out_specs = [BS((tm, tn), lambda i, j: (i, j))]
    out_shape = [SDS((t, n), F32)]
    if res is not None:
        in_specs.append(BS((tm, tn), lambda i, j: (i, j)))
        args.append(res)
    if next_g is not None:
        in_specs.append(BS((1, n), lambda i, j: (0, 0)))
        args.append(next_g.reshape(1, n))
        out_specs.append(BS((tm, tn), lambda i, j: (i, j)))
        out_shape.append(SDS((t, n), BF16))
    got = pl.pallas_call(
        body, name=f"mm_nn_{tag}", grid=(t // tm, n // tn), in_specs=in_specs, out_specs=out_specs,
        out_shape=out_shape, compiler_params=_cp("parallel", "parallel"),
    )(*args)
    return got if next_g is not None else got[0]


def _mm_nn_norm_bwd(parts, w, x, g, dres, tag, tm=512):
    t = parts[0].shape[1]
    n_parts = len(parts)

    def body(*refs):
        w_ref, x_ref, g_ref, dr_ref, dx_ref, dxb_ref, dg_ref = refs[n_parts:]

        @pl.when(pl.program_id(0) == 0)
        def _():
            dg_ref[...] = jnp.zeros_like(dg_ref)

        dh = None
        row = 0
        for a_ref, part in zip(refs, parts):
            for c in range(part.shape[0]):
                term = _dot_nn(a_ref[c].astype(BF16), w_ref[row:row + part.shape[2], :])
                dh = term if dh is None else dh + term
                row += part.shape[2]
        xv = x_ref[...]
        r = lax.rsqrt(jnp.mean(xv * xv, axis=-1, keepdims=True) + RMS_EPS)
        xh = xv * r
        u = dh * g_ref[...]
        dx = dr_ref[...] + r * (u - xh * jnp.mean(xh * u, axis=-1, keepdims=True))
        dx_ref[...] = dx
        dxb_ref[...] = dx.astype(BF16)
        dg_ref[...] += jnp.sum(dh * xh, axis=0, keepdims=True)

    row = BS((tm, D), lambda i: (i, 0))
    vec = BS((1, D), lambda i: (0, 0))
    return pl.pallas_call(
        body, name=f"mm_nn_norm_bwd_{tag}", grid=(t // tm,),
        in_specs=[BS((p.shape[0], tm, p.shape[2]), lambda i: (0, i, 0)) for p in parts]
        + [BS(w.shape, lambda i: (0, 0), pipeline_mode=pl.Buffered(1)), row, vec, row],
        out_specs=[row, row, vec], out_shape=[SDS((t, D), F32), SDS((t, D), BF16), SDS((1, D), F32)],
        compiler_params=_cp("arbitrary"),
    )(*parts, w, x, g.reshape(1, D), dres)


def _mm_nt_rows(a, w, tag, tm, tn, n_total, w_row0, rope=None):
    t, k = a.shape
    assert w_row0 % tn == 0 and n_total % tn == 0
    j0 = w_row0 // tn

    def body(a_ref, w_ref, *rest):
        o_ref = rest[-1]
        o_ref[...] = _dot_nt(a_ref[...].astype(BF16), w_ref[...])
        if rope is not None:
            @pl.when(pl.program_id(0) == 0)
            def _():
                c = rest[0][...]
                sg = rest[1][...]
                first = (lax.broadcasted_iota(jnp.int32, (tm, 128), 1) % HEAD_DIM) < HEAD_DIM // 2
                for col in range(0, rope[2], 128):
                    v = o_ref[:, col:col + 128]
                    o_ref[:, col:col + 128] = v * c + _swap_halves(v, first) * sg

    in_specs = [BS((tm, k), lambda j, i: (i, 0)), BS((tn, k), lambda j, i: (j0 + j, 0))]
    args = [a, w]
    if rope is not None:
        assert rope[2] <= tn
        in_specs += [BS((tm, 128), lambda j, i: (i % (S // tm), 0))] * 2
        args += [rope[0], rope[1]]
    return pl.pallas_call(
        body, name=f"mm_nt_{tag}", grid=(n_total // tn, t // tm), in_specs=in_specs,
        out_specs=BS((tm, tn), lambda j, i: (i, j)), out_shape=SDS((t, n_total), F32),
        compiler_params=_cp("parallel", "parallel"),
    )(*args)


def _mm_tn(a, b, tag, scale=1.0, tmm=None, into=None, row0=0, rows=None):
    c_n, t, m = a.shape
    n = b.shape[1]
    if tmm is None:
        tmm = max(w for w in (768, 512, 256) if m % w == 0 and row0 % w == 0)
    tiles = m // tmm
    block0 = row0 // tmm
    assert row0 % tmm == 0 and m % tmm == 0

    def body(a_ref, b_ref, *rest):
        rest[-1][...] = (_dot_tn(a_ref[...].astype(BF16), b_ref[...].astype(BF16)) * scale).astype(BF16)

    in_specs = [BS((None, t, tmm), lambda c, mi: (c, 0, mi)), BS((t, n), lambda c, mi: (0, 0))]
    args = [a, b]
    if into is not None:
        in_specs.append(BS(memory_space=pl.ANY))
        args.append(into)
    return pl.pallas_call(
        body, name=f"mm_tn_{tag}", grid=(c_n, tiles), in_specs=in_specs,
        out_specs=BS((tmm, n), lambda c, mi: (block0 + c * tiles + mi, 0)),
        out_shape=SDS((rows or c_n * m, n) if into is None else into.shape, BF16),
        input_output_aliases={} if into is None else {2: 0},
        compiler_params=_cp("parallel", "parallel"),
    )(*args)


def _ffn_up(hn, wut, tag):
    t = hn.shape[0]
    tm, tn = 512, 1408

    def body(h_ref, w_ref, gu_ref, act_ref):
        h = h_ref[...]
        g = _dot_nt(h, w_ref[0])
        u = _dot_nt(h, w_ref[1])
        sg = jax.nn.sigmoid(g)
        silu = g * sg
        gu_ref[0] = (u * (sg + silu * (1.0 - sg))).astype(BF16)
        gu_ref[1] = silu.astype(BF16)
        act_ref[...] = (silu * u).astype(BF16)

    return pl.pallas_call(
        body, name=f"ffn_up_{tag}", grid=(F // tn, t // tm),
        in_specs=[BS((tm, D), lambda j, i: (i, 0)), BS((2, tn, D), lambda j, i: (0, j, 0))],
        out_specs=[BS((2, tm, tn), lambda j, i: (0, i, j)), BS((tm, tn), lambda j, i: (i, j))],
        out_shape=[SDS((2, t, F), BF16), SDS((t, F), BF16)],
        compiler_params=_cp("parallel", "parallel"),
    )(hn, wut)


def _ffn_dact(dxo, wd, gu, tie, tag):
    t = dxo.shape[0]
    tm, tn = 512, 1408

    def body(d_ref, w_ref, gu_ref, tie_ref, o_ref):
        dact = _dot_nt(d_ref[...] * 0.5, w_ref[...])
        o_ref[0] = (dact * gu_ref[0].astype(F32)).astype(BF16)
        o_ref[1] = (dact * gu_ref[1].astype(F32)).astype(BF16)

    return pl.pallas_call(
        body, name=f"ffn_dact_{tag}", grid=(F // tn, t // tm),
        in_specs=[BS((tm, D), lambda j, i: (i, 0)), BS((tn, D), lambda j, i: (j, 0)),
                  BS((2, tm, tn), lambda j, i: (0, i, j)), BS((8, 128), lambda j, i: (0, 0))],
        out_specs=BS((2, tm, tn), lambda j, i: (0, i, j)),
        out_shape=SDS((2, t, F), BF16), compiler_params=_cp("parallel", "parallel"),
    )(dxo, wd, gu, tie)


def _rope_tables():
    half = HEAD_DIM // 2
    inv_freq = ROPE_THETA ** (-jnp.arange(half, dtype=F32) / half)
    ang = jnp.arange(S).astype(F32)[:, None] * inv_freq[None, :]
    cos, sin = jnp.cos(ang), jnp.sin(ang)
    return jnp.concatenate([cos, cos, cos, cos], axis=1), jnp.concatenate([-sin, sin, -sin, sin], axis=1)


def _swap_halves(t, first_half):
    return jnp.where(first_half, pltpu.roll(t, 96, 1), pltpu.roll(t, 32, 1))


def _rope_bwd(dqs, dks, dvs, cos_t, sin_t):
    t = dqs[0].shape[0]
    tm = 1024

    def body(*refs):
        c = refs[9][...]
        sg = refs[10][...]
        o_ref = refs[11]
        first = (lax.broadcasted_iota(jnp.int32, (tm, 128), 1) % HEAD_DIM) < HEAD_DIM // 2
        for a in range(6):
            for hp in range(2):
                v = refs[a][:, 128 * hp:128 * (hp + 1)]
                col = 128 * (2 * a + hp)
                o_ref[:, col:col + 128] = (v * c + _swap_halves(v * sg, first)).astype(BF16)
        for a in range(6, 9):
            o_ref[:, 256 * a:256 * (a + 1)] = refs[a][...].astype(BF16)

    blk = BS((tm, 256), lambda i: (i, 0))
    tab = BS((tm, 128), lambda i: (i % (S // tm), 0))
    return pl.pallas_call(
        body, name="rope_bwd", grid=(t // tm,), in_specs=[blk] * 9 + [tab, tab],
        out_specs=BS((None, tm, QKV_A), lambda i: (0, i, 0)), out_shape=SDS((1, t, QKV_A), BF16),
        compiler_params=_cp("parallel"),
    )(*dqs, *dks, *dvs, cos_t, sin_t)


def _head_masks():
    lane = lax.broadcasted_iota(jnp.int32, (1, 128), 1)
    m0 = (lane < HEAD_DIM).astype(F32)
    return m0, 1.0 - m0


def _dil_geometry(d):
    sub = S // d
    q_rows = 128
    k_rows = min(256, sub)
    return sub, q_rows, sub // q_rows, k_rows


def _dil_tile(idx, d, keys_on_rows=False):
    sub, q_rows, nb, k_rows = _dil_geometry(d)
    r = idx // nb
    n = idx % nb
    k_sub = jnp.clip(q_rows * n - HALF, 0, sub - k_rows)
    if d == 1:
        q_start = pl.multiple_of(q_rows * n, q_rows)
        k_start = pl.multiple_of(k_sub, HALF)
    else:
        q_start = q_rows * n * d + r
        k_start = k_sub * d + r
    if keys_on_rows:
        ii = lax.broadcasted_iota(jnp.int32, (k_rows, 2 * q_rows), 1) % q_rows
        jj = lax.broadcasted_iota(jnp.int32, (k_rows, 2 * q_rows), 0)
    else:
        ii = lax.broadcasted_iota(jnp.int32, (q_rows, k_rows), 0)
        jj = lax.broadcasted_iota(jnp.int32, (q_rows, k_rows), 1)
    valid = jnp.abs(jj - ii + (k_sub - q_rows * n)) <= HALF
    return q_start, k_start, valid


def _dil_specs(grp):
    qs = BS((S, 128), lambda b, hp: (b, 2 * grp + hp))
    ks = BS((S, 128), lambda b, hp: (b, 6 + 2 * grp + hp))
    vs = BS((S, 128), lambda b, hp: (b, 12 + 2 * grp + hp))
    own = BS((S, 128), lambda b, hp: (b, hp))
    return qs, ks, vs, own


def _dil_fwd(qkr, proj, grp):
    t = qkr.shape[0]
    d = DILATIONS[grp]
    _, q_rows, nb, k_rows = _dil_geometry(d)

    def body(q_ref, k_ref, v_ref, o_ref, l_ref):
        masks = _head_masks()

        def step(i0, carry):
            geo = [_dil_tile(i0 * DIL_FWD_TILES + j, d) for j in range(DIL_FWD_TILES)]
            tiles = [(j, h) for j in range(DIL_FWD_TILES) for h in range(2)]
            qs = [q_ref[_ds(g[0], q_rows, d), :] for g in geo]
            kbs = [k_ref[_ds(g[1], k_rows, d), :].astype(BF16) for g in geo]
            ss = [jnp.where(geo[j][2], _dot_nt((qs[j] * masks[h]).astype(BF16), kbs[j]) * SCALE, NEG) for j, h in tiles]
            mxs = [jnp.max(s, axis=1, keepdims=True) for s in ss]
            ps = [jnp.exp(s - mx) for s, mx in zip(ss, mxs)]
            dens = [jnp.sum(p, axis=1, keepdims=True) for p in ps]
            vs = [v_ref[_ds(g[1], k_rows, d), :] for g in geo]
            outs = [_dot_nn(p.astype(BF16), (vs[j] * masks[h]).astype(BF16)) / den
                    for p, den, (j, h) in zip(ps, dens, tiles)]
            for j, g in enumerate(geo):
                o_ref[_ds(g[0], q_rows, d), :] = outs[2 * j] + outs[2 * j + 1]
                l_ref[_ds(g[0], q_rows, d), :] = (
                    (mxs[2 * j] + jnp.log(dens[2 * j])) * masks[0] + (mxs[2 * j + 1] + jnp.log(dens[2 * j + 1])) * masks[1])
            return carry

        lax.fori_loop(0, d * nb // DIL_FWD_TILES, step, 0)

    qs, ks, vs, own = _dil_specs(grp)
    return pl.pallas_call(
        body, name=f"dil_fwd_{grp}", grid=(t // S, 2), in_specs=[qs, ks, vs], out_specs=[own, own],
        out_shape=[SDS((t, 256), F32), SDS((t, 256), F32)], compiler_params=_cp("parallel", "parallel"),
    )(qkr, qkr, proj)


def _dil_bwd(qkr, proj, do, dlp, lse, grp):
    t = qkr.shape[0]
    d = DILATIONS[grp]
    _, q_rows, nb, k_rows = _dil_geometry(d)

    def body(q_ref, k_ref, v_ref, do_ref, dl_ref, l_ref, dq_ref, dk_ref, dv_ref):
        masks = _head_masks()
        dk_ref[...] = jnp.zeros_like(dk_ref)
        dv_ref[...] = jnp.zeros_like(dv_ref)

        def as_row(x2):
            xt = x2.T
            return jnp.concatenate([xt[0:1], xt[HEAD_DIM:HEAD_DIM + 1]], axis=1)

        def step(i0, carry):
            geo = [_dil_tile(i0 * DIL_BWD_TILES + j, d, keys_on_rows=True) for j in range(DIL_BWD_TILES)]
            q_ds = [_ds(g[0], q_rows, d) for g in geo]
            k_ds = [_ds(g[1], k_rows, d) for g in geo]
            qbs = [_both_heads(q_ref[r, :], masks).astype(BF16) for r in q_ds]
            kbs = [k_ref[r, :].astype(BF16) for r in k_ds]
            vbs = [v_ref[r, :].astype(BF16) for r in k_ds]
            dobs = [_both_heads(do_ref[r, :], masks).astype(BF16) for r in q_ds]
            l_rows = [as_row(l_ref[r, :]) for r in q_ds]
            dl_rows = [as_row(dl_ref[r, :]) for r in q_ds]
            ss = [jnp.where(g[2], _dot_nt(kb, qb) * SCALE, NEG) for g, kb, qb in zip(geo, kbs, qbs)]
            ps = [jnp.exp(s - lr) for s, lr in zip(ss, l_rows)]
            dps = [_dot_nt(vb, dob) for vb, dob in zip(vbs, dobs)]
            dss = [(p * (dp - dr)).astype(BF16) for p, dp, dr in zip(ps, dps, dl_rows)]
            dks = [_dot_nn(ds, qb) for ds, qb in zip(dss, qbs)]
            dvs = [_dot_nn(p.astype(BF16), dob) for p, dob in zip(ps, dobs)]
            dqs = [_own_heads(_dot_tn(ds, kb), masks) for ds, kb in zip(dss, kbs)]
            for j in range(DIL_BWD_TILES):
                dq_ref[q_ds[j], :] = dqs[j] * SCALE
                dk_ref[k_ds[j], :] += dks[j] * SCALE
                dv_ref[k_ds[j], :] += dvs[j]
            return carry

        lax.fori_loop(0, d * nb // DIL_BWD_TILES, step, 0)

    qs, ks, vs, own = _dil_specs(grp)
    return pl.pallas_call(
        body, name=f"dil_bwd_{grp}", grid=(t // S, 2), in_specs=[qs, ks, vs, own, own, own],
        out_specs=[own, own, own], out_shape=[SDS((t, 256), F32)] * 3,
        compiler_params=_cp("parallel", "parallel"),
    )(qkr, qkr, proj, do, dlp, lse)


def _mix_weights(l0, l1, l2):
    mx = jnp.maximum(jnp.maximum(l0, l1), l2)
    e0, e1, e2 = jnp.exp(l0 - mx), jnp.exp(l1 - mx), jnp.exp(l2 - mx)
    den = e0 + e1 + e2
    return e0 / den, e1 / den, e2 / den


def _combine_fwd(outs, lses):
    t = outs[0].shape[0]
    tm = 1024

    def body(o0, o1, o2, l0, l1, l2, y_ref):
        w0, w1, w2 = _mix_weights(l0[...], l1[...], l2[...])
        y_ref[...] = w0 * o0[...] + w1 * o1[...] + w2 * o2[...]

    blk = BS((tm, 256), lambda i: (i, 0))
    return pl.pallas_call(
        body, name="combine_fwd", grid=(t // tm,), in_specs=[blk] * 6, out_specs=blk,
        out_shape=SDS((t, 256), F32), compiler_params=_cp("parallel"),
    )(*outs, *lses)


def _head_sum(x):
    a = lax.broadcasted_iota(jnp.int32, (256, 256), 0) // HEAD_DIM
    b = lax.broadcasted_iota(jnp.int32, (256, 256), 1) // HEAD_DIM
    ones = (a == b).astype(BF16)
    hi = x.astype(BF16)
    lo = (x - hi.astype(F32)).astype(BF16)
    return _dot_nn(hi, ones) + _dot_nn(lo, ones)


def _combine_bwd(dya, outs, lses):
    t = dya.shape[0]
    tm = 1024

    def body(dy_ref, o0, o1, o2, l0, l1, l2, d0, d1, d2, e0, e1, e2):
        ws = _mix_weights(l0[...], l1[...], l2[...])
        dy = dy_ref[...]
        ya = ws[0] * o0[...] + ws[1] * o1[...] + ws[2] * o2[...]
        hs = _head_sum(dy * ya)
        for w, d_ref, e_ref in zip(ws, (d0, d1, d2), (e0, e1, e2)):
            d_ref[...] = w * dy
            e_ref[...] = w * hs

    blk = BS((tm, 256), lambda i: (i, 0))
    return pl.pallas_call(
        body, name="combine_bwd", grid=(t // tm,), in_specs=[blk] * 7, out_specs=[blk] * 6,
        out_shape=[SDS((t, 256), F32)] * 6, compiler_params=_cp("parallel"),
    )(dya, *outs, *lses)


def _na_bias_table(rel_bias):
    kw = NA_KR * GRID_W
    rev = jnp.pad(rel_bias.astype(F32)[:, :, ::-1], ((0, 0), (0, 0), (0, 128 - 31)))

    def body(r_ref, o_ref):
        lane = lax.broadcasted_iota(jnp.int32, (GRID_W, 128), 1)
        j = lax.broadcasted_iota(jnp.int32, (GRID_W, 128), 0)
        q = lane % GRID_W
        win_lo = jnp.clip(q - 8, 0, GRID_W - 16)
        valid = (j >= win_lo) & (j < win_lo + 16)
        for cls in range(NA_KR):
            for k in range(NA_KR):
                tiles = []
                for h in range(2):
                    row = jnp.broadcast_to(r_ref[h, cls + k:cls + k + 1, :], (GRID_W, 128))
                    tiles.append(pltpu.roll(row, (128 - 15 + GRID_W * h) % 128, 1, stride=1, stride_axis=0))
                o_ref[cls, GRID_W * k:GRID_W * (k + 1), :] = jnp.where(
                    valid, jnp.where(lane < GRID_W, tiles[0], tiles[1]), NEG)

    return pl.pallas_call(
        body, name="na_bias_table", grid=(4,),
        in_specs=[BS((2, 2 * NA_KR - 1, 128), lambda hp: (hp, 0, 0))],
        out_specs=BS((None, NA_KR, kw, 128), lambda hp: (hp, 0, 0, 0)),
        out_shape=SDS((4, NA_KR, kw, 128), F32), compiler_params=_cp("parallel"),
    )(rev)


def _na_row(i):
    lo = jnp.clip(i - NA_KR // 2, 0, NA_ROWS - NA_KR)
    return pl.multiple_of(GRID_W * i, GRID_W), pl.multiple_of(GRID_W * lo, GRID_W), lo - i + NA_KR - 1


def _both_heads(x, masks):
    return jnp.concatenate([x * masks[0], x * masks[1]], axis=0)


def _own_heads(r, masks):
    half = r.shape[0] // 2
    return r[:half] * masks[0] + r[half:] * masks[1]


def _na_fwd(proj, bias):
    t = proj.shape[0]
    kw = NA_KR * GRID_W

    def body(q_ref, k_ref, v_ref, b_ref, o_ref, l_ref):
        masks = _head_masks()

        def step(i0, carry):
            idx = [i0 * NA_FWD_ROWS + j for j in range(NA_FWD_ROWS)]
            rows = [_na_row(i) for i in idx]
            qbs = [_both_heads(q_ref[pl.ds(r[0], GRID_W), :], masks).astype(BF16) for r in rows]
            kbs = [k_ref[pl.ds(r[1], kw), :].astype(BF16) for r in rows]
            ss = [_dot_nt(kb, qb) * SCALE + b_ref[r[2]] for kb, qb, r in zip(kbs, qbs, rows)]
            mxs = [jnp.max(s, axis=0, keepdims=True) for s in ss]
            ps = [jnp.exp(s - mx) for s, mx in zip(ss, mxs)]
            dens = [jnp.sum(p, axis=0, keepdims=True) for p in ps]
            pbs = [(p / den).astype(BF16) for p, den in zip(ps, dens)]
            vbs = [v_ref[pl.ds(r[1], kw), :].astype(BF16) for r in rows]
            outs = [_own_heads(_dot_tn(pb, vb), masks) for pb, vb in zip(pbs, vbs)]
            for j, r in enumerate(rows):
                o_ref[pl.ds(r[0], GRID_W), :] = outs[j]
                l_ref[pl.ds(idx[j], 1), :] = mxs[j] + jnp.log(dens[j])
            return carry

        lax.fori_loop(0, NA_ROWS // NA_FWD_ROWS, step, 0)

    c0 = QKV_A // 128
    return pl.pallas_call(
        body, name="na_fwd", grid=(t // S, 4),
        in_specs=[BS((S, 128), lambda b, hp: (b, c0 + hp)), BS((S, 128), lambda b, hp: (b, c0 + 4 + hp)),
                  BS((S, 128), lambda b, hp: (b, c0 + 8 + hp)),
                  BS((None, NA_KR, kw, 128), lambda b, hp: (hp, 0, 0, 0))],
        out_specs=[BS((S, 128), lambda b, hp: (b, hp)), BS((None, None, NA_ROWS, 128), lambda b, hp: (b, hp, 0, 0))],
        out_shape=[SDS((t, 512), F32), SDS((t // S, 4, NA_ROWS, 128), F32)],
        compiler_params=_cp("parallel", "parallel"),
    )(proj, proj, proj, bias)


def _na_bwd(proj, bias, dyb, yb, lse):
    t = proj.shape[0]
    kw = NA_KR * GRID_W

    def body(q_ref, k_ref, v_ref, b_ref, do_ref, o_ref, l_ref, d_ref, db_ref):
        masks = _head_masks()
        ones = jnp.ones((8, 128), BF16)

        @pl.when(pl.program_id(1) == 0)
        def _():
            db_ref[...] = jnp.zeros_like(db_ref)

        d_ref[1:3] = jnp.zeros((2, S, 128), F32)

        def row_sums(x):
            hi = x.astype(BF16)
            lo = (x - hi.astype(F32)).astype(BF16)
            return (_dot_nt(ones, hi) + _dot_nt(ones, lo))[0:1]

        def step(i0, carry):
            idx = [i0 * NA_BWD_ROWS + j for j in range(NA_BWD_ROWS)]
            rows = [_na_row(i) for i in idx]
            q_ds = [pl.ds(r[0], GRID_W) for r in rows]
            k_ds = [pl.ds(r[1], kw) for r in rows]
            qbs = [_both_heads(q_ref[r, :], masks).astype(BF16) for r in q_ds]
            kbs = [k_ref[r, :].astype(BF16) for r in k_ds]
            vbs = [v_ref[r, :].astype(BF16) for r in k_ds]
            dos = [do_ref[r, :] for r in q_ds]
            dobs = [_both_heads(do, masks).astype(BF16) for do in dos]
            deltas = [row_sums(_both_heads(do * o_ref[r, :], masks)) for do, r in zip(dos, q_ds)]
            ss = [_dot_nt(kb, qb) * SCALE + b_ref[r[2]] for kb, qb, r in zip(kbs, qbs, rows)]
            ps = [jnp.exp(s - l_ref[pl.ds(i, 1), :]) for s, i in zip(ss, idx)]
            dps = [_dot_nt(vb, dob) for vb, dob in zip(vbs, dobs)]
            dss = [p * (dp - delta) for p, dp, delta in zip(ps, dps, deltas)]
            for ds, r in zip(dss, rows):
                db_ref[r[2]] += ds
            dsbs = [ds.astype(BF16) for ds in dss]
            dks = [_dot_nn(dsb, qb) for dsb, qb in zip(dsbs, qbs)]
            dvs = [_dot_nn(p.astype(BF16), dob) for p, dob in zip(ps, dobs)]
            dqs = [_own_heads(_dot_tn(dsb, kb), masks) for dsb, kb in zip(dsbs, kbs)]
            for j in range(NA_BWD_ROWS):
                d_ref[0, q_ds[j], :] = dqs[j] * SCALE
                d_ref[1, k_ds[j], :] += dks[j] * SCALE
                d_ref[2, k_ds[j], :] += dvs[j]
            return carry

        lax.fori_loop(0, NA_ROWS // NA_BWD_ROWS, step, 0)

    c0 = QKV_A // 128
    own = BS((S, 128), lambda hp, b: (b, hp))
    tab = BS((None, NA_KR, kw, 128), lambda hp, b: (hp, 0, 0, 0))
    return pl.pallas_call(
        body, name="na_bwd", grid=(4, t // S),
        in_specs=[BS((S, 128), lambda hp, b: (b, c0 + hp)), BS((S, 128), lambda hp, b: (b, c0 + 4 + hp)),
                  BS((S, 128), lambda hp, b: (b, c0 + 8 + hp)), tab, own, own,
                  BS((None, None, NA_ROWS, 128), lambda hp, b: (b, hp, 0, 0))],
        out_specs=[BS((3, S, 128), lambda hp, b: (0, b, hp)), tab],
        out_shape=[SDS((3, t, 512), F32), SDS((4, NA_KR, kw, 128), F32)],
        compiler_params=_cp("parallel", "arbitrary"),
    )(proj, proj, proj, bias, dyb, yb, lse)


def _na_dbias_lane_map():
    kw = NA_KR * GRID_W
    lane = np.arange(kw)
    blk, m = lane // GRID_W, lane % GRID_W
    target = np.full(kw, -1)
    target[m < 16] = (blk * 32 + 15 + m)[m < 16]
    target[m >= 49] = (((blk + 1) % NA_KR) * 32 + m - 49)[m >= 49]
    return jnp.asarray(target[:, None] == np.arange(kw)[None, :], BF16)


def _na_dbias(db):
    kw = NA_KR * GRID_W

    def body(x_ref, map_ref, o_ref, z_ref):
        for cls in range(NA_KR):
            xt = x_ref[cls].T
            for h in range(2):
                xv = xt[GRID_W * h:GRID_W * (h + 1)]
                y = xv[0:8]
                for g in range(1, GRID_W // 8):
                    y = y + pltpu.roll(xv[8 * g:8 * g + 8], kw - 8 * g, 1)
                d = y[0:1]
                for s in range(1, 8):
                    d = d + pltpu.roll(y[s:s + 1], kw - s, 1)
                z_ref[h, cls:cls + 1, :] = d
        for h in range(2):
            z = z_ref[h]
            hi = z.astype(BF16)
            lo = (z - hi.astype(F32)).astype(BF16)
            e = _dot_nn(hi, map_ref[...]) + _dot_nn(lo, map_ref[...])
            out = e[0:1]
            for cls in range(1, NA_KR):
                out = out + pltpu.roll(e[cls:cls + 1], 32 * cls, 1)
            o_ref[h] = jnp.broadcast_to(out, (8, kw))

    return pl.pallas_call(
        body, name="na_dbias", grid=(4,),
        in_specs=[BS((None, NA_KR, kw, 128), lambda hp: (hp, 0, 0, 0)), BS((kw, kw), lambda hp: (0, 0))],
        out_specs=BS((2, 8, kw), lambda hp: (hp, 0, 0)), out_shape=SDS((8, 8, kw), F32),
        scratch_shapes=[pltpu.VMEM((2, 8, kw), F32)], compiler_params=_cp("parallel"),
    )(db, _na_dbias_lane_map())


def _merge_fwd(ya, yb, proj, wat, wbt):
    t = ya.shape[0]
    tm, tn = 1024, 256
    ca = (QKV_A + QKV_B) // tn
    cb = ca + D // tn

    def body(ya_ref, yb_ref, la_ref, lb_ref, wa_ref, wb_ref, m_ref, za_ref, zb_ref):
        za = _dot_nt(ya_ref[...].astype(BF16), wa_ref[...])
        zb = _dot_nt(yb_ref[...].astype(BF16), wb_ref[...])
        m_ref[...] = (jax.nn.sigmoid(la_ref[...]) * za + jax.nn.sigmoid(lb_ref[...]) * zb).astype(BF16)
        za_ref[...] = za.astype(BF16)
        zb_ref[...] = zb.astype(BF16)

    out = BS((tm, tn), lambda i, j: (i, j))
    return pl.pallas_call(
        body, name="merge_fwd", grid=(t // tm, D // tn),
        in_specs=[BS((tm, 256), lambda i, j: (i, 0)), BS((tm, 512), lambda i, j: (i, 0)),
                  BS((tm, tn), lambda i, j: (i, ca + j)), BS((tm, tn), lambda i, j: (i, cb + j)),
                  BS((tn, 256), lambda i, j: (j, 0)), BS((tn, 512), lambda i, j: (j, 0))],
        out_specs=[out, out, out], out_shape=[SDS((t, D), BF16)] * 3,
        compiler_params=_cp("parallel", "parallel"),
    )(ya, yb, proj, proj, wat, wbt)


def _merge_bwd(dxo, wo, za, zb, proj):
    t = dxo.shape[0]
    tm, tn = 1024, 256
    ca = (QKV_A + QKV_B) // tn
    cb = ca + D // tn

    def body(d_ref, w_ref, za_ref, zb_ref, la_ref, lb_ref, dza_ref, dzb_ref, dl_ref):
        dmv = _dot_nt(d_ref[...], w_ref[...])
        ga = jax.nn.sigmoid(la_ref[...])
        gb = jax.nn.sigmoid(lb_ref[...])
        dza_ref[...] = (dmv * ga).astype(BF16)
        dzb_ref[...] = (dmv * gb).astype(BF16)
        dl_ref[0] = (dmv * za_ref[...].astype(F32) * ga * (1.0 - ga)).astype(BF16)
        dl_ref[1] = (dmv * zb_ref[...].astype(F32) * gb * (1.0 - gb)).astype(BF16)

    blk = BS((tm, tn), lambda i, j: (i, j))
    return pl.pallas_call(
        body, name="merge_bwd", grid=(t // tm, D // tn),
        in_specs=[BS((tm, D), lambda i, j: (i, 0)), BS((tn, D), lambda i, j: (j, 0)), blk, blk,
                  BS((tm, tn), lambda i, j: (i, ca + j)), BS((tm, tn), lambda i, j: (i, cb + j))],
        out_specs=[blk, blk, BS((2, tm, tn), lambda i, j: (0, i, j))],
        out_shape=[SDS((t, D), BF16), SDS((t, D), BF16), SDS((2, t, D), BF16)],
        compiler_params=_cp("parallel", "parallel"),
    )(dxo, wo, za, zb, proj, proj)


def _adamw_update(w, g, m, v):
    mn = ADAM_B1 * m + (1.0 - ADAM_B1) * g
    vn = ADAM_B2 * v + (1.0 - ADAM_B2) * (g * g)
    m_hat = mn / (1.0 - ADAM_B1 ** ADAM_STEP)
    v_hat = vn / (1.0 - ADAM_B2 ** ADAM_STEP)
    return -ADAM_LR * (m_hat / (jnp.sqrt(v_hat) + ADAM_EPS) + ADAM_WD * w), mn, vn


def _sum_adamw(recv0, recv1, w, m, v, tag):
    _, r, c = recv0.shape
    tr = max(rows for rows in range(16, r + 1, 16) if r % rows == 0 and rows * c <= 192 * 1024)

    def body(a_ref, b_ref, w_ref, m_ref, v_ref, g_ref, d_ref, mo_ref, vo_ref):
        for layer, ref in enumerate((a_ref, b_ref)):
            g = ref[0].astype(F32)
            for s in range(1, N_DEV):
                g = g + ref[s].astype(F32)
            g_ref[layer] = g
            d_ref[layer], mo_ref[layer], vo_ref[layer] = _adamw_update(w_ref[layer], g, m_ref[layer], v_ref[layer])

    slots = BS((N_DEV, tr, c), lambda i: (0, i, 0))
    blk = BS((2, tr, c), lambda i: (0, i, 0))
    return pl.pallas_call(
        body, name=f"sum_adamw_{tag}", grid=(r // tr,), in_specs=[slots, slots, blk, blk, blk],
        out_specs=[blk] * 4, out_shape=[SDS((2, r, c), F32)] * 4, compiler_params=_cp("parallel"),
    )(recv0, recv1, w, m, v)


def _adamw(w, g, m, v, tag):
    layers, r, c = w.shape
    tr = next(r // k for k in (1, 2, 4, 8) if r // k <= 384 and r % (8 * k) == 0)

    def body(w_ref, g_ref, m_ref, v_ref, d_ref, mo_ref, vo_ref):
        d_ref[...], mo_ref[...], vo_ref[...] = _adamw_update(w_ref[...], g_ref[...], m_ref[...], v_ref[...])

    blk = BS((None, tr, c), lambda l, i: (l, i, 0))
    return pl.pallas_call(
        body, name=f"adamw_{tag}", grid=(layers, r // tr), in_specs=[blk] * 4, out_specs=[blk] * 3,
        out_shape=[SDS((layers, r, c), F32)] * 3, compiler_params=_cp("parallel", "parallel"),
    )(w, g, m, v)


def _place():
    return lax.axis_index("x"), lax.axis_index("y"), lax.axis_index("c")


def _flip(coord, bit):
    return 1 - coord if bit else coord


def _peers(x, y, c):
    peers = []
    for mask in range(1, N_DEV):
        p = (_flip(x, mask & 4), _flip(y, mask & 2), _flip(c, mask & 1))
        peers.append((p, 4 * p[0] + 2 * p[1] + p[2]))
    return peers


def _copy_plan(mode, src, land, x, y, c):
    me = 4 * x + 2 * y + c

    def device(mask):
        p = (_flip(x, mask & 4), _flip(y, mask & 2), _flip(c, mask & 1))
        return p, 4 * p[0] + 2 * p[1] + p[2]

    if mode == "scatter":
        r = land.shape[1]
        return [(p, src.at[pl.ds(i * r, r), :], land.at[me], land.at[i])
                for p, i in map(device, (1, 2, 3, 4, 5, 6, 7, 0))]
    r = land.shape[0] // N_DEV

    def rows(i):
        return land.at[pl.ds(i * r, r), :]

    if mode == "gather":
        return [(p, src, rows(me), rows(i)) for p, i in map(device, (1, 4, 2, 6, 0))]
    sibling = device(1)[0]
    return [(sibling, rows(device(m)[1]), rows(device(m)[1]), rows(device(m | 1)[1])) for m in (4, 2, 6)]


COPIES = dict(scatter=8, gather=5, forward=3)
HBM_SPEC = BS(memory_space=pltpu.HBM)
SEM_SPEC = BS(memory_space=pltpu.SEMAPHORE)
DATAFLOW = pltpu.SideEffectType.DATAFLOW_SIDE_EFFECTING


def _fresh(shape, dtype, tag):
    def body(o_ref):
        del o_ref

    return pl.pallas_call(body, name=f"fresh_{tag}", out_specs=BS(memory_space=pl.ANY), out_shape=SDS(shape, dtype))()


def _exchange_start(mode, srcs, lands, after, tag):
    if lands is None and mode == "gather":
        lands = [_fresh((N_DEV * s.shape[0], s.shape[1]), s.dtype, f"{tag}_{a}") for a, s in enumerate(srcs)]
    elif lands is None:
        lands = [_fresh((N_DEV, s.shape[0] // N_DEV, s.shape[1]), s.dtype, f"{tag}_{a}") for a, s in enumerate(srcs)]
    n, n_src, n_cp = len(lands), len(srcs), COPIES[mode]
    behind = [] if after is None else [after]

    def body(*refs):
        src_refs, land_refs = refs[:n_src], refs[n_src:n_src + n]
        send_sems, recv_sems = refs[n_src + n + len(behind)], refs[n_src + n + len(behind) + 1]
        token = refs[-1]
        for a in range(n):
            plan = _copy_plan(mode, src_refs[a] if n_src else None, land_refs[a], *_place())
            for k, (p, out, there, _) in enumerate(plan):
                pltpu.make_async_remote_copy(
                    src_ref=out, dst_ref=there, send_sem=send_sems.at[n_cp * a + k],
                    recv_sem=recv_sems.at[n_cp * a + k], device_id=p, device_id_type=MESH).start()
        token[...] = jnp.zeros_like(token)

    both = [*srcs, *lands]
    res = pl.pallas_call(
        body, name=f"{mode}_start_{tag}",
        out_shape=(pltpu.SemaphoreType.DMA((n_cp * n,)), pltpu.SemaphoreType.DMA((n_cp * n,)),
                   *[pltpu.HBM(v.shape, v.dtype) for v in both], SDS((8, 128), F32)),
        in_specs=[HBM_SPEC] * len(both) + [BS(memory_space=pl.ANY)] * len(behind),
        out_specs=(SEM_SPEC, SEM_SPEC, *[HBM_SPEC] * len(both), BS(memory_space=pltpu.VMEM)),
        input_output_aliases={i: 2 + i for i in range(len(both))},
        compiler_params=pltpu.CompilerParams(has_side_effects=DATAFLOW),
    )(*[pltpu.with_memory_space_constraint(v, pltpu.HBM) for v in both], *behind)
    return (mode, res[0], res[1], res[2:2 + n_src], res[2 + n_src:2 + n_src + n]), res[-1]


def _exchange_wait(handle, after, tag, which=None):
    mode, send_sems, recv_sems, srcs, lands = handle
    which = list(range(len(lands))) if which is None else list(which)
    n_cp = COPIES[mode]
    lands = [lands[a] for a in which]
    srcs = [srcs[a] for a in which] if srcs else []
    n, n_src = len(lands), len(srcs)
    afters = list(after) if isinstance(after, (tuple, list)) else [after]

    def body(*refs):
        src_refs, land_refs = refs[:n_src], refs[n_src:n_src + n]
        send_ref, recv_ref = refs[n_src + n], refs[n_src + n + 1]
        for i, a in enumerate(which):
            plan = _copy_plan(mode, src_refs[i] if n_src else None, land_refs[i], *_place())
            for k, (p, out, _, here) in enumerate(plan):
                cp = pltpu.make_async_remote_copy(
                    src_ref=out, dst_ref=here, send_sem=send_ref.at[n_cp * a + k], recv_sem=recv_ref.at[n_cp * a + k],
                    device_id=p, device_id_type=MESH)
                cp.wait_send()
                cp.wait_recv()

    both = [*srcs, *lands]
    res = pl.pallas_call(
        body, name=f"{mode}_wait_{tag}", out_shape=tuple(pltpu.HBM(v.shape, v.dtype) for v in both),
        in_specs=[HBM_SPEC] * len(both) + [SEM_SPEC, SEM_SPEC] + [BS(memory_space=pl.ANY)] * len(afters),
        out_specs=tuple([HBM_SPEC] * len(both)),
        input_output_aliases={i: i for i in range(len(both))},
        compiler_params=pltpu.CompilerParams(has_side_effects=DATAFLOW),
    )(*both, send_sems, recv_sems, *afters)
    return list(res[n_src:])


def _allreduce_small(vec, behind):
    rows = vec.shape[0]

    def body(x_ref, behind_ref, o_ref, buf_ref, send_sems, recv_sems):
        x, y, c = _place()
        me = 4 * x + 2 * y + c
        buf_ref[me] = x_ref[...]
        peers = _peers(x, y, c)

        def copy(k, slot):
            return pltpu.make_async_remote_copy(
                src_ref=x_ref, dst_ref=buf_ref.at[slot], send_sem=send_sems.at[k], recv_sem=recv_sems.at[k],
                device_id=peers[k][0], device_id_type=MESH)

        sends = [copy(k, me) for k in range(N_DEV - 1)]
        for cp in sends:
            cp.start()
        for k in range(N_DEV - 1):
            copy(k, peers[k][1]).wait_recv()
        for cp in sends:
            cp.wait_send()
        acc = buf_ref[0]
        for s in range(1, N_DEV):
            acc = acc + buf_ref[s]
        o_ref[...] = acc

    vmem = BS(memory_space=pltpu.VMEM)
    return pl.pallas_call(
        body, name="allreduce_small", in_specs=[vmem, BS(memory_space=pl.ANY)], out_specs=vmem,
        out_shape=SDS((rows, 128), F32),
        scratch_shapes=[pltpu.VMEM((N_DEV, rows, 128), F32), pltpu.SemaphoreType.DMA((7,)),
                        pltpu.SemaphoreType.DMA((7,))],
        compiler_params=pltpu.CompilerParams(has_side_effects=True),
    )(vec, behind)


def _ffn_forward(x, hn, fetch, names, tag, next_g):
    gu, act = _ffn_up(hn, fetch(names[0], hn).reshape(2, F, D), tag)
    got = _mm_nn(act[None], fetch(names[1], act)[None], f"down_{tag}", res=x, scale=0.5, tm=512, next_g=next_g)
    out, hn_next = got if next_g is not None else (got, None)
    return out, hn_next, (x, hn, gu, act)


def _ffn_backward(dxo, dxo_b, saved, norm_g, wut, wd, tag, send):
    x, hn, gu, act = saved
    d_wd = _mm_tn(act[None], dxo_b, f"dwd_{tag}", scale=0.5)
    du = _ffn_dact(dxo_b, wd, gu, send(("down",), [d_wd]), tag)
    d_wut = _mm_tn(du, hn, f"dwu_{tag}")
    token = send(("up",), [d_wut])
    return _mm_nn_norm_bwd([du], wut.reshape(2 * F, D), x, norm_g + token[0, 0], dxo, tag, tm=256)


def _mixer_forward(x, hn, fetch, bias, tables, tag, next_g):
    proj = _mm_nt_rows(hn, fetch("win", hn), f"proj_{tag}", 512, IN_W // 2, IN_W, 0, rope=(*tables, 2 * QKV_A // 3))
    qkr = proj
    outs, lses = [], []
    for grp in range(3):
        o, l = _dil_fwd(qkr, proj, grp)
        outs.append(o)
        lses.append(l)
    ya = _combine_fwd(outs, lses)
    yb, lse_b = _na_fwd(proj, bias)
    merged, za, zb = _merge_fwd(ya, yb, proj, fetch("wa", yb), fetch("wb", yb))
    out, hn_next = _mm_nn(merged[None], fetch("wo", merged)[None], f"out_{tag}", res=x, next_g=next_g)
    return out, hn_next, (x, hn, proj, qkr, outs, lses, ya, yb, lse_b, merged, za, zb)


def _mixer_backward(dxo, dxo_b, saved, norm_g, w, bias, tables, tag, send):
    wint, wat, wbt, wo = w
    x, hn, proj, qkr, outs, lses, ya, yb, lse_b, merged, za, zb = saved
    d_wo = _mm_tn(merged[None], dxo_b, f"dwo_{tag}")
    dza, dzb, dlog = _merge_bwd(dxo_b, wo, za, zb, proj)
    dya = _mm_nn(dza[None], wat[None], f"dya_{tag}")
    dyb = _mm_nn(dzb[None], wbt[None], f"dyb_{tag}")
    d_wat = _mm_tn(dza[None], ya, f"dwa_{tag}")
    d_wbt = _mm_tn(dzb[None], yb, f"dwb_{tag}")
    cb = _combine_bwd(dya, outs, lses)
    dqs, dks, dvs = [], [], []
    for grp in range(3):
        dq, dk, dv = _dil_bwd(qkr, proj, cb[grp], cb[3 + grp], lses[grp], grp)
        dqs.append(dq)
        dks.append(dk)
        dvs.append(dv)
    d_qkv_b, dbias_tab = _na_bwd(proj, bias, dyb, yb, lse_b)
    dbias = _na_dbias(dbias_tab)
    dproj = [_rope_bwd(dqs, dks, dvs, *tables), d_qkv_b, dlog]
    d_wint, row = None, 0
    for i, p in enumerate(dproj):
        d_wint = _mm_tn(p, hn, f"dwin{i}_{tag}", into=d_wint, row0=row, rows=IN_W)
        row += p.shape[0] * p.shape[2]
    token = send(("win", "wa", "wb", "wo"), [d_wint, d_wat, d_wbt, d_wo])
    dx, dx_b, dg = _mm_nn_norm_bwd(dproj, wint, x, norm_g + token[0, 0], dxo, f"mix_{tag}")
    dbias = dbias[:, 0, :480].reshape(8, 15, 32)[:, :, :31]
    return dx, dx_b, dg, dbias


def _pack_small(norms, biases, final, loss=None):
    parts = []
    for layer in range(DEPTH):
        parts += [norms[0][layer], norms[1][layer], norms[2][layer],
                  jnp.pad(biases[layer].reshape(-1), (0, BIAS_PAD - 8 * 15 * 31))]
    parts.append(final)
    flat = jnp.concatenate([p.reshape(-1).astype(F32) for p in parts])
    if loss is not None:
        flat = jnp.concatenate([flat, loss.reshape(-1)])
    return jnp.pad(flat, (0, SMALL_ROWS * 128 - flat.shape[0])).reshape(SMALL_ROWS, 128)


def _unpack_small(packed):
    flat = packed.reshape(-1)
    norms, biases = ([], [], []), []
    pos = 0
    for _ in range(DEPTH):
        for k in range(3):
            norms[k].append(flat[pos:pos + D])
            pos += D
        biases.append(flat[pos:pos + 8 * 15 * 31].reshape(8, 15, 31))
        pos += BIAS_PAD
    final = flat[pos:pos + D]
    pos += D
    return [jnp.stack(n) for n in norms], jnp.stack(biases), final, flat[pos]


def kernel(x, ffn1_norm, ffn1_w_up, ffn1_w_down, mix_norm, w_in, na_rel_bias, w_branch_a, w_branch_b, w_out, ffn2_norm, ffn2_w_up, ffn2_w_down, final_norm, loss_target, m_ffn1_norm, m_ffn1_w_up, m_ffn1_w_down, m_mix_norm, m_w_in, m_na_rel_bias, m_w_branch_a, m_w_branch_b, m_w_out, m_ffn2_norm, m_ffn2_w_up, m_ffn2_w_down, m_final_norm, v_ffn1_norm, v_ffn1_w_up, v_ffn1_w_down, v_mix_norm, v_w_in, v_na_rel_bias, v_w_branch_a, v_w_branch_b, v_w_out, v_ffn2_norm, v_ffn2_w_up, v_ffn2_w_down, v_final_norm):
    t = x.shape[0] * x.shape[1]
    xs = x.reshape(t, D)
    tgt = loss_target.reshape(t, D)
    tables = _rope_tables()

    col_sharded = dict(up1=ffn1_w_up, win=w_in, wa=w_branch_a, wb=w_branch_b, up2=ffn2_w_up)
    row_sharded = dict(down1=ffn1_w_down, wo=w_out, down2=ffn2_w_down)
    shard = [{} for _ in range(DEPTH)]
    for layer in range(DEPTH):
        for name, arr in col_sharded.items():
            shard[layer][name] = arr[layer].T.astype(BF16)
        for name, arr in row_sharded.items():
            shard[layer][name] = arr[layer].astype(BF16)

    weights = [{} for _ in range(DEPTH)]
    travel = [(0, ("up1",)), (0, ("down1",)), (0, ("win",)), (0, ("wa", "wb", "wo")), (0, ("up2", "down2")),
              (1, ("up1", "down1")), (1, ("win",)), (1, ("wa", "wb", "wo")), (1, ("up2", "down2"))]
    group_of, chips_done, sibling_done = {}, {}, {}
    count = 0
    for i, (layer, names) in enumerate(travel):
        chips_done[i] = list(range(count, count + len(names)))
        count += len(names)
        for n in names:
            group_of[layer, n] = (i, names)
    gathered, token = _exchange_start(
        "gather", [shard[layer][n] for layer, names in travel for n in names], None, None, "w")
    zero = token[0, 0]

    biases = [_na_bias_table(na_rel_bias[layer] + zero) for layer in range(DEPTH)]

    def pass_on(i, behind):
        if i in chips_done:
            lands = _exchange_wait(gathered, behind, f"w{i}", which=chips_done.pop(i))
            sibling_done[i], _ = _exchange_start("forward", [], lands, None, f"p{i}")

    def fetcher(layer):
        def fetch(name, behind):
            if (layer, name) in group_of:
                i, names = group_of[layer, name]
                if i == 0:
                    behind = (behind, *biases)
                pass_on(i, behind)
                pass_on(i + 1, behind)
                for n, got in zip(names, _exchange_wait(sibling_done.pop(i), behind, f"p{i}")):
                    weights[layer][n] = got
                    del group_of[layer, n]
            return weights[layer][name]
        return fetch

    saved = []
    h = xs
    hn = _norm_fwd(xs, ffn1_norm[0] + zero, "first")
    for layer in range(DEPTH):
        bias = biases[layer]
        fetch = fetcher(layer)
        after_ffn2 = ffn1_norm[layer + 1] if layer + 1 < DEPTH else None
        h, hn, s1 = _ffn_forward(h, hn, fetch, ("up1", "down1"), f"f1l{layer}", mix_norm[layer])
        h, hn, s2 = _mixer_forward(h, hn, fetch, bias, tables, f"l{layer}", ffn2_norm[layer])
        h, hn, s3 = _ffn_forward(h, hn, fetch, ("up2", "down2"), f"f2l{layer}", after_ffn2)
        saved.append((s1, s2, s3, bias))
    loss_part, dh, dh_b, d_final = _loss_head(h, final_norm, tgt)

    d_norms = ([None] * DEPTH, [None] * DEPTH, [None] * DEPTH)
    d_bias = [None] * DEPTH
    sent = {}

    def sender(layer, suffix):
        def send(names, grads):
            tag = f"g{layer}{names[0]}{suffix}"
            handle, token = _exchange_start("scatter", grads, None, None, tag)
            for i, n in enumerate(names):
                sent[layer, n + suffix] = (handle, i, tag)
            return token
        return send

    for layer in reversed(range(DEPTH)):
        w = weights[layer]
        s1, s2, s3, bias = saved[layer]
        dh, dh_b, d_norms[2][layer] = _ffn_backward(
            dh, dh_b, s3, ffn2_norm[layer], w["up2"].reshape(2, F, D), w["down2"], f"f2l{layer}", sender(layer, "2"))
        dh, dh_b, d_norms[1][layer], d_bias[layer] = _mixer_backward(
            dh, dh_b, s2, mix_norm[layer], (w["win"], w["wa"], w["wb"], w["wo"]), bias, tables, f"l{layer}",
            sender(layer, ""))
        dh, dh_b, d_norms[0][layer] = _ffn_backward(
            dh, dh_b, s1, ffn1_norm[layer], w["up1"].reshape(2, F, D), w["down1"], f"f1l{layer}", sender(layer, "1"))
    grad_x = dh.reshape(x.shape)

    originals = dict(up1=(ffn1_w_up, m_ffn1_w_up, v_ffn1_w_up), down1=(ffn1_w_down, m_ffn1_w_down, v_ffn1_w_down),
                     win=(w_in, m_w_in, v_w_in), wa=(w_branch_a, m_w_branch_a, v_w_branch_a),
                     wb=(w_branch_b, m_w_branch_b, v_w_branch_b), wo=(w_out, m_w_out, v_w_out),
                     up2=(ffn2_w_up, m_ffn2_w_up, v_ffn2_w_up), down2=(ffn2_w_down, m_ffn2_w_down, v_ffn2_w_down))
    big = {}
    behind = dh
    landed = {}

    def received(layer, name):
        handle, i, tag = sent[layer, name]
        if tag not in landed:
            landed[tag] = _exchange_wait(handle, behind, tag)
        return landed[tag][i]

    for name in ("down2", "up2", "win", "wa", "wb", "wo", "down1", "up1"):
        wv, mv, vv = originals[name]
        if name in col_sharded:
            wv, mv, vv = (jnp.swapaxes(t, 1, 2) for t in (wv, mv, vv))
        big[name] = tuple(_sum_adamw(received(0, name), received(1, name), wv, mv, vv, name))
        behind = big[name][1]
        if name in col_sharded:
            big[name] = tuple(jnp.swapaxes(t, 1, 2) for t in big[name])

    small = _allreduce_small(_pack_small(d_norms, d_bias, d_final, loss_part[0, :1]), behind)
    g_norms, g_bias, g_final, loss = _unpack_small(small)
    w_small = _pack_small((ffn1_norm, mix_norm, ffn2_norm), na_rel_bias, final_norm)
    m_small = _pack_small((m_ffn1_norm, m_mix_norm, m_ffn2_norm), m_na_rel_bias, m_final_norm)
    v_small = _pack_small((v_ffn1_norm, v_mix_norm, v_ffn2_norm), v_na_rel_bias, v_final_norm)
    upd = _adamw(w_small[None], small[None], m_small[None], v_small[None], "small")
    small_out = [(g_norms, g_bias, g_final)] + [_unpack_small(u[0])[:3] for u in upd]

    outputs = [loss, grad_x]
    for kind in range(4):
        norms, bias_k, final_k = small_out[kind]
        outputs += [norms[0], big["up1"][kind], big["down1"][kind], norms[1], big["win"][kind], bias_k,
                    big["wa"][kind], big["wb"][kind], big["wo"][kind], norms[2], big["up2"][kind],
                    big["down2"][kind], final_k]
    return tuple(outputs)
```

```python
import numpy as np

import jax
import jax.numpy as jnp
from jax import lax
from jax.experimental import pallas as pl
from jax.experimental.pallas import tpu as pltpu

F32 = jnp.float32
BF16 = jnp.bfloat16
SDS = jax.ShapeDtypeStruct
BS = pl.BlockSpec
MESH = pl.DeviceIdType.MESH

D = 1024
S = 2048
F = 2816
DEPTH = 2
HEAD_DIM = 64
DILATIONS = (1, 4, 16)
HALF = 64
QKV_A = 2304
QKV_B = 1536
IN_W = 5888
N_DEV = 8
NA_ROWS = 32
GRID_W = 64
NA_KR = 8
ROPE_THETA = 10000.0
RMS_EPS = 1e-6
NEG = -1e30
SCALE = HEAD_DIM ** -0.5
ADAM_LR, ADAM_B1, ADAM_B2, ADAM_EPS, ADAM_WD, ADAM_STEP = 0.001, 0.9, 0.999, 1e-08, 0.01, 10
VMEM_LIMIT_V7X = 52 * 1024 * 1024
SMALL_ROWS = 120
BIAS_PAD = 3840
NA_FWD_ROWS = 8
NA_BWD_ROWS = 4
DIL_FWD_TILES = 8
DIL_BWD_TILES = 4


def _cp(*sem):
    return pltpu.CompilerParams(dimension_semantics=sem, vmem_limit_bytes=VMEM_LIMIT_V7X)


def _dot_nn(a, b):
    return jnp.dot(a, b, preferred_element_type=F32)


def _dot_nt(a, b):
    return lax.dot_general(a, b, (((1,), (1,)), ((), ())), preferred_element_type=F32)


def _dot_tn(a, b):
    return lax.dot_general(a, b, (((0,), (0,)), ((), ())), preferred_element_type=F32)


def _ds(start, size, stride):
    return pl.ds(start, size) if stride == 1 else pl.ds(start, size, stride=stride)


def _norm_fwd(x, g, tag):
    t = x.shape[0]
    tm = 512

    def body(x_ref, g_ref, o_ref):
        xv = x_ref[...]
        r = lax.rsqrt(jnp.mean(xv * xv, axis=-1, keepdims=True) + RMS_EPS)
        o_ref[...] = (xv * r * g_ref[...]).astype(BF16)

    return pl.pallas_call(
        body, name=f"norm_fwd_{tag}", grid=(t // tm,),
        in_specs=[BS((tm, D), lambda i: (i, 0)), BS((1, D), lambda i: (0, 0))],
        out_specs=BS((tm, D), lambda i: (i, 0)),
        out_shape=SDS((t, D), BF16), compiler_params=_cp("parallel"),
    )(x, g.reshape(1, D))


def _loss_head(x, g, tgt):
    t = x.shape[0]
    tm = 1024

    def body(x_ref, g_ref, t_ref, loss_ref, dx_ref, dxb_ref, dg_ref):
        @pl.when(pl.program_id(0) == 0)
        def _():
            dg_ref[...] = jnp.zeros_like(dg_ref)
            loss_ref[...] = jnp.zeros_like(loss_ref)

        xv = x_ref[...]
        gv = g_ref[...]
        r = lax.rsqrt(jnp.mean(xv * xv, axis=-1, keepdims=True) + RMS_EPS)
        xh = xv * r
        e = xh * gv - t_ref[...]
        loss_ref[...] += 0.5 * jnp.sum(jnp.mean(e * e, axis=-1, keepdims=True), axis=0, keepdims=True)
        dy = e * (1.0 / D)
        u = dy * gv
        dx = r * (u - xh * jnp.mean(xh * u, axis=-1, keepdims=True))
        dx_ref[...] = dx
        dxb_ref[...] = dx.astype(BF16)
        dg_ref[...] += jnp.sum(dy * xh, axis=0, keepdims=True)

    row = BS((tm, D), lambda i: (i, 0))
    vec = BS((1, D), lambda i: (0, 0))
    return pl.pallas_call(
        body, name="loss_head", grid=(t // tm,),
        in_specs=[row, vec, row], out_specs=[BS((1, 128), lambda i: (0, 0)), row, row, vec],
        out_shape=[SDS((1, 128), F32), SDS((t, D), F32), SDS((t, D), BF16), SDS((1, D), F32)],
        compiler_params=_cp("arbitrary"),
    )(x, g.reshape(1, D), tgt)


def _mm_nn(a, w, tag, res=None, scale=1.0, tm=1024, tn=None, next_g=None):
    c_n, t, k = a.shape
    n = w.shape[2]
    tn = n if tn is None else tn
    assert next_g is None or tn == n
    n_in = 2 + (res is not None) + (next_g is not None)

    def body(*refs):
        a_ref, w_ref = refs[0], refs[1]
        acc = _dot_nn(a_ref[0].astype(BF16), w_ref[0])
        for c in range(1, c_n):
            acc = acc + _dot_nn(a_ref[c].astype(BF16), w_ref[c])
        if scale != 1.0:
            acc = acc * scale
        if res is not None:
            acc = refs[2][...] + acc
        refs[n_in][...] = acc
        if next_g is not None:
            r = lax.rsqrt(jnp.mean(acc * acc, axis=-1, keepdims=True) + RMS_EPS)
            refs[n_in + 1][...] = (acc * r * refs[n_in - 1][...]).astype(BF16)

    w_mode = dict(pipeline_mode=pl.Buffered(1)) if tn == n else {}
    in_specs = [BS((c_n, tm, k), lambda i, j: (0, i, 0)), BS((c_n, k, tn), lambda i, j: (0, 0, j), **w_mode)]
    args = [a, w]
    out_specs = [BS((tm, tn), lambda i, j: (i, j))]
    out_shape = [SDS((t, n), F32)]
    if res is not None:
        in_specs.append(BS((tm, tn), lambda i, j: (i, j)))
        args.append(res)
    if next_g is not None:
        in_specs.append(BS((1, n), lambda i, j: (0, 0)))
        args.append(next_g.reshape(1, n))
        out_specs.append(BS((tm, tn), lambda i, j: (i, j)))
        out_shape.append(SDS((t, n), BF16))
    got = pl.pallas_call(
        body, name=f"mm_nn_{tag}", grid=(t // tm, n // tn), in_specs=in_specs, out_specs=out_specs,
        out_shape=out_shape, compiler_params=_cp("parallel", "parallel"),
    )(*args)
    return got if next_g is not None else got[0]


def _mm_nn_norm_bwd(parts, w, x, g, dres, tag, tm=256, single_w=False):
    t = parts[0].shape[1]
    n_parts = len(parts)

    def body(*refs):
        w_ref, x_ref, g_ref, dr_ref, dx_ref, dxb_ref, dg_ref = refs[n_parts:]

        @pl.when(pl.program_id(0) == 0)
        def _():
            dg_ref[...] = jnp.zeros_like(dg_ref)

        dh = None
        row = 0
        for a_ref, part in zip(refs, parts):
            for c in range(part.shape[0]):
                term = _dot_nn(a_ref[c].astype(BF16), w_ref[row:row + part.shape[2], :])
                dh = term if dh is None else dh + term
                row += part.shape[2]
        xv = x_ref[...]
        r = lax.rsqrt(jnp.mean(xv * xv, axis=-1, keepdims=True) + RMS_EPS)
        xh = xv * r
        u = dh * g_ref[...]
        dx = dr_ref[...] + r * (u - xh * jnp.mean(xh * u, axis=-1, keepdims=True))
        dx_ref[...] = dx
        dxb_ref[...] = dx.astype(BF16)
        dg_ref[...] += jnp.sum(dh * xh, axis=0, keepdims=True)

    row = BS((tm, D), lambda i: (i, 0))
    vec = BS((1, D), lambda i: (0, 0))
    return pl.pallas_call(
        body, name=f"mm_nn_norm_bwd_{tag}", grid=(t // tm,),
        in_specs=[BS((p.shape[0], tm, p.shape[2]), lambda i: (0, i, 0)) for p in parts]
        + [BS(w.shape, lambda i: (0, 0), **(dict(pipeline_mode=pl.Buffered(1)) if single_w else {})), row, vec, row],
        out_specs=[row, row, vec], out_shape=[SDS((t, D), F32), SDS((t, D), BF16), SDS((1, D), F32)],
        compiler_params=_cp("arbitrary"),
    )(*parts, w, x, g.reshape(1, D), dres)


def _mm_nt_rows(a, w, tag, tm, tn, n_total, w_row0, rope=None):
    t, k = a.shape
    assert w_row0 % tn == 0 and n_total % tn == 0
    j0 = w_row0 // tn

    def body(a_ref, w_ref, *rest):
        o_ref = rest[-1]
        o_ref[...] = _dot_nt(a_ref[...].astype(BF16), w_ref[...])
        if rope is not None:
            @pl.when(pl.program_id(0) == 0)
            def _():
                c = rest[0][...]
                sg = rest[1][...]
                first = (lax.broadcasted_iota(jnp.int32, (tm, 128), 1) % HEAD_DIM) < HEAD_DIM // 2
                for col in range(0, rope[2], 128):
                    v = o_ref[:, col:col + 128]
                    o_ref[:, col:col + 128] = v * c + _swap_halves(v, first) * sg

    in_specs = [BS((tm, k), lambda j, i: (i, 0)), BS((tn, k), lambda j, i: (j0 + j, 0))]
    args = [a, w]
    if rope is not None:
        assert rope[2] <= tn
        in_specs += [BS((tm, 128), lambda j, i: (i % (S // tm), 0))] * 2
        args += [rope[0], rope[1]]
    return pl.pallas_call(
        body, name=f"mm_nt_{tag}", grid=(n_total // tn, t // tm), in_specs=in_specs,
        out_specs=BS((tm, tn), lambda j, i: (i, j)), out_shape=SDS((t, n_total), F32),
        compiler_params=_cp("parallel", "parallel"),
    )(*args)


def _mm_tn(a, b, tag, scale=1.0, tmm=None, into=None, row0=0, rows=None):
    c_n, t, m = a.shape
    n = b.shape[1]
    if tmm is None:
        tmm = max(w for w in (768, 512, 256) if m % w == 0 and row0 % w == 0)
    tiles = m // tmm
    block0 = row0 // tmm
    assert row0 % tmm == 0 and m % tmm == 0

    def body(a_ref, b_ref, *rest):
        rest[-1][...] = (_dot_tn(a_ref[...].astype(BF16), b_ref[...].astype(BF16)) * scale).astype(BF16)

    in_specs = [BS((None, t, tmm), lambda c, mi: (c, 0, mi)), BS((t, n), lambda c, mi: (0, 0))]
    args = [a, b]
    if into is not None:
        in_specs.append(BS(memory_space=pl.ANY))
        args.append(into)
    return pl.pallas_call(
        body, name=f"mm_tn_{tag}", grid=(c_n, tiles), in_specs=in_specs,
        out_specs=BS((tmm, n), lambda c, mi: (block0 + c * tiles + mi, 0)),
        out_shape=SDS((rows or c_n * m, n) if into is None else into.shape, BF16),
        input_output_aliases={} if into is None else {2: 0},
        compiler_params=_cp("parallel", "parallel"),
    )(*args)


def _ffn_up(hn, wut, tag):
    t = hn.shape[0]
    tm, tn = 512, 1408

    def body(h_ref, w_ref, gu_ref, act_ref):
        h = h_ref[...]
        g = _dot_nt(h, w_ref[0])
        u = _dot_nt(h, w_ref[1])
        sg = jax.nn.sigmoid(g)
        silu = g * sg
        gu_ref[0] = (u * (sg + silu * (1.0 - sg))).astype(BF16)
        gu_ref[1] = silu.astype(BF16)
        act_ref[...] = (silu * u).astype(BF16)

    return pl.pallas_call(
        body, name=f"ffn_up_{tag}", grid=(F // tn, t // tm),
        in_specs=[BS((tm, D), lambda j, i: (i, 0)), BS((2, tn, D), lambda j, i: (0, j, 0))],
        out_specs=[BS((2, tm, tn), lambda j, i: (0, i, j)), BS((tm, tn), lambda j, i: (i, j))],
        out_shape=[SDS((2, t, F), BF16), SDS((t, F), BF16)],
        compiler_params=_cp("parallel", "parallel"),
    )(hn, wut)


def _ffn_dact(dxo, wd, gu, tie, tag):
    t = dxo.shape[0]
    tm, tn = 1024, 1408

    def body(d_ref, w_ref, gu_ref, tie_ref, o_ref):
        dact = _dot_nt(d_ref[...] * 0.5, w_ref[...])
        o_ref[0] = (dact * gu_ref[0].astype(F32)).astype(BF16)
        o_ref[1] = (dact * gu_ref[1].astype(F32)).astype(BF16)

    return pl.pallas_call(
        body, name=f"ffn_dact_{tag}", grid=(F // tn, t // tm),
        in_specs=[BS((tm, D), lambda j, i: (i, 0)), BS((tn, D), lambda j, i: (j, 0)),
                  BS((2, tm, tn), lambda j, i: (0, i, j)), BS((8, 128), lambda j, i: (0, 0))],
        out_specs=BS((2, tm, tn), lambda j, i: (0, i, j)),
        out_shape=SDS((2, t, F), BF16), compiler_params=_cp("parallel", "parallel"),
    )(dxo, wd, gu, tie)


def _rope_tables():
    half = HEAD_DIM // 2
    inv_freq = ROPE_THETA ** (-jnp.arange(half, dtype=F32) / half)
    ang = jnp.arange(S).astype(F32)[:, None] * inv_freq[None, :]
    cos, sin = jnp.cos(ang), jnp.sin(ang)
    return jnp.concatenate([cos, cos, cos, cos], axis=1), jnp.concatenate([-sin, sin, -sin, sin], axis=1)


def _swap_halves(t, first_half):
    return jnp.where(first_half, pltpu.roll(t, 96, 1), pltpu.roll(t, 32, 1))


def _rope_bwd(dqs, dks, dvs, cos_t, sin_t):
    t = dqs[0].shape[0]
    tm = 1024

    def body(*refs):
        c = refs[9][...]
        sg = refs[10][...]
        o_ref = refs[11]
        first = (lax.broadcasted_iota(jnp.int32, (tm, 128), 1) % HEAD_DIM) < HEAD_DIM // 2
        for a in range(6):
            for hp in range(2):
                v = refs[a][:, 128 * hp:128 * (hp + 1)]
                col = 128 * (2 * a + hp)
                o_ref[:, col:col + 128] = (v * c + _swap_halves(v * sg, first)).astype(BF16)
        for a in range(6, 9):
            o_ref[:, 256 * a:256 * (a + 1)] = refs[a][...].astype(BF16)

    blk = BS((tm, 256), lambda i: (i, 0))
    tab = BS((tm, 128), lambda i: (i % (S // tm), 0))
    return pl.pallas_call(
        body, name="rope_bwd", grid=(t // tm,), in_specs=[blk] * 9 + [tab, tab],
        out_specs=BS((None, tm, QKV_A), lambda i: (0, i, 0)), out_shape=SDS((1, t, QKV_A), BF16),
        compiler_params=_cp("parallel"),
    )(*dqs, *dks, *dvs, cos_t, sin_t)


def _head_masks():
    lane = lax.broadcasted_iota(jnp.int32, (1, 128), 1)
    m0 = (lane < HEAD_DIM).astype(F32)
    return m0, 1.0 - m0


def _dil_geometry(d):
    sub = S // d
    q_rows = 128
    k_rows = min(256, sub)
    return sub, q_rows, sub // q_rows, k_rows


def _dil_tile(idx, d, keys_on_rows=False):
    sub, q_rows, nb, k_rows = _dil_geometry(d)
    r = idx // nb
    n = idx % nb
    k_sub = jnp.clip(q_rows * n - HALF, 0, sub - k_rows)
    if d == 1:
        q_start = pl.multiple_of(q_rows * n, q_rows)
        k_start = pl.multiple_of(k_sub, HALF)
    else:
        q_start = q_rows * n * d + r
        k_start = k_sub * d + r
    if keys_on_rows:
        ii = lax.broadcasted_iota(jnp.int32, (k_rows, 2 * q_rows), 1) % q_rows
        jj = lax.broadcasted_iota(jnp.int32, (k_rows, 2 * q_rows), 0)
    else:
        ii = lax.broadcasted_iota(jnp.int32, (q_rows, k_rows), 0)
        jj = lax.broadcasted_iota(jnp.int32, (q_rows, k_rows), 1)
    valid = jnp.abs(jj - ii + (k_sub - q_rows * n)) <= HALF
    return q_start, k_start, valid


def _dil_specs(grp):
    qs = BS((S, 128), lambda b, hp: (b, 2 * grp + hp))
    ks = BS((S, 128), lambda b, hp: (b, 6 + 2 * grp + hp))
    vs = BS((S, 128), lambda b, hp: (b, 12 + 2 * grp + hp))
    own = BS((S, 128), lambda b, hp: (b, hp))
    return qs, ks, vs, own


def _dil_fwd(qkr, proj, grp):
    t = qkr.shape[0]
    d = DILATIONS[grp]
    _, q_rows, nb, k_rows = _dil_geometry(d)

    def body(q_ref, k_ref, v_ref, o_ref, l_ref):
        masks = _head_masks()

        def step(i0, carry):
            geo = [_dil_tile(i0 * DIL_FWD_TILES + j, d) for j in range(DIL_FWD_TILES)]
            tiles = [(j, h) for j in range(DIL_FWD_TILES) for h in range(2)]
            qs = [q_ref[_ds(g[0], q_rows, d), :] for g in geo]
            kbs = [k_ref[_ds(g[1], k_rows, d), :].astype(BF16) for g in geo]
            ss = [jnp.where(geo[j][2], _dot_nt((qs[j] * masks[h]).astype(BF16), kbs[j]) * SCALE, NEG) for j, h in tiles]
            mxs = [jnp.max(s, axis=1, keepdims=True) for s in ss]
            ps = [jnp.exp(s - mx) for s, mx in zip(ss, mxs)]
            dens = [jnp.sum(p, axis=1, keepdims=True) for p in ps]
            vs = [v_ref[_ds(g[1], k_rows, d), :] for g in geo]
            outs = [_dot_nn(p.astype(BF16), (vs[j] * masks[h]).astype(BF16)) / den
                    for p, den, (j, h) in zip(ps, dens, tiles)]
            for j, g in enumerate(geo):
                o_ref[_ds(g[0], q_rows, d), :] = outs[2 * j] + outs[2 * j + 1]
                l_ref[_ds(g[0], q_rows, d), :] = (
                    (mxs[2 * j] + jnp.log(dens[2 * j])) * masks[0] + (mxs[2 * j + 1] + jnp.log(dens[2 * j + 1])) * masks[1])
            return carry

        lax.fori_loop(0, d * nb // DIL_FWD_TILES, step, 0)

    qs, ks, vs, own = _dil_specs(grp)
    return pl.pallas_call(
        body, name=f"dil_fwd_{grp}", grid=(t // S, 2), in_specs=[qs, ks, vs], out_specs=[own, own],
        out_shape=[SDS((t, 256), F32), SDS((t, 256), F32)], compiler_params=_cp("parallel", "parallel"),
    )(qkr, qkr, proj)


def _dil_bwd(qkr, proj, do, dlp, lse, grp):
    t = qkr.shape[0]
    d = DILATIONS[grp]
    _, q_rows, nb, k_rows = _dil_geometry(d)

    def body(q_ref, k_ref, v_ref, do_ref, dl_ref, l_ref, dq_ref, dk_ref, dv_ref):
        masks = _head_masks()
        dk_ref[...] = jnp.zeros_like(dk_ref)
        dv_ref[...] = jnp.zeros_like(dv_ref)

        def as_row(x2):
            xt = x2.T
            return jnp.concatenate([xt[0:1], xt[HEAD_DIM:HEAD_DIM + 1]], axis=1)

        def step(i0, carry):
            geo = [_dil_tile(i0 * DIL_BWD_TILES + j, d, keys_on_rows=True) for j in range(DIL_BWD_TILES)]
            q_ds = [_ds(g[0], q_rows, d) for g in geo]
            k_ds = [_ds(g[1], k_rows, d) for g in geo]
            qbs = [_both_heads(q_ref[r, :], masks).astype(BF16) for r in q_ds]
            kbs = [k_ref[r, :].astype(BF16) for r in k_ds]
            vbs = [v_ref[r, :].astype(BF16) for r in k_ds]
            dobs = [_both_heads(do_ref[r, :], masks).astype(BF16) for r in q_ds]
            l_rows = [as_row(l_ref[r, :]) for r in q_ds]
            dl_rows = [as_row(dl_ref[r, :]) for r in q_ds]
            ss = [jnp.where(g[2], _dot_nt(kb, qb) * SCALE, NEG) for g, kb, qb in zip(geo, kbs, qbs)]
            ps = [jnp.exp(s - lr) for s, lr in zip(ss, l_rows)]
            dps = [_dot_nt(vb, dob) for vb, dob in zip(vbs, dobs)]
            dss = [(p * (dp - dr)).astype(BF16) for p, dp, dr in zip(ps, dps, dl_rows)]
            dks = [_dot_nn(ds, qb) for ds, qb in zip(dss, qbs)]
            dvs = [_dot_nn(p.astype(BF16), dob) for p, dob in zip(ps, dobs)]
            dqs = [_own_heads(_dot_tn(ds, kb), masks) for ds, kb in zip(dss, kbs)]
            for j in range(DIL_BWD_TILES):
                dq_ref[q_ds[j], :] = dqs[j] * SCALE
                dk_ref[k_ds[j], :] += dks[j] * SCALE
                dv_ref[k_ds[j], :] += dvs[j]
            return carry

        lax.fori_loop(0, d * nb // DIL_BWD_TILES, step, 0)

    qs, ks, vs, own = _dil_specs(grp)
    return pl.pallas_call(
        body, name=f"dil_bwd_{grp}", grid=(t // S, 2), in_specs=[qs, ks, vs, own, own, own],
        out_specs=[own, own, own], out_shape=[SDS((t, 256), F32)] * 3,
        compiler_params=_cp("parallel", "parallel"),
    )(qkr, qkr, proj, do, dlp, lse)


def _mix_weights(l0, l1, l2):
    mx = jnp.maximum(jnp.maximum(l0, l1), l2)
    e0, e1, e2 = jnp.exp(l0 - mx), jnp.exp(l1 - mx), jnp.exp(l2 - mx)
    den = e0 + e1 + e2
    return e0 / den, e1 / den, e2 / den


def _combine_fwd(outs, lses):
    t = outs[0].shape[0]
    tm = 1024

    def body(o0, o1, o2, l0, l1, l2, y_ref):
        w0, w1, w2 = _mix_weights(l0[...], l1[...], l2[...])
        y_ref[...] = w0 * o0[...] + w1 * o1[...] + w2 * o2[...]

    blk = BS((tm, 256), lambda i: (i, 0))
    return pl.pallas_call(
        body, name="combine_fwd", grid=(t // tm,), in_specs=[blk] * 6, out_specs=blk,
        out_shape=SDS((t, 256), F32), compiler_params=_cp("parallel"),
    )(*outs, *lses)


def _head_sum(x):
    a = lax.broadcasted_iota(jnp.int32, (256, 256), 0) // HEAD_DIM
    b = lax.broadcasted_iota(jnp.int32, (256, 256), 1) // HEAD_DIM
    ones = (a == b).astype(BF16)
    hi = x.astype(BF16)
    lo = (x - hi.astype(F32)).astype(BF16)
    return _dot_nn(hi, ones) + _dot_nn(lo, ones)


def _combine_bwd(dya, outs, lses):
    t = dya.shape[0]
    tm = 1024

    def body(dy_ref, o0, o1, o2, l0, l1, l2, d0, d1, d2, e0, e1, e2):
        ws = _mix_weights(l0[...], l1[...], l2[...])
        dy = dy_ref[...]
        ya = ws[0] * o0[...] + ws[1] * o1[...] + ws[2] * o2[...]
        hs = _head_sum(dy * ya)
        for w, d_ref, e_ref in zip(ws, (d0, d1, d2), (e0, e1, e2)):
            d_ref[...] = w * dy
            e_ref[...] = w * hs

    blk = BS((tm, 256), lambda i: (i, 0))
    return pl.pallas_call(
        body, name="combine_bwd", grid=(t // tm,), in_specs=[blk] * 7, out_specs=[blk] * 6,
        out_shape=[SDS((t, 256), F32)] * 6, compiler_params=_cp("parallel"),
    )(dya, *outs, *lses)


def _na_bias_table(rel_bias):
    kw = NA_KR * GRID_W
    rev = jnp.pad(rel_bias.astype(F32)[:, :, ::-1], ((0, 0), (0, 0), (0, 128 - 31)))

    def body(r_ref, o_ref):
        lane = lax.broadcasted_iota(jnp.int32, (GRID_W, 128), 1)
        j = lax.broadcasted_iota(jnp.int32, (GRID_W, 128), 0)
        q = lane % GRID_W
        win_lo = jnp.clip(q - 8, 0, GRID_W - 16)
        valid = (j >= win_lo) & (j < win_lo + 16)
        for cls in range(NA_KR):
            for k in range(NA_KR):
                tiles = []
                for h in range(2):
                    row = jnp.broadcast_to(r_ref[h, cls + k:cls + k + 1, :], (GRID_W, 128))
                    tiles.append(pltpu.roll(row, (128 - 15 + GRID_W * h) % 128, 1, stride=1, stride_axis=0))
                o_ref[cls, GRID_W * k:GRID_W * (k + 1), :] = jnp.where(
                    valid, jnp.where(lane < GRID_W, tiles[0], tiles[1]), NEG)

    return pl.pallas_call(
        body, name="na_bias_table", grid=(4,),
        in_specs=[BS((2, 2 * NA_KR - 1, 128), lambda hp: (hp, 0, 0))],
        out_specs=BS((None, NA_KR, kw, 128), lambda hp: (hp, 0, 0, 0)),
        out_shape=SDS((4, NA_KR, kw, 128), F32), compiler_params=_cp("parallel"),
    )(rev)


def _na_row(i):
    lo = jnp.clip(i - NA_KR // 2, 0, NA_ROWS - NA_KR)
    return pl.multiple_of(GRID_W * i, GRID_W), pl.multiple_of(GRID_W * lo, GRID_W), lo - i + NA_KR - 1


def _both_heads(x, masks):
    return jnp.concatenate([x * masks[0], x * masks[1]], axis=0)


def _own_heads(r, masks):
    half = r.shape[0] // 2
    return r[:half] * masks[0] + r[half:] * masks[1]


def _na_fwd(proj, bias):
    t = proj.shape[0]
    kw = NA_KR * GRID_W

    def body(q_ref, k_ref, v_ref, b_ref, o_ref, l_ref):
        masks = _head_masks()

        def step(i0, carry):
            idx = [i0 * NA_FWD_ROWS + j for j in range(NA_FWD_ROWS)]
            rows = [_na_row(i) for i in idx]
            qbs = [_both_heads(q_ref[pl.ds(r[0], GRID_W), :], masks).astype(BF16) for r in rows]
            kbs = [k_ref[pl.ds(r[1], kw), :].astype(BF16) for r in rows]
            ss = [_dot_nt(kb, qb) * SCALE + b_ref[r[2]] for kb, qb, r in zip(kbs, qbs, rows)]
            mxs = [jnp.max(s, axis=0, keepdims=True) for s in ss]
            ps = [jnp.exp(s - mx) for s, mx in zip(ss, mxs)]
            dens = [jnp.sum(p, axis=0, keepdims=True) for p in ps]
            pbs = [(p / den).astype(BF16) for p, den in zip(ps, dens)]
            vbs = [v_ref[pl.ds(r[1], kw), :].astype(BF16) for r in rows]
            outs = [_own_heads(_dot_tn(pb, vb), masks) for pb, vb in zip(pbs, vbs)]
            for j, r in enumerate(rows):
                o_ref[pl.ds(r[0], GRID_W), :] = outs[j]
                l_ref[pl.ds(idx[j], 1), :] = mxs[j] + jnp.log(dens[j])
            return carry

        lax.fori_loop(0, NA_ROWS // NA_FWD_ROWS, step, 0)

    c0 = QKV_A // 128
    return pl.pallas_call(
        body, name="na_fwd", grid=(t // S, 4),
        in_specs=[BS((S, 128), lambda b, hp: (b, c0 + hp)), BS((S, 128), lambda b, hp: (b, c0 + 4 + hp)),
                  BS((S, 128), lambda b, hp: (b, c0 + 8 + hp)),
                  BS((None, NA_KR, kw, 128), lambda b, hp: (hp, 0, 0, 0))],
        out_specs=[BS((S, 128), lambda b, hp: (b, hp)), BS((None, None, NA_ROWS, 128), lambda b, hp: (b, hp, 0, 0))],
        out_shape=[SDS((t, 512), F32), SDS((t // S, 4, NA_ROWS, 128), F32)],
        compiler_params=_cp("parallel", "parallel"),
    )(proj, proj, proj, bias)


def _na_bwd(proj, bias, dyb, yb, lse):
    t = proj.shape[0]
    kw = NA_KR * GRID_W

    def body(q_ref, k_ref, v_ref, b_ref, do_ref, o_ref, l_ref, d_ref, db_ref):
        masks = _head_masks()
        ones = jnp.ones((8, 128), BF16)

        @pl.when(pl.program_id(1) == 0)
        def _():
            db_ref[...] = jnp.zeros_like(db_ref)

        d_ref[1:3] = jnp.zeros((2, S, 128), F32)

        def row_sums(x):
            hi = x.astype(BF16)
            lo = (x - hi.astype(F32)).astype(BF16)
            return (_dot_nt(ones, hi) + _dot_nt(ones, lo))[0:1]

        def step(i0, carry):
            idx = [i0 * NA_BWD_ROWS + j for j in range(NA_BWD_ROWS)]
            rows = [_na_row(i) for i in idx]
            q_ds = [pl.ds(r[0], GRID_W) for r in rows]
            k_ds = [pl.ds(r[1], kw) for r in rows]
            qbs = [_both_heads(q_ref[r, :], masks).astype(BF16) for r in q_ds]
            kbs = [k_ref[r, :].astype(BF16) for r in k_ds]
            vbs = [v_ref[r, :].astype(BF16) for r in k_ds]
            dos = [do_ref[r, :] for r in q_ds]
            dobs = [_both_heads(do, masks).astype(BF16) for do in dos]
            deltas = [row_sums(_both_heads(do * o_ref[r, :], masks)) for do, r in zip(dos, q_ds)]
            ss = [_dot_nt(kb, qb) * SCALE + b_ref[r[2]] for kb, qb, r in zip(kbs, qbs, rows)]
            ps = [jnp.exp(s - l_ref[pl.ds(i, 1), :]) for s, i in zip(ss, idx)]
            dps = [_dot_nt(vb, dob) for vb, dob in zip(vbs, dobs)]
            dss = [p * (dp - delta) for p, dp, delta in zip(ps, dps, deltas)]
            for ds, r in zip(dss, rows):
                db_ref[r[2]] += ds
            dsbs = [ds.astype(BF16) for ds in dss]
            dks = [_dot_nn(dsb, qb) for dsb, qb in zip(dsbs, qbs)]
            dvs = [_dot_nn(p.astype(BF16), dob) for p, dob in zip(ps, dobs)]
            dqs = [_own_heads(_dot_tn(dsb, kb), masks) for dsb, kb in zip(dsbs, kbs)]
            for j in range(NA_BWD_ROWS):
                d_ref[0, q_ds[j], :] = dqs[j] * SCALE
                d_ref[1, k_ds[j], :] += dks[j] * SCALE
                d_ref[2, k_ds[j], :] += dvs[j]
            return carry

        lax.fori_loop(0, NA_ROWS // NA_BWD_ROWS, step, 0)

    c0 = QKV_A // 128
    own = BS((S, 128), lambda hp, b: (b, hp))
    tab = BS((None, NA_KR, kw, 128), lambda hp, b: (hp, 0, 0, 0))
    return pl.pallas_call(
        body, name="na_bwd", grid=(4, t // S),
        in_specs=[BS((S, 128), lambda hp, b: (b, c0 + hp)), BS((S, 128), lambda hp, b: (b, c0 + 4 + hp)),
                  BS((S, 128), lambda hp, b: (b, c0 + 8 + hp)), tab, own, own,
                  BS((None, None, NA_ROWS, 128), lambda hp, b: (b, hp, 0, 0))],
        out_specs=[BS((3, S, 128), lambda hp, b: (0, b, hp)), tab],
        out_shape=[SDS((3, t, 512), F32), SDS((4, NA_KR, kw, 128), F32)],
        compiler_params=_cp("parallel", "arbitrary"),
    )(proj, proj, proj, bias, dyb, yb, lse)


def _na_dbias_lane_map():
    kw = NA_KR * GRID_W
    lane = np.arange(kw)
    blk, m = lane // GRID_W, lane % GRID_W
    target = np.full(kw, -1)
    target[m < 16] = (blk * 32 + 15 + m)[m < 16]
    target[m >= 49] = (((blk + 1) % NA_KR) * 32 + m - 49)[m >= 49]
    return jnp.asarray(target[:, None] == np.arange(kw)[None, :], BF16)


def _na_dbias(db):
    kw = NA_KR * GRID_W

    def body(x_ref, map_ref, o_ref, z_ref):
        for cls in range(NA_KR):
            xt = x_ref[cls].T
            for h in range(2):
                xv = xt[GRID_W * h:GRID_W * (h + 1)]
                y = xv[0:8]
                for g in range(1, GRID_W // 8):
                    y = y + pltpu.roll(xv[8 * g:8 * g + 8], kw - 8 * g, 1)
                d = y[0:1]
                for s in range(1, 8):
                    d = d + pltpu.roll(y[s:s + 1], kw - s, 1)
                z_ref[h, cls:cls + 1, :] = d
        for h in range(2):
            z = z_ref[h]
            hi = z.astype(BF16)
            lo = (z - hi.astype(F32)).astype(BF16)
            e = _dot_nn(hi, map_ref[...]) + _dot_nn(lo, map_ref[...])
            out = e[0:1]
            for cls in range(1, NA_KR):
                out = out + pltpu.roll(e[cls:cls + 1], 32 * cls, 1)
            o_ref[h] = jnp.broadcast_to(out, (8, kw))

    return pl.pallas_call(
        body, name="na_dbias", grid=(4,),
        in_specs=[BS((None, NA_KR, kw, 128), lambda hp: (hp, 0, 0, 0)), BS((kw, kw), lambda hp: (0, 0))],
        out_specs=BS((2, 8, kw), lambda hp: (hp, 0, 0)), out_shape=SDS((8, 8, kw), F32),
        scratch_shapes=[pltpu.VMEM((2, 8, kw), F32)], compiler_params=_cp("parallel"),
    )(db, _na_dbias_lane_map())


def _merge_fwd(ya, yb, proj, wat, wbt):
    t = ya.shape[0]
    tm, tn = 1024, 256
    ca = (QKV_A + QKV_B) // tn
    cb = ca + D // tn

    def body(ya_ref, yb_ref, la_ref, lb_ref, wa_ref, wb_ref, m_ref, za_ref, zb_ref):
        za = _dot_nt(ya_ref[...].astype(BF16), wa_ref[...])
        zb = _dot_nt(yb_ref[...].astype(BF16), wb_ref[...])
        m_ref[...] = (jax.nn.sigmoid(la_ref[...]) * za + jax.nn.sigmoid(lb_ref[...]) * zb).astype(BF16)
        za_ref[...] = za.astype(BF16)
        zb_ref[...] = zb.astype(BF16)

    out = BS((tm, tn), lambda i, j: (i, j))
    return pl.pallas_call(
        body, name="merge_fwd", grid=(t // tm, D // tn),
        in_specs=[BS((tm, 256), lambda i, j: (i, 0)), BS((tm, 512), lambda i, j: (i, 0)),
                  BS((tm, tn), lambda i, j: (i, ca + j)), BS((tm, tn), lambda i, j: (i, cb + j)),
                  BS((tn, 256), lambda i, j: (j, 0)), BS((tn, 512), lambda i, j: (j, 0))],
        out_specs=[out, out, out], out_shape=[SDS((t, D), BF16)] * 3,
        compiler_params=_cp("parallel", "parallel"),
    )(ya, yb, proj, proj, wat, wbt)


def _merge_bwd(dxo, wo, za, zb, proj):
    t = dxo.shape[0]
    tm, tn = 1024, 256
    ca = (QKV_A + QKV_B) // tn
    cb = ca + D // tn

    def body(d_ref, w_ref, za_ref, zb_ref, la_ref, lb_ref, dza_ref, dzb_ref, dl_ref):
        dmv = _dot_nt(d_ref[...], w_ref[...])
        ga = jax.nn.sigmoid(la_ref[...])
        gb = jax.nn.sigmoid(lb_ref[...])
        dza_ref[...] = (dmv * ga).astype(BF16)
        dzb_ref[...] = (dmv * gb).astype(BF16)
        dl_ref[0] = (dmv * za_ref[...].astype(F32) * ga * (1.0 - ga)).astype(BF16)
        dl_ref[1] = (dmv * zb_ref[...].astype(F32) * gb * (1.0 - gb)).astype(BF16)

    blk = BS((tm, tn), lambda i, j: (i, j))
    return pl.pallas_call(
        body, name="merge_bwd", grid=(t // tm, D // tn),
        in_specs=[BS((tm, D), lambda i, j: (i, 0)), BS((tn, D), lambda i, j: (j, 0)), blk, blk,
                  BS((tm, tn), lambda i, j: (i, ca + j)), BS((tm, tn), lambda i, j: (i, cb + j))],
        out_specs=[blk, blk, BS((2, tm, tn), lambda i, j: (0, i, j))],
        out_shape=[SDS((t, D), BF16), SDS((t, D), BF16), SDS((2, t, D), BF16)],
        compiler_params=_cp("parallel", "parallel"),
    )(dxo, wo, za, zb, proj, proj)


def _adamw_update(w, g, m, v):
    mn = ADAM_B1 * m + (1.0 - ADAM_B1) * g
    vn = ADAM_B2 * v + (1.0 - ADAM_B2) * (g * g)
    m_hat = mn / (1.0 - ADAM_B1 ** ADAM_STEP)
    v_hat = vn / (1.0 - ADAM_B2 ** ADAM_STEP)
    return -ADAM_LR * (m_hat / (jnp.sqrt(v_hat) + ADAM_EPS) + ADAM_WD * w), mn, vn


def _sum_adamw(recv0, recv1, w, m, v, tag):
    _, r, c = recv0.shape
    tr = max(rows for rows in range(16, r + 1, 16) if r % rows == 0 and rows * c <= 384 * 1024)

    def body(a_ref, b_ref, w_ref, m_ref, v_ref, g_ref, d_ref, mo_ref, vo_ref):
        def update(ref):
            g = ref[0].astype(F32)
            for s in range(1, N_DEV):
                g = g + ref[s].astype(F32)
            g_ref[...] = g
            d_ref[...], mo_ref[...], vo_ref[...] = _adamw_update(w_ref[...], g, m_ref[...], v_ref[...])

        pl.when(pl.program_id(0) == 0)(lambda: update(a_ref))
        pl.when(pl.program_id(0) == 1)(lambda: update(b_ref))

    blk = BS((None, tr, c), lambda layer, i: (layer, i, 0))
    return pl.pallas_call(
        body, name=f"sum_adamw_{tag}", grid=(2, r // tr),
        in_specs=[BS((N_DEV, tr, c), lambda layer, i: (0, i * (1 - layer), 0)),
                  BS((N_DEV, tr, c), lambda layer, i: (0, i * layer, 0)), blk, blk, blk],
        out_specs=[blk] * 4, out_shape=[SDS((2, r, c), F32)] * 4, compiler_params=_cp("arbitrary", "arbitrary"),
    )(recv0, recv1, w, m, v)


def _adamw(w, g, m, v, tag):
    layers, r, c = w.shape
    tr = next(r // k for k in (1, 2, 4, 8) if r // k <= 384 and r % (8 * k) == 0)

    def body(w_ref, g_ref, m_ref, v_ref, d_ref, mo_ref, vo_ref):
        d_ref[...], mo_ref[...], vo_ref[...] = _adamw_update(w_ref[...], g_ref[...], m_ref[...], v_ref[...])

    blk = BS((None, tr, c), lambda l, i: (l, i, 0))
    return pl.pallas_call(
        body, name=f"adamw_{tag}", grid=(layers, r // tr), in_specs=[blk] * 4, out_specs=[blk] * 3,
        out_shape=[SDS((layers, r, c), F32)] * 3, compiler_params=_cp("parallel", "parallel"),
    )(w, g, m, v)


def _place():
    return lax.axis_index("x"), lax.axis_index("y"), lax.axis_index("c")


def _flip(coord, bit):
    return 1 - coord if bit else coord


def _peers(x, y, c):
    peers = []
    for mask in range(1, N_DEV):
        p = (_flip(x, mask & 4), _flip(y, mask & 2), _flip(c, mask & 1))
        peers.append((p, 4 * p[0] + 2 * p[1] + p[2]))
    return peers


def _copy_plan(mode, src, land, x, y, c):
    me = 4 * x + 2 * y + c

    def device(mask):
        p = (_flip(x, mask & 4), _flip(y, mask & 2), _flip(c, mask & 1))
        return p, 4 * p[0] + 2 * p[1] + p[2]

    if mode == "scatter":
        r = land.shape[1]
        return [(p, src.at[pl.ds(i * r, r), :], land.at[me], land.at[i])
                for p, i in map(device, (1, 2, 3, 4, 5, 6, 7, 0))]
    r = land.shape[0] // N_DEV

    def rows(i):
        return land.at[pl.ds(i * r, r), :]

    if mode == "gather":
        return [(p, src, rows(me), rows(i)) for p, i in map(device, (1, 4, 2, 6, 0))]
    sibling = device(1)[0]
    return [(sibling, rows(device(m)[1]), rows(device(m)[1]), rows(device(m | 1)[1])) for m in (4, 2, 6)]


COPIES = dict(scatter=8, gather=5, forward=3)
HBM_SPEC = BS(memory_space=pltpu.HBM)
SEM_SPEC = BS(memory_space=pltpu.SEMAPHORE)
DATAFLOW = pltpu.SideEffectType.DATAFLOW_SIDE_EFFECTING


def _fresh(shape, dtype, tag):
    def body(o_ref):
        del o_ref

    return pl.pallas_call(body, name=f"fresh_{tag}", out_specs=BS(memory_space=pl.ANY), out_shape=SDS(shape, dtype))()


def _exchange_start(mode, srcs, lands, after, tag):
    if lands is None and mode == "gather":
        lands = [_fresh((N_DEV * s.shape[0], s.shape[1]), s.dtype, f"{tag}_{a}") for a, s in enumerate(srcs)]
    elif lands is None:
        lands = [_fresh((N_DEV, s.shape[0] // N_DEV, s.shape[1]), s.dtype, f"{tag}_{a}") for a, s in enumerate(srcs)]
    n, n_src, n_cp = len(lands), len(srcs), COPIES[mode]
    behind = [] if after is None else [after]

    def body(*refs):
        src_refs, land_refs = refs[:n_src], refs[n_src:n_src + n]
        send_sems, recv_sems = refs[n_src + n + len(behind)], refs[n_src + n + len(behind) + 1]
        token = refs[-1]
        for a in range(n):
            plan = _copy_plan(mode, src_refs[a] if n_src else None, land_refs[a], *_place())
            for k, (p, out, there, _) in enumerate(plan):
                pltpu.make_async_remote_copy(
                    src_ref=out, dst_ref=there, send_sem=send_sems.at[n_cp * a + k],
                    recv_sem=recv_sems.at[n_cp * a + k], device_id=p, device_id_type=MESH).start()
        token[...] = jnp.zeros_like(token)

    both = [*srcs, *lands]
    res = pl.pallas_call(
        body, name=f"{mode}_start_{tag}",
        out_shape=(pltpu.SemaphoreType.DMA((n_cp * n,)), pltpu.SemaphoreType.DMA((n_cp * n,)),
                   *[pltpu.HBM(v.shape, v.dtype) for v in both], SDS((8, 128), F32)),
        in_specs=[HBM_SPEC] * len(both) + [BS(memory_space=pl.ANY)] * len(behind),
        out_specs=(SEM_SPEC, SEM_SPEC, *[HBM_SPEC] * len(both), BS(memory_space=pltpu.VMEM)),
        input_output_aliases={i: 2 + i for i in range(len(both))},
        compiler_params=pltpu.CompilerParams(has_side_effects=DATAFLOW),
    )(*[pltpu.with_memory_space_constraint(v, pltpu.HBM) for v in both], *behind)
    return (mode, res[0], res[1], res[2:2 + n_src], res[2 + n_src:2 + n_src + n]), res[-1]


def _exchange_wait(handle, after, tag, which=None):
    mode, send_sems, recv_sems, srcs, lands = handle
    which = list(range(len(lands))) if which is None else list(which)
    n_cp = COPIES[mode]
    lands = [lands[a] for a in which]
    srcs = [srcs[a] for a in which] if srcs else []
    n, n_src = len(lands), len(srcs)
    afters = list(after) if isinstance(after, (tuple, list)) else [after]

    def body(*refs):
        src_refs, land_refs = refs[:n_src], refs[n_src:n_src + n]
        send_ref, recv_ref = refs[n_src + n], refs[n_src + n + 1]
        for i, a in enumerate(which):
            plan = _copy_plan(mode, src_refs[i] if n_src else None, land_refs[i], *_place())
            for k, (p, out, _, here) in enumerate(plan):
                cp = pltpu.make_async_remote_copy(
                    src_ref=out, dst_ref=here, send_sem=send_ref.at[n_cp * a + k], recv_sem=recv_ref.at[n_cp * a + k],
                    device_id=p, device_id_type=MESH)
                cp.wait_send()
                cp.wait_recv()

    both = [*srcs, *lands]
    res = pl.pallas_call(
        body, name=f"{mode}_wait_{tag}", out_shape=tuple(pltpu.HBM(v.shape, v.dtype) for v in both),
        in_specs=[HBM_SPEC] * len(both) + [SEM_SPEC, SEM_SPEC] + [BS(memory_space=pl.ANY)] * len(afters),
        out_specs=tuple([HBM_SPEC] * len(both)),
        input_output_aliases={i: i for i in range(len(both))},
        compiler_params=pltpu.CompilerParams(has_side_effects=DATAFLOW),
    )(*both, send_sems, recv_sems, *afters)
    return list(res[n_src:])


def _allreduce_small(vec, behind):
    rows = vec.shape[0]

    def body(x_ref, behind_ref, o_ref, buf_ref, send_sems, recv_sems):
        x, y, c = _place()
        me = 4 * x + 2 * y + c
        buf_ref[me] = x_ref[...]
        peers = _peers(x, y, c)

        def copy(k, slot):
            return pltpu.make_async_remote_copy(
                src_ref=x_ref, dst_ref=buf_ref.at[slot], send_sem=send_sems.at[k], recv_sem=recv_sems.at[k],
                device_id=peers[k][0], device_id_type=MESH)

        sends = [copy(k, me) for k in range(N_DEV - 1)]
        for cp in sends:
            cp.start()
        for k in range(N_DEV - 1):
            copy(k, peers[k][1]).wait_recv()
        for cp in sends:
            cp.wait_send()
        acc = buf_ref[0]
        for s in range(1, N_DEV):
            acc = acc + buf_ref[s]
        o_ref[...] = acc

    vmem = BS(memory_space=pltpu.VMEM)
    return pl.pallas_call(
        body, name="allreduce_small", in_specs=[vmem, BS(memory_space=pl.ANY)], out_specs=vmem,
        out_shape=SDS((rows, 128), F32),
        scratch_shapes=[pltpu.VMEM((N_DEV, rows, 128), F32), pltpu.SemaphoreType.DMA((7,)),
                        pltpu.SemaphoreType.DMA((7,))],
        compiler_params=pltpu.CompilerParams(has_side_effects=True),
    )(vec, behind)


def _ffn_forward(x, hn, fetch, names, tag, next_g):
    gu, act = _ffn_up(hn, fetch(names[0], hn).reshape(2, F, D), tag)
    got = _mm_nn(act[None], fetch(names[1], act)[None], f"down_{tag}", res=x, scale=0.5, tm=512, next_g=next_g)
    out, hn_next = got if next_g is not None else (got, None)
    return out, hn_next, (x, hn, gu, act)


def _ffn_backward(dxo, dxo_b, saved, norm_g, wut, wd, tag, send):
    x, hn, gu, act = saved
    d_wd = _mm_tn(act[None], dxo_b, f"dwd_{tag}", scale=0.5)
    du = _ffn_dact(dxo_b, wd, gu, send(("down",), [d_wd]), tag)
    d_wut = _mm_tn(du, hn, f"dwu_{tag}")
    token = send(("up",), [d_wut])
    return _mm_nn_norm_bwd([du], wut.reshape(2 * F, D), x, norm_g + token[0, 0], dxo, tag)


def _mixer_forward(x, hn, fetch, bias, tables, tag, next_g):
    proj = _mm_nt_rows(hn, fetch("win", hn), f"proj_{tag}", 512, IN_W // 2, IN_W, 0, rope=(*tables, 2 * QKV_A // 3))
    qkr = proj
    outs, lses = [], []
    for grp in range(3):
        o, l = _dil_fwd(qkr, proj, grp)
        outs.append(o)
        lses.append(l)
    ya = _combine_fwd(outs, lses)
    yb, lse_b = _na_fwd(proj, bias)
    merged, za, zb = _merge_fwd(ya, yb, proj, fetch("wa", yb), fetch("wb", yb))
    out, hn_next = _mm_nn(merged[None], fetch("wo", merged)[None], f"out_{tag}", res=x, next_g=next_g)
    return out, hn_next, (x, hn, proj, qkr, outs, lses, ya, yb, lse_b, merged, za, zb)


def _mixer_backward(dxo, dxo_b, saved, norm_g, w, bias, tables, tag, send):
    wint, wat, wbt, wo = w
    x, hn, proj, qkr, outs, lses, ya, yb, lse_b, merged, za, zb = saved
    d_wo = _mm_tn(merged[None], dxo_b, f"dwo_{tag}")
    dza, dzb, dlog = _merge_bwd(dxo_b, wo, za, zb, proj)
    dya = _mm_nn(dza[None], wat[None], f"dya_{tag}")
    dyb = _mm_nn(dzb[None], wbt[None], f"dyb_{tag}")
    d_wat = _mm_tn(dza[None], ya, f"dwa_{tag}")
    d_wbt = _mm_tn(dzb[None], yb, f"dwb_{tag}")
    cb = _combine_bwd(dya, outs, lses)
    dqs, dks, dvs = [], [], []
    for grp in range(3):
        dq, dk, dv = _dil_bwd(qkr, proj, cb[grp], cb[3 + grp], lses[grp], grp)
        dqs.append(dq)
        dks.append(dk)
        dvs.append(dv)
    d_qkv_b, dbias_tab = _na_bwd(proj, bias, dyb, yb, lse_b)
    dbias = _na_dbias(dbias_tab)
    dproj = [_rope_bwd(dqs, dks, dvs, *tables), d_qkv_b, dlog]
    d_wint, row = None, 0
    for i, p in enumerate(dproj):
        d_wint = _mm_tn(p, hn, f"dwin{i}_{tag}", into=d_wint, row0=row, rows=IN_W)
        row += p.shape[0] * p.shape[2]
    token = send(("win", "wa", "wb", "wo"), [d_wint, d_wat, d_wbt, d_wo])
    dx, dx_b, dg = _mm_nn_norm_bwd(dproj, wint, x, norm_g + token[0, 0], dxo, f"mix_{tag}", tm=512, single_w=True)
    dbias = dbias[:, 0, :480].reshape(8, 15, 32)[:, :, :31]
    return dx, dx_b, dg, dbias


def _pack_small(norms, biases, final, loss=None):
    parts = []
    for layer in range(DEPTH):
        parts += [norms[0][layer], norms[1][layer], norms[2][layer],
                  jnp.pad(biases[layer].reshape(-1), (0, BIAS_PAD - 8 * 15 * 31))]
    parts.append(final)
    flat = jnp.concatenate([p.reshape(-1).astype(F32) for p in parts])
    if loss is not None:
        flat = jnp.concatenate([flat, loss.reshape(-1)])
    return jnp.pad(flat, (0, SMALL_ROWS * 128 - flat.shape[0])).reshape(SMALL_ROWS, 128)


def _unpack_small(packed):
    flat = packed.reshape(-1)
    norms, biases = ([], [], []), []
    pos = 0
    for _ in range(DEPTH):
        for k in range(3):
            norms[k].append(flat[pos:pos + D])
            pos += D
        biases.append(flat[pos:pos + 8 * 15 * 31].reshape(8, 15, 31))
        pos += BIAS_PAD
    final = flat[pos:pos + D]
    pos += D
    return [jnp.stack(n) for n in norms], jnp.stack(biases), final, flat[pos]


def kernel(x, ffn1_norm, ffn1_w_up, ffn1_w_down, mix_norm, w_in, na_rel_bias, w_branch_a, w_branch_b, w_out, ffn2_norm, ffn2_w_up, ffn2_w_down, final_norm, loss_target, m_ffn1_norm, m_ffn1_w_up, m_ffn1_w_down, m_mix_norm, m_w_in, m_na_rel_bias, m_w_branch_a, m_w_branch_b, m_w_out, m_ffn2_norm, m_ffn2_w_up, m_ffn2_w_down, m_final_norm, v_ffn1_norm, v_ffn1_w_up, v_ffn1_w_down, v_mix_norm, v_w_in, v_na_rel_bias, v_w_branch_a, v_w_branch_b, v_w_out, v_ffn2_norm, v_ffn2_w_up, v_ffn2_w_down, v_final_norm):
    t = x.shape[0] * x.shape[1]
    xs = x.reshape(t, D)
    tgt = loss_target.reshape(t, D)
    tables = _rope_tables()

    col_sharded = dict(up1=ffn1_w_up, win=w_in, wa=w_branch_a, wb=w_branch_b, up2=ffn2_w_up)
    row_sharded = dict(down1=ffn1_w_down, wo=w_out, down2=ffn2_w_down)
    shard = [{} for _ in range(DEPTH)]
    for layer in range(DEPTH):
        for name, arr in col_sharded.items():
            shard[layer][name] = arr[layer].T.astype(BF16)
        for name, arr in row_sharded.items():
            shard[layer][name] = arr[layer].astype(BF16)

    weights = [{} for _ in range(DEPTH)]
    travel = [(0, ("up1",)), (0, ("down1",)), (0, ("win",)), (0, ("wa", "wb", "wo")), (0, ("up2", "down2")),
              (1, ("up1", "down1")), (1, ("win",)), (1, ("wa", "wb", "wo")), (1, ("up2", "down2"))]
    group_of, chips_done, sibling_done = {}, {}, {}
    count = 0
    for i, (layer, names) in enumerate(travel):
        chips_done[i] = list(range(count, count + len(names)))
        count += len(names)
        for n in names:
            group_of[layer, n] = (i, names)
    gathered, token = _exchange_start(
        "gather", [shard[layer][n] for layer, names in travel for n in names], None, None, "w")
    zero = token[0, 0]

    biases = [_na_bias_table(na_rel_bias[layer] + zero) for layer in range(DEPTH)]

    def pass_on(i, behind):
        if i in chips_done:
            lands = _exchange_wait(gathered, behind, f"w{i}", which=chips_done.pop(i))
            sibling_done[i], _ = _exchange_start("forward", [], lands, None, f"p{i}")

    def fetcher(layer):
        def fetch(name, behind):
            if (layer, name) in group_of:
                i, names = group_of[layer, name]
                if i == 0:
                    behind = (behind, *biases)
                pass_on(i, behind)
                pass_on(i + 1, behind)
                for n, got in zip(names, _exchange_wait(sibling_done.pop(i), behind, f"p{i}")):
                    weights[layer][n] = got
                    del group_of[layer, n]
            return weights[layer][name]
        return fetch

    saved = []
    h = xs
    hn = _norm_fwd(xs, ffn1_norm[0] + zero, "first")
    for layer in range(DEPTH):
        bias = biases[layer]
        fetch = fetcher(layer)
        after_ffn2 = ffn1_norm[layer + 1] if layer + 1 < DEPTH else None
        h, hn, s1 = _ffn_forward(h, hn, fetch, ("up1", "down1"), f"f1l{layer}", mix_norm[layer])
        h, hn, s2 = _mixer_forward(h, hn, fetch, bias, tables, f"l{layer}", ffn2_norm[layer])
        h, hn, s3 = _ffn_forward(h, hn, fetch, ("up2", "down2"), f"f2l{layer}", after_ffn2)
        saved.append((s1, s2, s3, bias))
    loss_part, dh, dh_b, d_final = _loss_head(h, final_norm, tgt)

    d_norms = ([None] * DEPTH, [None] * DEPTH, [None] * DEPTH)
    d_bias = [None] * DEPTH
    sent = {}

    def sender(layer, suffix):
        def send(names, grads):
            tag = f"g{layer}{names[0]}{suffix}"
            handle, token = _exchange_start("scatter", grads, None, None, tag)
            for i, n in enumerate(names):
                sent[layer, n + suffix] = (handle, i, tag)
            return token
        return send

    for layer in reversed(range(DEPTH)):
        w = weights[layer]
        s1, s2, s3, bias = saved[layer]
        dh, dh_b, d_norms[2][layer] = _ffn_backward(
            dh, dh_b, s3, ffn2_norm[layer], w["up2"].reshape(2, F, D), w["down2"], f"f2l{layer}", sender(layer, "2"))
        dh, dh_b, d_norms[1][layer], d_bias[layer] = _mixer_backward(
            dh, dh_b, s2, mix_norm[layer], (w["win"], w["wa"], w["wb"], w["wo"]), bias, tables, f"l{layer}",
            sender(layer, ""))
        dh, dh_b, d_norms[0][layer] = _ffn_backward(
            dh, dh_b, s1, ffn1_norm[layer], w["up1"].reshape(2, F, D), w["down1"], f"f1l{layer}", sender(layer, "1"))
    grad_x = dh.reshape(x.shape)

    originals = dict(up1=(ffn1_w_up, m_ffn1_w_up, v_ffn1_w_up), down1=(ffn1_w_down, m_ffn1_w_down, v_ffn1_w_down),
                     win=(w_in, m_w_in, v_w_in), wa=(w_branch_a, m_w_branch_a, v_w_branch_a),
                     wb=(w_branch_b, m_w_branch_b, v_w_branch_b), wo=(w_out, m_w_out, v_w_out),
                     up2=(ffn2_w_up, m_ffn2_w_up, v_ffn2_w_up), down2=(ffn2_w_down, m_ffn2_w_down, v_ffn2_w_down))
    big = {}
    behind = dh
    landed = {}

    def received(layer, name):
        handle, i, tag = sent[layer, name]
        if tag not in landed:
            landed[tag] = _exchange_wait(handle, behind, tag)
        return landed[tag][i]

    for name in ("down2", "up2", "win", "wa", "wb", "wo", "down1", "up1"):
        wv, mv, vv = originals[name]
        if name in col_sharded:
            wv, mv, vv = (jnp.swapaxes(t, 1, 2) for t in (wv, mv, vv))
        big[name] = tuple(_sum_adamw(received(0, name), received(1, name), wv, mv, vv, name))
        behind = big[name][1]
        if name in col_sharded:
            big[name] = tuple(jnp.swapaxes(t, 1, 2) for t in big[name])

    small = _allreduce_small(_pack_small(d_norms, d_bias, d_final, loss_part[0, :1]), behind)
    g_norms, g_bias, g_final, loss = _unpack_small(small)
    w_small = _pack_small((ffn1_norm, mix_norm, ffn2_norm), na_rel_bias, final_norm)
    m_small = _pack_small((m_ffn1_norm, m_mix_norm, m_ffn2_norm), m_na_rel_bias, m_final_norm)
    v_small = _pack_small((v_ffn1_norm, v_mix_norm, v_ffn2_norm), v_na_rel_bias, v_final_norm)
    upd = _adamw(w_small[None], small[None], m_small[None], v_small[None], "small")
    small_out = [(g_norms, g_bias, g_final)] + [_unpack_small(u[0])[:3] for u in upd]

    outputs = [loss, grad_x]
    for kind in range(4):
        norms, bias_k, final_k = small_out[kind]
        outputs += [norms[0], big["up1"][kind], big["down1"][kind], norms[1], big["win"][kind], bias_k,
                    big["wa"][kind], big["wb"][kind], big["wo"][kind], norms[2], big["up2"][kind],
                    big["down2"][kind], final_k]
    return tuple(outputs)
```

```python
import numpy as np

import jax
import jax.numpy as jnp
from jax import lax
from jax.experimental import pallas as pl
from jax.experimental.pallas import tpu as pltpu

F32 = jnp.float32
BF16 = jnp.bfloat16
SDS = jax.ShapeDtypeStruct
BS = pl.BlockSpec
MESH = pl.DeviceIdType.MESH

D = 1024
S = 2048
F = 2816
DEPTH = 2
HEAD_DIM = 64
DILATIONS = (1, 4, 16)
HALF = 64
QKV_A = 2304
QKV_B = 1536
IN_W = 5888
N_DEV = 8
NA_ROWS = 32
GRID_W = 64
NA_KR = 8
ROPE_THETA = 10000.0
RMS_EPS = 1e-6
NEG = -1e30
SCALE = HEAD_DIM ** -0.5
ADAM_LR, ADAM_B1, ADAM_B2, ADAM_EPS, ADAM_WD, ADAM_STEP = 0.001, 0.9, 0.999, 1e-08, 0.01, 10
VMEM_LIMIT_V7X = 52 * 1024 * 1024
SMALL_ROWS = 120
BIAS_PAD = 3840
NA_FWD_ROWS = 8
NA_BWD_ROWS = 4
DIL_FWD_TILES = 8
DIL_BWD_TILES = 4


def _cp(*sem):
    return pltpu.CompilerParams(dimension_semantics=sem, vmem_limit_bytes=VMEM_LIMIT_V7X)


def _dot_nn(a, b):
    return jnp.dot(a, b, preferred_element_type=F32)


def _dot_nt(a, b):
    return lax.dot_general(a, b, (((1,), (1,)), ((), ())), preferred_element_type=F32)


def _dot_tn(a, b):
    return lax.dot_general(a, b, (((0,), (0,)), ((), ())), preferred_element_type=F32)


def _ds(start, size, stride):
    return pl.ds(start, size) if stride == 1 else pl.ds(start, size, stride=stride)


def _norm_fwd(x, g, tag):
    t = x.shape[0]
    tm = 512

    def body(x_ref, g_ref, o_ref):
        xv = x_ref[...]
        r = lax.rsqrt(jnp.mean(xv * xv, axis=-1, keepdims=True) + RMS_EPS)
        o_ref[...] = (xv * r * g_ref[...]).astype(BF16)

    return pl.pallas_call(
        body, name=f"norm_fwd_{tag}", grid=(t // tm,),
        in_specs=[BS((tm, D), lambda i: (i, 0)), BS((1, D), lambda i: (0, 0))],
        out_specs=BS((tm, D), lambda i: (i, 0)),
        out_shape=SDS((t, D), BF16), compiler_params=_cp("parallel"),
    )(x, g.reshape(1, D))


def _loss_head(x, g, tgt):
    t = x.shape[0]
    tm = 1024

    def body(x_ref, g_ref, t_ref, loss_ref, dx_ref, dxb_ref, dg_ref):
        @pl.when(pl.program_id(0) == 0)
        def _():
            dg_ref[...] = jnp.zeros_like(dg_ref)
            loss_ref[...] = jnp.zeros_like(loss_ref)

        xv = x_ref[...]
        gv = g_ref[...]
        r = lax.rsqrt(jnp.mean(xv * xv, axis=-1, keepdims=True) + RMS_EPS)
        xh = xv * r
        e = xh * gv - t_ref[...]
        loss_ref[...] += 0.5 * jnp.sum(jnp.mean(e * e, axis=-1, keepdims=True), axis=0, keepdims=True)
        dy = e * (1.0 / D)
        u = dy * gv
        dx = r * (u - xh * jnp.mean(xh * u, axis=-1, keepdims=True))
        dx_ref[...] = dx
        dxb_ref[...] = dx.astype(BF16)
        dg_ref[...] += jnp.sum(dy * xh, axis=0, keepdims=True)

    row = BS((tm, D), lambda i: (i, 0))
    vec = BS((1, D), lambda i: (0, 0))
    return pl.pallas_call(
        body, name="loss_head", grid=(t // tm,),
        in_specs=[row, vec, row], out_specs=[BS((1, 128), lambda i: (0, 0)), row, row, vec],
        out_shape=[SDS((1, 128), F32), SDS((t, D), F32), SDS((t, D), BF16), SDS((1, D), F32)],
        compiler_params=_cp("arbitrary"),
    )(x, g.reshape(1, D), tgt)


def _mm_nn(a, w, tag, res=None, scale=1.0, tm=1024, tn=None, next_g=None):
    c_n, t, k = a.shape
    n = w.shape[2]
    tn = n if tn is None else tn
    assert next_g is None or tn == n
    n_in = 2 + (res is not None) + (next_g is not None)

    def body(*refs):
        a_ref, w_ref = refs[0], refs[1]
        acc = _dot_nn(a_ref[0].astype(BF16), w_ref[0])
        for c in range(1, c_n):
            acc = acc + _dot_nn(a_ref[c].astype(BF16), w_ref[c])
        if scale != 1.0:
            acc = acc * scale
        if res is not None:
            acc = refs[2][...] + acc
        refs[n_in][...] = acc
        if next_g is not None:
            r = lax.rsqrt(jnp.mean(acc * acc, axis=-1, keepdims=True) + RMS_EPS)
            refs[n_in + 1][...] = (acc * r * refs[n_in - 1][...]).astype(BF16)

    w_mode = dict(pipeline_mode=pl.Buffered(1)) if tn == n else {}
    in_specs = [BS((c_n, tm, k), lambda i, j: (0, i, 0)), BS((c_n, k, tn), lambda i, j: (0, 0, j), **w_mode)]
    args = [a, w]
    out_specs = [BS((tm, tn), lambda i, j: (i, j))]
    out_shape = [SDS((t, n), F32)]
    if res is not None:
        in_specs.append(BS((tm, tn), lambda i, j: (i, j)))
        args.append(res)
    if next_g is not None:
        in_specs.append(BS((1, n), lambda i, j: (0, 0)))
        args.append(next_g.reshape(1, n))
        out_specs.append(BS((tm, tn), lambda i, j: (i, j)))
        out_shape.append(SDS((t, n), BF16))
    got = pl.pallas_call(
        body, name=f"mm_nn_{tag}", grid=(t // tm, n // tn), in_specs=in_specs, out_specs=out_specs,
        out_shape=out_shape, compiler_params=_cp("parallel", "parallel"),
    )(*args)
    return got if next_g is not None else got[0]


def _mm_nn_norm_bwd(parts, w, x, g, dres, tag, tm=256, single_w=False):
    t = parts[0].shape[1]
    n_parts = len(parts)

    def body(*refs):
        w_ref, x_ref, g_ref, dr_ref, dx_ref, dxb_ref, dg_ref = refs[n_parts:]

        @pl.when(pl.program_id(0) == 0)
        def _():
            dg_ref[...] = jnp.zeros_like(dg_ref)

        dh = None
        row = 0
        for a_ref, part in zip(refs, parts):
            for c in range(part.shape[0]):
                term = _dot_nn(a_ref[c].astype(BF16), w_ref[row:row + part.shape[2], :])
                dh = term if dh is None else dh + term
                row += part.shape[2]
        xv = x_ref[...]
        r = lax.rsqrt(jnp.mean(xv * xv, axis=-1, keepdims=True) + RMS_EPS)
        xh = xv * r
        u = dh * g_ref[...]
        dx = dr_ref[...] + r * (u - xh * jnp.mean(xh * u, axis=-1, keepdims=True))
        dx_ref[...] = dx
        dxb_ref[...] = dx.astype(BF16)
        dg_ref[...] += jnp.sum(dh * xh, axis=0, keepdims=True)

    row = BS((tm, D), lambda i: (i, 0))
    vec = BS((1, D), lambda i: (0, 0))
    return pl.pallas_call(
        body, name=f"mm_nn_norm_bwd_{tag}", grid=(t // tm,),
        in_specs=[BS((p.shape[0], tm, p.shape[2]), lambda i: (0, i, 0)) for p in parts]
        + [BS(w.shape, lambda i: (0, 0), **(dict(pipeline_mode=pl.Buffered(1)) if single_w else {})), row, vec, row],
        out_specs=[row, row, vec], out_shape=[SDS((t, D), F32), SDS((t, D), BF16), SDS((1, D), F32)],
        compiler_params=_cp("arbitrary"),
    )(*parts, w, x, g.reshape(1, D), dres)


def _mm_nt_rows(a, w, tag, tm, tn, n_total, w_row0, rope=None):
    t, k = a.shape
    assert w_row0 % tn == 0 and n_total % tn == 0
    j0 = w_row0 // tn

    def body(a_ref, w_ref, *rest):
        o_ref = rest[-1]
        o_ref[...] = _dot_nt(a_ref[...].astype(BF16), w_ref[...])
        if rope is not None:
            @pl.when(pl.program_id(0) == 0)
            def _():
                c = rest[0][...]
                sg = rest[1][...]
                first = (lax.broadcasted_iota(jnp.int32, (tm, 128), 1) % HEAD_DIM) < HEAD_DIM // 2
                for col in range(0, rope[2], 128):
                    v = o_ref[:, col:col + 128]
                    o_ref[:, col:col + 128] = v * c + _swap_halves(v, first) * sg

    in_specs = [BS((tm, k), lambda j, i: (i, 0)), BS((tn, k), lambda j, i: (j0 + j, 0))]
    args = [a, w]
    if rope is not None:
        assert rope[2] <= tn
        in_specs += [BS((tm, 128), lambda j, i: (i % (S // tm), 0))] * 2
        args += [rope[0], rope[1]]
    return pl.pallas_call(
        body, name=f"mm_nt_{tag}", grid=(n_total // tn, t // tm), in_specs=in_specs,
        out_specs=BS((tm, tn), lambda j, i: (i, j)), out_shape=SDS((t, n_total), F32),
        compiler_params=_cp("parallel", "parallel"),
    )(*args)


def _mm_tn(a, b, tag, scale=1.0, tmm=None, into=None, row0=0, rows=None):
    c_n, t, m = a.shape
    n = b.shape[1]
    if tmm is None:
        tmm = max(w for w in (768, 512, 256) if m % w == 0 and row0 % w == 0)
    tiles = m // tmm
    block0 = row0 // tmm
    assert row0 % tmm == 0 and m % tmm == 0

    def body(a_ref, b_ref, *rest):
        rest[-1][...] = (_dot_tn(a_ref[...].astype(BF16), b_ref[...].astype(BF16)) * scale).astype(BF16)

    in_specs = [BS((None, t, tmm), lambda c, mi: (c, 0, mi)), BS((t, n), lambda c, mi: (0, 0))]
    args = [a, b]
    if into is not None:
        in_specs.append(BS(memory_space=pl.ANY))
        args.append(into)
    return pl.pallas_call(
        body, name=f"mm_tn_{tag}", grid=(c_n, tiles), in_specs=in_specs,
        out_specs=BS((tmm, n), lambda c, mi: (block0 + c * tiles + mi, 0)),
        out_shape=SDS((rows or c_n * m, n) if into is None else into.shape, BF16),
        input_output_aliases={} if into is None else {2: 0},
        compiler_params=_cp("parallel", "parallel"),
    )(*args)


def _ffn_up(hn, wut, tag):
    t = hn.shape[0]
    tm, tn = 1024, 1408

    def body(h_ref, w_ref, gu_ref, act_ref):
        h = h_ref[...]
        g = _dot_nt(h, w_ref[0])
        u = _dot_nt(h, w_ref[1])
        sg = jax.nn.sigmoid(g)
        silu = g * sg
        gu_ref[0] = (u * (sg + silu * (1.0 - sg))).astype(BF16)
        gu_ref[1] = silu.astype(BF16)
        act_ref[...] = (silu * u).astype(BF16)

    return pl.pallas_call(
        body, name=f"ffn_up_{tag}", grid=(F // tn, t // tm),
        in_specs=[BS((tm, D), lambda j, i: (i, 0)), BS((2, tn, D), lambda j, i: (0, j, 0))],
        out_specs=[BS((2, tm, tn), lambda j, i: (0, i, j)), BS((tm, tn), lambda j, i: (i, j))],
        out_shape=[SDS((2, t, F), BF16), SDS((t, F), BF16)],
        compiler_params=_cp("parallel", "parallel"),
    )(hn, wut)


def _ffn_dact(dxo, wd, gu, tie, tag):
    t = dxo.shape[0]
    tm, tn = 1024, 1408

    def body(d_ref, w_ref, gu_ref, tie_ref, o_ref):
        dact = _dot_nt(d_ref[...] * 0.5, w_ref[...])
        o_ref[0] = (dact * gu_ref[0].astype(F32)).astype(BF16)
        o_ref[1] = (dact * gu_ref[1].astype(F32)).astype(BF16)

    return pl.pallas_call(
        body, name=f"ffn_dact_{tag}", grid=(F // tn, t // tm),
        in_specs=[BS((tm, D), lambda j, i: (i, 0)), BS((tn, D), lambda j, i: (j, 0)),
                  BS((2, tm, tn), lambda j, i: (0, i, j)), BS((8, 128), lambda j, i: (0, 0))],
        out_specs=BS((2, tm, tn), lambda j, i: (0, i, j)),
        out_shape=SDS((2, t, F), BF16), compiler_params=_cp("parallel", "parallel"),
    )(dxo, wd, gu, tie)


def _rope_tables():
    half = HEAD_DIM // 2
    inv_freq = ROPE_THETA ** (-jnp.arange(half, dtype=F32) / half)
    ang = jnp.arange(S).astype(F32)[:, None] * inv_freq[None, :]
    cos, sin = jnp.cos(ang), jnp.sin(ang)
    return jnp.concatenate([cos, cos, cos, cos], axis=1), jnp.concatenate([-sin, sin, -sin, sin], axis=1)


def _swap_halves(t, first_half):
    return jnp.where(first_half, pltpu.roll(t, 96, 1), pltpu.roll(t, 32, 1))


def _rope_bwd(dqs, dks, dvs, cos_t, sin_t):
    t = dqs[0].shape[0]
    tm = 1024

    def body(*refs):
        c = refs[9][...]
        sg = refs[10][...]
        o_ref = refs[11]
        first = (lax.broadcasted_iota(jnp.int32, (tm, 128), 1) % HEAD_DIM) < HEAD_DIM // 2
        for a in range(6):
            for hp in range(2):
                v = refs[a][:, 128 * hp:128 * (hp + 1)]
                col = 128 * (2 * a + hp)
                o_ref[:, col:col + 128] = (v * c + _swap_halves(v * sg, first)).astype(BF16)
        for a in range(6, 9):
            o_ref[:, 256 * a:256 * (a + 1)] = refs[a][...].astype(BF16)

    blk = BS((tm, 256), lambda i: (i, 0))
    tab = BS((tm, 128), lambda i: (i % (S // tm), 0))
    return pl.pallas_call(
        body, name="rope_bwd", grid=(t // tm,), in_specs=[blk] * 9 + [tab, tab],
        out_specs=BS((None, tm, QKV_A), lambda i: (0, i, 0)), out_shape=SDS((1, t, QKV_A), BF16),
        compiler_params=_cp("parallel"),
    )(*dqs, *dks, *dvs, cos_t, sin_t)


def _head_masks():
    lane = lax.broadcasted_iota(jnp.int32, (1, 128), 1)
    m0 = (lane < HEAD_DIM).astype(F32)
    return m0, 1.0 - m0


def _dil_geometry(d):
    sub = S // d
    q_rows = 128
    k_rows = min(256, sub)
    return sub, q_rows, sub // q_rows, k_rows


def _dil_tile(idx, d, keys_on_rows=False):
    sub, q_rows, nb, k_rows = _dil_geometry(d)
    r = idx // nb
    n = idx % nb
    k_sub = jnp.clip(q_rows * n - HALF, 0, sub - k_rows)
    if d == 1:
        q_start = pl.multiple_of(q_rows * n, q_rows)
        k_start = pl.multiple_of(k_sub, HALF)
    else:
        q_start = q_rows * n * d + r
        k_start = k_sub * d + r
    if keys_on_rows:
        ii = lax.broadcasted_iota(jnp.int32, (k_rows, 2 * q_rows), 1) % q_rows
        jj = lax.broadcasted_iota(jnp.int32, (k_rows, 2 * q_rows), 0)
    else:
        ii = lax.broadcasted_iota(jnp.int32, (q_rows, k_rows), 0)
        jj = lax.broadcasted_iota(jnp.int32, (q_rows, k_rows), 1)
    valid = jnp.abs(jj - ii + (k_sub - q_rows * n)) <= HALF
    return q_start, k_start, valid


def _dil_specs(grp):
    qs = BS((S, 128), lambda b, hp: (b, 2 * grp + hp))
    ks = BS((S, 128), lambda b, hp: (b, 6 + 2 * grp + hp))
    vs = BS((S, 128), lambda b, hp: (b, 12 + 2 * grp + hp))
    own = BS((S, 128), lambda b, hp: (b, hp))
    return qs, ks, vs, own


def _dil_fwd(qkr, proj, grp):
    t = qkr.shape[0]
    d = DILATIONS[grp]
    _, q_rows, nb, k_rows = _dil_geometry(d)

    def body(q_ref, k_ref, v_ref, o_ref, l_ref):
        masks = _head_masks()

        def step(i0, carry):
            geo = [_dil_tile(i0 * DIL_FWD_TILES + j, d) for j in range(DIL_FWD_TILES)]
            tiles = [(j, h) for j in range(DIL_FWD_TILES) for h in range(2)]
            qs = [q_ref[_ds(g[0], q_rows, d), :] for g in geo]
            kbs = [k_ref[_ds(g[1], k_rows, d), :].astype(BF16) for g in geo]
            ss = [jnp.where(geo[j][2], _dot_nt((qs[j] * masks[h]).astype(BF16), kbs[j]) * SCALE, NEG) for j, h in tiles]
            mxs = [jnp.max(s, axis=1, keepdims=True) for s in ss]
            ps = [jnp.exp(s - mx) for s, mx in zip(ss, mxs)]
            dens = [jnp.sum(p, axis=1, keepdims=True) for p in ps]
            vs = [v_ref[_ds(g[1], k_rows, d), :] for g in geo]
            outs = [_dot_nn(p.astype(BF16), (vs[j] * masks[h]).astype(BF16)) / den
                    for p, den, (j, h) in zip(ps, dens, tiles)]
            for j, g in enumerate(geo):
                o_ref[_ds(g[0], q_rows, d), :] = outs[2 * j] + outs[2 * j + 1]
                l_ref[_ds(g[0], q_rows, d), :] = (
                    (mxs[2 * j] + jnp.log(dens[2 * j])) * masks[0] + (mxs[2 * j + 1] + jnp.log(dens[2 * j + 1])) * masks[1])
            return carry

        lax.fori_loop(0, d * nb // DIL_FWD_TILES, step, 0)

    qs, ks, vs, own = _dil_specs(grp)
    return pl.pallas_call(
        body, name=f"dil_fwd_{grp}", grid=(t // S, 2), in_specs=[qs, ks, vs], out_specs=[own, own],
        out_shape=[SDS((t, 256), F32), SDS((t, 256), F32)], compiler_params=_cp("parallel", "parallel"),
    )(qkr, qkr, proj)


def _dil_bwd(qkr, proj, do, dlp, lse, grp):
    t = qkr.shape[0]
    d = DILATIONS[grp]
    _, q_rows, nb, k_rows = _dil_geometry(d)

    def body(q_ref, k_ref, v_ref, do_ref, dl_ref, l_ref, dq_ref, dk_ref, dv_ref):
        masks = _head_masks()
        dk_ref[...] = jnp.zeros_like(dk_ref)
        dv_ref[...] = jnp.zeros_like(dv_ref)

        def as_row(x2):
            xt = x2.T
            return jnp.concatenate([xt[0:1], xt[HEAD_DIM:HEAD_DIM + 1]], axis=1)

        def step(i0, carry):
            geo = [_dil_tile(i0 * DIL_BWD_TILES + j, d, keys_on_rows=True) for j in range(DIL_BWD_TILES)]
            q_ds = [_ds(g[0], q_rows, d) for g in geo]
            k_ds = [_ds(g[1], k_rows, d) for g in geo]
            qbs = [_both_heads(q_ref[r, :], masks).astype(BF16) for r in q_ds]
            kbs = [k_ref[r, :].astype(BF16) for r in k_ds]
            vbs = [v_ref[r, :].astype(BF16) for r in k_ds]
            dobs = [_both_heads(do_ref[r, :], masks).astype(BF16) for r in q_ds]
            l_rows = [as_row(l_ref[r, :]) for r in q_ds]
            dl_rows = [as_row(dl_ref[r, :]) for r in q_ds]
            ss = [jnp.where(g[2], _dot_nt(kb, qb) * SCALE, NEG) for g, kb, qb in zip(geo, kbs, qbs)]
            ps = [jnp.exp(s - lr) for s, lr in zip(ss, l_rows)]
            dps = [_dot_nt(vb, dob) for vb, dob in zip(vbs, dobs)]
            dss = [(p * (dp - dr)).astype(BF16) for p, dp, dr in zip(ps, dps, dl_rows)]
            dks = [_dot_nn(ds, qb) for ds, qb in zip(dss, qbs)]
            dvs = [_dot_nn(p.astype(BF16), dob) for p, dob in zip(ps, dobs)]
            dqs = [_own_heads(_dot_tn(ds, kb), masks) for ds, kb in zip(dss, kbs)]
            for j in range(DIL_BWD_TILES):
                dq_ref[q_ds[j], :] = dqs[j] * SCALE
                dk_ref[k_ds[j], :] += dks[j] * SCALE
                dv_ref[k_ds[j], :] += dvs[j]
            return carry

        lax.fori_loop(0, d * nb // DIL_BWD_TILES, step, 0)

    qs, ks, vs, own = _dil_specs(grp)
    return pl.pallas_call(
        body, name=f"dil_bwd_{grp}", grid=(t // S, 2), in_specs=[qs, ks, vs, own, own, own],
        out_specs=[own, own, own], out_shape=[SDS((t, 256), F32)] * 3,
        compiler_params=_cp("parallel", "parallel"),
    )(qkr, qkr, proj, do, dlp, lse)


def _mix_weights(l0, l1, l2):
    mx = jnp.maximum(jnp.maximum(l0, l1), l2)
    e0, e1, e2 = jnp.exp(l0 - mx), jnp.exp(l1 - mx), jnp.exp(l2 - mx)
    den = e0 + e1 + e2
    return e0 / den, e1 / den, e2 / den


def _combine_fwd(outs, lses):
    t = outs[0].shape[0]
    tm = 1024

    def body(o0, o1, o2, l0, l1, l2, y_ref):
        w0, w1, w2 = _mix_weights(l0[...], l1[...], l2[...])
        y_ref[...] = w0 * o0[...] + w1 * o1[...] + w2 * o2[...]

    blk = BS((tm, 256), lambda i: (i, 0))
    return pl.pallas_call(
        body, name="combine_fwd", grid=(t // tm,), in_specs=[blk] * 6, out_specs=blk,
        out_shape=SDS((t, 256), F32), compiler_params=_cp("parallel"),
    )(*outs, *lses)


def _head_sum(x):
    a = lax.broadcasted_iota(jnp.int32, (256, 256), 0) // HEAD_DIM
    b = lax.broadcasted_iota(jnp.int32, (256, 256), 1) // HEAD_DIM
    ones = (a == b).astype(BF16)
    hi = x.astype(BF16)
    lo = (x - hi.astype(F32)).astype(BF16)
    return _dot_nn(hi, ones) + _dot_nn(lo, ones)


def _combine_bwd(dya, outs, lses):
    t = dya.shape[0]
    tm = 1024

    def body(dy_ref, o0, o1, o2, l0, l1, l2, d0, d1, d2, e0, e1, e2):
        ws = _mix_weights(l0[...], l1[...], l2[...])
        dy = dy_ref[...]
        ya = ws[0] * o0[...] + ws[1] * o1[...] + ws[2] * o2[...]
        hs = _head_sum(dy * ya)
        for w, d_ref, e_ref in zip(ws, (d0, d1, d2), (e0, e1, e2)):
            d_ref[...] = w * dy
            e_ref[...] = w * hs

    blk = BS((tm, 256), lambda i: (i, 0))
    return pl.pallas_call(
        body, name="combine_bwd", grid=(t // tm,), in_specs=[blk] * 7, out_specs=[blk] * 6,
        out_shape=[SDS((t, 256), F32)] * 6, compiler_params=_cp("parallel"),
    )(dya, *outs, *lses)


def _na_bias_table(rel_bias):
    kw = NA_KR * GRID_W
    rev = jnp.pad(rel_bias.astype(F32)[:, :, ::-1], ((0, 0), (0, 0), (0, 128 - 31)))

    def body(r_ref, o_ref):
        lane = lax.broadcasted_iota(jnp.int32, (GRID_W, 128), 1)
        j = lax.broadcasted_iota(jnp.int32, (GRID_W, 128), 0)
        q = lane % GRID_W
        win_lo = jnp.clip(q - 8, 0, GRID_W - 16)
        valid = (j >= win_lo) & (j < win_lo + 16)
        for cls in range(NA_KR):
            for k in range(NA_KR):
                tiles = []
                for h in range(2):
                    row = jnp.broadcast_to(r_ref[h, cls + k:cls + k + 1, :], (GRID_W, 128))
                    tiles.append(pltpu.roll(row, (128 - 15 + GRID_W * h) % 128, 1, stride=1, stride_axis=0))
                o_ref[cls, GRID_W * k:GRID_W * (k + 1), :] = jnp.where(
                    valid, jnp.where(lane < GRID_W, tiles[0], tiles[1]), NEG)

    return pl.pallas_call(
        body, name="na_bias_table", grid=(4,),
        in_specs=[BS((2, 2 * NA_KR - 1, 128), lambda hp: (hp, 0, 0))],
        out_specs=BS((None, NA_KR, kw, 128), lambda hp: (hp, 0, 0, 0)),
        out_shape=SDS((4, NA_KR, kw, 128), F32), compiler_params=_cp("parallel"),
    )(rev)


def _na_row(i):
    lo = jnp.clip(i - NA_KR // 2, 0, NA_ROWS - NA_KR)
    return pl.multiple_of(GRID_W * i, GRID_W), pl.multiple_of(GRID_W * lo, GRID_W), lo - i + NA_KR - 1


def _both_heads(x, masks):
    return jnp.concatenate([x * masks[0], x * masks[1]], axis=0)


def _own_heads(r, masks):
    half = r.shape[0] // 2
    return r[:half] * masks[0] + r[half:] * masks[1]


def _na_fwd(proj, bias):
    t = proj.shape[0]
    kw = NA_KR * GRID_W

    def body(q_ref, k_ref, v_ref, b_ref, o_ref, l_ref):
        masks = _head_masks()

        def step(i0, carry):
            idx = [i0 * NA_FWD_ROWS + j for j in range(NA_FWD_ROWS)]
            rows = [_na_row(i) for i in idx]
            qbs = [_both_heads(q_ref[pl.ds(r[0], GRID_W), :], masks).astype(BF16) for r in rows]
            kbs = [k_ref[pl.ds(r[1], kw), :].astype(BF16) for r in rows]
            ss = [_dot_nt(kb, qb) * SCALE + b_ref[r[2]] for kb, qb, r in zip(kbs, qbs, rows)]
            mxs = [jnp.max(s, axis=0, keepdims=True) for s in ss]
            ps = [jnp.exp(s - mx) for s, mx in zip(ss, mxs)]
            dens = [jnp.sum(p, axis=0, keepdims=True) for p in ps]
            pbs = [(p / den).astype(BF16) for p, den in zip(ps, dens)]
            vbs = [v_ref[pl.ds(r[1], kw), :].astype(BF16) for r in rows]
            outs = [_own_heads(_dot_tn(pb, vb), masks) for pb, vb in zip(pbs, vbs)]
            for j, r in enumerate(rows):
                o_ref[pl.ds(r[0], GRID_W), :] = outs[j]
                l_ref[pl.ds(idx[j], 1), :] = mxs[j] + jnp.log(dens[j])
            return carry

        lax.fori_loop(0, NA_ROWS // NA_FWD_ROWS, step, 0)

    c0 = QKV_A // 128
    return pl.pallas_call(
        body, name="na_fwd", grid=(t // S, 4),
        in_specs=[BS((S, 128), lambda b, hp: (b, c0 + hp)), BS((S, 128), lambda b, hp: (b, c0 + 4 + hp)),
                  BS((S, 128), lambda b, hp: (b, c0 + 8 + hp)),
                  BS((None, NA_KR, kw, 128), lambda b, hp: (hp, 0, 0, 0))],
        out_specs=[BS((S, 128), lambda b, hp: (b, hp)), BS((None, None, NA_ROWS, 128), lambda b, hp: (b, hp, 0, 0))],
        out_shape=[SDS((t, 512), F32), SDS((t // S, 4, NA_ROWS, 128), F32)],
        compiler_params=_cp("parallel", "parallel"),
    )(proj, proj, proj, bias)


def _na_bwd(proj, bias, dyb, yb, lse):
    t = proj.shape[0]
    kw = NA_KR * GRID_W

    def body(q_ref, k_ref, v_ref, b_ref, do_ref, o_ref, l_ref, d_ref, db_ref):
        masks = _head_masks()
        ones = jnp.ones((8, 128), BF16)

        @pl.when(pl.program_id(1) == 0)
        def _():
            db_ref[...] = jnp.zeros_like(db_ref)

        d_ref[1:3] = jnp.zeros((2, S, 128), F32)

        def row_sums(x):
            hi = x.astype(BF16)
            lo = (x - hi.astype(F32)).astype(BF16)
            return (_dot_nt(ones, hi) + _dot_nt(ones, lo))[0:1]

        def step(i0, carry):
            idx = [i0 * NA_BWD_ROWS + j for j in range(NA_BWD_ROWS)]
            rows = [_na_row(i) for i in idx]
            q_ds = [pl.ds(r[0], GRID_W) for r in rows]
            k_ds = [pl.ds(r[1], kw) for r in rows]
            qbs = [_both_heads(q_ref[r, :], masks).astype(BF16) for r in q_ds]
            kbs = [k_ref[r, :].astype(BF16) for r in k_ds]
            vbs = [v_ref[r, :].astype(BF16) for r in k_ds]
            dos = [do_ref[r, :] for r in q_ds]
            dobs = [_both_heads(do, masks).astype(BF16) for do in dos]
            deltas = [row_sums(_both_heads(do * o_ref[r, :], masks)) for do, r in zip(dos, q_ds)]
            ss = [_dot_nt(kb, qb) * SCALE + b_ref[r[2]] for kb, qb, r in zip(kbs, qbs, rows)]
            ps = [jnp.exp(s - l_ref[pl.ds(i, 1), :]) for s, i in zip(ss, idx)]
            dps = [_dot_nt(vb, dob) for vb, dob in zip(vbs, dobs)]
            dss = [p * (dp - delta) for p, dp, delta in zip(ps, dps, deltas)]
            for ds, r in zip(dss, rows):
                db_ref[r[2]] += ds
            dsbs = [ds.astype(BF16) for ds in dss]
            dks = [_dot_nn(dsb, qb) for dsb, qb in zip(dsbs, qbs)]
            dvs = [_dot_nn(p.astype(BF16), dob) for p, dob in zip(ps, dobs)]
            dqs = [_own_heads(_dot_tn(dsb, kb), masks) for dsb, kb in zip(dsbs, kbs)]
            for j in range(NA_BWD_ROWS):
                d_ref[0, q_ds[j], :] = dqs[j] * SCALE
                d_ref[1, k_ds[j], :] += dks[j] * SCALE
                d_ref[2, k_ds[j], :] += dvs[j]
            return carry

        lax.fori_loop(0, NA_ROWS // NA_BWD_ROWS, step, 0)

    c0 = QKV_A // 128
    own = BS((S, 128), lambda hp, b: (b, hp))
    tab = BS((None, NA_KR, kw, 128), lambda hp, b: (hp, 0, 0, 0))
    return pl.pallas_call(
        body, name="na_bwd", grid=(4, t // S),
        in_specs=[BS((S, 128), lambda hp, b: (b, c0 + hp)), BS((S, 128), lambda hp, b: (b, c0 + 4 + hp)),
                  BS((S, 128), lambda hp, b: (b, c0 + 8 + hp)), tab, own, own,
                  BS((None, None, NA_ROWS, 128), lambda hp, b: (b, hp, 0, 0))],
        out_specs=[BS((3, S, 128), lambda hp, b: (0, b, hp)), tab],
        out_shape=[SDS((3, t, 512), F32), SDS((4, NA_KR, kw, 128), F32)],
        compiler_params=_cp("parallel", "arbitrary"),
    )(proj, proj, proj, bias, dyb, yb, lse)


def _na_dbias_lane_map():
    kw = NA_KR * GRID_W
    lane = np.arange(kw)
    blk, m = lane // GRID_W, lane % GRID_W
    target = np.full(kw, -1)
    target[m < 16] = (blk * 32 + 15 + m)[m < 16]
    target[m >= 49] = (((blk + 1) % NA_KR) * 32 + m - 49)[m >= 49]
    return jnp.asarray(target[:, None] == np.arange(kw)[None, :], BF16)


def _na_dbias(db):
    kw = NA_KR * GRID_W

    def body(x_ref, map_ref, o_ref, z_ref):
        for cls in range(NA_KR):
            xt = x_ref[cls].T
            for h in range(2):
                xv = xt[GRID_W * h:GRID_W * (h + 1)]
                y = xv[0:8]
                for g in range(1, GRID_W // 8):
                    y = y + pltpu.roll(xv[8 * g:8 * g + 8], kw - 8 * g, 1)
                d = y[0:1]
                for s in range(1, 8):
                    d = d + pltpu.roll(y[s:s + 1], kw - s, 1)
                z_ref[h, cls:cls + 1, :] = d
        for h in range(2):
            z = z_ref[h]
            hi = z.astype(BF16)
            lo = (z - hi.astype(F32)).astype(BF16)
            e = _dot_nn(hi, map_ref[...]) + _dot_nn(lo, map_ref[...])
            out = e[0:1]
            for cls in range(1, NA_KR):
                out = out + pltpu.roll(e[cls:cls + 1], 32 * cls, 1)
            o_ref[h] = jnp.broadcast_to(out, (8, kw))

    return pl.pallas_call(
        body, name="na_dbias", grid=(4,),
        in_specs=[BS((None, NA_KR, kw, 128), lambda hp: (hp, 0, 0, 0)), BS((kw, kw), lambda hp: (0, 0))],
        out_specs=BS((2, 8, kw), lambda hp: (hp, 0, 0)), out_shape=SDS((8, 8, kw), F32),
        scratch_shapes=[pltpu.VMEM((2, 8, kw), F32)], compiler_params=_cp("parallel"),
    )(db, _na_dbias_lane_map())


def _merge_fwd(ya, yb, proj, wat, wbt):
    t = ya.shape[0]
    tm, tn = 1024, 256
    ca = (QKV_A + QKV_B) // tn
    cb = ca + D // tn

    def body(ya_ref, yb_ref, la_ref, lb_ref, wa_ref, wb_ref, m_ref, za_ref, zb_ref):
        za = _dot_nt(ya_ref[...].astype(BF16), wa_ref[...])
        zb = _dot_nt(yb_ref[...].astype(BF16), wb_ref[...])
        m_ref[...] = (jax.nn.sigmoid(la_ref[...]) * za + jax.nn.sigmoid(lb_ref[...]) * zb).astype(BF16)
        za_ref[...] = za.astype(BF16)
        zb_ref[...] = zb.astype(BF16)

    out = BS((tm, tn), lambda i, j: (i, j))
    return pl.pallas_call(
        body, name="merge_fwd", grid=(t // tm, D // tn),
        in_specs=[BS((tm, 256), lambda i, j: (i, 0)), BS((tm, 512), lambda i, j: (i, 0)),
                  BS((tm, tn), lambda i, j: (i, ca + j)), BS((tm, tn), lambda i, j: (i, cb + j)),
                  BS((tn, 256), lambda i, j: (j, 0)), BS((tn, 512), lambda i, j: (j, 0))],
        out_specs=[out, out, out], out_shape=[SDS((t, D), BF16)] * 3,
        compiler_params=_cp("parallel", "parallel"),
    )(ya, yb, proj, proj, wat, wbt)


def _merge_bwd(dxo, wo, za, zb, proj):
    t = dxo.shape[0]
    tm, tn = 1024, 256
    ca = (QKV_A + QKV_B) // tn
    cb = ca + D // tn

    def body(d_ref, w_ref, za_ref, zb_ref, la_ref, lb_ref, dza_ref, dzb_ref, dl_ref):
        dmv = _dot_nt(d_ref[...], w_ref[...])
        ga = jax.nn.sigmoid(la_ref[...])
        gb = jax.nn.sigmoid(lb_ref[...])
        dza_ref[...] = (dmv * ga).astype(BF16)
        dzb_ref[...] = (dmv * gb).astype(BF16)
        dl_ref[0] = (dmv * za_ref[...].astype(F32) * ga * (1.0 - ga)).astype(BF16)
        dl_ref[1] = (dmv * zb_ref[...].astype(F32) * gb * (1.0 - gb)).astype(BF16)

    blk = BS((tm, tn), lambda i, j: (i, j))
    return pl.pallas_call(
        body, name="merge_bwd", grid=(t // tm, D // tn),
        in_specs=[BS((tm, D), lambda i, j: (i, 0)), BS((tn, D), lambda i, j: (j, 0)), blk, blk,
                  BS((tm, tn), lambda i, j: (i, ca + j)), BS((tm, tn), lambda i, j: (i, cb + j))],
        out_specs=[blk, blk, BS((2, tm, tn), lambda i, j: (0, i, j))],
        out_shape=[SDS((t, D), BF16), SDS((t, D), BF16), SDS((2, t, D), BF16)],
        compiler_params=_cp("parallel", "parallel"),
    )(dxo, wo, za, zb, proj, proj)


def _adamw_update(w, g, m, v):
    mn = ADAM_B1 * m + (1.0 - ADAM_B1) * g
    vn = ADAM_B2 * v + (1.0 - ADAM_B2) * (g * g)
    m_hat = mn / (1.0 - ADAM_B1 ** ADAM_STEP)
    v_hat = vn / (1.0 - ADAM_B2 ** ADAM_STEP)
    return -ADAM_LR * (m_hat / (jnp.sqrt(v_hat) + ADAM_EPS) + ADAM_WD * w), mn, vn


def _sum_adamw(recv0, recv1, w, m, v, tag):
    _, r, c = recv0.shape
    tr = max(rows for rows in range(16, r + 1, 16) if r % rows == 0 and rows * c <= 384 * 1024)

    def body(a_ref, b_ref, w_ref, m_ref, v_ref, g_ref, d_ref, mo_ref, vo_ref):
        def update(ref):
            g = ref[0].astype(F32)
            for s in range(1, N_DEV):
                g = g + ref[s].astype(F32)
            g_ref[...] = g
            d_ref[...], mo_ref[...], vo_ref[...] = _adamw_update(w_ref[...], g, m_ref[...], v_ref[...])

        pl.when(pl.program_id(0) == 0)(lambda: update(a_ref))
        pl.when(pl.program_id(0) == 1)(lambda: update(b_ref))

    blk = BS((None, tr, c), lambda layer, i: (layer, i, 0))
    return pl.pallas_call(
        body, name=f"sum_adamw_{tag}", grid=(2, r // tr),
        in_specs=[BS((N_DEV, tr, c), lambda layer, i: (0, i * (1 - layer), 0)),
                  BS((N_DEV, tr, c), lambda layer, i: (0, i * layer, 0)), blk, blk, blk],
        out_specs=[blk] * 4, out_shape=[SDS((2, r, c), F32)] * 4, compiler_params=_cp("arbitrary", "arbitrary"),
    )(recv0, recv1, w, m, v)


def _adamw(w, g, m, v, tag):
    layers, r, c = w.shape
    tr = next(r // k for k in (1, 2, 4, 8) if r // k <= 384 and r % (8 * k) == 0)

    def body(w_ref, g_ref, m_ref, v_ref, d_ref, mo_ref, vo_ref):
        d_ref[...], mo_ref[...], vo_ref[...] = _adamw_update(w_ref[...], g_ref[...], m_ref[...], v_ref[...])

    blk = BS((None, tr, c), lambda l, i: (l, i, 0))
    return pl.pallas_call(
        body, name=f"adamw_{tag}", grid=(layers, r // tr), in_specs=[blk] * 4, out_specs=[blk] * 3,
        out_shape=[SDS((layers, r, c), F32)] * 3, compiler_params=_cp("parallel", "parallel"),
    )(w, g, m, v)


def _place():
    return lax.axis_index("x"), lax.axis_index("y"), lax.axis_index("c")


def _flip(coord, bit):
    return 1 - coord if bit else coord


def _peers(x, y, c):
    peers = []
    for mask in range(1, N_DEV):
        p = (_flip(x, mask & 4), _flip(y, mask & 2), _flip(c, mask & 1))
        peers.append((p, 4 * p[0] + 2 * p[1] + p[2]))
    return peers


def _copy_plan(mode, src, land, x, y, c):
    me = 4 * x + 2 * y + c

    def device(mask):
        p = (_flip(x, mask & 4), _flip(y, mask & 2), _flip(c, mask & 1))
        return p, 4 * p[0] + 2 * p[1] + p[2]

    if mode == "scatter":
        r = land.shape[1]
        return [(p, src.at[pl.ds(i * r, r), :], land.at[me], land.at[i])
                for p, i in map(device, (1, 2, 3, 4, 5, 6, 7, 0))]
    r = land.shape[0] // N_DEV

    def rows(i):
        return land.at[pl.ds(i * r, r), :]

    if mode == "gather":
        return [(p, src, rows(me), rows(i)) for p, i in map(device, (1, 4, 2, 6, 0))]
    sibling = device(1)[0]
    return [(sibling, rows(device(m)[1]), rows(device(m)[1]), rows(device(m | 1)[1])) for m in (4, 2, 6)]


COPIES = dict(scatter=8, gather=5, forward=3)
HBM_SPEC = BS(memory_space=pltpu.HBM)
SEM_SPEC = BS(memory_space=pltpu.SEMAPHORE)
DATAFLOW = pltpu.SideEffectType.DATAFLOW_SIDE_EFFECTING


def _fresh(shape, dtype, tag):
    def body(o_ref):
        del o_ref

    return pl.pallas_call(body, name=f"fresh_{tag}", out_specs=BS(memory_space=pl.ANY), out_shape=SDS(shape, dtype))()


def _exchange_start(mode, srcs, lands, after, tag):
    if lands is None and mode == "gather":
        lands = [_fresh((N_DEV * s.shape[0], s.shape[1]), s.dtype, f"{tag}_{a}") for a, s in enumerate(srcs)]
    elif lands is None:
        lands = [_fresh((N_DEV, s.shape[0] // N_DEV, s.shape[1]), s.dtype, f"{tag}_{a}") for a, s in enumerate(srcs)]
    n, n_src, n_cp = len(lands), len(srcs), COPIES[mode]
    behind = [] if after is None else [after]

    def body(*refs):
        src_refs, land_refs = refs[:n_src], refs[n_src:n_src + n]
        send_sems, recv_sems = refs[n_src + n + len(behind)], refs[n_src + n + len(behind) + 1]
        token = refs[-1]
        for a in range(n):
            plan = _copy_plan(mode, src_refs[a] if n_src else None, land_refs[a], *_place())
            for k, (p, out, there, _) in enumerate(plan):
                pltpu.make_async_remote_copy(
                    src_ref=out, dst_ref=there, send_sem=send_sems.at[n_cp * a + k],
                    recv_sem=recv_sems.at[n_cp * a + k], device_id=p, device_id_type=MESH).start()
        token[...] = jnp.zeros_like(token)

    both = [*srcs, *lands]
    res = pl.pallas_call(
        body, name=f"{mode}_start_{tag}",
        out_shape=(pltpu.SemaphoreType.DMA((n_cp * n,)), pltpu.SemaphoreType.DMA((n_cp * n,)),
                   *[pltpu.HBM(v.shape, v.dtype) for v in both], SDS((8, 128), F32)),
        in_specs=[HBM_SPEC] * len(both) + [BS(memory_space=pl.ANY)] * len(behind),
        out_specs=(SEM_SPEC, SEM_SPEC, *[HBM_SPEC] * len(both), BS(memory_space=pltpu.VMEM)),
        input_output_aliases={i: 2 + i for i in range(len(both))},
        compiler_params=pltpu.CompilerParams(has_side_effects=DATAFLOW),
    )(*[pltpu.with_memory_space_constraint(v, pltpu.HBM) for v in both], *behind)
    return (mode, res[0], res[1], res[2:2 + n_src], res[2 + n_src:2 + n_src + n]), res[-1]


def _exchange_wait(handle, after, tag, which=None):
    mode, send_sems, recv_sems, srcs, lands = handle
    which = list(range(len(lands))) if which is None else list(which)
    n_cp = COPIES[mode]
    lands = [lands[a] for a in which]
    srcs = [srcs[a] for a in which] if srcs else []
    n, n_src = len(lands), len(srcs)
    afters = list(after) if isinstance(after, (tuple, list)) else [after]

    def body(*refs):
        src_refs, land_refs = refs[:n_src], refs[n_src:n_src + n]
        send_ref, recv_ref = refs[n_src + n], refs[n_src + n + 1]
        for i, a in enumerate(which):
            plan = _copy_plan(mode, src_refs[i] if n_src else None, land_refs[i], *_place())
            for k, (p, out, _, here) in enumerate(plan):
                cp = pltpu.make_async_remote_copy(
                    src_ref=out, dst_ref=here, send_sem=send_ref.at[n_cp * a + k], recv_sem=recv_ref.at[n_cp * a + k],
                    device_id=p, device_id_type=MESH)
                cp.wait_send()
                cp.wait_recv()

    both = [*srcs, *lands]
    res = pl.pallas_call(
        body, name=f"{mode}_wait_{tag}", out_shape=tuple(pltpu.HBM(v.shape, v.dtype) for v in both),
        in_specs=[HBM_SPEC] * len(both) + [SEM_SPEC, SEM_SPEC] + [BS(memory_space=pl.ANY)] * len(afters),
        out_specs=tuple([HBM_SPEC] * len(both)),
        input_output_aliases={i: i for i in range(len(both))},
        compiler_params=pltpu.CompilerParams(has_side_effects=DATAFLOW),
    )(*both, send_sems, recv_sems, *afters)
    return list(res[n_src:])


def _allreduce_small(vec, behind):
    rows = vec.shape[0]

    def body(x_ref, behind_ref, o_ref, buf_ref, send_sems, recv_sems):
        x, y, c = _place()
        me = 4 * x + 2 * y + c
        buf_ref[me] = x_ref[...]
        peers = _peers(x, y, c)

        def copy(k, slot):
            return pltpu.make_async_remote_copy(
                src_ref=x_ref, dst_ref=buf_ref.at[slot], send_sem=send_sems.at[k], recv_sem=recv_sems.at[k],
                device_id=peers[k][0], device_id_type=MESH)

        sends = [copy(k, me) for k in range(N_DEV - 1)]
        for cp in sends:
            cp.start()
        for k in range(N_DEV - 1):
            copy(k, peers[k][1]).wait_recv()
        for cp in sends:
            cp.wait_send()
        acc = buf_ref[0]
        for s in range(1, N_DEV):
            acc = acc + buf_ref[s]
        o_ref[...] = acc

    vmem = BS(memory_space=pltpu.VMEM)
    return pl.pallas_call(
        body, name="allreduce_small", in_specs=[vmem, BS(memory_space=pl.ANY)], out_specs=vmem,
        out_shape=SDS((rows, 128), F32),
        scratch_shapes=[pltpu.VMEM((N_DEV, rows, 128), F32), pltpu.SemaphoreType.DMA((7,)),
                        pltpu.SemaphoreType.DMA((7,))],
        compiler_params=pltpu.CompilerParams(has_side_effects=True),
    )(vec, behind)


def _ffn_forward(x, hn, fetch, names, tag, next_g):
    gu, act = _ffn_up(hn, fetch(names[0], hn).reshape(2, F, D), tag)
    got = _mm_nn(act[None], fetch(names[1], act)[None], f"down_{tag}", res=x, scale=0.5, tm=512, next_g=next_g)
    out, hn_next = got if next_g is not None else (got, None)
    return out, hn_next, (x, hn, gu, act)


def _ffn_backward(dxo, dxo_b, saved, norm_g, wut, wd, tag, send):
    x, hn, gu, act = saved
    d_wd = _mm_tn(act[None], dxo_b, f"dwd_{tag}", scale=0.5)
    du = _ffn_dact(dxo_b, wd, gu, send(("down",), [d_wd]), tag)
    d_wut = _mm_tn(du, hn, f"dwu_{tag}")
    token = send(("up",), [d_wut])
    return _mm_nn_norm_bwd([du], wut.reshape(2 * F, D), x, norm_g + token[0, 0], dxo, tag)


def _mixer_forward(x, hn, fetch, bias, tables, tag, next_g):
    proj = _mm_nt_rows(hn, fetch("win", hn), f"proj_{tag}", 1024, IN_W // 2, IN_W, 0, rope=(*tables, 2 * QKV_A // 3))
    qkr = proj
    outs, lses = [], []
    for grp in range(3):
        o, l = _dil_fwd(qkr, proj, grp)
        outs.append(o)
        lses.append(l)
    ya = _combine_fwd(outs, lses)
    yb, lse_b = _na_fwd(proj, bias)
    merged, za, zb = _merge_fwd(ya, yb, proj, fetch("wa", yb), fetch("wb", yb))
    out, hn_next = _mm_nn(merged[None], fetch("wo", merged)[None], f"out_{tag}", res=x, next_g=next_g)
    return out, hn_next, (x, hn, proj, qkr, outs, lses, ya, yb, lse_b, merged, za, zb)


def _mixer_backward(dxo, dxo_b, saved, norm_g, w, bias, tables, tag, send):
    wint, wat, wbt, wo = w
    x, hn, proj, qkr, outs, lses, ya, yb, lse_b, merged, za, zb = saved
    d_wo = _mm_tn(merged[None], dxo_b, f"dwo_{tag}")
    dza, dzb, dlog = _merge_bwd(dxo_b, wo, za, zb, proj)
    dya = _mm_nn(dza[None], wat[None], f"dya_{tag}")
    dyb = _mm_nn(dzb[None], wbt[None], f"dyb_{tag}")
    d_wat = _mm_tn(dza[None], ya, f"dwa_{tag}")
    d_wbt = _mm_tn(dzb[None], yb, f"dwb_{tag}")
    cb = _combine_bwd(dya, outs, lses)
    dqs, dks, dvs = [], [], []
    for grp in range(3):
        dq, dk, dv = _dil_bwd(qkr, proj, cb[grp], cb[3 + grp], lses[grp], grp)
        dqs.append(dq)
        dks.append(dk)
        dvs.append(dv)
    d_qkv_b, dbias_tab = _na_bwd(proj, bias, dyb, yb, lse_b)
    dbias = _na_dbias(dbias_tab)
    dproj = [_rope_bwd(dqs, dks, dvs, *tables), d_qkv_b, dlog]
    d_wint, row = None, 0
    for i, p in enumerate(dproj):
        d_wint = _mm_tn(p, hn, f"dwin{i}_{tag}", into=d_wint, row0=row, rows=IN_W)
        row += p.shape[0] * p.shape[2]
    token = send(("win", "wa", "wb", "wo"), [d_wint, d_wat, d_wbt, d_wo])
    dx, dx_b, dg = _mm_nn_norm_bwd(dproj, wint, x, norm_g + token[0, 0], dxo, f"mix_{tag}", tm=512, single_w=True)
    dbias = dbias[:, 0, :480].reshape(8, 15, 32)[:, :, :31]
    return dx, dx_b, dg, dbias


def _pack_small(norms, biases, final, loss=None):
    parts = []
    for layer in range(DEPTH):
        parts += [norms[0][layer], norms[1][layer], norms[2][layer],
                  jnp.pad(biases[layer].reshape(-1), (0, BIAS_PAD - 8 * 15 * 31))]
    parts.append(final)
    flat = jnp.concatenate([p.reshape(-1).astype(F32) for p in parts])
    if loss is not None:
        flat = jnp.concatenate([flat, loss.reshape(-1)])
    return jnp.pad(flat, (0, SMALL_ROWS * 128 - flat.shape[0])).reshape(SMALL_ROWS, 128)


def _unpack_small(packed):
    flat = packed.reshape(-1)
    norms, biases = ([], [], []), []
    pos = 0
    for _ in range(DEPTH):
        for k in range(3):
            norms[k].append(flat[pos:pos + D])
            pos += D
        biases.append(flat[pos:pos + 8 * 15 * 31].reshape(8, 15, 31))
        pos += BIAS_PAD
    final = flat[pos:pos + D]
    pos += D
    return [jnp.stack(n) for n in norms], jnp.stack(biases), final, flat[pos]


def kernel(x, ffn1_norm, ffn1_w_up, ffn1_w_down, mix_norm, w_in, na_rel_bias, w_branch_a, w_branch_b, w_out, ffn2_norm, ffn2_w_up, ffn2_w_down, final_norm, loss_target, m_ffn1_norm, m_ffn1_w_up, m_ffn1_w_down, m_mix_norm, m_w_in, m_na_rel_bias, m_w_branch_a, m_w_branch_b, m_w_out, m_ffn2_norm, m_ffn2_w_up, m_ffn2_w_down, m_final_norm, v_ffn1_norm, v_ffn1_w_up, v_ffn1_w_down, v_mix_norm, v_w_in, v_na_rel_bias, v_w_branch_a, v_w_branch_b, v_w_out, v_ffn2_norm, v_ffn2_w_up, v_ffn2_w_down, v_final_norm):
    t = x.shape[0] * x.shape[1]
    xs = x.reshape(t, D)
    tgt = loss_target.reshape(t, D)
    tables = _rope_tables()

    col_sharded = dict(up1=ffn1_w_up, win=w_in, wa=w_branch_a, wb=w_branch_b, up2=ffn2_w_up)
    row_sharded = dict(down1=ffn1_w_down, wo=w_out, down2=ffn2_w_down)
    shard = [{} for _ in range(DEPTH)]
    for layer in range(DEPTH):
        for name, arr in col_sharded.items():
            shard[layer][name] = arr[layer].T.astype(BF16)
        for name, arr in row_sharded.items():
            shard[layer][name] = arr[layer].astype(BF16)

    weights = [{} for _ in range(DEPTH)]
    travel = [(0, ("up1",)), (0, ("down1",)), (0, ("win",)), (0, ("wa", "wb", "wo")), (0, ("up2", "down2")),
              (1, ("up1", "down1")), (1, ("win",)), (1, ("wa", "wb", "wo")), (1, ("up2", "down2"))]
    group_of, chips_done, sibling_done = {}, {}, {}
    count = 0
    for i, (layer, names) in enumerate(travel):
        chips_done[i] = list(range(count, count + len(names)))
        count += len(names)
        for n in names:
            group_of[layer, n] = (i, names)
    gathered, token = _exchange_start(
        "gather", [shard[layer][n] for layer, names in travel for n in names], None, None, "w")
    zero = token[0, 0]

    biases = [_na_bias_table(na_rel_bias[layer] + zero) for layer in range(DEPTH)]

    def pass_on(i, behind):
        if i in chips_done:
            lands = _exchange_wait(gathered, behind, f"w{i}", which=chips_done.pop(i))
            sibling_done[i], _ = _exchange_start("forward", [], lands, None, f"p{i}")

    def fetcher(layer):
        def fetch(name, behind):
            if (layer, name) in group_of:
                i, names = group_of[layer, name]
                if i == 0:
                    behind = (behind, *biases)
                pass_on(i, behind)
                pass_on(i + 1, behind)
                for n, got in zip(names, _exchange_wait(sibling_done.pop(i), behind, f"p{i}")):
                    weights[layer][n] = got
                    del group_of[layer, n]
            return weights[layer][name]
        return fetch

    saved = []
    h = xs
    hn = _norm_fwd(xs, ffn1_norm[0] + zero, "first")
    for layer in range(DEPTH):
        bias = biases[layer]
        fetch = fetcher(layer)
        after_ffn2 = ffn1_norm[layer + 1] if layer + 1 < DEPTH else None
        h, hn, s1 = _ffn_forward(h, hn, fetch, ("up1", "down1"), f"f1l{layer}", mix_norm[layer])
        h, hn, s2 = _mixer_forward(h, hn, fetch, bias, tables, f"l{layer}", ffn2_norm[layer])
        h, hn, s3 = _ffn_forward(h, hn, fetch, ("up2", "down2"), f"f2l{layer}", after_ffn2)
        saved.append((s1, s2, s3, bias))
    loss_part, dh, dh_b, d_final = _loss_head(h, final_norm, tgt)

    d_norms = ([None] * DEPTH, [None] * DEPTH, [None] * DEPTH)
    d_bias = [None] * DEPTH
    sent = {}

    def sender(layer, suffix):
        def send(names, grads):
            tag = f"g{layer}{names[0]}{suffix}"
            handle, token = _exchange_start("scatter", grads, None, None, tag)
            for i, n in enumerate(names):
                sent[layer, n + suffix] = (handle, i, tag)
            return token
        return send

    for layer in reversed(range(DEPTH)):
        w = weights[layer]
        s1, s2, s3, bias = saved[layer]
        dh, dh_b, d_norms[2][layer] = _ffn_backward(
            dh, dh_b, s3, ffn2_norm[layer], w["up2"].reshape(2, F, D), w["down2"], f"f2l{layer}", sender(layer, "2"))
        dh, dh_b, d_norms[1][layer], d_bias[layer] = _mixer_backward(
            dh, dh_b, s2, mix_norm[layer], (w["win"], w["wa"], w["wb"], w["wo"]), bias, tables, f"l{layer}",
            sender(layer, ""))
        dh, dh_b, d_norms[0][layer] = _ffn_backward(
            dh, dh_b, s1, ffn1_norm[layer], w["up1"].reshape(2, F, D), w["down1"], f"f1l{layer}", sender(layer, "1"))
    grad_x = dh.reshape(x.shape)

    originals = dict(up1=(ffn1_w_up, m_ffn1_w_up, v_ffn1_w_up), down1=(ffn1_w_down, m_ffn1_w_down, v_ffn1_w_down),
                     win=(w_in, m_w_in, v_w_in), wa=(w_branch_a, m_w_branch_a, v_w_branch_a),
                     wb=(w_branch_b, m_w_branch_b, v_w_branch_b), wo=(w_out, m_w_out, v_w_out),
                     up2=(ffn2_w_up, m_ffn2_w_up, v_ffn2_w_up), down2=(ffn2_w_down, m_ffn2_w_down, v_ffn2_w_down))
    big = {}
    behind = dh
    landed = {}

    def received(layer, name):
        handle, i, tag = sent[layer, name]
        if tag not in landed:
            landed[tag] = _exchange_wait(handle, behind, tag)
        return landed[tag][i]

    for name in ("down2", "up2", "win", "wa", "wb", "wo", "down1", "up1"):
        wv, mv, vv = originals[name]
        if name in col_sharded:
            wv, mv, vv = (jnp.swapaxes(t, 1, 2) for t in (wv, mv, vv))
        big[name] = tuple(_sum_adamw(received(0, name), received(1, name), wv, mv, vv, name))
        behind = big[name][1]
        if name in col_sharded:
            big[name] = tuple(jnp.swapaxes(t, 1, 2) for t in big[name])

    small = _allreduce_small(_pack_small(d_norms, d_bias, d_final, loss_part[0, :1]), behind)
    g_norms, g_bias, g_final, loss = _unpack_small(small)
    w_small = _pack_small((ffn1_norm, mix_norm, ffn2_norm), na_rel_bias, final_norm)
    m_small = _pack_small((m_ffn1_norm, m_mix_norm, m_ffn2_norm), m_na_rel_bias, m_final_norm)
    v_small = _pack_small((v_ffn1_norm, v_mix_norm, v_ffn2_norm), v_na_rel_bias, v_final_norm)
    upd = _adamw(w_small[None], small[None], m_small[None], v_small[None], "small")
    small_out = [(g_norms, g_bias, g_final)] + [_unpack_small(u[0])[:3] for u in upd]

    outputs = [loss, grad_x]
    for kind in range(4):
        norms, bias_k, final_k = small_out[kind]
        outputs += [norms[0], big["up1"][kind], big["down1"][kind], norms[1], big["win"][kind], bias_k,
                    big["wa"][kind], big["wb"][kind], big["wo"][kind], norms[2], big["up2"][kind],
                    big["down2"][kind], final_k]
    return tuple(outputs)
```

```python
import numpy as np

import jax
import jax.numpy as jnp
from jax import lax
from jax.experimental import pallas as pl
from jax.experimental.pallas import tpu as pltpu

F32 = jnp.float32
BF16 = jnp.bfloat16
SDS = jax.ShapeDtypeStruct
BS = pl.BlockSpec
MESH = pl.DeviceIdType.MESH

D = 1024
S = 2048
F = 2816
DEPTH = 2
HEAD_DIM = 64
DILATIONS = (1, 4, 16)
HALF = 64
QKV_A = 2304
QKV_B = 1536
IN_W = 5888
N_DEV = 8
NA_ROWS = 32
GRID_W = 64
NA_KR = 8
ROPE_THETA = 10000.0
RMS_EPS = 1e-6
NEG = -1e30
SCALE = HEAD_DIM ** -0.5
ADAM_LR, ADAM_B1, ADAM_B2, ADAM_EPS, ADAM_WD, ADAM_STEP = 0.001, 0.9, 0.999, 1e-08, 0.01, 10
VMEM_LIMIT_V7X = 52 * 1024 * 1024
SMALL_ROWS = 120
BIAS_PAD = 3840
NA_FWD_ROWS = 8
NA_BWD_ROWS = 4
DIL_FWD_TILES = 8
DIL_BWD_TILES = 4


def _cp(*sem):
    return pltpu.CompilerParams(dimension_semantics=sem, vmem_limit_bytes=VMEM_LIMIT_V7X)


def _dot_nn(a, b):
    return jnp.dot(a, b, preferred_element_type=F32)


def _dot_nt(a, b):
    return lax.dot_general(a, b, (((1,), (1,)), ((), ())), preferred_element_type=F32)


def _dot_tn(a, b):
    return lax.dot_general(a, b, (((0,), (0,)), ((), ())), preferred_element_type=F32)


def _ds(start, size, stride):
    return pl.ds(start, size) if stride == 1 else pl.ds(start, size, stride=stride)


def _norm_fwd(x, g, tag):
    t = x.shape[0]
    tm = 512

    def body(x_ref, g_ref, o_ref):
        xv = x_ref[...]
        r = lax.rsqrt(jnp.mean(xv * xv, axis=-1, keepdims=True) + RMS_EPS)
        o_ref[...] = (xv * r * g_ref[...]).astype(BF16)

    return pl.pallas_call(
        body, name=f"norm_fwd_{tag}", grid=(t // tm,),
        in_specs=[BS((tm, D), lambda i: (i, 0)), BS((1, D), lambda i: (0, 0))],
        out_specs=BS((tm, D), lambda i: (i, 0)),
        out_shape=SDS((t, D), BF16), compiler_params=_cp("parallel"),
    )(x, g.reshape(1, D))


def _loss_head(x, g, tgt):
    t = x.shape[0]
    tm = 1024

    def body(x_ref, g_ref, t_ref, loss_ref, dx_ref, dxb_ref, dg_ref):
        @pl.when(pl.program_id(0) == 0)
        def _():
            dg_ref[...] = jnp.zeros_like(dg_ref)
            loss_ref[...] = jnp.zeros_like(loss_ref)

        xv = x_ref[...]
        gv = g_ref[...]
        r = lax.rsqrt(jnp.mean(xv * xv, axis=-1, keepdims=True) + RMS_EPS)
        xh = xv * r
        e = xh * gv - t_ref[...]
        loss_ref[...] += 0.5 * jnp.sum(jnp.mean(e * e, axis=-1, keepdims=True), axis=0, keepdims=True)
        dy = e * (1.0 / D)
        u = dy * gv
        dx = r * (u - xh * jnp.mean(xh * u, axis=-1, keepdims=True))
        dx_ref[...] = dx
        dxb_ref[...] = dx.astype(BF16)
        dg_ref[...] += jnp.sum(dy * xh, axis=0, keepdims=True)

    row = BS((tm, D), lambda i: (i, 0))
    vec = BS((1, D), lambda i: (0, 0))
    return pl.pallas_call(
        body, name="loss_head", grid=(t // tm,),
        in_specs=[row, vec, row], out_specs=[BS((1, 128), lambda i: (0, 0)), row, row, vec],
        out_shape=[SDS((1, 128), F32), SDS((t, D), F32), SDS((t, D), BF16), SDS((1, D), F32)],
        compiler_params=_cp("arbitrary"),
    )(x, g.reshape(1, D), tgt)


def _mm_nn(a, w, tag, res=None, scale=1.0, tm=1024, tn=None, next_g=None):
    c_n, t, k = a.shape
    n = w.shape[2]
    tn = n if tn is None else tn
    assert next_g is None or tn == n
    n_in = 2 + (res is not None) + (next_g is not None)

    def body(*refs):
        a_ref, w_ref = refs[0], refs[1]
        acc = _dot_nn(a_ref[0].astype(BF16), w_ref[0])
        for c in range(1, c_n):
            acc = acc + _dot_nn(a_ref[c].astype(BF16), w_ref[c])
        if scale != 1.0:
            acc = acc * scale
        if res is not None:
            acc = refs[2][...] + acc
        refs[n_in][...] = acc
        if next_g is not None:
            r = lax.rsqrt(jnp.mean(acc * acc, axis=-1, keepdims=True) + RMS_EPS)
            refs[n_in + 1][...] = (acc * r * refs[n_in - 1][...]).astype(BF16)

    w_mode = dict(pipeline_mode=pl.Buffered(1)) if tn == n else {}
    in_specs = [BS((c_n, tm, k), lambda i, j: (0, i, 0)), BS((c_n, k, tn), lambda i, j: (0, 0, j), **w_mode)]
    args = [a, w]
    out_specs = [BS((tm, tn), lambda i, j: (i, j))]
    out_shape = [SDS((t, n), F32)]
    if res is not None:
        in_specs.append(BS((tm, tn), lambda i, j: (i, j)))
        args.append(res)
    if next_g is not None:
        in_specs.append(BS((1, n), lambda i, j: (0, 0)))
        args.append(next_g.reshape(1, n))
        out_specs.append(BS((tm, tn), lambda i, j: (i, j)))
        out_shape.append(SDS((t, n), BF16))
    got = pl.pallas_call(
        body, name=f"mm_nn_{tag}", grid=(t // tm, n // tn), in_specs=in_specs, out_specs=out_specs,
        out_shape=out_shape, compiler_params=_cp("parallel", "parallel"),
    )(*args)
    return got if next_g is not None else got[0]


def _mm_nn_norm_bwd(parts, w, x, g, dres, tag, tm=256, single_w=False):
    t = parts[0].shape[1]
    n_parts = len(parts)

    def body(*refs):
        w_ref, x_ref, g_ref, dr_ref, dx_ref, dxb_ref, dg_ref = refs[n_parts:]

        @pl.when(pl.program_id(0) == 0)
        def _():
            dg_ref[...] = jnp.zeros_like(dg_ref)

        dh = None
        row = 0
        for a_ref, part in zip(refs, parts):
            for c in range(part.shape[0]):
                term = _dot_nn(a_ref[c].astype(BF16), w_ref[row:row + part.shape[2], :])
                dh = term if dh is None else dh + term
                row += part.shape[2]
        xv = x_ref[...]
        r = lax.rsqrt(jnp.mean(xv * xv, axis=-1, keepdims=True) + RMS_EPS)
        xh = xv * r
        u = dh * g_ref[...]
        dx = dr_ref[...] + r * (u - xh * jnp.mean(xh * u, axis=-1, keepdims=True))
        dx_ref[...] = dx
        dxb_ref[...] = dx.astype(BF16)
        dg_ref[...] += jnp.sum(dh * xh, axis=0, keepdims=True)

    row = BS((tm, D), lambda i: (i, 0))
    vec = BS((1, D), lambda i: (0, 0))
    return pl.pallas_call(
        body, name=f"mm_nn_norm_bwd_{tag}", grid=(t // tm,),
        in_specs=[BS((p.shape[0], tm, p.shape[2]), lambda i: (0, i, 0)) for p in parts]
        + [BS(w.shape, lambda i: (0, 0), **(dict(pipeline_mode=pl.Buffered(1)) if single_w else {})), row, vec, row],
        out_specs=[row, row, vec], out_shape=[SDS((t, D), F32), SDS((t, D), BF16), SDS((1, D), F32)],
        compiler_params=_cp("arbitrary"),
    )(*parts, w, x, g.reshape(1, D), dres)


def _mm_nt_rows(a, w, tag, tm, tn, n_total, w_row0, rope=None):
    t, k = a.shape
    assert w_row0 % tn == 0 and n_total % tn == 0
    j0 = w_row0 // tn

    def body(a_ref, w_ref, *rest):
        o_ref = rest[-1]
        o_ref[...] = _dot_nt(a_ref[...].astype(BF16), w_ref[...])
        if rope is not None:
            @pl.when(pl.program_id(0) == 0)
            def _():
                c = rest[0][...]
                sg = rest[1][...]
                first = (lax.broadcasted_iota(jnp.int32, (tm, 128), 1) % HEAD_DIM) < HEAD_DIM // 2
                for col in range(0, rope[2], 128):
                    v = o_ref[:, col:col + 128]
                    o_ref[:, col:col + 128] = v * c + _swap_halves(v, first) * sg

    in_specs = [BS((tm, k), lambda j, i: (i, 0)), BS((tn, k), lambda j, i: (j0 + j, 0))]
    args = [a, w]
    if rope is not None:
        assert rope[2] <= tn
        in_specs += [BS((tm, 128), lambda j, i: (i % (S // tm), 0))] * 2
        args += [rope[0], rope[1]]
    return pl.pallas_call(
        body, name=f"mm_nt_{tag}", grid=(n_total // tn, t // tm), in_specs=in_specs,
        out_specs=BS((tm, tn), lambda j, i: (i, j)), out_shape=SDS((t, n_total), F32),
        compiler_params=_cp("parallel", "parallel"),
    )(*args)


def _mm_tn(a, b, tag, scale=1.0, tmm=None, into=None, row0=0, rows=None):
    c_n, t, m = a.shape
    n = b.shape[1]
    if tmm is None:
        tmm = max(w for w in (1408, 768, 512, 256) if m % w == 0 and row0 % w == 0)
    tiles = m // tmm
    block0 = row0 // tmm
    assert row0 % tmm == 0 and m % tmm == 0

    def body(a_ref, b_ref, *rest):
        rest[-1][...] = (_dot_tn(a_ref[...].astype(BF16), b_ref[...].astype(BF16)) * scale).astype(BF16)

    in_specs = [BS((None, t, tmm), lambda c, mi: (c, 0, mi)), BS((t, n), lambda c, mi: (0, 0))]
    args = [a, b]
    if into is not None:
        in_specs.append(BS(memory_space=pl.ANY))
        args.append(into)
    return pl.pallas_call(
        body, name=f"mm_tn_{tag}", grid=(c_n, tiles), in_specs=in_specs,
        out_specs=BS((tmm, n), lambda c, mi: (block0 + c * tiles + mi, 0)),
        out_shape=SDS((rows or c_n * m, n) if into is None else into.shape, BF16),
        input_output_aliases={} if into is None else {2: 0},
        compiler_params=_cp("parallel", "parallel"),
    )(*args)


def _ffn_up(hn, wut, tag):
    t = hn.shape[0]
    tm, tn = 1024, 1408

    def body(h_ref, w_ref, gu_ref, act_ref):
        h = h_ref[...]
        g = _dot_nt(h, w_ref[0])
        u = _dot_nt(h, w_ref[1])
        sg = jax.nn.sigmoid(g)
        silu = g * sg
        gu_ref[0] = (u * (sg + silu * (1.0 - sg))).astype(BF16)
        gu_ref[1] = silu.astype(BF16)
        act_ref[...] = (silu * u).astype(BF16)

    return pl.pallas_call(
        body, name=f"ffn_up_{tag}", grid=(F // tn, t // tm),
        in_specs=[BS((tm, D), lambda j, i: (i, 0)), BS((2, tn, D), lambda j, i: (0, j, 0))],
        out_specs=[BS((2, tm, tn), lambda j, i: (0, i, j)), BS((tm, tn), lambda j, i: (i, j))],
        out_shape=[SDS((2, t, F), BF16), SDS((t, F), BF16)],
        compiler_params=_cp("parallel", "parallel"),
    )(hn, wut)


def _ffn_dact(dxo, wd, gu, tie, tag):
    t = dxo.shape[0]
    tm, tn = 1024, 1408

    def body(d_ref, w_ref, gu_ref, tie_ref, o_ref):
        dact = _dot_nt(d_ref[...] * 0.5, w_ref[...])
        o_ref[0] = (dact * gu_ref[0].astype(F32)).astype(BF16)
        o_ref[1] = (dact * gu_ref[1].astype(F32)).astype(BF16)

    return pl.pallas_call(
        body, name=f"ffn_dact_{tag}", grid=(F // tn, t // tm),
        in_specs=[BS((tm, D), lambda j, i: (i, 0)), BS((tn, D), lambda j, i: (j, 0)),
                  BS((2, tm, tn), lambda j, i: (0, i, j)), BS((8, 128), lambda j, i: (0, 0))],
        out_specs=BS((2, tm, tn), lambda j, i: (0, i, j)),
        out_shape=SDS((2, t, F), BF16), compiler_params=_cp("parallel", "parallel"),
    )(dxo, wd, gu, tie)


def _rope_tables():
    half = HEAD_DIM // 2
    inv_freq = ROPE_THETA ** (-jnp.arange(half, dtype=F32) / half)
    ang = jnp.arange(S).astype(F32)[:, None] * inv_freq[None, :]
    cos, sin = jnp.cos(ang), jnp.sin(ang)
    return jnp.concatenate([cos, cos, cos, cos], axis=1), jnp.concatenate([-sin, sin, -sin, sin], axis=1)


def _swap_halves(t, first_half):
    return jnp.where(first_half, pltpu.roll(t, 96, 1), pltpu.roll(t, 32, 1))


def _rope_bwd(dqs, dks, dvs, cos_t, sin_t):
    t = dqs[0].shape[0]
    tm = 1024

    def body(*refs):
        c = refs[9][...]
        sg = refs[10][...]
        o_ref = refs[11]
        first = (lax.broadcasted_iota(jnp.int32, (tm, 128), 1) % HEAD_DIM) < HEAD_DIM // 2
        for a in range(6):
            for hp in range(2):
                v = refs[a][:, 128 * hp:128 * (hp + 1)]
                col = 128 * (2 * a + hp)
                o_ref[:, col:col + 128] = (v * c + _swap_halves(v * sg, first)).astype(BF16)
        for a in range(6, 9):
            o_ref[:, 256 * a:256 * (a + 1)] = refs[a][...].astype(BF16)

    blk = BS((tm, 256), lambda i: (i, 0))
    tab = BS((tm, 128), lambda i: (i % (S // tm), 0))
    return pl.pallas_call(
        body, name="rope_bwd", grid=(t // tm,), in_specs=[blk] * 9 + [tab, tab],
        out_specs=BS((None, tm, QKV_A), lambda i: (0, i, 0)), out_shape=SDS((1, t, QKV_A), BF16),
        compiler_params=_cp("parallel"),
    )(*dqs, *dks, *dvs, cos_t, sin_t)


def _head_masks():
    lane = lax.broadcasted_iota(jnp.int32, (1, 128), 1)
    m0 = (lane < HEAD_DIM).astype(F32)
    return m0, 1.0 - m0


def _dil_geometry(d):
    sub = S // d
    q_rows = 128
    k_rows = min(256, sub)
    return sub, q_rows, sub // q_rows, k_rows


def _dil_tile(idx, d, keys_on_rows=False):
    sub, q_rows, nb, k_rows = _dil_geometry(d)
    r = idx // nb
    n = idx % nb
    k_sub = jnp.clip(q_rows * n - HALF, 0, sub - k_rows)
    if d == 1:
        q_start = pl.multiple_of(q_rows * n, q_rows)
        k_start = pl.multiple_of(k_sub, HALF)
    else:
        q_start = q_rows * n * d + r
        k_start = k_sub * d + r
    if keys_on_rows:
        ii = lax.broadcasted_iota(jnp.int32, (k_rows, 2 * q_rows), 1) % q_rows
        jj = lax.broadcasted_iota(jnp.int32, (k_rows, 2 * q_rows), 0)
    else:
        ii = lax.broadcasted_iota(jnp.int32, (q_rows, k_rows), 0)
        jj = lax.broadcasted_iota(jnp.int32, (q_rows, k_rows), 1)
    valid = jnp.abs(jj - ii + (k_sub - q_rows * n)) <= HALF
    return q_start, k_start, valid


def _dil_specs(grp):
    qs = BS((S, 128), lambda b, hp: (b, 2 * grp + hp))
    ks = BS((S, 128), lambda b, hp: (b, 6 + 2 * grp + hp))
    vs = BS((S, 128), lambda b, hp: (b, 12 + 2 * grp + hp))
    own = BS((S, 128), lambda b, hp: (b, hp))
    return qs, ks, vs, own


def _dil_fwd(qkr, proj, grp):
    t = qkr.shape[0]
    d = DILATIONS[grp]
    _, q_rows, nb, k_rows = _dil_geometry(d)

    def body(q_ref, k_ref, v_ref, o_ref, l_ref):
        masks = _head_masks()

        def step(i0, carry):
            geo = [_dil_tile(i0 * DIL_FWD_TILES + j, d) for j in range(DIL_FWD_TILES)]
            tiles = [(j, h) for j in range(DIL_FWD_TILES) for h in range(2)]
            qs = [q_ref[_ds(g[0], q_rows, d), :] for g in geo]
            kbs = [k_ref[_ds(g[1], k_rows, d), :].astype(BF16) for g in geo]
            ss = [jnp.where(geo[j][2], _dot_nt((qs[j] * masks[h]).astype(BF16), kbs[j]) * SCALE, NEG) for j, h in tiles]
            mxs = [jnp.max(s, axis=1, keepdims=True) for s in ss]
            ps = [jnp.exp(s - mx) for s, mx in zip(ss, mxs)]
            dens = [jnp.sum(p, axis=1, keepdims=True) for p in ps]
            vs = [v_ref[_ds(g[1], k_rows, d), :] for g in geo]
            outs = [_dot_nn(p.astype(BF16), (vs[j] * masks[h]).astype(BF16)) / den
                    for p, den, (j, h) in zip(ps, dens, tiles)]
            for j, g in enumerate(geo):
                o_ref[_ds(g[0], q_rows, d), :] = outs[2 * j] + outs[2 * j + 1]
                l_ref[_ds(g[0], q_rows, d), :] = (
                    (mxs[2 * j] + jnp.log(dens[2 * j])) * masks[0] + (mxs[2 * j + 1] + jnp.log(dens[2 * j + 1])) * masks[1])
            return carry

        lax.fori_loop(0, d * nb // DIL_FWD_TILES, step, 0)

    qs, ks, vs, own = _dil_specs(grp)
    return pl.pallas_call(
        body, name=f"dil_fwd_{grp}", grid=(t // S, 2), in_specs=[qs, ks, vs], out_specs=[own, own],
        out_shape=[SDS((t, 256), F32), SDS((t, 256), F32)], compiler_params=_cp("parallel", "parallel"),
    )(qkr, qkr, proj)


def _dil_bwd(qkr, proj, do, dlp, lse, grp):
    t = qkr.shape[0]
    d = DILATIONS[grp]
    _, q_rows, nb, k_rows = _dil_geometry(d)

    def body(q_ref, k_ref, v_ref, do_ref, dl_ref, l_ref, dq_ref, dk_ref, dv_ref):
        masks = _head_masks()
        dk_ref[...] = jnp.zeros_like(dk_ref)
        dv_ref[...] = jnp.zeros_like(dv_ref)

        def as_row(x2):
            xt = x2.T
            return jnp.concatenate([xt[0:1], xt[HEAD_DIM:HEAD_DIM + 1]], axis=1)

        def step(i0, carry):
            geo = [_dil_tile(i0 * DIL_BWD_TILES + j, d, keys_on_rows=True) for j in range(DIL_BWD_TILES)]
            q_ds = [_ds(g[0], q_rows, d) for g in geo]
            k_ds = [_ds(g[1], k_rows, d) for g in geo]
            qbs = [_both_heads(q_ref[r, :], masks).astype(BF16) for r in q_ds]
            kbs = [k_ref[r, :].astype(BF16) for r in k_ds]
            vbs = [v_ref[r, :].astype(BF16) for r in k_ds]
            dobs = [_both_heads(do_ref[r, :], masks).astype(BF16) for r in q_ds]
            l_rows = [as_row(l_ref[r, :]) for r in q_ds]
            dl_rows = [as_row(dl_ref[r, :]) for r in q_ds]
            ss = [jnp.where(g[2], _dot_nt(kb, qb) * SCALE, NEG) for g, kb, qb in zip(geo, kbs, qbs)]
            ps = [jnp.exp(s - lr) for s, lr in zip(ss, l_rows)]
            dps = [_dot_nt(vb, dob) for vb, dob in zip(vbs, dobs)]
            dss = [(p * (dp - dr)).astype(BF16) for p, dp, dr in zip(ps, dps, dl_rows)]
            dks = [_dot_nn(ds, qb) for ds, qb in zip(dss, qbs)]
            dvs = [_dot_nn(p.astype(BF16), dob) for p, dob in zip(ps, dobs)]
            dqs = [_own_heads(_dot_tn(ds, kb), masks) for ds, kb in zip(dss, kbs)]
            for j in range(DIL_BWD_TILES):
                dq_ref[q_ds[j], :] = dqs[j] * SCALE
                dk_ref[k_ds[j], :] += dks[j] * SCALE
                dv_ref[k_ds[j], :] += dvs[j]
            return carry

        lax.fori_loop(0, d * nb // DIL_BWD_TILES, step, 0)

    qs, ks, vs, own = _dil_specs(grp)
    return pl.pallas_call(
        body, name=f"dil_bwd_{grp}", grid=(t // S, 2), in_specs=[qs, ks, vs, own, own, own],
        out_specs=[own, own, own], out_shape=[SDS((t, 256), F32)] * 3,
        compiler_params=_cp("parallel", "parallel"),
    )(qkr, qkr, proj, do, dlp, lse)


def _mix_weights(l0, l1, l2):
    mx = jnp.maximum(jnp.maximum(l0, l1), l2)
    e0, e1, e2 = jnp.exp(l0 - mx), jnp.exp(l1 - mx), jnp.exp(l2 - mx)
    den = e0 + e1 + e2
    return e0 / den, e1 / den, e2 / den


def _combine_fwd(outs, lses):
    t = outs[0].shape[0]
    tm = 1024

    def body(o0, o1, o2, l0, l1, l2, y_ref):
        w0, w1, w2 = _mix_weights(l0[...], l1[...], l2[...])
        y_ref[...] = w0 * o0[...] + w1 * o1[...] + w2 * o2[...]

    blk = BS((tm, 256), lambda i: (i, 0))
    return pl.pallas_call(
        body, name="combine_fwd", grid=(t // tm,), in_specs=[blk] * 6, out_specs=blk,
        out_shape=SDS((t, 256), F32), compiler_params=_cp("parallel"),
    )(*outs, *lses)


def _head_sum(x):
    a = lax.broadcasted_iota(jnp.int32, (256, 256), 0) // HEAD_DIM
    b = lax.broadcasted_iota(jnp.int32, (256, 256), 1) // HEAD_DIM
    ones = (a == b).astype(BF16)
    hi = x.astype(BF16)
    lo = (x - hi.astype(F32)).astype(BF16)
    return _dot_nn(hi, ones) + _dot_nn(lo, ones)


def _combine_bwd(dya, outs, lses):
    t = dya.shape[0]
    tm = 1024

    def body(dy_ref, o0, o1, o2, l0, l1, l2, d0, d1, d2, e0, e1, e2):
        ws = _mix_weights(l0[...], l1[...], l2[...])
        dy = dy_ref[...]
        ya = ws[0] * o0[...] + ws[1] * o1[...] + ws[2] * o2[...]
        hs = _head_sum(dy * ya)
        for w, d_ref, e_ref in zip(ws, (d0, d1, d2), (e0, e1, e2)):
            d_ref[...] = w * dy
            e_ref[...] = w * hs

    blk = BS((tm, 256), lambda i: (i, 0))
    return pl.pallas_call(
        body, name="combine_bwd", grid=(t // tm,), in_specs=[blk] * 7, out_specs=[blk] * 6,
        out_shape=[SDS((t, 256), F32)] * 6, compiler_params=_cp("parallel"),
    )(dya, *outs, *lses)


def _na_bias_table(rel_bias):
    kw = NA_KR * GRID_W
    rev = jnp.pad(rel_bias.astype(F32)[:, :, ::-1], ((0, 0), (0, 0), (0, 128 - 31)))

    def body(r_ref, o_ref):
        lane = lax.broadcasted_iota(jnp.int32, (GRID_W, 128), 1)
        j = lax.broadcasted_iota(jnp.int32, (GRID_W, 128), 0)
        q = lane % GRID_W
        win_lo = jnp.clip(q - 8, 0, GRID_W - 16)
        valid = (j >= win_lo) & (j < win_lo + 16)
        for cls in range(NA_KR):
            for k in range(NA_KR):
                tiles = []
                for h in range(2):
                    row = jnp.broadcast_to(r_ref[h, cls + k:cls + k + 1, :], (GRID_W, 128))
                    tiles.append(pltpu.roll(row, (128 - 15 + GRID_W * h) % 128, 1, stride=1, stride_axis=0))
                o_ref[cls, GRID_W * k:GRID_W * (k + 1), :] = jnp.where(
                    valid, jnp.where(lane < GRID_W, tiles[0], tiles[1]), NEG)

    return pl.pallas_call(
        body, name="na_bias_table", grid=(4,),
        in_specs=[BS((2, 2 * NA_KR - 1, 128), lambda hp: (hp, 0, 0))],
        out_specs=BS((None, NA_KR, kw, 128), lambda hp: (hp, 0, 0, 0)),
        out_shape=SDS((4, NA_KR, kw, 128), F32), compiler_params=_cp("parallel"),
    )(rev)


def _na_row(i):
    lo = jnp.clip(i - NA_KR // 2, 0, NA_ROWS - NA_KR)
    return pl.multiple_of(GRID_W * i, GRID_W), pl.multiple_of(GRID_W * lo, GRID_W), lo - i + NA_KR - 1


def _both_heads(x, masks):
    return jnp.concatenate([x * masks[0], x * masks[1]], axis=0)


def _own_heads(r, masks):
    half = r.shape[0] // 2
    return r[:half] * masks[0] + r[half:] * masks[1]


def _na_fwd(proj, bias):
    t = proj.shape[0]
    kw = NA_KR * GRID_W

    def body(q_ref, k_ref, v_ref, b_ref, o_ref, l_ref):
        masks = _head_masks()

        def step(i0, carry):
            idx = [i0 * NA_FWD_ROWS + j for j in range(NA_FWD_ROWS)]
            rows = [_na_row(i) for i in idx]
            qbs = [_both_heads(q_ref[pl.ds(r[0], GRID_W), :], masks).astype(BF16) for r in rows]
            kbs = [k_ref[pl.ds(r[1], kw), :].astype(BF16) for r in rows]
            ss = [_dot_nt(kb, qb) * SCALE + b_ref[r[2]] for kb, qb, r in zip(kbs, qbs, rows)]
            mxs = [jnp.max(s, axis=0, keepdims=True) for s in ss]
            ps = [jnp.exp(s - mx) for s, mx in zip(ss, mxs)]
            dens = [jnp.sum(p, axis=0, keepdims=True) for p in ps]
            pbs = [(p / den).astype(BF16) for p, den in zip(ps, dens)]
            vbs = [v_ref[pl.ds(r[1], kw), :].astype(BF16) for r in rows]
            outs = [_own_heads(_dot_tn(pb, vb), masks) for pb, vb in zip(pbs, vbs)]
            for j, r in enumerate(rows):
                o_ref[pl.ds(r[0], GRID_W), :] = outs[j]
                l_ref[pl.ds(idx[j], 1), :] = mxs[j] + jnp.log(dens[j])
            return carry

        lax.fori_loop(0, NA_ROWS // NA_FWD_ROWS, step, 0)

    c0 = QKV_A // 128
    return pl.pallas_call(
        body, name="na_fwd", grid=(t // S, 4),
        in_specs=[BS((S, 128), lambda b, hp: (b, c0 + hp)), BS((S, 128), lambda b, hp: (b, c0 + 4 + hp)),
                  BS((S, 128), lambda b, hp: (b, c0 + 8 + hp)),
                  BS((None, NA_KR, kw, 128), lambda b, hp: (hp, 0, 0, 0))],
        out_specs=[BS((S, 128), lambda b, hp: (b, hp)), BS((None, None, NA_ROWS, 128), lambda b, hp: (b, hp, 0, 0))],
        out_shape=[SDS((t, 512), F32), SDS((t // S, 4, NA_ROWS, 128), F32)],
        compiler_params=_cp("parallel", "parallel"),
    )(proj, proj, proj, bias)


def _na_bwd(proj, bias, dyb, yb, lse):
    t = proj.shape[0]
    kw = NA_KR * GRID_W

    def body(q_ref, k_ref, v_ref, b_ref, do_ref, o_ref, l_ref, d_ref, db_ref):
        masks = _head_masks()
        ones = jnp.ones((8, 128), BF16)

        @pl.when(pl.program_id(1) == 0)
        def _():
            db_ref[...] = jnp.zeros_like(db_ref)

        d_ref[1:3] = jnp.zeros((2, S, 128), F32)

        def row_sums(x):
            hi = x.astype(BF16)
            lo = (x - hi.astype(F32)).astype(BF16)
            return (_dot_nt(ones, hi) + _dot_nt(ones, lo))[0:1]

        def step(i0, carry):
            idx = [i0 * NA_BWD_ROWS + j for j in range(NA_BWD_ROWS)]
            rows = [_na_row(i) for i in idx]
            q_ds = [pl.ds(r[0], GRID_W) for r in rows]
            k_ds = [pl.ds(r[1], kw) for r in rows]
            qbs = [_both_heads(q_ref[r, :], masks).astype(BF16) for r in q_ds]
            kbs = [k_ref[r, :].astype(BF16) for r in k_ds]
            vbs = [v_ref[r, :].astype(BF16) for r in k_ds]
            dos = [do_ref[r, :] for r in q_ds]
            dobs = [_both_heads(do, masks).astype(BF16) for do in dos]
            deltas = [row_sums(_both_heads(do * o_ref[r, :], masks)) for do, r in zip(dos, q_ds)]
            ss = [_dot_nt(kb, qb) * SCALE + b_ref[r[2]] for kb, qb, r in zip(kbs, qbs, rows)]
            ps = [jnp.exp(s - l_ref[pl.ds(i, 1), :]) for s, i in zip(ss, idx)]
            dps = [_dot_nt(vb, dob) for vb, dob in zip(vbs, dobs)]
            dss = [p * (dp - delta) for p, dp, delta in zip(ps, dps, deltas)]
            for ds, r in zip(dss, rows):
                db_ref[r[2]] += ds
            dsbs = [ds.astype(BF16) for ds in dss]
            dks = [_dot_nn(dsb, qb) for dsb, qb in zip(dsbs, qbs)]
            dvs = [_dot_nn(p.astype(BF16), dob) for p, dob in zip(ps, dobs)]
            dqs = [_own_heads(_dot_tn(dsb, kb), masks) for dsb, kb in zip(dsbs, kbs)]
            for j in range(NA_BWD_ROWS):
                d_ref[0, q_ds[j], :] = dqs[j] * SCALE
                d_ref[1, k_ds[j], :] += dks[j] * SCALE
                d_ref[2, k_ds[j], :] += dvs[j]
            return carry

        lax.fori_loop(0, NA_ROWS // NA_BWD_ROWS, step, 0)

    c0 = QKV_A // 128
    own = BS((S, 128), lambda hp, b: (b, hp))
    tab = BS((None, NA_KR, kw, 128), lambda hp, b: (hp, 0, 0, 0))
    return pl.pallas_call(
        body, name="na_bwd", grid=(4, t // S),
        in_specs=[BS((S, 128), lambda hp, b: (b, c0 + hp)), BS((S, 128), lambda hp, b: (b, c0 + 4 + hp)),
                  BS((S, 128), lambda hp, b: (b, c0 + 8 + hp)), tab, own, own,
                  BS((None, None, NA_ROWS, 128), lambda hp, b: (b, hp, 0, 0))],
        out_specs=[BS((3, S, 128), lambda hp, b: (0, b, hp)), tab],
        out_shape=[SDS((3, t, 512), F32), SDS((4, NA_KR, kw, 128), F32)],
        compiler_params=_cp("parallel", "arbitrary"),
    )(proj, proj, proj, bias, dyb, yb, lse)


def _na_dbias_lane_map():
    kw = NA_KR * GRID_W
    lane = np.arange(kw)
    blk, m = lane // GRID_W, lane % GRID_W
    target = np.full(kw, -1)
    target[m < 16] = (blk * 32 + 15 + m)[m < 16]
    target[m >= 49] = (((blk + 1) % NA_KR) * 32 + m - 49)[m >= 49]
    return jnp.asarray(target[:, None] == np.arange(kw)[None, :], BF16)


def _na_dbias(db):
    kw = NA_KR * GRID_W

    def body(x_ref, map_ref, o_ref, z_ref):
        for cls in range(NA_KR):
            xt = x_ref[cls].T
            for h in range(2):
                xv = xt[GRID_W * h:GRID_W * (h + 1)]
                y = xv[0:8]
                for g in range(1, GRID_W // 8):
                    y = y + pltpu.roll(xv[8 * g:8 * g + 8], kw - 8 * g, 1)
                d = y[0:1]
                for s in range(1, 8):
                    d = d + pltpu.roll(y[s:s + 1], kw - s, 1)
                z_ref[h, cls:cls + 1, :] = d
        for h in range(2):
            z = z_ref[h]
            hi = z.astype(BF16)
            lo = (z - hi.astype(F32)).astype(BF16)
            e = _dot_nn(hi, map_ref[...]) + _dot_nn(lo, map_ref[...])
            out = e[0:1]
            for cls in range(1, NA_KR):
                out = out + pltpu.roll(e[cls:cls + 1], 32 * cls, 1)
            o_ref[h] = jnp.broadcast_to(out, (8, kw))

    return pl.pallas_call(
        body, name="na_dbias", grid=(4,),
        in_specs=[BS((None, NA_KR, kw, 128), lambda hp: (hp, 0, 0, 0)), BS((kw, kw), lambda hp: (0, 0))],
        out_specs=BS((2, 8, kw), lambda hp: (hp, 0, 0)), out_shape=SDS((8, 8, kw), F32),
        scratch_shapes=[pltpu.VMEM((2, 8, kw), F32)], compiler_params=_cp("parallel"),
    )(db, _na_dbias_lane_map())


def _merge_fwd(ya, yb, proj, wat, wbt):
    t = ya.shape[0]
    tm, tn = 1024, 256
    ca = (QKV_A + QKV_B) // tn
    cb = ca + D // tn

    def body(ya_ref, yb_ref, la_ref, lb_ref, wa_ref, wb_ref, m_ref, za_ref, zb_ref):
        za = _dot_nt(ya_ref[...].astype(BF16), wa_ref[...])
        zb = _dot_nt(yb_ref[...].astype(BF16), wb_ref[...])
        m_ref[...] = (jax.nn.sigmoid(la_ref[...]) * za + jax.nn.sigmoid(lb_ref[...]) * zb).astype(BF16)
        za_ref[...] = za.astype(BF16)
        zb_ref[...] = zb.astype(BF16)

    out = BS((tm, tn), lambda i, j: (i, j))
    return pl.pallas_call(
        body, name="merge_fwd", grid=(t // tm, D // tn),
        in_specs=[BS((tm, 256), lambda i, j: (i, 0)), BS((tm, 512), lambda i, j: (i, 0)),
                  BS((tm, tn), lambda i, j: (i, ca + j)), BS((tm, tn), lambda i, j: (i, cb + j)),
                  BS((tn, 256), lambda i, j: (j, 0)), BS((tn, 512), lambda i, j: (j, 0))],
        out_specs=[out, out, out], out_shape=[SDS((t, D), BF16)] * 3,
        compiler_params=_cp("parallel", "parallel"),
    )(ya, yb, proj, proj, wat, wbt)


def _merge_bwd(dxo, wo, za, zb, proj):
    t = dxo.shape[0]
    tm, tn = 1024, 256
    ca = (QKV_A + QKV_B) // tn
    cb = ca + D // tn

    def body(d_ref, w_ref, za_ref, zb_ref, la_ref, lb_ref, dza_ref, dzb_ref, dl_ref):
        dmv = _dot_nt(d_ref[...], w_ref[...])
        ga = jax.nn.sigmoid(la_ref[...])
        gb = jax.nn.sigmoid(lb_ref[...])
        dza_ref[...] = (dmv * ga).astype(BF16)
        dzb_ref[...] = (dmv * gb).astype(BF16)
        dl_ref[0] = (dmv * za_ref[...].astype(F32) * ga * (1.0 - ga)).astype(BF16)
        dl_ref[1] = (dmv * zb_ref[...].astype(F32) * gb * (1.0 - gb)).astype(BF16)

    blk = BS((tm, tn), lambda i, j: (i, j))
    return pl.pallas_call(
        body, name="merge_bwd", grid=(t // tm, D // tn),
        in_specs=[BS((tm, D), lambda i, j: (i, 0)), BS((tn, D), lambda i, j: (j, 0)), blk, blk,
                  BS((tm, tn), lambda i, j: (i, ca + j)), BS((tm, tn), lambda i, j: (i, cb + j))],
        out_specs=[blk, blk, BS((2, tm, tn), lambda i, j: (0, i, j))],
        out_shape=[SDS((t, D), BF16), SDS((t, D), BF16), SDS((2, t, D), BF16)],
        compiler_params=_cp("parallel", "parallel"),
    )(dxo, wo, za, zb, proj, proj)


def _adamw_update(w, g, m, v):
    mn = ADAM_B1 * m + (1.0 - ADAM_B1) * g
    vn = ADAM_B2 * v + (1.0 - ADAM_B2) * (g * g)
    m_hat = mn / (1.0 - ADAM_B1 ** ADAM_STEP)
    v_hat = vn / (1.0 - ADAM_B2 ** ADAM_STEP)
    return -ADAM_LR * (m_hat / (jnp.sqrt(v_hat) + ADAM_EPS) + ADAM_WD * w), mn, vn


def _sum_adamw(recv0, recv1, w, m, v, tag):
    _, r, c = recv0.shape
    tr = max(rows for rows in range(16, r + 1, 16) if r % rows == 0 and rows * c <= 384 * 1024)

    def body(a_ref, b_ref, w_ref, m_ref, v_ref, g_ref, d_ref, mo_ref, vo_ref):
        def update(ref):
            g = ref[0].astype(F32)
            for s in range(1, N_DEV):
                g = g + ref[s].astype(F32)
            g_ref[...] = g
            d_ref[...], mo_ref[...], vo_ref[...] = _adamw_update(w_ref[...], g, m_ref[...], v_ref[...])

        pl.when(pl.program_id(0) == 0)(lambda: update(a_ref))
        pl.when(pl.program_id(0) == 1)(lambda: update(b_ref))

    blk = BS((None, tr, c), lambda layer, i: (layer, i, 0))
    return pl.pallas_call(
        body, name=f"sum_adamw_{tag}", grid=(2, r // tr),
        in_specs=[BS((N_DEV, tr, c), lambda layer, i: (0, i * (1 - layer), 0)),
                  BS((N_DEV, tr, c), lambda layer, i: (0, i * layer, 0)), blk, blk, blk],
        out_specs=[blk] * 4, out_shape=[SDS((2, r, c), F32)] * 4, compiler_params=_cp("arbitrary", "arbitrary"),
    )(recv0, recv1, w, m, v)


def _adamw(w, g, m, v, tag):
    layers, r, c = w.shape
    tr = next(r // k for k in (1, 2, 4, 8) if r // k <= 384 and r % (8 * k) == 0)

    def body(w_ref, g_ref, m_ref, v_ref, d_ref, mo_ref, vo_ref):
        d_ref[...], mo_ref[...], vo_ref[...] = _adamw_update(w_ref[...], g_ref[...], m_ref[...], v_ref[...])

    blk = BS((None, tr, c), lambda l, i: (l, i, 0))
    return pl.pallas_call(
        body, name=f"adamw_{tag}", grid=(layers, r // tr), in_specs=[blk] * 4, out_specs=[blk] * 3,
        out_shape=[SDS((layers, r, c), F32)] * 3, compiler_params=_cp("parallel", "parallel"),
    )(w, g, m, v)


def _place():
    return lax.axis_index("x"), lax.axis_index("y"), lax.axis_index("c")


def _flip(coord, bit):
    return 1 - coord if bit else coord


def _peers(x, y, c):
    peers = []
    for mask in range(1, N_DEV):
        p = (_flip(x, mask & 4), _flip(y, mask & 2), _flip(c, mask & 1))
        peers.append((p, 4 * p[0] + 2 * p[1] + p[2]))
    return peers


def _copy_plan(mode, src, land, x, y, c):
    me = 4 * x + 2 * y + c

    def device(mask):
        p = (_flip(x, mask & 4), _flip(y, mask & 2), _flip(c, mask & 1))
        return p, 4 * p[0] + 2 * p[1] + p[2]

    if mode == "scatter":
        r = land.shape[1]
        return [(p, src.at[pl.ds(i * r, r), :], land.at[me], land.at[i])
                for p, i in map(device, (1, 2, 3, 4, 5, 6, 7, 0))]
    r = land.shape[0] // N_DEV

    def rows(i):
        return land.at[pl.ds(i * r, r), :]

    if mode == "gather":
        return [(p, src, rows(me), rows(i)) for p, i in map(device, (1, 4, 2, 6, 0))]
    sibling = device(1)[0]
    return [(sibling, rows(device(m)[1]), rows(device(m)[1]), rows(device(m | 1)[1])) for m in (4, 2, 6)]


COPIES = dict(scatter=8, gather=5, forward=3)
HBM_SPEC = BS(memory_space=pltpu.HBM)
SEM_SPEC = BS(memory_space=pltpu.SEMAPHORE)
DATAFLOW = pltpu.SideEffectType.DATAFLOW_SIDE_EFFECTING


def _fresh(shape, dtype, tag):
    def body(o_ref):
        del o_ref

    return pl.pallas_call(body, name=f"fresh_{tag}", out_specs=BS(memory_space=pl.ANY), out_shape=SDS(shape, dtype))()


def _exchange_start(mode, srcs, lands, after, tag):
    if lands is None and mode == "gather":
        lands = [_fresh((N_DEV * s.shape[0], s.shape[1]), s.dtype, f"{tag}_{a}") for a, s in enumerate(srcs)]
    elif lands is None:
        lands = [_fresh((N_DEV, s.shape[0] // N_DEV, s.shape[1]), s.dtype, f"{tag}_{a}") for a, s in enumerate(srcs)]
    n, n_src, n_cp = len(lands), len(srcs), COPIES[mode]
    behind = [] if after is None else [after]

    def body(*refs):
        src_refs, land_refs = refs[:n_src], refs[n_src:n_src + n]
        send_sems, recv_sems = refs[n_src + n + len(behind)], refs[n_src + n + len(behind) + 1]
        token = refs[-1]
        for a in range(n):
            plan = _copy_plan(mode, src_refs[a] if n_src else None, land_refs[a], *_place())
            for k, (p, out, there, _) in enumerate(plan):
                pltpu.make_async_remote_copy(
                    src_ref=out, dst_ref=there, send_sem=send_sems.at[n_cp * a + k],
                    recv_sem=recv_sems.at[n_cp * a + k], device_id=p, device_id_type=MESH).start()
        token[...] = jnp.zeros_like(token)

    both = [*srcs, *lands]
    res = pl.pallas_call(
        body, name=f"{mode}_start_{tag}",
        out_shape=(pltpu.SemaphoreType.DMA((n_cp * n,)), pltpu.SemaphoreType.DMA((n_cp * n,)),
                   *[pltpu.HBM(v.shape, v.dtype) for v in both], SDS((8, 128), F32)),
        in_specs=[HBM_SPEC] * len(both) + [BS(memory_space=pl.ANY)] * len(behind),
        out_specs=(SEM_SPEC, SEM_SPEC, *[HBM_SPEC] * len(both), BS(memory_space=pltpu.VMEM)),
        input_output_aliases={i: 2 + i for i in range(len(both))},
        compiler_params=pltpu.CompilerParams(has_side_effects=DATAFLOW),
    )(*[pltpu.with_memory_space_constraint(v, pltpu.HBM) for v in both], *behind)
    return (mode, res[0], res[1], res[2:2 + n_src], res[2 + n_src:2 + n_src + n]), res[-1]


def _exchange_wait(handle, after, tag, which=None):
    mode, send_sems, recv_sems, srcs, lands = handle
    which = list(range(len(lands))) if which is None else list(which)
    n_cp = COPIES[mode]
    lands = [lands[a] for a in which]
    srcs = [srcs[a] for a in which] if srcs else []
    n, n_src = len(lands), len(srcs)
    afters = list(after) if isinstance(after, (tuple, list)) else [after]

    def body(*refs):
        src_refs, land_refs = refs[:n_src], refs[n_src:n_src + n]
        send_ref, recv_ref = refs[n_src + n], refs[n_src + n + 1]
        for i, a in enumerate(which):
            plan = _copy_plan(mode, src_refs[i] if n_src else None, land_refs[i], *_place())
            for k, (p, out, _, here) in enumerate(plan):
                cp = pltpu.make_async_remote_copy(
                    src_ref=out, dst_ref=here, send_sem=send_ref.at[n_cp * a + k], recv_sem=recv_ref.at[n_cp * a + k],
                    device_id=p, device_id_type=MESH)
                cp.wait_send()
                cp.wait_recv()

    both = [*srcs, *lands]
    res = pl.pallas_call(
        body, name=f"{mode}_wait_{tag}", out_shape=tuple(pltpu.HBM(v.shape, v.dtype) for v in both),
        in_specs=[HBM_SPEC] * len(both) + [SEM_SPEC, SEM_SPEC] + [BS(memory_space=pl.ANY)] * len(afters),
        out_specs=tuple([HBM_SPEC] * len(both)),
        input_output_aliases={i: i for i in range(len(both))},
        compiler_params=pltpu.CompilerParams(has_side_effects=DATAFLOW),
    )(*both, send_sems, recv_sems, *afters)
    return list(res[n_src:])


def _allreduce_small(vec, behind):
    rows = vec.shape[0]

    def body(x_ref, behind_ref, o_ref, buf_ref, send_sems, recv_sems):
        x, y, c = _place()
        me = 4 * x + 2 * y + c
        buf_ref[me] = x_ref[...]
        peers = _peers(x, y, c)

        def copy(k, slot):
            return pltpu.make_async_remote_copy(
                src_ref=x_ref, dst_ref=buf_ref.at[slot], send_sem=send_sems.at[k], recv_sem=recv_sems.at[k],
                device_id=peers[k][0], device_id_type=MESH)

        sends = [copy(k, me) for k in range(N_DEV - 1)]
        for cp in sends:
            cp.start()
        for k in range(N_DEV - 1):
            copy(k, peers[k][1]).wait_recv()
        for cp in sends:
            cp.wait_send()
        acc = buf_ref[0]
        for s in range(1, N_DEV):
            acc = acc + buf_ref[s]
        o_ref[...] = acc

    vmem = BS(memory_space=pltpu.VMEM)
    return pl.pallas_call(
        body, name="allreduce_small", in_specs=[vmem, BS(memory_space=pl.ANY)], out_specs=vmem,
        out_shape=SDS((rows, 128), F32),
        scratch_shapes=[pltpu.VMEM((N_DEV, rows, 128), F32), pltpu.SemaphoreType.DMA((7,)),
                        pltpu.SemaphoreType.DMA((7,))],
        compiler_params=pltpu.CompilerParams(has_side_effects=True),
    )(vec, behind)


def _ffn_forward(x, hn, fetch, names, tag, next_g):
    gu, act = _ffn_up(hn, fetch(names[0], hn).reshape(2, F, D), tag)
    got = _mm_nn(act[None], fetch(names[1], act)[None], f"down_{tag}", res=x, scale=0.5, tm=512, next_g=next_g)
    out, hn_next = got if next_g is not None else (got, None)
    return out, hn_next, (x, hn, gu, act)


def _ffn_backward(dxo, dxo_b, saved, norm_g, wut, wd, tag, send):
    x, hn, gu, act = saved
    d_wd = _mm_tn(act[None], dxo_b, f"dwd_{tag}", scale=0.5)
    du = _ffn_dact(dxo_b, wd, gu, send(("down",), [d_wd]), tag)
    d_wut = _mm_tn(du, hn, f"dwu_{tag}")
    token = send(("up",), [d_wut])
    return _mm_nn_norm_bwd([du], wut.reshape(2 * F, D), x, norm_g + token[0, 0], dxo, tag)


def _mixer_forward(x, hn, fetch, bias, tables, tag, next_g):
    proj = _mm_nt_rows(hn, fetch("win", hn), f"proj_{tag}", 1024, IN_W // 2, IN_W, 0, rope=(*tables, 2 * QKV_A // 3))
    qkr = proj
    outs, lses = [], []
    for grp in range(3):
        o, l = _dil_fwd(qkr, proj, grp)
        outs.append(o)
        lses.append(l)
    ya = _combine_fwd(outs, lses)
    yb, lse_b = _na_fwd(proj, bias)
    merged, za, zb = _merge_fwd(ya, yb, proj, fetch("wa", yb), fetch("wb", yb))
    out, hn_next = _mm_nn(merged[None], fetch("wo", merged)[None], f"out_{tag}", res=x, next_g=next_g)
    return out, hn_next, (x, hn, proj, qkr, outs, lses, ya, yb, lse_b, merged, za, zb)


def _mixer_backward(dxo, dxo_b, saved, norm_g, w, bias, tables, tag, send):
    wint, wat, wbt, wo = w
    x, hn, proj, qkr, outs, lses, ya, yb, lse_b, merged, za, zb = saved
    d_wo = _mm_tn(merged[None], dxo_b, f"dwo_{tag}")
    dza, dzb, dlog = _merge_bwd(dxo_b, wo, za, zb, proj)
    dya = _mm_nn(dza[None], wat[None], f"dya_{tag}")
    dyb = _mm_nn(dzb[None], wbt[None], f"dyb_{tag}")
    d_wat = _mm_tn(dza[None], ya, f"dwa_{tag}")
    d_wbt = _mm_tn(dzb[None], yb, f"dwb_{tag}")
    cb = _combine_bwd(dya, outs, lses)
    dqs, dks, dvs = [], [], []
    for grp in range(3):
        dq, dk, dv = _dil_bwd(qkr, proj, cb[grp], cb[3 + grp], lses[grp], grp)
        dqs.append(dq)
        dks.append(dk)
        dvs.append(dv)
    d_qkv_b, dbias_tab = _na_bwd(proj, bias, dyb, yb, lse_b)
    dbias = _na_dbias(dbias_tab)
    dproj = [_rope_bwd(dqs, dks, dvs, *tables), d_qkv_b, dlog]
    d_wint, row = None, 0
    for i, p in enumerate(dproj):
        d_wint = _mm_tn(p, hn, f"dwin{i}_{tag}", into=d_wint, row0=row, rows=IN_W)
        row += p.shape[0] * p.shape[2]
    token = send(("win", "wa", "wb", "wo"), [d_wint, d_wat, d_wbt, d_wo])
    dx, dx_b, dg = _mm_nn_norm_bwd(dproj, wint, x, norm_g + token[0, 0], dxo, f"mix_{tag}", tm=512, single_w=True)
    dbias = dbias[:, 0, :480].reshape(8, 15, 32)[:, :, :31]
    return dx, dx_b, dg, dbias


def _pack_small(norms, biases, final, loss=None):
    parts = []
    for layer in range(DEPTH):
        parts += [norms[0][layer], norms[1][layer], norms[2][layer],
                  jnp.pad(biases[layer].reshape(-1), (0, BIAS_PAD - 8 * 15 * 31))]
    parts.append(final)
    flat = jnp.concatenate([p.reshape(-1).astype(F32) for p in parts])
    if loss is not None:
        flat = jnp.concatenate([flat, loss.reshape(-1)])
    return jnp.pad(flat, (0, SMALL_ROWS * 128 - flat.shape[0])).reshape(SMALL_ROWS, 128)


def _unpack_small(packed):
    flat = packed.reshape(-1)
    norms, biases = ([], [], []), []
    pos = 0
    for _ in range(DEPTH):
        for k in range(3):
            norms[k].append(flat[pos:pos + D])
            pos += D
        biases.append(flat[pos:pos + 8 * 15 * 31].reshape(8, 15, 31))
        pos += BIAS_PAD
    final = flat[pos:pos + D]
    pos += D
    return [jnp.stack(n) for n in norms], jnp.stack(biases), final, flat[pos]


def kernel(x, ffn1_norm, ffn1_w_up, ffn1_w_down, mix_norm, w_in, na_rel_bias, w_branch_a, w_branch_b, w_out, ffn2_norm, ffn2_w_up, ffn2_w_down, final_norm, loss_target, m_ffn1_norm, m_ffn1_w_up, m_ffn1_w_down, m_mix_norm, m_w_in, m_na_rel_bias, m_w_branch_a, m_w_branch_b, m_w_out, m_ffn2_norm, m_ffn2_w_up, m_ffn2_w_down, m_final_norm, v_ffn1_norm, v_ffn1_w_up, v_ffn1_w_down, v_mix_norm, v_w_in, v_na_rel_bias, v_w_branch_a, v_w_branch_b, v_w_out, v_ffn2_norm, v_ffn2_w_up, v_ffn2_w_down, v_final_norm):
    t = x.shape[0] * x.shape[1]
    xs = x.reshape(t, D)
    tgt = loss_target.reshape(t, D)
    tables = _rope_tables()

    col_sharded = dict(up1=ffn1_w_up, win=w_in, wa=w_branch_a, wb=w_branch_b, up2=ffn2_w_up)
    row_sharded = dict(down1=ffn1_w_down, wo=w_out, down2=ffn2_w_down)
    shard = [{} for _ in range(DEPTH)]
    for layer in range(DEPTH):
        for name, arr in col_sharded.items():
            shard[layer][name] = arr[layer].T.astype(BF16)
        for name, arr in row_sharded.items():
            shard[layer][name] = arr[layer].astype(BF16)

    weights = [{} for _ in range(DEPTH)]
    travel = [(0, ("up1",)), (0, ("down1",)), (0, ("win",)), (0, ("wa", "wb", "wo")), (0, ("up2", "down2")),
              (1, ("up1", "down1")), (1, ("win",)), (1, ("wa", "wb", "wo")), (1, ("up2", "down2"))]
    group_of, chips_done, sibling_done = {}, {}, {}
    count = 0
    for i, (layer, names) in enumerate(travel):
        chips_done[i] = list(range(count, count + len(names)))
        count += len(names)
        for n in names:
            group_of[layer, n] = (i, names)
    gathered, token = _exchange_start(
        "gather", [shard[layer][n] for layer, names in travel for n in names], None, None, "w")
    zero = token[0, 0]

    biases = [_na_bias_table(na_rel_bias[layer] + zero) for layer in range(DEPTH)]

    def pass_on(i, behind):
        if i in chips_done:
            lands = _exchange_wait(gathered, behind, f"w{i}", which=chips_done.pop(i))
            sibling_done[i], _ = _exchange_start("forward", [], lands, None, f"p{i}")

    def fetcher(layer):
        def fetch(name, behind):
            if (layer, name) in group_of:
                i, names = group_of[layer, name]
                if i == 0:
                    behind = (behind, *biases)
                pass_on(i, behind)
                pass_on(i + 1, behind)
                for n, got in zip(names, _exchange_wait(sibling_done.pop(i), behind, f"p{i}")):
                    weights[layer][n] = got
                    del group_of[layer, n]
            return weights[layer][name]
        return fetch

    saved = []
    h = xs
    hn = _norm_fwd(xs, ffn1_norm[0] + zero, "first")
    for layer in range(DEPTH):
        bias = biases[layer]
        fetch = fetcher(layer)
        after_ffn2 = ffn1_norm[layer + 1] if layer + 1 < DEPTH else None
        h, hn, s1 = _ffn_forward(h, hn, fetch, ("up1", "down1"), f"f1l{layer}", mix_norm[layer])
        h, hn, s2 = _mixer_forward(h, hn, fetch, bias, tables, f"l{layer}", ffn2_norm[layer])
        h, hn, s3 = _ffn_forward(h, hn, fetch, ("up2", "down2"), f"f2l{layer}", after_ffn2)
        saved.append((s1, s2, s3, bias))
    loss_part, dh, dh_b, d_final = _loss_head(h, final_norm, tgt)

    d_norms = ([None] * DEPTH, [None] * DEPTH, [None] * DEPTH)
    d_bias = [None] * DEPTH
    sent = {}

    def sender(layer, suffix):
        def send(names, grads):
            tag = f"g{layer}{names[0]}{suffix}"
            handle, token = _exchange_start("scatter", grads, None, None, tag)
            for i, n in enumerate(names):
                sent[layer, n + suffix] = (handle, i, tag)
            return token
        return send

    for layer in reversed(range(DEPTH)):
        w = weights[layer]
        s1, s2, s3, bias = saved[layer]
        dh, dh_b, d_norms[2][layer] = _ffn_backward(
            dh, dh_b, s3, ffn2_norm[layer], w["up2"].reshape(2, F, D), w["down2"], f"f2l{layer}", sender(layer, "2"))
        dh, dh_b, d_norms[1][layer], d_bias[layer] = _mixer_backward(
            dh, dh_b, s2, mix_norm[layer], (w["win"], w["wa"], w["wb"], w["wo"]), bias, tables, f"l{layer}",
            sender(layer, ""))
        dh, dh_b, d_norms[0][layer] = _ffn_backward(
            dh, dh_b, s1, ffn1_norm[layer], w["up1"].reshape(2, F, D), w["down1"], f"f1l{layer}", sender(layer, "1"))
    grad_x = dh.reshape(x.shape)

    originals = dict(up1=(ffn1_w_up, m_ffn1_w_up, v_ffn1_w_up), down1=(ffn1_w_down, m_ffn1_w_down, v_ffn1_w_down),
                     win=(w_in, m_w_in, v_w_in), wa=(w_branch_a, m_w_branch_a, v_w_branch_a),
                     wb=(w_branch_b, m_w_branch_b, v_w_branch_b), wo=(w_out, m_w_out, v_w_out),
                     up2=(ffn2_w_up, m_ffn2_w_up, v_ffn2_w_up), down2=(ffn2_w_down, m_ffn2_w_down, v_ffn2_w_down))
    big = {}
    behind = dh
    landed = {}

    def received(layer, name):
        handle, i, tag = sent[layer, name]
        if tag not in landed:
            landed[tag] = _exchange_wait(handle, behind, tag)
        return landed[tag][i]

    for name in ("down2", "up2", "win", "wa", "wb", "wo", "down1", "up1"):
        wv, mv, vv = originals[name]
        if name in col_sharded:
            wv, mv, vv = (jnp.swapaxes(t, 1, 2) for t in (wv, mv, vv))
        big[name] = tuple(_sum_adamw(received(0, name), received(1, name), wv, mv, vv, name))
        behind = big[name][1]
        if name in col_sharded:
            big[name] = tuple(jnp.swapaxes(t, 1, 2) for t in big[name])

    small = _allreduce_small(_pack_small(d_norms, d_bias, d_final, loss_part[0, :1]), behind)
    g_norms, g_bias, g_final, loss = _unpack_small(small)
    w_small = _pack_small((ffn1_norm, mix_norm, ffn2_norm), na_rel_bias, final_norm)
    m_small = _pack_small((m_ffn1_norm, m_mix_norm, m_ffn2_norm), m_na_rel_bias, m_final_norm)
    v_small = _pack_small((v_ffn1_norm, v_mix_norm, v_ffn2_norm), v_na_rel_bias, v_final_norm)
    upd = _adamw(w_small[None], small[None], m_small[None], v_small[None], "small")
    small_out = [(g_norms, g_bias, g_final)] + [_unpack_small(u[0])[:3] for u in upd]

    outputs = [loss, grad_x]
    for kind in range(4):
        norms, bias_k, final_k = small_out[kind]
        outputs += [norms[0], big["up1"][kind], big["down1"][kind], norms[1], big["win"][kind], bias_k,
                    big["wa"][kind], big["wb"][kind], big["wo"][kind], norms[2], big["up2"][kind],
                    big["down2"][kind], final_k]
    return tuple(outputs)
```

```python
import numpy as np

import jax
import jax.numpy as jnp
from jax import lax
from jax.experimental import pallas as pl
from jax.experimental.pallas import tpu as pltpu

F32 = jnp.float32
BF16 = jnp.bfloat16
SDS = jax.ShapeDtypeStruct
BS = pl.BlockSpec
MESH = pl.DeviceIdType.MESH

D = 1024
S = 2048
F = 2816
DEPTH = 2
HEAD_DIM = 64
DILATIONS = (1, 4, 16)
HALF = 64
QKV_A = 2304
QKV_B = 1536
IN_W = 5888
N_DEV = 8
NA_ROWS = 32
GRID_W = 64
NA_KR = 8
ROPE_THETA = 10000.0
RMS_EPS = 1e-6
NEG = -1e30
SCALE = HEAD_DIM ** -0.5
ADAM_LR, ADAM_B1, ADAM_B2, ADAM_EPS, ADAM_WD, ADAM_STEP = 0.001, 0.9, 0.999, 1e-08, 0.01, 10
VMEM_LIMIT_V7X = 52 * 1024 * 1024
SMALL_ROWS = 120
BIAS_PAD = 3840
NA_FWD_ROWS = 8
NA_BWD_ROWS = 4
DIL_FWD_TILES = 8
DIL_BWD_TILES = 4


def _cp(*sem):
    return pltpu.CompilerParams(dimension_semantics=sem, vmem_limit_bytes=VMEM_LIMIT_V7X)


def _dot_nn(a, b):
    return jnp.dot(a, b, preferred_element_type=F32)


def _dot_nt(a, b):
    return lax.dot_general(a, b, (((1,), (1,)), ((), ())), preferred_element_type=F32)


def _dot_tn(a, b):
    return lax.dot_general(a, b, (((0,), (0,)), ((), ())), preferred_element_type=F32)


def _ds(start, size, stride):
    return pl.ds(start, size) if stride == 1 else pl.ds(start, size, stride=stride)


def _norm_fwd(x, g, tag):
    t = x.shape[0]
    tm = 512

    def body(x_ref, g_ref, o_ref):
        xv = x_ref[...]
        r = lax.rsqrt(jnp.mean(xv * xv, axis=-1, keepdims=True) + RMS_EPS)
        o_ref[...] = (xv * r * g_ref[...]).astype(BF16)

    return pl.pallas_call(
        body, name=f"norm_fwd_{tag}", grid=(t // tm,),
        in_specs=[BS((tm, D), lambda i: (i, 0)), BS((1, D), lambda i: (0, 0))],
        out_specs=BS((tm, D), lambda i: (i, 0)),
        out_shape=SDS((t, D), BF16), compiler_params=_cp("parallel"),
    )(x, g.reshape(1, D))


def _loss_head(x, g, tgt):
    t = x.shape[0]
    tm = 1024

    def body(x_ref, g_ref, t_ref, loss_ref, dx_ref, dxb_ref, dg_ref):
        @pl.when(pl.program_id(0) == 0)
        def _():
            dg_ref[...] = jnp.zeros_like(dg_ref)
            loss_ref[...] = jnp.zeros_like(loss_ref)

        xv = x_ref[...]
        gv = g_ref[...]
        r = lax.rsqrt(jnp.mean(xv * xv, axis=-1, keepdims=True) + RMS_EPS)
        xh = xv * r
        e = xh * gv - t_ref[...]
        loss_ref[...] += 0.5 * jnp.sum(jnp.mean(e * e, axis=-1, keepdims=True), axis=0, keepdims=True)
        dy = e * (1.0 / D)
        u = dy * gv
        dx = r * (u - xh * jnp.mean(xh * u, axis=-1, keepdims=True))
        dx_ref[...] = dx
        dxb_ref[...] = dx.astype(BF16)
        dg_ref[...] += jnp.sum(dy * xh, axis=0, keepdims=True)

    row = BS((tm, D), lambda i: (i, 0))
    vec = BS((1, D), lambda i: (0, 0))
    return pl.pallas_call(
        body, name="loss_head", grid=(t // tm,),
        in_specs=[row, vec, row], out_specs=[BS((1, 128), lambda i: (0, 0)), row, row, vec],
        out_shape=[SDS((1, 128), F32), SDS((t, D), F32), SDS((t, D), BF16), SDS((1, D), F32)],
        compiler_params=_cp("arbitrary"),
    )(x, g.reshape(1, D), tgt)


def _mm_nn(a, w, tag, res=None, scale=1.0, tm=1024, tn=None, next_g=None):
    c_n, t, k = a.shape
    n = w.shape[2]
    tn = n if tn is None else tn
    assert next_g is None or tn == n
    n_in = 2 + (res is not None) + (next_g is not None)

    def body(*refs):
        a_ref, w_ref = refs[0], refs[1]
        acc = _dot_nn(a_ref[0].astype(BF16), w_ref[0])
        for c in range(1, c_n):
            acc = acc + _dot_nn(a_ref[c].astype(BF16), w_ref[c])
        if scale != 1.0:
            acc = acc * scale
        if res is not None:
            acc = refs[2][...] + acc
        refs[n_in][...] = acc
        if next_g is not None:
            r = lax.rsqrt(jnp.mean(acc * acc, axis=-1, keepdims=True) + RMS_EPS)
            refs[n_in + 1][...] = (acc * r * refs[n_in - 1][...]).astype(BF16)

    w_mode = dict(pipeline_mode=pl.Buffered(1)) if tn == n else {}
    in_specs = [BS((c_n, tm, k), lambda i, j: (0, i, 0)), BS((c_n, k, tn), lambda i, j: (0, 0, j), **w_mode)]
    args = [a, w]
    out_specs = [BS((tm, tn), lambda i, j: (i, j))]
    out_shape = [SDS((t, n), F32)]
    if res is not None:
        in_specs.append(BS((tm, tn), lambda i, j: (i, j)))
        args.append(res)
    if next_g is not None:
        in_specs.append(BS((1, n), lambda i, j: (0, 0)))
        args.append(next_g.reshape(1, n))
        out_specs.append(BS((tm, tn), lambda i, j: (i, j)))
        out_shape.append(SDS((t, n), BF16))
    got = pl.pallas_call(
        body, name=f"mm_nn_{tag}", grid=(t // tm, n // tn), in_specs=in_specs, out_specs=out_specs,
        out_shape=out_shape, compiler_params=_cp("parallel", "parallel"),
    )(*args)
    return got if next_g is not None else got[0]


def _mm_nn_norm_bwd(parts, w, x, g, dres, tag, tm=256, single_w=False):
    t = parts[0].shape[1]
    n_parts = len(parts)

    def body(*refs):
        w_ref, x_ref, g_ref, dr_ref, dx_ref, dxb_ref, dg_ref = refs[n_parts:]

        @pl.when(pl.program_id(0) == 0)
        def _():
            dg_ref[...] = jnp.zeros_like(dg_ref)

        dh = None
        row = 0
        for a_ref, part in zip(refs, parts):
            for c in range(part.shape[0]):
                term = _dot_nn(a_ref[c].astype(BF16), w_ref[row:row + part.shape[2], :])
                dh = term if dh is None else dh + term
                row += part.shape[2]
        xv = x_ref[...]
        r = lax.rsqrt(jnp.mean(xv * xv, axis=-1, keepdims=True) + RMS_EPS)
        xh = xv * r
        u = dh * g_ref[...]
        dx = dr_ref[...] + r * (u - xh * jnp.mean(xh * u, axis=-1, keepdims=True))
        dx_ref[...] = dx
        dxb_ref[...] = dx.astype(BF16)
        dg_ref[...] += jnp.sum(dh * xh, axis=0, keepdims=True)

    row = BS((tm, D), lambda i: (i, 0))
    vec = BS((1, D), lambda i: (0, 0))
    return pl.pallas_call(
        body, name=f"mm_nn_norm_bwd_{tag}", grid=(t // tm,),
        in_specs=[BS((p.shape[0], tm, p.shape[2]), lambda i: (0, i, 0)) for p in parts]
        + [BS(w.shape, lambda i: (0, 0), **(dict(pipeline_mode=pl.Buffered(1)) if single_w else {})), row, vec, row],
        out_specs=[row, row, vec], out_shape=[SDS((t, D), F32), SDS((t, D), BF16), SDS((1, D), F32)],
        compiler_params=_cp("arbitrary"),
    )(*parts, w, x, g.reshape(1, D), dres)


def _mm_nt_rows(a, w, tag, tm, tn, n_total, w_row0, rope=None):
    t, k = a.shape
    assert w_row0 % tn == 0 and n_total % tn == 0
    j0 = w_row0 // tn

    def body(a_ref, w_ref, *rest):
        o_ref = rest[-1]
        o_ref[...] = _dot_nt(a_ref[...].astype(BF16), w_ref[...])
        if rope is not None:
            @pl.when(pl.program_id(0) == 0)
            def _():
                c = rest[0][...]
                sg = rest[1][...]
                first = (lax.broadcasted_iota(jnp.int32, (tm, 128), 1) % HEAD_DIM) < HEAD_DIM // 2
                for col in range(0, rope[2], 128):
                    v = o_ref[:, col:col + 128]
                    o_ref[:, col:col + 128] = v * c + _swap_halves(v, first) * sg

    in_specs = [BS((tm, k), lambda j, i: (i, 0)), BS((tn, k), lambda j, i: (j0 + j, 0))]
    args = [a, w]
    if rope is not None:
        assert rope[2] <= tn
        in_specs += [BS((tm, 128), lambda j, i: (i % (S // tm), 0))] * 2
        args += [rope[0], rope[1]]
    return pl.pallas_call(
        body, name=f"mm_nt_{tag}", grid=(n_total // tn, t // tm), in_specs=in_specs,
        out_specs=BS((tm, tn), lambda j, i: (i, j)), out_shape=SDS((t, n_total), F32),
        compiler_params=_cp("parallel", "parallel"),
    )(*args)


def _mm_tn(a, b, tag, scale=1.0, tmm=None, into=None, row0=0, rows=None):
    c_n, t, m = a.shape
    n = b.shape[1]
    if tmm is None:
        tmm = max(w for w in (1408, 768, 512, 256) if m % w == 0 and row0 % w == 0)
    tiles = m // tmm
    block0 = row0 // tmm
    assert row0 % tmm == 0 and m % tmm == 0

    def body(a_ref, b_ref, *rest):
        rest[-1][...] = (_dot_tn(a_ref[...].astype(BF16), b_ref[...].astype(BF16)) * scale).astype(BF16)

    in_specs = [BS((None, t, tmm), lambda c, mi: (c, 0, mi)), BS((t, n), lambda c, mi: (0, 0))]
    args = [a, b]
    if into is not None:
        in_specs.append(BS(memory_space=pl.ANY))
        args.append(into)
    return pl.pallas_call(
        body, name=f"mm_tn_{tag}", grid=(c_n, tiles), in_specs=in_specs,
        out_specs=BS((tmm, n), lambda c, mi: (block0 + c * tiles + mi, 0)),
        out_shape=SDS((rows or c_n * m, n) if into is None else into.shape, BF16),
        input_output_aliases={} if into is None else {2: 0},
        compiler_params=_cp("parallel", "parallel"),
    )(*args)


def _ffn_up(hn, wut, tag):
    t = hn.shape[0]
    tm, tn = 1024, 1408

    def body(h_ref, w_ref, gu_ref, act_ref):
        h = h_ref[...]
        g = _dot_nt(h, w_ref[0])
        u = _dot_nt(h, w_ref[1])
        sg = jax.nn.sigmoid(g)
        silu = g * sg
        gu_ref[0] = (u * (sg + silu * (1.0 - sg))).astype(BF16)
        gu_ref[1] = silu.astype(BF16)
        act_ref[...] = (silu * u).astype(BF16)

    return pl.pallas_call(
        body, name=f"ffn_up_{tag}", grid=(F // tn, t // tm),
        in_specs=[BS((tm, D), lambda j, i: (i, 0)), BS((2, tn, D), lambda j, i: (0, j, 0))],
        out_specs=[BS((2, tm, tn), lambda j, i: (0, i, j)), BS((tm, tn), lambda j, i: (i, j))],
        out_shape=[SDS((2, t, F), BF16), SDS((t, F), BF16)],
        compiler_params=_cp("parallel", "parallel"),
    )(hn, wut)


def _ffn_dact(dxo, wd, gu, tie, tag):
    t = dxo.shape[0]
    tm, tn = 1024, 1408

    def body(d_ref, w_ref, gu_ref, tie_ref, o_ref):
        dact = _dot_nt(d_ref[...] * 0.5, w_ref[...])
        o_ref[0] = (dact * gu_ref[0].astype(F32)).astype(BF16)
        o_ref[1] = (dact * gu_ref[1].astype(F32)).astype(BF16)

    return pl.pallas_call(
        body, name=f"ffn_dact_{tag}", grid=(F // tn, t // tm),
        in_specs=[BS((tm, D), lambda j, i: (i, 0)), BS((tn, D), lambda j, i: (j, 0)),
                  BS((2, tm, tn), lambda j, i: (0, i, j)), BS((8, 128), lambda j, i: (0, 0))],
        out_specs=BS((2, tm, tn), lambda j, i: (0, i, j)),
        out_shape=SDS((2, t, F), BF16), compiler_params=_cp("parallel", "parallel"),
    )(dxo, wd, gu, tie)


def _rope_tables():
    half = HEAD_DIM // 2
    inv_freq = ROPE_THETA ** (-jnp.arange(half, dtype=F32) / half)
    ang = jnp.arange(S).astype(F32)[:, None] * inv_freq[None, :]
    cos, sin = jnp.cos(ang), jnp.sin(ang)
    return jnp.concatenate([cos, cos, cos, cos], axis=1), jnp.concatenate([-sin, sin, -sin, sin], axis=1)


def _swap_halves(t, first_half):
    return jnp.where(first_half, pltpu.roll(t, 96, 1), pltpu.roll(t, 32, 1))


def _rope_bwd(dqs, dks, dvs, cos_t, sin_t):
    t = dqs[0].shape[0]
    tm = 1024

    def body(*refs):
        c = refs[9][...]
        sg = refs[10][...]
        o_ref = refs[11]
        first = (lax.broadcasted_iota(jnp.int32, (tm, 128), 1) % HEAD_DIM) < HEAD_DIM // 2
        for a in range(6):
            for hp in range(2):
                v = refs[a][:, 128 * hp:128 * (hp + 1)]
                col = 128 * (2 * a + hp)
                o_ref[:, col:col + 128] = (v * c + _swap_halves(v * sg, first)).astype(BF16)
        for a in range(6, 9):
            o_ref[:, 256 * a:256 * (a + 1)] = refs[a][...].astype(BF16)

    blk = BS((tm, 256), lambda i: (i, 0))
    tab = BS((tm, 128), lambda i: (i % (S // tm), 0))
    return pl.pallas_call(
        body, name="rope_bwd", grid=(t // tm,), in_specs=[blk] * 9 + [tab, tab],
        out_specs=BS((None, tm, QKV_A), lambda i: (0, i, 0)), out_shape=SDS((1, t, QKV_A), BF16),
        compiler_params=_cp("parallel"),
    )(*dqs, *dks, *dvs, cos_t, sin_t)


def _head_masks():
    lane = lax.broadcasted_iota(jnp.int32, (1, 128), 1)
    m0 = (lane < HEAD_DIM).astype(F32)
    return m0, 1.0 - m0


def _dil_geometry(d):
    sub = S // d
    q_rows = 128
    k_rows = min(256, sub)
    return sub, q_rows, sub // q_rows, k_rows


def _dil_tile(idx, d, keys_on_rows=False):
    sub, q_rows, nb, k_rows = _dil_geometry(d)
    r = idx // nb
    n = idx % nb
    k_sub = jnp.clip(q_rows * n - HALF, 0, sub - k_rows)
    if d == 1:
        q_start = pl.multiple_of(q_rows * n, q_rows)
        k_start = pl.multiple_of(k_sub, HALF)
    else:
        q_start = q_rows * n * d + r
        k_start = k_sub * d + r
    if keys_on_rows:
        ii = lax.broadcasted_iota(jnp.int32, (k_rows, 2 * q_rows), 1) % q_rows
        jj = lax.broadcasted_iota(jnp.int32, (k_rows, 2 * q_rows), 0)
    else:
        ii = lax.broadcasted_iota(jnp.int32, (q_rows, k_rows), 0)
        jj = lax.broadcasted_iota(jnp.int32, (q_rows, k_rows), 1)
    valid = jnp.abs(jj - ii + (k_sub - q_rows * n)) <= HALF
    return q_start, k_start, valid


def _dil_specs(grp):
    qs = BS((S, 128), lambda b, hp: (b, 2 * grp + hp))
    ks = BS((S, 128), lambda b, hp: (b, 6 + 2 * grp + hp))
    vs = BS((S, 128), lambda b, hp: (b, 12 + 2 * grp + hp))
    own = BS((S, 128), lambda b, hp: (b, hp))
    return qs, ks, vs, own


def _dil_fwd(qkr, proj, grp):
    t = qkr.shape[0]
    d = DILATIONS[grp]
    _, q_rows, nb, k_rows = _dil_geometry(d)

    def body(q_ref, k_ref, v_ref, o_ref, l_ref):
        masks = _head_masks()

        def step(i0, carry):
            geo = [_dil_tile(i0 * DIL_FWD_TILES + j, d) for j in range(DIL_FWD_TILES)]
            tiles = [(j, h) for j in range(DIL_FWD_TILES) for h in range(2)]
            qs = [q_ref[_ds(g[0], q_rows, d), :] for g in geo]
            kbs = [k_ref[_ds(g[1], k_rows, d), :].astype(BF16) for g in geo]
            ss = [jnp.where(geo[j][2], _dot_nt((qs[j] * masks[h]).astype(BF16), kbs[j]) * SCALE, NEG) for j, h in tiles]
            mxs = [jnp.max(s, axis=1, keepdims=True) for s in ss]
            ps = [jnp.exp(s - mx) for s, mx in zip(ss, mxs)]
            dens = [jnp.sum(p, axis=1, keepdims=True) for p in ps]
            vs = [v_ref[_ds(g[1], k_rows, d), :] for g in geo]
            outs = [_dot_nn(p.astype(BF16), (vs[j] * masks[h]).astype(BF16)) / den
                    for p, den, (j, h) in zip(ps, dens, tiles)]
            for j, g in enumerate(geo):
                o_ref[_ds(g[0], q_rows, d), :] = outs[2 * j] + outs[2 * j + 1]
                l_ref[_ds(g[0], q_rows, d), :] = (
                    (mxs[2 * j] + jnp.log(dens[2 * j])) * masks[0] + (mxs[2 * j + 1] + jnp.log(dens[2 * j + 1])) * masks[1])
            return carry

        lax.fori_loop(0, d * nb // DIL_FWD_TILES, step, 0)

    qs, ks, vs, own = _dil_specs(grp)
    return pl.pallas_call(
        body, name=f"dil_fwd_{grp}", grid=(t // S, 2), in_specs=[qs, ks, vs], out_specs=[own, own],
        out_shape=[SDS((t, 256), F32), SDS((t, 256), F32)], compiler_params=_cp("parallel", "parallel"),
    )(qkr, qkr, proj)


def _dil_bwd(qkr, proj, do, dlp, lse, grp):
    t = qkr.shape[0]
    d = DILATIONS[grp]
    _, q_rows, nb, k_rows = _dil_geometry(d)

    def body(q_ref, k_ref, v_ref, do_ref, dl_ref, l_ref, dq_ref, dk_ref, dv_ref):
        masks = _head_masks()
        dk_ref[...] = jnp.zeros_like(dk_ref)
        dv_ref[...] = jnp.zeros_like(dv_ref)

        def as_row(x2):
            xt = x2.T
            return jnp.concatenate([xt[0:1], xt[HEAD_DIM:HEAD_DIM + 1]], axis=1)

        def step(i0, carry):
            geo = [_dil_tile(i0 * DIL_BWD_TILES + j, d, keys_on_rows=True) for j in range(DIL_BWD_TILES)]
            q_ds = [_ds(g[0], q_rows, d) for g in geo]
            k_ds = [_ds(g[1], k_rows, d) for g in geo]
            qbs = [_both_heads(q_ref[r, :], masks).astype(BF16) for r in q_ds]
            kbs = [k_ref[r, :].astype(BF16) for r in k_ds]
            vbs = [v_ref[r, :].astype(BF16) for r in k_ds]
            dobs = [_both_heads(do_ref[r, :], masks).astype(BF16) for r in q_ds]
            l_rows = [as_row(l_ref[r, :]) for r in q_ds]
            dl_rows = [as_row(dl_ref[r, :]) for r in q_ds]
            ss = [jnp.where(g[2], _dot_nt(kb, qb) * SCALE, NEG) for g, kb, qb in zip(geo, kbs, qbs)]
            ps = [jnp.exp(s - lr) for s, lr in zip(ss, l_rows)]
            dps = [_dot_nt(vb, dob) for vb, dob in zip(vbs, dobs)]
            dss = [(p * (dp - dr)).astype(BF16) for p, dp, dr in zip(ps, dps, dl_rows)]
            dks = [_dot_nn(ds, qb) for ds, qb in zip(dss, qbs)]
            dvs = [_dot_nn(p.astype(BF16), dob) for p, dob in zip(ps, dobs)]
            dqs = [_own_heads(_dot_tn(ds, kb), masks) for ds, kb in zip(dss, kbs)]
            for j in range(DIL_BWD_TILES):
                dq_ref[q_ds[j], :] = dqs[j] * SCALE
                dk_ref[k_ds[j], :] += dks[j] * SCALE
                dv_ref[k_ds[j], :] += dvs[j]
            return carry

        lax.fori_loop(0, d * nb // DIL_BWD_TILES, step, 0)

    qs, ks, vs, own = _dil_specs(grp)
    return pl.pallas_call(
        body, name=f"dil_bwd_{grp}", grid=(t // S, 2), in_specs=[qs, ks, vs, own, own, own],
        out_specs=[own, own, own], out_shape=[SDS((t, 256), F32)] * 3,
        compiler_params=_cp("parallel", "parallel"),
    )(qkr, qkr, proj, do, dlp, lse)


def _mix_weights(l0, l1, l2):
    mx = jnp.maximum(jnp.maximum(l0, l1), l2)
    e0, e1, e2 = jnp.exp(l0 - mx), jnp.exp(l1 - mx), jnp.exp(l2 - mx)
    den = e0 + e1 + e2
    return e0 / den, e1 / den, e2 / den


def _combine_fwd(outs, lses):
    t = outs[0].shape[0]
    tm = 1024

    def body(o0, o1, o2, l0, l1, l2, y_ref):
        w0, w1, w2 = _mix_weights(l0[...], l1[...], l2[...])
        y_ref[...] = w0 * o0[...] + w1 * o1[...] + w2 * o2[...]

    blk = BS((tm, 256), lambda i: (i, 0))
    return pl.pallas_call(
        body, name="combine_fwd", grid=(t // tm,), in_specs=[blk] * 6, out_specs=blk,
        out_shape=SDS((t, 256), F32), compiler_params=_cp("parallel"),
    )(*outs, *lses)


def _head_sum(x):
    a = lax.broadcasted_iota(jnp.int32, (256, 256), 0) // HEAD_DIM
    b = lax.broadcasted_iota(jnp.int32, (256, 256), 1) // HEAD_DIM
    ones = (a == b).astype(BF16)
    hi = x.astype(BF16)
    lo = (x - hi.astype(F32)).astype(BF16)
    return _dot_nn(hi, ones) + _dot_nn(lo, ones)


def _combine_bwd(dya, outs, lses):
    t = dya.shape[0]
    tm = 1024

    def body(dy_ref, o0, o1, o2, l0, l1, l2, d0, d1, d2, e0, e1, e2):
        ws = _mix_weights(l0[...], l1[...], l2[...])
        dy = dy_ref[...]
        ya = ws[0] * o0[...] + ws[1] * o1[...] + ws[2] * o2[...]
        hs = _head_sum(dy * ya)
        for w, d_ref, e_ref in zip(ws, (d0, d1, d2), (e0, e1, e2)):
            d_ref[...] = w * dy
            e_ref[...] = w * hs

    blk = BS((tm, 256), lambda i: (i, 0))
    return pl.pallas_call(
        body, name="combine_bwd", grid=(t // tm,), in_specs=[blk] * 7, out_specs=[blk] * 6,
        out_shape=[SDS((t, 256), F32)] * 6, compiler_params=_cp("parallel"),
    )(dya, *outs, *lses)


def _na_bias_table(rel_bias):
    kw = NA_KR * GRID_W
    rev = jnp.pad(rel_bias.astype(F32)[:, :, ::-1], ((0, 0), (0, 0), (0, 128 - 31)))

    def body(r_ref, o_ref):
        lane = lax.broadcasted_iota(jnp.int32, (GRID_W, 128), 1)
        j = lax.broadcasted_iota(jnp.int32, (GRID_W, 128), 0)
        q = lane % GRID_W
        win_lo = jnp.clip(q - 8, 0, GRID_W - 16)
        valid = (j >= win_lo) & (j < win_lo + 16)
        for cls in range(NA_KR):
            for k in range(NA_KR):
                tiles = []
                for h in range(2):
                    row = jnp.broadcast_to(r_ref[h, cls + k:cls + k + 1, :], (GRID_W, 128))
                    tiles.append(pltpu.roll(row, (128 - 15 + GRID_W * h) % 128, 1, stride=1, stride_axis=0))
                o_ref[cls, GRID_W * k:GRID_W * (k + 1), :] = jnp.where(
                    valid, jnp.where(lane < GRID_W, tiles[0], tiles[1]), NEG)

    return pl.pallas_call(
        body, name="na_bias_table", grid=(4,),
        in_specs=[BS((2, 2 * NA_KR - 1, 128), lambda hp: (hp, 0, 0))],
        out_specs=BS((None, NA_KR, kw, 128), lambda hp: (hp, 0, 0, 0)),
        out_shape=SDS((4, NA_KR, kw, 128), F32), compiler_params=_cp("parallel"),
    )(rev)


def _na_row(i):
    lo = jnp.clip(i - NA_KR // 2, 0, NA_ROWS - NA_KR)
    return pl.multiple_of(GRID_W * i, GRID_W), pl.multiple_of(GRID_W * lo, GRID_W), lo - i + NA_KR - 1


def _both_heads(x, masks):
    return jnp.concatenate([x * masks[0], x * masks[1]], axis=0)


def _own_heads(r, masks):
    half = r.shape[0] // 2
    return r[:half] * masks[0] + r[half:] * masks[1]


def _na_fwd(proj, bias):
    t = proj.shape[0]
    kw = NA_KR * GRID_W

    def body(q_ref, k_ref, v_ref, b_ref, o_ref, l_ref):
        masks = _head_masks()

        def step(i0, carry):
            idx = [i0 * NA_FWD_ROWS + j for j in range(NA_FWD_ROWS)]
            rows = [_na_row(i) for i in idx]
            qbs = [_both_heads(q_ref[pl.ds(r[0], GRID_W), :], masks).astype(BF16) for r in rows]
            kbs = [k_ref[pl.ds(r[1], kw), :].astype(BF16) for r in rows]
            ss = [_dot_nt(kb, qb) * SCALE + b_ref[r[2]] for kb, qb, r in zip(kbs, qbs, rows)]
            mxs = [jnp.max(s, axis=0, keepdims=True) for s in ss]
            ps = [jnp.exp(s - mx) for s, mx in zip(ss, mxs)]
            dens = [jnp.sum(p, axis=0, keepdims=True) for p in ps]
            pbs = [(p / den).astype(BF16) for p, den in zip(ps, dens)]
            vbs = [v_ref[pl.ds(r[1], kw), :].astype(BF16) for r in rows]
            outs = [_own_heads(_dot_tn(pb, vb), masks) for pb, vb in zip(pbs, vbs)]
            for j, r in enumerate(rows):
                o_ref[pl.ds(r[0], GRID_W), :] = outs[j]
                l_ref[pl.ds(idx[j], 1), :] = mxs[j] + jnp.log(dens[j])
            return carry

        lax.fori_loop(0, NA_ROWS // NA_FWD_ROWS, step, 0)

    c0 = QKV_A // 128
    return pl.pallas_call(
        body, name="na_fwd", grid=(t // S, 4),
        in_specs=[BS((S, 128), lambda b, hp: (b, c0 + hp)), BS((S, 128), lambda b, hp: (b, c0 + 4 + hp)),
                  BS((S, 128), lambda b, hp: (b, c0 + 8 + hp)),
                  BS((None, NA_KR, kw, 128), lambda b, hp: (hp, 0, 0, 0))],
        out_specs=[BS((S, 128), lambda b, hp: (b, hp)), BS((None, None, NA_ROWS, 128), lambda b, hp: (b, hp, 0, 0))],
        out_shape=[SDS((t, 512), F32), SDS((t // S, 4, NA_ROWS, 128), F32)],
        compiler_params=_cp("parallel", "parallel"),
    )(proj, proj, proj, bias)


def _na_bwd(proj, bias, dyb, yb, lse):
    t = proj.shape[0]
    kw = NA_KR * GRID_W

    def body(q_ref, k_ref, v_ref, b_ref, do_ref, o_ref, l_ref, d_ref, db_ref):
        masks = _head_masks()
        ones = jnp.ones((8, 128), BF16)

        @pl.when(pl.program_id(1) == 0)
        def _():
            db_ref[...] = jnp.zeros_like(db_ref)

        d_ref[1:3] = jnp.zeros((2, S, 128), F32)

        def row_sums(x):
            hi = x.astype(BF16)
            lo = (x - hi.astype(F32)).astype(BF16)
            return (_dot_nt(ones, hi) + _dot_nt(ones, lo))[0:1]

        def step(i0, carry):
            idx = [i0 * NA_BWD_ROWS + j for j in range(NA_BWD_ROWS)]
            rows = [_na_row(i) for i in idx]
            q_ds = [pl.ds(r[0], GRID_W) for r in rows]
            k_ds = [pl.ds(r[1], kw) for r in rows]
            qbs = [_both_heads(q_ref[r, :], masks).astype(BF16) for r in q_ds]
            kbs = [k_ref[r, :].astype(BF16) for r in k_ds]
            vbs = [v_ref[r, :].astype(BF16) for r in k_ds]
            dos = [do_ref[r, :] for r in q_ds]
            dobs = [_both_heads(do, masks).astype(BF16) for do in dos]
            deltas = [row_sums(_both_heads(do * o_ref[r, :], masks)) for do, r in zip(dos, q_ds)]
            ss = [_dot_nt(kb, qb) * SCALE + b_ref[r[2]] for kb, qb, r in zip(kbs, qbs, rows)]
            ps = [jnp.exp(s - l_ref[pl.ds(i, 1), :]) for s, i in zip(ss, idx)]
            dps = [_dot_nt(vb, dob) for vb, dob in zip(vbs, dobs)]
            dss = [p * (dp - delta) for p, dp, delta in zip(ps, dps, deltas)]
            for ds, r in zip(dss, rows):
                db_ref[r[2]] += ds
            dsbs = [ds.astype(BF16) for ds in dss]
            dks = [_dot_nn(dsb, qb) for dsb, qb in zip(dsbs, qbs)]
            dvs = [_dot_nn(p.astype(BF16), dob) for p, dob in zip(ps, dobs)]
            dqs = [_own_heads(_dot_tn(dsb, kb), masks) for dsb, kb in zip(dsbs, kbs)]
            for j in range(NA_BWD_ROWS):
                d_ref[0, q_ds[j], :] = dqs[j] * SCALE
                d_ref[1, k_ds[j], :] += dks[j] * SCALE
                d_ref[2, k_ds[j], :] += dvs[j]
            return carry

        lax.fori_loop(0, NA_ROWS // NA_BWD_ROWS, step, 0)

    c0 = QKV_A // 128
    own = BS((S, 128), lambda hp, b: (b, hp))
    tab = BS((None, NA_KR, kw, 128), lambda hp, b: (hp, 0, 0, 0))
    return pl.pallas_call(
        body, name="na_bwd", grid=(4, t // S),
        in_specs=[BS((S, 128), lambda hp, b: (b, c0 + hp)), BS((S, 128), lambda hp, b: (b, c0 + 4 + hp)),
                  BS((S, 128), lambda hp, b: (b, c0 + 8 + hp)), tab, own, own,
                  BS((None, None, NA_ROWS, 128), lambda hp, b: (b, hp, 0, 0))],
        out_specs=[BS((3, S, 128), lambda hp, b: (0, b, hp)), tab],
        out_shape=[SDS((3, t, 512), F32), SDS((4, NA_KR, kw, 128), F32)],
        compiler_params=_cp("parallel", "arbitrary"),
    )(proj, proj, proj, bias, dyb, yb, lse)


def _na_dbias_lane_map():
    kw = NA_KR * GRID_W
    lane = np.arange(kw)
    blk, m = lane // GRID_W, lane % GRID_W
    target = np.full(kw, -1)
    target[m < 16] = (blk * 32 + 15 + m)[m < 16]
    target[m >= 49] = (((blk + 1) % NA_KR) * 32 + m - 49)[m >= 49]
    return jnp.asarray(target[:, None] == np.arange(kw)[None, :], BF16)


def _na_dbias(db):
    kw = NA_KR * GRID_W

    def body(x_ref, map_ref, o_ref, z_ref):
        for cls in range(NA_KR):
            xt = x_ref[cls].T
            for h in range(2):
                xv = xt[GRID_W * h:GRID_W * (h + 1)]
                y = xv[0:8]
                for g in range(1, GRID_W // 8):
                    y = y + pltpu.roll(xv[8 * g:8 * g + 8], kw - 8 * g, 1)
                d = y[0:1]
                for s in range(1, 8):
                    d = d + pltpu.roll(y[s:s + 1], kw - s, 1)
                z_ref[h, cls:cls + 1, :] = d
        for h in range(2):
            z = z_ref[h]
            hi = z.astype(BF16)
            lo = (z - hi.astype(F32)).astype(BF16)
            e = _dot_nn(hi, map_ref[...]) + _dot_nn(lo, map_ref[...])
            out = e[0:1]
            for cls in range(1, NA_KR):
                out = out + pltpu.roll(e[cls:cls + 1], 32 * cls, 1)
            o_ref[h] = jnp.broadcast_to(out, (8, kw))

    return pl.pallas_call(
        body, name="na_dbias", grid=(4,),
        in_specs=[BS((None, NA_KR, kw, 128), lambda hp: (hp, 0, 0, 0)), BS((kw, kw), lambda hp: (0, 0))],
        out_specs=BS((2, 8, kw), lambda hp: (hp, 0, 0)), out_shape=SDS((8, 8, kw), F32),
        scratch_shapes=[pltpu.VMEM((2, 8, kw), F32)], compiler_params=_cp("parallel"),
    )(db, _na_dbias_lane_map())


def _merge_fwd(ya, yb, proj, wat, wbt):
    t = ya.shape[0]
    tm, tn = 2048, 256
    ca = (QKV_A + QKV_B) // tn
    cb = ca + D // tn

    def body(ya_ref, yb_ref, la_ref, lb_ref, wa_ref, wb_ref, m_ref, za_ref, zb_ref):
        za = _dot_nt(ya_ref[...].astype(BF16), wa_ref[...])
        zb = _dot_nt(yb_ref[...].astype(BF16), wb_ref[...])
        m_ref[...] = (jax.nn.sigmoid(la_ref[...]) * za + jax.nn.sigmoid(lb_ref[...]) * zb).astype(BF16)
        za_ref[...] = za.astype(BF16)
        zb_ref[...] = zb.astype(BF16)

    out = BS((tm, tn), lambda i, j: (i, j))
    return pl.pallas_call(
        body, name="merge_fwd", grid=(t // tm, D // tn),
        in_specs=[BS((tm, 256), lambda i, j: (i, 0)), BS((tm, 512), lambda i, j: (i, 0)),
                  BS((tm, tn), lambda i, j: (i, ca + j)), BS((tm, tn), lambda i, j: (i, cb + j)),
                  BS((tn, 256), lambda i, j: (j, 0)), BS((tn, 512), lambda i, j: (j, 0))],
        out_specs=[out, out, out], out_shape=[SDS((t, D), BF16)] * 3,
        compiler_params=_cp("parallel", "parallel"),
    )(ya, yb, proj, proj, wat, wbt)


def _merge_bwd(dxo, wo, za, zb, proj):
    t = dxo.shape[0]
    tm, tn = 2048, 256
    ca = (QKV_A + QKV_B) // tn
    cb = ca + D // tn

    def body(d_ref, w_ref, za_ref, zb_ref, la_ref, lb_ref, dza_ref, dzb_ref, dl_ref):
        dmv = _dot_nt(d_ref[...], w_ref[...])
        ga = jax.nn.sigmoid(la_ref[...])
        gb = jax.nn.sigmoid(lb_ref[...])
        dza_ref[...] = (dmv * ga).astype(BF16)
        dzb_ref[...] = (dmv * gb).astype(BF16)
        dl_ref[0] = (dmv * za_ref[...].astype(F32) * ga * (1.0 - ga)).astype(BF16)
        dl_ref[1] = (dmv * zb_ref[...].astype(F32) * gb * (1.0 - gb)).astype(BF16)

    blk = BS((tm, tn), lambda i, j: (i, j))
    return pl.pallas_call(
        body, name="merge_bwd", grid=(t // tm, D // tn),
        in_specs=[BS((tm, D), lambda i, j: (i, 0)), BS((tn, D), lambda i, j: (j, 0)), blk, blk,
                  BS((tm, tn), lambda i, j: (i, ca + j)), BS((tm, tn), lambda i, j: (i, cb + j))],
        out_specs=[blk, blk, BS((2, tm, tn), lambda i, j: (0, i, j))],
        out_shape=[SDS((t, D), BF16), SDS((t, D), BF16), SDS((2, t, D), BF16)],
        compiler_params=_cp("parallel", "parallel"),
    )(dxo, wo, za, zb, proj, proj)


def _adamw_update(w, g, m, v):
    mn = ADAM_B1 * m + (1.0 - ADAM_B1) * g
    vn = ADAM_B2 * v + (1.0 - ADAM_B2) * (g * g)
    m_hat = mn / (1.0 - ADAM_B1 ** ADAM_STEP)
    v_hat = vn / (1.0 - ADAM_B2 ** ADAM_STEP)
    return -ADAM_LR * (m_hat / (jnp.sqrt(v_hat) + ADAM_EPS) + ADAM_WD * w), mn, vn


def _sum_adamw(recv0, recv1, w, m, v, tag):
    _, r, c = recv0.shape
    tr = max(rows for rows in range(16, r + 1, 16) if r % rows == 0 and rows * c <= 384 * 1024)

    def body(a_ref, b_ref, w_ref, m_ref, v_ref, g_ref, d_ref, mo_ref, vo_ref):
        def update(ref):
            g = ref[0].astype(F32)
            for s in range(1, N_DEV):
                g = g + ref[s].astype(F32)
            g_ref[...] = g
            d_ref[...], mo_ref[...], vo_ref[...] = _adamw_update(w_ref[...], g, m_ref[...], v_ref[...])

        pl.when(pl.program_id(0) == 0)(lambda: update(a_ref))
        pl.when(pl.program_id(0) == 1)(lambda: update(b_ref))

    blk = BS((None, tr, c), lambda layer, i: (layer, i, 0))
    return pl.pallas_call(
        body, name=f"sum_adamw_{tag}", grid=(2, r // tr),
        in_specs=[BS((N_DEV, tr, c), lambda layer, i: (0, i * (1 - layer), 0)),
                  BS((N_DEV, tr, c), lambda layer, i: (0, i * layer, 0)), blk, blk, blk],
        out_specs=[blk] * 4, out_shape=[SDS((2, r, c), F32)] * 4, compiler_params=_cp("arbitrary", "arbitrary"),
    )(recv0, recv1, w, m, v)


def _adamw(w, g, m, v, tag):
    layers, r, c = w.shape
    tr = next(r // k for k in (1, 2, 4, 8) if r // k <= 384 and r % (8 * k) == 0)

    def body(w_ref, g_ref, m_ref, v_ref, d_ref, mo_ref, vo_ref):
        d_ref[...], mo_ref[...], vo_ref[...] = _adamw_update(w_ref[...], g_ref[...], m_ref[...], v_ref[...])

    blk = BS((None, tr, c), lambda l, i: (l, i, 0))
    return pl.pallas_call(
        body, name=f"adamw_{tag}", grid=(layers, r // tr), in_specs=[blk] * 4, out_specs=[blk] * 3,
        out_shape=[SDS((layers, r, c), F32)] * 3, compiler_params=_cp("parallel", "parallel"),
    )(w, g, m, v)


def _place():
    return lax.axis_index("x"), lax.axis_index("y"), lax.axis_index("c")


def _flip(coord, bit):
    return 1 - coord if bit else coord


def _peers(x, y, c):
    peers = []
    for mask in range(1, N_DEV):
        p = (_flip(x, mask & 4), _flip(y, mask & 2), _flip(c, mask & 1))
        peers.append((p, 4 * p[0] + 2 * p[1] + p[2]))
    return peers


def _copy_plan(mode, src, land, x, y, c):
    me = 4 * x + 2 * y + c

    def device(mask):
        p = (_flip(x, mask & 4), _flip(y, mask & 2), _flip(c, mask & 1))
        return p, 4 * p[0] + 2 * p[1] + p[2]

    if mode == "scatter":
        r = land.shape[1]
        return [(p, src.at[pl.ds(i * r, r), :], land.at[me], land.at[i])
                for p, i in map(device, (1, 2, 3, 4, 5, 6, 7, 0))]
    r = land.shape[0] // N_DEV

    def rows(i):
        return land.at[pl.ds(i * r, r), :]

    if mode == "gather":
        return [(p, src, rows(me), rows(i)) for p, i in map(device, (1, 4, 2, 6, 0))]
    sibling = device(1)[0]
    return [(sibling, rows(device(m)[1]), rows(device(m)[1]), rows(device(m | 1)[1])) for m in (4, 2, 6)]


COPIES = dict(scatter=8, gather=5, forward=3)
HBM_SPEC = BS(memory_space=pltpu.HBM)
SEM_SPEC = BS(memory_space=pltpu.SEMAPHORE)
DATAFLOW = pltpu.SideEffectType.DATAFLOW_SIDE_EFFECTING


def _fresh(shape, dtype, tag):
    def body(o_ref):
        del o_ref

    return pl.pallas_call(body, name=f"fresh_{tag}", out_specs=BS(memory_space=pl.ANY), out_shape=SDS(shape, dtype))()


def _exchange_start(mode, srcs, lands, after, tag):
    if lands is None and mode == "gather":
        lands = [_fresh((N_DEV * s.shape[0], s.shape[1]), s.dtype, f"{tag}_{a}") for a, s in enumerate(srcs)]
    elif lands is None:
        lands = [_fresh((N_DEV, s.shape[0] // N_DEV, s.shape[1]), s.dtype, f"{tag}_{a}") for a, s in enumerate(srcs)]
    n, n_src, n_cp = len(lands), len(srcs), COPIES[mode]
    behind = [] if after is None else [after]

    def body(*refs):
        src_refs, land_refs = refs[:n_src], refs[n_src:n_src + n]
        send_sems, recv_sems = refs[n_src + n + len(behind)], refs[n_src + n + len(behind) + 1]
        token = refs[-1]
        for a in range(n):
            plan = _copy_plan(mode, src_refs[a] if n_src else None, land_refs[a], *_place())
            for k, (p, out, there, _) in enumerate(plan):
                pltpu.make_async_remote_copy(
                    src_ref=out, dst_ref=there, send_sem=send_sems.at[n_cp * a + k],
                    recv_sem=recv_sems.at[n_cp * a + k], device_id=p, device_id_type=MESH).start()
        token[...] = jnp.zeros_like(token)

    both = [*srcs, *lands]
    res = pl.pallas_call(
        body, name=f"{mode}_start_{tag}",
        out_shape=(pltpu.SemaphoreType.DMA((n_cp * n,)), pltpu.SemaphoreType.DMA((n_cp * n,)),
                   *[pltpu.HBM(v.shape, v.dtype) for v in both], SDS((8, 128), F32)),
        in_specs=[HBM_SPEC] * len(both) + [BS(memory_space=pl.ANY)] * len(behind),
        out_specs=(SEM_SPEC, SEM_SPEC, *[HBM_SPEC] * len(both), BS(memory_space=pltpu.VMEM)),
        input_output_aliases={i: 2 + i for i in range(len(both))},
        compiler_params=pltpu.CompilerParams(has_side_effects=DATAFLOW),
    )(*[pltpu.with_memory_space_constraint(v, pltpu.HBM) for v in both], *behind)
    return (mode, res[0], res[1], res[2:2 + n_src], res[2 + n_src:2 + n_src + n]), res[-1]


def _exchange_wait(handle, after, tag, which=None):
    mode, send_sems, recv_sems, srcs, lands = handle
    which = list(range(len(lands))) if which is None else list(which)
    n_cp = COPIES[mode]
    lands = [lands[a] for a in which]
    srcs = [srcs[a] for a in which] if srcs else []
    n, n_src = len(lands), len(srcs)
    afters = list(after) if isinstance(after, (tuple, list)) else [after]

    def body(*refs):
        src_refs, land_refs = refs[:n_src], refs[n_src:n_src + n]
        send_ref, recv_ref = refs[n_src + n], refs[n_src + n + 1]
        for i, a in enumerate(which):
            plan = _copy_plan(mode, src_refs[i] if n_src else None, land_refs[i], *_place())
            for k, (p, out, _, here) in enumerate(plan):
                cp = pltpu.make_async_remote_copy(
                    src_ref=out, dst_ref=here, send_sem=send_ref.at[n_cp * a + k], recv_sem=recv_ref.at[n_cp * a + k],
                    device_id=p, device_id_type=MESH)
                cp.wait_send()
                cp.wait_recv()

    both = [*srcs, *lands]
    res = pl.pallas_call(
        body, name=f"{mode}_wait_{tag}", out_shape=tuple(pltpu.HBM(v.shape, v.dtype) for v in both),
        in_specs=[HBM_SPEC] * len(both) + [SEM_SPEC, SEM_SPEC] + [BS(memory_space=pl.ANY)] * len(afters),
        out_specs=tuple([HBM_SPEC] * len(both)),
        input_output_aliases={i: i for i in range(len(both))},
        compiler_params=pltpu.CompilerParams(has_side_effects=DATAFLOW),
    )(*both, send_sems, recv_sems, *afters)
    return list(res[n_src:])


def _allreduce_small(vec, behind):
    rows = vec.shape[0]

    def body(x_ref, behind_ref, o_ref, buf_ref, send_sems, recv_sems):
        x, y, c = _place()
        me = 4 * x + 2 * y + c
        buf_ref[me] = x_ref[...]
        peers = _peers(x, y, c)

        def copy(k, slot):
            return pltpu.make_async_remote_copy(
                src_ref=x_ref, dst_ref=buf_ref.at[slot], send_sem=send_sems.at[k], recv_sem=recv_sems.at[k],
                device_id=peers[k][0], device_id_type=MESH)

        sends = [copy(k, me) for k in range(N_DEV - 1)]
        for cp in sends:
            cp.start()
        for k in range(N_DEV - 1):
            copy(k, peers[k][1]).wait_recv()
        for cp in sends:
            cp.wait_send()
        acc = buf_ref[0]
        for s in range(1, N_DEV):
            acc = acc + buf_ref[s]
        o_ref[...] = acc

    vmem = BS(memory_space=pltpu.VMEM)
    return pl.pallas_call(
        body, name="allreduce_small", in_specs=[vmem, BS(memory_space=pl.ANY)], out_specs=vmem,
        out_shape=SDS((rows, 128), F32),
        scratch_shapes=[pltpu.VMEM((N_DEV, rows, 128), F32), pltpu.SemaphoreType.DMA((7,)),
                        pltpu.SemaphoreType.DMA((7,))],
        compiler_params=pltpu.CompilerParams(has_side_effects=True),
    )(vec, behind)


def _ffn_forward(x, hn, fetch, names, tag, next_g):
    gu, act = _ffn_up(hn, fetch(names[0], hn).reshape(2, F, D), tag)
    got = _mm_nn(act[None], fetch(names[1], act)[None], f"down_{tag}", res=x, scale=0.5, tm=512, next_g=next_g)
    out, hn_next = got if next_g is not None else (got, None)
    return out, hn_next, (x, hn, gu, act)


def _ffn_backward(dxo, dxo_b, saved, norm_g, wut, wd, tag, send):
    x, hn, gu, act = saved
    d_wd = _mm_tn(act[None], dxo_b, f"dwd_{tag}", scale=0.5)
    du = _ffn_dact(dxo_b, wd, gu, send(("down",), [d_wd]), tag)
    d_wut = _mm_tn(du, hn, f"dwu_{tag}")
    token = send(("up",), [d_wut])
    return _mm_nn_norm_bwd([du], wut.reshape(2 * F, D), x, norm_g + token[0, 0], dxo, tag)


def _mixer_forward(x, hn, fetch, bias, tables, tag, next_g):
    proj = _mm_nt_rows(hn, fetch("win", hn), f"proj_{tag}", 1024, IN_W // 2, IN_W, 0, rope=(*tables, 2 * QKV_A // 3))
    qkr = proj
    outs, lses = [], []
    for grp in range(3):
        o, l = _dil_fwd(qkr, proj, grp)
        outs.append(o)
        lses.append(l)
    ya = _combine_fwd(outs, lses)
    yb, lse_b = _na_fwd(proj, bias)
    merged, za, zb = _merge_fwd(ya, yb, proj, fetch("wa", yb), fetch("wb", yb))
    out, hn_next = _mm_nn(merged[None], fetch("wo", merged)[None], f"out_{tag}", res=x, next_g=next_g)
    return out, hn_next, (x, hn, proj, qkr, outs, lses, ya, yb, lse_b, merged, za, zb)


def _mixer_backward(dxo, dxo_b, saved, norm_g, w, bias, tables, tag, send):
    wint, wat, wbt, wo = w
    x, hn, proj, qkr, outs, lses, ya, yb, lse_b, merged, za, zb = saved
    d_wo = _mm_tn(merged[None], dxo_b, f"dwo_{tag}")
    dza, dzb, dlog = _merge_bwd(dxo_b, wo, za, zb, proj)
    dya = _mm_nn(dza[None], wat[None], f"dya_{tag}")
    dyb = _mm_nn(dzb[None], wbt[None], f"dyb_{tag}")
    d_wat = _mm_tn(dza[None], ya, f"dwa_{tag}")
    d_wbt = _mm_tn(dzb[None], yb, f"dwb_{tag}")
    cb = _combine_bwd(dya, outs, lses)
    dqs, dks, dvs = [], [], []
    for grp in range(3):
        dq, dk, dv = _dil_bwd(qkr, proj, cb[grp], cb[3 + grp], lses[grp], grp)
        dqs.append(dq)
        dks.append(dk)
        dvs.append(dv)
    d_qkv_b, dbias_tab = _na_bwd(proj, bias, dyb, yb, lse_b)
    dbias = _na_dbias(dbias_tab)
    dproj = [_rope_bwd(dqs, dks, dvs, *tables), d_qkv_b, dlog]
    d_wint, row = None, 0
    for i, p in enumerate(dproj):
        d_wint = _mm_tn(p, hn, f"dwin{i}_{tag}", into=d_wint, row0=row, rows=IN_W)
        row += p.shape[0] * p.shape[2]
    token = send(("win", "wa", "wb", "wo"), [d_wint, d_wat, d_wbt, d_wo])
    dx, dx_b, dg = _mm_nn_norm_bwd(dproj, wint, x, norm_g + token[0, 0], dxo, f"mix_{tag}", tm=512, single_w=True)
    dbias = dbias[:, 0, :480].reshape(8, 15, 32)[:, :, :31]
    return dx, dx_b, dg, dbias


def _pack_small(norms, biases, final, loss=None):
    parts = []
    for layer in range(DEPTH):
        parts += [norms[0][layer], norms[1][layer], norms[2][layer],
                  jnp.pad(biases[layer].reshape(-1), (0, BIAS_PAD - 8 * 15 * 31))]
    parts.append(final)
    flat = jnp.concatenate([p.reshape(-1).astype(F32) for p in parts])
    if loss is not None:
        flat = jnp.concatenate([flat, loss.reshape(-1)])
    return jnp.pad(flat, (0, SMALL_ROWS * 128 - flat.shape[0])).reshape(SMALL_ROWS, 128)


def _unpack_small(packed):
    flat = packed.reshape(-1)
    norms, biases = ([], [], []), []
    pos = 0
    for _ in range(DEPTH):
        for k in range(3):
            norms[k].append(flat[pos:pos + D])
            pos += D
        biases.append(flat[pos:pos + 8 * 15 * 31].reshape(8, 15, 31))
        pos += BIAS_PAD
    final = flat[pos:pos + D]
    pos += D
    return [jnp.stack(n) for n in norms], jnp.stack(biases), final, flat[pos]


def kernel(x, ffn1_norm, ffn1_w_up, ffn1_w_down, mix_norm, w_in, na_rel_bias, w_branch_a, w_branch_b, w_out, ffn2_norm, ffn2_w_up, ffn2_w_down, final_norm, loss_target, m_ffn1_norm, m_ffn1_w_up, m_ffn1_w_down, m_mix_norm, m_w_in, m_na_rel_bias, m_w_branch_a, m_w_branch_b, m_w_out, m_ffn2_norm, m_ffn2_w_up, m_ffn2_w_down, m_final_norm, v_ffn1_norm, v_ffn1_w_up, v_ffn1_w_down, v_mix_norm, v_w_in, v_na_rel_bias, v_w_branch_a, v_w_branch_b, v_w_out, v_ffn2_norm, v_ffn2_w_up, v_ffn2_w_down, v_final_norm):
    t = x.shape[0] * x.shape[1]
    xs = x.reshape(t, D)
    tgt = loss_target.reshape(t, D)
    tables = _rope_tables()

    col_sharded = dict(up1=ffn1_w_up, win=w_in, wa=w_branch_a, wb=w_branch_b, up2=ffn2_w_up)
    row_sharded = dict(down1=ffn1_w_down, wo=w_out, down2=ffn2_w_down)
    shard = [{} for _ in range(DEPTH)]
    for layer in range(DEPTH):
        for name, arr in col_sharded.items():
            shard[layer][name] = arr[layer].T.astype(BF16)
        for name, arr in row_sharded.items():
            shard[layer][name] = arr[layer].astype(BF16)

    weights = [{} for _ in range(DEPTH)]
    travel = [(0, ("up1",)), (0, ("down1",)), (0, ("win",)), (0, ("wa", "wb", "wo")), (0, ("up2", "down2")),
              (1, ("up1", "down1")), (1, ("win",)), (1, ("wa", "wb", "wo")), (1, ("up2", "down2"))]
    group_of, chips_done, sibling_done = {}, {}, {}
    count = 0
    for i, (layer, names) in enumerate(travel):
        chips_done[i] = list(range(count, count + len(names)))
        count += len(names)
        for n in names:
            group_of[layer, n] = (i, names)
    gathered, token = _exchange_start(
        "gather", [shard[layer][n] for layer, names in travel for n in names], None, None, "w")
    zero = token[0, 0]

    biases = [_na_bias_table(na_rel_bias[layer] + zero) for layer in range(DEPTH)]

    def pass_on(i, behind):
        if i in chips_done:
            lands = _exchange_wait(gathered, behind, f"w{i}", which=chips_done.pop(i))
            sibling_done[i], _ = _exchange_start("forward", [], lands, None, f"p{i}")

    def fetcher(layer):
        def fetch(name, behind):
            if (layer, name) in group_of:
                i, names = group_of[layer, name]
                if i == 0:
                    behind = (behind, *biases)
                pass_on(i, behind)
                pass_on(i + 1, behind)
                for n, got in zip(names, _exchange_wait(sibling_done.pop(i), behind, f"p{i}")):
                    weights[layer][n] = got
                    del group_of[layer, n]
            return weights[layer][name]
        return fetch

    saved = []
    h = xs
    hn = _norm_fwd(xs, ffn1_norm[0] + zero, "first")
    for layer in range(DEPTH):
        bias = biases[layer]
        fetch = fetcher(layer)
        after_ffn2 = ffn1_norm[layer + 1] if layer + 1 < DEPTH else None
        h, hn, s1 = _ffn_forward(h, hn, fetch, ("up1", "down1"), f"f1l{layer}", mix_norm[layer])
        h, hn, s2 = _mixer_forward(h, hn, fetch, bias, tables, f"l{layer}", ffn2_norm[layer])
        h, hn, s3 = _ffn_forward(h, hn, fetch, ("up2", "down2"), f"f2l{layer}", after_ffn2)
        saved.append((s1, s2, s3, bias))
    loss_part, dh, dh_b, d_final = _loss_head(h, final_norm, tgt)

    d_norms = ([None] * DEPTH, [None] * DEPTH, [None] * DEPTH)
    d_bias = [None] * DEPTH
    sent = {}

    def sender(layer, suffix):
        def send(names, grads):
            tag = f"g{layer}{names[0]}{suffix}"
            handle, token = _exchange_start("scatter", grads, None, None, tag)
            for i, n in enumerate(names):
                sent[layer, n + suffix] = (handle, i, tag)
            return token
        return send

    for layer in reversed(range(DEPTH)):
        w = weights[layer]
        s1, s2, s3, bias = saved[layer]
        dh, dh_b, d_norms[2][layer] = _ffn_backward(
            dh, dh_b, s3, ffn2_norm[layer], w["up2"].reshape(2, F, D), w["down2"], f"f2l{layer}", sender(layer, "2"))
        dh, dh_b, d_norms[1][layer], d_bias[layer] = _mixer_backward(
            dh, dh_b, s2, mix_norm[layer], (w["win"], w["wa"], w["wb"], w["wo"]), bias, tables, f"l{layer}",
            sender(layer, ""))
        dh, dh_b, d_norms[0][layer] = _ffn_backward(
            dh, dh_b, s1, ffn1_norm[layer], w["up1"].reshape(2, F, D), w["down1"], f"f1l{layer}", sender(layer, "1"))
    grad_x = dh.reshape(x.shape)

    originals = dict(up1=(ffn1_w_up, m_ffn1_w_up, v_ffn1_w_up), down1=(ffn1_w_down, m_ffn1_w_down, v_ffn1_w_down),
                     win=(w_in, m_w_in, v_w_in), wa=(w_branch_a, m_w_branch_a, v_w_branch_a),
                     wb=(w_branch_b, m_w_branch_b, v_w_branch_b), wo=(w_out, m_w_out, v_w_out),
                     up2=(ffn2_w_up, m_ffn2_w_up, v_ffn2_w_up), down2=(ffn2_w_down, m_ffn2_w_down, v_ffn2_w_down))
    big = {}
    behind = dh
    landed = {}

    def received(layer, name):
        handle, i, tag = sent[layer, name]
        if tag not in landed:
            landed[tag] = _exchange_wait(handle, behind, tag)
        return landed[tag][i]

    for name in ("down2", "up2", "win", "wa", "wb", "wo", "down1", "up1"):
        wv, mv, vv = originals[name]
        if name in col_sharded:
            wv, mv, vv = (jnp.swapaxes(t, 1, 2) for t in (wv, mv, vv))
        big[name] = tuple(_sum_adamw(received(0, name), received(1, name), wv, mv, vv, name))
        behind = big[name][1]
        if name in col_sharded:
            big[name] = tuple(jnp.swapaxes(t, 1, 2) for t in big[name])

    small = _allreduce_small(_pack_small(d_norms, d_bias, d_final, loss_part[0, :1]), behind)
    g_norms, g_bias, g_final, loss = _unpack_small(small)
    w_small = _pack_small((ffn1_norm, mix_norm, ffn2_norm), na_rel_bias, final_norm)
    m_small = _pack_small((m_ffn1_norm, m_mix_norm, m_ffn2_norm), m_na_rel_bias, m_final_norm)
    v_small = _pack_small((v_ffn1_norm, v_mix_norm, v_ffn2_norm), v_na_rel_bias, v_final_norm)
    upd = _adamw(w_small[None], small[None], m_small[None], v_small[None], "small")
    small_out = [(g_norms, g_bias, g_final)] + [_unpack_small(u[0])[:3] for u in upd]

    outputs = [loss, grad_x]
    for kind in range(4):
        norms, bias_k, final_k = small_out[kind]
        outputs += [norms[0], big["up1"][kind], big["down1"][kind], norms[1], big["win"][kind], bias_k,
                    big["wa"][kind], big["wb"][kind], big["wo"][kind], norms[2], big["up2"][kind],
                    big["down2"][kind], final_k]
    return tuple(outputs)
```

```python
import numpy as np

import jax
import jax.numpy as jnp
from jax import lax
from jax.experimental import pallas as pl
from jax.experimental.pallas import tpu as pltpu

F32 = jnp.float32
BF16 = jnp.bfloat16
SDS = jax.ShapeDtypeStruct
BS = pl.BlockSpec
MESH = pl.DeviceIdType.MESH

D = 1024
S = 2048
F = 2816
DEPTH = 2
HEAD_DIM = 64
DILATIONS = (1, 4, 16)
HALF = 64
QKV_A = 2304
QKV_B = 1536
IN_W = 5888
N_DEV = 8
NA_ROWS = 32
GRID_W = 64
NA_KR = 8
ROPE_THETA = 10000.0
RMS_EPS = 1e-6
NEG = -1e30
SCALE = HEAD_DIM ** -0.5
ADAM_LR, ADAM_B1, ADAM_B2, ADAM_EPS, ADAM_WD, ADAM_STEP = 0.001, 0.9, 0.999, 1e-08, 0.01, 10
VMEM_LIMIT_V7X = 52 * 1024 * 1024
SMALL_ROWS = 120
BIAS_PAD = 3840
NA_FWD_ROWS = 8
NA_BWD_ROWS = 4
DIL_FWD_TILES = 8
DIL_BWD_TILES = 4


def _cp(*sem):
    return pltpu.CompilerParams(dimension_semantics=sem, vmem_limit_bytes=VMEM_LIMIT_V7X)


def _dot_nn(a, b):
    return jnp.dot(a, b, preferred_element_type=F32)


def _dot_nt(a, b):
    return lax.dot_general(a, b, (((1,), (1,)), ((), ())), preferred_element_type=F32)


def _dot_tn(a, b):
    return lax.dot_general(a, b, (((0,), (0,)), ((), ())), preferred_element_type=F32)


def _ds(start, size, stride):
    return pl.ds(start, size) if stride == 1 else pl.ds(start, size, stride=stride)


def _norm_fwd(x, g, tag):
    t = x.shape[0]
    tm = 512

    def body(x_ref, g_ref, o_ref):
        xv = x_ref[...]
        r = lax.rsqrt(jnp.mean(xv * xv, axis=-1, keepdims=True) + RMS_EPS)
        o_ref[...] = (xv * r * g_ref[...]).astype(BF16)

    return pl.pallas_call(
        body, name=f"norm_fwd_{tag}", grid=(t // tm,),
        in_specs=[BS((tm, D), lambda i: (i, 0)), BS((1, D), lambda i: (0, 0))],
        out_specs=BS((tm, D), lambda i: (i, 0)),
        out_shape=SDS((t, D), BF16), compiler_params=_cp("parallel"),
    )(x, g.reshape(1, D))


def _loss_head(x, g, tgt):
    t = x.shape[0]
    tm = 1024

    def body(x_ref, g_ref, t_ref, loss_ref, dx_ref, dxb_ref, dg_ref):
        @pl.when(pl.program_id(0) == 0)
        def _():
            dg_ref[...] = jnp.zeros_like(dg_ref)
            loss_ref[...] = jnp.zeros_like(loss_ref)

        xv = x_ref[...]
        gv = g_ref[...]
        r = lax.rsqrt(jnp.mean(xv * xv, axis=-1, keepdims=True) + RMS_EPS)
        xh = xv * r
        e = xh * gv - t_ref[...]
        loss_ref[...] += 0.5 * jnp.sum(jnp.mean(e * e, axis=-1, keepdims=True), axis=0, keepdims=True)
        dy = e * (1.0 / D)
        u = dy * gv
        dx = r * (u - xh * jnp.mean(xh * u, axis=-1, keepdims=True))
        dx_ref[...] = dx
        dxb_ref[...] = dx.astype(BF16)
        dg_ref[...] += jnp.sum(dy * xh, axis=0, keepdims=True)

    row = BS((tm, D), lambda i: (i, 0))
    vec = BS((1, D), lambda i: (0, 0))
    return pl.pallas_call(
        body, name="loss_head", grid=(t // tm,),
        in_specs=[row, vec, row], out_specs=[BS((1, 128), lambda i: (0, 0)), row, row, vec],
        out_shape=[SDS((1, 128), F32), SDS((t, D), F32), SDS((t, D), BF16), SDS((1, D), F32)],
        compiler_params=_cp("arbitrary"),
    )(x, g.reshape(1, D), tgt)


def _mm_nn(a, w, tag, res=None, scale=1.0, tm=1024, tn=None, next_g=None):
    c_n, t, k = a.shape
    n = w.shape[2]
    tn = n if tn is None else tn
    assert next_g is None or tn == n
    n_in = 2 + (res is not None) + (next_g is not None)

    def body(*refs):
        a_ref, w_ref = refs[0], refs[1]
        acc = _dot_nn(a_ref[0].astype(BF16), w_ref[0])
        for c in range(1, c_n):
            acc = acc + _dot_nn(a_ref[c].astype(BF16), w_ref[c])
        if scale != 1.0:
            acc = acc * scale
        if res is not None:
            acc = refs[2][...] + acc
        refs[n_in][...] = acc
        if next_g is not None:
            r = lax.rsqrt(jnp.mean(acc * acc, axis=-1, keepdims=True) + RMS_EPS)
            refs[n_in + 1][...] = (acc * r * refs[n_in - 1][...]).astype(BF16)

    w_mode = dict(pipeline_mode=pl.Buffered(1)) if tn == n else {}
    in_specs = [BS((c_n, tm, k), lambda i, j: (0, i, 0)), BS((c_n, k, tn), lambda i, j: (0, 0, j), **w_mode)]
    args = [a, w]
    out_specs = [BS((tm, tn), lambda i, j: (i, j))]
    out_shape = [SDS((t, n), F32)]
    if res is not None:
        in_specs.append(BS((tm, tn), lambda i, j: (i, j)))
        args.append(res)
    if next_g is not None:
        in_specs.append(BS((1, n), lambda i, j: (0, 0)))
        args.append(next_g.reshape(1, n))
        out_specs.append(BS((tm, tn), lambda i, j: (i, j)))
        out_shape.append(SDS((t, n), BF16))
    got = pl.pallas_call(
        body, name=f"mm_nn_{tag}", grid=(t // tm, n // tn), in_specs=in_specs, out_specs=out_specs,
        out_shape=out_shape, compiler_params=_cp("parallel", "parallel"),
    )(*args)
    return got if next_g is not None else got[0]


def _mm_nn_norm_bwd(parts, w, x, g, dres, tag, tm=256, single_w=False):
    t = parts[0].shape[1]
    n_parts = len(parts)

    def body(*refs):
        w_ref, x_ref, g_ref, dr_ref, dx_ref, dxb_ref, dg_ref = refs[n_parts:]

        @pl.when(pl.program_id(0) == 0)
        def _():
            dg_ref[...] = jnp.zeros_like(dg_ref)

        dh = None
        row = 0
        for a_ref, part in zip(refs, parts):
            for c in range(part.shape[0]):
                term = _dot_nn(a_ref[c].astype(BF16), w_ref[row:row + part.shape[2], :])
                dh = term if dh is None else dh + term
                row += part.shape[2]
        xv = x_ref[...]
        r = lax.rsqrt(jnp.mean(xv * xv, axis=-1, keepdims=True) + RMS_EPS)
        xh = xv * r
        u = dh * g_ref[...]
        dx = dr_ref[...] + r * (u - xh * jnp.mean(xh * u, axis=-1, keepdims=True))
        dx_ref[...] = dx
        dxb_ref[...] = dx.astype(BF16)
        dg_ref[...] += jnp.sum(dh * xh, axis=0, keepdims=True)

    row = BS((tm, D), lambda i: (i, 0))
    vec = BS((1, D), lambda i: (0, 0))
    return pl.pallas_call(
        body, name=f"mm_nn_norm_bwd_{tag}", grid=(t // tm,),
        in_specs=[BS((p.shape[0], tm, p.shape[2]), lambda i: (0, i, 0)) for p in parts]
        + [BS(w.shape, lambda i: (0, 0), **(dict(pipeline_mode=pl.Buffered(1)) if single_w else {})), row, vec, row],
        out_specs=[row, row, vec], out_shape=[SDS((t, D), F32), SDS((t, D), BF16), SDS((1, D), F32)],
        compiler_params=_cp("arbitrary"),
    )(*parts, w, x, g.reshape(1, D), dres)


def _mm_nt_rows(a, w, tag, tm, tn, n_total, w_row0, rope=None):
    t, k = a.shape
    assert w_row0 % tn == 0 and n_total % tn == 0
    j0 = w_row0 // tn

    def body(a_ref, w_ref, *rest):
        o_ref = rest[-1]
        o_ref[...] = _dot_nt(a_ref[...].astype(BF16), w_ref[...])
        if rope is not None:
            @pl.when(pl.program_id(0) == 0)
            def _():
                c = rest[0][...]
                sg = rest[1][...]
                first = (lax.broadcasted_iota(jnp.int32, (tm, 128), 1) % HEAD_DIM) < HEAD_DIM // 2
                for col in range(0, rope[2], 128):
                    v = o_ref[:, col:col + 128]
                    o_ref[:, col:col + 128] = v * c + _swap_halves(v, first) * sg

    in_specs = [BS((tm, k), lambda j, i: (i, 0)), BS((tn, k), lambda j, i: (j0 + j, 0))]
    args = [a, w]
    if rope is not None:
        assert rope[2] <= tn
        in_specs += [BS((tm, 128), lambda j, i: (i % (S // tm), 0))] * 2
        args += [rope[0], rope[1]]
    return pl.pallas_call(
        body, name=f"mm_nt_{tag}", grid=(n_total // tn, t // tm), in_specs=in_specs,
        out_specs=BS((tm, tn), lambda j, i: (i, j)), out_shape=SDS((t, n_total), F32),
        compiler_params=_cp("parallel", "parallel"),
    )(*args)


def _mm_tn(a, b, tag, scale=1.0, tmm=None, into=None, row0=0, rows=None):
    c_n, t, m = a.shape
    n = b.shape[1]
    if tmm is None:
        tmm = max(w for w in (1408, 768, 512, 256) if m % w == 0 and row0 % w == 0)
    tiles = m // tmm
    block0 = row0 // tmm
    assert row0 % tmm == 0 and m % tmm == 0

    def body(a_ref, b_ref, *rest):
        rest[-1][...] = (_dot_tn(a_ref[...].astype(BF16), b_ref[...].astype(BF16)) * scale).astype(BF16)

    in_specs = [BS((None, t, tmm), lambda c, mi: (c, 0, mi)), BS((t, n), lambda c, mi: (0, 0))]
    args = [a, b]
    if into is not None:
        in_specs.append(BS(memory_space=pl.ANY))
        args.append(into)
    return pl.pallas_call(
        body, name=f"mm_tn_{tag}", grid=(c_n, tiles), in_specs=in_specs,
        out_specs=BS((tmm, n), lambda c, mi: (block0 + c * tiles + mi, 0)),
        out_shape=SDS((rows or c_n * m, n) if into is None else into.shape, BF16),
        input_output_aliases={} if into is None else {2: 0},
        compiler_params=_cp("parallel", "parallel"),
    )(*args)


def _ffn_up(hn, wut, tag):
    t = hn.shape[0]
    tm, tn = 1024, 1408

    def body(h_ref, w_ref, gu_ref, act_ref):
        h = h_ref[...]
        g = _dot_nt(h, w_ref[0])
        u = _dot_nt(h, w_ref[1])
        sg = jax.nn.sigmoid(g)
        silu = g * sg
        gu_ref[0] = (u * (sg + silu * (1.0 - sg))).astype(BF16)
        gu_ref[1] = silu.astype(BF16)
        act_ref[...] = (silu * u).astype(BF16)

    return pl.pallas_call(
        body, name=f"ffn_up_{tag}", grid=(F // tn, t // tm),
        in_specs=[BS((tm, D), lambda j, i: (i, 0)), BS((2, tn, D), lambda j, i: (0, j, 0))],
        out_specs=[BS((2, tm, tn), lambda j, i: (0, i, j)), BS((tm, tn), lambda j, i: (i, j))],
        out_shape=[SDS((2, t, F), BF16), SDS((t, F), BF16)],
        compiler_params=_cp("parallel", "parallel"),
    )(hn, wut)


def _ffn_dact(dxo, wd, gu, tie, tag):
    t = dxo.shape[0]
    tm, tn = 1024, 1408

    def body(d_ref, w_ref, gu_ref, tie_ref, o_ref):
        dact = _dot_nt(d_ref[...] * 0.5, w_ref[...])
        o_ref[0] = (dact * gu_ref[0].astype(F32)).astype(BF16)
        o_ref[1] = (dact * gu_ref[1].astype(F32)).astype(BF16)

    return pl.pallas_call(
        body, name=f"ffn_dact_{tag}", grid=(F // tn, t // tm),
        in_specs=[BS((tm, D), lambda j, i: (i, 0)), BS((tn, D), lambda j, i: (j, 0)),
                  BS((2, tm, tn), lambda j, i: (0, i, j)), BS((8, 128), lambda j, i: (0, 0))],
        out_specs=BS((2, tm, tn), lambda j, i: (0, i, j)),
        out_shape=SDS((2, t, F), BF16), compiler_params=_cp("parallel", "parallel"),
    )(dxo, wd, gu, tie)


def _rope_tables():
    half = HEAD_DIM // 2
    inv_freq = ROPE_THETA ** (-jnp.arange(half, dtype=F32) / half)
    ang = jnp.arange(S).astype(F32)[:, None] * inv_freq[None, :]
    cos, sin = jnp.cos(ang), jnp.sin(ang)
    return jnp.concatenate([cos, cos, cos, cos], axis=1), jnp.concatenate([-sin, sin, -sin, sin], axis=1)


def _swap_halves(t, first_half):
    return jnp.where(first_half, pltpu.roll(t, 96, 1), pltpu.roll(t, 32, 1))


def _rope_bwd(dqs, dks, dvs, cos_t, sin_t):
    t = dqs[0].shape[0]
    tm = 1024

    def body(*refs):
        c = refs[9][...]
        sg = refs[10][...]
        o_ref = refs[11]
        first = (lax.broadcasted_iota(jnp.int32, (tm, 128), 1) % HEAD_DIM) < HEAD_DIM // 2
        for a in range(6):
            for hp in range(2):
                v = refs[a][:, 128 * hp:128 * (hp + 1)]
                col = 128 * (2 * a + hp)
                o_ref[:, col:col + 128] = (v * c + _swap_halves(v * sg, first)).astype(BF16)
        for a in range(6, 9):
            o_ref[:, 256 * a:256 * (a + 1)] = refs[a][...].astype(BF16)

    blk = BS((tm, 256), lambda i: (i, 0))
    tab = BS((tm, 128), lambda i: (i % (S // tm), 0))
    return pl.pallas_call(
        body, name="rope_bwd", grid=(t // tm,), in_specs=[blk] * 9 + [tab, tab],
        out_specs=BS((None, tm, QKV_A), lambda i: (0, i, 0)), out_shape=SDS((1, t, QKV_A), BF16),
        compiler_params=_cp("parallel"),
    )(*dqs, *dks, *dvs, cos_t, sin_t)


def _head_masks():
    lane = lax.broadcasted_iota(jnp.int32, (1, 128), 1)
    m0 = (lane < HEAD_DIM).astype(F32)
    return m0, 1.0 - m0


def _dil_geometry(d):
    sub = S // d
    q_rows = 128
    k_rows = min(256, sub)
    return sub, q_rows, sub // q_rows, k_rows


def _dil_tile(idx, d, keys_on_rows=False):
    sub, q_rows, nb, k_rows = _dil_geometry(d)
    r = idx // nb
    n = idx % nb
    k_sub = jnp.clip(q_rows * n - HALF, 0, sub - k_rows)
    if d == 1:
        q_start = pl.multiple_of(q_rows * n, q_rows)
        k_start = pl.multiple_of(k_sub, HALF)
    else:
        q_start = q_rows * n * d + r
        k_start = k_sub * d + r
    if keys_on_rows:
        ii = lax.broadcasted_iota(jnp.int32, (k_rows, 2 * q_rows), 1) % q_rows
        jj = lax.broadcasted_iota(jnp.int32, (k_rows, 2 * q_rows), 0)
    else:
        ii = lax.broadcasted_iota(jnp.int32, (q_rows, k_rows), 0)
        jj = lax.broadcasted_iota(jnp.int32, (q_rows, k_rows), 1)
    valid = jnp.abs(jj - ii + (k_sub - q_rows * n)) <= HALF
    return q_start, k_start, valid


def _dil_specs(grp):
    qs = BS((S, 128), lambda b, hp: (b, 2 * grp + hp))
    ks = BS((S, 128), lambda b, hp: (b, 6 + 2 * grp + hp))
    vs = BS((S, 128), lambda b, hp: (b, 12 + 2 * grp + hp))
    own = BS((S, 128), lambda b, hp: (b, hp))
    return qs, ks, vs, own


def _dil_fwd(qkr, proj, grp):
    t = qkr.shape[0]
    d = DILATIONS[grp]
    _, q_rows, nb, k_rows = _dil_geometry(d)

    def body(q_ref, k_ref, v_ref, o_ref, l_ref):
        masks = _head_masks()

        def step(i0, carry):
            geo = [_dil_tile(i0 * DIL_FWD_TILES + j, d) for j in range(DIL_FWD_TILES)]
            tiles = [(j, h) for j in range(DIL_FWD_TILES) for h in range(2)]
            qs = [q_ref[_ds(g[0], q_rows, d), :] for g in geo]
            kbs = [k_ref[_ds(g[1], k_rows, d), :].astype(BF16) for g in geo]
            ss = [jnp.where(geo[j][2], _dot_nt((qs[j] * masks[h]).astype(BF16), kbs[j]) * SCALE, NEG) for j, h in tiles]
            mxs = [jnp.max(s, axis=1, keepdims=True) for s in ss]
            ps = [jnp.exp(s - mx) for s, mx in zip(ss, mxs)]
            dens = [jnp.sum(p, axis=1, keepdims=True) for p in ps]
            vs = [v_ref[_ds(g[1], k_rows, d), :] for g in geo]
            outs = [_dot_nn(p.astype(BF16), (vs[j] * masks[h]).astype(BF16)) / den
                    for p, den, (j, h) in zip(ps, dens, tiles)]
            for j, g in enumerate(geo):
                o_ref[_ds(g[0], q_rows, d), :] = outs[2 * j] + outs[2 * j + 1]
                l_ref[_ds(g[0], q_rows, d), :] = (
                    (mxs[2 * j] + jnp.log(dens[2 * j])) * masks[0] + (mxs[2 * j + 1] + jnp.log(dens[2 * j + 1])) * masks[1])
            return carry

        lax.fori_loop(0, d * nb // DIL_FWD_TILES, step, 0)

    qs, ks, vs, own = _dil_specs(grp)
    return pl.pallas_call(
        body, name=f"dil_fwd_{grp}", grid=(t // S, 2), in_specs=[qs, ks, vs], out_specs=[own, own],
        out_shape=[SDS((t, 256), F32), SDS((t, 256), F32)], compiler_params=_cp("parallel", "parallel"),
    )(qkr, qkr, proj)


def _dil_bwd(qkr, proj, do, dlp, lse, grp):
    t = qkr.shape[0]
    d = DILATIONS[grp]
    _, q_rows, nb, k_rows = _dil_geometry(d)

    def body(q_ref, k_ref, v_ref, do_ref, dl_ref, l_ref, dq_ref, dk_ref, dv_ref):
        masks = _head_masks()
        dk_ref[...] = jnp.zeros_like(dk_ref)
        dv_ref[...] = jnp.zeros_like(dv_ref)

        def as_row(x2):
            xt = x2.T
            return jnp.concatenate([xt[0:1], xt[HEAD_DIM:HEAD_DIM + 1]], axis=1)

        def step(i0, carry):
            geo = [_dil_tile(i0 * DIL_BWD_TILES + j, d, keys_on_rows=True) for j in range(DIL_BWD_TILES)]
            q_ds = [_ds(g[0], q_rows, d) for g in geo]
            k_ds = [_ds(g[1], k_rows, d) for g in geo]
            qbs = [_both_heads(q_ref[r, :], masks).astype(BF16) for r in q_ds]
            kbs = [k_ref[r, :].astype(BF16) for r in k_ds]
            vbs = [v_ref[r, :].astype(BF16) for r in k_ds]
            dobs = [_both_heads(do_ref[r, :], masks).astype(BF16) for r in q_ds]
            l_rows = [as_row(l_ref[r, :]) for r in q_ds]
            dl_rows = [as_row(dl_ref[r, :]) for r in q_ds]
            ss = [jnp.where(g[2], _dot_nt(kb, qb) * SCALE, NEG) for g, kb, qb in zip(geo, kbs, qbs)]
            ps = [jnp.exp(s - lr) for s, lr in zip(ss, l_rows)]
            dps = [_dot_nt(vb, dob) for vb, dob in zip(vbs, dobs)]
            dss = [(p * (dp - dr)).astype(BF16) for p, dp, dr in zip(ps, dps, dl_rows)]
            dks = [_dot_nn(ds, qb) for ds, qb in zip(dss, qbs)]
            dvs = [_dot_nn(p.astype(BF16), dob) for p, dob in zip(ps, dobs)]
            dqs = [_own_heads(_dot_tn(ds, kb), masks) for ds, kb in zip(dss, kbs)]
            for j in range(DIL_BWD_TILES):
                dq_ref[q_ds[j], :] = dqs[j] * SCALE
                dk_ref[k_ds[j], :] += dks[j] * SCALE
                dv_ref[k_ds[j], :] += dvs[j]
            return carry

        lax.fori_loop(0, d * nb // DIL_BWD_TILES, step, 0)

    qs, ks, vs, own = _dil_specs(grp)
    return pl.pallas_call(
        body, name=f"dil_bwd_{grp}", grid=(t // S, 2), in_specs=[qs, ks, vs, own, own, own],
        out_specs=[own, own, own], out_shape=[SDS((t, 256), F32)] * 3,
        compiler_params=_cp("parallel", "parallel"),
    )(qkr, qkr, proj, do, dlp, lse)


def _mix_weights(l0, l1, l2):
    mx = jnp.maximum(jnp.maximum(l0, l1), l2)
    e0, e1, e2 = jnp.exp(l0 - mx), jnp.exp(l1 - mx), jnp.exp(l2 - mx)
    den = e0 + e1 + e2
    return e0 / den, e1 / den, e2 / den


def _combine_fwd(outs, lses):
    t = outs[0].shape[0]
    tm = 1024

    def body(o0, o1, o2, l0, l1, l2, y_ref):
        w0, w1, w2 = _mix_weights(l0[...], l1[...], l2[...])
        y_ref[...] = w0 * o0[...] + w1 * o1[...] + w2 * o2[...]

    blk = BS((tm, 256), lambda i: (i, 0))
    return pl.pallas_call(
        body, name="combine_fwd", grid=(t // tm,), in_specs=[blk] * 6, out_specs=blk,
        out_shape=SDS((t, 256), F32), compiler_params=_cp("parallel"),
    )(*outs, *lses)


def _head_sum(x):
    a = lax.broadcasted_iota(jnp.int32, (256, 256), 0) // HEAD_DIM
    b = lax.broadcasted_iota(jnp.int32, (256, 256), 1) // HEAD_DIM
    ones = (a == b).astype(BF16)
    hi = x.astype(BF16)
    lo = (x - hi.astype(F32)).astype(BF16)
    return _dot_nn(hi, ones) + _dot_nn(lo, ones)


def _combine_bwd(dya, outs, lses):
    t = dya.shape[0]
    tm = 1024

    def body(dy_ref, o0, o1, o2, l0, l1, l2, d0, d1, d2, e0, e1, e2):
        ws = _mix_weights(l0[...], l1[...], l2[...])
        dy = dy_ref[...]
        ya = ws[0] * o0[...] + ws[1] * o1[...] + ws[2] * o2[...]
        hs = _head_sum(dy * ya)
        for w, d_ref, e_ref in zip(ws, (d0, d1, d2), (e0, e1, e2)):
            d_ref[...] = w * dy
            e_ref[...] = w * hs

    blk = BS((tm, 256), lambda i: (i, 0))
    return pl.pallas_call(
        body, name="combine_bwd", grid=(t // tm,), in_specs=[blk] * 7, out_specs=[blk] * 6,
        out_shape=[SDS((t, 256), F32)] * 6, compiler_params=_cp("parallel"),
    )(dya, *outs, *lses)


def _na_bias_table(rel_bias):
    kw = NA_KR * GRID_W
    rev = jnp.pad(rel_bias.astype(F32)[:, :, ::-1], ((0, 0), (0, 0), (0, 128 - 31)))

    def body(r_ref, o_ref):
        lane = lax.broadcasted_iota(jnp.int32, (GRID_W, 128), 1)
        j = lax.broadcasted_iota(jnp.int32, (GRID_W, 128), 0)
        q = lane % GRID_W
        win_lo = jnp.clip(q - 8, 0, GRID_W - 16)
        valid = (j >= win_lo) & (j < win_lo + 16)
        for cls in range(NA_KR):
            for k in range(NA_KR):
                tiles = []
                for h in range(2):
                    row = jnp.broadcast_to(r_ref[h, cls + k:cls + k + 1, :], (GRID_W, 128))
                    tiles.append(pltpu.roll(row, (128 - 15 + GRID_W * h) % 128, 1, stride=1, stride_axis=0))
                o_ref[cls, GRID_W * k:GRID_W * (k + 1), :] = jnp.where(
                    valid, jnp.where(lane < GRID_W, tiles[0], tiles[1]), NEG)

    return pl.pallas_call(
        body, name="na_bias_table", grid=(4,),
        in_specs=[BS((2, 2 * NA_KR - 1, 128), lambda hp: (hp, 0, 0))],
        out_specs=BS((None, NA_KR, kw, 128), lambda hp: (hp, 0, 0, 0)),
        out_shape=SDS((4, NA_KR, kw, 128), F32), compiler_params=_cp("parallel"),
    )(rev)


def _na_row(i):
    lo = jnp.clip(i - NA_KR // 2, 0, NA_ROWS - NA_KR)
    return pl.multiple_of(GRID_W * i, GRID_W), pl.multiple_of(GRID_W * lo, GRID_W), lo - i + NA_KR - 1


def _both_heads(x, masks):
    return jnp.concatenate([x * masks[0], x * masks[1]], axis=0)


def _own_heads(r, masks):
    half = r.shape[0] // 2
    return r[:half] * masks[0] + r[half:] * masks[1]


def _na_fwd(proj, bias):
    t = proj.shape[0]
    kw = NA_KR * GRID_W

    def body(q_ref, k_ref, v_ref, b_ref, o_ref, l_ref):
        masks = _head_masks()

        def step(i0, carry):
            idx = [i0 * NA_FWD_ROWS + j for j in range(NA_FWD_ROWS)]
            rows = [_na_row(i) for i in idx]
            qbs = [_both_heads(q_ref[pl.ds(r[0], GRID_W), :], masks).astype(BF16) for r in rows]
            kbs = [k_ref[pl.ds(r[1], kw), :].astype(BF16) for r in rows]
            ss = [_dot_nt(kb, qb) * SCALE + b_ref[r[2]] for kb, qb, r in zip(kbs, qbs, rows)]
            mxs = [jnp.max(s, axis=0, keepdims=True) for s in ss]
            ps = [jnp.exp(s - mx) for s, mx in zip(ss, mxs)]
            dens = [jnp.sum(p, axis=0, keepdims=True) for p in ps]
            pbs = [(p / den).astype(BF16) for p, den in zip(ps, dens)]
            vbs = [v_ref[pl.ds(r[1], kw), :].astype(BF16) for r in rows]
            outs = [_own_heads(_dot_tn(pb, vb), masks) for pb, vb in zip(pbs, vbs)]
            for j, r in enumerate(rows):
                o_ref[pl.ds(r[0], GRID_W), :] = outs[j]
                l_ref[pl.ds(idx[j], 1), :] = mxs[j] + jnp.log(dens[j])
            return carry

        lax.fori_loop(0, NA_ROWS // NA_FWD_ROWS, step, 0)

    c0 = QKV_A // 128
    return pl.pallas_call(
        body, name="na_fwd", grid=(t // S, 4),
        in_specs=[BS((S, 128), lambda b, hp: (b, c0 + hp)), BS((S, 128), lambda b, hp: (b, c0 + 4 + hp)),
                  BS((S, 128), lambda b, hp: (b, c0 + 8 + hp)),
                  BS((None, NA_KR, kw, 128), lambda b, hp: (hp, 0, 0, 0))],
        out_specs=[BS((S, 128), lambda b, hp: (b, hp)), BS((None, None, NA_ROWS, 128), lambda b, hp: (b, hp, 0, 0))],
        out_shape=[SDS((t, 512), F32), SDS((t // S, 4, NA_ROWS, 128), F32)],
        compiler_params=_cp("parallel", "parallel"),
    )(proj, proj, proj, bias)


def _na_bwd(proj, bias, dyb, yb, lse):
    t = proj.shape[0]
    kw = NA_KR * GRID_W

    def body(q_ref, k_ref, v_ref, b_ref, do_ref, o_ref, l_ref, d_ref, db_ref):
        masks = _head_masks()
        ones = jnp.ones((8, 128), BF16)

        @pl.when(pl.program_id(1) == 0)
        def _():
            db_ref[...] = jnp.zeros_like(db_ref)

        d_ref[1:3] = jnp.zeros((2, S, 128), F32)

        def row_sums(x):
            hi = x.astype(BF16)
            lo = (x - hi.astype(F32)).astype(BF16)
            return (_dot_nt(ones, hi) + _dot_nt(ones, lo))[0:1]

        def step(i0, carry):
            idx = [i0 * NA_BWD_ROWS + j for j in range(NA_BWD_ROWS)]
            rows = [_na_row(i) for i in idx]
            q_ds = [pl.ds(r[0], GRID_W) for r in rows]
            k_ds = [pl.ds(r[1], kw) for r in rows]
            qbs = [_both_heads(q_ref[r, :], masks).astype(BF16) for r in q_ds]
            kbs = [k_ref[r, :].astype(BF16) for r in k_ds]
            vbs = [v_ref[r, :].astype(BF16) for r in k_ds]
            dos = [do_ref[r, :] for r in q_ds]
            dobs = [_both_heads(do, masks).astype(BF16) for do in dos]
            deltas = [row_sums(_both_heads(do * o_ref[r, :], masks)) for do, r in zip(dos, q_ds)]
            ss = [_dot_nt(kb, qb) * SCALE + b_ref[r[2]] for kb, qb, r in zip(kbs, qbs, rows)]
            ps = [jnp.exp(s - l_ref[pl.ds(i, 1), :]) for s, i in zip(ss, idx)]
            dps = [_dot_nt(vb, dob) for vb, dob in zip(vbs, dobs)]
            dss = [p * (dp - delta) for p, dp, delta in zip(ps, dps, deltas)]
            for ds, r in zip(dss, rows):
                db_ref[r[2]] += ds
            dsbs = [ds.astype(BF16) for ds in dss]
            dks = [_dot_nn(dsb, qb) for dsb, qb in zip(dsbs, qbs)]
            dvs = [_dot_nn(p.astype(BF16), dob) for p, dob in zip(ps, dobs)]
            dqs = [_own_heads(_dot_tn(dsb, kb), masks) for dsb, kb in zip(dsbs, kbs)]
            for j in range(NA_BWD_ROWS):
                d_ref[0, q_ds[j], :] = dqs[j] * SCALE
                d_ref[1, k_ds[j], :] += dks[j] * SCALE
                d_ref[2, k_ds[j], :] += dvs[j]
            return carry

        lax.fori_loop(0, NA_ROWS // NA_BWD_ROWS, step, 0)

    c0 = QKV_A // 128
    own = BS((S, 128), lambda hp, b: (b, hp))
    tab = BS((None, NA_KR, kw, 128), lambda hp, b: (hp, 0, 0, 0))
    return pl.pallas_call(
        body, name="na_bwd", grid=(4, t // S),
        in_specs=[BS((S, 128), lambda hp, b: (b, c0 + hp)), BS((S, 128), lambda hp, b: (b, c0 + 4 + hp)),
                  BS((S, 128), lambda hp, b: (b, c0 + 8 + hp)), tab, own, own,
                  BS((None, None, NA_ROWS, 128), lambda hp, b: (b, hp, 0, 0))],
        out_specs=[BS((3, S, 128), lambda hp, b: (0, b, hp)), tab],
        out_shape=[SDS((3, t, 512), F32), SDS((4, NA_KR, kw, 128), F32)],
        compiler_params=_cp("parallel", "arbitrary"),
    )(proj, proj, proj, bias, dyb, yb, lse)


def _na_dbias_lane_map():
    kw = NA_KR * GRID_W
    lane = np.arange(kw)
    blk, m = lane // GRID_W, lane % GRID_W
    target = np.full(kw, -1)
    target[m < 16] = (blk * 32 + 15 + m)[m < 16]
    target[m >= 49] = (((blk + 1) % NA_KR) * 32 + m - 49)[m >= 49]
    return jnp.asarray(target[:, None] == np.arange(kw)[None, :], BF16)


def _na_dbias(db):
    kw = NA_KR * GRID_W

    def body(x_ref, map_ref, o_ref, z_ref):
        for cls in range(NA_KR):
            xt = x_ref[cls].T
            for h in range(2):
                xv = xt[GRID_W * h:GRID_W * (h + 1)]
                y = xv[0:8]
                for g in range(1, GRID_W // 8):
                    y = y + pltpu.roll(xv[8 * g:8 * g + 8], kw - 8 * g, 1)
                d = y[0:1]
                for s in range(1, 8):
                    d = d + pltpu.roll(y[s:s + 1], kw - s, 1)
                z_ref[h, cls:cls + 1, :] = d
        for h in range(2):
            z = z_ref[h]
            hi = z.astype(BF16)
            lo = (z - hi.astype(F32)).astype(BF16)
            e = _dot_nn(hi, map_ref[...]) + _dot_nn(lo, map_ref[...])
            out = e[0:1]
            for cls in range(1, NA_KR):
                out = out + pltpu.roll(e[cls:cls + 1], 32 * cls, 1)
            o_ref[h] = jnp.broadcast_to(out, (8, kw))

    return pl.pallas_call(
        body, name="na_dbias", grid=(4,),
        in_specs=[BS((None, NA_KR, kw, 128), lambda hp: (hp, 0, 0, 0)), BS((kw, kw), lambda hp: (0, 0))],
        out_specs=BS((2, 8, kw), lambda hp: (hp, 0, 0)), out_shape=SDS((8, 8, kw), F32),
        scratch_shapes=[pltpu.VMEM((2, 8, kw), F32)], compiler_params=_cp("parallel"),
    )(db, _na_dbias_lane_map())


def _merge_fwd(ya, yb, proj, wat, wbt):
    t = ya.shape[0]
    tm, tn = 2048, 256
    ca = (QKV_A + QKV_B) // tn
    cb = ca + D // tn

    def body(ya_ref, yb_ref, la_ref, lb_ref, wa_ref, wb_ref, m_ref, za_ref, zb_ref):
        za = _dot_nt(ya_ref[...].astype(BF16), wa_ref[...])
        zb = _dot_nt(yb_ref[...].astype(BF16), wb_ref[...])
        m_ref[...] = (jax.nn.sigmoid(la_ref[...]) * za + jax.nn.sigmoid(lb_ref[...]) * zb).astype(BF16)
        za_ref[...] = za.astype(BF16)
        zb_ref[...] = zb.astype(BF16)

    out = BS((tm, tn), lambda i, j: (i, j))
    return pl.pallas_call(
        body, name="merge_fwd", grid=(t // tm, D // tn),
        in_specs=[BS((tm, 256), lambda i, j: (i, 0)), BS((tm, 512), lambda i, j: (i, 0)),
                  BS((tm, tn), lambda i, j: (i, ca + j)), BS((tm, tn), lambda i, j: (i, cb + j)),
                  BS((tn, 256), lambda i, j: (j, 0)), BS((tn, 512), lambda i, j: (j, 0))],
        out_specs=[out, out, out], out_shape=[SDS((t, D), BF16)] * 3,
        compiler_params=_cp("parallel", "parallel"),
    )(ya, yb, proj, proj, wat, wbt)


def _merge_bwd(dxo, wo, za, zb, proj):
    t = dxo.shape[0]
    tm, tn = 2048, 256
    ca = (QKV_A + QKV_B) // tn
    cb = ca + D // tn

    def body(d_ref, w_ref, za_ref, zb_ref, la_ref, lb_ref, dza_ref, dzb_ref, dl_ref):
        dmv = _dot_nt(d_ref[...], w_ref[...])
        ga = jax.nn.sigmoid(la_ref[...])
        gb = jax.nn.sigmoid(lb_ref[...])
        dza_ref[...] = (dmv * ga).astype(BF16)
        dzb_ref[...] = (dmv * gb).astype(BF16)
        dl_ref[0] = (dmv * za_ref[...].astype(F32) * ga * (1.0 - ga)).astype(BF16)
        dl_ref[1] = (dmv * zb_ref[...].astype(F32) * gb * (1.0 - gb)).astype(BF16)

    blk = BS((tm, tn), lambda i, j: (i, j))
    return pl.pallas_call(
        body, name="merge_bwd", grid=(t // tm, D // tn),
        in_specs=[BS((tm, D), lambda i, j: (i, 0)), BS((tn, D), lambda i, j: (j, 0)), blk, blk,
                  BS((tm, tn), lambda i, j: (i, ca + j)), BS((tm, tn), lambda i, j: (i, cb + j))],
        out_specs=[blk, blk, BS((2, tm, tn), lambda i, j: (0, i, j))],
        out_shape=[SDS((t, D), BF16), SDS((t, D), BF16), SDS((2, t, D), BF16)],
        compiler_params=_cp("parallel", "parallel"),
    )(dxo, wo, za, zb, proj, proj)


def _adamw_update(w, g, m, v):
    mn = ADAM_B1 * m + (1.0 - ADAM_B1) * g
    vn = ADAM_B2 * v + (1.0 - ADAM_B2) * (g * g)
    m_hat = mn / (1.0 - ADAM_B1 ** ADAM_STEP)
    v_hat = vn / (1.0 - ADAM_B2 ** ADAM_STEP)
    return -ADAM_LR * (m_hat / (jnp.sqrt(v_hat) + ADAM_EPS) + ADAM_WD * w), mn, vn


def _sum_adamw(recv0, recv1, w, m, v, tag):
    _, r, c = recv0.shape
    tr = max(rows for rows in range(16, r + 1, 16) if r % rows == 0 and rows * c <= 384 * 1024)

    def body(a_ref, b_ref, w_ref, m_ref, v_ref, g_ref, d_ref, mo_ref, vo_ref):
        def update(ref):
            g = ref[0].astype(F32)
            for s in range(1, N_DEV):
                g = g + ref[s].astype(F32)
            g_ref[...] = g
            d_ref[...], mo_ref[...], vo_ref[...] = _adamw_update(w_ref[...], g, m_ref[...], v_ref[...])

        pl.when(pl.program_id(0) == 0)(lambda: update(a_ref))
        pl.when(pl.program_id(0) == 1)(lambda: update(b_ref))

    blk = BS((None, tr, c), lambda layer, i: (layer, i, 0))
    return pl.pallas_call(
        body, name=f"sum_adamw_{tag}", grid=(2, r // tr),
        in_specs=[BS((N_DEV, tr, c), lambda layer, i: (0, i * (1 - layer), 0)),
                  BS((N_DEV, tr, c), lambda layer, i: (0, i * layer, 0)), blk, blk, blk],
        out_specs=[blk] * 4, out_shape=[SDS((2, r, c), F32)] * 4, compiler_params=_cp("arbitrary", "arbitrary"),
    )(recv0, recv1, w, m, v)


def _adamw(w, g, m, v, tag):
    layers, r, c = w.shape
    tr = next(r // k for k in (1, 2, 4, 8) if r // k <= 384 and r % (8 * k) == 0)

    def body(w_ref, g_ref, m_ref, v_ref, d_ref, mo_ref, vo_ref):
        d_ref[...], mo_ref[...], vo_ref[...] = _adamw_update(w_ref[...], g_ref[...], m_ref[...], v_ref[...])

    blk = BS((None, tr, c), lambda l, i: (l, i, 0))
    return pl.pallas_call(
        body, name=f"adamw_{tag}", grid=(layers, r // tr), in_specs=[blk] * 4, out_specs=[blk] * 3,
        out_shape=[SDS((layers, r, c), F32)] * 3, compiler_params=_cp("parallel", "parallel"),
    )(w, g, m, v)


def _place():
    return lax.axis_index("x"), lax.axis_index("y"), lax.axis_index("c")


def _flip(coord, bit):
    return 1 - coord if bit else coord


def _peers(x, y, c):
    peers = []
    for mask in range(1, N_DEV):
        p = (_flip(x, mask & 4), _flip(y, mask & 2), _flip(c, mask & 1))
        peers.append((p, 4 * p[0] + 2 * p[1] + p[2]))
    return peers


def _copy_plan(mode, src, land, x, y, c):
    me = 4 * x + 2 * y + c

    def device(mask):
        p = (_flip(x, mask & 4), _flip(y, mask & 2), _flip(c, mask & 1))
        return p, 4 * p[0] + 2 * p[1] + p[2]

    if mode == "scatter":
        r = land.shape[1]
        return [(p, src.at[pl.ds(i * r, r), :], land.at[me], land.at[i])
                for p, i in map(device, (1, 2, 3, 4, 5, 6, 7, 0))]
    r = land.shape[0] // N_DEV

    def rows(i):
        return land.at[pl.ds(i * r, r), :]

    if mode == "gather":
        return [(p, src, rows(me), rows(i)) for p, i in map(device, (1, 4, 2, 6, 0))]
    sibling = device(1)[0]
    return [(sibling, rows(device(m)[1]), rows(device(m)[1]), rows(device(m | 1)[1])) for m in (4, 2, 6)]


COPIES = dict(scatter=8, gather=5, forward=3)
HBM_SPEC = BS(memory_space=pltpu.HBM)
SEM_SPEC = BS(memory_space=pltpu.SEMAPHORE)
DATAFLOW = pltpu.SideEffectType.DATAFLOW_SIDE_EFFECTING


def _fresh(shape, dtype, tag):
    def body(o_ref):
        del o_ref

    return pl.pallas_call(body, name=f"fresh_{tag}", out_specs=BS(memory_space=pl.ANY), out_shape=SDS(shape, dtype))()


def _exchange_start(mode, srcs, lands, after, tag):
    if lands is None and mode == "gather":
        lands = [_fresh((N_DEV * s.shape[0], s.shape[1]), s.dtype, f"{tag}_{a}") for a, s in enumerate(srcs)]
    elif lands is None:
        lands = [_fresh((N_DEV, s.shape[0] // N_DEV, s.shape[1]), s.dtype, f"{tag}_{a}") for a, s in enumerate(srcs)]
    n, n_src, n_cp = len(lands), len(srcs), COPIES[mode]
    behind = [] if after is None else [after]

    def body(*refs):
        src_refs, land_refs = refs[:n_src], refs[n_src:n_src + n]
        send_sems, recv_sems = refs[n_src + n + len(behind)], refs[n_src + n + len(behind) + 1]
        token = refs[-1]
        for a in range(n):
            plan = _copy_plan(mode, src_refs[a] if n_src else None, land_refs[a], *_place())
            for k, (p, out, there, _) in enumerate(plan):
                pltpu.make_async_remote_copy(
                    src_ref=out, dst_ref=there, send_sem=send_sems.at[n_cp * a + k],
                    recv_sem=recv_sems.at[n_cp * a + k], device_id=p, device_id_type=MESH).start()
        token[...] = jnp.zeros_like(token)

    both = [*srcs, *lands]
    res = pl.pallas_call(
        body, name=f"{mode}_start_{tag}",
        out_shape=(pltpu.SemaphoreType.DMA((n_cp * n,)), pltpu.SemaphoreType.DMA((n_cp * n,)),
                   *[pltpu.HBM(v.shape, v.dtype) for v in both], SDS((8, 128), F32)),
        in_specs=[HBM_SPEC] * len(both) + [BS(memory_space=pl.ANY)] * len(behind),
        out_specs=(SEM_SPEC, SEM_SPEC, *[HBM_SPEC] * len(both), BS(memory_space=pltpu.VMEM)),
        input_output_aliases={i: 2 + i for i in range(len(both))},
        compiler_params=pltpu.CompilerParams(has_side_effects=DATAFLOW),
    )(*[pltpu.with_memory_space_constraint(v, pltpu.HBM) for v in both], *behind)
    return (mode, res[0], res[1], res[2:2 + n_src], res[2 + n_src:2 + n_src + n]), res[-1]


def _exchange_wait(handle, after, tag, which=None):
    mode, send_sems, recv_sems, srcs, lands = handle
    which = list(range(len(lands))) if which is None else list(which)
    n_cp = COPIES[mode]
    lands = [lands[a] for a in which]
    srcs = [srcs[a] for a in which] if srcs else []
    n, n_src = len(lands), len(srcs)
    afters = list(after) if isinstance(after, (tuple, list)) else [after]

    def body(*refs):
        src_refs, land_refs = refs[:n_src], refs[n_src:n_src + n]
        send_ref, recv_ref = refs[n_src + n], refs[n_src + n + 1]
        for i, a in enumerate(which):
            plan = _copy_plan(mode, src_refs[i] if n_src else None, land_refs[i], *_place())
            for k, (p, out, _, here) in enumerate(plan):
                cp = pltpu.make_async_remote_copy(
                    src_ref=out, dst_ref=here, send_sem=send_ref.at[n_cp * a + k], recv_sem=recv_ref.at[n_cp * a + k],
                    device_id=p, device_id_type=MESH)
                cp.wait_send()
                cp.wait_recv()

    both = [*srcs, *lands]
    res = pl.pallas_call(
        body, name=f"{mode}_wait_{tag}", out_shape=tuple(pltpu.HBM(v.shape, v.dtype) for v in both),
        in_specs=[HBM_SPEC] * len(both) + [SEM_SPEC, SEM_SPEC] + [BS(memory_space=pl.ANY)] * len(afters),
        out_specs=tuple([HBM_SPEC] * len(both)),
        input_output_aliases={i: i for i in range(len(both))},
        compiler_params=pltpu.CompilerParams(has_side_effects=DATAFLOW),
    )(*both, send_sems, recv_sems, *afters)
    return list(res[n_src:])


def _allreduce_small(vec, behind):
    rows = vec.shape[0]

    def body(x_ref, behind_ref, o_ref, buf_ref, send_sems, recv_sems):
        x, y, c = _place()
        me = 4 * x + 2 * y + c
        buf_ref[me] = x_ref[...]
        peers = _peers(x, y, c)

        def copy(k, slot):
            return pltpu.make_async_remote_copy(
                src_ref=x_ref, dst_ref=buf_ref.at[slot], send_sem=send_sems.at[k], recv_sem=recv_sems.at[k],
                device_id=peers[k][0], device_id_type=MESH)

        sends = [copy(k, me) for k in range(N_DEV - 1)]
        for cp in sends:
            cp.start()
        for k in range(N_DEV - 1):
            copy(k, peers[k][1]).wait_recv()
        for cp in sends:
            cp.wait_send()
        acc = buf_ref[0]
        for s in range(1, N_DEV):
            acc = acc + buf_ref[s]
        o_ref[...] = acc

    vmem = BS(memory_space=pltpu.VMEM)
    return pl.pallas_call(
        body, name="allreduce_small", in_specs=[vmem, BS(memory_space=pl.ANY)], out_specs=vmem,
        out_shape=SDS((rows, 128), F32),
        scratch_shapes=[pltpu.VMEM((N_DEV, rows, 128), F32), pltpu.SemaphoreType.DMA((7,)),
                        pltpu.SemaphoreType.DMA((7,))],
        compiler_params=pltpu.CompilerParams(has_side_effects=True),
    )(vec, behind)


def _ffn_forward(x, hn, fetch, names, tag, next_g):
    gu, act = _ffn_up(hn, fetch(names[0], hn).reshape(2, F, D), tag)
    got = _mm_nn(act[None], fetch(names[1], act)[None], f"down_{tag}", res=x, scale=0.5, tm=512, next_g=next_g)
    out, hn_next = got if next_g is not None else (got, None)
    return out, hn_next, (x, hn, gu, act)


def _ffn_backward(dxo, dxo_b, saved, norm_g, wut, wd, tag, send):
    x, hn, gu, act = saved
    d_wd = _mm_tn(act[None], dxo_b, f"dwd_{tag}", scale=0.5)
    du = _ffn_dact(dxo_b, wd, gu, send(("down",), [d_wd]), tag)
    d_wut = _mm_tn(du, hn, f"dwu_{tag}")
    token = send(("up",), [d_wut])
    return _mm_nn_norm_bwd([du], wut.reshape(2 * F, D), x, norm_g + token[0, 0], dxo, tag)


def _mixer_forward(x, hn, fetch, bias, tables, tag, next_g):
    proj = _mm_nt_rows(hn, fetch("win", hn), f"proj_{tag}", 1024, IN_W // 2, IN_W, 0, rope=(*tables, 2 * QKV_A // 3))
    qkr = proj
    outs, lses = [], []
    for grp in range(3):
        o, l = _dil_fwd(qkr, proj, grp)
        outs.append(o)
        lses.append(l)
    ya = _combine_fwd(outs, lses)
    yb, lse_b = _na_fwd(proj, bias)
    merged, za, zb = _merge_fwd(ya, yb, proj, fetch("wa", yb), fetch("wb", yb))
    out, hn_next = _mm_nn(merged[None], fetch("wo", merged)[None], f"out_{tag}", res=x, next_g=next_g)
    return out, hn_next, (x, hn, proj, qkr, outs, lses, ya, yb, lse_b, merged, za, zb)


def _mixer_backward(dxo, dxo_b, saved, norm_g, w, bias, tables, tag, send):
    wint, wat, wbt, wo = w
    x, hn, proj, qkr, outs, lses, ya, yb, lse_b, merged, za, zb = saved
    d_wo = _mm_tn(merged[None], dxo_b, f"dwo_{tag}")
    dza, dzb, dlog = _merge_bwd(dxo_b, wo, za, zb, proj)
    dya = _mm_nn(dza[None], wat[None], f"dya_{tag}")
    dyb = _mm_nn(dzb[None], wbt[None], f"dyb_{tag}")
    d_wat = _mm_tn(dza[None], ya, f"dwa_{tag}")
    d_wbt = _mm_tn(dzb[None], yb, f"dwb_{tag}")
    cb = _combine_bwd(dya, outs, lses)
    dqs, dks, dvs = [], [], []
    for grp in range(3):
        dq, dk, dv = _dil_bwd(qkr, proj, cb[grp], cb[3 + grp], lses[grp], grp)
        dqs.append(dq)
        dks.append(dk)
        dvs.append(dv)
    d_qkv_b, dbias_tab = _na_bwd(proj, bias, dyb, yb, lse_b)
    dbias = _na_dbias(dbias_tab)
    dproj = [_rope_bwd(dqs, dks, dvs, *tables), d_qkv_b, dlog]
    d_wint, row = None, 0
    for i, p in enumerate(dproj):
        d_wint = _mm_tn(p, hn, f"dwin{i}_{tag}", into=d_wint, row0=row, rows=IN_W)
        row += p.shape[0] * p.shape[2]
    token = send(("win", "wa", "wb", "wo"), [d_wint, d_wat, d_wbt, d_wo])
    dx, dx_b, dg = _mm_nn_norm_bwd(dproj, wint, x, norm_g + token[0, 0], dxo, f"mix_{tag}", tm=512, single_w=True)
    dbias = dbias[:, 0, :480].reshape(8, 15, 32)[:, :, :31]
    return dx, dx_b, dg, dbias


def _pack_small(norms, biases, final, loss=None):
    parts = []
    for layer in range(DEPTH):
        parts += [norms[0][layer], norms[1][layer], norms[2][layer],
                  jnp.pad(biases[layer].reshape(-1), (0, BIAS_PAD - 8 * 15 * 31))]
    parts.append(final)
    flat = jnp.concatenate([p.reshape(-1).astype(F32) for p in parts])
    if loss is not None:
        flat = jnp.concatenate([flat, loss.reshape(-1)])
    return jnp.pad(flat, (0, SMALL_ROWS * 128 - flat.shape[0])).reshape(SMALL_ROWS, 128)


def _unpack_small(packed):
    flat = packed.reshape(-1)
    norms, biases = ([], [], []), []
    pos = 0
    for _ in range(DEPTH):
        for k in range(3):
            norms[k].append(flat[pos:pos + D])
            pos += D
        biases.append(flat[pos:pos + 8 * 15 * 31].reshape(8, 15, 31))
        pos += BIAS_PAD
    final = flat[pos:pos + D]
    pos += D
    return [jnp.stack(n) for n in norms], jnp.stack(biases), final, flat[pos]


def kernel(x, ffn1_norm, ffn1_w_up, ffn1_w_down, mix_norm, w_in, na_rel_bias, w_branch_a, w_branch_b, w_out, ffn2_norm, ffn2_w_up, ffn2_w_down, final_norm, loss_target, m_ffn1_norm, m_ffn1_w_up, m_ffn1_w_down, m_mix_norm, m_w_in, m_na_rel_bias, m_w_branch_a, m_w_branch_b, m_w_out, m_ffn2_norm, m_ffn2_w_up, m_ffn2_w_down, m_final_norm, v_ffn1_norm, v_ffn1_w_up, v_ffn1_w_down, v_mix_norm, v_w_in, v_na_rel_bias, v_w_branch_a, v_w_branch_b, v_w_out, v_ffn2_norm, v_ffn2_w_up, v_ffn2_w_down, v_final_norm):
    t = x.shape[0] * x.shape[1]
    xs = x.reshape(t, D)
    tgt = loss_target.reshape(t, D)
    tables = _rope_tables()

    col_sharded = dict(up1=ffn1_w_up, win=w_in, wa=w_branch_a, wb=w_branch_b, up2=ffn2_w_up)
    row_sharded = dict(down1=ffn1_w_down, wo=w_out, down2=ffn2_w_down)
    shard = [{} for _ in range(DEPTH)]
    for layer in range(DEPTH):
        for name, arr in col_sharded.items():
            shard[layer][name] = arr[layer].T.astype(BF16)
        for name, arr in row_sharded.items():
            shard[layer][name] = arr[layer].astype(BF16)

    weights = [{} for _ in range(DEPTH)]
    travel = [(0, ("up1",)), (0, ("down1",)), (0, ("win",)), (0, ("wa", "wb", "wo")), (0, ("up2", "down2")),
              (1, ("up1", "down1")), (1, ("win",)), (1, ("wa", "wb", "wo")), (1, ("up2", "down2"))]
    group_of, chips_done, sibling_done, passing = {}, {}, {}, {}
    count = 0
    for i, (layer, names) in enumerate(travel):
        chips_done[i] = list(range(count, count + len(names)))
        count += len(names)
        for n in names:
            group_of[layer, n] = (i, names)
    gathered, token = _exchange_start(
        "gather", [shard[layer][n] for layer, names in travel for n in names], None, None, "w")
    zero = token[0, 0]

    biases = [_na_bias_table(na_rel_bias[layer] + zero) for layer in range(DEPTH)]

    def pass_on(i, behind):
        if i in chips_done:
            lands = _exchange_wait(gathered, behind, f"w{i}", which=chips_done.pop(i))
            sibling_done[i], passing[i] = _exchange_start("forward", [], lands, None, f"p{i}")

    def fetcher(layer):
        def fetch(name, behind):
            if (layer, name) in group_of:
                i, names = group_of[layer, name]
                if i == 0:
                    behind = (behind, *biases)
                behind = behind if isinstance(behind, tuple) else (behind,)
                pass_on(i, behind)
                pass_on(i + 1, behind)
                if i + 1 in passing:
                    behind = (*behind, passing[i + 1])
                for n, got in zip(names, _exchange_wait(sibling_done.pop(i), behind, f"p{i}")):
                    weights[layer][n] = got
                    del group_of[layer, n]
            return weights[layer][name]
        return fetch

    saved = []
    h = xs
    hn = _norm_fwd(xs, ffn1_norm[0] + zero, "first")
    for layer in range(DEPTH):
        bias = biases[layer]
        fetch = fetcher(layer)
        after_ffn2 = ffn1_norm[layer + 1] if layer + 1 < DEPTH else None
        h, hn, s1 = _ffn_forward(h, hn, fetch, ("up1", "down1"), f"f1l{layer}", mix_norm[layer])
        h, hn, s2 = _mixer_forward(h, hn, fetch, bias, tables, f"l{layer}", ffn2_norm[layer])
        h, hn, s3 = _ffn_forward(h, hn, fetch, ("up2", "down2"), f"f2l{layer}", after_ffn2)
        saved.append((s1, s2, s3, bias))
    loss_part, dh, dh_b, d_final = _loss_head(h, final_norm, tgt)

    d_norms = ([None] * DEPTH, [None] * DEPTH, [None] * DEPTH)
    d_bias = [None] * DEPTH
    sent = {}

    def sender(layer, suffix):
        def send(names, grads):
            tag = f"g{layer}{names[0]}{suffix}"
            handle, token = _exchange_start("scatter", grads, None, None, tag)
            for i, n in enumerate(names):
                sent[layer, n + suffix] = (handle, i, tag)
            return token
        return send

    for layer in reversed(range(DEPTH)):
        w = weights[layer]
        s1, s2, s3, bias = saved[layer]
        dh, dh_b, d_norms[2][layer] = _ffn_backward(
            dh, dh_b, s3, ffn2_norm[layer], w["up2"].reshape(2, F, D), w["down2"], f"f2l{layer}", sender(layer, "2"))
        dh, dh_b, d_norms[1][layer], d_bias[layer] = _mixer_backward(
            dh, dh_b, s2, mix_norm[layer], (w["win"], w["wa"], w["wb"], w["wo"]), bias, tables, f"l{layer}",
            sender(layer, ""))
        dh, dh_b, d_norms[0][layer] = _ffn_backward(
            dh, dh_b, s1, ffn1_norm[layer], w["up1"].reshape(2, F, D), w["down1"], f"f1l{layer}", sender(layer, "1"))
    grad_x = dh.reshape(x.shape)

    originals = dict(up1=(ffn1_w_up, m_ffn1_w_up, v_ffn1_w_up), down1=(ffn1_w_down, m_ffn1_w_down, v_ffn1_w_down),
                     win=(w_in, m_w_in, v_w_in), wa=(w_branch_a, m_w_branch_a, v_w_branch_a),
                     wb=(w_branch_b, m_w_branch_b, v_w_branch_b), wo=(w_out, m_w_out, v_w_out),
                     up2=(ffn2_w_up, m_ffn2_w_up, v_ffn2_w_up), down2=(ffn2_w_down, m_ffn2_w_down, v_ffn2_w_down))
    big = {}
    behind = dh
    landed = {}

    def received(layer, name):
        handle, i, tag = sent[layer, name]
        if tag not in landed:
            landed[tag] = _exchange_wait(handle, behind, tag)
        return landed[tag][i]

    for name in ("down2", "up2", "win", "wa", "wb", "wo", "down1", "up1"):
        wv, mv, vv = originals[name]
        if name in col_sharded:
            wv, mv, vv = (jnp.swapaxes(t, 1, 2) for t in (wv, mv, vv))
        big[name] = tuple(_sum_adamw(received(0, name), received(1, name), wv, mv, vv, name))
        behind = big[name][1]
        if name in col_sharded:
            big[name] = tuple(jnp.swapaxes(t, 1, 2) for t in big[name])

    small = _allreduce_small(_pack_small(d_norms, d_bias, d_final, loss_part[0, :1]), behind)
    g_norms, g_bias, g_final, loss = _unpack_small(small)
    w_small = _pack_small((ffn1_norm, mix_norm, ffn2_norm), na_rel_bias, final_norm)
    m_small = _pack_small((m_ffn1_norm, m_mix_norm, m_ffn2_norm), m_na_rel_bias, m_final_norm)
    v_small = _pack_small((v_ffn1_norm, v_mix_norm, v_ffn2_norm), v_na_rel_bias, v_final_norm)
    upd = _adamw(w_small[None], small[None], m_small[None], v_small[None], "small")
    small_out = [(g_norms, g_bias, g_final)] + [_unpack_small(u[0])[:3] for u in upd]

    outputs = [loss, grad_x]
    for kind in range(4):
        norms, bias_k, final_k = small_out[kind]
        outputs += [norms[0], big["up1"][kind], big["down1"][kind], norms[1], big["win"][kind], bias_k,
                    big["wa"][kind], big["wb"][kind], big["wo"][kind], norms[2], big["up2"][kind],
                    big["down2"][kind], final_k]
    return tuple(outputs)
```

```python
import numpy as np

import jax
import jax.numpy as jnp
from jax import lax
from jax.experimental import pallas as pl
from jax.experimental.pallas import tpu as pltpu

F32 = jnp.float32
BF16 = jnp.bfloat16
SDS = jax.ShapeDtypeStruct
BS = pl.BlockSpec
MESH = pl.DeviceIdType.MESH

D = 1024
S = 2048
F = 2816
DEPTH = 2
HEAD_DIM = 64
DILATIONS = (1, 4, 16)
HALF = 64
QKV_A = 2304
QKV_B = 1536
IN_W = 5888
N_DEV = 8
NA_ROWS = 32
GRID_W = 64
NA_KR = 8
ROPE_THETA = 10000.0
RMS_EPS = 1e-6
NEG = -1e30
SCALE = HEAD_DIM ** -0.5
ADAM_LR, ADAM_B1, ADAM_B2, ADAM_EPS, ADAM_WD, ADAM_STEP = 0.001, 0.9, 0.999, 1e-08, 0.01, 10
VMEM_LIMIT_V7X = 52 * 1024 * 1024
SMALL_ROWS = 120
BIAS_PAD = 3840
NA_FWD_ROWS = 8
NA_BWD_ROWS = 4
DIL_FWD_TILES = 8
DIL_BWD_TILES = 4


def _cp(*sem):
    return pltpu.CompilerParams(dimension_semantics=sem, vmem_limit_bytes=VMEM_LIMIT_V7X)


def _dot_nn(a, b):
    return jnp.dot(a, b, preferred_element_type=F32)


def _dot_nt(a, b):
    return lax.dot_general(a, b, (((1,), (1,)), ((), ())), preferred_element_type=F32)


def _dot_tn(a, b):
    return lax.dot_general(a, b, (((0,), (0,)), ((), ())), preferred_element_type=F32)


def _ds(start, size, stride):
    return pl.ds(start, size) if stride == 1 else pl.ds(start, size, stride=stride)


def _norm_fwd(x, g, tag):
    t = x.shape[0]
    tm = 512

    def body(x_ref, g_ref, o_ref):
        xv = x_ref[...]
        r = lax.rsqrt(jnp.mean(xv * xv, axis=-1, keepdims=True) + RMS_EPS)
        o_ref[...] = (xv * r * g_ref[...]).astype(BF16)

    return pl.pallas_call(
        body, name=f"norm_fwd_{tag}", grid=(t // tm,),
        in_specs=[BS((tm, D), lambda i: (i, 0)), BS((1, D), lambda i: (0, 0))],
        out_specs=BS((tm, D), lambda i: (i, 0)),
        out_shape=SDS((t, D), BF16), compiler_params=_cp("parallel"),
    )(x, g.reshape(1, D))


def _loss_head(x, g, tgt):
    t = x.shape[0]
    tm = 1024

    def body(x_ref, g_ref, t_ref, loss_ref, dx_ref, dxb_ref, dg_ref):
        @pl.when(pl.program_id(0) == 0)
        def _():
            dg_ref[...] = jnp.zeros_like(dg_ref)
            loss_ref[...] = jnp.zeros_like(loss_ref)

        xv = x_ref[...]
        gv = g_ref[...]
        r = lax.rsqrt(jnp.mean(xv * xv, axis=-1, keepdims=True) + RMS_EPS)
        xh = xv * r
        e = xh * gv - t_ref[...]
        loss_ref[...] += 0.5 * jnp.sum(jnp.mean(e * e, axis=-1, keepdims=True), axis=0, keepdims=True)
        dy = e * (1.0 / D)
        u = dy * gv
        dx = r * (u - xh * jnp.mean(xh * u, axis=-1, keepdims=True))
        dx_ref[...] = dx
        dxb_ref[...] = dx.astype(BF16)
        dg_ref[...] += jnp.sum(dy * xh, axis=0, keepdims=True)

    row = BS((tm, D), lambda i: (i, 0))
    vec = BS((1, D), lambda i: (0, 0))
    return pl.pallas_call(
        body, name="loss_head", grid=(t // tm,),
        in_specs=[row, vec, row], out_specs=[BS((1, 128), lambda i: (0, 0)), row, row, vec],
        out_shape=[SDS((1, 128), F32), SDS((t, D), F32), SDS((t, D), BF16), SDS((1, D), F32)],
        compiler_params=_cp("arbitrary"),
    )(x, g.reshape(1, D), tgt)


def _mm_nn(a, w, tag, res=None, scale=1.0, tm=1024, tn=None, next_g=None):
    c_n, t, k = a.shape
    n = w.shape[2]
    tn = n if tn is None else tn
    assert next_g is None or tn == n
    n_in = 2 + (res is not None) + (next_g is not None)

    def body(*refs):
        a_ref, w_ref = refs[0], refs[1]
        acc = _dot_nn(a_ref[0].astype(BF16), w_ref[0])
        for c in range(1, c_n):
            acc = acc + _dot_nn(a_ref[c].astype(BF16), w_ref[c])
        if scale != 1.0:
            acc = acc * scale
        if res is not None:
            acc = refs[2][...] + acc
        refs[n_in][...] = acc
        if next_g is not None:
            r = lax.rsqrt(jnp.mean(acc * acc, axis=-1, keepdims=True) + RMS_EPS)
            refs[n_in + 1][...] = (acc * r * refs[n_in - 1][...]).astype(BF16)

    w_mode = dict(pipeline_mode=pl.Buffered(1)) if tn == n else {}
    in_specs = [BS((c_n, tm, k), lambda i, j: (0, i, 0)), BS((c_n, k, tn), lambda i, j: (0, 0, j), **w_mode)]
    args = [a, w]
    out_specs = [BS((tm, tn), lambda i, j: (i, j))]
    out_shape = [SDS((t, n), F32)]
    if res is not None:
        in_specs.append(BS((tm, tn), lambda i, j: (i, j)))
        args.append(res)
    if next_g is not None:
        in_specs.append(BS((1, n), lambda i, j: (0, 0)))
        args.append(next_g.reshape(1, n))
        out_specs.append(BS((tm, tn), lambda i, j: (i, j)))
        out_shape.append(SDS((t, n), BF16))
    got = pl.pallas_call(
        body, name=f"mm_nn_{tag}", grid=(t // tm, n // tn), in_specs=in_specs, out_specs=out_specs,
        out_shape=out_shape, compiler_params=_cp("parallel", "parallel"),
    )(*args)
    return got if next_g is not None else got[0]


def _mm_nn_norm_bwd(parts, w, x, g, dres, tag, tm=256, single_w=False):
    t = parts[0].shape[1]
    n_parts = len(parts)

    def body(*refs):
        w_ref, x_ref, g_ref, dr_ref, dx_ref, dxb_ref, dg_ref = refs[n_parts:]

        @pl.when(pl.program_id(0) == 0)
        def _():
            dg_ref[...] = jnp.zeros_like(dg_ref)

        dh = None
        row = 0
        for a_ref, part in zip(refs, parts):
            for c in range(part.shape[0]):
                term = _dot_nn(a_ref[c].astype(BF16), w_ref[row:row + part.shape[2], :])
                dh = term if dh is None else dh + term
                row += part.shape[2]
        xv = x_ref[...]
        r = lax.rsqrt(jnp.mean(xv * xv, axis=-1, keepdims=True) + RMS_EPS)
        xh = xv * r
        u = dh * g_ref[...]
        dx = dr_ref[...] + r * (u - xh * jnp.mean(xh * u, axis=-1, keepdims=True))
        dx_ref[...] = dx
        dxb_ref[...] = dx.astype(BF16)
        dg_ref[...] += jnp.sum(dh * xh, axis=0, keepdims=True)

    row = BS((tm, D), lambda i: (i, 0))
    vec = BS((1, D), lambda i: (0, 0))
    return pl.pallas_call(
        body, name=f"mm_nn_norm_bwd_{tag}", grid=(t // tm,),
        in_specs=[BS((p.shape[0], tm, p.shape[2]), lambda i: (0, i, 0)) for p in parts]
        + [BS(w.shape, lambda i: (0, 0), **(dict(pipeline_mode=pl.Buffered(1)) if single_w else {})), row, vec, row],
        out_specs=[row, row, vec], out_shape=[SDS((t, D), F32), SDS((t, D), BF16), SDS((1, D), F32)],
        compiler_params=_cp("arbitrary"),
    )(*parts, w, x, g.reshape(1, D), dres)


def _mm_nt_rows(a, w, tag, tm, tn, n_total, w_row0, rope=None):
    t, k = a.shape
    assert w_row0 % tn == 0 and n_total % tn == 0
    j0 = w_row0 // tn

    def body(a_ref, w_ref, *rest):
        o_ref = rest[-1]
        o_ref[...] = _dot_nt(a_ref[...].astype(BF16), w_ref[...])
        if rope is not None:
            @pl.when(pl.program_id(0) == 0)
            def _():
                c = rest[0][...]
                sg = rest[1][...]
                first = (lax.broadcasted_iota(jnp.int32, (tm, 128), 1) % HEAD_DIM) < HEAD_DIM // 2
                for col in range(0, rope[2], 128):
                    v = o_ref[:, col:col + 128]
                    o_ref[:, col:col + 128] = v * c + _swap_halves(v, first) * sg

    in_specs = [BS((tm, k), lambda j, i: (i, 0)), BS((tn, k), lambda j, i: (j0 + j, 0))]
    args = [a, w]
    if rope is not None:
        assert rope[2] <= tn
        in_specs += [BS((tm, 128), lambda j, i: (i % (S // tm), 0))] * 2
        args += [rope[0], rope[1]]
    return pl.pallas_call(
        body, name=f"mm_nt_{tag}", grid=(n_total // tn, t // tm), in_specs=in_specs,
        out_specs=BS((tm, tn), lambda j, i: (i, j)), out_shape=SDS((t, n_total), F32),
        compiler_params=_cp("parallel", "parallel"),
    )(*args)


def _mm_tn(a, b, tag, scale=1.0, tmm=None, into=None, row0=0, rows=None):
    c_n, t, m = a.shape
    n = b.shape[1]
    if tmm is None:
        tmm = max(w for w in (1408, 768, 512, 256) if m % w == 0 and row0 % w == 0)
    tiles = m // tmm
    block0 = row0 // tmm
    assert row0 % tmm == 0 and m % tmm == 0

    def body(a_ref, b_ref, *rest):
        rest[-1][...] = (_dot_tn(a_ref[...].astype(BF16), b_ref[...].astype(BF16)) * scale).astype(BF16)

    in_specs = [BS((None, t, tmm), lambda c, mi: (c, 0, mi)), BS((t, n), lambda c, mi: (0, 0))]
    args = [a, b]
    if into is not None:
        in_specs.append(BS(memory_space=pl.ANY))
        args.append(into)
    return pl.pallas_call(
        body, name=f"mm_tn_{tag}", grid=(c_n, tiles), in_specs=in_specs,
        out_specs=BS((tmm, n), lambda c, mi: (block0 + c * tiles + mi, 0)),
        out_shape=SDS((rows or c_n * m, n) if into is None else into.shape, BF16),
        input_output_aliases={} if into is None else {2: 0},
        compiler_params=_cp("parallel", "parallel"),
    )(*args)


def _ffn_up(hn, wut, tag):
    t = hn.shape[0]
    tm, tn = 1024, 1408

    def body(h_ref, w_ref, gu_ref, act_ref):
        h = h_ref[...]
        g = _dot_nt(h, w_ref[0])
        u = _dot_nt(h, w_ref[1])
        sg = jax.nn.sigmoid(g)
        silu = g * sg
        gu_ref[0] = (u * (sg + silu * (1.0 - sg))).astype(BF16)
        gu_ref[1] = silu.astype(BF16)
        act_ref[...] = (silu * u).astype(BF16)

    return pl.pallas_call(
        body, name=f"ffn_up_{tag}", grid=(F // tn, t // tm),
        in_specs=[BS((tm, D), lambda j, i: (i, 0)), BS((2, tn, D), lambda j, i: (0, j, 0))],
        out_specs=[BS((2, tm, tn), lambda j, i: (0, i, j)), BS((tm, tn), lambda j, i: (i, j))],
        out_shape=[SDS((2, t, F), BF16), SDS((t, F), BF16)],
        compiler_params=_cp("parallel", "parallel"),
    )(hn, wut)


def _ffn_dact(dxo, wd, gu, tie, tag):
    t = dxo.shape[0]
    tm, tn = 1024, 1408

    def body(d_ref, w_ref, gu_ref, tie_ref, o_ref):
        dact = _dot_nt(d_ref[...] * 0.5, w_ref[...])
        o_ref[0] = (dact * gu_ref[0].astype(F32)).astype(BF16)
        o_ref[1] = (dact * gu_ref[1].astype(F32)).astype(BF16)

    return pl.pallas_call(
        body, name=f"ffn_dact_{tag}", grid=(F // tn, t // tm),
        in_specs=[BS((tm, D), lambda j, i: (i, 0)), BS((tn, D), lambda j, i: (j, 0)),
                  BS((2, tm, tn), lambda j, i: (0, i, j)), BS((8, 128), lambda j, i: (0, 0))],
        out_specs=BS((2, tm, tn), lambda j, i: (0, i, j)),
        out_shape=SDS((2, t, F), BF16), compiler_params=_cp("parallel", "parallel"),
    )(dxo, wd, gu, tie)


def _rope_tables():
    half = HEAD_DIM // 2
    inv_freq = ROPE_THETA ** (-jnp.arange(half, dtype=F32) / half)
    ang = jnp.arange(S).astype(F32)[:, None] * inv_freq[None, :]
    cos, sin = jnp.cos(ang), jnp.sin(ang)
    return jnp.concatenate([cos, cos, cos, cos], axis=1), jnp.concatenate([-sin, sin, -sin, sin], axis=1)


def _swap_halves(t, first_half):
    return jnp.where(first_half, pltpu.roll(t, 96, 1), pltpu.roll(t, 32, 1))


def _rope_bwd(dqs, dks, dvs, cos_t, sin_t):
    t = dqs[0].shape[0]
    tm = 1024

    def body(*refs):
        c = refs[9][...]
        sg = refs[10][...]
        o_ref = refs[11]
        first = (lax.broadcasted_iota(jnp.int32, (tm, 128), 1) % HEAD_DIM) < HEAD_DIM // 2
        for a in range(6):
            for hp in range(2):
                v = refs[a][:, 128 * hp:128 * (hp + 1)]
                col = 128 * (2 * a + hp)
                o_ref[:, col:col + 128] = (v * c + _swap_halves(v * sg, first)).astype(BF16)
        for a in range(6, 9):
            o_ref[:, 256 * a:256 * (a + 1)] = refs[a][...].astype(BF16)

    blk = BS((tm, 256), lambda i: (i, 0))
    tab = BS((tm, 128), lambda i: (i % (S // tm), 0))
    return pl.pallas_call(
        body, name="rope_bwd", grid=(t // tm,), in_specs=[blk] * 9 + [tab, tab],
        out_specs=BS((None, tm, QKV_A), lambda i: (0, i, 0)), out_shape=SDS((1, t, QKV_A), BF16),
        compiler_params=_cp("parallel"),
    )(*dqs, *dks, *dvs, cos_t, sin_t)


def _head_masks():
    lane = lax.broadcasted_iota(jnp.int32, (1, 128), 1)
    m0 = (lane < HEAD_DIM).astype(F32)
    return m0, 1.0 - m0


def _dil_geometry(d):
    sub = S // d
    q_rows = 128
    k_rows = min(256, sub)
    return sub, q_rows, sub // q_rows, k_rows


def _dil_tile(idx, d, keys_on_rows=False):
    sub, q_rows, nb, k_rows = _dil_geometry(d)
    r = idx // nb
    n = idx % nb
    k_sub = jnp.clip(q_rows * n - HALF, 0, sub - k_rows)
    if d == 1:
        q_start = pl.multiple_of(q_rows * n, q_rows)
        k_start = pl.multiple_of(k_sub, HALF)
    else:
        q_start = q_rows * n * d + r
        k_start = k_sub * d + r
    if keys_on_rows:
        ii = lax.broadcasted_iota(jnp.int32, (k_rows, 2 * q_rows), 1) % q_rows
        jj = lax.broadcasted_iota(jnp.int32, (k_rows, 2 * q_rows), 0)
    else:
        ii = lax.broadcasted_iota(jnp.int32, (q_rows, k_rows), 0)
        jj = lax.broadcasted_iota(jnp.int32, (q_rows, k_rows), 1)
    valid = jnp.abs(jj - ii + (k_sub - q_rows * n)) <= HALF
    return q_start, k_start, valid


def _dil_specs(grp):
    qs = BS((S, 128), lambda b, hp: (b, 2 * grp + hp))
    ks = BS((S, 128), lambda b, hp: (b, 6 + 2 * grp + hp))
    vs = BS((S, 128), lambda b, hp: (b, 12 + 2 * grp + hp))
    own = BS((S, 128), lambda b, hp: (b, hp))
    return qs, ks, vs, own


def _dil_fwd(qkr, proj, grp):
    t = qkr.shape[0]
    d = DILATIONS[grp]
    _, q_rows, nb, k_rows = _dil_geometry(d)

    def body(q_ref, k_ref, v_ref, o_ref, l_ref):
        masks = _head_masks()

        def step(i0, carry):
            geo = [_dil_tile(i0 * DIL_FWD_TILES + j, d) for j in range(DIL_FWD_TILES)]
            tiles = [(j, h) for j in range(DIL_FWD_TILES) for h in range(2)]
            qs = [q_ref[_ds(g[0], q_rows, d), :] for g in geo]
            kbs = [k_ref[_ds(g[1], k_rows, d), :].astype(BF16) for g in geo]
            ss = [jnp.where(geo[j][2], _dot_nt((qs[j] * masks[h]).astype(BF16), kbs[j]) * SCALE, NEG) for j, h in tiles]
            mxs = [jnp.max(s, axis=1, keepdims=True) for s in ss]
            ps = [jnp.exp(s - mx) for s, mx in zip(ss, mxs)]
            dens = [jnp.sum(p, axis=1, keepdims=True) for p in ps]
            vs = [v_ref[_ds(g[1], k_rows, d), :] for g in geo]
            outs = [_dot_nn(p.astype(BF16), (vs[j] * masks[h]).astype(BF16)) / den
                    for p, den, (j, h) in zip(ps, dens, tiles)]
            for j, g in enumerate(geo):
                o_ref[_ds(g[0], q_rows, d), :] = outs[2 * j] + outs[2 * j + 1]
                l_ref[_ds(g[0], q_rows, d), :] = (
                    (mxs[2 * j] + jnp.log(dens[2 * j])) * masks[0] + (mxs[2 * j + 1] + jnp.log(dens[2 * j + 1])) * masks[1])
            return carry

        lax.fori_loop(0, d * nb // DIL_FWD_TILES, step, 0)

    qs, ks, vs, own = _dil_specs(grp)
    return pl.pallas_call(
        body, name=f"dil_fwd_{grp}", grid=(t // S, 2), in_specs=[qs, ks, vs], out_specs=[own, own],
        out_shape=[SDS((t, 256), F32), SDS((t, 256), F32)], compiler_params=_cp("parallel", "parallel"),
    )(qkr, qkr, proj)


def _dil_bwd(qkr, proj, do, dlp, lse, grp):
    t = qkr.shape[0]
    d = DILATIONS[grp]
    _, q_rows, nb, k_rows = _dil_geometry(d)

    def body(q_ref, k_ref, v_ref, do_ref, dl_ref, l_ref, dq_ref, dk_ref, dv_ref):
        masks = _head_masks()
        dk_ref[...] = jnp.zeros_like(dk_ref)
        dv_ref[...] = jnp.zeros_like(dv_ref)

        def as_row(x2):
            xt = x2.T
            return jnp.concatenate([xt[0:1], xt[HEAD_DIM:HEAD_DIM + 1]], axis=1)

        def step(i0, carry):
            geo = [_dil_tile(i0 * DIL_BWD_TILES + j, d, keys_on_rows=True) for j in range(DIL_BWD_TILES)]
            q_ds = [_ds(g[0], q_rows, d) for g in geo]
            k_ds = [_ds(g[1], k_rows, d) for g in geo]
            qbs = [_both_heads(q_ref[r, :], masks).astype(BF16) for r in q_ds]
            kbs = [k_ref[r, :].astype(BF16) for r in k_ds]
            vbs = [v_ref[r, :].astype(BF16) for r in k_ds]
            dobs = [_both_heads(do_ref[r, :], masks).astype(BF16) for r in q_ds]
            l_rows = [as_row(l_ref[r, :]) for r in q_ds]
            dl_rows = [as_row(dl_ref[r, :]) for r in q_ds]
            ss = [jnp.where(g[2], _dot_nt(kb, qb) * SCALE, NEG) for g, kb, qb in zip(geo, kbs, qbs)]
            ps = [jnp.exp(s - lr) for s, lr in zip(ss, l_rows)]
            dps = [_dot_nt(vb, dob) for vb, dob in zip(vbs, dobs)]
            dss = [(p * (dp - dr)).astype(BF16) for p, dp, dr in zip(ps, dps, dl_rows)]
            dks = [_dot_nn(ds, qb) for ds, qb in zip(dss, qbs)]
            dvs = [_dot_nn(p.astype(BF16), dob) for p, dob in zip(ps, dobs)]
            dqs = [_own_heads(_dot_tn(ds, kb), masks) for ds, kb in zip(dss, kbs)]
            for j in range(DIL_BWD_TILES):
                dq_ref[q_ds[j], :] = dqs[j] * SCALE
                dk_ref[k_ds[j], :] += dks[j] * SCALE
                dv_ref[k_ds[j], :] += dvs[j]
            return carry

        lax.fori_loop(0, d * nb // DIL_BWD_TILES, step, 0)

    qs, ks, vs, own = _dil_specs(grp)
    return pl.pallas_call(
        body, name=f"dil_bwd_{grp}", grid=(t // S, 2), in_specs=[qs, ks, vs, own, own, own],
        out_specs=[own, own, own], out_shape=[SDS((t, 256), F32)] * 3,
        compiler_params=_cp("parallel", "parallel"),
    )(qkr, qkr, proj, do, dlp, lse)


def _mix_weights(l0, l1, l2):
    mx = jnp.maximum(jnp.maximum(l0, l1), l2)
    e0, e1, e2 = jnp.exp(l0 - mx), jnp.exp(l1 - mx), jnp.exp(l2 - mx)
    den = e0 + e1 + e2
    return e0 / den, e1 / den, e2 / den


def _combine_fwd(outs, lses):
    t = outs[0].shape[0]
    tm = 1024

    def body(o0, o1, o2, l0, l1, l2, y_ref):
        w0, w1, w2 = _mix_weights(l0[...], l1[...], l2[...])
        y_ref[...] = w0 * o0[...] + w1 * o1[...] + w2 * o2[...]

    blk = BS((tm, 256), lambda i: (i, 0))
    return pl.pallas_call(
        body, name="combine_fwd", grid=(t // tm,), in_specs=[blk] * 6, out_specs=blk,
        out_shape=SDS((t, 256), F32), compiler_params=_cp("parallel"),
    )(*outs, *lses)


def _head_sum(x):
    a = lax.broadcasted_iota(jnp.int32, (256, 256), 0) // HEAD_DIM
    b = lax.broadcasted_iota(jnp.int32, (256, 256), 1) // HEAD_DIM
    ones = (a == b).astype(BF16)
    hi = x.astype(BF16)
    lo = (x - hi.astype(F32)).astype(BF16)
    return _dot_nn(hi, ones) + _dot_nn(lo, ones)


def _combine_bwd(dya, outs, lses):
    t = dya.shape[0]
    tm = 1024

    def body(dy_ref, o0, o1, o2, l0, l1, l2, d0, d1, d2, e0, e1, e2):
        ws = _mix_weights(l0[...], l1[...], l2[...])
        dy = dy_ref[...]
        ya = ws[0] * o0[...] + ws[1] * o1[...] + ws[2] * o2[...]
        hs = _head_sum(dy * ya)
        for w, d_ref, e_ref in zip(ws, (d0, d1, d2), (e0, e1, e2)):
            d_ref[...] = w * dy
            e_ref[...] = w * hs

    blk = BS((tm, 256), lambda i: (i, 0))
    return pl.pallas_call(
        body, name="combine_bwd", grid=(t // tm,), in_specs=[blk] * 7, out_specs=[blk] * 6,
        out_shape=[SDS((t, 256), F32)] * 6, compiler_params=_cp("parallel"),
    )(dya, *outs, *lses)


def _na_bias_table(rel_bias):
    kw = NA_KR * GRID_W
    rev = jnp.pad(rel_bias.astype(F32)[:, :, ::-1], ((0, 0), (0, 0), (0, 128 - 31)))

    def body(r_ref, o_ref):
        lane = lax.broadcasted_iota(jnp.int32, (GRID_W, 128), 1)
        j = lax.broadcasted_iota(jnp.int32, (GRID_W, 128), 0)
        q = lane % GRID_W
        win_lo = jnp.clip(q - 8, 0, GRID_W - 16)
        valid = (j >= win_lo) & (j < win_lo + 16)
        for cls in range(NA_KR):
            for k in range(NA_KR):
                tiles = []
                for h in range(2):
                    row = jnp.broadcast_to(r_ref[h, cls + k:cls + k + 1, :], (GRID_W, 128))
                    tiles.append(pltpu.roll(row, (128 - 15 + GRID_W * h) % 128, 1, stride=1, stride_axis=0))
                o_ref[cls, GRID_W * k:GRID_W * (k + 1), :] = jnp.where(
                    valid, jnp.where(lane < GRID_W, tiles[0], tiles[1]), NEG)

    return pl.pallas_call(
        body, name="na_bias_table", grid=(4,),
        in_specs=[BS((2, 2 * NA_KR - 1, 128), lambda hp: (hp, 0, 0))],
        out_specs=BS((None, NA_KR, kw, 128), lambda hp: (hp, 0, 0, 0)),
        out_shape=SDS((4, NA_KR, kw, 128), F32), compiler_params=_cp("parallel"),
    )(rev)


def _na_row(i):
    lo = jnp.clip(i - NA_KR // 2, 0, NA_ROWS - NA_KR)
    return pl.multiple_of(GRID_W * i, GRID_W), pl.multiple_of(GRID_W * lo, GRID_W), lo - i + NA_KR - 1


def _both_heads(x, masks):
    return jnp.concatenate([x * masks[0], x * masks[1]], axis=0)


def _own_heads(r, masks):
    half = r.shape[0] // 2
    return r[:half] * masks[0] + r[half:] * masks[1]


def _na_fwd(proj, bias):
    t = proj.shape[0]
    kw = NA_KR * GRID_W

    def body(q_ref, k_ref, v_ref, b_ref, o_ref, l_ref):
        masks = _head_masks()

        def step(i0, carry):
            idx = [i0 * NA_FWD_ROWS + j for j in range(NA_FWD_ROWS)]
            rows = [_na_row(i) for i in idx]
            qbs = [_both_heads(q_ref[pl.ds(r[0], GRID_W), :], masks).astype(BF16) for r in rows]
            kbs = [k_ref[pl.ds(r[1], kw), :].astype(BF16) for r in rows]
            ss = [_dot_nt(kb, qb) * SCALE + b_ref[r[2]] for kb, qb, r in zip(kbs, qbs, rows)]
            mxs = [jnp.max(s, axis=0, keepdims=True) for s in ss]
            ps = [jnp.exp(s - mx) for s, mx in zip(ss, mxs)]
            dens = [jnp.sum(p, axis=0, keepdims=True) for p in ps]
            pbs = [(p / den).astype(BF16) for p, den in zip(ps, dens)]
            vbs = [v_ref[pl.ds(r[1], kw), :].astype(BF16) for r in rows]
            outs = [_own_heads(_dot_tn(pb, vb), masks) for pb, vb in zip(pbs, vbs)]
            for j, r in enumerate(rows):
                o_ref[pl.ds(r[0], GRID_W), :] = outs[j]
                l_ref[pl.ds(idx[j], 1), :] = mxs[j] + jnp.log(dens[j])
            return carry

        lax.fori_loop(0, NA_ROWS // NA_FWD_ROWS, step, 0)

    c0 = QKV_A // 128
    return pl.pallas_call(
        body, name="na_fwd", grid=(t // S, 4),
        in_specs=[BS((S, 128), lambda b, hp: (b, c0 + hp)), BS((S, 128), lambda b, hp: (b, c0 + 4 + hp)),
                  BS((S, 128), lambda b, hp: (b, c0 + 8 + hp)),
                  BS((None, NA_KR, kw, 128), lambda b, hp: (hp, 0, 0, 0))],
        out_specs=[BS((S, 128), lambda b, hp: (b, hp)), BS((None, None, NA_ROWS, 128), lambda b, hp: (b, hp, 0, 0))],
        out_shape=[SDS((t, 512), F32), SDS((t // S, 4, NA_ROWS, 128), F32)],
        compiler_params=_cp("parallel", "parallel"),
    )(proj, proj, proj, bias)


def _na_bwd(proj, bias, dyb, yb, lse):
    t = proj.shape[0]
    kw = NA_KR * GRID_W

    def body(q_ref, k_ref, v_ref, b_ref, do_ref, o_ref, l_ref, d_ref, db_ref):
        masks = _head_masks()
        ones = jnp.ones((8, 128), BF16)

        @pl.when(pl.program_id(1) == 0)
        def _():
            db_ref[...] = jnp.zeros_like(db_ref)

        d_ref[1:3] = jnp.zeros((2, S, 128), F32)

        def row_sums(x):
            hi = x.astype(BF16)
            lo = (x - hi.astype(F32)).astype(BF16)
            return (_dot_nt(ones, hi) + _dot_nt(ones, lo))[0:1]

        def step(i0, carry):
            idx = [i0 * NA_BWD_ROWS + j for j in range(NA_BWD_ROWS)]
            rows = [_na_row(i) for i in idx]
            q_ds = [pl.ds(r[0], GRID_W) for r in rows]
            k_ds = [pl.ds(r[1], kw) for r in rows]
            qbs = [_both_heads(q_ref[r, :], masks).astype(BF16) for r in q_ds]
            kbs = [k_ref[r, :].astype(BF16) for r in k_ds]
            vbs = [v_ref[r, :].astype(BF16) for r in k_ds]
            dos = [do_ref[r, :] for r in q_ds]
            dobs = [_both_heads(do, masks).astype(BF16) for do in dos]
            deltas = [row_sums(_both_heads(do * o_ref[r, :], masks)) for do, r in zip(dos, q_ds)]
            ss = [_dot_nt(kb, qb) * SCALE + b_ref[r[2]] for kb, qb, r in zip(kbs, qbs, rows)]
            ps = [jnp.exp(s - l_ref[pl.ds(i, 1), :]) for s, i in zip(ss, idx)]
            dps = [_dot_nt(vb, dob) for vb, dob in zip(vbs, dobs)]
            dss = [p * (dp - delta) for p, dp, delta in zip(ps, dps, deltas)]
            for ds, r in zip(dss, rows):
                db_ref[r[2]] += ds
            dsbs = [ds.astype(BF16) for ds in dss]
            dks = [_dot_nn(dsb, qb) for dsb, qb in zip(dsbs, qbs)]
            dvs = [_dot_nn(p.astype(BF16), dob) for p, dob in zip(ps, dobs)]
            dqs = [_own_heads(_dot_tn(dsb, kb), masks) for dsb, kb in zip(dsbs, kbs)]
            for j in range(NA_BWD_ROWS):
                d_ref[0, q_ds[j], :] = dqs[j] * SCALE
                d_ref[1, k_ds[j], :] += dks[j] * SCALE
                d_ref[2, k_ds[j], :] += dvs[j]
            return carry

        lax.fori_loop(0, NA_ROWS // NA_BWD_ROWS, step, 0)

    c0 = QKV_A // 128
    own = BS((S, 128), lambda hp, b: (b, hp))
    tab = BS((None, NA_KR, kw, 128), lambda hp, b: (hp, 0, 0, 0))
    return pl.pallas_call(
        body, name="na_bwd", grid=(4, t // S),
        in_specs=[BS((S, 128), lambda hp, b: (b, c0 + hp)), BS((S, 128), lambda hp, b: (b, c0 + 4 + hp)),
                  BS((S, 128), lambda hp, b: (b, c0 + 8 + hp)), tab, own, own,
                  BS((None, None, NA_ROWS, 128), lambda hp, b: (b, hp, 0, 0))],
        out_specs=[BS((3, S, 128), lambda hp, b: (0, b, hp)), tab],
        out_shape=[SDS((3, t, 512), F32), SDS((4, NA_KR, kw, 128), F32)],
        compiler_params=_cp("parallel", "arbitrary"),
    )(proj, proj, proj, bias, dyb, yb, lse)


def _na_dbias_lane_map():
    kw = NA_KR * GRID_W
    lane = np.arange(kw)
    blk, m = lane // GRID_W, lane % GRID_W
    target = np.full(kw, -1)
    target[m < 16] = (blk * 32 + 15 + m)[m < 16]
    target[m >= 49] = (((blk + 1) % NA_KR) * 32 + m - 49)[m >= 49]
    return jnp.asarray(target[:, None] == np.arange(kw)[None, :], BF16)


def _na_dbias(db):
    kw = NA_KR * GRID_W

    def body(x_ref, map_ref, o_ref, z_ref):
        for cls in range(NA_KR):
            xt = x_ref[cls].T
            for h in range(2):
                xv = xt[GRID_W * h:GRID_W * (h + 1)]
                y = xv[0:8]
                for g in range(1, GRID_W // 8):
                    y = y + pltpu.roll(xv[8 * g:8 * g + 8], kw - 8 * g, 1)
                d = y[0:1]
                for s in range(1, 8):
                    d = d + pltpu.roll(y[s:s + 1], kw - s, 1)
                z_ref[h, cls:cls + 1, :] = d
        for h in range(2):
            z = z_ref[h]
            hi = z.astype(BF16)
            lo = (z - hi.astype(F32)).astype(BF16)
            e = _dot_nn(hi, map_ref[...]) + _dot_nn(lo, map_ref[...])
            out = e[0:1]
            for cls in range(1, NA_KR):
                out = out + pltpu.roll(e[cls:cls + 1], 32 * cls, 1)
            o_ref[h] = jnp.broadcast_to(out, (8, kw))

    return pl.pallas_call(
        body, name="na_dbias", grid=(4,),
        in_specs=[BS((None, NA_KR, kw, 128), lambda hp: (hp, 0, 0, 0)), BS((kw, kw), lambda hp: (0, 0))],
        out_specs=BS((2, 8, kw), lambda hp: (hp, 0, 0)), out_shape=SDS((8, 8, kw), F32),
        scratch_shapes=[pltpu.VMEM((2, 8, kw), F32)], compiler_params=_cp("parallel"),
    )(db, _na_dbias_lane_map())


def _merge_fwd(ya, yb, proj, wat, wbt):
    t = ya.shape[0]
    tm, tn = 2048, 256
    ca = (QKV_A + QKV_B) // tn
    cb = ca + D // tn

    def body(ya_ref, yb_ref, la_ref, lb_ref, wa_ref, wb_ref, m_ref, za_ref, zb_ref):
        za = _dot_nt(ya_ref[...].astype(BF16), wa_ref[...])
        zb = _dot_nt(yb_ref[...].astype(BF16), wb_ref[...])
        m_ref[...] = (jax.nn.sigmoid(la_ref[...]) * za + jax.nn.sigmoid(lb_ref[...]) * zb).astype(BF16)
        za_ref[...] = za.astype(BF16)
        zb_ref[...] = zb.astype(BF16)

    out = BS((tm, tn), lambda i, j: (i, j))
    return pl.pallas_call(
        body, name="merge_fwd", grid=(t // tm, D // tn),
        in_specs=[BS((tm, 256), lambda i, j: (i, 0)), BS((tm, 512), lambda i, j: (i, 0)),
                  BS((tm, tn), lambda i, j: (i, ca + j)), BS((tm, tn), lambda i, j: (i, cb + j)),
                  BS((tn, 256), lambda i, j: (j, 0)), BS((tn, 512), lambda i, j: (j, 0))],
        out_specs=[out, out, out], out_shape=[SDS((t, D), BF16)] * 3,
        compiler_params=_cp("parallel", "parallel"),
    )(ya, yb, proj, proj, wat, wbt)


def _merge_bwd(dxo, wo, za, zb, proj):
    t = dxo.shape[0]
    tm, tn = 2048, 256
    ca = (QKV_A + QKV_B) // tn
    cb = ca + D // tn

    def body(d_ref, w_ref, za_ref, zb_ref, la_ref, lb_ref, dza_ref, dzb_ref, dl_ref):
        dmv = _dot_nt(d_ref[...], w_ref[...])
        ga = jax.nn.sigmoid(la_ref[...])
        gb = jax.nn.sigmoid(lb_ref[...])
        dza_ref[...] = (dmv * ga).astype(BF16)
        dzb_ref[...] = (dmv * gb).astype(BF16)
        dl_ref[0] = (dmv * za_ref[...].astype(F32) * ga * (1.0 - ga)).astype(BF16)
        dl_ref[1] = (dmv * zb_ref[...].astype(F32) * gb * (1.0 - gb)).astype(BF16)

    blk = BS((tm, tn), lambda i, j: (i, j))
    return pl.pallas_call(
        body, name="merge_bwd", grid=(t // tm, D // tn),
        in_specs=[BS((tm, D), lambda i, j: (i, 0)), BS((tn, D), lambda i, j: (j, 0)), blk, blk,
                  BS((tm, tn), lambda i, j: (i, ca + j)), BS((tm, tn), lambda i, j: (i, cb + j))],
        out_specs=[blk, blk, BS((2, tm, tn), lambda i, j: (0, i, j))],
        out_shape=[SDS((t, D), BF16), SDS((t, D), BF16), SDS((2, t, D), BF16)],
        compiler_params=_cp("parallel", "parallel"),
    )(dxo, wo, za, zb, proj, proj)


def _adamw_update(w, g, m, v):
    mn = ADAM_B1 * m + (1.0 - ADAM_B1) * g
    vn = ADAM_B2 * v + (1.0 - ADAM_B2) * (g * g)
    m_hat = mn / (1.0 - ADAM_B1 ** ADAM_STEP)
    v_hat = vn / (1.0 - ADAM_B2 ** ADAM_STEP)
    return -ADAM_LR * (m_hat / (jnp.sqrt(v_hat) + ADAM_EPS) + ADAM_WD * w), mn, vn


def _sum_adamw(recv0, recv1, w, m, v, tag):
    _, r, c = recv0.shape
    tr = max(rows for rows in range(16, r + 1, 16) if r % rows == 0 and rows * c <= 384 * 1024)

    def body(a_ref, b_ref, w_ref, m_ref, v_ref, g_ref, d_ref, mo_ref, vo_ref):
        def update(ref):
            g = ref[0].astype(F32)
            for s in range(1, N_DEV):
                g = g + ref[s].astype(F32)
            g_ref[...] = g
            d_ref[...], mo_ref[...], vo_ref[...] = _adamw_update(w_ref[...], g, m_ref[...], v_ref[...])

        pl.when(pl.program_id(0) == 0)(lambda: update(a_ref))
        pl.when(pl.program_id(0) == 1)(lambda: update(b_ref))

    blk = BS((None, tr, c), lambda layer, i: (layer, i, 0))
    return pl.pallas_call(
        body, name=f"sum_adamw_{tag}", grid=(2, r // tr),
        in_specs=[BS((N_DEV, tr, c), lambda layer, i: (0, i * (1 - layer), 0)),
                  BS((N_DEV, tr, c), lambda layer, i: (0, i * layer, 0)), blk, blk, blk],
        out_specs=[blk] * 4, out_shape=[SDS((2, r, c), F32)] * 4, compiler_params=_cp("arbitrary", "arbitrary"),
    )(recv0, recv1, w, m, v)


def _adamw(w, g, m, v, tag):
    layers, r, c = w.shape
    tr = next(r // k for k in (1, 2, 4, 8) if r // k <= 384 and r % (8 * k) == 0)

    def body(w_ref, g_ref, m_ref, v_ref, d_ref, mo_ref, vo_ref):
        d_ref[...], mo_ref[...], vo_ref[...] = _adamw_update(w_ref[...], g_ref[...], m_ref[...], v_ref[...])

    blk = BS((None, tr, c), lambda l, i: (l, i, 0))
    return pl.pallas_call(
        body, name=f"adamw_{tag}", grid=(layers, r // tr), in_specs=[blk] * 4, out_specs=[blk] * 3,
        out_shape=[SDS((layers, r, c), F32)] * 3, compiler_params=_cp("parallel", "parallel"),
    )(w, g, m, v)


def _place():
    return lax.axis_index("x"), lax.axis_index("y"), lax.axis_index("c")


def _flip(coord, bit):
    return 1 - coord if bit else coord


def _peers(x, y, c):
    peers = []
    for mask in range(1, N_DEV):
        p = (_flip(x, mask & 4), _flip(y, mask & 2), _flip(c, mask & 1))
        peers.append((p, 4 * p[0] + 2 * p[1] + p[2]))
    return peers


def _copy_plan(mode, src, land, x, y, c):
    me = 4 * x + 2 * y + c

    def device(mask):
        p = (_flip(x, mask & 4), _flip(y, mask & 2), _flip(c, mask & 1))
        return p, 4 * p[0] + 2 * p[1] + p[2]

    if mode == "scatter":
        r = land.shape[1]
        return [(p, src.at[pl.ds(i * r, r), :], land.at[me], land.at[i])
                for p, i in map(device, (1, 2, 3, 4, 5, 6, 7, 0))]
    r = land.shape[0] // N_DEV

    def rows(i):
        return land.at[pl.ds(i * r, r), :]

    if mode == "gather":
        return [(p, src, rows(me), rows(i)) for p, i in map(device, (1, 4, 2, 6, 0))]
    sibling = device(1)[0]
    return [(sibling, rows(device(m)[1]), rows(device(m)[1]), rows(device(m | 1)[1])) for m in (4, 2, 6)]


COPIES = dict(scatter=8, gather=5, forward=3)
HBM_SPEC = BS(memory_space=pltpu.HBM)
SEM_SPEC = BS(memory_space=pltpu.SEMAPHORE)
DATAFLOW = pltpu.SideEffectType.DATAFLOW_SIDE_EFFECTING


def _fresh(shape, dtype, tag):
    def body(o_ref):
        del o_ref

    return pl.pallas_call(body, name=f"fresh_{tag}", out_specs=BS(memory_space=pl.ANY), out_shape=SDS(shape, dtype))()


def _exchange_start(mode, srcs, lands, after, tag):
    if lands is None and mode == "gather":
        lands = [_fresh((N_DEV * s.shape[0], s.shape[1]), s.dtype, f"{tag}_{a}") for a, s in enumerate(srcs)]
    elif lands is None:
        lands = [_fresh((N_DEV, s.shape[0] // N_DEV, s.shape[1]), s.dtype, f"{tag}_{a}") for a, s in enumerate(srcs)]
    n, n_src, n_cp = len(lands), len(srcs), COPIES[mode]
    behind = [] if after is None else [after]

    def body(*refs):
        src_refs, land_refs = refs[:n_src], refs[n_src:n_src + n]
        send_sems, recv_sems = refs[n_src + n + len(behind)], refs[n_src + n + len(behind) + 1]
        token = refs[-1]
        for a in range(n):
            plan = _copy_plan(mode, src_refs[a] if n_src else None, land_refs[a], *_place())
            for k, (p, out, there, _) in enumerate(plan):
                pltpu.make_async_remote_copy(
                    src_ref=out, dst_ref=there, send_sem=send_sems.at[n_cp * a + k],
                    recv_sem=recv_sems.at[n_cp * a + k], device_id=p, device_id_type=MESH).start()
        token[...] = jnp.zeros_like(token)

    both = [*srcs, *lands]
    res = pl.pallas_call(
        body, name=f"{mode}_start_{tag}",
        out_shape=(pltpu.SemaphoreType.DMA((n_cp * n,)), pltpu.SemaphoreType.DMA((n_cp * n,)),
                   *[pltpu.HBM(v.shape, v.dtype) for v in both], SDS((8, 128), F32)),
        in_specs=[HBM_SPEC] * len(both) + [BS(memory_space=pl.ANY)] * len(behind),
        out_specs=(SEM_SPEC, SEM_SPEC, *[HBM_SPEC] * len(both), BS(memory_space=pltpu.VMEM)),
        input_output_aliases={i: 2 + i for i in range(len(both))},
        compiler_params=pltpu.CompilerParams(has_side_effects=DATAFLOW),
    )(*[pltpu.with_memory_space_constraint(v, pltpu.HBM) for v in both], *behind)
    return (mode, res[0], res[1], res[2:2 + n_src], res[2 + n_src:2 + n_src + n]), res[-1]


def _exchange_wait(handle, after, tag, which=None):
    mode, send_sems, recv_sems, srcs, lands = handle
    which = list(range(len(lands))) if which is None else list(which)
    n_cp = COPIES[mode]
    lands = [lands[a] for a in which]
    srcs = [srcs[a] for a in which] if srcs else []
    n, n_src = len(lands), len(srcs)
    afters = list(after) if isinstance(after, (tuple, list)) else [after]

    def body(*refs):
        src_refs, land_refs = refs[:n_src], refs[n_src:n_src + n]
        send_ref, recv_ref = refs[n_src + n], refs[n_src + n + 1]
        for i, a in enumerate(which):
            plan = _copy_plan(mode, src_refs[i] if n_src else None, land_refs[i], *_place())
            for k, (p, out, _, here) in enumerate(plan):
                cp = pltpu.make_async_remote_copy(
                    src_ref=out, dst_ref=here, send_sem=send_ref.at[n_cp * a + k], recv_sem=recv_ref.at[n_cp * a + k],
                    device_id=p, device_id_type=MESH)
                cp.wait_send()
                cp.wait_recv()

    both = [*srcs, *lands]
    res = pl.pallas_call(
        body, name=f"{mode}_wait_{tag}", out_shape=tuple(pltpu.HBM(v.shape, v.dtype) for v in both),
        in_specs=[HBM_SPEC] * len(both) + [SEM_SPEC, SEM_SPEC] + [BS(memory_space=pl.ANY)] * len(afters),
        out_specs=tuple([HBM_SPEC] * len(both)),
        input_output_aliases={i: i for i in range(len(both))},
        compiler_params=pltpu.CompilerParams(has_side_effects=DATAFLOW),
    )(*both, send_sems, recv_sems, *afters)
    return list(res[n_src:])


def _allreduce_small(vec, behind):
    rows = vec.shape[0]

    def body(x_ref, behind_ref, o_ref, buf_ref, send_sems, recv_sems):
        x, y, c = _place()
        me = 4 * x + 2 * y + c
        buf_ref[me] = x_ref[...]
        peers = _peers(x, y, c)

        def copy(k, slot):
            return pltpu.make_async_remote_copy(
                src_ref=x_ref, dst_ref=buf_ref.at[slot], send_sem=send_sems.at[k], recv_sem=recv_sems.at[k],
                device_id=peers[k][0], device_id_type=MESH)

        sends = [copy(k, me) for k in range(N_DEV - 1)]
        for cp in sends:
            cp.start()
        for k in range(N_DEV - 1):
            copy(k, peers[k][1]).wait_recv()
        for cp in sends:
            cp.wait_send()
        acc = buf_ref[0]
        for s in range(1, N_DEV):
            acc = acc + buf_ref[s]
        o_ref[...] = acc

    vmem = BS(memory_space=pltpu.VMEM)
    return pl.pallas_call(
        body, name="allreduce_small", in_specs=[vmem, BS(memory_space=pl.ANY)], out_specs=vmem,
        out_shape=SDS((rows, 128), F32),
        scratch_shapes=[pltpu.VMEM((N_DEV, rows, 128), F32), pltpu.SemaphoreType.DMA((7,)),
                        pltpu.SemaphoreType.DMA((7,))],
        compiler_params=pltpu.CompilerParams(has_side_effects=True),
    )(vec, behind)


def _ffn_forward(x, hn, fetch, names, tag, next_g):
    gu, act = _ffn_up(hn, fetch(names[0], hn).reshape(2, F, D), tag)
    got = _mm_nn(act[None], fetch(names[1], act)[None], f"down_{tag}", res=x, scale=0.5, tm=512, next_g=next_g)
    out, hn_next = got if next_g is not None else (got, None)
    return out, hn_next, (x, hn, gu, act)


def _ffn_backward(dxo, dxo_b, saved, norm_g, wut, wd, tag, send):
    x, hn, gu, act = saved
    d_wd = _mm_tn(act[None], dxo_b, f"dwd_{tag}", scale=0.5)
    du = _ffn_dact(dxo_b, wd, gu, send(("down",), [d_wd]), tag)
    d_wut = _mm_tn(du, hn, f"dwu_{tag}")
    token = send(("up",), [d_wut])
    return _mm_nn_norm_bwd([du], wut.reshape(2 * F, D), x, norm_g + token[0, 0], dxo, tag)


def _mixer_forward(x, hn, fetch, bias, tables, tag, next_g):
    proj = _mm_nt_rows(hn, fetch("win", hn), f"proj_{tag}", 1024, IN_W // 2, IN_W, 0, rope=(*tables, 2 * QKV_A // 3))
    qkr = proj
    outs, lses = [], []
    for grp in range(3):
        o, l = _dil_fwd(qkr, proj, grp)
        outs.append(o)
        lses.append(l)
    ya = _combine_fwd(outs, lses)
    yb, lse_b = _na_fwd(proj, bias)
    merged, za, zb = _merge_fwd(ya, yb, proj, fetch("wa", yb), fetch("wb", yb))
    out, hn_next = _mm_nn(merged[None], fetch("wo", merged)[None], f"out_{tag}", res=x, next_g=next_g)
    return out, hn_next, (x, hn, proj, qkr, outs, lses, ya, yb, lse_b, merged, za, zb)


def _mixer_backward(dxo, dxo_b, saved, norm_g, w, bias, tables, tag, send):
    wint, wat, wbt, wo = w
    x, hn, proj, qkr, outs, lses, ya, yb, lse_b, merged, za, zb = saved
    d_wo = _mm_tn(merged[None], dxo_b, f"dwo_{tag}")
    dza, dzb, dlog = _merge_bwd(dxo_b, wo, za, zb, proj)
    dya = _mm_nn(dza[None], wat[None], f"dya_{tag}")
    dyb = _mm_nn(dzb[None], wbt[None], f"dyb_{tag}")
    d_wat = _mm_tn(dza[None], ya, f"dwa_{tag}")
    d_wbt = _mm_tn(dzb[None], yb, f"dwb_{tag}")
    cb = _combine_bwd(dya, outs, lses)
    dqs, dks, dvs = [], [], []
    for grp in range(3):
        dq, dk, dv = _dil_bwd(qkr, proj, cb[grp], cb[3 + grp], lses[grp], grp)
        dqs.append(dq)
        dks.append(dk)
        dvs.append(dv)
    d_qkv_b, dbias_tab = _na_bwd(proj, bias, dyb, yb, lse_b)
    dbias = _na_dbias(dbias_tab)
    dproj = [_rope_bwd(dqs, dks, dvs, *tables), d_qkv_b, dlog]
    d_wint, row = None, 0
    for i, p in enumerate(dproj):
        d_wint = _mm_tn(p, hn, f"dwin{i}_{tag}", into=d_wint, row0=row, rows=IN_W)
        row += p.shape[0] * p.shape[2]
    token = send(("win", "wa", "wb", "wo"), [d_wint, d_wat, d_wbt, d_wo])
    dx, dx_b, dg = _mm_nn_norm_bwd(dproj, wint, x, norm_g + token[0, 0], dxo, f"mix_{tag}", tm=512, single_w=True)
    dbias = dbias[:, 0, :480].reshape(8, 15, 32)[:, :, :31]
    return dx, dx_b, dg, dbias


def _pack_small(norms, biases, final, loss=None):
    parts = []
    for layer in range(DEPTH):
        parts += [norms[0][layer], norms[1][layer], norms[2][layer],
                  jnp.pad(biases[layer].reshape(-1), (0, BIAS_PAD - 8 * 15 * 31))]
    parts.append(final)
    flat = jnp.concatenate([p.reshape(-1).astype(F32) for p in parts])
    if loss is not None:
        flat = jnp.concatenate([flat, loss.reshape(-1)])
    return jnp.pad(flat, (0, SMALL_ROWS * 128 - flat.shape[0])).reshape(SMALL_ROWS, 128)


def _unpack_small(packed):
    flat = packed.reshape(-1)
    norms, biases = ([], [], []), []
    pos = 0
    for _ in range(DEPTH):
        for k in range(3):
            norms[k].append(flat[pos:pos + D])
            pos += D
        biases.append(flat[pos:pos + 8 * 15 * 31].reshape(8, 15, 31))
        pos += BIAS_PAD
    final = flat[pos:pos + D]
    pos += D
    return [jnp.stack(n) for n in norms], jnp.stack(biases), final, flat[pos]


def kernel(x, ffn1_norm, ffn1_w_up, ffn1_w_down, mix_norm, w_in, na_rel_bias, w_branch_a, w_branch_b, w_out, ffn2_norm, ffn2_w_up, ffn2_w_down, final_norm, loss_target, m_ffn1_norm, m_ffn1_w_up, m_ffn1_w_down, m_mix_norm, m_w_in, m_na_rel_bias, m_w_branch_a, m_w_branch_b, m_w_out, m_ffn2_norm, m_ffn2_w_up, m_ffn2_w_down, m_final_norm, v_ffn1_norm, v_ffn1_w_up, v_ffn1_w_down, v_mix_norm, v_w_in, v_na_rel_bias, v_w_branch_a, v_w_branch_b, v_w_out, v_ffn2_norm, v_ffn2_w_up, v_ffn2_w_down, v_final_norm):
    t = x.shape[0] * x.shape[1]
    xs = x.reshape(t, D)
    tgt = loss_target.reshape(t, D)
    tables = _rope_tables()

    col_sharded = dict(up1=ffn1_w_up, win=w_in, wa=w_branch_a, wb=w_branch_b, up2=ffn2_w_up)
    row_sharded = dict(down1=ffn1_w_down, wo=w_out, down2=ffn2_w_down)
    shard = [{} for _ in range(DEPTH)]
    for layer in range(DEPTH):
        for name, arr in col_sharded.items():
            shard[layer][name] = arr[layer].T.astype(BF16)
        for name, arr in row_sharded.items():
            shard[layer][name] = arr[layer].astype(BF16)

    weights = [{} for _ in range(DEPTH)]
    travel = [(0, ("up1",)), (0, ("down1",)), (0, ("win",)), (0, ("wa", "wb", "wo")), (0, ("up2", "down2")),
              (1, ("up1", "down1")), (1, ("win",)), (1, ("wa", "wb", "wo")), (1, ("up2", "down2"))]
    group_of, chips_done, sibling_done, passing = {}, {}, {}, {}
    count = 0
    for i, (layer, names) in enumerate(travel):
        chips_done[i] = list(range(count, count + len(names)))
        count += len(names)
        for n in names:
            group_of[layer, n] = (i, names)
    gathered, token = _exchange_start(
        "gather", [shard[layer][n] for layer, names in travel for n in names], None, None, "w")
    zero = token[0, 0]

    biases = [_na_bias_table(na_rel_bias[layer] + zero) for layer in range(DEPTH)]

    def pass_on(i, behind):
        if i in chips_done:
            lands = _exchange_wait(gathered, behind, f"w{i}", which=chips_done.pop(i))
            sibling_done[i], passing[i] = _exchange_start("forward", [], lands, None, f"p{i}")

    def fetcher(layer):
        def fetch(name, behind):
            if (layer, name) in group_of:
                i, names = group_of[layer, name]
                if i == 0:
                    behind = (behind, *biases)
                behind = behind if isinstance(behind, tuple) else (behind,)
                pass_on(i, behind)
                if i > 0:
                    pass_on(i + 1, behind)
                if i + 1 in passing:
                    behind = (*behind, passing[i + 1])
                for n, got in zip(names, _exchange_wait(sibling_done.pop(i), behind, f"p{i}")):
                    weights[layer][n] = got
                    del group_of[layer, n]
            return weights[layer][name]
        return fetch

    saved = []
    h = xs
    hn = _norm_fwd(xs, ffn1_norm[0] + zero, "first")
    for layer in range(DEPTH):
        bias = biases[layer]
        fetch = fetcher(layer)
        after_ffn2 = ffn1_norm[layer + 1] if layer + 1 < DEPTH else None
        h, hn, s1 = _ffn_forward(h, hn, fetch, ("up1", "down1"), f"f1l{layer}", mix_norm[layer])
        h, hn, s2 = _mixer_forward(h, hn, fetch, bias, tables, f"l{layer}", ffn2_norm[layer])
        h, hn, s3 = _ffn_forward(h, hn, fetch, ("up2", "down2"), f"f2l{layer}", after_ffn2)
        saved.append((s1, s2, s3, bias))
    loss_part, dh, dh_b, d_final = _loss_head(h, final_norm, tgt)

    d_norms = ([None] * DEPTH, [None] * DEPTH, [None] * DEPTH)
    d_bias = [None] * DEPTH
    sent = {}

    def sender(layer, suffix):
        def send(names, grads):
            tag = f"g{layer}{names[0]}{suffix}"
            handle, token = _exchange_start("scatter", grads, None, None, tag)
            for i, n in enumerate(names):
                sent[layer, n + suffix] = (handle, i, tag)
            return token
        return send

    for layer in reversed(range(DEPTH)):
        w = weights[layer]
        s1, s2, s3, bias = saved[layer]
        dh, dh_b, d_norms[2][layer] = _ffn_backward(
            dh, dh_b, s3, ffn2_norm[layer], w["up2"].reshape(2, F, D), w["down2"], f"f2l{layer}", sender(layer, "2"))
        dh, dh_b, d_norms[1][layer], d_bias[layer] = _mixer_backward(
            dh, dh_b, s2, mix_norm[layer], (w["win"], w["wa"], w["wb"], w["wo"]), bias, tables, f"l{layer}",
            sender(layer, ""))
        dh, dh_b, d_norms[0][layer] = _ffn_backward(
            dh, dh_b, s1, ffn1_norm[layer], w["up1"].reshape(2, F, D), w["down1"], f"f1l{layer}", sender(layer, "1"))
    grad_x = dh.reshape(x.shape)

    originals = dict(up1=(ffn1_w_up, m_ffn1_w_up, v_ffn1_w_up), down1=(ffn1_w_down, m_ffn1_w_down, v_ffn1_w_down),
                     win=(w_in, m_w_in, v_w_in), wa=(w_branch_a, m_w_branch_a, v_w_branch_a),
                     wb=(w_branch_b, m_w_branch_b, v_w_branch_b), wo=(w_out, m_w_out, v_w_out),
                     up2=(ffn2_w_up, m_ffn2_w_up, v_ffn2_w_up), down2=(ffn2_w_down, m_ffn2_w_down, v_ffn2_w_down))
    big = {}
    behind = dh
    landed = {}

    def received(layer, name):
        handle, i, tag = sent[layer, name]
        if tag not in landed:
            landed[tag] = _exchange_wait(handle, behind, tag)
        return landed[tag][i]

    for name in ("down2", "up2", "win", "wa", "wb", "wo", "down1", "up1"):
        wv, mv, vv = originals[name]
        if name in col_sharded:
            wv, mv, vv = (jnp.swapaxes(t, 1, 2) for t in (wv, mv, vv))
        big[name] = tuple(_sum_adamw(received(0, name), received(1, name), wv, mv, vv, name))
        behind = big[name][1]
        if name in col_sharded:
            big[name] = tuple(jnp.swapaxes(t, 1, 2) for t in big[name])

    small = _allreduce_small(_pack_small(d_norms, d_bias, d_final, loss_part[0, :1]), behind)
    g_norms, g_bias, g_final, loss = _unpack_small(small)
    w_small = _pack_small((ffn1_norm, mix_norm, ffn2_norm), na_rel_bias, final_norm)
    m_small = _pack_small((m_ffn1_norm, m_mix_norm, m_ffn2_norm), m_na_rel_bias, m_final_norm)
    v_small = _pack_small((v_ffn1_norm, v_mix_norm, v_ffn2_norm), v_na_rel_bias, v_final_norm)
    upd = _adamw(w_small[None], small[None], m_small[None], v_small[None], "small")
    small_out = [(g_norms, g_bias, g_final)] + [_unpack_small(u[0])[:3] for u in upd]

    outputs = [loss, grad_x]
    for kind in range(4):
        norms, bias_k, final_k = small_out[kind]
        outputs += [norms[0], big["up1"][kind], big["down1"][kind], norms[1], big["win"][kind], bias_k,
                    big["wa"][kind], big["wb"][kind], big["wo"][kind], norms[2], big["up2"][kind],
                    big["down2"][kind], final_k]
    return tuple(outputs)
```

```python
import numpy as np

import jax
import jax.numpy as jnp
from jax import lax
from jax.experimental import pallas as pl
from jax.experimental.pallas import tpu as pltpu

F32 = jnp.float32
BF16 = jnp.bfloat16
SDS = jax.ShapeDtypeStruct
BS = pl.BlockSpec
MESH = pl.DeviceIdType.MESH

D = 1024
S = 2048
F = 2816
DEPTH = 2
HEAD_DIM = 64
DILATIONS = (1, 4, 16)
HALF = 64
QKV_A = 2304
QKV_B = 1536
IN_W = 5888
N_DEV = 8
NA_ROWS = 32
GRID_W = 64
NA_KR = 8
ROPE_THETA = 10000.0
RMS_EPS = 1e-6
NEG = -1e30
SCALE = HEAD_DIM ** -0.5
ADAM_LR, ADAM_B1, ADAM_B2, ADAM_EPS, ADAM_WD, ADAM_STEP = 0.001, 0.9, 0.999, 1e-08, 0.01, 10
VMEM_LIMIT_V7X = 52 * 1024 * 1024
SMALL_ROWS = 120
BIAS_PAD = 3840
NA_FWD_ROWS = 8
NA_BWD_ROWS = 4
DIL_FWD_TILES = 8
DIL_BWD_TILES = 4


def _cp(*sem):
    return pltpu.CompilerParams(dimension_semantics=sem, vmem_limit_bytes=VMEM_LIMIT_V7X)


def _dot_nn(a, b):
    return jnp.dot(a, b, preferred_element_type=F32)


def _dot_nt(a, b):
    return lax.dot_general(a, b, (((1,), (1,)), ((), ())), preferred_element_type=F32)


def _dot_tn(a, b):
    return lax.dot_general(a, b, (((0,), (0,)), ((), ())), preferred_element_type=F32)


def _ds(start, size, stride):
    return pl.ds(start, size) if stride == 1 else pl.ds(start, size, stride=stride)


def _norm_fwd(x, g, tag):
    t = x.shape[0]
    tm = 512

    def body(x_ref, g_ref, o_ref):
        xv = x_ref[...]
        r = lax.rsqrt(jnp.mean(xv * xv, axis=-1, keepdims=True) + RMS_EPS)
        o_ref[...] = (xv * r * g_ref[...]).astype(BF16)

    return pl.pallas_call(
        body, name=f"norm_fwd_{tag}", grid=(t // tm,),
        in_specs=[BS((tm, D), lambda i: (i, 0)), BS((1, D), lambda i: (0, 0))],
        out_specs=BS((tm, D), lambda i: (i, 0)),
        out_shape=SDS((t, D), BF16), compiler_params=_cp("parallel"),
    )(x, g.reshape(1, D))


def _loss_head(x, g, tgt):
    t = x.shape[0]
    tm = 1024

    def body(x_ref, g_ref, t_ref, loss_ref, dx_ref, dxb_ref, dg_ref):
        @pl.when(pl.program_id(0) == 0)
        def _():
            dg_ref[...] = jnp.zeros_like(dg_ref)
            loss_ref[...] = jnp.zeros_like(loss_ref)

        xv = x_ref[...]
        gv = g_ref[...]
        r = lax.rsqrt(jnp.mean(xv * xv, axis=-1, keepdims=True) + RMS_EPS)
        xh = xv * r
        e = xh * gv - t_ref[...]
        loss_ref[...] += 0.5 * jnp.sum(jnp.mean(e * e, axis=-1, keepdims=True), axis=0, keepdims=True)
        dy = e * (1.0 / D)
        u = dy * gv
        dx = r * (u - xh * jnp.mean(xh * u, axis=-1, keepdims=True))
        dx_ref[...] = dx
        dxb_ref[...] = dx.astype(BF16)
        dg_ref[...] += jnp.sum(dy * xh, axis=0, keepdims=True)

    row = BS((tm, D), lambda i: (i, 0))
    vec = BS((1, D), lambda i: (0, 0))
    return pl.pallas_call(
        body, name="loss_head", grid=(t // tm,),
        in_specs=[row, vec, row], out_specs=[BS((1, 128), lambda i: (0, 0)), row, row, vec],
        out_shape=[SDS((1, 128), F32), SDS((t, D), F32), SDS((t, D), BF16), SDS((1, D), F32)],
        compiler_params=_cp("arbitrary"),
    )(x, g.reshape(1, D), tgt)


def _mm_nn(a, w, tag, res=None, scale=1.0, tm=1024, tn=None, next_g=None):
    c_n, t, k = a.shape
    n = w.shape[2]
    tn = n if tn is None else tn
    assert next_g is None or tn == n
    n_in = 2 + (res is not None) + (next_g is not None)

    def body(*refs):
        a_ref, w_ref = refs[0], refs[1]
        acc = _dot_nn(a_ref[0].astype(BF16), w_ref[0])
        for c in range(1, c_n):
            acc = acc + _dot_nn(a_ref[c].astype(BF16), w_ref[c])
        if scale != 1.0:
            acc = acc * scale
        if res is not None:
            acc = refs[2][...] + acc
        refs[n_in][...] = acc
        if next_g is not None:
            r = lax.rsqrt(jnp.mean(acc * acc, axis=-1, keepdims=True) + RMS_EPS)
            refs[n_in + 1][...] = (acc * r * refs[n_in - 1][...]).astype(BF16)

    w_mode = dict(pipeline_mode=pl.Buffered(1)) if tn == n else {}
    in_specs = [BS((c_n, tm, k), lambda i, j: (0, i, 0)), BS((c_n, k, tn), lambda i, j: (0, 0, j), **w_mode)]
    args = [a, w]
    out_specs = [BS((tm, tn), lambda i, j: (i, j))]
    out_shape = [SDS((t, n), F32)]
    if res is not None:
        in_specs.append(BS((tm, tn), lambda i, j: (i, j)))
        args.append(res)
    if next_g is not None:
        in_specs.append(BS((1, n), lambda i, j: (0, 0)))
        args.append(next_g.reshape(1, n))
        out_specs.append(BS((tm, tn), lambda i, j: (i, j)))
        out_shape.append(SDS((t, n), BF16))
    got = pl.pallas_call(
        body, name=f"mm_nn_{tag}", grid=(t // tm, n // tn), in_specs=in_specs, out_specs=out_specs,
        out_shape=out_shape, compiler_params=_cp("parallel", "parallel"),
    )(*args)
    return got if next_g is not None else got[0]


def _mm_nn_norm_bwd(parts, w, x, g, dres, tag, tm=256, single_w=False):
    t = parts[0].shape[1]
    n_parts = len(parts)

    def body(*refs):
        w_ref, x_ref, g_ref, dr_ref, dx_ref, dxb_ref, dg_ref = refs[n_parts:]

        @pl.when(pl.program_id(0) == 0)
        def _():
            dg_ref[...] = jnp.zeros_like(dg_ref)

        dh = None
        row = 0
        for a_ref, part in zip(refs, parts):
            for c in range(part.shape[0]):
                term = _dot_nn(a_ref[c].astype(BF16), w_ref[row:row + part.shape[2], :])
                dh = term if dh is None else dh + term
                row += part.shape[2]
        xv = x_ref[...]
        r = lax.rsqrt(jnp.mean(xv * xv, axis=-1, keepdims=True) + RMS_EPS)
        xh = xv * r
        u = dh * g_ref[...]
        dx = dr_ref[...] + r * (u - xh * jnp.mean(xh * u, axis=-1, keepdims=True))
        dx_ref[...] = dx
        dxb_ref[...] = dx.astype(BF16)
        dg_ref[...] += jnp.sum(dh * xh, axis=0, keepdims=True)

    row = BS((tm, D), lambda i: (i, 0))
    vec = BS((1, D), lambda i: (0, 0))
    return pl.pallas_call(
        body, name=f"mm_nn_norm_bwd_{tag}", grid=(t // tm,),
        in_specs=[BS((p.shape[0], tm, p.shape[2]), lambda i: (0, i, 0)) for p in parts]
        + [BS(w.shape, lambda i: (0, 0), **(dict(pipeline_mode=pl.Buffered(1)) if single_w else {})), row, vec, row],
        out_specs=[row, row, vec], out_shape=[SDS((t, D), F32), SDS((t, D), BF16), SDS((1, D), F32)],
        compiler_params=_cp("arbitrary"),
    )(*parts, w, x, g.reshape(1, D), dres)


def _mm_nt_rows(a, w, tag, tm, tn, n_total, w_row0, rope=None):
    t, k = a.shape
    assert w_row0 % tn == 0 and n_total % tn == 0
    j0 = w_row0 // tn

    def body(a_ref, w_ref, *rest):
        o_ref = rest[-1]
        o_ref[...] = _dot_nt(a_ref[...].astype(BF16), w_ref[...])
        if rope is not None:
            @pl.when(pl.program_id(0) == 0)
            def _():
                c = rest[0][...]
                sg = rest[1][...]
                first = (lax.broadcasted_iota(jnp.int32, (tm, 128), 1) % HEAD_DIM) < HEAD_DIM // 2
                for col in range(0, rope[2], 128):
                    v = o_ref[:, col:col + 128]
                    o_ref[:, col:col + 128] = v * c + _swap_halves(v, first) * sg

    in_specs = [BS((tm, k), lambda j, i: (i, 0)), BS((tn, k), lambda j, i: (j0 + j, 0))]
    args = [a, w]
    if rope is not None:
        assert rope[2] <= tn
        in_specs += [BS((tm, 128), lambda j, i: (i % (S // tm), 0))] * 2
        args += [rope[0], rope[1]]
    return pl.pallas_call(
        body, name=f"mm_nt_{tag}", grid=(n_total // tn, t // tm), in_specs=in_specs,
        out_specs=BS((tm, tn), lambda j, i: (i, j)), out_shape=SDS((t, n_total), F32),
        compiler_params=_cp("parallel", "parallel"),
    )(*args)


def _mm_tn(a, b, tag, scale=1.0, tmm=None, into=None, row0=0, rows=None):
    c_n, t, m = a.shape
    n = b.shape[1]
    if tmm is None:
        tmm = max(w for w in (1408, 768, 512, 256) if m % w == 0 and row0 % w == 0)
    tiles = m // tmm
    block0 = row0 // tmm
    assert row0 % tmm == 0 and m % tmm == 0

    def body(a_ref, b_ref, *rest):
        rest[-1][...] = (_dot_tn(a_ref[...].astype(BF16), b_ref[...].astype(BF16)) * scale).astype(BF16)

    in_specs = [BS((None, t, tmm), lambda c, mi: (c, 0, mi)), BS((t, n), lambda c, mi: (0, 0))]
    args = [a, b]
    if into is not None:
        in_specs.append(BS(memory_space=pl.ANY))
        args.append(into)
    return pl.pallas_call(
        body, name=f"mm_tn_{tag}", grid=(c_n, tiles), in_specs=in_specs,
        out_specs=BS((tmm, n), lambda c, mi: (block0 + c * tiles + mi, 0)),
        out_shape=SDS((rows or c_n * m, n) if into is None else into.shape, BF16),
        input_output_aliases={} if into is None else {2: 0},
        compiler_params=_cp("parallel", "parallel"),
    )(*args)


def _ffn_up(hn, wut, tag):
    t = hn.shape[0]
    tm, tn = 1024, 1408

    def body(h_ref, w_ref, gu_ref, act_ref):
        h = h_ref[...]
        g = _dot_nt(h, w_ref[0])
        u = _dot_nt(h, w_ref[1])
        sg = jax.nn.sigmoid(g)
        silu = g * sg
        gu_ref[0] = (u * (sg + silu * (1.0 - sg))).astype(BF16)
        gu_ref[1] = silu.astype(BF16)
        act_ref[...] = (silu * u).astype(BF16)

    return pl.pallas_call(
        body, name=f"ffn_up_{tag}", grid=(F // tn, t // tm),
        in_specs=[BS((tm, D), lambda j, i: (i, 0)), BS((2, tn, D), lambda j, i: (0, j, 0))],
        out_specs=[BS((2, tm, tn), lambda j, i: (0, i, j)), BS((tm, tn), lambda j, i: (i, j))],
        out_shape=[SDS((2, t, F), BF16), SDS((t, F), BF16)],
        compiler_params=_cp("parallel", "parallel"),
    )(hn, wut)


def _ffn_dact(dxo, wd, gu, tie, tag):
    t = dxo.shape[0]
    tm, tn = 1024, 1408

    def body(d_ref, w_ref, gu_ref, tie_ref, o_ref):
        dact = _dot_nt(d_ref[...] * 0.5, w_ref[...])
        o_ref[0] = (dact * gu_ref[0].astype(F32)).astype(BF16)
        o_ref[1] = (dact * gu_ref[1].astype(F32)).astype(BF16)

    return pl.pallas_call(
        body, name=f"ffn_dact_{tag}", grid=(F // tn, t // tm),
        in_specs=[BS((tm, D), lambda j, i: (i, 0)), BS((tn, D), lambda j, i: (j, 0)),
                  BS((2, tm, tn), lambda j, i: (0, i, j)), BS((8, 128), lambda j, i: (0, 0))],
        out_specs=BS((2, tm, tn), lambda j, i: (0, i, j)),
        out_shape=SDS((2, t, F), BF16), compiler_params=_cp("parallel", "parallel"),
    )(dxo, wd, gu, tie)


def _rope_tables():
    half = HEAD_DIM // 2
    inv_freq = ROPE_THETA ** (-jnp.arange(half, dtype=F32) / half)
    ang = jnp.arange(S).astype(F32)[:, None] * inv_freq[None, :]
    cos, sin = jnp.cos(ang), jnp.sin(ang)
    return jnp.concatenate([cos, cos, cos, cos], axis=1), jnp.concatenate([-sin, sin, -sin, sin], axis=1)


def _swap_halves(t, first_half):
    return jnp.where(first_half, pltpu.roll(t, 96, 1), pltpu.roll(t, 32, 1))


def _rope_bwd(dqs, dks, dvs, cos_t, sin_t):
    t = dqs[0].shape[0]
    tm = 1024

    def body(*refs):
        c = refs[9][...]
        sg = refs[10][...]
        o_ref = refs[11]
        first = (lax.broadcasted_iota(jnp.int32, (tm, 128), 1) % HEAD_DIM) < HEAD_DIM // 2
        for a in range(6):
            for hp in range(2):
                v = refs[a][:, 128 * hp:128 * (hp + 1)]
                col = 128 * (2 * a + hp)
                o_ref[:, col:col + 128] = (v * c + _swap_halves(v * sg, first)).astype(BF16)
        for a in range(6, 9):
            o_ref[:, 256 * a:256 * (a + 1)] = refs[a][...].astype(BF16)

    blk = BS((tm, 256), lambda i: (i, 0))
    tab = BS((tm, 128), lambda i: (i % (S // tm), 0))
    return pl.pallas_call(
        body, name="rope_bwd", grid=(t // tm,), in_specs=[blk] * 9 + [tab, tab],
        out_specs=BS((None, tm, QKV_A), lambda i: (0, i, 0)), out_shape=SDS((1, t, QKV_A), BF16),
        compiler_params=_cp("parallel"),
    )(*dqs, *dks, *dvs, cos_t, sin_t)


def _head_masks():
    lane = lax.broadcasted_iota(jnp.int32, (1, 128), 1)
    m0 = (lane < HEAD_DIM).astype(F32)
    return m0, 1.0 - m0


def _dil_geometry(d):
    sub = S // d
    q_rows = 128
    k_rows = min(256, sub)
    return sub, q_rows, sub // q_rows, k_rows


def _dil_tile(idx, d, keys_on_rows=False):
    sub, q_rows, nb, k_rows = _dil_geometry(d)
    r = idx // nb
    n = idx % nb
    k_sub = jnp.clip(q_rows * n - HALF, 0, sub - k_rows)
    if d == 1:
        q_start = pl.multiple_of(q_rows * n, q_rows)
        k_start = pl.multiple_of(k_sub, HALF)
    else:
        q_start = q_rows * n * d + r
        k_start = k_sub * d + r
    if keys_on_rows:
        ii = lax.broadcasted_iota(jnp.int32, (k_rows, 2 * q_rows), 1) % q_rows
        jj = lax.broadcasted_iota(jnp.int32, (k_rows, 2 * q_rows), 0)
    else:
        ii = lax.broadcasted_iota(jnp.int32, (q_rows, k_rows), 0)
        jj = lax.broadcasted_iota(jnp.int32, (q_rows, k_rows), 1)
    valid = jnp.abs(jj - ii + (k_sub - q_rows * n)) <= HALF
    return q_start, k_start, valid


def _dil_specs(grp):
    qs = BS((S, 128), lambda b, hp: (b, 2 * grp + hp))
    ks = BS((S, 128), lambda b, hp: (b, 6 + 2 * grp + hp))
    vs = BS((S, 128), lambda b, hp: (b, 12 + 2 * grp + hp))
    own = BS((S, 128), lambda b, hp: (b, hp))
    return qs, ks, vs, own


def _dil_fwd(qkr, proj, grp):
    t = qkr.shape[0]
    d = DILATIONS[grp]
    _, q_rows, nb, k_rows = _dil_geometry(d)

    def body(q_ref, k_ref, v_ref, o_ref, l_ref):
        masks = _head_masks()

        def step(i0, carry):
            geo = [_dil_tile(i0 * DIL_FWD_TILES + j, d) for j in range(DIL_FWD_TILES)]
            tiles = [(j, h) for j in range(DIL_FWD_TILES) for h in range(2)]
            qs = [q_ref[_ds(g[0], q_rows, d), :] for g in geo]
            kbs = [k_ref[_ds(g[1], k_rows, d), :].astype(BF16) for g in geo]
            ss = [jnp.where(geo[j][2], _dot_nt((qs[j] * masks[h]).astype(BF16), kbs[j]) * SCALE, NEG) for j, h in tiles]
            mxs = [jnp.max(s, axis=1, keepdims=True) for s in ss]
            ps = [jnp.exp(s - mx) for s, mx in zip(ss, mxs)]
            dens = [jnp.sum(p, axis=1, keepdims=True) for p in ps]
            vs = [v_ref[_ds(g[1], k_rows, d), :] for g in geo]
            outs = [_dot_nn(p.astype(BF16), (vs[j] * masks[h]).astype(BF16)) / den
                    for p, den, (j, h) in zip(ps, dens, tiles)]
            for j, g in enumerate(geo):
                o_ref[_ds(g[0], q_rows, d), :] = outs[2 * j] + outs[2 * j + 1]
                l_ref[_ds(g[0], q_rows, d), :] = (
                    (mxs[2 * j] + jnp.log(dens[2 * j])) * masks[0] + (mxs[2 * j + 1] + jnp.log(dens[2 * j + 1])) * masks[1])
            return carry

        lax.fori_loop(0, d * nb // DIL_FWD_TILES, step, 0)

    qs, ks, vs, own = _dil_specs(grp)
    return pl.pallas_call(
        body, name=f"dil_fwd_{grp}", grid=(t // S, 2), in_specs=[qs, ks, vs], out_specs=[own, own],
        out_shape=[SDS((t, 256), F32), SDS((t, 256), F32)], compiler_params=_cp("parallel", "parallel"),
    )(qkr, qkr, proj)


def _dil_bwd(qkr, proj, do, dlp, lse, grp):
    t = qkr.shape[0]
    d = DILATIONS[grp]
    _, q_rows, nb, k_rows = _dil_geometry(d)

    def body(q_ref, k_ref, v_ref, do_ref, dl_ref, l_ref, dq_ref, dk_ref, dv_ref):
        masks = _head_masks()
        dk_ref[...] = jnp.zeros_like(dk_ref)
        dv_ref[...] = jnp.zeros_like(dv_ref)

        def as_row(x2):
            xt = x2.T
            return jnp.concatenate([xt[0:1], xt[HEAD_DIM:HEAD_DIM + 1]], axis=1)

        def step(i0, carry):
            geo = [_dil_tile(i0 * DIL_BWD_TILES + j, d, keys_on_rows=True) for j in range(DIL_BWD_TILES)]
            q_ds = [_ds(g[0], q_rows, d) for g in geo]
            k_ds = [_ds(g[1], k_rows, d) for g in geo]
            qbs = [_both_heads(q_ref[r, :], masks).astype(BF16) for r in q_ds]
            kbs = [k_ref[r, :].astype(BF16) for r in k_ds]
            vbs = [v_ref[r, :].astype(BF16) for r in k_ds]
            dobs = [_both_heads(do_ref[r, :], masks).astype(BF16) for r in q_ds]
            l_rows = [as_row(l_ref[r, :]) for r in q_ds]
            dl_rows = [as_row(dl_ref[r, :]) for r in q_ds]
            ss = [jnp.where(g[2], _dot_nt(kb, qb) * SCALE, NEG) for g, kb, qb in zip(geo, kbs, qbs)]
            ps = [jnp.exp(s - lr) for s, lr in zip(ss, l_rows)]
            dps = [_dot_nt(vb, dob) for vb, dob in zip(vbs, dobs)]
            dss = [(p * (dp - dr)).astype(BF16) for p, dp, dr in zip(ps, dps, dl_rows)]
            dks = [_dot_nn(ds, qb) for ds, qb in zip(dss, qbs)]
            dvs = [_dot_nn(p.astype(BF16), dob) for p, dob in zip(ps, dobs)]
            dqs = [_own_heads(_dot_tn(ds, kb), masks) for ds, kb in zip(dss, kbs)]
            for j in range(DIL_BWD_TILES):
                dq_ref[q_ds[j], :] = dqs[j] * SCALE
                dk_ref[k_ds[j], :] += dks[j] * SCALE
                dv_ref[k_ds[j], :] += dvs[j]
            return carry

        lax.fori_loop(0, d * nb // DIL_BWD_TILES, step, 0)

    qs, ks, vs, own = _dil_specs(grp)
    return pl.pallas_call(
        body, name=f"dil_bwd_{grp}", grid=(t // S, 2), in_specs=[qs, ks, vs, own, own, own],
        out_specs=[own, own, own], out_shape=[SDS((t, 256), F32)] * 3,
        compiler_params=_cp("parallel", "parallel"),
    )(qkr, qkr, proj, do, dlp, lse)


def _mix_weights(l0, l1, l2):
    mx = jnp.maximum(jnp.maximum(l0, l1), l2)
    e0, e1, e2 = jnp.exp(l0 - mx), jnp.exp(l1 - mx), jnp.exp(l2 - mx)
    den = e0 + e1 + e2
    return e0 / den, e1 / den, e2 / den


def _combine_fwd(outs, lses):
    t = outs[0].shape[0]
    tm = 1024

    def body(o0, o1, o2, l0, l1, l2, y_ref):
        w0, w1, w2 = _mix_weights(l0[...], l1[...], l2[...])
        y_ref[...] = w0 * o0[...] + w1 * o1[...] + w2 * o2[...]

    blk = BS((tm, 256), lambda i: (i, 0))
    return pl.pallas_call(
        body, name="combine_fwd", grid=(t // tm,), in_specs=[blk] * 6, out_specs=blk,
        out_shape=SDS((t, 256), F32), compiler_params=_cp("parallel"),
    )(*outs, *lses)


def _head_sum(x):
    a = lax.broadcasted_iota(jnp.int32, (256, 256), 0) // HEAD_DIM
    b = lax.broadcasted_iota(jnp.int32, (256, 256), 1) // HEAD_DIM
    ones = (a == b).astype(BF16)
    hi = x.astype(BF16)
    lo = (x - hi.astype(F32)).astype(BF16)
    return _dot_nn(hi, ones) + _dot_nn(lo, ones)


def _combine_bwd(dya, outs, lses):
    t = dya.shape[0]
    tm = 1024

    def body(dy_ref, o0, o1, o2, l0, l1, l2, d0, d1, d2, e0, e1, e2):
        ws = _mix_weights(l0[...], l1[...], l2[...])
        dy = dy_ref[...]
        ya = ws[0] * o0[...] + ws[1] * o1[...] + ws[2] * o2[...]
        hs = _head_sum(dy * ya)
        for w, d_ref, e_ref in zip(ws, (d0, d1, d2), (e0, e1, e2)):
            d_ref[...] = w * dy
            e_ref[...] = w * hs

    blk = BS((tm, 256), lambda i: (i, 0))
    return pl.pallas_call(
        body, name="combine_bwd", grid=(t // tm,), in_specs=[blk] * 7, out_specs=[blk] * 6,
        out_shape=[SDS((t, 256), F32)] * 6, compiler_params=_cp("parallel"),
    )(dya, *outs, *lses)


def _na_bias_table(rel_bias):
    kw = NA_KR * GRID_W
    rev = jnp.pad(rel_bias.astype(F32)[:, :, ::-1], ((0, 0), (0, 0), (0, 128 - 31)))

    def body(r_ref, o_ref):
        lane = lax.broadcasted_iota(jnp.int32, (GRID_W, 128), 1)
        j = lax.broadcasted_iota(jnp.int32, (GRID_W, 128), 0)
        q = lane % GRID_W
        win_lo = jnp.clip(q - 8, 0, GRID_W - 16)
        valid = (j >= win_lo) & (j < win_lo + 16)
        for cls in range(NA_KR):
            for k in range(NA_KR):
                tiles = []
                for h in range(2):
                    row = jnp.broadcast_to(r_ref[h, cls + k:cls + k + 1, :], (GRID_W, 128))
                    tiles.append(pltpu.roll(row, (128 - 15 + GRID_W * h) % 128, 1, stride=1, stride_axis=0))
                o_ref[cls, GRID_W * k:GRID_W * (k + 1), :] = jnp.where(
                    valid, jnp.where(lane < GRID_W, tiles[0], tiles[1]), NEG)

    return pl.pallas_call(
        body, name="na_bias_table", grid=(4,),
        in_specs=[BS((2, 2 * NA_KR - 1, 128), lambda hp: (hp, 0, 0))],
        out_specs=BS((None, NA_KR, kw, 128), lambda hp: (hp, 0, 0, 0)),
        out_shape=SDS((4, NA_KR, kw, 128), F32), compiler_params=_cp("parallel"),
    )(rev)


def _na_row(i):
    lo = jnp.clip(i - NA_KR // 2, 0, NA_ROWS - NA_KR)
    return pl.multiple_of(GRID_W * i, GRID_W), pl.multiple_of(GRID_W * lo, GRID_W), lo - i + NA_KR - 1


def _both_heads(x, masks):
    return jnp.concatenate([x * masks[0], x * masks[1]], axis=0)


def _own_heads(r, masks):
    half = r.shape[0] // 2
    return r[:half] * masks[0] + r[half:] * masks[1]


def _na_fwd(proj, bias):
    t = proj.shape[0]
    kw = NA_KR * GRID_W

    def body(q_ref, k_ref, v_ref, b_ref, o_ref, l_ref):
        masks = _head_masks()

        def step(i0, carry):
            idx = [i0 * NA_FWD_ROWS + j for j in range(NA_FWD_ROWS)]
            rows = [_na_row(i) for i in idx]
            qbs = [_both_heads(q_ref[pl.ds(r[0], GRID_W), :], masks).astype(BF16) for r in rows]
            kbs = [k_ref[pl.ds(r[1], kw), :].astype(BF16) for r in rows]
            ss = [_dot_nt(kb, qb) * SCALE + b_ref[r[2]] for kb, qb, r in zip(kbs, qbs, rows)]
            mxs = [jnp.max(s, axis=0, keepdims=True) for s in ss]
            ps = [jnp.exp(s - mx) for s, mx in zip(ss, mxs)]
            dens = [jnp.sum(p, axis=0, keepdims=True) for p in ps]
            pbs = [(p / den).astype(BF16) for p, den in zip(ps, dens)]
            vbs = [v_ref[pl.ds(r[1], kw), :].astype(BF16) for r in rows]
            outs = [_own_heads(_dot_tn(pb, vb), masks) for pb, vb in zip(pbs, vbs)]
            for j, r in enumerate(rows):
                o_ref[pl.ds(r[0], GRID_W), :] = outs[j]
                l_ref[pl.ds(idx[j], 1), :] = mxs[j] + jnp.log(dens[j])
            return carry

        lax.fori_loop(0, NA_ROWS // NA_FWD_ROWS, step, 0)

    c0 = QKV_A // 128
    return pl.pallas_call(
        body, name="na_fwd", grid=(t // S, 4),
        in_specs=[BS((S, 128), lambda b, hp: (b, c0 + hp)), BS((S, 128), lambda b, hp: (b, c0 + 4 + hp)),
                  BS((S, 128), lambda b, hp: (b, c0 + 8 + hp)),
                  BS((None, NA_KR, kw, 128), lambda b, hp: (hp, 0, 0, 0))],
        out_specs=[BS((S, 128), lambda b, hp: (b, hp)), BS((None, None, NA_ROWS, 128), lambda b, hp: (b, hp, 0, 0))],
        out_shape=[SDS((t, 512), F32), SDS((t // S, 4, NA_ROWS, 128), F32)],
        compiler_params=_cp("parallel", "parallel"),
    )(proj, proj, proj, bias)


def _na_bwd(proj, bias, dyb, yb, lse):
    t = proj.shape[0]
    kw = NA_KR * GRID_W

    def body(q_ref, k_ref, v_ref, b_ref, do_ref, o_ref, l_ref, d_ref, db_ref):
        masks = _head_masks()
        ones = jnp.ones((8, 128), BF16)

        @pl.when(pl.program_id(1) == 0)
        def _():
            db_ref[...] = jnp.zeros_like(db_ref)

        d_ref[1:3] = jnp.zeros((2, S, 128), F32)

        def row_sums(x):
            hi = x.astype(BF16)
            lo = (x - hi.astype(F32)).astype(BF16)
            return (_dot_nt(ones, hi) + _dot_nt(ones, lo))[0:1]

        def step(i0, carry):
            idx = [i0 * NA_BWD_ROWS + j for j in range(NA_BWD_ROWS)]
            rows = [_na_row(i) for i in idx]
            q_ds = [pl.ds(r[0], GRID_W) for r in rows]
            k_ds = [pl.ds(r[1], kw) for r in rows]
            qbs = [_both_heads(q_ref[r, :], masks).astype(BF16) for r in q_ds]
            kbs = [k_ref[r, :].astype(BF16) for r in k_ds]
            vbs = [v_ref[r, :].astype(BF16) for r in k_ds]
            dos = [do_ref[r, :] for r in q_ds]
            dobs = [_both_heads(do, masks).astype(BF16) for do in dos]
            deltas = [row_sums(_both_heads(do * o_ref[r, :], masks)) for do, r in zip(dos, q_ds)]
            ss = [_dot_nt(kb, qb) * SCALE + b_ref[r[2]] for kb, qb, r in zip(kbs, qbs, rows)]
            ps = [jnp.exp(s - l_ref[pl.ds(i, 1), :]) for s, i in zip(ss, idx)]
            dps = [_dot_nt(vb, dob) for vb, dob in zip(vbs, dobs)]
            dss = [p * (dp - delta) for p, dp, delta in zip(ps, dps, deltas)]
            for ds, r in zip(dss, rows):
                db_ref[r[2]] += ds
            dsbs = [ds.astype(BF16) for ds in dss]
            dks = [_dot_nn(dsb, qb) for dsb, qb in zip(dsbs, qbs)]
            dvs = [_dot_nn(p.astype(BF16), dob) for p, dob in zip(ps, dobs)]
            dqs = [_own_heads(_dot_tn(dsb, kb), masks) for dsb, kb in zip(dsbs, kbs)]
            for j in range(NA_BWD_ROWS):
                d_ref[0, q_ds[j], :] = dqs[j] * SCALE
                d_ref[1, k_ds[j], :] += dks[j] * SCALE
                d_ref[2, k_ds[j], :] += dvs[j]
            return carry

        lax.fori_loop(0, NA_ROWS // NA_BWD_ROWS, step, 0)

    c0 = QKV_A // 128
    own = BS((S, 128), lambda hp, b: (b, hp))
    tab = BS((None, NA_KR, kw, 128), lambda hp, b: (hp, 0, 0, 0))
    return pl.pallas_call(
        body, name="na_bwd", grid=(4, t // S),
        in_specs=[BS((S, 128), lambda hp, b: (b, c0 + hp)), BS((S, 128), lambda hp, b: (b, c0 + 4 + hp)),
                  BS((S, 128), lambda hp, b: (b, c0 + 8 + hp)), tab, own, own,
                  BS((None, None, NA_ROWS, 128), lambda hp, b: (b, hp, 0, 0))],
        out_specs=[BS((3, S, 128), lambda hp, b: (0, b, hp)), tab],
        out_shape=[SDS((3, t, 512), F32), SDS((4, NA_KR, kw, 128), F32)],
        compiler_params=_cp("parallel", "arbitrary"),
    )(proj, proj, proj, bias, dyb, yb, lse)


def _na_dbias_lane_map():
    kw = NA_KR * GRID_W
    lane = np.arange(kw)
    blk, m = lane // GRID_W, lane % GRID_W
    target = np.full(kw, -1)
    target[m < 16] = (blk * 32 + 15 + m)[m < 16]
    target[m >= 49] = (((blk + 1) % NA_KR) * 32 + m - 49)[m >= 49]
    return jnp.asarray(target[:, None] == np.arange(kw)[None, :], BF16)


def _na_dbias(db):
    kw = NA_KR * GRID_W

    def body(x_ref, map_ref, o_ref, z_ref):
        for cls in range(NA_KR):
            xt = x_ref[cls].T
            for h in range(2):
                xv = xt[GRID_W * h:GRID_W * (h + 1)]
                y = xv[0:8]
                for g in range(1, GRID_W // 8):
                    y = y + pltpu.roll(xv[8 * g:8 * g + 8], kw - 8 * g, 1)
                d = y[0:1]
                for s in range(1, 8):
                    d = d + pltpu.roll(y[s:s + 1], kw - s, 1)
                z_ref[h, cls:cls + 1, :] = d
        for h in range(2):
            z = z_ref[h]
            hi = z.astype(BF16)
            lo = (z - hi.astype(F32)).astype(BF16)
            e = _dot_nn(hi, map_ref[...]) + _dot_nn(lo, map_ref[...])
            out = e[0:1]
            for cls in range(1, NA_KR):
                out = out + pltpu.roll(e[cls:cls + 1], 32 * cls, 1)
            o_ref[h] = jnp.broadcast_to(out, (8, kw))

    return pl.pallas_call(
        body, name="na_dbias", grid=(4,),
        in_specs=[BS((None, NA_KR, kw, 128), lambda hp: (hp, 0, 0, 0)), BS((kw, kw), lambda hp: (0, 0))],
        out_specs=BS((2, 8, kw), lambda hp: (hp, 0, 0)), out_shape=SDS((8, 8, kw), F32),
        scratch_shapes=[pltpu.VMEM((2, 8, kw), F32)], compiler_params=_cp("parallel"),
    )(db, _na_dbias_lane_map())


def _merge_fwd(ya, yb, proj, wat, wbt):
    t = ya.shape[0]
    tm, tn = 2048, 256
    ca = (QKV_A + QKV_B) // tn
    cb = ca + D // tn

    def body(ya_ref, yb_ref, la_ref, lb_ref, wa_ref, wb_ref, m_ref, za_ref, zb_ref):
        za = _dot_nt(ya_ref[...].astype(BF16), wa_ref[...])
        zb = _dot_nt(yb_ref[...].astype(BF16), wb_ref[...])
        m_ref[...] = (jax.nn.sigmoid(la_ref[...]) * za + jax.nn.sigmoid(lb_ref[...]) * zb).astype(BF16)
        za_ref[...] = za.astype(BF16)
        zb_ref[...] = zb.astype(BF16)

    out = BS((tm, tn), lambda i, j: (i, j))
    return pl.pallas_call(
        body, name="merge_fwd", grid=(t // tm, D // tn),
        in_specs=[BS((tm, 256), lambda i, j: (i, 0)), BS((tm, 512), lambda i, j: (i, 0)),
                  BS((tm, tn), lambda i, j: (i, ca + j)), BS((tm, tn), lambda i, j: (i, cb + j)),
                  BS((tn, 256), lambda i, j: (j, 0)), BS((tn, 512), lambda i, j: (j, 0))],
        out_specs=[out, out, out], out_shape=[SDS((t, D), BF16)] * 3,
        compiler_params=_cp("parallel", "parallel"),
    )(ya, yb, proj, proj, wat, wbt)


def _merge_bwd(dxo, wo, za, zb, proj):
    t = dxo.shape[0]
    tm, tn = 2048, 256
    ca = (QKV_A + QKV_B) // tn
    cb = ca + D // tn

    def body(d_ref, w_ref, za_ref, zb_ref, la_ref, lb_ref, dza_ref, dzb_ref, dl_ref):
        dmv = _dot_nt(d_ref[...], w_ref[...])
        ga = jax.nn.sigmoid(la_ref[...])
        gb = jax.nn.sigmoid(lb_ref[...])
        dza_ref[...] = (dmv * ga).astype(BF16)
        dzb_ref[...] = (dmv * gb).astype(BF16)
        dl_ref[0] = (dmv * za_ref[...].astype(F32) * ga * (1.0 - ga)).astype(BF16)
        dl_ref[1] = (dmv * zb_ref[...].astype(F32) * gb * (1.0 - gb)).astype(BF16)

    blk = BS((tm, tn), lambda i, j: (i, j))
    return pl.pallas_call(
        body, name="merge_bwd", grid=(t // tm, D // tn),
        in_specs=[BS((tm, D), lambda i, j: (i, 0)), BS((tn, D), lambda i, j: (j, 0)), blk, blk,
                  BS((tm, tn), lambda i, j: (i, ca + j)), BS((tm, tn), lambda i, j: (i, cb + j))],
        out_specs=[blk, blk, BS((2, tm, tn), lambda i, j: (0, i, j))],
        out_shape=[SDS((t, D), BF16), SDS((t, D), BF16), SDS((2, t, D), BF16)],
        compiler_params=_cp("parallel", "parallel"),
    )(dxo, wo, za, zb, proj, proj)


def _adamw_update(w, g, m, v):
    mn = ADAM_B1 * m + (1.0 - ADAM_B1) * g
    vn = ADAM_B2 * v + (1.0 - ADAM_B2) * (g * g)
    m_hat = mn / (1.0 - ADAM_B1 ** ADAM_STEP)
    v_hat = vn / (1.0 - ADAM_B2 ** ADAM_STEP)
    return -ADAM_LR * (m_hat / (jnp.sqrt(v_hat) + ADAM_EPS) + ADAM_WD * w), mn, vn


def _sum_adamw(recv0, recv1, w, m, v, tag):
    _, r, c = recv0.shape
    tr = max(rows for rows in range(16, r + 1, 16) if r % rows == 0 and rows * c <= 384 * 1024)

    def body(a_ref, b_ref, w_ref, m_ref, v_ref, g_ref, d_ref, mo_ref, vo_ref):
        def update(ref):
            g = ref[0].astype(F32)
            for s in range(1, N_DEV):
                g = g + ref[s].astype(F32)
            g_ref[...] = g
            d_ref[...], mo_ref[...], vo_ref[...] = _adamw_update(w_ref[...], g, m_ref[...], v_ref[...])

        pl.when(pl.program_id(0) == 0)(lambda: update(a_ref))
        pl.when(pl.program_id(0) == 1)(lambda: update(b_ref))

    blk = BS((None, tr, c), lambda layer, i: (layer, i, 0))
    return pl.pallas_call(
        body, name=f"sum_adamw_{tag}", grid=(2, r // tr),
        in_specs=[BS((N_DEV, tr, c), lambda layer, i: (0, i * (1 - layer), 0)),
                  BS((N_DEV, tr, c), lambda layer, i: (0, i * layer, 0)), blk, blk, blk],
        out_specs=[blk] * 4, out_shape=[SDS((2, r, c), F32)] * 4, compiler_params=_cp("arbitrary", "arbitrary"),
    )(recv0, recv1, w, m, v)


def _adamw(w, g, m, v, tag):
    layers, r, c = w.shape
    tr = next(r // k for k in (1, 2, 4, 8) if r // k <= 384 and r % (8 * k) == 0)

    def body(w_ref, g_ref, m_ref, v_ref, d_ref, mo_ref, vo_ref):
        d_ref[...], mo_ref[...], vo_ref[...] = _adamw_update(w_ref[...], g_ref[...], m_ref[...], v_ref[...])

    blk = BS((None, tr, c), lambda l, i: (l, i, 0))
    return pl.pallas_call(
        body, name=f"adamw_{tag}", grid=(layers, r // tr), in_specs=[blk] * 4, out_specs=[blk] * 3,
        out_shape=[SDS((layers, r, c), F32)] * 3, compiler_params=_cp("parallel", "parallel"),
    )(w, g, m, v)


def _place():
    return lax.axis_index("x"), lax.axis_index("y"), lax.axis_index("c")


def _flip(coord, bit):
    return 1 - coord if bit else coord


def _peers(x, y, c):
    peers = []
    for mask in range(1, N_DEV):
        p = (_flip(x, mask & 4), _flip(y, mask & 2), _flip(c, mask & 1))
        peers.append((p, 4 * p[0] + 2 * p[1] + p[2]))
    return peers


def _copy_plan(mode, src, land, x, y, c):
    me = 4 * x + 2 * y + c

    def device(mask):
        p = (_flip(x, mask & 4), _flip(y, mask & 2), _flip(c, mask & 1))
        return p, 4 * p[0] + 2 * p[1] + p[2]

    if mode == "scatter":
        r = land.shape[1]
        return [(p, src.at[pl.ds(i * r, r), :], land.at[me], land.at[i])
                for p, i in map(device, (1, 2, 3, 4, 5, 6, 7, 0))]
    r = land.shape[0] // N_DEV

    def rows(i):
        return land.at[pl.ds(i * r, r), :]

    if mode == "gather":
        return [(p, src, rows(me), rows(i)) for p, i in map(device, (1, 4, 2, 6, 0))]
    sibling = device(1)[0]
    return [(sibling, rows(device(m)[1]), rows(device(m)[1]), rows(device(m | 1)[1])) for m in (4, 2, 6)]


COPIES = dict(scatter=8, gather=5, forward=3)
HBM_SPEC = BS(memory_space=pltpu.HBM)
SEM_SPEC = BS(memory_space=pltpu.SEMAPHORE)
DATAFLOW = pltpu.SideEffectType.DATAFLOW_SIDE_EFFECTING


def _fresh(shape, dtype, tag):
    def body(o_ref):
        del o_ref

    return pl.pallas_call(body, name=f"fresh_{tag}", out_specs=BS(memory_space=pl.ANY), out_shape=SDS(shape, dtype))()


def _exchange_start(mode, srcs, lands, after, tag):
    if lands is None and mode == "gather":
        lands = [_fresh((N_DEV * s.shape[0], s.shape[1]), s.dtype, f"{tag}_{a}") for a, s in enumerate(srcs)]
    elif lands is None:
        lands = [_fresh((N_DEV, s.shape[0] // N_DEV, s.shape[1]), s.dtype, f"{tag}_{a}") for a, s in enumerate(srcs)]
    n, n_src, n_cp = len(lands), len(srcs), COPIES[mode]
    behind = [] if after is None else [after]

    def body(*refs):
        src_refs, land_refs = refs[:n_src], refs[n_src:n_src + n]
        send_sems, recv_sems = refs[n_src + n + len(behind)], refs[n_src + n + len(behind) + 1]
        token = refs[-1]
        for a in range(n):
            plan = _copy_plan(mode, src_refs[a] if n_src else None, land_refs[a], *_place())
            for k, (p, out, there, _) in enumerate(plan):
                pltpu.make_async_remote_copy(
                    src_ref=out, dst_ref=there, send_sem=send_sems.at[n_cp * a + k],
                    recv_sem=recv_sems.at[n_cp * a + k], device_id=p, device_id_type=MESH).start()
        token[...] = jnp.zeros_like(token)

    both = [*srcs, *lands]
    res = pl.pallas_call(
        body, name=f"{mode}_start_{tag}",
        out_shape=(pltpu.SemaphoreType.DMA((n_cp * n,)), pltpu.SemaphoreType.DMA((n_cp * n,)),
                   *[pltpu.HBM(v.shape, v.dtype) for v in both], SDS((8, 128), F32)),
        in_specs=[HBM_SPEC] * len(both) + [BS(memory_space=pl.ANY)] * len(behind),
        out_specs=(SEM_SPEC, SEM_SPEC, *[HBM_SPEC] * len(both), BS(memory_space=pltpu.VMEM)),
        input_output_aliases={i: 2 + i for i in range(len(both))},
        compiler_params=pltpu.CompilerParams(has_side_effects=DATAFLOW),
    )(*[pltpu.with_memory_space_constraint(v, pltpu.HBM) for v in both], *behind)
    return (mode, res[0], res[1], res[2:2 + n_src], res[2 + n_src:2 + n_src + n]), res[-1]


def _exchange_wait(handle, after, tag, which=None):
    mode, send_sems, recv_sems, srcs, lands = handle
    which = list(range(len(lands))) if which is None else list(which)
    n_cp = COPIES[mode]
    lands = [lands[a] for a in which]
    srcs = [srcs[a] for a in which] if srcs else []
    n, n_src = len(lands), len(srcs)
    afters = list(after) if isinstance(after, (tuple, list)) else [after]

    def body(*refs):
        src_refs, land_refs = refs[:n_src], refs[n_src:n_src + n]
        send_ref, recv_ref = refs[n_src + n], refs[n_src + n + 1]
        for i, a in enumerate(which):
            plan = _copy_plan(mode, src_refs[i] if n_src else None, land_refs[i], *_place())
            for k, (p, out, _, here) in enumerate(plan):
                cp = pltpu.make_async_remote_copy(
                    src_ref=out, dst_ref=here, send_sem=send_ref.at[n_cp * a + k], recv_sem=recv_ref.at[n_cp * a + k],
                    device_id=p, device_id_type=MESH)
                cp.wait_send()
                cp.wait_recv()

    both = [*srcs, *lands]
    res = pl.pallas_call(
        body, name=f"{mode}_wait_{tag}", out_shape=tuple(pltpu.HBM(v.shape, v.dtype) for v in both),
        in_specs=[HBM_SPEC] * len(both) + [SEM_SPEC, SEM_SPEC] + [BS(memory_space=pl.ANY)] * len(afters),
        out_specs=tuple([HBM_SPEC] * len(both)),
        input_output_aliases={i: i for i in range(len(both))},
        compiler_params=pltpu.CompilerParams(has_side_effects=DATAFLOW),
    )(*both, send_sems, recv_sems, *afters)
    return list(res[n_src:])


def _allreduce_small(vec, behind):
    rows = vec.shape[0]

    def body(x_ref, behind_ref, o_ref, buf_ref, send_sems, recv_sems):
        x, y, c = _place()
        me = 4 * x + 2 * y + c
        buf_ref[me] = x_ref[...]
        peers = _peers(x, y, c)

        def copy(k, slot):
            return pltpu.make_async_remote_copy(
                src_ref=x_ref, dst_ref=buf_ref.at[slot], send_sem=send_sems.at[k], recv_sem=recv_sems.at[k],
                device_id=peers[k][0], device_id_type=MESH)

        sends = [copy(k, me) for k in range(N_DEV - 1)]
        for cp in sends:
            cp.start()
        for k in range(N_DEV - 1):
            copy(k, peers[k][1]).wait_recv()
        for cp in sends:
            cp.wait_send()
        acc = buf_ref[0]
        for s in range(1, N_DEV):
            acc = acc + buf_ref[s]
        o_ref[...] = acc

    vmem = BS(memory_space=pltpu.VMEM)
    return pl.pallas_call(
        body, name="allreduce_small", in_specs=[vmem, BS(memory_space=pl.ANY)], out_specs=vmem,
        out_shape=SDS((rows, 128), F32),
        scratch_shapes=[pltpu.VMEM((N_DEV, rows, 128), F32), pltpu.SemaphoreType.DMA((7,)),
                        pltpu.SemaphoreType.DMA((7,))],
        compiler_params=pltpu.CompilerParams(has_side_effects=True),
    )(vec, behind)


def _ffn_forward(x, hn, fetch, names, tag, next_g):
    gu, act = _ffn_up(hn, fetch(names[0], hn).reshape(2, F, D), tag)
    got = _mm_nn(act[None], fetch(names[1], act)[None], f"down_{tag}", res=x, scale=0.5, tm=512, next_g=next_g)
    out, hn_next = got if next_g is not None else (got, None)
    return out, hn_next, (x, hn, gu, act)


def _ffn_backward(dxo, dxo_b, saved, norm_g, wut, wd, tag, send):
    x, hn, gu, act = saved
    d_wd = _mm_tn(act[None], dxo_b, f"dwd_{tag}", scale=0.5)
    du = _ffn_dact(dxo_b, wd, gu, send(("down",), [d_wd]), tag)
    d_wut = _mm_tn(du, hn, f"dwu_{tag}")
    token = send(("up",), [d_wut])
    return _mm_nn_norm_bwd([du], wut.reshape(2 * F, D), x, norm_g + token[0, 0], dxo, tag)


def _mixer_forward(x, hn, fetch, bias, tables, tag, next_g):
    proj = _mm_nt_rows(hn, fetch("win", hn), f"proj_{tag}", 1024, IN_W // 2, IN_W, 0, rope=(*tables, 2 * QKV_A // 3))
    qkr = proj
    outs, lses = [], []
    for grp in range(3):
        o, l = _dil_fwd(qkr, proj, grp)
        outs.append(o)
        lses.append(l)
    ya = _combine_fwd(outs, lses)
    yb, lse_b = _na_fwd(proj, bias)
    merged, za, zb = _merge_fwd(ya, yb, proj, fetch("wa", yb), fetch("wb", yb))
    out, hn_next = _mm_nn(merged[None], fetch("wo", merged)[None], f"out_{tag}", res=x, next_g=next_g)
    return out, hn_next, (x, hn, proj, qkr, outs, lses, ya, yb, lse_b, merged, za, zb)


def _mixer_backward(dxo, dxo_b, saved, norm_g, w, bias, tables, tag, send):
    wint, wat, wbt, wo = w
    x, hn, proj, qkr, outs, lses, ya, yb, lse_b, merged, za, zb = saved
    d_wo = _mm_tn(merged[None], dxo_b, f"dwo_{tag}")
    dza, dzb, dlog = _merge_bwd(dxo_b, wo, za, zb, proj)
    dya = _mm_nn(dza[None], wat[None], f"dya_{tag}")
    dyb = _mm_nn(dzb[None], wbt[None], f"dyb_{tag}")
    d_wat = _mm_tn(dza[None], ya, f"dwa_{tag}")
    d_wbt = _mm_tn(dzb[None], yb, f"dwb_{tag}")
    cb = _combine_bwd(dya, outs, lses)
    dqs, dks, dvs = [], [], []
    for grp in range(3):
        dq, dk, dv = _dil_bwd(qkr, proj, cb[grp], cb[3 + grp], lses[grp], grp)
        dqs.append(dq)
        dks.append(dk)
        dvs.append(dv)
    d_qkv_b, dbias_tab = _na_bwd(proj, bias, dyb, yb, lse_b)
    dbias = _na_dbias(dbias_tab)
    dproj = [_rope_bwd(dqs, dks, dvs, *tables), d_qkv_b, dlog]
    d_wint, row = None, 0
    for i, p in enumerate(dproj):
        d_wint = _mm_tn(p, hn, f"dwin{i}_{tag}", into=d_wint, row0=row, rows=IN_W)
        row += p.shape[0] * p.shape[2]
    token = send(("win", "wa", "wb", "wo"), [d_wint, d_wat, d_wbt, d_wo])
    dx, dx_b, dg = _mm_nn_norm_bwd(dproj, wint, x, norm_g + token[0, 0], dxo, f"mix_{tag}", tm=512, single_w=True)
    dbias = dbias[:, 0, :480].reshape(8, 15, 32)[:, :, :31]
    return dx, dx_b, dg, dbias


def _pack_small(norms, biases, final, loss=None):
    parts = []
    for layer in range(DEPTH):
        parts += [norms[0][layer], norms[1][layer], norms[2][layer],
                  jnp.pad(biases[layer].reshape(-1), (0, BIAS_PAD - 8 * 15 * 31))]
    parts.append(final)
    flat = jnp.concatenate([p.reshape(-1).astype(F32) for p in parts])
    if loss is not None:
        flat = jnp.concatenate([flat, loss.reshape(-1)])
    return jnp.pad(flat, (0, SMALL_ROWS * 128 - flat.shape[0])).reshape(SMALL_ROWS, 128)


def _unpack_small(packed):
    flat = packed.reshape(-1)
    norms, biases = ([], [], []), []
    pos = 0
    for _ in range(DEPTH):
        for k in range(3):
            norms[k].append(flat[pos:pos + D])
            pos += D
        biases.append(flat[pos:pos + 8 * 15 * 31].reshape(8, 15, 31))
        pos += BIAS_PAD
    final = flat[pos:pos + D]
    pos += D
    return [jnp.stack(n) for n in norms], jnp.stack(biases), final, flat[pos]


def kernel(x, ffn1_norm, ffn1_w_up, ffn1_w_down, mix_norm, w_in, na_rel_bias, w_branch_a, w_branch_b, w_out, ffn2_norm, ffn2_w_up, ffn2_w_down, final_norm, loss_target, m_ffn1_norm, m_ffn1_w_up, m_ffn1_w_down, m_mix_norm, m_w_in, m_na_rel_bias, m_w_branch_a, m_w_branch_b, m_w_out, m_ffn2_norm, m_ffn2_w_up, m_ffn2_w_down, m_final_norm, v_ffn1_norm, v_ffn1_w_up, v_ffn1_w_down, v_mix_norm, v_w_in, v_na_rel_bias, v_w_branch_a, v_w_branch_b, v_w_out, v_ffn2_norm, v_ffn2_w_up, v_ffn2_w_down, v_final_norm):
    t = x.shape[0] * x.shape[1]
    xs = x.reshape(t, D)
    tgt = loss_target.reshape(t, D)
    tables = _rope_tables()

    col_sharded = dict(up1=ffn1_w_up, win=w_in, wa=w_branch_a, wb=w_branch_b, up2=ffn2_w_up)
    row_sharded = dict(down1=ffn1_w_down, wo=w_out, down2=ffn2_w_down)
    shard = [{} for _ in range(DEPTH)]
    for layer in range(DEPTH):
        for name, arr in col_sharded.items():
            shard[layer][name] = arr[layer].T.astype(BF16)
        for name, arr in row_sharded.items():
            shard[layer][name] = arr[layer].astype(BF16)

    weights = [{} for _ in range(DEPTH)]
    travel = [(0, ("up1",)), (0, ("down1",)), (0, ("win",)), (0, ("wa", "wb", "wo")), (0, ("up2", "down2")),
              (1, ("up1", "down1")), (1, ("win",)), (1, ("wa", "wb", "wo")), (1, ("up2", "down2"))]
    group_of, chips_done, sibling_done, passing = {}, {}, {}, {}
    count = 0
    for i, (layer, names) in enumerate(travel):
        chips_done[i] = list(range(count, count + len(names)))
        count += len(names)
        for n in names:
            group_of[layer, n] = (i, names)
    gathered, token = _exchange_start(
        "gather", [shard[layer][n] for layer, names in travel for n in names], None, None, "w")
    zero = token[0, 0]

    biases = [_na_bias_table(na_rel_bias[layer] + zero) for layer in range(DEPTH)]

    def pass_on(i, behind):
        if i in chips_done:
            lands = _exchange_wait(gathered, behind, f"w{i}", which=chips_done.pop(i))
            sibling_done[i], passing[i] = _exchange_start("forward", [], lands, None, f"p{i}")

    def fetcher(layer):
        def fetch(name, behind):
            if (layer, name) in group_of:
                i, names = group_of[layer, name]
                if i == 0:
                    behind = (behind, *biases)
                behind = behind if isinstance(behind, tuple) else (behind,)
                pass_on(i, behind)
                if i > 0:
                    pass_on(i + 1, behind)
                if i + 1 in passing:
                    behind = (*behind, passing[i + 1])
                for n, got in zip(names, _exchange_wait(sibling_done.pop(i), behind, f"p{i}")):
                    weights[layer][n] = got
                    del group_of[layer, n]
            return weights[layer][name]
        return fetch

    saved = []
    h = xs
    hn = _norm_fwd(xs, ffn1_norm[0] + zero, "first")
    for layer in range(DEPTH):
        bias = biases[layer]
        fetch = fetcher(layer)
        after_ffn2 = ffn1_norm[layer + 1] if layer + 1 < DEPTH else None
        h, hn, s1 = _ffn_forward(h, hn, fetch, ("up1", "down1"), f"f1l{layer}", mix_norm[layer])
        h, hn, s2 = _mixer_forward(h, hn, fetch, bias, tables, f"l{layer}", ffn2_norm[layer])
        h, hn, s3 = _ffn_forward(h, hn, fetch, ("up2", "down2"), f"f2l{layer}", after_ffn2)
        saved.append((s1, s2, s3, bias))
    loss_part, dh, dh_b, d_final = _loss_head(h, final_norm, tgt)

    d_norms = ([None] * DEPTH, [None] * DEPTH, [None] * DEPTH)
    d_bias = [None] * DEPTH
    sent = {}

    def sender(layer, suffix):
        def send(names, grads):
            tag = f"g{layer}{names[0]}{suffix}"
            handle, token = _exchange_start("scatter", grads, None, None, tag)
            for i, n in enumerate(names):
                sent[layer, n + suffix] = (handle, i, tag)
            return token
        return send

    for layer in reversed(range(DEPTH)):
        w = weights[layer]
        s1, s2, s3, bias = saved[layer]
        dh, dh_b, d_norms[2][layer] = _ffn_backward(
            dh, dh_b, s3, ffn2_norm[layer], w["up2"].reshape(2, F, D), w["down2"], f"f2l{layer}", sender(layer, "2"))
        dh, dh_b, d_norms[1][layer], d_bias[layer] = _mixer_backward(
            dh, dh_b, s2, mix_norm[layer], (w["win"], w["wa"], w["wb"], w["wo"]), bias, tables, f"l{layer}",
            sender(layer, ""))
        dh, dh_b, d_norms[0][layer] = _ffn_backward(
            dh, dh_b, s1, ffn1_norm[layer], w["up1"].reshape(2, F, D), w["down1"], f"f1l{layer}", sender(layer, "1"))
    grad_x = dh.reshape(x.shape)

    originals = dict(up1=(ffn1_w_up, m_ffn1_w_up, v_ffn1_w_up), down1=(ffn1_w_down, m_ffn1_w_down, v_ffn1_w_down),
                     win=(w_in, m_w_in, v_w_in), wa=(w_branch_a, m_w_branch_a, v_w_branch_a),
                     wb=(w_branch_b, m_w_branch_b, v_w_branch_b), wo=(w_out, m_w_out, v_w_out),
                     up2=(ffn2_w_up, m_ffn2_w_up, v_ffn2_w_up), down2=(ffn2_w_down, m_ffn2_w_down, v_ffn2_w_down))
    big = {}
    behind = dh
    landed = {}

    def received(layer, name):
        handle, i, tag = sent[layer, name]
        if tag not in landed:
            landed[tag] = _exchange_wait(handle, behind, tag)
        return landed[tag][i]

    for name in ("down2", "up2", "win", "wa", "wb", "wo", "down1", "up1"):
        if name == "up1":
            small = _allreduce_small(_pack_small(d_norms, d_bias, d_final, loss_part[0, :1]), behind)
            g_norms, g_bias, g_final, loss = _unpack_small(small)
            w_small = _pack_small((ffn1_norm, mix_norm, ffn2_norm), na_rel_bias, final_norm)
            m_small = _pack_small((m_ffn1_norm, m_mix_norm, m_ffn2_norm), m_na_rel_bias, m_final_norm)
            v_small = _pack_small((v_ffn1_norm, v_mix_norm, v_ffn2_norm), v_na_rel_bias, v_final_norm)
            upd = _adamw(w_small[None], small[None], m_small[None], v_small[None], "small")
            small_out = [(g_norms, g_bias, g_final)] + [_unpack_small(u[0])[:3] for u in upd]
            behind = upd[0]
        wv, mv, vv = originals[name]
        if name in col_sharded:
            wv, mv, vv = (jnp.swapaxes(t, 1, 2) for t in (wv, mv, vv))
        big[name] = tuple(_sum_adamw(received(0, name), received(1, name), wv, mv, vv, name))
        behind = big[name][1]
        if name in col_sharded:
            big[name] = tuple(jnp.swapaxes(t, 1, 2) for t in big[name])

    outputs = [loss, grad_x]
    for kind in range(4):
        norms, bias_k, final_k = small_out[kind]
        outputs += [norms[0], big["up1"][kind], big["down1"][kind], norms[1], big["win"][kind], bias_k,
                    big["wa"][kind], big["wb"][kind], big["wo"][kind], norms[2], big["up2"][kind],
                    big["down2"][kind], final_k]
    return tuple(outputs)
```

```python
import numpy as np

import jax
import jax.numpy as jnp
from jax import lax
from jax.experimental import pallas as pl
from jax.experimental.pallas import tpu as pltpu

F32 = jnp.float32
BF16 = jnp.bfloat16
SDS = jax.ShapeDtypeStruct
BS = pl.BlockSpec
MESH = pl.DeviceIdType.MESH

D = 1024
S = 2048
F = 2816
DEPTH = 2
HEAD_DIM = 64
DILATIONS = (1, 4, 16)
HALF = 64
QKV_A = 2304
QKV_B = 1536
IN_W = 5888
N_DEV = 8
NA_ROWS = 32
GRID_W = 64
NA_KR = 8
ROPE_THETA = 10000.0
RMS_EPS = 1e-6
NEG = -1e30
SCALE = HEAD_DIM ** -0.5
ADAM_LR, ADAM_B1, ADAM_B2, ADAM_EPS, ADAM_WD, ADAM_STEP = 0.001, 0.9, 0.999, 1e-08, 0.01, 10
VMEM_LIMIT_V7X = 52 * 1024 * 1024
SMALL_ROWS = 120
BIAS_PAD = 3840
NA_FWD_ROWS = 8
NA_BWD_ROWS = 4
DIL_FWD_TILES = 8
DIL_BWD_TILES = 4


def _cp(*sem):
    return pltpu.CompilerParams(dimension_semantics=sem, vmem_limit_bytes=VMEM_LIMIT_V7X)


def _dot_nn(a, b):
    return jnp.dot(a, b, preferred_element_type=F32)


def _dot_nt(a, b):
    return lax.dot_general(a, b, (((1,), (1,)), ((), ())), preferred_element_type=F32)


def _dot_tn(a, b):
    return lax.dot_general(a, b, (((0,), (0,)), ((), ())), preferred_element_type=F32)


def _ds(start, size, stride):
    return pl.ds(start, size) if stride == 1 else pl.ds(start, size, stride=stride)


def _norm_fwd(x, g, tag):
    t = x.shape[0]
    tm = 512

    def body(x_ref, g_ref, o_ref):
        xv = x_ref[...]
        r = lax.rsqrt(jnp.mean(xv * xv, axis=-1, keepdims=True) + RMS_EPS)
        o_ref[...] = (xv * r * g_ref[...]).astype(BF16)

    return pl.pallas_call(
        body, name=f"norm_fwd_{tag}", grid=(t // tm,),
        in_specs=[BS((tm, D), lambda i: (i, 0)), BS((1, D), lambda i: (0, 0))],
        out_specs=BS((tm, D), lambda i: (i, 0)),
        out_shape=SDS((t, D), BF16), compiler_params=_cp("parallel"),
    )(x, g.reshape(1, D))


def _loss_head(x, g, tgt):
    t = x.shape[0]
    tm = 1024

    def body(x_ref, g_ref, t_ref, loss_ref, dx_ref, dxb_ref, dg_ref):
        @pl.when(pl.program_id(0) == 0)
        def _():
            dg_ref[...] = jnp.zeros_like(dg_ref)
            loss_ref[...] = jnp.zeros_like(loss_ref)

        xv = x_ref[...]
        gv = g_ref[...]
        r = lax.rsqrt(jnp.mean(xv * xv, axis=-1, keepdims=True) + RMS_EPS)
        xh = xv * r
        e = xh * gv - t_ref[...]
        loss_ref[...] += 0.5 * jnp.sum(jnp.mean(e * e, axis=-1, keepdims=True), axis=0, keepdims=True)
        dy = e * (1.0 / D)
        u = dy * gv
        dx = r * (u - xh * jnp.mean(xh * u, axis=-1, keepdims=True))
        dx_ref[...] = dx
        dxb_ref[...] = dx.astype(BF16)
        dg_ref[...] += jnp.sum(dy * xh, axis=0, keepdims=True)

    row = BS((tm, D), lambda i: (i, 0))
    vec = BS((1, D), lambda i: (0, 0))
    return pl.pallas_call(
        body, name="loss_head", grid=(t // tm,),
        in_specs=[row, vec, row], out_specs=[BS((1, 128), lambda i: (0, 0)), row, row, vec],
        out_shape=[SDS((1, 128), F32), SDS((t, D), F32), SDS((t, D), BF16), SDS((1, D), F32)],
        compiler_params=_cp("arbitrary"),
    )(x, g.reshape(1, D), tgt)


def _mm_nn(a, w, tag, res=None, scale=1.0, tm=1024, tn=None, next_g=None):
    c_n, t, k = a.shape
    n = w.shape[2]
    tn = n if tn is None else tn
    assert next_g is None or tn == n
    n_in = 2 + (res is not None) + (next_g is not None)

    def body(*refs):
        a_ref, w_ref = refs[0], refs[1]
        acc = _dot_nn(a_ref[0].astype(BF16), w_ref[0])
        for c in range(1, c_n):
            acc = acc + _dot_nn(a_ref[c].astype(BF16), w_ref[c])
        if scale != 1.0:
            acc = acc * scale
        if res is not None:
            acc = refs[2][...] + acc
        refs[n_in][...] = acc
        if next_g is not None:
            r = lax.rsqrt(jnp.mean(acc * acc, axis=-1, keepdims=True) + RMS_EPS)
            refs[n_in + 1][...] = (acc * r * refs[n_in - 1][...]).astype(BF16)

    w_mode = dict(pipeline_mode=pl.Buffered(1)) if tn == n else {}
    in_specs = [BS((c_n, tm, k), lambda i, j: (0, i, 0)), BS((c_n, k, tn), lambda i, j: (0, 0, j), **w_mode)]
    args = [a, w]
    out_specs = [BS((tm, tn), lambda i, j: (i, j))]
    out_shape = [SDS((t, n), F32)]
    if res is not None:
        in_specs.append(BS((tm, tn), lambda i, j: (i, j)))
        args.append(res)
    if next_g is not None:
        in_specs.append(BS((1, n), lambda i, j: (0, 0)))
        args.append(next_g.reshape(1, n))
        out_specs.append(BS((tm, tn), lambda i, j: (i, j)))
        out_shape.append(SDS((t, n), BF16))
    got = pl.pallas_call(
        body, name=f"mm_nn_{tag}", grid=(t // tm, n // tn), in_specs=in_specs, out_specs=out_specs,
        out_shape=out_shape, compiler_params=_cp("parallel", "parallel"),
    )(*args)
    return got if next_g is not None else got[0]


def _mm_nn_norm_bwd(parts, w, x, g, dres, tag, tm=256, single_w=False):
    t = parts[0].shape[1]
    n_parts = len(parts)

    def body(*refs):
        w_ref, x_ref, g_ref, dr_ref, dx_ref, dxb_ref, dg_ref = refs[n_parts:]

        @pl.when(pl.program_id(0) == 0)
        def _():
            dg_ref[...] = jnp.zeros_like(dg_ref)

        dh = None
        row = 0
        for a_ref, part in zip(refs, parts):
            for c in range(part.shape[0]):
                term = _dot_nn(a_ref[c].astype(BF16), w_ref[row:row + part.shape[2], :])
                dh = term if dh is None else dh + term
                row += part.shape[2]
        xv = x_ref[...]
        r = lax.rsqrt(jnp.mean(xv * xv, axis=-1, keepdims=True) + RMS_EPS)
        xh = xv * r
        u = dh * g_ref[...]
        dx = dr_ref[...] + r * (u - xh * jnp.mean(xh * u, axis=-1, keepdims=True))
        dx_ref[...] = dx
        dxb_ref[...] = dx.astype(BF16)
        dg_ref[...] += jnp.sum(dh * xh, axis=0, keepdims=True)

    row = BS((tm, D), lambda i: (i, 0))
    vec = BS((1, D), lambda i: (0, 0))
    return pl.pallas_call(
        body, name=f"mm_nn_norm_bwd_{tag}", grid=(t // tm,),
        in_specs=[BS((p.shape[0], tm, p.shape[2]), lambda i: (0, i, 0)) for p in parts]
        + [BS(w.shape, lambda i: (0, 0), **(dict(pipeline_mode=pl.Buffered(1)) if single_w else {})), row, vec, row],
        out_specs=[row, row, vec], out_shape=[SDS((t, D), F32), SDS((t, D), BF16), SDS((1, D), F32)],
        compiler_params=_cp("arbitrary"),
    )(*parts, w, x, g.reshape(1, D), dres)


def _mm_nt_rows(a, w, tag, tm, tn, n_total, w_row0, rope=None):
    t, k = a.shape
    assert w_row0 % tn == 0 and n_total % tn == 0
    j0 = w_row0 // tn

    def body(a_ref, w_ref, *rest):
        o_ref = rest[-1]
        o_ref[...] = _dot_nt(a_ref[...].astype(BF16), w_ref[...])
        if rope is not None:
            @pl.when(pl.program_id(0) == 0)
            def _():
                c = rest[0][...]
                sg = rest[1][...]
                first = (lax.broadcasted_iota(jnp.int32, (tm, 128), 1) % HEAD_DIM) < HEAD_DIM // 2
                for col in range(0, rope[2], 128):
                    v = o_ref[:, col:col + 128]
                    o_ref[:, col:col + 128] = v * c + _swap_halves(v, first) * sg

    in_specs = [BS((tm, k), lambda j, i: (i, 0)), BS((tn, k), lambda j, i: (j0 + j, 0))]
    args = [a, w]
    if rope is not None:
        assert rope[2] <= tn
        in_specs += [BS((tm, 128), lambda j, i: (i % (S // tm), 0))] * 2
        args += [rope[0], rope[1]]
    return pl.pallas_call(
        body, name=f"mm_nt_{tag}", grid=(n_total // tn, t // tm), in_specs=in_specs,
        out_specs=BS((tm, tn), lambda j, i: (i, j)), out_shape=SDS((t, n_total), F32),
        compiler_params=_cp("parallel", "parallel"),
    )(*args)


def _mm_tn(a, b, tag, scale=1.0, tmm=None, into=None, row0=0, rows=None):
    c_n, t, m = a.shape
    n = b.shape[1]
    if tmm is None:
        tmm = max(w for w in (1408, 768, 512, 256) if m % w == 0 and row0 % w == 0)
    tiles = m // tmm
    block0 = row0 // tmm
    assert row0 % tmm == 0 and m % tmm == 0

    def body(a_ref, b_ref, *rest):
        rest[-1][...] = (_dot_tn(a_ref[...].astype(BF16), b_ref[...].astype(BF16)) * scale).astype(BF16)

    in_specs = [BS((None, t, tmm), lambda c, mi: (c, 0, mi)), BS((t, n), lambda c, mi: (0, 0))]
    args = [a, b]
    if into is not None:
        in_specs.append(BS(memory_space=pl.ANY))
        args.append(into)
    return pl.pallas_call(
        body, name=f"mm_tn_{tag}", grid=(c_n, tiles), in_specs=in_specs,
        out_specs=BS((tmm, n), lambda c, mi: (block0 + c * tiles + mi, 0)),
        out_shape=SDS((rows or c_n * m, n) if into is None else into.shape, BF16),
        input_output_aliases={} if into is None else {2: 0},
        compiler_params=_cp("parallel", "parallel"),
    )(*args)


def _ffn_up(hn, wut, tag):
    t = hn.shape[0]
    tm, tn = 1024, 1408

    def body(h_ref, w_ref, gu_ref, act_ref):
        h = h_ref[...]
        g = _dot_nt(h, w_ref[0])
        u = _dot_nt(h, w_ref[1])
        sg = jax.nn.sigmoid(g)
        silu = g * sg
        gu_ref[0] = (u * (sg + silu * (1.0 - sg))).astype(BF16)
        gu_ref[1] = silu.astype(BF16)
        act_ref[...] = (silu * u).astype(BF16)

    return pl.pallas_call(
        body, name=f"ffn_up_{tag}", grid=(F // tn, t // tm),
        in_specs=[BS((tm, D), lambda j, i: (i, 0)), BS((2, tn, D), lambda j, i: (0, j, 0))],
        out_specs=[BS((2, tm, tn), lambda j, i: (0, i, j)), BS((tm, tn), lambda j, i: (i, j))],
        out_shape=[SDS((2, t, F), BF16), SDS((t, F), BF16)],
        compiler_params=_cp("parallel", "parallel"),
    )(hn, wut)


def _ffn_dact(dxo, wd, gu, tie, tag):
    t = dxo.shape[0]
    tm, tn = 1024, 1408

    def body(d_ref, w_ref, gu_ref, tie_ref, o_ref):
        dact = _dot_nt(d_ref[...] * 0.5, w_ref[...])
        o_ref[0] = (dact * gu_ref[0].astype(F32)).astype(BF16)
        o_ref[1] = (dact * gu_ref[1].astype(F32)).astype(BF16)

    return pl.pallas_call(
        body, name=f"ffn_dact_{tag}", grid=(F // tn, t // tm),
        in_specs=[BS((tm, D), lambda j, i: (i, 0)), BS((tn, D), lambda j, i: (j, 0)),
                  BS((2, tm, tn), lambda j, i: (0, i, j)), BS((8, 128), lambda j, i: (0, 0))],
        out_specs=BS((2, tm, tn), lambda j, i: (0, i, j)),
        out_shape=SDS((2, t, F), BF16), compiler_params=_cp("parallel", "parallel"),
    )(dxo, wd, gu, tie)


def _rope_tables():
    half = HEAD_DIM // 2
    inv_freq = ROPE_THETA ** (-jnp.arange(half, dtype=F32) / half)
    ang = jnp.arange(S).astype(F32)[:, None] * inv_freq[None, :]
    cos, sin = jnp.cos(ang), jnp.sin(ang)
    return jnp.concatenate([cos, cos, cos, cos], axis=1), jnp.concatenate([-sin, sin, -sin, sin], axis=1)


def _swap_halves(t, first_half):
    return jnp.where(first_half, pltpu.roll(t, 96, 1), pltpu.roll(t, 32, 1))


def _rope_bwd(dqs, dks, dvs, cos_t, sin_t):
    t = dqs[0].shape[0]
    tm = 1024

    def body(*refs):
        c = refs[9][...]
        sg = refs[10][...]
        o_ref = refs[11]
        first = (lax.broadcasted_iota(jnp.int32, (tm, 128), 1) % HEAD_DIM) < HEAD_DIM // 2
        for a in range(6):
            for hp in range(2):
                v = refs[a][:, 128 * hp:128 * (hp + 1)]
                col = 128 * (2 * a + hp)
                o_ref[:, col:col + 128] = (v * c + _swap_halves(v * sg, first)).astype(BF16)
        for a in range(6, 9):
            o_ref[:, 256 * a:256 * (a + 1)] = refs[a][...].astype(BF16)

    blk = BS((tm, 256), lambda i: (i, 0))
    tab = BS((tm, 128), lambda i: (i % (S // tm), 0))
    return pl.pallas_call(
        body, name="rope_bwd", grid=(t // tm,), in_specs=[blk] * 9 + [tab, tab],
        out_specs=BS((None, tm, QKV_A), lambda i: (0, i, 0)), out_shape=SDS((1, t, QKV_A), BF16),
        compiler_params=_cp("parallel"),
    )(*dqs, *dks, *dvs, cos_t, sin_t)


def _head_masks():
    lane = lax.broadcasted_iota(jnp.int32, (1, 128), 1)
    m0 = (lane < HEAD_DIM).astype(F32)
    return m0, 1.0 - m0


def _dil_geometry(d):
    sub = S // d
    q_rows = 128
    k_rows = min(256, sub)
    return sub, q_rows, sub // q_rows, k_rows


def _dil_tile(idx, d, keys_on_rows=False):
    sub, q_rows, nb, k_rows = _dil_geometry(d)
    r = idx // nb
    n = idx % nb
    k_sub = jnp.clip(q_rows * n - HALF, 0, sub - k_rows)
    if d == 1:
        q_start = pl.multiple_of(q_rows * n, q_rows)
        k_start = pl.multiple_of(k_sub, HALF)
    else:
        q_start = q_rows * n * d + r
        k_start = k_sub * d + r
    if keys_on_rows:
        ii = lax.broadcasted_iota(jnp.int32, (k_rows, 2 * q_rows), 1) % q_rows
        jj = lax.broadcasted_iota(jnp.int32, (k_rows, 2 * q_rows), 0)
    else:
        ii = lax.broadcasted_iota(jnp.int32, (q_rows, k_rows), 0)
        jj = lax.broadcasted_iota(jnp.int32, (q_rows, k_rows), 1)
    valid = jnp.abs(jj - ii + (k_sub - q_rows * n)) <= HALF
    return q_start, k_start, valid


def _dil_specs(grp):
    qs = BS((S, 128), lambda b, hp: (b, 2 * grp + hp))
    ks = BS((S, 128), lambda b, hp: (b, 6 + 2 * grp + hp))
    vs = BS((S, 128), lambda b, hp: (b, 12 + 2 * grp + hp))
    own = BS((S, 128), lambda b, hp: (b, hp))
    return qs, ks, vs, own


def _dil_fwd(qkr, proj, grp):
    t = qkr.shape[0]
    d = DILATIONS[grp]
    _, q_rows, nb, k_rows = _dil_geometry(d)

    def body(q_ref, k_ref, v_ref, o_ref, l_ref):
        masks = _head_masks()

        def step(i0, carry):
            geo = [_dil_tile(i0 * DIL_FWD_TILES + j, d) for j in range(DIL_FWD_TILES)]
            tiles = [(j, h) for j in range(DIL_FWD_TILES) for h in range(2)]
            qs = [q_ref[_ds(g[0], q_rows, d), :] for g in geo]
            kbs = [k_ref[_ds(g[1], k_rows, d), :].astype(BF16) for g in geo]
            ss = [jnp.where(geo[j][2], _dot_nt((qs[j] * masks[h]).astype(BF16), kbs[j]) * SCALE, NEG) for j, h in tiles]
            mxs = [jnp.max(s, axis=1, keepdims=True) for s in ss]
            ps = [jnp.exp(s - mx) for s, mx in zip(ss, mxs)]
            dens = [jnp.sum(p, axis=1, keepdims=True) for p in ps]
            vs = [v_ref[_ds(g[1], k_rows, d), :] for g in geo]
            outs = [_dot_nn(p.astype(BF16), (vs[j] * masks[h]).astype(BF16)) / den
                    for p, den, (j, h) in zip(ps, dens, tiles)]
            for j, g in enumerate(geo):
                o_ref[_ds(g[0], q_rows, d), :] = outs[2 * j] + outs[2 * j + 1]
                l_ref[_ds(g[0], q_rows, d), :] = (
                    (mxs[2 * j] + jnp.log(dens[2 * j])) * masks[0] + (mxs[2 * j + 1] + jnp.log(dens[2 * j + 1])) * masks[1])
            return carry

        lax.fori_loop(0, d * nb // DIL_FWD_TILES, step, 0)

    qs, ks, vs, own = _dil_specs(grp)
    return pl.pallas_call(
        body, name=f"dil_fwd_{grp}", grid=(t // S, 2), in_specs=[qs, ks, vs], out_specs=[own, own],
        out_shape=[SDS((t, 256), F32), SDS((t, 256), F32)], compiler_params=_cp("parallel", "parallel"),
    )(qkr, qkr, proj)


def _dil_bwd(qkr, proj, do, dlp, lse, grp):
    t = qkr.shape[0]
    d = DILATIONS[grp]
    _, q_rows, nb, k_rows = _dil_geometry(d)

    def body(q_ref, k_ref, v_ref, do_ref, dl_ref, l_ref, dq_ref, dk_ref, dv_ref):
        masks = _head_masks()
        dk_ref[...] = jnp.zeros_like(dk_ref)
        dv_ref[...] = jnp.zeros_like(dv_ref)

        def as_row(x2):
            xt = x2.T
            return jnp.concatenate([xt[0:1], xt[HEAD_DIM:HEAD_DIM + 1]], axis=1)

        def step(i0, carry):
            geo = [_dil_tile(i0 * DIL_BWD_TILES + j, d, keys_on_rows=True) for j in range(DIL_BWD_TILES)]
            q_ds = [_ds(g[0], q_rows, d) for g in geo]
            k_ds = [_ds(g[1], k_rows, d) for g in geo]
            qbs = [_both_heads(q_ref[r, :], masks).astype(BF16) for r in q_ds]
            kbs = [k_ref[r, :].astype(BF16) for r in k_ds]
            vbs = [v_ref[r, :].astype(BF16) for r in k_ds]
            dobs = [_both_heads(do_ref[r, :], masks).astype(BF16) for r in q_ds]
            l_rows = [as_row(l_ref[r, :]) for r in q_ds]
            dl_rows = [as_row(dl_ref[r, :]) for r in q_ds]
            ss = [jnp.where(g[2], _dot_nt(kb, qb) * SCALE, NEG) for g, kb, qb in zip(geo, kbs, qbs)]
            ps = [jnp.exp(s - lr) for s, lr in zip(ss, l_rows)]
            dps = [_dot_nt(vb, dob) for vb, dob in zip(vbs, dobs)]
            dss = [(p * (dp - dr)).astype(BF16) for p, dp, dr in zip(ps, dps, dl_rows)]
            dks = [_dot_nn(ds, qb) for ds, qb in zip(dss, qbs)]
            dvs = [_dot_nn(p.astype(BF16), dob) for p, dob in zip(ps, dobs)]
            dqs = [_own_heads(_dot_tn(ds, kb), masks) for ds, kb in zip(dss, kbs)]
            for j in range(DIL_BWD_TILES):
                dq_ref[q_ds[j], :] = dqs[j] * SCALE
                dk_ref[k_ds[j], :] += dks[j] * SCALE
                dv_ref[k_ds[j], :] += dvs[j]
            return carry

        lax.fori_loop(0, d * nb // DIL_BWD_TILES, step, 0)

    qs, ks, vs, own = _dil_specs(grp)
    return pl.pallas_call(
        body, name=f"dil_bwd_{grp}", grid=(t // S, 2), in_specs=[qs, ks, vs, own, own, own],
        out_specs=[own, own, own], out_shape=[SDS((t, 256), F32)] * 3,
        compiler_params=_cp("parallel", "parallel"),
    )(qkr, qkr, proj, do, dlp, lse)


def _mix_weights(l0, l1, l2):
    mx = jnp.maximum(jnp.maximum(l0, l1), l2)
    e0, e1, e2 = jnp.exp(l0 - mx), jnp.exp(l1 - mx), jnp.exp(l2 - mx)
    den = e0 + e1 + e2
    return e0 / den, e1 / den, e2 / den


def _combine_fwd(outs, lses):
    t = outs[0].shape[0]
    tm = 1024

    def body(o0, o1, o2, l0, l1, l2, y_ref):
        w0, w1, w2 = _mix_weights(l0[...], l1[...], l2[...])
        y_ref[...] = w0 * o0[...] + w1 * o1[...] + w2 * o2[...]

    blk = BS((tm, 256), lambda i: (i, 0))
    return pl.pallas_call(
        body, name="combine_fwd", grid=(t // tm,), in_specs=[blk] * 6, out_specs=blk,
        out_shape=SDS((t, 256), F32), compiler_params=_cp("parallel"),
    )(*outs, *lses)


def _head_sum(x):
    a = lax.broadcasted_iota(jnp.int32, (256, 256), 0) // HEAD_DIM
    b = lax.broadcasted_iota(jnp.int32, (256, 256), 1) // HEAD_DIM
    ones = (a == b).astype(BF16)
    hi = x.astype(BF16)
    lo = (x - hi.astype(F32)).astype(BF16)
    return _dot_nn(hi, ones) + _dot_nn(lo, ones)


def _combine_bwd(dya, outs, lses):
    t = dya.shape[0]
    tm = 1024

    def body(dy_ref, o0, o1, o2, l0, l1, l2, d0, d1, d2, e0, e1, e2):
        ws = _mix_weights(l0[...], l1[...], l2[...])
        dy = dy_ref[...]
        ya = ws[0] * o0[...] + ws[1] * o1[...] + ws[2] * o2[...]
        hs = _head_sum(dy * ya)
        for w, d_ref, e_ref in zip(ws, (d0, d1, d2), (e0, e1, e2)):
            d_ref[...] = w * dy
            e_ref[...] = w * hs

    blk = BS((tm, 256), lambda i: (i, 0))
    return pl.pallas_call(
        body, name="combine_bwd", grid=(t // tm,), in_specs=[blk] * 7, out_specs=[blk] * 6,
        out_shape=[SDS((t, 256), F32)] * 6, compiler_params=_cp("parallel"),
    )(dya, *outs, *lses)


def _na_bias_table(rel_bias):
    kw = NA_KR * GRID_W
    rev = jnp.pad(rel_bias.astype(F32)[:, :, ::-1], ((0, 0), (0, 0), (0, 128 - 31)))

    def body(r_ref, o_ref):
        lane = lax.broadcasted_iota(jnp.int32, (GRID_W, 128), 1)
        j = lax.broadcasted_iota(jnp.int32, (GRID_W, 128), 0)
        q = lane % GRID_W
        win_lo = jnp.clip(q - 8, 0, GRID_W - 16)
        valid = (j >= win_lo) & (j < win_lo + 16)
        for cls in range(NA_KR):
            for k in range(NA_KR):
                tiles = []
                for h in range(2):
                    row = jnp.broadcast_to(r_ref[h, cls + k:cls + k + 1, :], (GRID_W, 128))
                    tiles.append(pltpu.roll(row, (128 - 15 + GRID_W * h) % 128, 1, stride=1, stride_axis=0))
                o_ref[cls, GRID_W * k:GRID_W * (k + 1), :] = jnp.where(
                    valid, jnp.where(lane < GRID_W, tiles[0], tiles[1]), NEG)

    return pl.pallas_call(
        body, name="na_bias_table", grid=(4,),
        in_specs=[BS((2, 2 * NA_KR - 1, 128), lambda hp: (hp, 0, 0))],
        out_specs=BS((None, NA_KR, kw, 128), lambda hp: (hp, 0, 0, 0)),
        out_shape=SDS((4, NA_KR, kw, 128), F32), compiler_params=_cp("parallel"),
    )(rev)


def _na_row(i):
    lo = jnp.clip(i - NA_KR // 2, 0, NA_ROWS - NA_KR)
    return pl.multiple_of(GRID_W * i, GRID_W), pl.multiple_of(GRID_W * lo, GRID_W), lo - i + NA_KR - 1


def _both_heads(x, masks):
    return jnp.concatenate([x * masks[0], x * masks[1]], axis=0)


def _own_heads(r, masks):
    half = r.shape[0] // 2
    return r[:half] * masks[0] + r[half:] * masks[1]


def _na_fwd(proj, bias):
    t = proj.shape[0]
    kw = NA_KR * GRID_W

    def body(q_ref, k_ref, v_ref, b_ref, o_ref, l_ref):
        masks = _head_masks()

        def step(i0, carry):
            idx = [i0 * NA_FWD_ROWS + j for j in range(NA_FWD_ROWS)]
            rows = [_na_row(i) for i in idx]
            qbs = [_both_heads(q_ref[pl.ds(r[0], GRID_W), :], masks).astype(BF16) for r in rows]
            kbs = [k_ref[pl.ds(r[1], kw), :].astype(BF16) for r in rows]
            ss = [_dot_nt(kb, qb) * SCALE + b_ref[r[2]] for kb, qb, r in zip(kbs, qbs, rows)]
            mxs = [jnp.max(s, axis=0, keepdims=True) for s in ss]
            ps = [jnp.exp(s - mx) for s, mx in zip(ss, mxs)]
            dens = [jnp.sum(p, axis=0, keepdims=True) for p in ps]
            pbs = [(p / den).astype(BF16) for p, den in zip(ps, dens)]
            vbs = [v_ref[pl.ds(r[1], kw), :].astype(BF16) for r in rows]
            outs = [_own_heads(_dot_tn(pb, vb), masks) for pb, vb in zip(pbs, vbs)]
            for j, r in enumerate(rows):
                o_ref[pl.ds(r[0], GRID_W), :] = outs[j]
                l_ref[pl.ds(idx[j], 1), :] = mxs[j] + jnp.log(dens[j])
            return carry

        lax.fori_loop(0, NA_ROWS // NA_FWD_ROWS, step, 0)

    c0 = QKV_A // 128
    return pl.pallas_call(
        body, name="na_fwd", grid=(t // S, 4),
        in_specs=[BS((S, 128), lambda b, hp: (b, c0 + hp)), BS((S, 128), lambda b, hp: (b, c0 + 4 + hp)),
                  BS((S, 128), lambda b, hp: (b, c0 + 8 + hp)),
                  BS((None, NA_KR, kw, 128), lambda b, hp: (hp, 0, 0, 0))],
        out_specs=[BS((S, 128), lambda b, hp: (b, hp)), BS((None, None, NA_ROWS, 128), lambda b, hp: (b, hp, 0, 0))],
        out_shape=[SDS((t, 512), F32), SDS((t // S, 4, NA_ROWS, 128), F32)],
        compiler_params=_cp("parallel", "parallel"),
    )(proj, proj, proj, bias)


def _na_bwd(proj, bias, dyb, yb, lse):
    t = proj.shape[0]
    kw = NA_KR * GRID_W

    def body(q_ref, k_ref, v_ref, b_ref, do_ref, o_ref, l_ref, d_ref, db_ref):
        masks = _head_masks()
        ones = jnp.ones((8, 128), BF16)

        @pl.when(pl.program_id(1) == 0)
        def _():
            db_ref[...] = jnp.zeros_like(db_ref)

        d_ref[1:3] = jnp.zeros((2, S, 128), F32)

        def row_sums(x):
            hi = x.astype(BF16)
            lo = (x - hi.astype(F32)).astype(BF16)
            return (_dot_nt(ones, hi) + _dot_nt(ones, lo))[0:1]

        def step(i0, carry):
            idx = [i0 * NA_BWD_ROWS + j for j in range(NA_BWD_ROWS)]
            rows = [_na_row(i) for i in idx]
            q_ds = [pl.ds(r[0], GRID_W) for r in rows]
            k_ds = [pl.ds(r[1], kw) for r in rows]
            qbs = [_both_heads(q_ref[r, :], masks).astype(BF16) for r in q_ds]
            kbs = [k_ref[r, :].astype(BF16) for r in k_ds]
            vbs = [v_ref[r, :].astype(BF16) for r in k_ds]
            dos = [do_ref[r, :] for r in q_ds]
            dobs = [_both_heads(do, masks).astype(BF16) for do in dos]
            deltas = [row_sums(_both_heads(do * o_ref[r, :], masks)) for do, r in zip(dos, q_ds)]
            ss = [_dot_nt(kb, qb) * SCALE + b_ref[r[2]] for kb, qb, r in zip(kbs, qbs, rows)]
            ps = [jnp.exp(s - l_ref[pl.ds(i, 1), :]) for s, i in zip(ss, idx)]
            dps = [_dot_nt(vb, dob) for vb, dob in zip(vbs, dobs)]
            dss = [p * (dp - delta) for p, dp, delta in zip(ps, dps, deltas)]
            for ds, r in zip(dss, rows):
                db_ref[r[2]] += ds
            dsbs = [ds.astype(BF16) for ds in dss]
            dks = [_dot_nn(dsb, qb) for dsb, qb in zip(dsbs, qbs)]
            dvs = [_dot_nn(p.astype(BF16), dob) for p, dob in zip(ps, dobs)]
            dqs = [_own_heads(_dot_tn(dsb, kb), masks) for dsb, kb in zip(dsbs, kbs)]
            for j in range(NA_BWD_ROWS):
                d_ref[0, q_ds[j], :] = dqs[j] * SCALE
                d_ref[1, k_ds[j], :] += dks[j] * SCALE
                d_ref[2, k_ds[j], :] += dvs[j]
            return carry

        lax.fori_loop(0, NA_ROWS // NA_BWD_ROWS, step, 0)

    c0 = QKV_A // 128
    own = BS((S, 128), lambda hp, b: (b, hp))
    tab = BS((None, NA_KR, kw, 128), lambda hp, b: (hp, 0, 0, 0))
    return pl.pallas_call(
        body, name="na_bwd", grid=(4, t // S),
        in_specs=[BS((S, 128), lambda hp, b: (b, c0 + hp)), BS((S, 128), lambda hp, b: (b, c0 + 4 + hp)),
                  BS((S, 128), lambda hp, b: (b, c0 + 8 + hp)), tab, own, own,
                  BS((None, None, NA_ROWS, 128), lambda hp, b: (b, hp, 0, 0))],
        out_specs=[BS((3, S, 128), lambda hp, b: (0, b, hp)), tab],
        out_shape=[SDS((3, t, 512), F32), SDS((4, NA_KR, kw, 128), F32)],
        compiler_params=_cp("parallel", "arbitrary"),
    )(proj, proj, proj, bias, dyb, yb, lse)


def _na_dbias_lane_map():
    kw = NA_KR * GRID_W
    lane = np.arange(kw)
    blk, m = lane // GRID_W, lane % GRID_W
    target = np.full(kw, -1)
    target[m < 16] = (blk * 32 + 15 + m)[m < 16]
    target[m >= 49] = (((blk + 1) % NA_KR) * 32 + m - 49)[m >= 49]
    return jnp.asarray(target[:, None] == np.arange(kw)[None, :], BF16)


def _na_dbias(db):
    kw = NA_KR * GRID_W

    def body(x_ref, map_ref, o_ref, z_ref):
        for cls in range(NA_KR):
            xt = x_ref[cls].T
            for h in range(2):
                xv = xt[GRID_W * h:GRID_W * (h + 1)]
                y = xv[0:8]
                for g in range(1, GRID_W // 8):
                    y = y + pltpu.roll(xv[8 * g:8 * g + 8], kw - 8 * g, 1)
                d = y[0:1]
                for s in range(1, 8):
                    d = d + pltpu.roll(y[s:s + 1], kw - s, 1)
                z_ref[h, cls:cls + 1, :] = d
        for h in range(2):
            z = z_ref[h]
            hi = z.astype(BF16)
            lo = (z - hi.astype(F32)).astype(BF16)
            e = _dot_nn(hi, map_ref[...]) + _dot_nn(lo, map_ref[...])
            out = e[0:1]
            for cls in range(1, NA_KR):
                out = out + pltpu.roll(e[cls:cls + 1], 32 * cls, 1)
            o_ref[h] = jnp.broadcast_to(out, (8, kw))

    return pl.pallas_call(
        body, name="na_dbias", grid=(4,),
        in_specs=[BS((None, NA_KR, kw, 128), lambda hp: (hp, 0, 0, 0)), BS((kw, kw), lambda hp: (0, 0))],
        out_specs=BS((2, 8, kw), lambda hp: (hp, 0, 0)), out_shape=SDS((8, 8, kw), F32),
        scratch_shapes=[pltpu.VMEM((2, 8, kw), F32)], compiler_params=_cp("parallel"),
    )(db, _na_dbias_lane_map())


def _merge_fwd(ya, yb, proj, wat, wbt):
    t = ya.shape[0]
    tm, tn = 2048, 256
    ca = (QKV_A + QKV_B) // tn
    cb = ca + D // tn

    def body(ya_ref, yb_ref, la_ref, lb_ref, wa_ref, wb_ref, m_ref, za_ref, zb_ref):
        za = _dot_nt(ya_ref[...].astype(BF16), wa_ref[...])
        zb = _dot_nt(yb_ref[...].astype(BF16), wb_ref[...])
        m_ref[...] = (jax.nn.sigmoid(la_ref[...]) * za + jax.nn.sigmoid(lb_ref[...]) * zb).astype(BF16)
        za_ref[...] = za.astype(BF16)
        zb_ref[...] = zb.astype(BF16)

    out = BS((tm, tn), lambda i, j: (i, j))
    return pl.pallas_call(
        body, name="merge_fwd", grid=(t // tm, D // tn),
        in_specs=[BS((tm, 256), lambda i, j: (i, 0)), BS((tm, 512), lambda i, j: (i, 0)),
                  BS((tm, tn), lambda i, j: (i, ca + j)), BS((tm, tn), lambda i, j: (i, cb + j)),
                  BS((tn, 256), lambda i, j: (j, 0)), BS((tn, 512), lambda i, j: (j, 0))],
        out_specs=[out, out, out], out_shape=[SDS((t, D), BF16)] * 3,
        compiler_params=_cp("parallel", "parallel"),
    )(ya, yb, proj, proj, wat, wbt)


def _merge_bwd(dxo, wo, za, zb, proj):
    t = dxo.shape[0]
    tm, tn = 2048, 256
    ca = (QKV_A + QKV_B) // tn
    cb = ca + D // tn

    def body(d_ref, w_ref, za_ref, zb_ref, la_ref, lb_ref, dza_ref, dzb_ref, dl_ref):
        dmv = _dot_nt(d_ref[...], w_ref[...])
        ga = jax.nn.sigmoid(la_ref[...])
        gb = jax.nn.sigmoid(lb_ref[...])
        dza_ref[...] = (dmv * ga).astype(BF16)
        dzb_ref[...] = (dmv * gb).astype(BF16)
        dl_ref[0] = (dmv * za_ref[...].astype(F32) * ga * (1.0 - ga)).astype(BF16)
        dl_ref[1] = (dmv * zb_ref[...].astype(F32) * gb * (1.0 - gb)).astype(BF16)

    blk = BS((tm, tn), lambda i, j: (i, j))
    return pl.pallas_call(
        body, name="merge_bwd", grid=(t // tm, D // tn),
        in_specs=[BS((tm, D), lambda i, j: (i, 0)), BS((tn, D), lambda i, j: (j, 0)), blk, blk,
                  BS((tm, tn), lambda i, j: (i, ca + j)), BS((tm, tn), lambda i, j: (i, cb + j))],
        out_specs=[blk, blk, BS((2, tm, tn), lambda i, j: (0, i, j))],
        out_shape=[SDS((t, D), BF16), SDS((t, D), BF16), SDS((2, t, D), BF16)],
        compiler_params=_cp("parallel", "parallel"),
    )(dxo, wo, za, zb, proj, proj)


def _adamw_update(w, g, m, v):
    mn = ADAM_B1 * m + (1.0 - ADAM_B1) * g
    vn = ADAM_B2 * v + (1.0 - ADAM_B2) * (g * g)
    m_hat = mn / (1.0 - ADAM_B1 ** ADAM_STEP)
    v_hat = vn / (1.0 - ADAM_B2 ** ADAM_STEP)
    return -ADAM_LR * (m_hat / (jnp.sqrt(v_hat) + ADAM_EPS) + ADAM_WD * w), mn, vn


def _sum_adamw(recv0, recv1, w, m, v, tag):
    _, r, c = recv0.shape
    tr = max(rows for rows in range(16, r + 1, 16) if r % rows == 0 and rows * c <= 384 * 1024)

    def body(a_ref, b_ref, w_ref, m_ref, v_ref, g_ref, d_ref, mo_ref, vo_ref):
        def update(ref):
            g = ref[0].astype(F32)
            for s in range(1, N_DEV):
                g = g + ref[s].astype(F32)
            g_ref[...] = g
            d_ref[...], mo_ref[...], vo_ref[...] = _adamw_update(w_ref[...], g, m_ref[...], v_ref[...])

        pl.when(pl.program_id(0) == 0)(lambda: update(a_ref))
        pl.when(pl.program_id(0) == 1)(lambda: update(b_ref))

    blk = BS((None, tr, c), lambda layer, i: (layer, i, 0))
    return pl.pallas_call(
        body, name=f"sum_adamw_{tag}", grid=(2, r // tr),
        in_specs=[BS((N_DEV, tr, c), lambda layer, i: (0, i * (1 - layer), 0)),
                  BS((N_DEV, tr, c), lambda layer, i: (0, i * layer, 0)), blk, blk, blk],
        out_specs=[blk] * 4, out_shape=[SDS((2, r, c), F32)] * 4, compiler_params=_cp("arbitrary", "arbitrary"),
    )(recv0, recv1, w, m, v)


def _adamw(w, g, m, v, tag):
    layers, r, c = w.shape
    tr = next(r // k for k in (1, 2, 4, 8) if r // k <= 384 and r % (8 * k) == 0)

    def body(w_ref, g_ref, m_ref, v_ref, d_ref, mo_ref, vo_ref):
        d_ref[...], mo_ref[...], vo_ref[...] = _adamw_update(w_ref[...], g_ref[...], m_ref[...], v_ref[...])

    blk = BS((None, tr, c), lambda l, i: (l, i, 0))
    return pl.pallas_call(
        body, name=f"adamw_{tag}", grid=(layers, r // tr), in_specs=[blk] * 4, out_specs=[blk] * 3,
        out_shape=[SDS((layers, r, c), F32)] * 3, compiler_params=_cp("parallel", "parallel"),
    )(w, g, m, v)


def _place():
    return lax.axis_index("x"), lax.axis_index("y"), lax.axis_index("c")


def _flip(coord, bit):
    return 1 - coord if bit else coord


def _peers(x, y, c):
    peers = []
    for mask in range(1, N_DEV):
        p = (_flip(x, mask & 4), _flip(y, mask & 2), _flip(c, mask & 1))
        peers.append((p, 4 * p[0] + 2 * p[1] + p[2]))
    return peers


def _copy_plan(mode, src, land, x, y, c):
    me = 4 * x + 2 * y + c

    def device(mask):
        p = (_flip(x, mask & 4), _flip(y, mask & 2), _flip(c, mask & 1))
        return p, 4 * p[0] + 2 * p[1] + p[2]

    if mode == "scatter":
        r = land.shape[1]
        return [(p, src.at[pl.ds(i * r, r), :], land.at[me], land.at[i])
                for p, i in map(device, (1, 2, 3, 4, 5, 6, 7, 0))]
    r = land.shape[0] // N_DEV

    def rows(i):
        return land.at[pl.ds(i * r, r), :]

    if mode == "gather":
        return [(p, src, rows(me), rows(i)) for p, i in map(device, (1, 4, 2, 6, 0))]
    sibling = device(1)[0]
    return [(sibling, rows(device(m)[1]), rows(device(m)[1]), rows(device(m | 1)[1])) for m in (4, 2, 6)]


COPIES = dict(scatter=8, gather=5, forward=3)
HBM_SPEC = BS(memory_space=pltpu.HBM)
SEM_SPEC = BS(memory_space=pltpu.SEMAPHORE)
DATAFLOW = pltpu.SideEffectType.DATAFLOW_SIDE_EFFECTING


def _fresh(shape, dtype, tag):
    def body(o_ref):
        del o_ref

    return pl.pallas_call(body, name=f"fresh_{tag}", out_specs=BS(memory_space=pl.ANY), out_shape=SDS(shape, dtype))()


def _exchange_start(mode, srcs, lands, after, tag):
    if lands is None and mode == "gather":
        lands = [_fresh((N_DEV * s.shape[0], s.shape[1]), s.dtype, f"{tag}_{a}") for a, s in enumerate(srcs)]
    elif lands is None:
        lands = [_fresh((N_DEV, s.shape[0] // N_DEV, s.shape[1]), s.dtype, f"{tag}_{a}") for a, s in enumerate(srcs)]
    n, n_src, n_cp = len(lands), len(srcs), COPIES[mode]
    behind = [] if after is None else [after]

    def body(*refs):
        src_refs, land_refs = refs[:n_src], refs[n_src:n_src + n]
        send_sems, recv_sems = refs[n_src + n + len(behind)], refs[n_src + n + len(behind) + 1]
        token = refs[-1]
        for a in range(n):
            plan = _copy_plan(mode, src_refs[a] if n_src else None, land_refs[a], *_place())
            for k, (p, out, there, _) in enumerate(plan):
                pltpu.make_async_remote_copy(
                    src_ref=out, dst_ref=there, send_sem=send_sems.at[n_cp * a + k],
                    recv_sem=recv_sems.at[n_cp * a + k], device_id=p, device_id_type=MESH).start()
        token[...] = jnp.zeros_like(token)

    both = [*srcs, *lands]
    res = pl.pallas_call(
        body, name=f"{mode}_start_{tag}",
        out_shape=(pltpu.SemaphoreType.DMA((n_cp * n,)), pltpu.SemaphoreType.DMA((n_cp * n,)),
                   *[pltpu.HBM(v.shape, v.dtype) for v in both], SDS((8, 128), F32)),
        in_specs=[HBM_SPEC] * len(both) + [BS(memory_space=pl.ANY)] * len(behind),
        out_specs=(SEM_SPEC, SEM_SPEC, *[HBM_SPEC] * len(both), BS(memory_space=pltpu.VMEM)),
        input_output_aliases={i: 2 + i for i in range(len(both))},
        compiler_params=pltpu.CompilerParams(has_side_effects=DATAFLOW),
    )(*[pltpu.with_memory_space_constraint(v, pltpu.HBM) for v in both], *behind)
    return (mode, res[0], res[1], res[2:2 + n_src], res[2 + n_src:2 + n_src + n]), res[-1]


def _exchange_wait(handle, after, tag, which=None):
    mode, send_sems, recv_sems, srcs, lands = handle
    which = list(range(len(lands))) if which is None else list(which)
    n_cp = COPIES[mode]
    lands = [lands[a] for a in which]
    srcs = [srcs[a] for a in which] if srcs else []
    n, n_src = len(lands), len(srcs)
    afters = list(after) if isinstance(after, (tuple, list)) else [after]

    def body(*refs):
        src_refs, land_refs = refs[:n_src], refs[n_src:n_src + n]
        send_ref, recv_ref = refs[n_src + n], refs[n_src + n + 1]
        for i, a in enumerate(which):
            plan = _copy_plan(mode, src_refs[i] if n_src else None, land_refs[i], *_place())
            for k, (p, out, _, here) in enumerate(plan):
                cp = pltpu.make_async_remote_copy(
                    src_ref=out, dst_ref=here, send_sem=send_ref.at[n_cp * a + k], recv_sem=recv_ref.at[n_cp * a + k],
                    device_id=p, device_id_type=MESH)
                cp.wait_send()
                cp.wait_recv()

    both = [*srcs, *lands]
    res = pl.pallas_call(
        body, name=f"{mode}_wait_{tag}", out_shape=tuple(pltpu.HBM(v.shape, v.dtype) for v in both),
        in_specs=[HBM_SPEC] * len(both) + [SEM_SPEC, SEM_SPEC] + [BS(memory_space=pl.ANY)] * len(afters),
        out_specs=tuple([HBM_SPEC] * len(both)),
        input_output_aliases={i: i for i in range(len(both))},
        compiler_params=pltpu.CompilerParams(has_side_effects=DATAFLOW),
    )(*both, send_sems, recv_sems, *afters)
    return list(res[n_src:])


def _allreduce_small(vec, behind):
    rows = vec.shape[0]

    def body(x_ref, behind_ref, o_ref, buf_ref, send_sems, recv_sems):
        x, y, c = _place()
        me = 4 * x + 2 * y + c
        buf_ref[me] = x_ref[...]
        peers = _peers(x, y, c)

        def copy(k, slot):
            return pltpu.make_async_remote_copy(
                src_ref=x_ref, dst_ref=buf_ref.at[slot], send_sem=send_sems.at[k], recv_sem=recv_sems.at[k],
                device_id=peers[k][0], device_id_type=MESH)

        sends = [copy(k, me) for k in range(N_DEV - 1)]
        for cp in sends:
            cp.start()
        for k in range(N_DEV - 1):
            copy(k, peers[k][1]).wait_recv()
        for cp in sends:
            cp.wait_send()
        acc = buf_ref[0]
        for s in range(1, N_DEV):
            acc = acc + buf_ref[s]
        o_ref[...] = acc

    vmem = BS(memory_space=pltpu.VMEM)
    return pl.pallas_call(
        body, name="allreduce_small", in_specs=[vmem, BS(memory_space=pl.ANY)], out_specs=vmem,
        out_shape=SDS((rows, 128), F32),
        scratch_shapes=[pltpu.VMEM((N_DEV, rows, 128), F32), pltpu.SemaphoreType.DMA((7,)),
                        pltpu.SemaphoreType.DMA((7,))],
        compiler_params=pltpu.CompilerParams(has_side_effects=True),
    )(vec, behind)


def _ffn_forward(x, hn, fetch, names, tag, next_g):
    gu, act = _ffn_up(hn, fetch(names[0], hn).reshape(2, F, D), tag)
    got = _mm_nn(act[None], fetch(names[1], act)[None], f"down_{tag}", res=x, scale=0.5, tm=512, next_g=next_g)
    out, hn_next = got if next_g is not None else (got, None)
    return out, hn_next, (x, hn, gu, act)


def _ffn_backward(dxo, dxo_b, saved, norm_g, wut, wd, tag, send):
    x, hn, gu, act = saved
    d_wd = _mm_tn(act[None], dxo_b, f"dwd_{tag}", scale=0.5)
    du = _ffn_dact(dxo_b, wd, gu, send(("down",), [d_wd]), tag)
    d_wut = _mm_tn(du, hn, f"dwu_{tag}")
    token = send(("up",), [d_wut])
    return _mm_nn_norm_bwd([du], wut.reshape(2 * F, D), x, norm_g + token[0, 0], dxo, tag)


def _mixer_forward(x, hn, fetch, bias, tables, tag, next_g):
    proj = _mm_nt_rows(hn, fetch("win", hn), f"proj_{tag}", 1024, IN_W // 2, IN_W, 0, rope=(*tables, 2 * QKV_A // 3))
    qkr = proj
    outs, lses = [], []
    for grp in range(3):
        o, l = _dil_fwd(qkr, proj, grp)
        outs.append(o)
        lses.append(l)
    ya = _combine_fwd(outs, lses)
    yb, lse_b = _na_fwd(proj, bias)
    merged, za, zb = _merge_fwd(ya, yb, proj, fetch("wa", yb), fetch("wb", yb))
    out, hn_next = _mm_nn(merged[None], fetch("wo", merged)[None], f"out_{tag}", res=x, next_g=next_g)
    return out, hn_next, (x, hn, proj, qkr, outs, lses, ya, yb, lse_b, merged, za, zb)


def _mixer_backward(dxo, dxo_b, saved, norm_g, w, bias, tables, tag, send):
    wint, wat, wbt, wo = w
    x, hn, proj, qkr, outs, lses, ya, yb, lse_b, merged, za, zb = saved
    d_wo = _mm_tn(merged[None], dxo_b, f"dwo_{tag}")
    dza, dzb, dlog = _merge_bwd(dxo_b, wo, za, zb, proj)
    dya = _mm_nn(dza[None], wat[None], f"dya_{tag}")
    dyb = _mm_nn(dzb[None], wbt[None], f"dyb_{tag}")
    d_wat = _mm_tn(dza[None], ya, f"dwa_{tag}")
    d_wbt = _mm_tn(dzb[None], yb, f"dwb_{tag}")
    cb = _combine_bwd(dya, outs, lses)
    dqs, dks, dvs = [], [], []
    for grp in range(3):
        dq, dk, dv = _dil_bwd(qkr, proj, cb[grp], cb[3 + grp], lses[grp], grp)
        dqs.append(dq)
        dks.append(dk)
        dvs.append(dv)
    d_qkv_b, dbias_tab = _na_bwd(proj, bias, dyb, yb, lse_b)
    dbias = _na_dbias(dbias_tab)
    dproj = [_rope_bwd(dqs, dks, dvs, *tables), d_qkv_b, dlog]
    d_wint, row = None, 0
    for i, p in enumerate(dproj):
        d_wint = _mm_tn(p, hn, f"dwin{i}_{tag}", into=d_wint, row0=row, rows=IN_W)
        row += p.shape[0] * p.shape[2]
    token = send(("win", "wa", "wb", "wo"), [d_wint, d_wat, d_wbt, d_wo])
    dx, dx_b, dg = _mm_nn_norm_bwd(dproj, wint, x, norm_g + token[0, 0], dxo, f"mix_{tag}", tm=512, single_w=True)
    dbias = dbias[:, 0, :480].reshape(8, 15, 32)[:, :, :31]
    return dx, dx_b, dg, dbias


def _pack_small(norms, biases, final, loss=None):
    parts = []
    for layer in range(DEPTH):
        parts += [norms[0][layer], norms[1][layer], norms[2][layer],
                  jnp.pad(biases[layer].reshape(-1), (0, BIAS_PAD - 8 * 15 * 31))]
    parts.append(final)
    flat = jnp.concatenate([p.reshape(-1).astype(F32) for p in parts])
    if loss is not None:
        flat = jnp.concatenate([flat, loss.reshape(-1)])
    return jnp.pad(flat, (0, SMALL_ROWS * 128 - flat.shape[0])).reshape(SMALL_ROWS, 128)


def _unpack_small(packed):
    flat = packed.reshape(-1)
    norms, biases = ([], [], []), []
    pos = 0
    for _ in range(DEPTH):
        for k in range(3):
            norms[k].append(flat[pos:pos + D])
            pos += D
        biases.append(flat[pos:pos + 8 * 15 * 31].reshape(8, 15, 31))
        pos += BIAS_PAD
    final = flat[pos:pos + D]
    pos += D
    return [jnp.stack(n) for n in norms], jnp.stack(biases), final, flat[pos]


def kernel(x, ffn1_norm, ffn1_w_up, ffn1_w_down, mix_norm, w_in, na_rel_bias, w_branch_a, w_branch_b, w_out, ffn2_norm, ffn2_w_up, ffn2_w_down, final_norm, loss_target, m_ffn1_norm, m_ffn1_w_up, m_ffn1_w_down, m_mix_norm, m_w_in, m_na_rel_bias, m_w_branch_a, m_w_branch_b, m_w_out, m_ffn2_norm, m_ffn2_w_up, m_ffn2_w_down, m_final_norm, v_ffn1_norm, v_ffn1_w_up, v_ffn1_w_down, v_mix_norm, v_w_in, v_na_rel_bias, v_w_branch_a, v_w_branch_b, v_w_out, v_ffn2_norm, v_ffn2_w_up, v_ffn2_w_down, v_final_norm):
    t = x.shape[0] * x.shape[1]
    xs = x.reshape(t, D)
    tgt = loss_target.reshape(t, D)
    tables = _rope_tables()

    col_sharded = dict(up1=ffn1_w_up, win=w_in, wa=w_branch_a, wb=w_branch_b, up2=ffn2_w_up)
    row_sharded = dict(down1=ffn1_w_down, wo=w_out, down2=ffn2_w_down)
    shard = [{} for _ in range(DEPTH)]
    for layer in range(DEPTH):
        for name, arr in col_sharded.items():
            shard[layer][name] = arr[layer].T.astype(BF16)
        for name, arr in row_sharded.items():
            shard[layer][name] = arr[layer].astype(BF16)

    weights = [{} for _ in range(DEPTH)]
    travel = [(0, ("up1",)), (0, ("down1",)), (0, ("win",)), (0, ("wa", "wb", "wo", "up2", "down2")),
              (1, ("up1", "down1", "win")), (1, ("wa", "wb", "wo", "up2", "down2"))]
    group_of, chips_done, sibling_done, passing = {}, {}, {}, {}
    count = 0
    for i, (layer, names) in enumerate(travel):
        chips_done[i] = list(range(count, count + len(names)))
        count += len(names)
        for n in names:
            group_of[layer, n] = (i, names)
    gathered, token = _exchange_start(
        "gather", [shard[layer][n] for layer, names in travel for n in names], None, None, "w")
    zero = token[0, 0]

    biases = [_na_bias_table(na_rel_bias[layer] + zero) for layer in range(DEPTH)]

    def pass_on(i, behind):
        if i in chips_done:
            lands = _exchange_wait(gathered, behind, f"w{i}", which=chips_done.pop(i))
            sibling_done[i], passing[i] = _exchange_start("forward", [], lands, None, f"p{i}")

    def fetcher(layer):
        def fetch(name, behind):
            if (layer, name) in group_of:
                i, names = group_of[layer, name]
                if i == 0:
                    behind = (behind, *biases)
                behind = behind if isinstance(behind, tuple) else (behind,)
                pass_on(i, behind)
                if i > 0:
                    pass_on(i + 1, behind)
                if i + 1 in passing:
                    behind = (*behind, passing[i + 1])
                for n, got in zip(names, _exchange_wait(sibling_done.pop(i), behind, f"p{i}")):
                    weights[layer][n] = got
                    del group_of[layer, n]
            return weights[layer][name]
        return fetch

    saved = []
    h = xs
    hn = _norm_fwd(xs, ffn1_norm[0] + zero, "first")
    for layer in range(DEPTH):
        bias = biases[layer]
        fetch = fetcher(layer)
        after_ffn2 = ffn1_norm[layer + 1] if layer + 1 < DEPTH else None
        h, hn, s1 = _ffn_forward(h, hn, fetch, ("up1", "down1"), f"f1l{layer}", mix_norm[layer])
        h, hn, s2 = _mixer_forward(h, hn, fetch, bias, tables, f"l{layer}", ffn2_norm[layer])
        h, hn, s3 = _ffn_forward(h, hn, fetch, ("up2", "down2"), f"f2l{layer}", after_ffn2)
        saved.append((s1, s2, s3, bias))
    loss_part, dh, dh_b, d_final = _loss_head(h, final_norm, tgt)

    d_norms = ([None] * DEPTH, [None] * DEPTH, [None] * DEPTH)
    d_bias = [None] * DEPTH
    sent = {}

    def sender(layer, suffix):
        def send(names, grads):
            tag = f"g{layer}{names[0]}{suffix}"
            handle, token = _exchange_start("scatter", grads, None, None, tag)
            for i, n in enumerate(names):
                sent[layer, n + suffix] = (handle, i, tag)
            return token
        return send

    for layer in reversed(range(DEPTH)):
        w = weights[layer]
        s1, s2, s3, bias = saved[layer]
        dh, dh_b, d_norms[2][layer] = _ffn_backward(
            dh, dh_b, s3, ffn2_norm[layer], w["up2"].reshape(2, F, D), w["down2"], f"f2l{layer}", sender(layer, "2"))
        dh, dh_b, d_norms[1][layer], d_bias[layer] = _mixer_backward(
            dh, dh_b, s2, mix_norm[layer], (w["win"], w["wa"], w["wb"], w["wo"]), bias, tables, f"l{layer}",
            sender(layer, ""))
        dh, dh_b, d_norms[0][layer] = _ffn_backward(
            dh, dh_b, s1, ffn1_norm[layer], w["up1"].reshape(2, F, D), w["down1"], f"f1l{layer}", sender(layer, "1"))
    grad_x = dh.reshape(x.shape)

    originals = dict(up1=(ffn1_w_up, m_ffn1_w_up, v_ffn1_w_up), down1=(ffn1_w_down, m_ffn1_w_down, v_ffn1_w_down),
                     win=(w_in, m_w_in, v_w_in), wa=(w_branch_a, m_w_branch_a, v_w_branch_a),
                     wb=(w_branch_b, m_w_branch_b, v_w_branch_b), wo=(w_out, m_w_out, v_w_out),
                     up2=(ffn2_w_up, m_ffn2_w_up, v_ffn2_w_up), down2=(ffn2_w_down, m_ffn2_w_down, v_ffn2_w_down))
    big = {}
    behind = dh
    landed = {}

    def received(layer, name):
        handle, i, tag = sent[layer, name]
        if tag not in landed:
            landed[tag] = _exchange_wait(handle, behind, tag)
        return landed[tag][i]

    for name in ("down2", "up2", "win", "wa", "wb", "wo", "down1", "up1"):
        if name == "up1":
            small = _allreduce_small(_pack_small(d_norms, d_bias, d_final, loss_part[0, :1]), behind)
            g_norms, g_bias, g_final, loss = _unpack_small(small)
            w_small = _pack_small((ffn1_norm, mix_norm, ffn2_norm), na_rel_bias, final_norm)
            m_small = _pack_small((m_ffn1_norm, m_mix_norm, m_ffn2_norm), m_na_rel_bias, m_final_norm)
            v_small = _pack_small((v_ffn1_norm, v_mix_norm, v_ffn2_norm), v_na_rel_bias, v_final_norm)
            upd = _adamw(w_small[None], small[None], m_small[None], v_small[None], "small")
            small_out = [(g_norms, g_bias, g_final)] + [_unpack_small(u[0])[:3] for u in upd]
            behind = upd[0]
        wv, mv, vv = originals[name]
        if name in col_sharded:
            wv, mv, vv = (jnp.swapaxes(t, 1, 2) for t in (wv, mv, vv))
        big[name] = tuple(_sum_adamw(received(0, name), received(1, name), wv, mv, vv, name))
        behind = big[name][1]
        if name in col_sharded:
            big[name] = tuple(jnp.swapaxes(t, 1, 2) for t in big[name])

    outputs = [loss, grad_x]
    for kind in range(4):
        norms, bias_k, final_k = small_out[kind]
        outputs += [norms[0], big["up1"][kind], big["down1"][kind], norms[1], big["win"][kind], bias_k,
                    big["wa"][kind], big["wb"][kind], big["wo"][kind], norms[2], big["up2"][kind],
                    big["down2"][kind], final_k]
    return tuple(outputs)
```

```python
import numpy as np

import jax
import jax.numpy as jnp
from jax import lax
from jax.experimental import pallas as pl
from jax.experimental.pallas import tpu as pltpu

F32 = jnp.float32
BF16 = jnp.bfloat16
SDS = jax.ShapeDtypeStruct
BS = pl.BlockSpec
MESH = pl.DeviceIdType.MESH

D = 1024
S = 2048
F = 2816
DEPTH = 2
HEAD_DIM = 64
DILATIONS = (1, 4, 16)
HALF = 64
QKV_A = 2304
QKV_B = 1536
IN_W = 5888
N_DEV = 8
NA_ROWS = 32
GRID_W = 64
NA_KR = 8
ROPE_THETA = 10000.0
RMS_EPS = 1e-6
NEG = -1e30
SCALE = HEAD_DIM ** -0.5
ADAM_LR, ADAM_B1, ADAM_B2, ADAM_EPS, ADAM_WD, ADAM_STEP = 0.001, 0.9, 0.999, 1e-08, 0.01, 10
VMEM_LIMIT_V7X = 52 * 1024 * 1024
SMALL_ROWS = 120
BIAS_PAD = 3840
NA_FWD_ROWS = 8
NA_BWD_ROWS = 4
DIL_FWD_TILES = 8
DIL_BWD_TILES = 4


def _cp(*sem):
    return pltpu.CompilerParams(dimension_semantics=sem, vmem_limit_bytes=VMEM_LIMIT_V7X)


def _dot_nn(a, b):
    return jnp.dot(a, b, preferred_element_type=F32)


def _dot_nt(a, b):
    return lax.dot_general(a, b, (((1,), (1,)), ((), ())), preferred_element_type=F32)


def _dot_tn(a, b):
    return lax.dot_general(a, b, (((0,), (0,)), ((), ())), preferred_element_type=F32)


def _ds(start, size, stride):
    return pl.ds(start, size) if stride == 1 else pl.ds(start, size, stride=stride)


def _norm_fwd(x, g, tag):
    t = x.shape[0]
    tm = 512

    def body(x_ref, g_ref, o_ref):
        xv = x_ref[...]
        r = lax.rsqrt(jnp.mean(xv * xv, axis=-1, keepdims=True) + RMS_EPS)
        o_ref[...] = (xv * r * g_ref[...]).astype(BF16)

    return pl.pallas_call(
        body, name=f"norm_fwd_{tag}", grid=(t // tm,),
        in_specs=[BS((tm, D), lambda i: (i, 0)), BS((1, D), lambda i: (0, 0))],
        out_specs=BS((tm, D), lambda i: (i, 0)),
        out_shape=SDS((t, D), BF16), compiler_params=_cp("parallel"),
    )(x, g.reshape(1, D))


def _loss_head(x, g, tgt):
    t = x.shape[0]
    tm = 1024

    def body(x_ref, g_ref, t_ref, loss_ref, dx_ref, dxb_ref, dg_ref):
        @pl.when(pl.program_id(0) == 0)
        def _():
            dg_ref[...] = jnp.zeros_like(dg_ref)
            loss_ref[...] = jnp.zeros_like(loss_ref)

        xv = x_ref[...]
        gv = g_ref[...]
        r = lax.rsqrt(jnp.mean(xv * xv, axis=-1, keepdims=True) + RMS_EPS)
        xh = xv * r
        e = xh * gv - t_ref[...]
        loss_ref[...] += 0.5 * jnp.sum(jnp.mean(e * e, axis=-1, keepdims=True), axis=0, keepdims=True)
        dy = e * (1.0 / D)
        u = dy * gv
        dx = r * (u - xh * jnp.mean(xh * u, axis=-1, keepdims=True))
        dx_ref[...] = dx
        dxb_ref[...] = dx.astype(BF16)
        dg_ref[...] += jnp.sum(dy * xh, axis=0, keepdims=True)

    row = BS((tm, D), lambda i: (i, 0))
    vec = BS((1, D), lambda i: (0, 0))
    return pl.pallas_call(
        body, name="loss_head", grid=(t // tm,),
        in_specs=[row, vec, row], out_specs=[BS((1, 128), lambda i: (0, 0)), row, row, vec],
        out_shape=[SDS((1, 128), F32), SDS((t, D), F32), SDS((t, D), BF16), SDS((1, D), F32)],
        compiler_params=_cp("arbitrary"),
    )(x, g.reshape(1, D), tgt)


def _mm_nn(a, w, tag, res=None, scale=1.0, tm=1024, tn=None, next_g=None):
    c_n, t, k = a.shape
    n = w.shape[2]
    tn = n if tn is None else tn
    assert next_g is None or tn == n
    n_in = 2 + (res is not None) + (next_g is not None)

    def body(*refs):
        a_ref, w_ref = refs[0], refs[1]
        acc = _dot_nn(a_ref[0].astype(BF16), w_ref[0])
        for c in range(1, c_n):
            acc = acc + _dot_nn(a_ref[c].astype(BF16), w_ref[c])
        if scale != 1.0:
            acc = acc * scale
        if res is not None:
            acc = refs[2][...] + acc
        refs[n_in][...] = acc
        if next_g is not None:
            r = lax.rsqrt(jnp.mean(acc * acc, axis=-1, keepdims=True) + RMS_EPS)
            refs[n_in + 1][...] = (acc * r * refs[n_in - 1][...]).astype(BF16)

    w_mode = dict(pipeline_mode=pl.Buffered(1)) if tn == n else {}
    in_specs = [BS((c_n, tm, k), lambda i, j: (0, i, 0)), BS((c_n, k, tn), lambda i, j: (0, 0, j), **w_mode)]
    args = [a, w]
    out_specs = [BS((tm, tn), lambda i, j: (i, j))]
    out_shape = [SDS((t, n), F32)]
    if res is not None:
        in_specs.append(BS((tm, tn), lambda i, j: (i, j)))
        args.append(res)
    if next_g is not None:
        in_specs.append(BS((1, n), lambda i, j: (0, 0)))
        args.append(next_g.reshape(1, n))
        out_specs.append(BS((tm, tn), lambda i, j: (i, j)))
        out_shape.append(SDS((t, n), BF16))
    got = pl.pallas_call(
        body, name=f"mm_nn_{tag}", grid=(t // tm, n // tn), in_specs=in_specs, out_specs=out_specs,
        out_shape=out_shape, compiler_params=_cp("parallel", "parallel"),
    )(*args)
    return got if next_g is not None else got[0]


def _mm_nn_norm_bwd(parts, w, x, g, dres, tag, tm=256, single_w=False):
    t = parts[0].shape[1]
    n_parts = len(parts)

    def body(*refs):
        w_ref, x_ref, g_ref, dr_ref, dx_ref, dxb_ref, dg_ref = refs[n_parts:]

        @pl.when(pl.program_id(0) == 0)
        def _():
            dg_ref[...] = jnp.zeros_like(dg_ref)

        dh = None
        row = 0
        for a_ref, part in zip(refs, parts):
            for c in range(part.shape[0]):
                term = _dot_nn(a_ref[c].astype(BF16), w_ref[row:row + part.shape[2], :])
                dh = term if dh is None else dh + term
                row += part.shape[2]
        xv = x_ref[...]
        r = lax.rsqrt(jnp.mean(xv * xv, axis=-1, keepdims=True) + RMS_EPS)
        xh = xv * r
        u = dh * g_ref[...]
        dx = dr_ref[...] + r * (u - xh * jnp.mean(xh * u, axis=-1, keepdims=True))
        dx_ref[...] = dx
        dxb_ref[...] = dx.astype(BF16)
        dg_ref[...] += jnp.sum(dh * xh, axis=0, keepdims=True)

    row = BS((tm, D), lambda i: (i, 0))
    vec = BS((1, D), lambda i: (0, 0))
    return pl.pallas_call(
        body, name=f"mm_nn_norm_bwd_{tag}", grid=(t // tm,),
        in_specs=[BS((p.shape[0], tm, p.shape[2]), lambda i: (0, i, 0)) for p in parts]
        + [BS(w.shape, lambda i: (0, 0), **(dict(pipeline_mode=pl.Buffered(1)) if single_w else {})), row, vec, row],
        out_specs=[row, row, vec], out_shape=[SDS((t, D), F32), SDS((t, D), BF16), SDS((1, D), F32)],
        compiler_params=_cp("arbitrary"),
    )(*parts, w, x, g.reshape(1, D), dres)


def _mm_nt_rows(a, w, tag, tm, tn, n_total, w_row0, rope=None):
    t, k = a.shape
    assert w_row0 % tn == 0 and n_total % tn == 0
    j0 = w_row0 // tn

    def body(a_ref, w_ref, *rest):
        o_ref = rest[-1]
        o_ref[...] = _dot_nt(a_ref[...].astype(BF16), w_ref[...])
        if rope is not None:
            @pl.when(pl.program_id(0) == 0)
            def _():
                c = rest[0][...]
                sg = rest[1][...]
                first = (lax.broadcasted_iota(jnp.int32, (tm, 128), 1) % HEAD_DIM) < HEAD_DIM // 2
                for col in range(0, rope[2], 128):
                    v = o_ref[:, col:col + 128]
                    o_ref[:, col:col + 128] = v * c + _swap_halves(v, first) * sg

    in_specs = [BS((tm, k), lambda j, i: (i, 0)), BS((tn, k), lambda j, i: (j0 + j, 0))]
    args = [a, w]
    if rope is not None:
        assert rope[2] <= tn
        in_specs += [BS((tm, 128), lambda j, i: (i % (S // tm), 0))] * 2
        args += [rope[0], rope[1]]
    return pl.pallas_call(
        body, name=f"mm_nt_{tag}", grid=(n_total // tn, t // tm), in_specs=in_specs,
        out_specs=BS((tm, tn), lambda j, i: (i, j)), out_shape=SDS((t, n_total), F32),
        compiler_params=_cp("parallel", "parallel"),
    )(*args)


def _mm_tn(a, b, tag, scale=1.0, tmm=None, into=None, row0=0, rows=None):
    c_n, t, m = a.shape
    n = b.shape[1]
    if tmm is None:
        tmm = max(w for w in (1408, 768, 512, 256) if m % w == 0 and row0 % w == 0)
    tiles = m // tmm
    block0 = row0 // tmm
    assert row0 % tmm == 0 and m % tmm == 0

    def body(a_ref, b_ref, *rest):
        rest[-1][...] = (_dot_tn(a_ref[...].astype(BF16), b_ref[...].astype(BF16)) * scale).astype(BF16)

    in_specs = [BS((None, t, tmm), lambda c, mi: (c, 0, mi)), BS((t, n), lambda c, mi: (0, 0))]
    args = [a, b]
    if into is not None:
        in_specs.append(BS(memory_space=pl.ANY))
        args.append(into)
    return pl.pallas_call(
        body, name=f"mm_tn_{tag}", grid=(c_n, tiles), in_specs=in_specs,
        out_specs=BS((tmm, n), lambda c, mi: (block0 + c * tiles + mi, 0)),
        out_shape=SDS((rows or c_n * m, n) if into is None else into.shape, BF16),
        input_output_aliases={} if into is None else {2: 0},
        compiler_params=_cp("parallel", "parallel"),
    )(*args)


def _ffn_up(hn, wut, tag):
    t = hn.shape[0]
    tm, tn = 1024, 1408

    def body(h_ref, w_ref, gu_ref, act_ref):
        h = h_ref[...]
        g = _dot_nt(h, w_ref[0])
        u = _dot_nt(h, w_ref[1])
        sg = jax.nn.sigmoid(g)
        silu = g * sg
        gu_ref[0] = (u * (sg + silu * (1.0 - sg))).astype(BF16)
        gu_ref[1] = silu.astype(BF16)
        act_ref[...] = (silu * u).astype(BF16)

    return pl.pallas_call(
        body, name=f"ffn_up_{tag}", grid=(F // tn, t // tm),
        in_specs=[BS((tm, D), lambda j, i: (i, 0)), BS((2, tn, D), lambda j, i: (0, j, 0))],
        out_specs=[BS((2, tm, tn), lambda j, i: (0, i, j)), BS((tm, tn), lambda j, i: (i, j))],
        out_shape=[SDS((2, t, F), BF16), SDS((t, F), BF16)],
        compiler_params=_cp("parallel", "parallel"),
    )(hn, wut)


def _ffn_dact(dxo, wd, gu, tie, tag):
    t = dxo.shape[0]
    tm, tn = 1024, 1408

    def body(d_ref, w_ref, gu_ref, tie_ref, o_ref):
        dact = _dot_nt(d_ref[...] * 0.5, w_ref[...])
        o_ref[0] = (dact * gu_ref[0].astype(F32)).astype(BF16)
        o_ref[1] = (dact * gu_ref[1].astype(F32)).astype(BF16)

    return pl.pallas_call(
        body, name=f"ffn_dact_{tag}", grid=(F // tn, t // tm),
        in_specs=[BS((tm, D), lambda j, i: (i, 0)), BS((tn, D), lambda j, i: (j, 0)),
                  BS((2, tm, tn), lambda j, i: (0, i, j)), BS((8, 128), lambda j, i: (0, 0))],
        out_specs=BS((2, tm, tn), lambda j, i: (0, i, j)),
        out_shape=SDS((2, t, F), BF16), compiler_params=_cp("parallel", "parallel"),
    )(dxo, wd, gu, tie)


def _rope_tables():
    half = HEAD_DIM // 2
    inv_freq = ROPE_THETA ** (-jnp.arange(half, dtype=F32) / half)
    ang = jnp.arange(S).astype(F32)[:, None] * inv_freq[None, :]
    cos, sin = jnp.cos(ang), jnp.sin(ang)
    return jnp.concatenate([cos, cos, cos, cos], axis=1), jnp.concatenate([-sin, sin, -sin, sin], axis=1)


def _swap_halves(t, first_half):
    return jnp.where(first_half, pltpu.roll(t, 96, 1), pltpu.roll(t, 32, 1))


def _rope_bwd(dqs, dks, dvs, cos_t, sin_t):
    t = dqs[0].shape[0]
    tm = 1024

    def body(*refs):
        c = refs[9][...]
        sg = refs[10][...]
        o_ref = refs[11]
        first = (lax.broadcasted_iota(jnp.int32, (tm, 128), 1) % HEAD_DIM) < HEAD_DIM // 2
        for a in range(6):
            for hp in range(2):
                v = refs[a][:, 128 * hp:128 * (hp + 1)]
                col = 128 * (2 * a + hp)
                o_ref[:, col:col + 128] = (v * c + _swap_halves(v * sg, first)).astype(BF16)
        for a in range(6, 9):
            o_ref[:, 256 * a:256 * (a + 1)] = refs[a][...].astype(BF16)

    blk = BS((tm, 256), lambda i: (i, 0))
    tab = BS((tm, 128), lambda i: (i % (S // tm), 0))
    return pl.pallas_call(
        body, name="rope_bwd", grid=(t // tm,), in_specs=[blk] * 9 + [tab, tab],
        out_specs=BS((None, tm, QKV_A), lambda i: (0, i, 0)), out_shape=SDS((1, t, QKV_A), BF16),
        compiler_params=_cp("parallel"),
    )(*dqs, *dks, *dvs, cos_t, sin_t)


def _head_masks():
    lane = lax.broadcasted_iota(jnp.int32, (1, 128), 1)
    m0 = (lane < HEAD_DIM).astype(F32)
    return m0, 1.0 - m0


def _dil_geometry(d):
    sub = S // d
    q_rows = 128
    k_rows = min(256, sub)
    return sub, q_rows, sub // q_rows, k_rows


def _dil_tile(idx, d, keys_on_rows=False):
    sub, q_rows, nb, k_rows = _dil_geometry(d)
    r = idx // nb
    n = idx % nb
    k_sub = jnp.clip(q_rows * n - HALF, 0, sub - k_rows)
    if d == 1:
        q_start = pl.multiple_of(q_rows * n, q_rows)
        k_start = pl.multiple_of(k_sub, HALF)
    else:
        q_start = q_rows * n * d + r
        k_start = k_sub * d + r
    if keys_on_rows:
        ii = lax.broadcasted_iota(jnp.int32, (k_rows, 2 * q_rows), 1) % q_rows
        jj = lax.broadcasted_iota(jnp.int32, (k_rows, 2 * q_rows), 0)
    else:
        ii = lax.broadcasted_iota(jnp.int32, (q_rows, k_rows), 0)
        jj = lax.broadcasted_iota(jnp.int32, (q_rows, k_rows), 1)
    valid = jnp.abs(jj - ii + (k_sub - q_rows * n)) <= HALF
    return q_start, k_start, valid


def _dil_specs(grp):
    qs = BS((S, 128), lambda b, hp: (b, 2 * grp + hp))
    ks = BS((S, 128), lambda b, hp: (b, 6 + 2 * grp + hp))
    vs = BS((S, 128), lambda b, hp: (b, 12 + 2 * grp + hp))
    own = BS((S, 128), lambda b, hp: (b, hp))
    return qs, ks, vs, own


def _dil_fwd(qkr, proj, grp):
    t = qkr.shape[0]
    d = DILATIONS[grp]
    _, q_rows, nb, k_rows = _dil_geometry(d)

    def body(q_ref, k_ref, v_ref, o_ref, l_ref):
        masks = _head_masks()

        def step(i0, carry):
            geo = [_dil_tile(i0 * DIL_FWD_TILES + j, d) for j in range(DIL_FWD_TILES)]
            tiles = [(j, h) for j in range(DIL_FWD_TILES) for h in range(2)]
            qs = [q_ref[_ds(g[0], q_rows, d), :] for g in geo]
            kbs = [k_ref[_ds(g[1], k_rows, d), :].astype(BF16) for g in geo]
            ss = [jnp.where(geo[j][2], _dot_nt((qs[j] * masks[h]).astype(BF16), kbs[j]) * SCALE, NEG) for j, h in tiles]
            mxs = [jnp.max(s, axis=1, keepdims=True) for s in ss]
            ps = [jnp.exp(s - mx) for s, mx in zip(ss, mxs)]
            dens = [jnp.sum(p, axis=1, keepdims=True) for p in ps]
            vs = [v_ref[_ds(g[1], k_rows, d), :] for g in geo]
            outs = [_dot_nn(p.astype(BF16), (vs[j] * masks[h]).astype(BF16)) / den
                    for p, den, (j, h) in zip(ps, dens, tiles)]
            for j, g in enumerate(geo):
                o_ref[_ds(g[0], q_rows, d), :] = outs[2 * j] + outs[2 * j + 1]
                l_ref[_ds(g[0], q_rows, d), :] = (
                    (mxs[2 * j] + jnp.log(dens[2 * j])) * masks[0] + (mxs[2 * j + 1] + jnp.log(dens[2 * j + 1])) * masks[1])
            return carry

        lax.fori_loop(0, d * nb // DIL_FWD_TILES, step, 0)

    qs, ks, vs, own = _dil_specs(grp)
    return pl.pallas_call(
        body, name=f"dil_fwd_{grp}", grid=(t // S, 2), in_specs=[qs, ks, vs], out_specs=[own, own],
        out_shape=[SDS((t, 256), F32), SDS((t, 256), F32)], compiler_params=_cp("parallel", "parallel"),
    )(qkr, qkr, proj)


def _dil_bwd(qkr, proj, do, dlp, lse, grp):
    t = qkr.shape[0]
    d = DILATIONS[grp]
    _, q_rows, nb, k_rows = _dil_geometry(d)

    def body(q_ref, k_ref, v_ref, do_ref, dl_ref, l_ref, dq_ref, dk_ref, dv_ref):
        masks = _head_masks()
        dk_ref[...] = jnp.zeros_like(dk_ref)
        dv_ref[...] = jnp.zeros_like(dv_ref)

        def as_row(x2):
            xt = x2.T
            return jnp.concatenate([xt[0:1], xt[HEAD_DIM:HEAD_DIM + 1]], axis=1)

        def step(i0, carry):
            geo = [_dil_tile(i0 * DIL_BWD_TILES + j, d, keys_on_rows=True) for j in range(DIL_BWD_TILES)]
            q_ds = [_ds(g[0], q_rows, d) for g in geo]
            k_ds = [_ds(g[1], k_rows, d) for g in geo]
            qbs = [_both_heads(q_ref[r, :], masks).astype(BF16) for r in q_ds]
            kbs = [k_ref[r, :].astype(BF16) for r in k_ds]
            vbs = [v_ref[r, :].astype(BF16) for r in k_ds]
            dobs = [_both_heads(do_ref[r, :], masks).astype(BF16) for r in q_ds]
            l_rows = [as_row(l_ref[r, :]) for r in q_ds]
            dl_rows = [as_row(dl_ref[r, :]) for r in q_ds]
            ss = [jnp.where(g[2], _dot_nt(kb, qb) * SCALE, NEG) for g, kb, qb in zip(geo, kbs, qbs)]
            ps = [jnp.exp(s - lr) for s, lr in zip(ss, l_rows)]
            dps = [_dot_nt(vb, dob) for vb, dob in zip(vbs, dobs)]
            dss = [(p * (dp - dr)).astype(BF16) for p, dp, dr in zip(ps, dps, dl_rows)]
            dks = [_dot_nn(ds, qb) for ds, qb in zip(dss, qbs)]
            dvs = [_dot_nn(p.astype(BF16), dob) for p, dob in zip(ps, dobs)]
            dqs = [_own_heads(_dot_tn(ds, kb), masks) for ds, kb in zip(dss, kbs)]
            for j in range(DIL_BWD_TILES):
                dq_ref[q_ds[j], :] = dqs[j] * SCALE
                dk_ref[k_ds[j], :] += dks[j] * SCALE
                dv_ref[k_ds[j], :] += dvs[j]
            return carry

        lax.fori_loop(0, d * nb // DIL_BWD_TILES, step, 0)

    qs, ks, vs, own = _dil_specs(grp)
    return pl.pallas_call(
        body, name=f"dil_bwd_{grp}", grid=(t // S, 2), in_specs=[qs, ks, vs, own, own, own],
        out_specs=[own, own, own], out_shape=[SDS((t, 256), F32)] * 3,
        compiler_params=_cp("parallel", "parallel"),
    )(qkr, qkr, proj, do, dlp, lse)


def _mix_weights(l0, l1, l2):
    mx = jnp.maximum(jnp.maximum(l0, l1), l2)
    e0, e1, e2 = jnp.exp(l0 - mx), jnp.exp(l1 - mx), jnp.exp(l2 - mx)
    den = e0 + e1 + e2
    return e0 / den, e1 / den, e2 / den


def _combine_fwd(outs, lses):
    t = outs[0].shape[0]
    tm = 1024

    def body(o0, o1, o2, l0, l1, l2, y_ref):
        w0, w1, w2 = _mix_weights(l0[...], l1[...], l2[...])
        y_ref[...] = w0 * o0[...] + w1 * o1[...] + w2 * o2[...]

    blk = BS((tm, 256), lambda i: (i, 0))
    return pl.pallas_call(
        body, name="combine_fwd", grid=(t // tm,), in_specs=[blk] * 6, out_specs=blk,
        out_shape=SDS((t, 256), F32), compiler_params=_cp("parallel"),
    )(*outs, *lses)


def _head_sum(x):
    a = lax.broadcasted_iota(jnp.int32, (256, 256), 0) // HEAD_DIM
    b = lax.broadcasted_iota(jnp.int32, (256, 256), 1) // HEAD_DIM
    ones = (a == b).astype(BF16)
    hi = x.astype(BF16)
    lo = (x - hi.astype(F32)).astype(BF16)
    return _dot_nn(hi, ones) + _dot_nn(lo, ones)


def _combine_bwd(dya, outs, lses):
    t = dya.shape[0]
    tm = 1024

    def body(dy_ref, o0, o1, o2, l0, l1, l2, d0, d1, d2, e0, e1, e2):
        ws = _mix_weights(l0[...], l1[...], l2[...])
        dy = dy_ref[...]
        ya = ws[0] * o0[...] + ws[1] * o1[...] + ws[2] * o2[...]
        hs = _head_sum(dy * ya)
        for w, d_ref, e_ref in zip(ws, (d0, d1, d2), (e0, e1, e2)):
            d_ref[...] = w * dy
            e_ref[...] = w * hs

    blk = BS((tm, 256), lambda i: (i, 0))
    return pl.pallas_call(
        body, name="combine_bwd", grid=(t // tm,), in_specs=[blk] * 7, out_specs=[blk] * 6,
        out_shape=[SDS((t, 256), F32)] * 6, compiler_params=_cp("parallel"),
    )(dya, *outs, *lses)


def _na_bias_table(rel_bias):
    kw = NA_KR * GRID_W
    rev = jnp.pad(rel_bias.astype(F32)[:, :, ::-1], ((0, 0), (0, 0), (0, 128 - 31)))

    def body(r_ref, o_ref):
        lane = lax.broadcasted_iota(jnp.int32, (GRID_W, 128), 1)
        j = lax.broadcasted_iota(jnp.int32, (GRID_W, 128), 0)
        q = lane % GRID_W
        win_lo = jnp.clip(q - 8, 0, GRID_W - 16)
        valid = (j >= win_lo) & (j < win_lo + 16)
        for cls in range(NA_KR):
            for k in range(NA_KR):
                tiles = []
                for h in range(2):
                    row = jnp.broadcast_to(r_ref[h, cls + k:cls + k + 1, :], (GRID_W, 128))
                    tiles.append(pltpu.roll(row, (128 - 15 + GRID_W * h) % 128, 1, stride=1, stride_axis=0))
                o_ref[cls, GRID_W * k:GRID_W * (k + 1), :] = jnp.where(
                    valid, jnp.where(lane < GRID_W, tiles[0], tiles[1]), NEG)

    return pl.pallas_call(
        body, name="na_bias_table", grid=(4,),
        in_specs=[BS((2, 2 * NA_KR - 1, 128), lambda hp: (hp, 0, 0))],
        out_specs=BS((None, NA_KR, kw, 128), lambda hp: (hp, 0, 0, 0)),
        out_shape=SDS((4, NA_KR, kw, 128), F32), compiler_params=_cp("parallel"),
    )(rev)


def _na_row(i):
    lo = jnp.clip(i - NA_KR // 2, 0, NA_ROWS - NA_KR)
    return pl.multiple_of(GRID_W * i, GRID_W), pl.multiple_of(GRID_W * lo, GRID_W), lo - i + NA_KR - 1


def _both_heads(x, masks):
    return jnp.concatenate([x * masks[0], x * masks[1]], axis=0)


def _own_heads(r, masks):
    half = r.shape[0] // 2
    return r[:half] * masks[0] + r[half:] * masks[1]


def _na_fwd(proj, bias):
    t = proj.shape[0]
    kw = NA_KR * GRID_W

    def body(q_ref, k_ref, v_ref, b_ref, o_ref, l_ref):
        masks = _head_masks()

        def step(i0, carry):
            idx = [i0 * NA_FWD_ROWS + j for j in range(NA_FWD_ROWS)]
            rows = [_na_row(i) for i in idx]
            qbs = [_both_heads(q_ref[pl.ds(r[0], GRID_W), :], masks).astype(BF16) for r in rows]
            kbs = [k_ref[pl.ds(r[1], kw), :].astype(BF16) for r in rows]
            ss = [_dot_nt(kb, qb) * SCALE + b_ref[r[2]] for kb, qb, r in zip(kbs, qbs, rows)]
            mxs = [jnp.max(s, axis=0, keepdims=True) for s in ss]
            ps = [jnp.exp(s - mx) for s, mx in zip(ss, mxs)]
            dens = [jnp.sum(p, axis=0, keepdims=True) for p in ps]
            pbs = [(p / den).astype(BF16) for p, den in zip(ps, dens)]
            vbs = [v_ref[pl.ds(r[1], kw), :].astype(BF16) for r in rows]
            outs = [_own_heads(_dot_tn(pb, vb), masks) for pb, vb in zip(pbs, vbs)]
            for j, r in enumerate(rows):
                o_ref[pl.ds(r[0], GRID_W), :] = outs[j]
                l_ref[pl.ds(idx[j], 1), :] = mxs[j] + jnp.log(dens[j])
            return carry

        lax.fori_loop(0, NA_ROWS // NA_FWD_ROWS, step, 0)

    c0 = QKV_A // 128
    return pl.pallas_call(
        body, name="na_fwd", grid=(t // S, 4),
        in_specs=[BS((S, 128), lambda b, hp: (b, c0 + hp)), BS((S, 128), lambda b, hp: (b, c0 + 4 + hp)),
                  BS((S, 128), lambda b, hp: (b, c0 + 8 + hp)),
                  BS((None, NA_KR, kw, 128), lambda b, hp: (hp, 0, 0, 0))],
        out_specs=[BS((S, 128), lambda b, hp: (b, hp)), BS((None, None, NA_ROWS, 128), lambda b, hp: (b, hp, 0, 0))],
        out_shape=[SDS((t, 512), F32), SDS((t // S, 4, NA_ROWS, 128), F32)],
        compiler_params=_cp("parallel", "parallel"),
    )(proj, proj, proj, bias)


def _na_bwd(proj, bias, dyb, yb, lse):
    t = proj.shape[0]
    kw = NA_KR * GRID_W

    def body(q_ref, k_ref, v_ref, b_ref, do_ref, o_ref, l_ref, d_ref, db_ref):
        masks = _head_masks()
        ones = jnp.ones((8, 128), BF16)

        @pl.when(pl.program_id(1) == 0)
        def _():
            db_ref[...] = jnp.zeros_like(db_ref)

        d_ref[1:3] = jnp.zeros((2, S, 128), F32)

        def row_sums(x):
            hi = x.astype(BF16)
            lo = (x - hi.astype(F32)).astype(BF16)
            return (_dot_nt(ones, hi) + _dot_nt(ones, lo))[0:1]

        def step(i0, carry):
            idx = [i0 * NA_BWD_ROWS + j for j in range(NA_BWD_ROWS)]
            rows = [_na_row(i) for i in idx]
            q_ds = [pl.ds(r[0], GRID_W) for r in rows]
            k_ds = [pl.ds(r[1], kw) for r in rows]
            qbs = [_both_heads(q_ref[r, :], masks).astype(BF16) for r in q_ds]
            kbs = [k_ref[r, :].astype(BF16) for r in k_ds]
            vbs = [v_ref[r, :].astype(BF16) for r in k_ds]
            dos = [do_ref[r, :] for r in q_ds]
            dobs = [_both_heads(do, masks).astype(BF16) for do in dos]
            deltas = [row_sums(_both_heads(do * o_ref[r, :], masks)) for do, r in zip(dos, q_ds)]
            ss = [_dot_nt(kb, qb) * SCALE + b_ref[r[2]] for kb, qb, r in zip(kbs, qbs, rows)]
            ps = [jnp.exp(s - l_ref[pl.ds(i, 1), :]) for s, i in zip(ss, idx)]
            dps = [_dot_nt(vb, dob) for vb, dob in zip(vbs, dobs)]
            dss = [p * (dp - delta) for p, dp, delta in zip(ps, dps, deltas)]
            for ds, r in zip(dss, rows):
                db_ref[r[2]] += ds
            dsbs = [ds.astype(BF16) for ds in dss]
            dks = [_dot_nn(dsb, qb) for dsb, qb in zip(dsbs, qbs)]
            dvs = [_dot_nn(p.astype(BF16), dob) for p, dob in zip(ps, dobs)]
            dqs = [_own_heads(_dot_tn(dsb, kb), masks) for dsb, kb in zip(dsbs, kbs)]
            for j in range(NA_BWD_ROWS):
                d_ref[0, q_ds[j], :] = dqs[j] * SCALE
                d_ref[1, k_ds[j], :] += dks[j] * SCALE
                d_ref[2, k_ds[j], :] += dvs[j]
            return carry

        lax.fori_loop(0, NA_ROWS // NA_BWD_ROWS, step, 0)

    c0 = QKV_A // 128
    own = BS((S, 128), lambda hp, b: (b, hp))
    tab = BS((None, NA_KR, kw, 128), lambda hp, b: (hp, 0, 0, 0))
    return pl.pallas_call(
        body, name="na_bwd", grid=(4, t // S),
        in_specs=[BS((S, 128), lambda hp, b: (b, c0 + hp)), BS((S, 128), lambda hp, b: (b, c0 + 4 + hp)),
                  BS((S, 128), lambda hp, b: (b, c0 + 8 + hp)), tab, own, own,
                  BS((None, None, NA_ROWS, 128), lambda hp, b: (b, hp, 0, 0))],
        out_specs=[BS((3, S, 128), lambda hp, b: (0, b, hp)), tab],
        out_shape=[SDS((3, t, 512), F32), SDS((4, NA_KR, kw, 128), F32)],
        compiler_params=_cp("parallel", "arbitrary"),
    )(proj, proj, proj, bias, dyb, yb, lse)


def _na_dbias_lane_map():
    kw = NA_KR * GRID_W
    lane = np.arange(kw)
    blk, m = lane // GRID_W, lane % GRID_W
    target = np.full(kw, -1)
    target[m < 16] = (blk * 32 + 15 + m)[m < 16]
    target[m >= 49] = (((blk + 1) % NA_KR) * 32 + m - 49)[m >= 49]
    return jnp.asarray(target[:, None] == np.arange(kw)[None, :], BF16)


def _na_dbias(db):
    kw = NA_KR * GRID_W

    def body(x_ref, map_ref, o_ref, z_ref):
        for cls in range(NA_KR):
            xt = x_ref[cls].T
            for h in range(2):
                xv = xt[GRID_W * h:GRID_W * (h + 1)]
                y = xv[0:8]
                for g in range(1, GRID_W // 8):
                    y = y + pltpu.roll(xv[8 * g:8 * g + 8], kw - 8 * g, 1)
                d = y[0:1]
                for s in range(1, 8):
                    d = d + pltpu.roll(y[s:s + 1], kw - s, 1)
                z_ref[h, cls:cls + 1, :] = d
        for h in range(2):
            z = z_ref[h]
            hi = z.astype(BF16)
            lo = (z - hi.astype(F32)).astype(BF16)
            e = _dot_nn(hi, map_ref[...]) + _dot_nn(lo, map_ref[...])
            out = e[0:1]
            for cls in range(1, NA_KR):
                out = out + pltpu.roll(e[cls:cls + 1], 32 * cls, 1)
            o_ref[h] = jnp.broadcast_to(out, (8, kw))

    return pl.pallas_call(
        body, name="na_dbias", grid=(4,),
        in_specs=[BS((None, NA_KR, kw, 128), lambda hp: (hp, 0, 0, 0)), BS((kw, kw), lambda hp: (0, 0))],
        out_specs=BS((2, 8, kw), lambda hp: (hp, 0, 0)), out_shape=SDS((8, 8, kw), F32),
        scratch_shapes=[pltpu.VMEM((2, 8, kw), F32)], compiler_params=_cp("parallel"),
    )(db, _na_dbias_lane_map())


def _merge_fwd(ya, yb, proj, wat, wbt):
    t = ya.shape[0]
    tm, tn = 2048, 256
    ca = (QKV_A + QKV_B) // tn
    cb = ca + D // tn

    def body(ya_ref, yb_ref, la_ref, lb_ref, wa_ref, wb_ref, m_ref, za_ref, zb_ref):
        za = _dot_nt(ya_ref[...].astype(BF16), wa_ref[...])
        zb = _dot_nt(yb_ref[...].astype(BF16), wb_ref[...])
        m_ref[...] = (jax.nn.sigmoid(la_ref[...]) * za + jax.nn.sigmoid(lb_ref[...]) * zb).astype(BF16)
        za_ref[...] = za.astype(BF16)
        zb_ref[...] = zb.astype(BF16)

    out = BS((tm, tn), lambda i, j: (i, j))
    return pl.pallas_call(
        body, name="merge_fwd", grid=(t // tm, D // tn),
        in_specs=[BS((tm, 256), lambda i, j: (i, 0)), BS((tm, 512), lambda i, j: (i, 0)),
                  BS((tm, tn), lambda i, j: (i, ca + j)), BS((tm, tn), lambda i, j: (i, cb + j)),
                  BS((tn, 256), lambda i, j: (j, 0)), BS((tn, 512), lambda i, j: (j, 0))],
        out_specs=[out, out, out], out_shape=[SDS((t, D), BF16)] * 3,
        compiler_params=_cp("parallel", "parallel"),
    )(ya, yb, proj, proj, wat, wbt)


def _merge_bwd(dxo, wo, za, zb, proj):
    t = dxo.shape[0]
    tm, tn = 2048, 256
    ca = (QKV_A + QKV_B) // tn
    cb = ca + D // tn

    def body(d_ref, w_ref, za_ref, zb_ref, la_ref, lb_ref, dza_ref, dzb_ref, dl_ref):
        dmv = _dot_nt(d_ref[...], w_ref[...])
        ga = jax.nn.sigmoid(la_ref[...])
        gb = jax.nn.sigmoid(lb_ref[...])
        dza_ref[...] = (dmv * ga).astype(BF16)
        dzb_ref[...] = (dmv * gb).astype(BF16)
        dl_ref[0] = (dmv * za_ref[...].astype(F32) * ga * (1.0 - ga)).astype(BF16)
        dl_ref[1] = (dmv * zb_ref[...].astype(F32) * gb * (1.0 - gb)).astype(BF16)

    blk = BS((tm, tn), lambda i, j: (i, j))
    return pl.pallas_call(
        body, name="merge_bwd", grid=(t // tm, D // tn),
        in_specs=[BS((tm, D), lambda i, j: (i, 0)), BS((tn, D), lambda i, j: (j, 0)), blk, blk,
                  BS((tm, tn), lambda i, j: (i, ca + j)), BS((tm, tn), lambda i, j: (i, cb + j))],
        out_specs=[blk, blk, BS((2, tm, tn), lambda i, j: (0, i, j))],
        out_shape=[SDS((t, D), BF16), SDS((t, D), BF16), SDS((2, t, D), BF16)],
        compiler_params=_cp("parallel", "parallel"),
    )(dxo, wo, za, zb, proj, proj)


def _adamw_update(w, g, m, v):
    mn = ADAM_B1 * m + (1.0 - ADAM_B1) * g
    vn = ADAM_B2 * v + (1.0 - ADAM_B2) * (g * g)
    m_hat = mn / (1.0 - ADAM_B1 ** ADAM_STEP)
    v_hat = vn / (1.0 - ADAM_B2 ** ADAM_STEP)
    return -ADAM_LR * (m_hat / (jnp.sqrt(v_hat) + ADAM_EPS) + ADAM_WD * w), mn, vn


def _sum_adamw(recv0, recv1, w, m, v, tag):
    _, r, c = recv0.shape
    tr = max(rows for rows in range(16, r + 1, 16) if r % rows == 0 and rows * c <= 384 * 1024)

    def body(a_ref, b_ref, w_ref, m_ref, v_ref, g_ref, d_ref, mo_ref, vo_ref):
        def update(ref):
            g = ref[0].astype(F32)
            for s in range(1, N_DEV):
                g = g + ref[s].astype(F32)
            g_ref[...] = g
            d_ref[...], mo_ref[...], vo_ref[...] = _adamw_update(w_ref[...], g, m_ref[...], v_ref[...])

        pl.when(pl.program_id(0) == 0)(lambda: update(a_ref))
        pl.when(pl.program_id(0) == 1)(lambda: update(b_ref))

    blk = BS((None, tr, c), lambda layer, i: (layer, i, 0))
    return pl.pallas_call(
        body, name=f"sum_adamw_{tag}", grid=(2, r // tr),
        in_specs=[BS((N_DEV, tr, c), lambda layer, i: (0, i * (1 - layer), 0)),
                  BS((N_DEV, tr, c), lambda layer, i: (0, i * layer, 0)), blk, blk, blk],
        out_specs=[blk] * 4, out_shape=[SDS((2, r, c), F32)] * 4, compiler_params=_cp("arbitrary", "arbitrary"),
    )(recv0, recv1, w, m, v)


def _adamw(w, g, m, v, tag):
    layers, r, c = w.shape
    tr = next(r // k for k in (1, 2, 4, 8) if r // k <= 384 and r % (8 * k) == 0)

    def body(w_ref, g_ref, m_ref, v_ref, d_ref, mo_ref, vo_ref):
        d_ref[...], mo_ref[...], vo_ref[...] = _adamw_update(w_ref[...], g_ref[...], m_ref[...], v_ref[...])

    blk = BS((None, tr, c), lambda l, i: (l, i, 0))
    return pl.pallas_call(
        body, name=f"adamw_{tag}", grid=(layers, r // tr), in_specs=[blk] * 4, out_specs=[blk] * 3,
        out_shape=[SDS((layers, r, c), F32)] * 3, compiler_params=_cp("parallel", "parallel"),
    )(w, g, m, v)


def _place():
    return lax.axis_index("x"), lax.axis_index("y"), lax.axis_index("c")


def _flip(coord, bit):
    return 1 - coord if bit else coord


def _peers(x, y, c):
    peers = []
    for mask in range(1, N_DEV):
        p = (_flip(x, mask & 4), _flip(y, mask & 2), _flip(c, mask & 1))
        peers.append((p, 4 * p[0] + 2 * p[1] + p[2]))
    return peers


def _copy_plan(mode, src, land, x, y, c):
    me = 4 * x + 2 * y + c

    def device(mask):
        p = (_flip(x, mask & 4), _flip(y, mask & 2), _flip(c, mask & 1))
        return p, 4 * p[0] + 2 * p[1] + p[2]

    if mode == "scatter":
        r = land.shape[1]
        return [(p, src.at[pl.ds(i * r, r), :], land.at[me], land.at[i])
                for p, i in map(device, (1, 2, 3, 4, 5, 6, 7, 0))]
    r = land.shape[0] // N_DEV

    def rows(i):
        return land.at[pl.ds(i * r, r), :]

    if mode == "gather":
        return [(p, src, rows(me), rows(i)) for p, i in map(device, (1, 4, 2, 6, 0))]
    sibling = device(1)[0]
    return [(sibling, rows(device(m)[1]), rows(device(m)[1]), rows(device(m | 1)[1])) for m in (4, 2, 6)]


COPIES = dict(scatter=8, gather=5, forward=3)
HBM_SPEC = BS(memory_space=pltpu.HBM)
SEM_SPEC = BS(memory_space=pltpu.SEMAPHORE)
DATAFLOW = pltpu.SideEffectType.DATAFLOW_SIDE_EFFECTING


def _fresh(shape, dtype, tag):
    def body(o_ref):
        del o_ref

    return pl.pallas_call(body, name=f"fresh_{tag}", out_specs=BS(memory_space=pl.ANY), out_shape=SDS(shape, dtype))()


def _exchange_start(mode, srcs, lands, after, tag):
    if lands is None and mode == "gather":
        lands = [_fresh((N_DEV * s.shape[0], s.shape[1]), s.dtype, f"{tag}_{a}") for a, s in enumerate(srcs)]
    elif lands is None:
        lands = [_fresh((N_DEV, s.shape[0] // N_DEV, s.shape[1]), s.dtype, f"{tag}_{a}") for a, s in enumerate(srcs)]
    n, n_src, n_cp = len(lands), len(srcs), COPIES[mode]
    behind = [] if after is None else [after]

    def body(*refs):
        src_refs, land_refs = refs[:n_src], refs[n_src:n_src + n]
        send_sems, recv_sems = refs[n_src + n + len(behind)], refs[n_src + n + len(behind) + 1]
        token = refs[-1]
        for a in range(n):
            plan = _copy_plan(mode, src_refs[a] if n_src else None, land_refs[a], *_place())
            for k, (p, out, there, _) in enumerate(plan):
                pltpu.make_async_remote_copy(
                    src_ref=out, dst_ref=there, send_sem=send_sems.at[n_cp * a + k],
                    recv_sem=recv_sems.at[n_cp * a + k], device_id=p, device_id_type=MESH).start()
        token[...] = jnp.zeros_like(token)

    both = [*srcs, *lands]
    res = pl.pallas_call(
        body, name=f"{mode}_start_{tag}",
        out_shape=(pltpu.SemaphoreType.DMA((n_cp * n,)), pltpu.SemaphoreType.DMA((n_cp * n,)),
                   *[pltpu.HBM(v.shape, v.dtype) for v in both], SDS((8, 128), F32)),
        in_specs=[HBM_SPEC] * len(both) + [BS(memory_space=pl.ANY)] * len(behind),
        out_specs=(SEM_SPEC, SEM_SPEC, *[HBM_SPEC] * len(both), BS(memory_space=pltpu.VMEM)),
        input_output_aliases={i: 2 + i for i in range(len(both))},
        compiler_params=pltpu.CompilerParams(has_side_effects=DATAFLOW),
    )(*[pltpu.with_memory_space_constraint(v, pltpu.HBM) for v in both], *behind)
    return (mode, res[0], res[1], res[2:2 + n_src], res[2 + n_src:2 + n_src + n]), res[-1]


def _exchange_wait(handle, after, tag, which=None):
    mode, send_sems, recv_sems, srcs, lands = handle
    which = list(range(len(lands))) if which is None else list(which)
    n_cp = COPIES[mode]
    lands = [lands[a] for a in which]
    srcs = [srcs[a] for a in which] if srcs else []
    n, n_src = len(lands), len(srcs)
    afters = list(after) if isinstance(after, (tuple, list)) else [after]

    def body(*refs):
        src_refs, land_refs = refs[:n_src], refs[n_src:n_src + n]
        send_ref, recv_ref = refs[n_src + n], refs[n_src + n + 1]
        for i, a in enumerate(which):
            plan = _copy_plan(mode, src_refs[i] if n_src else None, land_refs[i], *_place())
            for k, (p, out, _, here) in enumerate(plan):
                cp = pltpu.make_async_remote_copy(
                    src_ref=out, dst_ref=here, send_sem=send_ref.at[n_cp * a + k], recv_sem=recv_ref.at[n_cp * a + k],
                    device_id=p, device_id_type=MESH)
                cp.wait_send()
                cp.wait_recv()

    both = [*srcs, *lands]
    res = pl.pallas_call(
        body, name=f"{mode}_wait_{tag}", out_shape=tuple(pltpu.HBM(v.shape, v.dtype) for v in both),
        in_specs=[HBM_SPEC] * len(both) + [SEM_SPEC, SEM_SPEC] + [BS(memory_space=pl.ANY)] * len(afters),
        out_specs=tuple([HBM_SPEC] * len(both)),
        input_output_aliases={i: i for i in range(len(both))},
        compiler_params=pltpu.CompilerParams(has_side_effects=DATAFLOW),
    )(*both, send_sems, recv_sems, *afters)
    return list(res[n_src:])


def _gather_pass_on(handle, which, after, tag):
    _, g_send, g_recv, srcs, lands = handle
    which = list(which)
    n, n_g, n_f = len(which), COPIES["gather"], COPIES["forward"]
    both = [*[srcs[a] for a in which], *[lands[a] for a in which]]
    afters = list(after) if isinstance(after, (tuple, list)) else [after]

    def body(*refs):
        src_refs, land_refs = refs[:n], refs[n:2 * n]
        g_send_ref, g_recv_ref = refs[2 * n], refs[2 * n + 1]
        f_send_ref, f_recv_ref = refs[2 * n + 2 + len(afters)], refs[2 * n + 3 + len(afters)]
        token = refs[-1]
        for i, a in enumerate(which):
            for k, (p, out, _, here) in enumerate(_copy_plan("gather", src_refs[i], land_refs[i], *_place())):
                cp = pltpu.make_async_remote_copy(
                    src_ref=out, dst_ref=here, send_sem=g_send_ref.at[n_g * a + k], recv_sem=g_recv_ref.at[n_g * a + k],
                    device_id=p, device_id_type=MESH)
                cp.wait_send()
                cp.wait_recv()
        for i in range(n):
            for k, (p, out, there, _) in enumerate(_copy_plan("forward", None, land_refs[i], *_place())):
                pltpu.make_async_remote_copy(
                    src_ref=out, dst_ref=there, send_sem=f_send_ref.at[n_f * i + k], recv_sem=f_recv_ref.at[n_f * i + k],
                    device_id=p, device_id_type=MESH).start()
        token[...] = jnp.zeros_like(token)

    res = pl.pallas_call(
        body, name=f"gather_pass_on_{tag}",
        out_shape=(pltpu.SemaphoreType.DMA((n_f * n,)), pltpu.SemaphoreType.DMA((n_f * n,)),
                   *[pltpu.HBM(v.shape, v.dtype) for v in both], SDS((8, 128), F32)),
        in_specs=[HBM_SPEC] * (2 * n) + [SEM_SPEC, SEM_SPEC] + [BS(memory_space=pl.ANY)] * len(afters),
        out_specs=(SEM_SPEC, SEM_SPEC, *[HBM_SPEC] * (2 * n), BS(memory_space=pltpu.VMEM)),
        input_output_aliases={i: 2 + i for i in range(2 * n)},
        compiler_params=pltpu.CompilerParams(has_side_effects=DATAFLOW),
    )(*both, g_send, g_recv, *afters)
    return ("forward", res[0], res[1], [], res[2 + n:2 + 2 * n]), res[-1]


def _allreduce_small(vec, behind):
    rows = vec.shape[0]

    def body(x_ref, behind_ref, o_ref, buf_ref, send_sems, recv_sems):
        x, y, c = _place()
        me = 4 * x + 2 * y + c
        buf_ref[me] = x_ref[...]
        peers = _peers(x, y, c)

        def copy(k, slot):
            return pltpu.make_async_remote_copy(
                src_ref=x_ref, dst_ref=buf_ref.at[slot], send_sem=send_sems.at[k], recv_sem=recv_sems.at[k],
                device_id=peers[k][0], device_id_type=MESH)

        sends = [copy(k, me) for k in range(N_DEV - 1)]
        for cp in sends:
            cp.start()
        for k in range(N_DEV - 1):
            copy(k, peers[k][1]).wait_recv()
        for cp in sends:
            cp.wait_send()
        acc = buf_ref[0]
        for s in range(1, N_DEV):
            acc = acc + buf_ref[s]
        o_ref[...] = acc

    vmem = BS(memory_space=pltpu.VMEM)
    return pl.pallas_call(
        body, name="allreduce_small", in_specs=[vmem, BS(memory_space=pl.ANY)], out_specs=vmem,
        out_shape=SDS((rows, 128), F32),
        scratch_shapes=[pltpu.VMEM((N_DEV, rows, 128), F32), pltpu.SemaphoreType.DMA((7,)),
                        pltpu.SemaphoreType.DMA((7,))],
        compiler_params=pltpu.CompilerParams(has_side_effects=True),
    )(vec, behind)


def _ffn_forward(x, hn, fetch, names, tag, next_g):
    gu, act = _ffn_up(hn, fetch(names[0], hn).reshape(2, F, D), tag)
    got = _mm_nn(act[None], fetch(names[1], act)[None], f"down_{tag}", res=x, scale=0.5, tm=512, next_g=next_g)
    out, hn_next = got if next_g is not None else (got, None)
    return out, hn_next, (x, hn, gu, act)


def _ffn_backward(dxo, dxo_b, saved, norm_g, wut, wd, tag, send):
    x, hn, gu, act = saved
    d_wd = _mm_tn(act[None], dxo_b, f"dwd_{tag}", scale=0.5)
    du = _ffn_dact(dxo_b, wd, gu, send(("down",), [d_wd]), tag)
    d_wut = _mm_tn(du, hn, f"dwu_{tag}")
    token = send(("up",), [d_wut])
    return _mm_nn_norm_bwd([du], wut.reshape(2 * F, D), x, norm_g + token[0, 0], dxo, tag)


def _mixer_forward(x, hn, fetch, bias, tables, tag, next_g):
    proj = _mm_nt_rows(hn, fetch("win", hn), f"proj_{tag}", 1024, IN_W // 2, IN_W, 0, rope=(*tables, 2 * QKV_A // 3))
    qkr = proj
    outs, lses = [], []
    for grp in range(3):
        o, l = _dil_fwd(qkr, proj, grp)
        outs.append(o)
        lses.append(l)
    ya = _combine_fwd(outs, lses)
    yb, lse_b = _na_fwd(proj, bias)
    merged, za, zb = _merge_fwd(ya, yb, proj, fetch("wa", yb), fetch("wb", yb))
    out, hn_next = _mm_nn(merged[None], fetch("wo", merged)[None], f"out_{tag}", res=x, next_g=next_g)
    return out, hn_next, (x, hn, proj, qkr, outs, lses, ya, yb, lse_b, merged, za, zb)


def _mixer_backward(dxo, dxo_b, saved, norm_g, w, bias, tables, tag, send):
    wint, wat, wbt, wo = w
    x, hn, proj, qkr, outs, lses, ya, yb, lse_b, merged, za, zb = saved
    d_wo = _mm_tn(merged[None], dxo_b, f"dwo_{tag}")
    dza, dzb, dlog = _merge_bwd(dxo_b, wo, za, zb, proj)
    dya = _mm_nn(dza[None], wat[None], f"dya_{tag}")
    dyb = _mm_nn(dzb[None], wbt[None], f"dyb_{tag}")
    d_wat = _mm_tn(dza[None], ya, f"dwa_{tag}")
    d_wbt = _mm_tn(dzb[None], yb, f"dwb_{tag}")
    cb = _combine_bwd(dya, outs, lses)
    dqs, dks, dvs = [], [], []
    for grp in range(3):
        dq, dk, dv = _dil_bwd(qkr, proj, cb[grp], cb[3 + grp], lses[grp], grp)
        dqs.append(dq)
        dks.append(dk)
        dvs.append(dv)
    d_qkv_b, dbias_tab = _na_bwd(proj, bias, dyb, yb, lse_b)
    dbias = _na_dbias(dbias_tab)
    dproj = [_rope_bwd(dqs, dks, dvs, *tables), d_qkv_b, dlog]
    d_wint, row = None, 0
    for i, p in enumerate(dproj):
        d_wint = _mm_tn(p, hn, f"dwin{i}_{tag}", into=d_wint, row0=row, rows=IN_W)
        row += p.shape[0] * p.shape[2]
    token = send(("win", "wa", "wb", "wo"), [d_wint, d_wat, d_wbt, d_wo])
    dx, dx_b, dg = _mm_nn_norm_bwd(dproj, wint, x, norm_g + token[0, 0], dxo, f"mix_{tag}", tm=512, single_w=True)
    dbias = dbias[:, 0, :480].reshape(8, 15, 32)[:, :, :31]
    return dx, dx_b, dg, dbias


def _pack_small(norms, biases, final, loss=None):
    parts = []
    for layer in range(DEPTH):
        parts += [norms[0][layer], norms[1][layer], norms[2][layer],
                  jnp.pad(biases[layer].reshape(-1), (0, BIAS_PAD - 8 * 15 * 31))]
    parts.append(final)
    flat = jnp.concatenate([p.reshape(-1).astype(F32) for p in parts])
    if loss is not None:
        flat = jnp.concatenate([flat, loss.reshape(-1)])
    return jnp.pad(flat, (0, SMALL_ROWS * 128 - flat.shape[0])).reshape(SMALL_ROWS, 128)


def _unpack_small(packed):
    flat = packed.reshape(-1)
    norms, biases = ([], [], []), []
    pos = 0
    for _ in range(DEPTH):
        for k in range(3):
            norms[k].append(flat[pos:pos + D])
            pos += D
        biases.append(flat[pos:pos + 8 * 15 * 31].reshape(8, 15, 31))
        pos += BIAS_PAD
    final = flat[pos:pos + D]
    pos += D
    return [jnp.stack(n) for n in norms], jnp.stack(biases), final, flat[pos]


def kernel(x, ffn1_norm, ffn1_w_up, ffn1_w_down, mix_norm, w_in, na_rel_bias, w_branch_a, w_branch_b, w_out, ffn2_norm, ffn2_w_up, ffn2_w_down, final_norm, loss_target, m_ffn1_norm, m_ffn1_w_up, m_ffn1_w_down, m_mix_norm, m_w_in, m_na_rel_bias, m_w_branch_a, m_w_branch_b, m_w_out, m_ffn2_norm, m_ffn2_w_up, m_ffn2_w_down, m_final_norm, v_ffn1_norm, v_ffn1_w_up, v_ffn1_w_down, v_mix_norm, v_w_in, v_na_rel_bias, v_w_branch_a, v_w_branch_b, v_w_out, v_ffn2_norm, v_ffn2_w_up, v_ffn2_w_down, v_final_norm):
    t = x.shape[0] * x.shape[1]
    xs = x.reshape(t, D)
    tgt = loss_target.reshape(t, D)
    tables = _rope_tables()

    col_sharded = dict(up1=ffn1_w_up, win=w_in, wa=w_branch_a, wb=w_branch_b, up2=ffn2_w_up)
    row_sharded = dict(down1=ffn1_w_down, wo=w_out, down2=ffn2_w_down)
    shard = [{} for _ in range(DEPTH)]
    for layer in range(DEPTH):
        for name, arr in col_sharded.items():
            shard[layer][name] = arr[layer].T.astype(BF16)
        for name, arr in row_sharded.items():
            shard[layer][name] = arr[layer].astype(BF16)

    weights = [{} for _ in range(DEPTH)]
    travel = [(0, ("up1",)), (0, ("down1",)), (0, ("win",)), (0, ("wa", "wb", "wo")), (0, ("up2", "down2")),
              (1, ("up1", "down1")), (1, ("win",)), (1, ("wa", "wb", "wo")), (1, ("up2", "down2"))]
    group_of, chips_done, sibling_done, passing = {}, {}, {}, {}
    count = 0
    for i, (layer, names) in enumerate(travel):
        chips_done[i] = list(range(count, count + len(names)))
        count += len(names)
        for n in names:
            group_of[layer, n] = (i, names)
    gathered, token = _exchange_start(
        "gather", [shard[layer][n] for layer, names in travel for n in names], None, None, "w")
    zero = token[0, 0]

    biases = [_na_bias_table(na_rel_bias[layer] + zero) for layer in range(DEPTH)]

    def pass_on(i, behind):
        if i in chips_done:
            sibling_done[i], passing[i] = _gather_pass_on(gathered, chips_done.pop(i), behind, f"w{i}")

    def fetcher(layer):
        def fetch(name, behind):
            if (layer, name) in group_of:
                i, names = group_of[layer, name]
                if i == 0:
                    behind = (behind, *biases)
                behind = behind if isinstance(behind, tuple) else (behind,)
                pass_on(i, behind)
                if i > 0:
                    pass_on(i + 1, behind)
                if i + 1 in passing:
                    behind = (*behind, passing[i + 1])
                for n, got in zip(names, _exchange_wait(sibling_done.pop(i), behind, f"p{i}")):
                    weights[layer][n] = got
                    del group_of[layer, n]
            return weights[layer][name]
        return fetch

    saved = []
    h = xs
    hn = _norm_fwd(xs, ffn1_norm[0] + zero, "first")
    for layer in range(DEPTH):
        bias = biases[layer]
        fetch = fetcher(layer)
        after_ffn2 = ffn1_norm[layer + 1] if layer + 1 < DEPTH else None
        h, hn, s1 = _ffn_forward(h, hn, fetch, ("up1", "down1"), f"f1l{layer}", mix_norm[layer])
        h, hn, s2 = _mixer_forward(h, hn, fetch, bias, tables, f"l{layer}", ffn2_norm[layer])
        h, hn, s3 = _ffn_forward(h, hn, fetch, ("up2", "down2"), f"f2l{layer}", after_ffn2)
        saved.append((s1, s2, s3, bias))
    loss_part, dh, dh_b, d_final = _loss_head(h, final_norm, tgt)

    d_norms = ([None] * DEPTH, [None] * DEPTH, [None] * DEPTH)
    d_bias = [None] * DEPTH
    sent = {}

    def sender(layer, suffix):
        def send(names, grads):
            tag = f"g{layer}{names[0]}{suffix}"
            handle, token = _exchange_start("scatter", grads, None, None, tag)
            for i, n in enumerate(names):
                sent[layer, n + suffix] = (handle, i, tag)
            return token
        return send

    for layer in reversed(range(DEPTH)):
        w = weights[layer]
        s1, s2, s3, bias = saved[layer]
        dh, dh_b, d_norms[2][layer] = _ffn_backward(
            dh, dh_b, s3, ffn2_norm[layer], w["up2"].reshape(2, F, D), w["down2"], f"f2l{layer}", sender(layer, "2"))
        dh, dh_b, d_norms[1][layer], d_bias[layer] = _mixer_backward(
            dh, dh_b, s2, mix_norm[layer], (w["win"], w["wa"], w["wb"], w["wo"]), bias, tables, f"l{layer}",
            sender(layer, ""))
        dh, dh_b, d_norms[0][layer] = _ffn_backward(
            dh, dh_b, s1, ffn1_norm[layer], w["up1"].reshape(2, F, D), w["down1"], f"f1l{layer}", sender(layer, "1"))
    grad_x = dh.reshape(x.shape)

    originals = dict(up1=(ffn1_w_up, m_ffn1_w_up, v_ffn1_w_up), down1=(ffn1_w_down, m_ffn1_w_down, v_ffn1_w_down),
                     win=(w_in, m_w_in, v_w_in), wa=(w_branch_a, m_w_branch_a, v_w_branch_a),
                     wb=(w_branch_b, m_w_branch_b, v_w_branch_b), wo=(w_out, m_w_out, v_w_out),
                     up2=(ffn2_w_up, m_ffn2_w_up, v_ffn2_w_up), down2=(ffn2_w_down, m_ffn2_w_down, v_ffn2_w_down))
    big = {}
    behind = dh
    landed = {}

    def received(layer, name):
        handle, i, tag = sent[layer, name]
        if tag not in landed:
            landed[tag] = _exchange_wait(handle, behind, tag)
        return landed[tag][i]

    for name in ("down2", "up2", "win", "wa", "wb", "wo", "down1", "up1"):
        if name == "up1":
            small = _allreduce_small(_pack_small(d_norms, d_bias, d_final, loss_part[0, :1]), behind)
            g_norms, g_bias, g_final, loss = _unpack_small(small)
            w_small = _pack_small((ffn1_norm, mix_norm, ffn2_norm), na_rel_bias, final_norm)
            m_small = _pack_small((m_ffn1_norm, m_mix_norm, m_ffn2_norm), m_na_rel_bias, m_final_norm)
            v_small = _pack_small((v_ffn1_norm, v_mix_norm, v_ffn2_norm), v_na_rel_bias, v_final_norm)
            upd = _adamw(w_small[None], small[None], m_small[None], v_small[None], "small")
            small_out = [(g_norms, g_bias, g_final)] + [_unpack_small(u[0])[:3] for u in upd]
            behind = upd[0]
        wv, mv, vv = originals[name]
        if name in col_sharded:
            wv, mv, vv = (jnp.swapaxes(t, 1, 2) for t in (wv, mv, vv))
        big[name] = tuple(_sum_adamw(received(0, name), received(1, name), wv, mv, vv, name))
        behind = big[name][1]
        if name in col_sharded:
            big[name] = tuple(jnp.swapaxes(t, 1, 2) for t in big[name])

    outputs = [loss, grad_x]
    for kind in range(4):
        norms, bias_k, final_k = small_out[kind]
        outputs += [norms[0], big["up1"][kind], big["down1"][kind], norms[1], big["win"][kind], bias_k,
                    big["wa"][kind], big["wb"][kind], big["wo"][kind], norms[2], big["up2"][kind],
                    big["down2"][kind], final_k]
    return tuple(outputs)
```
